```python
import jax, jax.numpy as jnp
from jax import lax
import numpy as np

D_MODEL = 1024
BATCH = 8
SEQ = 2048
DEPTH = 1

N_Q_HEADS = 16
N_KV_HEADS = 4
HEAD_DIM = 64
WINDOW = 128
ATTN_BLOCK = WINDOW
SSD_EXPAND = 2
D_INNER = SSD_EXPAND * D_MODEL
SSD_HEAD_DIM = 64
N_SSD_HEADS = D_INNER // SSD_HEAD_DIM
N_SSD_GROUPS = 4
D_STATE = 128
SSD_CONV = 4
CHUNK = 128
D_FF = 2816
FFN_CONV = 3
EPS = 1e-5
NEG = -1e30

Q_DIM = N_Q_HEADS * HEAD_DIM
KV_DIM = N_KV_HEADS * HEAD_DIM
BC_DIM = N_SSD_GROUPS * D_STATE
XBC_DIM = D_INNER + 2 * BC_DIM
IN_SPLITS = (Q_DIM, KV_DIM, KV_DIM, D_INNER, XBC_DIM, N_SSD_HEADS, D_MODEL, D_MODEL)
IN_DIM = sum(IN_SPLITS)

kernel_name = "hybrid_swa_sink_ssd_convffn"


def _split(t, sizes):
    idx = np.cumsum(np.array(sizes))[:-1].tolist()
    return jnp.split(t, idx, axis=-1)


def rmsnorm(x, w):
    xf = x.astype(jnp.float32)
    y = xf * lax.rsqrt(jnp.mean(xf * xf, axis=-1, keepdims=True) + EPS)
    return (y * w.astype(jnp.float32)).astype(x.dtype)


def causal_dwconv(x, w, b):
    K = w.shape[0]
    S = x.shape[1]
    xp = jnp.pad(x, ((0, 0), (K - 1, 0), (0, 0)))
    y = xp[:, 0:S] * w[0]
    for k in range(1, K):
        y = y + xp[:, k:k + S] * w[k]
    return y + b


def banded_sink_attention(q, k, v, sinks):
    Bsz, S, _ = q.shape
    W = ATTN_BLOCK
    nb = S // W
    G = N_Q_HEADS // N_KV_HEADS
    qb = q.reshape(Bsz, nb, W, N_KV_HEADS, G, HEAD_DIM)

    def band(t):
        t = t.reshape(Bsz, S, N_KV_HEADS, HEAD_DIM)
        tp = jnp.pad(t, ((0, 0), (W, 0), (0, 0), (0, 0)))
        prev = tp[:, :S].reshape(Bsz, nb, W, N_KV_HEADS, HEAD_DIM)
        cur = t.reshape(Bsz, nb, W, N_KV_HEADS, HEAD_DIM)
        return jnp.concatenate([prev, cur], axis=2)

    kb, vb = band(k), band(v)
    scores = jnp.einsum('bnqhgd,bnshd->bnhgqs', qb, kb).astype(jnp.float32) * (HEAD_DIM ** -0.5)
    qi = jnp.arange(W)[:, None]
    si = jnp.arange(2 * W)[None, :]
    dist = W + qi - si
    kpos = jnp.arange(nb)[:, None, None] * W - W + si[None]
    valid = (dist >= 0)[None] & (dist < WINDOW)[None] & (kpos >= 0)
    valid = valid[None, :, None, None]
    scores = jnp.where(valid, scores, NEG)
    sink = sinks.astype(jnp.float32).reshape(N_KV_HEADS, G)[None, None, :, :, None]
    m = jnp.maximum(scores.max(axis=-1), sink)
    p = jnp.where(valid, jnp.exp(scores - m[..., None]), 0.0)
    denom = p.sum(axis=-1) + jnp.exp(sink - m)
    probs = (p / denom[..., None]).astype(v.dtype)
    out = jnp.einsum('bnhgqs,bnshd->bnqhgd', probs, vb)
    return out.reshape(Bsz, S, Q_DIM)


def ssd_chunked(xs, dt, a_log, bmat, cmat, d_skip):
    Bsz, S, H, P = xs.shape
    G, N = bmat.shape[2], bmat.shape[3]
    J = H // G
    L = CHUNK
    nc = S // L
    A = -jnp.exp(a_log.astype(jnp.float32))
    dA = (dt * A).reshape(Bsz, nc, L, G, J)
    xf = xs.astype(jnp.float32)
    xc = (xf * dt[..., None]).reshape(Bsz, nc, L, G, J, P)
    bc = bmat.astype(jnp.float32).reshape(Bsz, nc, L, G, N)
    cc = cmat.astype(jnp.float32).reshape(Bsz, nc, L, G, N)
    a_cs = jnp.cumsum(dA, axis=2)
    seg = a_cs[:, :, :, None] - a_cs[:, :, None, :]
    causal = jnp.tril(jnp.ones((L, L), dtype=bool))[:, :, None, None]
    decay = jnp.where(causal, jnp.exp(jnp.where(causal, seg, 0.0)), 0.0)
    cb = jnp.einsum('bclgn,bcsgn->bclsg', cc, bc)
    y_diag = jnp.einsum('bclsgj,bcsgjp->bclgjp', cb[..., None] * decay, xc)
    decay_states = jnp.exp(a_cs[:, :, -1:] - a_cs)
    states = jnp.einsum('bclgn,bclgjp->bcgjpn', bc, xc * decay_states[..., None])
    chunk_decay = jnp.exp(a_cs[:, :, -1])

    def step(h, inp):
        st, dec = inp
        h_new = h * dec[..., None, None] + st
        return h_new, h

    h0 = jnp.zeros((Bsz, G, J, P, N), jnp.float32)
    _, prev = lax.scan(step, h0, (jnp.swapaxes(states, 0, 1), jnp.swapaxes(chunk_decay, 0, 1)))
    prev = jnp.swapaxes(prev, 0, 1)
    y_off = jnp.einsum('bclgn,bcgjpn->bclgjp', cc, prev) * jnp.exp(a_cs)[..., None]
    y = (y_diag + y_off).reshape(Bsz, S, H, P)
    return y + xf * d_skip.astype(jnp.float32)[:, None]


def _fwd_setup_inputs(seed: int = 0) -> dict:
    key = jax.random.key(seed)
    ks = jax.random.split(key, 24)
    f32 = jnp.float32
    nrm = lambda k, shape, scale: jax.random.normal(k, shape, f32) * scale
    dt0 = jnp.exp(jax.random.uniform(ks[9], (DEPTH, N_SSD_HEADS), f32,
                                     jnp.log(0.001), jnp.log(0.1)))
    return {
        "x": nrm(ks[0], (BATCH, SEQ, D_MODEL), 1.0),
        "norm1_w": 1.0 + nrm(ks[1], (DEPTH, D_MODEL), 0.02),
        "w_in": nrm(ks[2], (DEPTH, D_MODEL, IN_DIM), D_MODEL ** -0.5),
        "b_gate": nrm(ks[3], (DEPTH, 2 * D_MODEL), 0.02),
        "attn_sinks": nrm(ks[4], (DEPTH, N_Q_HEADS), 0.5),
        "w_attn_o": nrm(ks[5], (DEPTH, Q_DIM, D_MODEL), Q_DIM ** -0.5),
        "ssd_conv_w": nrm(ks[6], (DEPTH, SSD_CONV, XBC_DIM), SSD_CONV ** -0.5),
        "ssd_conv_b": nrm(ks[7], (DEPTH, XBC_DIM), 0.02),
        "dt_bias": dt0 + jnp.log(-jnp.expm1(-dt0)),
        "a_log": jnp.log(jax.random.uniform(ks[10], (DEPTH, N_SSD_HEADS), f32, 1.0, 16.0)),
        "d_skip": 1.0 + nrm(ks[11], (DEPTH, N_SSD_HEADS), 0.02),
        "ssd_norm_w": 1.0 + nrm(ks[12], (DEPTH, D_INNER), 0.02),
        "w_ssd_o": nrm(ks[13], (DEPTH, D_INNER, D_MODEL), D_INNER ** -0.5),
        "w_out": nrm(ks[14], (DEPTH, D_MODEL, D_MODEL), D_MODEL ** -0.5),
        "norm2_w": 1.0 + nrm(ks[15], (DEPTH, D_MODEL), 0.02),
        "w_up": nrm(ks[16], (DEPTH, D_MODEL, 2 * D_FF), D_MODEL ** -0.5),
        "ffn_conv_w": nrm(ks[17], (DEPTH, FFN_CONV, 2 * D_FF), FFN_CONV ** -0.5),
        "ffn_conv_b": nrm(ks[18], (DEPTH, 2 * D_FF), 0.02),
        "w_down": nrm(ks[19], (DEPTH, D_FF, D_MODEL), D_FF ** -0.5),
        "final_norm_w": 1.0 + nrm(ks[20], (D_MODEL,), 0.02),
    }


def _fwd_reference(x, norm1_w, w_in, b_gate, attn_sinks, w_attn_o, ssd_conv_w, ssd_conv_b, dt_bias,
              a_log, d_skip, ssd_norm_w, w_ssd_o, w_out, norm2_w, w_up, ffn_conv_w, ffn_conv_b,
              w_down, final_norm_w):
    Bsz, S, _ = x.shape
    h = x
    for layer in range(DEPTH):
        xn = rmsnorm(h, norm1_w[layer])
        proj = xn @ w_in[layer]
        q, k, v, z, xbc, dt_raw, ga_raw, gs_raw = _split(proj, IN_SPLITS)
        ba, bs = _split(b_gate[layer], (D_MODEL, D_MODEL))
        gate_a = jax.nn.sigmoid(ga_raw + ba)
        gate_s = jax.nn.sigmoid(gs_raw + bs)
        attn = banded_sink_attention(q, k, v, attn_sinks[layer]) @ w_attn_o[layer]
        xbc = jax.nn.silu(causal_dwconv(xbc, ssd_conv_w[layer], ssd_conv_b[layer]))
        xs, bm, cm = _split(xbc, (D_INNER, BC_DIM, BC_DIM))
        dt = jax.nn.softplus(dt_raw.astype(jnp.float32) + dt_bias[layer].astype(jnp.float32))
        y = ssd_chunked(xs.reshape(Bsz, S, N_SSD_HEADS, SSD_HEAD_DIM), dt, a_log[layer],
                        bm.reshape(Bsz, S, N_SSD_GROUPS, D_STATE),
                        cm.reshape(Bsz, S, N_SSD_GROUPS, D_STATE), d_skip[layer])
        y = y.reshape(Bsz, S, D_INNER) * jax.nn.silu(z.astype(jnp.float32))
        yg = y.reshape(Bsz, S, N_SSD_GROUPS, D_INNER // N_SSD_GROUPS)
        yg = yg * lax.rsqrt(jnp.mean(yg * yg, axis=-1, keepdims=True) + EPS)
        y = (yg.reshape(Bsz, S, D_INNER) * ssd_norm_w[layer].astype(jnp.float32)).astype(x.dtype)
        ssd_out = y @ w_ssd_o[layer]
        mix = (gate_a * attn + gate_s * ssd_out) @ w_out[layer]
        h = h + mix.astype(h.dtype)
        hn = rmsnorm(h, norm2_w[layer])
        u = causal_dwconv(hn @ w_up[layer], ffn_conv_w[layer], ffn_conv_b[layer])
        val, gt = _split(u, (D_FF, D_FF))
        h = h + ((jax.nn.silu(gt) * val) @ w_down[layer]).astype(h.dtype)
    return rmsnorm(h, final_norm_w)


import jax as _jax
import jax.numpy as _jnp

TWIN_FORMAT = 'train_step'
FWD_PARAMS = ['x', 'norm1_w', 'w_in', 'b_gate', 'attn_sinks', 'w_attn_o', 'ssd_conv_w', 'ssd_conv_b', 'dt_bias', 'a_log', 'd_skip', 'ssd_norm_w', 'w_ssd_o', 'w_out', 'norm2_w', 'w_up', 'ffn_conv_w', 'ffn_conv_b', 'w_down', 'final_norm_w']
TWIN_WEIGHTS = ['norm1_w', 'w_in', 'b_gate', 'attn_sinks', 'w_attn_o', 'ssd_conv_w', 'ssd_conv_b', 'dt_bias', 'a_log', 'd_skip', 'ssd_norm_w', 'w_ssd_o', 'w_out', 'norm2_w', 'w_up', 'ffn_conv_w', 'ffn_conv_b', 'w_down', 'final_norm_w']
TWIN_DIFF_INPUT = 'x'
TWIN_INPUTS = ['x', 'norm1_w', 'w_in', 'b_gate', 'attn_sinks', 'w_attn_o', 'ssd_conv_w', 'ssd_conv_b', 'dt_bias', 'a_log', 'd_skip', 'ssd_norm_w', 'w_ssd_o', 'w_out', 'norm2_w', 'w_up', 'ffn_conv_w', 'ffn_conv_b', 'w_down', 'final_norm_w', 'loss_target', 'm_norm1_w', 'm_w_in', 'm_b_gate', 'm_attn_sinks', 'm_w_attn_o', 'm_ssd_conv_w', 'm_ssd_conv_b', 'm_dt_bias', 'm_a_log', 'm_d_skip', 'm_ssd_norm_w', 'm_w_ssd_o', 'm_w_out', 'm_norm2_w', 'm_w_up', 'm_ffn_conv_w', 'm_ffn_conv_b', 'm_w_down', 'm_final_norm_w', 'v_norm1_w', 'v_w_in', 'v_b_gate', 'v_attn_sinks', 'v_w_attn_o', 'v_ssd_conv_w', 'v_ssd_conv_b', 'v_dt_bias', 'v_a_log', 'v_d_skip', 'v_ssd_norm_w', 'v_w_ssd_o', 'v_w_out', 'v_norm2_w', 'v_w_up', 'v_ffn_conv_w', 'v_ffn_conv_b', 'v_w_down', 'v_final_norm_w']
TWIN_OUTPUTS = ['loss', 'grad_x', 'grad_norm1_w', 'grad_w_in', 'grad_b_gate', 'grad_attn_sinks', 'grad_w_attn_o', 'grad_ssd_conv_w', 'grad_ssd_conv_b', 'grad_dt_bias', 'grad_a_log', 'grad_d_skip', 'grad_ssd_norm_w', 'grad_w_ssd_o', 'grad_w_out', 'grad_norm2_w', 'grad_w_up', 'grad_ffn_conv_w', 'grad_ffn_conv_b', 'grad_w_down', 'grad_final_norm_w', 'delta_norm1_w', 'delta_w_in', 'delta_b_gate', 'delta_attn_sinks', 'delta_w_attn_o', 'delta_ssd_conv_w', 'delta_ssd_conv_b', 'delta_dt_bias', 'delta_a_log', 'delta_d_skip', 'delta_ssd_norm_w', 'delta_w_ssd_o', 'delta_w_out', 'delta_norm2_w', 'delta_w_up', 'delta_ffn_conv_w', 'delta_ffn_conv_b', 'delta_w_down', 'delta_final_norm_w', 'new_m_norm1_w', 'new_m_w_in', 'new_m_b_gate', 'new_m_attn_sinks', 'new_m_w_attn_o', 'new_m_ssd_conv_w', 'new_m_ssd_conv_b', 'new_m_dt_bias', 'new_m_a_log', 'new_m_d_skip', 'new_m_ssd_norm_w', 'new_m_w_ssd_o', 'new_m_w_out', 'new_m_norm2_w', 'new_m_w_up', 'new_m_ffn_conv_w', 'new_m_ffn_conv_b', 'new_m_w_down', 'new_m_final_norm_w', 'new_v_norm1_w', 'new_v_w_in', 'new_v_b_gate', 'new_v_attn_sinks', 'new_v_w_attn_o', 'new_v_ssd_conv_w', 'new_v_ssd_conv_b', 'new_v_dt_bias', 'new_v_a_log', 'new_v_d_skip', 'new_v_ssd_norm_w', 'new_v_w_ssd_o', 'new_v_w_out', 'new_v_norm2_w', 'new_v_w_up', 'new_v_ffn_conv_w', 'new_v_ffn_conv_b', 'new_v_w_down', 'new_v_final_norm_w']
TWIN_LEAF_KINDS = {'loss': 'loss', 'grad_x': 'grad_x', 'grad_norm1_w': 'grad_w', 'grad_w_in': 'grad_w', 'grad_b_gate': 'grad_w', 'grad_attn_sinks': 'grad_w', 'grad_w_attn_o': 'grad_w', 'grad_ssd_conv_w': 'grad_w', 'grad_ssd_conv_b': 'grad_w', 'grad_dt_bias': 'grad_w', 'grad_a_log': 'grad_w', 'grad_d_skip': 'grad_w', 'grad_ssd_norm_w': 'grad_w', 'grad_w_ssd_o': 'grad_w', 'grad_w_out': 'grad_w', 'grad_norm2_w': 'grad_w', 'grad_w_up': 'grad_w', 'grad_ffn_conv_w': 'grad_w', 'grad_ffn_conv_b': 'grad_w', 'grad_w_down': 'grad_w', 'grad_final_norm_w': 'grad_w', 'delta_norm1_w': 'delta_w', 'delta_w_in': 'delta_w', 'delta_b_gate': 'delta_w', 'delta_attn_sinks': 'delta_w', 'delta_w_attn_o': 'delta_w', 'delta_ssd_conv_w': 'delta_w', 'delta_ssd_conv_b': 'delta_w', 'delta_dt_bias': 'delta_w', 'delta_a_log': 'delta_w', 'delta_d_skip': 'delta_w', 'delta_ssd_norm_w': 'delta_w', 'delta_w_ssd_o': 'delta_w', 'delta_w_out': 'delta_w', 'delta_norm2_w': 'delta_w', 'delta_w_up': 'delta_w', 'delta_ffn_conv_w': 'delta_w', 'delta_ffn_conv_b': 'delta_w', 'delta_w_down': 'delta_w', 'delta_final_norm_w': 'delta_w', 'new_m_norm1_w': 'new_m', 'new_m_w_in': 'new_m', 'new_m_b_gate': 'new_m', 'new_m_attn_sinks': 'new_m', 'new_m_w_attn_o': 'new_m', 'new_m_ssd_conv_w': 'new_m', 'new_m_ssd_conv_b': 'new_m', 'new_m_dt_bias': 'new_m', 'new_m_a_log': 'new_m', 'new_m_d_skip': 'new_m', 'new_m_ssd_norm_w': 'new_m', 'new_m_w_ssd_o': 'new_m', 'new_m_w_out': 'new_m', 'new_m_norm2_w': 'new_m', 'new_m_w_up': 'new_m', 'new_m_ffn_conv_w': 'new_m', 'new_m_ffn_conv_b': 'new_m', 'new_m_w_down': 'new_m', 'new_m_final_norm_w': 'new_m', 'new_v_norm1_w': 'new_v', 'new_v_w_in': 'new_v', 'new_v_b_gate': 'new_v', 'new_v_attn_sinks': 'new_v', 'new_v_w_attn_o': 'new_v', 'new_v_ssd_conv_w': 'new_v', 'new_v_ssd_conv_b': 'new_v', 'new_v_dt_bias': 'new_v', 'new_v_a_log': 'new_v', 'new_v_d_skip': 'new_v', 'new_v_ssd_norm_w': 'new_v', 'new_v_w_ssd_o': 'new_v', 'new_v_w_out': 'new_v', 'new_v_norm2_w': 'new_v', 'new_v_w_up': 'new_v', 'new_v_ffn_conv_w': 'new_v', 'new_v_ffn_conv_b': 'new_v', 'new_v_w_down': 'new_v', 'new_v_final_norm_w': 'new_v'}


def _forward(args):
    return _fwd_reference(*[args[k] for k in FWD_PARAMS])


def _output_shape():
    out = _jax.eval_shape(lambda: _forward(_fwd_setup_inputs(0)))
    return out.shape, out.dtype

N_MICROBATCH = 1
ADAM_LR = 0.001
ADAM_B1 = 0.9
ADAM_B2 = 0.999
ADAM_EPS = 1e-08
ADAM_WD = 0.01
ADAM_STEP = 10
PER_EXAMPLE_BATCH_AXIS = {'x': 0, 'loss_target': 0}
SHARED_INPUTS = []
_WEIGHT_DTYPES = {'norm1_w': _jnp.float32, 'w_in': _jnp.float32, 'b_gate': _jnp.float32, 'attn_sinks': _jnp.float32, 'w_attn_o': _jnp.float32, 'ssd_conv_w': _jnp.float32, 'ssd_conv_b': _jnp.float32, 'dt_bias': _jnp.float32, 'a_log': _jnp.float32, 'd_skip': _jnp.float32, 'ssd_norm_w': _jnp.float32, 'w_ssd_o': _jnp.float32, 'w_out': _jnp.float32, 'norm2_w': _jnp.float32, 'w_up': _jnp.float32, 'ffn_conv_w': _jnp.float32, 'ffn_conv_b': _jnp.float32, 'w_down': _jnp.float32, 'final_norm_w': _jnp.float32}
MOMENT_SCALE = {'norm1_w': 1.097840e-01, 'w_in': 3.586788e-02, 'b_gate': 1.916157e-02, 'attn_sinks': 9.844976e-03, 'w_attn_o': 1.366559e-02, 'ssd_conv_w': 4.025473e-02, 'ssd_conv_b': 5.421314e-02, 'dt_bias': 1.078865e-01, 'a_log': 1.275637e-01, 'd_skip': 3.782067e-01, 'ssd_norm_w': 4.689499e-02, 'w_ssd_o': 6.638150e-02, 'w_out': 6.710271e-02, 'norm2_w': 8.499149e-02, 'w_up': 3.586790e-02, 'ffn_conv_w': 3.627830e-02, 'ffn_conv_b': 3.635531e-02, 'w_down': 5.878778e-02, 'final_norm_w': 1.602150e+01}


def _to_microbatches(a, axis):
    t = _jnp.moveaxis(a, axis, 0)
    t = t.reshape((N_MICROBATCH, t.shape[0] // N_MICROBATCH) + t.shape[1:])
    return _jnp.moveaxis(t, 1, axis + 1)


def setup_inputs(seed: int = 0) -> dict:
    inp = _fwd_setup_inputs(seed)
    key = _jax.random.fold_in(_jax.random.key(seed), 7919)
    shape, _ = _output_shape()
    out = dict(inp)
    out["loss_target"] = _jax.random.normal(_jax.random.fold_in(key, 0), shape, _jnp.float32)
    for i, name in enumerate(TWIN_WEIGHTS):
        w = inp[name].astype(_jnp.float32)
        if MOMENT_SCALE is None:
            s = _jnp.sqrt(_jnp.mean(_jnp.square(w)) + 1e-30)
        else:
            s = MOMENT_SCALE[name]
        km, kv = _jax.random.split(_jax.random.fold_in(key, i + 1))
        out[name] = w
        out["m_" + name] = s * _jax.random.normal(km, w.shape, _jnp.float32)
        out["v_" + name] = (s * s) * _jax.random.uniform(kv, w.shape, _jnp.float32, 0.5, 1.5)
    if N_MICROBATCH > 1:
        for name, axis in PER_EXAMPLE_BATCH_AXIS.items():
            out[name] = _to_microbatches(out[name], axis)
    return {'x': out['x'], 'norm1_w': out['norm1_w'], 'w_in': out['w_in'], 'b_gate': out['b_gate'], 'attn_sinks': out['attn_sinks'], 'w_attn_o': out['w_attn_o'], 'ssd_conv_w': out['ssd_conv_w'], 'ssd_conv_b': out['ssd_conv_b'], 'dt_bias': out['dt_bias'], 'a_log': out['a_log'], 'd_skip': out['d_skip'], 'ssd_norm_w': out['ssd_norm_w'], 'w_ssd_o': out['w_ssd_o'], 'w_out': out['w_out'], 'norm2_w': out['norm2_w'], 'w_up': out['w_up'], 'ffn_conv_w': out['ffn_conv_w'], 'ffn_conv_b': out['ffn_conv_b'], 'w_down': out['w_down'], 'final_norm_w': out['final_norm_w'], 'loss_target': out['loss_target'], 'm_norm1_w': out['m_norm1_w'], 'm_w_in': out['m_w_in'], 'm_b_gate': out['m_b_gate'], 'm_attn_sinks': out['m_attn_sinks'], 'm_w_attn_o': out['m_w_attn_o'], 'm_ssd_conv_w': out['m_ssd_conv_w'], 'm_ssd_conv_b': out['m_ssd_conv_b'], 'm_dt_bias': out['m_dt_bias'], 'm_a_log': out['m_a_log'], 'm_d_skip': out['m_d_skip'], 'm_ssd_norm_w': out['m_ssd_norm_w'], 'm_w_ssd_o': out['m_w_ssd_o'], 'm_w_out': out['m_w_out'], 'm_norm2_w': out['m_norm2_w'], 'm_w_up': out['m_w_up'], 'm_ffn_conv_w': out['m_ffn_conv_w'], 'm_ffn_conv_b': out['m_ffn_conv_b'], 'm_w_down': out['m_w_down'], 'm_final_norm_w': out['m_final_norm_w'], 'v_norm1_w': out['v_norm1_w'], 'v_w_in': out['v_w_in'], 'v_b_gate': out['v_b_gate'], 'v_attn_sinks': out['v_attn_sinks'], 'v_w_attn_o': out['v_w_attn_o'], 'v_ssd_conv_w': out['v_ssd_conv_w'], 'v_ssd_conv_b': out['v_ssd_conv_b'], 'v_dt_bias': out['v_dt_bias'], 'v_a_log': out['v_a_log'], 'v_d_skip': out['v_d_skip'], 'v_ssd_norm_w': out['v_ssd_norm_w'], 'v_w_ssd_o': out['v_w_ssd_o'], 'v_w_out': out['v_w_out'], 'v_norm2_w': out['v_norm2_w'], 'v_w_up': out['v_w_up'], 'v_ffn_conv_w': out['v_ffn_conv_w'], 'v_ffn_conv_b': out['v_ffn_conv_b'], 'v_w_down': out['v_w_down'], 'v_final_norm_w': out['v_final_norm_w']}


def _loss(weights, diff, rest, loss_target):
    with _jax.named_scope("forward"):
        args = {**rest, TWIN_DIFF_INPUT: diff, **{k: w.astype(_WEIGHT_DTYPES[k]) for k, w in weights.items()}}
        y = _forward(args)
    with _jax.named_scope("loss_head"):
        err = _jnp.square(y.astype(_jnp.float32) - loss_target)
        return 0.5 * _jnp.sum(_jnp.mean(err, axis=-1)) if err.ndim else 0.5 * err


def _adamw(w, g, m, v):
    m = ADAM_B1 * m + (1.0 - ADAM_B1) * g
    v = ADAM_B2 * v + (1.0 - ADAM_B2) * _jnp.square(g)
    m_hat = m / (1.0 - ADAM_B1 ** ADAM_STEP)
    v_hat = v / (1.0 - ADAM_B2 ** ADAM_STEP)
    delta = -ADAM_LR * (m_hat / (_jnp.sqrt(v_hat) + ADAM_EPS) + ADAM_WD * w)
    return delta, m, v


def reference(x, norm1_w, w_in, b_gate, attn_sinks, w_attn_o, ssd_conv_w, ssd_conv_b, dt_bias, a_log, d_skip, ssd_norm_w, w_ssd_o, w_out, norm2_w, w_up, ffn_conv_w, ffn_conv_b, w_down, final_norm_w, loss_target, m_norm1_w, m_w_in, m_b_gate, m_attn_sinks, m_w_attn_o, m_ssd_conv_w, m_ssd_conv_b, m_dt_bias, m_a_log, m_d_skip, m_ssd_norm_w, m_w_ssd_o, m_w_out, m_norm2_w, m_w_up, m_ffn_conv_w, m_ffn_conv_b, m_w_down, m_final_norm_w, v_norm1_w, v_w_in, v_b_gate, v_attn_sinks, v_w_attn_o, v_ssd_conv_w, v_ssd_conv_b, v_dt_bias, v_a_log, v_d_skip, v_ssd_norm_w, v_w_ssd_o, v_w_out, v_norm2_w, v_w_up, v_ffn_conv_w, v_ffn_conv_b, v_w_down, v_final_norm_w):
    given = dict(x=x, norm1_w=norm1_w, w_in=w_in, b_gate=b_gate, attn_sinks=attn_sinks, w_attn_o=w_attn_o, ssd_conv_w=ssd_conv_w, ssd_conv_b=ssd_conv_b, dt_bias=dt_bias, a_log=a_log, d_skip=d_skip, ssd_norm_w=ssd_norm_w, w_ssd_o=w_ssd_o, w_out=w_out, norm2_w=norm2_w, w_up=w_up, ffn_conv_w=ffn_conv_w, ffn_conv_b=ffn_conv_b, w_down=w_down, final_norm_w=final_norm_w, loss_target=loss_target, m_norm1_w=m_norm1_w, m_w_in=m_w_in, m_b_gate=m_b_gate, m_attn_sinks=m_attn_sinks, m_w_attn_o=m_w_attn_o, m_ssd_conv_w=m_ssd_conv_w, m_ssd_conv_b=m_ssd_conv_b, m_dt_bias=m_dt_bias, m_a_log=m_a_log, m_d_skip=m_d_skip, m_ssd_norm_w=m_ssd_norm_w, m_w_ssd_o=m_w_ssd_o, m_w_out=m_w_out, m_norm2_w=m_norm2_w, m_w_up=m_w_up, m_ffn_conv_w=m_ffn_conv_w, m_ffn_conv_b=m_ffn_conv_b, m_w_down=m_w_down, m_final_norm_w=m_final_norm_w, v_norm1_w=v_norm1_w, v_w_in=v_w_in, v_b_gate=v_b_gate, v_attn_sinks=v_attn_sinks, v_w_attn_o=v_w_attn_o, v_ssd_conv_w=v_ssd_conv_w, v_ssd_conv_b=v_ssd_conv_b, v_dt_bias=v_dt_bias, v_a_log=v_a_log, v_d_skip=v_d_skip, v_ssd_norm_w=v_ssd_norm_w, v_w_ssd_o=v_w_ssd_o, v_w_out=v_w_out, v_norm2_w=v_norm2_w, v_w_up=v_w_up, v_ffn_conv_w=v_ffn_conv_w, v_ffn_conv_b=v_ffn_conv_b, v_w_down=v_w_down, v_final_norm_w=v_final_norm_w)
    weights = {n: given[n] for n in TWIN_WEIGHTS}
    shared = {n: given[n] for n in SHARED_INPUTS}
    per_example = {n: given[n] for n in ['x']}
    grad_fn = _jax.value_and_grad(_loss, argnums=(0, 1))

    def one_microbatch(ex, loss_target):
        ex = dict(ex)
        diff = ex.pop(TWIN_DIFF_INPUT)
        return grad_fn(weights, diff, {**shared, **ex}, loss_target)

    if N_MICROBATCH == 1:
        loss, (grad_w, grad_x) = one_microbatch(per_example, given["loss_target"])
    else:
        def body(carry, xs):
            loss_sum, grad_sum = carry
            l_k, (gw_k, gx_k) = one_microbatch(xs[0], xs[1])
            with _jax.named_scope("update"):
                return (loss_sum + l_k, _jax.tree.map(_jnp.add, grad_sum, gw_k)), gx_k

        init = (_jnp.zeros((), _jnp.float32), _jax.tree.map(_jnp.zeros_like, weights))
        (loss, grad_w), grad_x = _jax.lax.scan(body, init, (per_example, given["loss_target"]))
    with _jax.named_scope("update"):
        delta_w, new_m, new_v = {}, {}, {}
        for n in TWIN_WEIGHTS:
            delta_w[n], new_m[n], new_v[n] = _adamw(weights[n], grad_w[n], given["m_" + n], given["v_" + n])
    return (loss, grad_x, *[grad_w[n] for n in TWIN_WEIGHTS], *[delta_w[n] for n in TWIN_WEIGHTS],
            *[new_m[n] for n in TWIN_WEIGHTS], *[new_v[n] for n in TWIN_WEIGHTS])
```

```python
import functools

import jax
import jax.numpy as jnp
from jax import lax
from jax.experimental import pallas as pl
from jax.experimental.pallas import tpu as pltpu

F32 = jnp.float32
BF16 = jnp.bfloat16
HI = lax.Precision.HIGHEST

D_MODEL = 1024
Q_DIM = 1024
KV_DIM = 256
D_INNER = 2048
BC_DIM = 512
XBC_DIM = 3072
N_SSD_HEADS = 32
D_FF = 2816
IN_DIM = 8736
BLK = 128
EPS = 1e-5
NEG = -1e30

O_Q, O_K, O_V, O_Z, O_X, O_GA, O_GS, O_DT = 0, 1024, 1280, 1536, 3584, 6656, 7680, 8704
PW = 8960

ADAM_LR, ADAM_B1, ADAM_B2, ADAM_EPS, ADAM_WD, ADAM_STEP = 0.001, 0.9, 0.999, 1e-08, 0.01, 10

VMEM_LIMIT = 52 * 1024 * 1024
MESH = pl.DeviceIdType.MESH


def _cp(sem=None):
    return pltpu.CompilerParams(dimension_semantics=sem, vmem_limit_bytes=VMEM_LIMIT)


def _dot(a, b, prec=None):
    return jnp.dot(a, b, preferred_element_type=F32, precision=prec)


def _dot_nt(a, b, prec=None):
    return lax.dot_general(a, b, (((1,), (1,)), ((), ())), preferred_element_type=F32, precision=prec)


def _dot_tn(a, b, prec=None):
    return lax.dot_general(a, b, (((0,), (0,)), ((), ())), preferred_element_type=F32, precision=prec)


def _sigmoid(x):
    return 1.0 / (1.0 + jnp.exp(-x))


def _tile(n, want):
    t = min(n, want)
    while n % t:
        t -= 128
    return t


def _mm(a, b, *, name, ta=False, tb=False, out_dtype=F32, resid=None, tm=1024, tn=1024, tk=1024):
    m, k = (a.shape[1], a.shape[0]) if ta else a.shape
    n = b.shape[0] if tb else b.shape[1]
    tm, tn, tk = _tile(m, tm), _tile(n, tn), _tile(k, tk)
    nk = k // tk
    dn = (((0 if ta else 1,), (1 if tb else 0,)), ((), ()))

    def body(*refs):
        if resid is None:
            a_ref, b_ref, o_ref, acc = refs
        else:
            a_ref, b_ref, r_ref, o_ref, acc = refs
        kk = pl.program_id(2)
        part = lax.dot_general(a_ref[...].astype(BF16), b_ref[...].astype(BF16), dn, preferred_element_type=F32)

        @pl.when(kk == 0)
        def _():
            acc[...] = part

        @pl.when(kk > 0)
        def _():
            acc[...] += part

        @pl.when(kk == nk - 1)
        def _():
            r = acc[...]
            if resid is not None:
                r = r + r_ref[...]
            o_ref[...] = r.astype(out_dtype)

    a_spec = pl.BlockSpec((tk, tm), lambda i, j, q: (q, i)) if ta else pl.BlockSpec((tm, tk), lambda i, j, q: (i, q))
    b_spec = pl.BlockSpec((tn, tk), lambda i, j, q: (j, q)) if tb else pl.BlockSpec((tk, tn), lambda i, j, q: (q, j))
    o_spec = pl.BlockSpec((tm, tn), lambda i, j, q: (i, j))
    ins, specs = [a, b], [a_spec, b_spec]
    if resid is not None:
        ins.append(resid)
        specs.append(o_spec)
    return pl.pallas_call(
        body, name=name, grid=(m // tm, n // tn, nk), in_specs=specs, out_specs=o_spec,
        out_shape=jax.ShapeDtypeStruct((m, n), out_dtype), scratch_shapes=[pltpu.VMEM((tm, tn), F32)],
        compiler_params=_cp(("parallel", "parallel", "arbitrary")),
    )(*ins)


def _rms_fwd(x, w, *, name, tm=512):
    s, d = x.shape
    tm = _tile(s, tm)

    def body(x_ref, w_ref, o_ref):
        xv = x_ref[...]
        r = lax.rsqrt(jnp.mean(xv * xv, axis=-1, keepdims=True) + EPS)
        o_ref[...] = ((xv * r) * w_ref[...]).astype(BF16)

    return pl.pallas_call(
        body, name=name, grid=(s // tm,),
        in_specs=[pl.BlockSpec((tm, d), lambda i: (i, 0)), pl.BlockSpec((1, d), lambda i: (0, 0))],
        out_specs=pl.BlockSpec((tm, d), lambda i: (i, 0)),
        out_shape=jax.ShapeDtypeStruct((s, d), BF16), compiler_params=_cp(("parallel",)),
    )(x, w)


def _rms_bwd(dy, x, w, resid, *, name, tm=512):
    s, d = x.shape
    tm = _tile(s, tm)

    def body(dy_ref, x_ref, w_ref, r_ref, dx_ref, dw_ref):
        i = pl.program_id(0)
        xv = x_ref[...]
        r = lax.rsqrt(jnp.mean(xv * xv, axis=-1, keepdims=True) + EPS)
        xh = xv * r
        dyv = dy_ref[...]
        g = dyv * w_ref[...]
        dx_ref[...] = r_ref[...] + r * (g - xh * jnp.mean(g * xh, axis=-1, keepdims=True))
        part = jnp.sum(dyv * xh, axis=0, keepdims=True)

        @pl.when(i == 0)
        def _():
            dw_ref[...] = part

        @pl.when(i > 0)
        def _():
            dw_ref[...] += part

    row = pl.BlockSpec((tm, d), lambda i: (i, 0))
    vec = pl.BlockSpec((1, d), lambda i: (0, 0))
    return pl.pallas_call(
        body, name=name, grid=(s // tm,), in_specs=[row, row, vec, row], out_specs=[row, vec],
        out_shape=[jax.ShapeDtypeStruct((s, d), F32), jax.ShapeDtypeStruct((1, d), F32)],
        compiler_params=_cp(("arbitrary",)),
    )(dy, x, w, resid)


def _loss_bwd(h2, tgt, wf, *, tm=512):
    s, d = h2.shape
    tm = _tile(s, tm)

    def body(h_ref, t_ref, w_ref, dh_ref, loss_ref, dw_ref):
        i = pl.program_id(0)
        hv = h_ref[...]
        r = lax.rsqrt(jnp.mean(hv * hv, axis=-1, keepdims=True) + EPS)
        xh = hv * r
        wv = w_ref[...]
        e = xh * wv - t_ref[...]
        lpart = 0.5 * jnp.sum(jnp.mean(e * e, axis=-1, keepdims=True), axis=0, keepdims=True)
        dout = e * (1.0 / d)
        g = dout * wv
        dh_ref[...] = r * (g - xh * jnp.mean(g * xh, axis=-1, keepdims=True))
        part = jnp.sum(dout * xh, axis=0, keepdims=True)
        lrow = jnp.broadcast_to(lpart, (1, 128))

        @pl.when(i == 0)
        def _():
            dw_ref[...] = part
            loss_ref[...] = lrow

        @pl.when(i > 0)
        def _():
            dw_ref[...] += part
            loss_ref[...] += lrow

    row = pl.BlockSpec((tm, d), lambda i: (i, 0))
    vec = pl.BlockSpec((1, d), lambda i: (0, 0))
    return pl.pallas_call(
        body, name="loss_bwd", grid=(s // tm,), in_specs=[row, row, vec],
        out_specs=[row, pl.BlockSpec((1, 128), lambda i: (0, 0)), vec],
        out_shape=[jax.ShapeDtypeStruct((s, d), F32), jax.ShapeDtypeStruct((1, 128), F32),
                   jax.ShapeDtypeStruct((1, d), F32)],
        compiler_params=_cp(("arbitrary",)),
    )(h2, tgt, wf)


def _attn_mask(n):
    qi = lax.broadcasted_iota(jnp.int32, (4 * BLK, 2 * BLK), 0) & (BLK - 1)
    si = lax.broadcasted_iota(jnp.int32, (4 * BLK, 2 * BLK), 1)
    dist = BLK + qi - si
    kpos = n * BLK - BLK + si
    return (dist >= 0) & (dist < BLK) & (kpos >= 0)


def _attn_probs(q_ref, kc_ref, kp_ref, sk_ref, kvh, valid):
    hs = slice(kvh * 64, (kvh + 1) * 64)
    kb = jnp.concatenate([kp_ref[:, hs], kc_ref[:, hs]], axis=0).astype(BF16)
    qs = jnp.concatenate([q_ref[:, (kvh * 4 + g) * 64:(kvh * 4 + g + 1) * 64] for g in range(4)], axis=0).astype(BF16)
    s = _dot_nt(qs, kb) * 0.125
    s = jnp.where(valid, s, NEG)
    sink = jnp.concatenate(
        [jnp.broadcast_to(sk_ref[0:1, kvh * 4 + g:kvh * 4 + g + 1], (BLK, 1)) for g in range(4)], axis=0)
    m = jnp.maximum(jnp.max(s, axis=1, keepdims=True), sink)
    p = jnp.where(valid, jnp.exp(s - m), 0.0)
    es = jnp.exp(sink - m)
    denom = jnp.sum(p, axis=1, keepdims=True) + es
    return qs, kb, p / denom, es / denom


def _attn_fwd(proj, sinks):
    s = proj.shape[0]
    nb = s // BLK

    def body(q_ref, kc_ref, kp_ref, vc_ref, vp_ref, sk_ref, o_ref):
        valid = _attn_mask(pl.program_id(0))
        for kvh in range(4):
            hs = slice(kvh * 64, (kvh + 1) * 64)
            _, _, probs, _ = _attn_probs(q_ref, kc_ref, kp_ref, sk_ref, kvh, valid)
            vb = jnp.concatenate([vp_ref[:, hs], vc_ref[:, hs]], axis=0).astype(BF16)
            o = _dot(probs.astype(BF16), vb)
            for g in range(4):
                h = kvh * 4 + g
                o_ref[:, h * 64:(h + 1) * 64] = o[g * BLK:(g + 1) * BLK].astype(BF16)

    prev = lambda n: jnp.maximum(n - 1, 0)
    return pl.pallas_call(
        body, name="attn_fwd", grid=(nb,),
        in_specs=[pl.BlockSpec((BLK, Q_DIM), lambda n: (n, 0)),
                  pl.BlockSpec((BLK, KV_DIM), lambda n: (n, O_K // KV_DIM)),
                  pl.BlockSpec((BLK, KV_DIM), lambda n: (prev(n), O_K // KV_DIM)),
                  pl.BlockSpec((BLK, KV_DIM), lambda n: (n, O_V // KV_DIM)),
                  pl.BlockSpec((BLK, KV_DIM), lambda n: (prev(n), O_V // KV_DIM)),
                  pl.BlockSpec((1, 128), lambda n: (0, 0))],
        out_specs=pl.BlockSpec((BLK, Q_DIM), lambda n: (n, 0)),
        out_shape=jax.ShapeDtypeStruct((s, Q_DIM), BF16), compiler_params=_cp(("parallel",)),
    )(proj, proj, proj, proj, proj, sinks)


def _attn_bwd(proj, sinks, o, do):
    s = proj.shape[0]
    nb = s // BLK

    def body(q_ref, kc_ref, kp_ref, vc_ref, vp_ref, sk_ref, o_ref, do_ref,
             dq_ref, dk_ref, dv_ref, dsk_ref, ck, cv, nkp, nkc, nvp, nvc):
        n = pl.program_id(0)

        @pl.when(n == 0)
        def _():
            ck[...] = jnp.zeros_like(ck)
            cv[...] = jnp.zeros_like(cv)
            dsk_ref[...] = jnp.zeros_like(dsk_ref)

        @pl.when(n < nb)
        def _():
            valid = _attn_mask(n)
            lane = lax.broadcasted_iota(jnp.int32, (1, 128), 1)
            dsk = jnp.zeros((1, 128), F32)
            for kvh in range(4):
                hs = slice(kvh * 64, (kvh + 1) * 64)
                qs, kb, probs, psink = _attn_probs(q_ref, kc_ref, kp_ref, sk_ref, kvh, valid)
                vb = jnp.concatenate([vp_ref[:, hs], vc_ref[:, hs]], axis=0).astype(BF16)
                heads = [slice((kvh * 4 + g) * 64, (kvh * 4 + g + 1) * 64) for g in range(4)]
                dos = jnp.concatenate([do_ref[:, hh] for hh in heads], axis=0)
                os_ = jnp.concatenate([o_ref[:, hh] for hh in heads], axis=0).astype(F32)
                delta = jnp.sum(dos * os_, axis=1, keepdims=True)
                dos16 = dos.astype(BF16)
                dp = _dot_nt(dos16, vb)
                ds = (probs * (dp - delta) * 0.125).astype(BF16)
                dqs = _dot(ds, kb)
                dkb = _dot_tn(ds, qs)
                dvb = _dot_tn(probs.astype(BF16), dos16)
                nkp[:, hs] = dkb[:BLK]
                nkc[:, hs] = dkb[BLK:]
                nvp[:, hs] = dvb[:BLK]
                nvc[:, hs] = dvb[BLK:]
                sd = psink * delta
                for g in range(4):
                    dq_ref[:, heads[g]] = dqs[g * BLK:(g + 1) * BLK].astype(BF16)
                    val = -jnp.sum(sd[g * BLK:(g + 1) * BLK], axis=0, keepdims=True)
                    dsk = dsk + jnp.where(lane == kvh * 4 + g, val, 0.0)
            dsk_ref[0:1, :] += dsk
            dk_ref[...] = (ck[...] + nkp[...]).astype(BF16)
            dv_ref[...] = (cv[...] + nvp[...]).astype(BF16)
            ck[...] = nkc[...]
            cv[...] = nvc[...]

        @pl.when(n == nb)
        def _():
            dk_ref[...] = ck[...].astype(BF16)
            dv_ref[...] = cv[...].astype(BF16)

    cur = lambda n: jnp.minimum(n, nb - 1)
    prev = lambda n: jnp.maximum(jnp.minimum(n, nb - 1) - 1, 0)
    outb = lambda n: jnp.maximum(n - 1, 0)
    kv_scr = pltpu.VMEM((BLK, KV_DIM), F32)
    return pl.pallas_call(
        body, name="attn_bwd", grid=(nb + 1,),
        in_specs=[pl.BlockSpec((BLK, Q_DIM), lambda n: (cur(n), 0)),
                  pl.BlockSpec((BLK, KV_DIM), lambda n: (cur(n), O_K // KV_DIM)),
                  pl.BlockSpec((BLK, KV_DIM), lambda n: (prev(n), O_K // KV_DIM)),
                  pl.BlockSpec((BLK, KV_DIM), lambda n: (cur(n), O_V // KV_DIM)),
                  pl.BlockSpec((BLK, KV_DIM), lambda n: (prev(n), O_V // KV_DIM)),
                  pl.BlockSpec((1, 128), lambda n: (0, 0)),
                  pl.BlockSpec((BLK, Q_DIM), lambda n: (cur(n), 0)),
                  pl.BlockSpec((BLK, Q_DIM), lambda n: (cur(n), 0))],
        out_specs=[pl.BlockSpec((BLK, Q_DIM), lambda n: (cur(n), 0)),
                   pl.BlockSpec((BLK, KV_DIM), lambda n: (outb(n), 0)),
                   pl.BlockSpec((BLK, KV_DIM), lambda n: (outb(n), 0)),
                   pl.BlockSpec((8, 128), lambda n: (0, 0))],
        out_shape=[jax.ShapeDtypeStruct((s, Q_DIM), BF16), jax.ShapeDtypeStruct((s, KV_DIM), BF16),
                   jax.ShapeDtypeStruct((s, KV_DIM), BF16), jax.ShapeDtypeStruct((8, 128), F32)],
        scratch_shapes=[kv_scr] * 6, compiler_params=_cp(("arbitrary",)),
    )(proj, proj, proj, proj, proj, sinks, o, do)


def _shift_down(x, j):
    if j == 0:
        return x
    row = lax.broadcasted_iota(jnp.int32, x.shape, 0)
    return jnp.where(row >= j, pltpu.roll(x, j, 0), 0.0)


def _shift_up(x, j):
    if j == 0:
        return x
    s = x.shape[0]
    row = lax.broadcasted_iota(jnp.int32, x.shape, 0)
    return jnp.where(row < s - j, pltpu.roll(x, s - j, 0), 0.0)


def _conv(x, w_ref, b_ref):
    kk = w_ref.shape[0]
    y = _shift_down(x, kk - 1) * w_ref[0:1, :]
    for q in range(1, kk):
        y = y + _shift_down(x, kk - 1 - q) * w_ref[q:q + 1, :]
    return y + b_ref[...]


def _conv_bwd(dy, x, w_ref, dx_dtype):
    kk = w_ref.shape[0]
    dx = _shift_up(dy, kk - 1) * w_ref[0:1, :]
    dws = [jnp.sum(dy * _shift_down(x, kk - 1), axis=0, keepdims=True)]
    for q in range(1, kk):
        dx = dx + _shift_up(dy, kk - 1 - q) * w_ref[q:q + 1, :]
        dws.append(jnp.sum(dy * _shift_down(x, kk - 1 - q), axis=0, keepdims=True))
    return dx.astype(dx_dtype), dws, jnp.sum(dy, axis=0, keepdims=True)


def _dsilu(y, sg):
    return sg * (1.0 + y * (1.0 - sg))


CT = 256


def _ssd_conv_fwd(proj, w, b):
    s = proj.shape[0]

    def body(x_ref, w_ref, b_ref, o_ref):
        y = _conv(x_ref[...], w_ref, b_ref)
        o_ref[...] = y * _sigmoid(y)

    return pl.pallas_call(
        body, name="ssd_conv_fwd", grid=(XBC_DIM // CT,),
        in_specs=[pl.BlockSpec((s, CT), lambda i: (0, O_X // CT + i)), pl.BlockSpec((4, CT), lambda i: (0, i)),
                  pl.BlockSpec((1, CT), lambda i: (0, i))],
        out_specs=pl.BlockSpec((s, CT), lambda i: (0, i)),
        out_shape=jax.ShapeDtypeStruct((s, XBC_DIM), F32), compiler_params=_cp(("parallel",)),
    )(proj, w, b)


def _ssd_conv_bwd(dact, proj, w, b):
    s = proj.shape[0]

    def body(d_ref, x_ref, w_ref, b_ref, dx_ref, dw_ref, db_ref):
        x = x_ref[...]
        y = _conv(x, w_ref, b_ref)
        dy = d_ref[...] * _dsilu(y, _sigmoid(y))
        dx, dws, db = _conv_bwd(dy, x, w_ref, BF16)
        dx_ref[...] = dx
        for q in range(4):
            dw_ref[q:q + 1, :] = dws[q]
        db_ref[...] = db

    return pl.pallas_call(
        body, name="ssd_conv_bwd", grid=(XBC_DIM // CT,),
        in_specs=[pl.BlockSpec((s, CT), lambda i: (0, i)), pl.BlockSpec((s, CT), lambda i: (0, O_X // CT + i)),
                  pl.BlockSpec((4, CT), lambda i: (0, i)), pl.BlockSpec((1, CT), lambda i: (0, i))],
        out_specs=[pl.BlockSpec((s, CT), lambda i: (0, i)), pl.BlockSpec((4, CT), lambda i: (0, i)),
                   pl.BlockSpec((1, CT), lambda i: (0, i))],
        out_shape=[jax.ShapeDtypeStruct((s, XBC_DIM), BF16), jax.ShapeDtypeStruct((4, XBC_DIM), F32),
                   jax.ShapeDtypeStruct((1, XBC_DIM), F32)],
        compiler_params=_cp(("parallel",)),
    )(dact, proj, w, b)


NFT = D_FF // CT


def _ffn_act_fwd(up, w, b):
    s = up.shape[0]

    def body(v_ref, g_ref, wv_ref, wg_ref, bv_ref, bg_ref, o_ref):
        val = _conv(v_ref[...], wv_ref, bv_ref)
        gt = _conv(g_ref[...], wg_ref, bg_ref)
        o_ref[...] = ((gt * _sigmoid(gt)) * val).astype(BF16)

    col = lambda off: (lambda i: (0, off + i))
    return pl.pallas_call(
        body, name="ffn_act_fwd", grid=(NFT,),
        in_specs=[pl.BlockSpec((s, CT), col(0)), pl.BlockSpec((s, CT), col(NFT)),
                  pl.BlockSpec((3, CT), col(0)), pl.BlockSpec((3, CT), col(NFT)),
                  pl.BlockSpec((1, CT), col(0)), pl.BlockSpec((1, CT), col(NFT))],
        out_specs=pl.BlockSpec((s, CT), col(0)),
        out_shape=jax.ShapeDtypeStruct((s, D_FF), BF16), compiler_params=_cp(("parallel",)),
    )(up, up, w, w, b, b)


def _ffn_act_bwd(dact, up, w, b):
    s = up.shape[0]

    def body(d_ref, v_ref, g_ref, wv_ref, wg_ref, bv_ref, bg_ref, dx_ref, dw_ref, db_ref):
        i = pl.program_id(0)
        xv, xg = v_ref[...], g_ref[...]
        val = _conv(xv, wv_ref, bv_ref)
        gt = _conv(xg, wg_ref, bg_ref)
        sg = _sigmoid(gt)
        d = d_ref[...]

        def emit(dy, x, w_ref):
            dx, dws, db = _conv_bwd(dy, x, w_ref, BF16)
            dx_ref[...] = dx
            for q in range(3):
                dw_ref[q:q + 1, :] = dws[q]
            db_ref[...] = db

        @pl.when(i < NFT)
        def _():
            emit(d * (gt * sg), xv, wv_ref)

        @pl.when(i >= NFT)
        def _():
            emit(d * val * _dsilu(gt, sg), xg, wg_ref)

    half = lambda off: (lambda i: (0, off + i % NFT))
    own = lambda i: (0, i)
    return pl.pallas_call(
        body, name="ffn_act_bwd", grid=(2 * NFT,),
        in_specs=[pl.BlockSpec((s, CT), half(0)), pl.BlockSpec((s, CT), half(0)), pl.BlockSpec((s, CT), half(NFT)),
                  pl.BlockSpec((3, CT), half(0)), pl.BlockSpec((3, CT), half(NFT)),
                  pl.BlockSpec((1, CT), half(0)), pl.BlockSpec((1, CT), half(NFT))],
        out_specs=[pl.BlockSpec((s, CT), own), pl.BlockSpec((3, CT), own), pl.BlockSpec((1, CT), own)],
        out_shape=[jax.ShapeDtypeStruct((s, 2 * D_FF), BF16), jax.ShapeDtypeStruct((3, 2 * D_FF), F32),
                   jax.ShapeDtypeStruct((1, 2 * D_FF), F32)],
        compiler_params=_cp(("parallel",)),
    )(dact, up, up, w, w, b, b)


def _expand_mat():
    r = lax.broadcasted_iota(jnp.int32, (128, D_INNER), 0)
    c = lax.broadcasted_iota(jnp.int32, (128, D_INNER), 1)
    return ((c >> 6) == r).astype(F32)


def _reduce_mat():
    r = lax.broadcasted_iota(jnp.int32, (D_INNER, 128), 0)
    c = lax.broadcasted_iota(jnp.int32, (D_INNER, 128), 1)
    return ((r >> 6) == c).astype(F32)


def _tril():
    r = lax.broadcasted_iota(jnp.int32, (BLK, BLK), 0)
    c = lax.broadcasted_iota(jnp.int32, (BLK, BLK), 1)
    return r >= c


def _softplus(x):
    return jnp.maximum(x, 0.0) + jnp.log(1.0 + jnp.exp(-jnp.abs(x)))


def _ssd_common(dtraw_ref, dtb_ref, alog_ref):
    causal = _tril()
    tril = causal.astype(F32)
    e_mat = _expand_mat()
    a_neg = -jnp.exp(alog_ref[...])
    dt = _softplus(dtraw_ref[...] + dtb_ref[...])
    a = dt * a_neg
    a_cs = _dot(tril, a, HI)
    a_cs_t = a_cs.T
    a_last = a_cs[BLK - 1:BLK, :]
    dt_x = _dot(dt, e_mat, HI)
    ea_x = jnp.exp(_dot(a_cs, e_mat, HI))
    ds_x = jnp.exp(_dot(a_last - a_cs, e_mat, HI))
    ealast_col = jnp.exp(a_cs_t[:, BLK - 1:BLK])
    st_scale = _dot(_reduce_mat(), jnp.broadcast_to(ealast_col, (128, 128)), HI)
    return causal, e_mat, a_neg, dt, a_cs, a_cs_t, dt_x, ea_x, ds_x, st_scale


def _decay(a_cs, a_cs_t, h, causal):
    seg = a_cs[:, h:h + 1] - a_cs_t[h:h + 1, :]
    return jnp.where(causal, jnp.exp(jnp.where(causal, seg, 0.0)), 0.0)


def _ssd_fwd(xbc, proj, dt_bias, a_log, d_skip):
    s = xbc.shape[0]
    nc = s // BLK

    def body(xs_ref, b_ref, c_ref, dtraw_ref, dtb_ref, alog_ref, dskip_ref, y_ref, hp_ref, h_scr, xc16):
        @pl.when(pl.program_id(0) == 0)
        def _():
            h_scr[...] = jnp.zeros_like(h_scr)

        causal, e_mat, _, _, a_cs, a_cs_t, dt_x, ea_x, ds_x, st_scale = _ssd_common(dtraw_ref, dtb_ref, alog_ref)
        dskip_x = _dot(jnp.broadcast_to(dskip_ref[...], (8, 128)), e_mat, HI)[0:1]
        xs = xs_ref[...]
        xc = xs * dt_x
        xc16[...] = xc.astype(BF16)
        xcd = (xc * ds_x).astype(BF16)
        hp_ref[0] = h_scr[...]
        for g in range(4):
            gs = slice(g * 512, (g + 1) * 512)
            cg = c_ref[:, g * 128:(g + 1) * 128].astype(BF16)
            bg = b_ref[:, g * 128:(g + 1) * 128].astype(BF16)
            cb = _dot_nt(cg, bg)
            hg = h_scr[gs, :]
            yoff = _dot_nt(cg, hg.astype(BF16)) * ea_x[:, gs]
            for j in range(8):
                h = g * 8 + j
                hsl = slice(h * 64, (h + 1) * 64)
                mm = (cb * _decay(a_cs, a_cs_t, h, causal)).astype(BF16)
                y_ref[:, hsl] = _dot(mm, xc16[:, hsl])
            y_ref[:, gs] += yoff + xs[:, gs] * dskip_x[:, gs]
            h_scr[gs, :] = hg * st_scale[gs, :] + _dot_tn(xcd[:, gs], bg)

    vec = pl.BlockSpec((1, 128), lambda c: (0, 0))
    return pl.pallas_call(
        body, name="ssd_fwd", grid=(nc,),
        in_specs=[pl.BlockSpec((BLK, D_INNER), lambda c: (c, 0)),
                  pl.BlockSpec((BLK, BC_DIM), lambda c: (c, D_INNER // BC_DIM)),
                  pl.BlockSpec((BLK, BC_DIM), lambda c: (c, D_INNER // BC_DIM + 1)),
                  pl.BlockSpec((BLK, 128), lambda c: (c, O_DT // 128)), vec, vec, vec],
        out_specs=[pl.BlockSpec((BLK, D_INNER), lambda c: (c, 0)),
                   pl.BlockSpec((1, D_INNER, 128), lambda c: (c, 0, 0))],
        out_shape=[jax.ShapeDtypeStruct((s, D_INNER), F32), jax.ShapeDtypeStruct((nc, D_INNER, 128), F32)],
        scratch_shapes=[pltpu.VMEM((D_INNER, 128), F32), pltpu.VMEM((BLK, D_INNER), BF16)],
        compiler_params=_cp(("arbitrary",)),
    )(xbc, xbc, xbc, proj, dt_bias, a_log, d_skip)


def _ssd_bwd(xbc, proj, dt_bias, a_log, d_skip, hprev, dy):
    s = xbc.shape[0]
    nc = s // BLK

    def body(xs_ref, b_ref, c_ref, dtraw_ref, dtb_ref, alog_ref, dskip_ref, hp_ref, dy_ref,
             dxbc_ref, ddt_ref, dvec_ref, dh_scr, xc16, dy16, dxc_scr, dacs_r):
        step = pl.program_id(0)
        dacs_r[...] = jnp.zeros_like(dacs_r)

        @pl.when(step == 0)
        def _():
            dh_scr[...] = jnp.zeros_like(dh_scr)
            dvec_ref[...] = jnp.zeros_like(dvec_ref)

        causal, e_mat, a_neg, dt, a_cs, a_cs_t, dt_x, ea_x, ds_x, st_scale = _ssd_common(dtraw_ref, dtb_ref, alog_ref)
        r_mat = _reduce_mat()
        lane = lax.broadcasted_iota(jnp.int32, (1, 128), 1)
        dskip_x = _dot(jnp.broadcast_to(dskip_ref[...], (8, 128)), e_mat, HI)[0:1]
        xs = xs_ref[...]
        dy = dy_ref[...]
        xc = xs * dt_x
        xcd = xc * ds_x
        xc16[...] = xc.astype(BF16)
        dy16[...] = dy.astype(BF16)
        dyea = dy * ea_x
        dh = dh_scr[...]
        hp = hp_ref[0]
        dalast_rec = jnp.sum(_dot(e_mat, dh * hp * st_scale, HI), axis=1, keepdims=True)
        dacs = jnp.zeros((BLK, 128), F32)
        t0 = jnp.zeros((BLK, 128), F32)
        t1 = jnp.zeros((BLK, 128), F32)
        for g in range(4):
            gs = slice(g * 512, (g + 1) * 512)
            bsl = slice(g * 128, (g + 1) * 128)
            cg = c_ref[:, bsl].astype(BF16)
            bg = b_ref[:, bsl].astype(BF16)
            cb = _dot_nt(cg, bg)
            hg16 = hp[gs, :].astype(BF16)
            dhg16 = dh[gs, :].astype(BF16)
            raw = _dot_nt(cg, hg16)
            draw16 = dyea[:, gs].astype(BF16)
            dcg = _dot(draw16, hg16)
            dhp_g = _dot_tn(draw16, cg)
            dbg = _dot(xcd[:, gs].astype(BF16), dhg16)
            dxcd = _dot_nt(bg, dhg16)
            dcb = jnp.zeros((BLK, BLK), F32)
            for j in range(8):
                h = g * 8 + j
                hsl = slice(h * 64, (h + 1) * 64)
                decay = _decay(a_cs, a_cs_t, h, causal)
                m = cb * decay
                dm = _dot_nt(dy16[:, hsl], xc16[:, hsl])
                dxc_scr[:, hsl] = _dot_tn(m.astype(BF16), dy16[:, hsl])
                dcb = dcb + dm * decay
                dseg = dm * m
                oneh = jnp.where(lane == h, 1.0, 0.0)
                dacs = dacs + jnp.sum(dseg, axis=1, keepdims=True) * oneh
                dacs_r[h:h + 1, :] = jnp.sum(dseg, axis=0, keepdims=True)
            dcb16 = dcb.astype(BF16)
            dcg = dcg + _dot(dcb16, bg)
            dbg = dbg + _dot_tn(dcb16, cg)
            dxbc_ref[:, D_INNER + g * 128:D_INNER + (g + 1) * 128] = dbg
            dxbc_ref[:, D_INNER + BC_DIM + g * 128:D_INNER + BC_DIM + (g + 1) * 128] = dcg
            dxc_scr[:, gs] += dxcd * ds_x[:, gs]
            dh_scr[gs, :] = dh[gs, :] * st_scale[gs, :] + dhp_g
            t0 = t0 + _dot(dy[:, gs] * (raw * ea_x[:, gs]), r_mat[gs, :], HI)
            t1 = t1 + _dot(dxcd * xcd[:, gs], r_mat[gs, :], HI)
        dxc = dxc_scr[...]
        dacs = dacs + t0 - t1 - dacs_r[...].T
        dalast_row = jnp.broadcast_to(dalast_rec, (128, 128)).T[0:1] + jnp.sum(t1, axis=0, keepdims=True)
        row = lax.broadcasted_iota(jnp.int32, (BLK, 128), 0)
        dacs = dacs + jnp.where(row == BLK - 1, dalast_row, 0.0)
        da = _dot_tn(causal.astype(F32), dacs, HI)
        ddt = da * a_neg + _dot(dxc * xs, r_mat, HI)
        lmask = lax.broadcasted_iota(jnp.int32, (BLK, 128), 1) < N_SSD_HEADS
        ddtraw = jnp.where(lmask, ddt * _sigmoid(dtraw_ref[...] + dtb_ref[...]), 0.0)
        ddt_ref[...] = ddtraw.astype(BF16)
        dxbc_ref[:, 0:D_INNER] = dy * dskip_x + dxc * dt_x
        dvec_ref[0:1, :] += jnp.sum(ddtraw, axis=0, keepdims=True)
        dvec_ref[1:2, :] += jnp.where(lane < N_SSD_HEADS, jnp.sum(da * dt, axis=0, keepdims=True) * a_neg, 0.0)
        dvec_ref[2:3, :] += _dot(jnp.broadcast_to(jnp.sum(dy * xs, axis=0, keepdims=True), (8, D_INNER)), r_mat, HI)[0:1]

    rev = lambda c: nc - 1 - c
    vec = pl.BlockSpec((1, 128), lambda c: (0, 0))
    return pl.pallas_call(
        body, name="ssd_bwd", grid=(nc,),
        in_specs=[pl.BlockSpec((BLK, D_INNER), lambda c: (rev(c), 0)),
                  pl.BlockSpec((BLK, BC_DIM), lambda c: (rev(c), D_INNER // BC_DIM)),
                  pl.BlockSpec((BLK, BC_DIM), lambda c: (rev(c), D_INNER // BC_DIM + 1)),
                  pl.BlockSpec((BLK, 128), lambda c: (rev(c), O_DT // 128)), vec, vec, vec,
                  pl.BlockSpec((1, D_INNER, 128), lambda c: (rev(c), 0, 0)),
                  pl.BlockSpec((BLK, D_INNER), lambda c: (rev(c), 0))],
        out_specs=[pl.BlockSpec((BLK, XBC_DIM), lambda c: (rev(c), 0)),
                   pl.BlockSpec((BLK, 128), lambda c: (rev(c), 0)),
                   pl.BlockSpec((8, 128), lambda c: (0, 0))],
        out_shape=[jax.ShapeDtypeStruct((s, XBC_DIM), F32), jax.ShapeDtypeStruct((s, 128), BF16),
                   jax.ShapeDtypeStruct((8, 128), F32)],
        scratch_shapes=[pltpu.VMEM((D_INNER, 128), F32), pltpu.VMEM((BLK, D_INNER), BF16),
                        pltpu.VMEM((BLK, D_INNER), BF16), pltpu.VMEM((BLK, D_INNER), F32),
                        pltpu.VMEM((128, BLK), F32)],
        compiler_params=_cp(("arbitrary",)),
    )(xbc, xbc, xbc, proj, dt_bias, a_log, d_skip, hprev, dy)


GW = 512


def _gate_norm_fwd(y, proj, wn, *, tm=512):
    s = y.shape[0]
    tm = _tile(s, tm)

    def body(y_ref, z_ref, w_ref, o_ref):
        z = z_ref[...]
        y2 = y_ref[...] * (z * _sigmoid(z))
        r = lax.rsqrt(jnp.mean(y2 * y2, axis=-1, keepdims=True) + EPS)
        o_ref[...] = ((y2 * r) * w_ref[...]).astype(BF16)

    return pl.pallas_call(
        body, name="gate_norm_fwd", grid=(s // tm, 4),
        in_specs=[pl.BlockSpec((tm, GW), lambda i, g: (i, g)), pl.BlockSpec((tm, GW), lambda i, g: (i, O_Z // GW + g)),
                  pl.BlockSpec((1, GW), lambda i, g: (0, g))],
        out_specs=pl.BlockSpec((tm, GW), lambda i, g: (i, g)),
        out_shape=jax.ShapeDtypeStruct((s, D_INNER), BF16), compiler_params=_cp(("parallel", "parallel")),
    )(y, proj, wn)


def _gate_norm_bwd(dyn, y, proj, wn, *, tm=512):
    s = y.shape[0]
    tm = _tile(s, tm)

    def body(d_ref, y_ref, z_ref, w_ref, dy_ref, dz_ref, dw_ref):
        i = pl.program_id(1)
        z = z_ref[...]
        sg = _sigmoid(z)
        sz = z * sg
        yv = y_ref[...]
        y2 = yv * sz
        r = lax.rsqrt(jnp.mean(y2 * y2, axis=-1, keepdims=True) + EPS)
        xh = y2 * r
        dv = d_ref[...]
        g = dv * w_ref[...]
        dy2 = r * (g - xh * jnp.mean(g * xh, axis=-1, keepdims=True))
        dy_ref[...] = dy2 * sz
        dz_ref[...] = (dy2 * yv * _dsilu(z, sg)).astype(BF16)
        part = jnp.sum(dv * xh, axis=0, keepdims=True)

        @pl.when(i == 0)
        def _():
            dw_ref[...] = part

        @pl.when(i > 0)
        def _():
            dw_ref[...] += part

    blk = pl.BlockSpec((tm, GW), lambda g, i: (i, g))
    vec = pl.BlockSpec((1, GW), lambda g, i: (0, g))
    return pl.pallas_call(
        body, name="gate_norm_bwd", grid=(4, s // tm),
        in_specs=[blk, blk, pl.BlockSpec((tm, GW), lambda g, i: (i, O_Z // GW + g)), vec],
        out_specs=[blk, blk, vec],
        out_shape=[jax.ShapeDtypeStruct((s, D_INNER), F32), jax.ShapeDtypeStruct((s, D_INNER), BF16),
                   jax.ShapeDtypeStruct((1, D_INNER), F32)],
        compiler_params=_cp(("parallel", "arbitrary")),
    )(dyn, y, proj, wn)


def _merge_fwd(proj, b_gate, attn, ssd_out, *, tm=512):
    s = attn.shape[0]
    tm = _tile(s, tm)

    def body(ga_ref, gs_ref, ba_ref, bs_ref, a_ref, s_ref, o_ref):
        ga = _sigmoid(ga_ref[...] + ba_ref[...])
        gs = _sigmoid(gs_ref[...] + bs_ref[...])
        o_ref[...] = (ga * a_ref[...] + gs * s_ref[...]).astype(BF16)

    blk = pl.BlockSpec((tm, GW), lambda i, j: (i, j))
    return pl.pallas_call(
        body, name="merge_fwd", grid=(s // tm, 2),
        in_specs=[pl.BlockSpec((tm, GW), lambda i, j: (i, O_GA // GW + j)),
                  pl.BlockSpec((tm, GW), lambda i, j: (i, O_GS // GW + j)),
                  pl.BlockSpec((1, GW), lambda i, j: (0, j)), pl.BlockSpec((1, GW), lambda i, j: (0, 2 + j)), blk, blk],
        out_specs=blk, out_shape=jax.ShapeDtypeStruct((s, D_MODEL), BF16),
        compiler_params=_cp(("parallel", "parallel")),
    )(proj, proj, b_gate, b_gate, attn, ssd_out)


def _merge_bwd(dm, proj, b_gate, attn, ssd_out, *, tm=512):
    s = attn.shape[0]
    tm = _tile(s, tm)

    def body(d_ref, ga_ref, gs_ref, ba_ref, bs_ref, a_ref, s_ref, da_ref, ds_ref, dga_ref, dgs_ref, dba_ref, dbs_ref):
        i = pl.program_id(1)
        ga = _sigmoid(ga_ref[...] + ba_ref[...])
        gs = _sigmoid(gs_ref[...] + bs_ref[...])
        d = d_ref[...]
        da_ref[...] = (d * ga).astype(BF16)
        ds_ref[...] = (d * gs).astype(BF16)
        dga = d * a_ref[...] * (ga * (1.0 - ga))
        dgs = d * s_ref[...] * (gs * (1.0 - gs))
        dga_ref[...] = dga.astype(BF16)
        dgs_ref[...] = dgs.astype(BF16)
        pa = jnp.sum(dga, axis=0, keepdims=True)
        ps = jnp.sum(dgs, axis=0, keepdims=True)

        @pl.when(i == 0)
        def _():
            dba_ref[...] = pa
            dbs_ref[...] = ps

        @pl.when(i > 0)
        def _():
            dba_ref[...] += pa
            dbs_ref[...] += ps

    blk = pl.BlockSpec((tm, GW), lambda j, i: (i, j))
    vec = pl.BlockSpec((1, GW), lambda j, i: (0, j))
    sd = jax.ShapeDtypeStruct((s, D_MODEL), BF16)
    vd = jax.ShapeDtypeStruct((1, D_MODEL), F32)
    return pl.pallas_call(
        body, name="merge_bwd", grid=(2, s // tm),
        in_specs=[blk, pl.BlockSpec((tm, GW), lambda j, i: (i, O_GA // GW + j)),
                  pl.BlockSpec((tm, GW), lambda j, i: (i, O_GS // GW + j)),
                  vec, pl.BlockSpec((1, GW), lambda j, i: (0, 2 + j)), blk, blk],
        out_specs=[blk, blk, blk, blk, vec, vec], out_shape=[sd, sd, sd, sd, vd, vd],
        compiler_params=_cp(("parallel", "arbitrary")),
    )(dm, proj, proj, b_gate, b_gate, attn, ssd_out)


def _adamw(w, g, m, v, *, name, tm=128):
    r, c = w.shape
    tm = r if (r < tm or r % tm) else tm

    def body(w_ref, g_ref, m_ref, v_ref, d_ref, nm_ref, nv_ref):
        gv = g_ref[...]
        mn = ADAM_B1 * m_ref[...] + (1.0 - ADAM_B1) * gv
        vn = ADAM_B2 * v_ref[...] + (1.0 - ADAM_B2) * (gv * gv)
        m_hat = mn / (1.0 - ADAM_B1 ** ADAM_STEP)
        v_hat = vn / (1.0 - ADAM_B2 ** ADAM_STEP)
        d_ref[...] = -ADAM_LR * (m_hat / (jnp.sqrt(v_hat) + ADAM_EPS) + ADAM_WD * w_ref[...])
        nm_ref[...] = mn
        nv_ref[...] = vn

    blk = pl.BlockSpec((tm, c), lambda i: (i, 0))
    sd = jax.ShapeDtypeStruct((r, c), F32)
    return pl.pallas_call(
        body, name=name, grid=(r // tm,), in_specs=[blk] * 4, out_specs=[blk] * 3, out_shape=[sd] * 3,
        compiler_params=_cp(("parallel",)),
    )(w, g, m, v)


ANY = pl.BlockSpec(memory_space=pl.ANY)
N_CHIPS = 4


def _chip_of(k, x, y):
    return (x ^ (k >> 1), y ^ (k & 1))


def _all_gather_weights(shard):
    r, c = shard.shape
    hr = r // 2

    def body(sh_ref, out_ref, send_sems, recv_sems, local_sem):
        x, y, cc = lax.axis_index("x"), lax.axis_index("y"), lax.axis_index("c")

        def half(px, py, pc):
            return out_ref.at[2 * px + py, pl.ds(pc * hr, hr), :]

        def copy(k, px, py, pc, to, src=None):
            return pltpu.make_async_remote_copy(
                src_ref=half(px, py, pc) if src is None else src, dst_ref=half(px, py, pc),
                send_sem=send_sems.at[k], recv_sem=recv_sems.at[k], device_id=to, device_id_type=MESH)

        mine = pltpu.make_async_copy(sh_ref, out_ref.at[2 * x + y], local_sem)
        mine.start()
        chips = [_chip_of(k, x, y) for k in (1, 2, 3)]
        first = [copy(j, x, y, cc, (*chip, cc), src=sh_ref.at[pl.ds(cc * hr, hr), :]) for j, chip in enumerate(chips)]
        for cp in first:
            cp.start()
        passed = [copy(3 + j, *chip, cc, (x, y, 1 - cc)) for j, chip in enumerate(chips)]
        for j, chip in enumerate(chips):
            copy(j, *chip, cc, (x, y, cc)).wait_recv()
            passed[j].start()
        for j, chip in enumerate(chips):
            copy(3 + j, *chip, 1 - cc, (x, y, cc)).wait_recv()
        for cp in first + passed:
            cp.wait_send()
        mine.wait()

    return pl.pallas_call(
        body, name="all_gather_weights", in_specs=[ANY], out_specs=ANY,
        out_shape=jax.ShapeDtypeStruct((N_CHIPS, r, c), shard.dtype),
        scratch_shapes=[pltpu.SemaphoreType.DMA((6,)), pltpu.SemaphoreType.DMA((6,)), pltpu.SemaphoreType.DMA],
    )(shard)


def _cast_bf16(a, *, name, tm=512):
    n, r, c = a.shape
    tm = _tile(r, tm) if r % 128 == 0 else r

    def body(a_ref, o_ref):
        o_ref[...] = a_ref[...].astype(BF16)

    blk = pl.BlockSpec((1, tm, c), lambda i, j: (i, j, 0))
    return pl.pallas_call(body, name=name, grid=(n, r // tm), in_specs=[blk], out_specs=blk,
                          out_shape=jax.ShapeDtypeStruct(a.shape, BF16), compiler_params=_cp(("parallel", "parallel")))(a)


def _pair_exchange(g16, hr):
    n, r, c = g16.shape

    def body(g_ref, out_ref, send_sem, recv_sem):
        x, y, cc = lax.axis_index("x"), lax.axis_index("y"), lax.axis_index("c")
        cp = pltpu.make_async_remote_copy(
            src_ref=g_ref.at[:, pl.ds((1 - cc) * hr, hr), :], dst_ref=out_ref, send_sem=send_sem, recv_sem=recv_sem,
            device_id=(x, y, 1 - cc), device_id_type=MESH)
        cp.start()
        cp.wait()

    return pl.pallas_call(
        body, name="grad_pair_exchange", in_specs=[ANY], out_specs=ANY,
        out_shape=jax.ShapeDtypeStruct((n, hr, c), g16.dtype),
        scratch_shapes=[pltpu.SemaphoreType.DMA, pltpu.SemaphoreType.DMA],
    )(g16)


def _pair_add(g, recv, half_idx, hr, *, tm=384):
    n, r, c = g.shape
    nt = hr // tm

    def body(hi_ref, g_ref, r_ref, o32_ref, o16_ref):
        v = g_ref[...] + r_ref[...].astype(F32)
        o32_ref[...] = v
        o16_ref[...] = v.astype(BF16)

    gs = pltpu.PrefetchScalarGridSpec(
        num_scalar_prefetch=1, grid=(n, nt),
        in_specs=[pl.BlockSpec((1, tm, c), lambda i, j, hi: (i, hi[0] * nt + j, 0)),
                  pl.BlockSpec((1, tm, c), lambda i, j, hi: (i, j, 0))],
        out_specs=[pl.BlockSpec((1, tm, c), lambda i, j, hi: (i, j, 0))] * 2)
    return pl.pallas_call(
        body, name="grad_pair_add", grid_spec=gs,
        out_shape=[jax.ShapeDtypeStruct((n, hr, c), F32), jax.ShapeDtypeStruct((n, hr, c), BF16)],
        compiler_params=_cp(("parallel", "parallel")),
    )(half_idx, g, recv)


def _chip_exchange(p16):
    n, hr, c = p16.shape

    def body(p_ref, out_ref, send_sems, recv_sems):
        x, y, cc = lax.axis_index("x"), lax.axis_index("y"), lax.axis_index("c")
        cps = []
        for j, k in enumerate((1, 2, 3)):
            px, py = _chip_of(k, x, y)
            cps.append(pltpu.make_async_remote_copy(
                src_ref=p_ref.at[2 * px + py], dst_ref=out_ref.at[j], send_sem=send_sems.at[j], recv_sem=recv_sems.at[j],
                device_id=(px, py, cc), device_id_type=MESH))
        for cp in cps:
            cp.start()
        for cp in cps:
            cp.wait()

    return pl.pallas_call(
        body, name="grad_chip_exchange", in_specs=[ANY], out_specs=ANY,
        out_shape=jax.ShapeDtypeStruct((3, hr, c), p16.dtype),
        scratch_shapes=[pltpu.SemaphoreType.DMA((3,)), pltpu.SemaphoreType.DMA((3,))],
    )(p16)


def _chip_add(p32, recv, chip_idx, *, tm=384):
    n, hr, c = p32.shape

    def body(ci_ref, p_ref, r_ref, o_ref):
        o_ref[...] = ((p_ref[0] + r_ref[0].astype(F32)) + r_ref[1].astype(F32)) + r_ref[2].astype(F32)

    gs = pltpu.PrefetchScalarGridSpec(
        num_scalar_prefetch=1, grid=(hr // tm,),
        in_specs=[pl.BlockSpec((1, tm, c), lambda j, ci: (ci[0], j, 0)), pl.BlockSpec((3, tm, c), lambda j, ci: (0, j, 0))],
        out_specs=pl.BlockSpec((tm, c), lambda j, ci: (j, 0)))
    return pl.pallas_call(
        body, name="grad_chip_add", grid_spec=gs, out_shape=jax.ShapeDtypeStruct((hr, c), F32),
        compiler_params=_cp(("parallel",)),
    )(chip_idx, p32, recv)


def _pair_gather(f):
    hr, c = f.shape

    def body(f_ref, out_ref, send_sem, recv_sem, local_sem):
        x, y, cc = lax.axis_index("x"), lax.axis_index("y"), lax.axis_index("c")
        mine = pltpu.make_async_copy(f_ref, out_ref.at[pl.ds(cc * hr, hr), :], local_sem)
        mine.start()
        cp = pltpu.make_async_remote_copy(
            src_ref=f_ref, dst_ref=out_ref.at[pl.ds(cc * hr, hr), :], send_sem=send_sem, recv_sem=recv_sem,
            device_id=(x, y, 1 - cc), device_id_type=MESH)
        cp.start()
        cp.wait()
        mine.wait()

    return pl.pallas_call(
        body, name="grad_pair_gather", in_specs=[ANY], out_specs=ANY,
        out_shape=jax.ShapeDtypeStruct((2 * hr, c), f.dtype),
        scratch_shapes=[pltpu.SemaphoreType.DMA, pltpu.SemaphoreType.DMA, pltpu.SemaphoreType.DMA],
    )(f)


def _all_reduce_small(buf):
    r, c = buf.shape

    def body(b_ref, out_ref, gat, send_sems, recv_sems):
        x, y, cc = lax.axis_index("x"), lax.axis_index("y"), lax.axis_index("c")
        me = 4 * x + 2 * y + cc
        gat[me] = b_ref[...]
        cps = []
        for k in range(1, 8):
            px, py, pc = x ^ (k >> 2), y ^ ((k >> 1) & 1), cc ^ (k & 1)
            cps.append(pltpu.make_async_remote_copy(
                src_ref=b_ref, dst_ref=gat.at[me], send_sem=send_sems.at[k - 1], recv_sem=recv_sems.at[k - 1],
                device_id=(px, py, pc), device_id_type=MESH))
        for cp in cps:
            cp.start()
        for cp in cps:
            cp.wait()
        acc = gat[0]
        for d in range(1, 8):
            acc = acc + gat[d]
        out_ref[...] = acc

    vm = pl.BlockSpec(memory_space=pltpu.VMEM)
    return pl.pallas_call(
        body, name="all_reduce_small", in_specs=[vm], out_specs=vm, out_shape=jax.ShapeDtypeStruct((r, c), F32),
        scratch_shapes=[pltpu.VMEM((8, r, c), F32), pltpu.SemaphoreType.DMA((7,)), pltpu.SemaphoreType.DMA((7,))],
        compiler_params=pltpu.CompilerParams(vmem_limit_bytes=VMEM_LIMIT),
    )(buf)


BIG = ("w_in", "w_attn_o", "w_ssd_o", "w_out", "w_up", "w_down")
BIG_ROWS = (IN_DIM // 4, Q_DIM // 4, D_INNER // 4, D_MODEL // 4, 2 * D_FF // 4, D_FF // 4)
PACK_ROWS = 5376


def _pack_shards(parts):
    rows = [p.reshape(-1, D_MODEL) for p in parts]
    pad = PACK_ROWS - sum(BIG_ROWS)
    return jnp.concatenate(rows + [jnp.zeros((pad, D_MODEL), rows[0].dtype)], axis=0)


def _unpack_shards(buf):
    out, off = [], 0
    for n in BIG_ROWS:
        out.append(buf[off:off + n])
        off += n
    return out


def _permute_cols_in(w):
    pad = jnp.zeros((w.shape[0], PW - IN_DIM), w.dtype)
    return jnp.concatenate([w[:, :6656], w[:, 6688:], w[:, 6656:6688], pad], axis=1)


def _unpermute_cols_in(g):
    return jnp.concatenate([g[:, :6656], g[:, O_DT:O_DT + 32], g[:, 6656:O_DT]], axis=1)


SMALL = ("norm1_w", "b_gate", "attn_sinks", "ssd_conv_b", "dt_bias", "a_log", "d_skip", "ssd_norm_w", "norm2_w",
         "ffn_conv_b", "final_norm_w", "ssd_conv_w", "ffn_conv_w")


def _pad128(v):
    v = v.reshape(-1)
    return jnp.pad(v, (0, (-v.shape[0]) % 128))


def _pack_small(parts):
    flat = jnp.concatenate([_pad128(p) for p in parts])
    flat = jnp.pad(flat, (0, (-flat.shape[0]) % 1024))
    return flat.reshape(-1, 128)


def _unpack_small(buf, shapes):
    flat, out, off = buf.reshape(-1), [], 0
    for shp in shapes:
        n = 1
        for q in shp:
            n *= q
        out.append(flat[off:off + n].reshape(shp))
        off += n + (-n) % 128
    return out


def _vec128(v):
    return jnp.pad(v.reshape(1, -1), ((0, 0), (0, 128 - v.shape[-1])))


def kernel(x, norm1_w, w_in, b_gate, attn_sinks, w_attn_o, ssd_conv_w, ssd_conv_b, dt_bias, a_log, d_skip, ssd_norm_w, w_ssd_o, w_out, norm2_w, w_up, ffn_conv_w, ffn_conv_b, w_down, final_norm_w, loss_target, m_norm1_w, m_w_in, m_b_gate, m_attn_sinks, m_w_attn_o, m_ssd_conv_w, m_ssd_conv_b, m_dt_bias, m_a_log, m_d_skip, m_ssd_norm_w, m_w_ssd_o, m_w_out, m_norm2_w, m_w_up, m_ffn_conv_w, m_ffn_conv_b, m_w_down, m_final_norm_w, v_norm1_w, v_w_in, v_b_gate, v_attn_sinks, v_w_attn_o, v_ssd_conv_w, v_ssd_conv_b, v_dt_bias, v_a_log, v_d_skip, v_ssd_norm_w, v_w_ssd_o, v_w_out, v_norm2_w, v_w_up, v_ffn_conv_w, v_ffn_conv_b, v_w_down, v_final_norm_w):
    ix, iy, ic = lax.axis_index("x"), lax.axis_index("y"), lax.axis_index("c")
    chip = 2 * ix + iy
    x2 = x[0]
    tgt = loss_target[0]
    s = x2.shape[0]

    big_shards = dict(w_in=w_in[0], w_attn_o=w_attn_o[0], w_ssd_o=w_ssd_o[0], w_out=w_out[0], w_up=w_up[0], w_down=w_down[0])
    packed = _pack_shards([big_shards[n].astype(BF16) for n in BIG])
    gathered = _all_gather_weights(packed)
    per_chip = [_unpack_shards(gathered[j]) for j in range(N_CHIPS)]
    full = {}
    for i, n in enumerate(BIG):
        pieces = [per_chip[j][i] for j in range(N_CHIPS)]
        if n in ("w_in", "w_up"):
            rows = D_MODEL
            full[n] = jnp.concatenate([p.reshape(rows, -1) for p in pieces], axis=1)
        else:
            full[n] = jnp.concatenate(pieces, axis=0)
    w_in_p = _permute_cols_in(full["w_in"])
    small_sh = _pack_small([ssd_conv_w[0], ffn_conv_w[0]])
    small_all = _all_gather_weights(small_sh)
    sc_parts = [_unpack_small(small_all[j], [(4, XBC_DIM // 4), (3, 2 * D_FF // 4)]) for j in range(N_CHIPS)]
    ssd_cw = jnp.concatenate([p[0] for p in sc_parts], axis=1)
    ffn_cw = jnp.concatenate([p[1] for p in sc_parts], axis=1)

    sinks128 = _vec128(attn_sinks)
    dtb128, alog128, dskip128 = _vec128(dt_bias), _vec128(a_log), _vec128(d_skip)

    xn = _rms_fwd(x2, norm1_w, name="norm1_fwd")
    proj = _mm(xn, w_in_p, name="proj_fwd", tn=1280)
    attn_pre = _attn_fwd(proj, sinks128)
    attn = _mm(attn_pre, full["w_attn_o"], name="attn_o_fwd")
    xbc = _ssd_conv_fwd(proj, ssd_cw, ssd_conv_b)
    y_ssd, hprev = _ssd_fwd(xbc, proj, dtb128, alog128, dskip128)
    yn = _gate_norm_fwd(y_ssd, proj, ssd_norm_w)
    ssd_out = _mm(yn, full["w_ssd_o"], name="ssd_o_fwd")
    merged = _merge_fwd(proj, b_gate, attn, ssd_out)
    h1 = _mm(merged, full["w_out"], name="out_fwd", resid=x2)
    hn = _rms_fwd(h1, norm2_w, name="norm2_fwd")
    up = _mm(hn, full["w_up"], name="up_fwd", tn=1408)
    act = _ffn_act_fwd(up, ffn_cw, ffn_conv_b)
    h2 = _mm(act, full["w_down"], name="down_fwd", resid=h1, tk=1408)

    dh2, loss_blk, g_final = _loss_bwd(h2, tgt, final_norm_w.reshape(1, -1))
    dact = _mm(dh2, full["w_down"], name="down_dx", tb=True, tn=1408)
    g_down = _mm(act, dh2, name="down_dw", ta=True, tm=1408)
    dup, g_ffn_cw, g_ffn_cb = _ffn_act_bwd(dact, up, ffn_cw, ffn_conv_b)
    dhn = _mm(dup, full["w_up"], name="up_dx", tb=True, tk=1408)
    g_up = _mm(hn, dup, name="up_dw", ta=True, tn=1408)
    dh1, g_norm2 = _rms_bwd(dhn, h1, norm2_w, dh2, name="norm2_bwd")
    dmerged = _mm(dh1, full["w_out"], name="out_dx", tb=True)
    g_out = _mm(merged, dh1, name="out_dw", ta=True)
    dattn, dssd_out, dga, dgs, g_ba, g_bs = _merge_bwd(dmerged, proj, b_gate, attn, ssd_out)
    dyn = _mm(dssd_out, full["w_ssd_o"], name="ssd_o_dx", tb=True)
    g_ssd_o = _mm(yn, dssd_out, name="ssd_o_dw", ta=True)
    dy_ssd, dz, g_ssd_norm = _gate_norm_bwd(dyn, y_ssd, proj, ssd_norm_w)
    dxbc, ddt, dvec = _ssd_bwd(xbc, proj, dtb128, alog128, dskip128, hprev, dy_ssd)
    dxbc_raw, g_ssd_cw, g_ssd_cb = _ssd_conv_bwd(dxbc, proj, ssd_cw, ssd_conv_b)
    dattn_pre = _mm(dattn, full["w_attn_o"], name="attn_o_dx", tb=True)
    g_attn_o = _mm(attn_pre, dattn, name="attn_o_dw", ta=True)
    dq, dk, dv, dsk = _attn_bwd(proj, sinks128, attn_pre, dattn_pre)
    dproj = jnp.concatenate([dq, dk, dv, dz, dxbc_raw, dga, dgs, ddt, jnp.zeros((s, PW - O_DT - 128), BF16)], axis=1)
    dxn = _mm(dproj, w_in_p, name="proj_dx", tb=True, tk=1280)
    g_in_p = _mm(xn, dproj, name="proj_dw", ta=True, tn=1280)
    dx, g_norm1 = _rms_bwd(dxn, x2, norm1_w, dh1, name="norm1_bwd")

    g_in = _unpermute_cols_in(g_in_p)
    full_g = dict(w_in=g_in, w_attn_o=g_attn_o, w_ssd_o=g_ssd_o, w_out=g_out, w_up=g_up, w_down=g_down)
    slots = []
    for j in range(N_CHIPS):
        parts = []
        for i, n in enumerate(BIG):
            g = full_g[n]
            if n in ("w_in", "w_up"):
                wdt = g.shape[1] // N_CHIPS
                parts.append(g[:, j * wdt:(j + 1) * wdt])
            else:
                rws = g.shape[0] // N_CHIPS
                parts.append(g[j * rws:(j + 1) * rws])
        slots.append(_pack_shards(parts))
    g_packed = jnp.stack(slots)
    hr = PACK_ROWS // 2
    g16 = _cast_bf16(g_packed, name="grad_cast")
    recv_pair = _pair_exchange(g16, hr)
    p32, p16 = _pair_add(g_packed, recv_pair, ic.reshape(1).astype(jnp.int32), hr)
    recv_chip = _chip_exchange(p16)
    fin_half = _chip_add(p32, recv_chip, chip.reshape(1).astype(jnp.int32))
    g_shard = _pair_gather(fin_half)
    g_big = _unpack_shards(g_shard)
    big_grads = {n: g_big[i].reshape(big_shards[n].shape) for i, n in enumerate(BIG)}

    small_g = dict(
        norm1_w=g_norm1, b_gate=jnp.concatenate([g_ba, g_bs], axis=1), attn_sinks=dsk[0:1, :16], ssd_conv_b=g_ssd_cb,
        dt_bias=dvec[0:1, :32], a_log=dvec[1:2, :32], d_skip=dvec[2:3, :32], ssd_norm_w=g_ssd_norm, norm2_w=g_norm2,
        ffn_conv_b=g_ffn_cb, final_norm_w=g_final, ssd_conv_w=g_ssd_cw, ffn_conv_w=g_ffn_cw)
    small_buf = _pack_small([small_g[n] for n in SMALL] + [loss_blk])
    small_sum = _all_reduce_small(small_buf)
    small_shapes = [(1, D_MODEL), (1, 2 * D_MODEL), (1, 16), (1, XBC_DIM), (1, 32), (1, 32), (1, 32), (1, D_INNER),
                    (1, D_MODEL), (1, 2 * D_FF), (D_MODEL,), (4, XBC_DIM), (3, 2 * D_FF), (1, 128)]
    small_list = _unpack_small(small_sum, small_shapes)
    loss = small_list[-1][0, 0]
    grads = dict(zip(SMALL, small_list[:-1]))
    grads["ssd_conv_w"] = lax.dynamic_slice_in_dim(grads["ssd_conv_w"], chip * (XBC_DIM // 4), XBC_DIM // 4, axis=1)
    grads["ffn_conv_w"] = lax.dynamic_slice_in_dim(grads["ffn_conv_w"], chip * (2 * D_FF // 4), 2 * D_FF // 4, axis=1)
    grads.update(big_grads)

    weights = dict(norm1_w=norm1_w, w_in=w_in, b_gate=b_gate, attn_sinks=attn_sinks, w_attn_o=w_attn_o, ssd_conv_w=ssd_conv_w,
                   ssd_conv_b=ssd_conv_b, dt_bias=dt_bias, a_log=a_log, d_skip=d_skip, ssd_norm_w=ssd_norm_w, w_ssd_o=w_ssd_o,
                   w_out=w_out, norm2_w=norm2_w, w_up=w_up, ffn_conv_w=ffn_conv_w, ffn_conv_b=ffn_conv_b, w_down=w_down,
                   final_norm_w=final_norm_w)
    ms = dict(norm1_w=m_norm1_w, w_in=m_w_in, b_gate=m_b_gate, attn_sinks=m_attn_sinks, w_attn_o=m_w_attn_o,
              ssd_conv_w=m_ssd_conv_w, ssd_conv_b=m_ssd_conv_b, dt_bias=m_dt_bias, a_log=m_a_log, d_skip=m_d_skip,
              ssd_norm_w=m_ssd_norm_w, w_ssd_o=m_w_ssd_o, w_out=m_w_out, norm2_w=m_norm2_w, w_up=m_w_up,
              ffn_conv_w=m_ffn_conv_w, ffn_conv_b=m_ffn_conv_b, w_down=m_w_down, final_norm_w=m_final_norm_w)
    vs = dict(norm1_w=v_norm1_w, w_in=v_w_in, b_gate=v_b_gate, attn_sinks=v_attn_sinks, w_attn_o=v_w_attn_o,
              ssd_conv_w=v_ssd_conv_w, ssd_conv_b=v_ssd_conv_b, dt_bias=v_dt_bias, a_log=v_a_log, d_skip=v_d_skip,
              ssd_norm_w=v_ssd_norm_w, w_ssd_o=v_w_ssd_o, w_out=v_w_out, norm2_w=v_norm2_w, w_up=v_w_up,
              ffn_conv_w=v_ffn_conv_w, ffn_conv_b=v_ffn_conv_b, w_down=v_w_down, final_norm_w=v_final_norm_w)
    order = list(weights)
    deltas, new_m, new_v = {}, {}, {}
    for n in BIG:
        shp = weights[n].shape
        d_, m_, v_ = _adamw(weights[n][0], grads[n], ms[n][0], vs[n][0], name="adamw_" + n)
        deltas[n], new_m[n], new_v[n] = d_.reshape(shp), m_.reshape(shp), v_.reshape(shp)
    smalls = [n for n in order if n not in BIG]
    sw = _pack_small([weights[n] for n in smalls])
    sg = _pack_small([grads[n] for n in smalls])
    sm = _pack_small([ms[n] for n in smalls])
    sv = _pack_small([vs[n] for n in smalls])
    sd_, sm_, sv_ = _adamw(sw, sg, sm, sv, name="adamw_small")
    shapes = [weights[n].shape for n in smalls]
    for n, d_, m_, v_ in zip(smalls, _unpack_small(sd_, shapes), _unpack_small(sm_, shapes), _unpack_small(sv_, shapes)):
        deltas[n], new_m[n], new_v[n] = d_, m_, v_
    out_grads = [grads[n].reshape(weights[n].shape) for n in order]
    return (loss, dx[None], *out_grads, *[deltas[n] for n in order], *[new_m[n] for n in order], *[new_v[n] for n in order])
```

```python
import functools

import jax
import jax.numpy as jnp
from jax import lax
from jax.experimental import pallas as pl
from jax.experimental.pallas import tpu as pltpu

F32 = jnp.float32
BF16 = jnp.bfloat16
HI = lax.Precision.HIGHEST

D_MODEL = 1024
Q_DIM = 1024
KV_DIM = 256
D_INNER = 2048
BC_DIM = 512
XBC_DIM = 3072
N_SSD_HEADS = 32
D_FF = 2816
IN_DIM = 8736
BLK = 128
EPS = 1e-5
NEG = -1e30

O_Q, O_K, O_V, O_Z, O_X, O_GA, O_GS, O_DT = 0, 1024, 1280, 1536, 3584, 6656, 7680, 8704
PW = 8960

ADAM_LR, ADAM_B1, ADAM_B2, ADAM_EPS, ADAM_WD, ADAM_STEP = 0.001, 0.9, 0.999, 1e-08, 0.01, 10

VMEM_LIMIT = 52 * 1024 * 1024
MESH = pl.DeviceIdType.MESH


def _cp(sem=None):
    return pltpu.CompilerParams(dimension_semantics=sem, vmem_limit_bytes=VMEM_LIMIT)


def _dot(a, b, prec=None):
    return jnp.dot(a, b, preferred_element_type=F32, precision=prec)


def _dot_nt(a, b, prec=None):
    return lax.dot_general(a, b, (((1,), (1,)), ((), ())), preferred_element_type=F32, precision=prec)


def _dot_tn(a, b, prec=None):
    return lax.dot_general(a, b, (((0,), (0,)), ((), ())), preferred_element_type=F32, precision=prec)


def _sigmoid(x):
    return 1.0 / (1.0 + jnp.exp(-x))


def _tile(n, want):
    t = min(n, want)
    while n % t:
        t -= 128
    return t


def _mm(a, b, *, name, ta=False, tb=False, out_dtype=F32, resid=None, tm=1024, tn=1024, tk=1024):
    m, k = (a.shape[1], a.shape[0]) if ta else a.shape
    slots = b.ndim == 3
    if slots:
        n = b.shape[1] if tb else b.shape[0] * b.shape[2]
        tn, tk = (tn, b.shape[2]) if tb else (b.shape[2], tk)
    else:
        n = b.shape[0] if tb else b.shape[1]
    tm, tn, tk = _tile(m, tm), _tile(n, tn), _tile(k, tk)
    nk = k // tk
    dn = (((0 if ta else 1,), (1 if tb else 0,)), ((), ()))

    def body(*refs):
        if resid is None:
            a_ref, b_ref, o_ref, acc = refs
        else:
            a_ref, b_ref, r_ref, o_ref, acc = refs
        kk = pl.program_id(2)
        bv = b_ref[0] if slots else b_ref[...]
        part = lax.dot_general(a_ref[...].astype(BF16), bv.astype(BF16), dn, preferred_element_type=F32)

        @pl.when(kk == 0)
        def _():
            acc[...] = part

        @pl.when(kk > 0)
        def _():
            acc[...] += part

        @pl.when(kk == nk - 1)
        def _():
            r = acc[...]
            if resid is not None:
                r = r + r_ref[...]
            o_ref[...] = r.astype(out_dtype)

    a_spec = pl.BlockSpec((tk, tm), lambda i, j, q: (q, i)) if ta else pl.BlockSpec((tm, tk), lambda i, j, q: (i, q))
    if slots:
        b_spec = (pl.BlockSpec((1, tn, tk), lambda i, j, q: (q, j, 0)) if tb
                  else pl.BlockSpec((1, tk, tn), lambda i, j, q: (j, q, 0)))
    else:
        b_spec = pl.BlockSpec((tn, tk), lambda i, j, q: (j, q)) if tb else pl.BlockSpec((tk, tn), lambda i, j, q: (q, j))
    o_spec = pl.BlockSpec((tm, tn), lambda i, j, q: (i, j))
    ins, specs = [a, b], [a_spec, b_spec]
    if resid is not None:
        ins.append(resid)
        specs.append(o_spec)
    return pl.pallas_call(
        body, name=name, grid=(m // tm, n // tn, nk), in_specs=specs, out_specs=o_spec,
        out_shape=jax.ShapeDtypeStruct((m, n), out_dtype), scratch_shapes=[pltpu.VMEM((tm, tn), F32)],
        compiler_params=_cp(("parallel", "parallel", "arbitrary")),
    )(*ins)


def _rms_fwd(x, w, *, name, tm=512):
    s, d = x.shape
    tm = _tile(s, tm)

    def body(x_ref, w_ref, o_ref):
        xv = x_ref[...]
        r = lax.rsqrt(jnp.mean(xv * xv, axis=-1, keepdims=True) + EPS)
        o_ref[...] = ((xv * r) * w_ref[...]).astype(BF16)

    return pl.pallas_call(
        body, name=name, grid=(s // tm,),
        in_specs=[pl.BlockSpec((tm, d), lambda i: (i, 0)), pl.BlockSpec((1, d), lambda i: (0, 0))],
        out_specs=pl.BlockSpec((tm, d), lambda i: (i, 0)),
        out_shape=jax.ShapeDtypeStruct((s, d), BF16), compiler_params=_cp(("parallel",)),
    )(x, w)


def _rms_bwd(dy, x, w, resid, *, name, tm=512):
    s, d = x.shape
    tm = _tile(s, tm)

    def body(dy_ref, x_ref, w_ref, r_ref, dx_ref, dw_ref):
        i = pl.program_id(0)
        xv = x_ref[...]
        r = lax.rsqrt(jnp.mean(xv * xv, axis=-1, keepdims=True) + EPS)
        xh = xv * r
        dyv = dy_ref[...]
        g = dyv * w_ref[...]
        dx_ref[...] = r_ref[...] + r * (g - xh * jnp.mean(g * xh, axis=-1, keepdims=True))
        part = jnp.sum(dyv * xh, axis=0, keepdims=True)

        @pl.when(i == 0)
        def _():
            dw_ref[...] = part

        @pl.when(i > 0)
        def _():
            dw_ref[...] += part

    row = pl.BlockSpec((tm, d), lambda i: (i, 0))
    vec = pl.BlockSpec((1, d), lambda i: (0, 0))
    return pl.pallas_call(
        body, name=name, grid=(s // tm,), in_specs=[row, row, vec, row], out_specs=[row, vec],
        out_shape=[jax.ShapeDtypeStruct((s, d), F32), jax.ShapeDtypeStruct((1, d), F32)],
        compiler_params=_cp(("arbitrary",)),
    )(dy, x, w, resid)


def _loss_bwd(h2, tgt, wf, *, tm=512):
    s, d = h2.shape
    tm = _tile(s, tm)

    def body(h_ref, t_ref, w_ref, dh_ref, loss_ref, dw_ref):
        i = pl.program_id(0)
        hv = h_ref[...]
        r = lax.rsqrt(jnp.mean(hv * hv, axis=-1, keepdims=True) + EPS)
        xh = hv * r
        wv = w_ref[...]
        e = xh * wv - t_ref[...]
        lpart = 0.5 * jnp.sum(jnp.mean(e * e, axis=-1, keepdims=True), axis=0, keepdims=True)
        dout = e * (1.0 / d)
        g = dout * wv
        dh_ref[...] = r * (g - xh * jnp.mean(g * xh, axis=-1, keepdims=True))
        part = jnp.sum(dout * xh, axis=0, keepdims=True)
        lrow = jnp.broadcast_to(lpart, (1, 128))

        @pl.when(i == 0)
        def _():
            dw_ref[...] = part
            loss_ref[...] = lrow

        @pl.when(i > 0)
        def _():
            dw_ref[...] += part
            loss_ref[...] += lrow

    row = pl.BlockSpec((tm, d), lambda i: (i, 0))
    vec = pl.BlockSpec((1, d), lambda i: (0, 0))
    return pl.pallas_call(
        body, name="loss_bwd", grid=(s // tm,), in_specs=[row, row, vec],
        out_specs=[row, pl.BlockSpec((1, 128), lambda i: (0, 0)), vec],
        out_shape=[jax.ShapeDtypeStruct((s, d), F32), jax.ShapeDtypeStruct((1, 128), F32),
                   jax.ShapeDtypeStruct((1, d), F32)],
        compiler_params=_cp(("arbitrary",)),
    )(h2, tgt, wf)


def _attn_mask(n):
    qi = lax.broadcasted_iota(jnp.int32, (4 * BLK, 2 * BLK), 0) & (BLK - 1)
    si = lax.broadcasted_iota(jnp.int32, (4 * BLK, 2 * BLK), 1)
    dist = BLK + qi - si
    kpos = n * BLK - BLK + si
    return (dist >= 0) & (dist < BLK) & (kpos >= 0)


def _attn_probs(q_ref, kc_ref, kp_ref, sk_ref, kvh, valid):
    hs = slice(kvh * 64, (kvh + 1) * 64)
    kb = jnp.concatenate([kp_ref[:, hs], kc_ref[:, hs]], axis=0).astype(BF16)
    qs = jnp.concatenate([q_ref[:, (kvh * 4 + g) * 64:(kvh * 4 + g + 1) * 64] for g in range(4)], axis=0).astype(BF16)
    s = _dot_nt(qs, kb) * 0.125
    s = jnp.where(valid, s, NEG)
    sink = jnp.concatenate(
        [jnp.broadcast_to(sk_ref[0:1, kvh * 4 + g:kvh * 4 + g + 1], (BLK, 1)) for g in range(4)], axis=0)
    m = jnp.maximum(jnp.max(s, axis=1, keepdims=True), sink)
    p = jnp.where(valid, jnp.exp(s - m), 0.0)
    es = jnp.exp(sink - m)
    denom = jnp.sum(p, axis=1, keepdims=True) + es
    return qs, kb, p / denom, es / denom


def _attn_fwd(proj, sinks):
    s = proj.shape[0]
    nb = s // BLK

    def body(q_ref, kc_ref, kp_ref, vc_ref, vp_ref, sk_ref, o_ref):
        valid = _attn_mask(pl.program_id(0))
        for kvh in range(4):
            hs = slice(kvh * 64, (kvh + 1) * 64)
            _, _, probs, _ = _attn_probs(q_ref, kc_ref, kp_ref, sk_ref, kvh, valid)
            vb = jnp.concatenate([vp_ref[:, hs], vc_ref[:, hs]], axis=0).astype(BF16)
            o = _dot(probs.astype(BF16), vb)
            for g in range(4):
                h = kvh * 4 + g
                o_ref[:, h * 64:(h + 1) * 64] = o[g * BLK:(g + 1) * BLK].astype(BF16)

    prev = lambda n: jnp.maximum(n - 1, 0)
    return pl.pallas_call(
        body, name="attn_fwd", grid=(nb,),
        in_specs=[pl.BlockSpec((BLK, Q_DIM), lambda n: (n, 0)),
                  pl.BlockSpec((BLK, KV_DIM), lambda n: (n, O_K // KV_DIM)),
                  pl.BlockSpec((BLK, KV_DIM), lambda n: (prev(n), O_K // KV_DIM)),
                  pl.BlockSpec((BLK, KV_DIM), lambda n: (n, O_V // KV_DIM)),
                  pl.BlockSpec((BLK, KV_DIM), lambda n: (prev(n), O_V // KV_DIM)),
                  pl.BlockSpec((1, 128), lambda n: (0, 0))],
        out_specs=pl.BlockSpec((BLK, Q_DIM), lambda n: (n, 0)),
        out_shape=jax.ShapeDtypeStruct((s, Q_DIM), BF16), compiler_params=_cp(("parallel",)),
    )(proj, proj, proj, proj, proj, sinks)


def _attn_bwd(proj, sinks, o, do):
    s = proj.shape[0]
    nb = s // BLK

    def body(q_ref, kc_ref, kp_ref, vc_ref, vp_ref, sk_ref, o_ref, do_ref,
             dq_ref, dk_ref, dv_ref, dsk_ref, ck, cv, nkp, nkc, nvp, nvc):
        n = pl.program_id(0)

        @pl.when(n == 0)
        def _():
            ck[...] = jnp.zeros_like(ck)
            cv[...] = jnp.zeros_like(cv)
            dsk_ref[...] = jnp.zeros_like(dsk_ref)

        @pl.when(n < nb)
        def _():
            valid = _attn_mask(n)
            lane = lax.broadcasted_iota(jnp.int32, (1, 128), 1)
            dsk = jnp.zeros((1, 128), F32)
            for kvh in range(4):
                hs = slice(kvh * 64, (kvh + 1) * 64)
                qs, kb, probs, psink = _attn_probs(q_ref, kc_ref, kp_ref, sk_ref, kvh, valid)
                vb = jnp.concatenate([vp_ref[:, hs], vc_ref[:, hs]], axis=0).astype(BF16)
                heads = [slice((kvh * 4 + g) * 64, (kvh * 4 + g + 1) * 64) for g in range(4)]
                dos = jnp.concatenate([do_ref[:, hh] for hh in heads], axis=0)
                os_ = jnp.concatenate([o_ref[:, hh] for hh in heads], axis=0).astype(F32)
                delta = jnp.sum(dos * os_, axis=1, keepdims=True)
                dos16 = dos.astype(BF16)
                dp = _dot_nt(dos16, vb)
                ds = (probs * (dp - delta) * 0.125).astype(BF16)
                dqs = _dot(ds, kb)
                dkb = _dot_tn(ds, qs)
                dvb = _dot_tn(probs.astype(BF16), dos16)
                nkp[:, hs] = dkb[:BLK]
                nkc[:, hs] = dkb[BLK:]
                nvp[:, hs] = dvb[:BLK]
                nvc[:, hs] = dvb[BLK:]
                sd = psink * delta
                for g in range(4):
                    dq_ref[:, heads[g]] = dqs[g * BLK:(g + 1) * BLK].astype(BF16)
                    val = -jnp.sum(sd[g * BLK:(g + 1) * BLK], axis=0, keepdims=True)
                    dsk = dsk + jnp.where(lane == kvh * 4 + g, val, 0.0)
            dsk_ref[0:1, :] += dsk
            dk_ref[...] = (ck[...] + nkp[...]).astype(BF16)
            dv_ref[...] = (cv[...] + nvp[...]).astype(BF16)
            ck[...] = nkc[...]
            cv[...] = nvc[...]

        @pl.when(n == nb)
        def _():
            dk_ref[...] = ck[...].astype(BF16)
            dv_ref[...] = cv[...].astype(BF16)

    cur = lambda n: jnp.minimum(n, nb - 1)
    prev = lambda n: jnp.maximum(jnp.minimum(n, nb - 1) - 1, 0)
    outb = lambda n: jnp.maximum(n - 1, 0)
    kv_scr = pltpu.VMEM((BLK, KV_DIM), F32)
    return pl.pallas_call(
        body, name="attn_bwd", grid=(nb + 1,),
        in_specs=[pl.BlockSpec((BLK, Q_DIM), lambda n: (cur(n), 0)),
                  pl.BlockSpec((BLK, KV_DIM), lambda n: (cur(n), O_K // KV_DIM)),
                  pl.BlockSpec((BLK, KV_DIM), lambda n: (prev(n), O_K // KV_DIM)),
                  pl.BlockSpec((BLK, KV_DIM), lambda n: (cur(n), O_V // KV_DIM)),
                  pl.BlockSpec((BLK, KV_DIM), lambda n: (prev(n), O_V // KV_DIM)),
                  pl.BlockSpec((1, 128), lambda n: (0, 0)),
                  pl.BlockSpec((BLK, Q_DIM), lambda n: (cur(n), 0)),
                  pl.BlockSpec((BLK, Q_DIM), lambda n: (cur(n), 0))],
        out_specs=[pl.BlockSpec((BLK, Q_DIM), lambda n: (cur(n), 0)),
                   pl.BlockSpec((BLK, KV_DIM), lambda n: (outb(n), 0)),
                   pl.BlockSpec((BLK, KV_DIM), lambda n: (outb(n), 0)),
                   pl.BlockSpec((8, 128), lambda n: (0, 0))],
        out_shape=[jax.ShapeDtypeStruct((s, Q_DIM), BF16), jax.ShapeDtypeStruct((s, KV_DIM), BF16),
                   jax.ShapeDtypeStruct((s, KV_DIM), BF16), jax.ShapeDtypeStruct((8, 128), F32)],
        scratch_shapes=[kv_scr] * 6, compiler_params=_cp(("arbitrary",)),
    )(proj, proj, proj, proj, proj, sinks, o, do)


def _shift_down(x, j):
    if j == 0:
        return x
    row = lax.broadcasted_iota(jnp.int32, x.shape, 0)
    return jnp.where(row >= j, pltpu.roll(x, j, 0), 0.0)


def _shift_up(x, j):
    if j == 0:
        return x
    s = x.shape[0]
    row = lax.broadcasted_iota(jnp.int32, x.shape, 0)
    return jnp.where(row < s - j, pltpu.roll(x, s - j, 0), 0.0)


def _conv(x, w_ref, b_ref):
    kk = w_ref.shape[0]
    y = _shift_down(x, kk - 1) * w_ref[0:1, :]
    for q in range(1, kk):
        y = y + _shift_down(x, kk - 1 - q) * w_ref[q:q + 1, :]
    return y + b_ref[...]


def _conv_bwd(dy, x, w_ref, dx_dtype):
    kk = w_ref.shape[0]
    dx = _shift_up(dy, kk - 1) * w_ref[0:1, :]
    dws = [jnp.sum(dy * _shift_down(x, kk - 1), axis=0, keepdims=True)]
    for q in range(1, kk):
        dx = dx + _shift_up(dy, kk - 1 - q) * w_ref[q:q + 1, :]
        dws.append(jnp.sum(dy * _shift_down(x, kk - 1 - q), axis=0, keepdims=True))
    return dx.astype(dx_dtype), dws, jnp.sum(dy, axis=0, keepdims=True)


def _dsilu(y, sg):
    return sg * (1.0 + y * (1.0 - sg))


CT = 256


def _ssd_conv_fwd(proj, w, b):
    s = proj.shape[0]

    def body(x_ref, w_ref, b_ref, o_ref):
        y = _conv(x_ref[...], w_ref, b_ref)
        o_ref[...] = y * _sigmoid(y)

    return pl.pallas_call(
        body, name="ssd_conv_fwd", grid=(XBC_DIM // CT,),
        in_specs=[pl.BlockSpec((s, CT), lambda i: (0, O_X // CT + i)), pl.BlockSpec((4, CT), lambda i: (0, i)),
                  pl.BlockSpec((1, CT), lambda i: (0, i))],
        out_specs=pl.BlockSpec((s, CT), lambda i: (0, i)),
        out_shape=jax.ShapeDtypeStruct((s, XBC_DIM), F32), compiler_params=_cp(("parallel",)),
    )(proj, w, b)


def _ssd_conv_bwd(dact, proj, w, b):
    s = proj.shape[0]

    def body(d_ref, x_ref, w_ref, b_ref, dx_ref, dw_ref, db_ref):
        x = x_ref[...]
        y = _conv(x, w_ref, b_ref)
        dy = d_ref[...] * _dsilu(y, _sigmoid(y))
        dx, dws, db = _conv_bwd(dy, x, w_ref, BF16)
        dx_ref[...] = dx
        for q in range(4):
            dw_ref[q:q + 1, :] = dws[q]
        db_ref[...] = db

    return pl.pallas_call(
        body, name="ssd_conv_bwd", grid=(XBC_DIM // CT,),
        in_specs=[pl.BlockSpec((s, CT), lambda i: (0, i)), pl.BlockSpec((s, CT), lambda i: (0, O_X // CT + i)),
                  pl.BlockSpec((4, CT), lambda i: (0, i)), pl.BlockSpec((1, CT), lambda i: (0, i))],
        out_specs=[pl.BlockSpec((s, CT), lambda i: (0, i)), pl.BlockSpec((4, CT), lambda i: (0, i)),
                   pl.BlockSpec((1, CT), lambda i: (0, i))],
        out_shape=[jax.ShapeDtypeStruct((s, XBC_DIM), BF16), jax.ShapeDtypeStruct((4, XBC_DIM), F32),
                   jax.ShapeDtypeStruct((1, XBC_DIM), F32)],
        compiler_params=_cp(("parallel",)),
    )(dact, proj, w, b)


NFT = D_FF // CT


def _ffn_act_fwd(up, w, b):
    s = up.shape[0]

    def body(v_ref, g_ref, wv_ref, wg_ref, bv_ref, bg_ref, o_ref):
        val = _conv(v_ref[...], wv_ref, bv_ref)
        gt = _conv(g_ref[...], wg_ref, bg_ref)
        o_ref[...] = ((gt * _sigmoid(gt)) * val).astype(BF16)

    col = lambda off: (lambda i: (0, off + i))
    return pl.pallas_call(
        body, name="ffn_act_fwd", grid=(NFT,),
        in_specs=[pl.BlockSpec((s, CT), col(0)), pl.BlockSpec((s, CT), col(NFT)),
                  pl.BlockSpec((3, CT), col(0)), pl.BlockSpec((3, CT), col(NFT)),
                  pl.BlockSpec((1, CT), col(0)), pl.BlockSpec((1, CT), col(NFT))],
        out_specs=pl.BlockSpec((s, CT), col(0)),
        out_shape=jax.ShapeDtypeStruct((s, D_FF), BF16), compiler_params=_cp(("parallel",)),
    )(up, up, w, w, b, b)


def _ffn_act_bwd(dact, up, w, b):
    s = up.shape[0]

    def body(d_ref, v_ref, g_ref, wv_ref, wg_ref, bv_ref, bg_ref, dx_ref, dw_ref, db_ref):
        i = pl.program_id(0)
        xv, xg = v_ref[...], g_ref[...]
        val = _conv(xv, wv_ref, bv_ref)
        gt = _conv(xg, wg_ref, bg_ref)
        sg = _sigmoid(gt)
        d = d_ref[...]

        def emit(dy, x, w_ref):
            dx, dws, db = _conv_bwd(dy, x, w_ref, BF16)
            dx_ref[...] = dx
            for q in range(3):
                dw_ref[q:q + 1, :] = dws[q]
            db_ref[...] = db

        @pl.when(i < NFT)
        def _():
            emit(d * (gt * sg), xv, wv_ref)

        @pl.when(i >= NFT)
        def _():
            emit(d * val * _dsilu(gt, sg), xg, wg_ref)

    half = lambda off: (lambda i: (0, off + i % NFT))
    own = lambda i: (0, i)
    return pl.pallas_call(
        body, name="ffn_act_bwd", grid=(2 * NFT,),
        in_specs=[pl.BlockSpec((s, CT), half(0)), pl.BlockSpec((s, CT), half(0)), pl.BlockSpec((s, CT), half(NFT)),
                  pl.BlockSpec((3, CT), half(0)), pl.BlockSpec((3, CT), half(NFT)),
                  pl.BlockSpec((1, CT), half(0)), pl.BlockSpec((1, CT), half(NFT))],
        out_specs=[pl.BlockSpec((s, CT), own), pl.BlockSpec((3, CT), own), pl.BlockSpec((1, CT), own)],
        out_shape=[jax.ShapeDtypeStruct((s, 2 * D_FF), BF16), jax.ShapeDtypeStruct((3, 2 * D_FF), F32),
                   jax.ShapeDtypeStruct((1, 2 * D_FF), F32)],
        compiler_params=_cp(("parallel",)),
    )(dact, up, up, w, w, b, b)


def _expand_mat():
    r = lax.broadcasted_iota(jnp.int32, (128, D_INNER), 0)
    c = lax.broadcasted_iota(jnp.int32, (128, D_INNER), 1)
    return ((c >> 6) == r).astype(F32)


def _reduce_mat():
    r = lax.broadcasted_iota(jnp.int32, (D_INNER, 128), 0)
    c = lax.broadcasted_iota(jnp.int32, (D_INNER, 128), 1)
    return ((r >> 6) == c).astype(F32)


def _tril():
    r = lax.broadcasted_iota(jnp.int32, (BLK, BLK), 0)
    c = lax.broadcasted_iota(jnp.int32, (BLK, BLK), 1)
    return r >= c


def _softplus(x):
    return jnp.maximum(x, 0.0) + jnp.log(1.0 + jnp.exp(-jnp.abs(x)))


def _ssd_common(dtraw_ref, dtb_ref, alog_ref):
    causal = _tril()
    tril = causal.astype(F32)
    e_mat = _expand_mat()
    a_neg = -jnp.exp(alog_ref[...])
    dt = _softplus(dtraw_ref[...] + dtb_ref[...])
    a = dt * a_neg
    a_cs = _dot(tril, a, HI)
    a_cs_t = a_cs.T
    a_last = a_cs[BLK - 1:BLK, :]
    dt_x = _dot(dt, e_mat, HI)
    ea_x = jnp.exp(_dot(a_cs, e_mat, HI))
    ds_x = jnp.exp(_dot(a_last - a_cs, e_mat, HI))
    ealast_col = jnp.exp(a_cs_t[:, BLK - 1:BLK])
    st_scale = _dot(_reduce_mat(), jnp.broadcast_to(ealast_col, (128, 128)), HI)
    return causal, e_mat, a_neg, dt, a_cs, a_cs_t, dt_x, ea_x, ds_x, st_scale


def _decay(a_cs, a_cs_t, h, causal):
    seg = a_cs[:, h:h + 1] - a_cs_t[h:h + 1, :]
    return jnp.where(causal, jnp.exp(jnp.where(causal, seg, 0.0)), 0.0)


def _ssd_fwd(xbc, proj, dt_bias, a_log, d_skip):
    s = xbc.shape[0]
    nc = s // BLK

    def body(xs_ref, b_ref, c_ref, dtraw_ref, dtb_ref, alog_ref, dskip_ref, y_ref, hp_ref, h_scr, xc16):
        @pl.when(pl.program_id(0) == 0)
        def _():
            h_scr[...] = jnp.zeros_like(h_scr)

        causal, e_mat, _, _, a_cs, a_cs_t, dt_x, ea_x, ds_x, st_scale = _ssd_common(dtraw_ref, dtb_ref, alog_ref)
        dskip_x = _dot(jnp.broadcast_to(dskip_ref[...], (8, 128)), e_mat, HI)[0:1]
        xs = xs_ref[...]
        xc = xs * dt_x
        xc16[...] = xc.astype(BF16)
        xcd = (xc * ds_x).astype(BF16)
        hp_ref[0] = h_scr[...]
        for g in range(4):
            gs = slice(g * 512, (g + 1) * 512)
            cg = c_ref[:, g * 128:(g + 1) * 128].astype(BF16)
            bg = b_ref[:, g * 128:(g + 1) * 128].astype(BF16)
            cb = _dot_nt(cg, bg)
            hg = h_scr[gs, :]
            yoff = _dot_nt(cg, hg.astype(BF16)) * ea_x[:, gs]
            for j in range(8):
                h = g * 8 + j
                hsl = slice(h * 64, (h + 1) * 64)
                mm = (cb * _decay(a_cs, a_cs_t, h, causal)).astype(BF16)
                y_ref[:, hsl] = _dot(mm, xc16[:, hsl])
            y_ref[:, gs] += yoff + xs[:, gs] * dskip_x[:, gs]
            h_scr[gs, :] = hg * st_scale[gs, :] + _dot_tn(xcd[:, gs], bg)

    vec = pl.BlockSpec((1, 128), lambda c: (0, 0))
    return pl.pallas_call(
        body, name="ssd_fwd", grid=(nc,),
        in_specs=[pl.BlockSpec((BLK, D_INNER), lambda c: (c, 0)),
                  pl.BlockSpec((BLK, BC_DIM), lambda c: (c, D_INNER // BC_DIM)),
                  pl.BlockSpec((BLK, BC_DIM), lambda c: (c, D_INNER // BC_DIM + 1)),
                  pl.BlockSpec((BLK, 128), lambda c: (c, O_DT // 128)), vec, vec, vec],
        out_specs=[pl.BlockSpec((BLK, D_INNER), lambda c: (c, 0)),
                   pl.BlockSpec((1, D_INNER, 128), lambda c: (c, 0, 0))],
        out_shape=[jax.ShapeDtypeStruct((s, D_INNER), F32), jax.ShapeDtypeStruct((nc, D_INNER, 128), F32)],
        scratch_shapes=[pltpu.VMEM((D_INNER, 128), F32), pltpu.VMEM((BLK, D_INNER), BF16)],
        compiler_params=_cp(("arbitrary",)),
    )(xbc, xbc, xbc, proj, dt_bias, a_log, d_skip)


def _ssd_bwd(xbc, proj, dt_bias, a_log, d_skip, hprev, dy):
    s = xbc.shape[0]
    nc = s // BLK

    def body(xs_ref, b_ref, c_ref, dtraw_ref, dtb_ref, alog_ref, dskip_ref, hp_ref, dy_ref,
             dxbc_ref, ddt_ref, dvec_ref, dh_scr, xc16, dy16, dxc_scr, dacs_r):
        step = pl.program_id(0)
        dacs_r[...] = jnp.zeros_like(dacs_r)

        @pl.when(step == 0)
        def _():
            dh_scr[...] = jnp.zeros_like(dh_scr)
            dvec_ref[...] = jnp.zeros_like(dvec_ref)

        causal, e_mat, a_neg, dt, a_cs, a_cs_t, dt_x, ea_x, ds_x, st_scale = _ssd_common(dtraw_ref, dtb_ref, alog_ref)
        r_mat = _reduce_mat()
        lane = lax.broadcasted_iota(jnp.int32, (1, 128), 1)
        dskip_x = _dot(jnp.broadcast_to(dskip_ref[...], (8, 128)), e_mat, HI)[0:1]
        xs = xs_ref[...]
        dy = dy_ref[...]
        xc = xs * dt_x
        xcd = xc * ds_x
        xc16[...] = xc.astype(BF16)
        dy16[...] = dy.astype(BF16)
        dyea = dy * ea_x
        dh = dh_scr[...]
        hp = hp_ref[0]
        dalast_rec = jnp.sum(_dot(e_mat, dh * hp * st_scale, HI), axis=1, keepdims=True)
        dacs = jnp.zeros((BLK, 128), F32)
        t0 = jnp.zeros((BLK, 128), F32)
        t1 = jnp.zeros((BLK, 128), F32)
        for g in range(4):
            gs = slice(g * 512, (g + 1) * 512)
            bsl = slice(g * 128, (g + 1) * 128)
            cg = c_ref[:, bsl].astype(BF16)
            bg = b_ref[:, bsl].astype(BF16)
            cb = _dot_nt(cg, bg)
            hg16 = hp[gs, :].astype(BF16)
            dhg16 = dh[gs, :].astype(BF16)
            raw = _dot_nt(cg, hg16)
            draw16 = dyea[:, gs].astype(BF16)
            dcg = _dot(draw16, hg16)
            dhp_g = _dot_tn(draw16, cg)
            dbg = _dot(xcd[:, gs].astype(BF16), dhg16)
            dxcd = _dot_nt(bg, dhg16)
            dcb = jnp.zeros((BLK, BLK), F32)
            for j in range(8):
                h = g * 8 + j
                hsl = slice(h * 64, (h + 1) * 64)
                decay = _decay(a_cs, a_cs_t, h, causal)
                m = cb * decay
                dm = _dot_nt(dy16[:, hsl], xc16[:, hsl])
                dxc_scr[:, hsl] = _dot_tn(m.astype(BF16), dy16[:, hsl])
                dcb = dcb + dm * decay
                dseg = dm * m
                oneh = jnp.where(lane == h, 1.0, 0.0)
                dacs = dacs + jnp.sum(dseg, axis=1, keepdims=True) * oneh
                dacs_r[h:h + 1, :] = jnp.sum(dseg, axis=0, keepdims=True)
            dcb16 = dcb.astype(BF16)
            dcg = dcg + _dot(dcb16, bg)
            dbg = dbg + _dot_tn(dcb16, cg)
            dxbc_ref[:, D_INNER + g * 128:D_INNER + (g + 1) * 128] = dbg
            dxbc_ref[:, D_INNER + BC_DIM + g * 128:D_INNER + BC_DIM + (g + 1) * 128] = dcg
            dxc_scr[:, gs] += dxcd * ds_x[:, gs]
            dh_scr[gs, :] = dh[gs, :] * st_scale[gs, :] + dhp_g
            t0 = t0 + _dot(dy[:, gs] * (raw * ea_x[:, gs]), r_mat[gs, :], HI)
            t1 = t1 + _dot(dxcd * xcd[:, gs], r_mat[gs, :], HI)
        dxc = dxc_scr[...]
        dacs = dacs + t0 - t1 - dacs_r[...].T
        dalast_row = jnp.broadcast_to(dalast_rec, (128, 128)).T[0:1] + jnp.sum(t1, axis=0, keepdims=True)
        row = lax.broadcasted_iota(jnp.int32, (BLK, 128), 0)
        dacs = dacs + jnp.where(row == BLK - 1, dalast_row, 0.0)
        da = _dot_tn(causal.astype(F32), dacs, HI)
        ddt = da * a_neg + _dot(dxc * xs, r_mat, HI)
        lmask = lax.broadcasted_iota(jnp.int32, (BLK, 128), 1) < N_SSD_HEADS
        ddtraw = jnp.where(lmask, ddt * _sigmoid(dtraw_ref[...] + dtb_ref[...]), 0.0)
        ddt_ref[...] = ddtraw.astype(BF16)
        dxbc_ref[:, 0:D_INNER] = dy * dskip_x + dxc * dt_x
        dvec_ref[0:1, :] += jnp.sum(ddtraw, axis=0, keepdims=True)
        dvec_ref[1:2, :] += jnp.where(lane < N_SSD_HEADS, jnp.sum(da * dt, axis=0, keepdims=True) * a_neg, 0.0)
        dvec_ref[2:3, :] += _dot(jnp.broadcast_to(jnp.sum(dy * xs, axis=0, keepdims=True), (8, D_INNER)), r_mat, HI)[0:1]

    rev = lambda c: nc - 1 - c
    vec = pl.BlockSpec((1, 128), lambda c: (0, 0))
    return pl.pallas_call(
        body, name="ssd_bwd", grid=(nc,),
        in_specs=[pl.BlockSpec((BLK, D_INNER), lambda c: (rev(c), 0)),
                  pl.BlockSpec((BLK, BC_DIM), lambda c: (rev(c), D_INNER // BC_DIM)),
                  pl.BlockSpec((BLK, BC_DIM), lambda c: (rev(c), D_INNER // BC_DIM + 1)),
                  pl.BlockSpec((BLK, 128), lambda c: (rev(c), O_DT // 128)), vec, vec, vec,
                  pl.BlockSpec((1, D_INNER, 128), lambda c: (rev(c), 0, 0)),
                  pl.BlockSpec((BLK, D_INNER), lambda c: (rev(c), 0))],
        out_specs=[pl.BlockSpec((BLK, XBC_DIM), lambda c: (rev(c), 0)),
                   pl.BlockSpec((BLK, 128), lambda c: (rev(c), 0)),
                   pl.BlockSpec((8, 128), lambda c: (0, 0))],
        out_shape=[jax.ShapeDtypeStruct((s, XBC_DIM), F32), jax.ShapeDtypeStruct((s, 128), BF16),
                   jax.ShapeDtypeStruct((8, 128), F32)],
        scratch_shapes=[pltpu.VMEM((D_INNER, 128), F32), pltpu.VMEM((BLK, D_INNER), BF16),
                        pltpu.VMEM((BLK, D_INNER), BF16), pltpu.VMEM((BLK, D_INNER), F32),
                        pltpu.VMEM((128, BLK), F32)],
        compiler_params=_cp(("arbitrary",)),
    )(xbc, xbc, xbc, proj, dt_bias, a_log, d_skip, hprev, dy)


GW = 512


def _gate_norm_fwd(y, proj, wn, *, tm=512):
    s = y.shape[0]
    tm = _tile(s, tm)

    def body(y_ref, z_ref, w_ref, o_ref):
        z = z_ref[...]
        y2 = y_ref[...] * (z * _sigmoid(z))
        r = lax.rsqrt(jnp.mean(y2 * y2, axis=-1, keepdims=True) + EPS)
        o_ref[...] = ((y2 * r) * w_ref[...]).astype(BF16)

    return pl.pallas_call(
        body, name="gate_norm_fwd", grid=(s // tm, 4),
        in_specs=[pl.BlockSpec((tm, GW), lambda i, g: (i, g)), pl.BlockSpec((tm, GW), lambda i, g: (i, O_Z // GW + g)),
                  pl.BlockSpec((1, GW), lambda i, g: (0, g))],
        out_specs=pl.BlockSpec((tm, GW), lambda i, g: (i, g)),
        out_shape=jax.ShapeDtypeStruct((s, D_INNER), BF16), compiler_params=_cp(("parallel", "parallel")),
    )(y, proj, wn)


def _gate_norm_bwd(dyn, y, proj, wn, *, tm=512):
    s = y.shape[0]
    tm = _tile(s, tm)

    def body(d_ref, y_ref, z_ref, w_ref, dy_ref, dz_ref, dw_ref):
        i = pl.program_id(1)
        z = z_ref[...]
        sg = _sigmoid(z)
        sz = z * sg
        yv = y_ref[...]
        y2 = yv * sz
        r = lax.rsqrt(jnp.mean(y2 * y2, axis=-1, keepdims=True) + EPS)
        xh = y2 * r
        dv = d_ref[...]
        g = dv * w_ref[...]
        dy2 = r * (g - xh * jnp.mean(g * xh, axis=-1, keepdims=True))
        dy_ref[...] = dy2 * sz
        dz_ref[...] = (dy2 * yv * _dsilu(z, sg)).astype(BF16)
        part = jnp.sum(dv * xh, axis=0, keepdims=True)

        @pl.when(i == 0)
        def _():
            dw_ref[...] = part

        @pl.when(i > 0)
        def _():
            dw_ref[...] += part

    blk = pl.BlockSpec((tm, GW), lambda g, i: (i, g))
    vec = pl.BlockSpec((1, GW), lambda g, i: (0, g))
    return pl.pallas_call(
        body, name="gate_norm_bwd", grid=(4, s // tm),
        in_specs=[blk, blk, pl.BlockSpec((tm, GW), lambda g, i: (i, O_Z // GW + g)), vec],
        out_specs=[blk, blk, vec],
        out_shape=[jax.ShapeDtypeStruct((s, D_INNER), F32), jax.ShapeDtypeStruct((s, D_INNER), BF16),
                   jax.ShapeDtypeStruct((1, D_INNER), F32)],
        compiler_params=_cp(("parallel", "arbitrary")),
    )(dyn, y, proj, wn)


def _merge_fwd(proj, b_gate, attn, ssd_out, *, tm=512):
    s = attn.shape[0]
    tm = _tile(s, tm)

    def body(ga_ref, gs_ref, ba_ref, bs_ref, a_ref, s_ref, o_ref):
        ga = _sigmoid(ga_ref[...] + ba_ref[...])
        gs = _sigmoid(gs_ref[...] + bs_ref[...])
        o_ref[...] = (ga * a_ref[...] + gs * s_ref[...]).astype(BF16)

    blk = pl.BlockSpec((tm, GW), lambda i, j: (i, j))
    return pl.pallas_call(
        body, name="merge_fwd", grid=(s // tm, 2),
        in_specs=[pl.BlockSpec((tm, GW), lambda i, j: (i, O_GA // GW + j)),
                  pl.BlockSpec((tm, GW), lambda i, j: (i, O_GS // GW + j)),
                  pl.BlockSpec((1, GW), lambda i, j: (0, j)), pl.BlockSpec((1, GW), lambda i, j: (0, 2 + j)), blk, blk],
        out_specs=blk, out_shape=jax.ShapeDtypeStruct((s, D_MODEL), BF16),
        compiler_params=_cp(("parallel", "parallel")),
    )(proj, proj, b_gate, b_gate, attn, ssd_out)


def _merge_bwd(dm, proj, b_gate, attn, ssd_out, *, tm=512):
    s = attn.shape[0]
    tm = _tile(s, tm)

    def body(d_ref, ga_ref, gs_ref, ba_ref, bs_ref, a_ref, s_ref, da_ref, ds_ref, dga_ref, dgs_ref, dba_ref, dbs_ref):
        i = pl.program_id(1)
        ga = _sigmoid(ga_ref[...] + ba_ref[...])
        gs = _sigmoid(gs_ref[...] + bs_ref[...])
        d = d_ref[...]
        da_ref[...] = (d * ga).astype(BF16)
        ds_ref[...] = (d * gs).astype(BF16)
        dga = d * a_ref[...] * (ga * (1.0 - ga))
        dgs = d * s_ref[...] * (gs * (1.0 - gs))
        dga_ref[...] = dga.astype(BF16)
        dgs_ref[...] = dgs.astype(BF16)
        pa = jnp.sum(dga, axis=0, keepdims=True)
        ps = jnp.sum(dgs, axis=0, keepdims=True)

        @pl.when(i == 0)
        def _():
            dba_ref[...] = pa
            dbs_ref[...] = ps

        @pl.when(i > 0)
        def _():
            dba_ref[...] += pa
            dbs_ref[...] += ps

    blk = pl.BlockSpec((tm, GW), lambda j, i: (i, j))
    vec = pl.BlockSpec((1, GW), lambda j, i: (0, j))
    sd = jax.ShapeDtypeStruct((s, D_MODEL), BF16)
    vd = jax.ShapeDtypeStruct((1, D_MODEL), F32)
    return pl.pallas_call(
        body, name="merge_bwd", grid=(2, s // tm),
        in_specs=[blk, pl.BlockSpec((tm, GW), lambda j, i: (i, O_GA // GW + j)),
                  pl.BlockSpec((tm, GW), lambda j, i: (i, O_GS // GW + j)),
                  vec, pl.BlockSpec((1, GW), lambda j, i: (0, 2 + j)), blk, blk],
        out_specs=[blk, blk, blk, blk, vec, vec], out_shape=[sd, sd, sd, sd, vd, vd],
        compiler_params=_cp(("parallel", "arbitrary")),
    )(dm, proj, proj, b_gate, b_gate, attn, ssd_out)


def _adamw(w, g, m, v, *, name, tm=128):
    r, c = w.shape
    tm = r if (r < tm or r % tm) else tm

    def body(w_ref, g_ref, m_ref, v_ref, d_ref, nm_ref, nv_ref):
        gv = g_ref[...]
        mn = ADAM_B1 * m_ref[...] + (1.0 - ADAM_B1) * gv
        vn = ADAM_B2 * v_ref[...] + (1.0 - ADAM_B2) * (gv * gv)
        m_hat = mn / (1.0 - ADAM_B1 ** ADAM_STEP)
        v_hat = vn / (1.0 - ADAM_B2 ** ADAM_STEP)
        d_ref[...] = -ADAM_LR * (m_hat / (jnp.sqrt(v_hat) + ADAM_EPS) + ADAM_WD * w_ref[...])
        nm_ref[...] = mn
        nv_ref[...] = vn

    blk = pl.BlockSpec((tm, c), lambda i: (i, 0))
    sd = jax.ShapeDtypeStruct((r, c), F32)
    return pl.pallas_call(
        body, name=name, grid=(r // tm,), in_specs=[blk] * 4, out_specs=[blk] * 3, out_shape=[sd] * 3,
        compiler_params=_cp(("parallel",)),
    )(w, g, m, v)


ANY = pl.BlockSpec(memory_space=pl.ANY)
N_CHIPS = 4


def _chip_of(k, x, y):
    return (x ^ (k >> 1), y ^ (k & 1))


def _all_gather_small(shard):
    r, c = shard.shape
    hr = r // 2

    def body(sh_ref, out_ref, send_sems, recv_sems, local_sem):
        x, y, cc = lax.axis_index("x"), lax.axis_index("y"), lax.axis_index("c")

        def half(px, py, pc):
            return out_ref.at[2 * px + py, pl.ds(pc * hr, hr), :]

        def copy(k, px, py, pc, to, src=None):
            return pltpu.make_async_remote_copy(
                src_ref=half(px, py, pc) if src is None else src, dst_ref=half(px, py, pc),
                send_sem=send_sems.at[k], recv_sem=recv_sems.at[k], device_id=to, device_id_type=MESH)

        mine = pltpu.make_async_copy(sh_ref, out_ref.at[2 * x + y], local_sem)
        mine.start()
        chips = [_chip_of(k, x, y) for k in (1, 2, 3)]
        first = [copy(j, x, y, cc, (*chip, cc), src=sh_ref.at[pl.ds(cc * hr, hr), :]) for j, chip in enumerate(chips)]
        for cp in first:
            cp.start()
        passed = [copy(3 + j, *chip, cc, (x, y, 1 - cc)) for j, chip in enumerate(chips)]
        for j, chip in enumerate(chips):
            copy(j, *chip, cc, (x, y, cc)).wait_recv()
            passed[j].start()
        for j, chip in enumerate(chips):
            copy(3 + j, *chip, 1 - cc, (x, y, cc)).wait_recv()
        for cp in first + passed:
            cp.wait_send()
        mine.wait()

    return pl.pallas_call(
        body, name="all_gather_small", in_specs=[ANY], out_specs=ANY,
        out_shape=jax.ShapeDtypeStruct((N_CHIPS, r, c), shard.dtype),
        scratch_shapes=[pltpu.SemaphoreType.DMA((6,)), pltpu.SemaphoreType.DMA((6,)), pltpu.SemaphoreType.DMA],
    )(shard)


def _cast_bf16(a, *, name, tm=512):
    n, r, c = a.shape
    tm = _tile(r, tm) if r % 128 == 0 else r

    def body(a_ref, o_ref):
        o_ref[...] = a_ref[...].astype(BF16)

    blk = pl.BlockSpec((1, tm, c), lambda i, j: (i, j, 0))
    return pl.pallas_call(body, name=name, grid=(n, r // tm), in_specs=[blk], out_specs=blk,
                          out_shape=jax.ShapeDtypeStruct(a.shape, BF16), compiler_params=_cp(("parallel", "parallel")))(a)


def _pair_exchange(g16, hr):
    n, r, c = g16.shape

    def body(g_ref, out_ref, send_sem, recv_sem):
        x, y, cc = lax.axis_index("x"), lax.axis_index("y"), lax.axis_index("c")
        cp = pltpu.make_async_remote_copy(
            src_ref=g_ref.at[:, pl.ds((1 - cc) * hr, hr), :], dst_ref=out_ref, send_sem=send_sem, recv_sem=recv_sem,
            device_id=(x, y, 1 - cc), device_id_type=MESH)
        cp.start()
        cp.wait()

    return pl.pallas_call(
        body, name="grad_pair_exchange", in_specs=[ANY], out_specs=ANY,
        out_shape=jax.ShapeDtypeStruct((n, hr, c), g16.dtype),
        scratch_shapes=[pltpu.SemaphoreType.DMA, pltpu.SemaphoreType.DMA],
    )(g16)


def _pair_add(g, recv, half_idx, hr, *, tm=384):
    n, r, c = g.shape
    nt = hr // tm

    def body(hi_ref, g_ref, r_ref, o32_ref, o16_ref):
        v = g_ref[...] + r_ref[...].astype(F32)
        o32_ref[...] = v
        o16_ref[...] = v.astype(BF16)

    gs = pltpu.PrefetchScalarGridSpec(
        num_scalar_prefetch=1, grid=(n, nt),
        in_specs=[pl.BlockSpec((1, tm, c), lambda i, j, hi: (i, hi[0] * nt + j, 0)),
                  pl.BlockSpec((1, tm, c), lambda i, j, hi: (i, j, 0))],
        out_specs=[pl.BlockSpec((1, tm, c), lambda i, j, hi: (i, j, 0))] * 2)
    return pl.pallas_call(
        body, name="grad_pair_add", grid_spec=gs,
        out_shape=[jax.ShapeDtypeStruct((n, hr, c), F32), jax.ShapeDtypeStruct((n, hr, c), BF16)],
        compiler_params=_cp(("parallel", "parallel")),
    )(half_idx, g, recv)


def _chip_exchange(p16):
    n, hr, c = p16.shape

    def body(p_ref, out_ref, send_sems, recv_sems):
        x, y, cc = lax.axis_index("x"), lax.axis_index("y"), lax.axis_index("c")
        cps = []
        for j, k in enumerate((1, 2, 3)):
            px, py = _chip_of(k, x, y)
            cps.append(pltpu.make_async_remote_copy(
                src_ref=p_ref.at[2 * px + py], dst_ref=out_ref.at[j], send_sem=send_sems.at[j], recv_sem=recv_sems.at[j],
                device_id=(px, py, cc), device_id_type=MESH))
        for cp in cps:
            cp.start()
        for cp in cps:
            cp.wait()

    return pl.pallas_call(
        body, name="grad_chip_exchange", in_specs=[ANY], out_specs=ANY,
        out_shape=jax.ShapeDtypeStruct((3, hr, c), p16.dtype),
        scratch_shapes=[pltpu.SemaphoreType.DMA((3,)), pltpu.SemaphoreType.DMA((3,))],
    )(p16)


def _chip_add(p32, recv, chip_idx, *, tm=384):
    n, hr, c = p32.shape

    def body(ci_ref, p_ref, r_ref, o_ref):
        o_ref[...] = ((p_ref[0] + r_ref[0].astype(F32)) + r_ref[1].astype(F32)) + r_ref[2].astype(F32)

    gs = pltpu.PrefetchScalarGridSpec(
        num_scalar_prefetch=1, grid=(hr // tm,),
        in_specs=[pl.BlockSpec((1, tm, c), lambda j, ci: (ci[0], j, 0)), pl.BlockSpec((3, tm, c), lambda j, ci: (0, j, 0))],
        out_specs=pl.BlockSpec((tm, c), lambda j, ci: (j, 0)))
    return pl.pallas_call(
        body, name="grad_chip_add", grid_spec=gs, out_shape=jax.ShapeDtypeStruct((hr, c), F32),
        compiler_params=_cp(("parallel",)),
    )(chip_idx, p32, recv)


def _pair_gather(f):
    hr, c = f.shape

    def body(f_ref, out_ref, send_sem, recv_sem, local_sem):
        x, y, cc = lax.axis_index("x"), lax.axis_index("y"), lax.axis_index("c")
        mine = pltpu.make_async_copy(f_ref, out_ref.at[pl.ds(cc * hr, hr), :], local_sem)
        mine.start()
        cp = pltpu.make_async_remote_copy(
            src_ref=f_ref, dst_ref=out_ref.at[pl.ds(cc * hr, hr), :], send_sem=send_sem, recv_sem=recv_sem,
            device_id=(x, y, 1 - cc), device_id_type=MESH)
        cp.start()
        cp.wait()
        mine.wait()

    return pl.pallas_call(
        body, name="grad_pair_gather", in_specs=[ANY], out_specs=ANY,
        out_shape=jax.ShapeDtypeStruct((2 * hr, c), f.dtype),
        scratch_shapes=[pltpu.SemaphoreType.DMA, pltpu.SemaphoreType.DMA, pltpu.SemaphoreType.DMA],
    )(f)


def _all_reduce_small(buf):
    r, c = buf.shape

    def body(b_ref, out_ref, gat, send_sems, recv_sems):
        x, y, cc = lax.axis_index("x"), lax.axis_index("y"), lax.axis_index("c")
        me = 4 * x + 2 * y + cc
        gat[me] = b_ref[...]
        cps = []
        for k in range(1, 8):
            px, py, pc = x ^ (k >> 2), y ^ ((k >> 1) & 1), cc ^ (k & 1)
            cps.append(pltpu.make_async_remote_copy(
                src_ref=b_ref, dst_ref=gat.at[me], send_sem=send_sems.at[k - 1], recv_sem=recv_sems.at[k - 1],
                device_id=(px, py, pc), device_id_type=MESH))
        for cp in cps:
            cp.start()
        for cp in cps:
            cp.wait()
        acc = gat[0]
        for d in range(1, 8):
            acc = acc + gat[d]
        out_ref[...] = acc

    vm = pl.BlockSpec(memory_space=pltpu.VMEM)
    return pl.pallas_call(
        body, name="all_reduce_small", in_specs=[vm], out_specs=vm, out_shape=jax.ShapeDtypeStruct((r, c), F32),
        scratch_shapes=[pltpu.VMEM((8, r, c), F32), pltpu.SemaphoreType.DMA((7,)), pltpu.SemaphoreType.DMA((7,))],
        compiler_params=pltpu.CompilerParams(vmem_limit_bytes=VMEM_LIMIT),
    )(buf)


def _pipe(fn, ins, outs, tr):
    shape = ins[0].shape
    lead, (r, c) = shape[:-2], shape[-2:]
    assert len(lead) <= 1 and r % tr == 0
    nr = r // tr
    n = nr * (lead[0] if lead else 1)
    ni, no = len(ins), len(outs)

    def blk(ref, step):
        rows = pl.ds(pl.multiple_of((step % nr) * tr, tr), tr)
        return ref.at[step // nr, rows, :] if lead else ref.at[rows, :]

    def scoped(*bufs):
        ibufs, obufs, isem, osem = bufs[:ni], bufs[ni:ni + no], bufs[-2], bufs[-1]

        def in_copy(q, step, slot):
            return pltpu.make_async_copy(blk(ins[q], step), ibufs[q].at[slot], isem.at[q, slot])

        def out_copy(q, step, slot):
            return pltpu.make_async_copy(obufs[q].at[slot], blk(outs[q], step), osem.at[q, slot])

        for q in range(ni):
            in_copy(q, 0, 0).start()

        def body(step, carry):
            slot = step % 2

            @pl.when(step + 1 < n)
            def _():
                for q in range(ni):
                    in_copy(q, step + 1, 1 - slot).start()

            for q in range(ni):
                in_copy(q, step, slot).wait()

            @pl.when(step >= 2)
            def _():
                for q in range(no):
                    out_copy(q, step - 2, slot).wait()

            res = fn(*[ibufs[q][slot] for q in range(ni)])
            for q in range(no):
                obufs[q][slot] = res[q].astype(obufs[q].dtype)
                out_copy(q, step, slot).start()
            return carry

        lax.fori_loop(0, n, body, 0)
        for step in range(max(n - 2, 0), n):
            for q in range(no):
                out_copy(q, step, step % 2).wait()

    pl.run_scoped(scoped, *[pltpu.VMEM((2, tr, c), q.dtype) for q in ins], *[pltpu.VMEM((2, tr, c), q.dtype) for q in outs],
                  pltpu.SemaphoreType.DMA((ni, 2)), pltpu.SemaphoreType.DMA((no, 2)))


W_IN_PAD = 2304
BIG = ("w_in", "w_attn_o", "w_ssd_o", "w_out", "w_up", "w_down")
BIG_SHAPE = dict(w_in=(D_MODEL, W_IN_PAD), w_attn_o=(Q_DIM // 4, D_MODEL), w_ssd_o=(D_INNER // 4, D_MODEL),
                 w_out=(D_MODEL // 4, D_MODEL), w_up=(D_MODEL, 2 * D_FF // 4), w_down=(D_FF // 4, D_MODEL))
BIG_TR = dict(w_in=128, w_attn_o=128, w_ssd_o=128, w_out=128, w_up=128, w_down=176)
X_FIRST = dict(w_in=True, w_attn_o=True, w_ssd_o=False, w_out=True, w_up=False, w_down=False)


def _neighbours(x, y, x_first):
    xn, yn = (1 - x, y), (x, 1 - y)
    n1, n2 = (xn, yn) if x_first else (yn, xn)
    slot = lambda ch: 2 * ch[0] + ch[1]
    return n1, n2, slot(n1), slot(n2), slot((1 - x, 1 - y))


def _gather_big(shards):
    nt = len(BIG)

    def body(*refs):
        sh, out = refs[:nt], refs[nt:2 * nt]
        send_sems, recv_sems = refs[2 * nt:]
        x, y, cc = lax.axis_index("x"), lax.axis_index("y"), lax.axis_index("c")
        me = 2 * x + y
        sib = (x, y, 1 - cc)
        for t, n in enumerate(BIG):
            _pipe(lambda v: (v,), [sh[t]], [out[t].at[me]], BIG_TR[n])

        def copy(t, k, slot, pc, to):
            hr = BIG_SHAPE[BIG[t]][0] // 2
            ref = out[t].at[slot, pl.ds(pc * hr, hr), :]
            return pltpu.make_async_remote_copy(src_ref=ref, dst_ref=ref, send_sem=send_sems.at[6 * t + k],
                                                recv_sem=recv_sems.at[6 * t + k], device_id=to, device_id_type=MESH)

        started = []

        def start(cp):
            cp.start()
            started.append(cp)

        geo = [_neighbours(x, y, X_FIRST[n]) for n in BIG]
        for t in range(nt):
            n1, n2, _, _, _ = geo[t]
            start(copy(t, 0, me, cc, (*n1, cc)))
            start(copy(t, 1, me, cc, (*n2, cc)))
        for t in range(nt):
            n1, n2, s1, s2, sd = geo[t]
            copy(t, 0, s1, cc, sib).wait_recv()
            start(copy(t, 2, s1, cc, (*n2, cc)))
            start(copy(t, 3, s1, cc, sib))
            copy(t, 1, s2, cc, sib).wait_recv()
            start(copy(t, 4, s2, cc, sib))
        for t in range(nt):
            _, _, s1, s2, sd = geo[t]
            copy(t, 2, sd, cc, sib).wait_recv()
            start(copy(t, 5, sd, cc, sib))
        for t in range(nt):
            _, _, s1, s2, sd = geo[t]
            copy(t, 3, s1, 1 - cc, sib).wait_recv()
            copy(t, 4, s2, 1 - cc, sib).wait_recv()
            copy(t, 5, sd, 1 - cc, sib).wait_recv()
        for cp in started:
            cp.wait_send()

    return pl.pallas_call(
        body, name="gather_big", in_specs=[ANY] * nt, out_specs=[ANY] * nt,
        out_shape=[jax.ShapeDtypeStruct((N_CHIPS, *BIG_SHAPE[n]), BF16) for n in BIG],
        scratch_shapes=[pltpu.SemaphoreType.DMA((6 * nt,)), pltpu.SemaphoreType.DMA((6 * nt,))],
        compiler_params=pltpu.CompilerParams(vmem_limit_bytes=VMEM_LIMIT),
    )(*shards)


def _reduce_big(grads):
    nt = len(BIG)
    nw = 7

    def body(*refs):
        g = refs[:nt]
        fin = refs[nt:2 * nt]
        work = refs[2 * nt:2 * nt + nw * nt]
        send_sems, recv_sems = refs[2 * nt + nw * nt:]
        x, y, cc = lax.axis_index("x"), lax.axis_index("y"), lax.axis_index("c")
        me = 2 * x + y
        sib = (x, y, 1 - cc)
        started = []

        def rcopy(t, k, src, dst, to):
            cp = pltpu.make_async_remote_copy(src_ref=src, dst_ref=dst, send_sem=send_sems.at[5 * t + k],
                                              recv_sem=recv_sems.at[5 * t + k], device_id=to, device_id_type=MESH)
            return cp

        def start(cp):
            cp.start()
            started.append(cp)

        geo = [_neighbours(x, y, X_FIRST[n]) for n in BIG]
        hrs = [BIG_SHAPE[n][0] // 2 for n in BIG]
        wk = lambda t: work[nw * t:nw * (t + 1)]
        one = lambda ref, slot: ref.at[pl.ds(slot, 1)]
        for t in range(nt):
            recv_a = wk(t)[0]
            start(rcopy(t, 0, g[t].at[:, pl.ds((1 - cc) * hrs[t], hrs[t]), :], recv_a, sib))
        for t, n in enumerate(BIG):
            recv_a, p32, p16, r1, qme, qs2, r2 = wk(t)
            n1, n2, s1, s2, sd = geo[t]
            rcopy(t, 0, recv_a, recv_a, sib).wait_recv()
            _pipe(lambda a, b: (a + b, a + b), [g[t].at[:, pl.ds(cc * hrs[t], hrs[t]), :], recv_a], [p32, p16], BIG_TR[n])
            start(rcopy(t, 1, one(p16, s1), one(r1, 0), (*n1, cc)))
            start(rcopy(t, 2, one(p16, sd), one(r1, 1), (*n1, cc)))
        for t, n in enumerate(BIG):
            recv_a, p32, p16, r1, qme, qs2, r2 = wk(t)
            n1, n2, s1, s2, sd = geo[t]
            rcopy(t, 1, one(r1, 0), one(r1, 0), sib).wait_recv()
            rcopy(t, 2, one(r1, 1), one(r1, 1), sib).wait_recv()
            _pipe(lambda a, b: (a + b.astype(F32),), [one(p32, s2), one(r1, 1)], [qs2], BIG_TR[n])
            start(rcopy(t, 3, qs2, r2, (*n2, cc)))
            _pipe(lambda a, b: (a + b.astype(F32),), [one(p32, me), one(r1, 0)], [qme], BIG_TR[n])
        for t, n in enumerate(BIG):
            recv_a, p32, p16, r1, qme, qs2, r2 = wk(t)
            rcopy(t, 3, r2, r2, sib).wait_recv()
            mine = fin[t].at[pl.ds(cc * hrs[t], hrs[t]), :]
            _pipe(lambda a, b: (a + b.astype(F32),), [qme.at[0], r2.at[0]], [mine], BIG_TR[n])
            start(rcopy(t, 4, mine, mine, sib))
        for t in range(nt):
            other = fin[t].at[pl.ds((1 - cc) * hrs[t], hrs[t]), :]
            rcopy(t, 4, other, other, sib).wait_recv()
        for cp in started:
            cp.wait_send()

    outs = [jax.ShapeDtypeStruct(BIG_SHAPE[n], F32) for n in BIG]
    for n in BIG:
        r, c = BIG_SHAPE[n]
        hr = r // 2
        outs += [jax.ShapeDtypeStruct((4, hr, c), F32), jax.ShapeDtypeStruct((4, hr, c), F32),
                 jax.ShapeDtypeStruct((4, hr, c), BF16), jax.ShapeDtypeStruct((2, hr, c), BF16),
                 jax.ShapeDtypeStruct((1, hr, c), F32), jax.ShapeDtypeStruct((1, hr, c), BF16),
                 jax.ShapeDtypeStruct((1, hr, c), BF16)]
    res = pl.pallas_call(
        body, name="reduce_big", in_specs=[ANY] * nt, out_specs=[ANY] * len(outs), out_shape=outs,
        scratch_shapes=[pltpu.SemaphoreType.DMA((5 * nt,)), pltpu.SemaphoreType.DMA((5 * nt,))],
        compiler_params=pltpu.CompilerParams(vmem_limit_bytes=VMEM_LIMIT),
    )(*grads)
    return res[:nt]


def _proj_dw(xn, dproj_sh, *, tm=512, tk=1024):
    s, d = xn.shape
    tk = _tile(s, tk)
    nk = s // tk

    def body(a_ref, b_ref, o_ref, acc):
        kk = pl.program_id(2)
        part = _dot_tn(a_ref[...], b_ref[0])

        @pl.when(kk == 0)
        def _():
            acc[...] = part

        @pl.when(kk > 0)
        def _():
            acc[...] += part

        @pl.when(kk == nk - 1)
        def _():
            o_ref[0] = acc[...]

    return pl.pallas_call(
        body, name="proj_dw", grid=(N_CHIPS, d // tm, nk),
        in_specs=[pl.BlockSpec((tk, tm), lambda j, i, q: (q, i)), pl.BlockSpec((1, tk, W_IN_PAD), lambda j, i, q: (j, q, 0))],
        out_specs=pl.BlockSpec((1, tm, W_IN_PAD), lambda j, i, q: (j, i, 0)),
        out_shape=jax.ShapeDtypeStruct((N_CHIPS, d, W_IN_PAD), F32), scratch_shapes=[pltpu.VMEM((tm, W_IN_PAD), F32)],
        compiler_params=_cp(("parallel", "parallel", "arbitrary")),
    )(xn, dproj_sh)


def _proj_dx(dproj_sh, w_sh, *, tm=1024):
    s = dproj_sh.shape[1]
    d = w_sh.shape[1]
    tm = _tile(s, tm)

    def body(a_ref, b_ref, o_ref, acc):
        kk = pl.program_id(1)
        part = _dot_nt(a_ref[0], b_ref[0])

        @pl.when(kk == 0)
        def _():
            acc[...] = part

        @pl.when(kk > 0)
        def _():
            acc[...] += part

        @pl.when(kk == N_CHIPS - 1)
        def _():
            o_ref[...] = acc[...]

    return pl.pallas_call(
        body, name="proj_dx", grid=(s // tm, N_CHIPS),
        in_specs=[pl.BlockSpec((1, tm, W_IN_PAD), lambda i, q: (q, i, 0)), pl.BlockSpec((1, d, W_IN_PAD), lambda i, q: (q, 0, 0))],
        out_specs=pl.BlockSpec((tm, d), lambda i, q: (i, 0)),
        out_shape=jax.ShapeDtypeStruct((s, d), F32), scratch_shapes=[pltpu.VMEM((tm, d), F32)],
        compiler_params=_cp(("parallel", "arbitrary")),
    )(dproj_sh, w_sh)


def _up_dw(hn, dup, *, tk=1024):
    s, d = hn.shape
    wsh = 2 * D_FF // N_CHIPS
    tk = _tile(s, tk)
    nk = s // tk

    def body(a_ref, b_ref, o_ref, acc):
        kk = pl.program_id(1)
        part = _dot_tn(a_ref[...], b_ref[...])

        @pl.when(kk == 0)
        def _():
            acc[...] = part

        @pl.when(kk > 0)
        def _():
            acc[...] += part

        @pl.when(kk == nk - 1)
        def _():
            o_ref[0] = acc[...]

    return pl.pallas_call(
        body, name="up_dw", grid=(N_CHIPS, nk),
        in_specs=[pl.BlockSpec((tk, d), lambda j, q: (q, 0)), pl.BlockSpec((tk, wsh), lambda j, q: (q, j))],
        out_specs=pl.BlockSpec((1, d, wsh), lambda j, q: (j, 0, 0)),
        out_shape=jax.ShapeDtypeStruct((N_CHIPS, d, wsh), F32), scratch_shapes=[pltpu.VMEM((d, wsh), F32)],
        compiler_params=_cp(("parallel", "arbitrary")),
    )(hn, dup)


BIG_ROWS =(IN_DIM // 4, Q_DIM // 4, D_INNER // 4, D_MODEL // 4, 2 * D_FF // 4, D_FF // 4)
PACK_ROWS = 5376


def _pack_shards(parts):
    rows = [p.reshape(-1, D_MODEL) for p in parts]
    pad = PACK_ROWS - sum(BIG_ROWS)
    return jnp.concatenate(rows + [jnp.zeros((pad, D_MODEL), rows[0].dtype)], axis=0)


def _unpack_shards(buf):
    out, off = [], 0
    for n in BIG_ROWS:
        out.append(buf[off:off + n])
        off += n
    return out


def _permute_cols_in(w):
    pad = jnp.zeros((w.shape[0], PW - IN_DIM), w.dtype)
    return jnp.concatenate([w[:, :6656], w[:, 6688:], w[:, 6656:6688], pad], axis=1)


def _unpermute_cols_in(g):
    return jnp.concatenate([g[:, :6656], g[:, O_DT:O_DT + 32], g[:, 6656:O_DT]], axis=1)


SMALL = ("norm1_w", "b_gate", "attn_sinks", "ssd_conv_b", "dt_bias", "a_log", "d_skip", "ssd_norm_w", "norm2_w",
         "ffn_conv_b", "final_norm_w", "ssd_conv_w", "ffn_conv_w")


def _pad128(v):
    v = v.reshape(-1)
    return jnp.pad(v, (0, (-v.shape[0]) % 128))


def _pack_small(parts):
    flat = jnp.concatenate([_pad128(p) for p in parts])
    flat = jnp.pad(flat, (0, (-flat.shape[0]) % 1024))
    return flat.reshape(-1, 128)


def _unpack_small(buf, shapes):
    flat, out, off = buf.reshape(-1), [], 0
    for shp in shapes:
        n = 1
        for q in shp:
            n *= q
        out.append(flat[off:off + n].reshape(shp))
        off += n + (-n) % 128
    return out


def _vec128(v):
    return jnp.pad(v.reshape(1, -1), ((0, 0), (0, 128 - v.shape[-1])))


def kernel(x, norm1_w, w_in, b_gate, attn_sinks, w_attn_o, ssd_conv_w, ssd_conv_b, dt_bias, a_log, d_skip, ssd_norm_w, w_ssd_o, w_out, norm2_w, w_up, ffn_conv_w, ffn_conv_b, w_down, final_norm_w, loss_target, m_norm1_w, m_w_in, m_b_gate, m_attn_sinks, m_w_attn_o, m_ssd_conv_w, m_ssd_conv_b, m_dt_bias, m_a_log, m_d_skip, m_ssd_norm_w, m_w_ssd_o, m_w_out, m_norm2_w, m_w_up, m_ffn_conv_w, m_ffn_conv_b, m_w_down, m_final_norm_w, v_norm1_w, v_w_in, v_b_gate, v_attn_sinks, v_w_attn_o, v_ssd_conv_w, v_ssd_conv_b, v_dt_bias, v_a_log, v_d_skip, v_ssd_norm_w, v_w_ssd_o, v_w_out, v_norm2_w, v_w_up, v_ffn_conv_w, v_ffn_conv_b, v_w_down, v_final_norm_w):
    ix, iy, ic = lax.axis_index("x"), lax.axis_index("y"), lax.axis_index("c")
    chip = 2 * ix + iy
    x2 = x[0]
    tgt = loss_target[0]
    s = x2.shape[0]

    wsh = IN_DIM // N_CHIPS
    big_shards = dict(w_in=jnp.pad(w_in[0], ((0, 0), (0, W_IN_PAD - wsh))), w_attn_o=w_attn_o[0], w_ssd_o=w_ssd_o[0],
                      w_out=w_out[0], w_up=w_up[0], w_down=w_down[0])
    gathered = dict(zip(BIG, _gather_big([big_shards[n] for n in BIG])))
    full = {n: gathered[n].reshape(-1, D_MODEL) for n in ("w_attn_o", "w_ssd_o", "w_out", "w_down")}
    full["w_up"] = gathered["w_up"]
    w_in_p = _permute_cols_in(jnp.concatenate([gathered["w_in"][j, :, :wsh] for j in range(N_CHIPS)], axis=1))
    small_sh = _pack_small([ssd_conv_w[0], ffn_conv_w[0]])
    small_all = _all_gather_small(small_sh)
    sc_parts = [_unpack_small(small_all[j], [(4, XBC_DIM // 4), (3, 2 * D_FF // 4)]) for j in range(N_CHIPS)]
    ssd_cw = jnp.concatenate([p[0] for p in sc_parts], axis=1)
    ffn_cw = jnp.concatenate([p[1] for p in sc_parts], axis=1)

    sinks128 = _vec128(attn_sinks)
    dtb128, alog128, dskip128 = _vec128(dt_bias), _vec128(a_log), _vec128(d_skip)

    xn = _rms_fwd(x2, norm1_w, name="norm1_fwd")
    proj = _mm(xn, w_in_p, name="proj_fwd", tn=1280)
    attn_pre = _attn_fwd(proj, sinks128)
    attn = _mm(attn_pre, full["w_attn_o"], name="attn_o_fwd")
    xbc = _ssd_conv_fwd(proj, ssd_cw, ssd_conv_b)
    y_ssd, hprev = _ssd_fwd(xbc, proj, dtb128, alog128, dskip128)
    yn = _gate_norm_fwd(y_ssd, proj, ssd_norm_w)
    ssd_out = _mm(yn, full["w_ssd_o"], name="ssd_o_fwd")
    merged = _merge_fwd(proj, b_gate, attn, ssd_out)
    h1 = _mm(merged, full["w_out"], name="out_fwd", resid=x2)
    hn = _rms_fwd(h1, norm2_w, name="norm2_fwd")
    up = _mm(hn, full["w_up"], name="up_fwd")
    act = _ffn_act_fwd(up, ffn_cw, ffn_conv_b)
    h2 = _mm(act, full["w_down"], name="down_fwd", resid=h1, tk=1408)

    dh2, loss_blk, g_final = _loss_bwd(h2, tgt, final_norm_w.reshape(1, -1))
    dact = _mm(dh2, full["w_down"], name="down_dx", tb=True, tn=1408)
    g_down = _mm(act, dh2, name="down_dw", ta=True, tm=1408)
    dup, g_ffn_cw, g_ffn_cb = _ffn_act_bwd(dact, up, ffn_cw, ffn_conv_b)
    dhn = _mm(dup, full["w_up"], name="up_dx", tb=True)
    g_up = _up_dw(hn, dup)
    dh1, g_norm2 = _rms_bwd(dhn, h1, norm2_w, dh2, name="norm2_bwd")
    dmerged = _mm(dh1, full["w_out"], name="out_dx", tb=True)
    g_out = _mm(merged, dh1, name="out_dw", ta=True)
    dattn, dssd_out, dga, dgs, g_ba, g_bs = _merge_bwd(dmerged, proj, b_gate, attn, ssd_out)
    dyn = _mm(dssd_out, full["w_ssd_o"], name="ssd_o_dx", tb=True)
    g_ssd_o = _mm(yn, dssd_out, name="ssd_o_dw", ta=True)
    dy_ssd, dz, g_ssd_norm = _gate_norm_bwd(dyn, y_ssd, proj, ssd_norm_w)
    dxbc, ddt, dvec = _ssd_bwd(xbc, proj, dtb128, alog128, dskip128, hprev, dy_ssd)
    dxbc_raw, g_ssd_cw, g_ssd_cb = _ssd_conv_bwd(dxbc, proj, ssd_cw, ssd_conv_b)
    dattn_pre = _mm(dattn, full["w_attn_o"], name="attn_o_dx", tb=True)
    g_attn_o = _mm(attn_pre, dattn, name="attn_o_dw", ta=True)
    dq, dk, dv, dsk = _attn_bwd(proj, sinks128, attn_pre, dattn_pre)
    dproj = jnp.concatenate([dq, dk, dv, dz, dxbc_raw, ddt[:, :N_SSD_HEADS], dga, dgs], axis=1)
    dproj_sh = jnp.pad(dproj.reshape(s, N_CHIPS, wsh).transpose(1, 0, 2), ((0, 0), (0, 0), (0, W_IN_PAD - wsh)))
    dxn = _proj_dx(dproj_sh, gathered["w_in"])
    g_in = _proj_dw(xn, dproj_sh)
    dx, g_norm1 = _rms_bwd(dxn, x2, norm1_w, dh1, name="norm1_bwd")

    slot_g = dict(w_in=g_in, w_up=g_up)
    for n, g in (("w_attn_o", g_attn_o), ("w_ssd_o", g_ssd_o), ("w_out", g_out), ("w_down", g_down)):
        slot_g[n] = g.reshape(N_CHIPS, -1, D_MODEL)
    big_grads = dict(zip(BIG, _reduce_big([slot_g[n] for n in BIG])))
    big_grads["w_in"] = big_grads["w_in"][:, :wsh]

    small_g = dict(
        norm1_w=g_norm1, b_gate=jnp.concatenate([g_ba, g_bs], axis=1), attn_sinks=dsk[0:1, :16], ssd_conv_b=g_ssd_cb,
        dt_bias=dvec[0:1, :32], a_log=dvec[1:2, :32], d_skip=dvec[2:3, :32], ssd_norm_w=g_ssd_norm, norm2_w=g_norm2,
        ffn_conv_b=g_ffn_cb, final_norm_w=g_final, ssd_conv_w=g_ssd_cw, ffn_conv_w=g_ffn_cw)
    small_buf = _pack_small([small_g[n] for n in SMALL] + [loss_blk])
    small_sum = _all_reduce_small(small_buf)
    small_shapes = [(1, D_MODEL), (1, 2 * D_MODEL), (1, 16), (1, XBC_DIM), (1, 32), (1, 32), (1, 32), (1, D_INNER),
                    (1, D_MODEL), (1, 2 * D_FF), (D_MODEL,), (4, XBC_DIM), (3, 2 * D_FF), (1, 128)]
    small_list = _unpack_small(small_sum, small_shapes)
    loss = small_list[-1][0, 0]
    grads = dict(zip(SMALL, small_list[:-1]))
    grads["ssd_conv_w"] = lax.dynamic_slice_in_dim(grads["ssd_conv_w"], chip * (XBC_DIM // 4), XBC_DIM // 4, axis=1)
    grads["ffn_conv_w"] = lax.dynamic_slice_in_dim(grads["ffn_conv_w"], chip * (2 * D_FF // 4), 2 * D_FF // 4, axis=1)
    grads.update(big_grads)

    weights = dict(norm1_w=norm1_w, w_in=w_in, b_gate=b_gate, attn_sinks=attn_sinks, w_attn_o=w_attn_o, ssd_conv_w=ssd_conv_w,
                   ssd_conv_b=ssd_conv_b, dt_bias=dt_bias, a_log=a_log, d_skip=d_skip, ssd_norm_w=ssd_norm_w, w_ssd_o=w_ssd_o,
                   w_out=w_out, norm2_w=norm2_w, w_up=w_up, ffn_conv_w=ffn_conv_w, ffn_conv_b=ffn_conv_b, w_down=w_down,
                   final_norm_w=final_norm_w)
    ms = dict(norm1_w=m_norm1_w, w_in=m_w_in, b_gate=m_b_gate, attn_sinks=m_attn_sinks, w_attn_o=m_w_attn_o,
              ssd_conv_w=m_ssd_conv_w, ssd_conv_b=m_ssd_conv_b, dt_bias=m_dt_bias, a_log=m_a_log, d_skip=m_d_skip,
              ssd_norm_w=m_ssd_norm_w, w_ssd_o=m_w_ssd_o, w_out=m_w_out, norm2_w=m_norm2_w, w_up=m_w_up,
              ffn_conv_w=m_ffn_conv_w, ffn_conv_b=m_ffn_conv_b, w_down=m_w_down, final_norm_w=m_final_norm_w)
    vs = dict(norm1_w=v_norm1_w, w_in=v_w_in, b_gate=v_b_gate, attn_sinks=v_attn_sinks, w_attn_o=v_w_attn_o,
              ssd_conv_w=v_ssd_conv_w, ssd_conv_b=v_ssd_conv_b, dt_bias=v_dt_bias, a_log=v_a_log, d_skip=v_d_skip,
              ssd_norm_w=v_ssd_norm_w, w_ssd_o=v_w_ssd_o, w_out=v_w_out, norm2_w=v_norm2_w, w_up=v_w_up,
              ffn_conv_w=v_ffn_conv_w, ffn_conv_b=v_ffn_conv_b, w_down=v_w_down, final_norm_w=v_final_norm_w)
    order = list(weights)
    deltas, new_m, new_v = {}, {}, {}
    for n in BIG:
        shp = weights[n].shape
        d_, m_, v_ = _adamw(weights[n][0], grads[n], ms[n][0], vs[n][0], name="adamw_" + n)
        deltas[n], new_m[n], new_v[n] = d_.reshape(shp), m_.reshape(shp), v_.reshape(shp)
    smalls = [n for n in order if n not in BIG]
    sw = _pack_small([weights[n] for n in smalls])
    sg = _pack_small([grads[n] for n in smalls])
    sm = _pack_small([ms[n] for n in smalls])
    sv = _pack_small([vs[n] for n in smalls])
    sd_, sm_, sv_ = _adamw(sw, sg, sm, sv, name="adamw_small")
    shapes = [weights[n].shape for n in smalls]
    for n, d_, m_, v_ in zip(smalls, _unpack_small(sd_, shapes), _unpack_small(sm_, shapes), _unpack_small(sv_, shapes)):
        deltas[n], new_m[n], new_v[n] = d_, m_, v_
    out_grads = [grads[n].reshape(weights[n].shape) for n in order]
    return (loss, dx[None], *out_grads, *[deltas[n] for n in order], *[new_m[n] for n in order], *[new_v[n] for n in order])
```

```python
import functools

import jax
import jax.numpy as jnp
from jax import lax
from jax.experimental import pallas as pl
from jax.experimental.pallas import tpu as pltpu

F32 = jnp.float32
BF16 = jnp.bfloat16
HI = lax.Precision.HIGHEST

D_MODEL = 1024
Q_DIM = 1024
KV_DIM = 256
D_INNER = 2048
BC_DIM = 512
XBC_DIM = 3072
N_SSD_HEADS = 32
D_FF = 2816
IN_DIM = 8736
BLK = 128
EPS = 1e-5
NEG = -1e30

O_Q, O_K, O_V, O_Z, O_X, O_GA, O_GS, O_DT = 0, 1024, 1280, 1536, 3584, 6656, 7680, 8704
PW = 8960

ADAM_LR, ADAM_B1, ADAM_B2, ADAM_EPS, ADAM_WD, ADAM_STEP = 0.001, 0.9, 0.999, 1e-08, 0.01, 10

VMEM_LIMIT = 52 * 1024 * 1024
MESH = pl.DeviceIdType.MESH


def _cp(sem=None):
    return pltpu.CompilerParams(dimension_semantics=sem, vmem_limit_bytes=VMEM_LIMIT)


def _dot(a, b, prec=None):
    return jnp.dot(a, b, preferred_element_type=F32, precision=prec)


def _dot_nt(a, b, prec=None):
    return lax.dot_general(a, b, (((1,), (1,)), ((), ())), preferred_element_type=F32, precision=prec)


def _dot_tn(a, b, prec=None):
    return lax.dot_general(a, b, (((0,), (0,)), ((), ())), preferred_element_type=F32, precision=prec)


def _sigmoid(x):
    return 0.5 * jnp.tanh(0.5 * x) + 0.5


def _tile(n, want):
    t = min(n, want)
    while n % t:
        t -= 128
    return t


def _mm(a, b, *, name, ta=False, tb=False, out_dtype=F32, resid=None, tm=1024, tn=1024, tk=1024):
    m, k = (a.shape[1], a.shape[0]) if ta else a.shape
    slots = b.ndim == 3
    if slots:
        n = b.shape[1] if tb else b.shape[0] * b.shape[2]
        tn, tk = (tn, b.shape[2]) if tb else (b.shape[2], tk)
    else:
        n = b.shape[0] if tb else b.shape[1]
    tm, tn, tk = _tile(m, tm), _tile(n, tn), _tile(k, tk)
    nk = k // tk
    dn = (((0 if ta else 1,), (1 if tb else 0,)), ((), ()))

    def body(*refs):
        if resid is None:
            a_ref, b_ref, o_ref, acc = refs
        else:
            a_ref, b_ref, r_ref, o_ref, acc = refs
        kk = pl.program_id(2)
        bv = b_ref[0] if slots else b_ref[...]
        part = lax.dot_general(a_ref[...].astype(BF16), bv.astype(BF16), dn, preferred_element_type=F32)

        @pl.when(kk == 0)
        def _():
            acc[...] = part

        @pl.when(kk > 0)
        def _():
            acc[...] += part

        @pl.when(kk == nk - 1)
        def _():
            r = acc[...]
            if resid is not None:
                r = r + r_ref[...]
            o_ref[...] = r.astype(out_dtype)

    a_spec = pl.BlockSpec((tk, tm), lambda i, j, q: (q, i)) if ta else pl.BlockSpec((tm, tk), lambda i, j, q: (i, q))
    if slots:
        b_spec = (pl.BlockSpec((1, tn, tk), lambda i, j, q: (q, j, 0)) if tb
                  else pl.BlockSpec((1, tk, tn), lambda i, j, q: (j, q, 0)))
    else:
        b_spec = pl.BlockSpec((tn, tk), lambda i, j, q: (j, q)) if tb else pl.BlockSpec((tk, tn), lambda i, j, q: (q, j))
    o_spec = pl.BlockSpec((tm, tn), lambda i, j, q: (i, j))
    ins, specs = [a, b], [a_spec, b_spec]
    if resid is not None:
        ins.append(resid)
        specs.append(o_spec)
    return pl.pallas_call(
        body, name=name, grid=(m // tm, n // tn, nk), in_specs=specs, out_specs=o_spec,
        out_shape=jax.ShapeDtypeStruct((m, n), out_dtype), scratch_shapes=[pltpu.VMEM((tm, tn), F32)],
        compiler_params=_cp(("parallel", "parallel", "arbitrary")),
    )(*ins)


def _rms_fwd(x, w, *, name, tm=512):
    s, d = x.shape
    tm = _tile(s, tm)

    def body(x_ref, w_ref, o_ref):
        xv = x_ref[...]
        r = lax.rsqrt(jnp.mean(xv * xv, axis=-1, keepdims=True) + EPS)
        o_ref[...] = ((xv * r) * w_ref[...]).astype(BF16)

    return pl.pallas_call(
        body, name=name, grid=(s // tm,),
        in_specs=[pl.BlockSpec((tm, d), lambda i: (i, 0)), pl.BlockSpec((1, d), lambda i: (0, 0))],
        out_specs=pl.BlockSpec((tm, d), lambda i: (i, 0)),
        out_shape=jax.ShapeDtypeStruct((s, d), BF16), compiler_params=_cp(("parallel",)),
    )(x, w)


def _rms_bwd(dy, x, w, resid, *, name, tm=512):
    s, d = x.shape
    tm = _tile(s, tm)

    def body(dy_ref, x_ref, w_ref, r_ref, dx_ref, dw_ref):
        i = pl.program_id(0)
        xv = x_ref[...]
        r = lax.rsqrt(jnp.mean(xv * xv, axis=-1, keepdims=True) + EPS)
        xh = xv * r
        dyv = dy_ref[...]
        g = dyv * w_ref[...]
        dx_ref[...] = r_ref[...] + r * (g - xh * jnp.mean(g * xh, axis=-1, keepdims=True))
        part = jnp.sum(dyv * xh, axis=0, keepdims=True)

        @pl.when(i == 0)
        def _():
            dw_ref[...] = part

        @pl.when(i > 0)
        def _():
            dw_ref[...] += part

    row = pl.BlockSpec((tm, d), lambda i: (i, 0))
    vec = pl.BlockSpec((1, d), lambda i: (0, 0))
    return pl.pallas_call(
        body, name=name, grid=(s // tm,), in_specs=[row, row, vec, row], out_specs=[row, vec],
        out_shape=[jax.ShapeDtypeStruct((s, d), F32), jax.ShapeDtypeStruct((1, d), F32)],
        compiler_params=_cp(("arbitrary",)),
    )(dy, x, w, resid)


def _loss_bwd(h2, tgt, wf, *, tm=512):
    s, d = h2.shape
    tm = _tile(s, tm)

    def body(h_ref, t_ref, w_ref, dh_ref, loss_ref, dw_ref):
        i = pl.program_id(0)
        hv = h_ref[...]
        r = lax.rsqrt(jnp.mean(hv * hv, axis=-1, keepdims=True) + EPS)
        xh = hv * r
        wv = w_ref[...]
        e = xh * wv - t_ref[...]
        lpart = 0.5 * jnp.sum(jnp.mean(e * e, axis=-1, keepdims=True), axis=0, keepdims=True)
        dout = e * (1.0 / d)
        g = dout * wv
        dh_ref[...] = r * (g - xh * jnp.mean(g * xh, axis=-1, keepdims=True))
        part = jnp.sum(dout * xh, axis=0, keepdims=True)
        lrow = jnp.broadcast_to(lpart, (1, 128))

        @pl.when(i == 0)
        def _():
            dw_ref[...] = part
            loss_ref[...] = lrow

        @pl.when(i > 0)
        def _():
            dw_ref[...] += part
            loss_ref[...] += lrow

    row = pl.BlockSpec((tm, d), lambda i: (i, 0))
    vec = pl.BlockSpec((1, d), lambda i: (0, 0))
    return pl.pallas_call(
        body, name="loss_bwd", grid=(s // tm,), in_specs=[row, row, vec],
        out_specs=[row, pl.BlockSpec((1, 128), lambda i: (0, 0)), vec],
        out_shape=[jax.ShapeDtypeStruct((s, d), F32), jax.ShapeDtypeStruct((1, 128), F32),
                   jax.ShapeDtypeStruct((1, d), F32)],
        compiler_params=_cp(("arbitrary",)),
    )(h2, tgt, wf)


def _attn_mask(n):
    qi = lax.broadcasted_iota(jnp.int32, (4 * BLK, 2 * BLK), 0) & (BLK - 1)
    si = lax.broadcasted_iota(jnp.int32, (4 * BLK, 2 * BLK), 1)
    dist = BLK + qi - si
    kpos = n * BLK - BLK + si
    return (dist >= 0) & (dist < BLK) & (kpos >= 0)


def _attn_probs(q_ref, kc_ref, kp_ref, sk_ref, kvh, valid):
    hs = slice(kvh * 64, (kvh + 1) * 64)
    kb = jnp.concatenate([kp_ref[:, hs], kc_ref[:, hs]], axis=0).astype(BF16)
    qs = jnp.concatenate([q_ref[:, (kvh * 4 + g) * 64:(kvh * 4 + g + 1) * 64] for g in range(4)], axis=0).astype(BF16)
    s = _dot_nt(qs, kb) * 0.125
    s = jnp.where(valid, s, NEG)
    sink = jnp.concatenate(
        [jnp.broadcast_to(sk_ref[0:1, kvh * 4 + g:kvh * 4 + g + 1], (BLK, 1)) for g in range(4)], axis=0)
    m = jnp.maximum(jnp.max(s, axis=1, keepdims=True), sink)
    p = jnp.where(valid, jnp.exp(s - m), 0.0)
    es = jnp.exp(sink - m)
    denom = jnp.sum(p, axis=1, keepdims=True) + es
    return qs, kb, p / denom, es / denom


def _attn_fwd(proj, sinks):
    s = proj.shape[0]
    nb = s // BLK

    def body(q_ref, kc_ref, kp_ref, vc_ref, vp_ref, sk_ref, o_ref):
        valid = _attn_mask(pl.program_id(0))
        for kvh in range(4):
            hs = slice(kvh * 64, (kvh + 1) * 64)
            _, _, probs, _ = _attn_probs(q_ref, kc_ref, kp_ref, sk_ref, kvh, valid)
            vb = jnp.concatenate([vp_ref[:, hs], vc_ref[:, hs]], axis=0).astype(BF16)
            o = _dot(probs.astype(BF16), vb)
            for g in range(4):
                h = kvh * 4 + g
                o_ref[:, h * 64:(h + 1) * 64] = o[g * BLK:(g + 1) * BLK].astype(BF16)

    prev = lambda n: jnp.maximum(n - 1, 0)
    return pl.pallas_call(
        body, name="attn_fwd", grid=(nb,),
        in_specs=[pl.BlockSpec((BLK, Q_DIM), lambda n: (n, 0)),
                  pl.BlockSpec((BLK, KV_DIM), lambda n: (n, O_K // KV_DIM)),
                  pl.BlockSpec((BLK, KV_DIM), lambda n: (prev(n), O_K // KV_DIM)),
                  pl.BlockSpec((BLK, KV_DIM), lambda n: (n, O_V // KV_DIM)),
                  pl.BlockSpec((BLK, KV_DIM), lambda n: (prev(n), O_V // KV_DIM)),
                  pl.BlockSpec((1, 128), lambda n: (0, 0))],
        out_specs=pl.BlockSpec((BLK, Q_DIM), lambda n: (n, 0)),
        out_shape=jax.ShapeDtypeStruct((s, Q_DIM), BF16), compiler_params=_cp(("parallel",)),
    )(proj, proj, proj, proj, proj, sinks)


def _attn_bwd(proj, sinks, o, do):
    s = proj.shape[0]
    nb = s // BLK

    def body(q_ref, kc_ref, kp_ref, vc_ref, vp_ref, sk_ref, o_ref, do_ref,
             dq_ref, dk_ref, dv_ref, dsk_ref, ck, cv, nkp, nkc, nvp, nvc):
        n = pl.program_id(0)

        @pl.when(n == 0)
        def _():
            ck[...] = jnp.zeros_like(ck)
            cv[...] = jnp.zeros_like(cv)
            dsk_ref[...] = jnp.zeros_like(dsk_ref)

        @pl.when(n < nb)
        def _():
            valid = _attn_mask(n)
            lane = lax.broadcasted_iota(jnp.int32, (1, 128), 1)
            dsk = jnp.zeros((1, 128), F32)
            for kvh in range(4):
                hs = slice(kvh * 64, (kvh + 1) * 64)
                qs, kb, probs, psink = _attn_probs(q_ref, kc_ref, kp_ref, sk_ref, kvh, valid)
                vb = jnp.concatenate([vp_ref[:, hs], vc_ref[:, hs]], axis=0).astype(BF16)
                heads = [slice((kvh * 4 + g) * 64, (kvh * 4 + g + 1) * 64) for g in range(4)]
                dos = jnp.concatenate([do_ref[:, hh] for hh in heads], axis=0)
                os_ = jnp.concatenate([o_ref[:, hh] for hh in heads], axis=0).astype(F32)
                delta = jnp.sum(dos * os_, axis=1, keepdims=True)
                dos16 = dos.astype(BF16)
                dp = _dot_nt(dos16, vb)
                ds = (probs * (dp - delta) * 0.125).astype(BF16)
                dqs = _dot(ds, kb)
                dkb = _dot_tn(ds, qs)
                dvb = _dot_tn(probs.astype(BF16), dos16)
                nkp[:, hs] = dkb[:BLK]
                nkc[:, hs] = dkb[BLK:]
                nvp[:, hs] = dvb[:BLK]
                nvc[:, hs] = dvb[BLK:]
                sd = psink * delta
                for g in range(4):
                    dq_ref[:, heads[g]] = dqs[g * BLK:(g + 1) * BLK].astype(BF16)
                    val = -jnp.sum(sd[g * BLK:(g + 1) * BLK], axis=0, keepdims=True)
                    dsk = dsk + jnp.where(lane == kvh * 4 + g, val, 0.0)
            dsk_ref[0:1, :] += dsk
            dk_ref[...] = (ck[...] + nkp[...]).astype(BF16)
            dv_ref[...] = (cv[...] + nvp[...]).astype(BF16)
            ck[...] = nkc[...]
            cv[...] = nvc[...]

        @pl.when(n == nb)
        def _():
            dk_ref[...] = ck[...].astype(BF16)
            dv_ref[...] = cv[...].astype(BF16)

    cur = lambda n: jnp.minimum(n, nb - 1)
    prev = lambda n: jnp.maximum(jnp.minimum(n, nb - 1) - 1, 0)
    outb = lambda n: jnp.maximum(n - 1, 0)
    kv_scr = pltpu.VMEM((BLK, KV_DIM), F32)
    return pl.pallas_call(
        body, name="attn_bwd", grid=(nb + 1,),
        in_specs=[pl.BlockSpec((BLK, Q_DIM), lambda n: (cur(n), 0)),
                  pl.BlockSpec((BLK, KV_DIM), lambda n: (cur(n), O_K // KV_DIM)),
                  pl.BlockSpec((BLK, KV_DIM), lambda n: (prev(n), O_K // KV_DIM)),
                  pl.BlockSpec((BLK, KV_DIM), lambda n: (cur(n), O_V // KV_DIM)),
                  pl.BlockSpec((BLK, KV_DIM), lambda n: (prev(n), O_V // KV_DIM)),
                  pl.BlockSpec((1, 128), lambda n: (0, 0)),
                  pl.BlockSpec((BLK, Q_DIM), lambda n: (cur(n), 0)),
                  pl.BlockSpec((BLK, Q_DIM), lambda n: (cur(n), 0))],
        out_specs=[pl.BlockSpec((BLK, Q_DIM), lambda n: (cur(n), 0)),
                   pl.BlockSpec((BLK, KV_DIM), lambda n: (outb(n), 0)),
                   pl.BlockSpec((BLK, KV_DIM), lambda n: (outb(n), 0)),
                   pl.BlockSpec((8, 128), lambda n: (0, 0))],
        out_shape=[jax.ShapeDtypeStruct((s, Q_DIM), BF16), jax.ShapeDtypeStruct((s, KV_DIM), BF16),
                   jax.ShapeDtypeStruct((s, KV_DIM), BF16), jax.ShapeDtypeStruct((8, 128), F32)],
        scratch_shapes=[kv_scr] * 6, compiler_params=_cp(("arbitrary",)),
    )(proj, proj, proj, proj, proj, sinks, o, do)


def _shift_down(x, j):
    if j == 0:
        return x
    row = lax.broadcasted_iota(jnp.int32, x.shape, 0)
    return jnp.where(row >= j, pltpu.roll(x, j, 0), 0.0)


def _shift_up(x, j):
    if j == 0:
        return x
    s = x.shape[0]
    row = lax.broadcasted_iota(jnp.int32, x.shape, 0)
    return jnp.where(row < s - j, pltpu.roll(x, s - j, 0), 0.0)


def _conv(x, w_ref, b_ref):
    kk = w_ref.shape[0]
    y = _shift_down(x, kk - 1) * w_ref[0:1, :]
    for q in range(1, kk):
        y = y + _shift_down(x, kk - 1 - q) * w_ref[q:q + 1, :]
    return y + b_ref[...]


def _conv_bwd(dy, x, w_ref, dx_dtype):
    kk = w_ref.shape[0]
    dx = _shift_up(dy, kk - 1) * w_ref[0:1, :]
    dws = [jnp.sum(dy * _shift_down(x, kk - 1), axis=0, keepdims=True)]
    for q in range(1, kk):
        dx = dx + _shift_up(dy, kk - 1 - q) * w_ref[q:q + 1, :]
        dws.append(jnp.sum(dy * _shift_down(x, kk - 1 - q), axis=0, keepdims=True))
    return dx.astype(dx_dtype), dws, jnp.sum(dy, axis=0, keepdims=True)


def _dsilu(y, sg):
    return sg * (1.0 + y * (1.0 - sg))


CT = 256


def _ssd_conv_fwd(proj, w, b):
    s = proj.shape[0]

    def body(x_ref, w_ref, b_ref, o_ref):
        y = _conv(x_ref[...], w_ref, b_ref)
        o_ref[...] = y * _sigmoid(y)

    return pl.pallas_call(
        body, name="ssd_conv_fwd", grid=(XBC_DIM // CT,),
        in_specs=[pl.BlockSpec((s, CT), lambda i: (0, O_X // CT + i)), pl.BlockSpec((4, CT), lambda i: (0, i)),
                  pl.BlockSpec((1, CT), lambda i: (0, i))],
        out_specs=pl.BlockSpec((s, CT), lambda i: (0, i)),
        out_shape=jax.ShapeDtypeStruct((s, XBC_DIM), F32), compiler_params=_cp(("parallel",)),
    )(proj, w, b)


def _ssd_conv_bwd(dact, proj, w, b):
    s = proj.shape[0]

    def body(d_ref, x_ref, w_ref, b_ref, dx_ref, dw_ref, db_ref):
        x = x_ref[...]
        y = _conv(x, w_ref, b_ref)
        dy = d_ref[...] * _dsilu(y, _sigmoid(y))
        dx, dws, db = _conv_bwd(dy, x, w_ref, BF16)
        dx_ref[...] = dx
        for q in range(4):
            dw_ref[q:q + 1, :] = dws[q]
        db_ref[...] = db

    return pl.pallas_call(
        body, name="ssd_conv_bwd", grid=(XBC_DIM // CT,),
        in_specs=[pl.BlockSpec((s, CT), lambda i: (0, i)), pl.BlockSpec((s, CT), lambda i: (0, O_X // CT + i)),
                  pl.BlockSpec((4, CT), lambda i: (0, i)), pl.BlockSpec((1, CT), lambda i: (0, i))],
        out_specs=[pl.BlockSpec((s, CT), lambda i: (0, i)), pl.BlockSpec((4, CT), lambda i: (0, i)),
                   pl.BlockSpec((1, CT), lambda i: (0, i))],
        out_shape=[jax.ShapeDtypeStruct((s, XBC_DIM), BF16), jax.ShapeDtypeStruct((4, XBC_DIM), F32),
                   jax.ShapeDtypeStruct((1, XBC_DIM), F32)],
        compiler_params=_cp(("parallel",)),
    )(dact, proj, w, b)


NFT = D_FF // CT


def _ffn_act_fwd(up, w, b):
    s = up.shape[0]

    def body(v_ref, g_ref, wv_ref, wg_ref, bv_ref, bg_ref, o_ref):
        val = _conv(v_ref[...], wv_ref, bv_ref)
        gt = _conv(g_ref[...], wg_ref, bg_ref)
        o_ref[...] = ((gt * _sigmoid(gt)) * val).astype(BF16)

    col = lambda off: (lambda i: (0, off + i))
    return pl.pallas_call(
        body, name="ffn_act_fwd", grid=(NFT,),
        in_specs=[pl.BlockSpec((s, CT), col(0)), pl.BlockSpec((s, CT), col(NFT)),
                  pl.BlockSpec((3, CT), col(0)), pl.BlockSpec((3, CT), col(NFT)),
                  pl.BlockSpec((1, CT), col(0)), pl.BlockSpec((1, CT), col(NFT))],
        out_specs=pl.BlockSpec((s, CT), col(0)),
        out_shape=jax.ShapeDtypeStruct((s, D_FF), BF16), compiler_params=_cp(("parallel",)),
    )(up, up, w, w, b, b)


def _ffn_act_bwd(dact, up, w, b):
    s = up.shape[0]

    def body(d_ref, v_ref, g_ref, wv_ref, wg_ref, bv_ref, bg_ref, dx_ref, dw_ref, db_ref):
        xv, xg = v_ref[...], g_ref[...]
        val = _conv(xv, wv_ref, bv_ref)
        gt = _conv(xg, wg_ref, bg_ref)
        sg = _sigmoid(gt)
        d = d_ref[...]
        for half, (dy, x, w_ref) in enumerate(((d * (gt * sg), xv, wv_ref), (d * val * _dsilu(gt, sg), xg, wg_ref))):
            dx, dws, db = _conv_bwd(dy, x, w_ref, BF16)
            dx_ref[half] = dx
            for q in range(3):
                dw_ref[half, q:q + 1, :] = dws[q]
            db_ref[half] = db

    col = lambda off: (lambda i: (0, off + i))
    both = lambda i: (0, 0, i)
    return pl.pallas_call(
        body, name="ffn_act_bwd", grid=(NFT,),
        in_specs=[pl.BlockSpec((s, CT), col(0)), pl.BlockSpec((s, CT), col(0)), pl.BlockSpec((s, CT), col(NFT)),
                  pl.BlockSpec((3, CT), col(0)), pl.BlockSpec((3, CT), col(NFT)),
                  pl.BlockSpec((1, CT), col(0)), pl.BlockSpec((1, CT), col(NFT))],
        out_specs=[pl.BlockSpec((2, s, CT), both), pl.BlockSpec((2, 3, CT), both), pl.BlockSpec((2, 1, CT), both)],
        out_shape=[jax.ShapeDtypeStruct((2, s, D_FF), BF16), jax.ShapeDtypeStruct((2, 3, D_FF), F32),
                   jax.ShapeDtypeStruct((2, 1, D_FF), F32)],
        compiler_params=_cp(("parallel",)),
    )(dact, up, up, w, w, b, b)


def _expand_mat():
    r = lax.broadcasted_iota(jnp.int32, (128, D_INNER), 0)
    c = lax.broadcasted_iota(jnp.int32, (128, D_INNER), 1)
    return ((c >> 6) == r).astype(BF16)


def _reduce_mat():
    r = lax.broadcasted_iota(jnp.int32, (D_INNER, 128), 0)
    c = lax.broadcasted_iota(jnp.int32, (D_INNER, 128), 1)
    return ((r >> 6) == c).astype(BF16)


def _split(v, parts):
    out = []
    for _ in range(parts - 1):
        p = v.astype(BF16)
        out.append(p)
        v = v - p.astype(F32)
    out.append(v.astype(BF16))
    return out


def _sel_dot(v, sel, parts):
    acc = None
    for p in reversed(_split(v, parts)):
        t = _dot(p, sel)
        acc = t if acc is None else acc + t
    return acc


def _row8(v):
    return jnp.broadcast_to(v, (8, v.shape[1]))


def _tril():
    r = lax.broadcasted_iota(jnp.int32, (BLK, BLK), 0)
    c = lax.broadcasted_iota(jnp.int32, (BLK, BLK), 1)
    return r >= c


def _softplus(x):
    return jnp.maximum(x, 0.0) + jnp.log(1.0 + jnp.exp(-jnp.abs(x)))


def _ssd_common(dtraw_ref, dtb_ref, alog_ref):
    causal = _tril()
    e_mat = _expand_mat()
    a_neg = -jnp.exp(alog_ref[...])
    dt = _softplus(dtraw_ref[...] + dtb_ref[...])
    a_cs = _dot(causal.astype(F32), dt * a_neg, HI)
    a_cs_t = a_cs.T
    dt_x = _sel_dot(dt, e_mat, 3)
    acs_x = _sel_dot(a_cs, e_mat, 3)
    alast_x = acs_x[BLK - 1:BLK, :]
    ea_x = jnp.exp(acs_x)
    ds_x = jnp.exp(alast_x - acs_x)
    elast_x = jnp.exp(alast_x)
    return causal, e_mat, a_neg, dt, a_cs, a_cs_t, dt_x, ea_x, ds_x, elast_x


def _decay(a_cs, a_cs_t, h, causal):
    seg = a_cs[:, h:h + 1] - a_cs_t[h:h + 1, :]
    return jnp.where(causal, jnp.exp(jnp.where(causal, seg, 0.0)), 0.0)


def _ssd_fwd(xbc, proj, dt_bias, a_log, d_skip):
    s = xbc.shape[0]
    nc = s // BLK

    def body(xs_ref, b_ref, c_ref, dtraw_ref, dtb_ref, alog_ref, dskip_ref, y_ref, hp_ref, h_scr, xc16):
        @pl.when(pl.program_id(0) == 0)
        def _():
            h_scr[...] = jnp.zeros_like(h_scr)

        causal, e_mat, _, _, a_cs, a_cs_t, dt_x, ea_x, ds_x, elast_x = _ssd_common(dtraw_ref, dtb_ref, alog_ref)
        dskip_x = _sel_dot(_row8(dskip_ref[...]), e_mat, 3)[0:1]
        xs = xs_ref[...]
        xc = xs * dt_x
        xc16[...] = xc.astype(BF16)
        xcd = (xc * ds_x).astype(BF16)
        hp_ref[0] = h_scr[...]
        for g in range(4):
            gs = slice(g * 512, (g + 1) * 512)
            cg = c_ref[:, g * 128:(g + 1) * 128].astype(BF16)
            bg = b_ref[:, g * 128:(g + 1) * 128].astype(BF16)
            cb = _dot_nt(cg, bg)
            hg = h_scr[:, gs]
            yoff = _dot(cg, hg.astype(BF16)) * ea_x[:, gs]
            for j in range(8):
                h = g * 8 + j
                hsl = slice(h * 64, (h + 1) * 64)
                mm = (cb * _decay(a_cs, a_cs_t, h, causal)).astype(BF16)
                y_ref[:, hsl] = _dot(mm, xc16[:, hsl])
            y_ref[:, gs] += yoff + xs[:, gs] * dskip_x[:, gs]
            h_scr[:, gs] = hg * elast_x[:, gs] + _dot_tn(bg, xcd[:, gs])

    vec = pl.BlockSpec((1, 128), lambda c: (0, 0))
    return pl.pallas_call(
        body, name="ssd_fwd", grid=(nc,),
        in_specs=[pl.BlockSpec((BLK, D_INNER), lambda c: (c, 0)),
                  pl.BlockSpec((BLK, BC_DIM), lambda c: (c, D_INNER // BC_DIM)),
                  pl.BlockSpec((BLK, BC_DIM), lambda c: (c, D_INNER // BC_DIM + 1)),
                  pl.BlockSpec((BLK, 128), lambda c: (c, O_DT // 128)), vec, vec, vec],
        out_specs=[pl.BlockSpec((BLK, D_INNER), lambda c: (c, 0)),
                   pl.BlockSpec((1, 128, D_INNER), lambda c: (c, 0, 0))],
        out_shape=[jax.ShapeDtypeStruct((s, D_INNER), F32), jax.ShapeDtypeStruct((nc, 128, D_INNER), F32)],
        scratch_shapes=[pltpu.VMEM((128, D_INNER), F32), pltpu.VMEM((BLK, D_INNER), BF16)],
        compiler_params=_cp(("arbitrary",)),
    )(xbc, xbc, xbc, proj, dt_bias, a_log, d_skip)


def _ssd_bwd(xbc, proj, dt_bias, a_log, d_skip, hprev, dy):
    s = xbc.shape[0]
    nc = s // BLK

    def body(xs_ref, b_ref, c_ref, dtraw_ref, dtb_ref, alog_ref, dskip_ref, hp_ref, dy_ref,
             dxbc_ref, ddt_ref, dvec_ref, dh_scr, xc16, dy16, dxc_scr, dacs_r, tdiff):
        step = pl.program_id(0)
        dacs_r[...] = jnp.zeros_like(dacs_r)

        @pl.when(step == 0)
        def _():
            dh_scr[...] = jnp.zeros_like(dh_scr)
            dvec_ref[...] = jnp.zeros_like(dvec_ref)

        causal, e_mat, a_neg, dt, a_cs, a_cs_t, dt_x, ea_x, ds_x, elast_x = _ssd_common(dtraw_ref, dtb_ref, alog_ref)
        r_mat = _reduce_mat()
        lane = lax.broadcasted_iota(jnp.int32, (1, 128), 1)
        dskip_x = _sel_dot(_row8(dskip_ref[...]), e_mat, 3)[0:1]
        xs = xs_ref[...]
        dy = dy_ref[...]
        xc = xs * dt_x
        xcd = xc * ds_x
        xc16[...] = xc.astype(BF16)
        dy16[...] = dy.astype(BF16)
        dyea = dy * ea_x
        dh = dh_scr[...]
        hp = hp_ref[0]
        dalast_x = jnp.sum(dh * hp, axis=0, keepdims=True) * elast_x
        dacs = jnp.zeros((BLK, 128), F32)
        for g in range(4):
            gs = slice(g * 512, (g + 1) * 512)
            bsl = slice(g * 128, (g + 1) * 128)
            cg = c_ref[:, bsl].astype(BF16)
            bg = b_ref[:, bsl].astype(BF16)
            cb = _dot_nt(cg, bg)
            hg16 = hp[:, gs].astype(BF16)
            dhg16 = dh[:, gs].astype(BF16)
            raw = _dot(cg, hg16)
            draw16 = dyea[:, gs].astype(BF16)
            dcg = _dot_nt(draw16, hg16)
            dhp_g = _dot_tn(cg, draw16)
            dbg = _dot_nt(xcd[:, gs].astype(BF16), dhg16)
            dxcd = _dot(bg, dhg16)
            dcb = jnp.zeros((BLK, BLK), F32)
            for j in range(8):
                h = g * 8 + j
                hsl = slice(h * 64, (h + 1) * 64)
                decay = _decay(a_cs, a_cs_t, h, causal)
                m = cb * decay
                dm = _dot_nt(dy16[:, hsl], xc16[:, hsl])
                dxc_scr[:, hsl] = _dot_tn(m.astype(BF16), dy16[:, hsl])
                dcb = dcb + dm * decay
                dseg = dm * m
                oneh = jnp.where(lane == h, 1.0, 0.0)
                dacs = dacs + jnp.sum(dseg, axis=1, keepdims=True) * oneh
                dacs_r[h:h + 1, :] = jnp.sum(dseg, axis=0, keepdims=True)
            dcb16 = dcb.astype(BF16)
            dcg = dcg + _dot(dcb16, bg)
            dbg = dbg + _dot_tn(dcb16, cg)
            dxbc_ref[:, D_INNER + g * 128:D_INNER + (g + 1) * 128] = dbg
            dxbc_ref[:, D_INNER + BC_DIM + g * 128:D_INNER + BC_DIM + (g + 1) * 128] = dcg
            dxc_scr[:, gs] += dxcd * ds_x[:, gs]
            dh_scr[:, gs] = dh[:, gs] * elast_x[:, gs] + dhp_g
            tst = dxcd * xcd[:, gs]
            tdiff[:, gs] = dy[:, gs] * (raw * ea_x[:, gs]) - tst
            tdiff[BLK - 1:BLK, gs] += jnp.sum(tst, axis=0, keepdims=True)
        dxc = dxc_scr[...]
        row = lax.broadcasted_iota(jnp.int32, (BLK, D_INNER), 0)
        tfull = tdiff[...] + jnp.where(row == BLK - 1, dalast_x, 0.0)
        dacs = dacs + _sel_dot(tfull, r_mat, 2) - dacs_r[...].T
        da = _dot_tn(causal.astype(F32), dacs, HI)
        ddt = da * a_neg + _sel_dot(dxc * xs, r_mat, 2)
        lmask = lax.broadcasted_iota(jnp.int32, (BLK, 128), 1) < N_SSD_HEADS
        ddtraw = jnp.where(lmask, ddt * _sigmoid(dtraw_ref[...] + dtb_ref[...]), 0.0)
        ddt_ref[...] = ddtraw.astype(BF16)
        dxbc_ref[:, 0:D_INNER] = dy * dskip_x + dxc * dt_x
        dvec_ref[0:1, :] += jnp.sum(ddtraw, axis=0, keepdims=True)
        dvec_ref[1:2, :] += jnp.where(lane < N_SSD_HEADS, jnp.sum(da * dt, axis=0, keepdims=True) * a_neg, 0.0)
        dvec_ref[2:3, :] += _sel_dot(_row8(jnp.sum(dy * xs, axis=0, keepdims=True)), r_mat, 3)[0:1]

    rev = lambda c: nc - 1 - c
    vec = pl.BlockSpec((1, 128), lambda c: (0, 0))
    return pl.pallas_call(
        body, name="ssd_bwd", grid=(nc,),
        in_specs=[pl.BlockSpec((BLK, D_INNER), lambda c: (rev(c), 0)),
                  pl.BlockSpec((BLK, BC_DIM), lambda c: (rev(c), D_INNER // BC_DIM)),
                  pl.BlockSpec((BLK, BC_DIM), lambda c: (rev(c), D_INNER // BC_DIM + 1)),
                  pl.BlockSpec((BLK, 128), lambda c: (rev(c), O_DT // 128)), vec, vec, vec,
                  pl.BlockSpec((1, 128, D_INNER), lambda c: (rev(c), 0, 0)),
                  pl.BlockSpec((BLK, D_INNER), lambda c: (rev(c), 0))],
        out_specs=[pl.BlockSpec((BLK, XBC_DIM), lambda c: (rev(c), 0)),
                   pl.BlockSpec((BLK, 128), lambda c: (rev(c), 0)),
                   pl.BlockSpec((8, 128), lambda c: (0, 0))],
        out_shape=[jax.ShapeDtypeStruct((s, XBC_DIM), F32), jax.ShapeDtypeStruct((s, 128), BF16),
                   jax.ShapeDtypeStruct((8, 128), F32)],
        scratch_shapes=[pltpu.VMEM((128, D_INNER), F32), pltpu.VMEM((BLK, D_INNER), BF16),
                        pltpu.VMEM((BLK, D_INNER), BF16), pltpu.VMEM((BLK, D_INNER), F32),
                        pltpu.VMEM((128, BLK), F32), pltpu.VMEM((BLK, D_INNER), F32)],
        compiler_params=_cp(("arbitrary",)),
    )(xbc, xbc, xbc, proj, dt_bias, a_log, d_skip, hprev, dy)


GW = 512


def _gate_norm_fwd(y, proj, wn, *, tm=512):
    s = y.shape[0]
    tm = _tile(s, tm)

    def body(y_ref, z_ref, w_ref, o_ref):
        z = z_ref[...]
        y2 = y_ref[...] * (z * _sigmoid(z))
        r = lax.rsqrt(jnp.mean(y2 * y2, axis=-1, keepdims=True) + EPS)
        o_ref[...] = ((y2 * r) * w_ref[...]).astype(BF16)

    return pl.pallas_call(
        body, name="gate_norm_fwd", grid=(s // tm, 4),
        in_specs=[pl.BlockSpec((tm, GW), lambda i, g: (i, g)), pl.BlockSpec((tm, GW), lambda i, g: (i, O_Z // GW + g)),
                  pl.BlockSpec((1, GW), lambda i, g: (0, g))],
        out_specs=pl.BlockSpec((tm, GW), lambda i, g: (i, g)),
        out_shape=jax.ShapeDtypeStruct((s, D_INNER), BF16), compiler_params=_cp(("parallel", "parallel")),
    )(y, proj, wn)


def _gate_norm_bwd(dyn, y, proj, wn, *, tm=512):
    s = y.shape[0]
    tm = _tile(s, tm)

    def body(d_ref, y_ref, z_ref, w_ref, dy_ref, dz_ref, dw_ref):
        i = pl.program_id(1)
        z = z_ref[...]
        sg = _sigmoid(z)
        sz = z * sg
        yv = y_ref[...]
        y2 = yv * sz
        r = lax.rsqrt(jnp.mean(y2 * y2, axis=-1, keepdims=True) + EPS)
        xh = y2 * r
        dv = d_ref[...]
        g = dv * w_ref[...]
        dy2 = r * (g - xh * jnp.mean(g * xh, axis=-1, keepdims=True))
        dy_ref[...] = dy2 * sz
        dz_ref[...] = (dy2 * yv * _dsilu(z, sg)).astype(BF16)
        part = jnp.sum(dv * xh, axis=0, keepdims=True)

        @pl.when(i == 0)
        def _():
            dw_ref[...] = part

        @pl.when(i > 0)
        def _():
            dw_ref[...] += part

    blk = pl.BlockSpec((tm, GW), lambda g, i: (i, g))
    vec = pl.BlockSpec((1, GW), lambda g, i: (0, g))
    return pl.pallas_call(
        body, name="gate_norm_bwd", grid=(4, s // tm),
        in_specs=[blk, blk, pl.BlockSpec((tm, GW), lambda g, i: (i, O_Z // GW + g)), vec],
        out_specs=[blk, blk, vec],
        out_shape=[jax.ShapeDtypeStruct((s, D_INNER), F32), jax.ShapeDtypeStruct((s, D_INNER), BF16),
                   jax.ShapeDtypeStruct((1, D_INNER), F32)],
        compiler_params=_cp(("parallel", "arbitrary")),
    )(dyn, y, proj, wn)


def _merge_fwd(proj, b_gate, attn, ssd_out, *, tm=512):
    s = attn.shape[0]
    tm = _tile(s, tm)

    def body(ga_ref, gs_ref, ba_ref, bs_ref, a_ref, s_ref, o_ref):
        ga = _sigmoid(ga_ref[...] + ba_ref[...])
        gs = _sigmoid(gs_ref[...] + bs_ref[...])
        o_ref[...] = (ga * a_ref[...] + gs * s_ref[...]).astype(BF16)

    blk = pl.BlockSpec((tm, GW), lambda i, j: (i, j))
    return pl.pallas_call(
        body, name="merge_fwd", grid=(s // tm, 2),
        in_specs=[pl.BlockSpec((tm, GW), lambda i, j: (i, O_GA // GW + j)),
                  pl.BlockSpec((tm, GW), lambda i, j: (i, O_GS // GW + j)),
                  pl.BlockSpec((1, GW), lambda i, j: (0, j)), pl.BlockSpec((1, GW), lambda i, j: (0, 2 + j)), blk, blk],
        out_specs=blk, out_shape=jax.ShapeDtypeStruct((s, D_MODEL), BF16),
        compiler_params=_cp(("parallel", "parallel")),
    )(proj, proj, b_gate, b_gate, attn, ssd_out)


def _merge_bwd(dm, proj, b_gate, attn, ssd_out, *, tm=512):
    s = attn.shape[0]
    tm = _tile(s, tm)

    def body(d_ref, ga_ref, gs_ref, ba_ref, bs_ref, a_ref, s_ref, da_ref, ds_ref, dga_ref, dgs_ref, dba_ref, dbs_ref):
        i = pl.program_id(1)
        ga = _sigmoid(ga_ref[...] + ba_ref[...])
        gs = _sigmoid(gs_ref[...] + bs_ref[...])
        d = d_ref[...]
        da_ref[...] = (d * ga).astype(BF16)
        ds_ref[...] = (d * gs).astype(BF16)
        dga = d * a_ref[...] * (ga * (1.0 - ga))
        dgs = d * s_ref[...] * (gs * (1.0 - gs))
        dga_ref[...] = dga.astype(BF16)
        dgs_ref[...] = dgs.astype(BF16)
        pa = jnp.sum(dga, axis=0, keepdims=True)
        ps = jnp.sum(dgs, axis=0, keepdims=True)

        @pl.when(i == 0)
        def _():
            dba_ref[...] = pa
            dbs_ref[...] = ps

        @pl.when(i > 0)
        def _():
            dba_ref[...] += pa
            dbs_ref[...] += ps

    blk = pl.BlockSpec((tm, GW), lambda j, i: (i, j))
    vec = pl.BlockSpec((1, GW), lambda j, i: (0, j))
    sd = jax.ShapeDtypeStruct((s, D_MODEL), BF16)
    vd = jax.ShapeDtypeStruct((1, D_MODEL), F32)
    return pl.pallas_call(
        body, name="merge_bwd", grid=(2, s // tm),
        in_specs=[blk, pl.BlockSpec((tm, GW), lambda j, i: (i, O_GA // GW + j)),
                  pl.BlockSpec((tm, GW), lambda j, i: (i, O_GS // GW + j)),
                  vec, pl.BlockSpec((1, GW), lambda j, i: (0, 2 + j)), blk, blk],
        out_specs=[blk, blk, blk, blk, vec, vec], out_shape=[sd, sd, sd, sd, vd, vd],
        compiler_params=_cp(("parallel", "arbitrary")),
    )(dm, proj, proj, b_gate, b_gate, attn, ssd_out)


def _adamw(w, g, m, v, *, name, tm=128):
    r, c = w.shape
    tm = r if (r < tm or r % tm) else tm

    def body(w_ref, g_ref, m_ref, v_ref, d_ref, nm_ref, nv_ref):
        gv = g_ref[...]
        mn = ADAM_B1 * m_ref[...] + (1.0 - ADAM_B1) * gv
        vn = ADAM_B2 * v_ref[...] + (1.0 - ADAM_B2) * (gv * gv)
        m_hat = mn / (1.0 - ADAM_B1 ** ADAM_STEP)
        v_hat = vn / (1.0 - ADAM_B2 ** ADAM_STEP)
        d_ref[...] = -ADAM_LR * (m_hat / (jnp.sqrt(v_hat) + ADAM_EPS) + ADAM_WD * w_ref[...])
        nm_ref[...] = mn
        nv_ref[...] = vn

    blk = pl.BlockSpec((tm, c), lambda i: (i, 0))
    sd = jax.ShapeDtypeStruct((r, c), F32)
    return pl.pallas_call(
        body, name=name, grid=(r // tm,), in_specs=[blk] * 4, out_specs=[blk] * 3, out_shape=[sd] * 3,
        compiler_params=_cp(("parallel",)),
    )(w, g, m, v)


ANY = pl.BlockSpec(memory_space=pl.ANY)
N_CHIPS = 4


def _chip_of(k, x, y):
    return (x ^ (k >> 1), y ^ (k & 1))


def _all_gather_small(shard):
    r, c = shard.shape
    hr = r // 2

    def body(sh_ref, out_ref, send_sems, recv_sems, local_sem):
        x, y, cc = lax.axis_index("x"), lax.axis_index("y"), lax.axis_index("c")

        def half(px, py, pc):
            return out_ref.at[2 * px + py, pl.ds(pc * hr, hr), :]

        def copy(k, px, py, pc, to, src=None):
            return pltpu.make_async_remote_copy(
                src_ref=half(px, py, pc) if src is None else src, dst_ref=half(px, py, pc),
                send_sem=send_sems.at[k], recv_sem=recv_sems.at[k], device_id=to, device_id_type=MESH)

        mine = pltpu.make_async_copy(sh_ref, out_ref.at[2 * x + y], local_sem)
        mine.start()
        chips = [_chip_of(k, x, y) for k in (1, 2, 3)]
        first = [copy(j, x, y, cc, (*chip, cc), src=sh_ref.at[pl.ds(cc * hr, hr), :]) for j, chip in enumerate(chips)]
        for cp in first:
            cp.start()
        passed = [copy(3 + j, *chip, cc, (x, y, 1 - cc)) for j, chip in enumerate(chips)]
        for j, chip in enumerate(chips):
            copy(j, *chip, cc, (x, y, cc)).wait_recv()
            passed[j].start()
        for j, chip in enumerate(chips):
            copy(3 + j, *chip, 1 - cc, (x, y, cc)).wait_recv()
        for cp in first + passed:
            cp.wait_send()
        mine.wait()

    return pl.pallas_call(
        body, name="all_gather_small", in_specs=[ANY], out_specs=ANY,
        out_shape=jax.ShapeDtypeStruct((N_CHIPS, r, c), shard.dtype),
        scratch_shapes=[pltpu.SemaphoreType.DMA((6,)), pltpu.SemaphoreType.DMA((6,)), pltpu.SemaphoreType.DMA],
    )(shard)


def _cast_bf16(a, *, name, tm=512):
    n, r, c = a.shape
    tm = _tile(r, tm) if r % 128 == 0 else r

    def body(a_ref, o_ref):
        o_ref[...] = a_ref[...].astype(BF16)

    blk = pl.BlockSpec((1, tm, c), lambda i, j: (i, j, 0))
    return pl.pallas_call(body, name=name, grid=(n, r // tm), in_specs=[blk], out_specs=blk,
                          out_shape=jax.ShapeDtypeStruct(a.shape, BF16), compiler_params=_cp(("parallel", "parallel")))(a)


def _pair_exchange(g16, hr):
    n, r, c = g16.shape

    def body(g_ref, out_ref, send_sem, recv_sem):
        x, y, cc = lax.axis_index("x"), lax.axis_index("y"), lax.axis_index("c")
        cp = pltpu.make_async_remote_copy(
            src_ref=g_ref.at[:, pl.ds((1 - cc) * hr, hr), :], dst_ref=out_ref, send_sem=send_sem, recv_sem=recv_sem,
            device_id=(x, y, 1 - cc), device_id_type=MESH)
        cp.start()
        cp.wait()

    return pl.pallas_call(
        body, name="grad_pair_exchange", in_specs=[ANY], out_specs=ANY,
        out_shape=jax.ShapeDtypeStruct((n, hr, c), g16.dtype),
        scratch_shapes=[pltpu.SemaphoreType.DMA, pltpu.SemaphoreType.DMA],
    )(g16)


def _pair_add(g, recv, half_idx, hr, *, tm=384):
    n, r, c = g.shape
    nt = hr // tm

    def body(hi_ref, g_ref, r_ref, o32_ref, o16_ref):
        v = g_ref[...] + r_ref[...].astype(F32)
        o32_ref[...] = v
        o16_ref[...] = v.astype(BF16)

    gs = pltpu.PrefetchScalarGridSpec(
        num_scalar_prefetch=1, grid=(n, nt),
        in_specs=[pl.BlockSpec((1, tm, c), lambda i, j, hi: (i, hi[0] * nt + j, 0)),
                  pl.BlockSpec((1, tm, c), lambda i, j, hi: (i, j, 0))],
        out_specs=[pl.BlockSpec((1, tm, c), lambda i, j, hi: (i, j, 0))] * 2)
    return pl.pallas_call(
        body, name="grad_pair_add", grid_spec=gs,
        out_shape=[jax.ShapeDtypeStruct((n, hr, c), F32), jax.ShapeDtypeStruct((n, hr, c), BF16)],
        compiler_params=_cp(("parallel", "parallel")),
    )(half_idx, g, recv)


def _chip_exchange(p16):
    n, hr, c = p16.shape

    def body(p_ref, out_ref, send_sems, recv_sems):
        x, y, cc = lax.axis_index("x"), lax.axis_index("y"), lax.axis_index("c")
        cps = []
        for j, k in enumerate((1, 2, 3)):
            px, py = _chip_of(k, x, y)
            cps.append(pltpu.make_async_remote_copy(
                src_ref=p_ref.at[2 * px + py], dst_ref=out_ref.at[j], send_sem=send_sems.at[j], recv_sem=recv_sems.at[j],
                device_id=(px, py, cc), device_id_type=MESH))
        for cp in cps:
            cp.start()
        for cp in cps:
            cp.wait()

    return pl.pallas_call(
        body, name="grad_chip_exchange", in_specs=[ANY], out_specs=ANY,
        out_shape=jax.ShapeDtypeStruct((3, hr, c), p16.dtype),
        scratch_shapes=[pltpu.SemaphoreType.DMA((3,)), pltpu.SemaphoreType.DMA((3,))],
    )(p16)


def _chip_add(p32, recv, chip_idx, *, tm=384):
    n, hr, c = p32.shape

    def body(ci_ref, p_ref, r_ref, o_ref):
        o_ref[...] = ((p_ref[0] + r_ref[0].astype(F32)) + r_ref[1].astype(F32)) + r_ref[2].astype(F32)

    gs = pltpu.PrefetchScalarGridSpec(
        num_scalar_prefetch=1, grid=(hr // tm,),
        in_specs=[pl.BlockSpec((1, tm, c), lambda j, ci: (ci[0], j, 0)), pl.BlockSpec((3, tm, c), lambda j, ci: (0, j, 0))],
        out_specs=pl.BlockSpec((tm, c), lambda j, ci: (j, 0)))
    return pl.pallas_call(
        body, name="grad_chip_add", grid_spec=gs, out_shape=jax.ShapeDtypeStruct((hr, c), F32),
        compiler_params=_cp(("parallel",)),
    )(chip_idx, p32, recv)


def _pair_gather(f):
    hr, c = f.shape

    def body(f_ref, out_ref, send_sem, recv_sem, local_sem):
        x, y, cc = lax.axis_index("x"), lax.axis_index("y"), lax.axis_index("c")
        mine = pltpu.make_async_copy(f_ref, out_ref.at[pl.ds(cc * hr, hr), :], local_sem)
        mine.start()
        cp = pltpu.make_async_remote_copy(
            src_ref=f_ref, dst_ref=out_ref.at[pl.ds(cc * hr, hr), :], send_sem=send_sem, recv_sem=recv_sem,
            device_id=(x, y, 1 - cc), device_id_type=MESH)
        cp.start()
        cp.wait()
        mine.wait()

    return pl.pallas_call(
        body, name="grad_pair_gather", in_specs=[ANY], out_specs=ANY,
        out_shape=jax.ShapeDtypeStruct((2 * hr, c), f.dtype),
        scratch_shapes=[pltpu.SemaphoreType.DMA, pltpu.SemaphoreType.DMA, pltpu.SemaphoreType.DMA],
    )(f)


def _all_reduce_small(buf):
    r, c = buf.shape

    def body(b_ref, out_ref, gat, send_sems, recv_sems):
        x, y, cc = lax.axis_index("x"), lax.axis_index("y"), lax.axis_index("c")
        me = 4 * x + 2 * y + cc
        gat[me] = b_ref[...]
        cps = []
        for k in range(1, 8):
            px, py, pc = x ^ (k >> 2), y ^ ((k >> 1) & 1), cc ^ (k & 1)
            cps.append(pltpu.make_async_remote_copy(
                src_ref=b_ref, dst_ref=gat.at[me], send_sem=send_sems.at[k - 1], recv_sem=recv_sems.at[k - 1],
                device_id=(px, py, pc), device_id_type=MESH))
        for cp in cps:
            cp.start()
        for cp in cps:
            cp.wait()
        acc = gat[0]
        for d in range(1, 8):
            acc = acc + gat[d]
        out_ref[...] = acc

    vm = pl.BlockSpec(memory_space=pltpu.VMEM)
    return pl.pallas_call(
        body, name="all_reduce_small", in_specs=[vm], out_specs=vm, out_shape=jax.ShapeDtypeStruct((r, c), F32),
        scratch_shapes=[pltpu.VMEM((8, r, c), F32), pltpu.SemaphoreType.DMA((7,)), pltpu.SemaphoreType.DMA((7,))],
        compiler_params=pltpu.CompilerParams(vmem_limit_bytes=VMEM_LIMIT),
    )(buf)


def _pipe(fn, ins, outs, tr):
    shape = ins[0].shape
    lead, (r, c) = shape[:-2], shape[-2:]
    assert len(lead) <= 1 and r % tr == 0
    nr = r // tr
    n = nr * (lead[0] if lead else 1)
    ni, no = len(ins), len(outs)

    def blk(ref, step):
        rows = pl.ds(pl.multiple_of((step % nr) * tr, tr), tr)
        return ref.at[step // nr, rows, :] if lead else ref.at[rows, :]

    def scoped(*bufs):
        ibufs, obufs, isem, osem = bufs[:ni], bufs[ni:ni + no], bufs[-2], bufs[-1]

        def in_copy(q, step, slot):
            return pltpu.make_async_copy(blk(ins[q], step), ibufs[q].at[slot], isem.at[q, slot])

        def out_copy(q, step, slot):
            return pltpu.make_async_copy(obufs[q].at[slot], blk(outs[q], step), osem.at[q, slot])

        for q in range(ni):
            in_copy(q, 0, 0).start()

        def body(step, carry):
            slot = step % 2

            @pl.when(step + 1 < n)
            def _():
                for q in range(ni):
                    in_copy(q, step + 1, 1 - slot).start()

            for q in range(ni):
                in_copy(q, step, slot).wait()

            @pl.when(step >= 2)
            def _():
                for q in range(no):
                    out_copy(q, step - 2, slot).wait()

            res = fn(*[ibufs[q][slot] for q in range(ni)])
            for q in range(no):
                obufs[q][slot] = res[q].astype(obufs[q].dtype)
                out_copy(q, step, slot).start()
            return carry

        lax.fori_loop(0, n, body, 0)
        for step in range(max(n - 2, 0), n):
            for q in range(no):
                out_copy(q, step, step % 2).wait()

    pl.run_scoped(scoped, *[pltpu.VMEM((2, tr, c), q.dtype) for q in ins], *[pltpu.VMEM((2, tr, c), q.dtype) for q in outs],
                  pltpu.SemaphoreType.DMA((ni, 2)), pltpu.SemaphoreType.DMA((no, 2)))


W_IN_PAD = 2304
BIG = ("w_in", "w_attn_o", "w_ssd_o", "w_out", "w_up", "w_down")
BIG_SHAPE = dict(w_in=(D_MODEL, W_IN_PAD), w_attn_o=(Q_DIM // 4, D_MODEL), w_ssd_o=(D_INNER // 4, D_MODEL),
                 w_out=(D_MODEL // 4, D_MODEL), w_up=(D_MODEL, 2 * D_FF // 4), w_down=(D_FF // 4, D_MODEL))
BIG_TR = dict(w_in=128, w_attn_o=128, w_ssd_o=128, w_out=128, w_up=128, w_down=176)
X_FIRST = dict(w_in=True, w_attn_o=True, w_ssd_o=False, w_out=True, w_up=False, w_down=False)


def _neighbours(x, y, x_first):
    xn, yn = (1 - x, y), (x, 1 - y)
    n1, n2 = (xn, yn) if x_first else (yn, xn)
    slot = lambda ch: 2 * ch[0] + ch[1]
    return n1, n2, slot(n1), slot(n2), slot((1 - x, 1 - y))


def _gather_big(shards):
    nt = len(BIG)

    def body(*refs):
        sh, out = refs[:nt], refs[nt:2 * nt]
        send_sems, recv_sems = refs[2 * nt:]
        x, y, cc = lax.axis_index("x"), lax.axis_index("y"), lax.axis_index("c")
        me = 2 * x + y
        sib = (x, y, 1 - cc)
        for t, n in enumerate(BIG):
            _pipe(lambda v: (v,), [sh[t]], [out[t].at[me]], BIG_TR[n])

        def copy(t, k, slot, pc, to):
            hr = BIG_SHAPE[BIG[t]][0] // 2
            ref = out[t].at[slot, pl.ds(pc * hr, hr), :]
            return pltpu.make_async_remote_copy(src_ref=ref, dst_ref=ref, send_sem=send_sems.at[6 * t + k],
                                                recv_sem=recv_sems.at[6 * t + k], device_id=to, device_id_type=MESH)

        started = []

        def start(cp):
            cp.start()
            started.append(cp)

        geo = [_neighbours(x, y, X_FIRST[n]) for n in BIG]
        for t in range(nt):
            n1, n2, _, _, _ = geo[t]
            start(copy(t, 0, me, cc, (*n1, cc)))
            start(copy(t, 1, me, cc, (*n2, cc)))
        for t in range(nt):
            n1, n2, s1, s2, sd = geo[t]
            copy(t, 0, s1, cc, sib).wait_recv()
            start(copy(t, 2, s1, cc, (*n2, cc)))
            start(copy(t, 3, s1, cc, sib))
            copy(t, 1, s2, cc, sib).wait_recv()
            start(copy(t, 4, s2, cc, sib))
        for t in range(nt):
            _, _, s1, s2, sd = geo[t]
            copy(t, 2, sd, cc, sib).wait_recv()
            start(copy(t, 5, sd, cc, sib))
        for t in range(nt):
            _, _, s1, s2, sd = geo[t]
            copy(t, 3, s1, 1 - cc, sib).wait_recv()
            copy(t, 4, s2, 1 - cc, sib).wait_recv()
            copy(t, 5, sd, 1 - cc, sib).wait_recv()
        for cp in started:
            cp.wait_send()

    return pl.pallas_call(
        body, name="gather_big", in_specs=[ANY] * nt, out_specs=[ANY] * nt,
        out_shape=[jax.ShapeDtypeStruct((N_CHIPS, *BIG_SHAPE[n]), BF16) for n in BIG],
        scratch_shapes=[pltpu.SemaphoreType.DMA((6 * nt,)), pltpu.SemaphoreType.DMA((6 * nt,))],
        compiler_params=pltpu.CompilerParams(vmem_limit_bytes=VMEM_LIMIT),
    )(*shards)


def _reduce_big(grads):
    nt = len(BIG)
    nw = 7

    def body(*refs):
        g = refs[:nt]
        fin = refs[nt:2 * nt]
        work = refs[2 * nt:2 * nt + nw * nt]
        send_sems, recv_sems = refs[2 * nt + nw * nt:]
        x, y, cc = lax.axis_index("x"), lax.axis_index("y"), lax.axis_index("c")
        me = 2 * x + y
        sib = (x, y, 1 - cc)
        started = []

        def rcopy(t, k, src, dst, to):
            cp = pltpu.make_async_remote_copy(src_ref=src, dst_ref=dst, send_sem=send_sems.at[5 * t + k],
                                              recv_sem=recv_sems.at[5 * t + k], device_id=to, device_id_type=MESH)
            return cp

        def start(cp):
            cp.start()
            started.append(cp)

        geo = [_neighbours(x, y, X_FIRST[n]) for n in BIG]
        hrs = [BIG_SHAPE[n][0] // 2 for n in BIG]
        wk = lambda t: work[nw * t:nw * (t + 1)]
        one = lambda ref, slot: ref.at[pl.ds(slot, 1)]
        for t in range(nt):
            recv_a = wk(t)[0]
            start(rcopy(t, 0, g[t].at[:, pl.ds((1 - cc) * hrs[t], hrs[t]), :], recv_a, sib))
        for t, n in enumerate(BIG):
            recv_a, p32, p16, r1, qme, qs2, r2 = wk(t)
            n1, n2, s1, s2, sd = geo[t]
            rcopy(t, 0, recv_a, recv_a, sib).wait_recv()
            _pipe(lambda a, b: (a + b, a + b), [g[t].at[:, pl.ds(cc * hrs[t], hrs[t]), :], recv_a], [p32, p16], BIG_TR[n])
            start(rcopy(t, 1, one(p16, s1), one(r1, 0), (*n1, cc)))
            start(rcopy(t, 2, one(p16, sd), one(r1, 1), (*n1, cc)))
        for t, n in enumerate(BIG):
            recv_a, p32, p16, r1, qme, qs2, r2 = wk(t)
            n1, n2, s1, s2, sd = geo[t]
            rcopy(t, 1, one(r1, 0), one(r1, 0), sib).wait_recv()
            rcopy(t, 2, one(r1, 1), one(r1, 1), sib).wait_recv()
            _pipe(lambda a, b: (a + b.astype(F32),), [one(p32, s2), one(r1, 1)], [qs2], BIG_TR[n])
            start(rcopy(t, 3, qs2, r2, (*n2, cc)))
            _pipe(lambda a, b: (a + b.astype(F32),), [one(p32, me), one(r1, 0)], [qme], BIG_TR[n])
        for t, n in enumerate(BIG):
            recv_a, p32, p16, r1, qme, qs2, r2 = wk(t)
            rcopy(t, 3, r2, r2, sib).wait_recv()
            mine = fin[t].at[pl.ds(cc * hrs[t], hrs[t]), :]
            _pipe(lambda a, b: (a + b.astype(F32),), [qme.at[0], r2.at[0]], [mine], BIG_TR[n])
            start(rcopy(t, 4, mine, mine, sib))
        for t in range(nt):
            other = fin[t].at[pl.ds((1 - cc) * hrs[t], hrs[t]), :]
            rcopy(t, 4, other, other, sib).wait_recv()
        for cp in started:
            cp.wait_send()

    outs = [jax.ShapeDtypeStruct(BIG_SHAPE[n], F32) for n in BIG]
    for n in BIG:
        r, c = BIG_SHAPE[n]
        hr = r // 2
        outs += [jax.ShapeDtypeStruct((4, hr, c), F32), jax.ShapeDtypeStruct((4, hr, c), F32),
                 jax.ShapeDtypeStruct((4, hr, c), BF16), jax.ShapeDtypeStruct((2, hr, c), BF16),
                 jax.ShapeDtypeStruct((1, hr, c), F32), jax.ShapeDtypeStruct((1, hr, c), BF16),
                 jax.ShapeDtypeStruct((1, hr, c), BF16)]
    res = pl.pallas_call(
        body, name="reduce_big", in_specs=[ANY] * nt, out_specs=[ANY] * len(outs), out_shape=outs,
        scratch_shapes=[pltpu.SemaphoreType.DMA((5 * nt,)), pltpu.SemaphoreType.DMA((5 * nt,))],
        compiler_params=pltpu.CompilerParams(vmem_limit_bytes=VMEM_LIMIT),
    )(*grads)
    return res[:nt]


def _proj_dw(xn, dproj_sh, *, tm=512, tk=1024):
    s, d = xn.shape
    tk = _tile(s, tk)
    nk = s // tk

    def body(a_ref, b_ref, o_ref, acc):
        kk = pl.program_id(2)
        part = _dot_tn(a_ref[...], b_ref[0])

        @pl.when(kk == 0)
        def _():
            acc[...] = part

        @pl.when(kk > 0)
        def _():
            acc[...] += part

        @pl.when(kk == nk - 1)
        def _():
            o_ref[0] = acc[...]

    return pl.pallas_call(
        body, name="proj_dw", grid=(N_CHIPS, d // tm, nk),
        in_specs=[pl.BlockSpec((tk, tm), lambda j, i, q: (q, i)), pl.BlockSpec((1, tk, W_IN_PAD), lambda j, i, q: (j, q, 0))],
        out_specs=pl.BlockSpec((1, tm, W_IN_PAD), lambda j, i, q: (j, i, 0)),
        out_shape=jax.ShapeDtypeStruct((N_CHIPS, d, W_IN_PAD), F32), scratch_shapes=[pltpu.VMEM((tm, W_IN_PAD), F32)],
        compiler_params=_cp(("parallel", "parallel", "arbitrary")),
    )(xn, dproj_sh)


def _proj_dx(dproj_sh, w_sh, *, tm=1024):
    s = dproj_sh.shape[1]
    d = w_sh.shape[1]
    tm = _tile(s, tm)

    def body(a_ref, b_ref, o_ref, acc):
        kk = pl.program_id(1)
        part = _dot_nt(a_ref[0], b_ref[0])

        @pl.when(kk == 0)
        def _():
            acc[...] = part

        @pl.when(kk > 0)
        def _():
            acc[...] += part

        @pl.when(kk == N_CHIPS - 1)
        def _():
            o_ref[...] = acc[...]

    return pl.pallas_call(
        body, name="proj_dx", grid=(s // tm, N_CHIPS),
        in_specs=[pl.BlockSpec((1, tm, W_IN_PAD), lambda i, q: (q, i, 0)), pl.BlockSpec((1, d, W_IN_PAD), lambda i, q: (q, 0, 0))],
        out_specs=pl.BlockSpec((tm, d), lambda i, q: (i, 0)),
        out_shape=jax.ShapeDtypeStruct((s, d), F32), scratch_shapes=[pltpu.VMEM((tm, d), F32)],
        compiler_params=_cp(("parallel", "arbitrary")),
    )(dproj_sh, w_sh)


def _up_dx(dup, w_sh, *, tm=1024):
    s = dup.shape[1]
    d, wsh = w_sh.shape[1:]
    tm = _tile(s, tm)

    def body(a_ref, b_ref, o_ref, acc):
        kk = pl.program_id(1)
        part = _dot_nt(a_ref[0], b_ref[0])

        @pl.when(kk == 0)
        def _():
            acc[...] = part

        @pl.when(kk > 0)
        def _():
            acc[...] += part

        @pl.when(kk == N_CHIPS - 1)
        def _():
            o_ref[...] = acc[...]

    return pl.pallas_call(
        body, name="up_dx", grid=(s // tm, N_CHIPS),
        in_specs=[pl.BlockSpec((1, tm, wsh), lambda i, q: (q >> 1, i, q & 1)), pl.BlockSpec((1, d, wsh), lambda i, q: (q, 0, 0))],
        out_specs=pl.BlockSpec((tm, d), lambda i, q: (i, 0)),
        out_shape=jax.ShapeDtypeStruct((s, d), F32), scratch_shapes=[pltpu.VMEM((tm, d), F32)],
        compiler_params=_cp(("parallel", "arbitrary")),
    )(dup, w_sh)


def _up_dw(hn, dup, *, tk=1024):
    s, d = hn.shape
    wsh = 2 * D_FF // N_CHIPS
    tk = _tile(s, tk)
    nk = s // tk

    def body(a_ref, b_ref, o_ref, acc):
        kk = pl.program_id(1)
        part = _dot_tn(a_ref[...], b_ref[0])

        @pl.when(kk == 0)
        def _():
            acc[...] = part

        @pl.when(kk > 0)
        def _():
            acc[...] += part

        @pl.when(kk == nk - 1)
        def _():
            o_ref[0] = acc[...]

    return pl.pallas_call(
        body, name="up_dw", grid=(N_CHIPS, nk),
        in_specs=[pl.BlockSpec((tk, d), lambda j, q: (q, 0)), pl.BlockSpec((1, tk, wsh), lambda j, q: (j >> 1, q, j & 1))],
        out_specs=pl.BlockSpec((1, d, wsh), lambda j, q: (j, 0, 0)),
        out_shape=jax.ShapeDtypeStruct((N_CHIPS, d, wsh), F32), scratch_shapes=[pltpu.VMEM((d, wsh), F32)],
        compiler_params=_cp(("parallel", "arbitrary")),
    )(hn, dup)


BIG_ROWS =(IN_DIM // 4, Q_DIM // 4, D_INNER // 4, D_MODEL // 4, 2 * D_FF // 4, D_FF // 4)
PACK_ROWS = 5376


def _pack_shards(parts):
    rows = [p.reshape(-1, D_MODEL) for p in parts]
    pad = PACK_ROWS - sum(BIG_ROWS)
    return jnp.concatenate(rows + [jnp.zeros((pad, D_MODEL), rows[0].dtype)], axis=0)


def _unpack_shards(buf):
    out, off = [], 0
    for n in BIG_ROWS:
        out.append(buf[off:off + n])
        off += n
    return out


def _permute_cols_in(w):
    pad = jnp.zeros((w.shape[0], PW - IN_DIM), w.dtype)
    return jnp.concatenate([w[:, :6656], w[:, 6688:], w[:, 6656:6688], pad], axis=1)


def _unpermute_cols_in(g):
    return jnp.concatenate([g[:, :6656], g[:, O_DT:O_DT + 32], g[:, 6656:O_DT]], axis=1)


SMALL = ("norm1_w", "b_gate", "attn_sinks", "ssd_conv_b", "dt_bias", "a_log", "d_skip", "ssd_norm_w", "norm2_w",
         "ffn_conv_b", "final_norm_w", "ssd_conv_w", "ffn_conv_w")


def _pad128(v):
    v = v.reshape(-1)
    return jnp.pad(v, (0, (-v.shape[0]) % 128))


def _pack_small(parts):
    flat = jnp.concatenate([_pad128(p) for p in parts])
    flat = jnp.pad(flat, (0, (-flat.shape[0]) % 1024))
    return flat.reshape(-1, 128)


def _unpack_small(buf, shapes):
    flat, out, off = buf.reshape(-1), [], 0
    for shp in shapes:
        n = 1
        for q in shp:
            n *= q
        out.append(flat[off:off + n].reshape(shp))
        off += n + (-n) % 128
    return out


def _vec128(v):
    return jnp.pad(v.reshape(1, -1), ((0, 0), (0, 128 - v.shape[-1])))


def kernel(x, norm1_w, w_in, b_gate, attn_sinks, w_attn_o, ssd_conv_w, ssd_conv_b, dt_bias, a_log, d_skip, ssd_norm_w, w_ssd_o, w_out, norm2_w, w_up, ffn_conv_w, ffn_conv_b, w_down, final_norm_w, loss_target, m_norm1_w, m_w_in, m_b_gate, m_attn_sinks, m_w_attn_o, m_ssd_conv_w, m_ssd_conv_b, m_dt_bias, m_a_log, m_d_skip, m_ssd_norm_w, m_w_ssd_o, m_w_out, m_norm2_w, m_w_up, m_ffn_conv_w, m_ffn_conv_b, m_w_down, m_final_norm_w, v_norm1_w, v_w_in, v_b_gate, v_attn_sinks, v_w_attn_o, v_ssd_conv_w, v_ssd_conv_b, v_dt_bias, v_a_log, v_d_skip, v_ssd_norm_w, v_w_ssd_o, v_w_out, v_norm2_w, v_w_up, v_ffn_conv_w, v_ffn_conv_b, v_w_down, v_final_norm_w):
    ix, iy, ic = lax.axis_index("x"), lax.axis_index("y"), lax.axis_index("c")
    chip = 2 * ix + iy
    x2 = x[0]
    tgt = loss_target[0]
    s = x2.shape[0]

    wsh = IN_DIM // N_CHIPS
    big_shards = dict(w_in=jnp.pad(w_in[0], ((0, 0), (0, W_IN_PAD - wsh))), w_attn_o=w_attn_o[0], w_ssd_o=w_ssd_o[0],
                      w_out=w_out[0], w_up=w_up[0], w_down=w_down[0])
    gathered = dict(zip(BIG, _gather_big([big_shards[n] for n in BIG])))
    full = {n: gathered[n].reshape(-1, D_MODEL) for n in ("w_attn_o", "w_ssd_o", "w_out", "w_down")}
    full["w_up"] = gathered["w_up"]
    w_in_p = _permute_cols_in(jnp.concatenate([gathered["w_in"][j, :, :wsh] for j in range(N_CHIPS)], axis=1))
    small_sh = _pack_small([ssd_conv_w[0], ffn_conv_w[0]])
    small_all = _all_gather_small(small_sh)
    sc_parts = [_unpack_small(small_all[j], [(4, XBC_DIM // 4), (3, 2 * D_FF // 4)]) for j in range(N_CHIPS)]
    ssd_cw = jnp.concatenate([p[0] for p in sc_parts], axis=1)
    ffn_cw = jnp.concatenate([p[1] for p in sc_parts], axis=1)

    sinks128 = _vec128(attn_sinks)
    dtb128, alog128, dskip128 = _vec128(dt_bias), _vec128(a_log), _vec128(d_skip)

    xn = _rms_fwd(x2, norm1_w, name="norm1_fwd")
    proj = _mm(xn, w_in_p, name="proj_fwd", tn=1280)
    attn_pre = _attn_fwd(proj, sinks128)
    attn = _mm(attn_pre, full["w_attn_o"], name="attn_o_fwd")
    xbc = _ssd_conv_fwd(proj, ssd_cw, ssd_conv_b)
    y_ssd, hprev = _ssd_fwd(xbc, proj, dtb128, alog128, dskip128)
    yn = _gate_norm_fwd(y_ssd, proj, ssd_norm_w)
    ssd_out = _mm(yn, full["w_ssd_o"], name="ssd_o_fwd")
    merged = _merge_fwd(proj, b_gate, attn, ssd_out)
    h1 = _mm(merged, full["w_out"], name="out_fwd", resid=x2)
    hn = _rms_fwd(h1, norm2_w, name="norm2_fwd")
    up = _mm(hn, full["w_up"], name="up_fwd")
    act = _ffn_act_fwd(up, ffn_cw, ffn_conv_b)
    h2 = _mm(act, full["w_down"], name="down_fwd", resid=h1, tk=1408)

    dh2, loss_blk, g_final = _loss_bwd(h2, tgt, final_norm_w.reshape(1, -1))
    dact = _mm(dh2, full["w_down"], name="down_dx", tb=True, tn=1408)
    g_down = _mm(act, dh2, name="down_dw", ta=True, tm=1408)
    dup, g_ffn_cw, g_ffn_cb = _ffn_act_bwd(dact, up, ffn_cw, ffn_conv_b)
    dhn = _up_dx(dup, full["w_up"])
    g_up = _up_dw(hn, dup)
    dh1, g_norm2 = _rms_bwd(dhn, h1, norm2_w, dh2, name="norm2_bwd")
    dmerged = _mm(dh1, full["w_out"], name="out_dx", tb=True)
    g_out = _mm(merged, dh1, name="out_dw", ta=True)
    dattn, dssd_out, dga, dgs, g_ba, g_bs = _merge_bwd(dmerged, proj, b_gate, attn, ssd_out)
    dyn = _mm(dssd_out, full["w_ssd_o"], name="ssd_o_dx", tb=True)
    g_ssd_o = _mm(yn, dssd_out, name="ssd_o_dw", ta=True)
    dy_ssd, dz, g_ssd_norm = _gate_norm_bwd(dyn, y_ssd, proj, ssd_norm_w)
    dxbc, ddt, dvec = _ssd_bwd(xbc, proj, dtb128, alog128, dskip128, hprev, dy_ssd)
    dxbc_raw, g_ssd_cw, g_ssd_cb = _ssd_conv_bwd(dxbc, proj, ssd_cw, ssd_conv_b)
    dattn_pre = _mm(dattn, full["w_attn_o"], name="attn_o_dx", tb=True)
    g_attn_o = _mm(attn_pre, dattn, name="attn_o_dw", ta=True)
    dq, dk, dv, dsk = _attn_bwd(proj, sinks128, attn_pre, dattn_pre)
    dproj = jnp.concatenate([dq, dk, dv, dz, dxbc_raw, ddt[:, :N_SSD_HEADS], dga, dgs], axis=1)
    dproj_sh = jnp.pad(dproj.reshape(s, N_CHIPS, wsh).transpose(1, 0, 2), ((0, 0), (0, 0), (0, W_IN_PAD - wsh)))
    dxn = _proj_dx(dproj_sh, gathered["w_in"])
    g_in = _proj_dw(xn, dproj_sh)
    dx, g_norm1 = _rms_bwd(dxn, x2, norm1_w, dh1, name="norm1_bwd")

    slot_g = dict(w_in=g_in, w_up=g_up)
    for n, g in (("w_attn_o", g_attn_o), ("w_ssd_o", g_ssd_o), ("w_out", g_out), ("w_down", g_down)):
        slot_g[n] = g.reshape(N_CHIPS, -1, D_MODEL)
    big_grads = dict(zip(BIG, _reduce_big([slot_g[n] for n in BIG])))
    big_grads["w_in"] = big_grads["w_in"][:, :wsh]

    small_g = dict(
        norm1_w=g_norm1, b_gate=jnp.concatenate([g_ba, g_bs], axis=1), attn_sinks=dsk[0:1, :16], ssd_conv_b=g_ssd_cb,
        dt_bias=dvec[0:1, :32], a_log=dvec[1:2, :32], d_skip=dvec[2:3, :32], ssd_norm_w=g_ssd_norm, norm2_w=g_norm2,
        ffn_conv_b=jnp.concatenate([g_ffn_cb[0], g_ffn_cb[1]], axis=1), final_norm_w=g_final, ssd_conv_w=g_ssd_cw,
        ffn_conv_w=jnp.concatenate([g_ffn_cw[0], g_ffn_cw[1]], axis=1))
    small_buf = _pack_small([small_g[n] for n in SMALL] + [loss_blk])
    small_sum = _all_reduce_small(small_buf)
    small_shapes = [(1, D_MODEL), (1, 2 * D_MODEL), (1, 16), (1, XBC_DIM), (1, 32), (1, 32), (1, 32), (1, D_INNER),
                    (1, D_MODEL), (1, 2 * D_FF), (D_MODEL,), (4, XBC_DIM), (3, 2 * D_FF), (1, 128)]
    small_list = _unpack_small(small_sum, small_shapes)
    loss = small_list[-1][0, 0]
    grads = dict(zip(SMALL, small_list[:-1]))
    grads["ssd_conv_w"] = lax.dynamic_slice_in_dim(grads["ssd_conv_w"], chip * (XBC_DIM // 4), XBC_DIM // 4, axis=1)
    grads["ffn_conv_w"] = lax.dynamic_slice_in_dim(grads["ffn_conv_w"], chip * (2 * D_FF // 4), 2 * D_FF // 4, axis=1)
    grads.update(big_grads)

    weights = dict(norm1_w=norm1_w, w_in=w_in, b_gate=b_gate, attn_sinks=attn_sinks, w_attn_o=w_attn_o, ssd_conv_w=ssd_conv_w,
                   ssd_conv_b=ssd_conv_b, dt_bias=dt_bias, a_log=a_log, d_skip=d_skip, ssd_norm_w=ssd_norm_w, w_ssd_o=w_ssd_o,
                   w_out=w_out, norm2_w=norm2_w, w_up=w_up, ffn_conv_w=ffn_conv_w, ffn_conv_b=ffn_conv_b, w_down=w_down,
                   final_norm_w=final_norm_w)
    ms = dict(norm1_w=m_norm1_w, w_in=m_w_in, b_gate=m_b_gate, attn_sinks=m_attn_sinks, w_attn_o=m_w_attn_o,
              ssd_conv_w=m_ssd_conv_w, ssd_conv_b=m_ssd_conv_b, dt_bias=m_dt_bias, a_log=m_a_log, d_skip=m_d_skip,
              ssd_norm_w=m_ssd_norm_w, w_ssd_o=m_w_ssd_o, w_out=m_w_out, norm2_w=m_norm2_w, w_up=m_w_up,
              ffn_conv_w=m_ffn_conv_w, ffn_conv_b=m_ffn_conv_b, w_down=m_w_down, final_norm_w=m_final_norm_w)
    vs = dict(norm1_w=v_norm1_w, w_in=v_w_in, b_gate=v_b_gate, attn_sinks=v_attn_sinks, w_attn_o=v_w_attn_o,
              ssd_conv_w=v_ssd_conv_w, ssd_conv_b=v_ssd_conv_b, dt_bias=v_dt_bias, a_log=v_a_log, d_skip=v_d_skip,
              ssd_norm_w=v_ssd_norm_w, w_ssd_o=v_w_ssd_o, w_out=v_w_out, norm2_w=v_norm2_w, w_up=v_w_up,
              ffn_conv_w=v_ffn_conv_w, ffn_conv_b=v_ffn_conv_b, w_down=v_w_down, final_norm_w=v_final_norm_w)
    order = list(weights)
    deltas, new_m, new_v = {}, {}, {}
    for n in BIG:
        shp = weights[n].shape
        d_, m_, v_ = _adamw(weights[n][0], grads[n], ms[n][0], vs[n][0], name="adamw_" + n)
        deltas[n], new_m[n], new_v[n] = d_.reshape(shp), m_.reshape(shp), v_.reshape(shp)
    smalls = [n for n in order if n not in BIG]
    sw = _pack_small([weights[n] for n in smalls])
    sg = _pack_small([grads[n] for n in smalls])
    sm = _pack_small([ms[n] for n in smalls])
    sv = _pack_small([vs[n] for n in smalls])
    sd_, sm_, sv_ = _adamw(sw, sg, sm, sv, name="adamw_small")
    shapes = [weights[n].shape for n in smalls]
    for n, d_, m_, v_ in zip(smalls, _unpack_small(sd_, shapes), _unpack_small(sm_, shapes), _unpack_small(sv_, shapes)):
        deltas[n], new_m[n], new_v[n] = d_, m_, v_
    out_grads = [grads[n].reshape(weights[n].shape) for n in order]
    return (loss, dx[None], *out_grads, *[deltas[n] for n in order], *[new_m[n] for n in order], *[new_v[n] for n in order])
```

```python
import functools

import jax
import jax.numpy as jnp
from jax import lax
from jax.experimental import pallas as pl
from jax.experimental.pallas import tpu as pltpu

F32 = jnp.float32
BF16 = jnp.bfloat16
HI = lax.Precision.HIGHEST

D_MODEL = 1024
Q_DIM = 1024
KV_DIM = 256
D_INNER = 2048
BC_DIM = 512
XBC_DIM = 3072
N_SSD_HEADS = 32
D_FF = 2816
IN_DIM = 8736
BLK = 128
EPS = 1e-5
NEG = -1e30

O_Q, O_K, O_V, O_Z, O_X, O_GA, O_GS, O_DT = 0, 1024, 1280, 1536, 3584, 6656, 7680, 8704
PW = 8960

ADAM_LR, ADAM_B1, ADAM_B2, ADAM_EPS, ADAM_WD, ADAM_STEP = 0.001, 0.9, 0.999, 1e-08, 0.01, 10

VMEM_LIMIT = 52 * 1024 * 1024
MESH = pl.DeviceIdType.MESH


def _cp(sem=None):
    return pltpu.CompilerParams(dimension_semantics=sem, vmem_limit_bytes=VMEM_LIMIT)


def _dot(a, b, prec=None):
    return jnp.dot(a, b, preferred_element_type=F32, precision=prec)


def _dot_nt(a, b, prec=None):
    return lax.dot_general(a, b, (((1,), (1,)), ((), ())), preferred_element_type=F32, precision=prec)


def _dot_tn(a, b, prec=None):
    return lax.dot_general(a, b, (((0,), (0,)), ((), ())), preferred_element_type=F32, precision=prec)


def _sigmoid(x):
    return 0.5 * jnp.tanh(0.5 * x) + 0.5


def _tile(n, want):
    t = min(n, want)
    while n % t:
        t -= 128
    return t


def _mm(a, b, *, name, ta=False, tb=False, out_dtype=F32, resid=None, tm=1024, tn=1024, tk=1024, side=None):
    m, k = (a.shape[1], a.shape[0]) if ta else a.shape
    slots = b.ndim == 3
    if slots:
        n = b.shape[1] if tb else b.shape[0] * b.shape[2]
        tn, tk = (tn, b.shape[2]) if tb else (b.shape[2], tk)
    else:
        n = b.shape[0] if tb else b.shape[1]
    tm, tn, tk = _tile(m, tm), _tile(n, tn), _tile(k, tk)
    nk = k // tk
    dn = (((0 if ta else 1,), (1 if tb else 0,)), ((), ()))

    def body(*refs):
        if resid is None:
            a_ref, b_ref, o_ref, acc = refs
        else:
            a_ref, b_ref, r_ref, o_ref, acc = refs
        kk = pl.program_id(2)
        bv = b_ref[0] if slots else b_ref[...]
        part = lax.dot_general(a_ref[...].astype(BF16), bv.astype(BF16), dn, preferred_element_type=F32)

        @pl.when(kk == 0)
        def _():
            acc[...] = part

        @pl.when(kk > 0)
        def _():
            acc[...] += part

        @pl.when(kk == nk - 1)
        def _():
            r = acc[...]
            if resid is not None:
                r = r + r_ref[...]
            o_ref[...] = r.astype(out_dtype)

    a_spec = pl.BlockSpec((tk, tm), lambda i, j, q: (q, i)) if ta else pl.BlockSpec((tm, tk), lambda i, j, q: (i, q))
    if slots:
        b_spec = (pl.BlockSpec((1, tn, tk), lambda i, j, q: (q, j, 0)) if tb
                  else pl.BlockSpec((1, tk, tn), lambda i, j, q: (j, q, 0)))
    else:
        b_spec = pl.BlockSpec((tn, tk), lambda i, j, q: (j, q)) if tb else pl.BlockSpec((tk, tn), lambda i, j, q: (q, j))
    o_spec = pl.BlockSpec((tm, tn), lambda i, j, q: (i, j))
    ins, specs = [a, b], [a_spec, b_spec]
    if resid is not None:
        ins.append(resid)
        specs.append(o_spec)
    own, extra = _hosted(
        body, name=name, grid=(m // tm, n // tn, nk), in_specs=specs, out_specs=[o_spec],
        out_shape=[jax.ShapeDtypeStruct((m, n), out_dtype)], scratch_shapes=[pltpu.VMEM((tm, tn), F32)],
        args=ins, sem=("parallel", "parallel", "arbitrary"), side=side)
    return own[0] if side is None else (own[0], extra)


def _rms_fwd(x, w, *, name, tm=512):
    s, d = x.shape
    tm = _tile(s, tm)

    def body(x_ref, w_ref, o_ref):
        xv = x_ref[...]
        r = lax.rsqrt(jnp.mean(xv * xv, axis=-1, keepdims=True) + EPS)
        o_ref[...] = ((xv * r) * w_ref[...]).astype(BF16)

    return pl.pallas_call(
        body, name=name, grid=(s // tm,),
        in_specs=[pl.BlockSpec((tm, d), lambda i: (i, 0)), pl.BlockSpec((1, d), lambda i: (0, 0))],
        out_specs=pl.BlockSpec((tm, d), lambda i: (i, 0)),
        out_shape=jax.ShapeDtypeStruct((s, d), BF16), compiler_params=_cp(("parallel",)),
    )(x, w)


def _rms_bwd(dy, x, w, resid, *, name, tm=512):
    s, d = x.shape
    tm = _tile(s, tm)

    def body(dy_ref, x_ref, w_ref, r_ref, dx_ref, dw_ref):
        i = pl.program_id(0)
        xv = x_ref[...]
        r = lax.rsqrt(jnp.mean(xv * xv, axis=-1, keepdims=True) + EPS)
        xh = xv * r
        dyv = dy_ref[...]
        g = dyv * w_ref[...]
        dx_ref[...] = r_ref[...] + r * (g - xh * jnp.mean(g * xh, axis=-1, keepdims=True))
        part = jnp.sum(dyv * xh, axis=0, keepdims=True)

        @pl.when(i == 0)
        def _():
            dw_ref[...] = part

        @pl.when(i > 0)
        def _():
            dw_ref[...] += part

    row = pl.BlockSpec((tm, d), lambda i: (i, 0))
    vec = pl.BlockSpec((1, d), lambda i: (0, 0))
    return pl.pallas_call(
        body, name=name, grid=(s // tm,), in_specs=[row, row, vec, row], out_specs=[row, vec],
        out_shape=[jax.ShapeDtypeStruct((s, d), F32), jax.ShapeDtypeStruct((1, d), F32)],
        compiler_params=_cp(("arbitrary",)),
    )(dy, x, w, resid)


def _loss_bwd(h2, tgt, wf, *, tm=512):
    s, d = h2.shape
    tm = _tile(s, tm)

    def body(h_ref, t_ref, w_ref, dh_ref, loss_ref, dw_ref):
        i = pl.program_id(0)
        hv = h_ref[...]
        r = lax.rsqrt(jnp.mean(hv * hv, axis=-1, keepdims=True) + EPS)
        xh = hv * r
        wv = w_ref[...]
        e = xh * wv - t_ref[...]
        lpart = 0.5 * jnp.sum(jnp.mean(e * e, axis=-1, keepdims=True), axis=0, keepdims=True)
        dout = e * (1.0 / d)
        g = dout * wv
        dh_ref[...] = r * (g - xh * jnp.mean(g * xh, axis=-1, keepdims=True))
        part = jnp.sum(dout * xh, axis=0, keepdims=True)
        lrow = jnp.broadcast_to(lpart, (1, 128))

        @pl.when(i == 0)
        def _():
            dw_ref[...] = part
            loss_ref[...] = lrow

        @pl.when(i > 0)
        def _():
            dw_ref[...] += part
            loss_ref[...] += lrow

    row = pl.BlockSpec((tm, d), lambda i: (i, 0))
    vec = pl.BlockSpec((1, d), lambda i: (0, 0))
    return pl.pallas_call(
        body, name="loss_bwd", grid=(s // tm,), in_specs=[row, row, vec],
        out_specs=[row, pl.BlockSpec((1, 128), lambda i: (0, 0)), vec],
        out_shape=[jax.ShapeDtypeStruct((s, d), F32), jax.ShapeDtypeStruct((1, 128), F32),
                   jax.ShapeDtypeStruct((1, d), F32)],
        compiler_params=_cp(("arbitrary",)),
    )(h2, tgt, wf)


def _attn_mask(n):
    qi = lax.broadcasted_iota(jnp.int32, (4 * BLK, 2 * BLK), 0) & (BLK - 1)
    si = lax.broadcasted_iota(jnp.int32, (4 * BLK, 2 * BLK), 1)
    dist = BLK + qi - si
    kpos = n * BLK - BLK + si
    return (dist >= 0) & (dist < BLK) & (kpos >= 0)


def _attn_probs(q_ref, kc_ref, kp_ref, sk_ref, kvh, valid):
    hs = slice(kvh * 64, (kvh + 1) * 64)
    kb = jnp.concatenate([kp_ref[:, hs], kc_ref[:, hs]], axis=0).astype(BF16)
    qs = jnp.concatenate([q_ref[:, (kvh * 4 + g) * 64:(kvh * 4 + g + 1) * 64] for g in range(4)], axis=0).astype(BF16)
    s = _dot_nt(qs, kb) * 0.125
    s = jnp.where(valid, s, NEG)
    sink = jnp.concatenate(
        [jnp.broadcast_to(sk_ref[0:1, kvh * 4 + g:kvh * 4 + g + 1], (BLK, 1)) for g in range(4)], axis=0)
    m = jnp.maximum(jnp.max(s, axis=1, keepdims=True), sink)
    p = jnp.where(valid, jnp.exp(s - m), 0.0)
    es = jnp.exp(sink - m)
    denom = jnp.sum(p, axis=1, keepdims=True) + es
    return qs, kb, p / denom, es / denom


def _attn_fwd(proj, sinks, side=None):
    s = proj.shape[0]
    nb = s // BLK

    def body(q_ref, kc_ref, kp_ref, vc_ref, vp_ref, sk_ref, o_ref):
        valid = _attn_mask(pl.program_id(0))
        for kvh in range(4):
            hs = slice(kvh * 64, (kvh + 1) * 64)
            _, _, probs, _ = _attn_probs(q_ref, kc_ref, kp_ref, sk_ref, kvh, valid)
            vb = jnp.concatenate([vp_ref[:, hs], vc_ref[:, hs]], axis=0).astype(BF16)
            o = _dot(probs.astype(BF16), vb)
            for g in range(4):
                h = kvh * 4 + g
                o_ref[:, h * 64:(h + 1) * 64] = o[g * BLK:(g + 1) * BLK].astype(BF16)

    prev = lambda n: jnp.maximum(n - 1, 0)
    own, extra = _hosted(
        body, name="attn_fwd", grid=(nb,),
        in_specs=[pl.BlockSpec((BLK, Q_DIM), lambda n: (n, 0)),
                  pl.BlockSpec((BLK, KV_DIM), lambda n: (n, O_K // KV_DIM)),
                  pl.BlockSpec((BLK, KV_DIM), lambda n: (prev(n), O_K // KV_DIM)),
                  pl.BlockSpec((BLK, KV_DIM), lambda n: (n, O_V // KV_DIM)),
                  pl.BlockSpec((BLK, KV_DIM), lambda n: (prev(n), O_V // KV_DIM)),
                  pl.BlockSpec((1, 128), lambda n: (0, 0))],
        out_specs=[pl.BlockSpec((BLK, Q_DIM), lambda n: (n, 0))],
        out_shape=[jax.ShapeDtypeStruct((s, Q_DIM), BF16)], scratch_shapes=[],
        args=(proj, proj, proj, proj, proj, sinks), sem=("parallel",), side=side)
    return own[0] if side is None else (own[0], extra)


def _attn_bwd(proj, sinks, o, do, side=None):
    s = proj.shape[0]
    nb = s // BLK

    def body(q_ref, kc_ref, kp_ref, vc_ref, vp_ref, sk_ref, o_ref, do_ref,
             dq_ref, dk_ref, dv_ref, dsk_ref, ck, cv, nkp, nkc, nvp, nvc):
        n = pl.program_id(0)

        @pl.when(n == 0)
        def _():
            ck[...] = jnp.zeros_like(ck)
            cv[...] = jnp.zeros_like(cv)
            dsk_ref[...] = jnp.zeros_like(dsk_ref)

        @pl.when(n < nb)
        def _():
            valid = _attn_mask(n)
            lane = lax.broadcasted_iota(jnp.int32, (1, 128), 1)
            dsk = jnp.zeros((1, 128), F32)
            for kvh in range(4):
                hs = slice(kvh * 64, (kvh + 1) * 64)
                qs, kb, probs, psink = _attn_probs(q_ref, kc_ref, kp_ref, sk_ref, kvh, valid)
                vb = jnp.concatenate([vp_ref[:, hs], vc_ref[:, hs]], axis=0).astype(BF16)
                heads = [slice((kvh * 4 + g) * 64, (kvh * 4 + g + 1) * 64) for g in range(4)]
                dos = jnp.concatenate([do_ref[:, hh] for hh in heads], axis=0)
                os_ = jnp.concatenate([o_ref[:, hh] for hh in heads], axis=0).astype(F32)
                delta = jnp.sum(dos * os_, axis=1, keepdims=True)
                dos16 = dos.astype(BF16)
                dp = _dot_nt(dos16, vb)
                ds = (probs * (dp - delta) * 0.125).astype(BF16)
                dqs = _dot(ds, kb)
                dkb = _dot_tn(ds, qs)
                dvb = _dot_tn(probs.astype(BF16), dos16)
                nkp[:, hs] = dkb[:BLK]
                nkc[:, hs] = dkb[BLK:]
                nvp[:, hs] = dvb[:BLK]
                nvc[:, hs] = dvb[BLK:]
                sd = psink * delta
                for g in range(4):
                    dq_ref[:, heads[g]] = dqs[g * BLK:(g + 1) * BLK].astype(BF16)
                    val = -jnp.sum(sd[g * BLK:(g + 1) * BLK], axis=0, keepdims=True)
                    dsk = dsk + jnp.where(lane == kvh * 4 + g, val, 0.0)
            dsk_ref[0:1, :] += dsk
            dk_ref[...] = (ck[...] + nkp[...]).astype(BF16)
            dv_ref[...] = (cv[...] + nvp[...]).astype(BF16)
            ck[...] = nkc[...]
            cv[...] = nvc[...]

        @pl.when(n == nb)
        def _():
            dk_ref[...] = ck[...].astype(BF16)
            dv_ref[...] = cv[...].astype(BF16)

    cur = lambda n: jnp.minimum(n, nb - 1)
    prev = lambda n: jnp.maximum(jnp.minimum(n, nb - 1) - 1, 0)
    outb = lambda n: jnp.maximum(n - 1, 0)
    kv_scr = pltpu.VMEM((BLK, KV_DIM), F32)
    own, extra = _hosted(
        body, name="attn_bwd", grid=(nb + 1,),
        in_specs=[pl.BlockSpec((BLK, Q_DIM), lambda n: (cur(n), 0)),
                  pl.BlockSpec((BLK, KV_DIM), lambda n: (cur(n), O_K // KV_DIM)),
                  pl.BlockSpec((BLK, KV_DIM), lambda n: (prev(n), O_K // KV_DIM)),
                  pl.BlockSpec((BLK, KV_DIM), lambda n: (cur(n), O_V // KV_DIM)),
                  pl.BlockSpec((BLK, KV_DIM), lambda n: (prev(n), O_V // KV_DIM)),
                  pl.BlockSpec((1, 128), lambda n: (0, 0)),
                  pl.BlockSpec((BLK, Q_DIM), lambda n: (cur(n), 0)),
                  pl.BlockSpec((BLK, Q_DIM), lambda n: (cur(n), 0))],
        out_specs=[pl.BlockSpec((BLK, Q_DIM), lambda n: (cur(n), 0)),
                   pl.BlockSpec((BLK, KV_DIM), lambda n: (outb(n), 0)),
                   pl.BlockSpec((BLK, KV_DIM), lambda n: (outb(n), 0)),
                   pl.BlockSpec((8, 128), lambda n: (0, 0))],
        out_shape=[jax.ShapeDtypeStruct((s, Q_DIM), BF16), jax.ShapeDtypeStruct((s, KV_DIM), BF16),
                   jax.ShapeDtypeStruct((s, KV_DIM), BF16), jax.ShapeDtypeStruct((8, 128), F32)],
        scratch_shapes=[kv_scr] * 6, args=(proj, proj, proj, proj, proj, sinks, o, do), sem=("arbitrary",), side=side)
    return own if side is None else (own, extra)


def _shift_down(x, j):
    if j == 0:
        return x
    row = lax.broadcasted_iota(jnp.int32, x.shape, 0)
    return jnp.where(row >= j, pltpu.roll(x, j, 0), 0.0)


def _shift_up(x, j):
    if j == 0:
        return x
    s = x.shape[0]
    row = lax.broadcasted_iota(jnp.int32, x.shape, 0)
    return jnp.where(row < s - j, pltpu.roll(x, s - j, 0), 0.0)


def _conv(x, w_ref, b_ref):
    kk = w_ref.shape[0]
    y = _shift_down(x, kk - 1) * w_ref[0:1, :]
    for q in range(1, kk):
        y = y + _shift_down(x, kk - 1 - q) * w_ref[q:q + 1, :]
    return y + b_ref[...]


def _conv_bwd(dy, x, w_ref, dx_dtype):
    kk = w_ref.shape[0]
    dx = _shift_up(dy, kk - 1) * w_ref[0:1, :]
    dws = [jnp.sum(dy * _shift_down(x, kk - 1), axis=0, keepdims=True)]
    for q in range(1, kk):
        dx = dx + _shift_up(dy, kk - 1 - q) * w_ref[q:q + 1, :]
        dws.append(jnp.sum(dy * _shift_down(x, kk - 1 - q), axis=0, keepdims=True))
    return dx.astype(dx_dtype), dws, jnp.sum(dy, axis=0, keepdims=True)


def _dsilu(y, sg):
    return sg * (1.0 + y * (1.0 - sg))


CT = 256


def _ssd_conv_fwd(proj, w, b):
    s = proj.shape[0]

    def body(x_ref, w_ref, b_ref, o_ref):
        y = _conv(x_ref[...], w_ref, b_ref)
        o_ref[...] = y * _sigmoid(y)

    return pl.pallas_call(
        body, name="ssd_conv_fwd", grid=(XBC_DIM // CT,),
        in_specs=[pl.BlockSpec((s, CT), lambda i: (0, O_X // CT + i)), pl.BlockSpec((4, CT), lambda i: (0, i)),
                  pl.BlockSpec((1, CT), lambda i: (0, i))],
        out_specs=pl.BlockSpec((s, CT), lambda i: (0, i)),
        out_shape=jax.ShapeDtypeStruct((s, XBC_DIM), F32), compiler_params=_cp(("parallel",)),
    )(proj, w, b)


def _ssd_conv_bwd(dact, proj, w, b):
    s = proj.shape[0]

    def body(d_ref, x_ref, w_ref, b_ref, dx_ref, dw_ref, db_ref):
        x = x_ref[...]
        y = _conv(x, w_ref, b_ref)
        dy = d_ref[...] * _dsilu(y, _sigmoid(y))
        dx, dws, db = _conv_bwd(dy, x, w_ref, BF16)
        dx_ref[...] = dx
        for q in range(4):
            dw_ref[q:q + 1, :] = dws[q]
        db_ref[...] = db

    return pl.pallas_call(
        body, name="ssd_conv_bwd", grid=(XBC_DIM // CT,),
        in_specs=[pl.BlockSpec((s, CT), lambda i: (0, i)), pl.BlockSpec((s, CT), lambda i: (0, O_X // CT + i)),
                  pl.BlockSpec((4, CT), lambda i: (0, i)), pl.BlockSpec((1, CT), lambda i: (0, i))],
        out_specs=[pl.BlockSpec((s, CT), lambda i: (0, i)), pl.BlockSpec((4, CT), lambda i: (0, i)),
                   pl.BlockSpec((1, CT), lambda i: (0, i))],
        out_shape=[jax.ShapeDtypeStruct((s, XBC_DIM), BF16), jax.ShapeDtypeStruct((4, XBC_DIM), F32),
                   jax.ShapeDtypeStruct((1, XBC_DIM), F32)],
        compiler_params=_cp(("parallel",)),
    )(dact, proj, w, b)


NFT = D_FF // CT


def _ffn_act_fwd(up, w, b):
    s = up.shape[0]

    def body(v_ref, g_ref, wv_ref, wg_ref, bv_ref, bg_ref, o_ref):
        val = _conv(v_ref[...], wv_ref, bv_ref)
        gt = _conv(g_ref[...], wg_ref, bg_ref)
        o_ref[...] = ((gt * _sigmoid(gt)) * val).astype(BF16)

    col = lambda off: (lambda i: (0, off + i))
    return pl.pallas_call(
        body, name="ffn_act_fwd", grid=(NFT,),
        in_specs=[pl.BlockSpec((s, CT), col(0)), pl.BlockSpec((s, CT), col(NFT)),
                  pl.BlockSpec((3, CT), col(0)), pl.BlockSpec((3, CT), col(NFT)),
                  pl.BlockSpec((1, CT), col(0)), pl.BlockSpec((1, CT), col(NFT))],
        out_specs=pl.BlockSpec((s, CT), col(0)),
        out_shape=jax.ShapeDtypeStruct((s, D_FF), BF16), compiler_params=_cp(("parallel",)),
    )(up, up, w, w, b, b)


def _ffn_act_bwd(dact, up, w, b):
    s = up.shape[0]

    def body(d_ref, v_ref, g_ref, wv_ref, wg_ref, bv_ref, bg_ref, dx_ref, dw_ref, db_ref):
        xv, xg = v_ref[...], g_ref[...]
        val = _conv(xv, wv_ref, bv_ref)
        gt = _conv(xg, wg_ref, bg_ref)
        sg = _sigmoid(gt)
        d = d_ref[...]
        for half, (dy, x, w_ref) in enumerate(((d * (gt * sg), xv, wv_ref), (d * val * _dsilu(gt, sg), xg, wg_ref))):
            dx, dws, db = _conv_bwd(dy, x, w_ref, BF16)
            dx_ref[half] = dx
            for q in range(3):
                dw_ref[half, q:q + 1, :] = dws[q]
            db_ref[half] = db

    col = lambda off: (lambda i: (0, off + i))
    both = lambda i: (0, 0, i)
    return pl.pallas_call(
        body, name="ffn_act_bwd", grid=(NFT,),
        in_specs=[pl.BlockSpec((s, CT), col(0)), pl.BlockSpec((s, CT), col(0)), pl.BlockSpec((s, CT), col(NFT)),
                  pl.BlockSpec((3, CT), col(0)), pl.BlockSpec((3, CT), col(NFT)),
                  pl.BlockSpec((1, CT), col(0)), pl.BlockSpec((1, CT), col(NFT))],
        out_specs=[pl.BlockSpec((2, s, CT), both), pl.BlockSpec((2, 3, CT), both), pl.BlockSpec((2, 1, CT), both)],
        out_shape=[jax.ShapeDtypeStruct((2, s, D_FF), BF16), jax.ShapeDtypeStruct((2, 3, D_FF), F32),
                   jax.ShapeDtypeStruct((2, 1, D_FF), F32)],
        compiler_params=_cp(("parallel",)),
    )(dact, up, up, w, w, b, b)


def _expand_mat():
    r = lax.broadcasted_iota(jnp.int32, (128, D_INNER), 0)
    c = lax.broadcasted_iota(jnp.int32, (128, D_INNER), 1)
    return ((c >> 6) == r).astype(BF16)


def _reduce_mat():
    r = lax.broadcasted_iota(jnp.int32, (D_INNER, 128), 0)
    c = lax.broadcasted_iota(jnp.int32, (D_INNER, 128), 1)
    return ((r >> 6) == c).astype(BF16)


def _split(v, parts):
    out = []
    for _ in range(parts - 1):
        p = v.astype(BF16)
        out.append(p)
        v = v - p.astype(F32)
    out.append(v.astype(BF16))
    return out


def _sel_dot(v, sel, parts):
    acc = None
    for p in reversed(_split(v, parts)):
        t = _dot(p, sel)
        acc = t if acc is None else acc + t
    return acc


def _row8(v):
    return jnp.broadcast_to(v, (8, v.shape[1]))


def _tril():
    r = lax.broadcasted_iota(jnp.int32, (BLK, BLK), 0)
    c = lax.broadcasted_iota(jnp.int32, (BLK, BLK), 1)
    return r >= c


def _softplus(x):
    return jnp.maximum(x, 0.0) + jnp.log(1.0 + jnp.exp(-jnp.abs(x)))


def _ssd_common(dtraw_ref, dtb_ref, alog_ref):
    causal = _tril()
    e_mat = _expand_mat()
    a_neg = -jnp.exp(alog_ref[...])
    dt = _softplus(dtraw_ref[...] + dtb_ref[...])
    a_cs = _dot(causal.astype(F32), dt * a_neg, HI)
    a_cs_t = a_cs.T
    dt_x = _sel_dot(dt, e_mat, 3)
    acs_x = _sel_dot(a_cs, e_mat, 3)
    alast_x = acs_x[BLK - 1:BLK, :]
    ea_x = jnp.exp(acs_x)
    ds_x = jnp.exp(alast_x - acs_x)
    elast_x = jnp.exp(alast_x)
    return causal, e_mat, a_neg, dt, a_cs, a_cs_t, dt_x, ea_x, ds_x, elast_x


def _decay(a_cs, a_cs_t, h, causal):
    seg = a_cs[:, h:h + 1] - a_cs_t[h:h + 1, :]
    return jnp.where(causal, jnp.exp(jnp.where(causal, seg, 0.0)), 0.0)


def _ssd_fwd(xbc, proj, dt_bias, a_log, d_skip):
    s = xbc.shape[0]
    nc = s // BLK

    def body(xs_ref, b_ref, c_ref, dtraw_ref, dtb_ref, alog_ref, dskip_ref, y_ref, hp_ref, h_scr, xc16):
        @pl.when(pl.program_id(0) == 0)
        def _():
            h_scr[...] = jnp.zeros_like(h_scr)

        causal, e_mat, _, _, a_cs, a_cs_t, dt_x, ea_x, ds_x, elast_x = _ssd_common(dtraw_ref, dtb_ref, alog_ref)
        dskip_x = _sel_dot(_row8(dskip_ref[...]), e_mat, 3)[0:1]
        xs = xs_ref[...]
        xc = xs * dt_x
        xc16[...] = xc.astype(BF16)
        xcd = (xc * ds_x).astype(BF16)
        hp_ref[0] = h_scr[...]
        for g in range(4):
            gs = slice(g * 512, (g + 1) * 512)
            cg = c_ref[:, g * 128:(g + 1) * 128].astype(BF16)
            bg = b_ref[:, g * 128:(g + 1) * 128].astype(BF16)
            cb = _dot_nt(cg, bg)
            hg = h_scr[:, gs]
            yoff = _dot(cg, hg.astype(BF16)) * ea_x[:, gs]
            for j in range(8):
                h = g * 8 + j
                hsl = slice(h * 64, (h + 1) * 64)
                mm = (cb * _decay(a_cs, a_cs_t, h, causal)).astype(BF16)
                y_ref[:, hsl] = _dot(mm, xc16[:, hsl])
            y_ref[:, gs] += yoff + xs[:, gs] * dskip_x[:, gs]
            h_scr[:, gs] = hg * elast_x[:, gs] + _dot_tn(bg, xcd[:, gs])

    vec = pl.BlockSpec((1, 128), lambda c: (0, 0))
    return pl.pallas_call(
        body, name="ssd_fwd", grid=(nc,),
        in_specs=[pl.BlockSpec((BLK, D_INNER), lambda c: (c, 0)),
                  pl.BlockSpec((BLK, BC_DIM), lambda c: (c, D_INNER // BC_DIM)),
                  pl.BlockSpec((BLK, BC_DIM), lambda c: (c, D_INNER // BC_DIM + 1)),
                  pl.BlockSpec((BLK, 128), lambda c: (c, O_DT // 128)), vec, vec, vec],
        out_specs=[pl.BlockSpec((BLK, D_INNER), lambda c: (c, 0)),
                   pl.BlockSpec((1, 128, D_INNER), lambda c: (c, 0, 0))],
        out_shape=[jax.ShapeDtypeStruct((s, D_INNER), F32), jax.ShapeDtypeStruct((nc, 128, D_INNER), F32)],
        scratch_shapes=[pltpu.VMEM((128, D_INNER), F32), pltpu.VMEM((BLK, D_INNER), BF16)],
        compiler_params=_cp(("arbitrary",)),
    )(xbc, xbc, xbc, proj, dt_bias, a_log, d_skip)


def _ssd_bwd(xbc, proj, dt_bias, a_log, d_skip, hprev, dy, side=None):
    s = xbc.shape[0]
    nc = s // BLK

    def body(xs_ref, b_ref, c_ref, dtraw_ref, dtb_ref, alog_ref, dskip_ref, hp_ref, dy_ref,
             dxbc_ref, ddt_ref, dvec_ref, dh_scr, xc16, dy16, dxc_scr, dacs_r, tdiff):
        step = pl.program_id(0)
        dacs_r[...] = jnp.zeros_like(dacs_r)

        @pl.when(step == 0)
        def _():
            dh_scr[...] = jnp.zeros_like(dh_scr)
            dvec_ref[...] = jnp.zeros_like(dvec_ref)

        causal, e_mat, a_neg, dt, a_cs, a_cs_t, dt_x, ea_x, ds_x, elast_x = _ssd_common(dtraw_ref, dtb_ref, alog_ref)
        r_mat = _reduce_mat()
        lane = lax.broadcasted_iota(jnp.int32, (1, 128), 1)
        dskip_x = _sel_dot(_row8(dskip_ref[...]), e_mat, 3)[0:1]
        xs = xs_ref[...]
        dy = dy_ref[...]
        xc = xs * dt_x
        xcd = xc * ds_x
        xc16[...] = xc.astype(BF16)
        dy16[...] = dy.astype(BF16)
        dyea = dy * ea_x
        dh = dh_scr[...]
        hp = hp_ref[0]
        dalast_x = jnp.sum(dh * hp, axis=0, keepdims=True) * elast_x
        dacs = jnp.zeros((BLK, 128), F32)
        for g in range(4):
            gs = slice(g * 512, (g + 1) * 512)
            bsl = slice(g * 128, (g + 1) * 128)
            cg = c_ref[:, bsl].astype(BF16)
            bg = b_ref[:, bsl].astype(BF16)
            cb = _dot_nt(cg, bg)
            hg16 = hp[:, gs].astype(BF16)
            dhg16 = dh[:, gs].astype(BF16)
            raw = _dot(cg, hg16)
            draw16 = dyea[:, gs].astype(BF16)
            dcg = _dot_nt(draw16, hg16)
            dhp_g = _dot_tn(cg, draw16)
            dbg = _dot_nt(xcd[:, gs].astype(BF16), dhg16)
            dxcd = _dot(bg, dhg16)
            dcb = jnp.zeros((BLK, BLK), F32)
            for j in range(8):
                h = g * 8 + j
                hsl = slice(h * 64, (h + 1) * 64)
                decay = _decay(a_cs, a_cs_t, h, causal)
                m = cb * decay
                dm = _dot_nt(dy16[:, hsl], xc16[:, hsl])
                dxc_scr[:, hsl] = _dot_tn(m.astype(BF16), dy16[:, hsl])
                dcb = dcb + dm * decay
                dseg = dm * m
                oneh = jnp.where(lane == h, 1.0, 0.0)
                dacs = dacs + jnp.sum(dseg, axis=1, keepdims=True) * oneh
                dacs_r[h:h + 1, :] = jnp.sum(dseg, axis=0, keepdims=True)
            dcb16 = dcb.astype(BF16)
            dcg = dcg + _dot(dcb16, bg)
            dbg = dbg + _dot_tn(dcb16, cg)
            dxbc_ref[:, D_INNER + g * 128:D_INNER + (g + 1) * 128] = dbg
            dxbc_ref[:, D_INNER + BC_DIM + g * 128:D_INNER + BC_DIM + (g + 1) * 128] = dcg
            dxc_scr[:, gs] += dxcd * ds_x[:, gs]
            dh_scr[:, gs] = dh[:, gs] * elast_x[:, gs] + dhp_g
            tst = dxcd * xcd[:, gs]
            tdiff[:, gs] = dy[:, gs] * (raw * ea_x[:, gs]) - tst
            tdiff[BLK - 1:BLK, gs] += jnp.sum(tst, axis=0, keepdims=True)
        dxc = dxc_scr[...]
        row = lax.broadcasted_iota(jnp.int32, (BLK, D_INNER), 0)
        tfull = tdiff[...] + jnp.where(row == BLK - 1, dalast_x, 0.0)
        dacs = dacs + _sel_dot(tfull, r_mat, 2) - dacs_r[...].T
        da = _dot_tn(causal.astype(F32), dacs, HI)
        ddt = da * a_neg + _sel_dot(dxc * xs, r_mat, 2)
        lmask = lax.broadcasted_iota(jnp.int32, (BLK, 128), 1) < N_SSD_HEADS
        ddtraw = jnp.where(lmask, ddt * _sigmoid(dtraw_ref[...] + dtb_ref[...]), 0.0)
        ddt_ref[...] = ddtraw.astype(BF16)
        dxbc_ref[:, 0:D_INNER] = dy * dskip_x + dxc * dt_x
        dvec_ref[0:1, :] += jnp.sum(ddtraw, axis=0, keepdims=True)
        dvec_ref[1:2, :] += jnp.where(lane < N_SSD_HEADS, jnp.sum(da * dt, axis=0, keepdims=True) * a_neg, 0.0)
        dvec_ref[2:3, :] += _sel_dot(_row8(jnp.sum(dy * xs, axis=0, keepdims=True)), r_mat, 3)[0:1]

    rev = lambda c: nc - 1 - c
    vec = pl.BlockSpec((1, 128), lambda c: (0, 0))
    own, extra = _hosted(
        body, name="ssd_bwd", grid=(nc,),
        in_specs=[pl.BlockSpec((BLK, D_INNER), lambda c: (rev(c), 0)),
                  pl.BlockSpec((BLK, BC_DIM), lambda c: (rev(c), D_INNER // BC_DIM)),
                  pl.BlockSpec((BLK, BC_DIM), lambda c: (rev(c), D_INNER // BC_DIM + 1)),
                  pl.BlockSpec((BLK, 128), lambda c: (rev(c), O_DT // 128)), vec, vec, vec,
                  pl.BlockSpec((1, 128, D_INNER), lambda c: (rev(c), 0, 0)),
                  pl.BlockSpec((BLK, D_INNER), lambda c: (rev(c), 0))],
        out_specs=[pl.BlockSpec((BLK, XBC_DIM), lambda c: (rev(c), 0)),
                   pl.BlockSpec((BLK, 128), lambda c: (rev(c), 0)),
                   pl.BlockSpec((8, 128), lambda c: (0, 0))],
        out_shape=[jax.ShapeDtypeStruct((s, XBC_DIM), F32), jax.ShapeDtypeStruct((s, 128), BF16),
                   jax.ShapeDtypeStruct((8, 128), F32)],
        scratch_shapes=[pltpu.VMEM((128, D_INNER), F32), pltpu.VMEM((BLK, D_INNER), BF16),
                        pltpu.VMEM((BLK, D_INNER), BF16), pltpu.VMEM((BLK, D_INNER), F32),
                        pltpu.VMEM((128, BLK), F32), pltpu.VMEM((BLK, D_INNER), F32)],
        args=(xbc, xbc, xbc, proj, dt_bias, a_log, d_skip, hprev, dy), sem=("arbitrary",), side=side)
    return own if side is None else (own, extra)


GW = 512


def _gate_norm_fwd(y, proj, wn, *, tm=512):
    s = y.shape[0]
    tm = _tile(s, tm)

    def body(y_ref, z_ref, w_ref, o_ref):
        z = z_ref[...]
        y2 = y_ref[...] * (z * _sigmoid(z))
        r = lax.rsqrt(jnp.mean(y2 * y2, axis=-1, keepdims=True) + EPS)
        o_ref[...] = ((y2 * r) * w_ref[...]).astype(BF16)

    return pl.pallas_call(
        body, name="gate_norm_fwd", grid=(s // tm, 4),
        in_specs=[pl.BlockSpec((tm, GW), lambda i, g: (i, g)), pl.BlockSpec((tm, GW), lambda i, g: (i, O_Z // GW + g)),
                  pl.BlockSpec((1, GW), lambda i, g: (0, g))],
        out_specs=pl.BlockSpec((tm, GW), lambda i, g: (i, g)),
        out_shape=jax.ShapeDtypeStruct((s, D_INNER), BF16), compiler_params=_cp(("parallel", "parallel")),
    )(y, proj, wn)


def _gate_norm_bwd(dyn, y, proj, wn, *, tm=512):
    s = y.shape[0]
    tm = _tile(s, tm)

    def body(d_ref, y_ref, z_ref, w_ref, dy_ref, dz_ref, dw_ref):
        i = pl.program_id(1)
        z = z_ref[...]
        sg = _sigmoid(z)
        sz = z * sg
        yv = y_ref[...]
        y2 = yv * sz
        r = lax.rsqrt(jnp.mean(y2 * y2, axis=-1, keepdims=True) + EPS)
        xh = y2 * r
        dv = d_ref[...]
        g = dv * w_ref[...]
        dy2 = r * (g - xh * jnp.mean(g * xh, axis=-1, keepdims=True))
        dy_ref[...] = dy2 * sz
        dz_ref[...] = (dy2 * yv * _dsilu(z, sg)).astype(BF16)
        part = jnp.sum(dv * xh, axis=0, keepdims=True)

        @pl.when(i == 0)
        def _():
            dw_ref[...] = part

        @pl.when(i > 0)
        def _():
            dw_ref[...] += part

    blk = pl.BlockSpec((tm, GW), lambda g, i: (i, g))
    vec = pl.BlockSpec((1, GW), lambda g, i: (0, g))
    return pl.pallas_call(
        body, name="gate_norm_bwd", grid=(4, s // tm),
        in_specs=[blk, blk, pl.BlockSpec((tm, GW), lambda g, i: (i, O_Z // GW + g)), vec],
        out_specs=[blk, blk, vec],
        out_shape=[jax.ShapeDtypeStruct((s, D_INNER), F32), jax.ShapeDtypeStruct((s, D_INNER), BF16),
                   jax.ShapeDtypeStruct((1, D_INNER), F32)],
        compiler_params=_cp(("parallel", "arbitrary")),
    )(dyn, y, proj, wn)


def _merge_fwd(proj, b_gate, attn, ssd_out, *, tm=512):
    s = attn.shape[0]
    tm = _tile(s, tm)

    def body(ga_ref, gs_ref, ba_ref, bs_ref, a_ref, s_ref, o_ref):
        ga = _sigmoid(ga_ref[...] + ba_ref[...])
        gs = _sigmoid(gs_ref[...] + bs_ref[...])
        o_ref[...] = (ga * a_ref[...] + gs * s_ref[...]).astype(BF16)

    blk = pl.BlockSpec((tm, GW), lambda i, j: (i, j))
    return pl.pallas_call(
        body, name="merge_fwd", grid=(s // tm, 2),
        in_specs=[pl.BlockSpec((tm, GW), lambda i, j: (i, O_GA // GW + j)),
                  pl.BlockSpec((tm, GW), lambda i, j: (i, O_GS // GW + j)),
                  pl.BlockSpec((1, GW), lambda i, j: (0, j)), pl.BlockSpec((1, GW), lambda i, j: (0, 2 + j)), blk, blk],
        out_specs=blk, out_shape=jax.ShapeDtypeStruct((s, D_MODEL), BF16),
        compiler_params=_cp(("parallel", "parallel")),
    )(proj, proj, b_gate, b_gate, attn, ssd_out)


def _merge_bwd(dm, proj, b_gate, attn, ssd_out, *, tm=512):
    s = attn.shape[0]
    tm = _tile(s, tm)

    def body(d_ref, ga_ref, gs_ref, ba_ref, bs_ref, a_ref, s_ref, da_ref, ds_ref, dga_ref, dgs_ref, dba_ref, dbs_ref):
        i = pl.program_id(1)
        ga = _sigmoid(ga_ref[...] + ba_ref[...])
        gs = _sigmoid(gs_ref[...] + bs_ref[...])
        d = d_ref[...]
        da_ref[...] = (d * ga).astype(BF16)
        ds_ref[...] = (d * gs).astype(BF16)
        dga = d * a_ref[...] * (ga * (1.0 - ga))
        dgs = d * s_ref[...] * (gs * (1.0 - gs))
        dga_ref[...] = dga.astype(BF16)
        dgs_ref[...] = dgs.astype(BF16)
        pa = jnp.sum(dga, axis=0, keepdims=True)
        ps = jnp.sum(dgs, axis=0, keepdims=True)

        @pl.when(i == 0)
        def _():
            dba_ref[...] = pa
            dbs_ref[...] = ps

        @pl.when(i > 0)
        def _():
            dba_ref[...] += pa
            dbs_ref[...] += ps

    blk = pl.BlockSpec((tm, GW), lambda j, i: (i, j))
    vec = pl.BlockSpec((1, GW), lambda j, i: (0, j))
    sd = jax.ShapeDtypeStruct((s, D_MODEL), BF16)
    vd = jax.ShapeDtypeStruct((1, D_MODEL), F32)
    return pl.pallas_call(
        body, name="merge_bwd", grid=(2, s // tm),
        in_specs=[blk, pl.BlockSpec((tm, GW), lambda j, i: (i, O_GA // GW + j)),
                  pl.BlockSpec((tm, GW), lambda j, i: (i, O_GS // GW + j)),
                  vec, pl.BlockSpec((1, GW), lambda j, i: (0, 2 + j)), blk, blk],
        out_specs=[blk, blk, blk, blk, vec, vec], out_shape=[sd, sd, sd, sd, vd, vd],
        compiler_params=_cp(("parallel", "arbitrary")),
    )(dm, proj, proj, b_gate, b_gate, attn, ssd_out)


def _adamw(w, g, m, v, *, name, tm=128):
    r, c = w.shape
    tm = r if (r < tm or r % tm) else tm

    def body(w_ref, g_ref, m_ref, v_ref, d_ref, nm_ref, nv_ref):
        gv = g_ref[...]
        mn = ADAM_B1 * m_ref[...] + (1.0 - ADAM_B1) * gv
        vn = ADAM_B2 * v_ref[...] + (1.0 - ADAM_B2) * (gv * gv)
        m_hat = mn / (1.0 - ADAM_B1 ** ADAM_STEP)
        v_hat = vn / (1.0 - ADAM_B2 ** ADAM_STEP)
        d_ref[...] = -ADAM_LR * (m_hat / (jnp.sqrt(v_hat) + ADAM_EPS) + ADAM_WD * w_ref[...])
        nm_ref[...] = mn
        nv_ref[...] = vn

    blk = pl.BlockSpec((tm, c), lambda i: (i, 0))
    sd = jax.ShapeDtypeStruct((r, c), F32)
    return pl.pallas_call(
        body, name=name, grid=(r // tm,), in_specs=[blk] * 4, out_specs=[blk] * 3, out_shape=[sd] * 3,
        compiler_params=_cp(("parallel",)),
    )(w, g, m, v)


ANY = pl.BlockSpec(memory_space=pl.ANY)
N_CHIPS = 4


def _chip_of(k, x, y):
    return (x ^ (k >> 1), y ^ (k & 1))


def _all_gather_small(shard):
    r, c = shard.shape
    hr = r // 2

    def body(sh_ref, out_ref, send_sems, recv_sems, local_sem):
        x, y, cc = lax.axis_index("x"), lax.axis_index("y"), lax.axis_index("c")

        def half(px, py, pc):
            return out_ref.at[2 * px + py, pl.ds(pc * hr, hr), :]

        def copy(k, px, py, pc, to, src=None):
            return pltpu.make_async_remote_copy(
                src_ref=half(px, py, pc) if src is None else src, dst_ref=half(px, py, pc),
                send_sem=send_sems.at[k], recv_sem=recv_sems.at[k], device_id=to, device_id_type=MESH)

        mine = pltpu.make_async_copy(sh_ref, out_ref.at[2 * x + y], local_sem)
        mine.start()
        chips = [_chip_of(k, x, y) for k in (1, 2, 3)]
        first = [copy(j, x, y, cc, (*chip, cc), src=sh_ref.at[pl.ds(cc * hr, hr), :]) for j, chip in enumerate(chips)]
        for cp in first:
            cp.start()
        passed = [copy(3 + j, *chip, cc, (x, y, 1 - cc)) for j, chip in enumerate(chips)]
        for j, chip in enumerate(chips):
            copy(j, *chip, cc, (x, y, cc)).wait_recv()
            passed[j].start()
        for j, chip in enumerate(chips):
            copy(3 + j, *chip, 1 - cc, (x, y, cc)).wait_recv()
        for cp in first + passed:
            cp.wait_send()
        mine.wait()

    return pl.pallas_call(
        body, name="all_gather_small", in_specs=[ANY], out_specs=ANY,
        out_shape=jax.ShapeDtypeStruct((N_CHIPS, r, c), shard.dtype),
        scratch_shapes=[pltpu.SemaphoreType.DMA((6,)), pltpu.SemaphoreType.DMA((6,)), pltpu.SemaphoreType.DMA],
    )(shard)


def _cast_bf16(a, *, name, tm=512):
    n, r, c = a.shape
    tm = _tile(r, tm) if r % 128 == 0 else r

    def body(a_ref, o_ref):
        o_ref[...] = a_ref[...].astype(BF16)

    blk = pl.BlockSpec((1, tm, c), lambda i, j: (i, j, 0))
    return pl.pallas_call(body, name=name, grid=(n, r // tm), in_specs=[blk], out_specs=blk,
                          out_shape=jax.ShapeDtypeStruct(a.shape, BF16), compiler_params=_cp(("parallel", "parallel")))(a)


def _pair_exchange(g16, hr):
    n, r, c = g16.shape

    def body(g_ref, out_ref, send_sem, recv_sem):
        x, y, cc = lax.axis_index("x"), lax.axis_index("y"), lax.axis_index("c")
        cp = pltpu.make_async_remote_copy(
            src_ref=g_ref.at[:, pl.ds((1 - cc) * hr, hr), :], dst_ref=out_ref, send_sem=send_sem, recv_sem=recv_sem,
            device_id=(x, y, 1 - cc), device_id_type=MESH)
        cp.start()
        cp.wait()

    return pl.pallas_call(
        body, name="grad_pair_exchange", in_specs=[ANY], out_specs=ANY,
        out_shape=jax.ShapeDtypeStruct((n, hr, c), g16.dtype),
        scratch_shapes=[pltpu.SemaphoreType.DMA, pltpu.SemaphoreType.DMA],
    )(g16)


def _pair_add(g, recv, half_idx, hr, *, tm=384):
    n, r, c = g.shape
    nt = hr // tm

    def body(hi_ref, g_ref, r_ref, o32_ref, o16_ref):
        v = g_ref[...] + r_ref[...].astype(F32)
        o32_ref[...] = v
        o16_ref[...] = v.astype(BF16)

    gs = pltpu.PrefetchScalarGridSpec(
        num_scalar_prefetch=1, grid=(n, nt),
        in_specs=[pl.BlockSpec((1, tm, c), lambda i, j, hi: (i, hi[0] * nt + j, 0)),
                  pl.BlockSpec((1, tm, c), lambda i, j, hi: (i, j, 0))],
        out_specs=[pl.BlockSpec((1, tm, c), lambda i, j, hi: (i, j, 0))] * 2)
    return pl.pallas_call(
        body, name="grad_pair_add", grid_spec=gs,
        out_shape=[jax.ShapeDtypeStruct((n, hr, c), F32), jax.ShapeDtypeStruct((n, hr, c), BF16)],
        compiler_params=_cp(("parallel", "parallel")),
    )(half_idx, g, recv)


def _chip_exchange(p16):
    n, hr, c = p16.shape

    def body(p_ref, out_ref, send_sems, recv_sems):
        x, y, cc = lax.axis_index("x"), lax.axis_index("y"), lax.axis_index("c")
        cps = []
        for j, k in enumerate((1, 2, 3)):
            px, py = _chip_of(k, x, y)
            cps.append(pltpu.make_async_remote_copy(
                src_ref=p_ref.at[2 * px + py], dst_ref=out_ref.at[j], send_sem=send_sems.at[j], recv_sem=recv_sems.at[j],
                device_id=(px, py, cc), device_id_type=MESH))
        for cp in cps:
            cp.start()
        for cp in cps:
            cp.wait()

    return pl.pallas_call(
        body, name="grad_chip_exchange", in_specs=[ANY], out_specs=ANY,
        out_shape=jax.ShapeDtypeStruct((3, hr, c), p16.dtype),
        scratch_shapes=[pltpu.SemaphoreType.DMA((3,)), pltpu.SemaphoreType.DMA((3,))],
    )(p16)


def _chip_add(p32, recv, chip_idx, *, tm=384):
    n, hr, c = p32.shape

    def body(ci_ref, p_ref, r_ref, o_ref):
        o_ref[...] = ((p_ref[0] + r_ref[0].astype(F32)) + r_ref[1].astype(F32)) + r_ref[2].astype(F32)

    gs = pltpu.PrefetchScalarGridSpec(
        num_scalar_prefetch=1, grid=(hr // tm,),
        in_specs=[pl.BlockSpec((1, tm, c), lambda j, ci: (ci[0], j, 0)), pl.BlockSpec((3, tm, c), lambda j, ci: (0, j, 0))],
        out_specs=pl.BlockSpec((tm, c), lambda j, ci: (j, 0)))
    return pl.pallas_call(
        body, name="grad_chip_add", grid_spec=gs, out_shape=jax.ShapeDtypeStruct((hr, c), F32),
        compiler_params=_cp(("parallel",)),
    )(chip_idx, p32, recv)


def _pair_gather(f):
    hr, c = f.shape

    def body(f_ref, out_ref, send_sem, recv_sem, local_sem):
        x, y, cc = lax.axis_index("x"), lax.axis_index("y"), lax.axis_index("c")
        mine = pltpu.make_async_copy(f_ref, out_ref.at[pl.ds(cc * hr, hr), :], local_sem)
        mine.start()
        cp = pltpu.make_async_remote_copy(
            src_ref=f_ref, dst_ref=out_ref.at[pl.ds(cc * hr, hr), :], send_sem=send_sem, recv_sem=recv_sem,
            device_id=(x, y, 1 - cc), device_id_type=MESH)
        cp.start()
        cp.wait()
        mine.wait()

    return pl.pallas_call(
        body, name="grad_pair_gather", in_specs=[ANY], out_specs=ANY,
        out_shape=jax.ShapeDtypeStruct((2 * hr, c), f.dtype),
        scratch_shapes=[pltpu.SemaphoreType.DMA, pltpu.SemaphoreType.DMA, pltpu.SemaphoreType.DMA],
    )(f)


def _all_reduce_small(buf):
    r, c = buf.shape

    def body(b_ref, out_ref, gat, send_sems, recv_sems):
        x, y, cc = lax.axis_index("x"), lax.axis_index("y"), lax.axis_index("c")
        me = 4 * x + 2 * y + cc
        gat[me] = b_ref[...]
        cps = []
        for k in range(1, 8):
            px, py, pc = x ^ (k >> 2), y ^ ((k >> 1) & 1), cc ^ (k & 1)
            cps.append(pltpu.make_async_remote_copy(
                src_ref=b_ref, dst_ref=gat.at[me], send_sem=send_sems.at[k - 1], recv_sem=recv_sems.at[k - 1],
                device_id=(px, py, pc), device_id_type=MESH))
        for cp in cps:
            cp.start()
        for cp in cps:
            cp.wait()
        acc = gat[0]
        for d in range(1, 8):
            acc = acc + gat[d]
        out_ref[...] = acc

    vm = pl.BlockSpec(memory_space=pltpu.VMEM)
    return pl.pallas_call(
        body, name="all_reduce_small", in_specs=[vm], out_specs=vm, out_shape=jax.ShapeDtypeStruct((r, c), F32),
        scratch_shapes=[pltpu.VMEM((8, r, c), F32), pltpu.SemaphoreType.DMA((7,)), pltpu.SemaphoreType.DMA((7,))],
        compiler_params=pltpu.CompilerParams(vmem_limit_bytes=VMEM_LIMIT),
    )(buf)


def _pipe(fn, ins, outs, tr):
    shape = ins[0].shape
    lead, (r, c) = shape[:-2], shape[-2:]
    assert len(lead) <= 1 and r % tr == 0
    nr = r // tr
    n = nr * (lead[0] if lead else 1)
    ni, no = len(ins), len(outs)

    def blk(ref, step):
        rows = pl.ds(pl.multiple_of((step % nr) * tr, tr), tr)
        return ref.at[step // nr, rows, :] if lead else ref.at[rows, :]

    def scoped(*bufs):
        ibufs, obufs, isem, osem = bufs[:ni], bufs[ni:ni + no], bufs[-2], bufs[-1]

        def in_copy(q, step, slot):
            return pltpu.make_async_copy(blk(ins[q], step), ibufs[q].at[slot], isem.at[q, slot])

        def out_copy(q, step, slot):
            return pltpu.make_async_copy(obufs[q].at[slot], blk(outs[q], step), osem.at[q, slot])

        for q in range(ni):
            in_copy(q, 0, 0).start()

        def body(step, carry):
            slot = step % 2

            @pl.when(step + 1 < n)
            def _():
                for q in range(ni):
                    in_copy(q, step + 1, 1 - slot).start()

            for q in range(ni):
                in_copy(q, step, slot).wait()

            @pl.when(step >= 2)
            def _():
                for q in range(no):
                    out_copy(q, step - 2, slot).wait()

            res = fn(*[ibufs[q][slot] for q in range(ni)])
            for q in range(no):
                obufs[q][slot] = res[q].astype(obufs[q].dtype)
                out_copy(q, step, slot).start()
            return carry

        lax.fori_loop(0, n, body, 0)
        for step in range(max(n - 2, 0), n):
            for q in range(no):
                out_copy(q, step, step % 2).wait()

    pl.run_scoped(scoped, *[pltpu.VMEM((2, tr, c), q.dtype) for q in ins], *[pltpu.VMEM((2, tr, c), q.dtype) for q in outs],
                  pltpu.SemaphoreType.DMA((ni, 2)), pltpu.SemaphoreType.DMA((no, 2)))


W_IN_PAD = 2304
BIG = ("w_in", "w_attn_o", "w_ssd_o", "w_out", "w_up", "w_down")
BIG_SHAPE = dict(w_in=(D_MODEL, W_IN_PAD), w_attn_o=(Q_DIM // 4, D_MODEL), w_ssd_o=(D_INNER // 4, D_MODEL),
                 w_out=(D_MODEL // 4, D_MODEL), w_up=(D_MODEL, 2 * D_FF // 4), w_down=(D_FF // 4, D_MODEL))
BIG_TR = dict(w_in=128, w_attn_o=128, w_ssd_o=128, w_out=128, w_up=128, w_down=176)
X_FIRST = dict(w_in=True, w_attn_o=True, w_ssd_o=False, w_out=True, w_up=False, w_down=False)


def _neighbours(x, y, x_first):
    xn, yn = (1 - x, y), (x, 1 - y)
    n1, n2 = (xn, yn) if x_first else (yn, xn)
    slot = lambda ch: 2 * ch[0] + ch[1]
    return n1, n2, slot(n1), slot(n2), slot((1 - x, 1 - y))


def _gather_big(shards):
    nt = len(BIG)

    def body(*refs):
        sh, out = refs[:nt], refs[nt:2 * nt]
        send_sems, recv_sems = refs[2 * nt:]
        x, y, cc = lax.axis_index("x"), lax.axis_index("y"), lax.axis_index("c")
        me = 2 * x + y
        sib = (x, y, 1 - cc)
        for t, n in enumerate(BIG):
            _pipe(lambda v: (v,), [sh[t]], [out[t].at[me]], BIG_TR[n])

        def copy(t, k, slot, pc, to):
            hr = BIG_SHAPE[BIG[t]][0] // 2
            ref = out[t].at[slot, pl.ds(pc * hr, hr), :]
            return pltpu.make_async_remote_copy(src_ref=ref, dst_ref=ref, send_sem=send_sems.at[6 * t + k],
                                                recv_sem=recv_sems.at[6 * t + k], device_id=to, device_id_type=MESH)

        started = []

        def start(cp):
            cp.start()
            started.append(cp)

        geo = [_neighbours(x, y, X_FIRST[n]) for n in BIG]
        for t in range(nt):
            n1, n2, _, _, _ = geo[t]
            start(copy(t, 0, me, cc, (*n1, cc)))
            start(copy(t, 1, me, cc, (*n2, cc)))
        for t in range(nt):
            n1, n2, s1, s2, sd = geo[t]
            copy(t, 0, s1, cc, sib).wait_recv()
            start(copy(t, 2, s1, cc, (*n2, cc)))
            start(copy(t, 3, s1, cc, sib))
            copy(t, 1, s2, cc, sib).wait_recv()
            start(copy(t, 4, s2, cc, sib))
        for t in range(nt):
            _, _, s1, s2, sd = geo[t]
            copy(t, 2, sd, cc, sib).wait_recv()
            start(copy(t, 5, sd, cc, sib))
        for t in range(nt):
            _, _, s1, s2, sd = geo[t]
            copy(t, 3, s1, 1 - cc, sib).wait_recv()
            copy(t, 4, s2, 1 - cc, sib).wait_recv()
            copy(t, 5, sd, 1 - cc, sib).wait_recv()
        for cp in started:
            cp.wait_send()

    return pl.pallas_call(
        body, name="gather_big", in_specs=[ANY] * nt, out_specs=[ANY] * nt,
        out_shape=[jax.ShapeDtypeStruct((N_CHIPS, *BIG_SHAPE[n]), BF16) for n in BIG],
        scratch_shapes=[pltpu.SemaphoreType.DMA((6 * nt,)), pltpu.SemaphoreType.DMA((6 * nt,))],
        compiler_params=pltpu.CompilerParams(vmem_limit_bytes=VMEM_LIMIT),
    )(*shards)


def _reduce_big(grads):
    nt = len(BIG)
    nw = 7

    def body(*refs):
        g = refs[:nt]
        fin = refs[nt:2 * nt]
        work = refs[2 * nt:2 * nt + nw * nt]
        send_sems, recv_sems = refs[2 * nt + nw * nt:]
        x, y, cc = lax.axis_index("x"), lax.axis_index("y"), lax.axis_index("c")
        me = 2 * x + y
        sib = (x, y, 1 - cc)
        started = []

        def rcopy(t, k, src, dst, to):
            cp = pltpu.make_async_remote_copy(src_ref=src, dst_ref=dst, send_sem=send_sems.at[5 * t + k],
                                              recv_sem=recv_sems.at[5 * t + k], device_id=to, device_id_type=MESH)
            return cp

        def start(cp):
            cp.start()
            started.append(cp)

        geo = [_neighbours(x, y, X_FIRST[n]) for n in BIG]
        hrs = [BIG_SHAPE[n][0] // 2 for n in BIG]
        wk = lambda t: work[nw * t:nw * (t + 1)]
        one = lambda ref, slot: ref.at[pl.ds(slot, 1)]
        for t in range(nt):
            recv_a = wk(t)[0]
            start(rcopy(t, 0, g[t].at[:, pl.ds((1 - cc) * hrs[t], hrs[t]), :], recv_a, sib))
        for t, n in enumerate(BIG):
            recv_a, p32, p16, r1, qme, qs2, r2 = wk(t)
            n1, n2, s1, s2, sd = geo[t]
            rcopy(t, 0, recv_a, recv_a, sib).wait_recv()
            _pipe(lambda a, b: (a + b, a + b), [g[t].at[:, pl.ds(cc * hrs[t], hrs[t]), :], recv_a], [p32, p16], BIG_TR[n])
            start(rcopy(t, 1, one(p16, s1), one(r1, 0), (*n1, cc)))
            start(rcopy(t, 2, one(p16, sd), one(r1, 1), (*n1, cc)))
        for t, n in enumerate(BIG):
            recv_a, p32, p16, r1, qme, qs2, r2 = wk(t)
            n1, n2, s1, s2, sd = geo[t]
            rcopy(t, 1, one(r1, 0), one(r1, 0), sib).wait_recv()
            rcopy(t, 2, one(r1, 1), one(r1, 1), sib).wait_recv()
            _pipe(lambda a, b: (a + b.astype(F32),), [one(p32, s2), one(r1, 1)], [qs2], BIG_TR[n])
            start(rcopy(t, 3, qs2, r2, (*n2, cc)))
            _pipe(lambda a, b: (a + b.astype(F32),), [one(p32, me), one(r1, 0)], [qme], BIG_TR[n])
        for t, n in enumerate(BIG):
            recv_a, p32, p16, r1, qme, qs2, r2 = wk(t)
            rcopy(t, 3, r2, r2, sib).wait_recv()
            mine = fin[t].at[pl.ds(cc * hrs[t], hrs[t]), :]
            _pipe(lambda a, b: (a + b.astype(F32),), [qme.at[0], r2.at[0]], [mine], BIG_TR[n])
            start(rcopy(t, 4, mine, mine, sib))
        for t in range(nt):
            other = fin[t].at[pl.ds((1 - cc) * hrs[t], hrs[t]), :]
            rcopy(t, 4, other, other, sib).wait_recv()
        for cp in started:
            cp.wait_send()

    outs = [jax.ShapeDtypeStruct(BIG_SHAPE[n], F32) for n in BIG]
    for n in BIG:
        r, c = BIG_SHAPE[n]
        hr = r // 2
        outs += [jax.ShapeDtypeStruct((4, hr, c), F32), jax.ShapeDtypeStruct((4, hr, c), F32),
                 jax.ShapeDtypeStruct((4, hr, c), BF16), jax.ShapeDtypeStruct((2, hr, c), BF16),
                 jax.ShapeDtypeStruct((1, hr, c), F32), jax.ShapeDtypeStruct((1, hr, c), BF16),
                 jax.ShapeDtypeStruct((1, hr, c), BF16)]
    res = pl.pallas_call(
        body, name="reduce_big", in_specs=[ANY] * nt, out_specs=[ANY] * len(outs), out_shape=outs,
        scratch_shapes=[pltpu.SemaphoreType.DMA((5 * nt,)), pltpu.SemaphoreType.DMA((5 * nt,))],
        compiler_params=pltpu.CompilerParams(vmem_limit_bytes=VMEM_LIMIT),
    )(*grads)
    return res[:nt]


def _quarters(names):
    out = []
    for i, n in enumerate(names):
        q = BIG_SHAPE[n][0] // 4
        tr = 128 if q % 128 == 0 else q
        out += [(i, True, 0, q, tr), (i, False, q, q, tr)]
    return out


class _GatherJob:
    def __init__(self, names, shards):
        self.names = names
        self.inputs = list(shards)
        self.out_shapes = [jax.ShapeDtypeStruct((N_CHIPS, *BIG_SHAPE[n]), BF16) for n in names]
        self.ent = _quarters(names)
        self.scratch = [pltpu.SemaphoreType.DMA((6 * len(self.ent),)), pltpu.SemaphoreType.DMA((6 * len(self.ent),))]

    def phases(self, sh, out, scr):
        send_sems, recv_sems = scr
        names, ent = self.names, self.ent
        x, y, cc = lax.axis_index("x"), lax.axis_index("y"), lax.axis_index("c")
        me = 2 * x + y
        sib = (x, y, 1 - cc)
        geo = [_neighbours(x, y, e[1]) for e in ent]
        started = []

        def copy(i, k, slot, pc, to):
            arr, _, roff, rows, _ = ent[i]
            hr = BIG_SHAPE[names[arr]][0] // 2
            ref = out[arr].at[slot, pl.ds(pc * hr + roff, rows), :]
            return pltpu.make_async_remote_copy(src_ref=ref, dst_ref=ref, send_sem=send_sems.at[6 * i + k],
                                                recv_sem=recv_sems.at[6 * i + k], device_id=to, device_id_type=MESH)

        def start(*a):
            copy(*a).start()
            started.append(a)

        def p0():
            for t, n in enumerate(names):
                _pipe(lambda v: (v,), [sh[t]], [out[t].at[me]], BIG_TR[n])
            for i in range(len(ent)):
                n1, n2, _, _, _ = geo[i]
                start(i, 0, me, cc, (*n1, cc))
                start(i, 1, me, cc, (*n2, cc))

        def p1():
            for i in range(len(ent)):
                n1, n2, s1, s2, sd = geo[i]
                copy(i, 0, s1, cc, sib).wait_recv()
                start(i, 2, s1, cc, (*n2, cc))
                start(i, 3, s1, cc, sib)
                copy(i, 1, s2, cc, sib).wait_recv()
                start(i, 4, s2, cc, sib)

        def p2():
            for i in range(len(ent)):
                sd = geo[i][4]
                copy(i, 2, sd, cc, sib).wait_recv()
                start(i, 5, sd, cc, sib)

        def p3():
            for i in range(len(ent)):
                _, _, s1, s2, sd = geo[i]
                copy(i, 3, s1, 1 - cc, sib).wait_recv()
                copy(i, 4, s2, 1 - cc, sib).wait_recv()
                copy(i, 5, sd, 1 - cc, sib).wait_recv()
            for a in started:
                copy(*a).wait_send()

        return [p0, p1, p2, p3]


class _ReduceJob:
    NW = 7

    def __init__(self, names, grads):
        self.names = names
        self.inputs = list(grads)
        self.ent = _quarters(names)
        self.out_shapes = [jax.ShapeDtypeStruct(BIG_SHAPE[n], F32) for n in names]
        for arr, _, _, rows, _ in self.ent:
            c = BIG_SHAPE[names[arr]][1]
            self.out_shapes += [jax.ShapeDtypeStruct((4, rows, c), F32), jax.ShapeDtypeStruct((4, rows, c), F32),
                                jax.ShapeDtypeStruct((4, rows, c), BF16), jax.ShapeDtypeStruct((2, rows, c), BF16),
                                jax.ShapeDtypeStruct((1, rows, c), F32), jax.ShapeDtypeStruct((1, rows, c), BF16),
                                jax.ShapeDtypeStruct((1, rows, c), BF16)]
        self.scratch = [pltpu.SemaphoreType.DMA((5 * len(self.ent),)), pltpu.SemaphoreType.DMA((5 * len(self.ent),))]

    def phases(self, g, outs, scr):
        send_sems, recv_sems = scr
        names, ent, nw = self.names, self.ent, self.NW
        nt = len(names)
        fin, work = outs[:nt], outs[nt:]
        x, y, cc = lax.axis_index("x"), lax.axis_index("y"), lax.axis_index("c")
        me = 2 * x + y
        sib = (x, y, 1 - cc)
        geo = [_neighbours(x, y, e[1]) for e in ent]
        started = []
        wk = lambda i: work[nw * i:nw * (i + 1)]
        one = lambda ref, slot: ref.at[pl.ds(slot, 1)]

        def rows_of(i, pc):
            arr, _, roff, rows, _ = ent[i]
            return pl.ds(pc * (BIG_SHAPE[names[arr]][0] // 2) + roff, rows)

        def rcopy(i, k, src, dst, to):
            return pltpu.make_async_remote_copy(src_ref=src, dst_ref=dst, send_sem=send_sems.at[5 * i + k],
                                                recv_sem=recv_sems.at[5 * i + k], device_id=to, device_id_type=MESH)

        def start(make):
            make().start()
            started.append(make)

        def p0():
            for i, e in enumerate(ent):
                start(lambda i=i, e=e: rcopy(i, 0, g[e[0]].at[:, rows_of(i, 1 - cc), :], wk(i)[0], sib))

        def p1():
            for i, e in enumerate(ent):
                recv_a, p32, p16, r1 = wk(i)[:4]
                n1, n2, s1, s2, sd = geo[i]
                rcopy(i, 0, recv_a, recv_a, sib).wait_recv()
                _pipe(lambda a, b: (a + b, a + b), [g[e[0]].at[:, rows_of(i, cc), :], recv_a], [p32, p16], e[4])
                start(lambda i=i, s1=s1, n1=n1: rcopy(i, 1, one(wk(i)[2], s1), one(wk(i)[3], 0), (*n1, cc)))
                start(lambda i=i, sd=sd, n1=n1: rcopy(i, 2, one(wk(i)[2], sd), one(wk(i)[3], 1), (*n1, cc)))

        def p2():
            for i, e in enumerate(ent):
                _, p32, _, r1, qme, qs2, r2 = wk(i)
                n1, n2, s1, s2, sd = geo[i]
                rcopy(i, 1, one(r1, 0), one(r1, 0), sib).wait_recv()
                rcopy(i, 2, one(r1, 1), one(r1, 1), sib).wait_recv()
                _pipe(lambda a, b: (a + b.astype(F32),), [one(p32, s2), one(r1, 1)], [qs2], e[4])
                start(lambda i=i, n2=n2: rcopy(i, 3, wk(i)[5], wk(i)[6], (*n2, cc)))
                _pipe(lambda a, b: (a + b.astype(F32),), [one(p32, me), one(r1, 0)], [qme], e[4])

        def p3():
            for i, e in enumerate(ent):
                qme, r2 = wk(i)[4], wk(i)[6]
                rcopy(i, 3, r2, r2, sib).wait_recv()
                mine = fin[e[0]].at[rows_of(i, cc), :]
                _pipe(lambda a, b: (a + b.astype(F32),), [qme.at[0], r2.at[0]], [mine], e[4])
                start(lambda i=i, e=e: rcopy(i, 4, fin[e[0]].at[rows_of(i, cc), :], fin[e[0]].at[rows_of(i, cc), :], sib))

        def p4():
            for i, e in enumerate(ent):
                other = fin[e[0]].at[rows_of(i, 1 - cc), :]
                rcopy(i, 4, other, other, sib).wait_recv()
            for make in started:
                make().wait_send()

        return [p0, p1, p2, p3, p4]


def _run_job(job, name):
    ni, no = len(job.inputs), len(job.out_shapes)

    def body(*refs):
        for ph in job.phases(refs[:ni], refs[ni:ni + no], refs[ni + no:]):
            ph()

    return pl.pallas_call(
        body, name=name, in_specs=[ANY] * ni, out_specs=[ANY] * no, out_shape=job.out_shapes, scratch_shapes=job.scratch,
        compiler_params=pltpu.CompilerParams(vmem_limit_bytes=VMEM_LIMIT),
    )(*job.inputs)


def _hosted(body, *, name, grid, in_specs, out_specs, out_shape, scratch_shapes, args, sem, side=None):
    if side is None:
        return pl.pallas_call(body, name=name, grid=grid, in_specs=in_specs, out_specs=out_specs, out_shape=out_shape,
                              scratch_shapes=scratch_shapes, compiler_params=_cp(sem))(*args), None
    job = side
    ni, no, ns = len(in_specs), len(out_specs), len(scratch_shapes)
    ji, jo = len(job.inputs), len(job.out_shapes)
    n_steps = 1
    for extent in grid:
        n_steps *= extent

    def wrapped(*refs):
        own_in, refs = refs[:ni], refs[ni:]
        job_in, refs = refs[:ji], refs[ji:]
        own_out, refs = refs[:no], refs[no:]
        job_out, refs = refs[:jo], refs[jo:]
        own_scr, job_scr = refs[:ns], refs[ns:]
        step = 0
        for d, extent in enumerate(grid):
            step = step * extent + pl.program_id(d)
        phases = job.phases(job_in, job_out, job_scr)
        steps = [(k * n_steps) // len(phases) for k in range(len(phases) - 1)] + [n_steps - 1]
        for at, ph in zip(steps, phases):
            pl.when(step == at)(ph)
        body(*own_in, *own_out, *own_scr)

    res = pl.pallas_call(
        wrapped, name=name, grid=grid, in_specs=list(in_specs) + [ANY] * ji, out_specs=list(out_specs) + [ANY] * jo,
        out_shape=list(out_shape) + list(job.out_shapes), scratch_shapes=list(scratch_shapes) + list(job.scratch),
        compiler_params=_cp(("arbitrary",) * len(grid)),
    )(*args, *job.inputs)
    return res[:no], res[no:]


def _proj_dw(xn, dproj_sh, *, tm=512, tk=1024):
    s, d = xn.shape
    tk = _tile(s, tk)
    nk = s // tk

    def body(a_ref, b_ref, o_ref, acc):
        kk = pl.program_id(2)
        part = _dot_tn(a_ref[...], b_ref[0])

        @pl.when(kk == 0)
        def _():
            acc[...] = part

        @pl.when(kk > 0)
        def _():
            acc[...] += part

        @pl.when(kk == nk - 1)
        def _():
            o_ref[0] = acc[...]

    return pl.pallas_call(
        body, name="proj_dw", grid=(N_CHIPS, d // tm, nk),
        in_specs=[pl.BlockSpec((tk, tm), lambda j, i, q: (q, i)), pl.BlockSpec((1, tk, W_IN_PAD), lambda j, i, q: (j, q, 0))],
        out_specs=pl.BlockSpec((1, tm, W_IN_PAD), lambda j, i, q: (j, i, 0)),
        out_shape=jax.ShapeDtypeStruct((N_CHIPS, d, W_IN_PAD), F32), scratch_shapes=[pltpu.VMEM((tm, W_IN_PAD), F32)],
        compiler_params=_cp(("parallel", "parallel", "arbitrary")),
    )(xn, dproj_sh)


def _proj_dx(dproj_sh, w_sh, *, tm=1024):
    s = dproj_sh.shape[1]
    d = w_sh.shape[1]
    tm = _tile(s, tm)

    def body(a_ref, b_ref, o_ref, acc):
        kk = pl.program_id(1)
        part = _dot_nt(a_ref[0], b_ref[0])

        @pl.when(kk == 0)
        def _():
            acc[...] = part

        @pl.when(kk > 0)
        def _():
            acc[...] += part

        @pl.when(kk == N_CHIPS - 1)
        def _():
            o_ref[...] = acc[...]

    return pl.pallas_call(
        body, name="proj_dx", grid=(s // tm, N_CHIPS),
        in_specs=[pl.BlockSpec((1, tm, W_IN_PAD), lambda i, q: (q, i, 0)), pl.BlockSpec((1, d, W_IN_PAD), lambda i, q: (q, 0, 0))],
        out_specs=pl.BlockSpec((tm, d), lambda i, q: (i, 0)),
        out_shape=jax.ShapeDtypeStruct((s, d), F32), scratch_shapes=[pltpu.VMEM((tm, d), F32)],
        compiler_params=_cp(("parallel", "arbitrary")),
    )(dproj_sh, w_sh)


def _up_dx(dup, w_sh, *, tm=1024):
    s = dup.shape[1]
    d, wsh = w_sh.shape[1:]
    tm = _tile(s, tm)

    def body(a_ref, b_ref, o_ref, acc):
        kk = pl.program_id(1)
        part = _dot_nt(a_ref[0], b_ref[0])

        @pl.when(kk == 0)
        def _():
            acc[...] = part

        @pl.when(kk > 0)
        def _():
            acc[...] += part

        @pl.when(kk == N_CHIPS - 1)
        def _():
            o_ref[...] = acc[...]

    return pl.pallas_call(
        body, name="up_dx", grid=(s // tm, N_CHIPS),
        in_specs=[pl.BlockSpec((1, tm, wsh), lambda i, q: (q >> 1, i, q & 1)), pl.BlockSpec((1, d, wsh), lambda i, q: (q, 0, 0))],
        out_specs=pl.BlockSpec((tm, d), lambda i, q: (i, 0)),
        out_shape=jax.ShapeDtypeStruct((s, d), F32), scratch_shapes=[pltpu.VMEM((tm, d), F32)],
        compiler_params=_cp(("parallel", "arbitrary")),
    )(dup, w_sh)


def _up_dw(hn, dup, *, tk=1024):
    s, d = hn.shape
    wsh = 2 * D_FF // N_CHIPS
    tk = _tile(s, tk)
    nk = s // tk

    def body(a_ref, b_ref, o_ref, acc):
        kk = pl.program_id(1)
        part = _dot_tn(a_ref[...], b_ref[0])

        @pl.when(kk == 0)
        def _():
            acc[...] = part

        @pl.when(kk > 0)
        def _():
            acc[...] += part

        @pl.when(kk == nk - 1)
        def _():
            o_ref[0] = acc[...]

    return pl.pallas_call(
        body, name="up_dw", grid=(N_CHIPS, nk),
        in_specs=[pl.BlockSpec((tk, d), lambda j, q: (q, 0)), pl.BlockSpec((1, tk, wsh), lambda j, q: (j >> 1, q, j & 1))],
        out_specs=pl.BlockSpec((1, d, wsh), lambda j, q: (j, 0, 0)),
        out_shape=jax.ShapeDtypeStruct((N_CHIPS, d, wsh), F32), scratch_shapes=[pltpu.VMEM((d, wsh), F32)],
        compiler_params=_cp(("parallel", "arbitrary")),
    )(hn, dup)


BIG_ROWS =(IN_DIM // 4, Q_DIM // 4, D_INNER // 4, D_MODEL // 4, 2 * D_FF // 4, D_FF // 4)
PACK_ROWS = 5376


def _pack_shards(parts):
    rows = [p.reshape(-1, D_MODEL) for p in parts]
    pad = PACK_ROWS - sum(BIG_ROWS)
    return jnp.concatenate(rows + [jnp.zeros((pad, D_MODEL), rows[0].dtype)], axis=0)


def _unpack_shards(buf):
    out, off = [], 0
    for n in BIG_ROWS:
        out.append(buf[off:off + n])
        off += n
    return out


def _permute_cols_in(w):
    pad = jnp.zeros((w.shape[0], PW - IN_DIM), w.dtype)
    return jnp.concatenate([w[:, :6656], w[:, 6688:], w[:, 6656:6688], pad], axis=1)


def _unpermute_cols_in(g):
    return jnp.concatenate([g[:, :6656], g[:, O_DT:O_DT + 32], g[:, 6656:O_DT]], axis=1)


SMALL = ("norm1_w", "b_gate", "attn_sinks", "ssd_conv_b", "dt_bias", "a_log", "d_skip", "ssd_norm_w", "norm2_w",
         "ffn_conv_b", "final_norm_w", "ssd_conv_w", "ffn_conv_w")


def _pad128(v):
    v = v.reshape(-1)
    return jnp.pad(v, (0, (-v.shape[0]) % 128))


def _pack_small(parts):
    flat = jnp.concatenate([_pad128(p) for p in parts])
    flat = jnp.pad(flat, (0, (-flat.shape[0]) % 1024))
    return flat.reshape(-1, 128)


def _unpack_small(buf, shapes):
    flat, out, off = buf.reshape(-1), [], 0
    for shp in shapes:
        n = 1
        for q in shp:
            n *= q
        out.append(flat[off:off + n].reshape(shp))
        off += n + (-n) % 128
    return out


def _vec128(v):
    return jnp.pad(v.reshape(1, -1), ((0, 0), (0, 128 - v.shape[-1])))


def kernel(x, norm1_w, w_in, b_gate, attn_sinks, w_attn_o, ssd_conv_w, ssd_conv_b, dt_bias, a_log, d_skip, ssd_norm_w, w_ssd_o, w_out, norm2_w, w_up, ffn_conv_w, ffn_conv_b, w_down, final_norm_w, loss_target, m_norm1_w, m_w_in, m_b_gate, m_attn_sinks, m_w_attn_o, m_ssd_conv_w, m_ssd_conv_b, m_dt_bias, m_a_log, m_d_skip, m_ssd_norm_w, m_w_ssd_o, m_w_out, m_norm2_w, m_w_up, m_ffn_conv_w, m_ffn_conv_b, m_w_down, m_final_norm_w, v_norm1_w, v_w_in, v_b_gate, v_attn_sinks, v_w_attn_o, v_ssd_conv_w, v_ssd_conv_b, v_dt_bias, v_a_log, v_d_skip, v_ssd_norm_w, v_w_ssd_o, v_w_out, v_norm2_w, v_w_up, v_ffn_conv_w, v_ffn_conv_b, v_w_down, v_final_norm_w):
    ix, iy, ic = lax.axis_index("x"), lax.axis_index("y"), lax.axis_index("c")
    chip = 2 * ix + iy
    x2 = x[0]
    tgt = loss_target[0]
    s = x2.shape[0]

    wsh = IN_DIM // N_CHIPS
    big_shards = dict(w_in=jnp.pad(w_in[0], ((0, 0), (0, W_IN_PAD - wsh))), w_attn_o=w_attn_o[0], w_ssd_o=w_ssd_o[0],
                      w_out=w_out[0], w_up=w_up[0], w_down=w_down[0])
    gathered = {}
    (gathered["w_in"],) = _run_job(_GatherJob(("w_in",), [big_shards["w_in"]]), "gather_w_in")
    early, late = ("w_attn_o", "w_ssd_o", "w_out"), ("w_up", "w_down")
    gather_early = _GatherJob(early, [big_shards[n] for n in early])
    gather_late = _GatherJob(late, [big_shards[n] for n in late])
    w_in_p = _permute_cols_in(jnp.concatenate([gathered["w_in"][j, :, :wsh] for j in range(N_CHIPS)], axis=1))
    small_sh = _pack_small([ssd_conv_w[0], ffn_conv_w[0]])
    small_all = _all_gather_small(small_sh)
    sc_parts = [_unpack_small(small_all[j], [(4, XBC_DIM // 4), (3, 2 * D_FF // 4)]) for j in range(N_CHIPS)]
    ssd_cw = jnp.concatenate([p[0] for p in sc_parts], axis=1)
    ffn_cw = jnp.concatenate([p[1] for p in sc_parts], axis=1)

    sinks128 = _vec128(attn_sinks)
    dtb128, alog128, dskip128 = _vec128(dt_bias), _vec128(a_log), _vec128(d_skip)

    xn = _rms_fwd(x2, norm1_w, name="norm1_fwd")
    proj, got = _mm(xn, w_in_p, name="proj_fwd", tn=1280, side=gather_early)
    gathered.update(zip(early, got))
    attn_pre, got = _attn_fwd(proj, sinks128, side=gather_late)
    gathered.update(zip(late, got))
    full = {n: gathered[n].reshape(-1, D_MODEL) for n in ("w_attn_o", "w_ssd_o", "w_out", "w_down")}
    full["w_up"] = gathered["w_up"]
    attn = _mm(attn_pre, full["w_attn_o"], name="attn_o_fwd")
    xbc = _ssd_conv_fwd(proj, ssd_cw, ssd_conv_b)
    y_ssd, hprev = _ssd_fwd(xbc, proj, dtb128, alog128, dskip128)
    yn = _gate_norm_fwd(y_ssd, proj, ssd_norm_w)
    ssd_out = _mm(yn, full["w_ssd_o"], name="ssd_o_fwd")
    merged = _merge_fwd(proj, b_gate, attn, ssd_out)
    h1 = _mm(merged, full["w_out"], name="out_fwd", resid=x2)
    hn = _rms_fwd(h1, norm2_w, name="norm2_fwd")
    up = _mm(hn, full["w_up"], name="up_fwd")
    act = _ffn_act_fwd(up, ffn_cw, ffn_conv_b)
    h2 = _mm(act, full["w_down"], name="down_fwd", resid=h1, tk=1408)

    dh2, loss_blk, g_final = _loss_bwd(h2, tgt, final_norm_w.reshape(1, -1))
    dact = _mm(dh2, full["w_down"], name="down_dx", tb=True, tn=1408)
    g_down = _mm(act, dh2, name="down_dw", ta=True, tm=1408)
    dup, g_ffn_cw, g_ffn_cb = _ffn_act_bwd(dact, up, ffn_cw, ffn_conv_b)
    dhn = _up_dx(dup, full["w_up"])
    g_up = _up_dw(hn, dup)
    dh1, g_norm2 = _rms_bwd(dhn, h1, norm2_w, dh2, name="norm2_bwd")
    dmerged = _mm(dh1, full["w_out"], name="out_dx", tb=True)
    g_out = _mm(merged, dh1, name="out_dw", ta=True)
    dattn, dssd_out, dga, dgs, g_ba, g_bs = _merge_bwd(dmerged, proj, b_gate, attn, ssd_out)
    dyn = _mm(dssd_out, full["w_ssd_o"], name="ssd_o_dx", tb=True)
    g_ssd_o = _mm(yn, dssd_out, name="ssd_o_dw", ta=True)
    dy_ssd, dz, g_ssd_norm = _gate_norm_bwd(dyn, y_ssd, proj, ssd_norm_w)
    slot = lambda g: g.reshape(N_CHIPS, -1, D_MODEL)
    big_grads = {}
    red = ("w_down", "w_up")
    (dxbc, ddt, dvec), got = _ssd_bwd(xbc, proj, dtb128, alog128, dskip128, hprev, dy_ssd,
                                      side=_ReduceJob(red, [slot(g_down), g_up]))
    big_grads.update(zip(red, got))
    dxbc_raw, g_ssd_cw, g_ssd_cb = _ssd_conv_bwd(dxbc, proj, ssd_cw, ssd_conv_b)
    dattn_pre = _mm(dattn, full["w_attn_o"], name="attn_o_dx", tb=True)
    g_attn_o = _mm(attn_pre, dattn, name="attn_o_dw", ta=True)
    red = ("w_out", "w_ssd_o", "w_attn_o")
    (dq, dk, dv, dsk), got = _attn_bwd(proj, sinks128, attn_pre, dattn_pre,
                                       side=_ReduceJob(red, [slot(g_out), slot(g_ssd_o), slot(g_attn_o)]))
    big_grads.update(zip(red, got))
    dproj = jnp.concatenate([dq, dk, dv, dz, dxbc_raw, ddt[:, :N_SSD_HEADS], dga, dgs], axis=1)
    dproj_sh = jnp.pad(dproj.reshape(s, N_CHIPS, wsh).transpose(1, 0, 2), ((0, 0), (0, 0), (0, W_IN_PAD - wsh)))
    dxn = _proj_dx(dproj_sh, gathered["w_in"])
    g_in = _proj_dw(xn, dproj_sh)
    dx, g_norm1 = _rms_bwd(dxn, x2, norm1_w, dh1, name="norm1_bwd")

    big_grads["w_in"] = _run_job(_ReduceJob(("w_in",), [g_in]), "reduce_w_in")[0][:, :wsh]

    small_g = dict(
        norm1_w=g_norm1, b_gate=jnp.concatenate([g_ba, g_bs], axis=1), attn_sinks=dsk[0:1, :16], ssd_conv_b=g_ssd_cb,
        dt_bias=dvec[0:1, :32], a_log=dvec[1:2, :32], d_skip=dvec[2:3, :32], ssd_norm_w=g_ssd_norm, norm2_w=g_norm2,
        ffn_conv_b=jnp.concatenate([g_ffn_cb[0], g_ffn_cb[1]], axis=1), final_norm_w=g_final, ssd_conv_w=g_ssd_cw,
        ffn_conv_w=jnp.concatenate([g_ffn_cw[0], g_ffn_cw[1]], axis=1))
    small_buf = _pack_small([small_g[n] for n in SMALL] + [loss_blk])
    small_sum = _all_reduce_small(small_buf)
    small_shapes = [(1, D_MODEL), (1, 2 * D_MODEL), (1, 16), (1, XBC_DIM), (1, 32), (1, 32), (1, 32), (1, D_INNER),
                    (1, D_MODEL), (1, 2 * D_FF), (D_MODEL,), (4, XBC_DIM), (3, 2 * D_FF), (1, 128)]
    small_list = _unpack_small(small_sum, small_shapes)
    loss = small_list[-1][0, 0]
    grads = dict(zip(SMALL, small_list[:-1]))
    grads["ssd_conv_w"] = lax.dynamic_slice_in_dim(grads["ssd_conv_w"], chip * (XBC_DIM // 4), XBC_DIM // 4, axis=1)
    grads["ffn_conv_w"] = lax.dynamic_slice_in_dim(grads["ffn_conv_w"], chip * (2 * D_FF // 4), 2 * D_FF // 4, axis=1)
    grads.update(big_grads)

    weights = dict(norm1_w=norm1_w, w_in=w_in, b_gate=b_gate, attn_sinks=attn_sinks, w_attn_o=w_attn_o, ssd_conv_w=ssd_conv_w,
                   ssd_conv_b=ssd_conv_b, dt_bias=dt_bias, a_log=a_log, d_skip=d_skip, ssd_norm_w=ssd_norm_w, w_ssd_o=w_ssd_o,
                   w_out=w_out, norm2_w=norm2_w, w_up=w_up, ffn_conv_w=ffn_conv_w, ffn_conv_b=ffn_conv_b, w_down=w_down,
                   final_norm_w=final_norm_w)
    ms = dict(norm1_w=m_norm1_w, w_in=m_w_in, b_gate=m_b_gate, attn_sinks=m_attn_sinks, w_attn_o=m_w_attn_o,
              ssd_conv_w=m_ssd_conv_w, ssd_conv_b=m_ssd_conv_b, dt_bias=m_dt_bias, a_log=m_a_log, d_skip=m_d_skip,
              ssd_norm_w=m_ssd_norm_w, w_ssd_o=m_w_ssd_o, w_out=m_w_out, norm2_w=m_norm2_w, w_up=m_w_up,
              ffn_conv_w=m_ffn_conv_w, ffn_conv_b=m_ffn_conv_b, w_down=m_w_down, final_norm_w=m_final_norm_w)
    vs = dict(norm1_w=v_norm1_w, w_in=v_w_in, b_gate=v_b_gate, attn_sinks=v_attn_sinks, w_attn_o=v_w_attn_o,
              ssd_conv_w=v_ssd_conv_w, ssd_conv_b=v_ssd_conv_b, dt_bias=v_dt_bias, a_log=v_a_log, d_skip=v_d_skip,
              ssd_norm_w=v_ssd_norm_w, w_ssd_o=v_w_ssd_o, w_out=v_w_out, norm2_w=v_norm2_w, w_up=v_w_up,
              ffn_conv_w=v_ffn_conv_w, ffn_conv_b=v_ffn_conv_b, w_down=v_w_down, final_norm_w=v_final_norm_w)
    order = list(weights)
    deltas, new_m, new_v = {}, {}, {}
    for n in BIG:
        shp = weights[n].shape
        d_, m_, v_ = _adamw(weights[n][0], grads[n], ms[n][0], vs[n][0], name="adamw_" + n)
        deltas[n], new_m[n], new_v[n] = d_.reshape(shp), m_.reshape(shp), v_.reshape(shp)
    smalls = [n for n in order if n not in BIG]
    sw = _pack_small([weights[n] for n in smalls])
    sg = _pack_small([grads[n] for n in smalls])
    sm = _pack_small([ms[n] for n in smalls])
    sv = _pack_small([vs[n] for n in smalls])
    sd_, sm_, sv_ = _adamw(sw, sg, sm, sv, name="adamw_small")
    shapes = [weights[n].shape for n in smalls]
    for n, d_, m_, v_ in zip(smalls, _unpack_small(sd_, shapes), _unpack_small(sm_, shapes), _unpack_small(sv_, shapes)):
        deltas[n], new_m[n], new_v[n] = d_, m_, v_
    out_grads = [grads[n].reshape(weights[n].shape) for n in order]
    return (loss, dx[None], *out_grads, *[deltas[n] for n in order], *[new_m[n] for n in order], *[new_v[n] for n in order])
```

```python
import functools

import jax
import jax.numpy as jnp
from jax import lax
from jax.experimental import pallas as pl
from jax.experimental.pallas import tpu as pltpu

F32 = jnp.float32
BF16 = jnp.bfloat16
HI = lax.Precision.HIGHEST

D_MODEL = 1024
Q_DIM = 1024
KV_DIM = 256
D_INNER = 2048
BC_DIM = 512
XBC_DIM = 3072
N_SSD_HEADS = 32
D_FF = 2816
IN_DIM = 8736
BLK = 128
EPS = 1e-5
NEG = -1e30

O_Q, O_K, O_V, O_Z, O_X, O_GA, O_GS, O_DT = 0, 1024, 1280, 1536, 3584, 6656, 7680, 8704
PW = 8960

ADAM_LR, ADAM_B1, ADAM_B2, ADAM_EPS, ADAM_WD, ADAM_STEP = 0.001, 0.9, 0.999, 1e-08, 0.01, 10

VMEM_LIMIT = 52 * 1024 * 1024
MESH = pl.DeviceIdType.MESH


def _cp(sem=None):
    return pltpu.CompilerParams(dimension_semantics=sem, vmem_limit_bytes=VMEM_LIMIT)


def _dot(a, b, prec=None):
    return jnp.dot(a, b, preferred_element_type=F32, precision=prec)


def _dot_nt(a, b, prec=None):
    return lax.dot_general(a, b, (((1,), (1,)), ((), ())), preferred_element_type=F32, precision=prec)


def _dot_tn(a, b, prec=None):
    return lax.dot_general(a, b, (((0,), (0,)), ((), ())), preferred_element_type=F32, precision=prec)


def _sigmoid(x):
    return 0.5 * jnp.tanh(0.5 * x) + 0.5


def _tile(n, want):
    t = min(n, want)
    while n % t:
        t -= 128
    return t


def _mm(a, b, *, name, ta=False, tb=False, out_dtype=F32, resid=None, tm=1024, tn=1024, tk=1024, side=None):
    m, k = (a.shape[1], a.shape[0]) if ta else a.shape
    slots = b.ndim == 3
    if slots:
        n = b.shape[1] if tb else b.shape[0] * b.shape[2]
        tn, tk = (tn, b.shape[2]) if tb else (b.shape[2], tk)
    else:
        n = b.shape[0] if tb else b.shape[1]
    tm, tn, tk = _tile(m, tm), _tile(n, tn), _tile(k, tk)
    nk = k // tk
    dn = (((0 if ta else 1,), (1 if tb else 0,)), ((), ()))

    def body(*refs):
        if resid is None:
            a_ref, b_ref, o_ref, acc = refs
        else:
            a_ref, b_ref, r_ref, o_ref, acc = refs
        kk = pl.program_id(2)
        bv = b_ref[0] if slots else b_ref[...]
        part = lax.dot_general(a_ref[...].astype(BF16), bv.astype(BF16), dn, preferred_element_type=F32)

        @pl.when(kk == 0)
        def _():
            acc[...] = part

        @pl.when(kk > 0)
        def _():
            acc[...] += part

        @pl.when(kk == nk - 1)
        def _():
            r = acc[...]
            if resid is not None:
                r = r + r_ref[...]
            o_ref[...] = r.astype(out_dtype)

    a_spec = pl.BlockSpec((tk, tm), lambda i, j, q: (q, i)) if ta else pl.BlockSpec((tm, tk), lambda i, j, q: (i, q))
    if slots:
        b_spec = (pl.BlockSpec((1, tn, tk), lambda i, j, q: (q, j, 0)) if tb
                  else pl.BlockSpec((1, tk, tn), lambda i, j, q: (j, q, 0)))
    else:
        b_spec = pl.BlockSpec((tn, tk), lambda i, j, q: (j, q)) if tb else pl.BlockSpec((tk, tn), lambda i, j, q: (q, j))
    o_spec = pl.BlockSpec((tm, tn), lambda i, j, q: (i, j))
    ins, specs = [a, b], [a_spec, b_spec]
    if resid is not None:
        ins.append(resid)
        specs.append(o_spec)
    own, extra = _hosted(
        body, name=name, grid=(m // tm, n // tn, nk), in_specs=specs, out_specs=[o_spec],
        out_shape=[jax.ShapeDtypeStruct((m, n), out_dtype)], scratch_shapes=[pltpu.VMEM((tm, tn), F32)],
        args=ins, sem=("parallel", "parallel", "arbitrary"), side=side)
    return own[0] if side is None else (own[0], extra)


def _rms_fwd(x, w, *, name, tm=512):
    s, d = x.shape
    tm = _tile(s, tm)

    def body(x_ref, w_ref, o_ref):
        xv = x_ref[...]
        r = lax.rsqrt(jnp.mean(xv * xv, axis=-1, keepdims=True) + EPS)
        o_ref[...] = ((xv * r) * w_ref[...]).astype(BF16)

    return pl.pallas_call(
        body, name=name, grid=(s // tm,),
        in_specs=[pl.BlockSpec((tm, d), lambda i: (i, 0)), pl.BlockSpec((1, d), lambda i: (0, 0))],
        out_specs=pl.BlockSpec((tm, d), lambda i: (i, 0)),
        out_shape=jax.ShapeDtypeStruct((s, d), BF16), compiler_params=_cp(("parallel",)),
    )(x, w)


def _rms_bwd(dy, x, w, resid, *, name, tm=512):
    s, d = x.shape
    tm = _tile(s, tm)

    def body(dy_ref, x_ref, w_ref, r_ref, dx_ref, dw_ref):
        i = pl.program_id(0)
        xv = x_ref[...]
        r = lax.rsqrt(jnp.mean(xv * xv, axis=-1, keepdims=True) + EPS)
        xh = xv * r
        dyv = dy_ref[...]
        g = dyv * w_ref[...]
        dx_ref[...] = r_ref[...] + r * (g - xh * jnp.mean(g * xh, axis=-1, keepdims=True))
        part = jnp.sum(dyv * xh, axis=0, keepdims=True)

        @pl.when(i == 0)
        def _():
            dw_ref[...] = part

        @pl.when(i > 0)
        def _():
            dw_ref[...] += part

    row = pl.BlockSpec((tm, d), lambda i: (i, 0))
    vec = pl.BlockSpec((1, d), lambda i: (0, 0))
    return pl.pallas_call(
        body, name=name, grid=(s // tm,), in_specs=[row, row, vec, row], out_specs=[row, vec],
        out_shape=[jax.ShapeDtypeStruct((s, d), F32), jax.ShapeDtypeStruct((1, d), F32)],
        compiler_params=_cp(("arbitrary",)),
    )(dy, x, w, resid)


def _loss_bwd(h2, tgt, wf, *, tm=512):
    s, d = h2.shape
    tm = _tile(s, tm)

    def body(h_ref, t_ref, w_ref, dh_ref, loss_ref, dw_ref):
        i = pl.program_id(0)
        hv = h_ref[...]
        r = lax.rsqrt(jnp.mean(hv * hv, axis=-1, keepdims=True) + EPS)
        xh = hv * r
        wv = w_ref[...]
        e = xh * wv - t_ref[...]
        lpart = 0.5 * jnp.sum(jnp.mean(e * e, axis=-1, keepdims=True), axis=0, keepdims=True)
        dout = e * (1.0 / d)
        g = dout * wv
        dh_ref[...] = r * (g - xh * jnp.mean(g * xh, axis=-1, keepdims=True))
        part = jnp.sum(dout * xh, axis=0, keepdims=True)
        lrow = jnp.broadcast_to(lpart, (1, 128))

        @pl.when(i == 0)
        def _():
            dw_ref[...] = part
            loss_ref[...] = lrow

        @pl.when(i > 0)
        def _():
            dw_ref[...] += part
            loss_ref[...] += lrow

    row = pl.BlockSpec((tm, d), lambda i: (i, 0))
    vec = pl.BlockSpec((1, d), lambda i: (0, 0))
    return pl.pallas_call(
        body, name="loss_bwd", grid=(s // tm,), in_specs=[row, row, vec],
        out_specs=[row, pl.BlockSpec((1, 128), lambda i: (0, 0)), vec],
        out_shape=[jax.ShapeDtypeStruct((s, d), F32), jax.ShapeDtypeStruct((1, 128), F32),
                   jax.ShapeDtypeStruct((1, d), F32)],
        compiler_params=_cp(("arbitrary",)),
    )(h2, tgt, wf)


def _attn_mask(n):
    qi = lax.broadcasted_iota(jnp.int32, (4 * BLK, 2 * BLK), 0) & (BLK - 1)
    si = lax.broadcasted_iota(jnp.int32, (4 * BLK, 2 * BLK), 1)
    dist = BLK + qi - si
    kpos = n * BLK - BLK + si
    return (dist >= 0) & (dist < BLK) & (kpos >= 0)


def _attn_probs(q_ref, kc_ref, kp_ref, sk_ref, kvh, valid):
    hs = slice(kvh * 64, (kvh + 1) * 64)
    kb = jnp.concatenate([kp_ref[:, hs], kc_ref[:, hs]], axis=0).astype(BF16)
    qs = jnp.concatenate([q_ref[:, (kvh * 4 + g) * 64:(kvh * 4 + g + 1) * 64] for g in range(4)], axis=0).astype(BF16)
    s = _dot_nt(qs, kb) * 0.125
    s = jnp.where(valid, s, NEG)
    sink = jnp.concatenate(
        [jnp.broadcast_to(sk_ref[0:1, kvh * 4 + g:kvh * 4 + g + 1], (BLK, 1)) for g in range(4)], axis=0)
    m = jnp.maximum(jnp.max(s, axis=1, keepdims=True), sink)
    p = jnp.where(valid, jnp.exp(s - m), 0.0)
    es = jnp.exp(sink - m)
    denom = jnp.sum(p, axis=1, keepdims=True) + es
    return qs, kb, p / denom, es / denom


def _attn_fwd(proj, sinks, side=None):
    s = proj.shape[0]
    nb = s // BLK

    def body(q_ref, kc_ref, kp_ref, vc_ref, vp_ref, sk_ref, o_ref):
        valid = _attn_mask(pl.program_id(0))
        for kvh in range(4):
            hs = slice(kvh * 64, (kvh + 1) * 64)
            _, _, probs, _ = _attn_probs(q_ref, kc_ref, kp_ref, sk_ref, kvh, valid)
            vb = jnp.concatenate([vp_ref[:, hs], vc_ref[:, hs]], axis=0).astype(BF16)
            o = _dot(probs.astype(BF16), vb)
            for g in range(4):
                h = kvh * 4 + g
                o_ref[:, h * 64:(h + 1) * 64] = o[g * BLK:(g + 1) * BLK].astype(BF16)

    prev = lambda n: jnp.maximum(n - 1, 0)
    own, extra = _hosted(
        body, name="attn_fwd", grid=(nb,),
        in_specs=[pl.BlockSpec((BLK, Q_DIM), lambda n: (n, 0)),
                  pl.BlockSpec((BLK, KV_DIM), lambda n: (n, O_K // KV_DIM)),
                  pl.BlockSpec((BLK, KV_DIM), lambda n: (prev(n), O_K // KV_DIM)),
                  pl.BlockSpec((BLK, KV_DIM), lambda n: (n, O_V // KV_DIM)),
                  pl.BlockSpec((BLK, KV_DIM), lambda n: (prev(n), O_V // KV_DIM)),
                  pl.BlockSpec((1, 128), lambda n: (0, 0))],
        out_specs=[pl.BlockSpec((BLK, Q_DIM), lambda n: (n, 0))],
        out_shape=[jax.ShapeDtypeStruct((s, Q_DIM), BF16)], scratch_shapes=[],
        args=(proj, proj, proj, proj, proj, sinks), sem=("parallel",), side=side)
    return own[0] if side is None else (own[0], extra)


def _attn_bwd(proj, sinks, o, do, side=None):
    s = proj.shape[0]
    nb = s // BLK

    def body(q_ref, kc_ref, kp_ref, vc_ref, vp_ref, sk_ref, o_ref, do_ref,
             dq_ref, dk_ref, dv_ref, dsk_ref, ck, cv, nkp, nkc, nvp, nvc):
        n = pl.program_id(0)

        @pl.when(n == 0)
        def _():
            ck[...] = jnp.zeros_like(ck)
            cv[...] = jnp.zeros_like(cv)
            dsk_ref[...] = jnp.zeros_like(dsk_ref)

        @pl.when(n < nb)
        def _():
            valid = _attn_mask(n)
            lane = lax.broadcasted_iota(jnp.int32, (1, 128), 1)
            dsk = jnp.zeros((1, 128), F32)
            for kvh in range(4):
                hs = slice(kvh * 64, (kvh + 1) * 64)
                qs, kb, probs, psink = _attn_probs(q_ref, kc_ref, kp_ref, sk_ref, kvh, valid)
                vb = jnp.concatenate([vp_ref[:, hs], vc_ref[:, hs]], axis=0).astype(BF16)
                heads = [slice((kvh * 4 + g) * 64, (kvh * 4 + g + 1) * 64) for g in range(4)]
                dos = jnp.concatenate([do_ref[:, hh] for hh in heads], axis=0)
                os_ = jnp.concatenate([o_ref[:, hh] for hh in heads], axis=0).astype(F32)
                delta = jnp.sum(dos * os_, axis=1, keepdims=True)
                dos16 = dos.astype(BF16)
                dp = _dot_nt(dos16, vb)
                ds = (probs * (dp - delta) * 0.125).astype(BF16)
                dqs = _dot(ds, kb)
                dkb = _dot_tn(ds, qs)
                dvb = _dot_tn(probs.astype(BF16), dos16)
                nkp[:, hs] = dkb[:BLK]
                nkc[:, hs] = dkb[BLK:]
                nvp[:, hs] = dvb[:BLK]
                nvc[:, hs] = dvb[BLK:]
                sd = psink * delta
                for g in range(4):
                    dq_ref[:, heads[g]] = dqs[g * BLK:(g + 1) * BLK].astype(BF16)
                    val = -jnp.sum(sd[g * BLK:(g + 1) * BLK], axis=0, keepdims=True)
                    dsk = dsk + jnp.where(lane == kvh * 4 + g, val, 0.0)
            dsk_ref[0:1, :] += dsk
            dk_ref[...] = (ck[...] + nkp[...]).astype(BF16)
            dv_ref[...] = (cv[...] + nvp[...]).astype(BF16)
            ck[...] = nkc[...]
            cv[...] = nvc[...]

        @pl.when(n == nb)
        def _():
            dk_ref[...] = ck[...].astype(BF16)
            dv_ref[...] = cv[...].astype(BF16)

    cur = lambda n: jnp.minimum(n, nb - 1)
    prev = lambda n: jnp.maximum(jnp.minimum(n, nb - 1) - 1, 0)
    outb = lambda n: jnp.maximum(n - 1, 0)
    kv_scr = pltpu.VMEM((BLK, KV_DIM), F32)
    own, extra = _hosted(
        body, name="attn_bwd", grid=(nb + 1,),
        in_specs=[pl.BlockSpec((BLK, Q_DIM), lambda n: (cur(n), 0)),
                  pl.BlockSpec((BLK, KV_DIM), lambda n: (cur(n), O_K // KV_DIM)),
                  pl.BlockSpec((BLK, KV_DIM), lambda n: (prev(n), O_K // KV_DIM)),
                  pl.BlockSpec((BLK, KV_DIM), lambda n: (cur(n), O_V // KV_DIM)),
                  pl.BlockSpec((BLK, KV_DIM), lambda n: (prev(n), O_V // KV_DIM)),
                  pl.BlockSpec((1, 128), lambda n: (0, 0)),
                  pl.BlockSpec((BLK, Q_DIM), lambda n: (cur(n), 0)),
                  pl.BlockSpec((BLK, Q_DIM), lambda n: (cur(n), 0))],
        out_specs=[pl.BlockSpec((BLK, Q_DIM), lambda n: (cur(n), 0)),
                   pl.BlockSpec((BLK, KV_DIM), lambda n: (outb(n), 0)),
                   pl.BlockSpec((BLK, KV_DIM), lambda n: (outb(n), 0)),
                   pl.BlockSpec((8, 128), lambda n: (0, 0))],
        out_shape=[jax.ShapeDtypeStruct((s, Q_DIM), BF16), jax.ShapeDtypeStruct((s, KV_DIM), BF16),
                   jax.ShapeDtypeStruct((s, KV_DIM), BF16), jax.ShapeDtypeStruct((8, 128), F32)],
        scratch_shapes=[kv_scr] * 6, args=(proj, proj, proj, proj, proj, sinks, o, do), sem=("arbitrary",), side=side)
    return own if side is None else (own, extra)


def _shift_down(x, j):
    if j == 0:
        return x
    row = lax.broadcasted_iota(jnp.int32, x.shape, 0)
    return jnp.where(row >= j, pltpu.roll(x, j, 0), 0.0)


def _shift_up(x, j):
    if j == 0:
        return x
    s = x.shape[0]
    row = lax.broadcasted_iota(jnp.int32, x.shape, 0)
    return jnp.where(row < s - j, pltpu.roll(x, s - j, 0), 0.0)


def _conv(x, w_ref, b_ref):
    kk = w_ref.shape[0]
    y = _shift_down(x, kk - 1) * w_ref[0:1, :]
    for q in range(1, kk):
        y = y + _shift_down(x, kk - 1 - q) * w_ref[q:q + 1, :]
    return y + b_ref[...]


def _conv_bwd(dy, x, w_ref, dx_dtype):
    kk = w_ref.shape[0]
    dx = _shift_up(dy, kk - 1) * w_ref[0:1, :]
    dws = [jnp.sum(dy * _shift_down(x, kk - 1), axis=0, keepdims=True)]
    for q in range(1, kk):
        dx = dx + _shift_up(dy, kk - 1 - q) * w_ref[q:q + 1, :]
        dws.append(jnp.sum(dy * _shift_down(x, kk - 1 - q), axis=0, keepdims=True))
    return dx.astype(dx_dtype), dws, jnp.sum(dy, axis=0, keepdims=True)


def _dsilu(y, sg):
    return sg * (1.0 + y * (1.0 - sg))


CT = 256


def _ssd_conv_fwd(proj, w, b):
    s = proj.shape[0]

    def body(x_ref, w_ref, b_ref, o_ref):
        y = _conv(x_ref[...], w_ref, b_ref)
        o_ref[...] = y * _sigmoid(y)

    return pl.pallas_call(
        body, name="ssd_conv_fwd", grid=(XBC_DIM // CT,),
        in_specs=[pl.BlockSpec((s, CT), lambda i: (0, O_X // CT + i)), pl.BlockSpec((4, CT), lambda i: (0, i)),
                  pl.BlockSpec((1, CT), lambda i: (0, i))],
        out_specs=pl.BlockSpec((s, CT), lambda i: (0, i)),
        out_shape=jax.ShapeDtypeStruct((s, XBC_DIM), F32), compiler_params=_cp(("parallel",)),
    )(proj, w, b)


def _ssd_conv_bwd(dact, proj, w, b):
    s = proj.shape[0]

    def body(d_ref, x_ref, w_ref, b_ref, dx_ref, dw_ref, db_ref):
        x = x_ref[...]
        y = _conv(x, w_ref, b_ref)
        dy = d_ref[...] * _dsilu(y, _sigmoid(y))
        dx, dws, db = _conv_bwd(dy, x, w_ref, BF16)
        dx_ref[...] = dx
        for q in range(4):
            dw_ref[q:q + 1, :] = dws[q]
        db_ref[...] = db

    return pl.pallas_call(
        body, name="ssd_conv_bwd", grid=(XBC_DIM // CT,),
        in_specs=[pl.BlockSpec((s, CT), lambda i: (0, i)), pl.BlockSpec((s, CT), lambda i: (0, O_X // CT + i)),
                  pl.BlockSpec((4, CT), lambda i: (0, i)), pl.BlockSpec((1, CT), lambda i: (0, i))],
        out_specs=[pl.BlockSpec((s, CT), lambda i: (0, i)), pl.BlockSpec((4, CT), lambda i: (0, i)),
                   pl.BlockSpec((1, CT), lambda i: (0, i))],
        out_shape=[jax.ShapeDtypeStruct((s, XBC_DIM), BF16), jax.ShapeDtypeStruct((4, XBC_DIM), F32),
                   jax.ShapeDtypeStruct((1, XBC_DIM), F32)],
        compiler_params=_cp(("parallel",)),
    )(dact, proj, w, b)


NFT = D_FF // CT


def _ffn_act_fwd(up, w, b):
    s = up.shape[0]

    def body(v_ref, g_ref, wv_ref, wg_ref, bv_ref, bg_ref, o_ref):
        val = _conv(v_ref[...], wv_ref, bv_ref)
        gt = _conv(g_ref[...], wg_ref, bg_ref)
        o_ref[...] = ((gt * _sigmoid(gt)) * val).astype(BF16)

    col = lambda off: (lambda i: (0, off + i))
    return pl.pallas_call(
        body, name="ffn_act_fwd", grid=(NFT,),
        in_specs=[pl.BlockSpec((s, CT), col(0)), pl.BlockSpec((s, CT), col(NFT)),
                  pl.BlockSpec((3, CT), col(0)), pl.BlockSpec((3, CT), col(NFT)),
                  pl.BlockSpec((1, CT), col(0)), pl.BlockSpec((1, CT), col(NFT))],
        out_specs=pl.BlockSpec((s, CT), col(0)),
        out_shape=jax.ShapeDtypeStruct((s, D_FF), BF16), compiler_params=_cp(("parallel",)),
    )(up, up, w, w, b, b)


def _ffn_act_bwd(dact, up, w, b):
    s = up.shape[0]

    def body(d_ref, v_ref, g_ref, wv_ref, wg_ref, bv_ref, bg_ref, dx_ref, dw_ref, db_ref):
        xv, xg = v_ref[...], g_ref[...]
        val = _conv(xv, wv_ref, bv_ref)
        gt = _conv(xg, wg_ref, bg_ref)
        sg = _sigmoid(gt)
        d = d_ref[...]
        for half, (dy, x, w_ref) in enumerate(((d * (gt * sg), xv, wv_ref), (d * val * _dsilu(gt, sg), xg, wg_ref))):
            dx, dws, db = _conv_bwd(dy, x, w_ref, BF16)
            dx_ref[half] = dx
            for q in range(3):
                dw_ref[half, q:q + 1, :] = dws[q]
            db_ref[half] = db

    col = lambda off: (lambda i: (0, off + i))
    both = lambda i: (0, 0, i)
    return pl.pallas_call(
        body, name="ffn_act_bwd", grid=(NFT,),
        in_specs=[pl.BlockSpec((s, CT), col(0)), pl.BlockSpec((s, CT), col(0)), pl.BlockSpec((s, CT), col(NFT)),
                  pl.BlockSpec((3, CT), col(0)), pl.BlockSpec((3, CT), col(NFT)),
                  pl.BlockSpec((1, CT), col(0)), pl.BlockSpec((1, CT), col(NFT))],
        out_specs=[pl.BlockSpec((2, s, CT), both), pl.BlockSpec((2, 3, CT), both), pl.BlockSpec((2, 1, CT), both)],
        out_shape=[jax.ShapeDtypeStruct((2, s, D_FF), BF16), jax.ShapeDtypeStruct((2, 3, D_FF), F32),
                   jax.ShapeDtypeStruct((2, 1, D_FF), F32)],
        compiler_params=_cp(("parallel",)),
    )(dact, up, up, w, w, b, b)


def _expand_mat():
    r = lax.broadcasted_iota(jnp.int32, (128, D_INNER), 0)
    c = lax.broadcasted_iota(jnp.int32, (128, D_INNER), 1)
    return ((c >> 6) == r).astype(BF16)


def _reduce_mat():
    r = lax.broadcasted_iota(jnp.int32, (D_INNER, 128), 0)
    c = lax.broadcasted_iota(jnp.int32, (D_INNER, 128), 1)
    return ((r >> 6) == c).astype(BF16)


def _split(v, parts):
    out = []
    for _ in range(parts - 1):
        p = v.astype(BF16)
        out.append(p)
        v = v - p.astype(F32)
    out.append(v.astype(BF16))
    return out


def _sel_dot(v, sel, parts):
    acc = None
    for p in reversed(_split(v, parts)):
        t = _dot(p, sel)
        acc = t if acc is None else acc + t
    return acc


def _row8(v):
    return jnp.broadcast_to(v, (8, v.shape[1]))


def _tril():
    r = lax.broadcasted_iota(jnp.int32, (BLK, BLK), 0)
    c = lax.broadcasted_iota(jnp.int32, (BLK, BLK), 1)
    return r >= c


def _softplus(x):
    return jnp.maximum(x, 0.0) + jnp.log(1.0 + jnp.exp(-jnp.abs(x)))


def _ssd_common(dtraw_ref, dtb_ref, alog_ref):
    causal = _tril()
    e_mat = _expand_mat()
    a_neg = -jnp.exp(alog_ref[...])
    dt = _softplus(dtraw_ref[...] + dtb_ref[...])
    a_cs = _dot(causal.astype(F32), dt * a_neg, HI)
    a_cs_t = a_cs.T
    dt_x = _sel_dot(dt, e_mat, 3)
    acs_x = _sel_dot(a_cs, e_mat, 3)
    alast_x = acs_x[BLK - 1:BLK, :]
    ea_x = jnp.exp(acs_x)
    ds_x = jnp.exp(alast_x - acs_x)
    elast_x = jnp.exp(alast_x)
    return causal, e_mat, a_neg, dt, a_cs, a_cs_t, dt_x, ea_x, ds_x, elast_x


def _decay(a_cs, a_cs_t, h, causal):
    seg = a_cs[:, h:h + 1] - a_cs_t[h:h + 1, :]
    return jnp.where(causal, jnp.exp(jnp.where(causal, seg, 0.0)), 0.0)


def _ssd_fwd(xbc, proj, dt_bias, a_log, d_skip):
    s = xbc.shape[0]
    nc = s // BLK

    def body(xs_ref, b_ref, c_ref, dtraw_ref, dtb_ref, alog_ref, dskip_ref, y_ref, hp_ref, h_scr, xc16):
        @pl.when(pl.program_id(0) == 0)
        def _():
            h_scr[...] = jnp.zeros_like(h_scr)

        causal, e_mat, _, _, a_cs, a_cs_t, dt_x, ea_x, ds_x, elast_x = _ssd_common(dtraw_ref, dtb_ref, alog_ref)
        dskip_x = _sel_dot(_row8(dskip_ref[...]), e_mat, 3)[0:1]
        xs = xs_ref[...]
        xc = xs * dt_x
        xc16[...] = xc.astype(BF16)
        xcd = (xc * ds_x).astype(BF16)
        hp_ref[0] = h_scr[...]
        for g in range(4):
            gs = slice(g * 512, (g + 1) * 512)
            cg = c_ref[:, g * 128:(g + 1) * 128].astype(BF16)
            bg = b_ref[:, g * 128:(g + 1) * 128].astype(BF16)
            cb = _dot_nt(cg, bg)
            hg = h_scr[:, gs]
            yoff = _dot(cg, hg.astype(BF16)) * ea_x[:, gs]
            for j in range(8):
                h = g * 8 + j
                hsl = slice(h * 64, (h + 1) * 64)
                mm = (cb * _decay(a_cs, a_cs_t, h, causal)).astype(BF16)
                y_ref[:, hsl] = _dot(mm, xc16[:, hsl])
            y_ref[:, gs] += yoff + xs[:, gs] * dskip_x[:, gs]
            h_scr[:, gs] = hg * elast_x[:, gs] + _dot_tn(bg, xcd[:, gs])

    vec = pl.BlockSpec((1, 128), lambda c: (0, 0))
    return pl.pallas_call(
        body, name="ssd_fwd", grid=(nc,),
        in_specs=[pl.BlockSpec((BLK, D_INNER), lambda c: (c, 0)),
                  pl.BlockSpec((BLK, BC_DIM), lambda c: (c, D_INNER // BC_DIM)),
                  pl.BlockSpec((BLK, BC_DIM), lambda c: (c, D_INNER // BC_DIM + 1)),
                  pl.BlockSpec((BLK, 128), lambda c: (c, O_DT // 128)), vec, vec, vec],
        out_specs=[pl.BlockSpec((BLK, D_INNER), lambda c: (c, 0)),
                   pl.BlockSpec((1, 128, D_INNER), lambda c: (c, 0, 0))],
        out_shape=[jax.ShapeDtypeStruct((s, D_INNER), F32), jax.ShapeDtypeStruct((nc, 128, D_INNER), F32)],
        scratch_shapes=[pltpu.VMEM((128, D_INNER), F32), pltpu.VMEM((BLK, D_INNER), BF16)],
        compiler_params=_cp(("arbitrary",)),
    )(xbc, xbc, xbc, proj, dt_bias, a_log, d_skip)


def _ssd_bwd(xbc, proj, dt_bias, a_log, d_skip, hprev, dy, side=None):
    s = xbc.shape[0]
    nc = s // BLK

    def body(xs_ref, b_ref, c_ref, dtraw_ref, dtb_ref, alog_ref, dskip_ref, hp_ref, dy_ref,
             dxbc_ref, ddt_ref, dvec_ref, dh_scr, xc16, dy16, dxc_scr, dacs_r, tdiff):
        step = pl.program_id(0)
        dacs_r[...] = jnp.zeros_like(dacs_r)

        @pl.when(step == 0)
        def _():
            dh_scr[...] = jnp.zeros_like(dh_scr)
            dvec_ref[...] = jnp.zeros_like(dvec_ref)

        causal, e_mat, a_neg, dt, a_cs, a_cs_t, dt_x, ea_x, ds_x, elast_x = _ssd_common(dtraw_ref, dtb_ref, alog_ref)
        r_mat = _reduce_mat()
        lane = lax.broadcasted_iota(jnp.int32, (1, 128), 1)
        dskip_x = _sel_dot(_row8(dskip_ref[...]), e_mat, 3)[0:1]
        xs = xs_ref[...]
        dy = dy_ref[...]
        xc = xs * dt_x
        xcd = xc * ds_x
        xc16[...] = xc.astype(BF16)
        dy16[...] = dy.astype(BF16)
        dyea = dy * ea_x
        dh = dh_scr[...]
        hp = hp_ref[0]
        dalast_x = jnp.sum(dh * hp, axis=0, keepdims=True) * elast_x
        dacs = jnp.zeros((BLK, 128), F32)
        for g in range(4):
            gs = slice(g * 512, (g + 1) * 512)
            bsl = slice(g * 128, (g + 1) * 128)
            cg = c_ref[:, bsl].astype(BF16)
            bg = b_ref[:, bsl].astype(BF16)
            cb = _dot_nt(cg, bg)
            hg16 = hp[:, gs].astype(BF16)
            dhg16 = dh[:, gs].astype(BF16)
            raw = _dot(cg, hg16)
            draw16 = dyea[:, gs].astype(BF16)
            dcg = _dot_nt(draw16, hg16)
            dhp_g = _dot_tn(cg, draw16)
            dbg = _dot_nt(xcd[:, gs].astype(BF16), dhg16)
            dxcd = _dot(bg, dhg16)
            dcb = jnp.zeros((BLK, BLK), F32)
            for j in range(8):
                h = g * 8 + j
                hsl = slice(h * 64, (h + 1) * 64)
                decay = _decay(a_cs, a_cs_t, h, causal)
                m = cb * decay
                dm = _dot_nt(dy16[:, hsl], xc16[:, hsl])
                dxc_scr[:, hsl] = _dot_tn(m.astype(BF16), dy16[:, hsl])
                dcb = dcb + dm * decay
                dseg = dm * m
                oneh = jnp.where(lane == h, 1.0, 0.0)
                dacs = dacs + jnp.sum(dseg, axis=1, keepdims=True) * oneh
                dacs_r[h:h + 1, :] = jnp.sum(dseg, axis=0, keepdims=True)
            dcb16 = dcb.astype(BF16)
            dcg = dcg + _dot(dcb16, bg)
            dbg = dbg + _dot_tn(dcb16, cg)
            dxbc_ref[:, D_INNER + g * 128:D_INNER + (g + 1) * 128] = dbg
            dxbc_ref[:, D_INNER + BC_DIM + g * 128:D_INNER + BC_DIM + (g + 1) * 128] = dcg
            dxc_scr[:, gs] += dxcd * ds_x[:, gs]
            dh_scr[:, gs] = dh[:, gs] * elast_x[:, gs] + dhp_g
            tst = dxcd * xcd[:, gs]
            tdiff[:, gs] = dy[:, gs] * (raw * ea_x[:, gs]) - tst
            tdiff[BLK - 1:BLK, gs] += jnp.sum(tst, axis=0, keepdims=True)
        dxc = dxc_scr[...]
        row = lax.broadcasted_iota(jnp.int32, (BLK, D_INNER), 0)
        tfull = tdiff[...] + jnp.where(row == BLK - 1, dalast_x, 0.0)
        dacs = dacs + _sel_dot(tfull, r_mat, 2) - dacs_r[...].T
        da = _dot_tn(causal.astype(F32), dacs, HI)
        ddt = da * a_neg + _sel_dot(dxc * xs, r_mat, 2)
        lmask = lax.broadcasted_iota(jnp.int32, (BLK, 128), 1) < N_SSD_HEADS
        ddtraw = jnp.where(lmask, ddt * _sigmoid(dtraw_ref[...] + dtb_ref[...]), 0.0)
        ddt_ref[...] = ddtraw.astype(BF16)
        dxbc_ref[:, 0:D_INNER] = dy * dskip_x + dxc * dt_x
        dvec_ref[0:1, :] += jnp.sum(ddtraw, axis=0, keepdims=True)
        dvec_ref[1:2, :] += jnp.where(lane < N_SSD_HEADS, jnp.sum(da * dt, axis=0, keepdims=True) * a_neg, 0.0)
        dvec_ref[2:3, :] += _sel_dot(_row8(jnp.sum(dy * xs, axis=0, keepdims=True)), r_mat, 3)[0:1]

    rev = lambda c: nc - 1 - c
    vec = pl.BlockSpec((1, 128), lambda c: (0, 0))
    own, extra = _hosted(
        body, name="ssd_bwd", grid=(nc,),
        in_specs=[pl.BlockSpec((BLK, D_INNER), lambda c: (rev(c), 0)),
                  pl.BlockSpec((BLK, BC_DIM), lambda c: (rev(c), D_INNER // BC_DIM)),
                  pl.BlockSpec((BLK, BC_DIM), lambda c: (rev(c), D_INNER // BC_DIM + 1)),
                  pl.BlockSpec((BLK, 128), lambda c: (rev(c), O_DT // 128)), vec, vec, vec,
                  pl.BlockSpec((1, 128, D_INNER), lambda c: (rev(c), 0, 0)),
                  pl.BlockSpec((BLK, D_INNER), lambda c: (rev(c), 0))],
        out_specs=[pl.BlockSpec((BLK, XBC_DIM), lambda c: (rev(c), 0)),
                   pl.BlockSpec((BLK, 128), lambda c: (rev(c), 0)),
                   pl.BlockSpec((8, 128), lambda c: (0, 0))],
        out_shape=[jax.ShapeDtypeStruct((s, XBC_DIM), F32), jax.ShapeDtypeStruct((s, 128), BF16),
                   jax.ShapeDtypeStruct((8, 128), F32)],
        scratch_shapes=[pltpu.VMEM((128, D_INNER), F32), pltpu.VMEM((BLK, D_INNER), BF16),
                        pltpu.VMEM((BLK, D_INNER), BF16), pltpu.VMEM((BLK, D_INNER), F32),
                        pltpu.VMEM((128, BLK), F32), pltpu.VMEM((BLK, D_INNER), F32)],
        args=(xbc, xbc, xbc, proj, dt_bias, a_log, d_skip, hprev, dy), sem=("arbitrary",), side=side)
    return own if side is None else (own, extra)


GW = 512


def _gate_norm_fwd(y, proj, wn, *, tm=512):
    s = y.shape[0]
    tm = _tile(s, tm)

    def body(y_ref, z_ref, w_ref, o_ref):
        z = z_ref[...]
        y2 = y_ref[...] * (z * _sigmoid(z))
        r = lax.rsqrt(jnp.mean(y2 * y2, axis=-1, keepdims=True) + EPS)
        o_ref[...] = ((y2 * r) * w_ref[...]).astype(BF16)

    return pl.pallas_call(
        body, name="gate_norm_fwd", grid=(s // tm, 4),
        in_specs=[pl.BlockSpec((tm, GW), lambda i, g: (i, g)), pl.BlockSpec((tm, GW), lambda i, g: (i, O_Z // GW + g)),
                  pl.BlockSpec((1, GW), lambda i, g: (0, g))],
        out_specs=pl.BlockSpec((tm, GW), lambda i, g: (i, g)),
        out_shape=jax.ShapeDtypeStruct((s, D_INNER), BF16), compiler_params=_cp(("parallel", "parallel")),
    )(y, proj, wn)


def _gate_norm_bwd(dyn, y, proj, wn, *, tm=512):
    s = y.shape[0]
    tm = _tile(s, tm)

    def body(d_ref, y_ref, z_ref, w_ref, dy_ref, dz_ref, dw_ref):
        i = pl.program_id(1)
        z = z_ref[...]
        sg = _sigmoid(z)
        sz = z * sg
        yv = y_ref[...]
        y2 = yv * sz
        r = lax.rsqrt(jnp.mean(y2 * y2, axis=-1, keepdims=True) + EPS)
        xh = y2 * r
        dv = d_ref[...]
        g = dv * w_ref[...]
        dy2 = r * (g - xh * jnp.mean(g * xh, axis=-1, keepdims=True))
        dy_ref[...] = dy2 * sz
        dz_ref[...] = (dy2 * yv * _dsilu(z, sg)).astype(BF16)
        part = jnp.sum(dv * xh, axis=0, keepdims=True)

        @pl.when(i == 0)
        def _():
            dw_ref[...] = part

        @pl.when(i > 0)
        def _():
            dw_ref[...] += part

    blk = pl.BlockSpec((tm, GW), lambda g, i: (i, g))
    vec = pl.BlockSpec((1, GW), lambda g, i: (0, g))
    return pl.pallas_call(
        body, name="gate_norm_bwd", grid=(4, s // tm),
        in_specs=[blk, blk, pl.BlockSpec((tm, GW), lambda g, i: (i, O_Z // GW + g)), vec],
        out_specs=[blk, blk, vec],
        out_shape=[jax.ShapeDtypeStruct((s, D_INNER), F32), jax.ShapeDtypeStruct((s, D_INNER), BF16),
                   jax.ShapeDtypeStruct((1, D_INNER), F32)],
        compiler_params=_cp(("parallel", "arbitrary")),
    )(dyn, y, proj, wn)


def _merge_fwd(proj, b_gate, attn, ssd_out, *, tm=512):
    s = attn.shape[0]
    tm = _tile(s, tm)

    def body(ga_ref, gs_ref, ba_ref, bs_ref, a_ref, s_ref, o_ref):
        ga = _sigmoid(ga_ref[...] + ba_ref[...])
        gs = _sigmoid(gs_ref[...] + bs_ref[...])
        o_ref[...] = (ga * a_ref[...] + gs * s_ref[...]).astype(BF16)

    blk = pl.BlockSpec((tm, GW), lambda i, j: (i, j))
    return pl.pallas_call(
        body, name="merge_fwd", grid=(s // tm, 2),
        in_specs=[pl.BlockSpec((tm, GW), lambda i, j: (i, O_GA // GW + j)),
                  pl.BlockSpec((tm, GW), lambda i, j: (i, O_GS // GW + j)),
                  pl.BlockSpec((1, GW), lambda i, j: (0, j)), pl.BlockSpec((1, GW), lambda i, j: (0, 2 + j)), blk, blk],
        out_specs=blk, out_shape=jax.ShapeDtypeStruct((s, D_MODEL), BF16),
        compiler_params=_cp(("parallel", "parallel")),
    )(proj, proj, b_gate, b_gate, attn, ssd_out)


def _merge_bwd(dm, proj, b_gate, attn, ssd_out, *, tm=512):
    s = attn.shape[0]
    tm = _tile(s, tm)

    def body(d_ref, ga_ref, gs_ref, ba_ref, bs_ref, a_ref, s_ref, da_ref, ds_ref, dga_ref, dgs_ref, dba_ref, dbs_ref):
        i = pl.program_id(1)
        ga = _sigmoid(ga_ref[...] + ba_ref[...])
        gs = _sigmoid(gs_ref[...] + bs_ref[...])
        d = d_ref[...]
        da_ref[...] = (d * ga).astype(BF16)
        ds_ref[...] = (d * gs).astype(BF16)
        dga = d * a_ref[...] * (ga * (1.0 - ga))
        dgs = d * s_ref[...] * (gs * (1.0 - gs))
        dga_ref[...] = dga.astype(BF16)
        dgs_ref[...] = dgs.astype(BF16)
        pa = jnp.sum(dga, axis=0, keepdims=True)
        ps = jnp.sum(dgs, axis=0, keepdims=True)

        @pl.when(i == 0)
        def _():
            dba_ref[...] = pa
            dbs_ref[...] = ps

        @pl.when(i > 0)
        def _():
            dba_ref[...] += pa
            dbs_ref[...] += ps

    blk = pl.BlockSpec((tm, GW), lambda j, i: (i, j))
    vec = pl.BlockSpec((1, GW), lambda j, i: (0, j))
    sd = jax.ShapeDtypeStruct((s, D_MODEL), BF16)
    vd = jax.ShapeDtypeStruct((1, D_MODEL), F32)
    return pl.pallas_call(
        body, name="merge_bwd", grid=(2, s // tm),
        in_specs=[blk, pl.BlockSpec((tm, GW), lambda j, i: (i, O_GA // GW + j)),
                  pl.BlockSpec((tm, GW), lambda j, i: (i, O_GS // GW + j)),
                  vec, pl.BlockSpec((1, GW), lambda j, i: (0, 2 + j)), blk, blk],
        out_specs=[blk, blk, blk, blk, vec, vec], out_shape=[sd, sd, sd, sd, vd, vd],
        compiler_params=_cp(("parallel", "arbitrary")),
    )(dm, proj, proj, b_gate, b_gate, attn, ssd_out)


def _adamw_math(w, g, m, v):
    mn = ADAM_B1 * m + (1.0 - ADAM_B1) * g
    vn = ADAM_B2 * v + (1.0 - ADAM_B2) * (g * g)
    m_hat = mn / (1.0 - ADAM_B1 ** ADAM_STEP)
    v_hat = vn / (1.0 - ADAM_B2 ** ADAM_STEP)
    return -ADAM_LR * (m_hat / (jnp.sqrt(v_hat) + ADAM_EPS) + ADAM_WD * w), mn, vn


def _adamw_many(ws, gs, ms, vs):
    n = len(ws)

    def body(*refs):
        outs = refs[4 * n:]
        for i in range(n):
            res = _adamw_math(*[refs[q * n + i][...] for q in range(4)])
            for q in range(3):
                outs[q * n + i][...] = res[q]

    return pl.pallas_call(body, name="adamw_small", out_shape=[jax.ShapeDtypeStruct(w.shape, F32) for w in ws] * 3,
                          compiler_params=_cp())(*ws, *gs, *ms, *vs)


def _adamw(w, g, m, v, *, name, tm=128):
    r, c = w.shape
    tm = r if (r < tm or r % tm) else tm

    def body(w_ref, g_ref, m_ref, v_ref, d_ref, nm_ref, nv_ref):
        d_ref[...], nm_ref[...], nv_ref[...] = _adamw_math(w_ref[...], g_ref[...], m_ref[...], v_ref[...])

    blk = pl.BlockSpec((tm, c), lambda i: (i, 0))
    sd = jax.ShapeDtypeStruct((r, c), F32)
    return pl.pallas_call(
        body, name=name, grid=(r // tm,), in_specs=[blk] * 4, out_specs=[blk] * 3, out_shape=[sd] * 3,
        compiler_params=_cp(("parallel",)),
    )(w, g, m, v)


ANY = pl.BlockSpec(memory_space=pl.ANY)
N_CHIPS = 4


def _chip_of(k, x, y):
    return (x ^ (k >> 1), y ^ (k & 1))


def _all_gather_small(shard):
    r, c = shard.shape
    hr = r // 2

    def body(sh_ref, out_ref, send_sems, recv_sems, local_sem):
        x, y, cc = lax.axis_index("x"), lax.axis_index("y"), lax.axis_index("c")

        def half(px, py, pc):
            return out_ref.at[2 * px + py, pl.ds(pc * hr, hr), :]

        def copy(k, px, py, pc, to, src=None):
            return pltpu.make_async_remote_copy(
                src_ref=half(px, py, pc) if src is None else src, dst_ref=half(px, py, pc),
                send_sem=send_sems.at[k], recv_sem=recv_sems.at[k], device_id=to, device_id_type=MESH)

        mine = pltpu.make_async_copy(sh_ref, out_ref.at[2 * x + y], local_sem)
        mine.start()
        chips = [_chip_of(k, x, y) for k in (1, 2, 3)]
        first = [copy(j, x, y, cc, (*chip, cc), src=sh_ref.at[pl.ds(cc * hr, hr), :]) for j, chip in enumerate(chips)]
        for cp in first:
            cp.start()
        passed = [copy(3 + j, *chip, cc, (x, y, 1 - cc)) for j, chip in enumerate(chips)]
        for j, chip in enumerate(chips):
            copy(j, *chip, cc, (x, y, cc)).wait_recv()
            passed[j].start()
        for j, chip in enumerate(chips):
            copy(3 + j, *chip, 1 - cc, (x, y, cc)).wait_recv()
        for cp in first + passed:
            cp.wait_send()
        mine.wait()

    return pl.pallas_call(
        body, name="all_gather_small", in_specs=[ANY], out_specs=ANY,
        out_shape=jax.ShapeDtypeStruct((N_CHIPS, r, c), shard.dtype),
        scratch_shapes=[pltpu.SemaphoreType.DMA((6,)), pltpu.SemaphoreType.DMA((6,)), pltpu.SemaphoreType.DMA],
    )(shard)


def _cast_bf16(a, *, name, tm=512):
    n, r, c = a.shape
    tm = _tile(r, tm) if r % 128 == 0 else r

    def body(a_ref, o_ref):
        o_ref[...] = a_ref[...].astype(BF16)

    blk = pl.BlockSpec((1, tm, c), lambda i, j: (i, j, 0))
    return pl.pallas_call(body, name=name, grid=(n, r // tm), in_specs=[blk], out_specs=blk,
                          out_shape=jax.ShapeDtypeStruct(a.shape, BF16), compiler_params=_cp(("parallel", "parallel")))(a)


def _pair_exchange(g16, hr):
    n, r, c = g16.shape

    def body(g_ref, out_ref, send_sem, recv_sem):
        x, y, cc = lax.axis_index("x"), lax.axis_index("y"), lax.axis_index("c")
        cp = pltpu.make_async_remote_copy(
            src_ref=g_ref.at[:, pl.ds((1 - cc) * hr, hr), :], dst_ref=out_ref, send_sem=send_sem, recv_sem=recv_sem,
            device_id=(x, y, 1 - cc), device_id_type=MESH)
        cp.start()
        cp.wait()

    return pl.pallas_call(
        body, name="grad_pair_exchange", in_specs=[ANY], out_specs=ANY,
        out_shape=jax.ShapeDtypeStruct((n, hr, c), g16.dtype),
        scratch_shapes=[pltpu.SemaphoreType.DMA, pltpu.SemaphoreType.DMA],
    )(g16)


def _pair_add(g, recv, half_idx, hr, *, tm=384):
    n, r, c = g.shape
    nt = hr // tm

    def body(hi_ref, g_ref, r_ref, o32_ref, o16_ref):
        v = g_ref[...] + r_ref[...].astype(F32)
        o32_ref[...] = v
        o16_ref[...] = v.astype(BF16)

    gs = pltpu.PrefetchScalarGridSpec(
        num_scalar_prefetch=1, grid=(n, nt),
        in_specs=[pl.BlockSpec((1, tm, c), lambda i, j, hi: (i, hi[0] * nt + j, 0)),
                  pl.BlockSpec((1, tm, c), lambda i, j, hi: (i, j, 0))],
        out_specs=[pl.BlockSpec((1, tm, c), lambda i, j, hi: (i, j, 0))] * 2)
    return pl.pallas_call(
        body, name="grad_pair_add", grid_spec=gs,
        out_shape=[jax.ShapeDtypeStruct((n, hr, c), F32), jax.ShapeDtypeStruct((n, hr, c), BF16)],
        compiler_params=_cp(("parallel", "parallel")),
    )(half_idx, g, recv)


def _chip_exchange(p16):
    n, hr, c = p16.shape

    def body(p_ref, out_ref, send_sems, recv_sems):
        x, y, cc = lax.axis_index("x"), lax.axis_index("y"), lax.axis_index("c")
        cps = []
        for j, k in enumerate((1, 2, 3)):
            px, py = _chip_of(k, x, y)
            cps.append(pltpu.make_async_remote_copy(
                src_ref=p_ref.at[2 * px + py], dst_ref=out_ref.at[j], send_sem=send_sems.at[j], recv_sem=recv_sems.at[j],
                device_id=(px, py, cc), device_id_type=MESH))
        for cp in cps:
            cp.start()
        for cp in cps:
            cp.wait()

    return pl.pallas_call(
        body, name="grad_chip_exchange", in_specs=[ANY], out_specs=ANY,
        out_shape=jax.ShapeDtypeStruct((3, hr, c), p16.dtype),
        scratch_shapes=[pltpu.SemaphoreType.DMA((3,)), pltpu.SemaphoreType.DMA((3,))],
    )(p16)


def _chip_add(p32, recv, chip_idx, *, tm=384):
    n, hr, c = p32.shape

    def body(ci_ref, p_ref, r_ref, o_ref):
        o_ref[...] = ((p_ref[0] + r_ref[0].astype(F32)) + r_ref[1].astype(F32)) + r_ref[2].astype(F32)

    gs = pltpu.PrefetchScalarGridSpec(
        num_scalar_prefetch=1, grid=(hr // tm,),
        in_specs=[pl.BlockSpec((1, tm, c), lambda j, ci: (ci[0], j, 0)), pl.BlockSpec((3, tm, c), lambda j, ci: (0, j, 0))],
        out_specs=pl.BlockSpec((tm, c), lambda j, ci: (j, 0)))
    return pl.pallas_call(
        body, name="grad_chip_add", grid_spec=gs, out_shape=jax.ShapeDtypeStruct((hr, c), F32),
        compiler_params=_cp(("parallel",)),
    )(chip_idx, p32, recv)


def _pair_gather(f):
    hr, c = f.shape

    def body(f_ref, out_ref, send_sem, recv_sem, local_sem):
        x, y, cc = lax.axis_index("x"), lax.axis_index("y"), lax.axis_index("c")
        mine = pltpu.make_async_copy(f_ref, out_ref.at[pl.ds(cc * hr, hr), :], local_sem)
        mine.start()
        cp = pltpu.make_async_remote_copy(
            src_ref=f_ref, dst_ref=out_ref.at[pl.ds(cc * hr, hr), :], send_sem=send_sem, recv_sem=recv_sem,
            device_id=(x, y, 1 - cc), device_id_type=MESH)
        cp.start()
        cp.wait()
        mine.wait()

    return pl.pallas_call(
        body, name="grad_pair_gather", in_specs=[ANY], out_specs=ANY,
        out_shape=jax.ShapeDtypeStruct((2 * hr, c), f.dtype),
        scratch_shapes=[pltpu.SemaphoreType.DMA, pltpu.SemaphoreType.DMA, pltpu.SemaphoreType.DMA],
    )(f)


def _all_reduce_small(buf):
    r, c = buf.shape

    def body(b_ref, out_ref, gat, send_sems, recv_sems):
        x, y, cc = lax.axis_index("x"), lax.axis_index("y"), lax.axis_index("c")
        me = 4 * x + 2 * y + cc
        gat[me] = b_ref[...]
        cps = []
        for k in range(1, 8):
            px, py, pc = x ^ (k >> 2), y ^ ((k >> 1) & 1), cc ^ (k & 1)
            cps.append(pltpu.make_async_remote_copy(
                src_ref=b_ref, dst_ref=gat.at[me], send_sem=send_sems.at[k - 1], recv_sem=recv_sems.at[k - 1],
                device_id=(px, py, pc), device_id_type=MESH))
        for cp in cps:
            cp.start()
        for cp in cps:
            cp.wait()
        acc = gat[0]
        for d in range(1, 8):
            acc = acc + gat[d]
        out_ref[...] = acc

    vm = pl.BlockSpec(memory_space=pltpu.VMEM)
    return pl.pallas_call(
        body, name="all_reduce_small", in_specs=[vm], out_specs=vm, out_shape=jax.ShapeDtypeStruct((r, c), F32),
        scratch_shapes=[pltpu.VMEM((8, r, c), F32), pltpu.SemaphoreType.DMA((7,)), pltpu.SemaphoreType.DMA((7,))],
        compiler_params=pltpu.CompilerParams(vmem_limit_bytes=VMEM_LIMIT),
    )(buf)


def _pipe(fn, ins, outs, tr):
    shape = ins[0].shape
    lead, (r, c) = shape[:-2], shape[-2:]
    assert len(lead) <= 1 and r % tr == 0
    nr = r // tr
    n = nr * (lead[0] if lead else 1)
    ni, no = len(ins), len(outs)

    def blk(ref, step):
        rows = pl.ds(pl.multiple_of((step % nr) * tr, tr), tr)
        return ref.at[step // nr, rows, :] if lead else ref.at[rows, :]

    def scoped(*bufs):
        ibufs, obufs, isem, osem = bufs[:ni], bufs[ni:ni + no], bufs[-2], bufs[-1]

        def in_copy(q, step, slot):
            return pltpu.make_async_copy(blk(ins[q], step), ibufs[q].at[slot], isem.at[q, slot])

        def out_copy(q, step, slot):
            return pltpu.make_async_copy(obufs[q].at[slot], blk(outs[q], step), osem.at[q, slot])

        for q in range(ni):
            in_copy(q, 0, 0).start()

        def body(step, carry):
            slot = step % 2

            @pl.when(step + 1 < n)
            def _():
                for q in range(ni):
                    in_copy(q, step + 1, 1 - slot).start()

            for q in range(ni):
                in_copy(q, step, slot).wait()

            @pl.when(step >= 2)
            def _():
                for q in range(no):
                    out_copy(q, step - 2, slot).wait()

            res = fn(*[ibufs[q][slot] for q in range(ni)])
            for q in range(no):
                obufs[q][slot] = res[q].astype(obufs[q].dtype)
                out_copy(q, step, slot).start()
            return carry

        lax.fori_loop(0, n, body, 0)
        for step in range(max(n - 2, 0), n):
            for q in range(no):
                out_copy(q, step, step % 2).wait()

    pl.run_scoped(scoped, *[pltpu.VMEM((2, tr, c), q.dtype) for q in ins], *[pltpu.VMEM((2, tr, c), q.dtype) for q in outs],
                  pltpu.SemaphoreType.DMA((ni, 2)), pltpu.SemaphoreType.DMA((no, 2)))


W_IN_PAD = 2304
BIG = ("w_in", "w_attn_o", "w_ssd_o", "w_out", "w_up", "w_down")
BIG_SHAPE = dict(w_in=(D_MODEL, W_IN_PAD), w_attn_o=(Q_DIM // 4, D_MODEL), w_ssd_o=(D_INNER // 4, D_MODEL),
                 w_out=(D_MODEL // 4, D_MODEL), w_up=(D_MODEL, 2 * D_FF // 4), w_down=(D_FF // 4, D_MODEL))
BIG_TR = dict(w_in=128, w_attn_o=128, w_ssd_o=128, w_out=128, w_up=128, w_down=176)
X_FIRST = dict(w_in=True, w_attn_o=True, w_ssd_o=False, w_out=True, w_up=False, w_down=False)


def _neighbours(x, y, x_first):
    xn, yn = (1 - x, y), (x, 1 - y)
    n1, n2 = (xn, yn) if x_first else (yn, xn)
    slot = lambda ch: 2 * ch[0] + ch[1]
    return n1, n2, slot(n1), slot(n2), slot((1 - x, 1 - y))


def _gather_big(shards):
    nt = len(BIG)

    def body(*refs):
        sh, out = refs[:nt], refs[nt:2 * nt]
        send_sems, recv_sems = refs[2 * nt:]
        x, y, cc = lax.axis_index("x"), lax.axis_index("y"), lax.axis_index("c")
        me = 2 * x + y
        sib = (x, y, 1 - cc)
        for t, n in enumerate(BIG):
            _pipe(lambda v: (v,), [sh[t]], [out[t].at[me]], BIG_TR[n])

        def copy(t, k, slot, pc, to):
            hr = BIG_SHAPE[BIG[t]][0] // 2
            ref = out[t].at[slot, pl.ds(pc * hr, hr), :]
            return pltpu.make_async_remote_copy(src_ref=ref, dst_ref=ref, send_sem=send_sems.at[6 * t + k],
                                                recv_sem=recv_sems.at[6 * t + k], device_id=to, device_id_type=MESH)

        started = []

        def start(cp):
            cp.start()
            started.append(cp)

        geo = [_neighbours(x, y, X_FIRST[n]) for n in BIG]
        for t in range(nt):
            n1, n2, _, _, _ = geo[t]
            start(copy(t, 0, me, cc, (*n1, cc)))
            start(copy(t, 1, me, cc, (*n2, cc)))
        for t in range(nt):
            n1, n2, s1, s2, sd = geo[t]
            copy(t, 0, s1, cc, sib).wait_recv()
            start(copy(t, 2, s1, cc, (*n2, cc)))
            start(copy(t, 3, s1, cc, sib))
            copy(t, 1, s2, cc, sib).wait_recv()
            start(copy(t, 4, s2, cc, sib))
        for t in range(nt):
            _, _, s1, s2, sd = geo[t]
            copy(t, 2, sd, cc, sib).wait_recv()
            start(copy(t, 5, sd, cc, sib))
        for t in range(nt):
            _, _, s1, s2, sd = geo[t]
            copy(t, 3, s1, 1 - cc, sib).wait_recv()
            copy(t, 4, s2, 1 - cc, sib).wait_recv()
            copy(t, 5, sd, 1 - cc, sib).wait_recv()
        for cp in started:
            cp.wait_send()

    return pl.pallas_call(
        body, name="gather_big", in_specs=[ANY] * nt, out_specs=[ANY] * nt,
        out_shape=[jax.ShapeDtypeStruct((N_CHIPS, *BIG_SHAPE[n]), BF16) for n in BIG],
        scratch_shapes=[pltpu.SemaphoreType.DMA((6 * nt,)), pltpu.SemaphoreType.DMA((6 * nt,))],
        compiler_params=pltpu.CompilerParams(vmem_limit_bytes=VMEM_LIMIT),
    )(*shards)


def _reduce_big(grads):
    nt = len(BIG)
    nw = 7

    def body(*refs):
        g = refs[:nt]
        fin = refs[nt:2 * nt]
        work = refs[2 * nt:2 * nt + nw * nt]
        send_sems, recv_sems = refs[2 * nt + nw * nt:]
        x, y, cc = lax.axis_index("x"), lax.axis_index("y"), lax.axis_index("c")
        me = 2 * x + y
        sib = (x, y, 1 - cc)
        started = []

        def rcopy(t, k, src, dst, to):
            cp = pltpu.make_async_remote_copy(src_ref=src, dst_ref=dst, send_sem=send_sems.at[5 * t + k],
                                              recv_sem=recv_sems.at[5 * t + k], device_id=to, device_id_type=MESH)
            return cp

        def start(cp):
            cp.start()
            started.append(cp)

        geo = [_neighbours(x, y, X_FIRST[n]) for n in BIG]
        hrs = [BIG_SHAPE[n][0] // 2 for n in BIG]
        wk = lambda t: work[nw * t:nw * (t + 1)]
        one = lambda ref, slot: ref.at[pl.ds(slot, 1)]
        for t in range(nt):
            recv_a = wk(t)[0]
            start(rcopy(t, 0, g[t].at[:, pl.ds((1 - cc) * hrs[t], hrs[t]), :], recv_a, sib))
        for t, n in enumerate(BIG):
            recv_a, p32, p16, r1, qme, qs2, r2 = wk(t)
            n1, n2, s1, s2, sd = geo[t]
            rcopy(t, 0, recv_a, recv_a, sib).wait_recv()
            _pipe(lambda a, b: (a + b, a + b), [g[t].at[:, pl.ds(cc * hrs[t], hrs[t]), :], recv_a], [p32, p16], BIG_TR[n])
            start(rcopy(t, 1, one(p16, s1), one(r1, 0), (*n1, cc)))
            start(rcopy(t, 2, one(p16, sd), one(r1, 1), (*n1, cc)))
        for t, n in enumerate(BIG):
            recv_a, p32, p16, r1, qme, qs2, r2 = wk(t)
            n1, n2, s1, s2, sd = geo[t]
            rcopy(t, 1, one(r1, 0), one(r1, 0), sib).wait_recv()
            rcopy(t, 2, one(r1, 1), one(r1, 1), sib).wait_recv()
            _pipe(lambda a, b: (a + b.astype(F32),), [one(p32, s2), one(r1, 1)], [qs2], BIG_TR[n])
            start(rcopy(t, 3, qs2, r2, (*n2, cc)))
            _pipe(lambda a, b: (a + b.astype(F32),), [one(p32, me), one(r1, 0)], [qme], BIG_TR[n])
        for t, n in enumerate(BIG):
            recv_a, p32, p16, r1, qme, qs2, r2 = wk(t)
            rcopy(t, 3, r2, r2, sib).wait_recv()
            mine = fin[t].at[pl.ds(cc * hrs[t], hrs[t]), :]
            _pipe(lambda a, b: (a + b.astype(F32),), [qme.at[0], r2.at[0]], [mine], BIG_TR[n])
            start(rcopy(t, 4, mine, mine, sib))
        for t in range(nt):
            other = fin[t].at[pl.ds((1 - cc) * hrs[t], hrs[t]), :]
            rcopy(t, 4, other, other, sib).wait_recv()
        for cp in started:
            cp.wait_send()

    outs = [jax.ShapeDtypeStruct(BIG_SHAPE[n], F32) for n in BIG]
    for n in BIG:
        r, c = BIG_SHAPE[n]
        hr = r // 2
        outs += [jax.ShapeDtypeStruct((4, hr, c), F32), jax.ShapeDtypeStruct((4, hr, c), F32),
                 jax.ShapeDtypeStruct((4, hr, c), BF16), jax.ShapeDtypeStruct((2, hr, c), BF16),
                 jax.ShapeDtypeStruct((1, hr, c), F32), jax.ShapeDtypeStruct((1, hr, c), BF16),
                 jax.ShapeDtypeStruct((1, hr, c), BF16)]
    res = pl.pallas_call(
        body, name="reduce_big", in_specs=[ANY] * nt, out_specs=[ANY] * len(outs), out_shape=outs,
        scratch_shapes=[pltpu.SemaphoreType.DMA((5 * nt,)), pltpu.SemaphoreType.DMA((5 * nt,))],
        compiler_params=pltpu.CompilerParams(vmem_limit_bytes=VMEM_LIMIT),
    )(*grads)
    return res[:nt]


def _quarters(names):
    out = []
    for i, n in enumerate(names):
        q = BIG_SHAPE[n][0] // 4
        tr = 128 if q % 128 == 0 else q
        out += [(i, True, 0, q, tr), (i, False, q, q, tr)]
    return out


class _GatherJob:
    def __init__(self, names, shards, at=None):
        self.names = names
        self.at = at
        self.inputs = list(shards)
        self.out_shapes = [jax.ShapeDtypeStruct((N_CHIPS, *BIG_SHAPE[n]), BF16) for n in names]
        self.ent = _quarters(names)
        self.scratch = [pltpu.SemaphoreType.DMA((6 * len(self.ent),)), pltpu.SemaphoreType.DMA((6 * len(self.ent),))]

    def phases(self, sh, out, scr):
        send_sems, recv_sems = scr
        names, ent = self.names, self.ent
        x, y, cc = lax.axis_index("x"), lax.axis_index("y"), lax.axis_index("c")
        me = 2 * x + y
        sib = (x, y, 1 - cc)
        geo = [_neighbours(x, y, e[1]) for e in ent]
        started = []

        def copy(i, k, slot, pc, to):
            arr, _, roff, rows, _ = ent[i]
            hr = BIG_SHAPE[names[arr]][0] // 2
            ref = out[arr].at[slot, pl.ds(pc * hr + roff, rows), :]
            return pltpu.make_async_remote_copy(src_ref=ref, dst_ref=ref, send_sem=send_sems.at[6 * i + k],
                                                recv_sem=recv_sems.at[6 * i + k], device_id=to, device_id_type=MESH)

        def start(*a):
            copy(*a).start()
            started.append(a)

        def p0():
            for t, n in enumerate(names):
                _pipe(lambda v: (v,), [sh[t]], [out[t].at[me]], BIG_TR[n])
            for i in range(len(ent)):
                n1, n2, _, _, _ = geo[i]
                start(i, 0, me, cc, (*n1, cc))
                start(i, 1, me, cc, (*n2, cc))

        def p1():
            for i in range(len(ent)):
                n1, n2, s1, s2, sd = geo[i]
                copy(i, 0, s1, cc, sib).wait_recv()
                start(i, 2, s1, cc, (*n2, cc))
                start(i, 3, s1, cc, sib)
                copy(i, 1, s2, cc, sib).wait_recv()
                start(i, 4, s2, cc, sib)

        def p2():
            for i in range(len(ent)):
                sd = geo[i][4]
                copy(i, 2, sd, cc, sib).wait_recv()
                start(i, 5, sd, cc, sib)

        def p3():
            for i in range(len(ent)):
                _, _, s1, s2, sd = geo[i]
                copy(i, 3, s1, 1 - cc, sib).wait_recv()
                copy(i, 4, s2, 1 - cc, sib).wait_recv()
                copy(i, 5, sd, 1 - cc, sib).wait_recv()
            for a in started:
                copy(*a).wait_send()

        return [p0, p1, p2, p3]


class _ReduceJob:
    NW = 7

    def __init__(self, names, grads, at=None):
        self.names = names
        self.at = at
        self.inputs = list(grads)
        self.ent = _quarters(names)
        self.out_shapes = [jax.ShapeDtypeStruct(BIG_SHAPE[n], F32) for n in names]
        for arr, _, _, rows, _ in self.ent:
            c = BIG_SHAPE[names[arr]][1]
            self.out_shapes += [jax.ShapeDtypeStruct((4, rows, c), F32), jax.ShapeDtypeStruct((4, rows, c), F32),
                                jax.ShapeDtypeStruct((4, rows, c), BF16), jax.ShapeDtypeStruct((2, rows, c), BF16),
                                jax.ShapeDtypeStruct((1, rows, c), F32), jax.ShapeDtypeStruct((1, rows, c), BF16),
                                jax.ShapeDtypeStruct((1, rows, c), BF16)]
        self.scratch = [pltpu.SemaphoreType.DMA((5 * len(self.ent),)), pltpu.SemaphoreType.DMA((5 * len(self.ent),))]

    def phases(self, g, outs, scr):
        send_sems, recv_sems = scr
        names, ent, nw = self.names, self.ent, self.NW
        nt = len(names)
        fin, work = outs[:nt], outs[nt:]
        x, y, cc = lax.axis_index("x"), lax.axis_index("y"), lax.axis_index("c")
        me = 2 * x + y
        sib = (x, y, 1 - cc)
        geo = [_neighbours(x, y, e[1]) for e in ent]
        started = []
        wk = lambda i: work[nw * i:nw * (i + 1)]
        one = lambda ref, slot: ref.at[pl.ds(slot, 1)]

        def rows_of(i, pc):
            arr, _, roff, rows, _ = ent[i]
            return pl.ds(pc * (BIG_SHAPE[names[arr]][0] // 2) + roff, rows)

        def rcopy(i, k, src, dst, to):
            return pltpu.make_async_remote_copy(src_ref=src, dst_ref=dst, send_sem=send_sems.at[5 * i + k],
                                                recv_sem=recv_sems.at[5 * i + k], device_id=to, device_id_type=MESH)

        def start(make):
            make().start()
            started.append(make)

        def p0():
            for i, e in enumerate(ent):
                start(lambda i=i, e=e: rcopy(i, 0, g[e[0]].at[:, rows_of(i, 1 - cc), :], wk(i)[0], sib))

        def p1():
            for i, e in enumerate(ent):
                recv_a, p32, p16, r1 = wk(i)[:4]
                n1, n2, s1, s2, sd = geo[i]
                rcopy(i, 0, recv_a, recv_a, sib).wait_recv()
                _pipe(lambda a, b: (a + b, a + b), [g[e[0]].at[:, rows_of(i, cc), :], recv_a], [p32, p16], e[4])
                start(lambda i=i, s1=s1, n1=n1: rcopy(i, 1, one(wk(i)[2], s1), one(wk(i)[3], 0), (*n1, cc)))
                start(lambda i=i, sd=sd, n1=n1: rcopy(i, 2, one(wk(i)[2], sd), one(wk(i)[3], 1), (*n1, cc)))

        def p2():
            for i, e in enumerate(ent):
                _, p32, _, r1, qme, qs2, r2 = wk(i)
                n1, n2, s1, s2, sd = geo[i]
                rcopy(i, 1, one(r1, 0), one(r1, 0), sib).wait_recv()
                rcopy(i, 2, one(r1, 1), one(r1, 1), sib).wait_recv()
                _pipe(lambda a, b: (a + b.astype(F32),), [one(p32, s2), one(r1, 1)], [qs2], e[4])
                start(lambda i=i, n2=n2: rcopy(i, 3, wk(i)[5], wk(i)[6], (*n2, cc)))
                _pipe(lambda a, b: (a + b.astype(F32),), [one(p32, me), one(r1, 0)], [qme], e[4])

        def p3():
            for i, e in enumerate(ent):
                qme, r2 = wk(i)[4], wk(i)[6]
                rcopy(i, 3, r2, r2, sib).wait_recv()
                mine = fin[e[0]].at[rows_of(i, cc), :]
                _pipe(lambda a, b: (a + b.astype(F32),), [qme.at[0], r2.at[0]], [mine], e[4])
                start(lambda i=i, e=e: rcopy(i, 4, fin[e[0]].at[rows_of(i, cc), :], fin[e[0]].at[rows_of(i, cc), :], sib))

        def p4():
            for i, e in enumerate(ent):
                other = fin[e[0]].at[rows_of(i, 1 - cc), :]
                rcopy(i, 4, other, other, sib).wait_recv()
            for make in started:
                make().wait_send()

        return [p0, p1, p2, p3, p4]


def _run_job(job, name):
    ni, no = len(job.inputs), len(job.out_shapes)

    def body(*refs):
        for ph in job.phases(refs[:ni], refs[ni:ni + no], refs[ni + no:]):
            ph()

    return pl.pallas_call(
        body, name=name, in_specs=[ANY] * ni, out_specs=[ANY] * no, out_shape=job.out_shapes, scratch_shapes=job.scratch,
        compiler_params=pltpu.CompilerParams(vmem_limit_bytes=VMEM_LIMIT),
    )(*job.inputs)


def _hosted(body, *, name, grid, in_specs, out_specs, out_shape, scratch_shapes, args, sem, side=None):
    if side is None:
        return pl.pallas_call(body, name=name, grid=grid, in_specs=in_specs, out_specs=out_specs, out_shape=out_shape,
                              scratch_shapes=scratch_shapes, compiler_params=_cp(sem))(*args), None
    job = side
    ni, no, ns = len(in_specs), len(out_specs), len(scratch_shapes)
    ji, jo = len(job.inputs), len(job.out_shapes)
    n_steps = 1
    for extent in grid:
        n_steps *= extent

    def wrapped(*refs):
        own_in, refs = refs[:ni], refs[ni:]
        job_in, refs = refs[:ji], refs[ji:]
        own_out, refs = refs[:no], refs[no:]
        job_out, refs = refs[:jo], refs[jo:]
        own_scr, job_scr = refs[:ns], refs[ns:]
        step = 0
        for d, extent in enumerate(grid):
            step = step * extent + pl.program_id(d)
        phases = job.phases(job_in, job_out, job_scr)
        steps = [min(int(f * n_steps), n_steps - 1) for f in job.at] + [n_steps - 1]
        assert len(steps) == len(phases) and steps == sorted(steps)
        for at, ph in zip(steps, phases):
            pl.when(step == at)(ph)
        body(*own_in, *own_out, *own_scr)

    res = pl.pallas_call(
        wrapped, name=name, grid=grid, in_specs=list(in_specs) + [ANY] * ji, out_specs=list(out_specs) + [ANY] * jo,
        out_shape=list(out_shape) + list(job.out_shapes), scratch_shapes=list(scratch_shapes) + list(job.scratch),
        compiler_params=_cp(("arbitrary",) * len(grid)),
    )(*args, *job.inputs)
    return res[:no], res[no:]


def _proj_dw(xn, dproj_sh, *, tm=512, tk=1024):
    s, d = xn.shape
    tk = _tile(s, tk)
    nk = s // tk

    def body(a_ref, b_ref, o_ref, acc):
        kk = pl.program_id(2)
        part = _dot_tn(a_ref[...], b_ref[0])

        @pl.when(kk == 0)
        def _():
            acc[...] = part

        @pl.when(kk > 0)
        def _():
            acc[...] += part

        @pl.when(kk == nk - 1)
        def _():
            o_ref[0] = acc[...]

    return pl.pallas_call(
        body, name="proj_dw", grid=(N_CHIPS, d // tm, nk),
        in_specs=[pl.BlockSpec((tk, tm), lambda j, i, q: (q, i)), pl.BlockSpec((1, tk, W_IN_PAD), lambda j, i, q: (j, q, 0))],
        out_specs=pl.BlockSpec((1, tm, W_IN_PAD), lambda j, i, q: (j, i, 0)),
        out_shape=jax.ShapeDtypeStruct((N_CHIPS, d, W_IN_PAD), F32), scratch_shapes=[pltpu.VMEM((tm, W_IN_PAD), F32)],
        compiler_params=_cp(("parallel", "parallel", "arbitrary")),
    )(xn, dproj_sh)


def _proj_dx(dproj_sh, w_sh, *, tm=1024):
    s = dproj_sh.shape[1]
    d = w_sh.shape[1]
    tm = _tile(s, tm)

    def body(a_ref, b_ref, o_ref, acc):
        kk = pl.program_id(1)
        part = _dot_nt(a_ref[0], b_ref[0])

        @pl.when(kk == 0)
        def _():
            acc[...] = part

        @pl.when(kk > 0)
        def _():
            acc[...] += part

        @pl.when(kk == N_CHIPS - 1)
        def _():
            o_ref[...] = acc[...]

    return pl.pallas_call(
        body, name="proj_dx", grid=(s // tm, N_CHIPS),
        in_specs=[pl.BlockSpec((1, tm, W_IN_PAD), lambda i, q: (q, i, 0)), pl.BlockSpec((1, d, W_IN_PAD), lambda i, q: (q, 0, 0))],
        out_specs=pl.BlockSpec((tm, d), lambda i, q: (i, 0)),
        out_shape=jax.ShapeDtypeStruct((s, d), F32), scratch_shapes=[pltpu.VMEM((tm, d), F32)],
        compiler_params=_cp(("parallel", "arbitrary")),
    )(dproj_sh, w_sh)


def _up_dx(dup, w_sh, *, tm=1024):
    s = dup.shape[1]
    d, wsh = w_sh.shape[1:]
    tm = _tile(s, tm)

    def body(a_ref, b_ref, o_ref, acc):
        kk = pl.program_id(1)
        part = _dot_nt(a_ref[0], b_ref[0])

        @pl.when(kk == 0)
        def _():
            acc[...] = part

        @pl.when(kk > 0)
        def _():
            acc[...] += part

        @pl.when(kk == N_CHIPS - 1)
        def _():
            o_ref[...] = acc[...]

    return pl.pallas_call(
        body, name="up_dx", grid=(s // tm, N_CHIPS),
        in_specs=[pl.BlockSpec((1, tm, wsh), lambda i, q: (q >> 1, i, q & 1)), pl.BlockSpec((1, d, wsh), lambda i, q: (q, 0, 0))],
        out_specs=pl.BlockSpec((tm, d), lambda i, q: (i, 0)),
        out_shape=jax.ShapeDtypeStruct((s, d), F32), scratch_shapes=[pltpu.VMEM((tm, d), F32)],
        compiler_params=_cp(("parallel", "arbitrary")),
    )(dup, w_sh)


def _up_dw(hn, dup, *, tk=1024):
    s, d = hn.shape
    wsh = 2 * D_FF // N_CHIPS
    tk = _tile(s, tk)
    nk = s // tk

    def body(a_ref, b_ref, o_ref, acc):
        kk = pl.program_id(1)
        part = _dot_tn(a_ref[...], b_ref[0])

        @pl.when(kk == 0)
        def _():
            acc[...] = part

        @pl.when(kk > 0)
        def _():
            acc[...] += part

        @pl.when(kk == nk - 1)
        def _():
            o_ref[0] = acc[...]

    return pl.pallas_call(
        body, name="up_dw", grid=(N_CHIPS, nk),
        in_specs=[pl.BlockSpec((tk, d), lambda j, q: (q, 0)), pl.BlockSpec((1, tk, wsh), lambda j, q: (j >> 1, q, j & 1))],
        out_specs=pl.BlockSpec((1, d, wsh), lambda j, q: (j, 0, 0)),
        out_shape=jax.ShapeDtypeStruct((N_CHIPS, d, wsh), F32), scratch_shapes=[pltpu.VMEM((d, wsh), F32)],
        compiler_params=_cp(("parallel", "arbitrary")),
    )(hn, dup)


BIG_ROWS =(IN_DIM // 4, Q_DIM // 4, D_INNER // 4, D_MODEL // 4, 2 * D_FF // 4, D_FF // 4)
PACK_ROWS = 5376


def _pack_shards(parts):
    rows = [p.reshape(-1, D_MODEL) for p in parts]
    pad = PACK_ROWS - sum(BIG_ROWS)
    return jnp.concatenate(rows + [jnp.zeros((pad, D_MODEL), rows[0].dtype)], axis=0)


def _unpack_shards(buf):
    out, off = [], 0
    for n in BIG_ROWS:
        out.append(buf[off:off + n])
        off += n
    return out


def _permute_cols_in(w):
    pad = jnp.zeros((w.shape[0], PW - IN_DIM), w.dtype)
    return jnp.concatenate([w[:, :6656], w[:, 6688:], w[:, 6656:6688], pad], axis=1)


def _unpermute_cols_in(g):
    return jnp.concatenate([g[:, :6656], g[:, O_DT:O_DT + 32], g[:, 6656:O_DT]], axis=1)


SMALL = ("norm1_w", "b_gate", "attn_sinks", "ssd_conv_b", "dt_bias", "a_log", "d_skip", "ssd_norm_w", "norm2_w",
         "ffn_conv_b", "final_norm_w", "ssd_conv_w", "ffn_conv_w")


def _pad128(v):
    v = v.reshape(-1)
    return jnp.pad(v, (0, (-v.shape[0]) % 128))


def _pack_small(parts):
    flat = jnp.concatenate([_pad128(p) for p in parts])
    flat = jnp.pad(flat, (0, (-flat.shape[0]) % 1024))
    return flat.reshape(-1, 128)


def _unpack_small(buf, shapes):
    flat, out, off = buf.reshape(-1), [], 0
    for shp in shapes:
        n = 1
        for q in shp:
            n *= q
        out.append(flat[off:off + n].reshape(shp))
        off += n + (-n) % 128
    return out


def _vec128(v):
    return jnp.pad(v.reshape(1, -1), ((0, 0), (0, 128 - v.shape[-1])))


def kernel(x, norm1_w, w_in, b_gate, attn_sinks, w_attn_o, ssd_conv_w, ssd_conv_b, dt_bias, a_log, d_skip, ssd_norm_w, w_ssd_o, w_out, norm2_w, w_up, ffn_conv_w, ffn_conv_b, w_down, final_norm_w, loss_target, m_norm1_w, m_w_in, m_b_gate, m_attn_sinks, m_w_attn_o, m_ssd_conv_w, m_ssd_conv_b, m_dt_bias, m_a_log, m_d_skip, m_ssd_norm_w, m_w_ssd_o, m_w_out, m_norm2_w, m_w_up, m_ffn_conv_w, m_ffn_conv_b, m_w_down, m_final_norm_w, v_norm1_w, v_w_in, v_b_gate, v_attn_sinks, v_w_attn_o, v_ssd_conv_w, v_ssd_conv_b, v_dt_bias, v_a_log, v_d_skip, v_ssd_norm_w, v_w_ssd_o, v_w_out, v_norm2_w, v_w_up, v_ffn_conv_w, v_ffn_conv_b, v_w_down, v_final_norm_w):
    ix, iy, ic = lax.axis_index("x"), lax.axis_index("y"), lax.axis_index("c")
    chip = 2 * ix + iy
    x2 = x[0]
    tgt = loss_target[0]
    s = x2.shape[0]

    wsh = IN_DIM // N_CHIPS
    big_shards = dict(w_in=jnp.pad(w_in[0], ((0, 0), (0, W_IN_PAD - wsh))), w_attn_o=w_attn_o[0], w_ssd_o=w_ssd_o[0],
                      w_out=w_out[0], w_up=w_up[0], w_down=w_down[0])
    gathered = {}
    (gathered["w_in"],) = _run_job(_GatherJob(("w_in",), [big_shards["w_in"]]), "gather_w_in")
    early, late = ("w_attn_o", "w_ssd_o", "w_out"), ("w_up", "w_down")
    gather_early = _GatherJob(early, [big_shards[n] for n in early], at=(0.0, 0.5, 0.8))
    gather_late = _GatherJob(late, [big_shards[n] for n in late], at=(0.0, 0.55, 0.85))
    gw = gathered["w_in"]
    lo, hi = O_GA - 3 * wsh, O_GA + N_SSD_HEADS - 3 * wsh
    w_in_p = jnp.concatenate([gw[0, :, :wsh], gw[1, :, :wsh], gw[2, :, :wsh], gw[3, :, :lo], gw[3, :, hi:wsh],
                              gw[3, :, lo:hi], jnp.zeros((D_MODEL, PW - IN_DIM), BF16)], axis=1)
    small_sh = _pack_small([ssd_conv_w[0], ffn_conv_w[0]])
    small_all = _all_gather_small(small_sh)
    sc_parts = [_unpack_small(small_all[j], [(4, XBC_DIM // 4), (3, 2 * D_FF // 4)]) for j in range(N_CHIPS)]
    ssd_cw = jnp.concatenate([p[0] for p in sc_parts], axis=1)
    ffn_cw = jnp.concatenate([p[1] for p in sc_parts], axis=1)

    sinks128 = _vec128(attn_sinks)
    dtb128, alog128, dskip128 = _vec128(dt_bias), _vec128(a_log), _vec128(d_skip)

    xn = _rms_fwd(x2, norm1_w, name="norm1_fwd")
    proj, got = _mm(xn, w_in_p, name="proj_fwd", tn=1280, side=gather_early)
    gathered.update(zip(early, got))
    attn_pre, got = _attn_fwd(proj, sinks128, side=gather_late)
    gathered.update(zip(late, got))
    full = {n: gathered[n].reshape(-1, D_MODEL) for n in ("w_attn_o", "w_ssd_o", "w_out", "w_down")}
    full["w_up"] = gathered["w_up"]
    attn = _mm(attn_pre, full["w_attn_o"], name="attn_o_fwd")
    xbc = _ssd_conv_fwd(proj, ssd_cw, ssd_conv_b)
    y_ssd, hprev = _ssd_fwd(xbc, proj, dtb128, alog128, dskip128)
    yn = _gate_norm_fwd(y_ssd, proj, ssd_norm_w)
    ssd_out = _mm(yn, full["w_ssd_o"], name="ssd_o_fwd")
    merged = _merge_fwd(proj, b_gate, attn, ssd_out)
    h1 = _mm(merged, full["w_out"], name="out_fwd", resid=x2)
    hn = _rms_fwd(h1, norm2_w, name="norm2_fwd")
    up = _mm(hn, full["w_up"], name="up_fwd")
    act = _ffn_act_fwd(up, ffn_cw, ffn_conv_b)
    h2 = _mm(act, full["w_down"], name="down_fwd", resid=h1, tk=1408)

    dh2, loss_blk, g_final = _loss_bwd(h2, tgt, final_norm_w.reshape(1, -1))
    dact = _mm(dh2, full["w_down"], name="down_dx", tb=True, tn=1408)
    g_down = _mm(act, dh2, name="down_dw", ta=True, tm=1408)
    dup, g_ffn_cw, g_ffn_cb = _ffn_act_bwd(dact, up, ffn_cw, ffn_conv_b)
    dhn = _up_dx(dup, full["w_up"])
    g_up = _up_dw(hn, dup)
    dh1, g_norm2 = _rms_bwd(dhn, h1, norm2_w, dh2, name="norm2_bwd")
    dmerged = _mm(dh1, full["w_out"], name="out_dx", tb=True)
    g_out = _mm(merged, dh1, name="out_dw", ta=True)
    dattn, dssd_out, dga, dgs, g_ba, g_bs = _merge_bwd(dmerged, proj, b_gate, attn, ssd_out)
    dyn = _mm(dssd_out, full["w_ssd_o"], name="ssd_o_dx", tb=True)
    g_ssd_o = _mm(yn, dssd_out, name="ssd_o_dw", ta=True)
    dy_ssd, dz, g_ssd_norm = _gate_norm_bwd(dyn, y_ssd, proj, ssd_norm_w)
    slot = lambda g: g.reshape(N_CHIPS, -1, D_MODEL)
    big_grads = {}
    red = ("w_down", "w_up")
    (dxbc, ddt, dvec), got = _ssd_bwd(xbc, proj, dtb128, alog128, dskip128, hprev, dy_ssd,
                                      side=_ReduceJob(red, [slot(g_down), g_up], at=(0.0, 0.3, 0.8, 0.95)))
    big_grads.update(zip(red, got))
    dxbc_raw, g_ssd_cw, g_ssd_cb = _ssd_conv_bwd(dxbc, proj, ssd_cw, ssd_conv_b)
    dattn_pre = _mm(dattn, full["w_attn_o"], name="attn_o_dx", tb=True)
    g_attn_o = _mm(attn_pre, dattn, name="attn_o_dw", ta=True)
    red = ("w_out", "w_ssd_o", "w_attn_o")
    (dq, dk, dv, dsk), got = _attn_bwd(proj, sinks128, attn_pre, dattn_pre,
                                       side=_ReduceJob(red, [slot(g_out), slot(g_ssd_o), slot(g_attn_o)],
                                                       at=(0.0, 0.2, 0.5, 0.7)))
    big_grads.update(zip(red, got))
    pieces = [dq, dk, dv, dz, dxbc_raw, ddt[:, :N_SSD_HEADS], dga, dgs]
    shards_d, off = [[] for _ in range(N_CHIPS)], 0
    for p in pieces:
        for j in range(N_CHIPS):
            a, b = max(off, j * wsh), min(off + p.shape[1], (j + 1) * wsh)
            if a < b:
                shards_d[j].append(p[:, a - off:b - off])
        off += p.shape[1]
    zpad = jnp.zeros((s, W_IN_PAD - wsh), BF16)
    dproj_sh = jnp.stack([jnp.concatenate(sh + [zpad], axis=1) for sh in shards_d])
    dxn = _proj_dx(dproj_sh, gathered["w_in"])
    g_in = _proj_dw(xn, dproj_sh)
    dx, g_norm1 = _rms_bwd(dxn, x2, norm1_w, dh1, name="norm1_bwd")

    big_grads["w_in"] = _run_job(_ReduceJob(("w_in",), [g_in]), "reduce_w_in")[0][:, :wsh]

    small_g = dict(
        norm1_w=g_norm1, b_gate=jnp.concatenate([g_ba, g_bs], axis=1), attn_sinks=dsk[0:1, :16], ssd_conv_b=g_ssd_cb,
        dt_bias=dvec[0:1, :32], a_log=dvec[1:2, :32], d_skip=dvec[2:3, :32], ssd_norm_w=g_ssd_norm, norm2_w=g_norm2,
        ffn_conv_b=jnp.concatenate([g_ffn_cb[0], g_ffn_cb[1]], axis=1), final_norm_w=g_final, ssd_conv_w=g_ssd_cw,
        ffn_conv_w=jnp.concatenate([g_ffn_cw[0], g_ffn_cw[1]], axis=1))
    small_buf = _pack_small([small_g[n] for n in SMALL] + [loss_blk])
    small_sum = _all_reduce_small(small_buf)
    small_shapes = [(1, D_MODEL), (1, 2 * D_MODEL), (1, 16), (1, XBC_DIM), (1, 32), (1, 32), (1, 32), (1, D_INNER),
                    (1, D_MODEL), (1, 2 * D_FF), (D_MODEL,), (4, XBC_DIM), (3, 2 * D_FF), (1, 128)]
    small_list = _unpack_small(small_sum, small_shapes)
    loss = small_list[-1][0, 0]
    grads = dict(zip(SMALL, small_list[:-1]))
    grads["ssd_conv_w"] = lax.dynamic_slice_in_dim(grads["ssd_conv_w"], chip * (XBC_DIM // 4), XBC_DIM // 4, axis=1)
    grads["ffn_conv_w"] = lax.dynamic_slice_in_dim(grads["ffn_conv_w"], chip * (2 * D_FF // 4), 2 * D_FF // 4, axis=1)
    grads.update(big_grads)

    weights = dict(norm1_w=norm1_w, w_in=w_in, b_gate=b_gate, attn_sinks=attn_sinks, w_attn_o=w_attn_o, ssd_conv_w=ssd_conv_w,
                   ssd_conv_b=ssd_conv_b, dt_bias=dt_bias, a_log=a_log, d_skip=d_skip, ssd_norm_w=ssd_norm_w, w_ssd_o=w_ssd_o,
                   w_out=w_out, norm2_w=norm2_w, w_up=w_up, ffn_conv_w=ffn_conv_w, ffn_conv_b=ffn_conv_b, w_down=w_down,
                   final_norm_w=final_norm_w)
    ms = dict(norm1_w=m_norm1_w, w_in=m_w_in, b_gate=m_b_gate, attn_sinks=m_attn_sinks, w_attn_o=m_w_attn_o,
              ssd_conv_w=m_ssd_conv_w, ssd_conv_b=m_ssd_conv_b, dt_bias=m_dt_bias, a_log=m_a_log, d_skip=m_d_skip,
              ssd_norm_w=m_ssd_norm_w, w_ssd_o=m_w_ssd_o, w_out=m_w_out, norm2_w=m_norm2_w, w_up=m_w_up,
              ffn_conv_w=m_ffn_conv_w, ffn_conv_b=m_ffn_conv_b, w_down=m_w_down, final_norm_w=m_final_norm_w)
    vs = dict(norm1_w=v_norm1_w, w_in=v_w_in, b_gate=v_b_gate, attn_sinks=v_attn_sinks, w_attn_o=v_w_attn_o,
              ssd_conv_w=v_ssd_conv_w, ssd_conv_b=v_ssd_conv_b, dt_bias=v_dt_bias, a_log=v_a_log, d_skip=v_d_skip,
              ssd_norm_w=v_ssd_norm_w, w_ssd_o=v_w_ssd_o, w_out=v_w_out, norm2_w=v_norm2_w, w_up=v_w_up,
              ffn_conv_w=v_ffn_conv_w, ffn_conv_b=v_ffn_conv_b, w_down=v_w_down, final_norm_w=v_final_norm_w)
    order = list(weights)
    deltas, new_m, new_v = {}, {}, {}
    for n in BIG:
        shp = weights[n].shape
        d_, m_, v_ = _adamw(weights[n][0], grads[n], ms[n][0], vs[n][0], name="adamw_" + n)
        deltas[n], new_m[n], new_v[n] = d_.reshape(shp), m_.reshape(shp), v_.reshape(shp)
    smalls = [n for n in order if n not in BIG]
    as2d = lambda a: a.reshape(-1, a.shape[-1])
    res = _adamw_many(*[[as2d(src[n][0] if src[n].ndim == 3 else src[n]) for n in smalls] for src in (weights, grads, ms, vs)])
    for i, n in enumerate(smalls):
        deltas[n], new_m[n], new_v[n] = (res[q * len(smalls) + i].reshape(weights[n].shape) for q in range(3))
    out_grads = [grads[n].reshape(weights[n].shape) for n in order]
    return (loss, dx[None], *out_grads, *[deltas[n] for n in order], *[new_m[n] for n in order], *[new_v[n] for n in order])
```

```python
import functools

import jax
import jax.numpy as jnp
from jax import lax
from jax.experimental import pallas as pl
from jax.experimental.pallas import tpu as pltpu

F32 = jnp.float32
BF16 = jnp.bfloat16
HI = lax.Precision.HIGHEST

D_MODEL = 1024
Q_DIM = 1024
KV_DIM = 256
D_INNER = 2048
BC_DIM = 512
XBC_DIM = 3072
N_SSD_HEADS = 32
D_FF = 2816
IN_DIM = 8736
BLK = 128
EPS = 1e-5
NEG = -1e30

O_Q, O_K, O_V, O_Z, O_X, O_GA, O_GS, O_DT = 0, 1024, 1280, 1536, 3584, 6656, 7680, 8704
PW = 8960

ADAM_LR, ADAM_B1, ADAM_B2, ADAM_EPS, ADAM_WD, ADAM_STEP = 0.001, 0.9, 0.999, 1e-08, 0.01, 10

VMEM_LIMIT = 52 * 1024 * 1024
MESH = pl.DeviceIdType.MESH


def _cp(sem=None):
    return pltpu.CompilerParams(dimension_semantics=sem, vmem_limit_bytes=VMEM_LIMIT)


def _dot(a, b, prec=None):
    return jnp.dot(a, b, preferred_element_type=F32, precision=prec)


def _dot_nt(a, b, prec=None):
    return lax.dot_general(a, b, (((1,), (1,)), ((), ())), preferred_element_type=F32, precision=prec)


def _dot_tn(a, b, prec=None):
    return lax.dot_general(a, b, (((0,), (0,)), ((), ())), preferred_element_type=F32, precision=prec)


def _sigmoid(x):
    return 0.5 * jnp.tanh(0.5 * x) + 0.5


def _tile(n, want):
    t = min(n, want)
    while n % t:
        t -= 128
    return t


def _mm(a, b, *, name, ta=False, tb=False, out_dtype=F32, resid=None, tm=1024, tn=1024, tk=1024, side=None):
    m, k = (a.shape[1], a.shape[0]) if ta else a.shape
    slots = b.ndim == 3
    if slots:
        n = b.shape[1] if tb else b.shape[0] * b.shape[2]
        tn, tk = (tn, b.shape[2]) if tb else (b.shape[2], tk)
    else:
        n = b.shape[0] if tb else b.shape[1]
    tm, tn, tk = _tile(m, tm), _tile(n, tn), _tile(k, tk)
    nk = k // tk
    dn = (((0 if ta else 1,), (1 if tb else 0,)), ((), ()))

    def body(*refs):
        if resid is None:
            a_ref, b_ref, o_ref, acc = refs
        else:
            a_ref, b_ref, r_ref, o_ref, acc = refs
        kk = pl.program_id(2)
        bv = b_ref[0] if slots else b_ref[...]
        part = lax.dot_general(a_ref[...].astype(BF16), bv.astype(BF16), dn, preferred_element_type=F32)

        @pl.when(kk == 0)
        def _():
            acc[...] = part

        @pl.when(kk > 0)
        def _():
            acc[...] += part

        @pl.when(kk == nk - 1)
        def _():
            r = acc[...]
            if resid is not None:
                r = r + r_ref[...]
            o_ref[...] = r.astype(out_dtype)

    a_spec = pl.BlockSpec((tk, tm), lambda i, j, q: (q, i)) if ta else pl.BlockSpec((tm, tk), lambda i, j, q: (i, q))
    if slots:
        b_spec = (pl.BlockSpec((1, tn, tk), lambda i, j, q: (q, j, 0)) if tb
                  else pl.BlockSpec((1, tk, tn), lambda i, j, q: (j, q, 0)))
    else:
        b_spec = pl.BlockSpec((tn, tk), lambda i, j, q: (j, q)) if tb else pl.BlockSpec((tk, tn), lambda i, j, q: (q, j))
    o_spec = pl.BlockSpec((tm, tn), lambda i, j, q: (i, j))
    ins, specs = [a, b], [a_spec, b_spec]
    if resid is not None:
        ins.append(resid)
        specs.append(o_spec)
    own, extra = _hosted(
        body, name=name, grid=(m // tm, n // tn, nk), in_specs=specs, out_specs=[o_spec],
        out_shape=[jax.ShapeDtypeStruct((m, n), out_dtype)], scratch_shapes=[pltpu.VMEM((tm, tn), F32)],
        args=ins, sem=("parallel", "parallel", "arbitrary"), side=side)
    return own[0] if side is None else (own[0], extra)


def _rms_fwd(x, w, *, name, tm=512):
    s, d = x.shape
    tm = _tile(s, tm)

    def body(x_ref, w_ref, o_ref):
        xv = x_ref[...]
        r = lax.rsqrt(jnp.mean(xv * xv, axis=-1, keepdims=True) + EPS)
        o_ref[...] = ((xv * r) * w_ref[...]).astype(BF16)

    return pl.pallas_call(
        body, name=name, grid=(s // tm,),
        in_specs=[pl.BlockSpec((tm, d), lambda i: (i, 0)), pl.BlockSpec((1, d), lambda i: (0, 0))],
        out_specs=pl.BlockSpec((tm, d), lambda i: (i, 0)),
        out_shape=jax.ShapeDtypeStruct((s, d), BF16), compiler_params=_cp(("parallel",)),
    )(x, w)


def _rms_bwd(dy, x, w, resid, *, name, tm=512):
    s, d = x.shape
    tm = _tile(s, tm)

    def body(dy_ref, x_ref, w_ref, r_ref, dx_ref, dw_ref):
        i = pl.program_id(0)
        xv = x_ref[...]
        r = lax.rsqrt(jnp.mean(xv * xv, axis=-1, keepdims=True) + EPS)
        xh = xv * r
        dyv = dy_ref[...]
        g = dyv * w_ref[...]
        dx_ref[...] = r_ref[...] + r * (g - xh * jnp.mean(g * xh, axis=-1, keepdims=True))
        part = jnp.sum(dyv * xh, axis=0, keepdims=True)

        @pl.when(i == 0)
        def _():
            dw_ref[...] = part

        @pl.when(i > 0)
        def _():
            dw_ref[...] += part

    row = pl.BlockSpec((tm, d), lambda i: (i, 0))
    vec = pl.BlockSpec((1, d), lambda i: (0, 0))
    return pl.pallas_call(
        body, name=name, grid=(s // tm,), in_specs=[row, row, vec, row], out_specs=[row, vec],
        out_shape=[jax.ShapeDtypeStruct((s, d), F32), jax.ShapeDtypeStruct((1, d), F32)],
        compiler_params=_cp(("arbitrary",)),
    )(dy, x, w, resid)


def _loss_bwd(h2, tgt, wf, *, tm=512):
    s, d = h2.shape
    tm = _tile(s, tm)

    def body(h_ref, t_ref, w_ref, dh_ref, loss_ref, dw_ref):
        i = pl.program_id(0)
        hv = h_ref[...]
        r = lax.rsqrt(jnp.mean(hv * hv, axis=-1, keepdims=True) + EPS)
        xh = hv * r
        wv = w_ref[...]
        e = xh * wv - t_ref[...]
        lpart = 0.5 * jnp.sum(jnp.mean(e * e, axis=-1, keepdims=True), axis=0, keepdims=True)
        dout = e * (1.0 / d)
        g = dout * wv
        dh_ref[...] = r * (g - xh * jnp.mean(g * xh, axis=-1, keepdims=True))
        part = jnp.sum(dout * xh, axis=0, keepdims=True)
        lrow = jnp.broadcast_to(lpart, (1, 128))

        @pl.when(i == 0)
        def _():
            dw_ref[...] = part
            loss_ref[...] = lrow

        @pl.when(i > 0)
        def _():
            dw_ref[...] += part
            loss_ref[...] += lrow

    row = pl.BlockSpec((tm, d), lambda i: (i, 0))
    vec = pl.BlockSpec((1, d), lambda i: (0, 0))
    return pl.pallas_call(
        body, name="loss_bwd", grid=(s // tm,), in_specs=[row, row, vec],
        out_specs=[row, pl.BlockSpec((1, 128), lambda i: (0, 0)), vec],
        out_shape=[jax.ShapeDtypeStruct((s, d), F32), jax.ShapeDtypeStruct((1, 128), F32),
                   jax.ShapeDtypeStruct((1, d), F32)],
        compiler_params=_cp(("arbitrary",)),
    )(h2, tgt, wf)


def _attn_mask(n):
    qi = lax.broadcasted_iota(jnp.int32, (4 * BLK, 2 * BLK), 0) & (BLK - 1)
    si = lax.broadcasted_iota(jnp.int32, (4 * BLK, 2 * BLK), 1)
    dist = BLK + qi - si
    kpos = n * BLK - BLK + si
    return (dist >= 0) & (dist < BLK) & (kpos >= 0)


def _attn_probs(q_ref, kc_ref, kp_ref, sk_ref, kvh, valid):
    hs = slice(kvh * 64, (kvh + 1) * 64)
    kb = jnp.concatenate([kp_ref[:, hs], kc_ref[:, hs]], axis=0).astype(BF16)
    qs = jnp.concatenate([q_ref[:, (kvh * 4 + g) * 64:(kvh * 4 + g + 1) * 64] for g in range(4)], axis=0).astype(BF16)
    s = _dot_nt(qs, kb) * 0.125
    s = jnp.where(valid, s, NEG)
    sink = jnp.concatenate(
        [jnp.broadcast_to(sk_ref[0:1, kvh * 4 + g:kvh * 4 + g + 1], (BLK, 1)) for g in range(4)], axis=0)
    m = jnp.maximum(jnp.max(s, axis=1, keepdims=True), sink)
    p = jnp.where(valid, jnp.exp(s - m), 0.0)
    es = jnp.exp(sink - m)
    denom = jnp.sum(p, axis=1, keepdims=True) + es
    return qs, kb, p / denom, es / denom


def _attn_fwd(proj, sinks, side=None):
    s = proj.shape[0]
    nb = s // BLK

    def body(q_ref, kc_ref, kp_ref, vc_ref, vp_ref, sk_ref, o_ref):
        valid = _attn_mask(pl.program_id(0))
        for kvh in range(4):
            hs = slice(kvh * 64, (kvh + 1) * 64)
            _, _, probs, _ = _attn_probs(q_ref, kc_ref, kp_ref, sk_ref, kvh, valid)
            vb = jnp.concatenate([vp_ref[:, hs], vc_ref[:, hs]], axis=0).astype(BF16)
            o = _dot(probs.astype(BF16), vb)
            for g in range(4):
                h = kvh * 4 + g
                o_ref[:, h * 64:(h + 1) * 64] = o[g * BLK:(g + 1) * BLK].astype(BF16)

    prev = lambda n: jnp.maximum(n - 1, 0)
    own, extra = _hosted(
        body, name="attn_fwd", grid=(nb,),
        in_specs=[pl.BlockSpec((BLK, Q_DIM), lambda n: (n, 0)),
                  pl.BlockSpec((BLK, KV_DIM), lambda n: (n, O_K // KV_DIM)),
                  pl.BlockSpec((BLK, KV_DIM), lambda n: (prev(n), O_K // KV_DIM)),
                  pl.BlockSpec((BLK, KV_DIM), lambda n: (n, O_V // KV_DIM)),
                  pl.BlockSpec((BLK, KV_DIM), lambda n: (prev(n), O_V // KV_DIM)),
                  pl.BlockSpec((1, 128), lambda n: (0, 0))],
        out_specs=[pl.BlockSpec((BLK, Q_DIM), lambda n: (n, 0))],
        out_shape=[jax.ShapeDtypeStruct((s, Q_DIM), BF16)], scratch_shapes=[],
        args=(proj, proj, proj, proj, proj, sinks), sem=("parallel",), side=side)
    return own[0] if side is None else (own[0], extra)


def _attn_bwd(proj, sinks, o, do, side=None):
    s = proj.shape[0]
    nb = s // BLK

    def body(q_ref, kc_ref, kp_ref, vc_ref, vp_ref, sk_ref, o_ref, do_ref,
             dq_ref, dk_ref, dv_ref, dsk_ref, ck, cv, nkp, nkc, nvp, nvc):
        n = pl.program_id(0)

        @pl.when(n == 0)
        def _():
            ck[...] = jnp.zeros_like(ck)
            cv[...] = jnp.zeros_like(cv)
            dsk_ref[...] = jnp.zeros_like(dsk_ref)

        @pl.when(n < nb)
        def _():
            valid = _attn_mask(n)
            lane = lax.broadcasted_iota(jnp.int32, (1, 128), 1)
            dsk = jnp.zeros((1, 128), F32)
            for kvh in range(4):
                hs = slice(kvh * 64, (kvh + 1) * 64)
                qs, kb, probs, psink = _attn_probs(q_ref, kc_ref, kp_ref, sk_ref, kvh, valid)
                vb = jnp.concatenate([vp_ref[:, hs], vc_ref[:, hs]], axis=0).astype(BF16)
                heads = [slice((kvh * 4 + g) * 64, (kvh * 4 + g + 1) * 64) for g in range(4)]
                dos = jnp.concatenate([do_ref[:, hh] for hh in heads], axis=0)
                os_ = jnp.concatenate([o_ref[:, hh] for hh in heads], axis=0).astype(F32)
                delta = jnp.sum(dos * os_, axis=1, keepdims=True)
                dos16 = dos.astype(BF16)
                dp = _dot_nt(dos16, vb)
                ds = (probs * (dp - delta) * 0.125).astype(BF16)
                dqs = _dot(ds, kb)
                dkb = _dot_tn(ds, qs)
                dvb = _dot_tn(probs.astype(BF16), dos16)
                nkp[:, hs] = dkb[:BLK]
                nkc[:, hs] = dkb[BLK:]
                nvp[:, hs] = dvb[:BLK]
                nvc[:, hs] = dvb[BLK:]
                sd = psink * delta
                for g in range(4):
                    dq_ref[:, heads[g]] = dqs[g * BLK:(g + 1) * BLK].astype(BF16)
                    val = -jnp.sum(sd[g * BLK:(g + 1) * BLK], axis=0, keepdims=True)
                    dsk = dsk + jnp.where(lane == kvh * 4 + g, val, 0.0)
            dsk_ref[0:1, :] += dsk
            dk_ref[...] = (ck[...] + nkp[...]).astype(BF16)
            dv_ref[...] = (cv[...] + nvp[...]).astype(BF16)
            ck[...] = nkc[...]
            cv[...] = nvc[...]

        @pl.when(n == nb)
        def _():
            dk_ref[...] = ck[...].astype(BF16)
            dv_ref[...] = cv[...].astype(BF16)

    cur = lambda n: jnp.minimum(n, nb - 1)
    prev = lambda n: jnp.maximum(jnp.minimum(n, nb - 1) - 1, 0)
    outb = lambda n: jnp.maximum(n - 1, 0)
    kv_scr = pltpu.VMEM((BLK, KV_DIM), F32)
    own, extra = _hosted(
        body, name="attn_bwd", grid=(nb + 1,),
        in_specs=[pl.BlockSpec((BLK, Q_DIM), lambda n: (cur(n), 0)),
                  pl.BlockSpec((BLK, KV_DIM), lambda n: (cur(n), O_K // KV_DIM)),
                  pl.BlockSpec((BLK, KV_DIM), lambda n: (prev(n), O_K // KV_DIM)),
                  pl.BlockSpec((BLK, KV_DIM), lambda n: (cur(n), O_V // KV_DIM)),
                  pl.BlockSpec((BLK, KV_DIM), lambda n: (prev(n), O_V // KV_DIM)),
                  pl.BlockSpec((1, 128), lambda n: (0, 0)),
                  pl.BlockSpec((BLK, Q_DIM), lambda n: (cur(n), 0)),
                  pl.BlockSpec((BLK, Q_DIM), lambda n: (cur(n), 0))],
        out_specs=[pl.BlockSpec((BLK, Q_DIM), lambda n: (cur(n), 0)),
                   pl.BlockSpec((BLK, KV_DIM), lambda n: (outb(n), 0)),
                   pl.BlockSpec((BLK, KV_DIM), lambda n: (outb(n), 0)),
                   pl.BlockSpec((8, 128), lambda n: (0, 0))],
        out_shape=[jax.ShapeDtypeStruct((s, Q_DIM), BF16), jax.ShapeDtypeStruct((s, KV_DIM), BF16),
                   jax.ShapeDtypeStruct((s, KV_DIM), BF16), jax.ShapeDtypeStruct((8, 128), F32)],
        scratch_shapes=[kv_scr] * 6, args=(proj, proj, proj, proj, proj, sinks, o, do), sem=("arbitrary",), side=side)
    return own if side is None else (own, extra)


def _shift_down(x, j):
    if j == 0:
        return x
    row = lax.broadcasted_iota(jnp.int32, x.shape, 0)
    return jnp.where(row >= j, pltpu.roll(x, j, 0), 0.0)


def _shift_up(x, j):
    if j == 0:
        return x
    s = x.shape[0]
    row = lax.broadcasted_iota(jnp.int32, x.shape, 0)
    return jnp.where(row < s - j, pltpu.roll(x, s - j, 0), 0.0)


def _conv(x, w_ref, b_ref):
    kk = w_ref.shape[0]
    y = _shift_down(x, kk - 1) * w_ref[0:1, :]
    for q in range(1, kk):
        y = y + _shift_down(x, kk - 1 - q) * w_ref[q:q + 1, :]
    return y + b_ref[...]


def _conv_bwd(dy, x, w_ref, dx_dtype):
    kk = w_ref.shape[0]
    dx = _shift_up(dy, kk - 1) * w_ref[0:1, :]
    dws = [jnp.sum(dy * _shift_down(x, kk - 1), axis=0, keepdims=True)]
    for q in range(1, kk):
        dx = dx + _shift_up(dy, kk - 1 - q) * w_ref[q:q + 1, :]
        dws.append(jnp.sum(dy * _shift_down(x, kk - 1 - q), axis=0, keepdims=True))
    return dx.astype(dx_dtype), dws, jnp.sum(dy, axis=0, keepdims=True)


def _dsilu(y, sg):
    return sg * (1.0 + y * (1.0 - sg))


CT = 256


def _ssd_conv_fwd(proj, w, b):
    s = proj.shape[0]

    def body(x_ref, w_ref, b_ref, o_ref):
        y = _conv(x_ref[...], w_ref, b_ref)
        o_ref[...] = y * _sigmoid(y)

    return pl.pallas_call(
        body, name="ssd_conv_fwd", grid=(XBC_DIM // CT,),
        in_specs=[pl.BlockSpec((s, CT), lambda i: (0, O_X // CT + i)), pl.BlockSpec((4, CT), lambda i: (0, i)),
                  pl.BlockSpec((1, CT), lambda i: (0, i))],
        out_specs=pl.BlockSpec((s, CT), lambda i: (0, i)),
        out_shape=jax.ShapeDtypeStruct((s, XBC_DIM), F32), compiler_params=_cp(("parallel",)),
    )(proj, w, b)


def _ssd_conv_bwd(dact, proj, w, b):
    s = proj.shape[0]

    def body(d_ref, x_ref, w_ref, b_ref, dx_ref, dw_ref, db_ref):
        x = x_ref[...]
        y = _conv(x, w_ref, b_ref)
        dy = d_ref[...] * _dsilu(y, _sigmoid(y))
        dx, dws, db = _conv_bwd(dy, x, w_ref, BF16)
        dx_ref[...] = dx
        for q in range(4):
            dw_ref[q:q + 1, :] = dws[q]
        db_ref[...] = db

    return pl.pallas_call(
        body, name="ssd_conv_bwd", grid=(XBC_DIM // CT,),
        in_specs=[pl.BlockSpec((s, CT), lambda i: (0, i)), pl.BlockSpec((s, CT), lambda i: (0, O_X // CT + i)),
                  pl.BlockSpec((4, CT), lambda i: (0, i)), pl.BlockSpec((1, CT), lambda i: (0, i))],
        out_specs=[pl.BlockSpec((s, CT), lambda i: (0, i)), pl.BlockSpec((4, CT), lambda i: (0, i)),
                   pl.BlockSpec((1, CT), lambda i: (0, i))],
        out_shape=[jax.ShapeDtypeStruct((s, XBC_DIM), BF16), jax.ShapeDtypeStruct((4, XBC_DIM), F32),
                   jax.ShapeDtypeStruct((1, XBC_DIM), F32)],
        compiler_params=_cp(("parallel",)),
    )(dact, proj, w, b)


NFT = D_FF // CT


def _ffn_act_fwd(up, w, b):
    s = up.shape[0]

    def body(v_ref, g_ref, wv_ref, wg_ref, bv_ref, bg_ref, o_ref):
        val = _conv(v_ref[...], wv_ref, bv_ref)
        gt = _conv(g_ref[...], wg_ref, bg_ref)
        o_ref[...] = ((gt * _sigmoid(gt)) * val).astype(BF16)

    col = lambda off: (lambda i: (0, off + i))
    return pl.pallas_call(
        body, name="ffn_act_fwd", grid=(NFT,),
        in_specs=[pl.BlockSpec((s, CT), col(0)), pl.BlockSpec((s, CT), col(NFT)),
                  pl.BlockSpec((3, CT), col(0)), pl.BlockSpec((3, CT), col(NFT)),
                  pl.BlockSpec((1, CT), col(0)), pl.BlockSpec((1, CT), col(NFT))],
        out_specs=pl.BlockSpec((s, CT), col(0)),
        out_shape=jax.ShapeDtypeStruct((s, D_FF), BF16), compiler_params=_cp(("parallel",)),
    )(up, up, w, w, b, b)


def _ffn_act_bwd(dact, up, w, b):
    s = up.shape[0]

    def body(d_ref, v_ref, g_ref, wv_ref, wg_ref, bv_ref, bg_ref, dx_ref, dw_ref, db_ref):
        xv, xg = v_ref[...], g_ref[...]
        val = _conv(xv, wv_ref, bv_ref)
        gt = _conv(xg, wg_ref, bg_ref)
        sg = _sigmoid(gt)
        d = d_ref[...]
        for half, (dy, x, w_ref) in enumerate(((d * (gt * sg), xv, wv_ref), (d * val * _dsilu(gt, sg), xg, wg_ref))):
            dx, dws, db = _conv_bwd(dy, x, w_ref, BF16)
            dx_ref[half] = dx
            for q in range(3):
                dw_ref[half, q:q + 1, :] = dws[q]
            db_ref[half] = db

    col = lambda off: (lambda i: (0, off + i))
    both = lambda i: (0, 0, i)
    return pl.pallas_call(
        body, name="ffn_act_bwd", grid=(NFT,),
        in_specs=[pl.BlockSpec((s, CT), col(0)), pl.BlockSpec((s, CT), col(0)), pl.BlockSpec((s, CT), col(NFT)),
                  pl.BlockSpec((3, CT), col(0)), pl.BlockSpec((3, CT), col(NFT)),
                  pl.BlockSpec((1, CT), col(0)), pl.BlockSpec((1, CT), col(NFT))],
        out_specs=[pl.BlockSpec((2, s, CT), both), pl.BlockSpec((2, 3, CT), both), pl.BlockSpec((2, 1, CT), both)],
        out_shape=[jax.ShapeDtypeStruct((2, s, D_FF), BF16), jax.ShapeDtypeStruct((2, 3, D_FF), F32),
                   jax.ShapeDtypeStruct((2, 1, D_FF), F32)],
        compiler_params=_cp(("parallel",)),
    )(dact, up, up, w, w, b, b)


def _expand_mat():
    r = lax.broadcasted_iota(jnp.int32, (128, D_INNER), 0)
    c = lax.broadcasted_iota(jnp.int32, (128, D_INNER), 1)
    return ((c >> 6) == r).astype(BF16)


def _reduce_mat():
    r = lax.broadcasted_iota(jnp.int32, (D_INNER, 128), 0)
    c = lax.broadcasted_iota(jnp.int32, (D_INNER, 128), 1)
    return ((r >> 6) == c).astype(BF16)


def _split(v, parts):
    out = []
    for _ in range(parts - 1):
        p = v.astype(BF16)
        out.append(p)
        v = v - p.astype(F32)
    out.append(v.astype(BF16))
    return out


def _sel_dot(v, sel, parts):
    acc = None
    for p in reversed(_split(v, parts)):
        t = _dot(p, sel)
        acc = t if acc is None else acc + t
    return acc


def _row8(v):
    return jnp.broadcast_to(v, (8, v.shape[1]))


def _tril():
    r = lax.broadcasted_iota(jnp.int32, (BLK, BLK), 0)
    c = lax.broadcasted_iota(jnp.int32, (BLK, BLK), 1)
    return r >= c


def _softplus(x):
    return jnp.maximum(x, 0.0) + jnp.log(1.0 + jnp.exp(-jnp.abs(x)))


def _ssd_common(dtraw_ref, dtb_ref, alog_ref):
    causal = _tril()
    e_mat = _expand_mat()
    a_neg = -jnp.exp(alog_ref[...])
    dt = _softplus(dtraw_ref[...] + dtb_ref[...])
    a_cs = _dot(causal.astype(F32), dt * a_neg, HI)
    a_cs_t = a_cs.T
    dt_x = _sel_dot(dt, e_mat, 3)
    acs_x = _sel_dot(a_cs, e_mat, 3)
    alast_x = acs_x[BLK - 1:BLK, :]
    ea_x = jnp.exp(acs_x)
    ds_x = jnp.exp(alast_x - acs_x)
    elast_x = jnp.exp(alast_x)
    return causal, e_mat, a_neg, dt, a_cs, a_cs_t, dt_x, ea_x, ds_x, elast_x


def _decay(a_cs, a_cs_t, h, causal):
    seg = a_cs[:, h:h + 1] - a_cs_t[h:h + 1, :]
    return jnp.where(causal, jnp.exp(jnp.where(causal, seg, 0.0)), 0.0)


def _ssd_fwd(xbc, proj, dt_bias, a_log, d_skip):
    s = xbc.shape[0]
    nc = s // BLK

    def body(xs_ref, b_ref, c_ref, dtraw_ref, dtb_ref, alog_ref, dskip_ref, y_ref, hp_ref, h_scr, xc16):
        @pl.when(pl.program_id(0) == 0)
        def _():
            h_scr[...] = jnp.zeros_like(h_scr)

        causal, e_mat, _, _, a_cs, a_cs_t, dt_x, ea_x, ds_x, elast_x = _ssd_common(dtraw_ref, dtb_ref, alog_ref)
        dskip_x = _sel_dot(_row8(dskip_ref[...]), e_mat, 3)[0:1]
        xs = xs_ref[...]
        xc = xs * dt_x
        xc16[...] = xc.astype(BF16)
        xcd = (xc * ds_x).astype(BF16)
        hp_ref[0] = h_scr[...]
        for g in range(4):
            gs = slice(g * 512, (g + 1) * 512)
            cg = c_ref[:, g * 128:(g + 1) * 128].astype(BF16)
            bg = b_ref[:, g * 128:(g + 1) * 128].astype(BF16)
            cb = _dot_nt(cg, bg)
            hg = h_scr[:, gs]
            yoff = _dot(cg, hg.astype(BF16)) * ea_x[:, gs]
            for j in range(8):
                h = g * 8 + j
                hsl = slice(h * 64, (h + 1) * 64)
                mm = (cb * _decay(a_cs, a_cs_t, h, causal)).astype(BF16)
                y_ref[:, hsl] = _dot(mm, xc16[:, hsl])
            y_ref[:, gs] += yoff + xs[:, gs] * dskip_x[:, gs]
            h_scr[:, gs] = hg * elast_x[:, gs] + _dot_tn(bg, xcd[:, gs])

    vec = pl.BlockSpec((1, 128), lambda c: (0, 0))
    return pl.pallas_call(
        body, name="ssd_fwd", grid=(nc,),
        in_specs=[pl.BlockSpec((BLK, D_INNER), lambda c: (c, 0)),
                  pl.BlockSpec((BLK, BC_DIM), lambda c: (c, D_INNER // BC_DIM)),
                  pl.BlockSpec((BLK, BC_DIM), lambda c: (c, D_INNER // BC_DIM + 1)),
                  pl.BlockSpec((BLK, 128), lambda c: (c, O_DT // 128)), vec, vec, vec],
        out_specs=[pl.BlockSpec((BLK, D_INNER), lambda c: (c, 0)),
                   pl.BlockSpec((1, 128, D_INNER), lambda c: (c, 0, 0))],
        out_shape=[jax.ShapeDtypeStruct((s, D_INNER), F32), jax.ShapeDtypeStruct((nc, 128, D_INNER), F32)],
        scratch_shapes=[pltpu.VMEM((128, D_INNER), F32), pltpu.VMEM((BLK, D_INNER), BF16)],
        compiler_params=_cp(("arbitrary",)),
    )(xbc, xbc, xbc, proj, dt_bias, a_log, d_skip)


def _ssd_bwd(xbc, proj, dt_bias, a_log, d_skip, hprev, dy, side=None):
    s = xbc.shape[0]
    nc = s // BLK

    def body(xs_ref, b_ref, c_ref, dtraw_ref, dtb_ref, alog_ref, dskip_ref, hp_ref, dy_ref,
             dxbc_ref, ddt_ref, dvec_ref, dh_scr, xc16, dy16, dxc_scr, dacs_r, tdiff):
        step = pl.program_id(0)
        dacs_r[...] = jnp.zeros_like(dacs_r)

        @pl.when(step == 0)
        def _():
            dh_scr[...] = jnp.zeros_like(dh_scr)
            dvec_ref[...] = jnp.zeros_like(dvec_ref)

        causal, e_mat, a_neg, dt, a_cs, a_cs_t, dt_x, ea_x, ds_x, elast_x = _ssd_common(dtraw_ref, dtb_ref, alog_ref)
        r_mat = _reduce_mat()
        lane = lax.broadcasted_iota(jnp.int32, (1, 128), 1)
        dskip_x = _sel_dot(_row8(dskip_ref[...]), e_mat, 3)[0:1]
        xs = xs_ref[...]
        dy = dy_ref[...]
        xc = xs * dt_x
        xcd = xc * ds_x
        xc16[...] = xc.astype(BF16)
        dy16[...] = dy.astype(BF16)
        dyea = dy * ea_x
        dh = dh_scr[...]
        hp = hp_ref[0]
        dalast_x = jnp.sum(dh * hp, axis=0, keepdims=True) * elast_x
        dacs = jnp.zeros((BLK, 128), F32)
        for g in range(4):
            gs = slice(g * 512, (g + 1) * 512)
            bsl = slice(g * 128, (g + 1) * 128)
            cg = c_ref[:, bsl].astype(BF16)
            bg = b_ref[:, bsl].astype(BF16)
            cb = _dot_nt(cg, bg)
            hg16 = hp[:, gs].astype(BF16)
            dhg16 = dh[:, gs].astype(BF16)
            raw = _dot(cg, hg16)
            draw16 = dyea[:, gs].astype(BF16)
            dcg = _dot_nt(draw16, hg16)
            dhp_g = _dot_tn(cg, draw16)
            dbg = _dot_nt(xcd[:, gs].astype(BF16), dhg16)
            dxcd = _dot(bg, dhg16)
            dcb = jnp.zeros((BLK, BLK), F32)
            for j in range(8):
                h = g * 8 + j
                hsl = slice(h * 64, (h + 1) * 64)
                decay = _decay(a_cs, a_cs_t, h, causal)
                m = cb * decay
                dm = _dot_nt(dy16[:, hsl], xc16[:, hsl])
                dxc_scr[:, hsl] = _dot_tn(m.astype(BF16), dy16[:, hsl])
                dcb = dcb + dm * decay
                dseg = dm * m
                oneh = jnp.where(lane == h, 1.0, 0.0)
                dacs = dacs + jnp.sum(dseg, axis=1, keepdims=True) * oneh
                dacs_r[h:h + 1, :] = jnp.sum(dseg, axis=0, keepdims=True)
            dcb16 = dcb.astype(BF16)
            dcg = dcg + _dot(dcb16, bg)
            dbg = dbg + _dot_tn(dcb16, cg)
            dxbc_ref[:, D_INNER + g * 128:D_INNER + (g + 1) * 128] = dbg
            dxbc_ref[:, D_INNER + BC_DIM + g * 128:D_INNER + BC_DIM + (g + 1) * 128] = dcg
            dxc_scr[:, gs] += dxcd * ds_x[:, gs]
            dh_scr[:, gs] = dh[:, gs] * elast_x[:, gs] + dhp_g
            tst = dxcd * xcd[:, gs]
            tdiff[:, gs] = dy[:, gs] * (raw * ea_x[:, gs]) - tst
            tdiff[BLK - 1:BLK, gs] += jnp.sum(tst, axis=0, keepdims=True)
        dxc = dxc_scr[...]
        row = lax.broadcasted_iota(jnp.int32, (BLK, D_INNER), 0)
        tfull = tdiff[...] + jnp.where(row == BLK - 1, dalast_x, 0.0)
        dacs = dacs + _sel_dot(tfull, r_mat, 2) - dacs_r[...].T
        da = _dot_tn(causal.astype(F32), dacs, HI)
        ddt = da * a_neg + _sel_dot(dxc * xs, r_mat, 2)
        lmask = lax.broadcasted_iota(jnp.int32, (BLK, 128), 1) < N_SSD_HEADS
        ddtraw = jnp.where(lmask, ddt * _sigmoid(dtraw_ref[...] + dtb_ref[...]), 0.0)
        ddt_ref[...] = ddtraw.astype(BF16)
        dxbc_ref[:, 0:D_INNER] = dy * dskip_x + dxc * dt_x
        dvec_ref[0:1, :] += jnp.sum(ddtraw, axis=0, keepdims=True)
        dvec_ref[1:2, :] += jnp.where(lane < N_SSD_HEADS, jnp.sum(da * dt, axis=0, keepdims=True) * a_neg, 0.0)
        dvec_ref[2:3, :] += _sel_dot(_row8(jnp.sum(dy * xs, axis=0, keepdims=True)), r_mat, 3)[0:1]

    rev = lambda c: nc - 1 - c
    vec = pl.BlockSpec((1, 128), lambda c: (0, 0))
    own, extra = _hosted(
        body, name="ssd_bwd", grid=(nc,),
        in_specs=[pl.BlockSpec((BLK, D_INNER), lambda c: (rev(c), 0)),
                  pl.BlockSpec((BLK, BC_DIM), lambda c: (rev(c), D_INNER // BC_DIM)),
                  pl.BlockSpec((BLK, BC_DIM), lambda c: (rev(c), D_INNER // BC_DIM + 1)),
                  pl.BlockSpec((BLK, 128), lambda c: (rev(c), O_DT // 128)), vec, vec, vec,
                  pl.BlockSpec((1, 128, D_INNER), lambda c: (rev(c), 0, 0)),
                  pl.BlockSpec((BLK, D_INNER), lambda c: (rev(c), 0))],
        out_specs=[pl.BlockSpec((BLK, XBC_DIM), lambda c: (rev(c), 0)),
                   pl.BlockSpec((BLK, 128), lambda c: (rev(c), 0)),
                   pl.BlockSpec((8, 128), lambda c: (0, 0))],
        out_shape=[jax.ShapeDtypeStruct((s, XBC_DIM), F32), jax.ShapeDtypeStruct((s, 128), BF16),
                   jax.ShapeDtypeStruct((8, 128), F32)],
        scratch_shapes=[pltpu.VMEM((128, D_INNER), F32), pltpu.VMEM((BLK, D_INNER), BF16),
                        pltpu.VMEM((BLK, D_INNER), BF16), pltpu.VMEM((BLK, D_INNER), F32),
                        pltpu.VMEM((128, BLK), F32), pltpu.VMEM((BLK, D_INNER), F32)],
        args=(xbc, xbc, xbc, proj, dt_bias, a_log, d_skip, hprev, dy), sem=("arbitrary",), side=side)
    return own if side is None else (own, extra)


GW = 512


def _gate_norm_fwd(y, proj, wn, *, tm=512):
    s = y.shape[0]
    tm = _tile(s, tm)

    def body(y_ref, z_ref, w_ref, o_ref):
        z = z_ref[...]
        y2 = y_ref[...] * (z * _sigmoid(z))
        r = lax.rsqrt(jnp.mean(y2 * y2, axis=-1, keepdims=True) + EPS)
        o_ref[...] = ((y2 * r) * w_ref[...]).astype(BF16)

    return pl.pallas_call(
        body, name="gate_norm_fwd", grid=(s // tm, 4),
        in_specs=[pl.BlockSpec((tm, GW), lambda i, g: (i, g)), pl.BlockSpec((tm, GW), lambda i, g: (i, O_Z // GW + g)),
                  pl.BlockSpec((1, GW), lambda i, g: (0, g))],
        out_specs=pl.BlockSpec((tm, GW), lambda i, g: (i, g)),
        out_shape=jax.ShapeDtypeStruct((s, D_INNER), BF16), compiler_params=_cp(("parallel", "parallel")),
    )(y, proj, wn)


def _gate_norm_bwd(dyn, y, proj, wn, *, tm=512):
    s = y.shape[0]
    tm = _tile(s, tm)

    def body(d_ref, y_ref, z_ref, w_ref, dy_ref, dz_ref, dw_ref):
        i = pl.program_id(1)
        z = z_ref[...]
        sg = _sigmoid(z)
        sz = z * sg
        yv = y_ref[...]
        y2 = yv * sz
        r = lax.rsqrt(jnp.mean(y2 * y2, axis=-1, keepdims=True) + EPS)
        xh = y2 * r
        dv = d_ref[...]
        g = dv * w_ref[...]
        dy2 = r * (g - xh * jnp.mean(g * xh, axis=-1, keepdims=True))
        dy_ref[...] = dy2 * sz
        dz_ref[...] = (dy2 * yv * _dsilu(z, sg)).astype(BF16)
        part = jnp.sum(dv * xh, axis=0, keepdims=True)

        @pl.when(i == 0)
        def _():
            dw_ref[...] = part

        @pl.when(i > 0)
        def _():
            dw_ref[...] += part

    blk = pl.BlockSpec((tm, GW), lambda g, i: (i, g))
    vec = pl.BlockSpec((1, GW), lambda g, i: (0, g))
    return pl.pallas_call(
        body, name="gate_norm_bwd", grid=(4, s // tm),
        in_specs=[blk, blk, pl.BlockSpec((tm, GW), lambda g, i: (i, O_Z // GW + g)), vec],
        out_specs=[blk, blk, vec],
        out_shape=[jax.ShapeDtypeStruct((s, D_INNER), F32), jax.ShapeDtypeStruct((s, D_INNER), BF16),
                   jax.ShapeDtypeStruct((1, D_INNER), F32)],
        compiler_params=_cp(("parallel", "arbitrary")),
    )(dyn, y, proj, wn)


def _merge_fwd(proj, b_gate, attn, ssd_out, *, tm=512):
    s = attn.shape[0]
    tm = _tile(s, tm)

    def body(ga_ref, gs_ref, ba_ref, bs_ref, a_ref, s_ref, o_ref):
        ga = _sigmoid(ga_ref[...] + ba_ref[...])
        gs = _sigmoid(gs_ref[...] + bs_ref[...])
        o_ref[...] = (ga * a_ref[...] + gs * s_ref[...]).astype(BF16)

    blk = pl.BlockSpec((tm, GW), lambda i, j: (i, j))
    return pl.pallas_call(
        body, name="merge_fwd", grid=(s // tm, 2),
        in_specs=[pl.BlockSpec((tm, GW), lambda i, j: (i, O_GA // GW + j)),
                  pl.BlockSpec((tm, GW), lambda i, j: (i, O_GS // GW + j)),
                  pl.BlockSpec((1, GW), lambda i, j: (0, j)), pl.BlockSpec((1, GW), lambda i, j: (0, 2 + j)), blk, blk],
        out_specs=blk, out_shape=jax.ShapeDtypeStruct((s, D_MODEL), BF16),
        compiler_params=_cp(("parallel", "parallel")),
    )(proj, proj, b_gate, b_gate, attn, ssd_out)


def _merge_bwd(dm, proj, b_gate, attn, ssd_out, *, tm=512):
    s = attn.shape[0]
    tm = _tile(s, tm)

    def body(d_ref, ga_ref, gs_ref, ba_ref, bs_ref, a_ref, s_ref, da_ref, ds_ref, dga_ref, dgs_ref, dba_ref, dbs_ref):
        i = pl.program_id(1)
        ga = _sigmoid(ga_ref[...] + ba_ref[...])
        gs = _sigmoid(gs_ref[...] + bs_ref[...])
        d = d_ref[...]
        da_ref[...] = (d * ga).astype(BF16)
        ds_ref[...] = (d * gs).astype(BF16)
        dga = d * a_ref[...] * (ga * (1.0 - ga))
        dgs = d * s_ref[...] * (gs * (1.0 - gs))
        dga_ref[...] = dga.astype(BF16)
        dgs_ref[...] = dgs.astype(BF16)
        pa = jnp.sum(dga, axis=0, keepdims=True)
        ps = jnp.sum(dgs, axis=0, keepdims=True)

        @pl.when(i == 0)
        def _():
            dba_ref[...] = pa
            dbs_ref[...] = ps

        @pl.when(i > 0)
        def _():
            dba_ref[...] += pa
            dbs_ref[...] += ps

    blk = pl.BlockSpec((tm, GW), lambda j, i: (i, j))
    vec = pl.BlockSpec((1, GW), lambda j, i: (0, j))
    sd = jax.ShapeDtypeStruct((s, D_MODEL), BF16)
    vd = jax.ShapeDtypeStruct((1, D_MODEL), F32)
    return pl.pallas_call(
        body, name="merge_bwd", grid=(2, s // tm),
        in_specs=[blk, pl.BlockSpec((tm, GW), lambda j, i: (i, O_GA // GW + j)),
                  pl.BlockSpec((tm, GW), lambda j, i: (i, O_GS // GW + j)),
                  vec, pl.BlockSpec((1, GW), lambda j, i: (0, 2 + j)), blk, blk],
        out_specs=[blk, blk, blk, blk, vec, vec], out_shape=[sd, sd, sd, sd, vd, vd],
        compiler_params=_cp(("parallel", "arbitrary")),
    )(dm, proj, proj, b_gate, b_gate, attn, ssd_out)


def _adamw_math(w, g, m, v):
    mn = ADAM_B1 * m + (1.0 - ADAM_B1) * g
    vn = ADAM_B2 * v + (1.0 - ADAM_B2) * (g * g)
    m_hat = mn / (1.0 - ADAM_B1 ** ADAM_STEP)
    v_hat = vn / (1.0 - ADAM_B2 ** ADAM_STEP)
    return -ADAM_LR * (m_hat / (jnp.sqrt(v_hat) + ADAM_EPS) + ADAM_WD * w), mn, vn


def _adamw_many(ws, gs, ms, vs):
    n = len(ws)

    def body(*refs):
        outs = refs[4 * n:]
        for i in range(n):
            res = _adamw_math(*[refs[q * n + i][...] for q in range(4)])
            for q in range(3):
                outs[q * n + i][...] = res[q]

    return pl.pallas_call(body, name="adamw_small", out_shape=[jax.ShapeDtypeStruct(w.shape, F32) for w in ws] * 3,
                          compiler_params=_cp())(*ws, *gs, *ms, *vs)


def _adamw(w, g, m, v, *, name, tm=128):
    r, c = w.shape
    tm = r if (r < tm or r % tm) else tm

    def body(w_ref, g_ref, m_ref, v_ref, d_ref, nm_ref, nv_ref):
        d_ref[...], nm_ref[...], nv_ref[...] = _adamw_math(w_ref[...], g_ref[...], m_ref[...], v_ref[...])

    blk = pl.BlockSpec((tm, c), lambda i: (i, 0))
    sd = jax.ShapeDtypeStruct((r, c), F32)
    return pl.pallas_call(
        body, name=name, grid=(r // tm,), in_specs=[blk] * 4, out_specs=[blk] * 3, out_shape=[sd] * 3,
        compiler_params=_cp(("parallel",)),
    )(w, g, m, v)


ANY = pl.BlockSpec(memory_space=pl.ANY)
N_CHIPS = 4


def _chip_of(k, x, y):
    return (x ^ (k >> 1), y ^ (k & 1))


def _all_gather_small(shard):
    r, c = shard.shape
    hr = r // 2

    def body(sh_ref, out_ref, send_sems, recv_sems, local_sem):
        x, y, cc = lax.axis_index("x"), lax.axis_index("y"), lax.axis_index("c")

        def half(px, py, pc):
            return out_ref.at[2 * px + py, pl.ds(pc * hr, hr), :]

        def copy(k, px, py, pc, to, src=None):
            return pltpu.make_async_remote_copy(
                src_ref=half(px, py, pc) if src is None else src, dst_ref=half(px, py, pc),
                send_sem=send_sems.at[k], recv_sem=recv_sems.at[k], device_id=to, device_id_type=MESH)

        mine = pltpu.make_async_copy(sh_ref, out_ref.at[2 * x + y], local_sem)
        mine.start()
        chips = [_chip_of(k, x, y) for k in (1, 2, 3)]
        first = [copy(j, x, y, cc, (*chip, cc), src=sh_ref.at[pl.ds(cc * hr, hr), :]) for j, chip in enumerate(chips)]
        for cp in first:
            cp.start()
        passed = [copy(3 + j, *chip, cc, (x, y, 1 - cc)) for j, chip in enumerate(chips)]
        for j, chip in enumerate(chips):
            copy(j, *chip, cc, (x, y, cc)).wait_recv()
            passed[j].start()
        for j, chip in enumerate(chips):
            copy(3 + j, *chip, 1 - cc, (x, y, cc)).wait_recv()
        for cp in first + passed:
            cp.wait_send()
        mine.wait()

    return pl.pallas_call(
        body, name="all_gather_small", in_specs=[ANY], out_specs=ANY,
        out_shape=jax.ShapeDtypeStruct((N_CHIPS, r, c), shard.dtype),
        scratch_shapes=[pltpu.SemaphoreType.DMA((6,)), pltpu.SemaphoreType.DMA((6,)), pltpu.SemaphoreType.DMA],
    )(shard)


def _cast_bf16(a, *, name, tm=512):
    n, r, c = a.shape
    tm = _tile(r, tm) if r % 128 == 0 else r

    def body(a_ref, o_ref):
        o_ref[...] = a_ref[...].astype(BF16)

    blk = pl.BlockSpec((1, tm, c), lambda i, j: (i, j, 0))
    return pl.pallas_call(body, name=name, grid=(n, r // tm), in_specs=[blk], out_specs=blk,
                          out_shape=jax.ShapeDtypeStruct(a.shape, BF16), compiler_params=_cp(("parallel", "parallel")))(a)


def _pair_exchange(g16, hr):
    n, r, c = g16.shape

    def body(g_ref, out_ref, send_sem, recv_sem):
        x, y, cc = lax.axis_index("x"), lax.axis_index("y"), lax.axis_index("c")
        cp = pltpu.make_async_remote_copy(
            src_ref=g_ref.at[:, pl.ds((1 - cc) * hr, hr), :], dst_ref=out_ref, send_sem=send_sem, recv_sem=recv_sem,
            device_id=(x, y, 1 - cc), device_id_type=MESH)
        cp.start()
        cp.wait()

    return pl.pallas_call(
        body, name="grad_pair_exchange", in_specs=[ANY], out_specs=ANY,
        out_shape=jax.ShapeDtypeStruct((n, hr, c), g16.dtype),
        scratch_shapes=[pltpu.SemaphoreType.DMA, pltpu.SemaphoreType.DMA],
    )(g16)


def _pair_add(g, recv, half_idx, hr, *, tm=384):
    n, r, c = g.shape
    nt = hr // tm

    def body(hi_ref, g_ref, r_ref, o32_ref, o16_ref):
        v = g_ref[...] + r_ref[...].astype(F32)
        o32_ref[...] = v
        o16_ref[...] = v.astype(BF16)

    gs = pltpu.PrefetchScalarGridSpec(
        num_scalar_prefetch=1, grid=(n, nt),
        in_specs=[pl.BlockSpec((1, tm, c), lambda i, j, hi: (i, hi[0] * nt + j, 0)),
                  pl.BlockSpec((1, tm, c), lambda i, j, hi: (i, j, 0))],
        out_specs=[pl.BlockSpec((1, tm, c), lambda i, j, hi: (i, j, 0))] * 2)
    return pl.pallas_call(
        body, name="grad_pair_add", grid_spec=gs,
        out_shape=[jax.ShapeDtypeStruct((n, hr, c), F32), jax.ShapeDtypeStruct((n, hr, c), BF16)],
        compiler_params=_cp(("parallel", "parallel")),
    )(half_idx, g, recv)


def _chip_exchange(p16):
    n, hr, c = p16.shape

    def body(p_ref, out_ref, send_sems, recv_sems):
        x, y, cc = lax.axis_index("x"), lax.axis_index("y"), lax.axis_index("c")
        cps = []
        for j, k in enumerate((1, 2, 3)):
            px, py = _chip_of(k, x, y)
            cps.append(pltpu.make_async_remote_copy(
                src_ref=p_ref.at[2 * px + py], dst_ref=out_ref.at[j], send_sem=send_sems.at[j], recv_sem=recv_sems.at[j],
                device_id=(px, py, cc), device_id_type=MESH))
        for cp in cps:
            cp.start()
        for cp in cps:
            cp.wait()

    return pl.pallas_call(
        body, name="grad_chip_exchange", in_specs=[ANY], out_specs=ANY,
        out_shape=jax.ShapeDtypeStruct((3, hr, c), p16.dtype),
        scratch_shapes=[pltpu.SemaphoreType.DMA((3,)), pltpu.SemaphoreType.DMA((3,))],
    )(p16)


def _chip_add(p32, recv, chip_idx, *, tm=384):
    n, hr, c = p32.shape

    def body(ci_ref, p_ref, r_ref, o_ref):
        o_ref[...] = ((p_ref[0] + r_ref[0].astype(F32)) + r_ref[1].astype(F32)) + r_ref[2].astype(F32)

    gs = pltpu.PrefetchScalarGridSpec(
        num_scalar_prefetch=1, grid=(hr // tm,),
        in_specs=[pl.BlockSpec((1, tm, c), lambda j, ci: (ci[0], j, 0)), pl.BlockSpec((3, tm, c), lambda j, ci: (0, j, 0))],
        out_specs=pl.BlockSpec((tm, c), lambda j, ci: (j, 0)))
    return pl.pallas_call(
        body, name="grad_chip_add", grid_spec=gs, out_shape=jax.ShapeDtypeStruct((hr, c), F32),
        compiler_params=_cp(("parallel",)),
    )(chip_idx, p32, recv)


def _pair_gather(f):
    hr, c = f.shape

    def body(f_ref, out_ref, send_sem, recv_sem, local_sem):
        x, y, cc = lax.axis_index("x"), lax.axis_index("y"), lax.axis_index("c")
        mine = pltpu.make_async_copy(f_ref, out_ref.at[pl.ds(cc * hr, hr), :], local_sem)
        mine.start()
        cp = pltpu.make_async_remote_copy(
            src_ref=f_ref, dst_ref=out_ref.at[pl.ds(cc * hr, hr), :], send_sem=send_sem, recv_sem=recv_sem,
            device_id=(x, y, 1 - cc), device_id_type=MESH)
        cp.start()
        cp.wait()
        mine.wait()

    return pl.pallas_call(
        body, name="grad_pair_gather", in_specs=[ANY], out_specs=ANY,
        out_shape=jax.ShapeDtypeStruct((2 * hr, c), f.dtype),
        scratch_shapes=[pltpu.SemaphoreType.DMA, pltpu.SemaphoreType.DMA, pltpu.SemaphoreType.DMA],
    )(f)


def _all_reduce_small(buf):
    r, c = buf.shape

    def body(b_ref, out_ref, gat, send_sems, recv_sems):
        x, y, cc = lax.axis_index("x"), lax.axis_index("y"), lax.axis_index("c")
        me = 4 * x + 2 * y + cc
        gat[me] = b_ref[...]
        cps = []
        for k in range(1, 8):
            px, py, pc = x ^ (k >> 2), y ^ ((k >> 1) & 1), cc ^ (k & 1)
            cps.append(pltpu.make_async_remote_copy(
                src_ref=b_ref, dst_ref=gat.at[me], send_sem=send_sems.at[k - 1], recv_sem=recv_sems.at[k - 1],
                device_id=(px, py, pc), device_id_type=MESH))
        for cp in cps:
            cp.start()
        for cp in cps:
            cp.wait()
        acc = gat[0]
        for d in range(1, 8):
            acc = acc + gat[d]
        out_ref[...] = acc

    vm = pl.BlockSpec(memory_space=pltpu.VMEM)
    return pl.pallas_call(
        body, name="all_reduce_small", in_specs=[vm], out_specs=vm, out_shape=jax.ShapeDtypeStruct((r, c), F32),
        scratch_shapes=[pltpu.VMEM((8, r, c), F32), pltpu.SemaphoreType.DMA((7,)), pltpu.SemaphoreType.DMA((7,))],
        compiler_params=pltpu.CompilerParams(vmem_limit_bytes=VMEM_LIMIT),
    )(buf)


def _pipe(fn, ins, outs, tr):
    shape = ins[0].shape
    lead, (r, c) = shape[:-2], shape[-2:]
    assert len(lead) <= 1 and r % tr == 0
    nr = r // tr
    n = nr * (lead[0] if lead else 1)
    ni, no = len(ins), len(outs)

    def blk(ref, step):
        rows = pl.ds(pl.multiple_of((step % nr) * tr, tr), tr)
        return ref.at[step // nr, rows, :] if lead else ref.at[rows, :]

    def scoped(*bufs):
        ibufs, obufs, isem, osem = bufs[:ni], bufs[ni:ni + no], bufs[-2], bufs[-1]

        def in_copy(q, step, slot):
            return pltpu.make_async_copy(blk(ins[q], step), ibufs[q].at[slot], isem.at[q, slot])

        def out_copy(q, step, slot):
            return pltpu.make_async_copy(obufs[q].at[slot], blk(outs[q], step), osem.at[q, slot])

        for q in range(ni):
            in_copy(q, 0, 0).start()

        def body(step, carry):
            slot = step % 2

            @pl.when(step + 1 < n)
            def _():
                for q in range(ni):
                    in_copy(q, step + 1, 1 - slot).start()

            for q in range(ni):
                in_copy(q, step, slot).wait()

            @pl.when(step >= 2)
            def _():
                for q in range(no):
                    out_copy(q, step - 2, slot).wait()

            res = fn(*[ibufs[q][slot] for q in range(ni)])
            for q in range(no):
                obufs[q][slot] = res[q].astype(obufs[q].dtype)
                out_copy(q, step, slot).start()
            return carry

        lax.fori_loop(0, n, body, 0)
        for step in range(max(n - 2, 0), n):
            for q in range(no):
                out_copy(q, step, step % 2).wait()

    pl.run_scoped(scoped, *[pltpu.VMEM((2, tr, c), q.dtype) for q in ins], *[pltpu.VMEM((2, tr, c), q.dtype) for q in outs],
                  pltpu.SemaphoreType.DMA((ni, 2)), pltpu.SemaphoreType.DMA((no, 2)))


W_IN_PAD = 2304
BIG = ("w_in", "w_attn_o", "w_ssd_o", "w_out", "w_up", "w_down")
BIG_SHAPE = dict(w_in=(D_MODEL, W_IN_PAD), w_attn_o=(Q_DIM // 4, D_MODEL), w_ssd_o=(D_INNER // 4, D_MODEL),
                 w_out=(D_MODEL // 4, D_MODEL), w_up=(D_MODEL, 2 * D_FF // 4), w_down=(D_FF // 4, D_MODEL))
BIG_TR = dict(w_in=128, w_attn_o=128, w_ssd_o=128, w_out=128, w_up=128, w_down=176)
X_FIRST = dict(w_in=True, w_attn_o=True, w_ssd_o=False, w_out=True, w_up=False, w_down=False)


def _neighbours(x, y, x_first):
    xn, yn = (1 - x, y), (x, 1 - y)
    n1, n2 = (xn, yn) if x_first else (yn, xn)
    slot = lambda ch: 2 * ch[0] + ch[1]
    return n1, n2, slot(n1), slot(n2), slot((1 - x, 1 - y))


def _gather_big(shards):
    nt = len(BIG)

    def body(*refs):
        sh, out = refs[:nt], refs[nt:2 * nt]
        send_sems, recv_sems = refs[2 * nt:]
        x, y, cc = lax.axis_index("x"), lax.axis_index("y"), lax.axis_index("c")
        me = 2 * x + y
        sib = (x, y, 1 - cc)
        for t, n in enumerate(BIG):
            _pipe(lambda v: (v,), [sh[t]], [out[t].at[me]], BIG_TR[n])

        def copy(t, k, slot, pc, to):
            hr = BIG_SHAPE[BIG[t]][0] // 2
            ref = out[t].at[slot, pl.ds(pc * hr, hr), :]
            return pltpu.make_async_remote_copy(src_ref=ref, dst_ref=ref, send_sem=send_sems.at[6 * t + k],
                                                recv_sem=recv_sems.at[6 * t + k], device_id=to, device_id_type=MESH)

        started = []

        def start(cp):
            cp.start()
            started.append(cp)

        geo = [_neighbours(x, y, X_FIRST[n]) for n in BIG]
        for t in range(nt):
            n1, n2, _, _, _ = geo[t]
            start(copy(t, 0, me, cc, (*n1, cc)))
            start(copy(t, 1, me, cc, (*n2, cc)))
        for t in range(nt):
            n1, n2, s1, s2, sd = geo[t]
            copy(t, 0, s1, cc, sib).wait_recv()
            start(copy(t, 2, s1, cc, (*n2, cc)))
            start(copy(t, 3, s1, cc, sib))
            copy(t, 1, s2, cc, sib).wait_recv()
            start(copy(t, 4, s2, cc, sib))
        for t in range(nt):
            _, _, s1, s2, sd = geo[t]
            copy(t, 2, sd, cc, sib).wait_recv()
            start(copy(t, 5, sd, cc, sib))
        for t in range(nt):
            _, _, s1, s2, sd = geo[t]
            copy(t, 3, s1, 1 - cc, sib).wait_recv()
            copy(t, 4, s2, 1 - cc, sib).wait_recv()
            copy(t, 5, sd, 1 - cc, sib).wait_recv()
        for cp in started:
            cp.wait_send()

    return pl.pallas_call(
        body, name="gather_big", in_specs=[ANY] * nt, out_specs=[ANY] * nt,
        out_shape=[jax.ShapeDtypeStruct((N_CHIPS, *BIG_SHAPE[n]), BF16) for n in BIG],
        scratch_shapes=[pltpu.SemaphoreType.DMA((6 * nt,)), pltpu.SemaphoreType.DMA((6 * nt,))],
        compiler_params=pltpu.CompilerParams(vmem_limit_bytes=VMEM_LIMIT),
    )(*shards)


def _reduce_big(grads):
    nt = len(BIG)
    nw = 7

    def body(*refs):
        g = refs[:nt]
        fin = refs[nt:2 * nt]
        work = refs[2 * nt:2 * nt + nw * nt]
        send_sems, recv_sems = refs[2 * nt + nw * nt:]
        x, y, cc = lax.axis_index("x"), lax.axis_index("y"), lax.axis_index("c")
        me = 2 * x + y
        sib = (x, y, 1 - cc)
        started = []

        def rcopy(t, k, src, dst, to):
            cp = pltpu.make_async_remote_copy(src_ref=src, dst_ref=dst, send_sem=send_sems.at[5 * t + k],
                                              recv_sem=recv_sems.at[5 * t + k], device_id=to, device_id_type=MESH)
            return cp

        def start(cp):
            cp.start()
            started.append(cp)

        geo = [_neighbours(x, y, X_FIRST[n]) for n in BIG]
        hrs = [BIG_SHAPE[n][0] // 2 for n in BIG]
        wk = lambda t: work[nw * t:nw * (t + 1)]
        one = lambda ref, slot: ref.at[pl.ds(slot, 1)]
        for t in range(nt):
            recv_a = wk(t)[0]
            start(rcopy(t, 0, g[t].at[:, pl.ds((1 - cc) * hrs[t], hrs[t]), :], recv_a, sib))
        for t, n in enumerate(BIG):
            recv_a, p32, p16, r1, qme, qs2, r2 = wk(t)
            n1, n2, s1, s2, sd = geo[t]
            rcopy(t, 0, recv_a, recv_a, sib).wait_recv()
            _pipe(lambda a, b: (a + b, a + b), [g[t].at[:, pl.ds(cc * hrs[t], hrs[t]), :], recv_a], [p32, p16], BIG_TR[n])
            start(rcopy(t, 1, one(p16, s1), one(r1, 0), (*n1, cc)))
            start(rcopy(t, 2, one(p16, sd), one(r1, 1), (*n1, cc)))
        for t, n in enumerate(BIG):
            recv_a, p32, p16, r1, qme, qs2, r2 = wk(t)
            n1, n2, s1, s2, sd = geo[t]
            rcopy(t, 1, one(r1, 0), one(r1, 0), sib).wait_recv()
            rcopy(t, 2, one(r1, 1), one(r1, 1), sib).wait_recv()
            _pipe(lambda a, b: (a + b.astype(F32),), [one(p32, s2), one(r1, 1)], [qs2], BIG_TR[n])
            start(rcopy(t, 3, qs2, r2, (*n2, cc)))
            _pipe(lambda a, b: (a + b.astype(F32),), [one(p32, me), one(r1, 0)], [qme], BIG_TR[n])
        for t, n in enumerate(BIG):
            recv_a, p32, p16, r1, qme, qs2, r2 = wk(t)
            rcopy(t, 3, r2, r2, sib).wait_recv()
            mine = fin[t].at[pl.ds(cc * hrs[t], hrs[t]), :]
            _pipe(lambda a, b: (a + b.astype(F32),), [qme.at[0], r2.at[0]], [mine], BIG_TR[n])
            start(rcopy(t, 4, mine, mine, sib))
        for t in range(nt):
            other = fin[t].at[pl.ds((1 - cc) * hrs[t], hrs[t]), :]
            rcopy(t, 4, other, other, sib).wait_recv()
        for cp in started:
            cp.wait_send()

    outs = [jax.ShapeDtypeStruct(BIG_SHAPE[n], F32) for n in BIG]
    for n in BIG:
        r, c = BIG_SHAPE[n]
        hr = r // 2
        outs += [jax.ShapeDtypeStruct((4, hr, c), F32), jax.ShapeDtypeStruct((4, hr, c), F32),
                 jax.ShapeDtypeStruct((4, hr, c), BF16), jax.ShapeDtypeStruct((2, hr, c), BF16),
                 jax.ShapeDtypeStruct((1, hr, c), F32), jax.ShapeDtypeStruct((1, hr, c), BF16),
                 jax.ShapeDtypeStruct((1, hr, c), BF16)]
    res = pl.pallas_call(
        body, name="reduce_big", in_specs=[ANY] * nt, out_specs=[ANY] * len(outs), out_shape=outs,
        scratch_shapes=[pltpu.SemaphoreType.DMA((5 * nt,)), pltpu.SemaphoreType.DMA((5 * nt,))],
        compiler_params=pltpu.CompilerParams(vmem_limit_bytes=VMEM_LIMIT),
    )(*grads)
    return res[:nt]


WHOLE_X_FIRST = dict(w_ssd_o=True, w_out=False, w_attn_o=False)


def _quarters(names):
    out = []
    for i, n in enumerate(names):
        if n in WHOLE_X_FIRST:
            h = BIG_SHAPE[n][0] // 2
            out.append((i, WHOLE_X_FIRST[n], 0, h, 128))
        else:
            q = BIG_SHAPE[n][0] // 4
            tr = 128 if q % 128 == 0 else q
            out += [(i, True, 0, q, tr), (i, False, q, q, tr)]
    return out


class _GatherJob:
    def __init__(self, names, shards, at=None):
        self.names = names
        self.at = at
        self.inputs = list(shards)
        self.out_shapes = [jax.ShapeDtypeStruct((N_CHIPS, *BIG_SHAPE[n]), BF16) for n in names]
        self.ent = _quarters(names)
        self.scratch = [pltpu.SemaphoreType.DMA((6 * len(self.ent),)), pltpu.SemaphoreType.DMA((6 * len(self.ent),))]

    def phases(self, sh, out, scr):
        send_sems, recv_sems = scr
        names, ent = self.names, self.ent
        x, y, cc = lax.axis_index("x"), lax.axis_index("y"), lax.axis_index("c")
        me = 2 * x + y
        sib = (x, y, 1 - cc)
        geo = [_neighbours(x, y, e[1]) for e in ent]
        started = []

        def copy(i, k, slot, pc, to):
            arr, _, roff, rows, _ = ent[i]
            hr = BIG_SHAPE[names[arr]][0] // 2
            ref = out[arr].at[slot, pl.ds(pc * hr + roff, rows), :]
            return pltpu.make_async_remote_copy(src_ref=ref, dst_ref=ref, send_sem=send_sems.at[6 * i + k],
                                                recv_sem=recv_sems.at[6 * i + k], device_id=to, device_id_type=MESH)

        def start(*a):
            copy(*a).start()
            started.append(a)

        def p0():
            for t, n in enumerate(names):
                _pipe(lambda v: (v,), [sh[t]], [out[t].at[me]], BIG_TR[n])
            for i in range(len(ent)):
                n1, n2, _, _, _ = geo[i]
                start(i, 0, me, cc, (*n1, cc))
                start(i, 1, me, cc, (*n2, cc))

        def p1():
            for i in range(len(ent)):
                n1, n2, s1, s2, sd = geo[i]
                copy(i, 0, s1, cc, sib).wait_recv()
                start(i, 2, s1, cc, (*n2, cc))
                start(i, 3, s1, cc, sib)
                copy(i, 1, s2, cc, sib).wait_recv()
                start(i, 4, s2, cc, sib)

        def p2():
            for i in range(len(ent)):
                sd = geo[i][4]
                copy(i, 2, sd, cc, sib).wait_recv()
                start(i, 5, sd, cc, sib)

        def p3():
            for i in range(len(ent)):
                _, _, s1, s2, sd = geo[i]
                copy(i, 3, s1, 1 - cc, sib).wait_recv()
                copy(i, 4, s2, 1 - cc, sib).wait_recv()
                copy(i, 5, sd, 1 - cc, sib).wait_recv()
            for a in started:
                copy(*a).wait_send()

        return [p0, p1, p2, p3]


class _ReduceJob:
    NW = 7

    def __init__(self, names, grads, at=None):
        self.names = names
        self.at = at
        self.inputs = list(grads)
        self.ent = _quarters(names)
        self.out_shapes = [jax.ShapeDtypeStruct(BIG_SHAPE[n], F32) for n in names]
        for arr, _, _, rows, _ in self.ent:
            c = BIG_SHAPE[names[arr]][1]
            self.out_shapes += [jax.ShapeDtypeStruct((4, rows, c), F32), jax.ShapeDtypeStruct((4, rows, c), F32),
                                jax.ShapeDtypeStruct((4, rows, c), BF16), jax.ShapeDtypeStruct((2, rows, c), BF16),
                                jax.ShapeDtypeStruct((1, rows, c), F32), jax.ShapeDtypeStruct((1, rows, c), BF16),
                                jax.ShapeDtypeStruct((1, rows, c), BF16)]
        self.scratch = [pltpu.SemaphoreType.DMA((5 * len(self.ent),)), pltpu.SemaphoreType.DMA((5 * len(self.ent),))]

    def phases(self, g, outs, scr):
        send_sems, recv_sems = scr
        names, ent, nw = self.names, self.ent, self.NW
        nt = len(names)
        fin, work = outs[:nt], outs[nt:]
        x, y, cc = lax.axis_index("x"), lax.axis_index("y"), lax.axis_index("c")
        me = 2 * x + y
        sib = (x, y, 1 - cc)
        geo = [_neighbours(x, y, e[1]) for e in ent]
        started = []
        wk = lambda i: work[nw * i:nw * (i + 1)]
        one = lambda ref, slot: ref.at[pl.ds(slot, 1)]

        def rows_of(i, pc):
            arr, _, roff, rows, _ = ent[i]
            return pl.ds(pc * (BIG_SHAPE[names[arr]][0] // 2) + roff, rows)

        def rcopy(i, k, src, dst, to):
            return pltpu.make_async_remote_copy(src_ref=src, dst_ref=dst, send_sem=send_sems.at[5 * i + k],
                                                recv_sem=recv_sems.at[5 * i + k], device_id=to, device_id_type=MESH)

        def start(make):
            make().start()
            started.append(make)

        def p0():
            for i, e in enumerate(ent):
                start(lambda i=i, e=e: rcopy(i, 0, g[e[0]].at[:, rows_of(i, 1 - cc), :], wk(i)[0], sib))

        def p1():
            for i, e in enumerate(ent):
                recv_a, p32, p16, r1 = wk(i)[:4]
                n1, n2, s1, s2, sd = geo[i]
                rcopy(i, 0, recv_a, recv_a, sib).wait_recv()
                _pipe(lambda a, b: (a + b, a + b), [g[e[0]].at[:, rows_of(i, cc), :], recv_a], [p32, p16], e[4])
                start(lambda i=i, s1=s1, n1=n1: rcopy(i, 1, one(wk(i)[2], s1), one(wk(i)[3], 0), (*n1, cc)))
                start(lambda i=i, sd=sd, n1=n1: rcopy(i, 2, one(wk(i)[2], sd), one(wk(i)[3], 1), (*n1, cc)))

        def p2():
            for i, e in enumerate(ent):
                _, p32, _, r1, qme, qs2, r2 = wk(i)
                n1, n2, s1, s2, sd = geo[i]
                rcopy(i, 1, one(r1, 0), one(r1, 0), sib).wait_recv()
                rcopy(i, 2, one(r1, 1), one(r1, 1), sib).wait_recv()
                _pipe(lambda a, b, c, d: (a + b.astype(F32), c + d.astype(F32)),
                      [one(p32, s2), one(r1, 1), one(p32, me), one(r1, 0)], [qs2, qme], e[4])
                start(lambda i=i, n2=n2: rcopy(i, 3, wk(i)[5], wk(i)[6], (*n2, cc)))

        def p3():
            for i, e in enumerate(ent):
                qme, r2 = wk(i)[4], wk(i)[6]
                rcopy(i, 3, r2, r2, sib).wait_recv()
                mine = fin[e[0]].at[rows_of(i, cc), :]
                _pipe(lambda a, b: (a + b.astype(F32),), [qme.at[0], r2.at[0]], [mine], e[4])
                start(lambda i=i, e=e: rcopy(i, 4, fin[e[0]].at[rows_of(i, cc), :], fin[e[0]].at[rows_of(i, cc), :], sib))

        def p4():
            for i, e in enumerate(ent):
                other = fin[e[0]].at[rows_of(i, 1 - cc), :]
                rcopy(i, 4, other, other, sib).wait_recv()
            for make in started:
                make().wait_send()

        return [p0, p1, p2, p3, p4]


def _run_job(job, name):
    ni, no = len(job.inputs), len(job.out_shapes)

    def body(*refs):
        for ph in job.phases(refs[:ni], refs[ni:ni + no], refs[ni + no:]):
            ph()

    return pl.pallas_call(
        body, name=name, in_specs=[ANY] * ni, out_specs=[ANY] * no, out_shape=job.out_shapes, scratch_shapes=job.scratch,
        compiler_params=pltpu.CompilerParams(vmem_limit_bytes=VMEM_LIMIT),
    )(*job.inputs)


def _hosted(body, *, name, grid, in_specs, out_specs, out_shape, scratch_shapes, args, sem, side=None):
    if side is None:
        return pl.pallas_call(body, name=name, grid=grid, in_specs=in_specs, out_specs=out_specs, out_shape=out_shape,
                              scratch_shapes=scratch_shapes, compiler_params=_cp(sem))(*args), None
    job = side
    ni, no, ns = len(in_specs), len(out_specs), len(scratch_shapes)
    ji, jo = len(job.inputs), len(job.out_shapes)
    n_steps = 1
    for extent in grid:
        n_steps *= extent

    def wrapped(*refs):
        own_in, refs = refs[:ni], refs[ni:]
        job_in, refs = refs[:ji], refs[ji:]
        own_out, refs = refs[:no], refs[no:]
        job_out, refs = refs[:jo], refs[jo:]
        own_scr, job_scr = refs[:ns], refs[ns:]
        step = 0
        for d, extent in enumerate(grid):
            step = step * extent + pl.program_id(d)
        phases = job.phases(job_in, job_out, job_scr)
        steps = [min(int(f * n_steps), n_steps - 1) for f in job.at] + [n_steps - 1]
        assert len(steps) == len(phases) and steps == sorted(steps)
        for at, ph in zip(steps, phases):
            pl.when(step == at)(ph)
        body(*own_in, *own_out, *own_scr)

    res = pl.pallas_call(
        wrapped, name=name, grid=grid, in_specs=list(in_specs) + [ANY] * ji, out_specs=list(out_specs) + [ANY] * jo,
        out_shape=list(out_shape) + list(job.out_shapes), scratch_shapes=list(scratch_shapes) + list(job.scratch),
        compiler_params=_cp(("arbitrary",) * len(grid)),
    )(*args, *job.inputs)
    return res[:no], res[no:]


def _proj_dw(xn, dproj_sh, *, tm=512, tk=1024):
    s, d = xn.shape
    tk = _tile(s, tk)
    nk = s // tk

    def body(a_ref, b_ref, o_ref, acc):
        kk = pl.program_id(2)
        part = _dot_tn(a_ref[...], b_ref[0])

        @pl.when(kk == 0)
        def _():
            acc[...] = part

        @pl.when(kk > 0)
        def _():
            acc[...] += part

        @pl.when(kk == nk - 1)
        def _():
            o_ref[0] = acc[...]

    return pl.pallas_call(
        body, name="proj_dw", grid=(N_CHIPS, d // tm, nk),
        in_specs=[pl.BlockSpec((tk, tm), lambda j, i, q: (q, i)), pl.BlockSpec((1, tk, W_IN_PAD), lambda j, i, q: (j, q, 0))],
        out_specs=pl.BlockSpec((1, tm, W_IN_PAD), lambda j, i, q: (j, i, 0)),
        out_shape=jax.ShapeDtypeStruct((N_CHIPS, d, W_IN_PAD), F32), scratch_shapes=[pltpu.VMEM((tm, W_IN_PAD), F32)],
        compiler_params=_cp(("parallel", "parallel", "arbitrary")),
    )(xn, dproj_sh)


def _proj_dx(dproj_sh, w_sh, *, tm=1024, side=None):
    s = dproj_sh.shape[1]
    d = w_sh.shape[1]
    tm = _tile(s, tm)

    def body(a_ref, b_ref, o_ref, acc):
        kk = pl.program_id(1)
        part = _dot_nt(a_ref[0], b_ref[0])

        @pl.when(kk == 0)
        def _():
            acc[...] = part

        @pl.when(kk > 0)
        def _():
            acc[...] += part

        @pl.when(kk == N_CHIPS - 1)
        def _():
            o_ref[...] = acc[...]

    own, extra = _hosted(
        body, name="proj_dx", grid=(s // tm, N_CHIPS),
        in_specs=[pl.BlockSpec((1, tm, W_IN_PAD), lambda i, q: (q, i, 0)), pl.BlockSpec((1, d, W_IN_PAD), lambda i, q: (q, 0, 0))],
        out_specs=[pl.BlockSpec((tm, d), lambda i, q: (i, 0))],
        out_shape=[jax.ShapeDtypeStruct((s, d), F32)], scratch_shapes=[pltpu.VMEM((tm, d), F32)],
        args=(dproj_sh, w_sh), sem=("parallel", "arbitrary"), side=side)
    return own[0] if side is None else (own[0], extra)


def _up_dx(dup, w_sh, *, tm=1024):
    s = dup.shape[1]
    d, wsh = w_sh.shape[1:]
    tm = _tile(s, tm)

    def body(a_ref, b_ref, o_ref, acc):
        kk = pl.program_id(1)
        part = _dot_nt(a_ref[0], b_ref[0])

        @pl.when(kk == 0)
        def _():
            acc[...] = part

        @pl.when(kk > 0)
        def _():
            acc[...] += part

        @pl.when(kk == N_CHIPS - 1)
        def _():
            o_ref[...] = acc[...]

    return pl.pallas_call(
        body, name="up_dx", grid=(s // tm, N_CHIPS),
        in_specs=[pl.BlockSpec((1, tm, wsh), lambda i, q: (q >> 1, i, q & 1)), pl.BlockSpec((1, d, wsh), lambda i, q: (q, 0, 0))],
        out_specs=pl.BlockSpec((tm, d), lambda i, q: (i, 0)),
        out_shape=jax.ShapeDtypeStruct((s, d), F32), scratch_shapes=[pltpu.VMEM((tm, d), F32)],
        compiler_params=_cp(("parallel", "arbitrary")),
    )(dup, w_sh)


def _up_dw(hn, dup, *, tk=1024):
    s, d = hn.shape
    wsh = 2 * D_FF // N_CHIPS
    tk = _tile(s, tk)
    nk = s // tk

    def body(a_ref, b_ref, o_ref, acc):
        kk = pl.program_id(1)
        part = _dot_tn(a_ref[...], b_ref[0])

        @pl.when(kk == 0)
        def _():
            acc[...] = part

        @pl.when(kk > 0)
        def _():
            acc[...] += part

        @pl.when(kk == nk - 1)
        def _():
            o_ref[0] = acc[...]

    return pl.pallas_call(
        body, name="up_dw", grid=(N_CHIPS, nk),
        in_specs=[pl.BlockSpec((tk, d), lambda j, q: (q, 0)), pl.BlockSpec((1, tk, wsh), lambda j, q: (j >> 1, q, j & 1))],
        out_specs=pl.BlockSpec((1, d, wsh), lambda j, q: (j, 0, 0)),
        out_shape=jax.ShapeDtypeStruct((N_CHIPS, d, wsh), F32), scratch_shapes=[pltpu.VMEM((d, wsh), F32)],
        compiler_params=_cp(("parallel", "arbitrary")),
    )(hn, dup)


BIG_ROWS =(IN_DIM // 4, Q_DIM // 4, D_INNER // 4, D_MODEL // 4, 2 * D_FF // 4, D_FF // 4)
PACK_ROWS = 5376


def _pack_shards(parts):
    rows = [p.reshape(-1, D_MODEL) for p in parts]
    pad = PACK_ROWS - sum(BIG_ROWS)
    return jnp.concatenate(rows + [jnp.zeros((pad, D_MODEL), rows[0].dtype)], axis=0)


def _unpack_shards(buf):
    out, off = [], 0
    for n in BIG_ROWS:
        out.append(buf[off:off + n])
        off += n
    return out


def _permute_cols_in(w):
    pad = jnp.zeros((w.shape[0], PW - IN_DIM), w.dtype)
    return jnp.concatenate([w[:, :6656], w[:, 6688:], w[:, 6656:6688], pad], axis=1)


def _unpermute_cols_in(g):
    return jnp.concatenate([g[:, :6656], g[:, O_DT:O_DT + 32], g[:, 6656:O_DT]], axis=1)


SMALL = ("norm1_w", "b_gate", "attn_sinks", "ssd_conv_b", "dt_bias", "a_log", "d_skip", "ssd_norm_w", "norm2_w",
         "ffn_conv_b", "final_norm_w", "ssd_conv_w", "ffn_conv_w")


def _pad128(v):
    v = v.reshape(-1)
    return jnp.pad(v, (0, (-v.shape[0]) % 128))


def _pack_small(parts):
    flat = jnp.concatenate([_pad128(p) for p in parts])
    flat = jnp.pad(flat, (0, (-flat.shape[0]) % 1024))
    return flat.reshape(-1, 128)


def _unpack_small(buf, shapes):
    flat, out, off = buf.reshape(-1), [], 0
    for shp in shapes:
        n = 1
        for q in shp:
            n *= q
        out.append(flat[off:off + n].reshape(shp))
        off += n + (-n) % 128
    return out


def _vec128(v):
    return jnp.pad(v.reshape(1, -1), ((0, 0), (0, 128 - v.shape[-1])))


def kernel(x, norm1_w, w_in, b_gate, attn_sinks, w_attn_o, ssd_conv_w, ssd_conv_b, dt_bias, a_log, d_skip, ssd_norm_w, w_ssd_o, w_out, norm2_w, w_up, ffn_conv_w, ffn_conv_b, w_down, final_norm_w, loss_target, m_norm1_w, m_w_in, m_b_gate, m_attn_sinks, m_w_attn_o, m_ssd_conv_w, m_ssd_conv_b, m_dt_bias, m_a_log, m_d_skip, m_ssd_norm_w, m_w_ssd_o, m_w_out, m_norm2_w, m_w_up, m_ffn_conv_w, m_ffn_conv_b, m_w_down, m_final_norm_w, v_norm1_w, v_w_in, v_b_gate, v_attn_sinks, v_w_attn_o, v_ssd_conv_w, v_ssd_conv_b, v_dt_bias, v_a_log, v_d_skip, v_ssd_norm_w, v_w_ssd_o, v_w_out, v_norm2_w, v_w_up, v_ffn_conv_w, v_ffn_conv_b, v_w_down, v_final_norm_w):
    ix, iy, ic = lax.axis_index("x"), lax.axis_index("y"), lax.axis_index("c")
    chip = 2 * ix + iy
    x2 = x[0]
    tgt = loss_target[0]
    s = x2.shape[0]

    wsh = IN_DIM // N_CHIPS
    big_shards = dict(w_in=jnp.pad(w_in[0], ((0, 0), (0, W_IN_PAD - wsh))), w_attn_o=w_attn_o[0], w_ssd_o=w_ssd_o[0],
                      w_out=w_out[0], w_up=w_up[0], w_down=w_down[0])
    gathered = {}
    (gathered["w_in"],) = _run_job(_GatherJob(("w_in",), [big_shards["w_in"]]), "gather_w_in")
    early, late = ("w_attn_o", "w_ssd_o", "w_out"), ("w_up", "w_down")
    gather_early = _GatherJob(early, [big_shards[n] for n in early], at=(0.0, 0.5, 0.8))
    gather_late = _GatherJob(late, [big_shards[n] for n in late], at=(0.0, 0.55, 0.85))
    gw = gathered["w_in"]
    lo, hi = O_GA - 3 * wsh, O_GA + N_SSD_HEADS - 3 * wsh
    w_in_p = jnp.concatenate([gw[0, :, :wsh], gw[1, :, :wsh], gw[2, :, :wsh], gw[3, :, :lo], gw[3, :, hi:wsh],
                              gw[3, :, lo:hi], jnp.zeros((D_MODEL, PW - IN_DIM), BF16)], axis=1)
    small_sh = _pack_small([ssd_conv_w[0], ffn_conv_w[0]])
    small_all = _all_gather_small(small_sh)
    sc_parts = [_unpack_small(small_all[j], [(4, XBC_DIM // 4), (3, 2 * D_FF // 4)]) for j in range(N_CHIPS)]
    ssd_cw = jnp.concatenate([p[0] for p in sc_parts], axis=1)
    ffn_cw = jnp.concatenate([p[1] for p in sc_parts], axis=1)

    sinks128 = _vec128(attn_sinks)
    dtb128, alog128, dskip128 = _vec128(dt_bias), _vec128(a_log), _vec128(d_skip)

    xn = _rms_fwd(x2, norm1_w, name="norm1_fwd")
    proj, got = _mm(xn, w_in_p, name="proj_fwd", tn=1280, side=gather_early)
    gathered.update(zip(early, got))
    attn_pre, got = _attn_fwd(proj, sinks128, side=gather_late)
    gathered.update(zip(late, got))
    full = {n: gathered[n].reshape(-1, D_MODEL) for n in ("w_attn_o", "w_ssd_o", "w_out", "w_down")}
    full["w_up"] = gathered["w_up"]
    attn = _mm(attn_pre, full["w_attn_o"], name="attn_o_fwd")
    xbc = _ssd_conv_fwd(proj, ssd_cw, ssd_conv_b)
    y_ssd, hprev = _ssd_fwd(xbc, proj, dtb128, alog128, dskip128)
    yn = _gate_norm_fwd(y_ssd, proj, ssd_norm_w)
    ssd_out = _mm(yn, full["w_ssd_o"], name="ssd_o_fwd")
    merged = _merge_fwd(proj, b_gate, attn, ssd_out)
    h1 = _mm(merged, full["w_out"], name="out_fwd", resid=x2)
    hn = _rms_fwd(h1, norm2_w, name="norm2_fwd")
    up = _mm(hn, full["w_up"], name="up_fwd")
    act = _ffn_act_fwd(up, ffn_cw, ffn_conv_b)
    h2 = _mm(act, full["w_down"], name="down_fwd", resid=h1, tk=1408)

    dh2, loss_blk, g_final = _loss_bwd(h2, tgt, final_norm_w.reshape(1, -1))
    dact = _mm(dh2, full["w_down"], name="down_dx", tb=True, tn=1408)
    g_down = _mm(act, dh2, name="down_dw", ta=True, tm=1408)
    dup, g_ffn_cw, g_ffn_cb = _ffn_act_bwd(dact, up, ffn_cw, ffn_conv_b)
    dhn = _up_dx(dup, full["w_up"])
    g_up = _up_dw(hn, dup)
    dh1, g_norm2 = _rms_bwd(dhn, h1, norm2_w, dh2, name="norm2_bwd")
    dmerged = _mm(dh1, full["w_out"], name="out_dx", tb=True)
    g_out = _mm(merged, dh1, name="out_dw", ta=True)
    dattn, dssd_out, dga, dgs, g_ba, g_bs = _merge_bwd(dmerged, proj, b_gate, attn, ssd_out)
    dyn = _mm(dssd_out, full["w_ssd_o"], name="ssd_o_dx", tb=True)
    g_ssd_o = _mm(yn, dssd_out, name="ssd_o_dw", ta=True)
    dy_ssd, dz, g_ssd_norm = _gate_norm_bwd(dyn, y_ssd, proj, ssd_norm_w)
    slot = lambda g: g.reshape(N_CHIPS, -1, D_MODEL)
    big_grads = {}
    red = ("w_down", "w_up")
    (dxbc, ddt, dvec), got = _ssd_bwd(xbc, proj, dtb128, alog128, dskip128, hprev, dy_ssd,
                                      side=_ReduceJob(red, [slot(g_down), g_up], at=(0.0, 0.3, 0.8, 0.95)))
    big_grads.update(zip(red, got))
    dxbc_raw, g_ssd_cw, g_ssd_cb = _ssd_conv_bwd(dxbc, proj, ssd_cw, ssd_conv_b)
    dattn_pre = _mm(dattn, full["w_attn_o"], name="attn_o_dx", tb=True)
    g_attn_o = _mm(attn_pre, dattn, name="attn_o_dw", ta=True)
    red = ("w_out", "w_ssd_o", "w_attn_o")
    (dq, dk, dv, dsk), got = _attn_bwd(proj, sinks128, attn_pre, dattn_pre,
                                       side=_ReduceJob(red, [slot(g_out), slot(g_ssd_o), slot(g_attn_o)],
                                                       at=(0.0, 0.2, 0.5, 0.7)))
    big_grads.update(zip(red, got))
    pieces = [dq, dk, dv, dz, dxbc_raw, ddt[:, :N_SSD_HEADS], dga, dgs]
    shards_d, off = [[] for _ in range(N_CHIPS)], 0
    for p in pieces:
        for j in range(N_CHIPS):
            a, b = max(off, j * wsh), min(off + p.shape[1], (j + 1) * wsh)
            if a < b:
                shards_d[j].append(p[:, a - off:b - off])
        off += p.shape[1]
    zpad = jnp.zeros((s, W_IN_PAD - wsh), BF16)
    dproj_sh = jnp.stack([jnp.concatenate(sh + [zpad], axis=1) for sh in shards_d])
    g_in = _proj_dw(xn, dproj_sh)
    dxn, got = _proj_dx(dproj_sh, gathered["w_in"], side=_ReduceJob(("w_in",), [g_in], at=(0.0, 0.3, 0.8, 0.95)))
    big_grads["w_in"] = got[0][:, :wsh]
    dx, g_norm1 = _rms_bwd(dxn, x2, norm1_w, dh1, name="norm1_bwd")


    small_g = dict(
        norm1_w=g_norm1, b_gate=jnp.concatenate([g_ba, g_bs], axis=1), attn_sinks=dsk[0:1, :16], ssd_conv_b=g_ssd_cb,
        dt_bias=dvec[0:1, :32], a_log=dvec[1:2, :32], d_skip=dvec[2:3, :32], ssd_norm_w=g_ssd_norm, norm2_w=g_norm2,
        ffn_conv_b=jnp.concatenate([g_ffn_cb[0], g_ffn_cb[1]], axis=1), final_norm_w=g_final, ssd_conv_w=g_ssd_cw,
        ffn_conv_w=jnp.concatenate([g_ffn_cw[0], g_ffn_cw[1]], axis=1))
    small_buf = _pack_small([small_g[n] for n in SMALL] + [loss_blk])
    small_sum = _all_reduce_small(small_buf)
    small_shapes = [(1, D_MODEL), (1, 2 * D_MODEL), (1, 16), (1, XBC_DIM), (1, 32), (1, 32), (1, 32), (1, D_INNER),
                    (1, D_MODEL), (1, 2 * D_FF), (D_MODEL,), (4, XBC_DIM), (3, 2 * D_FF), (1, 128)]
    small_list = _unpack_small(small_sum, small_shapes)
    loss = small_list[-1][0, 0]
    grads = dict(zip(SMALL, small_list[:-1]))
    grads["ssd_conv_w"] = lax.dynamic_slice_in_dim(grads["ssd_conv_w"], chip * (XBC_DIM // 4), XBC_DIM // 4, axis=1)
    grads["ffn_conv_w"] = lax.dynamic_slice_in_dim(grads["ffn_conv_w"], chip * (2 * D_FF // 4), 2 * D_FF // 4, axis=1)
    grads.update(big_grads)

    weights = dict(norm1_w=norm1_w, w_in=w_in, b_gate=b_gate, attn_sinks=attn_sinks, w_attn_o=w_attn_o, ssd_conv_w=ssd_conv_w,
                   ssd_conv_b=ssd_conv_b, dt_bias=dt_bias, a_log=a_log, d_skip=d_skip, ssd_norm_w=ssd_norm_w, w_ssd_o=w_ssd_o,
                   w_out=w_out, norm2_w=norm2_w, w_up=w_up, ffn_conv_w=ffn_conv_w, ffn_conv_b=ffn_conv_b, w_down=w_down,
                   final_norm_w=final_norm_w)
    ms = dict(norm1_w=m_norm1_w, w_in=m_w_in, b_gate=m_b_gate, attn_sinks=m_attn_sinks, w_attn_o=m_w_attn_o,
              ssd_conv_w=m_ssd_conv_w, ssd_conv_b=m_ssd_conv_b, dt_bias=m_dt_bias, a_log=m_a_log, d_skip=m_d_skip,
              ssd_norm_w=m_ssd_norm_w, w_ssd_o=m_w_ssd_o, w_out=m_w_out, norm2_w=m_norm2_w, w_up=m_w_up,
              ffn_conv_w=m_ffn_conv_w, ffn_conv_b=m_ffn_conv_b, w_down=m_w_down, final_norm_w=m_final_norm_w)
    vs = dict(norm1_w=v_norm1_w, w_in=v_w_in, b_gate=v_b_gate, attn_sinks=v_attn_sinks, w_attn_o=v_w_attn_o,
              ssd_conv_w=v_ssd_conv_w, ssd_conv_b=v_ssd_conv_b, dt_bias=v_dt_bias, a_log=v_a_log, d_skip=v_d_skip,
              ssd_norm_w=v_ssd_norm_w, w_ssd_o=v_w_ssd_o, w_out=v_w_out, norm2_w=v_norm2_w, w_up=v_w_up,
              ffn_conv_w=v_ffn_conv_w, ffn_conv_b=v_ffn_conv_b, w_down=v_w_down, final_norm_w=v_final_norm_w)
    order = list(weights)
    deltas, new_m, new_v = {}, {}, {}
    for n in BIG:
        shp = weights[n].shape
        d_, m_, v_ = _adamw(weights[n][0], grads[n], ms[n][0], vs[n][0], name="adamw_" + n)
        deltas[n], new_m[n], new_v[n] = d_.reshape(shp), m_.reshape(shp), v_.reshape(shp)
    smalls = [n for n in order if n not in BIG]
    as2d = lambda a: a.reshape(-1, a.shape[-1])
    res = _adamw_many(*[[as2d(src[n][0] if src[n].ndim == 3 else src[n]) for n in smalls] for src in (weights, grads, ms, vs)])
    for i, n in enumerate(smalls):
        deltas[n], new_m[n], new_v[n] = (res[q * len(smalls) + i].reshape(weights[n].shape) for q in range(3))
    out_grads = [grads[n].reshape(weights[n].shape) for n in order]
    return (loss, dx[None], *out_grads, *[deltas[n] for n in order], *[new_m[n] for n in order], *[new_v[n] for n in order])
```

```python
import functools

import jax
import jax.numpy as jnp
from jax import lax
from jax.experimental import pallas as pl
from jax.experimental.pallas import tpu as pltpu

F32 = jnp.float32
BF16 = jnp.bfloat16
HI = lax.Precision.HIGHEST

D_MODEL = 1024
Q_DIM = 1024
KV_DIM = 256
D_INNER = 2048
BC_DIM = 512
XBC_DIM = 3072
N_SSD_HEADS = 32
D_FF = 2816
IN_DIM = 8736
BLK = 128
EPS = 1e-5
NEG = -1e30

O_Q, O_K, O_V, O_Z, O_X, O_GA, O_GS, O_DT = 0, 1024, 1280, 1536, 3584, 6656, 7680, 8704
PW = 8960

ADAM_LR, ADAM_B1, ADAM_B2, ADAM_EPS, ADAM_WD, ADAM_STEP = 0.001, 0.9, 0.999, 1e-08, 0.01, 10

VMEM_LIMIT = 52 * 1024 * 1024
MESH = pl.DeviceIdType.MESH


def _cp(sem=None):
    return pltpu.CompilerParams(dimension_semantics=sem, vmem_limit_bytes=VMEM_LIMIT)


def _dot(a, b, prec=None):
    return jnp.dot(a, b, preferred_element_type=F32, precision=prec)


def _dot_nt(a, b, prec=None):
    return lax.dot_general(a, b, (((1,), (1,)), ((), ())), preferred_element_type=F32, precision=prec)


def _dot_tn(a, b, prec=None):
    return lax.dot_general(a, b, (((0,), (0,)), ((), ())), preferred_element_type=F32, precision=prec)


def _sigmoid(x):
    return 0.5 * jnp.tanh(0.5 * x) + 0.5


def _tile(n, want):
    t = min(n, want)
    while n % t:
        t -= 128
    return t


def _mm(a, b, *, name, ta=False, tb=False, out_dtype=F32, resid=None, tm=1024, tn=1024, tk=1024, side=None):
    m, k = (a.shape[1], a.shape[0]) if ta else a.shape
    slots = b.ndim == 3
    if slots:
        n = b.shape[1] if tb else b.shape[0] * b.shape[2]
        tn, tk = (tn, b.shape[2]) if tb else (b.shape[2], tk)
    else:
        n = b.shape[0] if tb else b.shape[1]
    tm, tn, tk = _tile(m, tm), _tile(n, tn), _tile(k, tk)
    nk = k // tk
    dn = (((0 if ta else 1,), (1 if tb else 0,)), ((), ()))

    def body(*refs):
        if resid is None:
            a_ref, b_ref, o_ref, acc = refs
        else:
            a_ref, b_ref, r_ref, o_ref, acc = refs
        kk = pl.program_id(2)
        bv = b_ref[0] if slots else b_ref[...]
        part = lax.dot_general(a_ref[...].astype(BF16), bv.astype(BF16), dn, preferred_element_type=F32)

        @pl.when(kk == 0)
        def _():
            acc[...] = part

        @pl.when(kk > 0)
        def _():
            acc[...] += part

        @pl.when(kk == nk - 1)
        def _():
            r = acc[...]
            if resid is not None:
                r = r + r_ref[...]
            o_ref[...] = r.astype(out_dtype)

    a_spec = pl.BlockSpec((tk, tm), lambda i, j, q: (q, i)) if ta else pl.BlockSpec((tm, tk), lambda i, j, q: (i, q))
    if slots:
        b_spec = (pl.BlockSpec((1, tn, tk), lambda i, j, q: (q, j, 0)) if tb
                  else pl.BlockSpec((1, tk, tn), lambda i, j, q: (j, q, 0)))
    else:
        b_spec = pl.BlockSpec((tn, tk), lambda i, j, q: (j, q)) if tb else pl.BlockSpec((tk, tn), lambda i, j, q: (q, j))
    o_spec = pl.BlockSpec((tm, tn), lambda i, j, q: (i, j))
    ins, specs = [a, b], [a_spec, b_spec]
    if resid is not None:
        ins.append(resid)
        specs.append(o_spec)
    own, extra = _hosted(
        body, name=name, grid=(m // tm, n // tn, nk), in_specs=specs, out_specs=[o_spec],
        out_shape=[jax.ShapeDtypeStruct((m, n), out_dtype)], scratch_shapes=[pltpu.VMEM((tm, tn), F32)],
        args=ins, sem=("parallel", "parallel", "arbitrary"), side=side)
    return own[0] if side is None else (own[0], extra)


def _rms_fwd(x, w, *, name, tm=512):
    s, d = x.shape
    tm = _tile(s, tm)

    def body(x_ref, w_ref, o_ref):
        xv = x_ref[...]
        r = lax.rsqrt(jnp.mean(xv * xv, axis=-1, keepdims=True) + EPS)
        o_ref[...] = ((xv * r) * w_ref[...]).astype(BF16)

    return pl.pallas_call(
        body, name=name, grid=(s // tm,),
        in_specs=[pl.BlockSpec((tm, d), lambda i: (i, 0)), pl.BlockSpec((1, d), lambda i: (0, 0))],
        out_specs=pl.BlockSpec((tm, d), lambda i: (i, 0)),
        out_shape=jax.ShapeDtypeStruct((s, d), BF16), compiler_params=_cp(("parallel",)),
    )(x, w)


def _rms_bwd(dy, x, w, resid, *, name, tm=512):
    s, d = x.shape
    tm = _tile(s, tm)

    def body(dy_ref, x_ref, w_ref, r_ref, dx_ref, dw_ref):
        i = pl.program_id(0)
        xv = x_ref[...]
        r = lax.rsqrt(jnp.mean(xv * xv, axis=-1, keepdims=True) + EPS)
        xh = xv * r
        dyv = dy_ref[...]
        g = dyv * w_ref[...]
        dx_ref[...] = r_ref[...] + r * (g - xh * jnp.mean(g * xh, axis=-1, keepdims=True))
        part = jnp.sum(dyv * xh, axis=0, keepdims=True)

        @pl.when(i == 0)
        def _():
            dw_ref[...] = part

        @pl.when(i > 0)
        def _():
            dw_ref[...] += part

    row = pl.BlockSpec((tm, d), lambda i: (i, 0))
    vec = pl.BlockSpec((1, d), lambda i: (0, 0))
    return pl.pallas_call(
        body, name=name, grid=(s // tm,), in_specs=[row, row, vec, row], out_specs=[row, vec],
        out_shape=[jax.ShapeDtypeStruct((s, d), F32), jax.ShapeDtypeStruct((1, d), F32)],
        compiler_params=_cp(("arbitrary",)),
    )(dy, x, w, resid)


def _loss_bwd(h2, tgt, wf, *, tm=512):
    s, d = h2.shape
    tm = _tile(s, tm)

    def body(h_ref, t_ref, w_ref, dh_ref, loss_ref, dw_ref):
        i = pl.program_id(0)
        hv = h_ref[...]
        r = lax.rsqrt(jnp.mean(hv * hv, axis=-1, keepdims=True) + EPS)
        xh = hv * r
        wv = w_ref[...]
        e = xh * wv - t_ref[...]
        lpart = 0.5 * jnp.sum(jnp.mean(e * e, axis=-1, keepdims=True), axis=0, keepdims=True)
        dout = e * (1.0 / d)
        g = dout * wv
        dh_ref[...] = r * (g - xh * jnp.mean(g * xh, axis=-1, keepdims=True))
        part = jnp.sum(dout * xh, axis=0, keepdims=True)
        lrow = jnp.broadcast_to(lpart, (1, 128))

        @pl.when(i == 0)
        def _():
            dw_ref[...] = part
            loss_ref[...] = lrow

        @pl.when(i > 0)
        def _():
            dw_ref[...] += part
            loss_ref[...] += lrow

    row = pl.BlockSpec((tm, d), lambda i: (i, 0))
    vec = pl.BlockSpec((1, d), lambda i: (0, 0))
    return pl.pallas_call(
        body, name="loss_bwd", grid=(s // tm,), in_specs=[row, row, vec],
        out_specs=[row, pl.BlockSpec((1, 128), lambda i: (0, 0)), vec],
        out_shape=[jax.ShapeDtypeStruct((s, d), F32), jax.ShapeDtypeStruct((1, 128), F32),
                   jax.ShapeDtypeStruct((1, d), F32)],
        compiler_params=_cp(("arbitrary",)),
    )(h2, tgt, wf)


def _attn_mask(n):
    qi = lax.broadcasted_iota(jnp.int32, (4 * BLK, 2 * BLK), 0) & (BLK - 1)
    si = lax.broadcasted_iota(jnp.int32, (4 * BLK, 2 * BLK), 1)
    dist = BLK + qi - si
    kpos = n * BLK - BLK + si
    return (dist >= 0) & (dist < BLK) & (kpos >= 0)


def _attn_probs(q_ref, kc_ref, kp_ref, sk_ref, kvh, valid):
    hs = slice(kvh * 64, (kvh + 1) * 64)
    kb = jnp.concatenate([kp_ref[:, hs], kc_ref[:, hs]], axis=0).astype(BF16)
    qs = jnp.concatenate([q_ref[:, (kvh * 4 + g) * 64:(kvh * 4 + g + 1) * 64] for g in range(4)], axis=0).astype(BF16)
    s = _dot_nt(qs, kb) * 0.125
    s = jnp.where(valid, s, NEG)
    sink = jnp.concatenate(
        [jnp.broadcast_to(sk_ref[0:1, kvh * 4 + g:kvh * 4 + g + 1], (BLK, 1)) for g in range(4)], axis=0)
    m = jnp.maximum(jnp.max(s, axis=1, keepdims=True), sink)
    p = jnp.where(valid, jnp.exp(s - m), 0.0)
    es = jnp.exp(sink - m)
    denom = jnp.sum(p, axis=1, keepdims=True) + es
    return qs, kb, p / denom, es / denom


def _attn_fwd(proj, sinks, side=None):
    s = proj.shape[0]
    nb = s // BLK

    def body(q_ref, kc_ref, kp_ref, vc_ref, vp_ref, sk_ref, o_ref):
        valid = _attn_mask(pl.program_id(0))
        for kvh in range(4):
            hs = slice(kvh * 64, (kvh + 1) * 64)
            _, _, probs, _ = _attn_probs(q_ref, kc_ref, kp_ref, sk_ref, kvh, valid)
            vb = jnp.concatenate([vp_ref[:, hs], vc_ref[:, hs]], axis=0).astype(BF16)
            o = _dot(probs.astype(BF16), vb)
            for g in range(4):
                h = kvh * 4 + g
                o_ref[:, h * 64:(h + 1) * 64] = o[g * BLK:(g + 1) * BLK].astype(BF16)

    prev = lambda n: jnp.maximum(n - 1, 0)
    own, extra = _hosted(
        body, name="attn_fwd", grid=(nb,),
        in_specs=[pl.BlockSpec((BLK, Q_DIM), lambda n: (n, 0)),
                  pl.BlockSpec((BLK, KV_DIM), lambda n: (n, O_K // KV_DIM)),
                  pl.BlockSpec((BLK, KV_DIM), lambda n: (prev(n), O_K // KV_DIM)),
                  pl.BlockSpec((BLK, KV_DIM), lambda n: (n, O_V // KV_DIM)),
                  pl.BlockSpec((BLK, KV_DIM), lambda n: (prev(n), O_V // KV_DIM)),
                  pl.BlockSpec((1, 128), lambda n: (0, 0))],
        out_specs=[pl.BlockSpec((BLK, Q_DIM), lambda n: (n, 0))],
        out_shape=[jax.ShapeDtypeStruct((s, Q_DIM), BF16)], scratch_shapes=[],
        args=(proj, proj, proj, proj, proj, sinks), sem=("parallel",), side=side)
    return own[0] if side is None else (own[0], extra)


def _attn_bwd(proj, sinks, o, do, side=None):
    s = proj.shape[0]
    nb = s // BLK

    def body(q_ref, kc_ref, kp_ref, vc_ref, vp_ref, sk_ref, o_ref, do_ref,
             dq_ref, dk_ref, dv_ref, dsk_ref, ck, cv, nkp, nkc, nvp, nvc):
        n = pl.program_id(0)

        @pl.when(n == 0)
        def _():
            ck[...] = jnp.zeros_like(ck)
            cv[...] = jnp.zeros_like(cv)
            dsk_ref[...] = jnp.zeros_like(dsk_ref)

        @pl.when(n < nb)
        def _():
            valid = _attn_mask(n)
            lane = lax.broadcasted_iota(jnp.int32, (1, 128), 1)
            dsk = jnp.zeros((1, 128), F32)
            for kvh in range(4):
                hs = slice(kvh * 64, (kvh + 1) * 64)
                qs, kb, probs, psink = _attn_probs(q_ref, kc_ref, kp_ref, sk_ref, kvh, valid)
                vb = jnp.concatenate([vp_ref[:, hs], vc_ref[:, hs]], axis=0).astype(BF16)
                heads = [slice((kvh * 4 + g) * 64, (kvh * 4 + g + 1) * 64) for g in range(4)]
                dos = jnp.concatenate([do_ref[:, hh] for hh in heads], axis=0)
                os_ = jnp.concatenate([o_ref[:, hh] for hh in heads], axis=0).astype(F32)
                delta = jnp.sum(dos * os_, axis=1, keepdims=True)
                dos16 = dos.astype(BF16)
                dp = _dot_nt(dos16, vb)
                ds = (probs * (dp - delta) * 0.125).astype(BF16)
                dqs = _dot(ds, kb)
                dkb = _dot_tn(ds, qs)
                dvb = _dot_tn(probs.astype(BF16), dos16)
                nkp[:, hs] = dkb[:BLK]
                nkc[:, hs] = dkb[BLK:]
                nvp[:, hs] = dvb[:BLK]
                nvc[:, hs] = dvb[BLK:]
                sd = psink * delta
                for g in range(4):
                    dq_ref[:, heads[g]] = dqs[g * BLK:(g + 1) * BLK].astype(BF16)
                    val = -jnp.sum(sd[g * BLK:(g + 1) * BLK], axis=0, keepdims=True)
                    dsk = dsk + jnp.where(lane == kvh * 4 + g, val, 0.0)
            dsk_ref[0:1, :] += dsk
            dk_ref[...] = (ck[...] + nkp[...]).astype(BF16)
            dv_ref[...] = (cv[...] + nvp[...]).astype(BF16)
            ck[...] = nkc[...]
            cv[...] = nvc[...]

        @pl.when(n == nb)
        def _():
            dk_ref[...] = ck[...].astype(BF16)
            dv_ref[...] = cv[...].astype(BF16)

    cur = lambda n: jnp.minimum(n, nb - 1)
    prev = lambda n: jnp.maximum(jnp.minimum(n, nb - 1) - 1, 0)
    outb = lambda n: jnp.maximum(n - 1, 0)
    kv_scr = pltpu.VMEM((BLK, KV_DIM), F32)
    own, extra = _hosted(
        body, name="attn_bwd", grid=(nb + 1,),
        in_specs=[pl.BlockSpec((BLK, Q_DIM), lambda n: (cur(n), 0)),
                  pl.BlockSpec((BLK, KV_DIM), lambda n: (cur(n), O_K // KV_DIM)),
                  pl.BlockSpec((BLK, KV_DIM), lambda n: (prev(n), O_K // KV_DIM)),
                  pl.BlockSpec((BLK, KV_DIM), lambda n: (cur(n), O_V // KV_DIM)),
                  pl.BlockSpec((BLK, KV_DIM), lambda n: (prev(n), O_V // KV_DIM)),
                  pl.BlockSpec((1, 128), lambda n: (0, 0)),
                  pl.BlockSpec((BLK, Q_DIM), lambda n: (cur(n), 0)),
                  pl.BlockSpec((BLK, Q_DIM), lambda n: (cur(n), 0))],
        out_specs=[pl.BlockSpec((BLK, Q_DIM), lambda n: (cur(n), 0)),
                   pl.BlockSpec((BLK, KV_DIM), lambda n: (outb(n), 0)),
                   pl.BlockSpec((BLK, KV_DIM), lambda n: (outb(n), 0)),
                   pl.BlockSpec((8, 128), lambda n: (0, 0))],
        out_shape=[jax.ShapeDtypeStruct((s, Q_DIM), BF16), jax.ShapeDtypeStruct((s, KV_DIM), BF16),
                   jax.ShapeDtypeStruct((s, KV_DIM), BF16), jax.ShapeDtypeStruct((8, 128), F32)],
        scratch_shapes=[kv_scr] * 6, args=(proj, proj, proj, proj, proj, sinks, o, do), sem=("arbitrary",), side=side)
    return own if side is None else (own, extra)


def _shift_down(x, j):
    if j == 0:
        return x
    row = lax.broadcasted_iota(jnp.int32, x.shape, 0)
    return jnp.where(row >= j, pltpu.roll(x, j, 0), 0.0)


def _shift_up(x, j):
    if j == 0:
        return x
    s = x.shape[0]
    row = lax.broadcasted_iota(jnp.int32, x.shape, 0)
    return jnp.where(row < s - j, pltpu.roll(x, s - j, 0), 0.0)


def _conv(x, w_ref, b_ref):
    kk = w_ref.shape[0]
    y = _shift_down(x, kk - 1) * w_ref[0:1, :]
    for q in range(1, kk):
        y = y + _shift_down(x, kk - 1 - q) * w_ref[q:q + 1, :]
    return y + b_ref[...]


def _conv_bwd(dy, x, w_ref, dx_dtype):
    kk = w_ref.shape[0]
    dx = _shift_up(dy, kk - 1) * w_ref[0:1, :]
    dws = [jnp.sum(dy * _shift_down(x, kk - 1), axis=0, keepdims=True)]
    for q in range(1, kk):
        dx = dx + _shift_up(dy, kk - 1 - q) * w_ref[q:q + 1, :]
        dws.append(jnp.sum(dy * _shift_down(x, kk - 1 - q), axis=0, keepdims=True))
    return dx.astype(dx_dtype), dws, jnp.sum(dy, axis=0, keepdims=True)


def _dsilu(y, sg):
    return sg * (1.0 + y * (1.0 - sg))


CT = 256


def _ssd_conv_fwd(proj, w, b):
    s = proj.shape[0]

    def body(x_ref, w_ref, b_ref, o_ref):
        y = _conv(x_ref[...], w_ref, b_ref)
        o_ref[...] = y * _sigmoid(y)

    return pl.pallas_call(
        body, name="ssd_conv_fwd", grid=(XBC_DIM // CT,),
        in_specs=[pl.BlockSpec((s, CT), lambda i: (0, O_X // CT + i)), pl.BlockSpec((4, CT), lambda i: (0, i)),
                  pl.BlockSpec((1, CT), lambda i: (0, i))],
        out_specs=pl.BlockSpec((s, CT), lambda i: (0, i)),
        out_shape=jax.ShapeDtypeStruct((s, XBC_DIM), F32), compiler_params=_cp(("parallel",)),
    )(proj, w, b)


def _ssd_conv_bwd(dact, proj, w, b):
    s = proj.shape[0]

    def body(d_ref, x_ref, w_ref, b_ref, dx_ref, dw_ref, db_ref):
        x = x_ref[...]
        y = _conv(x, w_ref, b_ref)
        dy = d_ref[...] * _dsilu(y, _sigmoid(y))
        dx, dws, db = _conv_bwd(dy, x, w_ref, BF16)
        dx_ref[...] = dx
        for q in range(4):
            dw_ref[q:q + 1, :] = dws[q]
        db_ref[...] = db

    return pl.pallas_call(
        body, name="ssd_conv_bwd", grid=(XBC_DIM // CT,),
        in_specs=[pl.BlockSpec((s, CT), lambda i: (0, i)), pl.BlockSpec((s, CT), lambda i: (0, O_X // CT + i)),
                  pl.BlockSpec((4, CT), lambda i: (0, i)), pl.BlockSpec((1, CT), lambda i: (0, i))],
        out_specs=[pl.BlockSpec((s, CT), lambda i: (0, i)), pl.BlockSpec((4, CT), lambda i: (0, i)),
                   pl.BlockSpec((1, CT), lambda i: (0, i))],
        out_shape=[jax.ShapeDtypeStruct((s, XBC_DIM), BF16), jax.ShapeDtypeStruct((4, XBC_DIM), F32),
                   jax.ShapeDtypeStruct((1, XBC_DIM), F32)],
        compiler_params=_cp(("parallel",)),
    )(dact, proj, w, b)


NFT = D_FF // CT


def _ffn_act_fwd(up, w, b):
    s = up.shape[0]

    def body(v_ref, g_ref, wv_ref, wg_ref, bv_ref, bg_ref, o_ref):
        val = _conv(v_ref[...], wv_ref, bv_ref)
        gt = _conv(g_ref[...], wg_ref, bg_ref)
        o_ref[...] = ((gt * _sigmoid(gt)) * val).astype(BF16)

    col = lambda off: (lambda i: (0, off + i))
    return pl.pallas_call(
        body, name="ffn_act_fwd", grid=(NFT,),
        in_specs=[pl.BlockSpec((s, CT), col(0)), pl.BlockSpec((s, CT), col(NFT)),
                  pl.BlockSpec((3, CT), col(0)), pl.BlockSpec((3, CT), col(NFT)),
                  pl.BlockSpec((1, CT), col(0)), pl.BlockSpec((1, CT), col(NFT))],
        out_specs=pl.BlockSpec((s, CT), col(0)),
        out_shape=jax.ShapeDtypeStruct((s, D_FF), BF16), compiler_params=_cp(("parallel",)),
    )(up, up, w, w, b, b)


def _ffn_act_bwd(dact, up, w, b):
    s = up.shape[0]

    def body(d_ref, v_ref, g_ref, wv_ref, wg_ref, bv_ref, bg_ref, dx_ref, dw_ref, db_ref):
        xv, xg = v_ref[...], g_ref[...]
        val = _conv(xv, wv_ref, bv_ref)
        gt = _conv(xg, wg_ref, bg_ref)
        sg = _sigmoid(gt)
        d = d_ref[...]
        for half, (dy, x, w_ref) in enumerate(((d * (gt * sg), xv, wv_ref), (d * val * _dsilu(gt, sg), xg, wg_ref))):
            dx, dws, db = _conv_bwd(dy, x, w_ref, BF16)
            dx_ref[half] = dx
            for q in range(3):
                dw_ref[half, q:q + 1, :] = dws[q]
            db_ref[half] = db

    col = lambda off: (lambda i: (0, off + i))
    both = lambda i: (0, 0, i)
    return pl.pallas_call(
        body, name="ffn_act_bwd", grid=(NFT,),
        in_specs=[pl.BlockSpec((s, CT), col(0)), pl.BlockSpec((s, CT), col(0)), pl.BlockSpec((s, CT), col(NFT)),
                  pl.BlockSpec((3, CT), col(0)), pl.BlockSpec((3, CT), col(NFT)),
                  pl.BlockSpec((1, CT), col(0)), pl.BlockSpec((1, CT), col(NFT))],
        out_specs=[pl.BlockSpec((2, s, CT), both), pl.BlockSpec((2, 3, CT), both), pl.BlockSpec((2, 1, CT), both)],
        out_shape=[jax.ShapeDtypeStruct((2, s, D_FF), BF16), jax.ShapeDtypeStruct((2, 3, D_FF), F32),
                   jax.ShapeDtypeStruct((2, 1, D_FF), F32)],
        compiler_params=_cp(("parallel",)),
    )(dact, up, up, w, w, b, b)


def _expand_mat():
    r = lax.broadcasted_iota(jnp.int32, (128, D_INNER), 0)
    c = lax.broadcasted_iota(jnp.int32, (128, D_INNER), 1)
    return ((c >> 6) == r).astype(BF16)


def _reduce_mat():
    r = lax.broadcasted_iota(jnp.int32, (D_INNER, 128), 0)
    c = lax.broadcasted_iota(jnp.int32, (D_INNER, 128), 1)
    return ((r >> 6) == c).astype(BF16)


def _split(v, parts):
    out = []
    for _ in range(parts - 1):
        p = v.astype(BF16)
        out.append(p)
        v = v - p.astype(F32)
    out.append(v.astype(BF16))
    return out


def _sel_dot(v, sel, parts):
    acc = None
    for p in reversed(_split(v, parts)):
        t = _dot(p, sel)
        acc = t if acc is None else acc + t
    return acc


def _row8(v):
    return jnp.broadcast_to(v, (8, v.shape[1]))


def _tril():
    r = lax.broadcasted_iota(jnp.int32, (BLK, BLK), 0)
    c = lax.broadcasted_iota(jnp.int32, (BLK, BLK), 1)
    return r >= c


def _softplus(x):
    return jnp.maximum(x, 0.0) + jnp.log(1.0 + jnp.exp(-jnp.abs(x)))


def _ssd_common(dtraw_ref, dtb_ref, alog_ref):
    causal = _tril()
    e_mat = _expand_mat()
    a_neg = -jnp.exp(alog_ref[...])
    dt = _softplus(dtraw_ref[...] + dtb_ref[...])
    a_cs = _dot(causal.astype(F32), dt * a_neg, HI)
    a_cs_t = a_cs.T
    dt_x = _sel_dot(dt, e_mat, 3)
    acs_x = _sel_dot(a_cs, e_mat, 3)
    alast_x = acs_x[BLK - 1:BLK, :]
    ea_x = jnp.exp(acs_x)
    ds_x = jnp.exp(alast_x - acs_x)
    elast_x = jnp.exp(alast_x)
    return causal, e_mat, a_neg, dt, a_cs, a_cs_t, dt_x, ea_x, ds_x, elast_x


def _decay(a_cs, a_cs_t, h, causal):
    seg = a_cs[:, h:h + 1] - a_cs_t[h:h + 1, :]
    return jnp.where(causal, jnp.exp(jnp.where(causal, seg, 0.0)), 0.0)


def _ssd_fwd(xbc, proj, dt_bias, a_log, d_skip):
    s = xbc.shape[0]
    nc = s // BLK

    def body(xs_ref, b_ref, c_ref, dtraw_ref, dtb_ref, alog_ref, dskip_ref, y_ref, hp_ref, h_scr, xc16):
        @pl.when(pl.program_id(0) == 0)
        def _():
            h_scr[...] = jnp.zeros_like(h_scr)

        causal, e_mat, _, _, a_cs, a_cs_t, dt_x, ea_x, ds_x, elast_x = _ssd_common(dtraw_ref, dtb_ref, alog_ref)
        dskip_x = _sel_dot(_row8(dskip_ref[...]), e_mat, 3)[0:1]
        xs = xs_ref[...]
        xc = xs * dt_x
        xc16[...] = xc.astype(BF16)
        xcd = (xc * ds_x).astype(BF16)
        hp_ref[0] = h_scr[...]
        for g in range(4):
            gs = slice(g * 512, (g + 1) * 512)
            cg = c_ref[:, g * 128:(g + 1) * 128].astype(BF16)
            bg = b_ref[:, g * 128:(g + 1) * 128].astype(BF16)
            cb = _dot_nt(cg, bg)
            hg = h_scr[:, gs]
            yoff = _dot(cg, hg.astype(BF16)) * ea_x[:, gs]
            for j in range(8):
                h = g * 8 + j
                hsl = slice(h * 64, (h + 1) * 64)
                mm = (cb * _decay(a_cs, a_cs_t, h, causal)).astype(BF16)
                y_ref[:, hsl] = _dot(mm, xc16[:, hsl])
            y_ref[:, gs] += yoff + xs[:, gs] * dskip_x[:, gs]
            h_scr[:, gs] = hg * elast_x[:, gs] + _dot_tn(bg, xcd[:, gs])

    vec = pl.BlockSpec((1, 128), lambda c: (0, 0))
    return pl.pallas_call(
        body, name="ssd_fwd", grid=(nc,),
        in_specs=[pl.BlockSpec((BLK, D_INNER), lambda c: (c, 0)),
                  pl.BlockSpec((BLK, BC_DIM), lambda c: (c, D_INNER // BC_DIM)),
                  pl.BlockSpec((BLK, BC_DIM), lambda c: (c, D_INNER // BC_DIM + 1)),
                  pl.BlockSpec((BLK, 128), lambda c: (c, O_DT // 128)), vec, vec, vec],
        out_specs=[pl.BlockSpec((BLK, D_INNER), lambda c: (c, 0)),
                   pl.BlockSpec((1, 128, D_INNER), lambda c: (c, 0, 0))],
        out_shape=[jax.ShapeDtypeStruct((s, D_INNER), F32), jax.ShapeDtypeStruct((nc, 128, D_INNER), F32)],
        scratch_shapes=[pltpu.VMEM((128, D_INNER), F32), pltpu.VMEM((BLK, D_INNER), BF16)],
        compiler_params=_cp(("arbitrary",)),
    )(xbc, xbc, xbc, proj, dt_bias, a_log, d_skip)


def _ssd_bwd(xbc, proj, dt_bias, a_log, d_skip, hprev, dy, side=None):
    s = xbc.shape[0]
    nc = s // BLK

    def body(xs_ref, b_ref, c_ref, dtraw_ref, dtb_ref, alog_ref, dskip_ref, hp_ref, dy_ref,
             dxbc_ref, ddt_ref, dvec_ref, dh_scr, xc16, dy16, dxc_scr, dacs_r, tdiff):
        step = pl.program_id(0)
        dacs_r[...] = jnp.zeros_like(dacs_r)

        @pl.when(step == 0)
        def _():
            dh_scr[...] = jnp.zeros_like(dh_scr)
            dvec_ref[...] = jnp.zeros_like(dvec_ref)

        causal, e_mat, a_neg, dt, a_cs, a_cs_t, dt_x, ea_x, ds_x, elast_x = _ssd_common(dtraw_ref, dtb_ref, alog_ref)
        r_mat = _reduce_mat()
        lane = lax.broadcasted_iota(jnp.int32, (1, 128), 1)
        dskip_x = _sel_dot(_row8(dskip_ref[...]), e_mat, 3)[0:1]
        xs = xs_ref[...]
        dy = dy_ref[...]
        xc = xs * dt_x
        xcd = xc * ds_x
        xc16[...] = xc.astype(BF16)
        dy16[...] = dy.astype(BF16)
        dyea = dy * ea_x
        dh = dh_scr[...]
        hp = hp_ref[0]
        dalast_x = jnp.sum(dh * hp, axis=0, keepdims=True) * elast_x
        dacs = jnp.zeros((BLK, 128), F32)
        for g in range(4):
            gs = slice(g * 512, (g + 1) * 512)
            bsl = slice(g * 128, (g + 1) * 128)
            cg = c_ref[:, bsl].astype(BF16)
            bg = b_ref[:, bsl].astype(BF16)
            cb = _dot_nt(cg, bg)
            hg16 = hp[:, gs].astype(BF16)
            dhg16 = dh[:, gs].astype(BF16)
            raw = _dot(cg, hg16)
            draw16 = dyea[:, gs].astype(BF16)
            dcg = _dot_nt(draw16, hg16)
            dhp_g = _dot_tn(cg, draw16)
            dbg = _dot_nt(xcd[:, gs].astype(BF16), dhg16)
            dxcd = _dot(bg, dhg16)
            dcb = jnp.zeros((BLK, BLK), F32)
            for j in range(8):
                h = g * 8 + j
                hsl = slice(h * 64, (h + 1) * 64)
                decay = _decay(a_cs, a_cs_t, h, causal)
                m = cb * decay
                dm = _dot_nt(dy16[:, hsl], xc16[:, hsl])
                dxc_scr[:, hsl] = _dot_tn(m.astype(BF16), dy16[:, hsl])
                dcb = dcb + dm * decay
                dseg = dm * m
                oneh = jnp.where(lane == h, 1.0, 0.0)
                dacs = dacs + jnp.sum(dseg, axis=1, keepdims=True) * oneh
                dacs_r[h:h + 1, :] = jnp.sum(dseg, axis=0, keepdims=True)
            dcb16 = dcb.astype(BF16)
            dcg = dcg + _dot(dcb16, bg)
            dbg = dbg + _dot_tn(dcb16, cg)
            dxbc_ref[:, D_INNER + g * 128:D_INNER + (g + 1) * 128] = dbg
            dxbc_ref[:, D_INNER + BC_DIM + g * 128:D_INNER + BC_DIM + (g + 1) * 128] = dcg
            dxc_scr[:, gs] += dxcd * ds_x[:, gs]
            dh_scr[:, gs] = dh[:, gs] * elast_x[:, gs] + dhp_g
            tst = dxcd * xcd[:, gs]
            tdiff[:, gs] = dy[:, gs] * (raw * ea_x[:, gs]) - tst
            tdiff[BLK - 1:BLK, gs] += jnp.sum(tst, axis=0, keepdims=True)
        dxc = dxc_scr[...]
        row = lax.broadcasted_iota(jnp.int32, (BLK, D_INNER), 0)
        tfull = tdiff[...] + jnp.where(row == BLK - 1, dalast_x, 0.0)
        dacs = dacs + _sel_dot(tfull, r_mat, 2) - dacs_r[...].T
        da = _dot_tn(causal.astype(F32), dacs, HI)
        ddt = da * a_neg + _sel_dot(dxc * xs, r_mat, 2)
        lmask = lax.broadcasted_iota(jnp.int32, (BLK, 128), 1) < N_SSD_HEADS
        ddtraw = jnp.where(lmask, ddt * _sigmoid(dtraw_ref[...] + dtb_ref[...]), 0.0)
        ddt_ref[...] = ddtraw.astype(BF16)
        dxbc_ref[:, 0:D_INNER] = dy * dskip_x + dxc * dt_x
        dvec_ref[0:1, :] += jnp.sum(ddtraw, axis=0, keepdims=True)
        dvec_ref[1:2, :] += jnp.where(lane < N_SSD_HEADS, jnp.sum(da * dt, axis=0, keepdims=True) * a_neg, 0.0)
        dvec_ref[2:3, :] += _sel_dot(_row8(jnp.sum(dy * xs, axis=0, keepdims=True)), r_mat, 3)[0:1]

    rev = lambda c: nc - 1 - c
    vec = pl.BlockSpec((1, 128), lambda c: (0, 0))
    own, extra = _hosted(
        body, name="ssd_bwd", grid=(nc,),
        in_specs=[pl.BlockSpec((BLK, D_INNER), lambda c: (rev(c), 0)),
                  pl.BlockSpec((BLK, BC_DIM), lambda c: (rev(c), D_INNER // BC_DIM)),
                  pl.BlockSpec((BLK, BC_DIM), lambda c: (rev(c), D_INNER // BC_DIM + 1)),
                  pl.BlockSpec((BLK, 128), lambda c: (rev(c), O_DT // 128)), vec, vec, vec,
                  pl.BlockSpec((1, 128, D_INNER), lambda c: (rev(c), 0, 0)),
                  pl.BlockSpec((BLK, D_INNER), lambda c: (rev(c), 0))],
        out_specs=[pl.BlockSpec((BLK, XBC_DIM), lambda c: (rev(c), 0)),
                   pl.BlockSpec((BLK, 128), lambda c: (rev(c), 0)),
                   pl.BlockSpec((8, 128), lambda c: (0, 0))],
        out_shape=[jax.ShapeDtypeStruct((s, XBC_DIM), F32), jax.ShapeDtypeStruct((s, 128), BF16),
                   jax.ShapeDtypeStruct((8, 128), F32)],
        scratch_shapes=[pltpu.VMEM((128, D_INNER), F32), pltpu.VMEM((BLK, D_INNER), BF16),
                        pltpu.VMEM((BLK, D_INNER), BF16), pltpu.VMEM((BLK, D_INNER), F32),
                        pltpu.VMEM((128, BLK), F32), pltpu.VMEM((BLK, D_INNER), F32)],
        args=(xbc, xbc, xbc, proj, dt_bias, a_log, d_skip, hprev, dy), sem=("arbitrary",), side=side)
    return own if side is None else (own, extra)


GW = 512


def _gate_norm_fwd(y, proj, wn, *, tm=512):
    s = y.shape[0]
    tm = _tile(s, tm)

    def body(y_ref, z_ref, w_ref, o_ref):
        z = z_ref[...]
        y2 = y_ref[...] * (z * _sigmoid(z))
        r = lax.rsqrt(jnp.mean(y2 * y2, axis=-1, keepdims=True) + EPS)
        o_ref[...] = ((y2 * r) * w_ref[...]).astype(BF16)

    return pl.pallas_call(
        body, name="gate_norm_fwd", grid=(s // tm, 4),
        in_specs=[pl.BlockSpec((tm, GW), lambda i, g: (i, g)), pl.BlockSpec((tm, GW), lambda i, g: (i, O_Z // GW + g)),
                  pl.BlockSpec((1, GW), lambda i, g: (0, g))],
        out_specs=pl.BlockSpec((tm, GW), lambda i, g: (i, g)),
        out_shape=jax.ShapeDtypeStruct((s, D_INNER), BF16), compiler_params=_cp(("parallel", "parallel")),
    )(y, proj, wn)


def _gate_norm_bwd(dyn, y, proj, wn, *, tm=512):
    s = y.shape[0]
    tm = _tile(s, tm)

    def body(d_ref, y_ref, z_ref, w_ref, dy_ref, dz_ref, dw_ref):
        i = pl.program_id(1)
        z = z_ref[...]
        sg = _sigmoid(z)
        sz = z * sg
        yv = y_ref[...]
        y2 = yv * sz
        r = lax.rsqrt(jnp.mean(y2 * y2, axis=-1, keepdims=True) + EPS)
        xh = y2 * r
        dv = d_ref[...]
        g = dv * w_ref[...]
        dy2 = r * (g - xh * jnp.mean(g * xh, axis=-1, keepdims=True))
        dy_ref[...] = dy2 * sz
        dz_ref[...] = (dy2 * yv * _dsilu(z, sg)).astype(BF16)
        part = jnp.sum(dv * xh, axis=0, keepdims=True)

        @pl.when(i == 0)
        def _():
            dw_ref[...] = part

        @pl.when(i > 0)
        def _():
            dw_ref[...] += part

    blk = pl.BlockSpec((tm, GW), lambda g, i: (i, g))
    vec = pl.BlockSpec((1, GW), lambda g, i: (0, g))
    return pl.pallas_call(
        body, name="gate_norm_bwd", grid=(4, s // tm),
        in_specs=[blk, blk, pl.BlockSpec((tm, GW), lambda g, i: (i, O_Z // GW + g)), vec],
        out_specs=[blk, blk, vec],
        out_shape=[jax.ShapeDtypeStruct((s, D_INNER), F32), jax.ShapeDtypeStruct((s, D_INNER), BF16),
                   jax.ShapeDtypeStruct((1, D_INNER), F32)],
        compiler_params=_cp(("parallel", "arbitrary")),
    )(dyn, y, proj, wn)


def _merge_fwd(proj, b_gate, attn, ssd_out, *, tm=512):
    s = attn.shape[0]
    tm = _tile(s, tm)

    def body(ga_ref, gs_ref, ba_ref, bs_ref, a_ref, s_ref, o_ref):
        ga = _sigmoid(ga_ref[...] + ba_ref[...])
        gs = _sigmoid(gs_ref[...] + bs_ref[...])
        o_ref[...] = (ga * a_ref[...] + gs * s_ref[...]).astype(BF16)

    blk = pl.BlockSpec((tm, GW), lambda i, j: (i, j))
    return pl.pallas_call(
        body, name="merge_fwd", grid=(s // tm, 2),
        in_specs=[pl.BlockSpec((tm, GW), lambda i, j: (i, O_GA // GW + j)),
                  pl.BlockSpec((tm, GW), lambda i, j: (i, O_GS // GW + j)),
                  pl.BlockSpec((1, GW), lambda i, j: (0, j)), pl.BlockSpec((1, GW), lambda i, j: (0, 2 + j)), blk, blk],
        out_specs=blk, out_shape=jax.ShapeDtypeStruct((s, D_MODEL), BF16),
        compiler_params=_cp(("parallel", "parallel")),
    )(proj, proj, b_gate, b_gate, attn, ssd_out)


def _merge_bwd(dm, proj, b_gate, attn, ssd_out, *, tm=512):
    s = attn.shape[0]
    tm = _tile(s, tm)

    def body(d_ref, ga_ref, gs_ref, ba_ref, bs_ref, a_ref, s_ref, da_ref, ds_ref, dga_ref, dgs_ref, dba_ref, dbs_ref):
        i = pl.program_id(1)
        ga = _sigmoid(ga_ref[...] + ba_ref[...])
        gs = _sigmoid(gs_ref[...] + bs_ref[...])
        d = d_ref[...]
        da_ref[...] = (d * ga).astype(BF16)
        ds_ref[...] = (d * gs).astype(BF16)
        dga = d * a_ref[...] * (ga * (1.0 - ga))
        dgs = d * s_ref[...] * (gs * (1.0 - gs))
        dga_ref[...] = dga.astype(BF16)
        dgs_ref[...] = dgs.astype(BF16)
        pa = jnp.sum(dga, axis=0, keepdims=True)
        ps = jnp.sum(dgs, axis=0, keepdims=True)

        @pl.when(i == 0)
        def _():
            dba_ref[...] = pa
            dbs_ref[...] = ps

        @pl.when(i > 0)
        def _():
            dba_ref[...] += pa
            dbs_ref[...] += ps

    blk = pl.BlockSpec((tm, GW), lambda j, i: (i, j))
    vec = pl.BlockSpec((1, GW), lambda j, i: (0, j))
    sd = jax.ShapeDtypeStruct((s, D_MODEL), BF16)
    vd = jax.ShapeDtypeStruct((1, D_MODEL), F32)
    return pl.pallas_call(
        body, name="merge_bwd", grid=(2, s // tm),
        in_specs=[blk, pl.BlockSpec((tm, GW), lambda j, i: (i, O_GA // GW + j)),
                  pl.BlockSpec((tm, GW), lambda j, i: (i, O_GS // GW + j)),
                  vec, pl.BlockSpec((1, GW), lambda j, i: (0, 2 + j)), blk, blk],
        out_specs=[blk, blk, blk, blk, vec, vec], out_shape=[sd, sd, sd, sd, vd, vd],
        compiler_params=_cp(("parallel", "arbitrary")),
    )(dm, proj, proj, b_gate, b_gate, attn, ssd_out)


def _adamw_math(w, g, m, v):
    mn = ADAM_B1 * m + (1.0 - ADAM_B1) * g
    vn = ADAM_B2 * v + (1.0 - ADAM_B2) * (g * g)
    m_hat = mn / (1.0 - ADAM_B1 ** ADAM_STEP)
    v_hat = vn / (1.0 - ADAM_B2 ** ADAM_STEP)
    return -ADAM_LR * (m_hat / (jnp.sqrt(v_hat) + ADAM_EPS) + ADAM_WD * w), mn, vn


def _adamw_many(ws, gs, ms, vs):
    n = len(ws)

    def body(*refs):
        outs = refs[4 * n:]
        for i in range(n):
            res = _adamw_math(*[refs[q * n + i][...] for q in range(4)])
            for q in range(3):
                outs[q * n + i][...] = res[q]

    return pl.pallas_call(body, name="adamw_small", out_shape=[jax.ShapeDtypeStruct(w.shape, F32) for w in ws] * 3,
                          compiler_params=_cp())(*ws, *gs, *ms, *vs)


def _adamw(w, g, m, v, *, name, tm=128):
    r, c = w.shape
    tm = r if (r < tm or r % tm) else tm

    def body(w_ref, g_ref, m_ref, v_ref, d_ref, nm_ref, nv_ref):
        d_ref[...], nm_ref[...], nv_ref[...] = _adamw_math(w_ref[...], g_ref[...], m_ref[...], v_ref[...])

    blk = pl.BlockSpec((tm, c), lambda i: (i, 0))
    sd = jax.ShapeDtypeStruct((r, c), F32)
    return pl.pallas_call(
        body, name=name, grid=(r // tm,), in_specs=[blk] * 4, out_specs=[blk] * 3, out_shape=[sd] * 3,
        compiler_params=_cp(("parallel",)),
    )(w, g, m, v)


ANY = pl.BlockSpec(memory_space=pl.ANY)
N_CHIPS = 4


def _chip_of(k, x, y):
    return (x ^ (k >> 1), y ^ (k & 1))


def _all_gather_small(shard):
    r, c = shard.shape
    hr = r // 2

    def body(sh_ref, out_ref, send_sems, recv_sems, local_sem):
        x, y, cc = lax.axis_index("x"), lax.axis_index("y"), lax.axis_index("c")

        def half(px, py, pc):
            return out_ref.at[2 * px + py, pl.ds(pc * hr, hr), :]

        def copy(k, px, py, pc, to, src=None):
            return pltpu.make_async_remote_copy(
                src_ref=half(px, py, pc) if src is None else src, dst_ref=half(px, py, pc),
                send_sem=send_sems.at[k], recv_sem=recv_sems.at[k], device_id=to, device_id_type=MESH)

        mine = pltpu.make_async_copy(sh_ref, out_ref.at[2 * x + y], local_sem)
        mine.start()
        chips = [_chip_of(k, x, y) for k in (1, 2, 3)]
        first = [copy(j, x, y, cc, (*chip, cc), src=sh_ref.at[pl.ds(cc * hr, hr), :]) for j, chip in enumerate(chips)]
        for cp in first:
            cp.start()
        passed = [copy(3 + j, *chip, cc, (x, y, 1 - cc)) for j, chip in enumerate(chips)]
        for j, chip in enumerate(chips):
            copy(j, *chip, cc, (x, y, cc)).wait_recv()
            passed[j].start()
        for j, chip in enumerate(chips):
            copy(3 + j, *chip, 1 - cc, (x, y, cc)).wait_recv()
        for cp in first + passed:
            cp.wait_send()
        mine.wait()

    return pl.pallas_call(
        body, name="all_gather_small", in_specs=[ANY], out_specs=ANY,
        out_shape=jax.ShapeDtypeStruct((N_CHIPS, r, c), shard.dtype),
        scratch_shapes=[pltpu.SemaphoreType.DMA((6,)), pltpu.SemaphoreType.DMA((6,)), pltpu.SemaphoreType.DMA],
    )(shard)


def _cast_bf16(a, *, name, tm=512):
    n, r, c = a.shape
    tm = _tile(r, tm) if r % 128 == 0 else r

    def body(a_ref, o_ref):
        o_ref[...] = a_ref[...].astype(BF16)

    blk = pl.BlockSpec((1, tm, c), lambda i, j: (i, j, 0))
    return pl.pallas_call(body, name=name, grid=(n, r // tm), in_specs=[blk], out_specs=blk,
                          out_shape=jax.ShapeDtypeStruct(a.shape, BF16), compiler_params=_cp(("parallel", "parallel")))(a)


def _pair_exchange(g16, hr):
    n, r, c = g16.shape

    def body(g_ref, out_ref, send_sem, recv_sem):
        x, y, cc = lax.axis_index("x"), lax.axis_index("y"), lax.axis_index("c")
        cp = pltpu.make_async_remote_copy(
            src_ref=g_ref.at[:, pl.ds((1 - cc) * hr, hr), :], dst_ref=out_ref, send_sem=send_sem, recv_sem=recv_sem,
            device_id=(x, y, 1 - cc), device_id_type=MESH)
        cp.start()
        cp.wait()

    return pl.pallas_call(
        body, name="grad_pair_exchange", in_specs=[ANY], out_specs=ANY,
        out_shape=jax.ShapeDtypeStruct((n, hr, c), g16.dtype),
        scratch_shapes=[pltpu.SemaphoreType.DMA, pltpu.SemaphoreType.DMA],
    )(g16)


def _pair_add(g, recv, half_idx, hr, *, tm=384):
    n, r, c = g.shape
    nt = hr // tm

    def body(hi_ref, g_ref, r_ref, o32_ref, o16_ref):
        v = g_ref[...] + r_ref[...].astype(F32)
        o32_ref[...] = v
        o16_ref[...] = v.astype(BF16)

    gs = pltpu.PrefetchScalarGridSpec(
        num_scalar_prefetch=1, grid=(n, nt),
        in_specs=[pl.BlockSpec((1, tm, c), lambda i, j, hi: (i, hi[0] * nt + j, 0)),
                  pl.BlockSpec((1, tm, c), lambda i, j, hi: (i, j, 0))],
        out_specs=[pl.BlockSpec((1, tm, c), lambda i, j, hi: (i, j, 0))] * 2)
    return pl.pallas_call(
        body, name="grad_pair_add", grid_spec=gs,
        out_shape=[jax.ShapeDtypeStruct((n, hr, c), F32), jax.ShapeDtypeStruct((n, hr, c), BF16)],
        compiler_params=_cp(("parallel", "parallel")),
    )(half_idx, g, recv)


def _chip_exchange(p16):
    n, hr, c = p16.shape

    def body(p_ref, out_ref, send_sems, recv_sems):
        x, y, cc = lax.axis_index("x"), lax.axis_index("y"), lax.axis_index("c")
        cps = []
        for j, k in enumerate((1, 2, 3)):
            px, py = _chip_of(k, x, y)
            cps.append(pltpu.make_async_remote_copy(
                src_ref=p_ref.at[2 * px + py], dst_ref=out_ref.at[j], send_sem=send_sems.at[j], recv_sem=recv_sems.at[j],
                device_id=(px, py, cc), device_id_type=MESH))
        for cp in cps:
            cp.start()
        for cp in cps:
            cp.wait()

    return pl.pallas_call(
        body, name="grad_chip_exchange", in_specs=[ANY], out_specs=ANY,
        out_shape=jax.ShapeDtypeStruct((3, hr, c), p16.dtype),
        scratch_shapes=[pltpu.SemaphoreType.DMA((3,)), pltpu.SemaphoreType.DMA((3,))],
    )(p16)


def _chip_add(p32, recv, chip_idx, *, tm=384):
    n, hr, c = p32.shape

    def body(ci_ref, p_ref, r_ref, o_ref):
        o_ref[...] = ((p_ref[0] + r_ref[0].astype(F32)) + r_ref[1].astype(F32)) + r_ref[2].astype(F32)

    gs = pltpu.PrefetchScalarGridSpec(
        num_scalar_prefetch=1, grid=(hr // tm,),
        in_specs=[pl.BlockSpec((1, tm, c), lambda j, ci: (ci[0], j, 0)), pl.BlockSpec((3, tm, c), lambda j, ci: (0, j, 0))],
        out_specs=pl.BlockSpec((tm, c), lambda j, ci: (j, 0)))
    return pl.pallas_call(
        body, name="grad_chip_add", grid_spec=gs, out_shape=jax.ShapeDtypeStruct((hr, c), F32),
        compiler_params=_cp(("parallel",)),
    )(chip_idx, p32, recv)


def _pair_gather(f):
    hr, c = f.shape

    def body(f_ref, out_ref, send_sem, recv_sem, local_sem):
        x, y, cc = lax.axis_index("x"), lax.axis_index("y"), lax.axis_index("c")
        mine = pltpu.make_async_copy(f_ref, out_ref.at[pl.ds(cc * hr, hr), :], local_sem)
        mine.start()
        cp = pltpu.make_async_remote_copy(
            src_ref=f_ref, dst_ref=out_ref.at[pl.ds(cc * hr, hr), :], send_sem=send_sem, recv_sem=recv_sem,
            device_id=(x, y, 1 - cc), device_id_type=MESH)
        cp.start()
        cp.wait()
        mine.wait()

    return pl.pallas_call(
        body, name="grad_pair_gather", in_specs=[ANY], out_specs=ANY,
        out_shape=jax.ShapeDtypeStruct((2 * hr, c), f.dtype),
        scratch_shapes=[pltpu.SemaphoreType.DMA, pltpu.SemaphoreType.DMA, pltpu.SemaphoreType.DMA],
    )(f)


def _all_reduce_small(buf):
    r, c = buf.shape

    def body(b_ref, out_ref, gat, send_sems, recv_sems):
        x, y, cc = lax.axis_index("x"), lax.axis_index("y"), lax.axis_index("c")
        me = 4 * x + 2 * y + cc
        gat[me] = b_ref[...]
        cps = []
        for k in range(1, 8):
            px, py, pc = x ^ (k >> 2), y ^ ((k >> 1) & 1), cc ^ (k & 1)
            cps.append(pltpu.make_async_remote_copy(
                src_ref=b_ref, dst_ref=gat.at[me], send_sem=send_sems.at[k - 1], recv_sem=recv_sems.at[k - 1],
                device_id=(px, py, pc), device_id_type=MESH))
        for cp in cps:
            cp.start()
        for cp in cps:
            cp.wait()
        acc = gat[0]
        for d in range(1, 8):
            acc = acc + gat[d]
        out_ref[...] = acc

    vm = pl.BlockSpec(memory_space=pltpu.VMEM)
    return pl.pallas_call(
        body, name="all_reduce_small", in_specs=[vm], out_specs=vm, out_shape=jax.ShapeDtypeStruct((r, c), F32),
        scratch_shapes=[pltpu.VMEM((8, r, c), F32), pltpu.SemaphoreType.DMA((7,)), pltpu.SemaphoreType.DMA((7,))],
        compiler_params=pltpu.CompilerParams(vmem_limit_bytes=VMEM_LIMIT),
    )(buf)


def _pipe(fn, ins, outs, tr):
    shape = ins[0].shape
    lead, (r, c) = shape[:-2], shape[-2:]
    assert len(lead) <= 1 and r % tr == 0
    nr = r // tr
    n = nr * (lead[0] if lead else 1)
    ni, no = len(ins), len(outs)

    def blk(ref, step):
        rows = pl.ds((step % nr) * tr, tr)
        return ref.at[step // nr, rows, :] if lead else ref.at[rows, :]

    def scoped(*bufs):
        ibufs, obufs, isem, osem = bufs[:ni], bufs[ni:ni + no], bufs[-2], bufs[-1]

        def in_copy(q, step, slot):
            return pltpu.make_async_copy(blk(ins[q], step), ibufs[q].at[slot], isem.at[q, slot])

        def out_copy(q, step, slot):
            return pltpu.make_async_copy(obufs[q].at[slot], blk(outs[q], step), osem.at[q, slot])

        for step in range(min(nbuf - 1, n)):
            for q in range(ni):
                in_copy(q, step, step % nbuf).start()
        for step in range(n):
            slot = step % nbuf
            if step + nbuf - 1 < n:
                for q in range(ni):
                    in_copy(q, step + nbuf - 1, (step + nbuf - 1) % nbuf).start()
            for q in range(ni):
                in_copy(q, step, slot).wait()
            if step >= nbuf:
                for q in range(no):
                    out_copy(q, step - nbuf, slot).wait()
            res = fn(*[ibufs[q][slot] for q in range(ni)])
            for q in range(no):
                obufs[q][slot] = res[q].astype(obufs[q].dtype)
                out_copy(q, step, slot).start()
        for step in range(max(n - nbuf, 0), n):
            for q in range(no):
                out_copy(q, step, step % nbuf).wait()

    assert n <= 8
    nbuf = min(n, 4)
    pl.run_scoped(scoped, *[pltpu.VMEM((nbuf, tr, c), q.dtype) for q in ins], *[pltpu.VMEM((nbuf, tr, c), q.dtype) for q in outs],
                  pltpu.SemaphoreType.DMA((ni, nbuf)), pltpu.SemaphoreType.DMA((no, nbuf)))


W_IN_PAD = 2304
BIG = ("w_in", "w_attn_o", "w_ssd_o", "w_out", "w_up", "w_down")
BIG_SHAPE = dict(w_in=(D_MODEL, W_IN_PAD), w_attn_o=(Q_DIM // 4, D_MODEL), w_ssd_o=(D_INNER // 4, D_MODEL),
                 w_out=(D_MODEL // 4, D_MODEL), w_up=(D_MODEL, 2 * D_FF // 4), w_down=(D_FF // 4, D_MODEL))
BIG_TR = dict(w_in=128, w_attn_o=128, w_ssd_o=128, w_out=128, w_up=128, w_down=176)
X_FIRST = dict(w_in=True, w_attn_o=True, w_ssd_o=False, w_out=True, w_up=False, w_down=False)


def _neighbours(x, y, x_first):
    xn, yn = (1 - x, y), (x, 1 - y)
    n1, n2 = (xn, yn) if x_first else (yn, xn)
    slot = lambda ch: 2 * ch[0] + ch[1]
    return n1, n2, slot(n1), slot(n2), slot((1 - x, 1 - y))


def _gather_big(shards):
    nt = len(BIG)

    def body(*refs):
        sh, out = refs[:nt], refs[nt:2 * nt]
        send_sems, recv_sems = refs[2 * nt:]
        x, y, cc = lax.axis_index("x"), lax.axis_index("y"), lax.axis_index("c")
        me = 2 * x + y
        sib = (x, y, 1 - cc)
        for t, n in enumerate(BIG):
            _pipe(lambda v: (v,), [sh[t]], [out[t].at[me]], BIG_TR[n])

        def copy(t, k, slot, pc, to):
            hr = BIG_SHAPE[BIG[t]][0] // 2
            ref = out[t].at[slot, pl.ds(pc * hr, hr), :]
            return pltpu.make_async_remote_copy(src_ref=ref, dst_ref=ref, send_sem=send_sems.at[6 * t + k],
                                                recv_sem=recv_sems.at[6 * t + k], device_id=to, device_id_type=MESH)

        started = []

        def start(cp):
            cp.start()
            started.append(cp)

        geo = [_neighbours(x, y, X_FIRST[n]) for n in BIG]
        for t in range(nt):
            n1, n2, _, _, _ = geo[t]
            start(copy(t, 0, me, cc, (*n1, cc)))
            start(copy(t, 1, me, cc, (*n2, cc)))
        for t in range(nt):
            n1, n2, s1, s2, sd = geo[t]
            copy(t, 0, s1, cc, sib).wait_recv()
            start(copy(t, 2, s1, cc, (*n2, cc)))
            start(copy(t, 3, s1, cc, sib))
            copy(t, 1, s2, cc, sib).wait_recv()
            start(copy(t, 4, s2, cc, sib))
        for t in range(nt):
            _, _, s1, s2, sd = geo[t]
            copy(t, 2, sd, cc, sib).wait_recv()
            start(copy(t, 5, sd, cc, sib))
        for t in range(nt):
            _, _, s1, s2, sd = geo[t]
            copy(t, 3, s1, 1 - cc, sib).wait_recv()
            copy(t, 4, s2, 1 - cc, sib).wait_recv()
            copy(t, 5, sd, 1 - cc, sib).wait_recv()
        for cp in started:
            cp.wait_send()

    return pl.pallas_call(
        body, name="gather_big", in_specs=[ANY] * nt, out_specs=[ANY] * nt,
        out_shape=[jax.ShapeDtypeStruct((N_CHIPS, *BIG_SHAPE[n]), BF16) for n in BIG],
        scratch_shapes=[pltpu.SemaphoreType.DMA((6 * nt,)), pltpu.SemaphoreType.DMA((6 * nt,))],
        compiler_params=pltpu.CompilerParams(vmem_limit_bytes=VMEM_LIMIT),
    )(*shards)


def _reduce_big(grads):
    nt = len(BIG)
    nw = 7

    def body(*refs):
        g = refs[:nt]
        fin = refs[nt:2 * nt]
        work = refs[2 * nt:2 * nt + nw * nt]
        send_sems, recv_sems = refs[2 * nt + nw * nt:]
        x, y, cc = lax.axis_index("x"), lax.axis_index("y"), lax.axis_index("c")
        me = 2 * x + y
        sib = (x, y, 1 - cc)
        started = []

        def rcopy(t, k, src, dst, to):
            cp = pltpu.make_async_remote_copy(src_ref=src, dst_ref=dst, send_sem=send_sems.at[5 * t + k],
                                              recv_sem=recv_sems.at[5 * t + k], device_id=to, device_id_type=MESH)
            return cp

        def start(cp):
            cp.start()
            started.append(cp)

        geo = [_neighbours(x, y, X_FIRST[n]) for n in BIG]
        hrs = [BIG_SHAPE[n][0] // 2 for n in BIG]
        wk = lambda t: work[nw * t:nw * (t + 1)]
        one = lambda ref, slot: ref.at[pl.ds(slot, 1)]
        for t in range(nt):
            recv_a = wk(t)[0]
            start(rcopy(t, 0, g[t].at[:, pl.ds((1 - cc) * hrs[t], hrs[t]), :], recv_a, sib))
        for t, n in enumerate(BIG):
            recv_a, p32, p16, r1, qme, qs2, r2 = wk(t)
            n1, n2, s1, s2, sd = geo[t]
            rcopy(t, 0, recv_a, recv_a, sib).wait_recv()
            _pipe(lambda a, b: (a + b, a + b), [g[t].at[:, pl.ds(cc * hrs[t], hrs[t]), :], recv_a], [p32, p16], BIG_TR[n])
            start(rcopy(t, 1, one(p16, s1), one(r1, 0), (*n1, cc)))
            start(rcopy(t, 2, one(p16, sd), one(r1, 1), (*n1, cc)))
        for t, n in enumerate(BIG):
            recv_a, p32, p16, r1, qme, qs2, r2 = wk(t)
            n1, n2, s1, s2, sd = geo[t]
            rcopy(t, 1, one(r1, 0), one(r1, 0), sib).wait_recv()
            rcopy(t, 2, one(r1, 1), one(r1, 1), sib).wait_recv()
            _pipe(lambda a, b: (a + b.astype(F32),), [one(p32, s2), one(r1, 1)], [qs2], BIG_TR[n])
            start(rcopy(t, 3, qs2, r2, (*n2, cc)))
            _pipe(lambda a, b: (a + b.astype(F32),), [one(p32, me), one(r1, 0)], [qme], BIG_TR[n])
        for t, n in enumerate(BIG):
            recv_a, p32, p16, r1, qme, qs2, r2 = wk(t)
            rcopy(t, 3, r2, r2, sib).wait_recv()
            mine = fin[t].at[pl.ds(cc * hrs[t], hrs[t]), :]
            _pipe(lambda a, b: (a + b.astype(F32),), [qme.at[0], r2.at[0]], [mine], BIG_TR[n])
            start(rcopy(t, 4, mine, mine, sib))
        for t in range(nt):
            other = fin[t].at[pl.ds((1 - cc) * hrs[t], hrs[t]), :]
            rcopy(t, 4, other, other, sib).wait_recv()
        for cp in started:
            cp.wait_send()

    outs = [jax.ShapeDtypeStruct(BIG_SHAPE[n], F32) for n in BIG]
    for n in BIG:
        r, c = BIG_SHAPE[n]
        hr = r // 2
        outs += [jax.ShapeDtypeStruct((4, hr, c), F32), jax.ShapeDtypeStruct((4, hr, c), F32),
                 jax.ShapeDtypeStruct((4, hr, c), BF16), jax.ShapeDtypeStruct((2, hr, c), BF16),
                 jax.ShapeDtypeStruct((1, hr, c), F32), jax.ShapeDtypeStruct((1, hr, c), BF16),
                 jax.ShapeDtypeStruct((1, hr, c), BF16)]
    res = pl.pallas_call(
        body, name="reduce_big", in_specs=[ANY] * nt, out_specs=[ANY] * len(outs), out_shape=outs,
        scratch_shapes=[pltpu.SemaphoreType.DMA((5 * nt,)), pltpu.SemaphoreType.DMA((5 * nt,))],
        compiler_params=pltpu.CompilerParams(vmem_limit_bytes=VMEM_LIMIT),
    )(*grads)
    return res[:nt]


WHOLE_X_FIRST = dict(w_ssd_o=True, w_out=False, w_attn_o=False)


def _quarters(names):
    out = []
    for i, n in enumerate(names):
        if n in WHOLE_X_FIRST:
            h = BIG_SHAPE[n][0] // 2
            out.append((i, WHOLE_X_FIRST[n], 0, h, 128))
        else:
            q = BIG_SHAPE[n][0] // 4
            tr = 128 if q % 128 == 0 else q
            out += [(i, True, 0, q, tr), (i, False, q, q, tr)]
    return out


class _GatherJob:
    def __init__(self, names, shards, at=None):
        self.names = names
        self.at = at
        self.inputs = list(shards)
        self.out_shapes = [jax.ShapeDtypeStruct((N_CHIPS, *BIG_SHAPE[n]), BF16) for n in names]
        self.ent = _quarters(names)
        self.scratch = [pltpu.SemaphoreType.DMA((6 * len(self.ent),)), pltpu.SemaphoreType.DMA((6 * len(self.ent),))]

    def phases(self, sh, out, scr):
        send_sems, recv_sems = scr
        names, ent = self.names, self.ent
        x, y, cc = lax.axis_index("x"), lax.axis_index("y"), lax.axis_index("c")
        me = 2 * x + y
        sib = (x, y, 1 - cc)
        geo = [_neighbours(x, y, e[1]) for e in ent]
        started = []

        def copy(i, k, slot, pc, to):
            arr, _, roff, rows, _ = ent[i]
            hr = BIG_SHAPE[names[arr]][0] // 2
            ref = out[arr].at[slot, pl.ds(pc * hr + roff, rows), :]
            return pltpu.make_async_remote_copy(src_ref=ref, dst_ref=ref, send_sem=send_sems.at[6 * i + k],
                                                recv_sem=recv_sems.at[6 * i + k], device_id=to, device_id_type=MESH)

        def start(*a):
            copy(*a).start()
            started.append(a)

        def p0():
            for t, n in enumerate(names):
                _pipe(lambda v: (v,), [sh[t]], [out[t].at[me]], BIG_TR[n])
            for i in range(len(ent)):
                n1, n2, _, _, _ = geo[i]
                start(i, 0, me, cc, (*n1, cc))
                start(i, 1, me, cc, (*n2, cc))

        def p1():
            for i in range(len(ent)):
                n1, n2, s1, s2, sd = geo[i]
                copy(i, 0, s1, cc, sib).wait_recv()
                start(i, 2, s1, cc, (*n2, cc))
                start(i, 3, s1, cc, sib)
                copy(i, 1, s2, cc, sib).wait_recv()
                start(i, 4, s2, cc, sib)

        def p2():
            for i in range(len(ent)):
                sd = geo[i][4]
                copy(i, 2, sd, cc, sib).wait_recv()
                start(i, 5, sd, cc, sib)

        def p3():
            for i in range(len(ent)):
                _, _, s1, s2, sd = geo[i]
                copy(i, 3, s1, 1 - cc, sib).wait_recv()
                copy(i, 4, s2, 1 - cc, sib).wait_recv()
                copy(i, 5, sd, 1 - cc, sib).wait_recv()
            for a in started:
                copy(*a).wait_send()

        return [p0, p1, p2, p3]


class _ReduceJob:
    NW = 7

    def __init__(self, names, grads, at=None):
        self.names = names
        self.at = at
        self.inputs = list(grads)
        self.ent = _quarters(names)
        self.out_shapes = [jax.ShapeDtypeStruct(BIG_SHAPE[n], F32) for n in names]
        for arr, _, _, rows, _ in self.ent:
            c = BIG_SHAPE[names[arr]][1]
            self.out_shapes += [jax.ShapeDtypeStruct((4, rows, c), F32), jax.ShapeDtypeStruct((4, rows, c), F32),
                                jax.ShapeDtypeStruct((4, rows, c), BF16), jax.ShapeDtypeStruct((2, rows, c), BF16),
                                jax.ShapeDtypeStruct((1, rows, c), F32), jax.ShapeDtypeStruct((1, rows, c), BF16),
                                jax.ShapeDtypeStruct((1, rows, c), BF16)]
        self.scratch = [pltpu.SemaphoreType.DMA((5 * len(self.ent),)), pltpu.SemaphoreType.DMA((5 * len(self.ent),))]

    def phases(self, g, outs, scr):
        send_sems, recv_sems = scr
        names, ent, nw = self.names, self.ent, self.NW
        nt = len(names)
        fin, work = outs[:nt], outs[nt:]
        x, y, cc = lax.axis_index("x"), lax.axis_index("y"), lax.axis_index("c")
        me = 2 * x + y
        sib = (x, y, 1 - cc)
        geo = [_neighbours(x, y, e[1]) for e in ent]
        started = []
        wk = lambda i: work[nw * i:nw * (i + 1)]
        one = lambda ref, slot: ref.at[pl.ds(slot, 1)]

        def rows_of(i, pc):
            arr, _, roff, rows, _ = ent[i]
            return pl.ds(pc * (BIG_SHAPE[names[arr]][0] // 2) + roff, rows)

        def rcopy(i, k, src, dst, to):
            return pltpu.make_async_remote_copy(src_ref=src, dst_ref=dst, send_sem=send_sems.at[5 * i + k],
                                                recv_sem=recv_sems.at[5 * i + k], device_id=to, device_id_type=MESH)

        def start(make):
            make().start()
            started.append(make)

        def p0():
            for i, e in enumerate(ent):
                start(lambda i=i, e=e: rcopy(i, 0, g[e[0]].at[:, rows_of(i, 1 - cc), :], wk(i)[0], sib))

        def p1():
            for i, e in enumerate(ent):
                recv_a, p32, p16, r1 = wk(i)[:4]
                n1, n2, s1, s2, sd = geo[i]
                rcopy(i, 0, recv_a, recv_a, sib).wait_recv()
                _pipe(lambda a, b: (a + b, a + b), [g[e[0]].at[:, rows_of(i, cc), :], recv_a], [p32, p16], e[4])
                start(lambda i=i, s1=s1, n1=n1: rcopy(i, 1, one(wk(i)[2], s1), one(wk(i)[3], 0), (*n1, cc)))
                start(lambda i=i, sd=sd, n1=n1: rcopy(i, 2, one(wk(i)[2], sd), one(wk(i)[3], 1), (*n1, cc)))

        def p2():
            for i, e in enumerate(ent):
                _, p32, _, r1, qme, qs2, r2 = wk(i)
                n1, n2, s1, s2, sd = geo[i]
                rcopy(i, 1, one(r1, 0), one(r1, 0), sib).wait_recv()
                rcopy(i, 2, one(r1, 1), one(r1, 1), sib).wait_recv()
                _pipe(lambda a, b, c, d: (a + b.astype(F32), c + d.astype(F32)),
                      [one(p32, s2), one(r1, 1), one(p32, me), one(r1, 0)], [qs2, qme], e[4])
                start(lambda i=i, n2=n2: rcopy(i, 3, wk(i)[5], wk(i)[6], (*n2, cc)))

        def p3():
            for i, e in enumerate(ent):
                qme, r2 = wk(i)[4], wk(i)[6]
                rcopy(i, 3, r2, r2, sib).wait_recv()
                mine = fin[e[0]].at[rows_of(i, cc), :]
                _pipe(lambda a, b: (a + b.astype(F32),), [qme.at[0], r2.at[0]], [mine], e[4])
                start(lambda i=i, e=e: rcopy(i, 4, fin[e[0]].at[rows_of(i, cc), :], fin[e[0]].at[rows_of(i, cc), :], sib))

        def p4():
            for i, e in enumerate(ent):
                other = fin[e[0]].at[rows_of(i, 1 - cc), :]
                rcopy(i, 4, other, other, sib).wait_recv()
            for make in started:
                make().wait_send()

        return [p0, p1, p2, p3, p4]


def _run_job(job, name):
    ni, no = len(job.inputs), len(job.out_shapes)

    def body(*refs):
        for ph in job.phases(refs[:ni], refs[ni:ni + no], refs[ni + no:]):
            ph()

    return pl.pallas_call(
        body, name=name, in_specs=[ANY] * ni, out_specs=[ANY] * no, out_shape=job.out_shapes, scratch_shapes=job.scratch,
        compiler_params=pltpu.CompilerParams(vmem_limit_bytes=VMEM_LIMIT),
    )(*job.inputs)


def _hosted(body, *, name, grid, in_specs, out_specs, out_shape, scratch_shapes, args, sem, side=None):
    if side is None:
        return pl.pallas_call(body, name=name, grid=grid, in_specs=in_specs, out_specs=out_specs, out_shape=out_shape,
                              scratch_shapes=scratch_shapes, compiler_params=_cp(sem))(*args), None
    job = side
    ni, no, ns = len(in_specs), len(out_specs), len(scratch_shapes)
    ji, jo = len(job.inputs), len(job.out_shapes)
    n_steps = 1
    for extent in grid:
        n_steps *= extent

    def wrapped(*refs):
        own_in, refs = refs[:ni], refs[ni:]
        job_in, refs = refs[:ji], refs[ji:]
        own_out, refs = refs[:no], refs[no:]
        job_out, refs = refs[:jo], refs[jo:]
        own_scr, job_scr = refs[:ns], refs[ns:]
        step = 0
        for d, extent in enumerate(grid):
            step = step * extent + pl.program_id(d)
        phases = job.phases(job_in, job_out, job_scr)
        steps = [min(int(f * n_steps), n_steps - 1) for f in job.at] + [n_steps - 1]
        assert len(steps) == len(phases) and steps == sorted(steps)
        for at, ph in zip(steps, phases):
            pl.when(step == at)(ph)
        body(*own_in, *own_out, *own_scr)

    res = pl.pallas_call(
        wrapped, name=name, grid=grid, in_specs=list(in_specs) + [ANY] * ji, out_specs=list(out_specs) + [ANY] * jo,
        out_shape=list(out_shape) + list(job.out_shapes), scratch_shapes=list(scratch_shapes) + list(job.scratch),
        compiler_params=_cp(("arbitrary",) * len(grid)),
    )(*args, *job.inputs)
    return res[:no], res[no:]


def _proj_dw(xn, dproj_sh, *, tm=512, tk=1024):
    s, d = xn.shape
    tk = _tile(s, tk)
    nk = s // tk

    def body(a_ref, b_ref, o_ref, acc):
        kk = pl.program_id(2)
        part = _dot_tn(a_ref[...], b_ref[0])

        @pl.when(kk == 0)
        def _():
            acc[...] = part

        @pl.when(kk > 0)
        def _():
            acc[...] += part

        @pl.when(kk == nk - 1)
        def _():
            o_ref[0] = acc[...]

    return pl.pallas_call(
        body, name="proj_dw", grid=(N_CHIPS, d // tm, nk),
        in_specs=[pl.BlockSpec((tk, tm), lambda j, i, q: (q, i)), pl.BlockSpec((1, tk, W_IN_PAD), lambda j, i, q: (j, q, 0))],
        out_specs=pl.BlockSpec((1, tm, W_IN_PAD), lambda j, i, q: (j, i, 0)),
        out_shape=jax.ShapeDtypeStruct((N_CHIPS, d, W_IN_PAD), F32), scratch_shapes=[pltpu.VMEM((tm, W_IN_PAD), F32)],
        compiler_params=_cp(("parallel", "parallel", "arbitrary")),
    )(xn, dproj_sh)


def _proj_dx(dproj_sh, w_sh, *, tm=1024, side=None):
    s = dproj_sh.shape[1]
    d = w_sh.shape[1]
    tm = _tile(s, tm)

    def body(a_ref, b_ref, o_ref, acc):
        kk = pl.program_id(1)
        part = _dot_nt(a_ref[0], b_ref[0])

        @pl.when(kk == 0)
        def _():
            acc[...] = part

        @pl.when(kk > 0)
        def _():
            acc[...] += part

        @pl.when(kk == N_CHIPS - 1)
        def _():
            o_ref[...] = acc[...]

    own, extra = _hosted(
        body, name="proj_dx", grid=(s // tm, N_CHIPS),
        in_specs=[pl.BlockSpec((1, tm, W_IN_PAD), lambda i, q: (q, i, 0)), pl.BlockSpec((1, d, W_IN_PAD), lambda i, q: (q, 0, 0))],
        out_specs=[pl.BlockSpec((tm, d), lambda i, q: (i, 0))],
        out_shape=[jax.ShapeDtypeStruct((s, d), F32)], scratch_shapes=[pltpu.VMEM((tm, d), F32)],
        args=(dproj_sh, w_sh), sem=("parallel", "arbitrary"), side=side)
    return own[0] if side is None else (own[0], extra)


def _up_dx(dup, w_sh, *, tm=1024):
    s = dup.shape[1]
    d, wsh = w_sh.shape[1:]
    tm = _tile(s, tm)

    def body(a_ref, b_ref, o_ref, acc):
        kk = pl.program_id(1)
        part = _dot_nt(a_ref[0], b_ref[0])

        @pl.when(kk == 0)
        def _():
            acc[...] = part

        @pl.when(kk > 0)
        def _():
            acc[...] += part

        @pl.when(kk == N_CHIPS - 1)
        def _():
            o_ref[...] = acc[...]

    return pl.pallas_call(
        body, name="up_dx", grid=(s // tm, N_CHIPS),
        in_specs=[pl.BlockSpec((1, tm, wsh), lambda i, q: (q >> 1, i, q & 1)), pl.BlockSpec((1, d, wsh), lambda i, q: (q, 0, 0))],
        out_specs=pl.BlockSpec((tm, d), lambda i, q: (i, 0)),
        out_shape=jax.ShapeDtypeStruct((s, d), F32), scratch_shapes=[pltpu.VMEM((tm, d), F32)],
        compiler_params=_cp(("parallel", "arbitrary")),
    )(dup, w_sh)


def _up_dw(hn, dup, *, tk=1024):
    s, d = hn.shape
    wsh = 2 * D_FF // N_CHIPS
    tk = _tile(s, tk)
    nk = s // tk

    def body(a_ref, b_ref, o_ref, acc):
        kk = pl.program_id(1)
        part = _dot_tn(a_ref[...], b_ref[0])

        @pl.when(kk == 0)
        def _():
            acc[...] = part

        @pl.when(kk > 0)
        def _():
            acc[...] += part

        @pl.when(kk == nk - 1)
        def _():
            o_ref[0] = acc[...]

    return pl.pallas_call(
        body, name="up_dw", grid=(N_CHIPS, nk),
        in_specs=[pl.BlockSpec((tk, d), lambda j, q: (q, 0)), pl.BlockSpec((1, tk, wsh), lambda j, q: (j >> 1, q, j & 1))],
        out_specs=pl.BlockSpec((1, d, wsh), lambda j, q: (j, 0, 0)),
        out_shape=jax.ShapeDtypeStruct((N_CHIPS, d, wsh), F32), scratch_shapes=[pltpu.VMEM((d, wsh), F32)],
        compiler_params=_cp(("parallel", "arbitrary")),
    )(hn, dup)


BIG_ROWS =(IN_DIM // 4, Q_DIM // 4, D_INNER // 4, D_MODEL // 4, 2 * D_FF // 4, D_FF // 4)
PACK_ROWS = 5376


def _pack_shards(parts):
    rows = [p.reshape(-1, D_MODEL) for p in parts]
    pad = PACK_ROWS - sum(BIG_ROWS)
    return jnp.concatenate(rows + [jnp.zeros((pad, D_MODEL), rows[0].dtype)], axis=0)


def _unpack_shards(buf):
    out, off = [], 0
    for n in BIG_ROWS:
        out.append(buf[off:off + n])
        off += n
    return out


def _permute_cols_in(w):
    pad = jnp.zeros((w.shape[0], PW - IN_DIM), w.dtype)
    return jnp.concatenate([w[:, :6656], w[:, 6688:], w[:, 6656:6688], pad], axis=1)


def _unpermute_cols_in(g):
    return jnp.concatenate([g[:, :6656], g[:, O_DT:O_DT + 32], g[:, 6656:O_DT]], axis=1)


SMALL = ("norm1_w", "b_gate", "attn_sinks", "ssd_conv_b", "dt_bias", "a_log", "d_skip", "ssd_norm_w", "norm2_w",
         "ffn_conv_b", "final_norm_w", "ssd_conv_w", "ffn_conv_w")


def _pad128(v):
    v = v.reshape(-1)
    return jnp.pad(v, (0, (-v.shape[0]) % 128))


def _pack_small(parts):
    flat = jnp.concatenate([_pad128(p) for p in parts])
    flat = jnp.pad(flat, (0, (-flat.shape[0]) % 1024))
    return flat.reshape(-1, 128)


def _unpack_small(buf, shapes):
    flat, out, off = buf.reshape(-1), [], 0
    for shp in shapes:
        n = 1
        for q in shp:
            n *= q
        out.append(flat[off:off + n].reshape(shp))
        off += n + (-n) % 128
    return out


def _vec128(v):
    return jnp.pad(v.reshape(1, -1), ((0, 0), (0, 128 - v.shape[-1])))


def kernel(x, norm1_w, w_in, b_gate, attn_sinks, w_attn_o, ssd_conv_w, ssd_conv_b, dt_bias, a_log, d_skip, ssd_norm_w, w_ssd_o, w_out, norm2_w, w_up, ffn_conv_w, ffn_conv_b, w_down, final_norm_w, loss_target, m_norm1_w, m_w_in, m_b_gate, m_attn_sinks, m_w_attn_o, m_ssd_conv_w, m_ssd_conv_b, m_dt_bias, m_a_log, m_d_skip, m_ssd_norm_w, m_w_ssd_o, m_w_out, m_norm2_w, m_w_up, m_ffn_conv_w, m_ffn_conv_b, m_w_down, m_final_norm_w, v_norm1_w, v_w_in, v_b_gate, v_attn_sinks, v_w_attn_o, v_ssd_conv_w, v_ssd_conv_b, v_dt_bias, v_a_log, v_d_skip, v_ssd_norm_w, v_w_ssd_o, v_w_out, v_norm2_w, v_w_up, v_ffn_conv_w, v_ffn_conv_b, v_w_down, v_final_norm_w):
    ix, iy, ic = lax.axis_index("x"), lax.axis_index("y"), lax.axis_index("c")
    chip = 2 * ix + iy
    x2 = x[0]
    tgt = loss_target[0]
    s = x2.shape[0]

    wsh = IN_DIM // N_CHIPS
    big_shards = dict(w_in=jnp.pad(w_in[0], ((0, 0), (0, W_IN_PAD - wsh))), w_attn_o=w_attn_o[0], w_ssd_o=w_ssd_o[0],
                      w_out=w_out[0], w_up=w_up[0], w_down=w_down[0])
    gathered = {}
    (gathered["w_in"],) = _run_job(_GatherJob(("w_in",), [big_shards["w_in"]]), "gather_w_in")
    early, late = ("w_attn_o", "w_ssd_o", "w_out"), ("w_up", "w_down")
    gather_early = _GatherJob(early, [big_shards[n] for n in early], at=(0.0, 0.5, 0.8))
    gather_late = _GatherJob(late, [big_shards[n] for n in late], at=(0.0, 0.55, 0.85))
    gw = gathered["w_in"]
    lo, hi = O_GA - 3 * wsh, O_GA + N_SSD_HEADS - 3 * wsh
    w_in_p = jnp.concatenate([gw[0, :, :wsh], gw[1, :, :wsh], gw[2, :, :wsh], gw[3, :, :lo], gw[3, :, hi:wsh],
                              gw[3, :, lo:hi], jnp.zeros((D_MODEL, PW - IN_DIM), BF16)], axis=1)
    small_sh = _pack_small([ssd_conv_w[0], ffn_conv_w[0]])
    small_all = _all_gather_small(small_sh)
    sc_parts = [_unpack_small(small_all[j], [(4, XBC_DIM // 4), (3, 2 * D_FF // 4)]) for j in range(N_CHIPS)]
    ssd_cw = jnp.concatenate([p[0] for p in sc_parts], axis=1)
    ffn_cw = jnp.concatenate([p[1] for p in sc_parts], axis=1)

    sinks128 = _vec128(attn_sinks)
    dtb128, alog128, dskip128 = _vec128(dt_bias), _vec128(a_log), _vec128(d_skip)

    xn = _rms_fwd(x2, norm1_w, name="norm1_fwd")
    proj, got = _mm(xn, w_in_p, name="proj_fwd", tn=1280, side=gather_early)
    gathered.update(zip(early, got))
    attn_pre, got = _attn_fwd(proj, sinks128, side=gather_late)
    gathered.update(zip(late, got))
    full = {n: gathered[n].reshape(-1, D_MODEL) for n in ("w_attn_o", "w_ssd_o", "w_out", "w_down")}
    full["w_up"] = gathered["w_up"]
    attn = _mm(attn_pre, full["w_attn_o"], name="attn_o_fwd")
    xbc = _ssd_conv_fwd(proj, ssd_cw, ssd_conv_b)
    y_ssd, hprev = _ssd_fwd(xbc, proj, dtb128, alog128, dskip128)
    yn = _gate_norm_fwd(y_ssd, proj, ssd_norm_w)
    ssd_out = _mm(yn, full["w_ssd_o"], name="ssd_o_fwd")
    merged = _merge_fwd(proj, b_gate, attn, ssd_out)
    h1 = _mm(merged, full["w_out"], name="out_fwd", resid=x2)
    hn = _rms_fwd(h1, norm2_w, name="norm2_fwd")
    up = _mm(hn, full["w_up"], name="up_fwd")
    act = _ffn_act_fwd(up, ffn_cw, ffn_conv_b)
    h2 = _mm(act, full["w_down"], name="down_fwd", resid=h1, tk=1408)

    dh2, loss_blk, g_final = _loss_bwd(h2, tgt, final_norm_w.reshape(1, -1))
    dact = _mm(dh2, full["w_down"], name="down_dx", tb=True, tn=1408)
    g_down = _mm(act, dh2, name="down_dw", ta=True, tm=1408)
    dup, g_ffn_cw, g_ffn_cb = _ffn_act_bwd(dact, up, ffn_cw, ffn_conv_b)
    dhn = _up_dx(dup, full["w_up"])
    g_up = _up_dw(hn, dup)
    dh1, g_norm2 = _rms_bwd(dhn, h1, norm2_w, dh2, name="norm2_bwd")
    dmerged = _mm(dh1, full["w_out"], name="out_dx", tb=True)
    g_out = _mm(merged, dh1, name="out_dw", ta=True)
    dattn, dssd_out, dga, dgs, g_ba, g_bs = _merge_bwd(dmerged, proj, b_gate, attn, ssd_out)
    dyn = _mm(dssd_out, full["w_ssd_o"], name="ssd_o_dx", tb=True)
    g_ssd_o = _mm(yn, dssd_out, name="ssd_o_dw", ta=True)
    dy_ssd, dz, g_ssd_norm = _gate_norm_bwd(dyn, y_ssd, proj, ssd_norm_w)
    slot = lambda g: g.reshape(N_CHIPS, -1, D_MODEL)
    big_grads = {}
    red = ("w_down", "w_up")
    (dxbc, ddt, dvec), got = _ssd_bwd(xbc, proj, dtb128, alog128, dskip128, hprev, dy_ssd,
                                      side=_ReduceJob(red, [slot(g_down), g_up], at=(0.0, 0.3, 0.8, 0.95)))
    big_grads.update(zip(red, got))
    dxbc_raw, g_ssd_cw, g_ssd_cb = _ssd_conv_bwd(dxbc, proj, ssd_cw, ssd_conv_b)
    dattn_pre = _mm(dattn, full["w_attn_o"], name="attn_o_dx", tb=True)
    g_attn_o = _mm(attn_pre, dattn, name="attn_o_dw", ta=True)
    red = ("w_out", "w_ssd_o", "w_attn_o")
    (dq, dk, dv, dsk), got = _attn_bwd(proj, sinks128, attn_pre, dattn_pre,
                                       side=_ReduceJob(red, [slot(g_out), slot(g_ssd_o), slot(g_attn_o)],
                                                       at=(0.0, 0.2, 0.5, 0.7)))
    big_grads.update(zip(red, got))
    pieces = [dq, dk, dv, dz, dxbc_raw, ddt[:, :N_SSD_HEADS], dga, dgs]
    shards_d, off = [[] for _ in range(N_CHIPS)], 0
    for p in pieces:
        for j in range(N_CHIPS):
            a, b = max(off, j * wsh), min(off + p.shape[1], (j + 1) * wsh)
            if a < b:
                shards_d[j].append(p[:, a - off:b - off])
        off += p.shape[1]
    zpad = jnp.zeros((s, W_IN_PAD - wsh), BF16)
    dproj_sh = jnp.stack([jnp.concatenate(sh + [zpad], axis=1) for sh in shards_d])
    g_in = _proj_dw(xn, dproj_sh)
    dxn, got = _proj_dx(dproj_sh, gathered["w_in"], side=_ReduceJob(("w_in",), [g_in], at=(0.0, 0.3, 0.8, 0.95)))
    big_grads["w_in"] = got[0][:, :wsh]
    dx, g_norm1 = _rms_bwd(dxn, x2, norm1_w, dh1, name="norm1_bwd")


    small_g = dict(
        norm1_w=g_norm1, b_gate=jnp.concatenate([g_ba, g_bs], axis=1), attn_sinks=dsk[0:1, :16], ssd_conv_b=g_ssd_cb,
        dt_bias=dvec[0:1, :32], a_log=dvec[1:2, :32], d_skip=dvec[2:3, :32], ssd_norm_w=g_ssd_norm, norm2_w=g_norm2,
        ffn_conv_b=jnp.concatenate([g_ffn_cb[0], g_ffn_cb[1]], axis=1), final_norm_w=g_final, ssd_conv_w=g_ssd_cw,
        ffn_conv_w=jnp.concatenate([g_ffn_cw[0], g_ffn_cw[1]], axis=1))
    small_buf = _pack_small([small_g[n] for n in SMALL] + [loss_blk])
    small_sum = _all_reduce_small(small_buf)
    small_shapes = [(1, D_MODEL), (1, 2 * D_MODEL), (1, 16), (1, XBC_DIM), (1, 32), (1, 32), (1, 32), (1, D_INNER),
                    (1, D_MODEL), (1, 2 * D_FF), (D_MODEL,), (4, XBC_DIM), (3, 2 * D_FF), (1, 128)]
    small_list = _unpack_small(small_sum, small_shapes)
    loss = small_list[-1][0, 0]
    grads = dict(zip(SMALL, small_list[:-1]))
    grads["ssd_conv_w"] = lax.dynamic_slice_in_dim(grads["ssd_conv_w"], chip * (XBC_DIM // 4), XBC_DIM // 4, axis=1)
    grads["ffn_conv_w"] = lax.dynamic_slice_in_dim(grads["ffn_conv_w"], chip * (2 * D_FF // 4), 2 * D_FF // 4, axis=1)
    grads.update(big_grads)

    weights = dict(norm1_w=norm1_w, w_in=w_in, b_gate=b_gate, attn_sinks=attn_sinks, w_attn_o=w_attn_o, ssd_conv_w=ssd_conv_w,
                   ssd_conv_b=ssd_conv_b, dt_bias=dt_bias, a_log=a_log, d_skip=d_skip, ssd_norm_w=ssd_norm_w, w_ssd_o=w_ssd_o,
                   w_out=w_out, norm2_w=norm2_w, w_up=w_up, ffn_conv_w=ffn_conv_w, ffn_conv_b=ffn_conv_b, w_down=w_down,
                   final_norm_w=final_norm_w)
    ms = dict(norm1_w=m_norm1_w, w_in=m_w_in, b_gate=m_b_gate, attn_sinks=m_attn_sinks, w_attn_o=m_w_attn_o,
              ssd_conv_w=m_ssd_conv_w, ssd_conv_b=m_ssd_conv_b, dt_bias=m_dt_bias, a_log=m_a_log, d_skip=m_d_skip,
              ssd_norm_w=m_ssd_norm_w, w_ssd_o=m_w_ssd_o, w_out=m_w_out, norm2_w=m_norm2_w, w_up=m_w_up,
              ffn_conv_w=m_ffn_conv_w, ffn_conv_b=m_ffn_conv_b, w_down=m_w_down, final_norm_w=m_final_norm_w)
    vs = dict(norm1_w=v_norm1_w, w_in=v_w_in, b_gate=v_b_gate, attn_sinks=v_attn_sinks, w_attn_o=v_w_attn_o,
              ssd_conv_w=v_ssd_conv_w, ssd_conv_b=v_ssd_conv_b, dt_bias=v_dt_bias, a_log=v_a_log, d_skip=v_d_skip,
              ssd_norm_w=v_ssd_norm_w, w_ssd_o=v_w_ssd_o, w_out=v_w_out, norm2_w=v_norm2_w, w_up=v_w_up,
              ffn_conv_w=v_ffn_conv_w, ffn_conv_b=v_ffn_conv_b, w_down=v_w_down, final_norm_w=v_final_norm_w)
    order = list(weights)
    deltas, new_m, new_v = {}, {}, {}
    for n in BIG:
        shp = weights[n].shape
        d_, m_, v_ = _adamw(weights[n][0], grads[n], ms[n][0], vs[n][0], name="adamw_" + n)
        deltas[n], new_m[n], new_v[n] = d_.reshape(shp), m_.reshape(shp), v_.reshape(shp)
    smalls = [n for n in order if n not in BIG]
    as2d = lambda a: a.reshape(-1, a.shape[-1])
    res = _adamw_many(*[[as2d(src[n][0] if src[n].ndim == 3 else src[n]) for n in smalls] for src in (weights, grads, ms, vs)])
    for i, n in enumerate(smalls):
        deltas[n], new_m[n], new_v[n] = (res[q * len(smalls) + i].reshape(weights[n].shape) for q in range(3))
    out_grads = [grads[n].reshape(weights[n].shape) for n in order]
    return (loss, dx[None], *out_grads, *[deltas[n] for n in order], *[new_m[n] for n in order], *[new_v[n] for n in order])
```

```python
import functools

import jax
import jax.numpy as jnp
from jax import lax
from jax.experimental import pallas as pl
from jax.experimental.pallas import tpu as pltpu

F32 = jnp.float32
BF16 = jnp.bfloat16
HI = lax.Precision.HIGHEST

D_MODEL = 1024
Q_DIM = 1024
KV_DIM = 256
D_INNER = 2048
BC_DIM = 512
XBC_DIM = 3072
N_SSD_HEADS = 32
D_FF = 2816
IN_DIM = 8736
BLK = 128
EPS = 1e-5
NEG = -1e30

O_Q, O_K, O_V, O_Z, O_X, O_GA, O_GS, O_DT = 0, 1024, 1280, 1536, 3584, 6656, 7680, 8704
PW = 8960

ADAM_LR, ADAM_B1, ADAM_B2, ADAM_EPS, ADAM_WD, ADAM_STEP = 0.001, 0.9, 0.999, 1e-08, 0.01, 10

VMEM_LIMIT = 52 * 1024 * 1024
MESH = pl.DeviceIdType.MESH


def _cp(sem=None):
    return pltpu.CompilerParams(dimension_semantics=sem, vmem_limit_bytes=VMEM_LIMIT)


def _dot(a, b, prec=None):
    return jnp.dot(a, b, preferred_element_type=F32, precision=prec)


def _dot_nt(a, b, prec=None):
    return lax.dot_general(a, b, (((1,), (1,)), ((), ())), preferred_element_type=F32, precision=prec)


def _dot_tn(a, b, prec=None):
    return lax.dot_general(a, b, (((0,), (0,)), ((), ())), preferred_element_type=F32, precision=prec)


def _sigmoid(x):
    return 0.5 * jnp.tanh(0.5 * x) + 0.5


def _tile(n, want):
    t = min(n, want)
    while n % t:
        t -= 128
    return t


def _mm(a, b, *, name, ta=False, tb=False, out_dtype=F32, resid=None, tm=1024, tn=1024, tk=1024, side=None):
    m, k = (a.shape[1], a.shape[0]) if ta else a.shape
    slots = b.ndim == 3
    if slots:
        n = b.shape[1] if tb else b.shape[0] * b.shape[2]
        tn, tk = (tn, b.shape[2]) if tb else (b.shape[2], tk)
    else:
        n = b.shape[0] if tb else b.shape[1]
    tm, tn, tk = _tile(m, tm), _tile(n, tn), _tile(k, tk)
    nk = k // tk
    dn = (((0 if ta else 1,), (1 if tb else 0,)), ((), ()))

    def body(*refs):
        if resid is None:
            a_ref, b_ref, o_ref, acc = refs
        else:
            a_ref, b_ref, r_ref, o_ref, acc = refs
        kk = pl.program_id(2)
        bv = b_ref[0] if slots else b_ref[...]
        part = lax.dot_general(a_ref[...].astype(BF16), bv.astype(BF16), dn, preferred_element_type=F32)

        @pl.when(kk == 0)
        def _():
            acc[...] = part

        @pl.when(kk > 0)
        def _():
            acc[...] += part

        @pl.when(kk == nk - 1)
        def _():
            r = acc[...]
            if resid is not None:
                r = r + r_ref[...]
            o_ref[...] = r.astype(out_dtype)

    a_spec = pl.BlockSpec((tk, tm), lambda i, j, q: (q, i)) if ta else pl.BlockSpec((tm, tk), lambda i, j, q: (i, q))
    if slots:
        b_spec = (pl.BlockSpec((1, tn, tk), lambda i, j, q: (q, j, 0)) if tb
                  else pl.BlockSpec((1, tk, tn), lambda i, j, q: (j, q, 0)))
    else:
        b_spec = pl.BlockSpec((tn, tk), lambda i, j, q: (j, q)) if tb else pl.BlockSpec((tk, tn), lambda i, j, q: (q, j))
    o_spec = pl.BlockSpec((tm, tn), lambda i, j, q: (i, j))
    ins, specs = [a, b], [a_spec, b_spec]
    if resid is not None:
        ins.append(resid)
        specs.append(o_spec)
    own, extra = _hosted(
        body, name=name, grid=(m // tm, n // tn, nk), in_specs=specs, out_specs=[o_spec],
        out_shape=[jax.ShapeDtypeStruct((m, n), out_dtype)], scratch_shapes=[pltpu.VMEM((tm, tn), F32)],
        args=ins, sem=("parallel", "parallel", "arbitrary"), side=side)
    return own[0] if side is None else (own[0], extra)


def _rms_fwd(x, w, *, name, tm=512):
    s, d = x.shape
    tm = _tile(s, tm)

    def body(x_ref, w_ref, o_ref):
        xv = x_ref[...]
        r = lax.rsqrt(jnp.mean(xv * xv, axis=-1, keepdims=True) + EPS)
        o_ref[...] = ((xv * r) * w_ref[...]).astype(BF16)

    return pl.pallas_call(
        body, name=name, grid=(s // tm,),
        in_specs=[pl.BlockSpec((tm, d), lambda i: (i, 0)), pl.BlockSpec((1, d), lambda i: (0, 0))],
        out_specs=pl.BlockSpec((tm, d), lambda i: (i, 0)),
        out_shape=jax.ShapeDtypeStruct((s, d), BF16), compiler_params=_cp(("parallel",)),
    )(x, w)


def _rms_bwd(dy, x, w, resid, *, name, tm=512):
    s, d = x.shape
    tm = _tile(s, tm)

    def body(dy_ref, x_ref, w_ref, r_ref, dx_ref, dw_ref):
        i = pl.program_id(0)
        xv = x_ref[...]
        r = lax.rsqrt(jnp.mean(xv * xv, axis=-1, keepdims=True) + EPS)
        xh = xv * r
        dyv = dy_ref[...]
        g = dyv * w_ref[...]
        dx_ref[...] = r_ref[...] + r * (g - xh * jnp.mean(g * xh, axis=-1, keepdims=True))
        part = jnp.sum(dyv * xh, axis=0, keepdims=True)

        @pl.when(i == 0)
        def _():
            dw_ref[...] = part

        @pl.when(i > 0)
        def _():
            dw_ref[...] += part

    row = pl.BlockSpec((tm, d), lambda i: (i, 0))
    vec = pl.BlockSpec((1, d), lambda i: (0, 0))
    return pl.pallas_call(
        body, name=name, grid=(s // tm,), in_specs=[row, row, vec, row], out_specs=[row, vec],
        out_shape=[jax.ShapeDtypeStruct((s, d), F32), jax.ShapeDtypeStruct((1, d), F32)],
        compiler_params=_cp(("arbitrary",)),
    )(dy, x, w, resid)


def _loss_bwd(h2, tgt, wf, *, tm=512):
    s, d = h2.shape
    tm = _tile(s, tm)

    def body(h_ref, t_ref, w_ref, dh_ref, loss_ref, dw_ref):
        i = pl.program_id(0)
        hv = h_ref[...]
        r = lax.rsqrt(jnp.mean(hv * hv, axis=-1, keepdims=True) + EPS)
        xh = hv * r
        wv = w_ref[...]
        e = xh * wv - t_ref[...]
        lpart = 0.5 * jnp.sum(jnp.mean(e * e, axis=-1, keepdims=True), axis=0, keepdims=True)
        dout = e * (1.0 / d)
        g = dout * wv
        dh_ref[...] = r * (g - xh * jnp.mean(g * xh, axis=-1, keepdims=True))
        part = jnp.sum(dout * xh, axis=0, keepdims=True)
        lrow = jnp.broadcast_to(lpart, (1, 128))

        @pl.when(i == 0)
        def _():
            dw_ref[...] = part
            loss_ref[...] = lrow

        @pl.when(i > 0)
        def _():
            dw_ref[...] += part
            loss_ref[...] += lrow

    row = pl.BlockSpec((tm, d), lambda i: (i, 0))
    vec = pl.BlockSpec((1, d), lambda i: (0, 0))
    return pl.pallas_call(
        body, name="loss_bwd", grid=(s // tm,), in_specs=[row, row, vec],
        out_specs=[row, pl.BlockSpec((1, 128), lambda i: (0, 0)), vec],
        out_shape=[jax.ShapeDtypeStruct((s, d), F32), jax.ShapeDtypeStruct((1, 128), F32),
                   jax.ShapeDtypeStruct((1, d), F32)],
        compiler_params=_cp(("arbitrary",)),
    )(h2, tgt, wf)


def _attn_mask(n):
    si = lax.broadcasted_iota(jnp.int32, (2 * BLK, 4 * BLK), 0)
    qi = lax.broadcasted_iota(jnp.int32, (2 * BLK, 4 * BLK), 1) & (BLK - 1)
    dist = BLK + qi - si
    kpos = n * BLK - BLK + si
    return (dist >= 0) & (dist < BLK) & (kpos >= 0)


def _attn_probs(q_ref, kc_ref, kp_ref, sk_ref, kvh, valid):
    rows = slice(kvh * 64, (kvh + 1) * 64)
    kt = jnp.concatenate([kp_ref[rows, :], kc_ref[rows, :]], axis=1).astype(BF16)
    qt = jnp.concatenate([q_ref[(kvh * 4 + g) * 64:(kvh * 4 + g + 1) * 64, :] for g in range(4)], axis=1).astype(BF16)
    s = _dot_tn(kt, qt) * 0.125
    s = jnp.where(valid, s, NEG)
    head = lax.broadcasted_iota(jnp.int32, (1, 4 * BLK), 1) >> 7
    sink = jnp.zeros((1, 4 * BLK), F32)
    for g in range(4):
        sink = jnp.where(head == g, sk_ref[0:1, kvh * 4 + g:kvh * 4 + g + 1], sink)
    m = jnp.maximum(jnp.max(s, axis=0, keepdims=True), sink)
    p = jnp.where(valid, jnp.exp(s - m), 0.0)
    es = jnp.exp(sink - m)
    inv = 1.0 / (jnp.sum(p, axis=0, keepdims=True) + es)
    return qt, kt, p * inv, es * inv


def _attn_in_specs(cur, prev):
    return [pl.BlockSpec((Q_DIM, BLK), lambda n: (0, cur(n))),
            pl.BlockSpec((KV_DIM, BLK), lambda n: (O_K // KV_DIM, cur(n))),
            pl.BlockSpec((KV_DIM, BLK), lambda n: (O_K // KV_DIM, prev(n))),
            pl.BlockSpec((KV_DIM, BLK), lambda n: (O_V // KV_DIM, cur(n))),
            pl.BlockSpec((KV_DIM, BLK), lambda n: (O_V // KV_DIM, prev(n))),
            pl.BlockSpec((1, 128), lambda n: (0, 0))]


def _attn_fwd(qkvt, sinks, side=None):
    s = qkvt.shape[1]
    nb = s // BLK

    def body(q_ref, kc_ref, kp_ref, vc_ref, vp_ref, sk_ref, o_ref):
        valid = _attn_mask(pl.program_id(0))
        for kvh in range(4):
            rows = slice(kvh * 64, (kvh + 1) * 64)
            _, _, probs, _ = _attn_probs(q_ref, kc_ref, kp_ref, sk_ref, kvh, valid)
            vt = jnp.concatenate([vp_ref[rows, :], vc_ref[rows, :]], axis=1).astype(BF16)
            o = _dot(vt, probs.astype(BF16))
            for g in range(4):
                h = kvh * 4 + g
                o_ref[h * 64:(h + 1) * 64, :] = o[:, g * BLK:(g + 1) * BLK].astype(BF16)

    own, extra = _hosted(
        body, name="attn_fwd", grid=(nb,), in_specs=_attn_in_specs(lambda n: n, lambda n: jnp.maximum(n - 1, 0)),
        out_specs=[pl.BlockSpec((Q_DIM, BLK), lambda n: (0, n))],
        out_shape=[jax.ShapeDtypeStruct((Q_DIM, s), BF16)], scratch_shapes=[],
        args=(qkvt, qkvt, qkvt, qkvt, qkvt, sinks), sem=("parallel",), side=side)
    return own[0] if side is None else (own[0], extra)


def _attn_bwd(qkvt, sinks, o, do, side=None):
    s = qkvt.shape[1]
    nb = s // BLK

    def body(q_ref, kc_ref, kp_ref, vc_ref, vp_ref, sk_ref, o_ref, do_ref, dq_ref, dk_ref, dv_ref, dsk_ref, ck, cv, nk, nv):
        n = pl.program_id(0)

        @pl.when(n == 0)
        def _():
            ck[...] = jnp.zeros_like(ck)
            cv[...] = jnp.zeros_like(cv)
            dsk_ref[...] = jnp.zeros_like(dsk_ref)

        @pl.when(n < nb)
        def _():
            valid = _attn_mask(n)
            lane = lax.broadcasted_iota(jnp.int32, (1, 128), 1)
            dsk = jnp.zeros((1, 128), F32)
            for kvh in range(4):
                rows = slice(kvh * 64, (kvh + 1) * 64)
                qt, kt, probs, psink = _attn_probs(q_ref, kc_ref, kp_ref, sk_ref, kvh, valid)
                vt = jnp.concatenate([vp_ref[rows, :], vc_ref[rows, :]], axis=1).astype(BF16)
                heads = [slice((kvh * 4 + g) * 64, (kvh * 4 + g + 1) * 64) for g in range(4)]
                dot = jnp.concatenate([do_ref[hh, :] for hh in heads], axis=1)
                ot = jnp.concatenate([o_ref[hh, :] for hh in heads], axis=1).astype(F32)
                delta = jnp.sum(dot * ot, axis=0, keepdims=True)
                dot16 = dot.astype(BF16)
                dp = _dot_tn(vt, dot16)
                ds = (probs * (dp - delta) * 0.125).astype(BF16)
                dqt = _dot(kt, ds)
                nk[rows, :] = _dot_nt(qt, ds)
                nv[rows, :] = _dot_nt(dot16, probs.astype(BF16))
                sd = psink * delta
                for g in range(4):
                    dq_ref[heads[g], :] = dqt[:, g * BLK:(g + 1) * BLK].astype(BF16)
                    val = -jnp.sum(sd[:, g * BLK:(g + 1) * BLK], axis=1, keepdims=True)
                    dsk = dsk + jnp.where(lane == kvh * 4 + g, val, 0.0)
            dsk_ref[0:1, :] += dsk
            dk_ref[...] = (ck[...] + nk[:, :BLK]).astype(BF16)
            dv_ref[...] = (cv[...] + nv[:, :BLK]).astype(BF16)
            ck[...] = nk[:, BLK:]
            cv[...] = nv[:, BLK:]

        @pl.when(n == nb)
        def _():
            dk_ref[...] = ck[...].astype(BF16)
            dv_ref[...] = cv[...].astype(BF16)

    cur = lambda n: jnp.minimum(n, nb - 1)
    prev = lambda n: jnp.maximum(jnp.minimum(n, nb - 1) - 1, 0)
    outb = lambda n: jnp.maximum(n - 1, 0)
    own, extra = _hosted(
        body, name="attn_bwd", grid=(nb + 1,),
        in_specs=_attn_in_specs(cur, prev) + [pl.BlockSpec((Q_DIM, BLK), lambda n: (0, cur(n))),
                                              pl.BlockSpec((Q_DIM, BLK), lambda n: (0, cur(n)))],
        out_specs=[pl.BlockSpec((Q_DIM, BLK), lambda n: (0, cur(n))),
                   pl.BlockSpec((KV_DIM, BLK), lambda n: (0, outb(n))),
                   pl.BlockSpec((KV_DIM, BLK), lambda n: (0, outb(n))),
                   pl.BlockSpec((8, 128), lambda n: (0, 0))],
        out_shape=[jax.ShapeDtypeStruct((Q_DIM, s), BF16), jax.ShapeDtypeStruct((KV_DIM, s), BF16),
                   jax.ShapeDtypeStruct((KV_DIM, s), BF16), jax.ShapeDtypeStruct((8, 128), F32)],
        scratch_shapes=[pltpu.VMEM((KV_DIM, BLK), F32)] * 2 + [pltpu.VMEM((KV_DIM, 2 * BLK), F32)] * 2,
        args=(qkvt, qkvt, qkvt, qkvt, qkvt, sinks, o, do), sem=("arbitrary",), side=side)
    return own if side is None else (own, extra)


def _shift_down(x, j):
    if j == 0:
        return x
    row = lax.broadcasted_iota(jnp.int32, x.shape, 0)
    return jnp.where(row >= j, pltpu.roll(x, j, 0), 0.0)


def _shift_up(x, j):
    if j == 0:
        return x
    s = x.shape[0]
    row = lax.broadcasted_iota(jnp.int32, x.shape, 0)
    return jnp.where(row < s - j, pltpu.roll(x, s - j, 0), 0.0)


def _conv(x, w_ref, b_ref):
    kk = w_ref.shape[0]
    y = _shift_down(x, kk - 1) * w_ref[0:1, :]
    for q in range(1, kk):
        y = y + _shift_down(x, kk - 1 - q) * w_ref[q:q + 1, :]
    return y + b_ref[...]


def _conv_bwd(dy, x, w_ref, dx_dtype):
    kk = w_ref.shape[0]
    dx = _shift_up(dy, kk - 1) * w_ref[0:1, :]
    dws = [jnp.sum(dy * _shift_down(x, kk - 1), axis=0, keepdims=True)]
    for q in range(1, kk):
        dx = dx + _shift_up(dy, kk - 1 - q) * w_ref[q:q + 1, :]
        dws.append(jnp.sum(dy * _shift_down(x, kk - 1 - q), axis=0, keepdims=True))
    return dx.astype(dx_dtype), dws, jnp.sum(dy, axis=0, keepdims=True)


def _dsilu(y, sg):
    return sg * (1.0 + y * (1.0 - sg))


CT = 256


def _ssd_conv_fwd(proj, w, b):
    s = proj.shape[0]

    def body(x_ref, w_ref, b_ref, o_ref):
        y = _conv(x_ref[...], w_ref, b_ref)
        o_ref[...] = y * _sigmoid(y)

    return pl.pallas_call(
        body, name="ssd_conv_fwd", grid=(XBC_DIM // CT,),
        in_specs=[pl.BlockSpec((s, CT), lambda i: (0, O_X // CT + i)), pl.BlockSpec((4, CT), lambda i: (0, i)),
                  pl.BlockSpec((1, CT), lambda i: (0, i))],
        out_specs=pl.BlockSpec((s, CT), lambda i: (0, i)),
        out_shape=jax.ShapeDtypeStruct((s, XBC_DIM), F32), compiler_params=_cp(("parallel",)),
    )(proj, w, b)


def _ssd_conv_bwd(dact, proj, w, b):
    s = proj.shape[0]

    def body(d_ref, x_ref, w_ref, b_ref, dx_ref, dw_ref, db_ref):
        x = x_ref[...]
        y = _conv(x, w_ref, b_ref)
        dy = d_ref[...] * _dsilu(y, _sigmoid(y))
        dx, dws, db = _conv_bwd(dy, x, w_ref, BF16)
        dx_ref[...] = dx
        for q in range(4):
            dw_ref[q:q + 1, :] = dws[q]
        db_ref[...] = db

    return pl.pallas_call(
        body, name="ssd_conv_bwd", grid=(XBC_DIM // CT,),
        in_specs=[pl.BlockSpec((s, CT), lambda i: (0, i)), pl.BlockSpec((s, CT), lambda i: (0, O_X // CT + i)),
                  pl.BlockSpec((4, CT), lambda i: (0, i)), pl.BlockSpec((1, CT), lambda i: (0, i))],
        out_specs=[pl.BlockSpec((s, CT), lambda i: (0, i)), pl.BlockSpec((4, CT), lambda i: (0, i)),
                   pl.BlockSpec((1, CT), lambda i: (0, i))],
        out_shape=[jax.ShapeDtypeStruct((s, XBC_DIM), BF16), jax.ShapeDtypeStruct((4, XBC_DIM), F32),
                   jax.ShapeDtypeStruct((1, XBC_DIM), F32)],
        compiler_params=_cp(("parallel",)),
    )(dact, proj, w, b)


NFT = D_FF // CT


def _ffn_act_fwd(up, w, b):
    s = up.shape[0]

    def body(v_ref, g_ref, wv_ref, wg_ref, bv_ref, bg_ref, o_ref):
        val = _conv(v_ref[...], wv_ref, bv_ref)
        gt = _conv(g_ref[...], wg_ref, bg_ref)
        o_ref[...] = ((gt * _sigmoid(gt)) * val).astype(BF16)

    col = lambda off: (lambda i: (0, off + i))
    return pl.pallas_call(
        body, name="ffn_act_fwd", grid=(NFT,),
        in_specs=[pl.BlockSpec((s, CT), col(0)), pl.BlockSpec((s, CT), col(NFT)),
                  pl.BlockSpec((3, CT), col(0)), pl.BlockSpec((3, CT), col(NFT)),
                  pl.BlockSpec((1, CT), col(0)), pl.BlockSpec((1, CT), col(NFT))],
        out_specs=pl.BlockSpec((s, CT), col(0)),
        out_shape=jax.ShapeDtypeStruct((s, D_FF), BF16), compiler_params=_cp(("parallel",)),
    )(up, up, w, w, b, b)


def _ffn_act_bwd(dact, up, w, b):
    s = up.shape[0]

    def body(d_ref, v_ref, g_ref, wv_ref, wg_ref, bv_ref, bg_ref, dx_ref, dw_ref, db_ref):
        xv, xg = v_ref[...], g_ref[...]
        val = _conv(xv, wv_ref, bv_ref)
        gt = _conv(xg, wg_ref, bg_ref)
        sg = _sigmoid(gt)
        d = d_ref[...]
        for half, (dy, x, w_ref) in enumerate(((d * (gt * sg), xv, wv_ref), (d * val * _dsilu(gt, sg), xg, wg_ref))):
            dx, dws, db = _conv_bwd(dy, x, w_ref, BF16)
            dx_ref[half] = dx
            for q in range(3):
                dw_ref[half, q:q + 1, :] = dws[q]
            db_ref[half] = db

    col = lambda off: (lambda i: (0, off + i))
    both = lambda i: (0, 0, i)
    return pl.pallas_call(
        body, name="ffn_act_bwd", grid=(NFT,),
        in_specs=[pl.BlockSpec((s, CT), col(0)), pl.BlockSpec((s, CT), col(0)), pl.BlockSpec((s, CT), col(NFT)),
                  pl.BlockSpec((3, CT), col(0)), pl.BlockSpec((3, CT), col(NFT)),
                  pl.BlockSpec((1, CT), col(0)), pl.BlockSpec((1, CT), col(NFT))],
        out_specs=[pl.BlockSpec((2, s, CT), both), pl.BlockSpec((2, 3, CT), both), pl.BlockSpec((2, 1, CT), both)],
        out_shape=[jax.ShapeDtypeStruct((2, s, D_FF), BF16), jax.ShapeDtypeStruct((2, 3, D_FF), F32),
                   jax.ShapeDtypeStruct((2, 1, D_FF), F32)],
        compiler_params=_cp(("parallel",)),
    )(dact, up, up, w, w, b, b)


def _expand_mat():
    r = lax.broadcasted_iota(jnp.int32, (128, D_INNER), 0)
    c = lax.broadcasted_iota(jnp.int32, (128, D_INNER), 1)
    return ((c >> 6) == r).astype(BF16)


def _reduce_mat():
    r = lax.broadcasted_iota(jnp.int32, (D_INNER, 128), 0)
    c = lax.broadcasted_iota(jnp.int32, (D_INNER, 128), 1)
    return ((r >> 6) == c).astype(BF16)


def _split(v, parts):
    out = []
    for _ in range(parts - 1):
        p = v.astype(BF16)
        out.append(p)
        v = v - p.astype(F32)
    out.append(v.astype(BF16))
    return out


def _sel_dot(v, sel, parts):
    acc = None
    for p in reversed(_split(v, parts)):
        t = _dot(p, sel)
        acc = t if acc is None else acc + t
    return acc


def _row8(v):
    return jnp.broadcast_to(v, (8, v.shape[1]))


def _tril():
    r = lax.broadcasted_iota(jnp.int32, (BLK, BLK), 0)
    c = lax.broadcasted_iota(jnp.int32, (BLK, BLK), 1)
    return r >= c


def _softplus(x):
    return jnp.maximum(x, 0.0) + jnp.log(1.0 + jnp.exp(-jnp.abs(x)))


def _ssd_common(dtraw_ref, dtb_ref, alog_ref):
    causal = _tril()
    e_mat = _expand_mat()
    a_neg = -jnp.exp(alog_ref[...])
    dt = _softplus(dtraw_ref[...] + dtb_ref[...])
    a_cs = _dot(causal.astype(F32), dt * a_neg, HI)
    a_cs_t = a_cs.T
    dt_x = _sel_dot(dt, e_mat, 3)
    acs_x = _sel_dot(a_cs, e_mat, 3)
    alast_x = acs_x[BLK - 1:BLK, :]
    ea_x = jnp.exp(acs_x)
    ds_x = jnp.exp(alast_x - acs_x)
    elast_x = jnp.exp(alast_x)
    return causal, e_mat, a_neg, dt, a_cs, a_cs_t, dt_x, ea_x, ds_x, elast_x


def _decay(a_cs, a_cs_t, h, causal):
    seg = a_cs[:, h:h + 1] - a_cs_t[h:h + 1, :]
    return jnp.where(causal, jnp.exp(jnp.where(causal, seg, 0.0)), 0.0)


def _ssd_fwd(xbc, proj, dt_bias, a_log, d_skip):
    s = xbc.shape[0]
    nc = s // BLK

    def body(xs_ref, b_ref, c_ref, dtraw_ref, dtb_ref, alog_ref, dskip_ref, y_ref, hp_ref, h_scr, xc16):
        @pl.when(pl.program_id(0) == 0)
        def _():
            h_scr[...] = jnp.zeros_like(h_scr)

        causal, e_mat, _, _, a_cs, a_cs_t, dt_x, ea_x, ds_x, elast_x = _ssd_common(dtraw_ref, dtb_ref, alog_ref)
        dskip_x = _sel_dot(_row8(dskip_ref[...]), e_mat, 3)[0:1]
        xs = xs_ref[...]
        xc = xs * dt_x
        xc16[...] = xc.astype(BF16)
        xcd = (xc * ds_x).astype(BF16)
        hp_ref[0] = h_scr[...]
        for g in range(4):
            gs = slice(g * 512, (g + 1) * 512)
            cg = c_ref[:, g * 128:(g + 1) * 128].astype(BF16)
            bg = b_ref[:, g * 128:(g + 1) * 128].astype(BF16)
            cb = _dot_nt(cg, bg)
            hg = h_scr[:, gs]
            yoff = _dot(cg, hg.astype(BF16)) * ea_x[:, gs]
            for j in range(8):
                h = g * 8 + j
                hsl = slice(h * 64, (h + 1) * 64)
                mm = (cb * _decay(a_cs, a_cs_t, h, causal)).astype(BF16)
                y_ref[:, hsl] = _dot(mm, xc16[:, hsl])
            y_ref[:, gs] += yoff + xs[:, gs] * dskip_x[:, gs]
            h_scr[:, gs] = hg * elast_x[:, gs] + _dot_tn(bg, xcd[:, gs])

    vec = pl.BlockSpec((1, 128), lambda c: (0, 0))
    return pl.pallas_call(
        body, name="ssd_fwd", grid=(nc,),
        in_specs=[pl.BlockSpec((BLK, D_INNER), lambda c: (c, 0)),
                  pl.BlockSpec((BLK, BC_DIM), lambda c: (c, D_INNER // BC_DIM)),
                  pl.BlockSpec((BLK, BC_DIM), lambda c: (c, D_INNER // BC_DIM + 1)),
                  pl.BlockSpec((BLK, 128), lambda c: (c, O_DT // 128)), vec, vec, vec],
        out_specs=[pl.BlockSpec((BLK, D_INNER), lambda c: (c, 0)),
                   pl.BlockSpec((1, 128, D_INNER), lambda c: (c, 0, 0))],
        out_shape=[jax.ShapeDtypeStruct((s, D_INNER), F32), jax.ShapeDtypeStruct((nc, 128, D_INNER), F32)],
        scratch_shapes=[pltpu.VMEM((128, D_INNER), F32), pltpu.VMEM((BLK, D_INNER), BF16)],
        compiler_params=_cp(("arbitrary",)),
    )(xbc, xbc, xbc, proj, dt_bias, a_log, d_skip)


def _ssd_bwd(xbc, proj, dt_bias, a_log, d_skip, hprev, dy, side=None):
    s = xbc.shape[0]
    nc = s // BLK

    def body(xs_ref, b_ref, c_ref, dtraw_ref, dtb_ref, alog_ref, dskip_ref, hp_ref, dy_ref,
             dxbc_ref, ddt_ref, dvec_ref, dh_scr, xc16, dy16, dxc_scr, dacs_r, tdiff):
        step = pl.program_id(0)
        dacs_r[...] = jnp.zeros_like(dacs_r)

        @pl.when(step == 0)
        def _():
            dh_scr[...] = jnp.zeros_like(dh_scr)
            dvec_ref[...] = jnp.zeros_like(dvec_ref)

        causal, e_mat, a_neg, dt, a_cs, a_cs_t, dt_x, ea_x, ds_x, elast_x = _ssd_common(dtraw_ref, dtb_ref, alog_ref)
        r_mat = _reduce_mat()
        lane = lax.broadcasted_iota(jnp.int32, (1, 128), 1)
        dskip_x = _sel_dot(_row8(dskip_ref[...]), e_mat, 3)[0:1]
        xs = xs_ref[...]
        dy = dy_ref[...]
        xc = xs * dt_x
        xcd = xc * ds_x
        xc16[...] = xc.astype(BF16)
        dy16[...] = dy.astype(BF16)
        dyea = dy * ea_x
        dh = dh_scr[...]
        hp = hp_ref[0]
        dalast_x = jnp.sum(dh * hp, axis=0, keepdims=True) * elast_x
        dacs = jnp.zeros((BLK, 128), F32)
        for g in range(4):
            gs = slice(g * 512, (g + 1) * 512)
            bsl = slice(g * 128, (g + 1) * 128)
            cg = c_ref[:, bsl].astype(BF16)
            bg = b_ref[:, bsl].astype(BF16)
            cb = _dot_nt(cg, bg)
            hg16 = hp[:, gs].astype(BF16)
            dhg16 = dh[:, gs].astype(BF16)
            raw = _dot(cg, hg16)
            draw16 = dyea[:, gs].astype(BF16)
            dcg = _dot_nt(draw16, hg16)
            dhp_g = _dot_tn(cg, draw16)
            dbg = _dot_nt(xcd[:, gs].astype(BF16), dhg16)
            dxcd = _dot(bg, dhg16)
            dcb = jnp.zeros((BLK, BLK), F32)
            for j in range(8):
                h = g * 8 + j
                hsl = slice(h * 64, (h + 1) * 64)
                decay = _decay(a_cs, a_cs_t, h, causal)
                m = cb * decay
                dm = _dot_nt(dy16[:, hsl], xc16[:, hsl])
                dxc_scr[:, hsl] = _dot_tn(m.astype(BF16), dy16[:, hsl])
                dcb = dcb + dm * decay
                dseg = dm * m
                oneh = jnp.where(lane == h, 1.0, 0.0)
                dacs = dacs + jnp.sum(dseg, axis=1, keepdims=True) * oneh
                dacs_r[h:h + 1, :] = jnp.sum(dseg, axis=0, keepdims=True)
            dcb16 = dcb.astype(BF16)
            dcg = dcg + _dot(dcb16, bg)
            dbg = dbg + _dot_tn(dcb16, cg)
            dxbc_ref[:, D_INNER + g * 128:D_INNER + (g + 1) * 128] = dbg
            dxbc_ref[:, D_INNER + BC_DIM + g * 128:D_INNER + BC_DIM + (g + 1) * 128] = dcg
            dxc_scr[:, gs] += dxcd * ds_x[:, gs]
            dh_scr[:, gs] = dh[:, gs] * elast_x[:, gs] + dhp_g
            tst = dxcd * xcd[:, gs]
            tdiff[:, gs] = dy[:, gs] * (raw * ea_x[:, gs]) - tst
            tdiff[BLK - 1:BLK, gs] += jnp.sum(tst, axis=0, keepdims=True)
        dxc = dxc_scr[...]
        row = lax.broadcasted_iota(jnp.int32, (BLK, D_INNER), 0)
        tfull = tdiff[...] + jnp.where(row == BLK - 1, dalast_x, 0.0)
        dacs = dacs + _sel_dot(tfull, r_mat, 2) - dacs_r[...].T
        da = _dot_tn(causal.astype(F32), dacs, HI)
        ddt = da * a_neg + _sel_dot(dxc * xs, r_mat, 2)
        lmask = lax.broadcasted_iota(jnp.int32, (BLK, 128), 1) < N_SSD_HEADS
        ddtraw = jnp.where(lmask, ddt * _sigmoid(dtraw_ref[...] + dtb_ref[...]), 0.0)
        ddt_ref[...] = ddtraw.astype(BF16)
        dxbc_ref[:, 0:D_INNER] = dy * dskip_x + dxc * dt_x
        dvec_ref[0:1, :] += jnp.sum(ddtraw, axis=0, keepdims=True)
        dvec_ref[1:2, :] += jnp.where(lane < N_SSD_HEADS, jnp.sum(da * dt, axis=0, keepdims=True) * a_neg, 0.0)
        dvec_ref[2:3, :] += _sel_dot(_row8(jnp.sum(dy * xs, axis=0, keepdims=True)), r_mat, 3)[0:1]

    rev = lambda c: nc - 1 - c
    vec = pl.BlockSpec((1, 128), lambda c: (0, 0))
    own, extra = _hosted(
        body, name="ssd_bwd", grid=(nc,),
        in_specs=[pl.BlockSpec((BLK, D_INNER), lambda c: (rev(c), 0)),
                  pl.BlockSpec((BLK, BC_DIM), lambda c: (rev(c), D_INNER // BC_DIM)),
                  pl.BlockSpec((BLK, BC_DIM), lambda c: (rev(c), D_INNER // BC_DIM + 1)),
                  pl.BlockSpec((BLK, 128), lambda c: (rev(c), O_DT // 128)), vec, vec, vec,
                  pl.BlockSpec((1, 128, D_INNER), lambda c: (rev(c), 0, 0)),
                  pl.BlockSpec((BLK, D_INNER), lambda c: (rev(c), 0))],
        out_specs=[pl.BlockSpec((BLK, XBC_DIM), lambda c: (rev(c), 0)),
                   pl.BlockSpec((BLK, 128), lambda c: (rev(c), 0)),
                   pl.BlockSpec((8, 128), lambda c: (0, 0))],
        out_shape=[jax.ShapeDtypeStruct((s, XBC_DIM), F32), jax.ShapeDtypeStruct((s, 128), BF16),
                   jax.ShapeDtypeStruct((8, 128), F32)],
        scratch_shapes=[pltpu.VMEM((128, D_INNER), F32), pltpu.VMEM((BLK, D_INNER), BF16),
                        pltpu.VMEM((BLK, D_INNER), BF16), pltpu.VMEM((BLK, D_INNER), F32),
                        pltpu.VMEM((128, BLK), F32), pltpu.VMEM((BLK, D_INNER), F32)],
        args=(xbc, xbc, xbc, proj, dt_bias, a_log, d_skip, hprev, dy), sem=("arbitrary",), side=side)
    return own if side is None else (own, extra)


GW = 512


def _gate_norm_fwd(y, proj, wn, *, tm=512):
    s = y.shape[0]
    tm = _tile(s, tm)

    def body(y_ref, z_ref, w_ref, o_ref):
        z = z_ref[...]
        y2 = y_ref[...] * (z * _sigmoid(z))
        r = lax.rsqrt(jnp.mean(y2 * y2, axis=-1, keepdims=True) + EPS)
        o_ref[...] = ((y2 * r) * w_ref[...]).astype(BF16)

    return pl.pallas_call(
        body, name="gate_norm_fwd", grid=(s // tm, 4),
        in_specs=[pl.BlockSpec((tm, GW), lambda i, g: (i, g)), pl.BlockSpec((tm, GW), lambda i, g: (i, O_Z // GW + g)),
                  pl.BlockSpec((1, GW), lambda i, g: (0, g))],
        out_specs=pl.BlockSpec((tm, GW), lambda i, g: (i, g)),
        out_shape=jax.ShapeDtypeStruct((s, D_INNER), BF16), compiler_params=_cp(("parallel", "parallel")),
    )(y, proj, wn)


def _gate_norm_bwd(dyn, y, proj, wn, *, tm=512):
    s = y.shape[0]
    tm = _tile(s, tm)

    def body(d_ref, y_ref, z_ref, w_ref, dy_ref, dz_ref, dw_ref):
        i = pl.program_id(1)
        z = z_ref[...]
        sg = _sigmoid(z)
        sz = z * sg
        yv = y_ref[...]
        y2 = yv * sz
        r = lax.rsqrt(jnp.mean(y2 * y2, axis=-1, keepdims=True) + EPS)
        xh = y2 * r
        dv = d_ref[...]
        g = dv * w_ref[...]
        dy2 = r * (g - xh * jnp.mean(g * xh, axis=-1, keepdims=True))
        dy_ref[...] = dy2 * sz
        dz_ref[...] = (dy2 * yv * _dsilu(z, sg)).astype(BF16)
        part = jnp.sum(dv * xh, axis=0, keepdims=True)

        @pl.when(i == 0)
        def _():
            dw_ref[...] = part

        @pl.when(i > 0)
        def _():
            dw_ref[...] += part

    blk = pl.BlockSpec((tm, GW), lambda g, i: (i, g))
    vec = pl.BlockSpec((1, GW), lambda g, i: (0, g))
    return pl.pallas_call(
        body, name="gate_norm_bwd", grid=(4, s // tm),
        in_specs=[blk, blk, pl.BlockSpec((tm, GW), lambda g, i: (i, O_Z // GW + g)), vec],
        out_specs=[blk, blk, vec],
        out_shape=[jax.ShapeDtypeStruct((s, D_INNER), F32), jax.ShapeDtypeStruct((s, D_INNER), BF16),
                   jax.ShapeDtypeStruct((1, D_INNER), F32)],
        compiler_params=_cp(("parallel", "arbitrary")),
    )(dyn, y, proj, wn)


def _merge_fwd(proj, b_gate, attn, ssd_out, *, tm=512):
    s = attn.shape[0]
    tm = _tile(s, tm)

    def body(ga_ref, gs_ref, ba_ref, bs_ref, a_ref, s_ref, o_ref):
        ga = _sigmoid(ga_ref[...] + ba_ref[...])
        gs = _sigmoid(gs_ref[...] + bs_ref[...])
        o_ref[...] = (ga * a_ref[...] + gs * s_ref[...]).astype(BF16)

    blk = pl.BlockSpec((tm, GW), lambda i, j: (i, j))
    return pl.pallas_call(
        body, name="merge_fwd", grid=(s // tm, 2),
        in_specs=[pl.BlockSpec((tm, GW), lambda i, j: (i, O_GA // GW + j)),
                  pl.BlockSpec((tm, GW), lambda i, j: (i, O_GS // GW + j)),
                  pl.BlockSpec((1, GW), lambda i, j: (0, j)), pl.BlockSpec((1, GW), lambda i, j: (0, 2 + j)), blk, blk],
        out_specs=blk, out_shape=jax.ShapeDtypeStruct((s, D_MODEL), BF16),
        compiler_params=_cp(("parallel", "parallel")),
    )(proj, proj, b_gate, b_gate, attn, ssd_out)


def _merge_bwd(dm, proj, b_gate, attn, ssd_out, *, tm=512):
    s = attn.shape[0]
    tm = _tile(s, tm)

    def body(d_ref, ga_ref, gs_ref, ba_ref, bs_ref, a_ref, s_ref, da_ref, ds_ref, dga_ref, dgs_ref, dba_ref, dbs_ref):
        i = pl.program_id(1)
        ga = _sigmoid(ga_ref[...] + ba_ref[...])
        gs = _sigmoid(gs_ref[...] + bs_ref[...])
        d = d_ref[...]
        da_ref[...] = (d * ga).astype(BF16)
        ds_ref[...] = (d * gs).astype(BF16)
        dga = d * a_ref[...] * (ga * (1.0 - ga))
        dgs = d * s_ref[...] * (gs * (1.0 - gs))
        dga_ref[...] = dga.astype(BF16)
        dgs_ref[...] = dgs.astype(BF16)
        pa = jnp.sum(dga, axis=0, keepdims=True)
        ps = jnp.sum(dgs, axis=0, keepdims=True)

        @pl.when(i == 0)
        def _():
            dba_ref[...] = pa
            dbs_ref[...] = ps

        @pl.when(i > 0)
        def _():
            dba_ref[...] += pa
            dbs_ref[...] += ps

    blk = pl.BlockSpec((tm, GW), lambda j, i: (i, j))
    vec = pl.BlockSpec((1, GW), lambda j, i: (0, j))
    sd = jax.ShapeDtypeStruct((s, D_MODEL), BF16)
    vd = jax.ShapeDtypeStruct((1, D_MODEL), F32)
    return pl.pallas_call(
        body, name="merge_bwd", grid=(2, s // tm),
        in_specs=[blk, pl.BlockSpec((tm, GW), lambda j, i: (i, O_GA // GW + j)),
                  pl.BlockSpec((tm, GW), lambda j, i: (i, O_GS // GW + j)),
                  vec, pl.BlockSpec((1, GW), lambda j, i: (0, 2 + j)), blk, blk],
        out_specs=[blk, blk, blk, blk, vec, vec], out_shape=[sd, sd, sd, sd, vd, vd],
        compiler_params=_cp(("parallel", "arbitrary")),
    )(dm, proj, proj, b_gate, b_gate, attn, ssd_out)


def _adamw_math(w, g, m, v):
    mn = ADAM_B1 * m + (1.0 - ADAM_B1) * g
    vn = ADAM_B2 * v + (1.0 - ADAM_B2) * (g * g)
    m_hat = mn / (1.0 - ADAM_B1 ** ADAM_STEP)
    v_hat = vn / (1.0 - ADAM_B2 ** ADAM_STEP)
    return -ADAM_LR * (m_hat / (jnp.sqrt(v_hat) + ADAM_EPS) + ADAM_WD * w), mn, vn


def _adamw_many(ws, gs, ms, vs):
    n = len(ws)

    def body(*refs):
        outs = refs[4 * n:]
        for i in range(n):
            res = _adamw_math(*[refs[q * n + i][...] for q in range(4)])
            for q in range(3):
                outs[q * n + i][...] = res[q]

    return pl.pallas_call(body, name="adamw_small", out_shape=[jax.ShapeDtypeStruct(w.shape, F32) for w in ws] * 3,
                          compiler_params=_cp())(*ws, *gs, *ms, *vs)


def _adamw(w, g, m, v, *, name, tm=128):
    r, c = w.shape
    tm = r if (r < tm or r % tm) else tm

    def body(w_ref, g_ref, m_ref, v_ref, d_ref, nm_ref, nv_ref):
        d_ref[...], nm_ref[...], nv_ref[...] = _adamw_math(w_ref[...], g_ref[...], m_ref[...], v_ref[...])

    blk = pl.BlockSpec((tm, c), lambda i: (i, 0))
    sd = jax.ShapeDtypeStruct((r, c), F32)
    return pl.pallas_call(
        body, name=name, grid=(r // tm,), in_specs=[blk] * 4, out_specs=[blk] * 3, out_shape=[sd] * 3,
        compiler_params=_cp(("parallel",)),
    )(w, g, m, v)


ANY = pl.BlockSpec(memory_space=pl.ANY)
N_CHIPS = 4


def _chip_of(k, x, y):
    return (x ^ (k >> 1), y ^ (k & 1))


def _all_gather_small(shard):
    r, c = shard.shape
    hr = r // 2

    def body(sh_ref, out_ref, send_sems, recv_sems, local_sem):
        x, y, cc = lax.axis_index("x"), lax.axis_index("y"), lax.axis_index("c")

        def half(px, py, pc):
            return out_ref.at[2 * px + py, pl.ds(pc * hr, hr), :]

        def copy(k, px, py, pc, to, src=None):
            return pltpu.make_async_remote_copy(
                src_ref=half(px, py, pc) if src is None else src, dst_ref=half(px, py, pc),
                send_sem=send_sems.at[k], recv_sem=recv_sems.at[k], device_id=to, device_id_type=MESH)

        mine = pltpu.make_async_copy(sh_ref, out_ref.at[2 * x + y], local_sem)
        mine.start()
        chips = [_chip_of(k, x, y) for k in (1, 2, 3)]
        first = [copy(j, x, y, cc, (*chip, cc), src=sh_ref.at[pl.ds(cc * hr, hr), :]) for j, chip in enumerate(chips)]
        for cp in first:
            cp.start()
        passed = [copy(3 + j, *chip, cc, (x, y, 1 - cc)) for j, chip in enumerate(chips)]
        for j, chip in enumerate(chips):
            copy(j, *chip, cc, (x, y, cc)).wait_recv()
            passed[j].start()
        for j, chip in enumerate(chips):
            copy(3 + j, *chip, 1 - cc, (x, y, cc)).wait_recv()
        for cp in first + passed:
            cp.wait_send()
        mine.wait()

    return pl.pallas_call(
        body, name="all_gather_small", in_specs=[ANY], out_specs=ANY,
        out_shape=jax.ShapeDtypeStruct((N_CHIPS, r, c), shard.dtype),
        scratch_shapes=[pltpu.SemaphoreType.DMA((6,)), pltpu.SemaphoreType.DMA((6,)), pltpu.SemaphoreType.DMA],
    )(shard)


def _cast_bf16(a, *, name, tm=512):
    n, r, c = a.shape
    tm = _tile(r, tm) if r % 128 == 0 else r

    def body(a_ref, o_ref):
        o_ref[...] = a_ref[...].astype(BF16)

    blk = pl.BlockSpec((1, tm, c), lambda i, j: (i, j, 0))
    return pl.pallas_call(body, name=name, grid=(n, r // tm), in_specs=[blk], out_specs=blk,
                          out_shape=jax.ShapeDtypeStruct(a.shape, BF16), compiler_params=_cp(("parallel", "parallel")))(a)


def _pair_exchange(g16, hr):
    n, r, c = g16.shape

    def body(g_ref, out_ref, send_sem, recv_sem):
        x, y, cc = lax.axis_index("x"), lax.axis_index("y"), lax.axis_index("c")
        cp = pltpu.make_async_remote_copy(
            src_ref=g_ref.at[:, pl.ds((1 - cc) * hr, hr), :], dst_ref=out_ref, send_sem=send_sem, recv_sem=recv_sem,
            device_id=(x, y, 1 - cc), device_id_type=MESH)
        cp.start()
        cp.wait()

    return pl.pallas_call(
        body, name="grad_pair_exchange", in_specs=[ANY], out_specs=ANY,
        out_shape=jax.ShapeDtypeStruct((n, hr, c), g16.dtype),
        scratch_shapes=[pltpu.SemaphoreType.DMA, pltpu.SemaphoreType.DMA],
    )(g16)


def _pair_add(g, recv, half_idx, hr, *, tm=384):
    n, r, c = g.shape
    nt = hr // tm

    def body(hi_ref, g_ref, r_ref, o32_ref, o16_ref):
        v = g_ref[...] + r_ref[...].astype(F32)
        o32_ref[...] = v
        o16_ref[...] = v.astype(BF16)

    gs = pltpu.PrefetchScalarGridSpec(
        num_scalar_prefetch=1, grid=(n, nt),
        in_specs=[pl.BlockSpec((1, tm, c), lambda i, j, hi: (i, hi[0] * nt + j, 0)),
                  pl.BlockSpec((1, tm, c), lambda i, j, hi: (i, j, 0))],
        out_specs=[pl.BlockSpec((1, tm, c), lambda i, j, hi: (i, j, 0))] * 2)
    return pl.pallas_call(
        body, name="grad_pair_add", grid_spec=gs,
        out_shape=[jax.ShapeDtypeStruct((n, hr, c), F32), jax.ShapeDtypeStruct((n, hr, c), BF16)],
        compiler_params=_cp(("parallel", "parallel")),
    )(half_idx, g, recv)


def _chip_exchange(p16):
    n, hr, c = p16.shape

    def body(p_ref, out_ref, send_sems, recv_sems):
        x, y, cc = lax.axis_index("x"), lax.axis_index("y"), lax.axis_index("c")
        cps = []
        for j, k in enumerate((1, 2, 3)):
            px, py = _chip_of(k, x, y)
            cps.append(pltpu.make_async_remote_copy(
                src_ref=p_ref.at[2 * px + py], dst_ref=out_ref.at[j], send_sem=send_sems.at[j], recv_sem=recv_sems.at[j],
                device_id=(px, py, cc), device_id_type=MESH))
        for cp in cps:
            cp.start()
        for cp in cps:
            cp.wait()

    return pl.pallas_call(
        body, name="grad_chip_exchange", in_specs=[ANY], out_specs=ANY,
        out_shape=jax.ShapeDtypeStruct((3, hr, c), p16.dtype),
        scratch_shapes=[pltpu.SemaphoreType.DMA((3,)), pltpu.SemaphoreType.DMA((3,))],
    )(p16)


def _chip_add(p32, recv, chip_idx, *, tm=384):
    n, hr, c = p32.shape

    def body(ci_ref, p_ref, r_ref, o_ref):
        o_ref[...] = ((p_ref[0] + r_ref[0].astype(F32)) + r_ref[1].astype(F32)) + r_ref[2].astype(F32)

    gs = pltpu.PrefetchScalarGridSpec(
        num_scalar_prefetch=1, grid=(hr // tm,),
        in_specs=[pl.BlockSpec((1, tm, c), lambda j, ci: (ci[0], j, 0)), pl.BlockSpec((3, tm, c), lambda j, ci: (0, j, 0))],
        out_specs=pl.BlockSpec((tm, c), lambda j, ci: (j, 0)))
    return pl.pallas_call(
        body, name="grad_chip_add", grid_spec=gs, out_shape=jax.ShapeDtypeStruct((hr, c), F32),
        compiler_params=_cp(("parallel",)),
    )(chip_idx, p32, recv)


def _pair_gather(f):
    hr, c = f.shape

    def body(f_ref, out_ref, send_sem, recv_sem, local_sem):
        x, y, cc = lax.axis_index("x"), lax.axis_index("y"), lax.axis_index("c")
        mine = pltpu.make_async_copy(f_ref, out_ref.at[pl.ds(cc * hr, hr), :], local_sem)
        mine.start()
        cp = pltpu.make_async_remote_copy(
            src_ref=f_ref, dst_ref=out_ref.at[pl.ds(cc * hr, hr), :], send_sem=send_sem, recv_sem=recv_sem,
            device_id=(x, y, 1 - cc), device_id_type=MESH)
        cp.start()
        cp.wait()
        mine.wait()

    return pl.pallas_call(
        body, name="grad_pair_gather", in_specs=[ANY], out_specs=ANY,
        out_shape=jax.ShapeDtypeStruct((2 * hr, c), f.dtype),
        scratch_shapes=[pltpu.SemaphoreType.DMA, pltpu.SemaphoreType.DMA, pltpu.SemaphoreType.DMA],
    )(f)


def _all_reduce_small(buf):
    r, c = buf.shape

    def body(b_ref, out_ref, gat, send_sems, recv_sems):
        x, y, cc = lax.axis_index("x"), lax.axis_index("y"), lax.axis_index("c")
        me = 4 * x + 2 * y + cc
        gat[me] = b_ref[...]
        cps = []
        for k in range(1, 8):
            px, py, pc = x ^ (k >> 2), y ^ ((k >> 1) & 1), cc ^ (k & 1)
            cps.append(pltpu.make_async_remote_copy(
                src_ref=b_ref, dst_ref=gat.at[me], send_sem=send_sems.at[k - 1], recv_sem=recv_sems.at[k - 1],
                device_id=(px, py, pc), device_id_type=MESH))
        for cp in cps:
            cp.start()
        for cp in cps:
            cp.wait()
        acc = gat[0]
        for d in range(1, 8):
            acc = acc + gat[d]
        out_ref[...] = acc

    vm = pl.BlockSpec(memory_space=pltpu.VMEM)
    return pl.pallas_call(
        body, name="all_reduce_small", in_specs=[vm], out_specs=vm, out_shape=jax.ShapeDtypeStruct((r, c), F32),
        scratch_shapes=[pltpu.VMEM((8, r, c), F32), pltpu.SemaphoreType.DMA((7,)), pltpu.SemaphoreType.DMA((7,))],
        compiler_params=pltpu.CompilerParams(vmem_limit_bytes=VMEM_LIMIT),
    )(buf)


def _pipe(fn, ins, outs, tr):
    shape = ins[0].shape
    lead, (r, c) = shape[:-2], shape[-2:]
    assert len(lead) <= 1 and r % tr == 0
    nr = r // tr
    n = nr * (lead[0] if lead else 1)
    ni, no = len(ins), len(outs)

    def blk(ref, step):
        rows = pl.ds((step % nr) * tr, tr)
        return ref.at[step // nr, rows, :] if lead else ref.at[rows, :]

    def scoped(*bufs):
        ibufs, obufs, isem, osem = bufs[:ni], bufs[ni:ni + no], bufs[-2], bufs[-1]

        def in_copy(q, step, slot):
            return pltpu.make_async_copy(blk(ins[q], step), ibufs[q].at[slot], isem.at[q, slot])

        def out_copy(q, step, slot):
            return pltpu.make_async_copy(obufs[q].at[slot], blk(outs[q], step), osem.at[q, slot])

        for step in range(min(nbuf - 1, n)):
            for q in range(ni):
                in_copy(q, step, step % nbuf).start()
        for step in range(n):
            slot = step % nbuf
            if step + nbuf - 1 < n:
                for q in range(ni):
                    in_copy(q, step + nbuf - 1, (step + nbuf - 1) % nbuf).start()
            for q in range(ni):
                in_copy(q, step, slot).wait()
            if step >= nbuf:
                for q in range(no):
                    out_copy(q, step - nbuf, slot).wait()
            res = fn(*[ibufs[q][slot] for q in range(ni)])
            for q in range(no):
                obufs[q][slot] = res[q].astype(obufs[q].dtype)
                out_copy(q, step, slot).start()
        for step in range(max(n - nbuf, 0), n):
            for q in range(no):
                out_copy(q, step, step % nbuf).wait()

    assert n <= 8
    nbuf = min(n, 4)
    pl.run_scoped(scoped, *[pltpu.VMEM((nbuf, tr, c), q.dtype) for q in ins], *[pltpu.VMEM((nbuf, tr, c), q.dtype) for q in outs],
                  pltpu.SemaphoreType.DMA((ni, nbuf)), pltpu.SemaphoreType.DMA((no, nbuf)))


W_IN_PAD = 2304
BIG = ("w_in", "w_attn_o", "w_ssd_o", "w_out", "w_up", "w_down")
BIG_SHAPE = dict(w_in=(D_MODEL, W_IN_PAD), w_attn_o=(Q_DIM // 4, D_MODEL), w_ssd_o=(D_INNER // 4, D_MODEL),
                 w_out=(D_MODEL // 4, D_MODEL), w_up=(D_MODEL, 2 * D_FF // 4), w_down=(D_FF // 4, D_MODEL))
BIG_TR = dict(w_in=128, w_attn_o=128, w_ssd_o=128, w_out=128, w_up=128, w_down=176)
X_FIRST = dict(w_in=True, w_attn_o=True, w_ssd_o=False, w_out=True, w_up=False, w_down=False)


def _neighbours(x, y, x_first):
    xn, yn = (1 - x, y), (x, 1 - y)
    n1, n2 = (xn, yn) if x_first else (yn, xn)
    slot = lambda ch: 2 * ch[0] + ch[1]
    return n1, n2, slot(n1), slot(n2), slot((1 - x, 1 - y))


def _gather_big(shards):
    nt = len(BIG)

    def body(*refs):
        sh, out = refs[:nt], refs[nt:2 * nt]
        send_sems, recv_sems = refs[2 * nt:]
        x, y, cc = lax.axis_index("x"), lax.axis_index("y"), lax.axis_index("c")
        me = 2 * x + y
        sib = (x, y, 1 - cc)
        for t, n in enumerate(BIG):
            _pipe(lambda v: (v,), [sh[t]], [out[t].at[me]], BIG_TR[n])

        def copy(t, k, slot, pc, to):
            hr = BIG_SHAPE[BIG[t]][0] // 2
            ref = out[t].at[slot, pl.ds(pc * hr, hr), :]
            return pltpu.make_async_remote_copy(src_ref=ref, dst_ref=ref, send_sem=send_sems.at[6 * t + k],
                                                recv_sem=recv_sems.at[6 * t + k], device_id=to, device_id_type=MESH)

        started = []

        def start(cp):
            cp.start()
            started.append(cp)

        geo = [_neighbours(x, y, X_FIRST[n]) for n in BIG]
        for t in range(nt):
            n1, n2, _, _, _ = geo[t]
            start(copy(t, 0, me, cc, (*n1, cc)))
            start(copy(t, 1, me, cc, (*n2, cc)))
        for t in range(nt):
            n1, n2, s1, s2, sd = geo[t]
            copy(t, 0, s1, cc, sib).wait_recv()
            start(copy(t, 2, s1, cc, (*n2, cc)))
            start(copy(t, 3, s1, cc, sib))
            copy(t, 1, s2, cc, sib).wait_recv()
            start(copy(t, 4, s2, cc, sib))
        for t in range(nt):
            _, _, s1, s2, sd = geo[t]
            copy(t, 2, sd, cc, sib).wait_recv()
            start(copy(t, 5, sd, cc, sib))
        for t in range(nt):
            _, _, s1, s2, sd = geo[t]
            copy(t, 3, s1, 1 - cc, sib).wait_recv()
            copy(t, 4, s2, 1 - cc, sib).wait_recv()
            copy(t, 5, sd, 1 - cc, sib).wait_recv()
        for cp in started:
            cp.wait_send()

    return pl.pallas_call(
        body, name="gather_big", in_specs=[ANY] * nt, out_specs=[ANY] * nt,
        out_shape=[jax.ShapeDtypeStruct((N_CHIPS, *BIG_SHAPE[n]), BF16) for n in BIG],
        scratch_shapes=[pltpu.SemaphoreType.DMA((6 * nt,)), pltpu.SemaphoreType.DMA((6 * nt,))],
        compiler_params=pltpu.CompilerParams(vmem_limit_bytes=VMEM_LIMIT),
    )(*shards)


def _reduce_big(grads):
    nt = len(BIG)
    nw = 7

    def body(*refs):
        g = refs[:nt]
        fin = refs[nt:2 * nt]
        work = refs[2 * nt:2 * nt + nw * nt]
        send_sems, recv_sems = refs[2 * nt + nw * nt:]
        x, y, cc = lax.axis_index("x"), lax.axis_index("y"), lax.axis_index("c")
        me = 2 * x + y
        sib = (x, y, 1 - cc)
        started = []

        def rcopy(t, k, src, dst, to):
            cp = pltpu.make_async_remote_copy(src_ref=src, dst_ref=dst, send_sem=send_sems.at[5 * t + k],
                                              recv_sem=recv_sems.at[5 * t + k], device_id=to, device_id_type=MESH)
            return cp

        def start(cp):
            cp.start()
            started.append(cp)

        geo = [_neighbours(x, y, X_FIRST[n]) for n in BIG]
        hrs = [BIG_SHAPE[n][0] // 2 for n in BIG]
        wk = lambda t: work[nw * t:nw * (t + 1)]
        one = lambda ref, slot: ref.at[pl.ds(slot, 1)]
        for t in range(nt):
            recv_a = wk(t)[0]
            start(rcopy(t, 0, g[t].at[:, pl.ds((1 - cc) * hrs[t], hrs[t]), :], recv_a, sib))
        for t, n in enumerate(BIG):
            recv_a, p32, p16, r1, qme, qs2, r2 = wk(t)
            n1, n2, s1, s2, sd = geo[t]
            rcopy(t, 0, recv_a, recv_a, sib).wait_recv()
            _pipe(lambda a, b: (a + b, a + b), [g[t].at[:, pl.ds(cc * hrs[t], hrs[t]), :], recv_a], [p32, p16], BIG_TR[n])
            start(rcopy(t, 1, one(p16, s1), one(r1, 0), (*n1, cc)))
            start(rcopy(t, 2, one(p16, sd), one(r1, 1), (*n1, cc)))
        for t, n in enumerate(BIG):
            recv_a, p32, p16, r1, qme, qs2, r2 = wk(t)
            n1, n2, s1, s2, sd = geo[t]
            rcopy(t, 1, one(r1, 0), one(r1, 0), sib).wait_recv()
            rcopy(t, 2, one(r1, 1), one(r1, 1), sib).wait_recv()
            _pipe(lambda a, b: (a + b.astype(F32),), [one(p32, s2), one(r1, 1)], [qs2], BIG_TR[n])
            start(rcopy(t, 3, qs2, r2, (*n2, cc)))
            _pipe(lambda a, b: (a + b.astype(F32),), [one(p32, me), one(r1, 0)], [qme], BIG_TR[n])
        for t, n in enumerate(BIG):
            recv_a, p32, p16, r1, qme, qs2, r2 = wk(t)
            rcopy(t, 3, r2, r2, sib).wait_recv()
            mine = fin[t].at[pl.ds(cc * hrs[t], hrs[t]), :]
            _pipe(lambda a, b: (a + b.astype(F32),), [qme.at[0], r2.at[0]], [mine], BIG_TR[n])
            start(rcopy(t, 4, mine, mine, sib))
        for t in range(nt):
            other = fin[t].at[pl.ds((1 - cc) * hrs[t], hrs[t]), :]
            rcopy(t, 4, other, other, sib).wait_recv()
        for cp in started:
            cp.wait_send()

    outs = [jax.ShapeDtypeStruct(BIG_SHAPE[n], F32) for n in BIG]
    for n in BIG:
        r, c = BIG_SHAPE[n]
        hr = r // 2
        outs += [jax.ShapeDtypeStruct((4, hr, c), F32), jax.ShapeDtypeStruct((4, hr, c), F32),
                 jax.ShapeDtypeStruct((4, hr, c), BF16), jax.ShapeDtypeStruct((2, hr, c), BF16),
                 jax.ShapeDtypeStruct((1, hr, c), F32), jax.ShapeDtypeStruct((1, hr, c), BF16),
                 jax.ShapeDtypeStruct((1, hr, c), BF16)]
    res = pl.pallas_call(
        body, name="reduce_big", in_specs=[ANY] * nt, out_specs=[ANY] * len(outs), out_shape=outs,
        scratch_shapes=[pltpu.SemaphoreType.DMA((5 * nt,)), pltpu.SemaphoreType.DMA((5 * nt,))],
        compiler_params=pltpu.CompilerParams(vmem_limit_bytes=VMEM_LIMIT),
    )(*grads)
    return res[:nt]


WHOLE_X_FIRST = dict(w_ssd_o=True, w_out=False, w_attn_o=False)


def _quarters(names):
    out = []
    for i, n in enumerate(names):
        if n in WHOLE_X_FIRST:
            h = BIG_SHAPE[n][0] // 2
            out.append((i, WHOLE_X_FIRST[n], 0, h, 128))
        else:
            q = BIG_SHAPE[n][0] // 4
            tr = 128 if q % 128 == 0 else q
            out += [(i, True, 0, q, tr), (i, False, q, q, tr)]
    return out


class _GatherJob:
    def __init__(self, names, shards, at=None):
        self.names = names
        self.at = at
        self.inputs = list(shards)
        self.out_shapes = [jax.ShapeDtypeStruct((N_CHIPS, *BIG_SHAPE[n]), BF16) for n in names]
        self.ent = _quarters(names)
        self.scratch = [pltpu.SemaphoreType.DMA((6 * len(self.ent),)), pltpu.SemaphoreType.DMA((6 * len(self.ent),))]

    def phases(self, sh, out, scr):
        send_sems, recv_sems = scr
        names, ent = self.names, self.ent
        x, y, cc = lax.axis_index("x"), lax.axis_index("y"), lax.axis_index("c")
        me = 2 * x + y
        sib = (x, y, 1 - cc)
        geo = [_neighbours(x, y, e[1]) for e in ent]
        started = []

        def copy(i, k, slot, pc, to):
            arr, _, roff, rows, _ = ent[i]
            hr = BIG_SHAPE[names[arr]][0] // 2
            ref = out[arr].at[slot, pl.ds(pc * hr + roff, rows), :]
            return pltpu.make_async_remote_copy(src_ref=ref, dst_ref=ref, send_sem=send_sems.at[6 * i + k],
                                                recv_sem=recv_sems.at[6 * i + k], device_id=to, device_id_type=MESH)

        def start(*a):
            copy(*a).start()
            started.append(a)

        def p0():
            for t, n in enumerate(names):
                _pipe(lambda v: (v,), [sh[t]], [out[t].at[me]], BIG_TR[n])
            for i in range(len(ent)):
                n1, n2, _, _, _ = geo[i]
                start(i, 0, me, cc, (*n1, cc))
                start(i, 1, me, cc, (*n2, cc))

        def p1():
            for i in range(len(ent)):
                n1, n2, s1, s2, sd = geo[i]
                copy(i, 0, s1, cc, sib).wait_recv()
                start(i, 2, s1, cc, (*n2, cc))
                start(i, 3, s1, cc, sib)
                copy(i, 1, s2, cc, sib).wait_recv()
                start(i, 4, s2, cc, sib)

        def p2():
            for i in range(len(ent)):
                sd = geo[i][4]
                copy(i, 2, sd, cc, sib).wait_recv()
                start(i, 5, sd, cc, sib)

        def p3():
            for i in range(len(ent)):
                _, _, s1, s2, sd = geo[i]
                copy(i, 3, s1, 1 - cc, sib).wait_recv()
                copy(i, 4, s2, 1 - cc, sib).wait_recv()
                copy(i, 5, sd, 1 - cc, sib).wait_recv()
            for a in started:
                copy(*a).wait_send()

        return [p0, p1, p2, p3]


class _ReduceJob:
    NW = 7

    def __init__(self, names, grads, at=None):
        self.names = names
        self.at = at
        self.inputs = list(grads)
        self.ent = _quarters(names)
        self.out_shapes = [jax.ShapeDtypeStruct(BIG_SHAPE[n], F32) for n in names]
        for arr, _, _, rows, _ in self.ent:
            c = BIG_SHAPE[names[arr]][1]
            self.out_shapes += [jax.ShapeDtypeStruct((4, rows, c), F32), jax.ShapeDtypeStruct((4, rows, c), F32),
                                jax.ShapeDtypeStruct((4, rows, c), BF16), jax.ShapeDtypeStruct((2, rows, c), BF16),
                                jax.ShapeDtypeStruct((1, rows, c), F32), jax.ShapeDtypeStruct((1, rows, c), BF16),
                                jax.ShapeDtypeStruct((1, rows, c), BF16)]
        self.scratch = [pltpu.SemaphoreType.DMA((5 * len(self.ent),)), pltpu.SemaphoreType.DMA((5 * len(self.ent),))]

    def phases(self, g, outs, scr):
        send_sems, recv_sems = scr
        names, ent, nw = self.names, self.ent, self.NW
        nt = len(names)
        fin, work = outs[:nt], outs[nt:]
        x, y, cc = lax.axis_index("x"), lax.axis_index("y"), lax.axis_index("c")
        me = 2 * x + y
        sib = (x, y, 1 - cc)
        geo = [_neighbours(x, y, e[1]) for e in ent]
        started = []
        wk = lambda i: work[nw * i:nw * (i + 1)]
        one = lambda ref, slot: ref.at[pl.ds(slot, 1)]

        def rows_of(i, pc):
            arr, _, roff, rows, _ = ent[i]
            return pl.ds(pc * (BIG_SHAPE[names[arr]][0] // 2) + roff, rows)

        def rcopy(i, k, src, dst, to):
            return pltpu.make_async_remote_copy(src_ref=src, dst_ref=dst, send_sem=send_sems.at[5 * i + k],
                                                recv_sem=recv_sems.at[5 * i + k], device_id=to, device_id_type=MESH)

        def start(make):
            make().start()
            started.append(make)

        def p0():
            for i, e in enumerate(ent):
                start(lambda i=i, e=e: rcopy(i, 0, g[e[0]].at[:, rows_of(i, 1 - cc), :], wk(i)[0], sib))

        def p1():
            for i, e in enumerate(ent):
                recv_a, p32, p16, r1 = wk(i)[:4]
                n1, n2, s1, s2, sd = geo[i]
                rcopy(i, 0, recv_a, recv_a, sib).wait_recv()
                _pipe(lambda a, b: (a + b, a + b), [g[e[0]].at[:, rows_of(i, cc), :], recv_a], [p32, p16], e[4])
                start(lambda i=i, s1=s1, n1=n1: rcopy(i, 1, one(wk(i)[2], s1), one(wk(i)[3], 0), (*n1, cc)))
                start(lambda i=i, sd=sd, n1=n1: rcopy(i, 2, one(wk(i)[2], sd), one(wk(i)[3], 1), (*n1, cc)))

        def p2():
            for i, e in enumerate(ent):
                _, p32, _, r1, qme, qs2, r2 = wk(i)
                n1, n2, s1, s2, sd = geo[i]
                rcopy(i, 1, one(r1, 0), one(r1, 0), sib).wait_recv()
                rcopy(i, 2, one(r1, 1), one(r1, 1), sib).wait_recv()
                _pipe(lambda a, b, c, d: (a + b.astype(F32), c + d.astype(F32)),
                      [one(p32, s2), one(r1, 1), one(p32, me), one(r1, 0)], [qs2, qme], e[4])
                start(lambda i=i, n2=n2: rcopy(i, 3, wk(i)[5], wk(i)[6], (*n2, cc)))

        def p3():
            for i, e in enumerate(ent):
                qme, r2 = wk(i)[4], wk(i)[6]
                rcopy(i, 3, r2, r2, sib).wait_recv()
                mine = fin[e[0]].at[rows_of(i, cc), :]
                _pipe(lambda a, b: (a + b.astype(F32),), [qme.at[0], r2.at[0]], [mine], e[4])
                start(lambda i=i, e=e: rcopy(i, 4, fin[e[0]].at[rows_of(i, cc), :], fin[e[0]].at[rows_of(i, cc), :], sib))

        def p4():
            for i, e in enumerate(ent):
                other = fin[e[0]].at[rows_of(i, 1 - cc), :]
                rcopy(i, 4, other, other, sib).wait_recv()
            for make in started:
                make().wait_send()

        return [p0, p1, p2, p3, p4]


def _run_job(job, name):
    ni, no = len(job.inputs), len(job.out_shapes)

    def body(*refs):
        for ph in job.phases(refs[:ni], refs[ni:ni + no], refs[ni + no:]):
            ph()

    return pl.pallas_call(
        body, name=name, in_specs=[ANY] * ni, out_specs=[ANY] * no, out_shape=job.out_shapes, scratch_shapes=job.scratch,
        compiler_params=pltpu.CompilerParams(vmem_limit_bytes=VMEM_LIMIT),
    )(*job.inputs)


def _hosted(body, *, name, grid, in_specs, out_specs, out_shape, scratch_shapes, args, sem, side=None):
    if side is None:
        return pl.pallas_call(body, name=name, grid=grid, in_specs=in_specs, out_specs=out_specs, out_shape=out_shape,
                              scratch_shapes=scratch_shapes, compiler_params=_cp(sem))(*args), None
    job = side
    ni, no, ns = len(in_specs), len(out_specs), len(scratch_shapes)
    ji, jo = len(job.inputs), len(job.out_shapes)
    n_steps = 1
    for extent in grid:
        n_steps *= extent

    def wrapped(*refs):
        own_in, refs = refs[:ni], refs[ni:]
        job_in, refs = refs[:ji], refs[ji:]
        own_out, refs = refs[:no], refs[no:]
        job_out, refs = refs[:jo], refs[jo:]
        own_scr, job_scr = refs[:ns], refs[ns:]
        step = 0
        for d, extent in enumerate(grid):
            step = step * extent + pl.program_id(d)
        phases = job.phases(job_in, job_out, job_scr)
        steps = [min(int(f * n_steps), n_steps - 1) for f in job.at] + [n_steps - 1]
        assert len(steps) == len(phases) and steps == sorted(steps)
        for at, ph in zip(steps, phases):
            pl.when(step == at)(ph)
        body(*own_in, *own_out, *own_scr)

    res = pl.pallas_call(
        wrapped, name=name, grid=grid, in_specs=list(in_specs) + [ANY] * ji, out_specs=list(out_specs) + [ANY] * jo,
        out_shape=list(out_shape) + list(job.out_shapes), scratch_shapes=list(scratch_shapes) + list(job.scratch),
        compiler_params=_cp(("arbitrary",) * len(grid)),
    )(*args, *job.inputs)
    return res[:no], res[no:]


def _proj_dw(xn, dproj_sh, *, tm=512, tk=1024):
    s, d = xn.shape
    tk = _tile(s, tk)
    nk = s // tk

    def body(a_ref, b_ref, o_ref, acc):
        kk = pl.program_id(2)
        part = _dot_tn(a_ref[...], b_ref[0])

        @pl.when(kk == 0)
        def _():
            acc[...] = part

        @pl.when(kk > 0)
        def _():
            acc[...] += part

        @pl.when(kk == nk - 1)
        def _():
            o_ref[0] = acc[...]

    return pl.pallas_call(
        body, name="proj_dw", grid=(N_CHIPS, d // tm, nk),
        in_specs=[pl.BlockSpec((tk, tm), lambda j, i, q: (q, i)), pl.BlockSpec((1, tk, W_IN_PAD), lambda j, i, q: (j, q, 0))],
        out_specs=pl.BlockSpec((1, tm, W_IN_PAD), lambda j, i, q: (j, i, 0)),
        out_shape=jax.ShapeDtypeStruct((N_CHIPS, d, W_IN_PAD), F32), scratch_shapes=[pltpu.VMEM((tm, W_IN_PAD), F32)],
        compiler_params=_cp(("parallel", "parallel", "arbitrary")),
    )(xn, dproj_sh)


def _proj_dx(dproj_sh, w_sh, *, tm=1024, side=None):
    s = dproj_sh.shape[1]
    d = w_sh.shape[1]
    tm = _tile(s, tm)

    def body(a_ref, b_ref, o_ref, acc):
        kk = pl.program_id(1)
        part = _dot_nt(a_ref[0], b_ref[0])

        @pl.when(kk == 0)
        def _():
            acc[...] = part

        @pl.when(kk > 0)
        def _():
            acc[...] += part

        @pl.when(kk == N_CHIPS - 1)
        def _():
            o_ref[...] = acc[...]

    own, extra = _hosted(
        body, name="proj_dx", grid=(s // tm, N_CHIPS),
        in_specs=[pl.BlockSpec((1, tm, W_IN_PAD), lambda i, q: (q, i, 0)), pl.BlockSpec((1, d, W_IN_PAD), lambda i, q: (q, 0, 0))],
        out_specs=[pl.BlockSpec((tm, d), lambda i, q: (i, 0))],
        out_shape=[jax.ShapeDtypeStruct((s, d), F32)], scratch_shapes=[pltpu.VMEM((tm, d), F32)],
        args=(dproj_sh, w_sh), sem=("parallel", "arbitrary"), side=side)
    return own[0] if side is None else (own[0], extra)


def _up_dx(dup, w_sh, *, tm=1024):
    s = dup.shape[1]
    d, wsh = w_sh.shape[1:]
    tm = _tile(s, tm)

    def body(a_ref, b_ref, o_ref, acc):
        kk = pl.program_id(1)
        part = _dot_nt(a_ref[0], b_ref[0])

        @pl.when(kk == 0)
        def _():
            acc[...] = part

        @pl.when(kk > 0)
        def _():
            acc[...] += part

        @pl.when(kk == N_CHIPS - 1)
        def _():
            o_ref[...] = acc[...]

    return pl.pallas_call(
        body, name="up_dx", grid=(s // tm, N_CHIPS),
        in_specs=[pl.BlockSpec((1, tm, wsh), lambda i, q: (q >> 1, i, q & 1)), pl.BlockSpec((1, d, wsh), lambda i, q: (q, 0, 0))],
        out_specs=pl.BlockSpec((tm, d), lambda i, q: (i, 0)),
        out_shape=jax.ShapeDtypeStruct((s, d), F32), scratch_shapes=[pltpu.VMEM((tm, d), F32)],
        compiler_params=_cp(("parallel", "arbitrary")),
    )(dup, w_sh)


def _up_dw(hn, dup, *, tk=1024):
    s, d = hn.shape
    wsh = 2 * D_FF // N_CHIPS
    tk = _tile(s, tk)
    nk = s // tk

    def body(a_ref, b_ref, o_ref, acc):
        kk = pl.program_id(1)
        part = _dot_tn(a_ref[...], b_ref[0])

        @pl.when(kk == 0)
        def _():
            acc[...] = part

        @pl.when(kk > 0)
        def _():
            acc[...] += part

        @pl.when(kk == nk - 1)
        def _():
            o_ref[0] = acc[...]

    return pl.pallas_call(
        body, name="up_dw", grid=(N_CHIPS, nk),
        in_specs=[pl.BlockSpec((tk, d), lambda j, q: (q, 0)), pl.BlockSpec((1, tk, wsh), lambda j, q: (j >> 1, q, j & 1))],
        out_specs=pl.BlockSpec((1, d, wsh), lambda j, q: (j, 0, 0)),
        out_shape=jax.ShapeDtypeStruct((N_CHIPS, d, wsh), F32), scratch_shapes=[pltpu.VMEM((d, wsh), F32)],
        compiler_params=_cp(("parallel", "arbitrary")),
    )(hn, dup)


BIG_ROWS =(IN_DIM // 4, Q_DIM // 4, D_INNER // 4, D_MODEL // 4, 2 * D_FF // 4, D_FF // 4)
PACK_ROWS = 5376


def _pack_shards(parts):
    rows = [p.reshape(-1, D_MODEL) for p in parts]
    pad = PACK_ROWS - sum(BIG_ROWS)
    return jnp.concatenate(rows + [jnp.zeros((pad, D_MODEL), rows[0].dtype)], axis=0)


def _unpack_shards(buf):
    out, off = [], 0
    for n in BIG_ROWS:
        out.append(buf[off:off + n])
        off += n
    return out


def _permute_cols_in(w):
    pad = jnp.zeros((w.shape[0], PW - IN_DIM), w.dtype)
    return jnp.concatenate([w[:, :6656], w[:, 6688:], w[:, 6656:6688], pad], axis=1)


def _unpermute_cols_in(g):
    return jnp.concatenate([g[:, :6656], g[:, O_DT:O_DT + 32], g[:, 6656:O_DT]], axis=1)


SMALL = ("norm1_w", "b_gate", "attn_sinks", "ssd_conv_b", "dt_bias", "a_log", "d_skip", "ssd_norm_w", "norm2_w",
         "ffn_conv_b", "final_norm_w", "ssd_conv_w", "ffn_conv_w")


def _pad128(v):
    v = v.reshape(-1)
    return jnp.pad(v, (0, (-v.shape[0]) % 128))


def _pack_small(parts):
    flat = jnp.concatenate([_pad128(p) for p in parts])
    flat = jnp.pad(flat, (0, (-flat.shape[0]) % 1024))
    return flat.reshape(-1, 128)


def _unpack_small(buf, shapes):
    flat, out, off = buf.reshape(-1), [], 0
    for shp in shapes:
        n = 1
        for q in shp:
            n *= q
        out.append(flat[off:off + n].reshape(shp))
        off += n + (-n) % 128
    return out


def _vec128(v):
    return jnp.pad(v.reshape(1, -1), ((0, 0), (0, 128 - v.shape[-1])))


def kernel(x, norm1_w, w_in, b_gate, attn_sinks, w_attn_o, ssd_conv_w, ssd_conv_b, dt_bias, a_log, d_skip, ssd_norm_w, w_ssd_o, w_out, norm2_w, w_up, ffn_conv_w, ffn_conv_b, w_down, final_norm_w, loss_target, m_norm1_w, m_w_in, m_b_gate, m_attn_sinks, m_w_attn_o, m_ssd_conv_w, m_ssd_conv_b, m_dt_bias, m_a_log, m_d_skip, m_ssd_norm_w, m_w_ssd_o, m_w_out, m_norm2_w, m_w_up, m_ffn_conv_w, m_ffn_conv_b, m_w_down, m_final_norm_w, v_norm1_w, v_w_in, v_b_gate, v_attn_sinks, v_w_attn_o, v_ssd_conv_w, v_ssd_conv_b, v_dt_bias, v_a_log, v_d_skip, v_ssd_norm_w, v_w_ssd_o, v_w_out, v_norm2_w, v_w_up, v_ffn_conv_w, v_ffn_conv_b, v_w_down, v_final_norm_w):
    ix, iy, ic = lax.axis_index("x"), lax.axis_index("y"), lax.axis_index("c")
    chip = 2 * ix + iy
    x2 = x[0]
    tgt = loss_target[0]
    s = x2.shape[0]

    wsh = IN_DIM // N_CHIPS
    big_shards = dict(w_in=jnp.pad(w_in[0], ((0, 0), (0, W_IN_PAD - wsh))), w_attn_o=w_attn_o[0], w_ssd_o=w_ssd_o[0],
                      w_out=w_out[0], w_up=w_up[0], w_down=w_down[0])
    gathered = {}
    (gathered["w_in"],) = _run_job(_GatherJob(("w_in",), [big_shards["w_in"]]), "gather_w_in")
    early, late = ("w_attn_o", "w_ssd_o", "w_out"), ("w_up", "w_down")
    gather_early = _GatherJob(early, [big_shards[n] for n in early], at=(0.0, 0.5, 0.8))
    gather_late = _GatherJob(late, [big_shards[n] for n in late], at=(0.0, 0.55, 0.85))
    gw = gathered["w_in"]
    lo, hi = O_GA - 3 * wsh, O_GA + N_SSD_HEADS - 3 * wsh
    w_in_p = jnp.concatenate([gw[0, :, :wsh], gw[1, :, :wsh], gw[2, :, :wsh], gw[3, :, :lo], gw[3, :, hi:wsh],
                              gw[3, :, lo:hi], jnp.zeros((D_MODEL, PW - IN_DIM), BF16)], axis=1)
    small_sh = _pack_small([ssd_conv_w[0], ffn_conv_w[0]])
    small_all = _all_gather_small(small_sh)
    sc_parts = [_unpack_small(small_all[j], [(4, XBC_DIM // 4), (3, 2 * D_FF // 4)]) for j in range(N_CHIPS)]
    ssd_cw = jnp.concatenate([p[0] for p in sc_parts], axis=1)
    ffn_cw = jnp.concatenate([p[1] for p in sc_parts], axis=1)

    sinks128 = _vec128(attn_sinks)
    dtb128, alog128, dskip128 = _vec128(dt_bias), _vec128(a_log), _vec128(d_skip)

    xn = _rms_fwd(x2, norm1_w, name="norm1_fwd")
    proj, got = _mm(xn, w_in_p, name="proj_fwd", tn=1280, side=gather_early)
    gathered.update(zip(early, got))
    qkvt = proj[:, :O_Z].T
    attn_pre, got = _attn_fwd(qkvt, sinks128, side=gather_late)
    gathered.update(zip(late, got))
    full = {n: gathered[n].reshape(-1, D_MODEL) for n in ("w_attn_o", "w_ssd_o", "w_out", "w_down")}
    full["w_up"] = gathered["w_up"]
    attn = _mm(attn_pre, full["w_attn_o"], name="attn_o_fwd", ta=True)
    xbc = _ssd_conv_fwd(proj, ssd_cw, ssd_conv_b)
    y_ssd, hprev = _ssd_fwd(xbc, proj, dtb128, alog128, dskip128)
    yn = _gate_norm_fwd(y_ssd, proj, ssd_norm_w)
    ssd_out = _mm(yn, full["w_ssd_o"], name="ssd_o_fwd")
    merged = _merge_fwd(proj, b_gate, attn, ssd_out)
    h1 = _mm(merged, full["w_out"], name="out_fwd", resid=x2)
    hn = _rms_fwd(h1, norm2_w, name="norm2_fwd")
    up = _mm(hn, full["w_up"], name="up_fwd")
    act = _ffn_act_fwd(up, ffn_cw, ffn_conv_b)
    h2 = _mm(act, full["w_down"], name="down_fwd", resid=h1, tk=1408)

    dh2, loss_blk, g_final = _loss_bwd(h2, tgt, final_norm_w.reshape(1, -1))
    dact = _mm(dh2, full["w_down"], name="down_dx", tb=True, tn=1408)
    g_down = _mm(act, dh2, name="down_dw", ta=True, tm=1408)
    dup, g_ffn_cw, g_ffn_cb = _ffn_act_bwd(dact, up, ffn_cw, ffn_conv_b)
    dhn = _up_dx(dup, full["w_up"])
    g_up = _up_dw(hn, dup)
    dh1, g_norm2 = _rms_bwd(dhn, h1, norm2_w, dh2, name="norm2_bwd")
    dmerged = _mm(dh1, full["w_out"], name="out_dx", tb=True)
    g_out = _mm(merged, dh1, name="out_dw", ta=True)
    dattn, dssd_out, dga, dgs, g_ba, g_bs = _merge_bwd(dmerged, proj, b_gate, attn, ssd_out)
    dyn = _mm(dssd_out, full["w_ssd_o"], name="ssd_o_dx", tb=True)
    g_ssd_o = _mm(yn, dssd_out, name="ssd_o_dw", ta=True)
    dy_ssd, dz, g_ssd_norm = _gate_norm_bwd(dyn, y_ssd, proj, ssd_norm_w)
    slot = lambda g: g.reshape(N_CHIPS, -1, D_MODEL)
    big_grads = {}
    red = ("w_down", "w_up")
    (dxbc, ddt, dvec), got = _ssd_bwd(xbc, proj, dtb128, alog128, dskip128, hprev, dy_ssd,
                                      side=_ReduceJob(red, [slot(g_down), g_up], at=(0.0, 0.3, 0.8, 0.95)))
    big_grads.update(zip(red, got))
    dxbc_raw, g_ssd_cw, g_ssd_cb = _ssd_conv_bwd(dxbc, proj, ssd_cw, ssd_conv_b)
    dattn_pre = _mm(full["w_attn_o"], dattn, name="attn_o_dx", tb=True)
    g_attn_o = _mm(attn_pre, dattn, name="attn_o_dw")
    red = ("w_out", "w_ssd_o", "w_attn_o")
    (dq, dk, dv, dsk), got = _attn_bwd(qkvt, sinks128, attn_pre, dattn_pre,
                                       side=_ReduceJob(red, [slot(g_out), slot(g_ssd_o), slot(g_attn_o)],
                                                       at=(0.0, 0.2, 0.5, 0.7)))
    big_grads.update(zip(red, got))
    pieces = [dq.T, dk.T, dv.T, dz, dxbc_raw, ddt[:, :N_SSD_HEADS], dga, dgs]
    shards_d, off = [[] for _ in range(N_CHIPS)], 0
    for p in pieces:
        for j in range(N_CHIPS):
            a, b = max(off, j * wsh), min(off + p.shape[1], (j + 1) * wsh)
            if a < b:
                shards_d[j].append(p[:, a - off:b - off])
        off += p.shape[1]
    zpad = jnp.zeros((s, W_IN_PAD - wsh), BF16)
    dproj_sh = jnp.stack([jnp.concatenate(sh + [zpad], axis=1) for sh in shards_d])
    g_in = _proj_dw(xn, dproj_sh)
    dxn, got = _proj_dx(dproj_sh, gathered["w_in"], side=_ReduceJob(("w_in",), [g_in], at=(0.0, 0.3, 0.8, 0.95)))
    big_grads["w_in"] = got[0][:, :wsh]
    dx, g_norm1 = _rms_bwd(dxn, x2, norm1_w, dh1, name="norm1_bwd")


    small_g = dict(
        norm1_w=g_norm1, b_gate=jnp.concatenate([g_ba, g_bs], axis=1), attn_sinks=dsk[0:1, :16], ssd_conv_b=g_ssd_cb,
        dt_bias=dvec[0:1, :32], a_log=dvec[1:2, :32], d_skip=dvec[2:3, :32], ssd_norm_w=g_ssd_norm, norm2_w=g_norm2,
        ffn_conv_b=jnp.concatenate([g_ffn_cb[0], g_ffn_cb[1]], axis=1), final_norm_w=g_final, ssd_conv_w=g_ssd_cw,
        ffn_conv_w=jnp.concatenate([g_ffn_cw[0], g_ffn_cw[1]], axis=1))
    small_buf = _pack_small([small_g[n] for n in SMALL] + [loss_blk])
    small_sum = _all_reduce_small(small_buf)
    small_shapes = [(1, D_MODEL), (1, 2 * D_MODEL), (1, 16), (1, XBC_DIM), (1, 32), (1, 32), (1, 32), (1, D_INNER),
                    (1, D_MODEL), (1, 2 * D_FF), (D_MODEL,), (4, XBC_DIM), (3, 2 * D_FF), (1, 128)]
    small_list = _unpack_small(small_sum, small_shapes)
    loss = small_list[-1][0, 0]
    grads = dict(zip(SMALL, small_list[:-1]))
    grads["ssd_conv_w"] = lax.dynamic_slice_in_dim(grads["ssd_conv_w"], chip * (XBC_DIM // 4), XBC_DIM // 4, axis=1)
    grads["ffn_conv_w"] = lax.dynamic_slice_in_dim(grads["ffn_conv_w"], chip * (2 * D_FF // 4), 2 * D_FF // 4, axis=1)
    grads.update(big_grads)

    weights = dict(norm1_w=norm1_w, w_in=w_in, b_gate=b_gate, attn_sinks=attn_sinks, w_attn_o=w_attn_o, ssd_conv_w=ssd_conv_w,
                   ssd_conv_b=ssd_conv_b, dt_bias=dt_bias, a_log=a_log, d_skip=d_skip, ssd_norm_w=ssd_norm_w, w_ssd_o=w_ssd_o,
                   w_out=w_out, norm2_w=norm2_w, w_up=w_up, ffn_conv_w=ffn_conv_w, ffn_conv_b=ffn_conv_b, w_down=w_down,
                   final_norm_w=final_norm_w)
    ms = dict(norm1_w=m_norm1_w, w_in=m_w_in, b_gate=m_b_gate, attn_sinks=m_attn_sinks, w_attn_o=m_w_attn_o,
              ssd_conv_w=m_ssd_conv_w, ssd_conv_b=m_ssd_conv_b, dt_bias=m_dt_bias, a_log=m_a_log, d_skip=m_d_skip,
              ssd_norm_w=m_ssd_norm_w, w_ssd_o=m_w_ssd_o, w_out=m_w_out, norm2_w=m_norm2_w, w_up=m_w_up,
              ffn_conv_w=m_ffn_conv_w, ffn_conv_b=m_ffn_conv_b, w_down=m_w_down, final_norm_w=m_final_norm_w)
    vs = dict(norm1_w=v_norm1_w, w_in=v_w_in, b_gate=v_b_gate, attn_sinks=v_attn_sinks, w_attn_o=v_w_attn_o,
              ssd_conv_w=v_ssd_conv_w, ssd_conv_b=v_ssd_conv_b, dt_bias=v_dt_bias, a_log=v_a_log, d_skip=v_d_skip,
              ssd_norm_w=v_ssd_norm_w, w_ssd_o=v_w_ssd_o, w_out=v_w_out, norm2_w=v_norm2_w, w_up=v_w_up,
              ffn_conv_w=v_ffn_conv_w, ffn_conv_b=v_ffn_conv_b, w_down=v_w_down, final_norm_w=v_final_norm_w)
    order = list(weights)
    deltas, new_m, new_v = {}, {}, {}
    for n in BIG:
        shp = weights[n].shape
        d_, m_, v_ = _adamw(weights[n][0], grads[n], ms[n][0], vs[n][0], name="adamw_" + n)
        deltas[n], new_m[n], new_v[n] = d_.reshape(shp), m_.reshape(shp), v_.reshape(shp)
    smalls = [n for n in order if n not in BIG]
    as2d = lambda a: a.reshape(-1, a.shape[-1])
    res = _adamw_many(*[[as2d(src[n][0] if src[n].ndim == 3 else src[n]) for n in smalls] for src in (weights, grads, ms, vs)])
    for i, n in enumerate(smalls):
        deltas[n], new_m[n], new_v[n] = (res[q * len(smalls) + i].reshape(weights[n].shape) for q in range(3))
    out_grads = [grads[n].reshape(weights[n].shape) for n in order]
    return (loss, dx[None], *out_grads, *[deltas[n] for n in order], *[new_m[n] for n in order], *[new_v[n] for n in order])
```

```python
import functools

import jax
import jax.numpy as jnp
from jax import lax
from jax.experimental import pallas as pl
from jax.experimental.pallas import tpu as pltpu

F32 = jnp.float32
BF16 = jnp.bfloat16
HI = lax.Precision.HIGHEST

D_MODEL = 1024
Q_DIM = 1024
KV_DIM = 256
D_INNER = 2048
BC_DIM = 512
XBC_DIM = 3072
N_SSD_HEADS = 32
D_FF = 2816
IN_DIM = 8736
BLK = 128
EPS = 1e-5
NEG = -1e30

O_Q, O_K, O_V, O_Z, O_X, O_GA, O_GS, O_DT = 0, 1024, 1280, 1536, 3584, 6656, 7680, 8704
PW = 8960

ADAM_LR, ADAM_B1, ADAM_B2, ADAM_EPS, ADAM_WD, ADAM_STEP = 0.001, 0.9, 0.999, 1e-08, 0.01, 10

VMEM_LIMIT = 52 * 1024 * 1024
MESH = pl.DeviceIdType.MESH


def _cp(sem=None):
    return pltpu.CompilerParams(dimension_semantics=sem, vmem_limit_bytes=VMEM_LIMIT)


def _dot(a, b, prec=None):
    return jnp.dot(a, b, preferred_element_type=F32, precision=prec)


def _dot_nt(a, b, prec=None):
    return lax.dot_general(a, b, (((1,), (1,)), ((), ())), preferred_element_type=F32, precision=prec)


def _dot_tn(a, b, prec=None):
    return lax.dot_general(a, b, (((0,), (0,)), ((), ())), preferred_element_type=F32, precision=prec)


def _sigmoid(x):
    return 0.5 * jnp.tanh(0.5 * x) + 0.5


def _tile(n, want):
    t = min(n, want)
    while n % t:
        t -= 128
    return t


def _mm(a, b, *, name, ta=False, tb=False, out_dtype=F32, resid=None, tm=1024, tn=1024, tk=1024, side=None):
    m, k = (a.shape[1], a.shape[0]) if ta else a.shape
    slots = b.ndim == 3
    if slots:
        n = b.shape[1] if tb else b.shape[0] * b.shape[2]
        tn, tk = (tn, b.shape[2]) if tb else (b.shape[2], tk)
    else:
        n = b.shape[0] if tb else b.shape[1]
    tm, tn, tk = _tile(m, tm), _tile(n, tn), _tile(k, tk)
    nk = k // tk
    dn = (((0 if ta else 1,), (1 if tb else 0,)), ((), ()))

    def body(*refs):
        if resid is None:
            a_ref, b_ref, o_ref, acc = refs
        else:
            a_ref, b_ref, r_ref, o_ref, acc = refs
        kk = pl.program_id(2)
        bv = b_ref[0] if slots else b_ref[...]
        part = lax.dot_general(a_ref[...].astype(BF16), bv.astype(BF16), dn, preferred_element_type=F32)

        @pl.when(kk == 0)
        def _():
            acc[...] = part

        @pl.when(kk > 0)
        def _():
            acc[...] += part

        @pl.when(kk == nk - 1)
        def _():
            r = acc[...]
            if resid is not None:
                r = r + r_ref[...]
            o_ref[...] = r.astype(out_dtype)

    a_spec = pl.BlockSpec((tk, tm), lambda i, j, q: (q, i)) if ta else pl.BlockSpec((tm, tk), lambda i, j, q: (i, q))
    if slots:
        b_spec = (pl.BlockSpec((1, tn, tk), lambda i, j, q: (q, j, 0)) if tb
                  else pl.BlockSpec((1, tk, tn), lambda i, j, q: (j, q, 0)))
    else:
        b_spec = pl.BlockSpec((tn, tk), lambda i, j, q: (j, q)) if tb else pl.BlockSpec((tk, tn), lambda i, j, q: (q, j))
    o_spec = pl.BlockSpec((tm, tn), lambda i, j, q: (i, j))
    ins, specs = [a, b], [a_spec, b_spec]
    if resid is not None:
        ins.append(resid)
        specs.append(o_spec)
    own, extra = _hosted(
        body, name=name, grid=(m // tm, n // tn, nk), in_specs=specs, out_specs=[o_spec],
        out_shape=[jax.ShapeDtypeStruct((m, n), out_dtype)], scratch_shapes=[pltpu.VMEM((tm, tn), F32)],
        args=ins, sem=("parallel", "parallel", "arbitrary"), side=side)
    return own[0] if side is None else (own[0], extra)


def _rms_fwd(x, w, *, name, tm=512, with_t=False):
    s, d = x.shape
    tm = _tile(s, tm)

    def body(x_ref, w_ref, o_ref, *t_ref):
        xv = x_ref[...]
        r = lax.rsqrt(jnp.mean(xv * xv, axis=-1, keepdims=True) + EPS)
        y = (xv * r) * w_ref[...]
        o_ref[...] = y.astype(BF16)
        if with_t:
            t_ref[0][...] = y.T.astype(BF16)

    row = pl.BlockSpec((tm, d), lambda i: (i, 0))
    res = pl.pallas_call(
        body, name=name, grid=(s // tm,), in_specs=[row, pl.BlockSpec((1, d), lambda i: (0, 0))],
        out_specs=[row] + [pl.BlockSpec((d, tm), lambda i: (0, i))] * with_t,
        out_shape=[jax.ShapeDtypeStruct((s, d), BF16)] + [jax.ShapeDtypeStruct((d, s), BF16)] * with_t,
        compiler_params=_cp(("parallel",)),
    )(x, w)
    return res if with_t else res[0]


def _rms_bwd(dy, x, w, resid, *, name, tm=512):
    s, d = x.shape
    tm = _tile(s, tm)

    def body(dy_ref, x_ref, w_ref, r_ref, dx_ref, dw_ref):
        i = pl.program_id(0)
        xv = x_ref[...]
        r = lax.rsqrt(jnp.mean(xv * xv, axis=-1, keepdims=True) + EPS)
        xh = xv * r
        dyv = dy_ref[...]
        g = dyv * w_ref[...]
        dx_ref[...] = r_ref[...] + r * (g - xh * jnp.mean(g * xh, axis=-1, keepdims=True))
        part = jnp.sum(dyv * xh, axis=0, keepdims=True)

        @pl.when(i == 0)
        def _():
            dw_ref[...] = part

        @pl.when(i > 0)
        def _():
            dw_ref[...] += part

    row = pl.BlockSpec((tm, d), lambda i: (i, 0))
    vec = pl.BlockSpec((1, d), lambda i: (0, 0))
    return pl.pallas_call(
        body, name=name, grid=(s // tm,), in_specs=[row, row, vec, row], out_specs=[row, vec],
        out_shape=[jax.ShapeDtypeStruct((s, d), F32), jax.ShapeDtypeStruct((1, d), F32)],
        compiler_params=_cp(("arbitrary",)),
    )(dy, x, w, resid)


def _loss_bwd(h2, tgt, wf, *, tm=512):
    s, d = h2.shape
    tm = _tile(s, tm)

    def body(h_ref, t_ref, w_ref, dh_ref, loss_ref, dw_ref):
        i = pl.program_id(0)
        hv = h_ref[...]
        r = lax.rsqrt(jnp.mean(hv * hv, axis=-1, keepdims=True) + EPS)
        xh = hv * r
        wv = w_ref[...]
        e = xh * wv - t_ref[...]
        lpart = 0.5 * jnp.sum(jnp.mean(e * e, axis=-1, keepdims=True), axis=0, keepdims=True)
        dout = e * (1.0 / d)
        g = dout * wv
        dh_ref[...] = r * (g - xh * jnp.mean(g * xh, axis=-1, keepdims=True))
        part = jnp.sum(dout * xh, axis=0, keepdims=True)
        lrow = jnp.broadcast_to(lpart, (1, 128))

        @pl.when(i == 0)
        def _():
            dw_ref[...] = part
            loss_ref[...] = lrow

        @pl.when(i > 0)
        def _():
            dw_ref[...] += part
            loss_ref[...] += lrow

    row = pl.BlockSpec((tm, d), lambda i: (i, 0))
    vec = pl.BlockSpec((1, d), lambda i: (0, 0))
    return pl.pallas_call(
        body, name="loss_bwd", grid=(s // tm,), in_specs=[row, row, vec],
        out_specs=[row, pl.BlockSpec((1, 128), lambda i: (0, 0)), vec],
        out_shape=[jax.ShapeDtypeStruct((s, d), F32), jax.ShapeDtypeStruct((1, 128), F32),
                   jax.ShapeDtypeStruct((1, d), F32)],
        compiler_params=_cp(("arbitrary",)),
    )(h2, tgt, wf)


def _attn_mask(n):
    si = lax.broadcasted_iota(jnp.int32, (2 * BLK, 4 * BLK), 0)
    qi = lax.broadcasted_iota(jnp.int32, (2 * BLK, 4 * BLK), 1) & (BLK - 1)
    dist = BLK + qi - si
    kpos = n * BLK - BLK + si
    return (dist >= 0) & (dist < BLK) & (kpos >= 0)


def _attn_probs(q_ref, kc_ref, kp_ref, sk_ref, kvh, valid):
    rows = slice(kvh * 64, (kvh + 1) * 64)
    kt = jnp.concatenate([kp_ref[rows, :], kc_ref[rows, :]], axis=1).astype(BF16)
    qt = jnp.concatenate([q_ref[(kvh * 4 + g) * 64:(kvh * 4 + g + 1) * 64, :] for g in range(4)], axis=1).astype(BF16)
    s = _dot_tn(kt, qt) * 0.125
    s = jnp.where(valid, s, NEG)
    head = lax.broadcasted_iota(jnp.int32, (1, 4 * BLK), 1) >> 7
    sink = jnp.zeros((1, 4 * BLK), F32)
    for g in range(4):
        sink = jnp.where(head == g, sk_ref[0:1, kvh * 4 + g:kvh * 4 + g + 1], sink)
    m = jnp.maximum(jnp.max(s, axis=0, keepdims=True), sink)
    p = jnp.where(valid, jnp.exp(s - m), 0.0)
    es = jnp.exp(sink - m)
    inv = 1.0 / (jnp.sum(p, axis=0, keepdims=True) + es)
    return qt, kt, p * inv, es * inv


def _attn_in_specs(cur, prev):
    return [pl.BlockSpec((Q_DIM, BLK), lambda n: (0, cur(n))),
            pl.BlockSpec((KV_DIM, BLK), lambda n: (O_K // KV_DIM, cur(n))),
            pl.BlockSpec((KV_DIM, BLK), lambda n: (O_K // KV_DIM, prev(n))),
            pl.BlockSpec((KV_DIM, BLK), lambda n: (O_V // KV_DIM, cur(n))),
            pl.BlockSpec((KV_DIM, BLK), lambda n: (O_V // KV_DIM, prev(n))),
            pl.BlockSpec((1, 128), lambda n: (0, 0))]


def _attn_fwd(qkvt, sinks, side=None):
    s = qkvt.shape[1]
    nb = s // BLK

    def body(q_ref, kc_ref, kp_ref, vc_ref, vp_ref, sk_ref, o_ref):
        valid = _attn_mask(pl.program_id(0))
        for kvh in range(4):
            rows = slice(kvh * 64, (kvh + 1) * 64)
            _, _, probs, _ = _attn_probs(q_ref, kc_ref, kp_ref, sk_ref, kvh, valid)
            vt = jnp.concatenate([vp_ref[rows, :], vc_ref[rows, :]], axis=1).astype(BF16)
            o = _dot(vt, probs.astype(BF16))
            for g in range(4):
                h = kvh * 4 + g
                o_ref[h * 64:(h + 1) * 64, :] = o[:, g * BLK:(g + 1) * BLK].astype(BF16)

    own, extra = _hosted(
        body, name="attn_fwd", grid=(nb,), in_specs=_attn_in_specs(lambda n: n, lambda n: jnp.maximum(n - 1, 0)),
        out_specs=[pl.BlockSpec((Q_DIM, BLK), lambda n: (0, n))],
        out_shape=[jax.ShapeDtypeStruct((Q_DIM, s), BF16)], scratch_shapes=[],
        args=(qkvt, qkvt, qkvt, qkvt, qkvt, sinks), sem=("parallel",), side=side)
    return own[0] if side is None else (own[0], extra)


def _attn_bwd(qkvt, sinks, o, do, side=None):
    s = qkvt.shape[1]
    nb = s // BLK

    def body(q_ref, kc_ref, kp_ref, vc_ref, vp_ref, sk_ref, o_ref, do_ref, dq_ref, dk_ref, dv_ref, dsk_ref, ck, cv, nk, nv):
        n = pl.program_id(0)

        @pl.when(n == 0)
        def _():
            ck[...] = jnp.zeros_like(ck)
            cv[...] = jnp.zeros_like(cv)
            dsk_ref[...] = jnp.zeros_like(dsk_ref)

        @pl.when(n < nb)
        def _():
            valid = _attn_mask(n)
            lane = lax.broadcasted_iota(jnp.int32, (1, 128), 1)
            dsk = jnp.zeros((1, 128), F32)
            for kvh in range(4):
                rows = slice(kvh * 64, (kvh + 1) * 64)
                qt, kt, probs, psink = _attn_probs(q_ref, kc_ref, kp_ref, sk_ref, kvh, valid)
                vt = jnp.concatenate([vp_ref[rows, :], vc_ref[rows, :]], axis=1).astype(BF16)
                heads = [slice((kvh * 4 + g) * 64, (kvh * 4 + g + 1) * 64) for g in range(4)]
                dot = jnp.concatenate([do_ref[hh, :] for hh in heads], axis=1)
                ot = jnp.concatenate([o_ref[hh, :] for hh in heads], axis=1).astype(F32)
                delta = jnp.sum(dot * ot, axis=0, keepdims=True)
                dot16 = dot.astype(BF16)
                dp = _dot_tn(vt, dot16)
                ds = (probs * (dp - delta) * 0.125).astype(BF16)
                dqt = _dot(kt, ds)
                nk[rows, :] = _dot_nt(qt, ds)
                nv[rows, :] = _dot_nt(dot16, probs.astype(BF16))
                sd = psink * delta
                for g in range(4):
                    dq_ref[heads[g], :] = dqt[:, g * BLK:(g + 1) * BLK].astype(BF16)
                    val = -jnp.sum(sd[:, g * BLK:(g + 1) * BLK], axis=1, keepdims=True)
                    dsk = dsk + jnp.where(lane == kvh * 4 + g, val, 0.0)
            dsk_ref[0:1, :] += dsk
            dk_ref[...] = (ck[...] + nk[:, :BLK]).astype(BF16)
            dv_ref[...] = (cv[...] + nv[:, :BLK]).astype(BF16)
            ck[...] = nk[:, BLK:]
            cv[...] = nv[:, BLK:]

        @pl.when(n == nb)
        def _():
            dk_ref[...] = ck[...].astype(BF16)
            dv_ref[...] = cv[...].astype(BF16)

    cur = lambda n: jnp.minimum(n, nb - 1)
    prev = lambda n: jnp.maximum(jnp.minimum(n, nb - 1) - 1, 0)
    outb = lambda n: jnp.maximum(n - 1, 0)
    own, extra = _hosted(
        body, name="attn_bwd", grid=(nb + 1,),
        in_specs=_attn_in_specs(cur, prev) + [pl.BlockSpec((Q_DIM, BLK), lambda n: (0, cur(n))),
                                              pl.BlockSpec((Q_DIM, BLK), lambda n: (0, cur(n)))],
        out_specs=[pl.BlockSpec((Q_DIM, BLK), lambda n: (0, cur(n))),
                   pl.BlockSpec((KV_DIM, BLK), lambda n: (0, outb(n))),
                   pl.BlockSpec((KV_DIM, BLK), lambda n: (0, outb(n))),
                   pl.BlockSpec((8, 128), lambda n: (0, 0))],
        out_shape=[jax.ShapeDtypeStruct((Q_DIM, s), BF16), jax.ShapeDtypeStruct((KV_DIM, s), BF16),
                   jax.ShapeDtypeStruct((KV_DIM, s), BF16), jax.ShapeDtypeStruct((8, 128), F32)],
        scratch_shapes=[pltpu.VMEM((KV_DIM, BLK), F32)] * 2 + [pltpu.VMEM((KV_DIM, 2 * BLK), F32)] * 2,
        args=(qkvt, qkvt, qkvt, qkvt, qkvt, sinks, o, do), sem=("arbitrary",), side=side)
    return own if side is None else (own, extra)


def _shift_down(x, j):
    if j == 0:
        return x
    row = lax.broadcasted_iota(jnp.int32, x.shape, 0)
    return jnp.where(row >= j, pltpu.roll(x, j, 0), 0.0)


def _shift_up(x, j):
    if j == 0:
        return x
    s = x.shape[0]
    row = lax.broadcasted_iota(jnp.int32, x.shape, 0)
    return jnp.where(row < s - j, pltpu.roll(x, s - j, 0), 0.0)


def _conv(x, w_ref, b_ref):
    kk = w_ref.shape[0]
    y = _shift_down(x, kk - 1) * w_ref[0:1, :]
    for q in range(1, kk):
        y = y + _shift_down(x, kk - 1 - q) * w_ref[q:q + 1, :]
    return y + b_ref[...]


def _conv_bwd(dy, x, w_ref, dx_dtype):
    kk = w_ref.shape[0]
    dx = _shift_up(dy, kk - 1) * w_ref[0:1, :]
    dws = [jnp.sum(dy * _shift_down(x, kk - 1), axis=0, keepdims=True)]
    for q in range(1, kk):
        dx = dx + _shift_up(dy, kk - 1 - q) * w_ref[q:q + 1, :]
        dws.append(jnp.sum(dy * _shift_down(x, kk - 1 - q), axis=0, keepdims=True))
    return dx.astype(dx_dtype), dws, jnp.sum(dy, axis=0, keepdims=True)


def _dsilu(y, sg):
    return sg * (1.0 + y * (1.0 - sg))


CT = 256


def _ssd_conv_fwd(proj, w, b):
    s = proj.shape[0]

    def body(x_ref, w_ref, b_ref, o_ref):
        y = _conv(x_ref[...], w_ref, b_ref)
        o_ref[...] = y * _sigmoid(y)

    return pl.pallas_call(
        body, name="ssd_conv_fwd", grid=(XBC_DIM // CT,),
        in_specs=[pl.BlockSpec((s, CT), lambda i: (0, O_X // CT + i)), pl.BlockSpec((4, CT), lambda i: (0, i)),
                  pl.BlockSpec((1, CT), lambda i: (0, i))],
        out_specs=pl.BlockSpec((s, CT), lambda i: (0, i)),
        out_shape=jax.ShapeDtypeStruct((s, XBC_DIM), F32), compiler_params=_cp(("parallel",)),
    )(proj, w, b)


def _ssd_conv_bwd(dact, proj, w, b):
    s = proj.shape[0]

    def body(d_ref, x_ref, w_ref, b_ref, dx_ref, dw_ref, db_ref):
        x = x_ref[...]
        y = _conv(x, w_ref, b_ref)
        dy = d_ref[...] * _dsilu(y, _sigmoid(y))
        dx, dws, db = _conv_bwd(dy, x, w_ref, BF16)
        dx_ref[...] = dx
        for q in range(4):
            dw_ref[q:q + 1, :] = dws[q]
        db_ref[...] = db

    return pl.pallas_call(
        body, name="ssd_conv_bwd", grid=(XBC_DIM // CT,),
        in_specs=[pl.BlockSpec((s, CT), lambda i: (0, i)), pl.BlockSpec((s, CT), lambda i: (0, O_X // CT + i)),
                  pl.BlockSpec((4, CT), lambda i: (0, i)), pl.BlockSpec((1, CT), lambda i: (0, i))],
        out_specs=[pl.BlockSpec((s, CT), lambda i: (0, i)), pl.BlockSpec((4, CT), lambda i: (0, i)),
                   pl.BlockSpec((1, CT), lambda i: (0, i))],
        out_shape=[jax.ShapeDtypeStruct((s, XBC_DIM), BF16), jax.ShapeDtypeStruct((4, XBC_DIM), F32),
                   jax.ShapeDtypeStruct((1, XBC_DIM), F32)],
        compiler_params=_cp(("parallel",)),
    )(dact, proj, w, b)


NFT = D_FF // CT


def _ffn_act_fwd(up, w, b):
    s = up.shape[0]

    def body(v_ref, g_ref, wv_ref, wg_ref, bv_ref, bg_ref, o_ref):
        val = _conv(v_ref[...], wv_ref, bv_ref)
        gt = _conv(g_ref[...], wg_ref, bg_ref)
        o_ref[...] = ((gt * _sigmoid(gt)) * val).astype(BF16)

    col = lambda off: (lambda i: (0, off + i))
    return pl.pallas_call(
        body, name="ffn_act_fwd", grid=(NFT,),
        in_specs=[pl.BlockSpec((s, CT), col(0)), pl.BlockSpec((s, CT), col(NFT)),
                  pl.BlockSpec((3, CT), col(0)), pl.BlockSpec((3, CT), col(NFT)),
                  pl.BlockSpec((1, CT), col(0)), pl.BlockSpec((1, CT), col(NFT))],
        out_specs=pl.BlockSpec((s, CT), col(0)),
        out_shape=jax.ShapeDtypeStruct((s, D_FF), BF16), compiler_params=_cp(("parallel",)),
    )(up, up, w, w, b, b)


def _ffn_act_bwd(dact, up, w, b):
    s = up.shape[0]

    def body(d_ref, v_ref, g_ref, wv_ref, wg_ref, bv_ref, bg_ref, dx_ref, dw_ref, db_ref):
        xv, xg = v_ref[...], g_ref[...]
        val = _conv(xv, wv_ref, bv_ref)
        gt = _conv(xg, wg_ref, bg_ref)
        sg = _sigmoid(gt)
        d = d_ref[...]
        for half, (dy, x, w_ref) in enumerate(((d * (gt * sg), xv, wv_ref), (d * val * _dsilu(gt, sg), xg, wg_ref))):
            dx, dws, db = _conv_bwd(dy, x, w_ref, BF16)
            dx_ref[half] = dx
            for q in range(3):
                dw_ref[half, q:q + 1, :] = dws[q]
            db_ref[half] = db

    col = lambda off: (lambda i: (0, off + i))
    both = lambda i: (0, 0, i)
    return pl.pallas_call(
        body, name="ffn_act_bwd", grid=(NFT,),
        in_specs=[pl.BlockSpec((s, CT), col(0)), pl.BlockSpec((s, CT), col(0)), pl.BlockSpec((s, CT), col(NFT)),
                  pl.BlockSpec((3, CT), col(0)), pl.BlockSpec((3, CT), col(NFT)),
                  pl.BlockSpec((1, CT), col(0)), pl.BlockSpec((1, CT), col(NFT))],
        out_specs=[pl.BlockSpec((2, s, CT), both), pl.BlockSpec((2, 3, CT), both), pl.BlockSpec((2, 1, CT), both)],
        out_shape=[jax.ShapeDtypeStruct((2, s, D_FF), BF16), jax.ShapeDtypeStruct((2, 3, D_FF), F32),
                   jax.ShapeDtypeStruct((2, 1, D_FF), F32)],
        compiler_params=_cp(("parallel",)),
    )(dact, up, up, w, w, b, b)


def _expand_mat():
    r = lax.broadcasted_iota(jnp.int32, (128, D_INNER), 0)
    c = lax.broadcasted_iota(jnp.int32, (128, D_INNER), 1)
    return ((c >> 6) == r).astype(BF16)


def _reduce_mat():
    r = lax.broadcasted_iota(jnp.int32, (D_INNER, 128), 0)
    c = lax.broadcasted_iota(jnp.int32, (D_INNER, 128), 1)
    return ((r >> 6) == c).astype(BF16)


def _split(v, parts):
    out = []
    for _ in range(parts - 1):
        p = v.astype(BF16)
        out.append(p)
        v = v - p.astype(F32)
    out.append(v.astype(BF16))
    return out


def _sel_dot(v, sel, parts):
    acc = None
    for p in reversed(_split(v, parts)):
        t = _dot(p, sel)
        acc = t if acc is None else acc + t
    return acc


def _row8(v):
    return jnp.broadcast_to(v, (8, v.shape[1]))


def _tril():
    r = lax.broadcasted_iota(jnp.int32, (BLK, BLK), 0)
    c = lax.broadcasted_iota(jnp.int32, (BLK, BLK), 1)
    return r >= c


def _softplus(x):
    return jnp.maximum(x, 0.0) + jnp.log(1.0 + jnp.exp(-jnp.abs(x)))


def _ssd_common(dtraw_ref, dtb_ref, alog_ref):
    causal = _tril()
    e_mat = _expand_mat()
    a_neg = -jnp.exp(alog_ref[...])
    dt = _softplus(dtraw_ref[...] + dtb_ref[...])
    a_cs = _dot(causal.astype(F32), dt * a_neg, HI)
    a_cs_t = a_cs.T
    dt_x = _sel_dot(dt, e_mat, 3)
    acs_x = _sel_dot(a_cs, e_mat, 3)
    alast_x = acs_x[BLK - 1:BLK, :]
    ea_x = jnp.exp(acs_x)
    ds_x = jnp.exp(alast_x - acs_x)
    elast_x = jnp.exp(alast_x)
    return causal, e_mat, a_neg, dt, a_cs, a_cs_t, dt_x, ea_x, ds_x, elast_x


def _decay(a_cs, a_cs_t, h, causal):
    seg = a_cs[:, h:h + 1] - a_cs_t[h:h + 1, :]
    return jnp.where(causal, jnp.exp(jnp.where(causal, seg, 0.0)), 0.0)


def _ssd_fwd(xbc, proj, dt_bias, a_log, d_skip, side=None):
    s = xbc.shape[0]
    nc = s // BLK

    def body(xs_ref, b_ref, c_ref, dtraw_ref, dtb_ref, alog_ref, dskip_ref, y_ref, hp_ref, h_scr, xc16):
        @pl.when(pl.program_id(0) == 0)
        def _():
            h_scr[...] = jnp.zeros_like(h_scr)

        causal, e_mat, _, _, a_cs, a_cs_t, dt_x, ea_x, ds_x, elast_x = _ssd_common(dtraw_ref, dtb_ref, alog_ref)
        dskip_x = _sel_dot(_row8(dskip_ref[...]), e_mat, 3)[0:1]
        xs = xs_ref[...]
        xc = xs * dt_x
        xc16[...] = xc.astype(BF16)
        xcd = (xc * ds_x).astype(BF16)
        hp_ref[0] = h_scr[...]
        for g in range(4):
            gs = slice(g * 512, (g + 1) * 512)
            cg = c_ref[:, g * 128:(g + 1) * 128].astype(BF16)
            bg = b_ref[:, g * 128:(g + 1) * 128].astype(BF16)
            cb = _dot_nt(cg, bg)
            hg = h_scr[:, gs]
            yoff = _dot(cg, hg.astype(BF16)) * ea_x[:, gs]
            for j in range(8):
                h = g * 8 + j
                hsl = slice(h * 64, (h + 1) * 64)
                mm = (cb * _decay(a_cs, a_cs_t, h, causal)).astype(BF16)
                y_ref[:, hsl] = _dot(mm, xc16[:, hsl])
            y_ref[:, gs] += yoff + xs[:, gs] * dskip_x[:, gs]
            h_scr[:, gs] = hg * elast_x[:, gs] + _dot_tn(bg, xcd[:, gs])

    vec = pl.BlockSpec((1, 128), lambda c: (0, 0))
    own, extra = _hosted(
        body, name="ssd_fwd", grid=(nc,),
        in_specs=[pl.BlockSpec((BLK, D_INNER), lambda c: (c, 0)),
                  pl.BlockSpec((BLK, BC_DIM), lambda c: (c, D_INNER // BC_DIM)),
                  pl.BlockSpec((BLK, BC_DIM), lambda c: (c, D_INNER // BC_DIM + 1)),
                  pl.BlockSpec((BLK, 128), lambda c: (c, O_DT // 128)), vec, vec, vec],
        out_specs=[pl.BlockSpec((BLK, D_INNER), lambda c: (c, 0)),
                   pl.BlockSpec((1, 128, D_INNER), lambda c: (c, 0, 0))],
        out_shape=[jax.ShapeDtypeStruct((s, D_INNER), F32), jax.ShapeDtypeStruct((nc, 128, D_INNER), F32)],
        scratch_shapes=[pltpu.VMEM((128, D_INNER), F32), pltpu.VMEM((BLK, D_INNER), BF16)],
        args=(xbc, xbc, xbc, proj, dt_bias, a_log, d_skip), sem=("arbitrary",), side=side)
    return own if side is None else (own, extra)


def _ssd_bwd(xbc, proj, dt_bias, a_log, d_skip, hprev, dy, side=None):
    s = xbc.shape[0]
    nc = s // BLK

    def body(xs_ref, b_ref, c_ref, dtraw_ref, dtb_ref, alog_ref, dskip_ref, hp_ref, dy_ref,
             dxbc_ref, ddt_ref, dvec_ref, dh_scr, xc16, dy16, dxc_scr, dacs_r, tdiff):
        step = pl.program_id(0)
        dacs_r[...] = jnp.zeros_like(dacs_r)

        @pl.when(step == 0)
        def _():
            dh_scr[...] = jnp.zeros_like(dh_scr)
            dvec_ref[...] = jnp.zeros_like(dvec_ref)

        causal, e_mat, a_neg, dt, a_cs, a_cs_t, dt_x, ea_x, ds_x, elast_x = _ssd_common(dtraw_ref, dtb_ref, alog_ref)
        r_mat = _reduce_mat()
        lane = lax.broadcasted_iota(jnp.int32, (1, 128), 1)
        dskip_x = _sel_dot(_row8(dskip_ref[...]), e_mat, 3)[0:1]
        xs = xs_ref[...]
        dy = dy_ref[...]
        xc = xs * dt_x
        xcd = xc * ds_x
        xc16[...] = xc.astype(BF16)
        dy16[...] = dy.astype(BF16)
        dyea = dy * ea_x
        dh = dh_scr[...]
        hp = hp_ref[0]
        dalast_x = jnp.sum(dh * hp, axis=0, keepdims=True) * elast_x
        dacs = jnp.zeros((BLK, 128), F32)
        for g in range(4):
            gs = slice(g * 512, (g + 1) * 512)
            bsl = slice(g * 128, (g + 1) * 128)
            cg = c_ref[:, bsl].astype(BF16)
            bg = b_ref[:, bsl].astype(BF16)
            cb = _dot_nt(cg, bg)
            hg16 = hp[:, gs].astype(BF16)
            dhg16 = dh[:, gs].astype(BF16)
            raw = _dot(cg, hg16)
            draw16 = dyea[:, gs].astype(BF16)
            dcg = _dot_nt(draw16, hg16)
            dhp_g = _dot_tn(cg, draw16)
            dbg = _dot_nt(xcd[:, gs].astype(BF16), dhg16)
            dxcd = _dot(bg, dhg16)
            dcb = jnp.zeros((BLK, BLK), F32)
            for j in range(8):
                h = g * 8 + j
                hsl = slice(h * 64, (h + 1) * 64)
                decay = _decay(a_cs, a_cs_t, h, causal)
                m = cb * decay
                dm = _dot_nt(dy16[:, hsl], xc16[:, hsl])
                dxc_scr[:, hsl] = _dot_tn(m.astype(BF16), dy16[:, hsl])
                dcb = dcb + dm * decay
                dseg = dm * m
                oneh = jnp.where(lane == h, 1.0, 0.0)
                dacs = dacs + jnp.sum(dseg, axis=1, keepdims=True) * oneh
                dacs_r[h:h + 1, :] = jnp.sum(dseg, axis=0, keepdims=True)
            dcb16 = dcb.astype(BF16)
            dcg = dcg + _dot(dcb16, bg)
            dbg = dbg + _dot_tn(dcb16, cg)
            dxbc_ref[:, D_INNER + g * 128:D_INNER + (g + 1) * 128] = dbg
            dxbc_ref[:, D_INNER + BC_DIM + g * 128:D_INNER + BC_DIM + (g + 1) * 128] = dcg
            dxc_scr[:, gs] += dxcd * ds_x[:, gs]
            dh_scr[:, gs] = dh[:, gs] * elast_x[:, gs] + dhp_g
            tst = dxcd * xcd[:, gs]
            tdiff[:, gs] = dy[:, gs] * (raw * ea_x[:, gs]) - tst
            tdiff[BLK - 1:BLK, gs] += jnp.sum(tst, axis=0, keepdims=True)
        dxc = dxc_scr[...]
        row = lax.broadcasted_iota(jnp.int32, (BLK, D_INNER), 0)
        tfull = tdiff[...] + jnp.where(row == BLK - 1, dalast_x, 0.0)
        dacs = dacs + _sel_dot(tfull, r_mat, 2) - dacs_r[...].T
        da = _dot_tn(causal.astype(F32), dacs, HI)
        ddt = da * a_neg + _sel_dot(dxc * xs, r_mat, 2)
        lmask = lax.broadcasted_iota(jnp.int32, (BLK, 128), 1) < N_SSD_HEADS
        ddtraw = jnp.where(lmask, ddt * _sigmoid(dtraw_ref[...] + dtb_ref[...]), 0.0)
        ddt_ref[...] = ddtraw.astype(BF16)
        dxbc_ref[:, 0:D_INNER] = dy * dskip_x + dxc * dt_x
        dvec_ref[0:1, :] += jnp.sum(ddtraw, axis=0, keepdims=True)
        dvec_ref[1:2, :] += jnp.where(lane < N_SSD_HEADS, jnp.sum(da * dt, axis=0, keepdims=True) * a_neg, 0.0)
        dvec_ref[2:3, :] += _sel_dot(_row8(jnp.sum(dy * xs, axis=0, keepdims=True)), r_mat, 3)[0:1]

    rev = lambda c: nc - 1 - c
    vec = pl.BlockSpec((1, 128), lambda c: (0, 0))
    own, extra = _hosted(
        body, name="ssd_bwd", grid=(nc,),
        in_specs=[pl.BlockSpec((BLK, D_INNER), lambda c: (rev(c), 0)),
                  pl.BlockSpec((BLK, BC_DIM), lambda c: (rev(c), D_INNER // BC_DIM)),
                  pl.BlockSpec((BLK, BC_DIM), lambda c: (rev(c), D_INNER // BC_DIM + 1)),
                  pl.BlockSpec((BLK, 128), lambda c: (rev(c), O_DT // 128)), vec, vec, vec,
                  pl.BlockSpec((1, 128, D_INNER), lambda c: (rev(c), 0, 0)),
                  pl.BlockSpec((BLK, D_INNER), lambda c: (rev(c), 0))],
        out_specs=[pl.BlockSpec((BLK, XBC_DIM), lambda c: (rev(c), 0)),
                   pl.BlockSpec((BLK, 128), lambda c: (rev(c), 0)),
                   pl.BlockSpec((8, 128), lambda c: (0, 0))],
        out_shape=[jax.ShapeDtypeStruct((s, XBC_DIM), F32), jax.ShapeDtypeStruct((s, 128), BF16),
                   jax.ShapeDtypeStruct((8, 128), F32)],
        scratch_shapes=[pltpu.VMEM((128, D_INNER), F32), pltpu.VMEM((BLK, D_INNER), BF16),
                        pltpu.VMEM((BLK, D_INNER), BF16), pltpu.VMEM((BLK, D_INNER), F32),
                        pltpu.VMEM((128, BLK), F32), pltpu.VMEM((BLK, D_INNER), F32)],
        args=(xbc, xbc, xbc, proj, dt_bias, a_log, d_skip, hprev, dy), sem=("arbitrary",), side=side)
    return own if side is None else (own, extra)


GW = 512


def _gate_norm_fwd(y, proj, wn, *, tm=512):
    s = y.shape[0]
    tm = _tile(s, tm)

    def body(y_ref, z_ref, w_ref, o_ref):
        z = z_ref[...]
        y2 = y_ref[...] * (z * _sigmoid(z))
        r = lax.rsqrt(jnp.mean(y2 * y2, axis=-1, keepdims=True) + EPS)
        o_ref[...] = ((y2 * r) * w_ref[...]).astype(BF16)

    return pl.pallas_call(
        body, name="gate_norm_fwd", grid=(s // tm, 4),
        in_specs=[pl.BlockSpec((tm, GW), lambda i, g: (i, g)), pl.BlockSpec((tm, GW), lambda i, g: (i, O_Z // GW + g)),
                  pl.BlockSpec((1, GW), lambda i, g: (0, g))],
        out_specs=pl.BlockSpec((tm, GW), lambda i, g: (i, g)),
        out_shape=jax.ShapeDtypeStruct((s, D_INNER), BF16), compiler_params=_cp(("parallel", "parallel")),
    )(y, proj, wn)


def _gate_norm_bwd(dyn, y, proj, wn, *, tm=512):
    s = y.shape[0]
    tm = _tile(s, tm)

    def body(d_ref, y_ref, z_ref, w_ref, dy_ref, dz_ref, dw_ref):
        i = pl.program_id(1)
        z = z_ref[...]
        sg = _sigmoid(z)
        sz = z * sg
        yv = y_ref[...]
        y2 = yv * sz
        r = lax.rsqrt(jnp.mean(y2 * y2, axis=-1, keepdims=True) + EPS)
        xh = y2 * r
        dv = d_ref[...]
        g = dv * w_ref[...]
        dy2 = r * (g - xh * jnp.mean(g * xh, axis=-1, keepdims=True))
        dy_ref[...] = dy2 * sz
        dz_ref[...] = (dy2 * yv * _dsilu(z, sg)).astype(BF16)
        part = jnp.sum(dv * xh, axis=0, keepdims=True)

        @pl.when(i == 0)
        def _():
            dw_ref[...] = part

        @pl.when(i > 0)
        def _():
            dw_ref[...] += part

    blk = pl.BlockSpec((tm, GW), lambda g, i: (i, g))
    vec = pl.BlockSpec((1, GW), lambda g, i: (0, g))
    return pl.pallas_call(
        body, name="gate_norm_bwd", grid=(4, s // tm),
        in_specs=[blk, blk, pl.BlockSpec((tm, GW), lambda g, i: (i, O_Z // GW + g)), vec],
        out_specs=[blk, blk, vec],
        out_shape=[jax.ShapeDtypeStruct((s, D_INNER), F32), jax.ShapeDtypeStruct((s, D_INNER), BF16),
                   jax.ShapeDtypeStruct((1, D_INNER), F32)],
        compiler_params=_cp(("parallel", "arbitrary")),
    )(dyn, y, proj, wn)


def _merge_fwd(proj, b_gate, attn, ssd_out, *, tm=512):
    s = attn.shape[0]
    tm = _tile(s, tm)

    def body(ga_ref, gs_ref, ba_ref, bs_ref, a_ref, s_ref, o_ref):
        ga = _sigmoid(ga_ref[...] + ba_ref[...])
        gs = _sigmoid(gs_ref[...] + bs_ref[...])
        o_ref[...] = (ga * a_ref[...] + gs * s_ref[...]).astype(BF16)

    blk = pl.BlockSpec((tm, GW), lambda i, j: (i, j))
    return pl.pallas_call(
        body, name="merge_fwd", grid=(s // tm, 2),
        in_specs=[pl.BlockSpec((tm, GW), lambda i, j: (i, O_GA // GW + j)),
                  pl.BlockSpec((tm, GW), lambda i, j: (i, O_GS // GW + j)),
                  pl.BlockSpec((1, GW), lambda i, j: (0, j)), pl.BlockSpec((1, GW), lambda i, j: (0, 2 + j)), blk, blk],
        out_specs=blk, out_shape=jax.ShapeDtypeStruct((s, D_MODEL), BF16),
        compiler_params=_cp(("parallel", "parallel")),
    )(proj, proj, b_gate, b_gate, attn, ssd_out)


def _merge_bwd(dm, proj, b_gate, attn, ssd_out, *, tm=512):
    s = attn.shape[0]
    tm = _tile(s, tm)

    def body(d_ref, ga_ref, gs_ref, ba_ref, bs_ref, a_ref, s_ref, da_ref, ds_ref, dga_ref, dgs_ref, dba_ref, dbs_ref):
        i = pl.program_id(1)
        ga = _sigmoid(ga_ref[...] + ba_ref[...])
        gs = _sigmoid(gs_ref[...] + bs_ref[...])
        d = d_ref[...]
        da_ref[...] = (d * ga).astype(BF16)
        ds_ref[...] = (d * gs).astype(BF16)
        dga = d * a_ref[...] * (ga * (1.0 - ga))
        dgs = d * s_ref[...] * (gs * (1.0 - gs))
        dga_ref[...] = dga.astype(BF16)
        dgs_ref[...] = dgs.astype(BF16)
        pa = jnp.sum(dga, axis=0, keepdims=True)
        ps = jnp.sum(dgs, axis=0, keepdims=True)

        @pl.when(i == 0)
        def _():
            dba_ref[...] = pa
            dbs_ref[...] = ps

        @pl.when(i > 0)
        def _():
            dba_ref[...] += pa
            dbs_ref[...] += ps

    blk = pl.BlockSpec((tm, GW), lambda j, i: (i, j))
    vec = pl.BlockSpec((1, GW), lambda j, i: (0, j))
    sd = jax.ShapeDtypeStruct((s, D_MODEL), BF16)
    vd = jax.ShapeDtypeStruct((1, D_MODEL), F32)
    return pl.pallas_call(
        body, name="merge_bwd", grid=(2, s // tm),
        in_specs=[blk, pl.BlockSpec((tm, GW), lambda j, i: (i, O_GA // GW + j)),
                  pl.BlockSpec((tm, GW), lambda j, i: (i, O_GS // GW + j)),
                  vec, pl.BlockSpec((1, GW), lambda j, i: (0, 2 + j)), blk, blk],
        out_specs=[blk, blk, blk, blk, vec, vec], out_shape=[sd, sd, sd, sd, vd, vd],
        compiler_params=_cp(("parallel", "arbitrary")),
    )(dm, proj, proj, b_gate, b_gate, attn, ssd_out)


def _adamw_math(w, g, m, v):
    mn = ADAM_B1 * m + (1.0 - ADAM_B1) * g
    vn = ADAM_B2 * v + (1.0 - ADAM_B2) * (g * g)
    m_hat = mn / (1.0 - ADAM_B1 ** ADAM_STEP)
    v_hat = vn / (1.0 - ADAM_B2 ** ADAM_STEP)
    return -ADAM_LR * (m_hat / (jnp.sqrt(v_hat) + ADAM_EPS) + ADAM_WD * w), mn, vn


def _adamw_many(ws, gs, ms, vs):
    n = len(ws)

    def body(*refs):
        outs = refs[4 * n:]
        for i in range(n):
            res = _adamw_math(*[refs[q * n + i][...] for q in range(4)])
            for q in range(3):
                outs[q * n + i][...] = res[q]

    return pl.pallas_call(body, name="adamw_small", out_shape=[jax.ShapeDtypeStruct(w.shape, F32) for w in ws] * 3,
                          compiler_params=_cp())(*ws, *gs, *ms, *vs)


def _adamw(w, g, m, v, *, name, tm=128):
    r, c = w.shape
    tm = r if (r < tm or r % tm) else tm

    def body(w_ref, g_ref, m_ref, v_ref, d_ref, nm_ref, nv_ref):
        d_ref[...], nm_ref[...], nv_ref[...] = _adamw_math(w_ref[...], g_ref[...], m_ref[...], v_ref[...])

    blk = pl.BlockSpec((tm, c), lambda i: (i, 0))
    sd = jax.ShapeDtypeStruct((r, c), F32)
    return pl.pallas_call(
        body, name=name, grid=(r // tm,), in_specs=[blk] * 4, out_specs=[blk] * 3, out_shape=[sd] * 3,
        compiler_params=_cp(("parallel",)),
    )(w, g, m, v)


ANY = pl.BlockSpec(memory_space=pl.ANY)
N_CHIPS = 4


def _chip_of(k, x, y):
    return (x ^ (k >> 1), y ^ (k & 1))


def _all_gather_small(shard):
    r, c = shard.shape
    hr = r // 2

    def body(sh_ref, out_ref, send_sems, recv_sems, local_sem):
        x, y, cc = lax.axis_index("x"), lax.axis_index("y"), lax.axis_index("c")

        def half(px, py, pc):
            return out_ref.at[2 * px + py, pl.ds(pc * hr, hr), :]

        def copy(k, px, py, pc, to, src=None):
            return pltpu.make_async_remote_copy(
                src_ref=half(px, py, pc) if src is None else src, dst_ref=half(px, py, pc),
                send_sem=send_sems.at[k], recv_sem=recv_sems.at[k], device_id=to, device_id_type=MESH)

        mine = pltpu.make_async_copy(sh_ref, out_ref.at[2 * x + y], local_sem)
        mine.start()
        chips = [_chip_of(k, x, y) for k in (1, 2, 3)]
        first = [copy(j, x, y, cc, (*chip, cc), src=sh_ref.at[pl.ds(cc * hr, hr), :]) for j, chip in enumerate(chips)]
        for cp in first:
            cp.start()
        passed = [copy(3 + j, *chip, cc, (x, y, 1 - cc)) for j, chip in enumerate(chips)]
        for j, chip in enumerate(chips):
            copy(j, *chip, cc, (x, y, cc)).wait_recv()
            passed[j].start()
        for j, chip in enumerate(chips):
            copy(3 + j, *chip, 1 - cc, (x, y, cc)).wait_recv()
        for cp in first + passed:
            cp.wait_send()
        mine.wait()

    return pl.pallas_call(
        body, name="all_gather_small", in_specs=[ANY], out_specs=ANY,
        out_shape=jax.ShapeDtypeStruct((N_CHIPS, r, c), shard.dtype),
        scratch_shapes=[pltpu.SemaphoreType.DMA((6,)), pltpu.SemaphoreType.DMA((6,)), pltpu.SemaphoreType.DMA],
    )(shard)


def _cast_bf16(a, *, name, tm=512):
    n, r, c = a.shape
    tm = _tile(r, tm) if r % 128 == 0 else r

    def body(a_ref, o_ref):
        o_ref[...] = a_ref[...].astype(BF16)

    blk = pl.BlockSpec((1, tm, c), lambda i, j: (i, j, 0))
    return pl.pallas_call(body, name=name, grid=(n, r // tm), in_specs=[blk], out_specs=blk,
                          out_shape=jax.ShapeDtypeStruct(a.shape, BF16), compiler_params=_cp(("parallel", "parallel")))(a)


def _pair_exchange(g16, hr):
    n, r, c = g16.shape

    def body(g_ref, out_ref, send_sem, recv_sem):
        x, y, cc = lax.axis_index("x"), lax.axis_index("y"), lax.axis_index("c")
        cp = pltpu.make_async_remote_copy(
            src_ref=g_ref.at[:, pl.ds((1 - cc) * hr, hr), :], dst_ref=out_ref, send_sem=send_sem, recv_sem=recv_sem,
            device_id=(x, y, 1 - cc), device_id_type=MESH)
        cp.start()
        cp.wait()

    return pl.pallas_call(
        body, name="grad_pair_exchange", in_specs=[ANY], out_specs=ANY,
        out_shape=jax.ShapeDtypeStruct((n, hr, c), g16.dtype),
        scratch_shapes=[pltpu.SemaphoreType.DMA, pltpu.SemaphoreType.DMA],
    )(g16)


def _pair_add(g, recv, half_idx, hr, *, tm=384):
    n, r, c = g.shape
    nt = hr // tm

    def body(hi_ref, g_ref, r_ref, o32_ref, o16_ref):
        v = g_ref[...] + r_ref[...].astype(F32)
        o32_ref[...] = v
        o16_ref[...] = v.astype(BF16)

    gs = pltpu.PrefetchScalarGridSpec(
        num_scalar_prefetch=1, grid=(n, nt),
        in_specs=[pl.BlockSpec((1, tm, c), lambda i, j, hi: (i, hi[0] * nt + j, 0)),
                  pl.BlockSpec((1, tm, c), lambda i, j, hi: (i, j, 0))],
        out_specs=[pl.BlockSpec((1, tm, c), lambda i, j, hi: (i, j, 0))] * 2)
    return pl.pallas_call(
        body, name="grad_pair_add", grid_spec=gs,
        out_shape=[jax.ShapeDtypeStruct((n, hr, c), F32), jax.ShapeDtypeStruct((n, hr, c), BF16)],
        compiler_params=_cp(("parallel", "parallel")),
    )(half_idx, g, recv)


def _chip_exchange(p16):
    n, hr, c = p16.shape

    def body(p_ref, out_ref, send_sems, recv_sems):
        x, y, cc = lax.axis_index("x"), lax.axis_index("y"), lax.axis_index("c")
        cps = []
        for j, k in enumerate((1, 2, 3)):
            px, py = _chip_of(k, x, y)
            cps.append(pltpu.make_async_remote_copy(
                src_ref=p_ref.at[2 * px + py], dst_ref=out_ref.at[j], send_sem=send_sems.at[j], recv_sem=recv_sems.at[j],
                device_id=(px, py, cc), device_id_type=MESH))
        for cp in cps:
            cp.start()
        for cp in cps:
            cp.wait()

    return pl.pallas_call(
        body, name="grad_chip_exchange", in_specs=[ANY], out_specs=ANY,
        out_shape=jax.ShapeDtypeStruct((3, hr, c), p16.dtype),
        scratch_shapes=[pltpu.SemaphoreType.DMA((3,)), pltpu.SemaphoreType.DMA((3,))],
    )(p16)


def _chip_add(p32, recv, chip_idx, *, tm=384):
    n, hr, c = p32.shape

    def body(ci_ref, p_ref, r_ref, o_ref):
        o_ref[...] = ((p_ref[0] + r_ref[0].astype(F32)) + r_ref[1].astype(F32)) + r_ref[2].astype(F32)

    gs = pltpu.PrefetchScalarGridSpec(
        num_scalar_prefetch=1, grid=(hr // tm,),
        in_specs=[pl.BlockSpec((1, tm, c), lambda j, ci: (ci[0], j, 0)), pl.BlockSpec((3, tm, c), lambda j, ci: (0, j, 0))],
        out_specs=pl.BlockSpec((tm, c), lambda j, ci: (j, 0)))
    return pl.pallas_call(
        body, name="grad_chip_add", grid_spec=gs, out_shape=jax.ShapeDtypeStruct((hr, c), F32),
        compiler_params=_cp(("parallel",)),
    )(chip_idx, p32, recv)


def _pair_gather(f):
    hr, c = f.shape

    def body(f_ref, out_ref, send_sem, recv_sem, local_sem):
        x, y, cc = lax.axis_index("x"), lax.axis_index("y"), lax.axis_index("c")
        mine = pltpu.make_async_copy(f_ref, out_ref.at[pl.ds(cc * hr, hr), :], local_sem)
        mine.start()
        cp = pltpu.make_async_remote_copy(
            src_ref=f_ref, dst_ref=out_ref.at[pl.ds(cc * hr, hr), :], send_sem=send_sem, recv_sem=recv_sem,
            device_id=(x, y, 1 - cc), device_id_type=MESH)
        cp.start()
        cp.wait()
        mine.wait()

    return pl.pallas_call(
        body, name="grad_pair_gather", in_specs=[ANY], out_specs=ANY,
        out_shape=jax.ShapeDtypeStruct((2 * hr, c), f.dtype),
        scratch_shapes=[pltpu.SemaphoreType.DMA, pltpu.SemaphoreType.DMA, pltpu.SemaphoreType.DMA],
    )(f)


def _all_reduce_small(buf):
    r, c = buf.shape

    def body(b_ref, out_ref, gat, send_sems, recv_sems):
        x, y, cc = lax.axis_index("x"), lax.axis_index("y"), lax.axis_index("c")
        me = 4 * x + 2 * y + cc
        gat[me] = b_ref[...]
        cps = []
        for k in range(1, 8):
            px, py, pc = x ^ (k >> 2), y ^ ((k >> 1) & 1), cc ^ (k & 1)
            cps.append(pltpu.make_async_remote_copy(
                src_ref=b_ref, dst_ref=gat.at[me], send_sem=send_sems.at[k - 1], recv_sem=recv_sems.at[k - 1],
                device_id=(px, py, pc), device_id_type=MESH))
        for cp in cps:
            cp.start()
        for cp in cps:
            cp.wait()
        acc = gat[0]
        for d in range(1, 8):
            acc = acc + gat[d]
        out_ref[...] = acc

    vm = pl.BlockSpec(memory_space=pltpu.VMEM)
    return pl.pallas_call(
        body, name="all_reduce_small", in_specs=[vm], out_specs=vm, out_shape=jax.ShapeDtypeStruct((r, c), F32),
        scratch_shapes=[pltpu.VMEM((8, r, c), F32), pltpu.SemaphoreType.DMA((7,)), pltpu.SemaphoreType.DMA((7,))],
        compiler_params=pltpu.CompilerParams(vmem_limit_bytes=VMEM_LIMIT),
    )(buf)


def _pipe(fn, ins, outs, tr):
    shape = ins[0].shape
    lead, (r, c) = shape[:-2], shape[-2:]
    assert len(lead) <= 1 and r % tr == 0
    nr = r // tr
    n = nr * (lead[0] if lead else 1)
    ni, no = len(ins), len(outs)

    def blk(ref, step):
        rows = pl.ds((step % nr) * tr, tr)
        return ref.at[step // nr, rows, :] if lead else ref.at[rows, :]

    def scoped(*bufs):
        ibufs, obufs, isem, osem = bufs[:ni], bufs[ni:ni + no], bufs[-2], bufs[-1]

        def in_copy(q, step, slot):
            return pltpu.make_async_copy(blk(ins[q], step), ibufs[q].at[slot], isem.at[q, slot])

        def out_copy(q, step, slot):
            return pltpu.make_async_copy(obufs[q].at[slot], blk(outs[q], step), osem.at[q, slot])

        for step in range(min(nbuf - 1, n)):
            for q in range(ni):
                in_copy(q, step, step % nbuf).start()
        for step in range(n):
            slot = step % nbuf
            if step + nbuf - 1 < n:
                for q in range(ni):
                    in_copy(q, step + nbuf - 1, (step + nbuf - 1) % nbuf).start()
            for q in range(ni):
                in_copy(q, step, slot).wait()
            if step >= nbuf:
                for q in range(no):
                    out_copy(q, step - nbuf, slot).wait()
            res = fn(*[ibufs[q][slot] for q in range(ni)])
            for q in range(no):
                obufs[q][slot] = res[q].astype(obufs[q].dtype)
                out_copy(q, step, slot).start()
        for step in range(max(n - nbuf, 0), n):
            for q in range(no):
                out_copy(q, step, step % nbuf).wait()

    assert n <= 8
    nbuf = min(n, 4)
    pl.run_scoped(scoped, *[pltpu.VMEM((nbuf, tr, c), q.dtype) for q in ins], *[pltpu.VMEM((nbuf, tr, c), q.dtype) for q in outs],
                  pltpu.SemaphoreType.DMA((ni, nbuf)), pltpu.SemaphoreType.DMA((no, nbuf)))


W_IN_PAD = 2304
BIG = ("w_in", "w_attn_o", "w_ssd_o", "w_out", "w_up", "w_down")
BIG_SHAPE = dict(w_in=(D_MODEL, W_IN_PAD), w_attn_o=(Q_DIM // 4, D_MODEL), w_ssd_o=(D_INNER // 4, D_MODEL),
                 w_out=(D_MODEL // 4, D_MODEL), w_up=(D_MODEL, 2 * D_FF // 4), w_down=(D_FF // 4, D_MODEL))
BIG_TR = dict(w_in=128, w_attn_o=128, w_ssd_o=128, w_out=128, w_up=128, w_down=176)
X_FIRST = dict(w_in=True, w_attn_o=True, w_ssd_o=False, w_out=True, w_up=False, w_down=False)


def _neighbours(x, y, x_first):
    xn, yn = (1 - x, y), (x, 1 - y)
    n1, n2 = (xn, yn) if x_first else (yn, xn)
    slot = lambda ch: 2 * ch[0] + ch[1]
    return n1, n2, slot(n1), slot(n2), slot((1 - x, 1 - y))


def _gather_big(shards):
    nt = len(BIG)

    def body(*refs):
        sh, out = refs[:nt], refs[nt:2 * nt]
        send_sems, recv_sems = refs[2 * nt:]
        x, y, cc = lax.axis_index("x"), lax.axis_index("y"), lax.axis_index("c")
        me = 2 * x + y
        sib = (x, y, 1 - cc)
        for t, n in enumerate(BIG):
            _pipe(lambda v: (v,), [sh[t]], [out[t].at[me]], BIG_TR[n])

        def copy(t, k, slot, pc, to):
            hr = BIG_SHAPE[BIG[t]][0] // 2
            ref = out[t].at[slot, pl.ds(pc * hr, hr), :]
            return pltpu.make_async_remote_copy(src_ref=ref, dst_ref=ref, send_sem=send_sems.at[6 * t + k],
                                                recv_sem=recv_sems.at[6 * t + k], device_id=to, device_id_type=MESH)

        started = []

        def start(cp):
            cp.start()
            started.append(cp)

        geo = [_neighbours(x, y, X_FIRST[n]) for n in BIG]
        for t in range(nt):
            n1, n2, _, _, _ = geo[t]
            start(copy(t, 0, me, cc, (*n1, cc)))
            start(copy(t, 1, me, cc, (*n2, cc)))
        for t in range(nt):
            n1, n2, s1, s2, sd = geo[t]
            copy(t, 0, s1, cc, sib).wait_recv()
            start(copy(t, 2, s1, cc, (*n2, cc)))
            start(copy(t, 3, s1, cc, sib))
            copy(t, 1, s2, cc, sib).wait_recv()
            start(copy(t, 4, s2, cc, sib))
        for t in range(nt):
            _, _, s1, s2, sd = geo[t]
            copy(t, 2, sd, cc, sib).wait_recv()
            start(copy(t, 5, sd, cc, sib))
        for t in range(nt):
            _, _, s1, s2, sd = geo[t]
            copy(t, 3, s1, 1 - cc, sib).wait_recv()
            copy(t, 4, s2, 1 - cc, sib).wait_recv()
            copy(t, 5, sd, 1 - cc, sib).wait_recv()
        for cp in started:
            cp.wait_send()

    return pl.pallas_call(
        body, name="gather_big", in_specs=[ANY] * nt, out_specs=[ANY] * nt,
        out_shape=[jax.ShapeDtypeStruct((N_CHIPS, *BIG_SHAPE[n]), BF16) for n in BIG],
        scratch_shapes=[pltpu.SemaphoreType.DMA((6 * nt,)), pltpu.SemaphoreType.DMA((6 * nt,))],
        compiler_params=pltpu.CompilerParams(vmem_limit_bytes=VMEM_LIMIT),
    )(*shards)


def _reduce_big(grads):
    nt = len(BIG)
    nw = 7

    def body(*refs):
        g = refs[:nt]
        fin = refs[nt:2 * nt]
        work = refs[2 * nt:2 * nt + nw * nt]
        send_sems, recv_sems = refs[2 * nt + nw * nt:]
        x, y, cc = lax.axis_index("x"), lax.axis_index("y"), lax.axis_index("c")
        me = 2 * x + y
        sib = (x, y, 1 - cc)
        started = []

        def rcopy(t, k, src, dst, to):
            cp = pltpu.make_async_remote_copy(src_ref=src, dst_ref=dst, send_sem=send_sems.at[5 * t + k],
                                              recv_sem=recv_sems.at[5 * t + k], device_id=to, device_id_type=MESH)
            return cp

        def start(cp):
            cp.start()
            started.append(cp)

        geo = [_neighbours(x, y, X_FIRST[n]) for n in BIG]
        hrs = [BIG_SHAPE[n][0] // 2 for n in BIG]
        wk = lambda t: work[nw * t:nw * (t + 1)]
        one = lambda ref, slot: ref.at[pl.ds(slot, 1)]
        for t in range(nt):
            recv_a = wk(t)[0]
            start(rcopy(t, 0, g[t].at[:, pl.ds((1 - cc) * hrs[t], hrs[t]), :], recv_a, sib))
        for t, n in enumerate(BIG):
            recv_a, p32, p16, r1, qme, qs2, r2 = wk(t)
            n1, n2, s1, s2, sd = geo[t]
            rcopy(t, 0, recv_a, recv_a, sib).wait_recv()
            _pipe(lambda a, b: (a + b, a + b), [g[t].at[:, pl.ds(cc * hrs[t], hrs[t]), :], recv_a], [p32, p16], BIG_TR[n])
            start(rcopy(t, 1, one(p16, s1), one(r1, 0), (*n1, cc)))
            start(rcopy(t, 2, one(p16, sd), one(r1, 1), (*n1, cc)))
        for t, n in enumerate(BIG):
            recv_a, p32, p16, r1, qme, qs2, r2 = wk(t)
            n1, n2, s1, s2, sd = geo[t]
            rcopy(t, 1, one(r1, 0), one(r1, 0), sib).wait_recv()
            rcopy(t, 2, one(r1, 1), one(r1, 1), sib).wait_recv()
            _pipe(lambda a, b: (a + b.astype(F32),), [one(p32, s2), one(r1, 1)], [qs2], BIG_TR[n])
            start(rcopy(t, 3, qs2, r2, (*n2, cc)))
            _pipe(lambda a, b: (a + b.astype(F32),), [one(p32, me), one(r1, 0)], [qme], BIG_TR[n])
        for t, n in enumerate(BIG):
            recv_a, p32, p16, r1, qme, qs2, r2 = wk(t)
            rcopy(t, 3, r2, r2, sib).wait_recv()
            mine = fin[t].at[pl.ds(cc * hrs[t], hrs[t]), :]
            _pipe(lambda a, b: (a + b.astype(F32),), [qme.at[0], r2.at[0]], [mine], BIG_TR[n])
            start(rcopy(t, 4, mine, mine, sib))
        for t in range(nt):
            other = fin[t].at[pl.ds((1 - cc) * hrs[t], hrs[t]), :]
            rcopy(t, 4, other, other, sib).wait_recv()
        for cp in started:
            cp.wait_send()

    outs = [jax.ShapeDtypeStruct(BIG_SHAPE[n], F32) for n in BIG]
    for n in BIG:
        r, c = BIG_SHAPE[n]
        hr = r // 2
        outs += [jax.ShapeDtypeStruct((4, hr, c), F32), jax.ShapeDtypeStruct((4, hr, c), F32),
                 jax.ShapeDtypeStruct((4, hr, c), BF16), jax.ShapeDtypeStruct((2, hr, c), BF16),
                 jax.ShapeDtypeStruct((1, hr, c), F32), jax.ShapeDtypeStruct((1, hr, c), BF16),
                 jax.ShapeDtypeStruct((1, hr, c), BF16)]
    res = pl.pallas_call(
        body, name="reduce_big", in_specs=[ANY] * nt, out_specs=[ANY] * len(outs), out_shape=outs,
        scratch_shapes=[pltpu.SemaphoreType.DMA((5 * nt,)), pltpu.SemaphoreType.DMA((5 * nt,))],
        compiler_params=pltpu.CompilerParams(vmem_limit_bytes=VMEM_LIMIT),
    )(*grads)
    return res[:nt]


WHOLE_X_FIRST = dict(w_ssd_o=True, w_out=False, w_attn_o=False)


def _quarters(names):
    out = []
    for i, n in enumerate(names):
        if n in WHOLE_X_FIRST:
            h = BIG_SHAPE[n][0] // 2
            out.append((i, WHOLE_X_FIRST[n], 0, h, 128))
        else:
            q = BIG_SHAPE[n][0] // 4
            tr = 128 if q % 128 == 0 else q
            out += [(i, True, 0, q, tr), (i, False, q, q, tr)]
    return out


class _GatherJob:
    def __init__(self, names, shards, at=None):
        self.names = names
        self.at = at
        self.inputs = list(shards)
        self.out_shapes = [jax.ShapeDtypeStruct((N_CHIPS, *BIG_SHAPE[n]), BF16) for n in names]
        self.ent = _quarters(names)
        self.scratch = [pltpu.SemaphoreType.DMA((6 * len(self.ent),)), pltpu.SemaphoreType.DMA((6 * len(self.ent),))]

    def phases(self, sh, out, scr):
        send_sems, recv_sems = scr
        names, ent = self.names, self.ent
        x, y, cc = lax.axis_index("x"), lax.axis_index("y"), lax.axis_index("c")
        me = 2 * x + y
        sib = (x, y, 1 - cc)
        geo = [_neighbours(x, y, e[1]) for e in ent]
        started = []

        def copy(i, k, slot, pc, to):
            arr, _, roff, rows, _ = ent[i]
            hr = BIG_SHAPE[names[arr]][0] // 2
            ref = out[arr].at[slot, pl.ds(pc * hr + roff, rows), :]
            return pltpu.make_async_remote_copy(src_ref=ref, dst_ref=ref, send_sem=send_sems.at[6 * i + k],
                                                recv_sem=recv_sems.at[6 * i + k], device_id=to, device_id_type=MESH)

        def start(*a):
            copy(*a).start()
            started.append(a)

        def p0():
            for t, n in enumerate(names):
                _pipe(lambda v: (v,), [sh[t]], [out[t].at[me]], BIG_TR[n])
            for i in range(len(ent)):
                n1, n2, _, _, _ = geo[i]
                start(i, 0, me, cc, (*n1, cc))
                start(i, 1, me, cc, (*n2, cc))

        def p1():
            for i in range(len(ent)):
                n1, n2, s1, s2, sd = geo[i]
                copy(i, 0, s1, cc, sib).wait_recv()
                start(i, 2, s1, cc, (*n2, cc))
                start(i, 3, s1, cc, sib)
                copy(i, 1, s2, cc, sib).wait_recv()
                start(i, 4, s2, cc, sib)

        def p2():
            for i in range(len(ent)):
                sd = geo[i][4]
                copy(i, 2, sd, cc, sib).wait_recv()
                start(i, 5, sd, cc, sib)

        def p3():
            for i in range(len(ent)):
                _, _, s1, s2, sd = geo[i]
                copy(i, 3, s1, 1 - cc, sib).wait_recv()
                copy(i, 4, s2, 1 - cc, sib).wait_recv()
                copy(i, 5, sd, 1 - cc, sib).wait_recv()
            for a in started:
                copy(*a).wait_send()

        return [p0, p1, p2, p3]


class _ReduceJob:
    NW = 7

    def __init__(self, names, grads, at=None):
        self.names = names
        self.at = at
        self.inputs = list(grads)
        self.ent = _quarters(names)
        self.out_shapes = [jax.ShapeDtypeStruct(BIG_SHAPE[n], F32) for n in names]
        for arr, _, _, rows, _ in self.ent:
            c = BIG_SHAPE[names[arr]][1]
            self.out_shapes += [jax.ShapeDtypeStruct((4, rows, c), F32), jax.ShapeDtypeStruct((4, rows, c), F32),
                                jax.ShapeDtypeStruct((4, rows, c), BF16), jax.ShapeDtypeStruct((2, rows, c), BF16),
                                jax.ShapeDtypeStruct((1, rows, c), F32), jax.ShapeDtypeStruct((1, rows, c), BF16),
                                jax.ShapeDtypeStruct((1, rows, c), BF16)]
        self.scratch = [pltpu.SemaphoreType.DMA((5 * len(self.ent),)), pltpu.SemaphoreType.DMA((5 * len(self.ent),))]

    def phases(self, g, outs, scr):
        send_sems, recv_sems = scr
        names, ent, nw = self.names, self.ent, self.NW
        nt = len(names)
        fin, work = outs[:nt], outs[nt:]
        x, y, cc = lax.axis_index("x"), lax.axis_index("y"), lax.axis_index("c")
        me = 2 * x + y
        sib = (x, y, 1 - cc)
        geo = [_neighbours(x, y, e[1]) for e in ent]
        started = []
        wk = lambda i: work[nw * i:nw * (i + 1)]
        one = lambda ref, slot: ref.at[pl.ds(slot, 1)]

        def rows_of(i, pc):
            arr, _, roff, rows, _ = ent[i]
            return pl.ds(pc * (BIG_SHAPE[names[arr]][0] // 2) + roff, rows)

        def rcopy(i, k, src, dst, to):
            return pltpu.make_async_remote_copy(src_ref=src, dst_ref=dst, send_sem=send_sems.at[5 * i + k],
                                                recv_sem=recv_sems.at[5 * i + k], device_id=to, device_id_type=MESH)

        def start(make):
            make().start()
            started.append(make)

        def p0():
            for i, e in enumerate(ent):
                start(lambda i=i, e=e: rcopy(i, 0, g[e[0]].at[:, rows_of(i, 1 - cc), :], wk(i)[0], sib))

        def p1():
            for i, e in enumerate(ent):
                recv_a, p32, p16, r1 = wk(i)[:4]
                n1, n2, s1, s2, sd = geo[i]
                rcopy(i, 0, recv_a, recv_a, sib).wait_recv()
                _pipe(lambda a, b: (a + b, a + b), [g[e[0]].at[:, rows_of(i, cc), :], recv_a], [p32, p16], e[4])
                start(lambda i=i, s1=s1, n1=n1: rcopy(i, 1, one(wk(i)[2], s1), one(wk(i)[3], 0), (*n1, cc)))
                start(lambda i=i, sd=sd, n1=n1: rcopy(i, 2, one(wk(i)[2], sd), one(wk(i)[3], 1), (*n1, cc)))

        def p2():
            for i, e in enumerate(ent):
                _, p32, _, r1, qme, qs2, r2 = wk(i)
                n1, n2, s1, s2, sd = geo[i]
                rcopy(i, 1, one(r1, 0), one(r1, 0), sib).wait_recv()
                rcopy(i, 2, one(r1, 1), one(r1, 1), sib).wait_recv()
                _pipe(lambda a, b, c, d: (a + b.astype(F32), c + d.astype(F32)),
                      [one(p32, s2), one(r1, 1), one(p32, me), one(r1, 0)], [qs2, qme], e[4])
                start(lambda i=i, n2=n2: rcopy(i, 3, wk(i)[5], wk(i)[6], (*n2, cc)))

        def p3():
            for i, e in enumerate(ent):
                qme, r2 = wk(i)[4], wk(i)[6]
                rcopy(i, 3, r2, r2, sib).wait_recv()
                mine = fin[e[0]].at[rows_of(i, cc), :]
                _pipe(lambda a, b: (a + b.astype(F32),), [qme.at[0], r2.at[0]], [mine], e[4])
                start(lambda i=i, e=e: rcopy(i, 4, fin[e[0]].at[rows_of(i, cc), :], fin[e[0]].at[rows_of(i, cc), :], sib))

        def p4():
            for i, e in enumerate(ent):
                other = fin[e[0]].at[rows_of(i, 1 - cc), :]
                rcopy(i, 4, other, other, sib).wait_recv()
            for make in started:
                make().wait_send()

        return [p0, p1, p2, p3, p4]


def _run_job(job, name):
    ni, no = len(job.inputs), len(job.out_shapes)

    def body(*refs):
        for ph in job.phases(refs[:ni], refs[ni:ni + no], refs[ni + no:]):
            ph()

    return pl.pallas_call(
        body, name=name, in_specs=[ANY] * ni, out_specs=[ANY] * no, out_shape=job.out_shapes, scratch_shapes=job.scratch,
        compiler_params=pltpu.CompilerParams(vmem_limit_bytes=VMEM_LIMIT),
    )(*job.inputs)


def _hosted(body, *, name, grid, in_specs, out_specs, out_shape, scratch_shapes, args, sem, side=None):
    if side is None:
        return pl.pallas_call(body, name=name, grid=grid, in_specs=in_specs, out_specs=out_specs, out_shape=out_shape,
                              scratch_shapes=scratch_shapes, compiler_params=_cp(sem))(*args), None
    job = side
    ni, no, ns = len(in_specs), len(out_specs), len(scratch_shapes)
    ji, jo = len(job.inputs), len(job.out_shapes)
    n_steps = 1
    for extent in grid:
        n_steps *= extent

    def wrapped(*refs):
        own_in, refs = refs[:ni], refs[ni:]
        job_in, refs = refs[:ji], refs[ji:]
        own_out, refs = refs[:no], refs[no:]
        job_out, refs = refs[:jo], refs[jo:]
        own_scr, job_scr = refs[:ns], refs[ns:]
        step = 0
        for d, extent in enumerate(grid):
            step = step * extent + pl.program_id(d)
        phases = job.phases(job_in, job_out, job_scr)
        steps = [min(int(f * n_steps), n_steps - 1) for f in job.at] + [n_steps - 1]
        assert len(steps) == len(phases) and steps == sorted(steps)
        for at, ph in zip(steps, phases):
            pl.when(step == at)(ph)
        body(*own_in, *own_out, *own_scr)

    res = pl.pallas_call(
        wrapped, name=name, grid=grid, in_specs=list(in_specs) + [ANY] * ji, out_specs=list(out_specs) + [ANY] * jo,
        out_shape=list(out_shape) + list(job.out_shapes), scratch_shapes=list(scratch_shapes) + list(job.scratch),
        compiler_params=_cp(("arbitrary",) * len(grid)),
    )(*args, *job.inputs)
    return res[:no], res[no:]


def _proj_dw(xn, dproj_sh, *, tm=512, tk=1024):
    s, d = xn.shape
    tk = _tile(s, tk)
    nk = s // tk

    def body(a_ref, b_ref, o_ref, acc):
        kk = pl.program_id(2)
        part = _dot_tn(a_ref[...], b_ref[0])

        @pl.when(kk == 0)
        def _():
            acc[...] = part

        @pl.when(kk > 0)
        def _():
            acc[...] += part

        @pl.when(kk == nk - 1)
        def _():
            o_ref[0] = acc[...]

    return pl.pallas_call(
        body, name="proj_dw", grid=(N_CHIPS, d // tm, nk),
        in_specs=[pl.BlockSpec((tk, tm), lambda j, i, q: (q, i)), pl.BlockSpec((1, tk, W_IN_PAD), lambda j, i, q: (j, q, 0))],
        out_specs=pl.BlockSpec((1, tm, W_IN_PAD), lambda j, i, q: (j, i, 0)),
        out_shape=jax.ShapeDtypeStruct((N_CHIPS, d, W_IN_PAD), F32), scratch_shapes=[pltpu.VMEM((tm, W_IN_PAD), F32)],
        compiler_params=_cp(("parallel", "parallel", "arbitrary")),
    )(xn, dproj_sh)


def _proj_dx(dproj_sh, w_sh, *, tm=1024, side=None):
    s = dproj_sh.shape[1]
    d = w_sh.shape[1]
    tm = _tile(s, tm)

    def body(a_ref, b_ref, o_ref, acc):
        kk = pl.program_id(1)
        part = _dot_nt(a_ref[0], b_ref[0])

        @pl.when(kk == 0)
        def _():
            acc[...] = part

        @pl.when(kk > 0)
        def _():
            acc[...] += part

        @pl.when(kk == N_CHIPS - 1)
        def _():
            o_ref[...] = acc[...]

    own, extra = _hosted(
        body, name="proj_dx", grid=(s // tm, N_CHIPS),
        in_specs=[pl.BlockSpec((1, tm, W_IN_PAD), lambda i, q: (q, i, 0)), pl.BlockSpec((1, d, W_IN_PAD), lambda i, q: (q, 0, 0))],
        out_specs=[pl.BlockSpec((tm, d), lambda i, q: (i, 0))],
        out_shape=[jax.ShapeDtypeStruct((s, d), F32)], scratch_shapes=[pltpu.VMEM((tm, d), F32)],
        args=(dproj_sh, w_sh), sem=("parallel", "arbitrary"), side=side)
    return own[0] if side is None else (own[0], extra)


def _up_dx(dup, w_sh, *, tm=1024):
    s = dup.shape[1]
    d, wsh = w_sh.shape[1:]
    tm = _tile(s, tm)

    def body(a_ref, b_ref, o_ref, acc):
        kk = pl.program_id(1)
        part = _dot_nt(a_ref[0], b_ref[0])

        @pl.when(kk == 0)
        def _():
            acc[...] = part

        @pl.when(kk > 0)
        def _():
            acc[...] += part

        @pl.when(kk == N_CHIPS - 1)
        def _():
            o_ref[...] = acc[...]

    return pl.pallas_call(
        body, name="up_dx", grid=(s // tm, N_CHIPS),
        in_specs=[pl.BlockSpec((1, tm, wsh), lambda i, q: (q >> 1, i, q & 1)), pl.BlockSpec((1, d, wsh), lambda i, q: (q, 0, 0))],
        out_specs=pl.BlockSpec((tm, d), lambda i, q: (i, 0)),
        out_shape=jax.ShapeDtypeStruct((s, d), F32), scratch_shapes=[pltpu.VMEM((tm, d), F32)],
        compiler_params=_cp(("parallel", "arbitrary")),
    )(dup, w_sh)


def _up_dw(hn, dup, *, tk=1024):
    s, d = hn.shape
    wsh = 2 * D_FF // N_CHIPS
    tk = _tile(s, tk)
    nk = s // tk

    def body(a_ref, b_ref, o_ref, acc):
        kk = pl.program_id(1)
        part = _dot_tn(a_ref[...], b_ref[0])

        @pl.when(kk == 0)
        def _():
            acc[...] = part

        @pl.when(kk > 0)
        def _():
            acc[...] += part

        @pl.when(kk == nk - 1)
        def _():
            o_ref[0] = acc[...]

    return pl.pallas_call(
        body, name="up_dw", grid=(N_CHIPS, nk),
        in_specs=[pl.BlockSpec((tk, d), lambda j, q: (q, 0)), pl.BlockSpec((1, tk, wsh), lambda j, q: (j >> 1, q, j & 1))],
        out_specs=pl.BlockSpec((1, d, wsh), lambda j, q: (j, 0, 0)),
        out_shape=jax.ShapeDtypeStruct((N_CHIPS, d, wsh), F32), scratch_shapes=[pltpu.VMEM((d, wsh), F32)],
        compiler_params=_cp(("parallel", "arbitrary")),
    )(hn, dup)


BIG_ROWS =(IN_DIM // 4, Q_DIM // 4, D_INNER // 4, D_MODEL // 4, 2 * D_FF // 4, D_FF // 4)
PACK_ROWS = 5376


def _pack_shards(parts):
    rows = [p.reshape(-1, D_MODEL) for p in parts]
    pad = PACK_ROWS - sum(BIG_ROWS)
    return jnp.concatenate(rows + [jnp.zeros((pad, D_MODEL), rows[0].dtype)], axis=0)


def _unpack_shards(buf):
    out, off = [], 0
    for n in BIG_ROWS:
        out.append(buf[off:off + n])
        off += n
    return out


def _permute_cols_in(w):
    pad = jnp.zeros((w.shape[0], PW - IN_DIM), w.dtype)
    return jnp.concatenate([w[:, :6656], w[:, 6688:], w[:, 6656:6688], pad], axis=1)


def _unpermute_cols_in(g):
    return jnp.concatenate([g[:, :6656], g[:, O_DT:O_DT + 32], g[:, 6656:O_DT]], axis=1)


SMALL = ("norm1_w", "b_gate", "attn_sinks", "ssd_conv_b", "dt_bias", "a_log", "d_skip", "ssd_norm_w", "norm2_w",
         "ffn_conv_b", "final_norm_w", "ssd_conv_w", "ffn_conv_w")


def _pad128(v):
    v = v.reshape(-1)
    return jnp.pad(v, (0, (-v.shape[0]) % 128))


def _pack_small(parts):
    flat = jnp.concatenate([_pad128(p) for p in parts])
    flat = jnp.pad(flat, (0, (-flat.shape[0]) % 1024))
    return flat.reshape(-1, 128)


def _unpack_small(buf, shapes):
    flat, out, off = buf.reshape(-1), [], 0
    for shp in shapes:
        n = 1
        for q in shp:
            n *= q
        out.append(flat[off:off + n].reshape(shp))
        off += n + (-n) % 128
    return out


def _vec128(v):
    return jnp.pad(v.reshape(1, -1), ((0, 0), (0, 128 - v.shape[-1])))


def kernel(x, norm1_w, w_in, b_gate, attn_sinks, w_attn_o, ssd_conv_w, ssd_conv_b, dt_bias, a_log, d_skip, ssd_norm_w, w_ssd_o, w_out, norm2_w, w_up, ffn_conv_w, ffn_conv_b, w_down, final_norm_w, loss_target, m_norm1_w, m_w_in, m_b_gate, m_attn_sinks, m_w_attn_o, m_ssd_conv_w, m_ssd_conv_b, m_dt_bias, m_a_log, m_d_skip, m_ssd_norm_w, m_w_ssd_o, m_w_out, m_norm2_w, m_w_up, m_ffn_conv_w, m_ffn_conv_b, m_w_down, m_final_norm_w, v_norm1_w, v_w_in, v_b_gate, v_attn_sinks, v_w_attn_o, v_ssd_conv_w, v_ssd_conv_b, v_dt_bias, v_a_log, v_d_skip, v_ssd_norm_w, v_w_ssd_o, v_w_out, v_norm2_w, v_w_up, v_ffn_conv_w, v_ffn_conv_b, v_w_down, v_final_norm_w):
    ix, iy, ic = lax.axis_index("x"), lax.axis_index("y"), lax.axis_index("c")
    chip = 2 * ix + iy
    x2 = x[0]
    tgt = loss_target[0]
    s = x2.shape[0]

    wsh = IN_DIM // N_CHIPS
    big_shards = dict(w_in=jnp.pad(w_in[0], ((0, 0), (0, W_IN_PAD - wsh))), w_attn_o=w_attn_o[0], w_ssd_o=w_ssd_o[0],
                      w_out=w_out[0], w_up=w_up[0], w_down=w_down[0])
    gathered = {}
    (gathered["w_in"],) = _run_job(_GatherJob(("w_in",), [big_shards["w_in"]]), "gather_w_in")
    early = ("w_attn_o", "w_ssd_o", "w_out")
    gather_early = _GatherJob(early, [big_shards[n] for n in early], at=(0.0, 0.5, 0.8))
    gather_up = _GatherJob(("w_up",), [big_shards["w_up"]], at=(0.0, 0.55, 0.85))
    gather_down = _GatherJob(("w_down",), [big_shards["w_down"]], at=(0.0, 0.5, 0.8))
    gw = gathered["w_in"]
    lo, hi = O_GA - 3 * wsh, O_GA + N_SSD_HEADS - 3 * wsh
    w_in_p = jnp.concatenate([gw[0, :, :wsh], gw[1, :, :wsh], gw[2, :, :wsh], gw[3, :, :lo], gw[3, :, hi:wsh],
                              gw[3, :, lo:hi], jnp.zeros((D_MODEL, PW - IN_DIM), BF16)], axis=1)
    small_sh = _pack_small([ssd_conv_w[0], ffn_conv_w[0]])
    small_all = _all_gather_small(small_sh)
    sc_parts = [_unpack_small(small_all[j], [(4, XBC_DIM // 4), (3, 2 * D_FF // 4)]) for j in range(N_CHIPS)]
    ssd_cw = jnp.concatenate([p[0] for p in sc_parts], axis=1)
    ffn_cw = jnp.concatenate([p[1] for p in sc_parts], axis=1)

    sinks128 = _vec128(attn_sinks)
    dtb128, alog128, dskip128 = _vec128(dt_bias), _vec128(a_log), _vec128(d_skip)

    xn, xnt = _rms_fwd(x2, norm1_w, name="norm1_fwd", with_t=True)
    proj, got = _mm(xn, w_in_p, name="proj_fwd", tn=1280, side=gather_early)
    gathered.update(zip(early, got))
    qkvt = _mm(w_in_p[:, :O_Z], xnt, name="qkv_fwd", ta=True)
    attn_pre, (gathered["w_up"],) = _attn_fwd(qkvt, sinks128, side=gather_up)
    xbc = _ssd_conv_fwd(proj, ssd_cw, ssd_conv_b)
    (y_ssd, hprev), (gathered["w_down"],) = _ssd_fwd(xbc, proj, dtb128, alog128, dskip128, side=gather_down)
    full = {n: gathered[n].reshape(-1, D_MODEL) for n in ("w_attn_o", "w_ssd_o", "w_out", "w_down")}
    full["w_up"] = gathered["w_up"]
    attn = _mm(attn_pre, full["w_attn_o"], name="attn_o_fwd", ta=True)
    yn = _gate_norm_fwd(y_ssd, proj, ssd_norm_w)
    ssd_out = _mm(yn, full["w_ssd_o"], name="ssd_o_fwd")
    merged = _merge_fwd(proj, b_gate, attn, ssd_out)
    h1 = _mm(merged, full["w_out"], name="out_fwd", resid=x2)
    hn = _rms_fwd(h1, norm2_w, name="norm2_fwd")
    up = _mm(hn, full["w_up"], name="up_fwd")
    act = _ffn_act_fwd(up, ffn_cw, ffn_conv_b)
    h2 = _mm(act, full["w_down"], name="down_fwd", resid=h1, tk=1408)

    dh2, loss_blk, g_final = _loss_bwd(h2, tgt, final_norm_w.reshape(1, -1))
    dact = _mm(dh2, full["w_down"], name="down_dx", tb=True, tn=1408)
    g_down = _mm(act, dh2, name="down_dw", ta=True, tm=1408)
    dup, g_ffn_cw, g_ffn_cb = _ffn_act_bwd(dact, up, ffn_cw, ffn_conv_b)
    dhn = _up_dx(dup, full["w_up"])
    g_up = _up_dw(hn, dup)
    dh1, g_norm2 = _rms_bwd(dhn, h1, norm2_w, dh2, name="norm2_bwd")
    dmerged = _mm(dh1, full["w_out"], name="out_dx", tb=True)
    g_out = _mm(merged, dh1, name="out_dw", ta=True)
    dattn, dssd_out, dga, dgs, g_ba, g_bs = _merge_bwd(dmerged, proj, b_gate, attn, ssd_out)
    dyn = _mm(dssd_out, full["w_ssd_o"], name="ssd_o_dx", tb=True)
    g_ssd_o = _mm(yn, dssd_out, name="ssd_o_dw", ta=True)
    dy_ssd, dz, g_ssd_norm = _gate_norm_bwd(dyn, y_ssd, proj, ssd_norm_w)
    slot = lambda g: g.reshape(N_CHIPS, -1, D_MODEL)
    big_grads = {}
    red = ("w_down", "w_up")
    (dxbc, ddt, dvec), got = _ssd_bwd(xbc, proj, dtb128, alog128, dskip128, hprev, dy_ssd,
                                      side=_ReduceJob(red, [slot(g_down), g_up], at=(0.0, 0.3, 0.8, 0.95)))
    big_grads.update(zip(red, got))
    dxbc_raw, g_ssd_cw, g_ssd_cb = _ssd_conv_bwd(dxbc, proj, ssd_cw, ssd_conv_b)
    dattn_pre = _mm(full["w_attn_o"], dattn, name="attn_o_dx", tb=True)
    g_attn_o = _mm(attn_pre, dattn, name="attn_o_dw")
    red = ("w_out", "w_ssd_o", "w_attn_o")
    (dq, dk, dv, dsk), got = _attn_bwd(qkvt, sinks128, attn_pre, dattn_pre,
                                       side=_ReduceJob(red, [slot(g_out), slot(g_ssd_o), slot(g_attn_o)],
                                                       at=(0.0, 0.2, 0.5, 0.7)))
    big_grads.update(zip(red, got))
    pieces = [dq.T, dk.T, dv.T, dz, dxbc_raw, ddt[:, :N_SSD_HEADS], dga, dgs]
    shards_d, off = [[] for _ in range(N_CHIPS)], 0
    for p in pieces:
        for j in range(N_CHIPS):
            a, b = max(off, j * wsh), min(off + p.shape[1], (j + 1) * wsh)
            if a < b:
                shards_d[j].append(p[:, a - off:b - off])
        off += p.shape[1]
    zpad = jnp.zeros((s, W_IN_PAD - wsh), BF16)
    dproj_sh = jnp.stack([jnp.concatenate(sh + [zpad], axis=1) for sh in shards_d])
    g_in = _proj_dw(xn, dproj_sh)
    dxn, got = _proj_dx(dproj_sh, gathered["w_in"], side=_ReduceJob(("w_in",), [g_in], at=(0.0, 0.3, 0.8, 0.95)))
    big_grads["w_in"] = got[0][:, :wsh]
    dx, g_norm1 = _rms_bwd(dxn, x2, norm1_w, dh1, name="norm1_bwd")


    small_g = dict(
        norm1_w=g_norm1, b_gate=jnp.concatenate([g_ba, g_bs], axis=1), attn_sinks=dsk[0:1, :16], ssd_conv_b=g_ssd_cb,
        dt_bias=dvec[0:1, :32], a_log=dvec[1:2, :32], d_skip=dvec[2:3, :32], ssd_norm_w=g_ssd_norm, norm2_w=g_norm2,
        ffn_conv_b=jnp.concatenate([g_ffn_cb[0], g_ffn_cb[1]], axis=1), final_norm_w=g_final, ssd_conv_w=g_ssd_cw,
        ffn_conv_w=jnp.concatenate([g_ffn_cw[0], g_ffn_cw[1]], axis=1))
    small_buf = _pack_small([small_g[n] for n in SMALL] + [loss_blk])
    small_sum = _all_reduce_small(small_buf)
    small_shapes = [(1, D_MODEL), (1, 2 * D_MODEL), (1, 16), (1, XBC_DIM), (1, 32), (1, 32), (1, 32), (1, D_INNER),
                    (1, D_MODEL), (1, 2 * D_FF), (D_MODEL,), (4, XBC_DIM), (3, 2 * D_FF), (1, 128)]
    small_list = _unpack_small(small_sum, small_shapes)
    loss = small_list[-1][0, 0]
    grads = dict(zip(SMALL, small_list[:-1]))
    grads["ssd_conv_w"] = lax.dynamic_slice_in_dim(grads["ssd_conv_w"], chip * (XBC_DIM // 4), XBC_DIM // 4, axis=1)
    grads["ffn_conv_w"] = lax.dynamic_slice_in_dim(grads["ffn_conv_w"], chip * (2 * D_FF // 4), 2 * D_FF // 4, axis=1)
    grads.update(big_grads)

    weights = dict(norm1_w=norm1_w, w_in=w_in, b_gate=b_gate, attn_sinks=attn_sinks, w_attn_o=w_attn_o, ssd_conv_w=ssd_conv_w,
                   ssd_conv_b=ssd_conv_b, dt_bias=dt_bias, a_log=a_log, d_skip=d_skip, ssd_norm_w=ssd_norm_w, w_ssd_o=w_ssd_o,
                   w_out=w_out, norm2_w=norm2_w, w_up=w_up, ffn_conv_w=ffn_conv_w, ffn_conv_b=ffn_conv_b, w_down=w_down,
                   final_norm_w=final_norm_w)
    ms = dict(norm1_w=m_norm1_w, w_in=m_w_in, b_gate=m_b_gate, attn_sinks=m_attn_sinks, w_attn_o=m_w_attn_o,
              ssd_conv_w=m_ssd_conv_w, ssd_conv_b=m_ssd_conv_b, dt_bias=m_dt_bias, a_log=m_a_log, d_skip=m_d_skip,
              ssd_norm_w=m_ssd_norm_w, w_ssd_o=m_w_ssd_o, w_out=m_w_out, norm2_w=m_norm2_w, w_up=m_w_up,
              ffn_conv_w=m_ffn_conv_w, ffn_conv_b=m_ffn_conv_b, w_down=m_w_down, final_norm_w=m_final_norm_w)
    vs = dict(norm1_w=v_norm1_w, w_in=v_w_in, b_gate=v_b_gate, attn_sinks=v_attn_sinks, w_attn_o=v_w_attn_o,
              ssd_conv_w=v_ssd_conv_w, ssd_conv_b=v_ssd_conv_b, dt_bias=v_dt_bias, a_log=v_a_log, d_skip=v_d_skip,
              ssd_norm_w=v_ssd_norm_w, w_ssd_o=v_w_ssd_o, w_out=v_w_out, norm2_w=v_norm2_w, w_up=v_w_up,
              ffn_conv_w=v_ffn_conv_w, ffn_conv_b=v_ffn_conv_b, w_down=v_w_down, final_norm_w=v_final_norm_w)
    order = list(weights)
    deltas, new_m, new_v = {}, {}, {}
    for n in BIG:
        shp = weights[n].shape
        d_, m_, v_ = _adamw(weights[n][0], grads[n], ms[n][0], vs[n][0], name="adamw_" + n)
        deltas[n], new_m[n], new_v[n] = d_.reshape(shp), m_.reshape(shp), v_.reshape(shp)
    smalls = [n for n in order if n not in BIG]
    as2d = lambda a: a.reshape(-1, a.shape[-1])
    res = _adamw_many(*[[as2d(src[n][0] if src[n].ndim == 3 else src[n]) for n in smalls] for src in (weights, grads, ms, vs)])
    for i, n in enumerate(smalls):
        deltas[n], new_m[n], new_v[n] = (res[q * len(smalls) + i].reshape(weights[n].shape) for q in range(3))
    out_grads = [grads[n].reshape(weights[n].shape) for n in order]
    return (loss, dx[None], *out_grads, *[deltas[n] for n in order], *[new_m[n] for n in order], *[new_v[n] for n in order])
```

```python
import functools

import jax
import jax.numpy as jnp
from jax import lax
from jax.experimental import pallas as pl
from jax.experimental.pallas import tpu as pltpu

F32 = jnp.float32
BF16 = jnp.bfloat16
HI = lax.Precision.HIGHEST

D_MODEL = 1024
Q_DIM = 1024
KV_DIM = 256
D_INNER = 2048
BC_DIM = 512
XBC_DIM = 3072
N_SSD_HEADS = 32
D_FF = 2816
IN_DIM = 8736
BLK = 128
EPS = 1e-5
NEG = -1e30

O_Q, O_K, O_V, O_Z, O_X, O_GA, O_GS, O_DT = 0, 1024, 1280, 1536, 3584, 6656, 7680, 8704
PW = 8960

ADAM_LR, ADAM_B1, ADAM_B2, ADAM_EPS, ADAM_WD, ADAM_STEP = 0.001, 0.9, 0.999, 1e-08, 0.01, 10

VMEM_LIMIT = 52 * 1024 * 1024
MESH = pl.DeviceIdType.MESH


def _cp(sem=None):
    return pltpu.CompilerParams(dimension_semantics=sem, vmem_limit_bytes=VMEM_LIMIT)


def _dot(a, b, prec=None):
    return jnp.dot(a, b, preferred_element_type=F32, precision=prec)


def _dot_nt(a, b, prec=None):
    return lax.dot_general(a, b, (((1,), (1,)), ((), ())), preferred_element_type=F32, precision=prec)


def _dot_tn(a, b, prec=None):
    return lax.dot_general(a, b, (((0,), (0,)), ((), ())), preferred_element_type=F32, precision=prec)


def _sigmoid(x):
    return 0.5 * jnp.tanh(0.5 * x) + 0.5


def _tile(n, want):
    t = min(n, want)
    while n % t:
        t -= 128
    return t


def _mm(a, b, *, name, ta=False, tb=False, out_dtype=F32, resid=None, tm=1024, tn=1024, tk=1024, side=None):
    m, k = (a.shape[1], a.shape[0]) if ta else a.shape
    slots = b.ndim == 3
    if slots:
        n = b.shape[1] if tb else b.shape[0] * b.shape[2]
        tn, tk = (tn, b.shape[2]) if tb else (b.shape[2], tk)
    else:
        n = b.shape[0] if tb else b.shape[1]
    tm, tn, tk = _tile(m, tm), _tile(n, tn), _tile(k, tk)
    nk = k // tk
    dn = (((0 if ta else 1,), (1 if tb else 0,)), ((), ()))

    def body(*refs):
        if resid is None:
            a_ref, b_ref, o_ref, acc = refs
        else:
            a_ref, b_ref, r_ref, o_ref, acc = refs
        kk = pl.program_id(2)
        bv = b_ref[0] if slots else b_ref[...]
        part = lax.dot_general(a_ref[...].astype(BF16), bv.astype(BF16), dn, preferred_element_type=F32)

        @pl.when(kk == 0)
        def _():
            acc[...] = part

        @pl.when(kk > 0)
        def _():
            acc[...] += part

        @pl.when(kk == nk - 1)
        def _():
            r = acc[...]
            if resid is not None:
                r = r + r_ref[...]
            o_ref[...] = r.astype(out_dtype)

    a_spec = pl.BlockSpec((tk, tm), lambda i, j, q: (q, i)) if ta else pl.BlockSpec((tm, tk), lambda i, j, q: (i, q))
    if slots:
        b_spec = (pl.BlockSpec((1, tn, tk), lambda i, j, q: (q, j, 0)) if tb
                  else pl.BlockSpec((1, tk, tn), lambda i, j, q: (j, q, 0)))
    else:
        b_spec = pl.BlockSpec((tn, tk), lambda i, j, q: (j, q)) if tb else pl.BlockSpec((tk, tn), lambda i, j, q: (q, j))
    o_spec = pl.BlockSpec((tm, tn), lambda i, j, q: (i, j))
    ins, specs = [a, b], [a_spec, b_spec]
    if resid is not None:
        ins.append(resid)
        specs.append(o_spec)
    own, extra = _hosted(
        body, name=name, grid=(m // tm, n // tn, nk), in_specs=specs, out_specs=[o_spec],
        out_shape=[jax.ShapeDtypeStruct((m, n), out_dtype)], scratch_shapes=[pltpu.VMEM((tm, tn), F32)],
        args=ins, sem=("parallel", "parallel", "arbitrary"), side=side)
    return own[0] if side is None else (own[0], extra)


def _rms_fwd(x, w, *, name, tm=512, with_t=False):
    s, d = x.shape
    tm = _tile(s, tm)

    def body(x_ref, w_ref, o_ref, *t_ref):
        xv = x_ref[...]
        r = lax.rsqrt(jnp.mean(xv * xv, axis=-1, keepdims=True) + EPS)
        y = (xv * r) * w_ref[...]
        o_ref[...] = y.astype(BF16)
        if with_t:
            t_ref[0][...] = y.T.astype(BF16)

    row = pl.BlockSpec((tm, d), lambda i: (i, 0))
    res = pl.pallas_call(
        body, name=name, grid=(s // tm,), in_specs=[row, pl.BlockSpec((1, d), lambda i: (0, 0))],
        out_specs=[row] + [pl.BlockSpec((d, tm), lambda i: (0, i))] * with_t,
        out_shape=[jax.ShapeDtypeStruct((s, d), BF16)] + [jax.ShapeDtypeStruct((d, s), BF16)] * with_t,
        compiler_params=_cp(("parallel",)),
    )(x, w)
    return res if with_t else res[0]


def _rms_bwd(dy, x, w, resid, *, name, tm=512):
    s, d = x.shape
    tm = _tile(s, tm)

    def body(dy_ref, x_ref, w_ref, r_ref, dx_ref, dw_ref):
        i = pl.program_id(0)
        xv = x_ref[...]
        r = lax.rsqrt(jnp.mean(xv * xv, axis=-1, keepdims=True) + EPS)
        xh = xv * r
        dyv = dy_ref[...]
        g = dyv * w_ref[...]
        dx_ref[...] = r_ref[...] + r * (g - xh * jnp.mean(g * xh, axis=-1, keepdims=True))
        part = jnp.sum(dyv * xh, axis=0, keepdims=True)

        @pl.when(i == 0)
        def _():
            dw_ref[...] = part

        @pl.when(i > 0)
        def _():
            dw_ref[...] += part

    row = pl.BlockSpec((tm, d), lambda i: (i, 0))
    vec = pl.BlockSpec((1, d), lambda i: (0, 0))
    return pl.pallas_call(
        body, name=name, grid=(s // tm,), in_specs=[row, row, vec, row], out_specs=[row, vec],
        out_shape=[jax.ShapeDtypeStruct((s, d), F32), jax.ShapeDtypeStruct((1, d), F32)],
        compiler_params=_cp(("arbitrary",)),
    )(dy, x, w, resid)


def _loss_bwd(h2, tgt, wf, *, tm=512):
    s, d = h2.shape
    tm = _tile(s, tm)

    def body(h_ref, t_ref, w_ref, dh_ref, loss_ref, dw_ref):
        i = pl.program_id(0)
        hv = h_ref[...]
        r = lax.rsqrt(jnp.mean(hv * hv, axis=-1, keepdims=True) + EPS)
        xh = hv * r
        wv = w_ref[...]
        e = xh * wv - t_ref[...]
        lpart = 0.5 * jnp.sum(jnp.mean(e * e, axis=-1, keepdims=True), axis=0, keepdims=True)
        dout = e * (1.0 / d)
        g = dout * wv
        dh_ref[...] = r * (g - xh * jnp.mean(g * xh, axis=-1, keepdims=True))
        part = jnp.sum(dout * xh, axis=0, keepdims=True)
        lrow = jnp.broadcast_to(lpart, (1, 128))

        @pl.when(i == 0)
        def _():
            dw_ref[...] = part
            loss_ref[...] = lrow

        @pl.when(i > 0)
        def _():
            dw_ref[...] += part
            loss_ref[...] += lrow

    row = pl.BlockSpec((tm, d), lambda i: (i, 0))
    vec = pl.BlockSpec((1, d), lambda i: (0, 0))
    return pl.pallas_call(
        body, name="loss_bwd", grid=(s // tm,), in_specs=[row, row, vec],
        out_specs=[row, pl.BlockSpec((1, 128), lambda i: (0, 0)), vec],
        out_shape=[jax.ShapeDtypeStruct((s, d), F32), jax.ShapeDtypeStruct((1, 128), F32),
                   jax.ShapeDtypeStruct((1, d), F32)],
        compiler_params=_cp(("arbitrary",)),
    )(h2, tgt, wf)


def _attn_mask(n):
    si = lax.broadcasted_iota(jnp.int32, (2 * BLK, 4 * BLK), 0)
    qi = lax.broadcasted_iota(jnp.int32, (2 * BLK, 4 * BLK), 1) & (BLK - 1)
    dist = BLK + qi - si
    kpos = n * BLK - BLK + si
    return (dist >= 0) & (dist < BLK) & (kpos >= 0)


def _attn_probs(q_ref, kc_ref, kp_ref, sk_ref, kvh, valid):
    rows = slice(kvh * 64, (kvh + 1) * 64)
    kt = jnp.concatenate([kp_ref[rows, :], kc_ref[rows, :]], axis=1).astype(BF16)
    qt = jnp.concatenate([q_ref[(kvh * 4 + g) * 64:(kvh * 4 + g + 1) * 64, :] for g in range(4)], axis=1).astype(BF16)
    s = _dot_tn(kt, qt) * 0.125
    s = jnp.where(valid, s, NEG)
    head = lax.broadcasted_iota(jnp.int32, (1, 4 * BLK), 1) >> 7
    sink = jnp.zeros((1, 4 * BLK), F32)
    for g in range(4):
        sink = jnp.where(head == g, sk_ref[0:1, kvh * 4 + g:kvh * 4 + g + 1], sink)
    m = jnp.maximum(jnp.max(s, axis=0, keepdims=True), sink)
    p = jnp.where(valid, jnp.exp(s - m), 0.0)
    es = jnp.exp(sink - m)
    inv = 1.0 / (jnp.sum(p, axis=0, keepdims=True) + es)
    return qt, kt, p * inv, es * inv


def _attn_in_specs(cur, prev):
    return [pl.BlockSpec((Q_DIM, BLK), lambda n: (0, cur(n))),
            pl.BlockSpec((KV_DIM, BLK), lambda n: (O_K // KV_DIM, cur(n))),
            pl.BlockSpec((KV_DIM, BLK), lambda n: (O_K // KV_DIM, prev(n))),
            pl.BlockSpec((KV_DIM, BLK), lambda n: (O_V // KV_DIM, cur(n))),
            pl.BlockSpec((KV_DIM, BLK), lambda n: (O_V // KV_DIM, prev(n))),
            pl.BlockSpec((1, 128), lambda n: (0, 0))]


def _attn_fwd(qkvt, sinks, side=None):
    s = qkvt.shape[1]
    nb = s // BLK

    def body(q_ref, kc_ref, kp_ref, vc_ref, vp_ref, sk_ref, o_ref):
        valid = _attn_mask(pl.program_id(0))
        for kvh in range(4):
            rows = slice(kvh * 64, (kvh + 1) * 64)
            _, _, probs, _ = _attn_probs(q_ref, kc_ref, kp_ref, sk_ref, kvh, valid)
            vt = jnp.concatenate([vp_ref[rows, :], vc_ref[rows, :]], axis=1).astype(BF16)
            o = _dot(vt, probs.astype(BF16))
            for g in range(4):
                h = kvh * 4 + g
                o_ref[h * 64:(h + 1) * 64, :] = o[:, g * BLK:(g + 1) * BLK].astype(BF16)

    own, extra = _hosted(
        body, name="attn_fwd", grid=(nb,), in_specs=_attn_in_specs(lambda n: n, lambda n: jnp.maximum(n - 1, 0)),
        out_specs=[pl.BlockSpec((Q_DIM, BLK), lambda n: (0, n))],
        out_shape=[jax.ShapeDtypeStruct((Q_DIM, s), BF16)], scratch_shapes=[],
        args=(qkvt, qkvt, qkvt, qkvt, qkvt, sinks), sem=("parallel",), side=side)
    return own[0] if side is None else (own[0], extra)


def _attn_bwd(qkvt, sinks, o, do, side=None):
    s = qkvt.shape[1]
    nb = s // BLK

    def body(q_ref, kc_ref, kp_ref, vc_ref, vp_ref, sk_ref, o_ref, do_ref, dq_ref, dk_ref, dv_ref, dsk_ref, ck, cv, nk, nv):
        n = pl.program_id(0)

        @pl.when(n == 0)
        def _():
            ck[...] = jnp.zeros_like(ck)
            cv[...] = jnp.zeros_like(cv)
            dsk_ref[...] = jnp.zeros_like(dsk_ref)

        @pl.when(n < nb)
        def _():
            valid = _attn_mask(n)
            lane = lax.broadcasted_iota(jnp.int32, (1, 128), 1)
            dsk = jnp.zeros((1, 128), F32)
            for kvh in range(4):
                rows = slice(kvh * 64, (kvh + 1) * 64)
                qt, kt, probs, psink = _attn_probs(q_ref, kc_ref, kp_ref, sk_ref, kvh, valid)
                vt = jnp.concatenate([vp_ref[rows, :], vc_ref[rows, :]], axis=1).astype(BF16)
                heads = [slice((kvh * 4 + g) * 64, (kvh * 4 + g + 1) * 64) for g in range(4)]
                dot = jnp.concatenate([do_ref[hh, :] for hh in heads], axis=1)
                ot = jnp.concatenate([o_ref[hh, :] for hh in heads], axis=1).astype(F32)
                delta = jnp.sum(dot * ot, axis=0, keepdims=True)
                dot16 = dot.astype(BF16)
                dp = _dot_tn(vt, dot16)
                ds = (probs * (dp - delta) * 0.125).astype(BF16)
                dqt = _dot(kt, ds)
                nk[rows, :] = _dot_nt(qt, ds)
                nv[rows, :] = _dot_nt(dot16, probs.astype(BF16))
                sd = psink * delta
                for g in range(4):
                    dq_ref[heads[g], :] = dqt[:, g * BLK:(g + 1) * BLK].astype(BF16)
                    val = -jnp.sum(sd[:, g * BLK:(g + 1) * BLK], axis=1, keepdims=True)
                    dsk = dsk + jnp.where(lane == kvh * 4 + g, val, 0.0)
            dsk_ref[0:1, :] += dsk
            dk_ref[...] = (ck[...] + nk[:, :BLK]).astype(BF16)
            dv_ref[...] = (cv[...] + nv[:, :BLK]).astype(BF16)
            ck[...] = nk[:, BLK:]
            cv[...] = nv[:, BLK:]

        @pl.when(n == nb)
        def _():
            dk_ref[...] = ck[...].astype(BF16)
            dv_ref[...] = cv[...].astype(BF16)

    cur = lambda n: jnp.minimum(n, nb - 1)
    prev = lambda n: jnp.maximum(jnp.minimum(n, nb - 1) - 1, 0)
    outb = lambda n: jnp.maximum(n - 1, 0)
    own, extra = _hosted(
        body, name="attn_bwd", grid=(nb + 1,),
        in_specs=_attn_in_specs(cur, prev) + [pl.BlockSpec((Q_DIM, BLK), lambda n: (0, cur(n))),
                                              pl.BlockSpec((Q_DIM, BLK), lambda n: (0, cur(n)))],
        out_specs=[pl.BlockSpec((Q_DIM, BLK), lambda n: (0, cur(n))),
                   pl.BlockSpec((KV_DIM, BLK), lambda n: (0, outb(n))),
                   pl.BlockSpec((KV_DIM, BLK), lambda n: (0, outb(n))),
                   pl.BlockSpec((8, 128), lambda n: (0, 0))],
        out_shape=[jax.ShapeDtypeStruct((Q_DIM, s), BF16), jax.ShapeDtypeStruct((KV_DIM, s), BF16),
                   jax.ShapeDtypeStruct((KV_DIM, s), BF16), jax.ShapeDtypeStruct((8, 128), F32)],
        scratch_shapes=[pltpu.VMEM((KV_DIM, BLK), F32)] * 2 + [pltpu.VMEM((KV_DIM, 2 * BLK), F32)] * 2,
        args=(qkvt, qkvt, qkvt, qkvt, qkvt, sinks, o, do), sem=("arbitrary",), side=side)
    return own if side is None else (own, extra)


def _shift_down(x, j):
    if j == 0:
        return x
    row = lax.broadcasted_iota(jnp.int32, x.shape, 0)
    return jnp.where(row >= j, pltpu.roll(x, j, 0), 0.0)


def _shift_up(x, j):
    if j == 0:
        return x
    s = x.shape[0]
    row = lax.broadcasted_iota(jnp.int32, x.shape, 0)
    return jnp.where(row < s - j, pltpu.roll(x, s - j, 0), 0.0)


def _conv(x, w_ref, b_ref):
    kk = w_ref.shape[0]
    y = _shift_down(x, kk - 1) * w_ref[0:1, :]
    for q in range(1, kk):
        y = y + _shift_down(x, kk - 1 - q) * w_ref[q:q + 1, :]
    return y + b_ref[...]


def _conv_bwd(dy, x, w_ref, dx_dtype):
    kk = w_ref.shape[0]
    dx = _shift_up(dy, kk - 1) * w_ref[0:1, :]
    dws = [jnp.sum(dy * _shift_down(x, kk - 1), axis=0, keepdims=True)]
    for q in range(1, kk):
        dx = dx + _shift_up(dy, kk - 1 - q) * w_ref[q:q + 1, :]
        dws.append(jnp.sum(dy * _shift_down(x, kk - 1 - q), axis=0, keepdims=True))
    return dx.astype(dx_dtype), dws, jnp.sum(dy, axis=0, keepdims=True)


def _dsilu(y, sg):
    return sg * (1.0 + y * (1.0 - sg))


CT = 256


def _ssd_conv_fwd(proj, w, b):
    s = proj.shape[0]

    def body(x_ref, w_ref, b_ref, o_ref):
        y = _conv(x_ref[...], w_ref, b_ref)
        o_ref[...] = y * _sigmoid(y)

    return pl.pallas_call(
        body, name="ssd_conv_fwd", grid=(XBC_DIM // CT,),
        in_specs=[pl.BlockSpec((s, CT), lambda i: (0, O_X // CT + i)), pl.BlockSpec((4, CT), lambda i: (0, i)),
                  pl.BlockSpec((1, CT), lambda i: (0, i))],
        out_specs=pl.BlockSpec((s, CT), lambda i: (0, i)),
        out_shape=jax.ShapeDtypeStruct((s, XBC_DIM), F32), compiler_params=_cp(("parallel",)),
    )(proj, w, b)


def _ssd_conv_bwd(dact, proj, w, b):
    s = proj.shape[0]

    def body(d_ref, x_ref, w_ref, b_ref, dx_ref, dw_ref, db_ref):
        x = x_ref[...]
        y = _conv(x, w_ref, b_ref)
        dy = d_ref[...] * _dsilu(y, _sigmoid(y))
        dx, dws, db = _conv_bwd(dy, x, w_ref, BF16)
        dx_ref[...] = dx
        for q in range(4):
            dw_ref[q:q + 1, :] = dws[q]
        db_ref[...] = db

    return pl.pallas_call(
        body, name="ssd_conv_bwd", grid=(XBC_DIM // CT,),
        in_specs=[pl.BlockSpec((s, CT), lambda i: (0, i)), pl.BlockSpec((s, CT), lambda i: (0, O_X // CT + i)),
                  pl.BlockSpec((4, CT), lambda i: (0, i)), pl.BlockSpec((1, CT), lambda i: (0, i))],
        out_specs=[pl.BlockSpec((s, CT), lambda i: (0, i)), pl.BlockSpec((4, CT), lambda i: (0, i)),
                   pl.BlockSpec((1, CT), lambda i: (0, i))],
        out_shape=[jax.ShapeDtypeStruct((s, XBC_DIM), BF16), jax.ShapeDtypeStruct((4, XBC_DIM), F32),
                   jax.ShapeDtypeStruct((1, XBC_DIM), F32)],
        compiler_params=_cp(("parallel",)),
    )(dact, proj, w, b)


NFT = D_FF // CT


def _ffn_act_fwd(up, w, b):
    s = up.shape[0]

    def body(v_ref, g_ref, wv_ref, wg_ref, bv_ref, bg_ref, o_ref):
        val = _conv(v_ref[...], wv_ref, bv_ref)
        gt = _conv(g_ref[...], wg_ref, bg_ref)
        o_ref[...] = ((gt * _sigmoid(gt)) * val).astype(BF16)

    col = lambda off: (lambda i: (0, off + i))
    return pl.pallas_call(
        body, name="ffn_act_fwd", grid=(NFT,),
        in_specs=[pl.BlockSpec((s, CT), col(0)), pl.BlockSpec((s, CT), col(NFT)),
                  pl.BlockSpec((3, CT), col(0)), pl.BlockSpec((3, CT), col(NFT)),
                  pl.BlockSpec((1, CT), col(0)), pl.BlockSpec((1, CT), col(NFT))],
        out_specs=pl.BlockSpec((s, CT), col(0)),
        out_shape=jax.ShapeDtypeStruct((s, D_FF), BF16), compiler_params=_cp(("parallel",)),
    )(up, up, w, w, b, b)


def _ffn_act_bwd(dact, up, w, b):
    s = up.shape[0]

    def body(d_ref, v_ref, g_ref, wv_ref, wg_ref, bv_ref, bg_ref, dx_ref, dw_ref, db_ref):
        xv, xg = v_ref[...], g_ref[...]
        val = _conv(xv, wv_ref, bv_ref)
        gt = _conv(xg, wg_ref, bg_ref)
        sg = _sigmoid(gt)
        d = d_ref[...]
        for half, (dy, x, w_ref) in enumerate(((d * (gt * sg), xv, wv_ref), (d * val * _dsilu(gt, sg), xg, wg_ref))):
            dx, dws, db = _conv_bwd(dy, x, w_ref, BF16)
            dx_ref[half] = dx
            for q in range(3):
                dw_ref[half, q:q + 1, :] = dws[q]
            db_ref[half] = db

    col = lambda off: (lambda i: (0, off + i))
    both = lambda i: (0, 0, i)
    return pl.pallas_call(
        body, name="ffn_act_bwd", grid=(NFT,),
        in_specs=[pl.BlockSpec((s, CT), col(0)), pl.BlockSpec((s, CT), col(0)), pl.BlockSpec((s, CT), col(NFT)),
                  pl.BlockSpec((3, CT), col(0)), pl.BlockSpec((3, CT), col(NFT)),
                  pl.BlockSpec((1, CT), col(0)), pl.BlockSpec((1, CT), col(NFT))],
        out_specs=[pl.BlockSpec((2, s, CT), both), pl.BlockSpec((2, 3, CT), both), pl.BlockSpec((2, 1, CT), both)],
        out_shape=[jax.ShapeDtypeStruct((2, s, D_FF), BF16), jax.ShapeDtypeStruct((2, 3, D_FF), F32),
                   jax.ShapeDtypeStruct((2, 1, D_FF), F32)],
        compiler_params=_cp(("parallel",)),
    )(dact, up, up, w, w, b, b)


def _expand_mat():
    r = lax.broadcasted_iota(jnp.int32, (128, D_INNER), 0)
    c = lax.broadcasted_iota(jnp.int32, (128, D_INNER), 1)
    return ((c >> 6) == r).astype(BF16)


def _reduce_mat():
    r = lax.broadcasted_iota(jnp.int32, (D_INNER, 128), 0)
    c = lax.broadcasted_iota(jnp.int32, (D_INNER, 128), 1)
    return ((r >> 6) == c).astype(BF16)


def _split(v, parts):
    out = []
    for _ in range(parts - 1):
        p = v.astype(BF16)
        out.append(p)
        v = v - p.astype(F32)
    out.append(v.astype(BF16))
    return out


def _sel_dot(v, sel, parts):
    acc = None
    for p in reversed(_split(v, parts)):
        t = _dot(p, sel)
        acc = t if acc is None else acc + t
    return acc


def _row8(v):
    return jnp.broadcast_to(v, (8, v.shape[1]))


def _tril():
    r = lax.broadcasted_iota(jnp.int32, (BLK, BLK), 0)
    c = lax.broadcasted_iota(jnp.int32, (BLK, BLK), 1)
    return r >= c


def _softplus(x):
    return jnp.maximum(x, 0.0) + jnp.log(1.0 + jnp.exp(-jnp.abs(x)))


def _ssd_common(dtraw_ref, dtb_ref, alog_ref):
    causal = _tril()
    e_mat = _expand_mat()
    a_neg = -jnp.exp(alog_ref[...])
    dt = _softplus(dtraw_ref[...] + dtb_ref[...])
    a_cs = _dot(causal.astype(F32), dt * a_neg, HI)
    a_cs_t = a_cs.T
    dt_x = _sel_dot(dt, e_mat, 3)
    acs_x = _sel_dot(a_cs, e_mat, 3)
    alast_x = acs_x[BLK - 1:BLK, :]
    ea_x = jnp.exp(acs_x)
    ds_x = jnp.exp(alast_x - acs_x)
    elast_x = jnp.exp(alast_x)
    return causal, e_mat, a_neg, dt, a_cs, a_cs_t, dt_x, ea_x, ds_x, elast_x


def _decay(a_cs, a_cs_t, h, causal):
    seg = a_cs[:, h:h + 1] - a_cs_t[h:h + 1, :]
    return jnp.where(causal, jnp.exp(jnp.where(causal, seg, 0.0)), 0.0)


def _ssd_fwd(xbc, proj, dt_bias, a_log, d_skip, side=None):
    s = xbc.shape[0]
    nc = s // BLK

    def body(xs_ref, b_ref, c_ref, dtraw_ref, dtb_ref, alog_ref, dskip_ref, y_ref, hp_ref, h_scr, xc16):
        @pl.when(pl.program_id(0) == 0)
        def _():
            h_scr[...] = jnp.zeros_like(h_scr)

        causal, e_mat, _, _, a_cs, a_cs_t, dt_x, ea_x, ds_x, elast_x = _ssd_common(dtraw_ref, dtb_ref, alog_ref)
        dskip_x = _sel_dot(_row8(dskip_ref[...]), e_mat, 3)[0:1]
        xs = xs_ref[...]
        xc = xs * dt_x
        xc16[...] = xc.astype(BF16)
        xcd = (xc * ds_x).astype(BF16)
        hp_ref[0] = h_scr[...]
        for g in range(4):
            gs = slice(g * 512, (g + 1) * 512)
            cg = c_ref[:, g * 128:(g + 1) * 128].astype(BF16)
            bg = b_ref[:, g * 128:(g + 1) * 128].astype(BF16)
            cb = _dot_nt(cg, bg)
            hg = h_scr[:, gs]
            yoff = _dot(cg, hg.astype(BF16)) * ea_x[:, gs]
            for j in range(8):
                h = g * 8 + j
                hsl = slice(h * 64, (h + 1) * 64)
                mm = (cb * _decay(a_cs, a_cs_t, h, causal)).astype(BF16)
                y_ref[:, hsl] = _dot(mm, xc16[:, hsl])
            y_ref[:, gs] += yoff + xs[:, gs] * dskip_x[:, gs]
            h_scr[:, gs] = hg * elast_x[:, gs] + _dot_tn(bg, xcd[:, gs])

    vec = pl.BlockSpec((1, 128), lambda c: (0, 0))
    own, extra = _hosted(
        body, name="ssd_fwd", grid=(nc,),
        in_specs=[pl.BlockSpec((BLK, D_INNER), lambda c: (c, 0)),
                  pl.BlockSpec((BLK, BC_DIM), lambda c: (c, D_INNER // BC_DIM)),
                  pl.BlockSpec((BLK, BC_DIM), lambda c: (c, D_INNER // BC_DIM + 1)),
                  pl.BlockSpec((BLK, 128), lambda c: (c, O_DT // 128)), vec, vec, vec],
        out_specs=[pl.BlockSpec((BLK, D_INNER), lambda c: (c, 0)),
                   pl.BlockSpec((1, 128, D_INNER), lambda c: (c, 0, 0))],
        out_shape=[jax.ShapeDtypeStruct((s, D_INNER), F32), jax.ShapeDtypeStruct((nc, 128, D_INNER), F32)],
        scratch_shapes=[pltpu.VMEM((128, D_INNER), F32), pltpu.VMEM((BLK, D_INNER), BF16)],
        args=(xbc, xbc, xbc, proj, dt_bias, a_log, d_skip), sem=("arbitrary",), side=side)
    return own if side is None else (own, extra)


def _ssd_bwd(xbc, proj, dt_bias, a_log, d_skip, hprev, dy, side=None):
    s = xbc.shape[0]
    nc = s // BLK

    def body(xs_ref, b_ref, c_ref, dtraw_ref, dtb_ref, alog_ref, dskip_ref, hp_ref, dy_ref,
             dxbc_ref, ddt_ref, dvec_ref, dh_scr, xc16, dy16, dxc_scr, dacs_r, tdiff):
        step = pl.program_id(0)
        dacs_r[...] = jnp.zeros_like(dacs_r)

        @pl.when(step == 0)
        def _():
            dh_scr[...] = jnp.zeros_like(dh_scr)
            dvec_ref[...] = jnp.zeros_like(dvec_ref)

        causal, e_mat, a_neg, dt, a_cs, a_cs_t, dt_x, ea_x, ds_x, elast_x = _ssd_common(dtraw_ref, dtb_ref, alog_ref)
        r_mat = _reduce_mat()
        lane = lax.broadcasted_iota(jnp.int32, (1, 128), 1)
        dskip_x = _sel_dot(_row8(dskip_ref[...]), e_mat, 3)[0:1]
        xs = xs_ref[...]
        dy = dy_ref[...]
        xc = xs * dt_x
        xcd = xc * ds_x
        xc16[...] = xc.astype(BF16)
        dy16[...] = dy.astype(BF16)
        dyea = dy * ea_x
        dh = dh_scr[...]
        hp = hp_ref[0]
        dalast_x = jnp.sum(dh * hp, axis=0, keepdims=True) * elast_x
        dacs = jnp.zeros((BLK, 128), F32)
        for g in range(4):
            gs = slice(g * 512, (g + 1) * 512)
            bsl = slice(g * 128, (g + 1) * 128)
            cg = c_ref[:, bsl].astype(BF16)
            bg = b_ref[:, bsl].astype(BF16)
            cb = _dot_nt(cg, bg)
            hg16 = hp[:, gs].astype(BF16)
            dhg16 = dh[:, gs].astype(BF16)
            raw = _dot(cg, hg16)
            draw16 = dyea[:, gs].astype(BF16)
            dcg = _dot_nt(draw16, hg16)
            dhp_g = _dot_tn(cg, draw16)
            dbg = _dot_nt(xcd[:, gs].astype(BF16), dhg16)
            dxcd = _dot(bg, dhg16)
            dcb = jnp.zeros((BLK, BLK), F32)
            for j in range(8):
                h = g * 8 + j
                hsl = slice(h * 64, (h + 1) * 64)
                decay = _decay(a_cs, a_cs_t, h, causal)
                m = cb * decay
                dm = _dot_nt(dy16[:, hsl], xc16[:, hsl])
                dxc_scr[:, hsl] = _dot_tn(m.astype(BF16), dy16[:, hsl])
                dcb = dcb + dm * decay
                dseg = dm * m
                oneh = jnp.where(lane == h, 1.0, 0.0)
                dacs = dacs + jnp.sum(dseg, axis=1, keepdims=True) * oneh
                dacs_r[h:h + 1, :] = jnp.sum(dseg, axis=0, keepdims=True)
            dcb16 = dcb.astype(BF16)
            dcg = dcg + _dot(dcb16, bg)
            dbg = dbg + _dot_tn(dcb16, cg)
            dxbc_ref[:, D_INNER + g * 128:D_INNER + (g + 1) * 128] = dbg
            dxbc_ref[:, D_INNER + BC_DIM + g * 128:D_INNER + BC_DIM + (g + 1) * 128] = dcg
            dxc_scr[:, gs] += dxcd * ds_x[:, gs]
            dh_scr[:, gs] = dh[:, gs] * elast_x[:, gs] + dhp_g
            tst = dxcd * xcd[:, gs]
            tdiff[:, gs] = dy[:, gs] * (raw * ea_x[:, gs]) - tst
            tdiff[BLK - 1:BLK, gs] += jnp.sum(tst, axis=0, keepdims=True)
        dxc = dxc_scr[...]
        row = lax.broadcasted_iota(jnp.int32, (BLK, D_INNER), 0)
        tfull = tdiff[...] + jnp.where(row == BLK - 1, dalast_x, 0.0)
        dacs = dacs + _sel_dot(tfull, r_mat, 2) - dacs_r[...].T
        da = _dot_tn(causal.astype(F32), dacs, HI)
        ddt = da * a_neg + _sel_dot(dxc * xs, r_mat, 2)
        lmask = lax.broadcasted_iota(jnp.int32, (BLK, 128), 1) < N_SSD_HEADS
        ddtraw = jnp.where(lmask, ddt * _sigmoid(dtraw_ref[...] + dtb_ref[...]), 0.0)
        ddt_ref[...] = ddtraw.astype(BF16)
        dxbc_ref[:, 0:D_INNER] = dy * dskip_x + dxc * dt_x
        dvec_ref[0:1, :] += jnp.sum(ddtraw, axis=0, keepdims=True)
        dvec_ref[1:2, :] += jnp.where(lane < N_SSD_HEADS, jnp.sum(da * dt, axis=0, keepdims=True) * a_neg, 0.0)
        dvec_ref[2:3, :] += _sel_dot(_row8(jnp.sum(dy * xs, axis=0, keepdims=True)), r_mat, 3)[0:1]

    rev = lambda c: nc - 1 - c
    vec = pl.BlockSpec((1, 128), lambda c: (0, 0))
    own, extra = _hosted(
        body, name="ssd_bwd", grid=(nc,),
        in_specs=[pl.BlockSpec((BLK, D_INNER), lambda c: (rev(c), 0)),
                  pl.BlockSpec((BLK, BC_DIM), lambda c: (rev(c), D_INNER // BC_DIM)),
                  pl.BlockSpec((BLK, BC_DIM), lambda c: (rev(c), D_INNER // BC_DIM + 1)),
                  pl.BlockSpec((BLK, 128), lambda c: (rev(c), O_DT // 128)), vec, vec, vec,
                  pl.BlockSpec((1, 128, D_INNER), lambda c: (rev(c), 0, 0)),
                  pl.BlockSpec((BLK, D_INNER), lambda c: (rev(c), 0))],
        out_specs=[pl.BlockSpec((BLK, XBC_DIM), lambda c: (rev(c), 0)),
                   pl.BlockSpec((BLK, 128), lambda c: (rev(c), 0)),
                   pl.BlockSpec((8, 128), lambda c: (0, 0))],
        out_shape=[jax.ShapeDtypeStruct((s, XBC_DIM), F32), jax.ShapeDtypeStruct((s, 128), BF16),
                   jax.ShapeDtypeStruct((8, 128), F32)],
        scratch_shapes=[pltpu.VMEM((128, D_INNER), F32), pltpu.VMEM((BLK, D_INNER), BF16),
                        pltpu.VMEM((BLK, D_INNER), BF16), pltpu.VMEM((BLK, D_INNER), F32),
                        pltpu.VMEM((128, BLK), F32), pltpu.VMEM((BLK, D_INNER), F32)],
        args=(xbc, xbc, xbc, proj, dt_bias, a_log, d_skip, hprev, dy), sem=("arbitrary",), side=side)
    return own if side is None else (own, extra)


GW = 512


def _gate_norm_fwd(y, proj, wn, *, tm=512):
    s = y.shape[0]
    tm = _tile(s, tm)

    def body(y_ref, z_ref, w_ref, o_ref):
        z = z_ref[...]
        y2 = y_ref[...] * (z * _sigmoid(z))
        r = lax.rsqrt(jnp.mean(y2 * y2, axis=-1, keepdims=True) + EPS)
        o_ref[...] = ((y2 * r) * w_ref[...]).astype(BF16)

    return pl.pallas_call(
        body, name="gate_norm_fwd", grid=(s // tm, 4),
        in_specs=[pl.BlockSpec((tm, GW), lambda i, g: (i, g)), pl.BlockSpec((tm, GW), lambda i, g: (i, O_Z // GW + g)),
                  pl.BlockSpec((1, GW), lambda i, g: (0, g))],
        out_specs=pl.BlockSpec((tm, GW), lambda i, g: (i, g)),
        out_shape=jax.ShapeDtypeStruct((s, D_INNER), BF16), compiler_params=_cp(("parallel", "parallel")),
    )(y, proj, wn)


def _gate_norm_bwd(dyn, y, proj, wn, *, tm=512):
    s = y.shape[0]
    tm = _tile(s, tm)

    def body(d_ref, y_ref, z_ref, w_ref, dy_ref, dz_ref, dw_ref):
        i = pl.program_id(1)
        z = z_ref[...]
        sg = _sigmoid(z)
        sz = z * sg
        yv = y_ref[...]
        y2 = yv * sz
        r = lax.rsqrt(jnp.mean(y2 * y2, axis=-1, keepdims=True) + EPS)
        xh = y2 * r
        dv = d_ref[...]
        g = dv * w_ref[...]
        dy2 = r * (g - xh * jnp.mean(g * xh, axis=-1, keepdims=True))
        dy_ref[...] = dy2 * sz
        dz_ref[...] = (dy2 * yv * _dsilu(z, sg)).astype(BF16)
        part = jnp.sum(dv * xh, axis=0, keepdims=True)

        @pl.when(i == 0)
        def _():
            dw_ref[...] = part

        @pl.when(i > 0)
        def _():
            dw_ref[...] += part

    blk = pl.BlockSpec((tm, GW), lambda g, i: (i, g))
    vec = pl.BlockSpec((1, GW), lambda g, i: (0, g))
    return pl.pallas_call(
        body, name="gate_norm_bwd", grid=(4, s // tm),
        in_specs=[blk, blk, pl.BlockSpec((tm, GW), lambda g, i: (i, O_Z // GW + g)), vec],
        out_specs=[blk, blk, vec],
        out_shape=[jax.ShapeDtypeStruct((s, D_INNER), F32), jax.ShapeDtypeStruct((s, D_INNER), BF16),
                   jax.ShapeDtypeStruct((1, D_INNER), F32)],
        compiler_params=_cp(("parallel", "arbitrary")),
    )(dyn, y, proj, wn)


def _merge_fwd(proj, b_gate, attn, ssd_out, *, tm=512):
    s = attn.shape[0]
    tm = _tile(s, tm)

    def body(ga_ref, gs_ref, ba_ref, bs_ref, a_ref, s_ref, o_ref):
        ga = _sigmoid(ga_ref[...] + ba_ref[...])
        gs = _sigmoid(gs_ref[...] + bs_ref[...])
        o_ref[...] = (ga * a_ref[...] + gs * s_ref[...]).astype(BF16)

    blk = pl.BlockSpec((tm, GW), lambda i, j: (i, j))
    return pl.pallas_call(
        body, name="merge_fwd", grid=(s // tm, 2),
        in_specs=[pl.BlockSpec((tm, GW), lambda i, j: (i, O_GA // GW + j)),
                  pl.BlockSpec((tm, GW), lambda i, j: (i, O_GS // GW + j)),
                  pl.BlockSpec((1, GW), lambda i, j: (0, j)), pl.BlockSpec((1, GW), lambda i, j: (0, 2 + j)), blk, blk],
        out_specs=blk, out_shape=jax.ShapeDtypeStruct((s, D_MODEL), BF16),
        compiler_params=_cp(("parallel", "parallel")),
    )(proj, proj, b_gate, b_gate, attn, ssd_out)


def _merge_bwd(dm, proj, b_gate, attn, ssd_out, *, tm=512):
    s = attn.shape[0]
    tm = _tile(s, tm)

    def body(d_ref, ga_ref, gs_ref, ba_ref, bs_ref, a_ref, s_ref, da_ref, ds_ref, dga_ref, dgs_ref, dba_ref, dbs_ref):
        i = pl.program_id(1)
        ga = _sigmoid(ga_ref[...] + ba_ref[...])
        gs = _sigmoid(gs_ref[...] + bs_ref[...])
        d = d_ref[...]
        da_ref[...] = (d * ga).astype(BF16)
        ds_ref[...] = (d * gs).astype(BF16)
        dga = d * a_ref[...] * (ga * (1.0 - ga))
        dgs = d * s_ref[...] * (gs * (1.0 - gs))
        dga_ref[...] = dga.astype(BF16)
        dgs_ref[...] = dgs.astype(BF16)
        pa = jnp.sum(dga, axis=0, keepdims=True)
        ps = jnp.sum(dgs, axis=0, keepdims=True)

        @pl.when(i == 0)
        def _():
            dba_ref[...] = pa
            dbs_ref[...] = ps

        @pl.when(i > 0)
        def _():
            dba_ref[...] += pa
            dbs_ref[...] += ps

    blk = pl.BlockSpec((tm, GW), lambda j, i: (i, j))
    vec = pl.BlockSpec((1, GW), lambda j, i: (0, j))
    sd = jax.ShapeDtypeStruct((s, D_MODEL), BF16)
    vd = jax.ShapeDtypeStruct((1, D_MODEL), F32)
    return pl.pallas_call(
        body, name="merge_bwd", grid=(2, s // tm),
        in_specs=[blk, pl.BlockSpec((tm, GW), lambda j, i: (i, O_GA // GW + j)),
                  pl.BlockSpec((tm, GW), lambda j, i: (i, O_GS // GW + j)),
                  vec, pl.BlockSpec((1, GW), lambda j, i: (0, 2 + j)), blk, blk],
        out_specs=[blk, blk, blk, blk, vec, vec], out_shape=[sd, sd, sd, sd, vd, vd],
        compiler_params=_cp(("parallel", "arbitrary")),
    )(dm, proj, proj, b_gate, b_gate, attn, ssd_out)


def _adamw_math(w, g, m, v):
    mn = ADAM_B1 * m + (1.0 - ADAM_B1) * g
    vn = ADAM_B2 * v + (1.0 - ADAM_B2) * (g * g)
    m_hat = mn / (1.0 - ADAM_B1 ** ADAM_STEP)
    v_hat = vn / (1.0 - ADAM_B2 ** ADAM_STEP)
    return -ADAM_LR * (m_hat / (jnp.sqrt(v_hat) + ADAM_EPS) + ADAM_WD * w), mn, vn


def _adamw_many(ws, gs, ms, vs):
    n = len(ws)

    def body(*refs):
        outs = refs[4 * n:]
        for i in range(n):
            res = _adamw_math(*[refs[q * n + i][...] for q in range(4)])
            for q in range(3):
                outs[q * n + i][...] = res[q]

    return pl.pallas_call(body, name="adamw_small", out_shape=[jax.ShapeDtypeStruct(w.shape, F32) for w in ws] * 3,
                          compiler_params=_cp())(*ws, *gs, *ms, *vs)


def _adamw(w, g, m, v, *, name, tm=128):
    r, c = w.shape
    tm = r if (r < tm or r % tm) else tm

    def body(w_ref, g_ref, m_ref, v_ref, d_ref, nm_ref, nv_ref):
        d_ref[...], nm_ref[...], nv_ref[...] = _adamw_math(w_ref[...], g_ref[...], m_ref[...], v_ref[...])

    blk = pl.BlockSpec((tm, c), lambda i: (i, 0))
    sd = jax.ShapeDtypeStruct((r, c), F32)
    return pl.pallas_call(
        body, name=name, grid=(r // tm,), in_specs=[blk] * 4, out_specs=[blk] * 3, out_shape=[sd] * 3,
        compiler_params=_cp(("parallel",)),
    )(w, g, m, v)


ANY = pl.BlockSpec(memory_space=pl.ANY)
N_CHIPS = 4


def _chip_of(k, x, y):
    return (x ^ (k >> 1), y ^ (k & 1))


def _all_gather_small(shard):
    r, c = shard.shape
    hr = r // 2

    def body(sh_ref, out_ref, send_sems, recv_sems, local_sem):
        x, y, cc = lax.axis_index("x"), lax.axis_index("y"), lax.axis_index("c")

        def half(px, py, pc):
            return out_ref.at[2 * px + py, pl.ds(pc * hr, hr), :]

        def copy(k, px, py, pc, to, src=None):
            return pltpu.make_async_remote_copy(
                src_ref=half(px, py, pc) if src is None else src, dst_ref=half(px, py, pc),
                send_sem=send_sems.at[k], recv_sem=recv_sems.at[k], device_id=to, device_id_type=MESH)

        mine = pltpu.make_async_copy(sh_ref, out_ref.at[2 * x + y], local_sem)
        mine.start()
        chips = [_chip_of(k, x, y) for k in (1, 2, 3)]
        first = [copy(j, x, y, cc, (*chip, cc), src=sh_ref.at[pl.ds(cc * hr, hr), :]) for j, chip in enumerate(chips)]
        for cp in first:
            cp.start()
        passed = [copy(3 + j, *chip, cc, (x, y, 1 - cc)) for j, chip in enumerate(chips)]
        for j, chip in enumerate(chips):
            copy(j, *chip, cc, (x, y, cc)).wait_recv()
            passed[j].start()
        for j, chip in enumerate(chips):
            copy(3 + j, *chip, 1 - cc, (x, y, cc)).wait_recv()
        for cp in first + passed:
            cp.wait_send()
        mine.wait()

    return pl.pallas_call(
        body, name="all_gather_small", in_specs=[ANY], out_specs=ANY,
        out_shape=jax.ShapeDtypeStruct((N_CHIPS, r, c), shard.dtype),
        scratch_shapes=[pltpu.SemaphoreType.DMA((6,)), pltpu.SemaphoreType.DMA((6,)), pltpu.SemaphoreType.DMA],
    )(shard)


def _cast_bf16(a, *, name, tm=512):
    n, r, c = a.shape
    tm = _tile(r, tm) if r % 128 == 0 else r

    def body(a_ref, o_ref):
        o_ref[...] = a_ref[...].astype(BF16)

    blk = pl.BlockSpec((1, tm, c), lambda i, j: (i, j, 0))
    return pl.pallas_call(body, name=name, grid=(n, r // tm), in_specs=[blk], out_specs=blk,
                          out_shape=jax.ShapeDtypeStruct(a.shape, BF16), compiler_params=_cp(("parallel", "parallel")))(a)


def _pair_exchange(g16, hr):
    n, r, c = g16.shape

    def body(g_ref, out_ref, send_sem, recv_sem):
        x, y, cc = lax.axis_index("x"), lax.axis_index("y"), lax.axis_index("c")
        cp = pltpu.make_async_remote_copy(
            src_ref=g_ref.at[:, pl.ds((1 - cc) * hr, hr), :], dst_ref=out_ref, send_sem=send_sem, recv_sem=recv_sem,
            device_id=(x, y, 1 - cc), device_id_type=MESH)
        cp.start()
        cp.wait()

    return pl.pallas_call(
        body, name="grad_pair_exchange", in_specs=[ANY], out_specs=ANY,
        out_shape=jax.ShapeDtypeStruct((n, hr, c), g16.dtype),
        scratch_shapes=[pltpu.SemaphoreType.DMA, pltpu.SemaphoreType.DMA],
    )(g16)


def _pair_add(g, recv, half_idx, hr, *, tm=384):
    n, r, c = g.shape
    nt = hr // tm

    def body(hi_ref, g_ref, r_ref, o32_ref, o16_ref):
        v = g_ref[...] + r_ref[...].astype(F32)
        o32_ref[...] = v
        o16_ref[...] = v.astype(BF16)

    gs = pltpu.PrefetchScalarGridSpec(
        num_scalar_prefetch=1, grid=(n, nt),
        in_specs=[pl.BlockSpec((1, tm, c), lambda i, j, hi: (i, hi[0] * nt + j, 0)),
                  pl.BlockSpec((1, tm, c), lambda i, j, hi: (i, j, 0))],
        out_specs=[pl.BlockSpec((1, tm, c), lambda i, j, hi: (i, j, 0))] * 2)
    return pl.pallas_call(
        body, name="grad_pair_add", grid_spec=gs,
        out_shape=[jax.ShapeDtypeStruct((n, hr, c), F32), jax.ShapeDtypeStruct((n, hr, c), BF16)],
        compiler_params=_cp(("parallel", "parallel")),
    )(half_idx, g, recv)


def _chip_exchange(p16):
    n, hr, c = p16.shape

    def body(p_ref, out_ref, send_sems, recv_sems):
        x, y, cc = lax.axis_index("x"), lax.axis_index("y"), lax.axis_index("c")
        cps = []
        for j, k in enumerate((1, 2, 3)):
            px, py = _chip_of(k, x, y)
            cps.append(pltpu.make_async_remote_copy(
                src_ref=p_ref.at[2 * px + py], dst_ref=out_ref.at[j], send_sem=send_sems.at[j], recv_sem=recv_sems.at[j],
                device_id=(px, py, cc), device_id_type=MESH))
        for cp in cps:
            cp.start()
        for cp in cps:
            cp.wait()

    return pl.pallas_call(
        body, name="grad_chip_exchange", in_specs=[ANY], out_specs=ANY,
        out_shape=jax.ShapeDtypeStruct((3, hr, c), p16.dtype),
        scratch_shapes=[pltpu.SemaphoreType.DMA((3,)), pltpu.SemaphoreType.DMA((3,))],
    )(p16)


def _chip_add(p32, recv, chip_idx, *, tm=384):
    n, hr, c = p32.shape

    def body(ci_ref, p_ref, r_ref, o_ref):
        o_ref[...] = ((p_ref[0] + r_ref[0].astype(F32)) + r_ref[1].astype(F32)) + r_ref[2].astype(F32)

    gs = pltpu.PrefetchScalarGridSpec(
        num_scalar_prefetch=1, grid=(hr // tm,),
        in_specs=[pl.BlockSpec((1, tm, c), lambda j, ci: (ci[0], j, 0)), pl.BlockSpec((3, tm, c), lambda j, ci: (0, j, 0))],
        out_specs=pl.BlockSpec((tm, c), lambda j, ci: (j, 0)))
    return pl.pallas_call(
        body, name="grad_chip_add", grid_spec=gs, out_shape=jax.ShapeDtypeStruct((hr, c), F32),
        compiler_params=_cp(("parallel",)),
    )(chip_idx, p32, recv)


def _pair_gather(f):
    hr, c = f.shape

    def body(f_ref, out_ref, send_sem, recv_sem, local_sem):
        x, y, cc = lax.axis_index("x"), lax.axis_index("y"), lax.axis_index("c")
        mine = pltpu.make_async_copy(f_ref, out_ref.at[pl.ds(cc * hr, hr), :], local_sem)
        mine.start()
        cp = pltpu.make_async_remote_copy(
            src_ref=f_ref, dst_ref=out_ref.at[pl.ds(cc * hr, hr), :], send_sem=send_sem, recv_sem=recv_sem,
            device_id=(x, y, 1 - cc), device_id_type=MESH)
        cp.start()
        cp.wait()
        mine.wait()

    return pl.pallas_call(
        body, name="grad_pair_gather", in_specs=[ANY], out_specs=ANY,
        out_shape=jax.ShapeDtypeStruct((2 * hr, c), f.dtype),
        scratch_shapes=[pltpu.SemaphoreType.DMA, pltpu.SemaphoreType.DMA, pltpu.SemaphoreType.DMA],
    )(f)


def _all_reduce_small(buf):
    r, c = buf.shape

    def body(b_ref, out_ref, gat, send_sems, recv_sems):
        x, y, cc = lax.axis_index("x"), lax.axis_index("y"), lax.axis_index("c")
        me = 4 * x + 2 * y + cc
        gat[me] = b_ref[...]
        cps = []
        for k in range(1, 8):
            px, py, pc = x ^ (k >> 2), y ^ ((k >> 1) & 1), cc ^ (k & 1)
            cps.append(pltpu.make_async_remote_copy(
                src_ref=b_ref, dst_ref=gat.at[me], send_sem=send_sems.at[k - 1], recv_sem=recv_sems.at[k - 1],
                device_id=(px, py, pc), device_id_type=MESH))
        for cp in cps:
            cp.start()
        for cp in cps:
            cp.wait()
        acc = gat[0]
        for d in range(1, 8):
            acc = acc + gat[d]
        out_ref[...] = acc

    vm = pl.BlockSpec(memory_space=pltpu.VMEM)
    return pl.pallas_call(
        body, name="all_reduce_small", in_specs=[vm], out_specs=vm, out_shape=jax.ShapeDtypeStruct((r, c), F32),
        scratch_shapes=[pltpu.VMEM((8, r, c), F32), pltpu.SemaphoreType.DMA((7,)), pltpu.SemaphoreType.DMA((7,))],
        compiler_params=pltpu.CompilerParams(vmem_limit_bytes=VMEM_LIMIT),
    )(buf)


def _pipe(fn, ins, outs, tr):
    shape = ins[0].shape
    lead, (r, c) = shape[:-2], shape[-2:]
    assert len(lead) <= 1 and r % tr == 0
    nr = r // tr
    n = nr * (lead[0] if lead else 1)
    ni, no = len(ins), len(outs)

    def blk(ref, step):
        rows = pl.ds((step % nr) * tr, tr)
        return ref.at[step // nr, rows, :] if lead else ref.at[rows, :]

    def scoped(*bufs):
        ibufs, obufs, isem, osem = bufs[:ni], bufs[ni:ni + no], bufs[-2], bufs[-1]

        def in_copy(q, step, slot):
            return pltpu.make_async_copy(blk(ins[q], step), ibufs[q].at[slot], isem.at[q, slot])

        def out_copy(q, step, slot):
            return pltpu.make_async_copy(obufs[q].at[slot], blk(outs[q], step), osem.at[q, slot])

        for step in range(min(nbuf - 1, n)):
            for q in range(ni):
                in_copy(q, step, step % nbuf).start()
        for step in range(n):
            slot = step % nbuf
            if step + nbuf - 1 < n:
                for q in range(ni):
                    in_copy(q, step + nbuf - 1, (step + nbuf - 1) % nbuf).start()
            for q in range(ni):
                in_copy(q, step, slot).wait()
            if step >= nbuf:
                for q in range(no):
                    out_copy(q, step - nbuf, slot).wait()
            res = fn(*[ibufs[q][slot] for q in range(ni)])
            for q in range(no):
                obufs[q][slot] = res[q].astype(obufs[q].dtype)
                out_copy(q, step, slot).start()
        for step in range(max(n - nbuf, 0), n):
            for q in range(no):
                out_copy(q, step, step % nbuf).wait()

    assert n <= 8
    nbuf = min(n, 4)
    pl.run_scoped(scoped, *[pltpu.VMEM((nbuf, tr, c), q.dtype) for q in ins], *[pltpu.VMEM((nbuf, tr, c), q.dtype) for q in outs],
                  pltpu.SemaphoreType.DMA((ni, nbuf)), pltpu.SemaphoreType.DMA((no, nbuf)))


W_IN_PAD = 2304
BIG = ("w_in", "w_attn_o", "w_ssd_o", "w_out", "w_up", "w_down")
BIG_SHAPE = dict(w_in=(D_MODEL, W_IN_PAD), w_attn_o=(Q_DIM // 4, D_MODEL), w_ssd_o=(D_INNER // 4, D_MODEL),
                 w_out=(D_MODEL // 4, D_MODEL), w_up=(D_MODEL, 2 * D_FF // 4), w_down=(D_FF // 4, D_MODEL))
BIG_TR = dict(w_in=128, w_attn_o=128, w_ssd_o=128, w_out=128, w_up=128, w_down=176)
X_FIRST = dict(w_in=True, w_attn_o=True, w_ssd_o=False, w_out=True, w_up=False, w_down=False)


def _neighbours(x, y, x_first):
    xn, yn = (1 - x, y), (x, 1 - y)
    n1, n2 = (xn, yn) if x_first else (yn, xn)
    slot = lambda ch: 2 * ch[0] + ch[1]
    return n1, n2, slot(n1), slot(n2), slot((1 - x, 1 - y))


def _gather_big(shards):
    nt = len(BIG)

    def body(*refs):
        sh, out = refs[:nt], refs[nt:2 * nt]
        send_sems, recv_sems = refs[2 * nt:]
        x, y, cc = lax.axis_index("x"), lax.axis_index("y"), lax.axis_index("c")
        me = 2 * x + y
        sib = (x, y, 1 - cc)
        for t, n in enumerate(BIG):
            _pipe(lambda v: (v,), [sh[t]], [out[t].at[me]], BIG_TR[n])

        def copy(t, k, slot, pc, to):
            hr = BIG_SHAPE[BIG[t]][0] // 2
            ref = out[t].at[slot, pl.ds(pc * hr, hr), :]
            return pltpu.make_async_remote_copy(src_ref=ref, dst_ref=ref, send_sem=send_sems.at[6 * t + k],
                                                recv_sem=recv_sems.at[6 * t + k], device_id=to, device_id_type=MESH)

        started = []

        def start(cp):
            cp.start()
            started.append(cp)

        geo = [_neighbours(x, y, X_FIRST[n]) for n in BIG]
        for t in range(nt):
            n1, n2, _, _, _ = geo[t]
            start(copy(t, 0, me, cc, (*n1, cc)))
            start(copy(t, 1, me, cc, (*n2, cc)))
        for t in range(nt):
            n1, n2, s1, s2, sd = geo[t]
            copy(t, 0, s1, cc, sib).wait_recv()
            start(copy(t, 2, s1, cc, (*n2, cc)))
            start(copy(t, 3, s1, cc, sib))
            copy(t, 1, s2, cc, sib).wait_recv()
            start(copy(t, 4, s2, cc, sib))
        for t in range(nt):
            _, _, s1, s2, sd = geo[t]
            copy(t, 2, sd, cc, sib).wait_recv()
            start(copy(t, 5, sd, cc, sib))
        for t in range(nt):
            _, _, s1, s2, sd = geo[t]
            copy(t, 3, s1, 1 - cc, sib).wait_recv()
            copy(t, 4, s2, 1 - cc, sib).wait_recv()
            copy(t, 5, sd, 1 - cc, sib).wait_recv()
        for cp in started:
            cp.wait_send()

    return pl.pallas_call(
        body, name="gather_big", in_specs=[ANY] * nt, out_specs=[ANY] * nt,
        out_shape=[jax.ShapeDtypeStruct((N_CHIPS, *BIG_SHAPE[n]), BF16) for n in BIG],
        scratch_shapes=[pltpu.SemaphoreType.DMA((6 * nt,)), pltpu.SemaphoreType.DMA((6 * nt,))],
        compiler_params=pltpu.CompilerParams(vmem_limit_bytes=VMEM_LIMIT),
    )(*shards)


def _reduce_big(grads):
    nt = len(BIG)
    nw = 7

    def body(*refs):
        g = refs[:nt]
        fin = refs[nt:2 * nt]
        work = refs[2 * nt:2 * nt + nw * nt]
        send_sems, recv_sems = refs[2 * nt + nw * nt:]
        x, y, cc = lax.axis_index("x"), lax.axis_index("y"), lax.axis_index("c")
        me = 2 * x + y
        sib = (x, y, 1 - cc)
        started = []

        def rcopy(t, k, src, dst, to):
            cp = pltpu.make_async_remote_copy(src_ref=src, dst_ref=dst, send_sem=send_sems.at[5 * t + k],
                                              recv_sem=recv_sems.at[5 * t + k], device_id=to, device_id_type=MESH)
            return cp

        def start(cp):
            cp.start()
            started.append(cp)

        geo = [_neighbours(x, y, X_FIRST[n]) for n in BIG]
        hrs = [BIG_SHAPE[n][0] // 2 for n in BIG]
        wk = lambda t: work[nw * t:nw * (t + 1)]
        one = lambda ref, slot: ref.at[pl.ds(slot, 1)]
        for t in range(nt):
            recv_a = wk(t)[0]
            start(rcopy(t, 0, g[t].at[:, pl.ds((1 - cc) * hrs[t], hrs[t]), :], recv_a, sib))
        for t, n in enumerate(BIG):
            recv_a, p32, p16, r1, qme, qs2, r2 = wk(t)
            n1, n2, s1, s2, sd = geo[t]
            rcopy(t, 0, recv_a, recv_a, sib).wait_recv()
            _pipe(lambda a, b: (a + b, a + b), [g[t].at[:, pl.ds(cc * hrs[t], hrs[t]), :], recv_a], [p32, p16], BIG_TR[n])
            start(rcopy(t, 1, one(p16, s1), one(r1, 0), (*n1, cc)))
            start(rcopy(t, 2, one(p16, sd), one(r1, 1), (*n1, cc)))
        for t, n in enumerate(BIG):
            recv_a, p32, p16, r1, qme, qs2, r2 = wk(t)
            n1, n2, s1, s2, sd = geo[t]
            rcopy(t, 1, one(r1, 0), one(r1, 0), sib).wait_recv()
            rcopy(t, 2, one(r1, 1), one(r1, 1), sib).wait_recv()
            _pipe(lambda a, b: (a + b.astype(F32),), [one(p32, s2), one(r1, 1)], [qs2], BIG_TR[n])
            start(rcopy(t, 3, qs2, r2, (*n2, cc)))
            _pipe(lambda a, b: (a + b.astype(F32),), [one(p32, me), one(r1, 0)], [qme], BIG_TR[n])
        for t, n in enumerate(BIG):
            recv_a, p32, p16, r1, qme, qs2, r2 = wk(t)
            rcopy(t, 3, r2, r2, sib).wait_recv()
            mine = fin[t].at[pl.ds(cc * hrs[t], hrs[t]), :]
            _pipe(lambda a, b: (a + b.astype(F32),), [qme.at[0], r2.at[0]], [mine], BIG_TR[n])
            start(rcopy(t, 4, mine, mine, sib))
        for t in range(nt):
            other = fin[t].at[pl.ds((1 - cc) * hrs[t], hrs[t]), :]
            rcopy(t, 4, other, other, sib).wait_recv()
        for cp in started:
            cp.wait_send()

    outs = [jax.ShapeDtypeStruct(BIG_SHAPE[n], F32) for n in BIG]
    for n in BIG:
        r, c = BIG_SHAPE[n]
        hr = r // 2
        outs += [jax.ShapeDtypeStruct((4, hr, c), F32), jax.ShapeDtypeStruct((4, hr, c), F32),
                 jax.ShapeDtypeStruct((4, hr, c), BF16), jax.ShapeDtypeStruct((2, hr, c), BF16),
                 jax.ShapeDtypeStruct((1, hr, c), F32), jax.ShapeDtypeStruct((1, hr, c), BF16),
                 jax.ShapeDtypeStruct((1, hr, c), BF16)]
    res = pl.pallas_call(
        body, name="reduce_big", in_specs=[ANY] * nt, out_specs=[ANY] * len(outs), out_shape=outs,
        scratch_shapes=[pltpu.SemaphoreType.DMA((5 * nt,)), pltpu.SemaphoreType.DMA((5 * nt,))],
        compiler_params=pltpu.CompilerParams(vmem_limit_bytes=VMEM_LIMIT),
    )(*grads)
    return res[:nt]


WHOLE_X_FIRST = dict(w_ssd_o=True, w_out=False, w_attn_o=False)


def _quarters(names):
    out = []
    for i, n in enumerate(names):
        if n in WHOLE_X_FIRST:
            h = BIG_SHAPE[n][0] // 2
            out.append((i, WHOLE_X_FIRST[n], 0, h, 128))
        else:
            q = BIG_SHAPE[n][0] // 4
            tr = 128 if q % 128 == 0 else q
            out += [(i, True, 0, q, tr), (i, False, q, q, tr)]
    return out


class _GatherJob:
    def __init__(self, names, shards, at=None):
        self.names = names
        self.at = at
        self.inputs = list(shards)
        self.out_shapes = [jax.ShapeDtypeStruct((N_CHIPS, *BIG_SHAPE[n]), BF16) for n in names]
        self.ent = _quarters(names)
        self.scratch = [pltpu.SemaphoreType.DMA((6 * len(self.ent),)), pltpu.SemaphoreType.DMA((6 * len(self.ent),))]

    def phases(self, sh, out, scr):
        send_sems, recv_sems = scr
        names, ent = self.names, self.ent
        x, y, cc = lax.axis_index("x"), lax.axis_index("y"), lax.axis_index("c")
        me = 2 * x + y
        sib = (x, y, 1 - cc)
        geo = [_neighbours(x, y, e[1]) for e in ent]
        started = []

        def copy(i, k, slot, pc, to):
            arr, _, roff, rows, _ = ent[i]
            hr = BIG_SHAPE[names[arr]][0] // 2
            ref = out[arr].at[slot, pl.ds(pc * hr + roff, rows), :]
            return pltpu.make_async_remote_copy(src_ref=ref, dst_ref=ref, send_sem=send_sems.at[6 * i + k],
                                                recv_sem=recv_sems.at[6 * i + k], device_id=to, device_id_type=MESH)

        def start(*a):
            copy(*a).start()
            started.append(a)

        def p0():
            for t, n in enumerate(names):
                _pipe(lambda v: (v,), [sh[t]], [out[t].at[me]], BIG_TR[n])
            for i in range(len(ent)):
                n1, n2, _, _, _ = geo[i]
                start(i, 0, me, cc, (*n1, cc))
                start(i, 1, me, cc, (*n2, cc))

        def p1():
            for i in range(len(ent)):
                n1, n2, s1, s2, sd = geo[i]
                copy(i, 0, s1, cc, sib).wait_recv()
                start(i, 2, s1, cc, (*n2, cc))
                start(i, 3, s1, cc, sib)
                copy(i, 1, s2, cc, sib).wait_recv()
                start(i, 4, s2, cc, sib)

        def p2():
            for i in range(len(ent)):
                sd = geo[i][4]
                copy(i, 2, sd, cc, sib).wait_recv()
                start(i, 5, sd, cc, sib)

        def p3():
            for i in range(len(ent)):
                _, _, s1, s2, sd = geo[i]
                copy(i, 3, s1, 1 - cc, sib).wait_recv()
                copy(i, 4, s2, 1 - cc, sib).wait_recv()
                copy(i, 5, sd, 1 - cc, sib).wait_recv()
            for a in started:
                copy(*a).wait_send()

        return [p0, p1, p2, p3]


class _ReduceJob:
    NW = 7

    def __init__(self, names, grads, at=None):
        self.names = names
        self.at = at
        self.inputs = list(grads)
        self.ent = _quarters(names)
        self.out_shapes = [jax.ShapeDtypeStruct(BIG_SHAPE[n], F32) for n in names]
        for arr, _, _, rows, _ in self.ent:
            c = BIG_SHAPE[names[arr]][1]
            self.out_shapes += [jax.ShapeDtypeStruct((4, rows, c), F32), jax.ShapeDtypeStruct((4, rows, c), F32),
                                jax.ShapeDtypeStruct((4, rows, c), BF16), jax.ShapeDtypeStruct((2, rows, c), BF16),
                                jax.ShapeDtypeStruct((1, rows, c), F32), jax.ShapeDtypeStruct((1, rows, c), BF16),
                                jax.ShapeDtypeStruct((1, rows, c), BF16)]
        self.scratch = [pltpu.SemaphoreType.DMA((5 * len(self.ent),)), pltpu.SemaphoreType.DMA((5 * len(self.ent),))]

    def phases(self, g, outs, scr):
        send_sems, recv_sems = scr
        names, ent, nw = self.names, self.ent, self.NW
        nt = len(names)
        fin, work = outs[:nt], outs[nt:]
        x, y, cc = lax.axis_index("x"), lax.axis_index("y"), lax.axis_index("c")
        me = 2 * x + y
        sib = (x, y, 1 - cc)
        geo = [_neighbours(x, y, e[1]) for e in ent]
        started = []
        wk = lambda i: work[nw * i:nw * (i + 1)]
        one = lambda ref, slot: ref.at[pl.ds(slot, 1)]

        def rows_of(i, pc):
            arr, _, roff, rows, _ = ent[i]
            return pl.ds(pc * (BIG_SHAPE[names[arr]][0] // 2) + roff, rows)

        def rcopy(i, k, src, dst, to):
            return pltpu.make_async_remote_copy(src_ref=src, dst_ref=dst, send_sem=send_sems.at[5 * i + k],
                                                recv_sem=recv_sems.at[5 * i + k], device_id=to, device_id_type=MESH)

        def start(make):
            make().start()
            started.append(make)

        def p0():
            for i, e in enumerate(ent):
                start(lambda i=i, e=e: rcopy(i, 0, g[e[0]].at[:, rows_of(i, 1 - cc), :], wk(i)[0], sib))

        def p1():
            for i, e in enumerate(ent):
                recv_a, p32, p16, r1 = wk(i)[:4]
                n1, n2, s1, s2, sd = geo[i]
                rcopy(i, 0, recv_a, recv_a, sib).wait_recv()
                _pipe(lambda a, b: (a + b, a + b), [g[e[0]].at[:, rows_of(i, cc), :], recv_a], [p32, p16], e[4])
                start(lambda i=i, s1=s1, n1=n1: rcopy(i, 1, one(wk(i)[2], s1), one(wk(i)[3], 0), (*n1, cc)))
                start(lambda i=i, sd=sd, n1=n1: rcopy(i, 2, one(wk(i)[2], sd), one(wk(i)[3], 1), (*n1, cc)))

        def p2():
            for i, e in enumerate(ent):
                _, p32, _, r1, qme, qs2, r2 = wk(i)
                n1, n2, s1, s2, sd = geo[i]
                rcopy(i, 1, one(r1, 0), one(r1, 0), sib).wait_recv()
                rcopy(i, 2, one(r1, 1), one(r1, 1), sib).wait_recv()
                _pipe(lambda a, b, c, d: (a + b.astype(F32), c + d.astype(F32)),
                      [one(p32, s2), one(r1, 1), one(p32, me), one(r1, 0)], [qs2, qme], e[4])
                start(lambda i=i, n2=n2: rcopy(i, 3, wk(i)[5], wk(i)[6], (*n2, cc)))

        def p3():
            for i, e in enumerate(ent):
                qme, r2 = wk(i)[4], wk(i)[6]
                rcopy(i, 3, r2, r2, sib).wait_recv()
                mine = fin[e[0]].at[rows_of(i, cc), :]
                _pipe(lambda a, b: (a + b.astype(F32),), [qme.at[0], r2.at[0]], [mine], e[4])
                start(lambda i=i, e=e: rcopy(i, 4, fin[e[0]].at[rows_of(i, cc), :], fin[e[0]].at[rows_of(i, cc), :], sib))

        def p4():
            for i, e in enumerate(ent):
                other = fin[e[0]].at[rows_of(i, 1 - cc), :]
                rcopy(i, 4, other, other, sib).wait_recv()
            for make in started:
                make().wait_send()

        return [p0, p1, p2, p3, p4]


def _run_job(job, name):
    ni, no = len(job.inputs), len(job.out_shapes)

    def body(*refs):
        for ph in job.phases(refs[:ni], refs[ni:ni + no], refs[ni + no:]):
            ph()

    return pl.pallas_call(
        body, name=name, in_specs=[ANY] * ni, out_specs=[ANY] * no, out_shape=job.out_shapes, scratch_shapes=job.scratch,
        compiler_params=pltpu.CompilerParams(vmem_limit_bytes=VMEM_LIMIT),
    )(*job.inputs)


def _hosted(body, *, name, grid, in_specs, out_specs, out_shape, scratch_shapes, args, sem, side=None):
    if side is None:
        return pl.pallas_call(body, name=name, grid=grid, in_specs=in_specs, out_specs=out_specs, out_shape=out_shape,
                              scratch_shapes=scratch_shapes, compiler_params=_cp(sem))(*args), None
    job = side
    ni, no, ns = len(in_specs), len(out_specs), len(scratch_shapes)
    ji, jo = len(job.inputs), len(job.out_shapes)
    n_steps = 1
    for extent in grid:
        n_steps *= extent

    def wrapped(*refs):
        own_in, refs = refs[:ni], refs[ni:]
        job_in, refs = refs[:ji], refs[ji:]
        own_out, refs = refs[:no], refs[no:]
        job_out, refs = refs[:jo], refs[jo:]
        own_scr, job_scr = refs[:ns], refs[ns:]
        step = 0
        for d, extent in enumerate(grid):
            step = step * extent + pl.program_id(d)
        phases = job.phases(job_in, job_out, job_scr)
        steps = [min(int(f * n_steps), n_steps - 1) for f in job.at] + [n_steps - 1]
        assert len(steps) == len(phases) and steps == sorted(steps)
        for at, ph in zip(steps, phases):
            pl.when(step == at)(ph)
        body(*own_in, *own_out, *own_scr)

    res = pl.pallas_call(
        wrapped, name=name, grid=grid, in_specs=list(in_specs) + [ANY] * ji, out_specs=list(out_specs) + [ANY] * jo,
        out_shape=list(out_shape) + list(job.out_shapes), scratch_shapes=list(scratch_shapes) + list(job.scratch),
        compiler_params=_cp(("arbitrary",) * len(grid)),
    )(*args, *job.inputs)
    return res[:no], res[no:]


def _proj_dw(xn, dproj_sh, *, tm=512, tk=1024):
    s, d = xn.shape
    tk = _tile(s, tk)
    nk = s // tk

    def body(a_ref, b_ref, o_ref, acc):
        kk = pl.program_id(2)
        part = _dot_tn(a_ref[...], b_ref[...])

        @pl.when(kk == 0)
        def _():
            acc[...] = part

        @pl.when(kk > 0)
        def _():
            acc[...] += part

        @pl.when(kk == nk - 1)
        def _():
            o_ref[0] = acc[...]

    return pl.pallas_call(
        body, name="proj_dw", grid=(N_CHIPS, d // tm, nk),
        in_specs=[pl.BlockSpec((tk, tm), lambda j, i, q: (q, i)), pl.BlockSpec((tk, W_IN_PAD), lambda j, i, q: (q, j))],
        out_specs=pl.BlockSpec((1, tm, W_IN_PAD), lambda j, i, q: (j, i, 0)),
        out_shape=jax.ShapeDtypeStruct((N_CHIPS, d, W_IN_PAD), F32), scratch_shapes=[pltpu.VMEM((tm, W_IN_PAD), F32)],
        compiler_params=_cp(("parallel", "parallel", "arbitrary")),
    )(xn, dproj_sh)


def _proj_dx(dproj_sh, w_sh, *, tm=1024, side=None):
    s = dproj_sh.shape[0]
    d = w_sh.shape[1]
    tm = _tile(s, tm)

    def body(a_ref, b_ref, o_ref, acc):
        kk = pl.program_id(1)
        part = _dot_nt(a_ref[...], b_ref[0])

        @pl.when(kk == 0)
        def _():
            acc[...] = part

        @pl.when(kk > 0)
        def _():
            acc[...] += part

        @pl.when(kk == N_CHIPS - 1)
        def _():
            o_ref[...] = acc[...]

    own, extra = _hosted(
        body, name="proj_dx", grid=(s // tm, N_CHIPS),
        in_specs=[pl.BlockSpec((tm, W_IN_PAD), lambda i, q: (i, q)), pl.BlockSpec((1, d, W_IN_PAD), lambda i, q: (q, 0, 0))],
        out_specs=[pl.BlockSpec((tm, d), lambda i, q: (i, 0))],
        out_shape=[jax.ShapeDtypeStruct((s, d), F32)], scratch_shapes=[pltpu.VMEM((tm, d), F32)],
        args=(dproj_sh, w_sh), sem=("parallel", "arbitrary"), side=side)
    return own[0] if side is None else (own[0], extra)


def _up_dx(dup, w_sh, *, tm=1024):
    s = dup.shape[1]
    d, wsh = w_sh.shape[1:]
    tm = _tile(s, tm)

    def body(a_ref, b_ref, o_ref, acc):
        kk = pl.program_id(1)
        part = _dot_nt(a_ref[0], b_ref[0])

        @pl.when(kk == 0)
        def _():
            acc[...] = part

        @pl.when(kk > 0)
        def _():
            acc[...] += part

        @pl.when(kk == N_CHIPS - 1)
        def _():
            o_ref[...] = acc[...]

    return pl.pallas_call(
        body, name="up_dx", grid=(s // tm, N_CHIPS),
        in_specs=[pl.BlockSpec((1, tm, wsh), lambda i, q: (q >> 1, i, q & 1)), pl.BlockSpec((1, d, wsh), lambda i, q: (q, 0, 0))],
        out_specs=pl.BlockSpec((tm, d), lambda i, q: (i, 0)),
        out_shape=jax.ShapeDtypeStruct((s, d), F32), scratch_shapes=[pltpu.VMEM((tm, d), F32)],
        compiler_params=_cp(("parallel", "arbitrary")),
    )(dup, w_sh)


def _up_dw(hn, dup, *, tk=1024):
    s, d = hn.shape
    wsh = 2 * D_FF // N_CHIPS
    tk = _tile(s, tk)
    nk = s // tk

    def body(a_ref, b_ref, o_ref, acc):
        kk = pl.program_id(1)
        part = _dot_tn(a_ref[...], b_ref[0])

        @pl.when(kk == 0)
        def _():
            acc[...] = part

        @pl.when(kk > 0)
        def _():
            acc[...] += part

        @pl.when(kk == nk - 1)
        def _():
            o_ref[0] = acc[...]

    return pl.pallas_call(
        body, name="up_dw", grid=(N_CHIPS, nk),
        in_specs=[pl.BlockSpec((tk, d), lambda j, q: (q, 0)), pl.BlockSpec((1, tk, wsh), lambda j, q: (j >> 1, q, j & 1))],
        out_specs=pl.BlockSpec((1, d, wsh), lambda j, q: (j, 0, 0)),
        out_shape=jax.ShapeDtypeStruct((N_CHIPS, d, wsh), F32), scratch_shapes=[pltpu.VMEM((d, wsh), F32)],
        compiler_params=_cp(("parallel", "arbitrary")),
    )(hn, dup)


BIG_ROWS =(IN_DIM // 4, Q_DIM // 4, D_INNER // 4, D_MODEL // 4, 2 * D_FF // 4, D_FF // 4)
PACK_ROWS = 5376


def _pack_shards(parts):
    rows = [p.reshape(-1, D_MODEL) for p in parts]
    pad = PACK_ROWS - sum(BIG_ROWS)
    return jnp.concatenate(rows + [jnp.zeros((pad, D_MODEL), rows[0].dtype)], axis=0)


def _unpack_shards(buf):
    out, off = [], 0
    for n in BIG_ROWS:
        out.append(buf[off:off + n])
        off += n
    return out


def _assemble(srcs, col_map, *, name, tr=256):
    arrays, lead = [], []
    for src in srcs:
        arr, j = src if isinstance(src, tuple) else (src, None)
        if not any(arr is a for a in arrays):
            arrays.append(arr)
        lead.append(([i for i, a in enumerate(arrays) if a is arr][0], j))
    rows = arrays[0].shape[-2]
    tr = _tile(rows, tr)
    out_w = len(col_map)
    tiles = []
    for t in range(out_w // 128):
        runs = []
        for lane in range(128):
            ent = col_map[t * 128 + lane]
            key = None if ent is None else (ent[0], ent[1] // 128, (lane - ent[1]) % 128)
            if runs and runs[-1][0] == key:
                runs[-1][2] = lane + 1
            else:
                runs.append([key, lane, lane + 1])
        tiles.append(runs)

    def body(*refs):
        o_ref = refs[-1]
        lane = lax.broadcasted_iota(jnp.int32, (tr, 128), 1)
        for t, runs in enumerate(tiles):
            acc = jnp.zeros((tr, 128), F32)
            for key, a, b in runs:
                if key is None:
                    continue
                sid, ct, shift = key
                ai, j = lead[sid]
                cols = slice(ct * 128, (ct + 1) * 128)
                piece = (refs[ai][:, cols] if j is None else refs[ai][j, :, cols]).astype(F32)
                if shift:
                    piece = pltpu.roll(piece, shift, 1)
                acc = piece if (a, b) == (0, 128) else jnp.where((lane >= a) & (lane < b), piece, acc)
            o_ref[:, t * 128:(t + 1) * 128] = acc.astype(BF16)

    specs = [pl.BlockSpec((tr, a.shape[1]), lambda i: (i, 0)) if a.ndim == 2
             else pl.BlockSpec((a.shape[0], tr, a.shape[2]), lambda i: (0, i, 0)) for a in arrays]
    return pl.pallas_call(
        body, name=name, grid=(rows // tr,), in_specs=specs, out_specs=pl.BlockSpec((tr, out_w), lambda i: (i, 0)),
        out_shape=jax.ShapeDtypeStruct((rows, out_w), BF16), compiler_params=_cp(("parallel",)),
    )(*arrays)


def _permute_cols_in(w):
    pad = jnp.zeros((w.shape[0], PW - IN_DIM), w.dtype)
    return jnp.concatenate([w[:, :6656], w[:, 6688:], w[:, 6656:6688], pad], axis=1)


def _unpermute_cols_in(g):
    return jnp.concatenate([g[:, :6656], g[:, O_DT:O_DT + 32], g[:, 6656:O_DT]], axis=1)


SMALL = ("norm1_w", "b_gate", "attn_sinks", "ssd_conv_b", "dt_bias", "a_log", "d_skip", "ssd_norm_w", "norm2_w",
         "ffn_conv_b", "final_norm_w", "ssd_conv_w", "ffn_conv_w")


def _pad128(v):
    v = v.reshape(-1)
    return jnp.pad(v, (0, (-v.shape[0]) % 128))


def _pack_small(parts):
    flat = jnp.concatenate([_pad128(p) for p in parts])
    flat = jnp.pad(flat, (0, (-flat.shape[0]) % 1024))
    return flat.reshape(-1, 128)


def _unpack_small(buf, shapes):
    flat, out, off = buf.reshape(-1), [], 0
    for shp in shapes:
        n = 1
        for q in shp:
            n *= q
        out.append(flat[off:off + n].reshape(shp))
        off += n + (-n) % 128
    return out


def _vec128(v):
    return jnp.pad(v.reshape(1, -1), ((0, 0), (0, 128 - v.shape[-1])))


def kernel(x, norm1_w, w_in, b_gate, attn_sinks, w_attn_o, ssd_conv_w, ssd_conv_b, dt_bias, a_log, d_skip, ssd_norm_w, w_ssd_o, w_out, norm2_w, w_up, ffn_conv_w, ffn_conv_b, w_down, final_norm_w, loss_target, m_norm1_w, m_w_in, m_b_gate, m_attn_sinks, m_w_attn_o, m_ssd_conv_w, m_ssd_conv_b, m_dt_bias, m_a_log, m_d_skip, m_ssd_norm_w, m_w_ssd_o, m_w_out, m_norm2_w, m_w_up, m_ffn_conv_w, m_ffn_conv_b, m_w_down, m_final_norm_w, v_norm1_w, v_w_in, v_b_gate, v_attn_sinks, v_w_attn_o, v_ssd_conv_w, v_ssd_conv_b, v_dt_bias, v_a_log, v_d_skip, v_ssd_norm_w, v_w_ssd_o, v_w_out, v_norm2_w, v_w_up, v_ffn_conv_w, v_ffn_conv_b, v_w_down, v_final_norm_w):
    ix, iy, ic = lax.axis_index("x"), lax.axis_index("y"), lax.axis_index("c")
    chip = 2 * ix + iy
    x2 = x[0]
    tgt = loss_target[0]
    s = x2.shape[0]

    wsh = IN_DIM // N_CHIPS
    big_shards = dict(w_in=jnp.pad(w_in[0], ((0, 0), (0, W_IN_PAD - wsh))), w_attn_o=w_attn_o[0], w_ssd_o=w_ssd_o[0],
                      w_out=w_out[0], w_up=w_up[0], w_down=w_down[0])
    gathered = {}
    (gathered["w_in"],) = _run_job(_GatherJob(("w_in",), [big_shards["w_in"]]), "gather_w_in")
    early = ("w_attn_o", "w_ssd_o", "w_out")
    gather_early = _GatherJob(early, [big_shards[n] for n in early], at=(0.0, 0.5, 0.8))
    gather_up = _GatherJob(("w_up",), [big_shards["w_up"]], at=(0.0, 0.55, 0.85))
    gather_down = _GatherJob(("w_down",), [big_shards["w_down"]], at=(0.0, 0.5, 0.8))
    gw = gathered["w_in"]
    perm = list(range(O_GA)) + list(range(O_GA + N_SSD_HEADS, IN_DIM)) + list(range(O_GA, O_GA + N_SSD_HEADS))
    w_in_p = _assemble([(gw, j) for j in range(N_CHIPS)], [divmod(o, wsh) for o in perm] + [None] * (PW - IN_DIM),
                       name="w_in_assemble")
    small_sh = _pack_small([ssd_conv_w[0], ffn_conv_w[0]])
    small_all = _all_gather_small(small_sh)
    sc_parts = [_unpack_small(small_all[j], [(4, XBC_DIM // 4), (3, 2 * D_FF // 4)]) for j in range(N_CHIPS)]
    ssd_cw = jnp.concatenate([p[0] for p in sc_parts], axis=1)
    ffn_cw = jnp.concatenate([p[1] for p in sc_parts], axis=1)

    sinks128 = _vec128(attn_sinks)
    dtb128, alog128, dskip128 = _vec128(dt_bias), _vec128(a_log), _vec128(d_skip)

    xn, xnt = _rms_fwd(x2, norm1_w, name="norm1_fwd", with_t=True)
    proj, got = _mm(xn, w_in_p, name="proj_fwd", tn=1280, side=gather_early)
    gathered.update(zip(early, got))
    qkvt = _mm(w_in_p[:, :O_Z], xnt, name="qkv_fwd", ta=True)
    attn_pre, (gathered["w_up"],) = _attn_fwd(qkvt, sinks128, side=gather_up)
    xbc = _ssd_conv_fwd(proj, ssd_cw, ssd_conv_b)
    (y_ssd, hprev), (gathered["w_down"],) = _ssd_fwd(xbc, proj, dtb128, alog128, dskip128, side=gather_down)
    full = {n: gathered[n].reshape(-1, D_MODEL) for n in ("w_attn_o", "w_ssd_o", "w_out", "w_down")}
    full["w_up"] = gathered["w_up"]
    attn = _mm(attn_pre, full["w_attn_o"], name="attn_o_fwd", ta=True)
    yn = _gate_norm_fwd(y_ssd, proj, ssd_norm_w)
    ssd_out = _mm(yn, full["w_ssd_o"], name="ssd_o_fwd")
    merged = _merge_fwd(proj, b_gate, attn, ssd_out)
    h1 = _mm(merged, full["w_out"], name="out_fwd", resid=x2)
    hn = _rms_fwd(h1, norm2_w, name="norm2_fwd")
    up = _mm(hn, full["w_up"], name="up_fwd")
    act = _ffn_act_fwd(up, ffn_cw, ffn_conv_b)
    h2 = _mm(act, full["w_down"], name="down_fwd", resid=h1, tk=1408)

    dh2, loss_blk, g_final = _loss_bwd(h2, tgt, final_norm_w.reshape(1, -1))
    dact = _mm(dh2, full["w_down"], name="down_dx", tb=True, tn=1408)
    g_down = _mm(act, dh2, name="down_dw", ta=True, tm=1408)
    dup, g_ffn_cw, g_ffn_cb = _ffn_act_bwd(dact, up, ffn_cw, ffn_conv_b)
    dhn = _up_dx(dup, full["w_up"])
    g_up = _up_dw(hn, dup)
    dh1, g_norm2 = _rms_bwd(dhn, h1, norm2_w, dh2, name="norm2_bwd")
    dmerged = _mm(dh1, full["w_out"], name="out_dx", tb=True)
    g_out = _mm(merged, dh1, name="out_dw", ta=True)
    dattn, dssd_out, dga, dgs, g_ba, g_bs = _merge_bwd(dmerged, proj, b_gate, attn, ssd_out)
    dyn = _mm(dssd_out, full["w_ssd_o"], name="ssd_o_dx", tb=True)
    g_ssd_o = _mm(yn, dssd_out, name="ssd_o_dw", ta=True)
    dy_ssd, dz, g_ssd_norm = _gate_norm_bwd(dyn, y_ssd, proj, ssd_norm_w)
    slot = lambda g: g.reshape(N_CHIPS, -1, D_MODEL)
    big_grads = {}
    red = ("w_down", "w_up")
    (dxbc, ddt, dvec), got = _ssd_bwd(xbc, proj, dtb128, alog128, dskip128, hprev, dy_ssd,
                                      side=_ReduceJob(red, [slot(g_down), g_up], at=(0.0, 0.3, 0.8, 0.95)))
    big_grads.update(zip(red, got))
    dxbc_raw, g_ssd_cw, g_ssd_cb = _ssd_conv_bwd(dxbc, proj, ssd_cw, ssd_conv_b)
    dattn_pre = _mm(full["w_attn_o"], dattn, name="attn_o_dx", tb=True)
    g_attn_o = _mm(attn_pre, dattn, name="attn_o_dw")
    red = ("w_out", "w_ssd_o", "w_attn_o")
    (dq, dk, dv, dsk), got = _attn_bwd(qkvt, sinks128, attn_pre, dattn_pre,
                                       side=_ReduceJob(red, [slot(g_out), slot(g_ssd_o), slot(g_attn_o)],
                                                       at=(0.0, 0.2, 0.5, 0.7)))
    big_grads.update(zip(red, got))
    pieces = [(dq.T, Q_DIM), (dk.T, KV_DIM), (dv.T, KV_DIM), (dz, D_INNER), (dxbc_raw, XBC_DIM), (ddt, N_SSD_HEADS),
              (dga, D_MODEL), (dgs, D_MODEL)]
    orig = [(i, c) for i, (_, w) in enumerate(pieces) for c in range(w)]
    dproj_sh = _assemble([p for p, _ in pieces],
                         [orig[j * wsh + c] if c < wsh else None for j in range(N_CHIPS) for c in range(W_IN_PAD)],
                         name="dproj_assemble")
    g_in = _proj_dw(xn, dproj_sh)
    dxn, got = _proj_dx(dproj_sh, gathered["w_in"], side=_ReduceJob(("w_in",), [g_in], at=(0.0, 0.3, 0.8, 0.95)))
    big_grads["w_in"] = got[0][:, :wsh]
    dx, g_norm1 = _rms_bwd(dxn, x2, norm1_w, dh1, name="norm1_bwd")


    small_g = dict(
        norm1_w=g_norm1, b_gate=jnp.concatenate([g_ba, g_bs], axis=1), attn_sinks=dsk[0:1, :16], ssd_conv_b=g_ssd_cb,
        dt_bias=dvec[0:1, :32], a_log=dvec[1:2, :32], d_skip=dvec[2:3, :32], ssd_norm_w=g_ssd_norm, norm2_w=g_norm2,
        ffn_conv_b=jnp.concatenate([g_ffn_cb[0], g_ffn_cb[1]], axis=1), final_norm_w=g_final, ssd_conv_w=g_ssd_cw,
        ffn_conv_w=jnp.concatenate([g_ffn_cw[0], g_ffn_cw[1]], axis=1))
    small_buf = _pack_small([small_g[n] for n in SMALL] + [loss_blk])
    small_sum = _all_reduce_small(small_buf)
    small_shapes = [(1, D_MODEL), (1, 2 * D_MODEL), (1, 16), (1, XBC_DIM), (1, 32), (1, 32), (1, 32), (1, D_INNER),
                    (1, D_MODEL), (1, 2 * D_FF), (D_MODEL,), (4, XBC_DIM), (3, 2 * D_FF), (1, 128)]
    small_list = _unpack_small(small_sum, small_shapes)
    loss = small_list[-1][0, 0]
    grads = dict(zip(SMALL, small_list[:-1]))
    grads["ssd_conv_w"] = lax.dynamic_slice_in_dim(grads["ssd_conv_w"], chip * (XBC_DIM // 4), XBC_DIM // 4, axis=1)
    grads["ffn_conv_w"] = lax.dynamic_slice_in_dim(grads["ffn_conv_w"], chip * (2 * D_FF // 4), 2 * D_FF // 4, axis=1)
    grads.update(big_grads)

    weights = dict(norm1_w=norm1_w, w_in=w_in, b_gate=b_gate, attn_sinks=attn_sinks, w_attn_o=w_attn_o, ssd_conv_w=ssd_conv_w,
                   ssd_conv_b=ssd_conv_b, dt_bias=dt_bias, a_log=a_log, d_skip=d_skip, ssd_norm_w=ssd_norm_w, w_ssd_o=w_ssd_o,
                   w_out=w_out, norm2_w=norm2_w, w_up=w_up, ffn_conv_w=ffn_conv_w, ffn_conv_b=ffn_conv_b, w_down=w_down,
                   final_norm_w=final_norm_w)
    ms = dict(norm1_w=m_norm1_w, w_in=m_w_in, b_gate=m_b_gate, attn_sinks=m_attn_sinks, w_attn_o=m_w_attn_o,
              ssd_conv_w=m_ssd_conv_w, ssd_conv_b=m_ssd_conv_b, dt_bias=m_dt_bias, a_log=m_a_log, d_skip=m_d_skip,
              ssd_norm_w=m_ssd_norm_w, w_ssd_o=m_w_ssd_o, w_out=m_w_out, norm2_w=m_norm2_w, w_up=m_w_up,
              ffn_conv_w=m_ffn_conv_w, ffn_conv_b=m_ffn_conv_b, w_down=m_w_down, final_norm_w=m_final_norm_w)
    vs = dict(norm1_w=v_norm1_w, w_in=v_w_in, b_gate=v_b_gate, attn_sinks=v_attn_sinks, w_attn_o=v_w_attn_o,
              ssd_conv_w=v_ssd_conv_w, ssd_conv_b=v_ssd_conv_b, dt_bias=v_dt_bias, a_log=v_a_log, d_skip=v_d_skip,
              ssd_norm_w=v_ssd_norm_w, w_ssd_o=v_w_ssd_o, w_out=v_w_out, norm2_w=v_norm2_w, w_up=v_w_up,
              ffn_conv_w=v_ffn_conv_w, ffn_conv_b=v_ffn_conv_b, w_down=v_w_down, final_norm_w=v_final_norm_w)
    order = list(weights)
    deltas, new_m, new_v = {}, {}, {}
    for n in BIG:
        shp = weights[n].shape
        d_, m_, v_ = _adamw(weights[n][0], grads[n], ms[n][0], vs[n][0], name="adamw_" + n)
        deltas[n], new_m[n], new_v[n] = d_.reshape(shp), m_.reshape(shp), v_.reshape(shp)
    smalls = [n for n in order if n not in BIG]
    as2d = lambda a: a.reshape(-1, a.shape[-1])
    res = _adamw_many(*[[as2d(src[n][0] if src[n].ndim == 3 else src[n]) for n in smalls] for src in (weights, grads, ms, vs)])
    for i, n in enumerate(smalls):
        deltas[n], new_m[n], new_v[n] = (res[q * len(smalls) + i].reshape(weights[n].shape) for q in range(3))
    out_grads = [grads[n].reshape(weights[n].shape) for n in order]
    return (loss, dx[None], *out_grads, *[deltas[n] for n in order], *[new_m[n] for n in order], *[new_v[n] for n in order])
```

```python
import functools

import jax
import jax.numpy as jnp
from jax import lax
from jax.experimental import pallas as pl
from jax.experimental.pallas import tpu as pltpu

F32 = jnp.float32
BF16 = jnp.bfloat16
HI = lax.Precision.HIGHEST

D_MODEL = 1024
Q_DIM = 1024
KV_DIM = 256
D_INNER = 2048
BC_DIM = 512
XBC_DIM = 3072
N_SSD_HEADS = 32
D_FF = 2816
IN_DIM = 8736
BLK = 128
EPS = 1e-5
NEG = -1e30

O_Q, O_K, O_V, O_Z, O_X, O_GA, O_GS, O_DT = 0, 1024, 1280, 1536, 3584, 6656, 7680, 8704
PW = 8960

ADAM_LR, ADAM_B1, ADAM_B2, ADAM_EPS, ADAM_WD, ADAM_STEP = 0.001, 0.9, 0.999, 1e-08, 0.01, 10

VMEM_LIMIT = 52 * 1024 * 1024
MESH = pl.DeviceIdType.MESH


def _cp(sem=None):
    return pltpu.CompilerParams(dimension_semantics=sem, vmem_limit_bytes=VMEM_LIMIT)


def _dot(a, b, prec=None):
    return jnp.dot(a, b, preferred_element_type=F32, precision=prec)


def _dot_nt(a, b, prec=None):
    return lax.dot_general(a, b, (((1,), (1,)), ((), ())), preferred_element_type=F32, precision=prec)


def _dot_tn(a, b, prec=None):
    return lax.dot_general(a, b, (((0,), (0,)), ((), ())), preferred_element_type=F32, precision=prec)


def _sigmoid(x):
    return 0.5 * jnp.tanh(0.5 * x) + 0.5


def _tile(n, want):
    t = min(n, want)
    while n % t:
        t -= 128
    return t


def _accumulate(acc, part, kk, nk, finish):
    if nk == 1:
        finish(part)
        return

    @pl.when(kk == 0)
    def _():
        acc[...] = part

    @pl.when(kk > 0)
    def _():
        acc[...] += part

    @pl.when(kk == nk - 1)
    def _():
        finish(acc[...])


def _mm(a, b, *, name, ta=False, tb=False, out_dtype=F32, resid=None, tm=1024, tn=1024, tk=1024, side=None):
    m, k = (a.shape[1], a.shape[0]) if ta else a.shape
    slots = b.ndim == 3
    if slots:
        n = b.shape[1] if tb else b.shape[0] * b.shape[2]
        tn, tk = (tn, b.shape[2]) if tb else (b.shape[2], tk)
    else:
        n = b.shape[0] if tb else b.shape[1]
    tm, tn, tk = _tile(m, tm), _tile(n, tn), _tile(k, tk)
    nk = k // tk
    dn = (((0 if ta else 1,), (1 if tb else 0,)), ((), ()))

    def body(*refs):
        if resid is None:
            a_ref, b_ref, o_ref, acc = refs
        else:
            a_ref, b_ref, r_ref, o_ref, acc = refs
        kk = pl.program_id(2)
        bv = b_ref[0] if slots else b_ref[...]
        part = lax.dot_general(a_ref[...].astype(BF16), bv.astype(BF16), dn, preferred_element_type=F32)

        def finish(r):
            if resid is not None:
                r = r + r_ref[...]
            o_ref[...] = r.astype(out_dtype)

        _accumulate(acc, part, kk, nk, finish)

    a_spec = pl.BlockSpec((tk, tm), lambda i, j, q: (q, i)) if ta else pl.BlockSpec((tm, tk), lambda i, j, q: (i, q))
    if slots:
        b_spec = (pl.BlockSpec((1, tn, tk), lambda i, j, q: (q, j, 0)) if tb
                  else pl.BlockSpec((1, tk, tn), lambda i, j, q: (j, q, 0)))
    else:
        b_spec = pl.BlockSpec((tn, tk), lambda i, j, q: (j, q)) if tb else pl.BlockSpec((tk, tn), lambda i, j, q: (q, j))
    o_spec = pl.BlockSpec((tm, tn), lambda i, j, q: (i, j))
    ins, specs = [a, b], [a_spec, b_spec]
    if resid is not None:
        ins.append(resid)
        specs.append(o_spec)
    own, extra = _hosted(
        body, name=name, grid=(m // tm, n // tn, nk), in_specs=specs, out_specs=[o_spec],
        out_shape=[jax.ShapeDtypeStruct((m, n), out_dtype)], scratch_shapes=[pltpu.VMEM((tm, tn), F32)],
        args=ins, sem=("parallel", "parallel", "arbitrary"), side=side)
    return own[0] if side is None else (own[0], extra)


def _rms_fwd(x, w, *, name, tm=512, with_t=False):
    s, d = x.shape
    tm = _tile(s, tm)

    def body(x_ref, w_ref, o_ref, *t_ref):
        xv = x_ref[...]
        r = lax.rsqrt(jnp.mean(xv * xv, axis=-1, keepdims=True) + EPS)
        y = (xv * r) * w_ref[...]
        o_ref[...] = y.astype(BF16)
        if with_t:
            t_ref[0][...] = y.T.astype(BF16)

    row = pl.BlockSpec((tm, d), lambda i: (i, 0))
    res = pl.pallas_call(
        body, name=name, grid=(s // tm,), in_specs=[row, pl.BlockSpec((1, d), lambda i: (0, 0))],
        out_specs=[row] + [pl.BlockSpec((d, tm), lambda i: (0, i))] * with_t,
        out_shape=[jax.ShapeDtypeStruct((s, d), BF16)] + [jax.ShapeDtypeStruct((d, s), BF16)] * with_t,
        compiler_params=_cp(("parallel",)),
    )(x, w)
    return res if with_t else res[0]


def _rms_bwd(dy, x, w, resid, *, name, tm=512):
    s, d = x.shape
    tm = _tile(s, tm)

    def body(dy_ref, x_ref, w_ref, r_ref, dx_ref, dw_ref):
        i = pl.program_id(0)
        xv = x_ref[...]
        r = lax.rsqrt(jnp.mean(xv * xv, axis=-1, keepdims=True) + EPS)
        xh = xv * r
        dyv = dy_ref[...]
        g = dyv * w_ref[...]
        dx_ref[...] = r_ref[...] + r * (g - xh * jnp.mean(g * xh, axis=-1, keepdims=True))
        part = jnp.sum(dyv * xh, axis=0, keepdims=True)

        @pl.when(i == 0)
        def _():
            dw_ref[...] = part

        @pl.when(i > 0)
        def _():
            dw_ref[...] += part

    row = pl.BlockSpec((tm, d), lambda i: (i, 0))
    vec = pl.BlockSpec((1, d), lambda i: (0, 0))
    return pl.pallas_call(
        body, name=name, grid=(s // tm,), in_specs=[row, row, vec, row], out_specs=[row, vec],
        out_shape=[jax.ShapeDtypeStruct((s, d), F32), jax.ShapeDtypeStruct((1, d), F32)],
        compiler_params=_cp(("arbitrary",)),
    )(dy, x, w, resid)


def _loss_bwd(h2, tgt, wf, *, tm=512):
    s, d = h2.shape
    tm = _tile(s, tm)

    def body(h_ref, t_ref, w_ref, dh_ref, loss_ref, dw_ref):
        i = pl.program_id(0)
        hv = h_ref[...]
        r = lax.rsqrt(jnp.mean(hv * hv, axis=-1, keepdims=True) + EPS)
        xh = hv * r
        wv = w_ref[...]
        e = xh * wv - t_ref[...]
        lpart = 0.5 * jnp.sum(jnp.mean(e * e, axis=-1, keepdims=True), axis=0, keepdims=True)
        dout = e * (1.0 / d)
        g = dout * wv
        dh_ref[...] = r * (g - xh * jnp.mean(g * xh, axis=-1, keepdims=True))
        part = jnp.sum(dout * xh, axis=0, keepdims=True)
        lrow = jnp.broadcast_to(lpart, (1, 128))

        @pl.when(i == 0)
        def _():
            dw_ref[...] = part
            loss_ref[...] = lrow

        @pl.when(i > 0)
        def _():
            dw_ref[...] += part
            loss_ref[...] += lrow

    row = pl.BlockSpec((tm, d), lambda i: (i, 0))
    vec = pl.BlockSpec((1, d), lambda i: (0, 0))
    return pl.pallas_call(
        body, name="loss_bwd", grid=(s // tm,), in_specs=[row, row, vec],
        out_specs=[row, pl.BlockSpec((1, 128), lambda i: (0, 0)), vec],
        out_shape=[jax.ShapeDtypeStruct((s, d), F32), jax.ShapeDtypeStruct((1, 128), F32),
                   jax.ShapeDtypeStruct((1, d), F32)],
        compiler_params=_cp(("arbitrary",)),
    )(h2, tgt, wf)


def _attn_mask(n):
    si = lax.broadcasted_iota(jnp.int32, (2 * BLK, 4 * BLK), 0)
    qi = lax.broadcasted_iota(jnp.int32, (2 * BLK, 4 * BLK), 1) & (BLK - 1)
    dist = BLK + qi - si
    kpos = n * BLK - BLK + si
    return (dist >= 0) & (dist < BLK) & (kpos >= 0)


def _attn_probs(q_ref, kc_ref, kp_ref, sk_ref, kvh, valid):
    rows = slice(kvh * 64, (kvh + 1) * 64)
    kt = jnp.concatenate([kp_ref[rows, :], kc_ref[rows, :]], axis=1).astype(BF16)
    qt = jnp.concatenate([q_ref[(kvh * 4 + g) * 64:(kvh * 4 + g + 1) * 64, :] for g in range(4)], axis=1).astype(BF16)
    s = _dot_tn(kt, qt) * 0.125
    s = jnp.where(valid, s, NEG)
    head = lax.broadcasted_iota(jnp.int32, (1, 4 * BLK), 1) >> 7
    sink = jnp.zeros((1, 4 * BLK), F32)
    for g in range(4):
        sink = jnp.where(head == g, sk_ref[0:1, kvh * 4 + g:kvh * 4 + g + 1], sink)
    m = jnp.maximum(jnp.max(s, axis=0, keepdims=True), sink)
    p = jnp.where(valid, jnp.exp(s - m), 0.0)
    es = jnp.exp(sink - m)
    inv = 1.0 / (jnp.sum(p, axis=0, keepdims=True) + es)
    return qt, kt, p * inv, es * inv


def _attn_in_specs(cur, prev):
    return [pl.BlockSpec((Q_DIM, BLK), lambda n: (0, cur(n))),
            pl.BlockSpec((KV_DIM, BLK), lambda n: (O_K // KV_DIM, cur(n))),
            pl.BlockSpec((KV_DIM, BLK), lambda n: (O_K // KV_DIM, prev(n))),
            pl.BlockSpec((KV_DIM, BLK), lambda n: (O_V // KV_DIM, cur(n))),
            pl.BlockSpec((KV_DIM, BLK), lambda n: (O_V // KV_DIM, prev(n))),
            pl.BlockSpec((1, 128), lambda n: (0, 0))]


def _attn_fwd(qkvt, sinks, side=None):
    s = qkvt.shape[1]
    nb = s // BLK

    def body(q_ref, kc_ref, kp_ref, vc_ref, vp_ref, sk_ref, o_ref):
        valid = _attn_mask(pl.program_id(0))
        for kvh in range(4):
            rows = slice(kvh * 64, (kvh + 1) * 64)
            _, _, probs, _ = _attn_probs(q_ref, kc_ref, kp_ref, sk_ref, kvh, valid)
            vt = jnp.concatenate([vp_ref[rows, :], vc_ref[rows, :]], axis=1).astype(BF16)
            o = _dot(vt, probs.astype(BF16))
            for g in range(4):
                h = kvh * 4 + g
                o_ref[h * 64:(h + 1) * 64, :] = o[:, g * BLK:(g + 1) * BLK].astype(BF16)

    own, extra = _hosted(
        body, name="attn_fwd", grid=(nb,), in_specs=_attn_in_specs(lambda n: n, lambda n: jnp.maximum(n - 1, 0)),
        out_specs=[pl.BlockSpec((Q_DIM, BLK), lambda n: (0, n))],
        out_shape=[jax.ShapeDtypeStruct((Q_DIM, s), BF16)], scratch_shapes=[],
        args=(qkvt, qkvt, qkvt, qkvt, qkvt, sinks), sem=("parallel",), side=side)
    return own[0] if side is None else (own[0], extra)


def _attn_bwd(qkvt, sinks, o, do, side=None):
    s = qkvt.shape[1]
    nb = s // BLK

    def body(q_ref, kc_ref, kp_ref, vc_ref, vp_ref, sk_ref, o_ref, do_ref, dq_ref, dk_ref, dv_ref, dsk_ref, ck, cv, nk, nv):
        n = pl.program_id(0)

        @pl.when(n == 0)
        def _():
            ck[...] = jnp.zeros_like(ck)
            cv[...] = jnp.zeros_like(cv)
            dsk_ref[...] = jnp.zeros_like(dsk_ref)

        @pl.when(n < nb)
        def _():
            valid = _attn_mask(n)
            lane = lax.broadcasted_iota(jnp.int32, (1, 128), 1)
            dsk = jnp.zeros((1, 128), F32)
            for kvh in range(4):
                rows = slice(kvh * 64, (kvh + 1) * 64)
                qt, kt, probs, psink = _attn_probs(q_ref, kc_ref, kp_ref, sk_ref, kvh, valid)
                vt = jnp.concatenate([vp_ref[rows, :], vc_ref[rows, :]], axis=1).astype(BF16)
                heads = [slice((kvh * 4 + g) * 64, (kvh * 4 + g + 1) * 64) for g in range(4)]
                dot = jnp.concatenate([do_ref[hh, :] for hh in heads], axis=1)
                ot = jnp.concatenate([o_ref[hh, :] for hh in heads], axis=1).astype(F32)
                delta = jnp.sum(dot * ot, axis=0, keepdims=True)
                dot16 = dot.astype(BF16)
                dp = _dot_tn(vt, dot16)
                ds = (probs * (dp - delta) * 0.125).astype(BF16)
                dqt = _dot(kt, ds)
                nk[rows, :] = _dot_nt(qt, ds)
                nv[rows, :] = _dot_nt(dot16, probs.astype(BF16))
                sd = psink * delta
                for g in range(4):
                    dq_ref[heads[g], :] = dqt[:, g * BLK:(g + 1) * BLK].astype(BF16)
                    val = -jnp.sum(sd[:, g * BLK:(g + 1) * BLK], axis=1, keepdims=True)
                    dsk = dsk + jnp.where(lane == kvh * 4 + g, val, 0.0)
            dsk_ref[0:1, :] += dsk
            dk_ref[...] = (ck[...] + nk[:, :BLK]).astype(BF16)
            dv_ref[...] = (cv[...] + nv[:, :BLK]).astype(BF16)
            ck[...] = nk[:, BLK:]
            cv[...] = nv[:, BLK:]

        @pl.when(n == nb)
        def _():
            dk_ref[...] = ck[...].astype(BF16)
            dv_ref[...] = cv[...].astype(BF16)

    cur = lambda n: jnp.minimum(n, nb - 1)
    prev = lambda n: jnp.maximum(jnp.minimum(n, nb - 1) - 1, 0)
    outb = lambda n: jnp.maximum(n - 1, 0)
    own, extra = _hosted(
        body, name="attn_bwd", grid=(nb + 1,),
        in_specs=_attn_in_specs(cur, prev) + [pl.BlockSpec((Q_DIM, BLK), lambda n: (0, cur(n))),
                                              pl.BlockSpec((Q_DIM, BLK), lambda n: (0, cur(n)))],
        out_specs=[pl.BlockSpec((Q_DIM, BLK), lambda n: (0, cur(n))),
                   pl.BlockSpec((KV_DIM, BLK), lambda n: (0, outb(n))),
                   pl.BlockSpec((KV_DIM, BLK), lambda n: (0, outb(n))),
                   pl.BlockSpec((8, 128), lambda n: (0, 0))],
        out_shape=[jax.ShapeDtypeStruct((Q_DIM, s), BF16), jax.ShapeDtypeStruct((KV_DIM, s), BF16),
                   jax.ShapeDtypeStruct((KV_DIM, s), BF16), jax.ShapeDtypeStruct((8, 128), F32)],
        scratch_shapes=[pltpu.VMEM((KV_DIM, BLK), F32)] * 2 + [pltpu.VMEM((KV_DIM, 2 * BLK), F32)] * 2,
        args=(qkvt, qkvt, qkvt, qkvt, qkvt, sinks, o, do), sem=("arbitrary",), side=side)
    return own if side is None else (own, extra)


def _shift_down(x, j):
    if j == 0:
        return x
    row = lax.broadcasted_iota(jnp.int32, x.shape, 0)
    return jnp.where(row >= j, pltpu.roll(x, j, 0), 0.0)


def _shift_up(x, j):
    if j == 0:
        return x
    s = x.shape[0]
    row = lax.broadcasted_iota(jnp.int32, x.shape, 0)
    return jnp.where(row < s - j, pltpu.roll(x, s - j, 0), 0.0)


def _conv(x, w_ref, b_ref):
    kk = w_ref.shape[0]
    y = _shift_down(x, kk - 1) * w_ref[0:1, :]
    for q in range(1, kk):
        y = y + _shift_down(x, kk - 1 - q) * w_ref[q:q + 1, :]
    return y + b_ref[...]


def _conv_bwd(dy, x, w_ref, dx_dtype):
    kk = w_ref.shape[0]
    dx = _shift_up(dy, kk - 1) * w_ref[0:1, :]
    dws = [jnp.sum(dy * _shift_down(x, kk - 1), axis=0, keepdims=True)]
    for q in range(1, kk):
        dx = dx + _shift_up(dy, kk - 1 - q) * w_ref[q:q + 1, :]
        dws.append(jnp.sum(dy * _shift_down(x, kk - 1 - q), axis=0, keepdims=True))
    return dx.astype(dx_dtype), dws, jnp.sum(dy, axis=0, keepdims=True)


def _dsilu(y, sg):
    return sg * (1.0 + y * (1.0 - sg))


CT = 256


def _ssd_conv_fwd(proj, w, b):
    s = proj.shape[0]

    def body(x_ref, w_ref, b_ref, o_ref):
        y = _conv(x_ref[...], w_ref, b_ref)
        o_ref[...] = y * _sigmoid(y)

    return pl.pallas_call(
        body, name="ssd_conv_fwd", grid=(XBC_DIM // CT,),
        in_specs=[pl.BlockSpec((s, CT), lambda i: (0, O_X // CT + i)), pl.BlockSpec((4, CT), lambda i: (0, i)),
                  pl.BlockSpec((1, CT), lambda i: (0, i))],
        out_specs=pl.BlockSpec((s, CT), lambda i: (0, i)),
        out_shape=jax.ShapeDtypeStruct((s, XBC_DIM), F32), compiler_params=_cp(("parallel",)),
    )(proj, w, b)


def _ssd_conv_bwd(dact, proj, w, b):
    s = proj.shape[0]

    def body(d_ref, x_ref, w_ref, b_ref, dx_ref, dw_ref, db_ref):
        x = x_ref[...]
        y = _conv(x, w_ref, b_ref)
        dy = d_ref[...] * _dsilu(y, _sigmoid(y))
        dx, dws, db = _conv_bwd(dy, x, w_ref, BF16)
        dx_ref[...] = dx
        for q in range(4):
            dw_ref[q:q + 1, :] = dws[q]
        db_ref[...] = db

    return pl.pallas_call(
        body, name="ssd_conv_bwd", grid=(XBC_DIM // CT,),
        in_specs=[pl.BlockSpec((s, CT), lambda i: (0, i)), pl.BlockSpec((s, CT), lambda i: (0, O_X // CT + i)),
                  pl.BlockSpec((4, CT), lambda i: (0, i)), pl.BlockSpec((1, CT), lambda i: (0, i))],
        out_specs=[pl.BlockSpec((s, CT), lambda i: (0, i)), pl.BlockSpec((4, CT), lambda i: (0, i)),
                   pl.BlockSpec((1, CT), lambda i: (0, i))],
        out_shape=[jax.ShapeDtypeStruct((s, XBC_DIM), BF16), jax.ShapeDtypeStruct((4, XBC_DIM), F32),
                   jax.ShapeDtypeStruct((1, XBC_DIM), F32)],
        compiler_params=_cp(("parallel",)),
    )(dact, proj, w, b)


NFT = D_FF // CT


def _ffn_act_fwd(up, w, b):
    s = up.shape[0]

    def body(v_ref, g_ref, wv_ref, wg_ref, bv_ref, bg_ref, o_ref):
        val = _conv(v_ref[...], wv_ref, bv_ref)
        gt = _conv(g_ref[...], wg_ref, bg_ref)
        o_ref[...] = ((gt * _sigmoid(gt)) * val).astype(BF16)

    col = lambda off: (lambda i: (0, off + i))
    return pl.pallas_call(
        body, name="ffn_act_fwd", grid=(NFT,),
        in_specs=[pl.BlockSpec((s, CT), col(0)), pl.BlockSpec((s, CT), col(NFT)),
                  pl.BlockSpec((3, CT), col(0)), pl.BlockSpec((3, CT), col(NFT)),
                  pl.BlockSpec((1, CT), col(0)), pl.BlockSpec((1, CT), col(NFT))],
        out_specs=pl.BlockSpec((s, CT), col(0)),
        out_shape=jax.ShapeDtypeStruct((s, D_FF), BF16), compiler_params=_cp(("parallel",)),
    )(up, up, w, w, b, b)


def _ffn_act_bwd(dact, up, w, b):
    s = up.shape[0]

    def body(d_ref, v_ref, g_ref, wv_ref, wg_ref, bv_ref, bg_ref, dx_ref, dw_ref, db_ref):
        xv, xg = v_ref[...], g_ref[...]
        val = _conv(xv, wv_ref, bv_ref)
        gt = _conv(xg, wg_ref, bg_ref)
        sg = _sigmoid(gt)
        d = d_ref[...]
        for half, (dy, x, w_ref) in enumerate(((d * (gt * sg), xv, wv_ref), (d * val * _dsilu(gt, sg), xg, wg_ref))):
            dx, dws, db = _conv_bwd(dy, x, w_ref, BF16)
            dx_ref[half] = dx
            for q in range(3):
                dw_ref[half, q:q + 1, :] = dws[q]
            db_ref[half] = db

    col = lambda off: (lambda i: (0, off + i))
    both = lambda i: (0, 0, i)
    return pl.pallas_call(
        body, name="ffn_act_bwd", grid=(NFT,),
        in_specs=[pl.BlockSpec((s, CT), col(0)), pl.BlockSpec((s, CT), col(0)), pl.BlockSpec((s, CT), col(NFT)),
                  pl.BlockSpec((3, CT), col(0)), pl.BlockSpec((3, CT), col(NFT)),
                  pl.BlockSpec((1, CT), col(0)), pl.BlockSpec((1, CT), col(NFT))],
        out_specs=[pl.BlockSpec((2, s, CT), both), pl.BlockSpec((2, 3, CT), both), pl.BlockSpec((2, 1, CT), both)],
        out_shape=[jax.ShapeDtypeStruct((2, s, D_FF), BF16), jax.ShapeDtypeStruct((2, 3, D_FF), F32),
                   jax.ShapeDtypeStruct((2, 1, D_FF), F32)],
        compiler_params=_cp(("parallel",)),
    )(dact, up, up, w, w, b, b)


def _expand_mat():
    r = lax.broadcasted_iota(jnp.int32, (128, D_INNER), 0)
    c = lax.broadcasted_iota(jnp.int32, (128, D_INNER), 1)
    return ((c >> 6) == r).astype(BF16)


def _reduce_mat():
    r = lax.broadcasted_iota(jnp.int32, (D_INNER, 128), 0)
    c = lax.broadcasted_iota(jnp.int32, (D_INNER, 128), 1)
    return ((r >> 6) == c).astype(BF16)


def _split(v, parts):
    out = []
    for _ in range(parts - 1):
        p = v.astype(BF16)
        out.append(p)
        v = v - p.astype(F32)
    out.append(v.astype(BF16))
    return out


def _sel_dot(v, sel, parts):
    acc = None
    for p in reversed(_split(v, parts)):
        t = _dot(p, sel)
        acc = t if acc is None else acc + t
    return acc


def _row8(v):
    return jnp.broadcast_to(v, (8, v.shape[1]))


def _tril():
    r = lax.broadcasted_iota(jnp.int32, (BLK, BLK), 0)
    c = lax.broadcasted_iota(jnp.int32, (BLK, BLK), 1)
    return r >= c


def _softplus(x):
    return jnp.maximum(x, 0.0) + jnp.log(1.0 + jnp.exp(-jnp.abs(x)))


def _ssd_common(dtraw_ref, dtb_ref, alog_ref):
    causal = _tril()
    e_mat = _expand_mat()
    a_neg = -jnp.exp(alog_ref[...])
    dt = _softplus(dtraw_ref[...] + dtb_ref[...])
    a_cs = _dot(causal.astype(F32), dt * a_neg, HI)
    a_cs_t = a_cs.T
    dt_x = _sel_dot(dt, e_mat, 3)
    acs_x = _sel_dot(a_cs, e_mat, 3)
    alast_x = acs_x[BLK - 1:BLK, :]
    ea_x = jnp.exp(acs_x)
    ds_x = jnp.exp(alast_x - acs_x)
    elast_x = jnp.exp(alast_x)
    return causal, e_mat, a_neg, dt, a_cs, a_cs_t, dt_x, ea_x, ds_x, elast_x


def _decay(a_cs, a_cs_t, h, causal):
    seg = a_cs[:, h:h + 1] - a_cs_t[h:h + 1, :]
    return jnp.where(causal, jnp.exp(jnp.where(causal, seg, 0.0)), 0.0)


def _ssd_fwd(xbc, proj, dt_bias, a_log, d_skip, side=None):
    s = xbc.shape[0]
    nc = s // BLK

    def body(xs_ref, b_ref, c_ref, dtraw_ref, dtb_ref, alog_ref, dskip_ref, y_ref, hp_ref, h_scr, xc16):
        @pl.when(pl.program_id(0) == 0)
        def _():
            h_scr[...] = jnp.zeros_like(h_scr)

        causal, e_mat, _, _, a_cs, a_cs_t, dt_x, ea_x, ds_x, elast_x = _ssd_common(dtraw_ref, dtb_ref, alog_ref)
        dskip_x = _sel_dot(_row8(dskip_ref[...]), e_mat, 3)[0:1]
        xs = xs_ref[...]
        xc = xs * dt_x
        xc16[...] = xc.astype(BF16)
        xcd = (xc * ds_x).astype(BF16)
        hp_ref[0] = h_scr[...]
        for g in range(4):
            gs = slice(g * 512, (g + 1) * 512)
            cg = c_ref[:, g * 128:(g + 1) * 128].astype(BF16)
            bg = b_ref[:, g * 128:(g + 1) * 128].astype(BF16)
            cb = _dot_nt(cg, bg)
            hg = h_scr[:, gs]
            yoff = _dot(cg, hg.astype(BF16)) * ea_x[:, gs]
            for j in range(8):
                h = g * 8 + j
                hsl = slice(h * 64, (h + 1) * 64)
                mm = (cb * _decay(a_cs, a_cs_t, h, causal)).astype(BF16)
                y_ref[:, hsl] = _dot(mm, xc16[:, hsl])
            y_ref[:, gs] += yoff + xs[:, gs] * dskip_x[:, gs]
            h_scr[:, gs] = hg * elast_x[:, gs] + _dot_tn(bg, xcd[:, gs])

    vec = pl.BlockSpec((1, 128), lambda c: (0, 0))
    own, extra = _hosted(
        body, name="ssd_fwd", grid=(nc,),
        in_specs=[pl.BlockSpec((BLK, D_INNER), lambda c: (c, 0)),
                  pl.BlockSpec((BLK, BC_DIM), lambda c: (c, D_INNER // BC_DIM)),
                  pl.BlockSpec((BLK, BC_DIM), lambda c: (c, D_INNER // BC_DIM + 1)),
                  pl.BlockSpec((BLK, 128), lambda c: (c, O_DT // 128)), vec, vec, vec],
        out_specs=[pl.BlockSpec((BLK, D_INNER), lambda c: (c, 0)),
                   pl.BlockSpec((1, 128, D_INNER), lambda c: (c, 0, 0))],
        out_shape=[jax.ShapeDtypeStruct((s, D_INNER), F32), jax.ShapeDtypeStruct((nc, 128, D_INNER), F32)],
        scratch_shapes=[pltpu.VMEM((128, D_INNER), F32), pltpu.VMEM((BLK, D_INNER), BF16)],
        args=(xbc, xbc, xbc, proj, dt_bias, a_log, d_skip), sem=("arbitrary",), side=side)
    return own if side is None else (own, extra)


def _ssd_bwd(xbc, proj, dt_bias, a_log, d_skip, hprev, dy, side=None):
    s = xbc.shape[0]
    nc = s // BLK

    def body(xs_ref, b_ref, c_ref, dtraw_ref, dtb_ref, alog_ref, dskip_ref, hp_ref, dy_ref,
             dxbc_ref, ddt_ref, dvec_ref, dh_scr, xc16, dy16, dxc_scr, dacs_r, tdiff):
        step = pl.program_id(0)
        dacs_r[...] = jnp.zeros_like(dacs_r)

        @pl.when(step == 0)
        def _():
            dh_scr[...] = jnp.zeros_like(dh_scr)
            dvec_ref[...] = jnp.zeros_like(dvec_ref)

        causal, e_mat, a_neg, dt, a_cs, a_cs_t, dt_x, ea_x, ds_x, elast_x = _ssd_common(dtraw_ref, dtb_ref, alog_ref)
        r_mat = _reduce_mat()
        lane = lax.broadcasted_iota(jnp.int32, (1, 128), 1)
        dskip_x = _sel_dot(_row8(dskip_ref[...]), e_mat, 3)[0:1]
        xs = xs_ref[...]
        dy = dy_ref[...]
        xc = xs * dt_x
        xcd = xc * ds_x
        xc16[...] = xc.astype(BF16)
        dy16[...] = dy.astype(BF16)
        dyea = dy * ea_x
        dh = dh_scr[...]
        hp = hp_ref[0]
        dalast_x = jnp.sum(dh * hp, axis=0, keepdims=True) * elast_x
        dacs = jnp.zeros((BLK, 128), F32)
        for g in range(4):
            gs = slice(g * 512, (g + 1) * 512)
            bsl = slice(g * 128, (g + 1) * 128)
            cg = c_ref[:, bsl].astype(BF16)
            bg = b_ref[:, bsl].astype(BF16)
            cb = _dot_nt(cg, bg)
            hg16 = hp[:, gs].astype(BF16)
            dhg16 = dh[:, gs].astype(BF16)
            raw = _dot(cg, hg16)
            draw16 = dyea[:, gs].astype(BF16)
            dcg = _dot_nt(draw16, hg16)
            dhp_g = _dot_tn(cg, draw16)
            dbg = _dot_nt(xcd[:, gs].astype(BF16), dhg16)
            dxcd = _dot(bg, dhg16)
            dcb = jnp.zeros((BLK, BLK), F32)
            for j in range(8):
                h = g * 8 + j
                hsl = slice(h * 64, (h + 1) * 64)
                decay = _decay(a_cs, a_cs_t, h, causal)
                m = cb * decay
                dm = _dot_nt(dy16[:, hsl], xc16[:, hsl])
                dxc_scr[:, hsl] = _dot_tn(m.astype(BF16), dy16[:, hsl])
                dcb = dcb + dm * decay
                dseg = dm * m
                oneh = jnp.where(lane == h, 1.0, 0.0)
                dacs = dacs + jnp.sum(dseg, axis=1, keepdims=True) * oneh
                dacs_r[h:h + 1, :] = jnp.sum(dseg, axis=0, keepdims=True)
            dcb16 = dcb.astype(BF16)
            dcg = dcg + _dot(dcb16, bg)
            dbg = dbg + _dot_tn(dcb16, cg)
            dxbc_ref[:, D_INNER + g * 128:D_INNER + (g + 1) * 128] = dbg
            dxbc_ref[:, D_INNER + BC_DIM + g * 128:D_INNER + BC_DIM + (g + 1) * 128] = dcg
            dxc_scr[:, gs] += dxcd * ds_x[:, gs]
            dh_scr[:, gs] = dh[:, gs] * elast_x[:, gs] + dhp_g
            tst = dxcd * xcd[:, gs]
            tdiff[:, gs] = dy[:, gs] * (raw * ea_x[:, gs]) - tst
            tdiff[BLK - 1:BLK, gs] += jnp.sum(tst, axis=0, keepdims=True)
        dxc = dxc_scr[...]
        row = lax.broadcasted_iota(jnp.int32, (BLK, D_INNER), 0)
        tfull = tdiff[...] + jnp.where(row == BLK - 1, dalast_x, 0.0)
        dacs = dacs + _sel_dot(tfull, r_mat, 2) - dacs_r[...].T
        da = _dot_tn(causal.astype(F32), dacs, HI)
        ddt = da * a_neg + _sel_dot(dxc * xs, r_mat, 2)
        lmask = lax.broadcasted_iota(jnp.int32, (BLK, 128), 1) < N_SSD_HEADS
        ddtraw = jnp.where(lmask, ddt * _sigmoid(dtraw_ref[...] + dtb_ref[...]), 0.0)
        ddt_ref[...] = ddtraw.astype(BF16)
        dxbc_ref[:, 0:D_INNER] = dy * dskip_x + dxc * dt_x
        dvec_ref[0:1, :] += jnp.sum(ddtraw, axis=0, keepdims=True)
        dvec_ref[1:2, :] += jnp.where(lane < N_SSD_HEADS, jnp.sum(da * dt, axis=0, keepdims=True) * a_neg, 0.0)
        dvec_ref[2:3, :] += _sel_dot(_row8(jnp.sum(dy * xs, axis=0, keepdims=True)), r_mat, 3)[0:1]

    rev = lambda c: nc - 1 - c
    vec = pl.BlockSpec((1, 128), lambda c: (0, 0))
    own, extra = _hosted(
        body, name="ssd_bwd", grid=(nc,),
        in_specs=[pl.BlockSpec((BLK, D_INNER), lambda c: (rev(c), 0)),
                  pl.BlockSpec((BLK, BC_DIM), lambda c: (rev(c), D_INNER // BC_DIM)),
                  pl.BlockSpec((BLK, BC_DIM), lambda c: (rev(c), D_INNER // BC_DIM + 1)),
                  pl.BlockSpec((BLK, 128), lambda c: (rev(c), O_DT // 128)), vec, vec, vec,
                  pl.BlockSpec((1, 128, D_INNER), lambda c: (rev(c), 0, 0)),
                  pl.BlockSpec((BLK, D_INNER), lambda c: (rev(c), 0))],
        out_specs=[pl.BlockSpec((BLK, XBC_DIM), lambda c: (rev(c), 0)),
                   pl.BlockSpec((BLK, 128), lambda c: (rev(c), 0)),
                   pl.BlockSpec((8, 128), lambda c: (0, 0))],
        out_shape=[jax.ShapeDtypeStruct((s, XBC_DIM), F32), jax.ShapeDtypeStruct((s, 128), BF16),
                   jax.ShapeDtypeStruct((8, 128), F32)],
        scratch_shapes=[pltpu.VMEM((128, D_INNER), F32), pltpu.VMEM((BLK, D_INNER), BF16),
                        pltpu.VMEM((BLK, D_INNER), BF16), pltpu.VMEM((BLK, D_INNER), F32),
                        pltpu.VMEM((128, BLK), F32), pltpu.VMEM((BLK, D_INNER), F32)],
        args=(xbc, xbc, xbc, proj, dt_bias, a_log, d_skip, hprev, dy), sem=("arbitrary",), side=side)
    return own if side is None else (own, extra)


GW = 512


def _gate_norm_fwd(y, proj, wn, *, tm=512):
    s = y.shape[0]
    tm = _tile(s, tm)

    def body(y_ref, z_ref, w_ref, o_ref):
        z = z_ref[...]
        y2 = y_ref[...] * (z * _sigmoid(z))
        r = lax.rsqrt(jnp.mean(y2 * y2, axis=-1, keepdims=True) + EPS)
        o_ref[...] = ((y2 * r) * w_ref[...]).astype(BF16)

    return pl.pallas_call(
        body, name="gate_norm_fwd", grid=(s // tm, 4),
        in_specs=[pl.BlockSpec((tm, GW), lambda i, g: (i, g)), pl.BlockSpec((tm, GW), lambda i, g: (i, O_Z // GW + g)),
                  pl.BlockSpec((1, GW), lambda i, g: (0, g))],
        out_specs=pl.BlockSpec((tm, GW), lambda i, g: (i, g)),
        out_shape=jax.ShapeDtypeStruct((s, D_INNER), BF16), compiler_params=_cp(("parallel", "parallel")),
    )(y, proj, wn)


def _gate_norm_bwd(dyn, y, proj, wn, *, tm=512):
    s = y.shape[0]
    tm = _tile(s, tm)

    def body(d_ref, y_ref, z_ref, w_ref, dy_ref, dz_ref, dw_ref):
        i = pl.program_id(1)
        z = z_ref[...]
        sg = _sigmoid(z)
        sz = z * sg
        yv = y_ref[...]
        y2 = yv * sz
        r = lax.rsqrt(jnp.mean(y2 * y2, axis=-1, keepdims=True) + EPS)
        xh = y2 * r
        dv = d_ref[...]
        g = dv * w_ref[...]
        dy2 = r * (g - xh * jnp.mean(g * xh, axis=-1, keepdims=True))
        dy_ref[...] = dy2 * sz
        dz_ref[...] = (dy2 * yv * _dsilu(z, sg)).astype(BF16)
        part = jnp.sum(dv * xh, axis=0, keepdims=True)

        @pl.when(i == 0)
        def _():
            dw_ref[...] = part

        @pl.when(i > 0)
        def _():
            dw_ref[...] += part

    blk = pl.BlockSpec((tm, GW), lambda g, i: (i, g))
    vec = pl.BlockSpec((1, GW), lambda g, i: (0, g))
    return pl.pallas_call(
        body, name="gate_norm_bwd", grid=(4, s // tm),
        in_specs=[blk, blk, pl.BlockSpec((tm, GW), lambda g, i: (i, O_Z // GW + g)), vec],
        out_specs=[blk, blk, vec],
        out_shape=[jax.ShapeDtypeStruct((s, D_INNER), F32), jax.ShapeDtypeStruct((s, D_INNER), BF16),
                   jax.ShapeDtypeStruct((1, D_INNER), F32)],
        compiler_params=_cp(("parallel", "arbitrary")),
    )(dyn, y, proj, wn)


def _merge_fwd(proj, b_gate, attn, ssd_out, *, tm=512):
    s = attn.shape[0]
    tm = _tile(s, tm)

    def body(ga_ref, gs_ref, ba_ref, bs_ref, a_ref, s_ref, o_ref):
        ga = _sigmoid(ga_ref[...] + ba_ref[...])
        gs = _sigmoid(gs_ref[...] + bs_ref[...])
        o_ref[...] = (ga * a_ref[...] + gs * s_ref[...]).astype(BF16)

    blk = pl.BlockSpec((tm, GW), lambda i, j: (i, j))
    return pl.pallas_call(
        body, name="merge_fwd", grid=(s // tm, 2),
        in_specs=[pl.BlockSpec((tm, GW), lambda i, j: (i, O_GA // GW + j)),
                  pl.BlockSpec((tm, GW), lambda i, j: (i, O_GS // GW + j)),
                  pl.BlockSpec((1, GW), lambda i, j: (0, j)), pl.BlockSpec((1, GW), lambda i, j: (0, 2 + j)), blk, blk],
        out_specs=blk, out_shape=jax.ShapeDtypeStruct((s, D_MODEL), BF16),
        compiler_params=_cp(("parallel", "parallel")),
    )(proj, proj, b_gate, b_gate, attn, ssd_out)


def _merge_bwd(dm, proj, b_gate, attn, ssd_out, *, tm=512):
    s = attn.shape[0]
    tm = _tile(s, tm)

    def body(d_ref, ga_ref, gs_ref, ba_ref, bs_ref, a_ref, s_ref, da_ref, ds_ref, dga_ref, dgs_ref, dba_ref, dbs_ref):
        i = pl.program_id(1)
        ga = _sigmoid(ga_ref[...] + ba_ref[...])
        gs = _sigmoid(gs_ref[...] + bs_ref[...])
        d = d_ref[...]
        da_ref[...] = (d * ga).astype(BF16)
        ds_ref[...] = (d * gs).astype(BF16)
        dga = d * a_ref[...] * (ga * (1.0 - ga))
        dgs = d * s_ref[...] * (gs * (1.0 - gs))
        dga_ref[...] = dga.astype(BF16)
        dgs_ref[...] = dgs.astype(BF16)
        pa = jnp.sum(dga, axis=0, keepdims=True)
        ps = jnp.sum(dgs, axis=0, keepdims=True)

        @pl.when(i == 0)
        def _():
            dba_ref[...] = pa
            dbs_ref[...] = ps

        @pl.when(i > 0)
        def _():
            dba_ref[...] += pa
            dbs_ref[...] += ps

    blk = pl.BlockSpec((tm, GW), lambda j, i: (i, j))
    vec = pl.BlockSpec((1, GW), lambda j, i: (0, j))
    sd = jax.ShapeDtypeStruct((s, D_MODEL), BF16)
    vd = jax.ShapeDtypeStruct((1, D_MODEL), F32)
    return pl.pallas_call(
        body, name="merge_bwd", grid=(2, s // tm),
        in_specs=[blk, pl.BlockSpec((tm, GW), lambda j, i: (i, O_GA // GW + j)),
                  pl.BlockSpec((tm, GW), lambda j, i: (i, O_GS // GW + j)),
                  vec, pl.BlockSpec((1, GW), lambda j, i: (0, 2 + j)), blk, blk],
        out_specs=[blk, blk, blk, blk, vec, vec], out_shape=[sd, sd, sd, sd, vd, vd],
        compiler_params=_cp(("parallel", "arbitrary")),
    )(dm, proj, proj, b_gate, b_gate, attn, ssd_out)


def _adamw_math(w, g, m, v):
    mn = ADAM_B1 * m + (1.0 - ADAM_B1) * g
    vn = ADAM_B2 * v + (1.0 - ADAM_B2) * (g * g)
    m_hat = mn / (1.0 - ADAM_B1 ** ADAM_STEP)
    v_hat = vn / (1.0 - ADAM_B2 ** ADAM_STEP)
    return -ADAM_LR * (m_hat / (jnp.sqrt(v_hat) + ADAM_EPS) + ADAM_WD * w), mn, vn


def _adamw_many(ws, gs, ms, vs):
    n = len(ws)

    def body(*refs):
        outs = refs[4 * n:]
        for i in range(n):
            res = _adamw_math(*[refs[q * n + i][...] for q in range(4)])
            for q in range(3):
                outs[q * n + i][...] = res[q]

    return pl.pallas_call(body, name="adamw_small", out_shape=[jax.ShapeDtypeStruct(w.shape, F32) for w in ws] * 3,
                          compiler_params=_cp())(*ws, *gs, *ms, *vs)


def _adamw(w, g, m, v, *, name, tm=128):
    r, c = w.shape
    tm = r if (r < tm or r % tm) else tm

    def body(w_ref, g_ref, m_ref, v_ref, d_ref, nm_ref, nv_ref, g_out):
        gv = g_ref[:, :c]
        d_ref[...], nm_ref[...], nv_ref[...] = _adamw_math(w_ref[...], gv, m_ref[...], v_ref[...])
        g_out[...] = gv

    blk = pl.BlockSpec((tm, c), lambda i: (i, 0))
    sd = jax.ShapeDtypeStruct((r, c), F32)
    return pl.pallas_call(
        body, name=name, grid=(r // tm,), in_specs=[blk, pl.BlockSpec((tm, g.shape[1]), lambda i: (i, 0)), blk, blk],
        out_specs=[blk] * 4, out_shape=[sd] * 4, compiler_params=_cp(("parallel",)),
    )(w, g, m, v)


ANY = pl.BlockSpec(memory_space=pl.ANY)
N_CHIPS = 4


def _chip_of(k, x, y):
    return (x ^ (k >> 1), y ^ (k & 1))


def _all_gather_small(shard):
    r, c = shard.shape
    hr = r // 2

    def body(sh_ref, out_ref, send_sems, recv_sems, local_sem):
        x, y, cc = lax.axis_index("x"), lax.axis_index("y"), lax.axis_index("c")

        def half(px, py, pc):
            return out_ref.at[2 * px + py, pl.ds(pc * hr, hr), :]

        def copy(k, px, py, pc, to, src=None):
            return pltpu.make_async_remote_copy(
                src_ref=half(px, py, pc) if src is None else src, dst_ref=half(px, py, pc),
                send_sem=send_sems.at[k], recv_sem=recv_sems.at[k], device_id=to, device_id_type=MESH)

        mine = pltpu.make_async_copy(sh_ref, out_ref.at[2 * x + y], local_sem)
        mine.start()
        chips = [_chip_of(k, x, y) for k in (1, 2, 3)]
        first = [copy(j, x, y, cc, (*chip, cc), src=sh_ref.at[pl.ds(cc * hr, hr), :]) for j, chip in enumerate(chips)]
        for cp in first:
            cp.start()
        passed = [copy(3 + j, *chip, cc, (x, y, 1 - cc)) for j, chip in enumerate(chips)]
        for j, chip in enumerate(chips):
            copy(j, *chip, cc, (x, y, cc)).wait_recv()
            passed[j].start()
        for j, chip in enumerate(chips):
            copy(3 + j, *chip, 1 - cc, (x, y, cc)).wait_recv()
        for cp in first + passed:
            cp.wait_send()
        mine.wait()

    return pl.pallas_call(
        body, name="all_gather_small", in_specs=[ANY], out_specs=ANY,
        out_shape=jax.ShapeDtypeStruct((N_CHIPS, r, c), shard.dtype),
        scratch_shapes=[pltpu.SemaphoreType.DMA((6,)), pltpu.SemaphoreType.DMA((6,)), pltpu.SemaphoreType.DMA],
    )(shard)


def _cast_bf16(a, *, name, tm=512):
    n, r, c = a.shape
    tm = _tile(r, tm) if r % 128 == 0 else r

    def body(a_ref, o_ref):
        o_ref[...] = a_ref[...].astype(BF16)

    blk = pl.BlockSpec((1, tm, c), lambda i, j: (i, j, 0))
    return pl.pallas_call(body, name=name, grid=(n, r // tm), in_specs=[blk], out_specs=blk,
                          out_shape=jax.ShapeDtypeStruct(a.shape, BF16), compiler_params=_cp(("parallel", "parallel")))(a)


def _pair_exchange(g16, hr):
    n, r, c = g16.shape

    def body(g_ref, out_ref, send_sem, recv_sem):
        x, y, cc = lax.axis_index("x"), lax.axis_index("y"), lax.axis_index("c")
        cp = pltpu.make_async_remote_copy(
            src_ref=g_ref.at[:, pl.ds((1 - cc) * hr, hr), :], dst_ref=out_ref, send_sem=send_sem, recv_sem=recv_sem,
            device_id=(x, y, 1 - cc), device_id_type=MESH)
        cp.start()
        cp.wait()

    return pl.pallas_call(
        body, name="grad_pair_exchange", in_specs=[ANY], out_specs=ANY,
        out_shape=jax.ShapeDtypeStruct((n, hr, c), g16.dtype),
        scratch_shapes=[pltpu.SemaphoreType.DMA, pltpu.SemaphoreType.DMA],
    )(g16)


def _pair_add(g, recv, half_idx, hr, *, tm=384):
    n, r, c = g.shape
    nt = hr // tm

    def body(hi_ref, g_ref, r_ref, o32_ref, o16_ref):
        v = g_ref[...] + r_ref[...].astype(F32)
        o32_ref[...] = v
        o16_ref[...] = v.astype(BF16)

    gs = pltpu.PrefetchScalarGridSpec(
        num_scalar_prefetch=1, grid=(n, nt),
        in_specs=[pl.BlockSpec((1, tm, c), lambda i, j, hi: (i, hi[0] * nt + j, 0)),
                  pl.BlockSpec((1, tm, c), lambda i, j, hi: (i, j, 0))],
        out_specs=[pl.BlockSpec((1, tm, c), lambda i, j, hi: (i, j, 0))] * 2)
    return pl.pallas_call(
        body, name="grad_pair_add", grid_spec=gs,
        out_shape=[jax.ShapeDtypeStruct((n, hr, c), F32), jax.ShapeDtypeStruct((n, hr, c), BF16)],
        compiler_params=_cp(("parallel", "parallel")),
    )(half_idx, g, recv)


def _chip_exchange(p16):
    n, hr, c = p16.shape

    def body(p_ref, out_ref, send_sems, recv_sems):
        x, y, cc = lax.axis_index("x"), lax.axis_index("y"), lax.axis_index("c")
        cps = []
        for j, k in enumerate((1, 2, 3)):
            px, py = _chip_of(k, x, y)
            cps.append(pltpu.make_async_remote_copy(
                src_ref=p_ref.at[2 * px + py], dst_ref=out_ref.at[j], send_sem=send_sems.at[j], recv_sem=recv_sems.at[j],
                device_id=(px, py, cc), device_id_type=MESH))
        for cp in cps:
            cp.start()
        for cp in cps:
            cp.wait()

    return pl.pallas_call(
        body, name="grad_chip_exchange", in_specs=[ANY], out_specs=ANY,
        out_shape=jax.ShapeDtypeStruct((3, hr, c), p16.dtype),
        scratch_shapes=[pltpu.SemaphoreType.DMA((3,)), pltpu.SemaphoreType.DMA((3,))],
    )(p16)


def _chip_add(p32, recv, chip_idx, *, tm=384):
    n, hr, c = p32.shape

    def body(ci_ref, p_ref, r_ref, o_ref):
        o_ref[...] = ((p_ref[0] + r_ref[0].astype(F32)) + r_ref[1].astype(F32)) + r_ref[2].astype(F32)

    gs = pltpu.PrefetchScalarGridSpec(
        num_scalar_prefetch=1, grid=(hr // tm,),
        in_specs=[pl.BlockSpec((1, tm, c), lambda j, ci: (ci[0], j, 0)), pl.BlockSpec((3, tm, c), lambda j, ci: (0, j, 0))],
        out_specs=pl.BlockSpec((tm, c), lambda j, ci: (j, 0)))
    return pl.pallas_call(
        body, name="grad_chip_add", grid_spec=gs, out_shape=jax.ShapeDtypeStruct((hr, c), F32),
        compiler_params=_cp(("parallel",)),
    )(chip_idx, p32, recv)


def _pair_gather(f):
    hr, c = f.shape

    def body(f_ref, out_ref, send_sem, recv_sem, local_sem):
        x, y, cc = lax.axis_index("x"), lax.axis_index("y"), lax.axis_index("c")
        mine = pltpu.make_async_copy(f_ref, out_ref.at[pl.ds(cc * hr, hr), :], local_sem)
        mine.start()
        cp = pltpu.make_async_remote_copy(
            src_ref=f_ref, dst_ref=out_ref.at[pl.ds(cc * hr, hr), :], send_sem=send_sem, recv_sem=recv_sem,
            device_id=(x, y, 1 - cc), device_id_type=MESH)
        cp.start()
        cp.wait()
        mine.wait()

    return pl.pallas_call(
        body, name="grad_pair_gather", in_specs=[ANY], out_specs=ANY,
        out_shape=jax.ShapeDtypeStruct((2 * hr, c), f.dtype),
        scratch_shapes=[pltpu.SemaphoreType.DMA, pltpu.SemaphoreType.DMA, pltpu.SemaphoreType.DMA],
    )(f)


def _all_reduce_small(buf):
    r, c = buf.shape

    def body(b_ref, out_ref, gat, send_sems, recv_sems):
        x, y, cc = lax.axis_index("x"), lax.axis_index("y"), lax.axis_index("c")
        me = 4 * x + 2 * y + cc
        gat[me] = b_ref[...]
        cps = []
        for k in range(1, 8):
            px, py, pc = x ^ (k >> 2), y ^ ((k >> 1) & 1), cc ^ (k & 1)
            cps.append(pltpu.make_async_remote_copy(
                src_ref=b_ref, dst_ref=gat.at[me], send_sem=send_sems.at[k - 1], recv_sem=recv_sems.at[k - 1],
                device_id=(px, py, pc), device_id_type=MESH))
        for cp in cps:
            cp.start()
        for cp in cps:
            cp.wait()
        acc = gat[0]
        for d in range(1, 8):
            acc = acc + gat[d]
        out_ref[...] = acc

    vm = pl.BlockSpec(memory_space=pltpu.VMEM)
    return pl.pallas_call(
        body, name="all_reduce_small", in_specs=[vm], out_specs=vm, out_shape=jax.ShapeDtypeStruct((r, c), F32),
        scratch_shapes=[pltpu.VMEM((8, r, c), F32), pltpu.SemaphoreType.DMA((7,)), pltpu.SemaphoreType.DMA((7,))],
        compiler_params=pltpu.CompilerParams(vmem_limit_bytes=VMEM_LIMIT),
    )(buf)


def _pipe(fn, ins, outs, tr):
    shape = ins[0].shape
    lead, (r, c) = shape[:-2], shape[-2:]
    assert len(lead) <= 1 and r % tr == 0
    nr = r // tr
    n = nr * (lead[0] if lead else 1)
    ni, no = len(ins), len(outs)

    def blk(ref, step):
        rows = pl.ds((step % nr) * tr, tr)
        return ref.at[step // nr, rows, :] if lead else ref.at[rows, :]

    def scoped(*bufs):
        ibufs, obufs, isem, osem = bufs[:ni], bufs[ni:ni + no], bufs[-2], bufs[-1]

        def in_copy(q, step, slot):
            return pltpu.make_async_copy(blk(ins[q], step), ibufs[q].at[slot], isem.at[q, slot])

        def out_copy(q, step, slot):
            return pltpu.make_async_copy(obufs[q].at[slot], blk(outs[q], step), osem.at[q, slot])

        for step in range(min(nbuf - 1, n)):
            for q in range(ni):
                in_copy(q, step, step % nbuf).start()
        for step in range(n):
            slot = step % nbuf
            if step + nbuf - 1 < n:
                for q in range(ni):
                    in_copy(q, step + nbuf - 1, (step + nbuf - 1) % nbuf).start()
            for q in range(ni):
                in_copy(q, step, slot).wait()
            if step >= nbuf:
                for q in range(no):
                    out_copy(q, step - nbuf, slot).wait()
            res = fn(*[ibufs[q][slot] for q in range(ni)])
            for q in range(no):
                obufs[q][slot] = res[q].astype(obufs[q].dtype)
                out_copy(q, step, slot).start()
        for step in range(max(n - nbuf, 0), n):
            for q in range(no):
                out_copy(q, step, step % nbuf).wait()

    assert n <= 8
    nbuf = min(n, 4)
    pl.run_scoped(scoped, *[pltpu.VMEM((nbuf, tr, c), q.dtype) for q in ins], *[pltpu.VMEM((nbuf, tr, c), q.dtype) for q in outs],
                  pltpu.SemaphoreType.DMA((ni, nbuf)), pltpu.SemaphoreType.DMA((no, nbuf)))


W_IN_PAD = 2304
BIG = ("w_in", "w_attn_o", "w_ssd_o", "w_out", "w_up", "w_down")
BIG_SHAPE = dict(w_in=(D_MODEL, W_IN_PAD), w_attn_o=(Q_DIM // 4, D_MODEL), w_ssd_o=(D_INNER // 4, D_MODEL),
                 w_out=(D_MODEL // 4, D_MODEL), w_up=(D_MODEL, 2 * D_FF // 4), w_down=(D_FF // 4, D_MODEL))
BIG_TR = dict(w_in=128, w_attn_o=128, w_ssd_o=128, w_out=128, w_up=128, w_down=176)
X_FIRST = dict(w_in=True, w_attn_o=True, w_ssd_o=False, w_out=True, w_up=False, w_down=False)


def _neighbours(x, y, x_first):
    xn, yn = (1 - x, y), (x, 1 - y)
    n1, n2 = (xn, yn) if x_first else (yn, xn)
    slot = lambda ch: 2 * ch[0] + ch[1]
    return n1, n2, slot(n1), slot(n2), slot((1 - x, 1 - y))


def _gather_big(shards):
    nt = len(BIG)

    def body(*refs):
        sh, out = refs[:nt], refs[nt:2 * nt]
        send_sems, recv_sems = refs[2 * nt:]
        x, y, cc = lax.axis_index("x"), lax.axis_index("y"), lax.axis_index("c")
        me = 2 * x + y
        sib = (x, y, 1 - cc)
        for t, n in enumerate(BIG):
            _pipe(lambda v: (v,), [sh[t]], [out[t].at[me]], BIG_TR[n])

        def copy(t, k, slot, pc, to):
            hr = BIG_SHAPE[BIG[t]][0] // 2
            ref = out[t].at[slot, pl.ds(pc * hr, hr), :]
            return pltpu.make_async_remote_copy(src_ref=ref, dst_ref=ref, send_sem=send_sems.at[6 * t + k],
                                                recv_sem=recv_sems.at[6 * t + k], device_id=to, device_id_type=MESH)

        started = []

        def start(cp):
            cp.start()
            started.append(cp)

        geo = [_neighbours(x, y, X_FIRST[n]) for n in BIG]
        for t in range(nt):
            n1, n2, _, _, _ = geo[t]
            start(copy(t, 0, me, cc, (*n1, cc)))
            start(copy(t, 1, me, cc, (*n2, cc)))
        for t in range(nt):
            n1, n2, s1, s2, sd = geo[t]
            copy(t, 0, s1, cc, sib).wait_recv()
            start(copy(t, 2, s1, cc, (*n2, cc)))
            start(copy(t, 3, s1, cc, sib))
            copy(t, 1, s2, cc, sib).wait_recv()
            start(copy(t, 4, s2, cc, sib))
        for t in range(nt):
            _, _, s1, s2, sd = geo[t]
            copy(t, 2, sd, cc, sib).wait_recv()
            start(copy(t, 5, sd, cc, sib))
        for t in range(nt):
            _, _, s1, s2, sd = geo[t]
            copy(t, 3, s1, 1 - cc, sib).wait_recv()
            copy(t, 4, s2, 1 - cc, sib).wait_recv()
            copy(t, 5, sd, 1 - cc, sib).wait_recv()
        for cp in started:
            cp.wait_send()

    return pl.pallas_call(
        body, name="gather_big", in_specs=[ANY] * nt, out_specs=[ANY] * nt,
        out_shape=[jax.ShapeDtypeStruct((N_CHIPS, *BIG_SHAPE[n]), BF16) for n in BIG],
        scratch_shapes=[pltpu.SemaphoreType.DMA((6 * nt,)), pltpu.SemaphoreType.DMA((6 * nt,))],
        compiler_params=pltpu.CompilerParams(vmem_limit_bytes=VMEM_LIMIT),
    )(*shards)


def _reduce_big(grads):
    nt = len(BIG)
    nw = 7

    def body(*refs):
        g = refs[:nt]
        fin = refs[nt:2 * nt]
        work = refs[2 * nt:2 * nt + nw * nt]
        send_sems, recv_sems = refs[2 * nt + nw * nt:]
        x, y, cc = lax.axis_index("x"), lax.axis_index("y"), lax.axis_index("c")
        me = 2 * x + y
        sib = (x, y, 1 - cc)
        started = []

        def rcopy(t, k, src, dst, to):
            cp = pltpu.make_async_remote_copy(src_ref=src, dst_ref=dst, send_sem=send_sems.at[5 * t + k],
                                              recv_sem=recv_sems.at[5 * t + k], device_id=to, device_id_type=MESH)
            return cp

        def start(cp):
            cp.start()
            started.append(cp)

        geo = [_neighbours(x, y, X_FIRST[n]) for n in BIG]
        hrs = [BIG_SHAPE[n][0] // 2 for n in BIG]
        wk = lambda t: work[nw * t:nw * (t + 1)]
        one = lambda ref, slot: ref.at[pl.ds(slot, 1)]
        for t in range(nt):
            recv_a = wk(t)[0]
            start(rcopy(t, 0, g[t].at[:, pl.ds((1 - cc) * hrs[t], hrs[t]), :], recv_a, sib))
        for t, n in enumerate(BIG):
            recv_a, p32, p16, r1, qme, qs2, r2 = wk(t)
            n1, n2, s1, s2, sd = geo[t]
            rcopy(t, 0, recv_a, recv_a, sib).wait_recv()
            _pipe(lambda a, b: (a + b, a + b), [g[t].at[:, pl.ds(cc * hrs[t], hrs[t]), :], recv_a], [p32, p16], BIG_TR[n])
            start(rcopy(t, 1, one(p16, s1), one(r1, 0), (*n1, cc)))
            start(rcopy(t, 2, one(p16, sd), one(r1, 1), (*n1, cc)))
        for t, n in enumerate(BIG):
            recv_a, p32, p16, r1, qme, qs2, r2 = wk(t)
            n1, n2, s1, s2, sd = geo[t]
            rcopy(t, 1, one(r1, 0), one(r1, 0), sib).wait_recv()
            rcopy(t, 2, one(r1, 1), one(r1, 1), sib).wait_recv()
            _pipe(lambda a, b: (a + b.astype(F32),), [one(p32, s2), one(r1, 1)], [qs2], BIG_TR[n])
            start(rcopy(t, 3, qs2, r2, (*n2, cc)))
            _pipe(lambda a, b: (a + b.astype(F32),), [one(p32, me), one(r1, 0)], [qme], BIG_TR[n])
        for t, n in enumerate(BIG):
            recv_a, p32, p16, r1, qme, qs2, r2 = wk(t)
            rcopy(t, 3, r2, r2, sib).wait_recv()
            mine = fin[t].at[pl.ds(cc * hrs[t], hrs[t]), :]
            _pipe(lambda a, b: (a + b.astype(F32),), [qme.at[0], r2.at[0]], [mine], BIG_TR[n])
            start(rcopy(t, 4, mine, mine, sib))
        for t in range(nt):
            other = fin[t].at[pl.ds((1 - cc) * hrs[t], hrs[t]), :]
            rcopy(t, 4, other, other, sib).wait_recv()
        for cp in started:
            cp.wait_send()

    outs = [jax.ShapeDtypeStruct(BIG_SHAPE[n], F32) for n in BIG]
    for n in BIG:
        r, c = BIG_SHAPE[n]
        hr = r // 2
        outs += [jax.ShapeDtypeStruct((4, hr, c), F32), jax.ShapeDtypeStruct((4, hr, c), F32),
                 jax.ShapeDtypeStruct((4, hr, c), BF16), jax.ShapeDtypeStruct((2, hr, c), BF16),
                 jax.ShapeDtypeStruct((1, hr, c), F32), jax.ShapeDtypeStruct((1, hr, c), BF16),
                 jax.ShapeDtypeStruct((1, hr, c), BF16)]
    res = pl.pallas_call(
        body, name="reduce_big", in_specs=[ANY] * nt, out_specs=[ANY] * len(outs), out_shape=outs,
        scratch_shapes=[pltpu.SemaphoreType.DMA((5 * nt,)), pltpu.SemaphoreType.DMA((5 * nt,))],
        compiler_params=pltpu.CompilerParams(vmem_limit_bytes=VMEM_LIMIT),
    )(*grads)
    return res[:nt]


WHOLE_X_FIRST = dict(w_ssd_o=True, w_out=False, w_attn_o=False)


def _quarters(names):
    out = []
    for i, n in enumerate(names):
        if n in WHOLE_X_FIRST:
            h = BIG_SHAPE[n][0] // 2
            out.append((i, WHOLE_X_FIRST[n], 0, h, 128))
        else:
            q = BIG_SHAPE[n][0] // 4
            tr = 128 if q % 128 == 0 else q
            out += [(i, True, 0, q, tr), (i, False, q, q, tr)]
    return out


class _GatherJob:
    def __init__(self, names, shards, at=None):
        self.names = names
        self.at = at
        self.inputs = list(shards)
        self.out_shapes = [jax.ShapeDtypeStruct((N_CHIPS, *BIG_SHAPE[n]), BF16) for n in names]
        self.ent = _quarters(names)
        self.scratch = [pltpu.SemaphoreType.DMA((6 * len(self.ent),)), pltpu.SemaphoreType.DMA((6 * len(self.ent),))]

    def phases(self, sh, out, scr):
        send_sems, recv_sems = scr
        names, ent = self.names, self.ent
        x, y, cc = lax.axis_index("x"), lax.axis_index("y"), lax.axis_index("c")
        me = 2 * x + y
        sib = (x, y, 1 - cc)
        geo = [_neighbours(x, y, e[1]) for e in ent]
        started = []

        def copy(i, k, slot, pc, to):
            arr, _, roff, rows, _ = ent[i]
            hr = BIG_SHAPE[names[arr]][0] // 2
            ref = out[arr].at[slot, pl.ds(pc * hr + roff, rows), :]
            return pltpu.make_async_remote_copy(src_ref=ref, dst_ref=ref, send_sem=send_sems.at[6 * i + k],
                                                recv_sem=recv_sems.at[6 * i + k], device_id=to, device_id_type=MESH)

        def start(*a):
            copy(*a).start()
            started.append(a)

        def p0():
            for t, n in enumerate(names):
                _pipe(lambda v: (v,), [sh[t]], [out[t].at[me]], BIG_TR[n])
            for i in range(len(ent)):
                n1, n2, _, _, _ = geo[i]
                start(i, 0, me, cc, (*n1, cc))
                start(i, 1, me, cc, (*n2, cc))

        def p1():
            for i in range(len(ent)):
                n1, n2, s1, s2, sd = geo[i]
                copy(i, 0, s1, cc, sib).wait_recv()
                start(i, 2, s1, cc, (*n2, cc))
                start(i, 3, s1, cc, sib)
                copy(i, 1, s2, cc, sib).wait_recv()
                start(i, 4, s2, cc, sib)

        def p2():
            for i in range(len(ent)):
                sd = geo[i][4]
                copy(i, 2, sd, cc, sib).wait_recv()
                start(i, 5, sd, cc, sib)

        def p3():
            for i in range(len(ent)):
                _, _, s1, s2, sd = geo[i]
                copy(i, 3, s1, 1 - cc, sib).wait_recv()
                copy(i, 4, s2, 1 - cc, sib).wait_recv()
                copy(i, 5, sd, 1 - cc, sib).wait_recv()
            for a in started:
                copy(*a).wait_send()

        return [p0, p1, p2, p3]


class _ReduceJob:
    NW = 7

    def __init__(self, names, grads, at=None):
        self.names = names
        self.at = at
        self.inputs = list(grads)
        self.ent = _quarters(names)
        self.out_shapes = [jax.ShapeDtypeStruct(BIG_SHAPE[n], F32) for n in names]
        for arr, _, _, rows, _ in self.ent:
            c = BIG_SHAPE[names[arr]][1]
            self.out_shapes += [jax.ShapeDtypeStruct((4, rows, c), F32), jax.ShapeDtypeStruct((4, rows, c), F32),
                                jax.ShapeDtypeStruct((4, rows, c), BF16), jax.ShapeDtypeStruct((2, rows, c), BF16),
                                jax.ShapeDtypeStruct((1, rows, c), F32), jax.ShapeDtypeStruct((1, rows, c), BF16),
                                jax.ShapeDtypeStruct((1, rows, c), BF16)]
        self.scratch = [pltpu.SemaphoreType.DMA((5 * len(self.ent),)), pltpu.SemaphoreType.DMA((5 * len(self.ent),))]

    def phases(self, g, outs, scr):
        send_sems, recv_sems = scr
        names, ent, nw = self.names, self.ent, self.NW
        nt = len(names)
        fin, work = outs[:nt], outs[nt:]
        x, y, cc = lax.axis_index("x"), lax.axis_index("y"), lax.axis_index("c")
        me = 2 * x + y
        sib = (x, y, 1 - cc)
        geo = [_neighbours(x, y, e[1]) for e in ent]
        started = []
        wk = lambda i: work[nw * i:nw * (i + 1)]
        one = lambda ref, slot: ref.at[pl.ds(slot, 1)]

        def rows_of(i, pc):
            arr, _, roff, rows, _ = ent[i]
            return pl.ds(pc * (BIG_SHAPE[names[arr]][0] // 2) + roff, rows)

        def rcopy(i, k, src, dst, to):
            return pltpu.make_async_remote_copy(src_ref=src, dst_ref=dst, send_sem=send_sems.at[5 * i + k],
                                                recv_sem=recv_sems.at[5 * i + k], device_id=to, device_id_type=MESH)

        def start(make):
            make().start()
            started.append(make)

        def p0():
            for i, e in enumerate(ent):
                start(lambda i=i, e=e: rcopy(i, 0, g[e[0]].at[:, rows_of(i, 1 - cc), :], wk(i)[0], sib))

        def p1():
            for i, e in enumerate(ent):
                recv_a, p32, p16, r1 = wk(i)[:4]
                n1, n2, s1, s2, sd = geo[i]
                rcopy(i, 0, recv_a, recv_a, sib).wait_recv()
                _pipe(lambda a, b: (a + b, a + b), [g[e[0]].at[:, rows_of(i, cc), :], recv_a], [p32, p16], e[4])
                start(lambda i=i, s1=s1, n1=n1: rcopy(i, 1, one(wk(i)[2], s1), one(wk(i)[3], 0), (*n1, cc)))
                start(lambda i=i, sd=sd, n1=n1: rcopy(i, 2, one(wk(i)[2], sd), one(wk(i)[3], 1), (*n1, cc)))

        def p2():
            for i, e in enumerate(ent):
                _, p32, _, r1, qme, qs2, r2 = wk(i)
                n1, n2, s1, s2, sd = geo[i]
                rcopy(i, 1, one(r1, 0), one(r1, 0), sib).wait_recv()
                rcopy(i, 2, one(r1, 1), one(r1, 1), sib).wait_recv()
                _pipe(lambda a, b, c, d: (a + b.astype(F32), c + d.astype(F32)),
                      [one(p32, s2), one(r1, 1), one(p32, me), one(r1, 0)], [qs2, qme], e[4])
                start(lambda i=i, n2=n2: rcopy(i, 3, wk(i)[5], wk(i)[6], (*n2, cc)))

        def p3():
            for i, e in enumerate(ent):
                qme, r2 = wk(i)[4], wk(i)[6]
                rcopy(i, 3, r2, r2, sib).wait_recv()
                mine = fin[e[0]].at[rows_of(i, cc), :]
                _pipe(lambda a, b: (a + b.astype(F32),), [qme.at[0], r2.at[0]], [mine], e[4])
                start(lambda i=i, e=e: rcopy(i, 4, fin[e[0]].at[rows_of(i, cc), :], fin[e[0]].at[rows_of(i, cc), :], sib))

        def p4():
            for i, e in enumerate(ent):
                other = fin[e[0]].at[rows_of(i, 1 - cc), :]
                rcopy(i, 4, other, other, sib).wait_recv()
            for make in started:
                make().wait_send()

        return [p0, p1, p2, p3, p4]


def _run_job(job, name):
    ni, no = len(job.inputs), len(job.out_shapes)

    def body(*refs):
        for ph in job.phases(refs[:ni], refs[ni:ni + no], refs[ni + no:]):
            ph()

    return pl.pallas_call(
        body, name=name, in_specs=[ANY] * ni, out_specs=[ANY] * no, out_shape=job.out_shapes, scratch_shapes=job.scratch,
        compiler_params=pltpu.CompilerParams(vmem_limit_bytes=VMEM_LIMIT),
    )(*job.inputs)


def _hosted(body, *, name, grid, in_specs, out_specs, out_shape, scratch_shapes, args, sem, side=None):
    if side is None:
        return pl.pallas_call(body, name=name, grid=grid, in_specs=in_specs, out_specs=out_specs, out_shape=out_shape,
                              scratch_shapes=scratch_shapes, compiler_params=_cp(sem))(*args), None
    job = side
    ni, no, ns = len(in_specs), len(out_specs), len(scratch_shapes)
    ji, jo = len(job.inputs), len(job.out_shapes)
    n_steps = 1
    for extent in grid:
        n_steps *= extent

    def wrapped(*refs):
        own_in, refs = refs[:ni], refs[ni:]
        job_in, refs = refs[:ji], refs[ji:]
        own_out, refs = refs[:no], refs[no:]
        job_out, refs = refs[:jo], refs[jo:]
        own_scr, job_scr = refs[:ns], refs[ns:]
        step = 0
        for d, extent in enumerate(grid):
            step = step * extent + pl.program_id(d)
        phases = job.phases(job_in, job_out, job_scr)
        steps = [min(int(f * n_steps), n_steps - 1) for f in job.at] + [n_steps - 1]
        assert len(steps) == len(phases) and steps == sorted(steps)
        for at, ph in zip(steps, phases):
            pl.when(step == at)(ph)
        body(*own_in, *own_out, *own_scr)

    res = pl.pallas_call(
        wrapped, name=name, grid=grid, in_specs=list(in_specs) + [ANY] * ji, out_specs=list(out_specs) + [ANY] * jo,
        out_shape=list(out_shape) + list(job.out_shapes), scratch_shapes=list(scratch_shapes) + list(job.scratch),
        compiler_params=_cp(("arbitrary",) * len(grid)),
    )(*args, *job.inputs)
    return res[:no], res[no:]


def _proj_dw(xn, dproj_sh, *, tm=512, tk=2048):
    s, d = xn.shape
    tk = _tile(s, tk)
    nk = s // tk

    def body(a_ref, b_ref, o_ref, acc):
        def finish(r):
            o_ref[0] = r

        _accumulate(acc, _dot_tn(a_ref[...], b_ref[...]), pl.program_id(2), nk, finish)

    return pl.pallas_call(
        body, name="proj_dw", grid=(N_CHIPS, d // tm, nk),
        in_specs=[pl.BlockSpec((tk, tm), lambda j, i, q: (q, i)), pl.BlockSpec((tk, W_IN_PAD), lambda j, i, q: (q, j))],
        out_specs=pl.BlockSpec((1, tm, W_IN_PAD), lambda j, i, q: (j, i, 0)),
        out_shape=jax.ShapeDtypeStruct((N_CHIPS, d, W_IN_PAD), F32), scratch_shapes=[pltpu.VMEM((tm, W_IN_PAD), F32)],
        compiler_params=_cp(("parallel", "parallel", "arbitrary")),
    )(xn, dproj_sh)


def _proj_dx(dproj_sh, w_sh, *, tm=1024, side=None):
    s = dproj_sh.shape[0]
    d = w_sh.shape[1]
    tm = _tile(s, tm)

    def body(a_ref, b_ref, o_ref, acc):
        kk = pl.program_id(1)
        part = _dot_nt(a_ref[...], b_ref[0])

        @pl.when(kk == 0)
        def _():
            acc[...] = part

        @pl.when(kk > 0)
        def _():
            acc[...] += part

        @pl.when(kk == N_CHIPS - 1)
        def _():
            o_ref[...] = acc[...]

    own, extra = _hosted(
        body, name="proj_dx", grid=(s // tm, N_CHIPS),
        in_specs=[pl.BlockSpec((tm, W_IN_PAD), lambda i, q: (i, q)), pl.BlockSpec((1, d, W_IN_PAD), lambda i, q: (q, 0, 0))],
        out_specs=[pl.BlockSpec((tm, d), lambda i, q: (i, 0))],
        out_shape=[jax.ShapeDtypeStruct((s, d), F32)], scratch_shapes=[pltpu.VMEM((tm, d), F32)],
        args=(dproj_sh, w_sh), sem=("parallel", "arbitrary"), side=side)
    return own[0] if side is None else (own[0], extra)


def _up_dx(dup, w_sh, *, tm=1024):
    s = dup.shape[1]
    d, wsh = w_sh.shape[1:]
    tm = _tile(s, tm)

    def body(a_ref, b_ref, o_ref, acc):
        kk = pl.program_id(1)
        part = _dot_nt(a_ref[0], b_ref[0])

        @pl.when(kk == 0)
        def _():
            acc[...] = part

        @pl.when(kk > 0)
        def _():
            acc[...] += part

        @pl.when(kk == N_CHIPS - 1)
        def _():
            o_ref[...] = acc[...]

    return pl.pallas_call(
        body, name="up_dx", grid=(s // tm, N_CHIPS),
        in_specs=[pl.BlockSpec((1, tm, wsh), lambda i, q: (q >> 1, i, q & 1)), pl.BlockSpec((1, d, wsh), lambda i, q: (q, 0, 0))],
        out_specs=pl.BlockSpec((tm, d), lambda i, q: (i, 0)),
        out_shape=jax.ShapeDtypeStruct((s, d), F32), scratch_shapes=[pltpu.VMEM((tm, d), F32)],
        compiler_params=_cp(("parallel", "arbitrary")),
    )(dup, w_sh)


def _up_dw(hn, dup, *, tk=2048):
    s, d = hn.shape
    wsh = 2 * D_FF // N_CHIPS
    tk = _tile(s, tk)
    nk = s // tk

    def body(a_ref, b_ref, o_ref, acc):
        def finish(r):
            o_ref[0] = r

        _accumulate(acc, _dot_tn(a_ref[...], b_ref[0]), pl.program_id(1), nk, finish)

    return pl.pallas_call(
        body, name="up_dw", grid=(N_CHIPS, nk),
        in_specs=[pl.BlockSpec((tk, d), lambda j, q: (q, 0)), pl.BlockSpec((1, tk, wsh), lambda j, q: (j >> 1, q, j & 1))],
        out_specs=pl.BlockSpec((1, d, wsh), lambda j, q: (j, 0, 0)),
        out_shape=jax.ShapeDtypeStruct((N_CHIPS, d, wsh), F32), scratch_shapes=[pltpu.VMEM((d, wsh), F32)],
        compiler_params=_cp(("parallel", "arbitrary")),
    )(hn, dup)


BIG_ROWS =(IN_DIM // 4, Q_DIM // 4, D_INNER // 4, D_MODEL // 4, 2 * D_FF // 4, D_FF // 4)
PACK_ROWS = 5376


def _pack_shards(parts):
    rows = [p.reshape(-1, D_MODEL) for p in parts]
    pad = PACK_ROWS - sum(BIG_ROWS)
    return jnp.concatenate(rows + [jnp.zeros((pad, D_MODEL), rows[0].dtype)], axis=0)


def _unpack_shards(buf):
    out, off = [], 0
    for n in BIG_ROWS:
        out.append(buf[off:off + n])
        off += n
    return out


def _assemble(srcs, col_map, *, name, tr=256):
    arrays, lead = [], []
    for src in srcs:
        arr, j = src if isinstance(src, tuple) else (src, None)
        if not any(arr is a for a in arrays):
            arrays.append(arr)
        lead.append(([i for i, a in enumerate(arrays) if a is arr][0], j))
    rows = arrays[0].shape[-2]
    tr = _tile(rows, tr)
    out_w = len(col_map)
    tiles = []
    for t in range(out_w // 128):
        runs = []
        for lane in range(128):
            ent = col_map[t * 128 + lane]
            key = None if ent is None else (ent[0], ent[1] // 128, (lane - ent[1]) % 128)
            if runs and runs[-1][0] == key:
                runs[-1][2] = lane + 1
            else:
                runs.append([key, lane, lane + 1])
        tiles.append(runs)

    def body(*refs):
        o_ref = refs[-1]
        lane = lax.broadcasted_iota(jnp.int32, (tr, 128), 1)
        for t, runs in enumerate(tiles):
            acc = jnp.zeros((tr, 128), F32)
            for key, a, b in runs:
                if key is None:
                    continue
                sid, ct, shift = key
                ai, j = lead[sid]
                cols = slice(ct * 128, (ct + 1) * 128)
                piece = (refs[ai][:, cols] if j is None else refs[ai][j, :, cols]).astype(F32)
                if shift:
                    piece = pltpu.roll(piece, shift, 1)
                acc = piece if (a, b) == (0, 128) else jnp.where((lane >= a) & (lane < b), piece, acc)
            o_ref[:, t * 128:(t + 1) * 128] = acc.astype(BF16)

    specs = [pl.BlockSpec((tr, a.shape[1]), lambda i: (i, 0)) if a.ndim == 2
             else pl.BlockSpec((a.shape[0], tr, a.shape[2]), lambda i: (0, i, 0)) for a in arrays]
    return pl.pallas_call(
        body, name=name, grid=(rows // tr,), in_specs=specs, out_specs=pl.BlockSpec((tr, out_w), lambda i: (i, 0)),
        out_shape=jax.ShapeDtypeStruct((rows, out_w), BF16), compiler_params=_cp(("parallel",)),
    )(*arrays)


def _permute_cols_in(w):
    pad = jnp.zeros((w.shape[0], PW - IN_DIM), w.dtype)
    return jnp.concatenate([w[:, :6656], w[:, 6688:], w[:, 6656:6688], pad], axis=1)


def _unpermute_cols_in(g):
    return jnp.concatenate([g[:, :6656], g[:, O_DT:O_DT + 32], g[:, 6656:O_DT]], axis=1)


SMALL = ("norm1_w", "b_gate", "attn_sinks", "ssd_conv_b", "dt_bias", "a_log", "d_skip", "ssd_norm_w", "norm2_w",
         "ffn_conv_b", "final_norm_w", "ssd_conv_w", "ffn_conv_w")


def _pad128(v):
    v = v.reshape(-1)
    return jnp.pad(v, (0, (-v.shape[0]) % 128))


def _pack_small(parts):
    flat = jnp.concatenate([_pad128(p) for p in parts])
    flat = jnp.pad(flat, (0, (-flat.shape[0]) % 1024))
    return flat.reshape(-1, 128)


def _unpack_small(buf, shapes):
    flat, out, off = buf.reshape(-1), [], 0
    for shp in shapes:
        n = 1
        for q in shp:
            n *= q
        out.append(flat[off:off + n].reshape(shp))
        off += n + (-n) % 128
    return out


def _vec128(v):
    return jnp.pad(v.reshape(1, -1), ((0, 0), (0, 128 - v.shape[-1])))


def kernel(x, norm1_w, w_in, b_gate, attn_sinks, w_attn_o, ssd_conv_w, ssd_conv_b, dt_bias, a_log, d_skip, ssd_norm_w, w_ssd_o, w_out, norm2_w, w_up, ffn_conv_w, ffn_conv_b, w_down, final_norm_w, loss_target, m_norm1_w, m_w_in, m_b_gate, m_attn_sinks, m_w_attn_o, m_ssd_conv_w, m_ssd_conv_b, m_dt_bias, m_a_log, m_d_skip, m_ssd_norm_w, m_w_ssd_o, m_w_out, m_norm2_w, m_w_up, m_ffn_conv_w, m_ffn_conv_b, m_w_down, m_final_norm_w, v_norm1_w, v_w_in, v_b_gate, v_attn_sinks, v_w_attn_o, v_ssd_conv_w, v_ssd_conv_b, v_dt_bias, v_a_log, v_d_skip, v_ssd_norm_w, v_w_ssd_o, v_w_out, v_norm2_w, v_w_up, v_ffn_conv_w, v_ffn_conv_b, v_w_down, v_final_norm_w):
    ix, iy, ic = lax.axis_index("x"), lax.axis_index("y"), lax.axis_index("c")
    chip = 2 * ix + iy
    x2 = x[0]
    tgt = loss_target[0]
    s = x2.shape[0]

    wsh = IN_DIM // N_CHIPS
    big_shards = dict(w_in=jnp.pad(w_in[0], ((0, 0), (0, W_IN_PAD - wsh))), w_attn_o=w_attn_o[0], w_ssd_o=w_ssd_o[0],
                      w_out=w_out[0], w_up=w_up[0], w_down=w_down[0])
    gathered = {}
    (gathered["w_in"],) = _run_job(_GatherJob(("w_in",), [big_shards["w_in"]]), "gather_w_in")
    early = ("w_attn_o", "w_ssd_o", "w_out")
    gather_early = _GatherJob(early, [big_shards[n] for n in early], at=(0.0, 0.5, 0.8))
    gather_up = _GatherJob(("w_up",), [big_shards["w_up"]], at=(0.0, 0.55, 0.85))
    gather_down = _GatherJob(("w_down",), [big_shards["w_down"]], at=(0.0, 0.5, 0.8))
    gw = gathered["w_in"]
    perm = list(range(O_GA)) + list(range(O_GA + N_SSD_HEADS, IN_DIM)) + list(range(O_GA, O_GA + N_SSD_HEADS))
    w_in_p = _assemble([(gw, j) for j in range(N_CHIPS)], [divmod(o, wsh) for o in perm] + [None] * (PW - IN_DIM),
                       name="w_in_assemble")
    small_sh = _pack_small([ssd_conv_w[0], ffn_conv_w[0]])
    small_all = _all_gather_small(small_sh)
    sc_parts = [_unpack_small(small_all[j], [(4, XBC_DIM // 4), (3, 2 * D_FF // 4)]) for j in range(N_CHIPS)]
    ssd_cw = jnp.concatenate([p[0] for p in sc_parts], axis=1)
    ffn_cw = jnp.concatenate([p[1] for p in sc_parts], axis=1)

    sinks128 = _vec128(attn_sinks)
    dtb128, alog128, dskip128 = _vec128(dt_bias), _vec128(a_log), _vec128(d_skip)

    xn, xnt = _rms_fwd(x2, norm1_w, name="norm1_fwd", with_t=True)
    proj, got = _mm(xn, w_in_p, name="proj_fwd", tn=1280, side=gather_early)
    gathered.update(zip(early, got))
    qkvt = _mm(w_in_p[:, :O_Z], xnt, name="qkv_fwd", ta=True)
    attn_pre, (gathered["w_up"],) = _attn_fwd(qkvt, sinks128, side=gather_up)
    xbc = _ssd_conv_fwd(proj, ssd_cw, ssd_conv_b)
    (y_ssd, hprev), (gathered["w_down"],) = _ssd_fwd(xbc, proj, dtb128, alog128, dskip128, side=gather_down)
    full = {n: gathered[n].reshape(-1, D_MODEL) for n in ("w_attn_o", "w_ssd_o", "w_out", "w_down")}
    full["w_up"] = gathered["w_up"]
    attn = _mm(attn_pre, full["w_attn_o"], name="attn_o_fwd", ta=True)
    yn = _gate_norm_fwd(y_ssd, proj, ssd_norm_w)
    ssd_out = _mm(yn, full["w_ssd_o"], name="ssd_o_fwd")
    merged = _merge_fwd(proj, b_gate, attn, ssd_out)
    h1 = _mm(merged, full["w_out"], name="out_fwd", resid=x2)
    hn = _rms_fwd(h1, norm2_w, name="norm2_fwd")
    up = _mm(hn, full["w_up"], name="up_fwd")
    act = _ffn_act_fwd(up, ffn_cw, ffn_conv_b)
    h2 = _mm(act, full["w_down"], name="down_fwd", resid=h1, tk=1408)

    dh2, loss_blk, g_final = _loss_bwd(h2, tgt, final_norm_w.reshape(1, -1))
    dact = _mm(dh2, full["w_down"], name="down_dx", tb=True, tn=1408)
    g_down = _mm(act, dh2, name="down_dw", ta=True, tm=1408)
    dup, g_ffn_cw, g_ffn_cb = _ffn_act_bwd(dact, up, ffn_cw, ffn_conv_b)
    dhn = _up_dx(dup, full["w_up"])
    g_up = _up_dw(hn, dup)
    dh1, g_norm2 = _rms_bwd(dhn, h1, norm2_w, dh2, name="norm2_bwd")
    dmerged = _mm(dh1, full["w_out"], name="out_dx", tb=True)
    g_out = _mm(merged, dh1, name="out_dw", ta=True)
    dattn, dssd_out, dga, dgs, g_ba, g_bs = _merge_bwd(dmerged, proj, b_gate, attn, ssd_out)
    dyn = _mm(dssd_out, full["w_ssd_o"], name="ssd_o_dx", tb=True)
    g_ssd_o = _mm(yn, dssd_out, name="ssd_o_dw", ta=True)
    dy_ssd, dz, g_ssd_norm = _gate_norm_bwd(dyn, y_ssd, proj, ssd_norm_w)
    slot = lambda g: g.reshape(N_CHIPS, -1, D_MODEL)
    big_grads = {}
    red = ("w_down", "w_up")
    (dxbc, ddt, dvec), got = _ssd_bwd(xbc, proj, dtb128, alog128, dskip128, hprev, dy_ssd,
                                      side=_ReduceJob(red, [slot(g_down), g_up], at=(0.0, 0.3, 0.8, 0.95)))
    big_grads.update(zip(red, got))
    dxbc_raw, g_ssd_cw, g_ssd_cb = _ssd_conv_bwd(dxbc, proj, ssd_cw, ssd_conv_b)
    dattn_pre = _mm(full["w_attn_o"], dattn, name="attn_o_dx", tb=True)
    g_attn_o = _mm(attn_pre, dattn, name="attn_o_dw")
    red = ("w_out", "w_ssd_o", "w_attn_o")
    (dq, dk, dv, dsk), got = _attn_bwd(qkvt, sinks128, attn_pre, dattn_pre,
                                       side=_ReduceJob(red, [slot(g_out), slot(g_ssd_o), slot(g_attn_o)],
                                                       at=(0.0, 0.2, 0.5, 0.7)))
    big_grads.update(zip(red, got))
    pieces = [(dq.T, Q_DIM), (dk.T, KV_DIM), (dv.T, KV_DIM), (dz, D_INNER), (dxbc_raw, XBC_DIM), (ddt, N_SSD_HEADS),
              (dga, D_MODEL), (dgs, D_MODEL)]
    orig = [(i, c) for i, (_, w) in enumerate(pieces) for c in range(w)]
    dproj_sh = _assemble([p for p, _ in pieces],
                         [orig[j * wsh + c] if c < wsh else None for j in range(N_CHIPS) for c in range(W_IN_PAD)],
                         name="dproj_assemble")
    g_in = _proj_dw(xn, dproj_sh)
    dxn, got = _proj_dx(dproj_sh, gathered["w_in"], side=_ReduceJob(("w_in",), [g_in], at=(0.0, 0.3, 0.8, 0.95)))
    big_grads["w_in"] = got[0]
    dx, g_norm1 = _rms_bwd(dxn, x2, norm1_w, dh1, name="norm1_bwd")


    small_g = dict(
        norm1_w=g_norm1, b_gate=jnp.concatenate([g_ba, g_bs], axis=1), attn_sinks=dsk[0:1, :16], ssd_conv_b=g_ssd_cb,
        dt_bias=dvec[0:1, :32], a_log=dvec[1:2, :32], d_skip=dvec[2:3, :32], ssd_norm_w=g_ssd_norm, norm2_w=g_norm2,
        ffn_conv_b=jnp.concatenate([g_ffn_cb[0], g_ffn_cb[1]], axis=1), final_norm_w=g_final, ssd_conv_w=g_ssd_cw,
        ffn_conv_w=jnp.concatenate([g_ffn_cw[0], g_ffn_cw[1]], axis=1))
    small_buf = _pack_small([small_g[n] for n in SMALL] + [loss_blk])
    small_sum = _all_reduce_small(small_buf)
    small_shapes = [(1, D_MODEL), (1, 2 * D_MODEL), (1, 16), (1, XBC_DIM), (1, 32), (1, 32), (1, 32), (1, D_INNER),
                    (1, D_MODEL), (1, 2 * D_FF), (D_MODEL,), (4, XBC_DIM), (3, 2 * D_FF), (1, 128)]
    small_list = _unpack_small(small_sum, small_shapes)
    loss = small_list[-1][0, 0]
    grads = dict(zip(SMALL, small_list[:-1]))
    grads["ssd_conv_w"] = lax.dynamic_slice_in_dim(grads["ssd_conv_w"], chip * (XBC_DIM // 4), XBC_DIM // 4, axis=1)
    grads["ffn_conv_w"] = lax.dynamic_slice_in_dim(grads["ffn_conv_w"], chip * (2 * D_FF // 4), 2 * D_FF // 4, axis=1)
    grads.update(big_grads)

    weights = dict(norm1_w=norm1_w, w_in=w_in, b_gate=b_gate, attn_sinks=attn_sinks, w_attn_o=w_attn_o, ssd_conv_w=ssd_conv_w,
                   ssd_conv_b=ssd_conv_b, dt_bias=dt_bias, a_log=a_log, d_skip=d_skip, ssd_norm_w=ssd_norm_w, w_ssd_o=w_ssd_o,
                   w_out=w_out, norm2_w=norm2_w, w_up=w_up, ffn_conv_w=ffn_conv_w, ffn_conv_b=ffn_conv_b, w_down=w_down,
                   final_norm_w=final_norm_w)
    ms = dict(norm1_w=m_norm1_w, w_in=m_w_in, b_gate=m_b_gate, attn_sinks=m_attn_sinks, w_attn_o=m_w_attn_o,
              ssd_conv_w=m_ssd_conv_w, ssd_conv_b=m_ssd_conv_b, dt_bias=m_dt_bias, a_log=m_a_log, d_skip=m_d_skip,
              ssd_norm_w=m_ssd_norm_w, w_ssd_o=m_w_ssd_o, w_out=m_w_out, norm2_w=m_norm2_w, w_up=m_w_up,
              ffn_conv_w=m_ffn_conv_w, ffn_conv_b=m_ffn_conv_b, w_down=m_w_down, final_norm_w=m_final_norm_w)
    vs = dict(norm1_w=v_norm1_w, w_in=v_w_in, b_gate=v_b_gate, attn_sinks=v_attn_sinks, w_attn_o=v_w_attn_o,
              ssd_conv_w=v_ssd_conv_w, ssd_conv_b=v_ssd_conv_b, dt_bias=v_dt_bias, a_log=v_a_log, d_skip=v_d_skip,
              ssd_norm_w=v_ssd_norm_w, w_ssd_o=v_w_ssd_o, w_out=v_w_out, norm2_w=v_norm2_w, w_up=v_w_up,
              ffn_conv_w=v_ffn_conv_w, ffn_conv_b=v_ffn_conv_b, w_down=v_w_down, final_norm_w=v_final_norm_w)
    order = list(weights)
    deltas, new_m, new_v = {}, {}, {}
    for n in BIG:
        shp = weights[n].shape
        d_, m_, v_, grads[n] = _adamw(weights[n][0], grads[n], ms[n][0], vs[n][0], name="adamw_" + n)
        deltas[n], new_m[n], new_v[n] = d_.reshape(shp), m_.reshape(shp), v_.reshape(shp)
    smalls = [n for n in order if n not in BIG]
    as2d = lambda a: a.reshape(-1, a.shape[-1])
    res = _adamw_many(*[[as2d(src[n][0] if src[n].ndim == 3 else src[n]) for n in smalls] for src in (weights, grads, ms, vs)])
    for i, n in enumerate(smalls):
        deltas[n], new_m[n], new_v[n] = (res[q * len(smalls) + i].reshape(weights[n].shape) for q in range(3))
    out_grads = [grads[n].reshape(weights[n].shape) for n in order]
    return (loss, dx[None], *out_grads, *[deltas[n] for n in order], *[new_m[n] for n in order], *[new_v[n] for n in order])
```

```python
import functools

import jax
import jax.numpy as jnp
from jax import lax
from jax.experimental import pallas as pl
from jax.experimental.pallas import tpu as pltpu

F32 = jnp.float32
BF16 = jnp.bfloat16
HI = lax.Precision.HIGHEST

D_MODEL = 1024
Q_DIM = 1024
KV_DIM = 256
D_INNER = 2048
BC_DIM = 512
XBC_DIM = 3072
N_SSD_HEADS = 32
D_FF = 2816
IN_DIM = 8736
BLK = 128
EPS = 1e-5
NEG = -1e30

O_Q, O_K, O_V, O_Z, O_X, O_GA, O_GS, O_DT = 0, 1024, 1280, 1536, 3584, 6656, 7680, 8704
PW = 8960

ADAM_LR, ADAM_B1, ADAM_B2, ADAM_EPS, ADAM_WD, ADAM_STEP = 0.001, 0.9, 0.999, 1e-08, 0.01, 10

VMEM_LIMIT = 52 * 1024 * 1024
MESH = pl.DeviceIdType.MESH


def _cp(sem=None):
    return pltpu.CompilerParams(dimension_semantics=sem, vmem_limit_bytes=VMEM_LIMIT)


def _dot(a, b, prec=None):
    return jnp.dot(a, b, preferred_element_type=F32, precision=prec)


def _dot_nt(a, b, prec=None):
    return lax.dot_general(a, b, (((1,), (1,)), ((), ())), preferred_element_type=F32, precision=prec)


def _dot_tn(a, b, prec=None):
    return lax.dot_general(a, b, (((0,), (0,)), ((), ())), preferred_element_type=F32, precision=prec)


def _sigmoid(x):
    return 0.5 * jnp.tanh(0.5 * x) + 0.5


def _tile(n, want):
    t = min(n, want)
    while n % t:
        t -= 128
    return t


def _accumulate(acc, part, kk, nk, finish):
    if nk == 1:
        finish(part)
        return

    @pl.when(kk == 0)
    def _():
        acc[...] = part

    @pl.when(kk > 0)
    def _():
        acc[...] += part

    @pl.when(kk == nk - 1)
    def _():
        finish(acc[...])


def _mm(a, b, *, name, ta=False, tb=False, out_dtype=F32, resid=None, tm=1024, tn=1024, tk=1024, side=None):
    m, k = (a.shape[1], a.shape[0]) if ta else a.shape
    slots = b.ndim == 3
    if slots:
        n = b.shape[1] if tb else b.shape[0] * b.shape[2]
        tn, tk = (tn, b.shape[2]) if tb else (b.shape[2], tk)
    else:
        n = b.shape[0] if tb else b.shape[1]
    tm, tn, tk = _tile(m, tm), _tile(n, tn), _tile(k, tk)
    nk = k // tk
    dn = (((0 if ta else 1,), (1 if tb else 0,)), ((), ()))

    def body(*refs):
        if resid is None:
            a_ref, b_ref, o_ref, acc = refs
        else:
            a_ref, b_ref, r_ref, o_ref, acc = refs
        kk = pl.program_id(2)
        bv = b_ref[0] if slots else b_ref[...]
        part = lax.dot_general(a_ref[...].astype(BF16), bv.astype(BF16), dn, preferred_element_type=F32)

        def finish(r):
            if resid is not None:
                r = r + r_ref[...]
            o_ref[...] = r.astype(out_dtype)

        _accumulate(acc, part, kk, nk, finish)

    a_spec = pl.BlockSpec((tk, tm), lambda i, j, q: (q, i)) if ta else pl.BlockSpec((tm, tk), lambda i, j, q: (i, q))
    if slots:
        b_spec = (pl.BlockSpec((1, tn, tk), lambda i, j, q: (q, j, 0)) if tb
                  else pl.BlockSpec((1, tk, tn), lambda i, j, q: (j, q, 0)))
    else:
        b_spec = pl.BlockSpec((tn, tk), lambda i, j, q: (j, q)) if tb else pl.BlockSpec((tk, tn), lambda i, j, q: (q, j))
    o_spec = pl.BlockSpec((tm, tn), lambda i, j, q: (i, j))
    ins, specs = [a, b], [a_spec, b_spec]
    if resid is not None:
        ins.append(resid)
        specs.append(o_spec)
    own, extra = _hosted(
        body, name=name, grid=(m // tm, n // tn, nk), in_specs=specs, out_specs=[o_spec],
        out_shape=[jax.ShapeDtypeStruct((m, n), out_dtype)], scratch_shapes=[pltpu.VMEM((tm, tn), F32)],
        args=ins, sem=("parallel", "parallel", "arbitrary"), side=side)
    return own[0] if side is None else (own[0], extra)


def _rms_fwd(x, w, *, name, tm=512, with_t=False):
    s, d = x.shape
    tm = _tile(s, tm)

    def body(x_ref, w_ref, o_ref, *t_ref):
        xv = x_ref[...]
        r = lax.rsqrt(jnp.mean(xv * xv, axis=-1, keepdims=True) + EPS)
        y = (xv * r) * w_ref[...]
        o_ref[...] = y.astype(BF16)
        if with_t:
            t_ref[0][...] = y.T.astype(BF16)

    row = pl.BlockSpec((tm, d), lambda i: (i, 0))
    res = pl.pallas_call(
        body, name=name, grid=(s // tm,), in_specs=[row, pl.BlockSpec((1, d), lambda i: (0, 0))],
        out_specs=[row] + [pl.BlockSpec((d, tm), lambda i: (0, i))] * with_t,
        out_shape=[jax.ShapeDtypeStruct((s, d), BF16)] + [jax.ShapeDtypeStruct((d, s), BF16)] * with_t,
        compiler_params=_cp(("parallel",)),
    )(x, w)
    return res if with_t else res[0]


def _rms_bwd(dy, x, w, resid, *, name, tm=512):
    s, d = x.shape
    tm = _tile(s, tm)

    def body(dy_ref, x_ref, w_ref, r_ref, dx_ref, dw_ref):
        i = pl.program_id(0)
        xv = x_ref[...]
        r = lax.rsqrt(jnp.mean(xv * xv, axis=-1, keepdims=True) + EPS)
        xh = xv * r
        dyv = dy_ref[...]
        g = dyv * w_ref[...]
        dx_ref[...] = r_ref[...] + r * (g - xh * jnp.mean(g * xh, axis=-1, keepdims=True))
        part = jnp.sum(dyv * xh, axis=0, keepdims=True)

        @pl.when(i == 0)
        def _():
            dw_ref[...] = part

        @pl.when(i > 0)
        def _():
            dw_ref[...] += part

    row = pl.BlockSpec((tm, d), lambda i: (i, 0))
    vec = pl.BlockSpec((1, d), lambda i: (0, 0))
    return pl.pallas_call(
        body, name=name, grid=(s // tm,), in_specs=[row, row, vec, row], out_specs=[row, vec],
        out_shape=[jax.ShapeDtypeStruct((s, d), F32), jax.ShapeDtypeStruct((1, d), F32)],
        compiler_params=_cp(("arbitrary",)),
    )(dy, x, w, resid)


def _loss_bwd(h2, tgt, wf, *, tm=512):
    s, d = h2.shape
    tm = _tile(s, tm)

    def body(h_ref, t_ref, w_ref, dh_ref, loss_ref, dw_ref):
        i = pl.program_id(0)
        hv = h_ref[...]
        r = lax.rsqrt(jnp.mean(hv * hv, axis=-1, keepdims=True) + EPS)
        xh = hv * r
        wv = w_ref[...]
        e = xh * wv - t_ref[...]
        lpart = 0.5 * jnp.sum(jnp.mean(e * e, axis=-1, keepdims=True), axis=0, keepdims=True)
        dout = e * (1.0 / d)
        g = dout * wv
        dh_ref[...] = r * (g - xh * jnp.mean(g * xh, axis=-1, keepdims=True))
        part = jnp.sum(dout * xh, axis=0, keepdims=True)
        lrow = jnp.broadcast_to(lpart, (1, 128))

        @pl.when(i == 0)
        def _():
            dw_ref[...] = part
            loss_ref[...] = lrow

        @pl.when(i > 0)
        def _():
            dw_ref[...] += part
            loss_ref[...] += lrow

    row = pl.BlockSpec((tm, d), lambda i: (i, 0))
    vec = pl.BlockSpec((1, d), lambda i: (0, 0))
    return pl.pallas_call(
        body, name="loss_bwd", grid=(s // tm,), in_specs=[row, row, vec],
        out_specs=[row, pl.BlockSpec((1, 128), lambda i: (0, 0)), vec],
        out_shape=[jax.ShapeDtypeStruct((s, d), F32), jax.ShapeDtypeStruct((1, 128), F32),
                   jax.ShapeDtypeStruct((1, d), F32)],
        compiler_params=_cp(("arbitrary",)),
    )(h2, tgt, wf)


def _attn_mask(n):
    si = lax.broadcasted_iota(jnp.int32, (2 * BLK, 4 * BLK), 0)
    qi = lax.broadcasted_iota(jnp.int32, (2 * BLK, 4 * BLK), 1) & (BLK - 1)
    dist = BLK + qi - si
    kpos = n * BLK - BLK + si
    return (dist >= 0) & (dist < BLK) & (kpos >= 0)


def _attn_probs(q_ref, kc_ref, kp_ref, sk_ref, kvh, valid):
    rows = slice(kvh * 64, (kvh + 1) * 64)
    kt = jnp.concatenate([kp_ref[rows, :], kc_ref[rows, :]], axis=1).astype(BF16)
    qt = jnp.concatenate([q_ref[(kvh * 4 + g) * 64:(kvh * 4 + g + 1) * 64, :] for g in range(4)], axis=1).astype(BF16)
    s = _dot_tn(kt, qt) * 0.125
    s = jnp.where(valid, s, NEG)
    head = lax.broadcasted_iota(jnp.int32, (1, 4 * BLK), 1) >> 7
    sink = jnp.zeros((1, 4 * BLK), F32)
    for g in range(4):
        sink = jnp.where(head == g, sk_ref[0:1, kvh * 4 + g:kvh * 4 + g + 1], sink)
    m = jnp.maximum(jnp.max(s, axis=0, keepdims=True), sink)
    p = jnp.where(valid, jnp.exp(s - m), 0.0)
    es = jnp.exp(sink - m)
    inv = 1.0 / (jnp.sum(p, axis=0, keepdims=True) + es)
    return qt, kt, p * inv, es * inv


def _attn_in_specs(cur, prev):
    return [pl.BlockSpec((Q_DIM, BLK), lambda n: (0, cur(n))),
            pl.BlockSpec((KV_DIM, BLK), lambda n: (O_K // KV_DIM, cur(n))),
            pl.BlockSpec((KV_DIM, BLK), lambda n: (O_K // KV_DIM, prev(n))),
            pl.BlockSpec((KV_DIM, BLK), lambda n: (O_V // KV_DIM, cur(n))),
            pl.BlockSpec((KV_DIM, BLK), lambda n: (O_V // KV_DIM, prev(n))),
            pl.BlockSpec((1, 128), lambda n: (0, 0))]


def _attn_fwd(qkvt, sinks, side=None):
    s = qkvt.shape[1]
    nb = s // BLK

    def body(q_ref, kc_ref, kp_ref, vc_ref, vp_ref, sk_ref, o_ref):
        valid = _attn_mask(pl.program_id(0))
        for kvh in range(4):
            rows = slice(kvh * 64, (kvh + 1) * 64)
            _, _, probs, _ = _attn_probs(q_ref, kc_ref, kp_ref, sk_ref, kvh, valid)
            vt = jnp.concatenate([vp_ref[rows, :], vc_ref[rows, :]], axis=1).astype(BF16)
            o = _dot(vt, probs.astype(BF16))
            for g in range(4):
                h = kvh * 4 + g
                o_ref[h * 64:(h + 1) * 64, :] = o[:, g * BLK:(g + 1) * BLK].astype(BF16)

    own, extra = _hosted(
        body, name="attn_fwd", grid=(nb,), in_specs=_attn_in_specs(lambda n: n, lambda n: jnp.maximum(n - 1, 0)),
        out_specs=[pl.BlockSpec((Q_DIM, BLK), lambda n: (0, n))],
        out_shape=[jax.ShapeDtypeStruct((Q_DIM, s), BF16)], scratch_shapes=[],
        args=(qkvt, qkvt, qkvt, qkvt, qkvt, sinks), sem=("parallel",), side=side)
    return own[0] if side is None else (own[0], extra)


def _attn_bwd(qkvt, sinks, o, do, side=None):
    s = qkvt.shape[1]
    nb = s // BLK

    def body(q_ref, kc_ref, kp_ref, vc_ref, vp_ref, sk_ref, o_ref, do_ref, dq_ref, dk_ref, dv_ref, dsk_ref, ck, cv, nk, nv):
        n = pl.program_id(0)

        @pl.when(n == 0)
        def _():
            ck[...] = jnp.zeros_like(ck)
            cv[...] = jnp.zeros_like(cv)
            dsk_ref[...] = jnp.zeros_like(dsk_ref)

        @pl.when(n < nb)
        def _():
            valid = _attn_mask(n)
            lane = lax.broadcasted_iota(jnp.int32, (1, 128), 1)
            dsk = jnp.zeros((1, 128), F32)
            for kvh in range(4):
                rows = slice(kvh * 64, (kvh + 1) * 64)
                qt, kt, probs, psink = _attn_probs(q_ref, kc_ref, kp_ref, sk_ref, kvh, valid)
                vt = jnp.concatenate([vp_ref[rows, :], vc_ref[rows, :]], axis=1).astype(BF16)
                heads = [slice((kvh * 4 + g) * 64, (kvh * 4 + g + 1) * 64) for g in range(4)]
                dot = jnp.concatenate([do_ref[hh, :] for hh in heads], axis=1)
                ot = jnp.concatenate([o_ref[hh, :] for hh in heads], axis=1).astype(F32)
                delta = jnp.sum(dot * ot, axis=0, keepdims=True)
                dot16 = dot.astype(BF16)
                dp = _dot_tn(vt, dot16)
                ds = (probs * (dp - delta) * 0.125).astype(BF16)
                dqt = _dot(kt, ds)
                nk[rows, :] = _dot_nt(qt, ds)
                nv[rows, :] = _dot_nt(dot16, probs.astype(BF16))
                sd = psink * delta
                for g in range(4):
                    dq_ref[heads[g], :] = dqt[:, g * BLK:(g + 1) * BLK].astype(BF16)
                    val = -jnp.sum(sd[:, g * BLK:(g + 1) * BLK], axis=1, keepdims=True)
                    dsk = dsk + jnp.where(lane == kvh * 4 + g, val, 0.0)
            dsk_ref[0:1, :] += dsk
            dk_ref[...] = (ck[...] + nk[:, :BLK]).astype(BF16)
            dv_ref[...] = (cv[...] + nv[:, :BLK]).astype(BF16)
            ck[...] = nk[:, BLK:]
            cv[...] = nv[:, BLK:]

        @pl.when(n == nb)
        def _():
            dk_ref[...] = ck[...].astype(BF16)
            dv_ref[...] = cv[...].astype(BF16)

    cur = lambda n: jnp.minimum(n, nb - 1)
    prev = lambda n: jnp.maximum(jnp.minimum(n, nb - 1) - 1, 0)
    outb = lambda n: jnp.maximum(n - 1, 0)
    own, extra = _hosted(
        body, name="attn_bwd", grid=(nb + 1,),
        in_specs=_attn_in_specs(cur, prev) + [pl.BlockSpec((Q_DIM, BLK), lambda n: (0, cur(n))),
                                              pl.BlockSpec((Q_DIM, BLK), lambda n: (0, cur(n)))],
        out_specs=[pl.BlockSpec((Q_DIM, BLK), lambda n: (0, cur(n))),
                   pl.BlockSpec((KV_DIM, BLK), lambda n: (0, outb(n))),
                   pl.BlockSpec((KV_DIM, BLK), lambda n: (0, outb(n))),
                   pl.BlockSpec((8, 128), lambda n: (0, 0))],
        out_shape=[jax.ShapeDtypeStruct((Q_DIM, s), BF16), jax.ShapeDtypeStruct((KV_DIM, s), BF16),
                   jax.ShapeDtypeStruct((KV_DIM, s), BF16), jax.ShapeDtypeStruct((8, 128), F32)],
        scratch_shapes=[pltpu.VMEM((KV_DIM, BLK), F32)] * 2 + [pltpu.VMEM((KV_DIM, 2 * BLK), F32)] * 2,
        args=(qkvt, qkvt, qkvt, qkvt, qkvt, sinks, o, do), sem=("arbitrary",), side=side)
    return own if side is None else (own, extra)


def _shift_down(x, j):
    if j == 0:
        return x
    row = lax.broadcasted_iota(jnp.int32, x.shape, 0)
    return jnp.where(row >= j, pltpu.roll(x, j, 0), 0.0)


def _shift_up(x, j):
    if j == 0:
        return x
    s = x.shape[0]
    row = lax.broadcasted_iota(jnp.int32, x.shape, 0)
    return jnp.where(row < s - j, pltpu.roll(x, s - j, 0), 0.0)


def _conv(x, w_ref, b_ref):
    kk = w_ref.shape[0]
    y = _shift_down(x, kk - 1) * w_ref[0:1, :]
    for q in range(1, kk):
        y = y + _shift_down(x, kk - 1 - q) * w_ref[q:q + 1, :]
    return y + b_ref[...]


def _conv_bwd(dy, x, w_ref, dx_dtype):
    kk = w_ref.shape[0]
    dx = _shift_up(dy, kk - 1) * w_ref[0:1, :]
    dws = [jnp.sum(dy * _shift_down(x, kk - 1), axis=0, keepdims=True)]
    for q in range(1, kk):
        dx = dx + _shift_up(dy, kk - 1 - q) * w_ref[q:q + 1, :]
        dws.append(jnp.sum(dy * _shift_down(x, kk - 1 - q), axis=0, keepdims=True))
    return dx.astype(dx_dtype), dws, jnp.sum(dy, axis=0, keepdims=True)


def _dsilu(y, sg):
    return sg * (1.0 + y * (1.0 - sg))


CT = 256


def _ssd_conv_fwd(proj, w, b):
    s = proj.shape[0]

    def body(x_ref, w_ref, b_ref, o_ref):
        y = _conv(x_ref[...], w_ref, b_ref)
        o_ref[...] = y * _sigmoid(y)

    return pl.pallas_call(
        body, name="ssd_conv_fwd", grid=(XBC_DIM // CT,),
        in_specs=[pl.BlockSpec((s, CT), lambda i: (0, O_X // CT + i)), pl.BlockSpec((4, CT), lambda i: (0, i)),
                  pl.BlockSpec((1, CT), lambda i: (0, i))],
        out_specs=pl.BlockSpec((s, CT), lambda i: (0, i)),
        out_shape=jax.ShapeDtypeStruct((s, XBC_DIM), F32), compiler_params=_cp(("parallel",)),
    )(proj, w, b)


def _ssd_conv_bwd(dact, proj, w, b):
    s = proj.shape[0]

    def body(d_ref, x_ref, w_ref, b_ref, dx_ref, dw_ref, db_ref):
        x = x_ref[...]
        y = _conv(x, w_ref, b_ref)
        dy = d_ref[...] * _dsilu(y, _sigmoid(y))
        dx, dws, db = _conv_bwd(dy, x, w_ref, BF16)
        dx_ref[...] = dx
        for q in range(4):
            dw_ref[q:q + 1, :] = dws[q]
        db_ref[...] = db

    return pl.pallas_call(
        body, name="ssd_conv_bwd", grid=(XBC_DIM // CT,),
        in_specs=[pl.BlockSpec((s, CT), lambda i: (0, i)), pl.BlockSpec((s, CT), lambda i: (0, O_X // CT + i)),
                  pl.BlockSpec((4, CT), lambda i: (0, i)), pl.BlockSpec((1, CT), lambda i: (0, i))],
        out_specs=[pl.BlockSpec((s, CT), lambda i: (0, i)), pl.BlockSpec((4, CT), lambda i: (0, i)),
                   pl.BlockSpec((1, CT), lambda i: (0, i))],
        out_shape=[jax.ShapeDtypeStruct((s, XBC_DIM), BF16), jax.ShapeDtypeStruct((4, XBC_DIM), F32),
                   jax.ShapeDtypeStruct((1, XBC_DIM), F32)],
        compiler_params=_cp(("parallel",)),
    )(dact, proj, w, b)


NFT = D_FF // CT


def _ffn_act_fwd(up, w, b):
    s = up.shape[0]

    def body(v_ref, g_ref, wv_ref, wg_ref, bv_ref, bg_ref, o_ref):
        val = _conv(v_ref[...], wv_ref, bv_ref)
        gt = _conv(g_ref[...], wg_ref, bg_ref)
        o_ref[...] = ((gt * _sigmoid(gt)) * val).astype(BF16)

    col = lambda off: (lambda i: (0, off + i))
    return pl.pallas_call(
        body, name="ffn_act_fwd", grid=(NFT,),
        in_specs=[pl.BlockSpec((s, CT), col(0)), pl.BlockSpec((s, CT), col(NFT)),
                  pl.BlockSpec((3, CT), col(0)), pl.BlockSpec((3, CT), col(NFT)),
                  pl.BlockSpec((1, CT), col(0)), pl.BlockSpec((1, CT), col(NFT))],
        out_specs=pl.BlockSpec((s, CT), col(0)),
        out_shape=jax.ShapeDtypeStruct((s, D_FF), BF16), compiler_params=_cp(("parallel",)),
    )(up, up, w, w, b, b)


def _ffn_act_bwd(dact, up, w, b):
    s = up.shape[0]

    def body(d_ref, v_ref, g_ref, wv_ref, wg_ref, bv_ref, bg_ref, dx_ref, dw_ref, db_ref):
        xv, xg = v_ref[...], g_ref[...]
        val = _conv(xv, wv_ref, bv_ref)
        gt = _conv(xg, wg_ref, bg_ref)
        sg = _sigmoid(gt)
        d = d_ref[...]
        for half, (dy, x, w_ref) in enumerate(((d * (gt * sg), xv, wv_ref), (d * val * _dsilu(gt, sg), xg, wg_ref))):
            dx, dws, db = _conv_bwd(dy, x, w_ref, BF16)
            dx_ref[half] = dx
            for q in range(3):
                dw_ref[half, q:q + 1, :] = dws[q]
            db_ref[half] = db

    col = lambda off: (lambda i: (0, off + i))
    both = lambda i: (0, 0, i)
    return pl.pallas_call(
        body, name="ffn_act_bwd", grid=(NFT,),
        in_specs=[pl.BlockSpec((s, CT), col(0)), pl.BlockSpec((s, CT), col(0)), pl.BlockSpec((s, CT), col(NFT)),
                  pl.BlockSpec((3, CT), col(0)), pl.BlockSpec((3, CT), col(NFT)),
                  pl.BlockSpec((1, CT), col(0)), pl.BlockSpec((1, CT), col(NFT))],
        out_specs=[pl.BlockSpec((2, s, CT), both), pl.BlockSpec((2, 3, CT), both), pl.BlockSpec((2, 1, CT), both)],
        out_shape=[jax.ShapeDtypeStruct((2, s, D_FF), BF16), jax.ShapeDtypeStruct((2, 3, D_FF), F32),
                   jax.ShapeDtypeStruct((2, 1, D_FF), F32)],
        compiler_params=_cp(("parallel",)),
    )(dact, up, up, w, w, b, b)


def _expand_mat():
    r = lax.broadcasted_iota(jnp.int32, (128, D_INNER), 0)
    c = lax.broadcasted_iota(jnp.int32, (128, D_INNER), 1)
    return ((c >> 6) == r).astype(BF16)


def _reduce_mat():
    r = lax.broadcasted_iota(jnp.int32, (D_INNER, 128), 0)
    c = lax.broadcasted_iota(jnp.int32, (D_INNER, 128), 1)
    return ((r >> 6) == c).astype(BF16)


def _split(v, parts):
    out = []
    for _ in range(parts - 1):
        p = v.astype(BF16)
        out.append(p)
        v = v - p.astype(F32)
    out.append(v.astype(BF16))
    return out


def _sel_dot(v, sel, parts):
    acc = None
    for p in reversed(_split(v, parts)):
        t = _dot(p, sel)
        acc = t if acc is None else acc + t
    return acc


def _row8(v):
    return jnp.broadcast_to(v, (8, v.shape[1]))


def _tril():
    r = lax.broadcasted_iota(jnp.int32, (BLK, BLK), 0)
    c = lax.broadcasted_iota(jnp.int32, (BLK, BLK), 1)
    return r >= c


def _softplus(x):
    return jnp.maximum(x, 0.0) + jnp.log(1.0 + jnp.exp(-jnp.abs(x)))


def _ssd_common(dtraw_ref, dtb_ref, alog_ref):
    causal = _tril()
    e_mat = _expand_mat()
    a_neg = -jnp.exp(alog_ref[...])
    dt = _softplus(dtraw_ref[...] + dtb_ref[...])
    a_cs = _dot(causal.astype(F32), dt * a_neg, HI)
    a_cs_t = a_cs.T
    dt_x = _sel_dot(dt, e_mat, 3)
    acs_x = _sel_dot(a_cs, e_mat, 3)
    alast_x = acs_x[BLK - 1:BLK, :]
    ea_x = jnp.exp(acs_x)
    ds_x = jnp.exp(alast_x - acs_x)
    elast_x = jnp.exp(alast_x)
    return causal, e_mat, a_neg, dt, a_cs, a_cs_t, dt_x, ea_x, ds_x, elast_x


def _decay(a_cs, a_cs_t, h, causal):
    seg = a_cs[:, h:h + 1] - a_cs_t[h:h + 1, :]
    return jnp.where(causal, jnp.exp(jnp.where(causal, seg, 0.0)), 0.0)


def _ssd_fwd(xbc, proj, dt_bias, a_log, d_skip, side=None):
    s = xbc.shape[0]
    nc = s // BLK

    def body(xs_ref, b_ref, c_ref, dtraw_ref, dtb_ref, alog_ref, dskip_ref, y_ref, hp_ref, h_scr, xc16):
        @pl.when(pl.program_id(0) == 0)
        def _():
            h_scr[...] = jnp.zeros_like(h_scr)

        causal, e_mat, _, _, a_cs, a_cs_t, dt_x, ea_x, ds_x, elast_x = _ssd_common(dtraw_ref, dtb_ref, alog_ref)
        dskip_x = _sel_dot(_row8(dskip_ref[...]), e_mat, 3)[0:1]
        xs = xs_ref[...]
        xc = xs * dt_x
        xc16[...] = xc.astype(BF16)
        xcd = (xc * ds_x).astype(BF16)
        hp_ref[0] = h_scr[...]
        for g in range(4):
            gs = slice(g * 512, (g + 1) * 512)
            cg = c_ref[:, g * 128:(g + 1) * 128].astype(BF16)
            bg = b_ref[:, g * 128:(g + 1) * 128].astype(BF16)
            cb = _dot_nt(cg, bg)
            hg = h_scr[:, gs]
            yoff = _dot(cg, hg.astype(BF16)) * ea_x[:, gs]
            for j in range(8):
                h = g * 8 + j
                hsl = slice(h * 64, (h + 1) * 64)
                mm = (cb * _decay(a_cs, a_cs_t, h, causal)).astype(BF16)
                y_ref[:, hsl] = _dot(mm, xc16[:, hsl])
            y_ref[:, gs] += yoff + xs[:, gs] * dskip_x[:, gs]
            h_scr[:, gs] = hg * elast_x[:, gs] + _dot_tn(bg, xcd[:, gs])

    vec = pl.BlockSpec((1, 128), lambda c: (0, 0))
    own, extra = _hosted(
        body, name="ssd_fwd", grid=(nc,),
        in_specs=[pl.BlockSpec((BLK, D_INNER), lambda c: (c, 0)),
                  pl.BlockSpec((BLK, BC_DIM), lambda c: (c, D_INNER // BC_DIM)),
                  pl.BlockSpec((BLK, BC_DIM), lambda c: (c, D_INNER // BC_DIM + 1)),
                  pl.BlockSpec((BLK, 128), lambda c: (c, O_DT // 128)), vec, vec, vec],
        out_specs=[pl.BlockSpec((BLK, D_INNER), lambda c: (c, 0)),
                   pl.BlockSpec((1, 128, D_INNER), lambda c: (c, 0, 0))],
        out_shape=[jax.ShapeDtypeStruct((s, D_INNER), F32), jax.ShapeDtypeStruct((nc, 128, D_INNER), F32)],
        scratch_shapes=[pltpu.VMEM((128, D_INNER), F32), pltpu.VMEM((BLK, D_INNER), BF16)],
        args=(xbc, xbc, xbc, proj, dt_bias, a_log, d_skip), sem=("arbitrary",), side=side)
    return own if side is None else (own, extra)


def _ssd_bwd(xbc, proj, dt_bias, a_log, d_skip, hprev, dy, side=None):
    s = xbc.shape[0]
    nc = s // BLK

    def body(xs_ref, b_ref, c_ref, dtraw_ref, dtb_ref, alog_ref, dskip_ref, hp_ref, dy_ref,
             dxbc_ref, ddt_ref, dvec_ref, dh_scr, xc16, dy16, dxc_scr, dacs_r, tdiff):
        step = pl.program_id(0)
        dacs_r[...] = jnp.zeros_like(dacs_r)

        @pl.when(step == 0)
        def _():
            dh_scr[...] = jnp.zeros_like(dh_scr)
            dvec_ref[...] = jnp.zeros_like(dvec_ref)

        causal, e_mat, a_neg, dt, a_cs, a_cs_t, dt_x, ea_x, ds_x, elast_x = _ssd_common(dtraw_ref, dtb_ref, alog_ref)
        r_mat = _reduce_mat()
        lane = lax.broadcasted_iota(jnp.int32, (1, 128), 1)
        dskip_x = _sel_dot(_row8(dskip_ref[...]), e_mat, 3)[0:1]
        xs = xs_ref[...]
        dy = dy_ref[...]
        xc = xs * dt_x
        xcd = xc * ds_x
        xc16[...] = xc.astype(BF16)
        dy16[...] = dy.astype(BF16)
        dyea = dy * ea_x
        dh = dh_scr[...]
        hp = hp_ref[0]
        dalast_x = jnp.sum(dh * hp, axis=0, keepdims=True) * elast_x
        dacs = jnp.zeros((BLK, 128), F32)
        for g in range(4):
            gs = slice(g * 512, (g + 1) * 512)
            bsl = slice(g * 128, (g + 1) * 128)
            cg = c_ref[:, bsl].astype(BF16)
            bg = b_ref[:, bsl].astype(BF16)
            cb = _dot_nt(cg, bg)
            hg16 = hp[:, gs].astype(BF16)
            dhg16 = dh[:, gs].astype(BF16)
            raw = _dot(cg, hg16)
            draw16 = dyea[:, gs].astype(BF16)
            dcg = _dot_nt(draw16, hg16)
            dhp_g = _dot_tn(cg, draw16)
            dbg = _dot_nt(xcd[:, gs].astype(BF16), dhg16)
            dxcd = _dot(bg, dhg16)
            dcb = jnp.zeros((BLK, BLK), F32)
            for j in range(8):
                h = g * 8 + j
                hsl = slice(h * 64, (h + 1) * 64)
                decay = _decay(a_cs, a_cs_t, h, causal)
                m = cb * decay
                dm = _dot_nt(dy16[:, hsl], xc16[:, hsl])
                dxc_scr[:, hsl] = _dot_tn(m.astype(BF16), dy16[:, hsl])
                dcb = dcb + dm * decay
                dseg = dm * m
                oneh = jnp.where(lane == h, 1.0, 0.0)
                dacs = dacs + jnp.sum(dseg, axis=1, keepdims=True) * oneh
                dacs_r[h:h + 1, :] = jnp.sum(dseg, axis=0, keepdims=True)
            dcb16 = dcb.astype(BF16)
            dcg = dcg + _dot(dcb16, bg)
            dbg = dbg + _dot_tn(dcb16, cg)
            dxbc_ref[:, D_INNER + g * 128:D_INNER + (g + 1) * 128] = dbg
            dxbc_ref[:, D_INNER + BC_DIM + g * 128:D_INNER + BC_DIM + (g + 1) * 128] = dcg
            dxc_scr[:, gs] += dxcd * ds_x[:, gs]
            dh_scr[:, gs] = dh[:, gs] * elast_x[:, gs] + dhp_g
            tst = dxcd * xcd[:, gs]
            tdiff[:, gs] = dy[:, gs] * (raw * ea_x[:, gs]) - tst
            tdiff[BLK - 1:BLK, gs] += jnp.sum(tst, axis=0, keepdims=True)
        dxc = dxc_scr[...]
        row = lax.broadcasted_iota(jnp.int32, (BLK, D_INNER), 0)
        tfull = tdiff[...] + jnp.where(row == BLK - 1, dalast_x, 0.0)
        dacs = dacs + _sel_dot(tfull, r_mat, 2) - dacs_r[...].T
        da = _dot_tn(causal.astype(F32), dacs, HI)
        ddt = da * a_neg + _sel_dot(dxc * xs, r_mat, 2)
        lmask = lax.broadcasted_iota(jnp.int32, (BLK, 128), 1) < N_SSD_HEADS
        ddtraw = jnp.where(lmask, ddt * _sigmoid(dtraw_ref[...] + dtb_ref[...]), 0.0)
        ddt_ref[...] = ddtraw.astype(BF16)
        dxbc_ref[:, 0:D_INNER] = dy * dskip_x + dxc * dt_x
        dvec_ref[0:1, :] += jnp.sum(ddtraw, axis=0, keepdims=True)
        dvec_ref[1:2, :] += jnp.where(lane < N_SSD_HEADS, jnp.sum(da * dt, axis=0, keepdims=True) * a_neg, 0.0)
        dvec_ref[2:3, :] += _sel_dot(_row8(jnp.sum(dy * xs, axis=0, keepdims=True)), r_mat, 3)[0:1]

    rev = lambda c: nc - 1 - c
    vec = pl.BlockSpec((1, 128), lambda c: (0, 0))
    own, extra = _hosted(
        body, name="ssd_bwd", grid=(nc,),
        in_specs=[pl.BlockSpec((BLK, D_INNER), lambda c: (rev(c), 0)),
                  pl.BlockSpec((BLK, BC_DIM), lambda c: (rev(c), D_INNER // BC_DIM)),
                  pl.BlockSpec((BLK, BC_DIM), lambda c: (rev(c), D_INNER // BC_DIM + 1)),
                  pl.BlockSpec((BLK, 128), lambda c: (rev(c), O_DT // 128)), vec, vec, vec,
                  pl.BlockSpec((1, 128, D_INNER), lambda c: (rev(c), 0, 0)),
                  pl.BlockSpec((BLK, D_INNER), lambda c: (rev(c), 0))],
        out_specs=[pl.BlockSpec((BLK, XBC_DIM), lambda c: (rev(c), 0)),
                   pl.BlockSpec((BLK, 128), lambda c: (rev(c), 0)),
                   pl.BlockSpec((8, 128), lambda c: (0, 0))],
        out_shape=[jax.ShapeDtypeStruct((s, XBC_DIM), F32), jax.ShapeDtypeStruct((s, 128), BF16),
                   jax.ShapeDtypeStruct((8, 128), F32)],
        scratch_shapes=[pltpu.VMEM((128, D_INNER), F32), pltpu.VMEM((BLK, D_INNER), BF16),
                        pltpu.VMEM((BLK, D_INNER), BF16), pltpu.VMEM((BLK, D_INNER), F32),
                        pltpu.VMEM((128, BLK), F32), pltpu.VMEM((BLK, D_INNER), F32)],
        args=(xbc, xbc, xbc, proj, dt_bias, a_log, d_skip, hprev, dy), sem=("arbitrary",), side=side)
    return own if side is None else (own, extra)


GW = 512


def _gate_norm_fwd(y, proj, wn, *, tm=512):
    s = y.shape[0]
    tm = _tile(s, tm)

    def body(y_ref, z_ref, w_ref, o_ref):
        z = z_ref[...]
        y2 = y_ref[...] * (z * _sigmoid(z))
        r = lax.rsqrt(jnp.mean(y2 * y2, axis=-1, keepdims=True) + EPS)
        o_ref[...] = ((y2 * r) * w_ref[...]).astype(BF16)

    return pl.pallas_call(
        body, name="gate_norm_fwd", grid=(s // tm, 4),
        in_specs=[pl.BlockSpec((tm, GW), lambda i, g: (i, g)), pl.BlockSpec((tm, GW), lambda i, g: (i, O_Z // GW + g)),
                  pl.BlockSpec((1, GW), lambda i, g: (0, g))],
        out_specs=pl.BlockSpec((tm, GW), lambda i, g: (i, g)),
        out_shape=jax.ShapeDtypeStruct((s, D_INNER), BF16), compiler_params=_cp(("parallel", "parallel")),
    )(y, proj, wn)


def _gate_norm_bwd(dyn, y, proj, wn, *, tm=512):
    s = y.shape[0]
    tm = _tile(s, tm)

    def body(d_ref, y_ref, z_ref, w_ref, dy_ref, dz_ref, dw_ref):
        i = pl.program_id(1)
        z = z_ref[...]
        sg = _sigmoid(z)
        sz = z * sg
        yv = y_ref[...]
        y2 = yv * sz
        r = lax.rsqrt(jnp.mean(y2 * y2, axis=-1, keepdims=True) + EPS)
        xh = y2 * r
        dv = d_ref[...]
        g = dv * w_ref[...]
        dy2 = r * (g - xh * jnp.mean(g * xh, axis=-1, keepdims=True))
        dy_ref[...] = dy2 * sz
        dz_ref[...] = (dy2 * yv * _dsilu(z, sg)).astype(BF16)
        part = jnp.sum(dv * xh, axis=0, keepdims=True)

        @pl.when(i == 0)
        def _():
            dw_ref[...] = part

        @pl.when(i > 0)
        def _():
            dw_ref[...] += part

    blk = pl.BlockSpec((tm, GW), lambda g, i: (i, g))
    vec = pl.BlockSpec((1, GW), lambda g, i: (0, g))
    return pl.pallas_call(
        body, name="gate_norm_bwd", grid=(4, s // tm),
        in_specs=[blk, blk, pl.BlockSpec((tm, GW), lambda g, i: (i, O_Z // GW + g)), vec],
        out_specs=[blk, blk, vec],
        out_shape=[jax.ShapeDtypeStruct((s, D_INNER), F32), jax.ShapeDtypeStruct((s, D_INNER), BF16),
                   jax.ShapeDtypeStruct((1, D_INNER), F32)],
        compiler_params=_cp(("parallel", "arbitrary")),
    )(dyn, y, proj, wn)


def _merge_fwd(proj, b_gate, attn, ssd_out, *, tm=512):
    s = attn.shape[0]
    tm = _tile(s, tm)

    def body(ga_ref, gs_ref, ba_ref, bs_ref, a_ref, s_ref, o_ref):
        ga = _sigmoid(ga_ref[...] + ba_ref[...])
        gs = _sigmoid(gs_ref[...] + bs_ref[...])
        o_ref[...] = (ga * a_ref[...] + gs * s_ref[...]).astype(BF16)

    blk = pl.BlockSpec((tm, GW), lambda i, j: (i, j))
    return pl.pallas_call(
        body, name="merge_fwd", grid=(s // tm, 2),
        in_specs=[pl.BlockSpec((tm, GW), lambda i, j: (i, O_GA // GW + j)),
                  pl.BlockSpec((tm, GW), lambda i, j: (i, O_GS // GW + j)),
                  pl.BlockSpec((1, GW), lambda i, j: (0, j)), pl.BlockSpec((1, GW), lambda i, j: (0, 2 + j)), blk, blk],
        out_specs=blk, out_shape=jax.ShapeDtypeStruct((s, D_MODEL), BF16),
        compiler_params=_cp(("parallel", "parallel")),
    )(proj, proj, b_gate, b_gate, attn, ssd_out)


def _merge_bwd(dm, proj, b_gate, attn, ssd_out, *, tm=512):
    s = attn.shape[0]
    tm = _tile(s, tm)

    def body(d_ref, ga_ref, gs_ref, ba_ref, bs_ref, a_ref, s_ref, da_ref, ds_ref, dga_ref, dgs_ref, dba_ref, dbs_ref):
        i = pl.program_id(1)
        ga = _sigmoid(ga_ref[...] + ba_ref[...])
        gs = _sigmoid(gs_ref[...] + bs_ref[...])
        d = d_ref[...]
        da_ref[...] = (d * ga).astype(BF16)
        ds_ref[...] = (d * gs).astype(BF16)
        dga = d * a_ref[...] * (ga * (1.0 - ga))
        dgs = d * s_ref[...] * (gs * (1.0 - gs))
        dga_ref[...] = dga.astype(BF16)
        dgs_ref[...] = dgs.astype(BF16)
        pa = jnp.sum(dga, axis=0, keepdims=True)
        ps = jnp.sum(dgs, axis=0, keepdims=True)

        @pl.when(i == 0)
        def _():
            dba_ref[...] = pa
            dbs_ref[...] = ps

        @pl.when(i > 0)
        def _():
            dba_ref[...] += pa
            dbs_ref[...] += ps

    blk = pl.BlockSpec((tm, GW), lambda j, i: (i, j))
    vec = pl.BlockSpec((1, GW), lambda j, i: (0, j))
    sd = jax.ShapeDtypeStruct((s, D_MODEL), BF16)
    vd = jax.ShapeDtypeStruct((1, D_MODEL), F32)
    return pl.pallas_call(
        body, name="merge_bwd", grid=(2, s // tm),
        in_specs=[blk, pl.BlockSpec((tm, GW), lambda j, i: (i, O_GA // GW + j)),
                  pl.BlockSpec((tm, GW), lambda j, i: (i, O_GS // GW + j)),
                  vec, pl.BlockSpec((1, GW), lambda j, i: (0, 2 + j)), blk, blk],
        out_specs=[blk, blk, blk, blk, vec, vec], out_shape=[sd, sd, sd, sd, vd, vd],
        compiler_params=_cp(("parallel", "arbitrary")),
    )(dm, proj, proj, b_gate, b_gate, attn, ssd_out)


def _adamw_math(w, g, m, v):
    mn = ADAM_B1 * m + (1.0 - ADAM_B1) * g
    vn = ADAM_B2 * v + (1.0 - ADAM_B2) * (g * g)
    m_hat = mn / (1.0 - ADAM_B1 ** ADAM_STEP)
    v_hat = vn / (1.0 - ADAM_B2 ** ADAM_STEP)
    return -ADAM_LR * (m_hat / (jnp.sqrt(v_hat) + ADAM_EPS) + ADAM_WD * w), mn, vn


def _adamw_many(ws, gs, ms, vs):
    n = len(ws)

    def body(*refs):
        outs = refs[4 * n:]
        for i in range(n):
            res = _adamw_math(*[refs[q * n + i][...] for q in range(4)])
            for q in range(3):
                outs[q * n + i][...] = res[q]

    return pl.pallas_call(body, name="adamw_small", out_shape=[jax.ShapeDtypeStruct(w.shape, F32) for w in ws] * 3,
                          compiler_params=_cp())(*ws, *gs, *ms, *vs)


def _adamw(w, g, m, v, *, name, tm=128):
    _, r, c = w.shape
    tm = r if (r < tm or r % tm) else tm

    def body(w_ref, g_ref, m_ref, v_ref, d_ref, nm_ref, nv_ref, g_out):
        gv = g_ref[:, :c]
        d_ref[0], nm_ref[0], nv_ref[0] = _adamw_math(w_ref[0], gv, m_ref[0], v_ref[0])
        g_out[0] = gv

    blk = pl.BlockSpec((1, tm, c), lambda i: (0, i, 0))
    sd = jax.ShapeDtypeStruct((1, r, c), F32)
    return pl.pallas_call(
        body, name=name, grid=(r // tm,), in_specs=[blk, pl.BlockSpec((tm, g.shape[1]), lambda i: (i, 0)), blk, blk],
        out_specs=[blk] * 4, out_shape=[sd] * 4, compiler_params=_cp(("parallel",)),
    )(w, g, m, v)


ANY = pl.BlockSpec(memory_space=pl.ANY)
N_CHIPS = 4


def _chip_of(k, x, y):
    return (x ^ (k >> 1), y ^ (k & 1))


def _all_gather_small(shard):
    r, c = shard.shape
    hr = r // 2

    def body(sh_ref, out_ref, send_sems, recv_sems, local_sem):
        x, y, cc = lax.axis_index("x"), lax.axis_index("y"), lax.axis_index("c")

        def half(px, py, pc):
            return out_ref.at[2 * px + py, pl.ds(pc * hr, hr), :]

        def copy(k, px, py, pc, to, src=None):
            return pltpu.make_async_remote_copy(
                src_ref=half(px, py, pc) if src is None else src, dst_ref=half(px, py, pc),
                send_sem=send_sems.at[k], recv_sem=recv_sems.at[k], device_id=to, device_id_type=MESH)

        mine = pltpu.make_async_copy(sh_ref, out_ref.at[2 * x + y], local_sem)
        mine.start()
        chips = [_chip_of(k, x, y) for k in (1, 2, 3)]
        first = [copy(j, x, y, cc, (*chip, cc), src=sh_ref.at[pl.ds(cc * hr, hr), :]) for j, chip in enumerate(chips)]
        for cp in first:
            cp.start()
        passed = [copy(3 + j, *chip, cc, (x, y, 1 - cc)) for j, chip in enumerate(chips)]
        for j, chip in enumerate(chips):
            copy(j, *chip, cc, (x, y, cc)).wait_recv()
            passed[j].start()
        for j, chip in enumerate(chips):
            copy(3 + j, *chip, 1 - cc, (x, y, cc)).wait_recv()
        for cp in first + passed:
            cp.wait_send()
        mine.wait()

    return pl.pallas_call(
        body, name="all_gather_small", in_specs=[ANY], out_specs=ANY,
        out_shape=jax.ShapeDtypeStruct((N_CHIPS, r, c), shard.dtype),
        scratch_shapes=[pltpu.SemaphoreType.DMA((6,)), pltpu.SemaphoreType.DMA((6,)), pltpu.SemaphoreType.DMA],
    )(shard)


def _cast_bf16(a, *, name, tm=512):
    n, r, c = a.shape
    tm = _tile(r, tm) if r % 128 == 0 else r

    def body(a_ref, o_ref):
        o_ref[...] = a_ref[...].astype(BF16)

    blk = pl.BlockSpec((1, tm, c), lambda i, j: (i, j, 0))
    return pl.pallas_call(body, name=name, grid=(n, r // tm), in_specs=[blk], out_specs=blk,
                          out_shape=jax.ShapeDtypeStruct(a.shape, BF16), compiler_params=_cp(("parallel", "parallel")))(a)


def _pair_exchange(g16, hr):
    n, r, c = g16.shape

    def body(g_ref, out_ref, send_sem, recv_sem):
        x, y, cc = lax.axis_index("x"), lax.axis_index("y"), lax.axis_index("c")
        cp = pltpu.make_async_remote_copy(
            src_ref=g_ref.at[:, pl.ds((1 - cc) * hr, hr), :], dst_ref=out_ref, send_sem=send_sem, recv_sem=recv_sem,
            device_id=(x, y, 1 - cc), device_id_type=MESH)
        cp.start()
        cp.wait()

    return pl.pallas_call(
        body, name="grad_pair_exchange", in_specs=[ANY], out_specs=ANY,
        out_shape=jax.ShapeDtypeStruct((n, hr, c), g16.dtype),
        scratch_shapes=[pltpu.SemaphoreType.DMA, pltpu.SemaphoreType.DMA],
    )(g16)


def _pair_add(g, recv, half_idx, hr, *, tm=384):
    n, r, c = g.shape
    nt = hr // tm

    def body(hi_ref, g_ref, r_ref, o32_ref, o16_ref):
        v = g_ref[...] + r_ref[...].astype(F32)
        o32_ref[...] = v
        o16_ref[...] = v.astype(BF16)

    gs = pltpu.PrefetchScalarGridSpec(
        num_scalar_prefetch=1, grid=(n, nt),
        in_specs=[pl.BlockSpec((1, tm, c), lambda i, j, hi: (i, hi[0] * nt + j, 0)),
                  pl.BlockSpec((1, tm, c), lambda i, j, hi: (i, j, 0))],
        out_specs=[pl.BlockSpec((1, tm, c), lambda i, j, hi: (i, j, 0))] * 2)
    return pl.pallas_call(
        body, name="grad_pair_add", grid_spec=gs,
        out_shape=[jax.ShapeDtypeStruct((n, hr, c), F32), jax.ShapeDtypeStruct((n, hr, c), BF16)],
        compiler_params=_cp(("parallel", "parallel")),
    )(half_idx, g, recv)


def _chip_exchange(p16):
    n, hr, c = p16.shape

    def body(p_ref, out_ref, send_sems, recv_sems):
        x, y, cc = lax.axis_index("x"), lax.axis_index("y"), lax.axis_index("c")
        cps = []
        for j, k in enumerate((1, 2, 3)):
            px, py = _chip_of(k, x, y)
            cps.append(pltpu.make_async_remote_copy(
                src_ref=p_ref.at[2 * px + py], dst_ref=out_ref.at[j], send_sem=send_sems.at[j], recv_sem=recv_sems.at[j],
                device_id=(px, py, cc), device_id_type=MESH))
        for cp in cps:
            cp.start()
        for cp in cps:
            cp.wait()

    return pl.pallas_call(
        body, name="grad_chip_exchange", in_specs=[ANY], out_specs=ANY,
        out_shape=jax.ShapeDtypeStruct((3, hr, c), p16.dtype),
        scratch_shapes=[pltpu.SemaphoreType.DMA((3,)), pltpu.SemaphoreType.DMA((3,))],
    )(p16)


def _chip_add(p32, recv, chip_idx, *, tm=384):
    n, hr, c = p32.shape

    def body(ci_ref, p_ref, r_ref, o_ref):
        o_ref[...] = ((p_ref[0] + r_ref[0].astype(F32)) + r_ref[1].astype(F32)) + r_ref[2].astype(F32)

    gs = pltpu.PrefetchScalarGridSpec(
        num_scalar_prefetch=1, grid=(hr // tm,),
        in_specs=[pl.BlockSpec((1, tm, c), lambda j, ci: (ci[0], j, 0)), pl.BlockSpec((3, tm, c), lambda j, ci: (0, j, 0))],
        out_specs=pl.BlockSpec((tm, c), lambda j, ci: (j, 0)))
    return pl.pallas_call(
        body, name="grad_chip_add", grid_spec=gs, out_shape=jax.ShapeDtypeStruct((hr, c), F32),
        compiler_params=_cp(("parallel",)),
    )(chip_idx, p32, recv)


def _pair_gather(f):
    hr, c = f.shape

    def body(f_ref, out_ref, send_sem, recv_sem, local_sem):
        x, y, cc = lax.axis_index("x"), lax.axis_index("y"), lax.axis_index("c")
        mine = pltpu.make_async_copy(f_ref, out_ref.at[pl.ds(cc * hr, hr), :], local_sem)
        mine.start()
        cp = pltpu.make_async_remote_copy(
            src_ref=f_ref, dst_ref=out_ref.at[pl.ds(cc * hr, hr), :], send_sem=send_sem, recv_sem=recv_sem,
            device_id=(x, y, 1 - cc), device_id_type=MESH)
        cp.start()
        cp.wait()
        mine.wait()

    return pl.pallas_call(
        body, name="grad_pair_gather", in_specs=[ANY], out_specs=ANY,
        out_shape=jax.ShapeDtypeStruct((2 * hr, c), f.dtype),
        scratch_shapes=[pltpu.SemaphoreType.DMA, pltpu.SemaphoreType.DMA, pltpu.SemaphoreType.DMA],
    )(f)


def _all_reduce_small(buf):
    r, c = buf.shape

    def body(b_ref, out_ref, gat, send_sems, recv_sems):
        x, y, cc = lax.axis_index("x"), lax.axis_index("y"), lax.axis_index("c")
        me = 4 * x + 2 * y + cc
        gat[me] = b_ref[...]
        cps = []
        for k in range(1, 8):
            px, py, pc = x ^ (k >> 2), y ^ ((k >> 1) & 1), cc ^ (k & 1)
            cps.append(pltpu.make_async_remote_copy(
                src_ref=b_ref, dst_ref=gat.at[me], send_sem=send_sems.at[k - 1], recv_sem=recv_sems.at[k - 1],
                device_id=(px, py, pc), device_id_type=MESH))
        for cp in cps:
            cp.start()
        for cp in cps:
            cp.wait()
        acc = gat[0]
        for d in range(1, 8):
            acc = acc + gat[d]
        out_ref[...] = acc

    vm = pl.BlockSpec(memory_space=pltpu.VMEM)
    return pl.pallas_call(
        body, name="all_reduce_small", in_specs=[vm], out_specs=vm, out_shape=jax.ShapeDtypeStruct((r, c), F32),
        scratch_shapes=[pltpu.VMEM((8, r, c), F32), pltpu.SemaphoreType.DMA((7,)), pltpu.SemaphoreType.DMA((7,))],
        compiler_params=pltpu.CompilerParams(vmem_limit_bytes=VMEM_LIMIT),
    )(buf)


def _pipe(fn, ins, outs, tr, depth=4):
    shape = ins[0].shape
    lead, (r, c) = shape[:-2], shape[-2:]
    assert len(lead) <= 1 and r % tr == 0
    nr = r // tr
    n = nr * (lead[0] if lead else 1)
    ni, no = len(ins), len(outs)

    def blk(ref, step):
        rows = pl.ds((step % nr) * tr, tr)
        return ref.at[step // nr, rows, :] if lead else ref.at[rows, :]

    def scoped(*bufs):
        ibufs, obufs, isem, osem = bufs[:ni], bufs[ni:ni + no], bufs[-2], bufs[-1]

        def in_copy(q, step, slot):
            return pltpu.make_async_copy(blk(ins[q], step), ibufs[q].at[slot], isem.at[q, slot])

        def out_copy(q, step, slot):
            return pltpu.make_async_copy(obufs[q].at[slot], blk(outs[q], step), osem.at[q, slot])

        for step in range(min(nbuf - 1, n)):
            for q in range(ni):
                in_copy(q, step, step % nbuf).start()
        for step in range(n):
            slot = step % nbuf
            if step + nbuf - 1 < n:
                for q in range(ni):
                    in_copy(q, step + nbuf - 1, (step + nbuf - 1) % nbuf).start()
            for q in range(ni):
                in_copy(q, step, slot).wait()
            if step >= nbuf:
                for q in range(no):
                    out_copy(q, step - nbuf, slot).wait()
            res = fn(*[ibufs[q][slot] for q in range(ni)])
            for q in range(no):
                obufs[q][slot] = res[q].astype(obufs[q].dtype)
                out_copy(q, step, slot).start()
        for step in range(max(n - nbuf, 0), n):
            for q in range(no):
                out_copy(q, step, step % nbuf).wait()

    assert n <= 8
    nbuf = min(n, depth)
    pl.run_scoped(scoped, *[pltpu.VMEM((nbuf, tr, c), q.dtype) for q in ins], *[pltpu.VMEM((nbuf, tr, c), q.dtype) for q in outs],
                  pltpu.SemaphoreType.DMA((ni, nbuf)), pltpu.SemaphoreType.DMA((no, nbuf)))


W_IN_PAD = 2304
BIG = ("w_in", "w_attn_o", "w_ssd_o", "w_out", "w_up", "w_down")
BIG_SHAPE = dict(w_in=(D_MODEL, W_IN_PAD), w_attn_o=(Q_DIM // 4, D_MODEL), w_ssd_o=(D_INNER // 4, D_MODEL),
                 w_out=(D_MODEL // 4, D_MODEL), w_up=(D_MODEL, 2 * D_FF // 4), w_down=(D_FF // 4, D_MODEL))
BIG_TR = dict(w_in=128, w_attn_o=128, w_ssd_o=128, w_out=128, w_up=128, w_down=176)
X_FIRST = dict(w_in=True, w_attn_o=True, w_ssd_o=False, w_out=True, w_up=False, w_down=False)


def _neighbours(x, y, x_first):
    xn, yn = (1 - x, y), (x, 1 - y)
    n1, n2 = (xn, yn) if x_first else (yn, xn)
    slot = lambda ch: 2 * ch[0] + ch[1]
    return n1, n2, slot(n1), slot(n2), slot((1 - x, 1 - y))


def _gather_big(shards):
    nt = len(BIG)

    def body(*refs):
        sh, out = refs[:nt], refs[nt:2 * nt]
        send_sems, recv_sems = refs[2 * nt:]
        x, y, cc = lax.axis_index("x"), lax.axis_index("y"), lax.axis_index("c")
        me = 2 * x + y
        sib = (x, y, 1 - cc)
        for t, n in enumerate(BIG):
            _pipe(lambda v: (v,), [sh[t]], [out[t].at[me]], BIG_TR[n])

        def copy(t, k, slot, pc, to):
            hr = BIG_SHAPE[BIG[t]][0] // 2
            ref = out[t].at[slot, pl.ds(pc * hr, hr), :]
            return pltpu.make_async_remote_copy(src_ref=ref, dst_ref=ref, send_sem=send_sems.at[6 * t + k],
                                                recv_sem=recv_sems.at[6 * t + k], device_id=to, device_id_type=MESH)

        started = []

        def start(cp):
            cp.start()
            started.append(cp)

        geo = [_neighbours(x, y, X_FIRST[n]) for n in BIG]
        for t in range(nt):
            n1, n2, _, _, _ = geo[t]
            start(copy(t, 0, me, cc, (*n1, cc)))
            start(copy(t, 1, me, cc, (*n2, cc)))
        for t in range(nt):
            n1, n2, s1, s2, sd = geo[t]
            copy(t, 0, s1, cc, sib).wait_recv()
            start(copy(t, 2, s1, cc, (*n2, cc)))
            start(copy(t, 3, s1, cc, sib))
            copy(t, 1, s2, cc, sib).wait_recv()
            start(copy(t, 4, s2, cc, sib))
        for t in range(nt):
            _, _, s1, s2, sd = geo[t]
            copy(t, 2, sd, cc, sib).wait_recv()
            start(copy(t, 5, sd, cc, sib))
        for t in range(nt):
            _, _, s1, s2, sd = geo[t]
            copy(t, 3, s1, 1 - cc, sib).wait_recv()
            copy(t, 4, s2, 1 - cc, sib).wait_recv()
            copy(t, 5, sd, 1 - cc, sib).wait_recv()
        for cp in started:
            cp.wait_send()

    return pl.pallas_call(
        body, name="gather_big", in_specs=[ANY] * nt, out_specs=[ANY] * nt,
        out_shape=[jax.ShapeDtypeStruct((N_CHIPS, *BIG_SHAPE[n]), BF16) for n in BIG],
        scratch_shapes=[pltpu.SemaphoreType.DMA((6 * nt,)), pltpu.SemaphoreType.DMA((6 * nt,))],
        compiler_params=pltpu.CompilerParams(vmem_limit_bytes=VMEM_LIMIT),
    )(*shards)


def _reduce_big(grads):
    nt = len(BIG)
    nw = 7

    def body(*refs):
        g = refs[:nt]
        fin = refs[nt:2 * nt]
        work = refs[2 * nt:2 * nt + nw * nt]
        send_sems, recv_sems = refs[2 * nt + nw * nt:]
        x, y, cc = lax.axis_index("x"), lax.axis_index("y"), lax.axis_index("c")
        me = 2 * x + y
        sib = (x, y, 1 - cc)
        started = []

        def rcopy(t, k, src, dst, to):
            cp = pltpu.make_async_remote_copy(src_ref=src, dst_ref=dst, send_sem=send_sems.at[5 * t + k],
                                              recv_sem=recv_sems.at[5 * t + k], device_id=to, device_id_type=MESH)
            return cp

        def start(cp):
            cp.start()
            started.append(cp)

        geo = [_neighbours(x, y, X_FIRST[n]) for n in BIG]
        hrs = [BIG_SHAPE[n][0] // 2 for n in BIG]
        wk = lambda t: work[nw * t:nw * (t + 1)]
        one = lambda ref, slot: ref.at[pl.ds(slot, 1)]
        for t in range(nt):
            recv_a = wk(t)[0]
            start(rcopy(t, 0, g[t].at[:, pl.ds((1 - cc) * hrs[t], hrs[t]), :], recv_a, sib))
        for t, n in enumerate(BIG):
            recv_a, p32, p16, r1, qme, qs2, r2 = wk(t)
            n1, n2, s1, s2, sd = geo[t]
            rcopy(t, 0, recv_a, recv_a, sib).wait_recv()
            _pipe(lambda a, b: (a + b, a + b), [g[t].at[:, pl.ds(cc * hrs[t], hrs[t]), :], recv_a], [p32, p16], BIG_TR[n])
            start(rcopy(t, 1, one(p16, s1), one(r1, 0), (*n1, cc)))
            start(rcopy(t, 2, one(p16, sd), one(r1, 1), (*n1, cc)))
        for t, n in enumerate(BIG):
            recv_a, p32, p16, r1, qme, qs2, r2 = wk(t)
            n1, n2, s1, s2, sd = geo[t]
            rcopy(t, 1, one(r1, 0), one(r1, 0), sib).wait_recv()
            rcopy(t, 2, one(r1, 1), one(r1, 1), sib).wait_recv()
            _pipe(lambda a, b: (a + b.astype(F32),), [one(p32, s2), one(r1, 1)], [qs2], BIG_TR[n])
            start(rcopy(t, 3, qs2, r2, (*n2, cc)))
            _pipe(lambda a, b: (a + b.astype(F32),), [one(p32, me), one(r1, 0)], [qme], BIG_TR[n])
        for t, n in enumerate(BIG):
            recv_a, p32, p16, r1, qme, qs2, r2 = wk(t)
            rcopy(t, 3, r2, r2, sib).wait_recv()
            mine = fin[t].at[pl.ds(cc * hrs[t], hrs[t]), :]
            _pipe(lambda a, b: (a + b.astype(F32),), [qme.at[0], r2.at[0]], [mine], BIG_TR[n])
            start(rcopy(t, 4, mine, mine, sib))
        for t in range(nt):
            other = fin[t].at[pl.ds((1 - cc) * hrs[t], hrs[t]), :]
            rcopy(t, 4, other, other, sib).wait_recv()
        for cp in started:
            cp.wait_send()

    outs = [jax.ShapeDtypeStruct(BIG_SHAPE[n], F32) for n in BIG]
    for n in BIG:
        r, c = BIG_SHAPE[n]
        hr = r // 2
        outs += [jax.ShapeDtypeStruct((4, hr, c), F32), jax.ShapeDtypeStruct((4, hr, c), F32),
                 jax.ShapeDtypeStruct((4, hr, c), BF16), jax.ShapeDtypeStruct((2, hr, c), BF16),
                 jax.ShapeDtypeStruct((1, hr, c), F32), jax.ShapeDtypeStruct((1, hr, c), BF16),
                 jax.ShapeDtypeStruct((1, hr, c), BF16)]
    res = pl.pallas_call(
        body, name="reduce_big", in_specs=[ANY] * nt, out_specs=[ANY] * len(outs), out_shape=outs,
        scratch_shapes=[pltpu.SemaphoreType.DMA((5 * nt,)), pltpu.SemaphoreType.DMA((5 * nt,))],
        compiler_params=pltpu.CompilerParams(vmem_limit_bytes=VMEM_LIMIT),
    )(*grads)
    return res[:nt]


WHOLE_X_FIRST = dict(w_ssd_o=True, w_out=False, w_attn_o=False)


def _quarters(names):
    out = []
    for i, n in enumerate(names):
        if n in WHOLE_X_FIRST:
            h = BIG_SHAPE[n][0] // 2
            out.append((i, WHOLE_X_FIRST[n], 0, h, 128))
        else:
            q = BIG_SHAPE[n][0] // 4
            tr = 128 if q % 128 == 0 else q
            out += [(i, True, 0, q, tr), (i, False, q, q, tr)]
    return out


class _GatherJob:
    def __init__(self, names, shards, at=None):
        self.names = names
        self.at = at
        self.inputs = list(shards)
        self.out_shapes = [jax.ShapeDtypeStruct((N_CHIPS, *BIG_SHAPE[n]), BF16) for n in names]
        self.ent = _quarters(names)
        self.scratch = [pltpu.SemaphoreType.DMA((6 * len(self.ent),)), pltpu.SemaphoreType.DMA((6 * len(self.ent),))]

    def phases(self, sh, out, scr):
        send_sems, recv_sems = scr
        names, ent = self.names, self.ent
        x, y, cc = lax.axis_index("x"), lax.axis_index("y"), lax.axis_index("c")
        me = 2 * x + y
        sib = (x, y, 1 - cc)
        geo = [_neighbours(x, y, e[1]) for e in ent]
        started = []

        def copy(i, k, slot, pc, to):
            arr, _, roff, rows, _ = ent[i]
            hr = BIG_SHAPE[names[arr]][0] // 2
            ref = out[arr].at[slot, pl.ds(pc * hr + roff, rows), :]
            return pltpu.make_async_remote_copy(src_ref=ref, dst_ref=ref, send_sem=send_sems.at[6 * i + k],
                                                recv_sem=recv_sems.at[6 * i + k], device_id=to, device_id_type=MESH)

        def start(*a):
            copy(*a).start()
            started.append(a)

        def p0():
            for t, n in enumerate(names):
                _pipe(lambda v: (v,), [sh[t]], [out[t].at[me]], BIG_TR[n])
            for i in range(len(ent)):
                n1, n2, _, _, _ = geo[i]
                start(i, 0, me, cc, (*n1, cc))
                start(i, 1, me, cc, (*n2, cc))

        def p1():
            for i in range(len(ent)):
                n1, n2, s1, s2, sd = geo[i]
                copy(i, 0, s1, cc, sib).wait_recv()
                start(i, 2, s1, cc, (*n2, cc))
                start(i, 3, s1, cc, sib)
                copy(i, 1, s2, cc, sib).wait_recv()
                start(i, 4, s2, cc, sib)

        def p2():
            for i in range(len(ent)):
                sd = geo[i][4]
                copy(i, 2, sd, cc, sib).wait_recv()
                start(i, 5, sd, cc, sib)

        def p3():
            for i in range(len(ent)):
                _, _, s1, s2, sd = geo[i]
                copy(i, 3, s1, 1 - cc, sib).wait_recv()
                copy(i, 4, s2, 1 - cc, sib).wait_recv()
                copy(i, 5, sd, 1 - cc, sib).wait_recv()
            for a in started:
                copy(*a).wait_send()

        return [p0, p1, p2, p3]


class _ReduceJob:
    NW = 7

    def __init__(self, names, grads, at=None):
        self.names = names
        self.at = at
        self.inputs = list(grads)
        self.ent = _quarters(names)
        self.out_shapes = [jax.ShapeDtypeStruct(BIG_SHAPE[n], F32) for n in names]
        for arr, _, _, rows, _ in self.ent:
            c = BIG_SHAPE[names[arr]][1]
            self.out_shapes += [jax.ShapeDtypeStruct((4, rows, c), F32), jax.ShapeDtypeStruct((4, rows, c), F32),
                                jax.ShapeDtypeStruct((4, rows, c), BF16), jax.ShapeDtypeStruct((2, rows, c), BF16),
                                jax.ShapeDtypeStruct((1, rows, c), F32), jax.ShapeDtypeStruct((1, rows, c), BF16),
                                jax.ShapeDtypeStruct((1, rows, c), BF16)]
        self.scratch = [pltpu.SemaphoreType.DMA((5 * len(self.ent),)), pltpu.SemaphoreType.DMA((5 * len(self.ent),))]

    def phases(self, g, outs, scr):
        send_sems, recv_sems = scr
        names, ent, nw = self.names, self.ent, self.NW
        nt = len(names)
        fin, work = outs[:nt], outs[nt:]
        x, y, cc = lax.axis_index("x"), lax.axis_index("y"), lax.axis_index("c")
        me = 2 * x + y
        sib = (x, y, 1 - cc)
        geo = [_neighbours(x, y, e[1]) for e in ent]
        started = []
        wk = lambda i: work[nw * i:nw * (i + 1)]
        one = lambda ref, slot: ref.at[pl.ds(slot, 1)]

        def rows_of(i, pc):
            arr, _, roff, rows, _ = ent[i]
            return pl.ds(pc * (BIG_SHAPE[names[arr]][0] // 2) + roff, rows)

        def rcopy(i, k, src, dst, to):
            return pltpu.make_async_remote_copy(src_ref=src, dst_ref=dst, send_sem=send_sems.at[5 * i + k],
                                                recv_sem=recv_sems.at[5 * i + k], device_id=to, device_id_type=MESH)

        def start(make):
            make().start()
            started.append(make)

        def p0():
            for i, e in enumerate(ent):
                start(lambda i=i, e=e: rcopy(i, 0, g[e[0]].at[:, rows_of(i, 1 - cc), :], wk(i)[0], sib))

        def p1():
            for i, e in enumerate(ent):
                recv_a, p32, p16, r1 = wk(i)[:4]
                n1, n2, s1, s2, sd = geo[i]
                rcopy(i, 0, recv_a, recv_a, sib).wait_recv()
                _pipe(lambda a, b: (a + b, a + b), [g[e[0]].at[:, rows_of(i, cc), :], recv_a], [p32, p16], e[4])
                start(lambda i=i, s1=s1, n1=n1: rcopy(i, 1, one(wk(i)[2], s1), one(wk(i)[3], 0), (*n1, cc)))
                start(lambda i=i, sd=sd, n1=n1: rcopy(i, 2, one(wk(i)[2], sd), one(wk(i)[3], 1), (*n1, cc)))

        def p2():
            for i, e in enumerate(ent):
                _, p32, _, r1, qme, qs2, r2 = wk(i)
                n1, n2, s1, s2, sd = geo[i]
                rcopy(i, 1, one(r1, 0), one(r1, 0), sib).wait_recv()
                rcopy(i, 2, one(r1, 1), one(r1, 1), sib).wait_recv()
                _pipe(lambda a, b, c, d: (a + b.astype(F32), c + d.astype(F32)),
                      [one(p32, s2), one(r1, 1), one(p32, me), one(r1, 0)], [qs2, qme], e[4])
                start(lambda i=i, n2=n2: rcopy(i, 3, wk(i)[5], wk(i)[6], (*n2, cc)))

        def p3():
            for i, e in enumerate(ent):
                qme, r2 = wk(i)[4], wk(i)[6]
                rcopy(i, 3, r2, r2, sib).wait_recv()
                mine = fin[e[0]].at[rows_of(i, cc), :]
                _pipe(lambda a, b: (a + b.astype(F32),), [qme.at[0], r2.at[0]], [mine], e[4])
                start(lambda i=i, e=e: rcopy(i, 4, fin[e[0]].at[rows_of(i, cc), :], fin[e[0]].at[rows_of(i, cc), :], sib))

        def p4():
            for i, e in enumerate(ent):
                other = fin[e[0]].at[rows_of(i, 1 - cc), :]
                rcopy(i, 4, other, other, sib).wait_recv()
            for make in started:
                make().wait_send()

        return [p0, p1, p2, p3, p4]


class _AdamJob:
    def __init__(self, names, ws, gs, ms, vs, groups):
        self.names, self.groups = names, groups
        self.inputs = [a for quad in zip(ws, gs, ms, vs) for a in quad]
        self.out_shapes = [jax.ShapeDtypeStruct(w.shape, F32) for w in ws for _ in range(4)]

    def work(self, ins, outs):
        def one(t):
            w, g, m, v = ins[4 * t:4 * t + 4]
            r = w.shape[1]
            tr = 128 if r % 128 == 0 else r // 4
            _pipe(lambda a, b, c, d: (*_adamw_math(a, b, c, d), b), [w.at[0], g, m.at[0], v.at[0]],
                  [o.at[0] for o in outs[4 * t:4 * t + 4]], tr, depth=2)

        def group(grp):
            def run():
                for n in grp:
                    one(self.names.index(n))
            return run

        return [group(grp) for grp in self.groups]


class _Interleaved:
    def __init__(self, job, work, at):
        self.job, self.wk, self.at = job, work, at
        self.inputs = job.inputs + work.inputs
        self.out_shapes = list(job.out_shapes) + list(work.out_shapes)
        self.scratch = job.scratch

    def phases(self, ins, outs, scr):
        nj, no = len(self.job.inputs), len(self.job.out_shapes)
        base = self.job.phases(ins[:nj], outs[:no], scr)
        work = self.wk.work(ins[nj:], outs[no:])
        mixed = []
        for k, ph in enumerate(base):
            mixed.append(ph)
            if k < len(work):
                mixed.append(work[k])
        return mixed


def _run_job(job, name):
    ni, no = len(job.inputs), len(job.out_shapes)

    def body(*refs):
        for ph in job.phases(refs[:ni], refs[ni:ni + no], refs[ni + no:]):
            ph()

    return pl.pallas_call(
        body, name=name, in_specs=[ANY] * ni, out_specs=[ANY] * no, out_shape=job.out_shapes, scratch_shapes=job.scratch,
        compiler_params=pltpu.CompilerParams(vmem_limit_bytes=VMEM_LIMIT),
    )(*job.inputs)


def _hosted(body, *, name, grid, in_specs, out_specs, out_shape, scratch_shapes, args, sem, side=None):
    if side is None:
        return pl.pallas_call(body, name=name, grid=grid, in_specs=in_specs, out_specs=out_specs, out_shape=out_shape,
                              scratch_shapes=scratch_shapes, compiler_params=_cp(sem))(*args), None
    job = side
    ni, no, ns = len(in_specs), len(out_specs), len(scratch_shapes)
    ji, jo = len(job.inputs), len(job.out_shapes)
    n_steps = 1
    for extent in grid:
        n_steps *= extent

    def wrapped(*refs):
        own_in, refs = refs[:ni], refs[ni:]
        job_in, refs = refs[:ji], refs[ji:]
        own_out, refs = refs[:no], refs[no:]
        job_out, refs = refs[:jo], refs[jo:]
        own_scr, job_scr = refs[:ns], refs[ns:]
        step = 0
        for d, extent in enumerate(grid):
            step = step * extent + pl.program_id(d)
        phases = job.phases(job_in, job_out, job_scr)
        steps = [min(int(f * n_steps), n_steps - 1) for f in job.at] + [n_steps - 1]
        assert len(steps) == len(phases) and steps == sorted(steps)
        for at, ph in zip(steps, phases):
            pl.when(step == at)(ph)
        body(*own_in, *own_out, *own_scr)

    res = pl.pallas_call(
        wrapped, name=name, grid=grid, in_specs=list(in_specs) + [ANY] * ji, out_specs=list(out_specs) + [ANY] * jo,
        out_shape=list(out_shape) + list(job.out_shapes), scratch_shapes=list(scratch_shapes) + list(job.scratch),
        compiler_params=_cp(("arbitrary",) * len(grid)),
    )(*args, *job.inputs)
    return res[:no], res[no:]


def _proj_dw(xnt, dproj_sh, *, tm=512, tk=2048):
    d, s = xnt.shape
    tk = _tile(s, tk)
    nk = s // tk

    def body(a_ref, b_ref, o_ref, acc):
        def finish(r):
            o_ref[0] = r

        _accumulate(acc, _dot(a_ref[...], b_ref[...]), pl.program_id(2), nk, finish)

    return pl.pallas_call(
        body, name="proj_dw", grid=(N_CHIPS, d // tm, nk),
        in_specs=[pl.BlockSpec((tm, tk), lambda j, i, q: (i, q)), pl.BlockSpec((tk, W_IN_PAD), lambda j, i, q: (q, j))],
        out_specs=pl.BlockSpec((1, tm, W_IN_PAD), lambda j, i, q: (j, i, 0)),
        out_shape=jax.ShapeDtypeStruct((N_CHIPS, d, W_IN_PAD), F32), scratch_shapes=[pltpu.VMEM((tm, W_IN_PAD), F32)],
        compiler_params=_cp(("parallel", "parallel", "arbitrary")),
    )(xnt, dproj_sh)


def _proj_dx(dproj_sh, w_sh, *, tm=1024, side=None):
    s = dproj_sh.shape[0]
    d = w_sh.shape[1]
    tm = _tile(s, tm)

    def body(a_ref, b_ref, o_ref, acc):
        kk = pl.program_id(1)
        part = _dot_nt(a_ref[...], b_ref[0])

        @pl.when(kk == 0)
        def _():
            acc[...] = part

        @pl.when(kk > 0)
        def _():
            acc[...] += part

        @pl.when(kk == N_CHIPS - 1)
        def _():
            o_ref[...] = acc[...]

    own, extra = _hosted(
        body, name="proj_dx", grid=(s // tm, N_CHIPS),
        in_specs=[pl.BlockSpec((tm, W_IN_PAD), lambda i, q: (i, q)), pl.BlockSpec((1, d, W_IN_PAD), lambda i, q: (q, 0, 0))],
        out_specs=[pl.BlockSpec((tm, d), lambda i, q: (i, 0))],
        out_shape=[jax.ShapeDtypeStruct((s, d), F32)], scratch_shapes=[pltpu.VMEM((tm, d), F32)],
        args=(dproj_sh, w_sh), sem=("parallel", "arbitrary"), side=side)
    return own[0] if side is None else (own[0], extra)


def _up_dx(dup, w_sh, *, tm=1024):
    s = dup.shape[1]
    d, wsh = w_sh.shape[1:]
    tm = _tile(s, tm)

    def body(a_ref, b_ref, o_ref, acc):
        kk = pl.program_id(1)
        part = _dot_nt(a_ref[0], b_ref[0])

        @pl.when(kk == 0)
        def _():
            acc[...] = part

        @pl.when(kk > 0)
        def _():
            acc[...] += part

        @pl.when(kk == N_CHIPS - 1)
        def _():
            o_ref[...] = acc[...]

    return pl.pallas_call(
        body, name="up_dx", grid=(s // tm, N_CHIPS),
        in_specs=[pl.BlockSpec((1, tm, wsh), lambda i, q: (q >> 1, i, q & 1)), pl.BlockSpec((1, d, wsh), lambda i, q: (q, 0, 0))],
        out_specs=pl.BlockSpec((tm, d), lambda i, q: (i, 0)),
        out_shape=jax.ShapeDtypeStruct((s, d), F32), scratch_shapes=[pltpu.VMEM((tm, d), F32)],
        compiler_params=_cp(("parallel", "arbitrary")),
    )(dup, w_sh)


def _up_dw(hnt, dup, *, tk=2048):
    d, s = hnt.shape
    wsh = 2 * D_FF // N_CHIPS
    tk = _tile(s, tk)
    nk = s // tk

    def body(a_ref, b_ref, o_ref, acc):
        def finish(r):
            o_ref[0] = r

        _accumulate(acc, _dot(a_ref[...], b_ref[0]), pl.program_id(1), nk, finish)

    return pl.pallas_call(
        body, name="up_dw", grid=(N_CHIPS, nk),
        in_specs=[pl.BlockSpec((d, tk), lambda j, q: (0, q)), pl.BlockSpec((1, tk, wsh), lambda j, q: (j >> 1, q, j & 1))],
        out_specs=pl.BlockSpec((1, d, wsh), lambda j, q: (j, 0, 0)),
        out_shape=jax.ShapeDtypeStruct((N_CHIPS, d, wsh), F32), scratch_shapes=[pltpu.VMEM((d, wsh), F32)],
        compiler_params=_cp(("parallel", "arbitrary")),
    )(hnt, dup)


BIG_ROWS =(IN_DIM // 4, Q_DIM // 4, D_INNER // 4, D_MODEL // 4, 2 * D_FF // 4, D_FF // 4)
PACK_ROWS = 5376


def _pack_shards(parts):
    rows = [p.reshape(-1, D_MODEL) for p in parts]
    pad = PACK_ROWS - sum(BIG_ROWS)
    return jnp.concatenate(rows + [jnp.zeros((pad, D_MODEL), rows[0].dtype)], axis=0)


def _unpack_shards(buf):
    out, off = [], 0
    for n in BIG_ROWS:
        out.append(buf[off:off + n])
        off += n
    return out


def _assemble(srcs, col_map, *, name, tr=256):
    arrays, lead = [], []
    for src in srcs:
        arr, j = src if isinstance(src, tuple) else (src, None)
        if not any(arr is a for a in arrays):
            arrays.append(arr)
        lead.append(([i for i, a in enumerate(arrays) if a is arr][0], j))
    rows = arrays[0].shape[-2]
    tr = _tile(rows, tr)
    out_w = len(col_map)
    tiles = []
    for t in range(out_w // 128):
        runs = []
        for lane in range(128):
            ent = col_map[t * 128 + lane]
            key = None if ent is None else (ent[0], ent[1] // 128, (lane - ent[1]) % 128)
            if runs and runs[-1][0] == key:
                runs[-1][2] = lane + 1
            else:
                runs.append([key, lane, lane + 1])
        tiles.append(runs)

    def body(*refs):
        o_ref = refs[-1]
        lane = lax.broadcasted_iota(jnp.int32, (tr, 128), 1)
        for t, runs in enumerate(tiles):
            acc = jnp.zeros((tr, 128), F32)
            for key, a, b in runs:
                if key is None:
                    continue
                sid, ct, shift = key
                ai, j = lead[sid]
                cols = slice(ct * 128, (ct + 1) * 128)
                piece = (refs[ai][:, cols] if j is None else refs[ai][j, :, cols]).astype(F32)
                if shift:
                    piece = pltpu.roll(piece, shift, 1)
                acc = piece if (a, b) == (0, 128) else jnp.where((lane >= a) & (lane < b), piece, acc)
            o_ref[:, t * 128:(t + 1) * 128] = acc.astype(BF16)

    specs = [pl.BlockSpec((tr, a.shape[1]), lambda i: (i, 0)) if a.ndim == 2
             else pl.BlockSpec((a.shape[0], tr, a.shape[2]), lambda i: (0, i, 0)) for a in arrays]
    return pl.pallas_call(
        body, name=name, grid=(rows // tr,), in_specs=specs, out_specs=pl.BlockSpec((tr, out_w), lambda i: (i, 0)),
        out_shape=jax.ShapeDtypeStruct((rows, out_w), BF16), compiler_params=_cp(("parallel",)),
    )(*arrays)


def _permute_cols_in(w):
    pad = jnp.zeros((w.shape[0], PW - IN_DIM), w.dtype)
    return jnp.concatenate([w[:, :6656], w[:, 6688:], w[:, 6656:6688], pad], axis=1)


def _unpermute_cols_in(g):
    return jnp.concatenate([g[:, :6656], g[:, O_DT:O_DT + 32], g[:, 6656:O_DT]], axis=1)


SMALL = ("norm1_w", "b_gate", "attn_sinks", "ssd_conv_b", "dt_bias", "a_log", "d_skip", "ssd_norm_w", "norm2_w",
         "ffn_conv_b", "final_norm_w", "ssd_conv_w", "ffn_conv_w")


def _pad128(v):
    v = v.reshape(-1)
    return jnp.pad(v, (0, (-v.shape[0]) % 128))


def _pack_small(parts):
    flat = jnp.concatenate([_pad128(p) for p in parts])
    flat = jnp.pad(flat, (0, (-flat.shape[0]) % 1024))
    return flat.reshape(-1, 128)


def _unpack_small(buf, shapes):
    flat, out, off = buf.reshape(-1), [], 0
    for shp in shapes:
        n = 1
        for q in shp:
            n *= q
        out.append(flat[off:off + n].reshape(shp))
        off += n + (-n) % 128
    return out


def _vec128(v):
    return jnp.pad(v.reshape(1, -1), ((0, 0), (0, 128 - v.shape[-1])))


def kernel(x, norm1_w, w_in, b_gate, attn_sinks, w_attn_o, ssd_conv_w, ssd_conv_b, dt_bias, a_log, d_skip, ssd_norm_w, w_ssd_o, w_out, norm2_w, w_up, ffn_conv_w, ffn_conv_b, w_down, final_norm_w, loss_target, m_norm1_w, m_w_in, m_b_gate, m_attn_sinks, m_w_attn_o, m_ssd_conv_w, m_ssd_conv_b, m_dt_bias, m_a_log, m_d_skip, m_ssd_norm_w, m_w_ssd_o, m_w_out, m_norm2_w, m_w_up, m_ffn_conv_w, m_ffn_conv_b, m_w_down, m_final_norm_w, v_norm1_w, v_w_in, v_b_gate, v_attn_sinks, v_w_attn_o, v_ssd_conv_w, v_ssd_conv_b, v_dt_bias, v_a_log, v_d_skip, v_ssd_norm_w, v_w_ssd_o, v_w_out, v_norm2_w, v_w_up, v_ffn_conv_w, v_ffn_conv_b, v_w_down, v_final_norm_w):
    ix, iy, ic = lax.axis_index("x"), lax.axis_index("y"), lax.axis_index("c")
    chip = 2 * ix + iy
    x2 = x[0]
    tgt = loss_target[0]
    s = x2.shape[0]

    wsh = IN_DIM // N_CHIPS
    big_shards = dict(w_in=jnp.pad(w_in[0], ((0, 0), (0, W_IN_PAD - wsh))), w_attn_o=w_attn_o[0], w_ssd_o=w_ssd_o[0],
                      w_out=w_out[0], w_up=w_up[0], w_down=w_down[0])
    gathered = {}
    (gathered["w_in"],) = _run_job(_GatherJob(("w_in",), [big_shards["w_in"]]), "gather_w_in")
    early = ("w_attn_o", "w_ssd_o", "w_out")
    gather_early = _GatherJob(early, [big_shards[n] for n in early], at=(0.0, 0.5, 0.8))
    gather_up = _GatherJob(("w_up",), [big_shards["w_up"]], at=(0.0, 0.55, 0.85))
    gather_down = _GatherJob(("w_down",), [big_shards["w_down"]], at=(0.0, 0.5, 0.8))
    gw = gathered["w_in"]
    perm = list(range(O_GA)) + list(range(O_GA + N_SSD_HEADS, IN_DIM)) + list(range(O_GA, O_GA + N_SSD_HEADS))
    w_in_p = _assemble([(gw, j) for j in range(N_CHIPS)], [divmod(o, wsh) for o in perm] + [None] * (PW - IN_DIM),
                       name="w_in_assemble")
    small_sh = _pack_small([ssd_conv_w[0], ffn_conv_w[0]])
    small_all = _all_gather_small(small_sh)
    sc_parts = [_unpack_small(small_all[j], [(4, XBC_DIM // 4), (3, 2 * D_FF // 4)]) for j in range(N_CHIPS)]
    ssd_cw = jnp.concatenate([p[0] for p in sc_parts], axis=1)
    ffn_cw = jnp.concatenate([p[1] for p in sc_parts], axis=1)

    sinks128 = _vec128(attn_sinks)
    dtb128, alog128, dskip128 = _vec128(dt_bias), _vec128(a_log), _vec128(d_skip)

    xn, xnt = _rms_fwd(x2, norm1_w, name="norm1_fwd", with_t=True)
    proj, got = _mm(xn, w_in_p, name="proj_fwd", tn=1280, side=gather_early)
    gathered.update(zip(early, got))
    qkvt = _mm(w_in_p[:, :O_Z], xnt, name="qkv_fwd", ta=True)
    attn_pre, (gathered["w_up"],) = _attn_fwd(qkvt, sinks128, side=gather_up)
    xbc = _ssd_conv_fwd(proj, ssd_cw, ssd_conv_b)
    (y_ssd, hprev), (gathered["w_down"],) = _ssd_fwd(xbc, proj, dtb128, alog128, dskip128, side=gather_down)
    full = {n: gathered[n].reshape(-1, D_MODEL) for n in ("w_attn_o", "w_ssd_o", "w_out", "w_down")}
    full["w_up"] = gathered["w_up"]
    attn = _mm(attn_pre, full["w_attn_o"], name="attn_o_fwd", ta=True)
    yn = _gate_norm_fwd(y_ssd, proj, ssd_norm_w)
    ssd_out = _mm(yn, full["w_ssd_o"], name="ssd_o_fwd")
    merged = _merge_fwd(proj, b_gate, attn, ssd_out)
    h1 = _mm(merged, full["w_out"], name="out_fwd", resid=x2)
    hn, hnt = _rms_fwd(h1, norm2_w, name="norm2_fwd", with_t=True)
    up = _mm(hn, full["w_up"], name="up_fwd")
    act = _ffn_act_fwd(up, ffn_cw, ffn_conv_b)
    h2 = _mm(act, full["w_down"], name="down_fwd", resid=h1, tk=1408)

    dh2, loss_blk, g_final = _loss_bwd(h2, tgt, final_norm_w.reshape(1, -1))
    dact = _mm(dh2, full["w_down"], name="down_dx", tb=True, tn=1408)
    g_down = _mm(act, dh2, name="down_dw", ta=True, tm=1408)
    dup, g_ffn_cw, g_ffn_cb = _ffn_act_bwd(dact, up, ffn_cw, ffn_conv_b)
    dhn = _up_dx(dup, full["w_up"])
    g_up = _up_dw(hnt, dup)
    dh1, g_norm2 = _rms_bwd(dhn, h1, norm2_w, dh2, name="norm2_bwd")
    dmerged = _mm(dh1, full["w_out"], name="out_dx", tb=True)
    g_out = _mm(merged, dh1, name="out_dw", ta=True)
    dattn, dssd_out, dga, dgs, g_ba, g_bs = _merge_bwd(dmerged, proj, b_gate, attn, ssd_out)
    dyn = _mm(dssd_out, full["w_ssd_o"], name="ssd_o_dx", tb=True)
    g_ssd_o = _mm(yn, dssd_out, name="ssd_o_dw", ta=True)
    dy_ssd, dz, g_ssd_norm = _gate_norm_bwd(dyn, y_ssd, proj, ssd_norm_w)
    slot = lambda g: g.reshape(N_CHIPS, -1, D_MODEL)
    big_grads = {}
    red = ("w_down", "w_up")
    (dxbc, ddt, dvec), got = _ssd_bwd(xbc, proj, dtb128, alog128, dskip128, hprev, dy_ssd,
                                      side=_ReduceJob(red, [slot(g_down), g_up], at=(0.0, 0.3, 0.8, 0.95)))
    big_grads.update(zip(red, got))
    dxbc_raw, g_ssd_cw, g_ssd_cb = _ssd_conv_bwd(dxbc, proj, ssd_cw, ssd_conv_b)
    dattn_pre = _mm(full["w_attn_o"], dattn, name="attn_o_dx", tb=True)
    g_attn_o = _mm(attn_pre, dattn, name="attn_o_dw")
    red = ("w_out", "w_ssd_o", "w_attn_o")
    (dq, dk, dv, dsk), got = _attn_bwd(qkvt, sinks128, attn_pre, dattn_pre,
                                       side=_ReduceJob(red, [slot(g_out), slot(g_ssd_o), slot(g_attn_o)],
                                                       at=(0.0, 0.2, 0.5, 0.7)))
    big_grads.update(zip(red, got))
    pieces = [(dq.T, Q_DIM), (dk.T, KV_DIM), (dv.T, KV_DIM), (dz, D_INNER), (dxbc_raw, XBC_DIM), (ddt, N_SSD_HEADS),
              (dga, D_MODEL), (dgs, D_MODEL)]
    orig = [(i, c) for i, (_, w) in enumerate(pieces) for c in range(w)]
    dproj_sh = _assemble([p for p, _ in pieces],
                         [orig[j * wsh + c] if c < wsh else None for j in range(N_CHIPS) for c in range(W_IN_PAD)],
                         name="dproj_assemble")
    g_in = _proj_dw(xnt, dproj_sh)
    dxn, got = _proj_dx(dproj_sh, gathered["w_in"], side=_ReduceJob(("w_in",), [g_in], at=(0.0, 0.3, 0.8, 0.95)))
    big_grads["w_in"] = got[0]
    dx, g_norm1 = _rms_bwd(dxn, x2, norm1_w, dh1, name="norm1_bwd")


    small_g = dict(
        norm1_w=g_norm1, b_gate=jnp.concatenate([g_ba, g_bs], axis=1), attn_sinks=dsk[0:1, :16], ssd_conv_b=g_ssd_cb,
        dt_bias=dvec[0:1, :32], a_log=dvec[1:2, :32], d_skip=dvec[2:3, :32], ssd_norm_w=g_ssd_norm, norm2_w=g_norm2,
        ffn_conv_b=jnp.concatenate([g_ffn_cb[0], g_ffn_cb[1]], axis=1), final_norm_w=g_final, ssd_conv_w=g_ssd_cw,
        ffn_conv_w=jnp.concatenate([g_ffn_cw[0], g_ffn_cw[1]], axis=1))
    small_buf = _pack_small([small_g[n] for n in SMALL] + [loss_blk])
    small_sum = _all_reduce_small(small_buf)
    small_shapes = [(1, D_MODEL), (1, 2 * D_MODEL), (1, 16), (1, XBC_DIM), (1, 32), (1, 32), (1, 32), (1, D_INNER),
                    (1, D_MODEL), (1, 2 * D_FF), (D_MODEL,), (4, XBC_DIM), (3, 2 * D_FF), (1, 128)]
    small_list = _unpack_small(small_sum, small_shapes)
    loss = small_list[-1][0, 0]
    grads = dict(zip(SMALL, small_list[:-1]))
    grads["ssd_conv_w"] = lax.dynamic_slice_in_dim(grads["ssd_conv_w"], chip * (XBC_DIM // 4), XBC_DIM // 4, axis=1)
    grads["ffn_conv_w"] = lax.dynamic_slice_in_dim(grads["ffn_conv_w"], chip * (2 * D_FF // 4), 2 * D_FF // 4, axis=1)
    grads.update(big_grads)

    weights = dict(norm1_w=norm1_w, w_in=w_in, b_gate=b_gate, attn_sinks=attn_sinks, w_attn_o=w_attn_o, ssd_conv_w=ssd_conv_w,
                   ssd_conv_b=ssd_conv_b, dt_bias=dt_bias, a_log=a_log, d_skip=d_skip, ssd_norm_w=ssd_norm_w, w_ssd_o=w_ssd_o,
                   w_out=w_out, norm2_w=norm2_w, w_up=w_up, ffn_conv_w=ffn_conv_w, ffn_conv_b=ffn_conv_b, w_down=w_down,
                   final_norm_w=final_norm_w)
    ms = dict(norm1_w=m_norm1_w, w_in=m_w_in, b_gate=m_b_gate, attn_sinks=m_attn_sinks, w_attn_o=m_w_attn_o,
              ssd_conv_w=m_ssd_conv_w, ssd_conv_b=m_ssd_conv_b, dt_bias=m_dt_bias, a_log=m_a_log, d_skip=m_d_skip,
              ssd_norm_w=m_ssd_norm_w, w_ssd_o=m_w_ssd_o, w_out=m_w_out, norm2_w=m_norm2_w, w_up=m_w_up,
              ffn_conv_w=m_ffn_conv_w, ffn_conv_b=m_ffn_conv_b, w_down=m_w_down, final_norm_w=m_final_norm_w)
    vs = dict(norm1_w=v_norm1_w, w_in=v_w_in, b_gate=v_b_gate, attn_sinks=v_attn_sinks, w_attn_o=v_w_attn_o,
              ssd_conv_w=v_ssd_conv_w, ssd_conv_b=v_ssd_conv_b, dt_bias=v_dt_bias, a_log=v_a_log, d_skip=v_d_skip,
              ssd_norm_w=v_ssd_norm_w, w_ssd_o=v_w_ssd_o, w_out=v_w_out, norm2_w=v_norm2_w, w_up=v_w_up,
              ffn_conv_w=v_ffn_conv_w, ffn_conv_b=v_ffn_conv_b, w_down=v_w_down, final_norm_w=v_final_norm_w)
    order = list(weights)
    deltas, new_m, new_v = {}, {}, {}
    for n in BIG:
        shp = weights[n].shape
        deltas[n], new_m[n], new_v[n], grads[n] = _adamw(weights[n], grads[n], ms[n], vs[n], name="adamw_" + n)
    smalls = [n for n in order if n not in BIG]
    as2d = lambda a: a.reshape(-1, a.shape[-1])
    res = _adamw_many(*[[as2d(src[n][0] if src[n].ndim == 3 else src[n]) for n in smalls] for src in (weights, grads, ms, vs)])
    for i, n in enumerate(smalls):
        deltas[n], new_m[n], new_v[n] = (res[q * len(smalls) + i].reshape(weights[n].shape) for q in range(3))
    out_grads = [grads[n].reshape(weights[n].shape) for n in order]
    return (loss, dx[None], *out_grads, *[deltas[n] for n in order], *[new_m[n] for n in order], *[new_v[n] for n in order])
```

```python
import functools

import jax
import jax.numpy as jnp
from jax import lax
from jax.experimental import pallas as pl
from jax.experimental.pallas import tpu as pltpu

F32 = jnp.float32
BF16 = jnp.bfloat16
HI = lax.Precision.HIGHEST

D_MODEL = 1024
Q_DIM = 1024
KV_DIM = 256
D_INNER = 2048
BC_DIM = 512
XBC_DIM = 3072
N_SSD_HEADS = 32
D_FF = 2816
IN_DIM = 8736
BLK = 128
EPS = 1e-5
NEG = -1e30

O_Q, O_K, O_V, O_Z, O_X, O_GA, O_GS, O_DT = 0, 1024, 1280, 1536, 3584, 6656, 7680, 8704
PW = 8960

ADAM_LR, ADAM_B1, ADAM_B2, ADAM_EPS, ADAM_WD, ADAM_STEP = 0.001, 0.9, 0.999, 1e-08, 0.01, 10

VMEM_LIMIT = 52 * 1024 * 1024
MESH = pl.DeviceIdType.MESH


def _cp(sem=None):
    return pltpu.CompilerParams(dimension_semantics=sem, vmem_limit_bytes=VMEM_LIMIT)


def _dot(a, b, prec=None):
    return jnp.dot(a, b, preferred_element_type=F32, precision=prec)


def _dot_nt(a, b, prec=None):
    return lax.dot_general(a, b, (((1,), (1,)), ((), ())), preferred_element_type=F32, precision=prec)


def _dot_tn(a, b, prec=None):
    return lax.dot_general(a, b, (((0,), (0,)), ((), ())), preferred_element_type=F32, precision=prec)


def _sigmoid(x):
    return 0.5 * jnp.tanh(0.5 * x) + 0.5


def _tile(n, want):
    t = min(n, want)
    while n % t:
        t -= 128
    return t


def _accumulate(acc, part, kk, nk, finish):
    if nk == 1:
        finish(part)
        return

    @pl.when(kk == 0)
    def _():
        acc[...] = part

    @pl.when(kk > 0)
    def _():
        acc[...] += part

    @pl.when(kk == nk - 1)
    def _():
        finish(acc[...])


def _mm(a, b, *, name, ta=False, tb=False, out_dtype=F32, resid=None, tm=1024, tn=1024, tk=1024, side=None):
    m, k = (a.shape[1], a.shape[0]) if ta else a.shape
    slots = b.ndim == 3
    if slots:
        n = b.shape[1] if tb else b.shape[0] * b.shape[2]
        tn, tk = (tn, b.shape[2]) if tb else (b.shape[2], tk)
    else:
        n = b.shape[0] if tb else b.shape[1]
    tm, tn, tk = _tile(m, tm), _tile(n, tn), _tile(k, tk)
    nk = k // tk
    dn = (((0 if ta else 1,), (1 if tb else 0,)), ((), ()))

    def body(*refs):
        if resid is None:
            a_ref, b_ref, o_ref, acc = refs
        else:
            a_ref, b_ref, r_ref, o_ref, acc = refs
        kk = pl.program_id(2)
        bv = b_ref[0] if slots else b_ref[...]
        part = lax.dot_general(a_ref[...].astype(BF16), bv.astype(BF16), dn, preferred_element_type=F32)

        def finish(r):
            if resid is not None:
                r = r + r_ref[...]
            o_ref[...] = r.astype(out_dtype)

        _accumulate(acc, part, kk, nk, finish)

    a_spec = pl.BlockSpec((tk, tm), lambda i, j, q: (q, i)) if ta else pl.BlockSpec((tm, tk), lambda i, j, q: (i, q))
    if slots:
        b_spec = (pl.BlockSpec((1, tn, tk), lambda i, j, q: (q, j, 0)) if tb
                  else pl.BlockSpec((1, tk, tn), lambda i, j, q: (j, q, 0)))
    else:
        b_spec = pl.BlockSpec((tn, tk), lambda i, j, q: (j, q)) if tb else pl.BlockSpec((tk, tn), lambda i, j, q: (q, j))
    o_spec = pl.BlockSpec((tm, tn), lambda i, j, q: (i, j))
    ins, specs = [a, b], [a_spec, b_spec]
    if resid is not None:
        ins.append(resid)
        specs.append(o_spec)
    own, extra = _hosted(
        body, name=name, grid=(m // tm, n // tn, nk), in_specs=specs, out_specs=[o_spec],
        out_shape=[jax.ShapeDtypeStruct((m, n), out_dtype)], scratch_shapes=[pltpu.VMEM((tm, tn), F32)],
        args=ins, sem=("parallel", "parallel", "arbitrary"), side=side)
    return own[0] if side is None else (own[0], extra)


def _rms_fwd(x, w, *, name, tm=512, with_t=False):
    s, d = x.shape
    tm = _tile(s, tm)

    def body(x_ref, w_ref, o_ref, *t_ref):
        xv = x_ref[...]
        r = lax.rsqrt(jnp.mean(xv * xv, axis=-1, keepdims=True) + EPS)
        y = (xv * r) * w_ref[...]
        o_ref[...] = y.astype(BF16)
        if with_t:
            t_ref[0][...] = y.T.astype(BF16)

    row = pl.BlockSpec((tm, d), lambda i: (i, 0))
    res = pl.pallas_call(
        body, name=name, grid=(s // tm,), in_specs=[row, pl.BlockSpec((1, d), lambda i: (0, 0))],
        out_specs=[row] + [pl.BlockSpec((d, tm), lambda i: (0, i))] * with_t,
        out_shape=[jax.ShapeDtypeStruct((s, d), BF16)] + [jax.ShapeDtypeStruct((d, s), BF16)] * with_t,
        compiler_params=_cp(("parallel",)),
    )(x, w)
    return res if with_t else res[0]


def _rms_bwd(dy, x, w, resid, *, name, tm=512):
    s, d = x.shape
    tm = _tile(s, tm)

    def body(dy_ref, x_ref, w_ref, r_ref, dx_ref, dw_ref):
        i = pl.program_id(0)
        xv = x_ref[...]
        r = lax.rsqrt(jnp.mean(xv * xv, axis=-1, keepdims=True) + EPS)
        xh = xv * r
        dyv = dy_ref[...]
        g = dyv * w_ref[...]
        dx_ref[...] = r_ref[...] + r * (g - xh * jnp.mean(g * xh, axis=-1, keepdims=True))
        part = jnp.sum(dyv * xh, axis=0, keepdims=True)

        @pl.when(i == 0)
        def _():
            dw_ref[...] = part

        @pl.when(i > 0)
        def _():
            dw_ref[...] += part

    row = pl.BlockSpec((tm, d), lambda i: (i, 0))
    vec = pl.BlockSpec((1, d), lambda i: (0, 0))
    return pl.pallas_call(
        body, name=name, grid=(s // tm,), in_specs=[row, row, vec, row], out_specs=[row, vec],
        out_shape=[jax.ShapeDtypeStruct((s, d), F32), jax.ShapeDtypeStruct((1, d), F32)],
        compiler_params=_cp(("arbitrary",)),
    )(dy, x, w, resid)


def _loss_bwd(h2, tgt, wf, *, tm=512):
    s, d = h2.shape
    tm = _tile(s, tm)

    def body(h_ref, t_ref, w_ref, dh_ref, loss_ref, dw_ref):
        i = pl.program_id(0)
        hv = h_ref[...]
        r = lax.rsqrt(jnp.mean(hv * hv, axis=-1, keepdims=True) + EPS)
        xh = hv * r
        wv = w_ref[...]
        e = xh * wv - t_ref[...]
        lpart = 0.5 * jnp.sum(jnp.mean(e * e, axis=-1, keepdims=True), axis=0, keepdims=True)
        dout = e * (1.0 / d)
        g = dout * wv
        dh_ref[...] = r * (g - xh * jnp.mean(g * xh, axis=-1, keepdims=True))
        part = jnp.sum(dout * xh, axis=0, keepdims=True)
        lrow = jnp.broadcast_to(lpart, (1, 128))

        @pl.when(i == 0)
        def _():
            dw_ref[...] = part
            loss_ref[...] = lrow

        @pl.when(i > 0)
        def _():
            dw_ref[...] += part
            loss_ref[...] += lrow

    row = pl.BlockSpec((tm, d), lambda i: (i, 0))
    vec = pl.BlockSpec((1, d), lambda i: (0, 0))
    return pl.pallas_call(
        body, name="loss_bwd", grid=(s // tm,), in_specs=[row, row, vec],
        out_specs=[row, pl.BlockSpec((1, 128), lambda i: (0, 0)), vec],
        out_shape=[jax.ShapeDtypeStruct((s, d), F32), jax.ShapeDtypeStruct((1, 128), F32),
                   jax.ShapeDtypeStruct((1, d), F32)],
        compiler_params=_cp(("arbitrary",)),
    )(h2, tgt, wf)


def _attn_mask(n):
    si = lax.broadcasted_iota(jnp.int32, (2 * BLK, 4 * BLK), 0)
    qi = lax.broadcasted_iota(jnp.int32, (2 * BLK, 4 * BLK), 1) & (BLK - 1)
    dist = BLK + qi - si
    kpos = n * BLK - BLK + si
    return (dist >= 0) & (dist < BLK) & (kpos >= 0)


def _attn_probs(q_ref, kc_ref, kp_ref, sk_ref, kvh, valid):
    rows = slice(kvh * 64, (kvh + 1) * 64)
    kt = jnp.concatenate([kp_ref[rows, :], kc_ref[rows, :]], axis=1).astype(BF16)
    qt = jnp.concatenate([q_ref[(kvh * 4 + g) * 64:(kvh * 4 + g + 1) * 64, :] for g in range(4)], axis=1).astype(BF16)
    s = _dot_tn(kt, qt) * 0.125
    s = jnp.where(valid, s, NEG)
    head = lax.broadcasted_iota(jnp.int32, (1, 4 * BLK), 1) >> 7
    sink = jnp.zeros((1, 4 * BLK), F32)
    for g in range(4):
        sink = jnp.where(head == g, sk_ref[0:1, kvh * 4 + g:kvh * 4 + g + 1], sink)
    m = jnp.maximum(jnp.max(s, axis=0, keepdims=True), sink)
    p = jnp.where(valid, jnp.exp(s - m), 0.0)
    es = jnp.exp(sink - m)
    inv = 1.0 / (jnp.sum(p, axis=0, keepdims=True) + es)
    return qt, kt, p * inv, es * inv


def _attn_in_specs(cur, prev):
    return [pl.BlockSpec((Q_DIM, BLK), lambda n: (0, cur(n))),
            pl.BlockSpec((KV_DIM, BLK), lambda n: (O_K // KV_DIM, cur(n))),
            pl.BlockSpec((KV_DIM, BLK), lambda n: (O_K // KV_DIM, prev(n))),
            pl.BlockSpec((KV_DIM, BLK), lambda n: (O_V // KV_DIM, cur(n))),
            pl.BlockSpec((KV_DIM, BLK), lambda n: (O_V // KV_DIM, prev(n))),
            pl.BlockSpec((1, 128), lambda n: (0, 0))]


def _attn_fwd(qkvt, sinks, side=None):
    s = qkvt.shape[1]
    nb = s // BLK

    def body(q_ref, kc_ref, kp_ref, vc_ref, vp_ref, sk_ref, o_ref):
        valid = _attn_mask(pl.program_id(0))
        for kvh in range(4):
            rows = slice(kvh * 64, (kvh + 1) * 64)
            _, _, probs, _ = _attn_probs(q_ref, kc_ref, kp_ref, sk_ref, kvh, valid)
            vt = jnp.concatenate([vp_ref[rows, :], vc_ref[rows, :]], axis=1).astype(BF16)
            o = _dot(vt, probs.astype(BF16))
            for g in range(4):
                h = kvh * 4 + g
                o_ref[h * 64:(h + 1) * 64, :] = o[:, g * BLK:(g + 1) * BLK].astype(BF16)

    own, extra = _hosted(
        body, name="attn_fwd", grid=(nb,), in_specs=_attn_in_specs(lambda n: n, lambda n: jnp.maximum(n - 1, 0)),
        out_specs=[pl.BlockSpec((Q_DIM, BLK), lambda n: (0, n))],
        out_shape=[jax.ShapeDtypeStruct((Q_DIM, s), BF16)], scratch_shapes=[],
        args=(qkvt, qkvt, qkvt, qkvt, qkvt, sinks), sem=("parallel",), side=side)
    return own[0] if side is None else (own[0], extra)


def _attn_bwd(qkvt, sinks, o, do, side=None):
    s = qkvt.shape[1]
    nb = s // BLK

    def body(q_ref, kc_ref, kp_ref, vc_ref, vp_ref, sk_ref, o_ref, do_ref, dq_ref, dk_ref, dv_ref, dsk_ref, ck, cv, nk, nv):
        n = pl.program_id(0)

        @pl.when(n == 0)
        def _():
            ck[...] = jnp.zeros_like(ck)
            cv[...] = jnp.zeros_like(cv)
            dsk_ref[...] = jnp.zeros_like(dsk_ref)

        @pl.when(n < nb)
        def _():
            valid = _attn_mask(n)
            lane = lax.broadcasted_iota(jnp.int32, (1, 128), 1)
            dsk = jnp.zeros((1, 128), F32)
            for kvh in range(4):
                rows = slice(kvh * 64, (kvh + 1) * 64)
                qt, kt, probs, psink = _attn_probs(q_ref, kc_ref, kp_ref, sk_ref, kvh, valid)
                vt = jnp.concatenate([vp_ref[rows, :], vc_ref[rows, :]], axis=1).astype(BF16)
                heads = [slice((kvh * 4 + g) * 64, (kvh * 4 + g + 1) * 64) for g in range(4)]
                dot = jnp.concatenate([do_ref[hh, :] for hh in heads], axis=1)
                ot = jnp.concatenate([o_ref[hh, :] for hh in heads], axis=1).astype(F32)
                delta = jnp.sum(dot * ot, axis=0, keepdims=True)
                dot16 = dot.astype(BF16)
                dp = _dot_tn(vt, dot16)
                ds = (probs * (dp - delta) * 0.125).astype(BF16)
                dqt = _dot(kt, ds)
                nk[rows, :] = _dot_nt(qt, ds)
                nv[rows, :] = _dot_nt(dot16, probs.astype(BF16))
                sd = psink * delta
                for g in range(4):
                    dq_ref[heads[g], :] = dqt[:, g * BLK:(g + 1) * BLK].astype(BF16)
                    val = -jnp.sum(sd[:, g * BLK:(g + 1) * BLK], axis=1, keepdims=True)
                    dsk = dsk + jnp.where(lane == kvh * 4 + g, val, 0.0)
            dsk_ref[0:1, :] += dsk
            dk_ref[...] = (ck[...] + nk[:, :BLK]).astype(BF16)
            dv_ref[...] = (cv[...] + nv[:, :BLK]).astype(BF16)
            ck[...] = nk[:, BLK:]
            cv[...] = nv[:, BLK:]

        @pl.when(n == nb)
        def _():
            dk_ref[...] = ck[...].astype(BF16)
            dv_ref[...] = cv[...].astype(BF16)

    cur = lambda n: jnp.minimum(n, nb - 1)
    prev = lambda n: jnp.maximum(jnp.minimum(n, nb - 1) - 1, 0)
    outb = lambda n: jnp.maximum(n - 1, 0)
    own, extra = _hosted(
        body, name="attn_bwd", grid=(nb + 1,),
        in_specs=_attn_in_specs(cur, prev) + [pl.BlockSpec((Q_DIM, BLK), lambda n: (0, cur(n))),
                                              pl.BlockSpec((Q_DIM, BLK), lambda n: (0, cur(n)))],
        out_specs=[pl.BlockSpec((Q_DIM, BLK), lambda n: (0, cur(n))),
                   pl.BlockSpec((KV_DIM, BLK), lambda n: (0, outb(n))),
                   pl.BlockSpec((KV_DIM, BLK), lambda n: (0, outb(n))),
                   pl.BlockSpec((8, 128), lambda n: (0, 0))],
        out_shape=[jax.ShapeDtypeStruct((Q_DIM, s), BF16), jax.ShapeDtypeStruct((KV_DIM, s), BF16),
                   jax.ShapeDtypeStruct((KV_DIM, s), BF16), jax.ShapeDtypeStruct((8, 128), F32)],
        scratch_shapes=[pltpu.VMEM((KV_DIM, BLK), F32)] * 2 + [pltpu.VMEM((KV_DIM, 2 * BLK), F32)] * 2,
        args=(qkvt, qkvt, qkvt, qkvt, qkvt, sinks, o, do), sem=("arbitrary",), side=side)
    return own if side is None else (own, extra)


def _shift_down(x, j):
    if j == 0:
        return x
    row = lax.broadcasted_iota(jnp.int32, x.shape, 0)
    return jnp.where(row >= j, pltpu.roll(x, j, 0), 0.0)


def _shift_up(x, j):
    if j == 0:
        return x
    s = x.shape[0]
    row = lax.broadcasted_iota(jnp.int32, x.shape, 0)
    return jnp.where(row < s - j, pltpu.roll(x, s - j, 0), 0.0)


def _conv(x, w_ref, b_ref):
    kk = w_ref.shape[0]
    y = _shift_down(x, kk - 1) * w_ref[0:1, :]
    for q in range(1, kk):
        y = y + _shift_down(x, kk - 1 - q) * w_ref[q:q + 1, :]
    return y + b_ref[...]


def _conv_bwd(dy, x, w_ref, dx_dtype):
    kk = w_ref.shape[0]
    dx = _shift_up(dy, kk - 1) * w_ref[0:1, :]
    dws = [jnp.sum(dy * _shift_down(x, kk - 1), axis=0, keepdims=True)]
    for q in range(1, kk):
        dx = dx + _shift_up(dy, kk - 1 - q) * w_ref[q:q + 1, :]
        dws.append(jnp.sum(dy * _shift_down(x, kk - 1 - q), axis=0, keepdims=True))
    return dx.astype(dx_dtype), dws, jnp.sum(dy, axis=0, keepdims=True)


def _dsilu(y, sg):
    return sg * (1.0 + y * (1.0 - sg))


CT = 256


def _ssd_conv_fwd(proj, w, b):
    s = proj.shape[0]

    def body(x_ref, w_ref, b_ref, o_ref):
        y = _conv(x_ref[...], w_ref, b_ref)
        o_ref[...] = y * _sigmoid(y)

    return pl.pallas_call(
        body, name="ssd_conv_fwd", grid=(XBC_DIM // CT,),
        in_specs=[pl.BlockSpec((s, CT), lambda i: (0, O_X // CT + i)), pl.BlockSpec((4, CT), lambda i: (0, i)),
                  pl.BlockSpec((1, CT), lambda i: (0, i))],
        out_specs=pl.BlockSpec((s, CT), lambda i: (0, i)),
        out_shape=jax.ShapeDtypeStruct((s, XBC_DIM), F32), compiler_params=_cp(("parallel",)),
    )(proj, w, b)


def _ssd_conv_bwd(dact, proj, w, b):
    s = proj.shape[0]

    def body(d_ref, x_ref, w_ref, b_ref, dx_ref, dw_ref, db_ref):
        x = x_ref[...]
        y = _conv(x, w_ref, b_ref)
        dy = d_ref[...] * _dsilu(y, _sigmoid(y))
        dx, dws, db = _conv_bwd(dy, x, w_ref, BF16)
        dx_ref[...] = dx
        for q in range(4):
            dw_ref[q:q + 1, :] = dws[q]
        db_ref[...] = db

    return pl.pallas_call(
        body, name="ssd_conv_bwd", grid=(XBC_DIM // CT,),
        in_specs=[pl.BlockSpec((s, CT), lambda i: (0, i)), pl.BlockSpec((s, CT), lambda i: (0, O_X // CT + i)),
                  pl.BlockSpec((4, CT), lambda i: (0, i)), pl.BlockSpec((1, CT), lambda i: (0, i))],
        out_specs=[pl.BlockSpec((s, CT), lambda i: (0, i)), pl.BlockSpec((4, CT), lambda i: (0, i)),
                   pl.BlockSpec((1, CT), lambda i: (0, i))],
        out_shape=[jax.ShapeDtypeStruct((s, XBC_DIM), BF16), jax.ShapeDtypeStruct((4, XBC_DIM), F32),
                   jax.ShapeDtypeStruct((1, XBC_DIM), F32)],
        compiler_params=_cp(("parallel",)),
    )(dact, proj, w, b)


NFT = D_FF // CT


def _ffn_act_fwd(up, w, b):
    s = up.shape[0]

    def body(v_ref, g_ref, wv_ref, wg_ref, bv_ref, bg_ref, o_ref):
        val = _conv(v_ref[...], wv_ref, bv_ref)
        gt = _conv(g_ref[...], wg_ref, bg_ref)
        o_ref[...] = ((gt * _sigmoid(gt)) * val).astype(BF16)

    col = lambda off: (lambda i: (0, off + i))
    return pl.pallas_call(
        body, name="ffn_act_fwd", grid=(NFT,),
        in_specs=[pl.BlockSpec((s, CT), col(0)), pl.BlockSpec((s, CT), col(NFT)),
                  pl.BlockSpec((3, CT), col(0)), pl.BlockSpec((3, CT), col(NFT)),
                  pl.BlockSpec((1, CT), col(0)), pl.BlockSpec((1, CT), col(NFT))],
        out_specs=pl.BlockSpec((s, CT), col(0)),
        out_shape=jax.ShapeDtypeStruct((s, D_FF), BF16), compiler_params=_cp(("parallel",)),
    )(up, up, w, w, b, b)


def _ffn_act_bwd(dact, up, w, b):
    s = up.shape[0]

    def body(d_ref, v_ref, g_ref, wv_ref, wg_ref, bv_ref, bg_ref, dx_ref, dw_ref, db_ref):
        xv, xg = v_ref[...], g_ref[...]
        val = _conv(xv, wv_ref, bv_ref)
        gt = _conv(xg, wg_ref, bg_ref)
        sg = _sigmoid(gt)
        d = d_ref[...]
        for half, (dy, x, w_ref) in enumerate(((d * (gt * sg), xv, wv_ref), (d * val * _dsilu(gt, sg), xg, wg_ref))):
            dx, dws, db = _conv_bwd(dy, x, w_ref, BF16)
            dx_ref[half] = dx
            for q in range(3):
                dw_ref[half, q:q + 1, :] = dws[q]
            db_ref[half] = db

    col = lambda off: (lambda i: (0, off + i))
    both = lambda i: (0, 0, i)
    return pl.pallas_call(
        body, name="ffn_act_bwd", grid=(NFT,),
        in_specs=[pl.BlockSpec((s, CT), col(0)), pl.BlockSpec((s, CT), col(0)), pl.BlockSpec((s, CT), col(NFT)),
                  pl.BlockSpec((3, CT), col(0)), pl.BlockSpec((3, CT), col(NFT)),
                  pl.BlockSpec((1, CT), col(0)), pl.BlockSpec((1, CT), col(NFT))],
        out_specs=[pl.BlockSpec((2, s, CT), both), pl.BlockSpec((2, 3, CT), both), pl.BlockSpec((2, 1, CT), both)],
        out_shape=[jax.ShapeDtypeStruct((2, s, D_FF), BF16), jax.ShapeDtypeStruct((2, 3, D_FF), F32),
                   jax.ShapeDtypeStruct((2, 1, D_FF), F32)],
        compiler_params=_cp(("parallel",)),
    )(dact, up, up, w, w, b, b)


def _expand_mat():
    r = lax.broadcasted_iota(jnp.int32, (128, D_INNER), 0)
    c = lax.broadcasted_iota(jnp.int32, (128, D_INNER), 1)
    return ((c >> 6) == r).astype(BF16)


def _reduce_mat():
    r = lax.broadcasted_iota(jnp.int32, (D_INNER, 128), 0)
    c = lax.broadcasted_iota(jnp.int32, (D_INNER, 128), 1)
    return ((r >> 6) == c).astype(BF16)


def _split(v, parts):
    out = []
    for _ in range(parts - 1):
        p = v.astype(BF16)
        out.append(p)
        v = v - p.astype(F32)
    out.append(v.astype(BF16))
    return out


def _sel_dot(v, sel, parts):
    acc = None
    for p in reversed(_split(v, parts)):
        t = _dot(p, sel)
        acc = t if acc is None else acc + t
    return acc


def _row8(v):
    return jnp.broadcast_to(v, (8, v.shape[1]))


def _tril():
    r = lax.broadcasted_iota(jnp.int32, (BLK, BLK), 0)
    c = lax.broadcasted_iota(jnp.int32, (BLK, BLK), 1)
    return r >= c


def _softplus(x):
    return jnp.maximum(x, 0.0) + jnp.log(1.0 + jnp.exp(-jnp.abs(x)))


def _ssd_common(dtraw_ref, dtb_ref, alog_ref):
    causal = _tril()
    e_mat = _expand_mat()
    a_neg = -jnp.exp(alog_ref[...])
    dt = _softplus(dtraw_ref[...] + dtb_ref[...])
    a_cs = _dot(causal.astype(F32), dt * a_neg, HI)
    a_cs_t = a_cs.T
    dt_x = _sel_dot(dt, e_mat, 3)
    acs_x = _sel_dot(a_cs, e_mat, 3)
    alast_x = acs_x[BLK - 1:BLK, :]
    ea_x = jnp.exp(acs_x)
    ds_x = jnp.exp(alast_x - acs_x)
    elast_x = jnp.exp(alast_x)
    return causal, e_mat, a_neg, dt, a_cs, a_cs_t, dt_x, ea_x, ds_x, elast_x


def _decay(a_cs, a_cs_t, h, causal):
    seg = a_cs[:, h:h + 1] - a_cs_t[h:h + 1, :]
    return jnp.where(causal, jnp.exp(jnp.where(causal, seg, 0.0)), 0.0)


def _ssd_fwd(xbc, proj, dt_bias, a_log, d_skip, side=None):
    s = xbc.shape[0]
    nc = s // BLK

    def body(xs_ref, b_ref, c_ref, dtraw_ref, dtb_ref, alog_ref, dskip_ref, y_ref, hp_ref, h_scr, xc16):
        @pl.when(pl.program_id(0) == 0)
        def _():
            h_scr[...] = jnp.zeros_like(h_scr)

        causal, e_mat, _, _, a_cs, a_cs_t, dt_x, ea_x, ds_x, elast_x = _ssd_common(dtraw_ref, dtb_ref, alog_ref)
        dskip_x = _sel_dot(_row8(dskip_ref[...]), e_mat, 3)[0:1]
        xs = xs_ref[...]
        xc = xs * dt_x
        xc16[...] = xc.astype(BF16)
        xcd = (xc * ds_x).astype(BF16)
        hp_ref[0] = h_scr[...]
        for g in range(4):
            gs = slice(g * 512, (g + 1) * 512)
            cg = c_ref[:, g * 128:(g + 1) * 128].astype(BF16)
            bg = b_ref[:, g * 128:(g + 1) * 128].astype(BF16)
            cb = _dot_nt(cg, bg)
            hg = h_scr[:, gs]
            yoff = _dot(cg, hg.astype(BF16)) * ea_x[:, gs]
            for j in range(8):
                h = g * 8 + j
                hsl = slice(h * 64, (h + 1) * 64)
                mm = (cb * _decay(a_cs, a_cs_t, h, causal)).astype(BF16)
                y_ref[:, hsl] = _dot(mm, xc16[:, hsl])
            y_ref[:, gs] += yoff + xs[:, gs] * dskip_x[:, gs]
            h_scr[:, gs] = hg * elast_x[:, gs] + _dot_tn(bg, xcd[:, gs])

    vec = pl.BlockSpec((1, 128), lambda c: (0, 0))
    own, extra = _hosted(
        body, name="ssd_fwd", grid=(nc,),
        in_specs=[pl.BlockSpec((BLK, D_INNER), lambda c: (c, 0)),
                  pl.BlockSpec((BLK, BC_DIM), lambda c: (c, D_INNER // BC_DIM)),
                  pl.BlockSpec((BLK, BC_DIM), lambda c: (c, D_INNER // BC_DIM + 1)),
                  pl.BlockSpec((BLK, 128), lambda c: (c, O_DT // 128)), vec, vec, vec],
        out_specs=[pl.BlockSpec((BLK, D_INNER), lambda c: (c, 0)),
                   pl.BlockSpec((1, 128, D_INNER), lambda c: (c, 0, 0))],
        out_shape=[jax.ShapeDtypeStruct((s, D_INNER), F32), jax.ShapeDtypeStruct((nc, 128, D_INNER), F32)],
        scratch_shapes=[pltpu.VMEM((128, D_INNER), F32), pltpu.VMEM((BLK, D_INNER), BF16)],
        args=(xbc, xbc, xbc, proj, dt_bias, a_log, d_skip), sem=("arbitrary",), side=side)
    return own if side is None else (own, extra)


def _ssd_bwd(xbc, proj, dt_bias, a_log, d_skip, hprev, dy, side=None):
    s = xbc.shape[0]
    nc = s // BLK

    def body(xs_ref, b_ref, c_ref, dtraw_ref, dtb_ref, alog_ref, dskip_ref, hp_ref, dy_ref,
             dxbc_ref, ddt_ref, dvec_ref, dh_scr, xc16, dy16, dxc_scr, dacs_r, tdiff):
        step = pl.program_id(0)
        dacs_r[...] = jnp.zeros_like(dacs_r)

        @pl.when(step == 0)
        def _():
            dh_scr[...] = jnp.zeros_like(dh_scr)
            dvec_ref[...] = jnp.zeros_like(dvec_ref)

        causal, e_mat, a_neg, dt, a_cs, a_cs_t, dt_x, ea_x, ds_x, elast_x = _ssd_common(dtraw_ref, dtb_ref, alog_ref)
        r_mat = _reduce_mat()
        lane = lax.broadcasted_iota(jnp.int32, (1, 128), 1)
        dskip_x = _sel_dot(_row8(dskip_ref[...]), e_mat, 3)[0:1]
        xs = xs_ref[...]
        dy = dy_ref[...]
        xc = xs * dt_x
        xcd = xc * ds_x
        xc16[...] = xc.astype(BF16)
        dy16[...] = dy.astype(BF16)
        dyea = dy * ea_x
        dh = dh_scr[...]
        hp = hp_ref[0]
        dalast_x = jnp.sum(dh * hp, axis=0, keepdims=True) * elast_x
        dacs = jnp.zeros((BLK, 128), F32)
        for g in range(4):
            gs = slice(g * 512, (g + 1) * 512)
            bsl = slice(g * 128, (g + 1) * 128)
            cg = c_ref[:, bsl].astype(BF16)
            bg = b_ref[:, bsl].astype(BF16)
            cb = _dot_nt(cg, bg)
            hg16 = hp[:, gs].astype(BF16)
            dhg16 = dh[:, gs].astype(BF16)
            raw = _dot(cg, hg16)
            draw16 = dyea[:, gs].astype(BF16)
            dcg = _dot_nt(draw16, hg16)
            dhp_g = _dot_tn(cg, draw16)
            dbg = _dot_nt(xcd[:, gs].astype(BF16), dhg16)
            dxcd = _dot(bg, dhg16)
            dcb = jnp.zeros((BLK, BLK), F32)
            for j in range(8):
                h = g * 8 + j
                hsl = slice(h * 64, (h + 1) * 64)
                decay = _decay(a_cs, a_cs_t, h, causal)
                m = cb * decay
                dm = _dot_nt(dy16[:, hsl], xc16[:, hsl])
                dxc_scr[:, hsl] = _dot_tn(m.astype(BF16), dy16[:, hsl])
                dcb = dcb + dm * decay
                dseg = dm * m
                oneh = jnp.where(lane == h, 1.0, 0.0)
                dacs = dacs + jnp.sum(dseg, axis=1, keepdims=True) * oneh
                dacs_r[h:h + 1, :] = jnp.sum(dseg, axis=0, keepdims=True)
            dcb16 = dcb.astype(BF16)
            dcg = dcg + _dot(dcb16, bg)
            dbg = dbg + _dot_tn(dcb16, cg)
            dxbc_ref[:, D_INNER + g * 128:D_INNER + (g + 1) * 128] = dbg
            dxbc_ref[:, D_INNER + BC_DIM + g * 128:D_INNER + BC_DIM + (g + 1) * 128] = dcg
            dxc_scr[:, gs] += dxcd * ds_x[:, gs]
            dh_scr[:, gs] = dh[:, gs] * elast_x[:, gs] + dhp_g
            tst = dxcd * xcd[:, gs]
            tdiff[:, gs] = dy[:, gs] * (raw * ea_x[:, gs]) - tst
            tdiff[BLK - 1:BLK, gs] += jnp.sum(tst, axis=0, keepdims=True)
        dxc = dxc_scr[...]
        row = lax.broadcasted_iota(jnp.int32, (BLK, D_INNER), 0)
        tfull = tdiff[...] + jnp.where(row == BLK - 1, dalast_x, 0.0)
        dacs = dacs + _sel_dot(tfull, r_mat, 2) - dacs_r[...].T
        da = _dot_tn(causal.astype(F32), dacs, HI)
        ddt = da * a_neg + _sel_dot(dxc * xs, r_mat, 2)
        lmask = lax.broadcasted_iota(jnp.int32, (BLK, 128), 1) < N_SSD_HEADS
        ddtraw = jnp.where(lmask, ddt * _sigmoid(dtraw_ref[...] + dtb_ref[...]), 0.0)
        ddt_ref[...] = ddtraw.astype(BF16)
        dxbc_ref[:, 0:D_INNER] = dy * dskip_x + dxc * dt_x
        dvec_ref[0:1, :] += jnp.sum(ddtraw, axis=0, keepdims=True)
        dvec_ref[1:2, :] += jnp.where(lane < N_SSD_HEADS, jnp.sum(da * dt, axis=0, keepdims=True) * a_neg, 0.0)
        dvec_ref[2:3, :] += _sel_dot(_row8(jnp.sum(dy * xs, axis=0, keepdims=True)), r_mat, 3)[0:1]

    rev = lambda c: nc - 1 - c
    vec = pl.BlockSpec((1, 128), lambda c: (0, 0))
    own, extra = _hosted(
        body, name="ssd_bwd", grid=(nc,),
        in_specs=[pl.BlockSpec((BLK, D_INNER), lambda c: (rev(c), 0)),
                  pl.BlockSpec((BLK, BC_DIM), lambda c: (rev(c), D_INNER // BC_DIM)),
                  pl.BlockSpec((BLK, BC_DIM), lambda c: (rev(c), D_INNER // BC_DIM + 1)),
                  pl.BlockSpec((BLK, 128), lambda c: (rev(c), O_DT // 128)), vec, vec, vec,
                  pl.BlockSpec((1, 128, D_INNER), lambda c: (rev(c), 0, 0)),
                  pl.BlockSpec((BLK, D_INNER), lambda c: (rev(c), 0))],
        out_specs=[pl.BlockSpec((BLK, XBC_DIM), lambda c: (rev(c), 0)),
                   pl.BlockSpec((BLK, 128), lambda c: (rev(c), 0)),
                   pl.BlockSpec((8, 128), lambda c: (0, 0))],
        out_shape=[jax.ShapeDtypeStruct((s, XBC_DIM), F32), jax.ShapeDtypeStruct((s, 128), BF16),
                   jax.ShapeDtypeStruct((8, 128), F32)],
        scratch_shapes=[pltpu.VMEM((128, D_INNER), F32), pltpu.VMEM((BLK, D_INNER), BF16),
                        pltpu.VMEM((BLK, D_INNER), BF16), pltpu.VMEM((BLK, D_INNER), F32),
                        pltpu.VMEM((128, BLK), F32), pltpu.VMEM((BLK, D_INNER), F32)],
        args=(xbc, xbc, xbc, proj, dt_bias, a_log, d_skip, hprev, dy), sem=("arbitrary",), side=side)
    return own if side is None else (own, extra)


GW = 512


def _gate_norm_fwd(y, proj, wn, *, tm=512):
    s = y.shape[0]
    tm = _tile(s, tm)

    def body(y_ref, z_ref, w_ref, o_ref):
        z = z_ref[...]
        y2 = y_ref[...] * (z * _sigmoid(z))
        r = lax.rsqrt(jnp.mean(y2 * y2, axis=-1, keepdims=True) + EPS)
        o_ref[...] = ((y2 * r) * w_ref[...]).astype(BF16)

    return pl.pallas_call(
        body, name="gate_norm_fwd", grid=(s // tm, 4),
        in_specs=[pl.BlockSpec((tm, GW), lambda i, g: (i, g)), pl.BlockSpec((tm, GW), lambda i, g: (i, O_Z // GW + g)),
                  pl.BlockSpec((1, GW), lambda i, g: (0, g))],
        out_specs=pl.BlockSpec((tm, GW), lambda i, g: (i, g)),
        out_shape=jax.ShapeDtypeStruct((s, D_INNER), BF16), compiler_params=_cp(("parallel", "parallel")),
    )(y, proj, wn)


def _gate_norm_bwd(dyn, y, proj, wn, *, tm=512):
    s = y.shape[0]
    tm = _tile(s, tm)

    def body(d_ref, y_ref, z_ref, w_ref, dy_ref, dz_ref, dw_ref):
        i = pl.program_id(1)
        z = z_ref[...]
        sg = _sigmoid(z)
        sz = z * sg
        yv = y_ref[...]
        y2 = yv * sz
        r = lax.rsqrt(jnp.mean(y2 * y2, axis=-1, keepdims=True) + EPS)
        xh = y2 * r
        dv = d_ref[...]
        g = dv * w_ref[...]
        dy2 = r * (g - xh * jnp.mean(g * xh, axis=-1, keepdims=True))
        dy_ref[...] = dy2 * sz
        dz_ref[...] = (dy2 * yv * _dsilu(z, sg)).astype(BF16)
        part = jnp.sum(dv * xh, axis=0, keepdims=True)

        @pl.when(i == 0)
        def _():
            dw_ref[...] = part

        @pl.when(i > 0)
        def _():
            dw_ref[...] += part

    blk = pl.BlockSpec((tm, GW), lambda g, i: (i, g))
    vec = pl.BlockSpec((1, GW), lambda g, i: (0, g))
    return pl.pallas_call(
        body, name="gate_norm_bwd", grid=(4, s // tm),
        in_specs=[blk, blk, pl.BlockSpec((tm, GW), lambda g, i: (i, O_Z // GW + g)), vec],
        out_specs=[blk, blk, vec],
        out_shape=[jax.ShapeDtypeStruct((s, D_INNER), F32), jax.ShapeDtypeStruct((s, D_INNER), BF16),
                   jax.ShapeDtypeStruct((1, D_INNER), F32)],
        compiler_params=_cp(("parallel", "arbitrary")),
    )(dyn, y, proj, wn)


def _merge_fwd(proj, b_gate, attn, ssd_out, *, tm=512):
    s = attn.shape[0]
    tm = _tile(s, tm)

    def body(ga_ref, gs_ref, ba_ref, bs_ref, a_ref, s_ref, o_ref):
        ga = _sigmoid(ga_ref[...] + ba_ref[...])
        gs = _sigmoid(gs_ref[...] + bs_ref[...])
        o_ref[...] = (ga * a_ref[...] + gs * s_ref[...]).astype(BF16)

    blk = pl.BlockSpec((tm, GW), lambda i, j: (i, j))
    return pl.pallas_call(
        body, name="merge_fwd", grid=(s // tm, 2),
        in_specs=[pl.BlockSpec((tm, GW), lambda i, j: (i, O_GA // GW + j)),
                  pl.BlockSpec((tm, GW), lambda i, j: (i, O_GS // GW + j)),
                  pl.BlockSpec((1, GW), lambda i, j: (0, j)), pl.BlockSpec((1, GW), lambda i, j: (0, 2 + j)), blk, blk],
        out_specs=blk, out_shape=jax.ShapeDtypeStruct((s, D_MODEL), BF16),
        compiler_params=_cp(("parallel", "parallel")),
    )(proj, proj, b_gate, b_gate, attn, ssd_out)


def _merge_bwd(dm, proj, b_gate, attn, ssd_out, *, tm=512):
    s = attn.shape[0]
    tm = _tile(s, tm)

    def body(d_ref, ga_ref, gs_ref, ba_ref, bs_ref, a_ref, s_ref, da_ref, ds_ref, dga_ref, dgs_ref, dba_ref, dbs_ref):
        i = pl.program_id(1)
        ga = _sigmoid(ga_ref[...] + ba_ref[...])
        gs = _sigmoid(gs_ref[...] + bs_ref[...])
        d = d_ref[...]
        da_ref[...] = (d * ga).astype(BF16)
        ds_ref[...] = (d * gs).astype(BF16)
        dga = d * a_ref[...] * (ga * (1.0 - ga))
        dgs = d * s_ref[...] * (gs * (1.0 - gs))
        dga_ref[...] = dga.astype(BF16)
        dgs_ref[...] = dgs.astype(BF16)
        pa = jnp.sum(dga, axis=0, keepdims=True)
        ps = jnp.sum(dgs, axis=0, keepdims=True)

        @pl.when(i == 0)
        def _():
            dba_ref[...] = pa
            dbs_ref[...] = ps

        @pl.when(i > 0)
        def _():
            dba_ref[...] += pa
            dbs_ref[...] += ps

    blk = pl.BlockSpec((tm, GW), lambda j, i: (i, j))
    vec = pl.BlockSpec((1, GW), lambda j, i: (0, j))
    sd = jax.ShapeDtypeStruct((s, D_MODEL), BF16)
    vd = jax.ShapeDtypeStruct((1, D_MODEL), F32)
    return pl.pallas_call(
        body, name="merge_bwd", grid=(2, s // tm),
        in_specs=[blk, pl.BlockSpec((tm, GW), lambda j, i: (i, O_GA // GW + j)),
                  pl.BlockSpec((tm, GW), lambda j, i: (i, O_GS // GW + j)),
                  vec, pl.BlockSpec((1, GW), lambda j, i: (0, 2 + j)), blk, blk],
        out_specs=[blk, blk, blk, blk, vec, vec], out_shape=[sd, sd, sd, sd, vd, vd],
        compiler_params=_cp(("parallel", "arbitrary")),
    )(dm, proj, proj, b_gate, b_gate, attn, ssd_out)


def _adamw_math(w, g, m, v):
    mn = ADAM_B1 * m + (1.0 - ADAM_B1) * g
    vn = ADAM_B2 * v + (1.0 - ADAM_B2) * (g * g)
    m_hat = mn / (1.0 - ADAM_B1 ** ADAM_STEP)
    v_hat = vn / (1.0 - ADAM_B2 ** ADAM_STEP)
    return -ADAM_LR * (m_hat / (jnp.sqrt(v_hat) + ADAM_EPS) + ADAM_WD * w), mn, vn


def _adamw_many(ws, gs, ms, vs):
    n = len(ws)

    def body(*refs):
        outs = refs[4 * n:]
        for i in range(n):
            res = _adamw_math(*[refs[q * n + i][...] for q in range(4)])
            for q in range(3):
                outs[q * n + i][...] = res[q]

    return pl.pallas_call(body, name="adamw_small", out_shape=[jax.ShapeDtypeStruct(w.shape, F32) for w in ws] * 3,
                          compiler_params=_cp())(*ws, *gs, *ms, *vs)


def _adamw(w, g, m, v, *, name, tm=128):
    r, c = w.shape
    tm = r if (r < tm or r % tm) else tm

    def body(w_ref, g_ref, m_ref, v_ref, d_ref, nm_ref, nv_ref, g_out):
        gv = g_ref[:, :c]
        d_ref[...], nm_ref[...], nv_ref[...] = _adamw_math(w_ref[...], gv, m_ref[...], v_ref[...])
        g_out[...] = gv

    blk = pl.BlockSpec((tm, c), lambda i: (i, 0))
    sd = jax.ShapeDtypeStruct((r, c), F32)
    return pl.pallas_call(
        body, name=name, grid=(r // tm,), in_specs=[blk, pl.BlockSpec((tm, g.shape[1]), lambda i: (i, 0)), blk, blk],
        out_specs=[blk] * 4, out_shape=[sd] * 4, compiler_params=_cp(("parallel",)),
    )(w, g, m, v)


ANY = pl.BlockSpec(memory_space=pl.ANY)
N_CHIPS = 4


def _chip_of(k, x, y):
    return (x ^ (k >> 1), y ^ (k & 1))


def _all_gather_small(shard):
    r, c = shard.shape
    hr = r // 2

    def body(sh_ref, out_ref, send_sems, recv_sems, local_sem):
        x, y, cc = lax.axis_index("x"), lax.axis_index("y"), lax.axis_index("c")

        def half(px, py, pc):
            return out_ref.at[2 * px + py, pl.ds(pc * hr, hr), :]

        def copy(k, px, py, pc, to, src=None):
            return pltpu.make_async_remote_copy(
                src_ref=half(px, py, pc) if src is None else src, dst_ref=half(px, py, pc),
                send_sem=send_sems.at[k], recv_sem=recv_sems.at[k], device_id=to, device_id_type=MESH)

        mine = pltpu.make_async_copy(sh_ref, out_ref.at[2 * x + y], local_sem)
        mine.start()
        chips = [_chip_of(k, x, y) for k in (1, 2, 3)]
        first = [copy(j, x, y, cc, (*chip, cc), src=sh_ref.at[pl.ds(cc * hr, hr), :]) for j, chip in enumerate(chips)]
        for cp in first:
            cp.start()
        passed = [copy(3 + j, *chip, cc, (x, y, 1 - cc)) for j, chip in enumerate(chips)]
        for j, chip in enumerate(chips):
            copy(j, *chip, cc, (x, y, cc)).wait_recv()
            passed[j].start()
        for j, chip in enumerate(chips):
            copy(3 + j, *chip, 1 - cc, (x, y, cc)).wait_recv()
        for cp in first + passed:
            cp.wait_send()
        mine.wait()

    return pl.pallas_call(
        body, name="all_gather_small", in_specs=[ANY], out_specs=ANY,
        out_shape=jax.ShapeDtypeStruct((N_CHIPS, r, c), shard.dtype),
        scratch_shapes=[pltpu.SemaphoreType.DMA((6,)), pltpu.SemaphoreType.DMA((6,)), pltpu.SemaphoreType.DMA],
    )(shard)


def _cast_bf16(a, *, name, tm=512):
    n, r, c = a.shape
    tm = _tile(r, tm) if r % 128 == 0 else r

    def body(a_ref, o_ref):
        o_ref[...] = a_ref[...].astype(BF16)

    blk = pl.BlockSpec((1, tm, c), lambda i, j: (i, j, 0))
    return pl.pallas_call(body, name=name, grid=(n, r // tm), in_specs=[blk], out_specs=blk,
                          out_shape=jax.ShapeDtypeStruct(a.shape, BF16), compiler_params=_cp(("parallel", "parallel")))(a)


def _pair_exchange(g16, hr):
    n, r, c = g16.shape

    def body(g_ref, out_ref, send_sem, recv_sem):
        x, y, cc = lax.axis_index("x"), lax.axis_index("y"), lax.axis_index("c")
        cp = pltpu.make_async_remote_copy(
            src_ref=g_ref.at[:, pl.ds((1 - cc) * hr, hr), :], dst_ref=out_ref, send_sem=send_sem, recv_sem=recv_sem,
            device_id=(x, y, 1 - cc), device_id_type=MESH)
        cp.start()
        cp.wait()

    return pl.pallas_call(
        body, name="grad_pair_exchange", in_specs=[ANY], out_specs=ANY,
        out_shape=jax.ShapeDtypeStruct((n, hr, c), g16.dtype),
        scratch_shapes=[pltpu.SemaphoreType.DMA, pltpu.SemaphoreType.DMA],
    )(g16)


def _pair_add(g, recv, half_idx, hr, *, tm=384):
    n, r, c = g.shape
    nt = hr // tm

    def body(hi_ref, g_ref, r_ref, o32_ref, o16_ref):
        v = g_ref[...] + r_ref[...].astype(F32)
        o32_ref[...] = v
        o16_ref[...] = v.astype(BF16)

    gs = pltpu.PrefetchScalarGridSpec(
        num_scalar_prefetch=1, grid=(n, nt),
        in_specs=[pl.BlockSpec((1, tm, c), lambda i, j, hi: (i, hi[0] * nt + j, 0)),
                  pl.BlockSpec((1, tm, c), lambda i, j, hi: (i, j, 0))],
        out_specs=[pl.BlockSpec((1, tm, c), lambda i, j, hi: (i, j, 0))] * 2)
    return pl.pallas_call(
        body, name="grad_pair_add", grid_spec=gs,
        out_shape=[jax.ShapeDtypeStruct((n, hr, c), F32), jax.ShapeDtypeStruct((n, hr, c), BF16)],
        compiler_params=_cp(("parallel", "parallel")),
    )(half_idx, g, recv)


def _chip_exchange(p16):
    n, hr, c = p16.shape

    def body(p_ref, out_ref, send_sems, recv_sems):
        x, y, cc = lax.axis_index("x"), lax.axis_index("y"), lax.axis_index("c")
        cps = []
        for j, k in enumerate((1, 2, 3)):
            px, py = _chip_of(k, x, y)
            cps.append(pltpu.make_async_remote_copy(
                src_ref=p_ref.at[2 * px + py], dst_ref=out_ref.at[j], send_sem=send_sems.at[j], recv_sem=recv_sems.at[j],
                device_id=(px, py, cc), device_id_type=MESH))
        for cp in cps:
            cp.start()
        for cp in cps:
            cp.wait()

    return pl.pallas_call(
        body, name="grad_chip_exchange", in_specs=[ANY], out_specs=ANY,
        out_shape=jax.ShapeDtypeStruct((3, hr, c), p16.dtype),
        scratch_shapes=[pltpu.SemaphoreType.DMA((3,)), pltpu.SemaphoreType.DMA((3,))],
    )(p16)


def _chip_add(p32, recv, chip_idx, *, tm=384):
    n, hr, c = p32.shape

    def body(ci_ref, p_ref, r_ref, o_ref):
        o_ref[...] = ((p_ref[0] + r_ref[0].astype(F32)) + r_ref[1].astype(F32)) + r_ref[2].astype(F32)

    gs = pltpu.PrefetchScalarGridSpec(
        num_scalar_prefetch=1, grid=(hr // tm,),
        in_specs=[pl.BlockSpec((1, tm, c), lambda j, ci: (ci[0], j, 0)), pl.BlockSpec((3, tm, c), lambda j, ci: (0, j, 0))],
        out_specs=pl.BlockSpec((tm, c), lambda j, ci: (j, 0)))
    return pl.pallas_call(
        body, name="grad_chip_add", grid_spec=gs, out_shape=jax.ShapeDtypeStruct((hr, c), F32),
        compiler_params=_cp(("parallel",)),
    )(chip_idx, p32, recv)


def _pair_gather(f):
    hr, c = f.shape

    def body(f_ref, out_ref, send_sem, recv_sem, local_sem):
        x, y, cc = lax.axis_index("x"), lax.axis_index("y"), lax.axis_index("c")
        mine = pltpu.make_async_copy(f_ref, out_ref.at[pl.ds(cc * hr, hr), :], local_sem)
        mine.start()
        cp = pltpu.make_async_remote_copy(
            src_ref=f_ref, dst_ref=out_ref.at[pl.ds(cc * hr, hr), :], send_sem=send_sem, recv_sem=recv_sem,
            device_id=(x, y, 1 - cc), device_id_type=MESH)
        cp.start()
        cp.wait()
        mine.wait()

    return pl.pallas_call(
        body, name="grad_pair_gather", in_specs=[ANY], out_specs=ANY,
        out_shape=jax.ShapeDtypeStruct((2 * hr, c), f.dtype),
        scratch_shapes=[pltpu.SemaphoreType.DMA, pltpu.SemaphoreType.DMA, pltpu.SemaphoreType.DMA],
    )(f)


def _all_reduce_small(buf):
    r, c = buf.shape

    def body(b_ref, out_ref, gat, send_sems, recv_sems):
        x, y, cc = lax.axis_index("x"), lax.axis_index("y"), lax.axis_index("c")
        me = 4 * x + 2 * y + cc
        gat[me] = b_ref[...]
        cps = []
        for k in range(1, 8):
            px, py, pc = x ^ (k >> 2), y ^ ((k >> 1) & 1), cc ^ (k & 1)
            cps.append(pltpu.make_async_remote_copy(
                src_ref=b_ref, dst_ref=gat.at[me], send_sem=send_sems.at[k - 1], recv_sem=recv_sems.at[k - 1],
                device_id=(px, py, pc), device_id_type=MESH))
        for cp in cps:
            cp.start()
        for cp in cps:
            cp.wait()
        acc = gat[0]
        for d in range(1, 8):
            acc = acc + gat[d]
        out_ref[...] = acc

    vm = pl.BlockSpec(memory_space=pltpu.VMEM)
    return pl.pallas_call(
        body, name="all_reduce_small", in_specs=[vm], out_specs=vm, out_shape=jax.ShapeDtypeStruct((r, c), F32),
        scratch_shapes=[pltpu.VMEM((8, r, c), F32), pltpu.SemaphoreType.DMA((7,)), pltpu.SemaphoreType.DMA((7,))],
        compiler_params=pltpu.CompilerParams(vmem_limit_bytes=VMEM_LIMIT),
    )(buf)


def _pipe(fn, ins, outs, tr, depth=4):
    shape = ins[0].shape
    lead, (r, c) = shape[:-2], shape[-2:]
    assert len(lead) <= 1 and r % tr == 0
    nr = r // tr
    n = nr * (lead[0] if lead else 1)
    ni, no = len(ins), len(outs)

    def blk(ref, step):
        rows = pl.ds((step % nr) * tr, tr)
        return ref.at[step // nr, rows, :] if lead else ref.at[rows, :]

    def scoped(*bufs):
        ibufs, obufs, isem, osem = bufs[:ni], bufs[ni:ni + no], bufs[-2], bufs[-1]

        def in_copy(q, step, slot):
            return pltpu.make_async_copy(blk(ins[q], step), ibufs[q].at[slot], isem.at[q, slot])

        def out_copy(q, step, slot):
            return pltpu.make_async_copy(obufs[q].at[slot], blk(outs[q], step), osem.at[q, slot])

        for step in range(min(nbuf - 1, n)):
            for q in range(ni):
                in_copy(q, step, step % nbuf).start()
        for step in range(n):
            slot = step % nbuf
            if step + nbuf - 1 < n:
                for q in range(ni):
                    in_copy(q, step + nbuf - 1, (step + nbuf - 1) % nbuf).start()
            for q in range(ni):
                in_copy(q, step, slot).wait()
            if step >= nbuf:
                for q in range(no):
                    out_copy(q, step - nbuf, slot).wait()
            res = fn(*[ibufs[q][slot] for q in range(ni)])
            for q in range(no):
                obufs[q][slot] = res[q].astype(obufs[q].dtype)
                out_copy(q, step, slot).start()
        for step in range(max(n - nbuf, 0), n):
            for q in range(no):
                out_copy(q, step, step % nbuf).wait()

    assert n <= 8
    nbuf = min(n, depth)
    pl.run_scoped(scoped, *[pltpu.VMEM((nbuf, tr, c), q.dtype) for q in ins], *[pltpu.VMEM((nbuf, tr, c), q.dtype) for q in outs],
                  pltpu.SemaphoreType.DMA((ni, nbuf)), pltpu.SemaphoreType.DMA((no, nbuf)))


W_IN_PAD = 2304
BIG = ("w_in", "w_attn_o", "w_ssd_o", "w_out", "w_up", "w_down")
BIG_SHAPE = dict(w_in=(D_MODEL, W_IN_PAD), w_attn_o=(Q_DIM // 4, D_MODEL), w_ssd_o=(D_INNER // 4, D_MODEL),
                 w_out=(D_MODEL // 4, D_MODEL), w_up=(D_MODEL, 2 * D_FF // 4), w_down=(D_FF // 4, D_MODEL))
BIG_TR = dict(w_in=128, w_attn_o=128, w_ssd_o=128, w_out=128, w_up=128, w_down=176)
X_FIRST = dict(w_in=True, w_attn_o=True, w_ssd_o=False, w_out=True, w_up=False, w_down=False)


def _neighbours(x, y, x_first):
    xn, yn = (1 - x, y), (x, 1 - y)
    n1, n2 = (xn, yn) if x_first else (yn, xn)
    slot = lambda ch: 2 * ch[0] + ch[1]
    return n1, n2, slot(n1), slot(n2), slot((1 - x, 1 - y))


def _gather_big(shards):
    nt = len(BIG)

    def body(*refs):
        sh, out = refs[:nt], refs[nt:2 * nt]
        send_sems, recv_sems = refs[2 * nt:]
        x, y, cc = lax.axis_index("x"), lax.axis_index("y"), lax.axis_index("c")
        me = 2 * x + y
        sib = (x, y, 1 - cc)
        for t, n in enumerate(BIG):
            _pipe(lambda v: (v,), [sh[t]], [out[t].at[me]], BIG_TR[n])

        def copy(t, k, slot, pc, to):
            hr = BIG_SHAPE[BIG[t]][0] // 2
            ref = out[t].at[slot, pl.ds(pc * hr, hr), :]
            return pltpu.make_async_remote_copy(src_ref=ref, dst_ref=ref, send_sem=send_sems.at[6 * t + k],
                                                recv_sem=recv_sems.at[6 * t + k], device_id=to, device_id_type=MESH)

        started = []

        def start(cp):
            cp.start()
            started.append(cp)

        geo = [_neighbours(x, y, X_FIRST[n]) for n in BIG]
        for t in range(nt):
            n1, n2, _, _, _ = geo[t]
            start(copy(t, 0, me, cc, (*n1, cc)))
            start(copy(t, 1, me, cc, (*n2, cc)))
        for t in range(nt):
            n1, n2, s1, s2, sd = geo[t]
            copy(t, 0, s1, cc, sib).wait_recv()
            start(copy(t, 2, s1, cc, (*n2, cc)))
            start(copy(t, 3, s1, cc, sib))
            copy(t, 1, s2, cc, sib).wait_recv()
            start(copy(t, 4, s2, cc, sib))
        for t in range(nt):
            _, _, s1, s2, sd = geo[t]
            copy(t, 2, sd, cc, sib).wait_recv()
            start(copy(t, 5, sd, cc, sib))
        for t in range(nt):
            _, _, s1, s2, sd = geo[t]
            copy(t, 3, s1, 1 - cc, sib).wait_recv()
            copy(t, 4, s2, 1 - cc, sib).wait_recv()
            copy(t, 5, sd, 1 - cc, sib).wait_recv()
        for cp in started:
            cp.wait_send()

    return pl.pallas_call(
        body, name="gather_big", in_specs=[ANY] * nt, out_specs=[ANY] * nt,
        out_shape=[jax.ShapeDtypeStruct((N_CHIPS, *BIG_SHAPE[n]), BF16) for n in BIG],
        scratch_shapes=[pltpu.SemaphoreType.DMA((6 * nt,)), pltpu.SemaphoreType.DMA((6 * nt,))],
        compiler_params=pltpu.CompilerParams(vmem_limit_bytes=VMEM_LIMIT),
    )(*shards)


def _reduce_big(grads):
    nt = len(BIG)
    nw = 7

    def body(*refs):
        g = refs[:nt]
        fin = refs[nt:2 * nt]
        work = refs[2 * nt:2 * nt + nw * nt]
        send_sems, recv_sems = refs[2 * nt + nw * nt:]
        x, y, cc = lax.axis_index("x"), lax.axis_index("y"), lax.axis_index("c")
        me = 2 * x + y
        sib = (x, y, 1 - cc)
        started = []

        def rcopy(t, k, src, dst, to):
            cp = pltpu.make_async_remote_copy(src_ref=src, dst_ref=dst, send_sem=send_sems.at[5 * t + k],
                                              recv_sem=recv_sems.at[5 * t + k], device_id=to, device_id_type=MESH)
            return cp

        def start(cp):
            cp.start()
            started.append(cp)

        geo = [_neighbours(x, y, X_FIRST[n]) for n in BIG]
        hrs = [BIG_SHAPE[n][0] // 2 for n in BIG]
        wk = lambda t: work[nw * t:nw * (t + 1)]
        one = lambda ref, slot: ref.at[pl.ds(slot, 1)]
        for t in range(nt):
            recv_a = wk(t)[0]
            start(rcopy(t, 0, g[t].at[:, pl.ds((1 - cc) * hrs[t], hrs[t]), :], recv_a, sib))
        for t, n in enumerate(BIG):
            recv_a, p32, p16, r1, qme, qs2, r2 = wk(t)
            n1, n2, s1, s2, sd = geo[t]
            rcopy(t, 0, recv_a, recv_a, sib).wait_recv()
            _pipe(lambda a, b: (a + b, a + b), [g[t].at[:, pl.ds(cc * hrs[t], hrs[t]), :], recv_a], [p32, p16], BIG_TR[n])
            start(rcopy(t, 1, one(p16, s1), one(r1, 0), (*n1, cc)))
            start(rcopy(t, 2, one(p16, sd), one(r1, 1), (*n1, cc)))
        for t, n in enumerate(BIG):
            recv_a, p32, p16, r1, qme, qs2, r2 = wk(t)
            n1, n2, s1, s2, sd = geo[t]
            rcopy(t, 1, one(r1, 0), one(r1, 0), sib).wait_recv()
            rcopy(t, 2, one(r1, 1), one(r1, 1), sib).wait_recv()
            _pipe(lambda a, b: (a + b.astype(F32),), [one(p32, s2), one(r1, 1)], [qs2], BIG_TR[n])
            start(rcopy(t, 3, qs2, r2, (*n2, cc)))
            _pipe(lambda a, b: (a + b.astype(F32),), [one(p32, me), one(r1, 0)], [qme], BIG_TR[n])
        for t, n in enumerate(BIG):
            recv_a, p32, p16, r1, qme, qs2, r2 = wk(t)
            rcopy(t, 3, r2, r2, sib).wait_recv()
            mine = fin[t].at[pl.ds(cc * hrs[t], hrs[t]), :]
            _pipe(lambda a, b: (a + b.astype(F32),), [qme.at[0], r2.at[0]], [mine], BIG_TR[n])
            start(rcopy(t, 4, mine, mine, sib))
        for t in range(nt):
            other = fin[t].at[pl.ds((1 - cc) * hrs[t], hrs[t]), :]
            rcopy(t, 4, other, other, sib).wait_recv()
        for cp in started:
            cp.wait_send()

    outs = [jax.ShapeDtypeStruct(BIG_SHAPE[n], F32) for n in BIG]
    for n in BIG:
        r, c = BIG_SHAPE[n]
        hr = r // 2
        outs += [jax.ShapeDtypeStruct((4, hr, c), F32), jax.ShapeDtypeStruct((4, hr, c), F32),
                 jax.ShapeDtypeStruct((4, hr, c), BF16), jax.ShapeDtypeStruct((2, hr, c), BF16),
                 jax.ShapeDtypeStruct((1, hr, c), F32), jax.ShapeDtypeStruct((1, hr, c), BF16),
                 jax.ShapeDtypeStruct((1, hr, c), BF16)]
    res = pl.pallas_call(
        body, name="reduce_big", in_specs=[ANY] * nt, out_specs=[ANY] * len(outs), out_shape=outs,
        scratch_shapes=[pltpu.SemaphoreType.DMA((5 * nt,)), pltpu.SemaphoreType.DMA((5 * nt,))],
        compiler_params=pltpu.CompilerParams(vmem_limit_bytes=VMEM_LIMIT),
    )(*grads)
    return res[:nt]


WHOLE_X_FIRST = dict(w_ssd_o=True, w_out=False, w_attn_o=False)


def _quarters(names):
    out = []
    for i, n in enumerate(names):
        if n in WHOLE_X_FIRST:
            h = BIG_SHAPE[n][0] // 2
            out.append((i, WHOLE_X_FIRST[n], 0, h, 128))
        else:
            q = BIG_SHAPE[n][0] // 4
            tr = 128 if q % 128 == 0 else q
            out += [(i, True, 0, q, tr), (i, False, q, q, tr)]
    return out


class _GatherJob:
    def __init__(self, names, shards, at=None):
        self.names = names
        self.at = at
        self.inputs = list(shards)
        self.out_shapes = [jax.ShapeDtypeStruct((N_CHIPS, *BIG_SHAPE[n]), BF16) for n in names]
        self.ent = _quarters(names)
        self.scratch = [pltpu.SemaphoreType.DMA((6 * len(self.ent),)), pltpu.SemaphoreType.DMA((6 * len(self.ent),))]

    def phases(self, sh, out, scr):
        send_sems, recv_sems = scr
        names, ent = self.names, self.ent
        x, y, cc = lax.axis_index("x"), lax.axis_index("y"), lax.axis_index("c")
        me = 2 * x + y
        sib = (x, y, 1 - cc)
        geo = [_neighbours(x, y, e[1]) for e in ent]
        started = []

        def copy(i, k, slot, pc, to):
            arr, _, roff, rows, _ = ent[i]
            hr = BIG_SHAPE[names[arr]][0] // 2
            ref = out[arr].at[slot, pl.ds(pc * hr + roff, rows), :]
            return pltpu.make_async_remote_copy(src_ref=ref, dst_ref=ref, send_sem=send_sems.at[6 * i + k],
                                                recv_sem=recv_sems.at[6 * i + k], device_id=to, device_id_type=MESH)

        def start(*a):
            copy(*a).start()
            started.append(a)

        def p0():
            for t, n in enumerate(names):
                _pipe(lambda v: (v,), [sh[t]], [out[t].at[me]], BIG_TR[n])
            for i in range(len(ent)):
                n1, n2, _, _, _ = geo[i]
                start(i, 0, me, cc, (*n1, cc))
                start(i, 1, me, cc, (*n2, cc))

        def p1():
            for i in range(len(ent)):
                n1, n2, s1, s2, sd = geo[i]
                copy(i, 0, s1, cc, sib).wait_recv()
                start(i, 2, s1, cc, (*n2, cc))
                start(i, 3, s1, cc, sib)
                copy(i, 1, s2, cc, sib).wait_recv()
                start(i, 4, s2, cc, sib)

        def p2():
            for i in range(len(ent)):
                sd = geo[i][4]
                copy(i, 2, sd, cc, sib).wait_recv()
                start(i, 5, sd, cc, sib)

        def p3():
            for i in range(len(ent)):
                _, _, s1, s2, sd = geo[i]
                copy(i, 3, s1, 1 - cc, sib).wait_recv()
                copy(i, 4, s2, 1 - cc, sib).wait_recv()
                copy(i, 5, sd, 1 - cc, sib).wait_recv()
            for a in started:
                copy(*a).wait_send()

        return [p0, p1, p2, p3]


class _ReduceJob:
    NW = 7

    def __init__(self, names, grads, at=None):
        self.names = names
        self.at = at
        self.inputs = list(grads)
        self.ent = _quarters(names)
        self.out_shapes = [jax.ShapeDtypeStruct(BIG_SHAPE[n], F32) for n in names]
        for arr, _, _, rows, _ in self.ent:
            c = BIG_SHAPE[names[arr]][1]
            self.out_shapes += [jax.ShapeDtypeStruct((4, rows, c), F32), jax.ShapeDtypeStruct((4, rows, c), F32),
                                jax.ShapeDtypeStruct((4, rows, c), BF16), jax.ShapeDtypeStruct((2, rows, c), BF16),
                                jax.ShapeDtypeStruct((1, rows, c), F32), jax.ShapeDtypeStruct((1, rows, c), BF16),
                                jax.ShapeDtypeStruct((1, rows, c), BF16)]
        self.scratch = [pltpu.SemaphoreType.DMA((5 * len(self.ent),)), pltpu.SemaphoreType.DMA((5 * len(self.ent),))]

    def phases(self, g, outs, scr):
        send_sems, recv_sems = scr
        names, ent, nw = self.names, self.ent, self.NW
        nt = len(names)
        fin, work = outs[:nt], outs[nt:]
        x, y, cc = lax.axis_index("x"), lax.axis_index("y"), lax.axis_index("c")
        me = 2 * x + y
        sib = (x, y, 1 - cc)
        geo = [_neighbours(x, y, e[1]) for e in ent]
        started = []
        wk = lambda i: work[nw * i:nw * (i + 1)]
        one = lambda ref, slot: ref.at[pl.ds(slot, 1)]

        def rows_of(i, pc):
            arr, _, roff, rows, _ = ent[i]
            return pl.ds(pc * (BIG_SHAPE[names[arr]][0] // 2) + roff, rows)

        def rcopy(i, k, src, dst, to):
            return pltpu.make_async_remote_copy(src_ref=src, dst_ref=dst, send_sem=send_sems.at[5 * i + k],
                                                recv_sem=recv_sems.at[5 * i + k], device_id=to, device_id_type=MESH)

        def start(make):
            make().start()
            started.append(make)

        def p0():
            for i, e in enumerate(ent):
                start(lambda i=i, e=e: rcopy(i, 0, g[e[0]].at[:, rows_of(i, 1 - cc), :], wk(i)[0], sib))

        def p1():
            for i, e in enumerate(ent):
                recv_a, p32, p16, r1 = wk(i)[:4]
                n1, n2, s1, s2, sd = geo[i]
                rcopy(i, 0, recv_a, recv_a, sib).wait_recv()
                _pipe(lambda a, b: (a + b, a + b), [g[e[0]].at[:, rows_of(i, cc), :], recv_a], [p32, p16], e[4])
                start(lambda i=i, s1=s1, n1=n1: rcopy(i, 1, one(wk(i)[2], s1), one(wk(i)[3], 0), (*n1, cc)))
                start(lambda i=i, sd=sd, n1=n1: rcopy(i, 2, one(wk(i)[2], sd), one(wk(i)[3], 1), (*n1, cc)))

        def p2():
            for i, e in enumerate(ent):
                _, p32, _, r1, qme, qs2, r2 = wk(i)
                n1, n2, s1, s2, sd = geo[i]
                rcopy(i, 1, one(r1, 0), one(r1, 0), sib).wait_recv()
                rcopy(i, 2, one(r1, 1), one(r1, 1), sib).wait_recv()
                _pipe(lambda a, b, c, d: (a + b.astype(F32), c + d.astype(F32)),
                      [one(p32, s2), one(r1, 1), one(p32, me), one(r1, 0)], [qs2, qme], e[4])
                start(lambda i=i, n2=n2: rcopy(i, 3, wk(i)[5], wk(i)[6], (*n2, cc)))

        def p3():
            for i, e in enumerate(ent):
                qme, r2 = wk(i)[4], wk(i)[6]
                rcopy(i, 3, r2, r2, sib).wait_recv()
                mine = fin[e[0]].at[rows_of(i, cc), :]
                _pipe(lambda a, b: (a + b.astype(F32),), [qme.at[0], r2.at[0]], [mine], e[4])
                start(lambda i=i, e=e: rcopy(i, 4, fin[e[0]].at[rows_of(i, cc), :], fin[e[0]].at[rows_of(i, cc), :], sib))

        def p4():
            for i, e in enumerate(ent):
                other = fin[e[0]].at[rows_of(i, 1 - cc), :]
                rcopy(i, 4, other, other, sib).wait_recv()
            for make in started:
                make().wait_send()

        return [p0, p1, p2, p3, p4]


class _AdamJob:
    def __init__(self, names, ws, gs, ms, vs, groups):
        self.names, self.groups = names, groups
        self.inputs = [a for quad in zip(ws, gs, ms, vs) for a in quad]
        self.out_shapes = [jax.ShapeDtypeStruct(w.shape, F32) for w in ws for _ in range(4)]

    def work(self, ins, outs):
        def one(t):
            w, g, m, v = ins[4 * t:4 * t + 4]
            r = w.shape[1]
            tr = 128 if r % 128 == 0 else r // 4
            _pipe(lambda a, b, c, d: (*_adamw_math(a, b, c, d), b), [w.at[0], g, m.at[0], v.at[0]],
                  [o.at[0] for o in outs[4 * t:4 * t + 4]], tr, depth=2)

        def group(grp):
            def run():
                for n in grp:
                    one(self.names.index(n))
            return run

        return [group(grp) for grp in self.groups]


class _Interleaved:
    def __init__(self, job, work, at):
        self.job, self.wk, self.at = job, work, at
        self.inputs = job.inputs + work.inputs
        self.out_shapes = list(job.out_shapes) + list(work.out_shapes)
        self.scratch = job.scratch

    def phases(self, ins, outs, scr):
        nj, no = len(self.job.inputs), len(self.job.out_shapes)
        base = self.job.phases(ins[:nj], outs[:no], scr)
        work = self.wk.work(ins[nj:], outs[no:])
        mixed = []
        for k, ph in enumerate(base):
            mixed.append(ph)
            if k < len(work):
                mixed.append(work[k])
        return mixed


def _run_job(job, name):
    ni, no = len(job.inputs), len(job.out_shapes)

    def body(*refs):
        for ph in job.phases(refs[:ni], refs[ni:ni + no], refs[ni + no:]):
            ph()

    return pl.pallas_call(
        body, name=name, in_specs=[ANY] * ni, out_specs=[ANY] * no, out_shape=job.out_shapes, scratch_shapes=job.scratch,
        compiler_params=pltpu.CompilerParams(vmem_limit_bytes=VMEM_LIMIT),
    )(*job.inputs)


def _hosted(body, *, name, grid, in_specs, out_specs, out_shape, scratch_shapes, args, sem, side=None):
    if side is None:
        return pl.pallas_call(body, name=name, grid=grid, in_specs=in_specs, out_specs=out_specs, out_shape=out_shape,
                              scratch_shapes=scratch_shapes, compiler_params=_cp(sem))(*args), None
    job = side
    ni, no, ns = len(in_specs), len(out_specs), len(scratch_shapes)
    ji, jo = len(job.inputs), len(job.out_shapes)
    n_steps = 1
    for extent in grid:
        n_steps *= extent

    def wrapped(*refs):
        own_in, refs = refs[:ni], refs[ni:]
        job_in, refs = refs[:ji], refs[ji:]
        own_out, refs = refs[:no], refs[no:]
        job_out, refs = refs[:jo], refs[jo:]
        own_scr, job_scr = refs[:ns], refs[ns:]
        step = 0
        for d, extent in enumerate(grid):
            step = step * extent + pl.program_id(d)
        phases = job.phases(job_in, job_out, job_scr)
        steps = [min(int(f * n_steps), n_steps - 1) for f in job.at] + [n_steps - 1]
        assert len(steps) == len(phases) and steps == sorted(steps)
        for at, ph in zip(steps, phases):
            pl.when(step == at)(ph)
        body(*own_in, *own_out, *own_scr)

    res = pl.pallas_call(
        wrapped, name=name, grid=grid, in_specs=list(in_specs) + [ANY] * ji, out_specs=list(out_specs) + [ANY] * jo,
        out_shape=list(out_shape) + list(job.out_shapes), scratch_shapes=list(scratch_shapes) + list(job.scratch),
        compiler_params=_cp(("arbitrary",) * len(grid)),
    )(*args, *job.inputs)
    return res[:no], res[no:]


def _proj_dw(xnt, dproj_sh, *, tm=512, tk=2048):
    d, s = xnt.shape
    tk = _tile(s, tk)
    nk = s // tk

    def body(a_ref, b_ref, o_ref, acc):
        def finish(r):
            o_ref[0] = r

        _accumulate(acc, _dot(a_ref[...], b_ref[...]), pl.program_id(2), nk, finish)

    return pl.pallas_call(
        body, name="proj_dw", grid=(N_CHIPS, d // tm, nk),
        in_specs=[pl.BlockSpec((tm, tk), lambda j, i, q: (i, q)), pl.BlockSpec((tk, W_IN_PAD), lambda j, i, q: (q, j))],
        out_specs=pl.BlockSpec((1, tm, W_IN_PAD), lambda j, i, q: (j, i, 0)),
        out_shape=jax.ShapeDtypeStruct((N_CHIPS, d, W_IN_PAD), F32), scratch_shapes=[pltpu.VMEM((tm, W_IN_PAD), F32)],
        compiler_params=_cp(("parallel", "parallel", "arbitrary")),
    )(xnt, dproj_sh)


def _proj_dx(dproj_sh, w_sh, *, tm=1024, side=None):
    s = dproj_sh.shape[0]
    d = w_sh.shape[1]
    tm = _tile(s, tm)

    def body(a_ref, b_ref, o_ref, acc):
        kk = pl.program_id(1)
        part = _dot_nt(a_ref[...], b_ref[0])

        @pl.when(kk == 0)
        def _():
            acc[...] = part

        @pl.when(kk > 0)
        def _():
            acc[...] += part

        @pl.when(kk == N_CHIPS - 1)
        def _():
            o_ref[...] = acc[...]

    own, extra = _hosted(
        body, name="proj_dx", grid=(s // tm, N_CHIPS),
        in_specs=[pl.BlockSpec((tm, W_IN_PAD), lambda i, q: (i, q)), pl.BlockSpec((1, d, W_IN_PAD), lambda i, q: (q, 0, 0))],
        out_specs=[pl.BlockSpec((tm, d), lambda i, q: (i, 0))],
        out_shape=[jax.ShapeDtypeStruct((s, d), F32)], scratch_shapes=[pltpu.VMEM((tm, d), F32)],
        args=(dproj_sh, w_sh), sem=("parallel", "arbitrary"), side=side)
    return own[0] if side is None else (own[0], extra)


def _up_dx(dup, w_sh, *, tm=1024):
    s = dup.shape[1]
    d, wsh = w_sh.shape[1:]
    tm = _tile(s, tm)

    def body(a_ref, b_ref, o_ref, acc):
        kk = pl.program_id(1)
        part = _dot_nt(a_ref[0], b_ref[0])

        @pl.when(kk == 0)
        def _():
            acc[...] = part

        @pl.when(kk > 0)
        def _():
            acc[...] += part

        @pl.when(kk == N_CHIPS - 1)
        def _():
            o_ref[...] = acc[...]

    return pl.pallas_call(
        body, name="up_dx", grid=(s // tm, N_CHIPS),
        in_specs=[pl.BlockSpec((1, tm, wsh), lambda i, q: (q >> 1, i, q & 1)), pl.BlockSpec((1, d, wsh), lambda i, q: (q, 0, 0))],
        out_specs=pl.BlockSpec((tm, d), lambda i, q: (i, 0)),
        out_shape=jax.ShapeDtypeStruct((s, d), F32), scratch_shapes=[pltpu.VMEM((tm, d), F32)],
        compiler_params=_cp(("parallel", "arbitrary")),
    )(dup, w_sh)


def _up_dw(hnt, dup, *, tk=2048):
    d, s = hnt.shape
    wsh = 2 * D_FF // N_CHIPS
    tk = _tile(s, tk)
    nk = s // tk

    def body(a_ref, b_ref, o_ref, acc):
        def finish(r):
            o_ref[0] = r

        _accumulate(acc, _dot(a_ref[...], b_ref[0]), pl.program_id(1), nk, finish)

    return pl.pallas_call(
        body, name="up_dw", grid=(N_CHIPS, nk),
        in_specs=[pl.BlockSpec((d, tk), lambda j, q: (0, q)), pl.BlockSpec((1, tk, wsh), lambda j, q: (j >> 1, q, j & 1))],
        out_specs=pl.BlockSpec((1, d, wsh), lambda j, q: (j, 0, 0)),
        out_shape=jax.ShapeDtypeStruct((N_CHIPS, d, wsh), F32), scratch_shapes=[pltpu.VMEM((d, wsh), F32)],
        compiler_params=_cp(("parallel", "arbitrary")),
    )(hnt, dup)


BIG_ROWS =(IN_DIM // 4, Q_DIM // 4, D_INNER // 4, D_MODEL // 4, 2 * D_FF // 4, D_FF // 4)
PACK_ROWS = 5376


def _pack_shards(parts):
    rows = [p.reshape(-1, D_MODEL) for p in parts]
    pad = PACK_ROWS - sum(BIG_ROWS)
    return jnp.concatenate(rows + [jnp.zeros((pad, D_MODEL), rows[0].dtype)], axis=0)


def _unpack_shards(buf):
    out, off = [], 0
    for n in BIG_ROWS:
        out.append(buf[off:off + n])
        off += n
    return out


def _assemble(srcs, col_map, *, name, tr=256):
    arrays, lead = [], []
    for src in srcs:
        arr, j = src if isinstance(src, tuple) else (src, None)
        if not any(arr is a for a in arrays):
            arrays.append(arr)
        lead.append(([i for i, a in enumerate(arrays) if a is arr][0], j))
    rows = arrays[0].shape[-2]
    tr = _tile(rows, tr)
    out_w = len(col_map)
    tiles = []
    for t in range(out_w // 128):
        runs = []
        for lane in range(128):
            ent = col_map[t * 128 + lane]
            key = None if ent is None else (ent[0], ent[1] // 128, (lane - ent[1]) % 128)
            if runs and runs[-1][0] == key:
                runs[-1][2] = lane + 1
            else:
                runs.append([key, lane, lane + 1])
        tiles.append(runs)

    def body(*refs):
        o_ref = refs[-1]
        lane = lax.broadcasted_iota(jnp.int32, (tr, 128), 1)
        for t, runs in enumerate(tiles):
            acc = jnp.zeros((tr, 128), F32)
            for key, a, b in runs:
                if key is None:
                    continue
                sid, ct, shift = key
                ai, j = lead[sid]
                cols = slice(ct * 128, (ct + 1) * 128)
                piece = (refs[ai][:, cols] if j is None else refs[ai][j, :, cols]).astype(F32)
                if shift:
                    piece = pltpu.roll(piece, shift, 1)
                acc = piece if (a, b) == (0, 128) else jnp.where((lane >= a) & (lane < b), piece, acc)
            o_ref[:, t * 128:(t + 1) * 128] = acc.astype(BF16)

    specs = [pl.BlockSpec((tr, a.shape[1]), lambda i: (i, 0)) if a.ndim == 2
             else pl.BlockSpec((a.shape[0], tr, a.shape[2]), lambda i: (0, i, 0)) for a in arrays]
    return pl.pallas_call(
        body, name=name, grid=(rows // tr,), in_specs=specs, out_specs=pl.BlockSpec((tr, out_w), lambda i: (i, 0)),
        out_shape=jax.ShapeDtypeStruct((rows, out_w), BF16), compiler_params=_cp(("parallel",)),
    )(*arrays)


def _permute_cols_in(w):
    pad = jnp.zeros((w.shape[0], PW - IN_DIM), w.dtype)
    return jnp.concatenate([w[:, :6656], w[:, 6688:], w[:, 6656:6688], pad], axis=1)


def _unpermute_cols_in(g):
    return jnp.concatenate([g[:, :6656], g[:, O_DT:O_DT + 32], g[:, 6656:O_DT]], axis=1)


SMALL = ("norm1_w", "b_gate", "attn_sinks", "ssd_conv_b", "dt_bias", "a_log", "d_skip", "ssd_norm_w", "norm2_w",
         "ffn_conv_b", "final_norm_w", "ssd_conv_w", "ffn_conv_w")


def _pad128(v):
    v = v.reshape(-1)
    return jnp.pad(v, (0, (-v.shape[0]) % 128))


def _pack_small(parts):
    flat = jnp.concatenate([_pad128(p) for p in parts])
    flat = jnp.pad(flat, (0, (-flat.shape[0]) % 1024))
    return flat.reshape(-1, 128)


def _unpack_small(buf, shapes):
    flat, out, off = buf.reshape(-1), [], 0
    for shp in shapes:
        n = 1
        for q in shp:
            n *= q
        out.append(flat[off:off + n].reshape(shp))
        off += n + (-n) % 128
    return out


def _vec128(v):
    return jnp.pad(v.reshape(1, -1), ((0, 0), (0, 128 - v.shape[-1])))


def kernel(x, norm1_w, w_in, b_gate, attn_sinks, w_attn_o, ssd_conv_w, ssd_conv_b, dt_bias, a_log, d_skip, ssd_norm_w, w_ssd_o, w_out, norm2_w, w_up, ffn_conv_w, ffn_conv_b, w_down, final_norm_w, loss_target, m_norm1_w, m_w_in, m_b_gate, m_attn_sinks, m_w_attn_o, m_ssd_conv_w, m_ssd_conv_b, m_dt_bias, m_a_log, m_d_skip, m_ssd_norm_w, m_w_ssd_o, m_w_out, m_norm2_w, m_w_up, m_ffn_conv_w, m_ffn_conv_b, m_w_down, m_final_norm_w, v_norm1_w, v_w_in, v_b_gate, v_attn_sinks, v_w_attn_o, v_ssd_conv_w, v_ssd_conv_b, v_dt_bias, v_a_log, v_d_skip, v_ssd_norm_w, v_w_ssd_o, v_w_out, v_norm2_w, v_w_up, v_ffn_conv_w, v_ffn_conv_b, v_w_down, v_final_norm_w):
    ix, iy, ic = lax.axis_index("x"), lax.axis_index("y"), lax.axis_index("c")
    chip = 2 * ix + iy
    x2 = x[0]
    tgt = loss_target[0]
    s = x2.shape[0]

    wsh = IN_DIM // N_CHIPS
    big_shards = dict(w_in=jnp.pad(w_in[0], ((0, 0), (0, W_IN_PAD - wsh))), w_attn_o=w_attn_o[0], w_ssd_o=w_ssd_o[0],
                      w_out=w_out[0], w_up=w_up[0], w_down=w_down[0])
    gathered = {}
    (gathered["w_in"],) = _run_job(_GatherJob(("w_in",), [big_shards["w_in"]]), "gather_w_in")
    early = ("w_attn_o", "w_ssd_o", "w_out")
    gather_early = _GatherJob(early, [big_shards[n] for n in early], at=(0.0, 0.5, 0.8))
    gather_up = _GatherJob(("w_up",), [big_shards["w_up"]], at=(0.0, 0.55, 0.85))
    gather_down = _GatherJob(("w_down",), [big_shards["w_down"]], at=(0.0, 0.5, 0.8))
    gw = gathered["w_in"]
    perm = list(range(O_GA)) + list(range(O_GA + N_SSD_HEADS, IN_DIM)) + list(range(O_GA, O_GA + N_SSD_HEADS))
    w_in_p = _assemble([(gw, j) for j in range(N_CHIPS)], [divmod(o, wsh) for o in perm] + [None] * (PW - IN_DIM),
                       name="w_in_assemble")
    small_sh = _pack_small([ssd_conv_w[0], ffn_conv_w[0]])
    small_all = _all_gather_small(small_sh)
    sc_parts = [_unpack_small(small_all[j], [(4, XBC_DIM // 4), (3, 2 * D_FF // 4)]) for j in range(N_CHIPS)]
    ssd_cw = jnp.concatenate([p[0] for p in sc_parts], axis=1)
    ffn_cw = jnp.concatenate([p[1] for p in sc_parts], axis=1)

    sinks128 = _vec128(attn_sinks)
    dtb128, alog128, dskip128 = _vec128(dt_bias), _vec128(a_log), _vec128(d_skip)

    xn, xnt = _rms_fwd(x2, norm1_w, name="norm1_fwd", with_t=True)
    proj, got = _mm(xn, w_in_p, name="proj_fwd", tn=1280, side=gather_early)
    gathered.update(zip(early, got))
    qkvt = _mm(w_in_p[:, :O_Z], xnt, name="qkv_fwd", ta=True)
    attn_pre, (gathered["w_up"],) = _attn_fwd(qkvt, sinks128, side=gather_up)
    xbc = _ssd_conv_fwd(proj, ssd_cw, ssd_conv_b)
    (y_ssd, hprev), (gathered["w_down"],) = _ssd_fwd(xbc, proj, dtb128, alog128, dskip128, side=gather_down)
    full = {n: gathered[n].reshape(-1, D_MODEL) for n in ("w_attn_o", "w_ssd_o", "w_out", "w_down")}
    full["w_up"] = gathered["w_up"]
    attn = _mm(attn_pre, full["w_attn_o"], name="attn_o_fwd", ta=True)
    yn = _gate_norm_fwd(y_ssd, proj, ssd_norm_w)
    ssd_out = _mm(yn, full["w_ssd_o"], name="ssd_o_fwd")
    merged = _merge_fwd(proj, b_gate, attn, ssd_out)
    h1 = _mm(merged, full["w_out"], name="out_fwd", resid=x2)
    hn, hnt = _rms_fwd(h1, norm2_w, name="norm2_fwd", with_t=True)
    up = _mm(hn, full["w_up"], name="up_fwd")
    act = _ffn_act_fwd(up, ffn_cw, ffn_conv_b)
    h2 = _mm(act, full["w_down"], name="down_fwd", resid=h1, tk=1408)

    dh2, loss_blk, g_final = _loss_bwd(h2, tgt, final_norm_w.reshape(1, -1))
    dact = _mm(dh2, full["w_down"], name="down_dx", tb=True, tn=1408)
    g_down = _mm(act, dh2, name="down_dw", ta=True, tm=1408)
    dup, g_ffn_cw, g_ffn_cb = _ffn_act_bwd(dact, up, ffn_cw, ffn_conv_b)
    dhn = _up_dx(dup, full["w_up"])
    g_up = _up_dw(hnt, dup)
    dh1, g_norm2 = _rms_bwd(dhn, h1, norm2_w, dh2, name="norm2_bwd")
    dmerged = _mm(dh1, full["w_out"], name="out_dx", tb=True)
    g_out = _mm(merged, dh1, name="out_dw", ta=True)
    dattn, dssd_out, dga, dgs, g_ba, g_bs = _merge_bwd(dmerged, proj, b_gate, attn, ssd_out)
    dyn = _mm(dssd_out, full["w_ssd_o"], name="ssd_o_dx", tb=True)
    g_ssd_o = _mm(yn, dssd_out, name="ssd_o_dw", ta=True)
    dy_ssd, dz, g_ssd_norm = _gate_norm_bwd(dyn, y_ssd, proj, ssd_norm_w)
    slot = lambda g: g.reshape(N_CHIPS, -1, D_MODEL)
    big_grads = {}
    red = ("w_down", "w_up")
    (dxbc, ddt, dvec), got = _ssd_bwd(xbc, proj, dtb128, alog128, dskip128, hprev, dy_ssd,
                                      side=_ReduceJob(red, [slot(g_down), g_up], at=(0.0, 0.3, 0.8, 0.95)))
    big_grads.update(zip(red, got))
    dxbc_raw, g_ssd_cw, g_ssd_cb = _ssd_conv_bwd(dxbc, proj, ssd_cw, ssd_conv_b)
    dattn_pre = _mm(full["w_attn_o"], dattn, name="attn_o_dx", tb=True)
    g_attn_o = _mm(attn_pre, dattn, name="attn_o_dw")
    red = ("w_out", "w_ssd_o", "w_attn_o")
    (dq, dk, dv, dsk), got = _attn_bwd(qkvt, sinks128, attn_pre, dattn_pre,
                                       side=_ReduceJob(red, [slot(g_out), slot(g_ssd_o), slot(g_attn_o)],
                                                       at=(0.0, 0.2, 0.5, 0.7)))
    big_grads.update(zip(red, got))
    pieces = [(dq.T, Q_DIM), (dk.T, KV_DIM), (dv.T, KV_DIM), (dz, D_INNER), (dxbc_raw, XBC_DIM), (ddt, N_SSD_HEADS),
              (dga, D_MODEL), (dgs, D_MODEL)]
    orig = [(i, c) for i, (_, w) in enumerate(pieces) for c in range(w)]
    dproj_sh = _assemble([p for p, _ in pieces],
                         [orig[j * wsh + c] if c < wsh else None for j in range(N_CHIPS) for c in range(W_IN_PAD)],
                         name="dproj_assemble")
    g_in = _proj_dw(xnt, dproj_sh)
    dxn, got = _proj_dx(dproj_sh, gathered["w_in"], side=_ReduceJob(("w_in",), [g_in], at=(0.0, 0.3, 0.8, 0.95)))
    big_grads["w_in"] = got[0]
    dx, g_norm1 = _rms_bwd(dxn, x2, norm1_w, dh1, name="norm1_bwd")


    small_g = dict(
        norm1_w=g_norm1, b_gate=jnp.concatenate([g_ba, g_bs], axis=1), attn_sinks=dsk[0:1, :16], ssd_conv_b=g_ssd_cb,
        dt_bias=dvec[0:1, :32], a_log=dvec[1:2, :32], d_skip=dvec[2:3, :32], ssd_norm_w=g_ssd_norm, norm2_w=g_norm2,
        ffn_conv_b=jnp.concatenate([g_ffn_cb[0], g_ffn_cb[1]], axis=1), final_norm_w=g_final, ssd_conv_w=g_ssd_cw,
        ffn_conv_w=jnp.concatenate([g_ffn_cw[0], g_ffn_cw[1]], axis=1))
    small_buf = _pack_small([small_g[n] for n in SMALL] + [loss_blk])
    small_sum = _all_reduce_small(small_buf)
    small_shapes = [(1, D_MODEL), (1, 2 * D_MODEL), (1, 16), (1, XBC_DIM), (1, 32), (1, 32), (1, 32), (1, D_INNER),
                    (1, D_MODEL), (1, 2 * D_FF), (D_MODEL,), (4, XBC_DIM), (3, 2 * D_FF), (1, 128)]
    small_list = _unpack_small(small_sum, small_shapes)
    loss = small_list[-1][0, 0]
    grads = dict(zip(SMALL, small_list[:-1]))
    grads["ssd_conv_w"] = lax.dynamic_slice_in_dim(grads["ssd_conv_w"], chip * (XBC_DIM // 4), XBC_DIM // 4, axis=1)
    grads["ffn_conv_w"] = lax.dynamic_slice_in_dim(grads["ffn_conv_w"], chip * (2 * D_FF // 4), 2 * D_FF // 4, axis=1)
    grads.update(big_grads)

    weights = dict(norm1_w=norm1_w, w_in=w_in, b_gate=b_gate, attn_sinks=attn_sinks, w_attn_o=w_attn_o, ssd_conv_w=ssd_conv_w,
                   ssd_conv_b=ssd_conv_b, dt_bias=dt_bias, a_log=a_log, d_skip=d_skip, ssd_norm_w=ssd_norm_w, w_ssd_o=w_ssd_o,
                   w_out=w_out, norm2_w=norm2_w, w_up=w_up, ffn_conv_w=ffn_conv_w, ffn_conv_b=ffn_conv_b, w_down=w_down,
                   final_norm_w=final_norm_w)
    ms = dict(norm1_w=m_norm1_w, w_in=m_w_in, b_gate=m_b_gate, attn_sinks=m_attn_sinks, w_attn_o=m_w_attn_o,
              ssd_conv_w=m_ssd_conv_w, ssd_conv_b=m_ssd_conv_b, dt_bias=m_dt_bias, a_log=m_a_log, d_skip=m_d_skip,
              ssd_norm_w=m_ssd_norm_w, w_ssd_o=m_w_ssd_o, w_out=m_w_out, norm2_w=m_norm2_w, w_up=m_w_up,
              ffn_conv_w=m_ffn_conv_w, ffn_conv_b=m_ffn_conv_b, w_down=m_w_down, final_norm_w=m_final_norm_w)
    vs = dict(norm1_w=v_norm1_w, w_in=v_w_in, b_gate=v_b_gate, attn_sinks=v_attn_sinks, w_attn_o=v_w_attn_o,
              ssd_conv_w=v_ssd_conv_w, ssd_conv_b=v_ssd_conv_b, dt_bias=v_dt_bias, a_log=v_a_log, d_skip=v_d_skip,
              ssd_norm_w=v_ssd_norm_w, w_ssd_o=v_w_ssd_o, w_out=v_w_out, norm2_w=v_norm2_w, w_up=v_w_up,
              ffn_conv_w=v_ffn_conv_w, ffn_conv_b=v_ffn_conv_b, w_down=v_w_down, final_norm_w=v_final_norm_w)
    order = list(weights)
    deltas, new_m, new_v = {}, {}, {}
    for n in BIG:
        shp = weights[n].shape
        res = _adamw(weights[n][0], grads[n], ms[n][0], vs[n][0], name="adamw_" + n)
        deltas[n], new_m[n], new_v[n], grads[n] = (a.reshape(shp) for a in res)
    smalls = [n for n in order if n not in BIG]
    as2d = lambda a: a.reshape(-1, a.shape[-1])
    res = _adamw_many(*[[as2d(src[n][0] if src[n].ndim == 3 else src[n]) for n in smalls] for src in (weights, grads, ms, vs)])
    for i, n in enumerate(smalls):
        deltas[n], new_m[n], new_v[n] = (res[q * len(smalls) + i].reshape(weights[n].shape) for q in range(3))
    out_grads = [grads[n].reshape(weights[n].shape) for n in order]
    return (loss, dx[None], *out_grads, *[deltas[n] for n in order], *[new_m[n] for n in order], *[new_v[n] for n in order])
```

```python
import functools

import jax
import jax.numpy as jnp
from jax import lax
from jax.experimental import pallas as pl
from jax.experimental.pallas import tpu as pltpu

F32 = jnp.float32
BF16 = jnp.bfloat16
HI = lax.Precision.HIGHEST

D_MODEL = 1024
Q_DIM = 1024
KV_DIM = 256
D_INNER = 2048
BC_DIM = 512
XBC_DIM = 3072
N_SSD_HEADS = 32
D_FF = 2816
IN_DIM = 8736
BLK = 128
EPS = 1e-5
NEG = -1e30

O_Q, O_K, O_V, O_Z, O_X, O_GA, O_GS, O_DT = 0, 1024, 1280, 1536, 3584, 6656, 7680, 8704
PW = 8960

ADAM_LR, ADAM_B1, ADAM_B2, ADAM_EPS, ADAM_WD, ADAM_STEP = 0.001, 0.9, 0.999, 1e-08, 0.01, 10

VMEM_LIMIT = 52 * 1024 * 1024
MESH = pl.DeviceIdType.MESH


def _cp(sem=None):
    return pltpu.CompilerParams(dimension_semantics=sem, vmem_limit_bytes=VMEM_LIMIT)


def _dot(a, b, prec=None):
    return jnp.dot(a, b, preferred_element_type=F32, precision=prec)


def _dot_nt(a, b, prec=None):
    return lax.dot_general(a, b, (((1,), (1,)), ((), ())), preferred_element_type=F32, precision=prec)


def _dot_tn(a, b, prec=None):
    return lax.dot_general(a, b, (((0,), (0,)), ((), ())), preferred_element_type=F32, precision=prec)


def _sigmoid(x):
    return 0.5 * jnp.tanh(0.5 * x) + 0.5


def _tile(n, want):
    t = min(n, want)
    while n % t:
        t -= 128
    return t


def _accumulate(acc, part, kk, nk, finish):
    if nk == 1:
        finish(part)
        return

    @pl.when(kk == 0)
    def _():
        acc[...] = part

    @pl.when(kk > 0)
    def _():
        acc[...] += part

    @pl.when(kk == nk - 1)
    def _():
        finish(acc[...])


def _mm(a, b, *, name, ta=False, tb=False, out_dtype=F32, resid=None, tm=1024, tn=1024, tk=1024, side=None):
    m, k = (a.shape[1], a.shape[0]) if ta else a.shape
    slots = b.ndim == 3
    if slots:
        n = b.shape[1] if tb else b.shape[0] * b.shape[2]
        tn, tk = (tn, b.shape[2]) if tb else (b.shape[2], tk)
    else:
        n = b.shape[0] if tb else b.shape[1]
    tm, tn, tk = _tile(m, tm), _tile(n, tn), _tile(k, tk)
    nk = k // tk
    dn = (((0 if ta else 1,), (1 if tb else 0,)), ((), ()))

    def body(*refs):
        if resid is None:
            a_ref, b_ref, o_ref, acc = refs
        else:
            a_ref, b_ref, r_ref, o_ref, acc = refs
        kk = pl.program_id(2)
        bv = b_ref[0] if slots else b_ref[...]
        part = lax.dot_general(a_ref[...].astype(BF16), bv.astype(BF16), dn, preferred_element_type=F32)

        def finish(r):
            if resid is not None:
                r = r + r_ref[...]
            o_ref[...] = r.astype(out_dtype)

        _accumulate(acc, part, kk, nk, finish)

    a_spec = pl.BlockSpec((tk, tm), lambda i, j, q: (q, i)) if ta else pl.BlockSpec((tm, tk), lambda i, j, q: (i, q))
    if slots:
        b_spec = (pl.BlockSpec((1, tn, tk), lambda i, j, q: (q, j, 0)) if tb
                  else pl.BlockSpec((1, tk, tn), lambda i, j, q: (j, q, 0)))
    else:
        b_spec = pl.BlockSpec((tn, tk), lambda i, j, q: (j, q)) if tb else pl.BlockSpec((tk, tn), lambda i, j, q: (q, j))
    o_spec = pl.BlockSpec((tm, tn), lambda i, j, q: (i, j))
    ins, specs = [a, b], [a_spec, b_spec]
    if resid is not None:
        ins.append(resid)
        specs.append(o_spec)
    own, extra = _hosted(
        body, name=name, grid=(m // tm, n // tn, nk), in_specs=specs, out_specs=[o_spec],
        out_shape=[jax.ShapeDtypeStruct((m, n), out_dtype)], scratch_shapes=[pltpu.VMEM((tm, tn), F32)],
        args=ins, sem=("parallel", "parallel", "arbitrary"), side=side)
    return own[0] if side is None else (own[0], extra)


def _rms_fwd(x, w, *, name, tm=512, with_t=False):
    s, d = x.shape
    tm = _tile(s, tm)

    def body(x_ref, w_ref, o_ref, *t_ref):
        xv = x_ref[...]
        r = lax.rsqrt(jnp.mean(xv * xv, axis=-1, keepdims=True) + EPS)
        y = (xv * r) * w_ref[...]
        o_ref[...] = y.astype(BF16)
        if with_t:
            t_ref[0][...] = y.T.astype(BF16)

    row = pl.BlockSpec((tm, d), lambda i: (i, 0))
    res = pl.pallas_call(
        body, name=name, grid=(s // tm,), in_specs=[row, pl.BlockSpec((1, d), lambda i: (0, 0))],
        out_specs=[row] + [pl.BlockSpec((d, tm), lambda i: (0, i))] * with_t,
        out_shape=[jax.ShapeDtypeStruct((s, d), BF16)] + [jax.ShapeDtypeStruct((d, s), BF16)] * with_t,
        compiler_params=_cp(("parallel",)),
    )(x, w)
    return res if with_t else res[0]


def _rms_bwd(dy, x, w, resid, *, name, tm=512):
    s, d = x.shape
    tm = _tile(s, tm)

    def body(dy_ref, x_ref, w_ref, r_ref, dx_ref, dw_ref):
        i = pl.program_id(0)
        xv = x_ref[...]
        r = lax.rsqrt(jnp.mean(xv * xv, axis=-1, keepdims=True) + EPS)
        xh = xv * r
        dyv = dy_ref[...]
        g = dyv * w_ref[...]
        dx_ref[...] = r_ref[...] + r * (g - xh * jnp.mean(g * xh, axis=-1, keepdims=True))
        part = jnp.sum(dyv * xh, axis=0, keepdims=True)

        @pl.when(i == 0)
        def _():
            dw_ref[...] = part

        @pl.when(i > 0)
        def _():
            dw_ref[...] += part

    row = pl.BlockSpec((tm, d), lambda i: (i, 0))
    vec = pl.BlockSpec((1, d), lambda i: (0, 0))
    return pl.pallas_call(
        body, name=name, grid=(s // tm,), in_specs=[row, row, vec, row], out_specs=[row, vec],
        out_shape=[jax.ShapeDtypeStruct((s, d), F32), jax.ShapeDtypeStruct((1, d), F32)],
        compiler_params=_cp(("arbitrary",)),
    )(dy, x, w, resid)


def _loss_bwd(h2, tgt, wf, *, tm=512):
    s, d = h2.shape
    tm = _tile(s, tm)

    def body(h_ref, t_ref, w_ref, dh_ref, loss_ref, dw_ref):
        i = pl.program_id(0)
        hv = h_ref[...]
        r = lax.rsqrt(jnp.mean(hv * hv, axis=-1, keepdims=True) + EPS)
        xh = hv * r
        wv = w_ref[...]
        e = xh * wv - t_ref[...]
        lpart = 0.5 * jnp.sum(jnp.mean(e * e, axis=-1, keepdims=True), axis=0, keepdims=True)
        dout = e * (1.0 / d)
        g = dout * wv
        dh_ref[...] = r * (g - xh * jnp.mean(g * xh, axis=-1, keepdims=True))
        part = jnp.sum(dout * xh, axis=0, keepdims=True)
        lrow = jnp.broadcast_to(lpart, (1, 128))

        @pl.when(i == 0)
        def _():
            dw_ref[...] = part
            loss_ref[...] = lrow

        @pl.when(i > 0)
        def _():
            dw_ref[...] += part
            loss_ref[...] += lrow

    row = pl.BlockSpec((tm, d), lambda i: (i, 0))
    vec = pl.BlockSpec((1, d), lambda i: (0, 0))
    return pl.pallas_call(
        body, name="loss_bwd", grid=(s // tm,), in_specs=[row, row, vec],
        out_specs=[row, pl.BlockSpec((1, 128), lambda i: (0, 0)), vec],
        out_shape=[jax.ShapeDtypeStruct((s, d), F32), jax.ShapeDtypeStruct((1, 128), F32),
                   jax.ShapeDtypeStruct((1, d), F32)],
        compiler_params=_cp(("arbitrary",)),
    )(h2, tgt, wf)


def _attn_mask(n):
    si = lax.broadcasted_iota(jnp.int32, (2 * BLK, 4 * BLK), 0)
    qi = lax.broadcasted_iota(jnp.int32, (2 * BLK, 4 * BLK), 1) & (BLK - 1)
    dist = BLK + qi - si
    kpos = n * BLK - BLK + si
    return (dist >= 0) & (dist < BLK) & (kpos >= 0)


def _attn_probs(q_ref, kc_ref, kp_ref, sk_ref, kvh, valid):
    rows = slice(kvh * 64, (kvh + 1) * 64)
    kt = jnp.concatenate([kp_ref[rows, :], kc_ref[rows, :]], axis=1).astype(BF16)
    qt = jnp.concatenate([q_ref[(kvh * 4 + g) * 64:(kvh * 4 + g + 1) * 64, :] for g in range(4)], axis=1).astype(BF16)
    s = _dot_tn(kt, qt) * 0.125
    s = jnp.where(valid, s, NEG)
    head = lax.broadcasted_iota(jnp.int32, (1, 4 * BLK), 1) >> 7
    sink = jnp.zeros((1, 4 * BLK), F32)
    for g in range(4):
        sink = jnp.where(head == g, sk_ref[0:1, kvh * 4 + g:kvh * 4 + g + 1], sink)
    m = jnp.maximum(jnp.max(s, axis=0, keepdims=True), sink)
    p = jnp.where(valid, jnp.exp(s - m), 0.0)
    es = jnp.exp(sink - m)
    inv = 1.0 / (jnp.sum(p, axis=0, keepdims=True) + es)
    return qt, kt, p * inv, es * inv


def _attn_in_specs(cur, prev):
    return [pl.BlockSpec((Q_DIM, BLK), lambda n: (0, cur(n))),
            pl.BlockSpec((KV_DIM, BLK), lambda n: (O_K // KV_DIM, cur(n))),
            pl.BlockSpec((KV_DIM, BLK), lambda n: (O_K // KV_DIM, prev(n))),
            pl.BlockSpec((KV_DIM, BLK), lambda n: (O_V // KV_DIM, cur(n))),
            pl.BlockSpec((KV_DIM, BLK), lambda n: (O_V // KV_DIM, prev(n))),
            pl.BlockSpec((1, 128), lambda n: (0, 0))]


def _attn_fwd(qkvt, sinks, side=None):
    s = qkvt.shape[1]
    nb = s // BLK

    def body(q_ref, kc_ref, kp_ref, vc_ref, vp_ref, sk_ref, o_ref):
        valid = _attn_mask(pl.program_id(0))
        for kvh in range(4):
            rows = slice(kvh * 64, (kvh + 1) * 64)
            _, _, probs, _ = _attn_probs(q_ref, kc_ref, kp_ref, sk_ref, kvh, valid)
            vt = jnp.concatenate([vp_ref[rows, :], vc_ref[rows, :]], axis=1).astype(BF16)
            o = _dot(vt, probs.astype(BF16))
            for g in range(4):
                h = kvh * 4 + g
                o_ref[h * 64:(h + 1) * 64, :] = o[:, g * BLK:(g + 1) * BLK].astype(BF16)

    own, extra = _hosted(
        body, name="attn_fwd", grid=(nb,), in_specs=_attn_in_specs(lambda n: n, lambda n: jnp.maximum(n - 1, 0)),
        out_specs=[pl.BlockSpec((Q_DIM, BLK), lambda n: (0, n))],
        out_shape=[jax.ShapeDtypeStruct((Q_DIM, s), BF16)], scratch_shapes=[],
        args=(qkvt, qkvt, qkvt, qkvt, qkvt, sinks), sem=("parallel",), side=side)
    return own[0] if side is None else (own[0], extra)


def _attn_bwd(qkvt, sinks, o, do, side=None):
    s = qkvt.shape[1]
    nb = s // BLK

    def body(q_ref, kc_ref, kp_ref, vc_ref, vp_ref, sk_ref, o_ref, do_ref, dq_ref, dk_ref, dv_ref, dsk_ref, ck, cv, nk, nv):
        n = pl.program_id(0)

        @pl.when(n == 0)
        def _():
            ck[...] = jnp.zeros_like(ck)
            cv[...] = jnp.zeros_like(cv)
            dsk_ref[...] = jnp.zeros_like(dsk_ref)

        @pl.when(n < nb)
        def _():
            valid = _attn_mask(n)
            lane = lax.broadcasted_iota(jnp.int32, (1, 128), 1)
            dsk = jnp.zeros((1, 128), F32)
            for kvh in range(4):
                rows = slice(kvh * 64, (kvh + 1) * 64)
                qt, kt, probs, psink = _attn_probs(q_ref, kc_ref, kp_ref, sk_ref, kvh, valid)
                vt = jnp.concatenate([vp_ref[rows, :], vc_ref[rows, :]], axis=1).astype(BF16)
                heads = [slice((kvh * 4 + g) * 64, (kvh * 4 + g + 1) * 64) for g in range(4)]
                dot = jnp.concatenate([do_ref[hh, :] for hh in heads], axis=1)
                ot = jnp.concatenate([o_ref[hh, :] for hh in heads], axis=1).astype(F32)
                delta = jnp.sum(dot * ot, axis=0, keepdims=True)
                dot16 = dot.astype(BF16)
                dp = _dot_tn(vt, dot16)
                ds = (probs * (dp - delta) * 0.125).astype(BF16)
                dqt = _dot(kt, ds)
                nk[rows, :] = _dot_nt(qt, ds)
                nv[rows, :] = _dot_nt(dot16, probs.astype(BF16))
                sd = psink * delta
                for g in range(4):
                    dq_ref[heads[g], :] = dqt[:, g * BLK:(g + 1) * BLK].astype(BF16)
                    val = -jnp.sum(sd[:, g * BLK:(g + 1) * BLK], axis=1, keepdims=True)
                    dsk = dsk + jnp.where(lane == kvh * 4 + g, val, 0.0)
            dsk_ref[0:1, :] += dsk
            dk_ref[...] = (ck[...] + nk[:, :BLK]).astype(BF16)
            dv_ref[...] = (cv[...] + nv[:, :BLK]).astype(BF16)
            ck[...] = nk[:, BLK:]
            cv[...] = nv[:, BLK:]

        @pl.when(n == nb)
        def _():
            dk_ref[...] = ck[...].astype(BF16)
            dv_ref[...] = cv[...].astype(BF16)

    cur = lambda n: jnp.minimum(n, nb - 1)
    prev = lambda n: jnp.maximum(jnp.minimum(n, nb - 1) - 1, 0)
    outb = lambda n: jnp.maximum(n - 1, 0)
    own, extra = _hosted(
        body, name="attn_bwd", grid=(nb + 1,),
        in_specs=_attn_in_specs(cur, prev) + [pl.BlockSpec((Q_DIM, BLK), lambda n: (0, cur(n))),
                                              pl.BlockSpec((Q_DIM, BLK), lambda n: (0, cur(n)))],
        out_specs=[pl.BlockSpec((Q_DIM, BLK), lambda n: (0, cur(n))),
                   pl.BlockSpec((KV_DIM, BLK), lambda n: (0, outb(n))),
                   pl.BlockSpec((KV_DIM, BLK), lambda n: (0, outb(n))),
                   pl.BlockSpec((8, 128), lambda n: (0, 0))],
        out_shape=[jax.ShapeDtypeStruct((Q_DIM, s), BF16), jax.ShapeDtypeStruct((KV_DIM, s), BF16),
                   jax.ShapeDtypeStruct((KV_DIM, s), BF16), jax.ShapeDtypeStruct((8, 128), F32)],
        scratch_shapes=[pltpu.VMEM((KV_DIM, BLK), F32)] * 2 + [pltpu.VMEM((KV_DIM, 2 * BLK), F32)] * 2,
        args=(qkvt, qkvt, qkvt, qkvt, qkvt, sinks, o, do), sem=("arbitrary",), side=side)
    return own if side is None else (own, extra)


def _shift_down(x, j):
    if j == 0:
        return x
    row = lax.broadcasted_iota(jnp.int32, x.shape, 0)
    return jnp.where(row >= j, pltpu.roll(x, j, 0), 0.0)


def _shift_up(x, j):
    if j == 0:
        return x
    s = x.shape[0]
    row = lax.broadcasted_iota(jnp.int32, x.shape, 0)
    return jnp.where(row < s - j, pltpu.roll(x, s - j, 0), 0.0)


def _conv(x, w_ref, b_ref):
    kk = w_ref.shape[0]
    y = _shift_down(x, kk - 1) * w_ref[0:1, :]
    for q in range(1, kk):
        y = y + _shift_down(x, kk - 1 - q) * w_ref[q:q + 1, :]
    return y + b_ref[...]


def _conv_bwd(dy, x, w_ref, dx_dtype):
    kk = w_ref.shape[0]
    dx = _shift_up(dy, kk - 1) * w_ref[0:1, :]
    dws = [jnp.sum(dy * _shift_down(x, kk - 1), axis=0, keepdims=True)]
    for q in range(1, kk):
        dx = dx + _shift_up(dy, kk - 1 - q) * w_ref[q:q + 1, :]
        dws.append(jnp.sum(dy * _shift_down(x, kk - 1 - q), axis=0, keepdims=True))
    return dx.astype(dx_dtype), dws, jnp.sum(dy, axis=0, keepdims=True)


def _dsilu(y, sg):
    return sg * (1.0 + y * (1.0 - sg))


CT = 256


def _ssd_conv_fwd(proj, w, b):
    s = proj.shape[0]

    def body(x_ref, w_ref, b_ref, o_ref):
        y = _conv(x_ref[...], w_ref, b_ref)
        o_ref[...] = y * _sigmoid(y)

    return pl.pallas_call(
        body, name="ssd_conv_fwd", grid=(XBC_DIM // CT,),
        in_specs=[pl.BlockSpec((s, CT), lambda i: (0, O_X // CT + i)), pl.BlockSpec((4, CT), lambda i: (0, i)),
                  pl.BlockSpec((1, CT), lambda i: (0, i))],
        out_specs=pl.BlockSpec((s, CT), lambda i: (0, i)),
        out_shape=jax.ShapeDtypeStruct((s, XBC_DIM), F32), compiler_params=_cp(("parallel",)),
    )(proj, w, b)


def _ssd_conv_bwd(dact, proj, w, b):
    s = proj.shape[0]

    def body(d_ref, x_ref, w_ref, b_ref, dx_ref, dw_ref, db_ref):
        x = x_ref[...]
        y = _conv(x, w_ref, b_ref)
        dy = d_ref[...] * _dsilu(y, _sigmoid(y))
        dx, dws, db = _conv_bwd(dy, x, w_ref, BF16)
        dx_ref[...] = dx
        for q in range(4):
            dw_ref[q:q + 1, :] = dws[q]
        db_ref[...] = db

    return pl.pallas_call(
        body, name="ssd_conv_bwd", grid=(XBC_DIM // CT,),
        in_specs=[pl.BlockSpec((s, CT), lambda i: (0, i)), pl.BlockSpec((s, CT), lambda i: (0, O_X // CT + i)),
                  pl.BlockSpec((4, CT), lambda i: (0, i)), pl.BlockSpec((1, CT), lambda i: (0, i))],
        out_specs=[pl.BlockSpec((s, CT), lambda i: (0, i)), pl.BlockSpec((4, CT), lambda i: (0, i)),
                   pl.BlockSpec((1, CT), lambda i: (0, i))],
        out_shape=[jax.ShapeDtypeStruct((s, XBC_DIM), BF16), jax.ShapeDtypeStruct((4, XBC_DIM), F32),
                   jax.ShapeDtypeStruct((1, XBC_DIM), F32)],
        compiler_params=_cp(("parallel",)),
    )(dact, proj, w, b)


NFT = D_FF // CT


def _ffn_act_fwd(up, w, b):
    s = up.shape[0]

    def body(v_ref, g_ref, wv_ref, wg_ref, bv_ref, bg_ref, o_ref):
        val = _conv(v_ref[...], wv_ref, bv_ref)
        gt = _conv(g_ref[...], wg_ref, bg_ref)
        o_ref[...] = ((gt * _sigmoid(gt)) * val).astype(BF16)

    col = lambda off: (lambda i: (0, off + i))
    return pl.pallas_call(
        body, name="ffn_act_fwd", grid=(NFT,),
        in_specs=[pl.BlockSpec((s, CT), col(0)), pl.BlockSpec((s, CT), col(NFT)),
                  pl.BlockSpec((3, CT), col(0)), pl.BlockSpec((3, CT), col(NFT)),
                  pl.BlockSpec((1, CT), col(0)), pl.BlockSpec((1, CT), col(NFT))],
        out_specs=pl.BlockSpec((s, CT), col(0)),
        out_shape=jax.ShapeDtypeStruct((s, D_FF), BF16), compiler_params=_cp(("parallel",)),
    )(up, up, w, w, b, b)


def _ffn_act_bwd(dact, up, w, b):
    s = up.shape[0]

    def body(d_ref, v_ref, g_ref, wv_ref, wg_ref, bv_ref, bg_ref, dx_ref, dw_ref, db_ref):
        xv, xg = v_ref[...], g_ref[...]
        val = _conv(xv, wv_ref, bv_ref)
        gt = _conv(xg, wg_ref, bg_ref)
        sg = _sigmoid(gt)
        d = d_ref[...]
        for half, (dy, x, w_ref) in enumerate(((d * (gt * sg), xv, wv_ref), (d * val * _dsilu(gt, sg), xg, wg_ref))):
            dx, dws, db = _conv_bwd(dy, x, w_ref, BF16)
            dx_ref[half] = dx
            for q in range(3):
                dw_ref[half, q:q + 1, :] = dws[q]
            db_ref[half] = db

    col = lambda off: (lambda i: (0, off + i))
    both = lambda i: (0, 0, i)
    return pl.pallas_call(
        body, name="ffn_act_bwd", grid=(NFT,),
        in_specs=[pl.BlockSpec((s, CT), col(0)), pl.BlockSpec((s, CT), col(0)), pl.BlockSpec((s, CT), col(NFT)),
                  pl.BlockSpec((3, CT), col(0)), pl.BlockSpec((3, CT), col(NFT)),
                  pl.BlockSpec((1, CT), col(0)), pl.BlockSpec((1, CT), col(NFT))],
        out_specs=[pl.BlockSpec((2, s, CT), both), pl.BlockSpec((2, 3, CT), both), pl.BlockSpec((2, 1, CT), both)],
        out_shape=[jax.ShapeDtypeStruct((2, s, D_FF), BF16), jax.ShapeDtypeStruct((2, 3, D_FF), F32),
                   jax.ShapeDtypeStruct((2, 1, D_FF), F32)],
        compiler_params=_cp(("parallel",)),
    )(dact, up, up, w, w, b, b)


def _expand_mat():
    r = lax.broadcasted_iota(jnp.int32, (128, D_INNER), 0)
    c = lax.broadcasted_iota(jnp.int32, (128, D_INNER), 1)
    return ((c >> 6) == r).astype(BF16)


def _reduce_mat():
    r = lax.broadcasted_iota(jnp.int32, (D_INNER, 128), 0)
    c = lax.broadcasted_iota(jnp.int32, (D_INNER, 128), 1)
    return ((r >> 6) == c).astype(BF16)


def _split(v, parts):
    out = []
    for _ in range(parts - 1):
        p = v.astype(BF16)
        out.append(p)
        v = v - p.astype(F32)
    out.append(v.astype(BF16))
    return out


def _sel_dot(v, sel, parts):
    acc = None
    for p in reversed(_split(v, parts)):
        t = _dot(p, sel)
        acc = t if acc is None else acc + t
    return acc


def _row8(v):
    return jnp.broadcast_to(v, (8, v.shape[1]))


def _tril():
    r = lax.broadcasted_iota(jnp.int32, (BLK, BLK), 0)
    c = lax.broadcasted_iota(jnp.int32, (BLK, BLK), 1)
    return r >= c


def _softplus(x):
    return jnp.maximum(x, 0.0) + jnp.log(1.0 + jnp.exp(-jnp.abs(x)))


def _ssd_common(dtraw_ref, dtb_ref, alog_ref):
    causal = _tril()
    e_mat = _expand_mat()
    a_neg = -jnp.exp(alog_ref[...])
    dt = _softplus(dtraw_ref[...] + dtb_ref[...])
    a_cs = _dot(causal.astype(F32), dt * a_neg, HI)
    a_cs_t = a_cs.T
    dt_x = _sel_dot(dt, e_mat, 3)
    acs_x = _sel_dot(a_cs, e_mat, 3)
    alast_x = acs_x[BLK - 1:BLK, :]
    ea_x = jnp.exp(acs_x)
    ds_x = jnp.exp(alast_x - acs_x)
    elast_x = jnp.exp(alast_x)
    return causal, e_mat, a_neg, dt, a_cs, a_cs_t, dt_x, ea_x, ds_x, elast_x


def _decay(a_cs, a_cs_t, h, causal):
    seg = a_cs[:, h:h + 1] - a_cs_t[h:h + 1, :]
    return jnp.where(causal, jnp.exp(jnp.where(causal, seg, 0.0)), 0.0)


def _ssd_fwd(xbc, proj, dt_bias, a_log, d_skip, side=None):
    s = xbc.shape[0]
    nc = s // BLK

    def body(xs_ref, b_ref, c_ref, dtraw_ref, dtb_ref, alog_ref, dskip_ref, y_ref, hp_ref, h_scr, xc16):
        @pl.when(pl.program_id(0) == 0)
        def _():
            h_scr[...] = jnp.zeros_like(h_scr)

        causal, e_mat, _, _, a_cs, a_cs_t, dt_x, ea_x, ds_x, elast_x = _ssd_common(dtraw_ref, dtb_ref, alog_ref)
        dskip_x = _sel_dot(_row8(dskip_ref[...]), e_mat, 3)[0:1]
        xs = xs_ref[...]
        xc = xs * dt_x
        xc16[...] = xc.astype(BF16)
        xcd = (xc * ds_x).astype(BF16)
        hp_ref[0] = h_scr[...]
        for g in range(4):
            gs = slice(g * 512, (g + 1) * 512)
            cg = c_ref[:, g * 128:(g + 1) * 128].astype(BF16)
            bg = b_ref[:, g * 128:(g + 1) * 128].astype(BF16)
            cb = _dot_nt(cg, bg)
            hg = h_scr[:, gs]
            yoff = _dot(cg, hg.astype(BF16)) * ea_x[:, gs]
            for j in range(8):
                h = g * 8 + j
                hsl = slice(h * 64, (h + 1) * 64)
                mm = (cb * _decay(a_cs, a_cs_t, h, causal)).astype(BF16)
                y_ref[:, hsl] = _dot(mm, xc16[:, hsl])
            y_ref[:, gs] += yoff + xs[:, gs] * dskip_x[:, gs]
            h_scr[:, gs] = hg * elast_x[:, gs] + _dot_tn(bg, xcd[:, gs])

    vec = pl.BlockSpec((1, 128), lambda c: (0, 0))
    own, extra = _hosted(
        body, name="ssd_fwd", grid=(nc,),
        in_specs=[pl.BlockSpec((BLK, D_INNER), lambda c: (c, 0)),
                  pl.BlockSpec((BLK, BC_DIM), lambda c: (c, D_INNER // BC_DIM)),
                  pl.BlockSpec((BLK, BC_DIM), lambda c: (c, D_INNER // BC_DIM + 1)),
                  pl.BlockSpec((BLK, 128), lambda c: (c, O_DT // 128)), vec, vec, vec],
        out_specs=[pl.BlockSpec((BLK, D_INNER), lambda c: (c, 0)),
                   pl.BlockSpec((1, 128, D_INNER), lambda c: (c, 0, 0))],
        out_shape=[jax.ShapeDtypeStruct((s, D_INNER), F32), jax.ShapeDtypeStruct((nc, 128, D_INNER), F32)],
        scratch_shapes=[pltpu.VMEM((128, D_INNER), F32), pltpu.VMEM((BLK, D_INNER), BF16)],
        args=(xbc, xbc, xbc, proj, dt_bias, a_log, d_skip), sem=("arbitrary",), side=side)
    return own if side is None else (own, extra)


def _ssd_bwd(xbc, proj, dt_bias, a_log, d_skip, hprev, dy, side=None):
    s = xbc.shape[0]
    nc = s // BLK

    def body(xs_ref, b_ref, c_ref, dtraw_ref, dtb_ref, alog_ref, dskip_ref, hp_ref, dy_ref,
             dxbc_ref, ddt_ref, dvec_ref, dh_scr, xc16, dy16, dxc_scr, dacs_r, tdiff):
        step = pl.program_id(0)
        dacs_r[...] = jnp.zeros_like(dacs_r)

        @pl.when(step == 0)
        def _():
            dh_scr[...] = jnp.zeros_like(dh_scr)
            dvec_ref[...] = jnp.zeros_like(dvec_ref)

        causal, e_mat, a_neg, dt, a_cs, a_cs_t, dt_x, ea_x, ds_x, elast_x = _ssd_common(dtraw_ref, dtb_ref, alog_ref)
        r_mat = _reduce_mat()
        lane = lax.broadcasted_iota(jnp.int32, (1, 128), 1)
        dskip_x = _sel_dot(_row8(dskip_ref[...]), e_mat, 3)[0:1]
        xs = xs_ref[...]
        dy = dy_ref[...]
        xc = xs * dt_x
        xcd = xc * ds_x
        xc16[...] = xc.astype(BF16)
        dy16[...] = dy.astype(BF16)
        dyea = dy * ea_x
        dh = dh_scr[...]
        hp = hp_ref[0]
        dalast_x = jnp.sum(dh * hp, axis=0, keepdims=True) * elast_x
        dacs = jnp.zeros((BLK, 128), F32)
        for g in range(4):
            gs = slice(g * 512, (g + 1) * 512)
            bsl = slice(g * 128, (g + 1) * 128)
            cg = c_ref[:, bsl].astype(BF16)
            bg = b_ref[:, bsl].astype(BF16)
            cb = _dot_nt(cg, bg)
            hg16 = hp[:, gs].astype(BF16)
            dhg16 = dh[:, gs].astype(BF16)
            raw = _dot(cg, hg16)
            draw16 = dyea[:, gs].astype(BF16)
            dcg = _dot_nt(draw16, hg16)
            dhp_g = _dot_tn(cg, draw16)
            dbg = _dot_nt(xcd[:, gs].astype(BF16), dhg16)
            dxcd = _dot(bg, dhg16)
            dcb = jnp.zeros((BLK, BLK), F32)
            for j in range(8):
                h = g * 8 + j
                hsl = slice(h * 64, (h + 1) * 64)
                decay = _decay(a_cs, a_cs_t, h, causal)
                m = cb * decay
                dm = _dot_nt(dy16[:, hsl], xc16[:, hsl])
                dxc_scr[:, hsl] = _dot_tn(m.astype(BF16), dy16[:, hsl])
                dcb = dcb + dm * decay
                dseg = dm * m
                oneh = jnp.where(lane == h, 1.0, 0.0)
                dacs = dacs + jnp.sum(dseg, axis=1, keepdims=True) * oneh
                dacs_r[h:h + 1, :] = jnp.sum(dseg, axis=0, keepdims=True)
            dcb16 = dcb.astype(BF16)
            dcg = dcg + _dot(dcb16, bg)
            dbg = dbg + _dot_tn(dcb16, cg)
            dxbc_ref[:, D_INNER + g * 128:D_INNER + (g + 1) * 128] = dbg
            dxbc_ref[:, D_INNER + BC_DIM + g * 128:D_INNER + BC_DIM + (g + 1) * 128] = dcg
            dxc_scr[:, gs] += dxcd * ds_x[:, gs]
            dh_scr[:, gs] = dh[:, gs] * elast_x[:, gs] + dhp_g
            tst = dxcd * xcd[:, gs]
            tdiff[:, gs] = dy[:, gs] * (raw * ea_x[:, gs]) - tst
            tdiff[BLK - 1:BLK, gs] += jnp.sum(tst, axis=0, keepdims=True)
        dxc = dxc_scr[...]
        row = lax.broadcasted_iota(jnp.int32, (BLK, D_INNER), 0)
        tfull = tdiff[...] + jnp.where(row == BLK - 1, dalast_x, 0.0)
        dacs = dacs + _sel_dot(tfull, r_mat, 2) - dacs_r[...].T
        da = _dot_tn(causal.astype(F32), dacs, HI)
        ddt = da * a_neg + _sel_dot(dxc * xs, r_mat, 2)
        lmask = lax.broadcasted_iota(jnp.int32, (BLK, 128), 1) < N_SSD_HEADS
        ddtraw = jnp.where(lmask, ddt * _sigmoid(dtraw_ref[...] + dtb_ref[...]), 0.0)
        ddt_ref[...] = ddtraw.astype(BF16)
        dxbc_ref[:, 0:D_INNER] = dy * dskip_x + dxc * dt_x
        dvec_ref[0:1, :] += jnp.sum(ddtraw, axis=0, keepdims=True)
        dvec_ref[1:2, :] += jnp.where(lane < N_SSD_HEADS, jnp.sum(da * dt, axis=0, keepdims=True) * a_neg, 0.0)
        dvec_ref[2:3, :] += _sel_dot(_row8(jnp.sum(dy * xs, axis=0, keepdims=True)), r_mat, 3)[0:1]

    rev = lambda c: nc - 1 - c
    vec = pl.BlockSpec((1, 128), lambda c: (0, 0))
    own, extra = _hosted(
        body, name="ssd_bwd", grid=(nc,),
        in_specs=[pl.BlockSpec((BLK, D_INNER), lambda c: (rev(c), 0)),
                  pl.BlockSpec((BLK, BC_DIM), lambda c: (rev(c), D_INNER // BC_DIM)),
                  pl.BlockSpec((BLK, BC_DIM), lambda c: (rev(c), D_INNER // BC_DIM + 1)),
                  pl.BlockSpec((BLK, 128), lambda c: (rev(c), O_DT // 128)), vec, vec, vec,
                  pl.BlockSpec((1, 128, D_INNER), lambda c: (rev(c), 0, 0)),
                  pl.BlockSpec((BLK, D_INNER), lambda c: (rev(c), 0))],
        out_specs=[pl.BlockSpec((BLK, XBC_DIM), lambda c: (rev(c), 0)),
                   pl.BlockSpec((BLK, 128), lambda c: (rev(c), 0)),
                   pl.BlockSpec((8, 128), lambda c: (0, 0))],
        out_shape=[jax.ShapeDtypeStruct((s, XBC_DIM), F32), jax.ShapeDtypeStruct((s, 128), BF16),
                   jax.ShapeDtypeStruct((8, 128), F32)],
        scratch_shapes=[pltpu.VMEM((128, D_INNER), F32), pltpu.VMEM((BLK, D_INNER), BF16),
                        pltpu.VMEM((BLK, D_INNER), BF16), pltpu.VMEM((BLK, D_INNER), F32),
                        pltpu.VMEM((128, BLK), F32), pltpu.VMEM((BLK, D_INNER), F32)],
        args=(xbc, xbc, xbc, proj, dt_bias, a_log, d_skip, hprev, dy), sem=("arbitrary",), side=side)
    return own if side is None else (own, extra)


GW = 512


def _gate_norm_fwd(y, proj, wn, *, tm=512):
    s = y.shape[0]
    tm = _tile(s, tm)

    def body(y_ref, z_ref, w_ref, o_ref):
        z = z_ref[...]
        y2 = y_ref[...] * (z * _sigmoid(z))
        r = lax.rsqrt(jnp.mean(y2 * y2, axis=-1, keepdims=True) + EPS)
        o_ref[...] = ((y2 * r) * w_ref[...]).astype(BF16)

    return pl.pallas_call(
        body, name="gate_norm_fwd", grid=(s // tm, 4),
        in_specs=[pl.BlockSpec((tm, GW), lambda i, g: (i, g)), pl.BlockSpec((tm, GW), lambda i, g: (i, O_Z // GW + g)),
                  pl.BlockSpec((1, GW), lambda i, g: (0, g))],
        out_specs=pl.BlockSpec((tm, GW), lambda i, g: (i, g)),
        out_shape=jax.ShapeDtypeStruct((s, D_INNER), BF16), compiler_params=_cp(("parallel", "parallel")),
    )(y, proj, wn)


def _gate_norm_bwd(dyn, y, proj, wn, *, tm=512):
    s = y.shape[0]
    tm = _tile(s, tm)

    def body(d_ref, y_ref, z_ref, w_ref, dy_ref, dz_ref, dw_ref):
        i = pl.program_id(1)
        z = z_ref[...]
        sg = _sigmoid(z)
        sz = z * sg
        yv = y_ref[...]
        y2 = yv * sz
        r = lax.rsqrt(jnp.mean(y2 * y2, axis=-1, keepdims=True) + EPS)
        xh = y2 * r
        dv = d_ref[...]
        g = dv * w_ref[...]
        dy2 = r * (g - xh * jnp.mean(g * xh, axis=-1, keepdims=True))
        dy_ref[...] = dy2 * sz
        dz_ref[...] = (dy2 * yv * _dsilu(z, sg)).astype(BF16)
        part = jnp.sum(dv * xh, axis=0, keepdims=True)

        @pl.when(i == 0)
        def _():
            dw_ref[...] = part

        @pl.when(i > 0)
        def _():
            dw_ref[...] += part

    blk = pl.BlockSpec((tm, GW), lambda g, i: (i, g))
    vec = pl.BlockSpec((1, GW), lambda g, i: (0, g))
    return pl.pallas_call(
        body, name="gate_norm_bwd", grid=(4, s // tm),
        in_specs=[blk, blk, pl.BlockSpec((tm, GW), lambda g, i: (i, O_Z // GW + g)), vec],
        out_specs=[blk, blk, vec],
        out_shape=[jax.ShapeDtypeStruct((s, D_INNER), F32), jax.ShapeDtypeStruct((s, D_INNER), BF16),
                   jax.ShapeDtypeStruct((1, D_INNER), F32)],
        compiler_params=_cp(("parallel", "arbitrary")),
    )(dyn, y, proj, wn)


def _merge_fwd(proj, b_gate, attn, ssd_out, *, tm=512):
    s = attn.shape[0]
    tm = _tile(s, tm)

    def body(ga_ref, gs_ref, ba_ref, bs_ref, a_ref, s_ref, o_ref):
        ga = _sigmoid(ga_ref[...] + ba_ref[...])
        gs = _sigmoid(gs_ref[...] + bs_ref[...])
        o_ref[...] = (ga * a_ref[...] + gs * s_ref[...]).astype(BF16)

    blk = pl.BlockSpec((tm, GW), lambda i, j: (i, j))
    return pl.pallas_call(
        body, name="merge_fwd", grid=(s // tm, 2),
        in_specs=[pl.BlockSpec((tm, GW), lambda i, j: (i, O_GA // GW + j)),
                  pl.BlockSpec((tm, GW), lambda i, j: (i, O_GS // GW + j)),
                  pl.BlockSpec((1, GW), lambda i, j: (0, j)), pl.BlockSpec((1, GW), lambda i, j: (0, 2 + j)), blk, blk],
        out_specs=blk, out_shape=jax.ShapeDtypeStruct((s, D_MODEL), BF16),
        compiler_params=_cp(("parallel", "parallel")),
    )(proj, proj, b_gate, b_gate, attn, ssd_out)


def _merge_bwd(dm, proj, b_gate, attn, ssd_out, *, tm=512):
    s = attn.shape[0]
    tm = _tile(s, tm)

    def body(d_ref, ga_ref, gs_ref, ba_ref, bs_ref, a_ref, s_ref, da_ref, ds_ref, dga_ref, dgs_ref, dba_ref, dbs_ref):
        i = pl.program_id(1)
        ga = _sigmoid(ga_ref[...] + ba_ref[...])
        gs = _sigmoid(gs_ref[...] + bs_ref[...])
        d = d_ref[...]
        da_ref[...] = (d * ga).astype(BF16)
        ds_ref[...] = (d * gs).astype(BF16)
        dga = d * a_ref[...] * (ga * (1.0 - ga))
        dgs = d * s_ref[...] * (gs * (1.0 - gs))
        dga_ref[...] = dga.astype(BF16)
        dgs_ref[...] = dgs.astype(BF16)
        pa = jnp.sum(dga, axis=0, keepdims=True)
        ps = jnp.sum(dgs, axis=0, keepdims=True)

        @pl.when(i == 0)
        def _():
            dba_ref[...] = pa
            dbs_ref[...] = ps

        @pl.when(i > 0)
        def _():
            dba_ref[...] += pa
            dbs_ref[...] += ps

    blk = pl.BlockSpec((tm, GW), lambda j, i: (i, j))
    vec = pl.BlockSpec((1, GW), lambda j, i: (0, j))
    sd = jax.ShapeDtypeStruct((s, D_MODEL), BF16)
    vd = jax.ShapeDtypeStruct((1, D_MODEL), F32)
    return pl.pallas_call(
        body, name="merge_bwd", grid=(2, s // tm),
        in_specs=[blk, pl.BlockSpec((tm, GW), lambda j, i: (i, O_GA // GW + j)),
                  pl.BlockSpec((tm, GW), lambda j, i: (i, O_GS // GW + j)),
                  vec, pl.BlockSpec((1, GW), lambda j, i: (0, 2 + j)), blk, blk],
        out_specs=[blk, blk, blk, blk, vec, vec], out_shape=[sd, sd, sd, sd, vd, vd],
        compiler_params=_cp(("parallel", "arbitrary")),
    )(dm, proj, proj, b_gate, b_gate, attn, ssd_out)


def _adamw_math(w, g, m, v):
    mn = ADAM_B1 * m + (1.0 - ADAM_B1) * g
    vn = ADAM_B2 * v + (1.0 - ADAM_B2) * (g * g)
    m_hat = mn / (1.0 - ADAM_B1 ** ADAM_STEP)
    v_hat = vn / (1.0 - ADAM_B2 ** ADAM_STEP)
    return -ADAM_LR * (m_hat / (jnp.sqrt(v_hat) + ADAM_EPS) + ADAM_WD * w), mn, vn


def _adamw_many(ws, gs, ms, vs):
    n = len(ws)

    def body(*refs):
        outs = refs[4 * n:]
        for i in range(n):
            res = _adamw_math(*[refs[q * n + i][...] for q in range(4)])
            for q in range(3):
                outs[q * n + i][...] = res[q]

    return pl.pallas_call(body, name="adamw_small", out_shape=[jax.ShapeDtypeStruct(w.shape, F32) for w in ws] * 3,
                          compiler_params=_cp())(*ws, *gs, *ms, *vs)


def _adamw(w, g, m, v, *, name, tm=128):
    r, c = w.shape
    tm = r if (r < tm or r % tm) else tm

    def body(w_ref, g_ref, m_ref, v_ref, d_ref, nm_ref, nv_ref, g_out):
        gv = g_ref[:, :c]
        d_ref[...], nm_ref[...], nv_ref[...] = _adamw_math(w_ref[...], gv, m_ref[...], v_ref[...])
        g_out[...] = gv

    blk = pl.BlockSpec((tm, c), lambda i: (i, 0))
    sd = jax.ShapeDtypeStruct((r, c), F32)
    return pl.pallas_call(
        body, name=name, grid=(r // tm,), in_specs=[blk, pl.BlockSpec((tm, g.shape[1]), lambda i: (i, 0)), blk, blk],
        out_specs=[blk] * 4, out_shape=[sd] * 4, compiler_params=_cp(("parallel",)),
    )(w, g, m, v)


ANY = pl.BlockSpec(memory_space=pl.ANY)
N_CHIPS = 4


def _chip_of(k, x, y):
    return (x ^ (k >> 1), y ^ (k & 1))


def _all_gather_small(shard):
    r, c = shard.shape
    hr = r // 2

    def body(sh_ref, out_ref, send_sems, recv_sems, local_sem):
        x, y, cc = lax.axis_index("x"), lax.axis_index("y"), lax.axis_index("c")

        def half(px, py, pc):
            return out_ref.at[2 * px + py, pl.ds(pc * hr, hr), :]

        def copy(k, px, py, pc, to, src=None):
            return pltpu.make_async_remote_copy(
                src_ref=half(px, py, pc) if src is None else src, dst_ref=half(px, py, pc),
                send_sem=send_sems.at[k], recv_sem=recv_sems.at[k], device_id=to, device_id_type=MESH)

        mine = pltpu.make_async_copy(sh_ref, out_ref.at[2 * x + y], local_sem)
        mine.start()
        chips = [_chip_of(k, x, y) for k in (1, 2, 3)]
        first = [copy(j, x, y, cc, (*chip, cc), src=sh_ref.at[pl.ds(cc * hr, hr), :]) for j, chip in enumerate(chips)]
        for cp in first:
            cp.start()
        passed = [copy(3 + j, *chip, cc, (x, y, 1 - cc)) for j, chip in enumerate(chips)]
        for j, chip in enumerate(chips):
            copy(j, *chip, cc, (x, y, cc)).wait_recv()
            passed[j].start()
        for j, chip in enumerate(chips):
            copy(3 + j, *chip, 1 - cc, (x, y, cc)).wait_recv()
        for cp in first + passed:
            cp.wait_send()
        mine.wait()

    return pl.pallas_call(
        body, name="all_gather_small", in_specs=[ANY], out_specs=ANY,
        out_shape=jax.ShapeDtypeStruct((N_CHIPS, r, c), shard.dtype),
        scratch_shapes=[pltpu.SemaphoreType.DMA((6,)), pltpu.SemaphoreType.DMA((6,)), pltpu.SemaphoreType.DMA],
    )(shard)


def _cast_bf16(a, *, name, tm=512):
    n, r, c = a.shape
    tm = _tile(r, tm) if r % 128 == 0 else r

    def body(a_ref, o_ref):
        o_ref[...] = a_ref[...].astype(BF16)

    blk = pl.BlockSpec((1, tm, c), lambda i, j: (i, j, 0))
    return pl.pallas_call(body, name=name, grid=(n, r // tm), in_specs=[blk], out_specs=blk,
                          out_shape=jax.ShapeDtypeStruct(a.shape, BF16), compiler_params=_cp(("parallel", "parallel")))(a)


def _pair_exchange(g16, hr):
    n, r, c = g16.shape

    def body(g_ref, out_ref, send_sem, recv_sem):
        x, y, cc = lax.axis_index("x"), lax.axis_index("y"), lax.axis_index("c")
        cp = pltpu.make_async_remote_copy(
            src_ref=g_ref.at[:, pl.ds((1 - cc) * hr, hr), :], dst_ref=out_ref, send_sem=send_sem, recv_sem=recv_sem,
            device_id=(x, y, 1 - cc), device_id_type=MESH)
        cp.start()
        cp.wait()

    return pl.pallas_call(
        body, name="grad_pair_exchange", in_specs=[ANY], out_specs=ANY,
        out_shape=jax.ShapeDtypeStruct((n, hr, c), g16.dtype),
        scratch_shapes=[pltpu.SemaphoreType.DMA, pltpu.SemaphoreType.DMA],
    )(g16)


def _pair_add(g, recv, half_idx, hr, *, tm=384):
    n, r, c = g.shape
    nt = hr // tm

    def body(hi_ref, g_ref, r_ref, o32_ref, o16_ref):
        v = g_ref[...] + r_ref[...].astype(F32)
        o32_ref[...] = v
        o16_ref[...] = v.astype(BF16)

    gs = pltpu.PrefetchScalarGridSpec(
        num_scalar_prefetch=1, grid=(n, nt),
        in_specs=[pl.BlockSpec((1, tm, c), lambda i, j, hi: (i, hi[0] * nt + j, 0)),
                  pl.BlockSpec((1, tm, c), lambda i, j, hi: (i, j, 0))],
        out_specs=[pl.BlockSpec((1, tm, c), lambda i, j, hi: (i, j, 0))] * 2)
    return pl.pallas_call(
        body, name="grad_pair_add", grid_spec=gs,
        out_shape=[jax.ShapeDtypeStruct((n, hr, c), F32), jax.ShapeDtypeStruct((n, hr, c), BF16)],
        compiler_params=_cp(("parallel", "parallel")),
    )(half_idx, g, recv)


def _chip_exchange(p16):
    n, hr, c = p16.shape

    def body(p_ref, out_ref, send_sems, recv_sems):
        x, y, cc = lax.axis_index("x"), lax.axis_index("y"), lax.axis_index("c")
        cps = []
        for j, k in enumerate((1, 2, 3)):
            px, py = _chip_of(k, x, y)
            cps.append(pltpu.make_async_remote_copy(
                src_ref=p_ref.at[2 * px + py], dst_ref=out_ref.at[j], send_sem=send_sems.at[j], recv_sem=recv_sems.at[j],
                device_id=(px, py, cc), device_id_type=MESH))
        for cp in cps:
            cp.start()
        for cp in cps:
            cp.wait()

    return pl.pallas_call(
        body, name="grad_chip_exchange", in_specs=[ANY], out_specs=ANY,
        out_shape=jax.ShapeDtypeStruct((3, hr, c), p16.dtype),
        scratch_shapes=[pltpu.SemaphoreType.DMA((3,)), pltpu.SemaphoreType.DMA((3,))],
    )(p16)


def _chip_add(p32, recv, chip_idx, *, tm=384):
    n, hr, c = p32.shape

    def body(ci_ref, p_ref, r_ref, o_ref):
        o_ref[...] = ((p_ref[0] + r_ref[0].astype(F32)) + r_ref[1].astype(F32)) + r_ref[2].astype(F32)

    gs = pltpu.PrefetchScalarGridSpec(
        num_scalar_prefetch=1, grid=(hr // tm,),
        in_specs=[pl.BlockSpec((1, tm, c), lambda j, ci: (ci[0], j, 0)), pl.BlockSpec((3, tm, c), lambda j, ci: (0, j, 0))],
        out_specs=pl.BlockSpec((tm, c), lambda j, ci: (j, 0)))
    return pl.pallas_call(
        body, name="grad_chip_add", grid_spec=gs, out_shape=jax.ShapeDtypeStruct((hr, c), F32),
        compiler_params=_cp(("parallel",)),
    )(chip_idx, p32, recv)


def _pair_gather(f):
    hr, c = f.shape

    def body(f_ref, out_ref, send_sem, recv_sem, local_sem):
        x, y, cc = lax.axis_index("x"), lax.axis_index("y"), lax.axis_index("c")
        mine = pltpu.make_async_copy(f_ref, out_ref.at[pl.ds(cc * hr, hr), :], local_sem)
        mine.start()
        cp = pltpu.make_async_remote_copy(
            src_ref=f_ref, dst_ref=out_ref.at[pl.ds(cc * hr, hr), :], send_sem=send_sem, recv_sem=recv_sem,
            device_id=(x, y, 1 - cc), device_id_type=MESH)
        cp.start()
        cp.wait()
        mine.wait()

    return pl.pallas_call(
        body, name="grad_pair_gather", in_specs=[ANY], out_specs=ANY,
        out_shape=jax.ShapeDtypeStruct((2 * hr, c), f.dtype),
        scratch_shapes=[pltpu.SemaphoreType.DMA, pltpu.SemaphoreType.DMA, pltpu.SemaphoreType.DMA],
    )(f)


def _all_reduce_small(buf):
    r, c = buf.shape

    def body(b_ref, out_ref, gat, send_sems, recv_sems):
        x, y, cc = lax.axis_index("x"), lax.axis_index("y"), lax.axis_index("c")
        me = 4 * x + 2 * y + cc
        gat[me] = b_ref[...]
        cps = []
        for k in range(1, 8):
            px, py, pc = x ^ (k >> 2), y ^ ((k >> 1) & 1), cc ^ (k & 1)
            cps.append(pltpu.make_async_remote_copy(
                src_ref=b_ref, dst_ref=gat.at[me], send_sem=send_sems.at[k - 1], recv_sem=recv_sems.at[k - 1],
                device_id=(px, py, pc), device_id_type=MESH))
        for cp in cps:
            cp.start()
        for cp in cps:
            cp.wait()
        acc = gat[0]
        for d in range(1, 8):
            acc = acc + gat[d]
        out_ref[...] = acc

    vm = pl.BlockSpec(memory_space=pltpu.VMEM)
    return pl.pallas_call(
        body, name="all_reduce_small", in_specs=[vm], out_specs=vm, out_shape=jax.ShapeDtypeStruct((r, c), F32),
        scratch_shapes=[pltpu.VMEM((8, r, c), F32), pltpu.SemaphoreType.DMA((7,)), pltpu.SemaphoreType.DMA((7,))],
        compiler_params=pltpu.CompilerParams(vmem_limit_bytes=VMEM_LIMIT),
    )(buf)


def _pipe(fn, ins, outs, tr, depth=4, slots=None):
    shape = ins[0].shape
    lead, (r, c) = shape[:-2], shape[-2:]
    assert len(lead) <= 1 and r % tr == 0
    nr = r // tr
    which = list(range(lead[0])) if lead and slots is None else slots
    n = nr * (len(which) if lead else 1)
    ni, no = len(ins), len(outs)

    def blk(ref, step):
        rows = pl.ds((step % nr) * tr, tr)
        return ref.at[which[step // nr], rows, :] if lead else ref.at[rows, :]

    def scoped(*bufs):
        ibufs, obufs, isem, osem = bufs[:ni], bufs[ni:ni + no], bufs[-2], bufs[-1]

        def in_copy(q, step, slot):
            return pltpu.make_async_copy(blk(ins[q], step), ibufs[q].at[slot], isem.at[q, slot])

        def out_copy(q, step, slot):
            return pltpu.make_async_copy(obufs[q].at[slot], blk(outs[q], step), osem.at[q, slot])

        for step in range(min(nbuf - 1, n)):
            for q in range(ni):
                in_copy(q, step, step % nbuf).start()
        for step in range(n):
            slot = step % nbuf
            if step + nbuf - 1 < n:
                for q in range(ni):
                    in_copy(q, step + nbuf - 1, (step + nbuf - 1) % nbuf).start()
            for q in range(ni):
                in_copy(q, step, slot).wait()
            if step >= nbuf:
                for q in range(no):
                    out_copy(q, step - nbuf, slot).wait()
            res = fn(*[ibufs[q][slot] for q in range(ni)])
            for q in range(no):
                obufs[q][slot] = res[q].astype(obufs[q].dtype)
                out_copy(q, step, slot).start()
        for step in range(max(n - nbuf, 0), n):
            for q in range(no):
                out_copy(q, step, step % nbuf).wait()

    assert n <= 8
    nbuf = min(n, depth)
    pl.run_scoped(scoped, *[pltpu.VMEM((nbuf, tr, c), q.dtype) for q in ins], *[pltpu.VMEM((nbuf, tr, c), q.dtype) for q in outs],
                  pltpu.SemaphoreType.DMA((ni, nbuf)), pltpu.SemaphoreType.DMA((no, nbuf)))


W_IN_PAD = 2304
BIG = ("w_in", "w_attn_o", "w_ssd_o", "w_out", "w_up", "w_down")
BIG_SHAPE = dict(w_in=(D_MODEL, W_IN_PAD), w_attn_o=(Q_DIM // 4, D_MODEL), w_ssd_o=(D_INNER // 4, D_MODEL),
                 w_out=(D_MODEL // 4, D_MODEL), w_up=(D_MODEL, 2 * D_FF // 4), w_down=(D_FF // 4, D_MODEL))
BIG_TR = dict(w_in=128, w_attn_o=128, w_ssd_o=128, w_out=128, w_up=128, w_down=176)
X_FIRST = dict(w_in=True, w_attn_o=True, w_ssd_o=False, w_out=True, w_up=False, w_down=False)


def _neighbours(x, y, x_first):
    xn, yn = (1 - x, y), (x, 1 - y)
    n1, n2 = (xn, yn) if x_first else (yn, xn)
    slot = lambda ch: 2 * ch[0] + ch[1]
    return n1, n2, slot(n1), slot(n2), slot((1 - x, 1 - y))


def _gather_big(shards):
    nt = len(BIG)

    def body(*refs):
        sh, out = refs[:nt], refs[nt:2 * nt]
        send_sems, recv_sems = refs[2 * nt:]
        x, y, cc = lax.axis_index("x"), lax.axis_index("y"), lax.axis_index("c")
        me = 2 * x + y
        sib = (x, y, 1 - cc)
        for t, n in enumerate(BIG):
            _pipe(lambda v: (v,), [sh[t]], [out[t].at[me]], BIG_TR[n])

        def copy(t, k, slot, pc, to):
            hr = BIG_SHAPE[BIG[t]][0] // 2
            ref = out[t].at[slot, pl.ds(pc * hr, hr), :]
            return pltpu.make_async_remote_copy(src_ref=ref, dst_ref=ref, send_sem=send_sems.at[6 * t + k],
                                                recv_sem=recv_sems.at[6 * t + k], device_id=to, device_id_type=MESH)

        started = []

        def start(cp):
            cp.start()
            started.append(cp)

        geo = [_neighbours(x, y, X_FIRST[n]) for n in BIG]
        for t in range(nt):
            n1, n2, _, _, _ = geo[t]
            start(copy(t, 0, me, cc, (*n1, cc)))
            start(copy(t, 1, me, cc, (*n2, cc)))
        for t in range(nt):
            n1, n2, s1, s2, sd = geo[t]
            copy(t, 0, s1, cc, sib).wait_recv()
            start(copy(t, 2, s1, cc, (*n2, cc)))
            start(copy(t, 3, s1, cc, sib))
            copy(t, 1, s2, cc, sib).wait_recv()
            start(copy(t, 4, s2, cc, sib))
        for t in range(nt):
            _, _, s1, s2, sd = geo[t]
            copy(t, 2, sd, cc, sib).wait_recv()
            start(copy(t, 5, sd, cc, sib))
        for t in range(nt):
            _, _, s1, s2, sd = geo[t]
            copy(t, 3, s1, 1 - cc, sib).wait_recv()
            copy(t, 4, s2, 1 - cc, sib).wait_recv()
            copy(t, 5, sd, 1 - cc, sib).wait_recv()
        for cp in started:
            cp.wait_send()

    return pl.pallas_call(
        body, name="gather_big", in_specs=[ANY] * nt, out_specs=[ANY] * nt,
        out_shape=[jax.ShapeDtypeStruct((N_CHIPS, *BIG_SHAPE[n]), BF16) for n in BIG],
        scratch_shapes=[pltpu.SemaphoreType.DMA((6 * nt,)), pltpu.SemaphoreType.DMA((6 * nt,))],
        compiler_params=pltpu.CompilerParams(vmem_limit_bytes=VMEM_LIMIT),
    )(*shards)


def _reduce_big(grads):
    nt = len(BIG)
    nw = 7

    def body(*refs):
        g = refs[:nt]
        fin = refs[nt:2 * nt]
        work = refs[2 * nt:2 * nt + nw * nt]
        send_sems, recv_sems = refs[2 * nt + nw * nt:]
        x, y, cc = lax.axis_index("x"), lax.axis_index("y"), lax.axis_index("c")
        me = 2 * x + y
        sib = (x, y, 1 - cc)
        started = []

        def rcopy(t, k, src, dst, to):
            cp = pltpu.make_async_remote_copy(src_ref=src, dst_ref=dst, send_sem=send_sems.at[5 * t + k],
                                              recv_sem=recv_sems.at[5 * t + k], device_id=to, device_id_type=MESH)
            return cp

        def start(cp):
            cp.start()
            started.append(cp)

        geo = [_neighbours(x, y, X_FIRST[n]) for n in BIG]
        hrs = [BIG_SHAPE[n][0] // 2 for n in BIG]
        wk = lambda t: work[nw * t:nw * (t + 1)]
        one = lambda ref, slot: ref.at[pl.ds(slot, 1)]
        for t in range(nt):
            recv_a = wk(t)[0]
            start(rcopy(t, 0, g[t].at[:, pl.ds((1 - cc) * hrs[t], hrs[t]), :], recv_a, sib))
        for t, n in enumerate(BIG):
            recv_a, p32, p16, r1, qme, qs2, r2 = wk(t)
            n1, n2, s1, s2, sd = geo[t]
            rcopy(t, 0, recv_a, recv_a, sib).wait_recv()
            _pipe(lambda a, b: (a + b, a + b), [g[t].at[:, pl.ds(cc * hrs[t], hrs[t]), :], recv_a], [p32, p16], BIG_TR[n])
            start(rcopy(t, 1, one(p16, s1), one(r1, 0), (*n1, cc)))
            start(rcopy(t, 2, one(p16, sd), one(r1, 1), (*n1, cc)))
        for t, n in enumerate(BIG):
            recv_a, p32, p16, r1, qme, qs2, r2 = wk(t)
            n1, n2, s1, s2, sd = geo[t]
            rcopy(t, 1, one(r1, 0), one(r1, 0), sib).wait_recv()
            rcopy(t, 2, one(r1, 1), one(r1, 1), sib).wait_recv()
            _pipe(lambda a, b: (a + b.astype(F32),), [one(p32, s2), one(r1, 1)], [qs2], BIG_TR[n])
            start(rcopy(t, 3, qs2, r2, (*n2, cc)))
            _pipe(lambda a, b: (a + b.astype(F32),), [one(p32, me), one(r1, 0)], [qme], BIG_TR[n])
        for t, n in enumerate(BIG):
            recv_a, p32, p16, r1, qme, qs2, r2 = wk(t)
            rcopy(t, 3, r2, r2, sib).wait_recv()
            mine = fin[t].at[pl.ds(cc * hrs[t], hrs[t]), :]
            _pipe(lambda a, b: (a + b.astype(F32),), [qme.at[0], r2.at[0]], [mine], BIG_TR[n])
            start(rcopy(t, 4, mine, mine, sib))
        for t in range(nt):
            other = fin[t].at[pl.ds((1 - cc) * hrs[t], hrs[t]), :]
            rcopy(t, 4, other, other, sib).wait_recv()
        for cp in started:
            cp.wait_send()

    outs = [jax.ShapeDtypeStruct(BIG_SHAPE[n], F32) for n in BIG]
    for n in BIG:
        r, c = BIG_SHAPE[n]
        hr = r // 2
        outs += [jax.ShapeDtypeStruct((4, hr, c), F32), jax.ShapeDtypeStruct((4, hr, c), F32),
                 jax.ShapeDtypeStruct((4, hr, c), BF16), jax.ShapeDtypeStruct((2, hr, c), BF16),
                 jax.ShapeDtypeStruct((1, hr, c), F32), jax.ShapeDtypeStruct((1, hr, c), BF16),
                 jax.ShapeDtypeStruct((1, hr, c), BF16)]
    res = pl.pallas_call(
        body, name="reduce_big", in_specs=[ANY] * nt, out_specs=[ANY] * len(outs), out_shape=outs,
        scratch_shapes=[pltpu.SemaphoreType.DMA((5 * nt,)), pltpu.SemaphoreType.DMA((5 * nt,))],
        compiler_params=pltpu.CompilerParams(vmem_limit_bytes=VMEM_LIMIT),
    )(*grads)
    return res[:nt]


WHOLE_X_FIRST = dict(w_ssd_o=True, w_out=False, w_attn_o=False)


def _quarters(names):
    out = []
    for i, n in enumerate(names):
        if n in WHOLE_X_FIRST:
            h = BIG_SHAPE[n][0] // 2
            out.append((i, WHOLE_X_FIRST[n], 0, h, 128))
        else:
            q = BIG_SHAPE[n][0] // 4
            tr = 128 if q % 128 == 0 else q
            out += [(i, True, 0, q, tr), (i, False, q, q, tr)]
    return out


class _GatherJob:
    def __init__(self, names, shards, at=None):
        self.names = names
        self.at = at
        self.inputs = list(shards)
        self.out_shapes = [jax.ShapeDtypeStruct((N_CHIPS, *BIG_SHAPE[n]), BF16) for n in names]
        self.ent = _quarters(names)
        self.scratch = [pltpu.SemaphoreType.DMA((6 * len(self.ent),)), pltpu.SemaphoreType.DMA((6 * len(self.ent),))]

    def phases(self, sh, out, scr):
        send_sems, recv_sems = scr
        names, ent = self.names, self.ent
        x, y, cc = lax.axis_index("x"), lax.axis_index("y"), lax.axis_index("c")
        me = 2 * x + y
        sib = (x, y, 1 - cc)
        geo = [_neighbours(x, y, e[1]) for e in ent]
        started = []

        def copy(i, k, slot, pc, to):
            arr, _, roff, rows, _ = ent[i]
            hr = BIG_SHAPE[names[arr]][0] // 2
            ref = out[arr].at[slot, pl.ds(pc * hr + roff, rows), :]
            return pltpu.make_async_remote_copy(src_ref=ref, dst_ref=ref, send_sem=send_sems.at[6 * i + k],
                                                recv_sem=recv_sems.at[6 * i + k], device_id=to, device_id_type=MESH)

        def start(*a):
            copy(*a).start()
            started.append(a)

        def p0():
            for t, n in enumerate(names):
                _pipe(lambda v: (v,), [sh[t]], [out[t].at[me]], BIG_TR[n])
            for i in range(len(ent)):
                n1, n2, _, _, _ = geo[i]
                start(i, 0, me, cc, (*n1, cc))
                start(i, 1, me, cc, (*n2, cc))

        def p1():
            for i in range(len(ent)):
                n1, n2, s1, s2, sd = geo[i]
                copy(i, 0, s1, cc, sib).wait_recv()
                start(i, 2, s1, cc, (*n2, cc))
                start(i, 3, s1, cc, sib)
                copy(i, 1, s2, cc, sib).wait_recv()
                start(i, 4, s2, cc, sib)

        def p2():
            for i in range(len(ent)):
                sd = geo[i][4]
                copy(i, 2, sd, cc, sib).wait_recv()
                start(i, 5, sd, cc, sib)

        def p3():
            for i in range(len(ent)):
                _, _, s1, s2, sd = geo[i]
                copy(i, 3, s1, 1 - cc, sib).wait_recv()
                copy(i, 4, s2, 1 - cc, sib).wait_recv()
                copy(i, 5, sd, 1 - cc, sib).wait_recv()
            for a in started:
                copy(*a).wait_send()

        return [p0, p1, p2, p3]


class _ReduceJob:
    NW = 7

    def __init__(self, names, grads, at=None):
        self.names = names
        self.at = at
        self.inputs = list(grads)
        self.ent = _quarters(names)
        self.out_shapes = [jax.ShapeDtypeStruct(BIG_SHAPE[n], F32) for n in names]
        for arr, _, _, rows, _ in self.ent:
            c = BIG_SHAPE[names[arr]][1]
            self.out_shapes += [jax.ShapeDtypeStruct((4, rows, c), F32), jax.ShapeDtypeStruct((4, rows, c), F32),
                                jax.ShapeDtypeStruct((4, rows, c), BF16), jax.ShapeDtypeStruct((2, rows, c), BF16),
                                jax.ShapeDtypeStruct((1, rows, c), F32), jax.ShapeDtypeStruct((1, rows, c), BF16),
                                jax.ShapeDtypeStruct((1, rows, c), BF16)]
        self.scratch = [pltpu.SemaphoreType.DMA((8 * len(self.ent),)), pltpu.SemaphoreType.DMA((8 * len(self.ent),))]

    def phases(self, g, outs, scr):
        send_sems, recv_sems = scr
        names, ent, nw = self.names, self.ent, self.NW
        nt = len(names)
        fin, work = outs[:nt], outs[nt:]
        x, y, cc = lax.axis_index("x"), lax.axis_index("y"), lax.axis_index("c")
        me = 2 * x + y
        sib = (x, y, 1 - cc)
        geo = [_neighbours(x, y, e[1]) for e in ent]
        started = []
        wk = lambda i: work[nw * i:nw * (i + 1)]
        one = lambda ref, slot: ref.at[pl.ds(slot, 1)]

        def rows_of(i, pc):
            arr, _, roff, rows, _ = ent[i]
            return pl.ds(pc * (BIG_SHAPE[names[arr]][0] // 2) + roff, rows)

        def rcopy(i, k, src, dst, to):
            return pltpu.make_async_remote_copy(src_ref=src, dst_ref=dst, send_sem=send_sems.at[8 * i + k],
                                                recv_sem=recv_sems.at[8 * i + k], device_id=to, device_id_type=MESH)

        def start(make):
            make().start()
            started.append(make)

        def pair(i, q, slot, pc):
            return rcopy(i, q, g[ent[i][0]].at[pl.ds(slot, 1), rows_of(i, pc), :], one(wk(i)[0], slot), sib)

        def p0():
            for i in range(len(ent)):
                _, _, s1, s2, sd = geo[i]
                for q, slot in enumerate((s1, sd, s2, me)):
                    start(lambda i=i, q=q, slot=slot: pair(i, q, slot, 1 - cc))

        def p1():
            for i, e in enumerate(ent):
                recv_a, _, p16, _ = wk(i)[:4]
                n1, n2, s1, s2, sd = geo[i]
                pair(i, 0, s1, cc).wait_recv()
                pair(i, 1, sd, cc).wait_recv()
                _pipe(lambda a, b: (a + b,), [g[e[0]].at[:, rows_of(i, cc), :], recv_a], [p16], e[4], slots=(s1, sd))
                start(lambda i=i, s1=s1, n1=n1: rcopy(i, 4, one(wk(i)[2], s1), one(wk(i)[3], 0), (*n1, cc)))
                start(lambda i=i, sd=sd, n1=n1: rcopy(i, 5, one(wk(i)[2], sd), one(wk(i)[3], 1), (*n1, cc)))
            for i, e in enumerate(ent):
                recv_a, p32 = wk(i)[:2]
                _, _, s1, s2, sd = geo[i]
                pair(i, 2, s2, cc).wait_recv()
                pair(i, 3, me, cc).wait_recv()
                _pipe(lambda a, b: (a + b,), [g[e[0]].at[:, rows_of(i, cc), :], recv_a], [p32], e[4], slots=(s2, me))

        def p2():
            for i, e in enumerate(ent):
                _, p32, _, r1, qme, qs2, r2 = wk(i)
                n1, n2, s1, s2, sd = geo[i]
                rcopy(i, 4, one(r1, 0), one(r1, 0), sib).wait_recv()
                rcopy(i, 5, one(r1, 1), one(r1, 1), sib).wait_recv()
                _pipe(lambda a, b, c, d: (a + b.astype(F32), c + d.astype(F32)),
                      [one(p32, s2), one(r1, 1), one(p32, me), one(r1, 0)], [qs2, qme], e[4])
                start(lambda i=i, n2=n2: rcopy(i, 6, wk(i)[5], wk(i)[6], (*n2, cc)))

        def p3():
            for i, e in enumerate(ent):
                qme, r2 = wk(i)[4], wk(i)[6]
                rcopy(i, 6, r2, r2, sib).wait_recv()
                mine = fin[e[0]].at[rows_of(i, cc), :]
                _pipe(lambda a, b: (a + b.astype(F32),), [qme.at[0], r2.at[0]], [mine], e[4])
                start(lambda i=i, e=e: rcopy(i, 7, fin[e[0]].at[rows_of(i, cc), :], fin[e[0]].at[rows_of(i, cc), :], sib))

        def p4():
            for i, e in enumerate(ent):
                other = fin[e[0]].at[rows_of(i, 1 - cc), :]
                rcopy(i, 7, other, other, sib).wait_recv()
            for make in started:
                make().wait_send()

        return [p0, p1, p2, p3, p4]


class _AdamJob:
    def __init__(self, names, ws, gs, ms, vs, groups):
        self.names, self.groups = names, groups
        self.inputs = [a for quad in zip(ws, gs, ms, vs) for a in quad]
        self.out_shapes = [jax.ShapeDtypeStruct(w.shape, F32) for w in ws for _ in range(4)]

    def work(self, ins, outs):
        def one(t):
            w, g, m, v = ins[4 * t:4 * t + 4]
            r = w.shape[1]
            tr = 128 if r % 128 == 0 else r // 4
            _pipe(lambda a, b, c, d: (*_adamw_math(a, b, c, d), b), [w.at[0], g, m.at[0], v.at[0]],
                  [o.at[0] for o in outs[4 * t:4 * t + 4]], tr, depth=2)

        def group(grp):
            def run():
                for n in grp:
                    one(self.names.index(n))
            return run

        return [group(grp) for grp in self.groups]


class _Interleaved:
    def __init__(self, job, work, at):
        self.job, self.wk, self.at = job, work, at
        self.inputs = job.inputs + work.inputs
        self.out_shapes = list(job.out_shapes) + list(work.out_shapes)
        self.scratch = job.scratch

    def phases(self, ins, outs, scr):
        nj, no = len(self.job.inputs), len(self.job.out_shapes)
        base = self.job.phases(ins[:nj], outs[:no], scr)
        work = self.wk.work(ins[nj:], outs[no:])
        mixed = []
        for k, ph in enumerate(base):
            mixed.append(ph)
            if k < len(work):
                mixed.append(work[k])
        return mixed


def _run_job(job, name):
    ni, no = len(job.inputs), len(job.out_shapes)

    def body(*refs):
        for ph in job.phases(refs[:ni], refs[ni:ni + no], refs[ni + no:]):
            ph()

    return pl.pallas_call(
        body, name=name, in_specs=[ANY] * ni, out_specs=[ANY] * no, out_shape=job.out_shapes, scratch_shapes=job.scratch,
        compiler_params=pltpu.CompilerParams(vmem_limit_bytes=VMEM_LIMIT),
    )(*job.inputs)


def _hosted(body, *, name, grid, in_specs, out_specs, out_shape, scratch_shapes, args, sem, side=None):
    if side is None:
        return pl.pallas_call(body, name=name, grid=grid, in_specs=in_specs, out_specs=out_specs, out_shape=out_shape,
                              scratch_shapes=scratch_shapes, compiler_params=_cp(sem))(*args), None
    job = side
    ni, no, ns = len(in_specs), len(out_specs), len(scratch_shapes)
    ji, jo = len(job.inputs), len(job.out_shapes)
    n_steps = 1
    for extent in grid:
        n_steps *= extent

    def wrapped(*refs):
        own_in, refs = refs[:ni], refs[ni:]
        job_in, refs = refs[:ji], refs[ji:]
        own_out, refs = refs[:no], refs[no:]
        job_out, refs = refs[:jo], refs[jo:]
        own_scr, job_scr = refs[:ns], refs[ns:]
        step = 0
        for d, extent in enumerate(grid):
            step = step * extent + pl.program_id(d)
        phases = job.phases(job_in, job_out, job_scr)
        steps = [min(int(f * n_steps), n_steps - 1) for f in job.at] + [n_steps - 1]
        assert len(steps) == len(phases) and steps == sorted(steps)
        for at, ph in zip(steps, phases):
            pl.when(step == at)(ph)
        body(*own_in, *own_out, *own_scr)

    res = pl.pallas_call(
        wrapped, name=name, grid=grid, in_specs=list(in_specs) + [ANY] * ji, out_specs=list(out_specs) + [ANY] * jo,
        out_shape=list(out_shape) + list(job.out_shapes), scratch_shapes=list(scratch_shapes) + list(job.scratch),
        compiler_params=_cp(("arbitrary",) * len(grid)),
    )(*args, *job.inputs)
    return res[:no], res[no:]


def _proj_dw(xnt, dproj_sh, *, tm=512, tk=2048):
    d, s = xnt.shape
    tk = _tile(s, tk)
    nk = s // tk

    def body(a_ref, b_ref, o_ref, acc):
        def finish(r):
            o_ref[0] = r

        _accumulate(acc, _dot(a_ref[...], b_ref[...]), pl.program_id(2), nk, finish)

    return pl.pallas_call(
        body, name="proj_dw", grid=(N_CHIPS, d // tm, nk),
        in_specs=[pl.BlockSpec((tm, tk), lambda j, i, q: (i, q)), pl.BlockSpec((tk, W_IN_PAD), lambda j, i, q: (q, j))],
        out_specs=pl.BlockSpec((1, tm, W_IN_PAD), lambda j, i, q: (j, i, 0)),
        out_shape=jax.ShapeDtypeStruct((N_CHIPS, d, W_IN_PAD), F32), scratch_shapes=[pltpu.VMEM((tm, W_IN_PAD), F32)],
        compiler_params=_cp(("parallel", "parallel", "arbitrary")),
    )(xnt, dproj_sh)


def _proj_dx(dproj_sh, w_sh, *, tm=1024, side=None):
    s = dproj_sh.shape[0]
    d = w_sh.shape[1]
    tm = _tile(s, tm)

    def body(a_ref, b_ref, o_ref, acc):
        kk = pl.program_id(1)
        part = _dot_nt(a_ref[...], b_ref[0])

        @pl.when(kk == 0)
        def _():
            acc[...] = part

        @pl.when(kk > 0)
        def _():
            acc[...] += part

        @pl.when(kk == N_CHIPS - 1)
        def _():
            o_ref[...] = acc[...]

    own, extra = _hosted(
        body, name="proj_dx", grid=(s // tm, N_CHIPS),
        in_specs=[pl.BlockSpec((tm, W_IN_PAD), lambda i, q: (i, q)), pl.BlockSpec((1, d, W_IN_PAD), lambda i, q: (q, 0, 0))],
        out_specs=[pl.BlockSpec((tm, d), lambda i, q: (i, 0))],
        out_shape=[jax.ShapeDtypeStruct((s, d), F32)], scratch_shapes=[pltpu.VMEM((tm, d), F32)],
        args=(dproj_sh, w_sh), sem=("parallel", "arbitrary"), side=side)
    return own[0] if side is None else (own[0], extra)


def _up_dx(dup, w_sh, *, tm=1024):
    s = dup.shape[1]
    d, wsh = w_sh.shape[1:]
    tm = _tile(s, tm)

    def body(a_ref, b_ref, o_ref, acc):
        kk = pl.program_id(1)
        part = _dot_nt(a_ref[0], b_ref[0])

        @pl.when(kk == 0)
        def _():
            acc[...] = part

        @pl.when(kk > 0)
        def _():
            acc[...] += part

        @pl.when(kk == N_CHIPS - 1)
        def _():
            o_ref[...] = acc[...]

    return pl.pallas_call(
        body, name="up_dx", grid=(s // tm, N_CHIPS),
        in_specs=[pl.BlockSpec((1, tm, wsh), lambda i, q: (q >> 1, i, q & 1)), pl.BlockSpec((1, d, wsh), lambda i, q: (q, 0, 0))],
        out_specs=pl.BlockSpec((tm, d), lambda i, q: (i, 0)),
        out_shape=jax.ShapeDtypeStruct((s, d), F32), scratch_shapes=[pltpu.VMEM((tm, d), F32)],
        compiler_params=_cp(("parallel", "arbitrary")),
    )(dup, w_sh)


def _up_dw(hnt, dup, *, tk=2048):
    d, s = hnt.shape
    wsh = 2 * D_FF // N_CHIPS
    tk = _tile(s, tk)
    nk = s // tk

    def body(a_ref, b_ref, o_ref, acc):
        def finish(r):
            o_ref[0] = r

        _accumulate(acc, _dot(a_ref[...], b_ref[0]), pl.program_id(1), nk, finish)

    return pl.pallas_call(
        body, name="up_dw", grid=(N_CHIPS, nk),
        in_specs=[pl.BlockSpec((d, tk), lambda j, q: (0, q)), pl.BlockSpec((1, tk, wsh), lambda j, q: (j >> 1, q, j & 1))],
        out_specs=pl.BlockSpec((1, d, wsh), lambda j, q: (j, 0, 0)),
        out_shape=jax.ShapeDtypeStruct((N_CHIPS, d, wsh), F32), scratch_shapes=[pltpu.VMEM((d, wsh), F32)],
        compiler_params=_cp(("parallel", "arbitrary")),
    )(hnt, dup)


BIG_ROWS =(IN_DIM // 4, Q_DIM // 4, D_INNER // 4, D_MODEL // 4, 2 * D_FF // 4, D_FF // 4)
PACK_ROWS = 5376


def _pack_shards(parts):
    rows = [p.reshape(-1, D_MODEL) for p in parts]
    pad = PACK_ROWS - sum(BIG_ROWS)
    return jnp.concatenate(rows + [jnp.zeros((pad, D_MODEL), rows[0].dtype)], axis=0)


def _unpack_shards(buf):
    out, off = [], 0
    for n in BIG_ROWS:
        out.append(buf[off:off + n])
        off += n
    return out


def _assemble(srcs, col_map, *, name, tr=256):
    arrays, lead = [], []
    for src in srcs:
        arr, j = src if isinstance(src, tuple) else (src, None)
        if not any(arr is a for a in arrays):
            arrays.append(arr)
        lead.append(([i for i, a in enumerate(arrays) if a is arr][0], j))
    rows = arrays[0].shape[-2]
    tr = _tile(rows, tr)
    out_w = len(col_map)
    tiles = []
    for t in range(out_w // 128):
        runs = []
        for lane in range(128):
            ent = col_map[t * 128 + lane]
            key = None if ent is None else (ent[0], ent[1] // 128, (lane - ent[1]) % 128)
            if runs and runs[-1][0] == key:
                runs[-1][2] = lane + 1
            else:
                runs.append([key, lane, lane + 1])
        tiles.append(runs)

    def body(*refs):
        o_ref = refs[-1]
        lane = lax.broadcasted_iota(jnp.int32, (tr, 128), 1)
        for t, runs in enumerate(tiles):
            acc = jnp.zeros((tr, 128), F32)
            for key, a, b in runs:
                if key is None:
                    continue
                sid, ct, shift = key
                ai, j = lead[sid]
                cols = slice(ct * 128, (ct + 1) * 128)
                piece = (refs[ai][:, cols] if j is None else refs[ai][j, :, cols]).astype(F32)
                if shift:
                    piece = pltpu.roll(piece, shift, 1)
                acc = piece if (a, b) == (0, 128) else jnp.where((lane >= a) & (lane < b), piece, acc)
            o_ref[:, t * 128:(t + 1) * 128] = acc.astype(BF16)

    specs = [pl.BlockSpec((tr, a.shape[1]), lambda i: (i, 0)) if a.ndim == 2
             else pl.BlockSpec((a.shape[0], tr, a.shape[2]), lambda i: (0, i, 0)) for a in arrays]
    return pl.pallas_call(
        body, name=name, grid=(rows // tr,), in_specs=specs, out_specs=pl.BlockSpec((tr, out_w), lambda i: (i, 0)),
        out_shape=jax.ShapeDtypeStruct((rows, out_w), BF16), compiler_params=_cp(("parallel",)),
    )(*arrays)


def _permute_cols_in(w):
    pad = jnp.zeros((w.shape[0], PW - IN_DIM), w.dtype)
    return jnp.concatenate([w[:, :6656], w[:, 6688:], w[:, 6656:6688], pad], axis=1)


def _unpermute_cols_in(g):
    return jnp.concatenate([g[:, :6656], g[:, O_DT:O_DT + 32], g[:, 6656:O_DT]], axis=1)


SMALL = ("norm1_w", "b_gate", "attn_sinks", "ssd_conv_b", "dt_bias", "a_log", "d_skip", "ssd_norm_w", "norm2_w",
         "ffn_conv_b", "final_norm_w", "ssd_conv_w", "ffn_conv_w")


def _pad128(v):
    v = v.reshape(-1)
    return jnp.pad(v, (0, (-v.shape[0]) % 128))


def _pack_small(parts):
    flat = jnp.concatenate([_pad128(p) for p in parts])
    flat = jnp.pad(flat, (0, (-flat.shape[0]) % 1024))
    return flat.reshape(-1, 128)


def _unpack_small(buf, shapes):
    flat, out, off = buf.reshape(-1), [], 0
    for shp in shapes:
        n = 1
        for q in shp:
            n *= q
        out.append(flat[off:off + n].reshape(shp))
        off += n + (-n) % 128
    return out


def _vec128(v):
    return jnp.pad(v.reshape(1, -1), ((0, 0), (0, 128 - v.shape[-1])))


def kernel(x, norm1_w, w_in, b_gate, attn_sinks, w_attn_o, ssd_conv_w, ssd_conv_b, dt_bias, a_log, d_skip, ssd_norm_w, w_ssd_o, w_out, norm2_w, w_up, ffn_conv_w, ffn_conv_b, w_down, final_norm_w, loss_target, m_norm1_w, m_w_in, m_b_gate, m_attn_sinks, m_w_attn_o, m_ssd_conv_w, m_ssd_conv_b, m_dt_bias, m_a_log, m_d_skip, m_ssd_norm_w, m_w_ssd_o, m_w_out, m_norm2_w, m_w_up, m_ffn_conv_w, m_ffn_conv_b, m_w_down, m_final_norm_w, v_norm1_w, v_w_in, v_b_gate, v_attn_sinks, v_w_attn_o, v_ssd_conv_w, v_ssd_conv_b, v_dt_bias, v_a_log, v_d_skip, v_ssd_norm_w, v_w_ssd_o, v_w_out, v_norm2_w, v_w_up, v_ffn_conv_w, v_ffn_conv_b, v_w_down, v_final_norm_w):
    ix, iy, ic = lax.axis_index("x"), lax.axis_index("y"), lax.axis_index("c")
    chip = 2 * ix + iy
    x2 = x[0]
    tgt = loss_target[0]
    s = x2.shape[0]

    wsh = IN_DIM // N_CHIPS
    big_shards = dict(w_in=jnp.pad(w_in[0], ((0, 0), (0, W_IN_PAD - wsh))), w_attn_o=w_attn_o[0], w_ssd_o=w_ssd_o[0],
                      w_out=w_out[0], w_up=w_up[0], w_down=w_down[0])
    gathered = {}
    (gathered["w_in"],) = _run_job(_GatherJob(("w_in",), [big_shards["w_in"]]), "gather_w_in")
    early = ("w_attn_o", "w_ssd_o", "w_out")
    gather_early = _GatherJob(early, [big_shards[n] for n in early], at=(0.0, 0.5, 0.8))
    gather_up = _GatherJob(("w_up",), [big_shards["w_up"]], at=(0.0, 0.55, 0.85))
    gather_down = _GatherJob(("w_down",), [big_shards["w_down"]], at=(0.0, 0.5, 0.8))
    gw = gathered["w_in"]
    perm = list(range(O_GA)) + list(range(O_GA + N_SSD_HEADS, IN_DIM)) + list(range(O_GA, O_GA + N_SSD_HEADS))
    w_in_p = _assemble([(gw, j) for j in range(N_CHIPS)], [divmod(o, wsh) for o in perm] + [None] * (PW - IN_DIM),
                       name="w_in_assemble")
    small_sh = _pack_small([ssd_conv_w[0], ffn_conv_w[0]])
    small_all = _all_gather_small(small_sh)
    sc_parts = [_unpack_small(small_all[j], [(4, XBC_DIM // 4), (3, 2 * D_FF // 4)]) for j in range(N_CHIPS)]
    ssd_cw = jnp.concatenate([p[0] for p in sc_parts], axis=1)
    ffn_cw = jnp.concatenate([p[1] for p in sc_parts], axis=1)

    sinks128 = _vec128(attn_sinks)
    dtb128, alog128, dskip128 = _vec128(dt_bias), _vec128(a_log), _vec128(d_skip)

    xn, xnt = _rms_fwd(x2, norm1_w, name="norm1_fwd", with_t=True)
    proj, got = _mm(xn, w_in_p, name="proj_fwd", tn=1280, side=gather_early)
    gathered.update(zip(early, got))
    qkvt = _mm(w_in_p[:, :O_Z], xnt, name="qkv_fwd", ta=True)
    attn_pre, (gathered["w_up"],) = _attn_fwd(qkvt, sinks128, side=gather_up)
    xbc = _ssd_conv_fwd(proj, ssd_cw, ssd_conv_b)
    (y_ssd, hprev), (gathered["w_down"],) = _ssd_fwd(xbc, proj, dtb128, alog128, dskip128, side=gather_down)
    full = {n: gathered[n].reshape(-1, D_MODEL) for n in ("w_attn_o", "w_ssd_o", "w_out", "w_down")}
    full["w_up"] = gathered["w_up"]
    attn = _mm(attn_pre, full["w_attn_o"], name="attn_o_fwd", ta=True)
    yn = _gate_norm_fwd(y_ssd, proj, ssd_norm_w)
    ssd_out = _mm(yn, full["w_ssd_o"], name="ssd_o_fwd")
    merged = _merge_fwd(proj, b_gate, attn, ssd_out)
    h1 = _mm(merged, full["w_out"], name="out_fwd", resid=x2)
    hn, hnt = _rms_fwd(h1, norm2_w, name="norm2_fwd", with_t=True)
    up = _mm(hn, full["w_up"], name="up_fwd")
    act = _ffn_act_fwd(up, ffn_cw, ffn_conv_b)
    h2 = _mm(act, full["w_down"], name="down_fwd", resid=h1, tk=1408)

    dh2, loss_blk, g_final = _loss_bwd(h2, tgt, final_norm_w.reshape(1, -1))
    dact = _mm(dh2, full["w_down"], name="down_dx", tb=True, tn=1408)
    g_down = _mm(act, dh2, name="down_dw", ta=True, tm=1408)
    dup, g_ffn_cw, g_ffn_cb = _ffn_act_bwd(dact, up, ffn_cw, ffn_conv_b)
    dhn = _up_dx(dup, full["w_up"])
    g_up = _up_dw(hnt, dup)
    dh1, g_norm2 = _rms_bwd(dhn, h1, norm2_w, dh2, name="norm2_bwd")
    dmerged = _mm(dh1, full["w_out"], name="out_dx", tb=True)
    g_out = _mm(merged, dh1, name="out_dw", ta=True)
    dattn, dssd_out, dga, dgs, g_ba, g_bs = _merge_bwd(dmerged, proj, b_gate, attn, ssd_out)
    dyn = _mm(dssd_out, full["w_ssd_o"], name="ssd_o_dx", tb=True)
    g_ssd_o = _mm(yn, dssd_out, name="ssd_o_dw", ta=True)
    dy_ssd, dz, g_ssd_norm = _gate_norm_bwd(dyn, y_ssd, proj, ssd_norm_w)
    slot = lambda g: g.reshape(N_CHIPS, -1, D_MODEL)
    big_grads = {}
    red = ("w_down", "w_up")
    (dxbc, ddt, dvec), got = _ssd_bwd(xbc, proj, dtb128, alog128, dskip128, hprev, dy_ssd,
                                      side=_ReduceJob(red, [slot(g_down), g_up], at=(0.0, 0.3, 0.8, 0.95)))
    big_grads.update(zip(red, got))
    dxbc_raw, g_ssd_cw, g_ssd_cb = _ssd_conv_bwd(dxbc, proj, ssd_cw, ssd_conv_b)
    dattn_pre = _mm(full["w_attn_o"], dattn, name="attn_o_dx", tb=True)
    g_attn_o = _mm(attn_pre, dattn, name="attn_o_dw")
    red = ("w_out", "w_ssd_o", "w_attn_o")
    (dq, dk, dv, dsk), got = _attn_bwd(qkvt, sinks128, attn_pre, dattn_pre,
                                       side=_ReduceJob(red, [slot(g_out), slot(g_ssd_o), slot(g_attn_o)],
                                                       at=(0.0, 0.2, 0.5, 0.7)))
    big_grads.update(zip(red, got))
    pieces = [(dq.T, Q_DIM), (dk.T, KV_DIM), (dv.T, KV_DIM), (dz, D_INNER), (dxbc_raw, XBC_DIM), (ddt, N_SSD_HEADS),
              (dga, D_MODEL), (dgs, D_MODEL)]
    orig = [(i, c) for i, (_, w) in enumerate(pieces) for c in range(w)]
    dproj_sh = _assemble([p for p, _ in pieces],
                         [orig[j * wsh + c] if c < wsh else None for j in range(N_CHIPS) for c in range(W_IN_PAD)],
                         name="dproj_assemble")
    g_in = _proj_dw(xnt, dproj_sh)
    dxn, got = _proj_dx(dproj_sh, gathered["w_in"], side=_ReduceJob(("w_in",), [g_in], at=(0.0, 0.3, 0.8, 0.95)))
    big_grads["w_in"] = got[0]
    dx, g_norm1 = _rms_bwd(dxn, x2, norm1_w, dh1, name="norm1_bwd")


    small_g = dict(
        norm1_w=g_norm1, b_gate=jnp.concatenate([g_ba, g_bs], axis=1), attn_sinks=dsk[0:1, :16], ssd_conv_b=g_ssd_cb,
        dt_bias=dvec[0:1, :32], a_log=dvec[1:2, :32], d_skip=dvec[2:3, :32], ssd_norm_w=g_ssd_norm, norm2_w=g_norm2,
        ffn_conv_b=jnp.concatenate([g_ffn_cb[0], g_ffn_cb[1]], axis=1), final_norm_w=g_final, ssd_conv_w=g_ssd_cw,
        ffn_conv_w=jnp.concatenate([g_ffn_cw[0], g_ffn_cw[1]], axis=1))
    small_buf = _pack_small([small_g[n] for n in SMALL] + [loss_blk])
    small_sum = _all_reduce_small(small_buf)
    small_shapes = [(1, D_MODEL), (1, 2 * D_MODEL), (1, 16), (1, XBC_DIM), (1, 32), (1, 32), (1, 32), (1, D_INNER),
                    (1, D_MODEL), (1, 2 * D_FF), (D_MODEL,), (4, XBC_DIM), (3, 2 * D_FF), (1, 128)]
    small_list = _unpack_small(small_sum, small_shapes)
    loss = small_list[-1][0, 0]
    grads = dict(zip(SMALL, small_list[:-1]))
    grads["ssd_conv_w"] = lax.dynamic_slice_in_dim(grads["ssd_conv_w"], chip * (XBC_DIM // 4), XBC_DIM // 4, axis=1)
    grads["ffn_conv_w"] = lax.dynamic_slice_in_dim(grads["ffn_conv_w"], chip * (2 * D_FF // 4), 2 * D_FF // 4, axis=1)
    grads.update(big_grads)

    weights = dict(norm1_w=norm1_w, w_in=w_in, b_gate=b_gate, attn_sinks=attn_sinks, w_attn_o=w_attn_o, ssd_conv_w=ssd_conv_w,
                   ssd_conv_b=ssd_conv_b, dt_bias=dt_bias, a_log=a_log, d_skip=d_skip, ssd_norm_w=ssd_norm_w, w_ssd_o=w_ssd_o,
                   w_out=w_out, norm2_w=norm2_w, w_up=w_up, ffn_conv_w=ffn_conv_w, ffn_conv_b=ffn_conv_b, w_down=w_down,
                   final_norm_w=final_norm_w)
    ms = dict(norm1_w=m_norm1_w, w_in=m_w_in, b_gate=m_b_gate, attn_sinks=m_attn_sinks, w_attn_o=m_w_attn_o,
              ssd_conv_w=m_ssd_conv_w, ssd_conv_b=m_ssd_conv_b, dt_bias=m_dt_bias, a_log=m_a_log, d_skip=m_d_skip,
              ssd_norm_w=m_ssd_norm_w, w_ssd_o=m_w_ssd_o, w_out=m_w_out, norm2_w=m_norm2_w, w_up=m_w_up,
              ffn_conv_w=m_ffn_conv_w, ffn_conv_b=m_ffn_conv_b, w_down=m_w_down, final_norm_w=m_final_norm_w)
    vs = dict(norm1_w=v_norm1_w, w_in=v_w_in, b_gate=v_b_gate, attn_sinks=v_attn_sinks, w_attn_o=v_w_attn_o,
              ssd_conv_w=v_ssd_conv_w, ssd_conv_b=v_ssd_conv_b, dt_bias=v_dt_bias, a_log=v_a_log, d_skip=v_d_skip,
              ssd_norm_w=v_ssd_norm_w, w_ssd_o=v_w_ssd_o, w_out=v_w_out, norm2_w=v_norm2_w, w_up=v_w_up,
              ffn_conv_w=v_ffn_conv_w, ffn_conv_b=v_ffn_conv_b, w_down=v_w_down, final_norm_w=v_final_norm_w)
    order = list(weights)
    deltas, new_m, new_v = {}, {}, {}
    for n in BIG:
        shp = weights[n].shape
        res = _adamw(weights[n][0], grads[n], ms[n][0], vs[n][0], name="adamw_" + n)
        deltas[n], new_m[n], new_v[n], grads[n] = (a.reshape(shp) for a in res)
    smalls = [n for n in order if n not in BIG]
    as2d = lambda a: a.reshape(-1, a.shape[-1])
    res = _adamw_many(*[[as2d(src[n][0] if src[n].ndim == 3 else src[n]) for n in smalls] for src in (weights, grads, ms, vs)])
    for i, n in enumerate(smalls):
        deltas[n], new_m[n], new_v[n] = (res[q * len(smalls) + i].reshape(weights[n].shape) for q in range(3))
    out_grads = [grads[n].reshape(weights[n].shape) for n in order]
    return (loss, dx[None], *out_grads, *[deltas[n] for n in order], *[new_m[n] for n in order], *[new_v[n] for n in order])
```

```python
import functools

import jax
import jax.numpy as jnp
from jax import lax
from jax.experimental import pallas as pl
from jax.experimental.pallas import tpu as pltpu

F32 = jnp.float32
BF16 = jnp.bfloat16
HI = lax.Precision.HIGHEST

D_MODEL = 1024
Q_DIM = 1024
KV_DIM = 256
D_INNER = 2048
BC_DIM = 512
XBC_DIM = 3072
N_SSD_HEADS = 32
D_FF = 2816
IN_DIM = 8736
BLK = 128
EPS = 1e-5
NEG = -1e30

O_Q, O_K, O_V, O_Z, O_X, O_GA, O_GS, O_DT = 0, 1024, 1280, 1536, 3584, 6656, 7680, 8704
PW = 8960

ADAM_LR, ADAM_B1, ADAM_B2, ADAM_EPS, ADAM_WD, ADAM_STEP = 0.001, 0.9, 0.999, 1e-08, 0.01, 10

VMEM_LIMIT = 52 * 1024 * 1024
MESH = pl.DeviceIdType.MESH


def _cp(sem=None):
    return pltpu.CompilerParams(dimension_semantics=sem, vmem_limit_bytes=VMEM_LIMIT)


def _dot(a, b, prec=None):
    return jnp.dot(a, b, preferred_element_type=F32, precision=prec)


def _dot_nt(a, b, prec=None):
    return lax.dot_general(a, b, (((1,), (1,)), ((), ())), preferred_element_type=F32, precision=prec)


def _dot_tn(a, b, prec=None):
    return lax.dot_general(a, b, (((0,), (0,)), ((), ())), preferred_element_type=F32, precision=prec)


def _sigmoid(x):
    return 0.5 * jnp.tanh(0.5 * x) + 0.5


def _tile(n, want):
    t = min(n, want)
    while n % t:
        t -= 128
    return t


def _accumulate(acc, part, kk, nk, finish):
    if nk == 1:
        finish(part)
        return

    @pl.when(kk == 0)
    def _():
        acc[...] = part

    @pl.when(kk > 0)
    def _():
        acc[...] += part

    @pl.when(kk == nk - 1)
    def _():
        finish(acc[...])


def _mm(a, b, *, name, ta=False, tb=False, out_dtype=F32, resid=None, tm=1024, tn=1024, tk=1024, side=None):
    m, k = (a.shape[1], a.shape[0]) if ta else a.shape
    slots = b.ndim == 3
    if slots:
        n = b.shape[1] if tb else b.shape[0] * b.shape[2]
        tn, tk = (tn, b.shape[2]) if tb else (b.shape[2], tk)
    else:
        n = b.shape[0] if tb else b.shape[1]
    tm, tn, tk = _tile(m, tm), _tile(n, tn), _tile(k, tk)
    nk = k // tk
    dn = (((0 if ta else 1,), (1 if tb else 0,)), ((), ()))

    def body(*refs):
        if resid is None:
            a_ref, b_ref, o_ref, acc = refs
        else:
            a_ref, b_ref, r_ref, o_ref, acc = refs
        kk = pl.program_id(2)
        bv = b_ref[0] if slots else b_ref[...]
        part = lax.dot_general(a_ref[...].astype(BF16), bv.astype(BF16), dn, preferred_element_type=F32)

        def finish(r):
            if resid is not None:
                r = r + r_ref[...]
            o_ref[...] = r.astype(out_dtype)

        _accumulate(acc, part, kk, nk, finish)

    a_spec = pl.BlockSpec((tk, tm), lambda i, j, q: (q, i)) if ta else pl.BlockSpec((tm, tk), lambda i, j, q: (i, q))
    if slots:
        b_spec = (pl.BlockSpec((1, tn, tk), lambda i, j, q: (q, j, 0)) if tb
                  else pl.BlockSpec((1, tk, tn), lambda i, j, q: (j, q, 0)))
    else:
        b_spec = pl.BlockSpec((tn, tk), lambda i, j, q: (j, q)) if tb else pl.BlockSpec((tk, tn), lambda i, j, q: (q, j))
    o_spec = pl.BlockSpec((tm, tn), lambda i, j, q: (i, j))
    ins, specs = [a, b], [a_spec, b_spec]
    if resid is not None:
        ins.append(resid)
        specs.append(o_spec)
    own, extra = _hosted(
        body, name=name, grid=(m // tm, n // tn, nk), in_specs=specs, out_specs=[o_spec],
        out_shape=[jax.ShapeDtypeStruct((m, n), out_dtype)], scratch_shapes=[pltpu.VMEM((tm, tn), F32)],
        args=ins, sem=("parallel", "parallel", "arbitrary"), side=side)
    return own[0] if side is None else (own[0], extra)


def _rms_fwd(x, w, *, name, tm=512, with_t=False):
    s, d = x.shape
    tm = _tile(s, tm)

    def body(x_ref, w_ref, o_ref, *t_ref):
        xv = x_ref[...]
        r = lax.rsqrt(jnp.mean(xv * xv, axis=-1, keepdims=True) + EPS)
        y = (xv * r) * w_ref[...]
        o_ref[...] = y.astype(BF16)
        if with_t:
            t_ref[0][...] = y.T.astype(BF16)

    row = pl.BlockSpec((tm, d), lambda i: (i, 0))
    res = pl.pallas_call(
        body, name=name, grid=(s // tm,), in_specs=[row, pl.BlockSpec((1, d), lambda i: (0, 0))],
        out_specs=[row] + [pl.BlockSpec((d, tm), lambda i: (0, i))] * with_t,
        out_shape=[jax.ShapeDtypeStruct((s, d), BF16)] + [jax.ShapeDtypeStruct((d, s), BF16)] * with_t,
        compiler_params=_cp(("parallel",)),
    )(x, w)
    return res if with_t else res[0]


def _rms_bwd(dy, x, w, resid, *, name, tm=512):
    s, d = x.shape
    tm = _tile(s, tm)

    def body(dy_ref, x_ref, w_ref, r_ref, dx_ref, dw_ref):
        i = pl.program_id(0)
        xv = x_ref[...]
        r = lax.rsqrt(jnp.mean(xv * xv, axis=-1, keepdims=True) + EPS)
        xh = xv * r
        dyv = dy_ref[...]
        g = dyv * w_ref[...]
        dx_ref[...] = r_ref[...] + r * (g - xh * jnp.mean(g * xh, axis=-1, keepdims=True))
        part = jnp.sum(dyv * xh, axis=0, keepdims=True)

        @pl.when(i == 0)
        def _():
            dw_ref[...] = part

        @pl.when(i > 0)
        def _():
            dw_ref[...] += part

    row = pl.BlockSpec((tm, d), lambda i: (i, 0))
    vec = pl.BlockSpec((1, d), lambda i: (0, 0))
    return pl.pallas_call(
        body, name=name, grid=(s // tm,), in_specs=[row, row, vec, row], out_specs=[row, vec],
        out_shape=[jax.ShapeDtypeStruct((s, d), F32), jax.ShapeDtypeStruct((1, d), F32)],
        compiler_params=_cp(("arbitrary",)),
    )(dy, x, w, resid)


def _loss_bwd(h2, tgt, wf, *, tm=512):
    s, d = h2.shape
    tm = _tile(s, tm)

    def body(h_ref, t_ref, w_ref, dh_ref, loss_ref, dw_ref):
        i = pl.program_id(0)
        hv = h_ref[...]
        r = lax.rsqrt(jnp.mean(hv * hv, axis=-1, keepdims=True) + EPS)
        xh = hv * r
        wv = w_ref[...]
        e = xh * wv - t_ref[...]
        lpart = 0.5 * jnp.sum(jnp.mean(e * e, axis=-1, keepdims=True), axis=0, keepdims=True)
        dout = e * (1.0 / d)
        g = dout * wv
        dh_ref[...] = r * (g - xh * jnp.mean(g * xh, axis=-1, keepdims=True))
        part = jnp.sum(dout * xh, axis=0, keepdims=True)
        lrow = jnp.broadcast_to(lpart, (1, 128))

        @pl.when(i == 0)
        def _():
            dw_ref[...] = part
            loss_ref[...] = lrow

        @pl.when(i > 0)
        def _():
            dw_ref[...] += part
            loss_ref[...] += lrow

    row = pl.BlockSpec((tm, d), lambda i: (i, 0))
    vec = pl.BlockSpec((1, d), lambda i: (0, 0))
    return pl.pallas_call(
        body, name="loss_bwd", grid=(s // tm,), in_specs=[row, row, vec],
        out_specs=[row, pl.BlockSpec((1, 128), lambda i: (0, 0)), vec],
        out_shape=[jax.ShapeDtypeStruct((s, d), F32), jax.ShapeDtypeStruct((1, 128), F32),
                   jax.ShapeDtypeStruct((1, d), F32)],
        compiler_params=_cp(("arbitrary",)),
    )(h2, tgt, wf)


def _attn_mask(n):
    si = lax.broadcasted_iota(jnp.int32, (2 * BLK, 4 * BLK), 0)
    qi = lax.broadcasted_iota(jnp.int32, (2 * BLK, 4 * BLK), 1) & (BLK - 1)
    dist = BLK + qi - si
    kpos = n * BLK - BLK + si
    return (dist >= 0) & (dist < BLK) & (kpos >= 0)


def _attn_probs(q_ref, kc_ref, kp_ref, sk_ref, kvh, valid):
    rows = slice(kvh * 64, (kvh + 1) * 64)
    kt = jnp.concatenate([kp_ref[rows, :], kc_ref[rows, :]], axis=1).astype(BF16)
    qt = jnp.concatenate([q_ref[(kvh * 4 + g) * 64:(kvh * 4 + g + 1) * 64, :] for g in range(4)], axis=1).astype(BF16)
    s = _dot_tn(kt, qt) * 0.125
    s = jnp.where(valid, s, NEG)
    head = lax.broadcasted_iota(jnp.int32, (1, 4 * BLK), 1) >> 7
    sink = jnp.zeros((1, 4 * BLK), F32)
    for g in range(4):
        sink = jnp.where(head == g, sk_ref[0:1, kvh * 4 + g:kvh * 4 + g + 1], sink)
    m = jnp.maximum(jnp.max(s, axis=0, keepdims=True), sink)
    p = jnp.where(valid, jnp.exp(s - m), 0.0)
    es = jnp.exp(sink - m)
    inv = 1.0 / (jnp.sum(p, axis=0, keepdims=True) + es)
    return qt, kt, p * inv, es * inv


def _attn_in_specs(cur, prev):
    return [pl.BlockSpec((Q_DIM, BLK), lambda n: (0, cur(n))),
            pl.BlockSpec((KV_DIM, BLK), lambda n: (O_K // KV_DIM, cur(n))),
            pl.BlockSpec((KV_DIM, BLK), lambda n: (O_K // KV_DIM, prev(n))),
            pl.BlockSpec((KV_DIM, BLK), lambda n: (O_V // KV_DIM, cur(n))),
            pl.BlockSpec((KV_DIM, BLK), lambda n: (O_V // KV_DIM, prev(n))),
            pl.BlockSpec((1, 128), lambda n: (0, 0))]


def _attn_fwd(qkvt, sinks, side=None):
    s = qkvt.shape[1]
    nb = s // BLK

    def body(q_ref, kc_ref, kp_ref, vc_ref, vp_ref, sk_ref, o_ref):
        valid = _attn_mask(pl.program_id(0))
        for kvh in range(4):
            rows = slice(kvh * 64, (kvh + 1) * 64)
            _, _, probs, _ = _attn_probs(q_ref, kc_ref, kp_ref, sk_ref, kvh, valid)
            vt = jnp.concatenate([vp_ref[rows, :], vc_ref[rows, :]], axis=1).astype(BF16)
            o = _dot(vt, probs.astype(BF16))
            for g in range(4):
                h = kvh * 4 + g
                o_ref[h * 64:(h + 1) * 64, :] = o[:, g * BLK:(g + 1) * BLK].astype(BF16)

    own, extra = _hosted(
        body, name="attn_fwd", grid=(nb,), in_specs=_attn_in_specs(lambda n: n, lambda n: jnp.maximum(n - 1, 0)),
        out_specs=[pl.BlockSpec((Q_DIM, BLK), lambda n: (0, n))],
        out_shape=[jax.ShapeDtypeStruct((Q_DIM, s), BF16)], scratch_shapes=[],
        args=(qkvt, qkvt, qkvt, qkvt, qkvt, sinks), sem=("parallel",), side=side)
    return own[0] if side is None else (own[0], extra)


def _attn_bwd(qkvt, sinks, o, do, side=None):
    s = qkvt.shape[1]
    nb = s // BLK

    def body(q_ref, kc_ref, kp_ref, vc_ref, vp_ref, sk_ref, o_ref, do_ref, dq_ref, dk_ref, dv_ref, dsk_ref, ck, cv, nk, nv):
        n = pl.program_id(0)

        @pl.when(n == 0)
        def _():
            ck[...] = jnp.zeros_like(ck)
            cv[...] = jnp.zeros_like(cv)
            dsk_ref[...] = jnp.zeros_like(dsk_ref)

        @pl.when(n < nb)
        def _():
            valid = _attn_mask(n)
            lane = lax.broadcasted_iota(jnp.int32, (1, 128), 1)
            dsk = jnp.zeros((1, 128), F32)
            for kvh in range(4):
                rows = slice(kvh * 64, (kvh + 1) * 64)
                qt, kt, probs, psink = _attn_probs(q_ref, kc_ref, kp_ref, sk_ref, kvh, valid)
                vt = jnp.concatenate([vp_ref[rows, :], vc_ref[rows, :]], axis=1).astype(BF16)
                heads = [slice((kvh * 4 + g) * 64, (kvh * 4 + g + 1) * 64) for g in range(4)]
                dot = jnp.concatenate([do_ref[hh, :] for hh in heads], axis=1)
                ot = jnp.concatenate([o_ref[hh, :] for hh in heads], axis=1).astype(F32)
                delta = jnp.sum(dot * ot, axis=0, keepdims=True)
                dot16 = dot.astype(BF16)
                dp = _dot_tn(vt, dot16)
                ds = (probs * (dp - delta) * 0.125).astype(BF16)
                dqt = _dot(kt, ds)
                nk[rows, :] = _dot_nt(qt, ds)
                nv[rows, :] = _dot_nt(dot16, probs.astype(BF16))
                sd = psink * delta
                for g in range(4):
                    dq_ref[heads[g], :] = dqt[:, g * BLK:(g + 1) * BLK].astype(BF16)
                    val = -jnp.sum(sd[:, g * BLK:(g + 1) * BLK], axis=1, keepdims=True)
                    dsk = dsk + jnp.where(lane == kvh * 4 + g, val, 0.0)
            dsk_ref[0:1, :] += dsk
            dk_ref[...] = (ck[...] + nk[:, :BLK]).astype(BF16)
            dv_ref[...] = (cv[...] + nv[:, :BLK]).astype(BF16)
            ck[...] = nk[:, BLK:]
            cv[...] = nv[:, BLK:]

        @pl.when(n == nb)
        def _():
            dk_ref[...] = ck[...].astype(BF16)
            dv_ref[...] = cv[...].astype(BF16)

    cur = lambda n: jnp.minimum(n, nb - 1)
    prev = lambda n: jnp.maximum(jnp.minimum(n, nb - 1) - 1, 0)
    outb = lambda n: jnp.maximum(n - 1, 0)
    own, extra = _hosted(
        body, name="attn_bwd", grid=(nb + 1,),
        in_specs=_attn_in_specs(cur, prev) + [pl.BlockSpec((Q_DIM, BLK), lambda n: (0, cur(n))),
                                              pl.BlockSpec((Q_DIM, BLK), lambda n: (0, cur(n)))],
        out_specs=[pl.BlockSpec((Q_DIM, BLK), lambda n: (0, cur(n))),
                   pl.BlockSpec((KV_DIM, BLK), lambda n: (0, outb(n))),
                   pl.BlockSpec((KV_DIM, BLK), lambda n: (0, outb(n))),
                   pl.BlockSpec((8, 128), lambda n: (0, 0))],
        out_shape=[jax.ShapeDtypeStruct((Q_DIM, s), BF16), jax.ShapeDtypeStruct((KV_DIM, s), BF16),
                   jax.ShapeDtypeStruct((KV_DIM, s), BF16), jax.ShapeDtypeStruct((8, 128), F32)],
        scratch_shapes=[pltpu.VMEM((KV_DIM, BLK), F32)] * 2 + [pltpu.VMEM((KV_DIM, 2 * BLK), F32)] * 2,
        args=(qkvt, qkvt, qkvt, qkvt, qkvt, sinks, o, do), sem=("arbitrary",), side=side)
    return own if side is None else (own, extra)


def _shift_down(x, j):
    if j == 0:
        return x
    row = lax.broadcasted_iota(jnp.int32, x.shape, 0)
    return jnp.where(row >= j, pltpu.roll(x, j, 0), 0.0)


def _shift_up(x, j):
    if j == 0:
        return x
    s = x.shape[0]
    row = lax.broadcasted_iota(jnp.int32, x.shape, 0)
    return jnp.where(row < s - j, pltpu.roll(x, s - j, 0), 0.0)


def _conv(x, w_ref, b_ref):
    kk = w_ref.shape[0]
    y = _shift_down(x, kk - 1) * w_ref[0:1, :]
    for q in range(1, kk):
        y = y + _shift_down(x, kk - 1 - q) * w_ref[q:q + 1, :]
    return y + b_ref[...]


def _conv_bwd(dy, x, w_ref, dx_dtype):
    kk = w_ref.shape[0]
    dx = _shift_up(dy, kk - 1) * w_ref[0:1, :]
    dws = [jnp.sum(dy * _shift_down(x, kk - 1), axis=0, keepdims=True)]
    for q in range(1, kk):
        dx = dx + _shift_up(dy, kk - 1 - q) * w_ref[q:q + 1, :]
        dws.append(jnp.sum(dy * _shift_down(x, kk - 1 - q), axis=0, keepdims=True))
    return dx.astype(dx_dtype), dws, jnp.sum(dy, axis=0, keepdims=True)


def _dsilu(y, sg):
    return sg * (1.0 + y * (1.0 - sg))


CT = 256


def _ssd_conv_fwd(proj, w, b):
    s = proj.shape[0]

    def body(x_ref, w_ref, b_ref, o_ref):
        y = _conv(x_ref[...], w_ref, b_ref)
        o_ref[...] = y * _sigmoid(y)

    return pl.pallas_call(
        body, name="ssd_conv_fwd", grid=(XBC_DIM // CT,),
        in_specs=[pl.BlockSpec((s, CT), lambda i: (0, O_X // CT + i)), pl.BlockSpec((4, CT), lambda i: (0, i)),
                  pl.BlockSpec((1, CT), lambda i: (0, i))],
        out_specs=pl.BlockSpec((s, CT), lambda i: (0, i)),
        out_shape=jax.ShapeDtypeStruct((s, XBC_DIM), F32), compiler_params=_cp(("parallel",)),
    )(proj, w, b)


def _ssd_conv_bwd(dact, proj, w, b):
    s = proj.shape[0]

    def body(d_ref, x_ref, w_ref, b_ref, dx_ref, dw_ref, db_ref):
        x = x_ref[...]
        y = _conv(x, w_ref, b_ref)
        dy = d_ref[...] * _dsilu(y, _sigmoid(y))
        dx, dws, db = _conv_bwd(dy, x, w_ref, BF16)
        dx_ref[...] = dx
        for q in range(4):
            dw_ref[q:q + 1, :] = dws[q]
        db_ref[...] = db

    return pl.pallas_call(
        body, name="ssd_conv_bwd", grid=(XBC_DIM // CT,),
        in_specs=[pl.BlockSpec((s, CT), lambda i: (0, i)), pl.BlockSpec((s, CT), lambda i: (0, O_X // CT + i)),
                  pl.BlockSpec((4, CT), lambda i: (0, i)), pl.BlockSpec((1, CT), lambda i: (0, i))],
        out_specs=[pl.BlockSpec((s, CT), lambda i: (0, i)), pl.BlockSpec((4, CT), lambda i: (0, i)),
                   pl.BlockSpec((1, CT), lambda i: (0, i))],
        out_shape=[jax.ShapeDtypeStruct((s, XBC_DIM), BF16), jax.ShapeDtypeStruct((4, XBC_DIM), F32),
                   jax.ShapeDtypeStruct((1, XBC_DIM), F32)],
        compiler_params=_cp(("parallel",)),
    )(dact, proj, w, b)


NFT = D_FF // CT


def _ffn_act_fwd(up, w, b):
    s = up.shape[0]

    def body(v_ref, g_ref, wv_ref, wg_ref, bv_ref, bg_ref, o_ref):
        val = _conv(v_ref[...], wv_ref, bv_ref)
        gt = _conv(g_ref[...], wg_ref, bg_ref)
        o_ref[...] = ((gt * _sigmoid(gt)) * val).astype(BF16)

    col = lambda off: (lambda i: (0, off + i))
    return pl.pallas_call(
        body, name="ffn_act_fwd", grid=(NFT,),
        in_specs=[pl.BlockSpec((s, CT), col(0)), pl.BlockSpec((s, CT), col(NFT)),
                  pl.BlockSpec((3, CT), col(0)), pl.BlockSpec((3, CT), col(NFT)),
                  pl.BlockSpec((1, CT), col(0)), pl.BlockSpec((1, CT), col(NFT))],
        out_specs=pl.BlockSpec((s, CT), col(0)),
        out_shape=jax.ShapeDtypeStruct((s, D_FF), BF16), compiler_params=_cp(("parallel",)),
    )(up, up, w, w, b, b)


def _ffn_act_bwd(dact, up, w, b):
    s = up.shape[0]

    def body(d_ref, v_ref, g_ref, wv_ref, wg_ref, bv_ref, bg_ref, dx_ref, dw_ref, db_ref):
        xv, xg = v_ref[...], g_ref[...]
        val = _conv(xv, wv_ref, bv_ref)
        gt = _conv(xg, wg_ref, bg_ref)
        sg = _sigmoid(gt)
        d = d_ref[...]
        for half, (dy, x, w_ref) in enumerate(((d * (gt * sg), xv, wv_ref), (d * val * _dsilu(gt, sg), xg, wg_ref))):
            dx, dws, db = _conv_bwd(dy, x, w_ref, BF16)
            dx_ref[half] = dx
            for q in range(3):
                dw_ref[half, q:q + 1, :] = dws[q]
            db_ref[half] = db

    col = lambda off: (lambda i: (0, off + i))
    both = lambda i: (0, 0, i)
    return pl.pallas_call(
        body, name="ffn_act_bwd", grid=(NFT,),
        in_specs=[pl.BlockSpec((s, CT), col(0)), pl.BlockSpec((s, CT), col(0)), pl.BlockSpec((s, CT), col(NFT)),
                  pl.BlockSpec((3, CT), col(0)), pl.BlockSpec((3, CT), col(NFT)),
                  pl.BlockSpec((1, CT), col(0)), pl.BlockSpec((1, CT), col(NFT))],
        out_specs=[pl.BlockSpec((2, s, CT), both), pl.BlockSpec((2, 3, CT), both), pl.BlockSpec((2, 1, CT), both)],
        out_shape=[jax.ShapeDtypeStruct((2, s, D_FF), BF16), jax.ShapeDtypeStruct((2, 3, D_FF), F32),
                   jax.ShapeDtypeStruct((2, 1, D_FF), F32)],
        compiler_params=_cp(("parallel",)),
    )(dact, up, up, w, w, b, b)


def _expand_mat():
    r = lax.broadcasted_iota(jnp.int32, (128, D_INNER), 0)
    c = lax.broadcasted_iota(jnp.int32, (128, D_INNER), 1)
    return ((c >> 6) == r).astype(BF16)


def _reduce_mat():
    r = lax.broadcasted_iota(jnp.int32, (D_INNER, 128), 0)
    c = lax.broadcasted_iota(jnp.int32, (D_INNER, 128), 1)
    return ((r >> 6) == c).astype(BF16)


def _split(v, parts):
    out = []
    for _ in range(parts - 1):
        p = v.astype(BF16)
        out.append(p)
        v = v - p.astype(F32)
    out.append(v.astype(BF16))
    return out


def _sel_dot(v, sel, parts):
    acc = None
    for p in reversed(_split(v, parts)):
        t = _dot(p, sel)
        acc = t if acc is None else acc + t
    return acc


def _row8(v):
    return jnp.broadcast_to(v, (8, v.shape[1]))


def _tril():
    r = lax.broadcasted_iota(jnp.int32, (BLK, BLK), 0)
    c = lax.broadcasted_iota(jnp.int32, (BLK, BLK), 1)
    return r >= c


def _softplus(x):
    return jnp.maximum(x, 0.0) + jnp.log(1.0 + jnp.exp(-jnp.abs(x)))


def _ssd_common(dtraw_ref, dtb_ref, alog_ref):
    causal = _tril()
    e_mat = _expand_mat()
    a_neg = -jnp.exp(alog_ref[...])
    dt = _softplus(dtraw_ref[...] + dtb_ref[...])
    a_cs = _dot(causal.astype(F32), dt * a_neg, HI)
    a_cs_t = a_cs.T
    dt_x = _sel_dot(dt, e_mat, 3)
    acs_x = _sel_dot(a_cs, e_mat, 3)
    alast_x = acs_x[BLK - 1:BLK, :]
    ea_x = jnp.exp(acs_x)
    ds_x = jnp.exp(alast_x - acs_x)
    elast_x = jnp.exp(alast_x)
    return causal, e_mat, a_neg, dt, a_cs, a_cs_t, dt_x, ea_x, ds_x, elast_x


def _decay(a_cs, a_cs_t, h, causal):
    seg = a_cs[:, h:h + 1] - a_cs_t[h:h + 1, :]
    return jnp.where(causal, jnp.exp(jnp.where(causal, seg, 0.0)), 0.0)


def _ssd_fwd(xbc, proj, dt_bias, a_log, d_skip, side=None):
    s = xbc.shape[0]
    nc = s // BLK

    def body(xs_ref, b_ref, c_ref, dtraw_ref, dtb_ref, alog_ref, dskip_ref, y_ref, hp_ref, h_scr, xc16):
        @pl.when(pl.program_id(0) == 0)
        def _():
            h_scr[...] = jnp.zeros_like(h_scr)

        causal, e_mat, _, _, a_cs, a_cs_t, dt_x, ea_x, ds_x, elast_x = _ssd_common(dtraw_ref, dtb_ref, alog_ref)
        dskip_x = _sel_dot(_row8(dskip_ref[...]), e_mat, 3)[0:1]
        xs = xs_ref[...]
        xc = xs * dt_x
        xc16[...] = xc.astype(BF16)
        xcd = (xc * ds_x).astype(BF16)
        hp_ref[0] = h_scr[...]
        for g in range(4):
            gs = slice(g * 512, (g + 1) * 512)
            cg = c_ref[:, g * 128:(g + 1) * 128].astype(BF16)
            bg = b_ref[:, g * 128:(g + 1) * 128].astype(BF16)
            cb = _dot_nt(cg, bg)
            hg = h_scr[:, gs]
            yoff = _dot(cg, hg.astype(BF16)) * ea_x[:, gs]
            for j in range(8):
                h = g * 8 + j
                hsl = slice(h * 64, (h + 1) * 64)
                mm = (cb * _decay(a_cs, a_cs_t, h, causal)).astype(BF16)
                y_ref[:, hsl] = _dot(mm, xc16[:, hsl])
            y_ref[:, gs] += yoff + xs[:, gs] * dskip_x[:, gs]
            h_scr[:, gs] = hg * elast_x[:, gs] + _dot_tn(bg, xcd[:, gs])

    vec = pl.BlockSpec((1, 128), lambda c: (0, 0))
    own, extra = _hosted(
        body, name="ssd_fwd", grid=(nc,),
        in_specs=[pl.BlockSpec((BLK, D_INNER), lambda c: (c, 0)),
                  pl.BlockSpec((BLK, BC_DIM), lambda c: (c, D_INNER // BC_DIM)),
                  pl.BlockSpec((BLK, BC_DIM), lambda c: (c, D_INNER // BC_DIM + 1)),
                  pl.BlockSpec((BLK, 128), lambda c: (c, O_DT // 128)), vec, vec, vec],
        out_specs=[pl.BlockSpec((BLK, D_INNER), lambda c: (c, 0)),
                   pl.BlockSpec((1, 128, D_INNER), lambda c: (c, 0, 0))],
        out_shape=[jax.ShapeDtypeStruct((s, D_INNER), F32), jax.ShapeDtypeStruct((nc, 128, D_INNER), F32)],
        scratch_shapes=[pltpu.VMEM((128, D_INNER), F32), pltpu.VMEM((BLK, D_INNER), BF16)],
        args=(xbc, xbc, xbc, proj, dt_bias, a_log, d_skip), sem=("arbitrary",), side=side)
    return own if side is None else (own, extra)


def _ssd_bwd(xbc, proj, dt_bias, a_log, d_skip, hprev, dy, side=None):
    s = xbc.shape[0]
    nc = s // BLK

    def body(xs_ref, b_ref, c_ref, dtraw_ref, dtb_ref, alog_ref, dskip_ref, hp_ref, dy_ref,
             dxbc_ref, ddt_ref, dvec_ref, dh_scr, xc16, dy16, dxc_scr, dacs_r, tdiff):
        step = pl.program_id(0)
        dacs_r[...] = jnp.zeros_like(dacs_r)

        @pl.when(step == 0)
        def _():
            dh_scr[...] = jnp.zeros_like(dh_scr)
            dvec_ref[...] = jnp.zeros_like(dvec_ref)

        causal, e_mat, a_neg, dt, a_cs, a_cs_t, dt_x, ea_x, ds_x, elast_x = _ssd_common(dtraw_ref, dtb_ref, alog_ref)
        r_mat = _reduce_mat()
        lane = lax.broadcasted_iota(jnp.int32, (1, 128), 1)
        dskip_x = _sel_dot(_row8(dskip_ref[...]), e_mat, 3)[0:1]
        xs = xs_ref[...]
        dy = dy_ref[...]
        xc = xs * dt_x
        xcd = xc * ds_x
        xc16[...] = xc.astype(BF16)
        dy16[...] = dy.astype(BF16)
        dyea = dy * ea_x
        dh = dh_scr[...]
        hp = hp_ref[0]
        dalast_x = jnp.sum(dh * hp, axis=0, keepdims=True) * elast_x
        dacs = jnp.zeros((BLK, 128), F32)
        for g in range(4):
            gs = slice(g * 512, (g + 1) * 512)
            bsl = slice(g * 128, (g + 1) * 128)
            cg = c_ref[:, bsl].astype(BF16)
            bg = b_ref[:, bsl].astype(BF16)
            cb = _dot_nt(cg, bg)
            hg16 = hp[:, gs].astype(BF16)
            dhg16 = dh[:, gs].astype(BF16)
            raw = _dot(cg, hg16)
            draw16 = dyea[:, gs].astype(BF16)
            dcg = _dot_nt(draw16, hg16)
            dhp_g = _dot_tn(cg, draw16)
            dbg = _dot_nt(xcd[:, gs].astype(BF16), dhg16)
            dxcd = _dot(bg, dhg16)
            dcb = jnp.zeros((BLK, BLK), F32)
            for j in range(8):
                h = g * 8 + j
                hsl = slice(h * 64, (h + 1) * 64)
                decay = _decay(a_cs, a_cs_t, h, causal)
                m = cb * decay
                dm = _dot_nt(dy16[:, hsl], xc16[:, hsl])
                dxc_scr[:, hsl] = _dot_tn(m.astype(BF16), dy16[:, hsl])
                dcb = dcb + dm * decay
                dseg = dm * m
                oneh = jnp.where(lane == h, 1.0, 0.0)
                dacs = dacs + jnp.sum(dseg, axis=1, keepdims=True) * oneh
                dacs_r[h:h + 1, :] = jnp.sum(dseg, axis=0, keepdims=True)
            dcb16 = dcb.astype(BF16)
            dcg = dcg + _dot(dcb16, bg)
            dbg = dbg + _dot_tn(dcb16, cg)
            dxbc_ref[:, D_INNER + g * 128:D_INNER + (g + 1) * 128] = dbg
            dxbc_ref[:, D_INNER + BC_DIM + g * 128:D_INNER + BC_DIM + (g + 1) * 128] = dcg
            dxc_scr[:, gs] += dxcd * ds_x[:, gs]
            dh_scr[:, gs] = dh[:, gs] * elast_x[:, gs] + dhp_g
            tst = dxcd * xcd[:, gs]
            tdiff[:, gs] = dy[:, gs] * (raw * ea_x[:, gs]) - tst
            tdiff[BLK - 1:BLK, gs] += jnp.sum(tst, axis=0, keepdims=True)
        dxc = dxc_scr[...]
        row = lax.broadcasted_iota(jnp.int32, (BLK, D_INNER), 0)
        tfull = tdiff[...] + jnp.where(row == BLK - 1, dalast_x, 0.0)
        dacs = dacs + _sel_dot(tfull, r_mat, 2) - dacs_r[...].T
        da = _dot_tn(causal.astype(F32), dacs, HI)
        ddt = da * a_neg + _sel_dot(dxc * xs, r_mat, 2)
        lmask = lax.broadcasted_iota(jnp.int32, (BLK, 128), 1) < N_SSD_HEADS
        ddtraw = jnp.where(lmask, ddt * _sigmoid(dtraw_ref[...] + dtb_ref[...]), 0.0)
        ddt_ref[...] = ddtraw.astype(BF16)
        dxbc_ref[:, 0:D_INNER] = dy * dskip_x + dxc * dt_x
        dvec_ref[0:1, :] += jnp.sum(ddtraw, axis=0, keepdims=True)
        dvec_ref[1:2, :] += jnp.where(lane < N_SSD_HEADS, jnp.sum(da * dt, axis=0, keepdims=True) * a_neg, 0.0)
        dvec_ref[2:3, :] += _sel_dot(_row8(jnp.sum(dy * xs, axis=0, keepdims=True)), r_mat, 3)[0:1]

    rev = lambda c: nc - 1 - c
    vec = pl.BlockSpec((1, 128), lambda c: (0, 0))
    own, extra = _hosted(
        body, name="ssd_bwd", grid=(nc,),
        in_specs=[pl.BlockSpec((BLK, D_INNER), lambda c: (rev(c), 0)),
                  pl.BlockSpec((BLK, BC_DIM), lambda c: (rev(c), D_INNER // BC_DIM)),
                  pl.BlockSpec((BLK, BC_DIM), lambda c: (rev(c), D_INNER // BC_DIM + 1)),
                  pl.BlockSpec((BLK, 128), lambda c: (rev(c), O_DT // 128)), vec, vec, vec,
                  pl.BlockSpec((1, 128, D_INNER), lambda c: (rev(c), 0, 0)),
                  pl.BlockSpec((BLK, D_INNER), lambda c: (rev(c), 0))],
        out_specs=[pl.BlockSpec((BLK, XBC_DIM), lambda c: (rev(c), 0)),
                   pl.BlockSpec((BLK, 128), lambda c: (rev(c), 0)),
                   pl.BlockSpec((8, 128), lambda c: (0, 0))],
        out_shape=[jax.ShapeDtypeStruct((s, XBC_DIM), F32), jax.ShapeDtypeStruct((s, 128), BF16),
                   jax.ShapeDtypeStruct((8, 128), F32)],
        scratch_shapes=[pltpu.VMEM((128, D_INNER), F32), pltpu.VMEM((BLK, D_INNER), BF16),
                        pltpu.VMEM((BLK, D_INNER), BF16), pltpu.VMEM((BLK, D_INNER), F32),
                        pltpu.VMEM((128, BLK), F32), pltpu.VMEM((BLK, D_INNER), F32)],
        args=(xbc, xbc, xbc, proj, dt_bias, a_log, d_skip, hprev, dy), sem=("arbitrary",), side=side)
    return own if side is None else (own, extra)


GW = 512


def _gate_norm_fwd(y, proj, wn, *, tm=512):
    s = y.shape[0]
    tm = _tile(s, tm)

    def body(y_ref, z_ref, w_ref, o_ref):
        z = z_ref[...]
        y2 = y_ref[...] * (z * _sigmoid(z))
        r = lax.rsqrt(jnp.mean(y2 * y2, axis=-1, keepdims=True) + EPS)
        o_ref[...] = ((y2 * r) * w_ref[...]).astype(BF16)

    return pl.pallas_call(
        body, name="gate_norm_fwd", grid=(s // tm, 4),
        in_specs=[pl.BlockSpec((tm, GW), lambda i, g: (i, g)), pl.BlockSpec((tm, GW), lambda i, g: (i, O_Z // GW + g)),
                  pl.BlockSpec((1, GW), lambda i, g: (0, g))],
        out_specs=pl.BlockSpec((tm, GW), lambda i, g: (i, g)),
        out_shape=jax.ShapeDtypeStruct((s, D_INNER), BF16), compiler_params=_cp(("parallel", "parallel")),
    )(y, proj, wn)


def _gate_norm_bwd(dyn, y, proj, wn, *, tm=512):
    s = y.shape[0]
    tm = _tile(s, tm)

    def body(d_ref, y_ref, z_ref, w_ref, dy_ref, dz_ref, dw_ref):
        i = pl.program_id(1)
        z = z_ref[...]
        sg = _sigmoid(z)
        sz = z * sg
        yv = y_ref[...]
        y2 = yv * sz
        r = lax.rsqrt(jnp.mean(y2 * y2, axis=-1, keepdims=True) + EPS)
        xh = y2 * r
        dv = d_ref[...]
        g = dv * w_ref[...]
        dy2 = r * (g - xh * jnp.mean(g * xh, axis=-1, keepdims=True))
        dy_ref[...] = dy2 * sz
        dz_ref[...] = (dy2 * yv * _dsilu(z, sg)).astype(BF16)
        part = jnp.sum(dv * xh, axis=0, keepdims=True)

        @pl.when(i == 0)
        def _():
            dw_ref[...] = part

        @pl.when(i > 0)
        def _():
            dw_ref[...] += part

    blk = pl.BlockSpec((tm, GW), lambda g, i: (i, g))
    vec = pl.BlockSpec((1, GW), lambda g, i: (0, g))
    return pl.pallas_call(
        body, name="gate_norm_bwd", grid=(4, s // tm),
        in_specs=[blk, blk, pl.BlockSpec((tm, GW), lambda g, i: (i, O_Z // GW + g)), vec],
        out_specs=[blk, blk, vec],
        out_shape=[jax.ShapeDtypeStruct((s, D_INNER), F32), jax.ShapeDtypeStruct((s, D_INNER), BF16),
                   jax.ShapeDtypeStruct((1, D_INNER), F32)],
        compiler_params=_cp(("parallel", "arbitrary")),
    )(dyn, y, proj, wn)


def _merge_fwd(proj, b_gate, attn, ssd_out, *, tm=512):
    s = attn.shape[0]
    tm = _tile(s, tm)

    def body(ga_ref, gs_ref, ba_ref, bs_ref, a_ref, s_ref, o_ref):
        ga = _sigmoid(ga_ref[...] + ba_ref[...])
        gs = _sigmoid(gs_ref[...] + bs_ref[...])
        o_ref[...] = (ga * a_ref[...] + gs * s_ref[...]).astype(BF16)

    blk = pl.BlockSpec((tm, GW), lambda i, j: (i, j))
    return pl.pallas_call(
        body, name="merge_fwd", grid=(s // tm, 2),
        in_specs=[pl.BlockSpec((tm, GW), lambda i, j: (i, O_GA // GW + j)),
                  pl.BlockSpec((tm, GW), lambda i, j: (i, O_GS // GW + j)),
                  pl.BlockSpec((1, GW), lambda i, j: (0, j)), pl.BlockSpec((1, GW), lambda i, j: (0, 2 + j)), blk, blk],
        out_specs=blk, out_shape=jax.ShapeDtypeStruct((s, D_MODEL), BF16),
        compiler_params=_cp(("parallel", "parallel")),
    )(proj, proj, b_gate, b_gate, attn, ssd_out)


def _merge_bwd(dm, proj, b_gate, attn, ssd_out, *, tm=512):
    s = attn.shape[0]
    tm = _tile(s, tm)

    def body(d_ref, ga_ref, gs_ref, ba_ref, bs_ref, a_ref, s_ref, da_ref, ds_ref, dga_ref, dgs_ref, dba_ref, dbs_ref):
        i = pl.program_id(1)
        ga = _sigmoid(ga_ref[...] + ba_ref[...])
        gs = _sigmoid(gs_ref[...] + bs_ref[...])
        d = d_ref[...]
        da_ref[...] = (d * ga).astype(BF16)
        ds_ref[...] = (d * gs).astype(BF16)
        dga = d * a_ref[...] * (ga * (1.0 - ga))
        dgs = d * s_ref[...] * (gs * (1.0 - gs))
        dga_ref[...] = dga.astype(BF16)
        dgs_ref[...] = dgs.astype(BF16)
        pa = jnp.sum(dga, axis=0, keepdims=True)
        ps = jnp.sum(dgs, axis=0, keepdims=True)

        @pl.when(i == 0)
        def _():
            dba_ref[...] = pa
            dbs_ref[...] = ps

        @pl.when(i > 0)
        def _():
            dba_ref[...] += pa
            dbs_ref[...] += ps

    blk = pl.BlockSpec((tm, GW), lambda j, i: (i, j))
    vec = pl.BlockSpec((1, GW), lambda j, i: (0, j))
    sd = jax.ShapeDtypeStruct((s, D_MODEL), BF16)
    vd = jax.ShapeDtypeStruct((1, D_MODEL), F32)
    return pl.pallas_call(
        body, name="merge_bwd", grid=(2, s // tm),
        in_specs=[blk, pl.BlockSpec((tm, GW), lambda j, i: (i, O_GA // GW + j)),
                  pl.BlockSpec((tm, GW), lambda j, i: (i, O_GS // GW + j)),
                  vec, pl.BlockSpec((1, GW), lambda j, i: (0, 2 + j)), blk, blk],
        out_specs=[blk, blk, blk, blk, vec, vec], out_shape=[sd, sd, sd, sd, vd, vd],
        compiler_params=_cp(("parallel", "arbitrary")),
    )(dm, proj, proj, b_gate, b_gate, attn, ssd_out)


def _adamw_math(w, g, m, v):
    mn = ADAM_B1 * m + (1.0 - ADAM_B1) * g
    vn = ADAM_B2 * v + (1.0 - ADAM_B2) * (g * g)
    m_hat = mn / (1.0 - ADAM_B1 ** ADAM_STEP)
    v_hat = vn / (1.0 - ADAM_B2 ** ADAM_STEP)
    return -ADAM_LR * (m_hat / (jnp.sqrt(v_hat) + ADAM_EPS) + ADAM_WD * w), mn, vn


def _adamw_many(ws, gs, ms, vs):
    n = len(ws)

    def body(*refs):
        outs = refs[4 * n:]
        for i in range(n):
            res = _adamw_math(*[refs[q * n + i][...] for q in range(4)])
            for q in range(3):
                outs[q * n + i][...] = res[q]

    return pl.pallas_call(body, name="adamw_small", out_shape=[jax.ShapeDtypeStruct(w.shape, F32) for w in ws] * 3,
                          compiler_params=_cp())(*ws, *gs, *ms, *vs)


def _adamw(w, g, m, v, *, name, tm=128):
    r, c = w.shape
    tm = r if (r < tm or r % tm) else tm

    def body(w_ref, g_ref, m_ref, v_ref, d_ref, nm_ref, nv_ref, g_out):
        gv = g_ref[:, :c]
        d_ref[...], nm_ref[...], nv_ref[...] = _adamw_math(w_ref[...], gv, m_ref[...], v_ref[...])
        g_out[...] = gv

    blk = pl.BlockSpec((tm, c), lambda i: (i, 0))
    sd = jax.ShapeDtypeStruct((r, c), F32)
    return pl.pallas_call(
        body, name=name, grid=(r // tm,), in_specs=[blk, pl.BlockSpec((tm, g.shape[1]), lambda i: (i, 0)), blk, blk],
        out_specs=[blk] * 4, out_shape=[sd] * 4, compiler_params=_cp(("parallel",)),
    )(w, g, m, v)


ANY = pl.BlockSpec(memory_space=pl.ANY)
N_CHIPS = 4


def _chip_of(k, x, y):
    return (x ^ (k >> 1), y ^ (k & 1))


def _all_gather_small(shard):
    r, c = shard.shape
    hr = r // 2

    def body(sh_ref, out_ref, send_sems, recv_sems, local_sem):
        x, y, cc = lax.axis_index("x"), lax.axis_index("y"), lax.axis_index("c")

        def half(px, py, pc):
            return out_ref.at[2 * px + py, pl.ds(pc * hr, hr), :]

        def copy(k, px, py, pc, to, src=None):
            return pltpu.make_async_remote_copy(
                src_ref=half(px, py, pc) if src is None else src, dst_ref=half(px, py, pc),
                send_sem=send_sems.at[k], recv_sem=recv_sems.at[k], device_id=to, device_id_type=MESH)

        mine = pltpu.make_async_copy(sh_ref, out_ref.at[2 * x + y], local_sem)
        mine.start()
        chips = [_chip_of(k, x, y) for k in (1, 2, 3)]
        first = [copy(j, x, y, cc, (*chip, cc), src=sh_ref.at[pl.ds(cc * hr, hr), :]) for j, chip in enumerate(chips)]
        for cp in first:
            cp.start()
        passed = [copy(3 + j, *chip, cc, (x, y, 1 - cc)) for j, chip in enumerate(chips)]
        for j, chip in enumerate(chips):
            copy(j, *chip, cc, (x, y, cc)).wait_recv()
            passed[j].start()
        for j, chip in enumerate(chips):
            copy(3 + j, *chip, 1 - cc, (x, y, cc)).wait_recv()
        for cp in first + passed:
            cp.wait_send()
        mine.wait()

    return pl.pallas_call(
        body, name="all_gather_small", in_specs=[ANY], out_specs=ANY,
        out_shape=jax.ShapeDtypeStruct((N_CHIPS, r, c), shard.dtype),
        scratch_shapes=[pltpu.SemaphoreType.DMA((6,)), pltpu.SemaphoreType.DMA((6,)), pltpu.SemaphoreType.DMA],
    )(shard)


def _cast_bf16(a, *, name, tm=512):
    n, r, c = a.shape
    tm = _tile(r, tm) if r % 128 == 0 else r

    def body(a_ref, o_ref):
        o_ref[...] = a_ref[...].astype(BF16)

    blk = pl.BlockSpec((1, tm, c), lambda i, j: (i, j, 0))
    return pl.pallas_call(body, name=name, grid=(n, r // tm), in_specs=[blk], out_specs=blk,
                          out_shape=jax.ShapeDtypeStruct(a.shape, BF16), compiler_params=_cp(("parallel", "parallel")))(a)


def _pair_exchange(g16, hr):
    n, r, c = g16.shape

    def body(g_ref, out_ref, send_sem, recv_sem):
        x, y, cc = lax.axis_index("x"), lax.axis_index("y"), lax.axis_index("c")
        cp = pltpu.make_async_remote_copy(
            src_ref=g_ref.at[:, pl.ds((1 - cc) * hr, hr), :], dst_ref=out_ref, send_sem=send_sem, recv_sem=recv_sem,
            device_id=(x, y, 1 - cc), device_id_type=MESH)
        cp.start()
        cp.wait()

    return pl.pallas_call(
        body, name="grad_pair_exchange", in_specs=[ANY], out_specs=ANY,
        out_shape=jax.ShapeDtypeStruct((n, hr, c), g16.dtype),
        scratch_shapes=[pltpu.SemaphoreType.DMA, pltpu.SemaphoreType.DMA],
    )(g16)


def _pair_add(g, recv, half_idx, hr, *, tm=384):
    n, r, c = g.shape
    nt = hr // tm

    def body(hi_ref, g_ref, r_ref, o32_ref, o16_ref):
        v = g_ref[...] + r_ref[...].astype(F32)
        o32_ref[...] = v
        o16_ref[...] = v.astype(BF16)

    gs = pltpu.PrefetchScalarGridSpec(
        num_scalar_prefetch=1, grid=(n, nt),
        in_specs=[pl.BlockSpec((1, tm, c), lambda i, j, hi: (i, hi[0] * nt + j, 0)),
                  pl.BlockSpec((1, tm, c), lambda i, j, hi: (i, j, 0))],
        out_specs=[pl.BlockSpec((1, tm, c), lambda i, j, hi: (i, j, 0))] * 2)
    return pl.pallas_call(
        body, name="grad_pair_add", grid_spec=gs,
        out_shape=[jax.ShapeDtypeStruct((n, hr, c), F32), jax.ShapeDtypeStruct((n, hr, c), BF16)],
        compiler_params=_cp(("parallel", "parallel")),
    )(half_idx, g, recv)


def _chip_exchange(p16):
    n, hr, c = p16.shape

    def body(p_ref, out_ref, send_sems, recv_sems):
        x, y, cc = lax.axis_index("x"), lax.axis_index("y"), lax.axis_index("c")
        cps = []
        for j, k in enumerate((1, 2, 3)):
            px, py = _chip_of(k, x, y)
            cps.append(pltpu.make_async_remote_copy(
                src_ref=p_ref.at[2 * px + py], dst_ref=out_ref.at[j], send_sem=send_sems.at[j], recv_sem=recv_sems.at[j],
                device_id=(px, py, cc), device_id_type=MESH))
        for cp in cps:
            cp.start()
        for cp in cps:
            cp.wait()

    return pl.pallas_call(
        body, name="grad_chip_exchange", in_specs=[ANY], out_specs=ANY,
        out_shape=jax.ShapeDtypeStruct((3, hr, c), p16.dtype),
        scratch_shapes=[pltpu.SemaphoreType.DMA((3,)), pltpu.SemaphoreType.DMA((3,))],
    )(p16)


def _chip_add(p32, recv, chip_idx, *, tm=384):
    n, hr, c = p32.shape

    def body(ci_ref, p_ref, r_ref, o_ref):
        o_ref[...] = ((p_ref[0] + r_ref[0].astype(F32)) + r_ref[1].astype(F32)) + r_ref[2].astype(F32)

    gs = pltpu.PrefetchScalarGridSpec(
        num_scalar_prefetch=1, grid=(hr // tm,),
        in_specs=[pl.BlockSpec((1, tm, c), lambda j, ci: (ci[0], j, 0)), pl.BlockSpec((3, tm, c), lambda j, ci: (0, j, 0))],
        out_specs=pl.BlockSpec((tm, c), lambda j, ci: (j, 0)))
    return pl.pallas_call(
        body, name="grad_chip_add", grid_spec=gs, out_shape=jax.ShapeDtypeStruct((hr, c), F32),
        compiler_params=_cp(("parallel",)),
    )(chip_idx, p32, recv)


def _pair_gather(f):
    hr, c = f.shape

    def body(f_ref, out_ref, send_sem, recv_sem, local_sem):
        x, y, cc = lax.axis_index("x"), lax.axis_index("y"), lax.axis_index("c")
        mine = pltpu.make_async_copy(f_ref, out_ref.at[pl.ds(cc * hr, hr), :], local_sem)
        mine.start()
        cp = pltpu.make_async_remote_copy(
            src_ref=f_ref, dst_ref=out_ref.at[pl.ds(cc * hr, hr), :], send_sem=send_sem, recv_sem=recv_sem,
            device_id=(x, y, 1 - cc), device_id_type=MESH)
        cp.start()
        cp.wait()
        mine.wait()

    return pl.pallas_call(
        body, name="grad_pair_gather", in_specs=[ANY], out_specs=ANY,
        out_shape=jax.ShapeDtypeStruct((2 * hr, c), f.dtype),
        scratch_shapes=[pltpu.SemaphoreType.DMA, pltpu.SemaphoreType.DMA, pltpu.SemaphoreType.DMA],
    )(f)


def _all_reduce_small(buf):
    r, c = buf.shape

    def body(b_ref, out_ref, gat, send_sems, recv_sems):
        x, y, cc = lax.axis_index("x"), lax.axis_index("y"), lax.axis_index("c")
        me = 4 * x + 2 * y + cc
        gat[me] = b_ref[...]
        cps = []
        for k in range(1, 8):
            px, py, pc = x ^ (k >> 2), y ^ ((k >> 1) & 1), cc ^ (k & 1)
            cps.append(pltpu.make_async_remote_copy(
                src_ref=b_ref, dst_ref=gat.at[me], send_sem=send_sems.at[k - 1], recv_sem=recv_sems.at[k - 1],
                device_id=(px, py, pc), device_id_type=MESH))
        for cp in cps:
            cp.start()
        for cp in cps:
            cp.wait()
        acc = gat[0]
        for d in range(1, 8):
            acc = acc + gat[d]
        out_ref[...] = acc

    vm = pl.BlockSpec(memory_space=pltpu.VMEM)
    return pl.pallas_call(
        body, name="all_reduce_small", in_specs=[vm], out_specs=vm, out_shape=jax.ShapeDtypeStruct((r, c), F32),
        scratch_shapes=[pltpu.VMEM((8, r, c), F32), pltpu.SemaphoreType.DMA((7,)), pltpu.SemaphoreType.DMA((7,))],
        compiler_params=pltpu.CompilerParams(vmem_limit_bytes=VMEM_LIMIT),
    )(buf)


def _pipe(fn, ins, outs, tr, depth=4, slots=None):
    shape = ins[0].shape
    lead, (r, c) = shape[:-2], shape[-2:]
    assert len(lead) <= 1 and r % tr == 0
    nr = r // tr
    which = list(range(lead[0])) if lead and slots is None else slots
    n = nr * (len(which) if lead else 1)
    ni, no = len(ins), len(outs)

    def blk(ref, step):
        rows = pl.ds((step % nr) * tr, tr)
        return ref.at[which[step // nr], rows, :] if lead else ref.at[rows, :]

    def scoped(*bufs):
        ibufs, obufs, isem, osem = bufs[:ni], bufs[ni:ni + no], bufs[-2], bufs[-1]

        def in_copy(q, step, slot):
            return pltpu.make_async_copy(blk(ins[q], step), ibufs[q].at[slot], isem.at[q, slot])

        def out_copy(q, step, slot):
            return pltpu.make_async_copy(obufs[q].at[slot], blk(outs[q], step), osem.at[q, slot])

        for step in range(min(nbuf - 1, n)):
            for q in range(ni):
                in_copy(q, step, step % nbuf).start()
        for step in range(n):
            slot = step % nbuf
            if step + nbuf - 1 < n:
                for q in range(ni):
                    in_copy(q, step + nbuf - 1, (step + nbuf - 1) % nbuf).start()
            for q in range(ni):
                in_copy(q, step, slot).wait()
            if step >= nbuf:
                for q in range(no):
                    out_copy(q, step - nbuf, slot).wait()
            res = fn(*[ibufs[q][slot] for q in range(ni)])
            for q in range(no):
                obufs[q][slot] = res[q].astype(obufs[q].dtype)
                out_copy(q, step, slot).start()
        for step in range(max(n - nbuf, 0), n):
            for q in range(no):
                out_copy(q, step, step % nbuf).wait()

    assert n <= 8
    nbuf = min(n, depth)
    pl.run_scoped(scoped, *[pltpu.VMEM((nbuf, tr, c), q.dtype) for q in ins], *[pltpu.VMEM((nbuf, tr, c), q.dtype) for q in outs],
                  pltpu.SemaphoreType.DMA((ni, nbuf)), pltpu.SemaphoreType.DMA((no, nbuf)))


W_IN_PAD = 2304
BIG = ("w_in", "w_attn_o", "w_ssd_o", "w_out", "w_up", "w_down")
BIG_SHAPE = dict(w_in=(D_MODEL, W_IN_PAD), w_attn_o=(Q_DIM // 4, D_MODEL), w_ssd_o=(D_INNER // 4, D_MODEL),
                 w_out=(D_MODEL // 4, D_MODEL), w_up=(D_MODEL, 2 * D_FF // 4), w_down=(D_FF // 4, D_MODEL))
BIG_TR = dict(w_in=128, w_attn_o=128, w_ssd_o=128, w_out=128, w_up=128, w_down=176)
X_FIRST = dict(w_in=True, w_attn_o=True, w_ssd_o=False, w_out=True, w_up=False, w_down=False)


def _neighbours(x, y, x_first):
    xn, yn = (1 - x, y), (x, 1 - y)
    n1, n2 = (xn, yn) if x_first else (yn, xn)
    slot = lambda ch: 2 * ch[0] + ch[1]
    return n1, n2, slot(n1), slot(n2), slot((1 - x, 1 - y))


def _gather_big(shards):
    nt = len(BIG)

    def body(*refs):
        sh, out = refs[:nt], refs[nt:2 * nt]
        send_sems, recv_sems = refs[2 * nt:]
        x, y, cc = lax.axis_index("x"), lax.axis_index("y"), lax.axis_index("c")
        me = 2 * x + y
        sib = (x, y, 1 - cc)
        for t, n in enumerate(BIG):
            _pipe(lambda v: (v,), [sh[t]], [out[t].at[me]], BIG_TR[n])

        def copy(t, k, slot, pc, to):
            hr = BIG_SHAPE[BIG[t]][0] // 2
            ref = out[t].at[slot, pl.ds(pc * hr, hr), :]
            return pltpu.make_async_remote_copy(src_ref=ref, dst_ref=ref, send_sem=send_sems.at[6 * t + k],
                                                recv_sem=recv_sems.at[6 * t + k], device_id=to, device_id_type=MESH)

        started = []

        def start(cp):
            cp.start()
            started.append(cp)

        geo = [_neighbours(x, y, X_FIRST[n]) for n in BIG]
        for t in range(nt):
            n1, n2, _, _, _ = geo[t]
            start(copy(t, 0, me, cc, (*n1, cc)))
            start(copy(t, 1, me, cc, (*n2, cc)))
        for t in range(nt):
            n1, n2, s1, s2, sd = geo[t]
            copy(t, 0, s1, cc, sib).wait_recv()
            start(copy(t, 2, s1, cc, (*n2, cc)))
            start(copy(t, 3, s1, cc, sib))
            copy(t, 1, s2, cc, sib).wait_recv()
            start(copy(t, 4, s2, cc, sib))
        for t in range(nt):
            _, _, s1, s2, sd = geo[t]
            copy(t, 2, sd, cc, sib).wait_recv()
            start(copy(t, 5, sd, cc, sib))
        for t in range(nt):
            _, _, s1, s2, sd = geo[t]
            copy(t, 3, s1, 1 - cc, sib).wait_recv()
            copy(t, 4, s2, 1 - cc, sib).wait_recv()
            copy(t, 5, sd, 1 - cc, sib).wait_recv()
        for cp in started:
            cp.wait_send()

    return pl.pallas_call(
        body, name="gather_big", in_specs=[ANY] * nt, out_specs=[ANY] * nt,
        out_shape=[jax.ShapeDtypeStruct((N_CHIPS, *BIG_SHAPE[n]), BF16) for n in BIG],
        scratch_shapes=[pltpu.SemaphoreType.DMA((6 * nt,)), pltpu.SemaphoreType.DMA((6 * nt,))],
        compiler_params=pltpu.CompilerParams(vmem_limit_bytes=VMEM_LIMIT),
    )(*shards)


def _reduce_big(grads):
    nt = len(BIG)
    nw = 7

    def body(*refs):
        g = refs[:nt]
        fin = refs[nt:2 * nt]
        work = refs[2 * nt:2 * nt + nw * nt]
        send_sems, recv_sems = refs[2 * nt + nw * nt:]
        x, y, cc = lax.axis_index("x"), lax.axis_index("y"), lax.axis_index("c")
        me = 2 * x + y
        sib = (x, y, 1 - cc)
        started = []

        def rcopy(t, k, src, dst, to):
            cp = pltpu.make_async_remote_copy(src_ref=src, dst_ref=dst, send_sem=send_sems.at[5 * t + k],
                                              recv_sem=recv_sems.at[5 * t + k], device_id=to, device_id_type=MESH)
            return cp

        def start(cp):
            cp.start()
            started.append(cp)

        geo = [_neighbours(x, y, X_FIRST[n]) for n in BIG]
        hrs = [BIG_SHAPE[n][0] // 2 for n in BIG]
        wk = lambda t: work[nw * t:nw * (t + 1)]
        one = lambda ref, slot: ref.at[pl.ds(slot, 1)]
        for t in range(nt):
            recv_a = wk(t)[0]
            start(rcopy(t, 0, g[t].at[:, pl.ds((1 - cc) * hrs[t], hrs[t]), :], recv_a, sib))
        for t, n in enumerate(BIG):
            recv_a, p32, p16, r1, qme, qs2, r2 = wk(t)
            n1, n2, s1, s2, sd = geo[t]
            rcopy(t, 0, recv_a, recv_a, sib).wait_recv()
            _pipe(lambda a, b: (a + b, a + b), [g[t].at[:, pl.ds(cc * hrs[t], hrs[t]), :], recv_a], [p32, p16], BIG_TR[n])
            start(rcopy(t, 1, one(p16, s1), one(r1, 0), (*n1, cc)))
            start(rcopy(t, 2, one(p16, sd), one(r1, 1), (*n1, cc)))
        for t, n in enumerate(BIG):
            recv_a, p32, p16, r1, qme, qs2, r2 = wk(t)
            n1, n2, s1, s2, sd = geo[t]
            rcopy(t, 1, one(r1, 0), one(r1, 0), sib).wait_recv()
            rcopy(t, 2, one(r1, 1), one(r1, 1), sib).wait_recv()
            _pipe(lambda a, b: (a + b.astype(F32),), [one(p32, s2), one(r1, 1)], [qs2], BIG_TR[n])
            start(rcopy(t, 3, qs2, r2, (*n2, cc)))
            _pipe(lambda a, b: (a + b.astype(F32),), [one(p32, me), one(r1, 0)], [qme], BIG_TR[n])
        for t, n in enumerate(BIG):
            recv_a, p32, p16, r1, qme, qs2, r2 = wk(t)
            rcopy(t, 3, r2, r2, sib).wait_recv()
            mine = fin[t].at[pl.ds(cc * hrs[t], hrs[t]), :]
            _pipe(lambda a, b: (a + b.astype(F32),), [qme.at[0], r2.at[0]], [mine], BIG_TR[n])
            start(rcopy(t, 4, mine, mine, sib))
        for t in range(nt):
            other = fin[t].at[pl.ds((1 - cc) * hrs[t], hrs[t]), :]
            rcopy(t, 4, other, other, sib).wait_recv()
        for cp in started:
            cp.wait_send()

    outs = [jax.ShapeDtypeStruct(BIG_SHAPE[n], F32) for n in BIG]
    for n in BIG:
        r, c = BIG_SHAPE[n]
        hr = r // 2
        outs += [jax.ShapeDtypeStruct((4, hr, c), F32), jax.ShapeDtypeStruct((4, hr, c), F32),
                 jax.ShapeDtypeStruct((4, hr, c), BF16), jax.ShapeDtypeStruct((2, hr, c), BF16),
                 jax.ShapeDtypeStruct((1, hr, c), F32), jax.ShapeDtypeStruct((1, hr, c), BF16),
                 jax.ShapeDtypeStruct((1, hr, c), BF16)]
    res = pl.pallas_call(
        body, name="reduce_big", in_specs=[ANY] * nt, out_specs=[ANY] * len(outs), out_shape=outs,
        scratch_shapes=[pltpu.SemaphoreType.DMA((5 * nt,)), pltpu.SemaphoreType.DMA((5 * nt,))],
        compiler_params=pltpu.CompilerParams(vmem_limit_bytes=VMEM_LIMIT),
    )(*grads)
    return res[:nt]


WHOLE_X_FIRST = dict(w_ssd_o=True, w_out=False, w_attn_o=False)


def _quarters(names):
    out = []
    for i, n in enumerate(names):
        if n in WHOLE_X_FIRST:
            h = BIG_SHAPE[n][0] // 2
            out.append((i, WHOLE_X_FIRST[n], 0, h, 128))
        else:
            q = BIG_SHAPE[n][0] // 4
            tr = 128 if q % 128 == 0 else q
            out += [(i, True, 0, q, tr), (i, False, q, q, tr)]
    return out


class _GatherJob:
    def __init__(self, names, shards, at=None):
        self.names = names
        self.at = at
        self.inputs = list(shards)
        self.out_shapes = [jax.ShapeDtypeStruct((N_CHIPS, *BIG_SHAPE[n]), BF16) for n in names]
        self.ent = _quarters(names)
        self.scratch = [pltpu.SemaphoreType.DMA((6 * len(self.ent),)), pltpu.SemaphoreType.DMA((6 * len(self.ent),))]

    def phases(self, sh, out, scr):
        send_sems, recv_sems = scr
        names, ent = self.names, self.ent
        x, y, cc = lax.axis_index("x"), lax.axis_index("y"), lax.axis_index("c")
        me = 2 * x + y
        sib = (x, y, 1 - cc)
        geo = [_neighbours(x, y, e[1]) for e in ent]
        started = []

        def copy(i, k, slot, pc, to):
            arr, _, roff, rows, _ = ent[i]
            hr = BIG_SHAPE[names[arr]][0] // 2
            ref = out[arr].at[slot, pl.ds(pc * hr + roff, rows), :]
            return pltpu.make_async_remote_copy(src_ref=ref, dst_ref=ref, send_sem=send_sems.at[6 * i + k],
                                                recv_sem=recv_sems.at[6 * i + k], device_id=to, device_id_type=MESH)

        def start(*a):
            copy(*a).start()
            started.append(a)

        def p0():
            for t, n in enumerate(names):
                _pipe(lambda v: (v,), [sh[t]], [out[t].at[me]], BIG_TR[n])
            for i in range(len(ent)):
                n1, n2, _, _, _ = geo[i]
                start(i, 0, me, cc, (*n1, cc))
                start(i, 1, me, cc, (*n2, cc))

        def p1():
            for i in range(len(ent)):
                n1, n2, s1, s2, sd = geo[i]
                copy(i, 0, s1, cc, sib).wait_recv()
                start(i, 2, s1, cc, (*n2, cc))
                start(i, 3, s1, cc, sib)
                copy(i, 1, s2, cc, sib).wait_recv()
                start(i, 4, s2, cc, sib)

        def p2():
            for i in range(len(ent)):
                sd = geo[i][4]
                copy(i, 2, sd, cc, sib).wait_recv()
                start(i, 5, sd, cc, sib)

        def p3():
            for i in range(len(ent)):
                _, _, s1, s2, sd = geo[i]
                copy(i, 3, s1, 1 - cc, sib).wait_recv()
                copy(i, 4, s2, 1 - cc, sib).wait_recv()
                copy(i, 5, sd, 1 - cc, sib).wait_recv()
            for a in started:
                copy(*a).wait_send()

        return [p0, p1, p2, p3]


class _ReduceJob:
    NW = 7

    def __init__(self, names, grads, at=None):
        self.names = names
        self.at = at
        self.inputs = list(grads)
        self.ent = _quarters(names)
        self.out_shapes = [jax.ShapeDtypeStruct(BIG_SHAPE[n], F32) for n in names]
        for arr, _, _, rows, _ in self.ent:
            c = BIG_SHAPE[names[arr]][1]
            self.out_shapes += [jax.ShapeDtypeStruct((4, rows, c), F32), jax.ShapeDtypeStruct((4, rows, c), F32),
                                jax.ShapeDtypeStruct((4, rows, c), BF16), jax.ShapeDtypeStruct((2, rows, c), BF16),
                                jax.ShapeDtypeStruct((1, rows, c), F32), jax.ShapeDtypeStruct((1, rows, c), BF16),
                                jax.ShapeDtypeStruct((1, rows, c), BF16)]
        self.scratch = [pltpu.SemaphoreType.DMA((8 * len(self.ent),)), pltpu.SemaphoreType.DMA((8 * len(self.ent),))]

    def phases(self, g, outs, scr):
        send_sems, recv_sems = scr
        names, ent, nw = self.names, self.ent, self.NW
        nt = len(names)
        fin, work = outs[:nt], outs[nt:]
        x, y, cc = lax.axis_index("x"), lax.axis_index("y"), lax.axis_index("c")
        me = 2 * x + y
        sib = (x, y, 1 - cc)
        geo = [_neighbours(x, y, e[1]) for e in ent]
        started = []
        wk = lambda i: work[nw * i:nw * (i + 1)]
        one = lambda ref, slot: ref.at[pl.ds(slot, 1)]

        def rows_of(i, pc):
            arr, _, roff, rows, _ = ent[i]
            return pl.ds(pc * (BIG_SHAPE[names[arr]][0] // 2) + roff, rows)

        def rcopy(i, k, src, dst, to):
            return pltpu.make_async_remote_copy(src_ref=src, dst_ref=dst, send_sem=send_sems.at[8 * i + k],
                                                recv_sem=recv_sems.at[8 * i + k], device_id=to, device_id_type=MESH)

        def start(make):
            make().start()
            started.append(make)

        def pair(i, q, slot, pc):
            return rcopy(i, q, g[ent[i][0]].at[pl.ds(slot, 1), rows_of(i, pc), :], one(wk(i)[0], slot), sib)

        def p0():
            for i in range(len(ent)):
                _, _, s1, s2, sd = geo[i]
                for q, slot in enumerate((s1, sd, s2, me)):
                    start(lambda i=i, q=q, slot=slot: pair(i, q, slot, 1 - cc))

        def p1():
            for i, e in enumerate(ent):
                recv_a, _, p16, _ = wk(i)[:4]
                n1, n2, s1, s2, sd = geo[i]
                pair(i, 0, s1, cc).wait_recv()
                pair(i, 1, sd, cc).wait_recv()
                _pipe(lambda a, b: (a + b,), [g[e[0]].at[:, rows_of(i, cc), :], recv_a], [p16], e[4], slots=(s1, sd))
                start(lambda i=i, s1=s1, n1=n1: rcopy(i, 4, one(wk(i)[2], s1), one(wk(i)[3], 0), (*n1, cc)))
                start(lambda i=i, sd=sd, n1=n1: rcopy(i, 5, one(wk(i)[2], sd), one(wk(i)[3], 1), (*n1, cc)))
            for i, e in enumerate(ent):
                recv_a, p32 = wk(i)[:2]
                _, _, s1, s2, sd = geo[i]
                pair(i, 2, s2, cc).wait_recv()
                pair(i, 3, me, cc).wait_recv()
                _pipe(lambda a, b: (a + b,), [g[e[0]].at[:, rows_of(i, cc), :], recv_a], [p32], e[4], slots=(s2, me))

        def p2():
            for i, e in enumerate(ent):
                _, p32, _, r1, qme, qs2, r2 = wk(i)
                n1, n2, s1, s2, sd = geo[i]
                rcopy(i, 4, one(r1, 0), one(r1, 0), sib).wait_recv()
                rcopy(i, 5, one(r1, 1), one(r1, 1), sib).wait_recv()
                _pipe(lambda a, b, c, d: (a + b.astype(F32), c + d.astype(F32)),
                      [one(p32, s2), one(r1, 1), one(p32, me), one(r1, 0)], [qs2, qme], e[4])
                start(lambda i=i, n2=n2: rcopy(i, 6, wk(i)[5], wk(i)[6], (*n2, cc)))

        def p3():
            for i, e in enumerate(ent):
                qme, r2 = wk(i)[4], wk(i)[6]
                rcopy(i, 6, r2, r2, sib).wait_recv()
                mine = fin[e[0]].at[rows_of(i, cc), :]
                _pipe(lambda a, b: (a + b.astype(F32),), [qme.at[0], r2.at[0]], [mine], e[4])
                start(lambda i=i, e=e: rcopy(i, 7, fin[e[0]].at[rows_of(i, cc), :], fin[e[0]].at[rows_of(i, cc), :], sib))

        def p4():
            for i, e in enumerate(ent):
                other = fin[e[0]].at[rows_of(i, 1 - cc), :]
                rcopy(i, 7, other, other, sib).wait_recv()
            for make in started:
                make().wait_send()

        return [p0, p1, p2, p3, p4]


class _AdamJob:
    def __init__(self, names, ws, gs, ms, vs, groups):
        self.names, self.groups = names, groups
        self.inputs = [a for quad in zip(ws, gs, ms, vs) for a in quad]
        self.out_shapes = [jax.ShapeDtypeStruct(w.shape, F32) for w in ws for _ in range(4)]

    def work(self, ins, outs):
        def one(t):
            w, g, m, v = ins[4 * t:4 * t + 4]
            r = w.shape[1]
            tr = 128 if r % 128 == 0 else r // 4
            _pipe(lambda a, b, c, d: (*_adamw_math(a, b, c, d), b), [w.at[0], g, m.at[0], v.at[0]],
                  [o.at[0] for o in outs[4 * t:4 * t + 4]], tr, depth=2)

        def group(grp):
            def run():
                for n in grp:
                    one(self.names.index(n))
            return run

        return [group(grp) for grp in self.groups]


class _Interleaved:
    def __init__(self, job, work, at):
        self.job, self.wk, self.at = job, work, at
        self.inputs = job.inputs + work.inputs
        self.out_shapes = list(job.out_shapes) + list(work.out_shapes)
        self.scratch = job.scratch

    def phases(self, ins, outs, scr):
        nj, no = len(self.job.inputs), len(self.job.out_shapes)
        base = self.job.phases(ins[:nj], outs[:no], scr)
        work = self.wk.work(ins[nj:], outs[no:])
        mixed = []
        for k, ph in enumerate(base):
            mixed.append(ph)
            if k < len(work):
                mixed.append(work[k])
        return mixed


def _run_job(job, name):
    ni, no = len(job.inputs), len(job.out_shapes)

    def body(*refs):
        for ph in job.phases(refs[:ni], refs[ni:ni + no], refs[ni + no:]):
            ph()

    return pl.pallas_call(
        body, name=name, in_specs=[ANY] * ni, out_specs=[ANY] * no, out_shape=job.out_shapes, scratch_shapes=job.scratch,
        compiler_params=pltpu.CompilerParams(vmem_limit_bytes=VMEM_LIMIT),
    )(*job.inputs)


def _hosted(body, *, name, grid, in_specs, out_specs, out_shape, scratch_shapes, args, sem, side=None):
    if side is None:
        return pl.pallas_call(body, name=name, grid=grid, in_specs=in_specs, out_specs=out_specs, out_shape=out_shape,
                              scratch_shapes=scratch_shapes, compiler_params=_cp(sem))(*args), None
    job = side
    ni, no, ns = len(in_specs), len(out_specs), len(scratch_shapes)
    ji, jo = len(job.inputs), len(job.out_shapes)
    n_steps = 1
    for extent in grid:
        n_steps *= extent

    def wrapped(*refs):
        own_in, refs = refs[:ni], refs[ni:]
        job_in, refs = refs[:ji], refs[ji:]
        own_out, refs = refs[:no], refs[no:]
        job_out, refs = refs[:jo], refs[jo:]
        own_scr, job_scr = refs[:ns], refs[ns:]
        step = 0
        for d, extent in enumerate(grid):
            step = step * extent + pl.program_id(d)
        phases = job.phases(job_in, job_out, job_scr)
        steps = [min(int(f * n_steps), n_steps - 1) for f in job.at] + [n_steps - 1]
        assert len(steps) == len(phases) and steps == sorted(steps)
        for at, ph in zip(steps, phases):
            pl.when(step == at)(ph)
        body(*own_in, *own_out, *own_scr)

    res = pl.pallas_call(
        wrapped, name=name, grid=grid, in_specs=list(in_specs) + [ANY] * ji, out_specs=list(out_specs) + [ANY] * jo,
        out_shape=list(out_shape) + list(job.out_shapes), scratch_shapes=list(scratch_shapes) + list(job.scratch),
        compiler_params=_cp(("arbitrary",) * len(grid)),
    )(*args, *job.inputs)
    return res[:no], res[no:]


def _proj_dw(xnt, dproj_sh, *, tm=512, tk=2048):
    d, s = xnt.shape
    tk = _tile(s, tk)
    nk = s // tk

    def body(a_ref, b_ref, o_ref, acc):
        def finish(r):
            o_ref[0] = r

        _accumulate(acc, _dot(a_ref[...], b_ref[...]), pl.program_id(2), nk, finish)

    return pl.pallas_call(
        body, name="proj_dw", grid=(N_CHIPS, d // tm, nk),
        in_specs=[pl.BlockSpec((tm, tk), lambda j, i, q: (i, q)), pl.BlockSpec((tk, W_IN_PAD), lambda j, i, q: (q, j))],
        out_specs=pl.BlockSpec((1, tm, W_IN_PAD), lambda j, i, q: (j, i, 0)),
        out_shape=jax.ShapeDtypeStruct((N_CHIPS, d, W_IN_PAD), F32), scratch_shapes=[pltpu.VMEM((tm, W_IN_PAD), F32)],
        compiler_params=_cp(("parallel", "parallel", "arbitrary")),
    )(xnt, dproj_sh)


def _proj_dx(dproj_sh, w_sh, *, tm=1024, side=None):
    s = dproj_sh.shape[0]
    d = w_sh.shape[1]
    tm = _tile(s, tm)

    def body(a_ref, b_ref, o_ref, acc):
        kk = pl.program_id(1)
        part = _dot_nt(a_ref[...], b_ref[0])

        @pl.when(kk == 0)
        def _():
            acc[...] = part

        @pl.when(kk > 0)
        def _():
            acc[...] += part

        @pl.when(kk == N_CHIPS - 1)
        def _():
            o_ref[...] = acc[...]

    own, extra = _hosted(
        body, name="proj_dx", grid=(s // tm, N_CHIPS),
        in_specs=[pl.BlockSpec((tm, W_IN_PAD), lambda i, q: (i, q)), pl.BlockSpec((1, d, W_IN_PAD), lambda i, q: (q, 0, 0))],
        out_specs=[pl.BlockSpec((tm, d), lambda i, q: (i, 0))],
        out_shape=[jax.ShapeDtypeStruct((s, d), F32)], scratch_shapes=[pltpu.VMEM((tm, d), F32)],
        args=(dproj_sh, w_sh), sem=("parallel", "arbitrary"), side=side)
    return own[0] if side is None else (own[0], extra)


def _up_dx(dup, w_sh, *, tm=1024):
    s = dup.shape[1]
    d, wsh = w_sh.shape[1:]
    tm = _tile(s, tm)

    def body(a_ref, b_ref, o_ref, acc):
        kk = pl.program_id(1)
        part = _dot_nt(a_ref[0], b_ref[0])

        @pl.when(kk == 0)
        def _():
            acc[...] = part

        @pl.when(kk > 0)
        def _():
            acc[...] += part

        @pl.when(kk == N_CHIPS - 1)
        def _():
            o_ref[...] = acc[...]

    return pl.pallas_call(
        body, name="up_dx", grid=(s // tm, N_CHIPS),
        in_specs=[pl.BlockSpec((1, tm, wsh), lambda i, q: (q >> 1, i, q & 1)), pl.BlockSpec((1, d, wsh), lambda i, q: (q, 0, 0))],
        out_specs=pl.BlockSpec((tm, d), lambda i, q: (i, 0)),
        out_shape=jax.ShapeDtypeStruct((s, d), F32), scratch_shapes=[pltpu.VMEM((tm, d), F32)],
        compiler_params=_cp(("parallel", "arbitrary")),
    )(dup, w_sh)


def _up_dw(hnt, dup, *, tk=2048):
    d, s = hnt.shape
    wsh = 2 * D_FF // N_CHIPS
    tk = _tile(s, tk)
    nk = s // tk

    def body(a_ref, b_ref, o_ref, acc):
        def finish(r):
            o_ref[0] = r

        _accumulate(acc, _dot(a_ref[...], b_ref[0]), pl.program_id(1), nk, finish)

    return pl.pallas_call(
        body, name="up_dw", grid=(N_CHIPS, nk),
        in_specs=[pl.BlockSpec((d, tk), lambda j, q: (0, q)), pl.BlockSpec((1, tk, wsh), lambda j, q: (j >> 1, q, j & 1))],
        out_specs=pl.BlockSpec((1, d, wsh), lambda j, q: (j, 0, 0)),
        out_shape=jax.ShapeDtypeStruct((N_CHIPS, d, wsh), F32), scratch_shapes=[pltpu.VMEM((d, wsh), F32)],
        compiler_params=_cp(("parallel", "arbitrary")),
    )(hnt, dup)


BIG_ROWS =(IN_DIM // 4, Q_DIM // 4, D_INNER // 4, D_MODEL // 4, 2 * D_FF // 4, D_FF // 4)
PACK_ROWS = 5376


def _pack_shards(parts):
    rows = [p.reshape(-1, D_MODEL) for p in parts]
    pad = PACK_ROWS - sum(BIG_ROWS)
    return jnp.concatenate(rows + [jnp.zeros((pad, D_MODEL), rows[0].dtype)], axis=0)


def _unpack_shards(buf):
    out, off = [], 0
    for n in BIG_ROWS:
        out.append(buf[off:off + n])
        off += n
    return out


def _assemble(srcs, col_map, *, name, tr=256):
    arrays, lead = [], []
    for src in srcs:
        arr, j = src if isinstance(src, tuple) else (src, None)
        if not any(arr is a for a in arrays):
            arrays.append(arr)
        lead.append(([i for i, a in enumerate(arrays) if a is arr][0], j))
    rows = arrays[0].shape[-2]
    tr = _tile(rows, tr)
    out_w = len(col_map)
    tiles = []
    for t in range(out_w // 128):
        runs = []
        for lane in range(128):
            ent = col_map[t * 128 + lane]
            key = None if ent is None else (ent[0], ent[1] // 128, (lane - ent[1]) % 128)
            if runs and runs[-1][0] == key:
                runs[-1][2] = lane + 1
            else:
                runs.append([key, lane, lane + 1])
        tiles.append(runs)

    def body(*refs):
        o_ref = refs[-1]
        lane = lax.broadcasted_iota(jnp.int32, (tr, 128), 1)
        for t, runs in enumerate(tiles):
            acc = jnp.zeros((tr, 128), F32)
            for key, a, b in runs:
                if key is None:
                    continue
                sid, ct, shift = key
                ai, j = lead[sid]
                cols = slice(ct * 128, (ct + 1) * 128)
                piece = (refs[ai][:, cols] if j is None else refs[ai][j, :, cols]).astype(F32)
                if shift:
                    piece = pltpu.roll(piece, shift, 1)
                acc = piece if (a, b) == (0, 128) else jnp.where((lane >= a) & (lane < b), piece, acc)
            o_ref[:, t * 128:(t + 1) * 128] = acc.astype(BF16)

    specs = [pl.BlockSpec((tr, a.shape[1]), lambda i: (i, 0)) if a.ndim == 2
             else pl.BlockSpec((a.shape[0], tr, a.shape[2]), lambda i: (0, i, 0)) for a in arrays]
    return pl.pallas_call(
        body, name=name, grid=(rows // tr,), in_specs=specs, out_specs=pl.BlockSpec((tr, out_w), lambda i: (i, 0)),
        out_shape=jax.ShapeDtypeStruct((rows, out_w), BF16), compiler_params=_cp(("parallel",)),
    )(*arrays)


def _permute_cols_in(w):
    pad = jnp.zeros((w.shape[0], PW - IN_DIM), w.dtype)
    return jnp.concatenate([w[:, :6656], w[:, 6688:], w[:, 6656:6688], pad], axis=1)


def _unpermute_cols_in(g):
    return jnp.concatenate([g[:, :6656], g[:, O_DT:O_DT + 32], g[:, 6656:O_DT]], axis=1)


SMALL = ("norm1_w", "b_gate", "attn_sinks", "ssd_conv_b", "dt_bias", "a_log", "d_skip", "ssd_norm_w", "norm2_w",
         "ffn_conv_b", "final_norm_w", "ssd_conv_w", "ffn_conv_w")


def _pad128(v):
    v = v.reshape(-1)
    return jnp.pad(v, (0, (-v.shape[0]) % 128))


def _pack_small(parts):
    flat = jnp.concatenate([_pad128(p) for p in parts])
    flat = jnp.pad(flat, (0, (-flat.shape[0]) % 1024))
    return flat.reshape(-1, 128)


def _unpack_small(buf, shapes):
    flat, out, off = buf.reshape(-1), [], 0
    for shp in shapes:
        n = 1
        for q in shp:
            n *= q
        out.append(flat[off:off + n].reshape(shp))
        off += n + (-n) % 128
    return out


def _vec128(v):
    return jnp.pad(v.reshape(1, -1), ((0, 0), (0, 128 - v.shape[-1])))


def kernel(x, norm1_w, w_in, b_gate, attn_sinks, w_attn_o, ssd_conv_w, ssd_conv_b, dt_bias, a_log, d_skip, ssd_norm_w, w_ssd_o, w_out, norm2_w, w_up, ffn_conv_w, ffn_conv_b, w_down, final_norm_w, loss_target, m_norm1_w, m_w_in, m_b_gate, m_attn_sinks, m_w_attn_o, m_ssd_conv_w, m_ssd_conv_b, m_dt_bias, m_a_log, m_d_skip, m_ssd_norm_w, m_w_ssd_o, m_w_out, m_norm2_w, m_w_up, m_ffn_conv_w, m_ffn_conv_b, m_w_down, m_final_norm_w, v_norm1_w, v_w_in, v_b_gate, v_attn_sinks, v_w_attn_o, v_ssd_conv_w, v_ssd_conv_b, v_dt_bias, v_a_log, v_d_skip, v_ssd_norm_w, v_w_ssd_o, v_w_out, v_norm2_w, v_w_up, v_ffn_conv_w, v_ffn_conv_b, v_w_down, v_final_norm_w):
    ix, iy, ic = lax.axis_index("x"), lax.axis_index("y"), lax.axis_index("c")
    chip = 2 * ix + iy
    x2 = x[0]
    tgt = loss_target[0]
    s = x2.shape[0]

    wsh = IN_DIM // N_CHIPS
    big_shards = dict(w_in=jnp.pad(w_in[0], ((0, 0), (0, W_IN_PAD - wsh))), w_attn_o=w_attn_o[0], w_ssd_o=w_ssd_o[0],
                      w_out=w_out[0], w_up=w_up[0], w_down=w_down[0])
    gathered = {}
    (gathered["w_in"],) = _run_job(_GatherJob(("w_in",), [big_shards["w_in"]]), "gather_w_in")
    early = ("w_attn_o", "w_ssd_o", "w_out")
    gather_early = _GatherJob(early, [big_shards[n] for n in early], at=(0.0, 0.5, 0.8))
    gather_up = _GatherJob(("w_up",), [big_shards["w_up"]], at=(0.0, 0.55, 0.85))
    gather_down = _GatherJob(("w_down",), [big_shards["w_down"]], at=(0.0, 0.5, 0.8))
    gw = gathered["w_in"]
    perm = list(range(O_GA)) + list(range(O_GA + N_SSD_HEADS, IN_DIM)) + list(range(O_GA, O_GA + N_SSD_HEADS))
    w_in_p = _assemble([(gw, j) for j in range(N_CHIPS)], [divmod(o, wsh) for o in perm] + [None] * (PW - IN_DIM),
                       name="w_in_assemble")
    small_sh = _pack_small([ssd_conv_w[0], ffn_conv_w[0]])
    small_all = _all_gather_small(small_sh)
    sc_parts = [_unpack_small(small_all[j], [(4, XBC_DIM // 4), (3, 2 * D_FF // 4)]) for j in range(N_CHIPS)]
    ssd_cw = jnp.concatenate([p[0] for p in sc_parts], axis=1)
    ffn_cw = jnp.concatenate([p[1] for p in sc_parts], axis=1)

    sinks128 = _vec128(attn_sinks)
    dtb128, alog128, dskip128 = _vec128(dt_bias), _vec128(a_log), _vec128(d_skip)

    xn, xnt = _rms_fwd(x2, norm1_w, name="norm1_fwd", with_t=True)
    proj, got = _mm(xn, w_in_p, name="proj_fwd", tn=1280, side=gather_early)
    gathered.update(zip(early, got))
    qkvt = _mm(w_in_p[:, :O_Z], xnt, name="qkv_fwd", ta=True)
    attn_pre, (gathered["w_up"],) = _attn_fwd(qkvt, sinks128, side=gather_up)
    xbc = _ssd_conv_fwd(proj, ssd_cw, ssd_conv_b)
    (y_ssd, hprev), (gathered["w_down"],) = _ssd_fwd(xbc, proj, dtb128, alog128, dskip128, side=gather_down)
    full = {n: gathered[n].reshape(-1, D_MODEL) for n in ("w_attn_o", "w_ssd_o", "w_out", "w_down")}
    full["w_up"] = gathered["w_up"]
    attn = _mm(attn_pre, full["w_attn_o"], name="attn_o_fwd", ta=True)
    yn = _gate_norm_fwd(y_ssd, proj, ssd_norm_w)
    ssd_out = _mm(yn, full["w_ssd_o"], name="ssd_o_fwd")
    merged = _merge_fwd(proj, b_gate, attn, ssd_out)
    h1 = _mm(merged, full["w_out"], name="out_fwd", resid=x2)
    hn, hnt = _rms_fwd(h1, norm2_w, name="norm2_fwd", with_t=True)
    up = _mm(hn, full["w_up"], name="up_fwd")
    act = _ffn_act_fwd(up, ffn_cw, ffn_conv_b)
    h2 = _mm(act, full["w_down"], name="down_fwd", resid=h1, tk=1408)

    dh2, loss_blk, g_final = _loss_bwd(h2, tgt, final_norm_w.reshape(1, -1))
    dact = _mm(dh2, full["w_down"], name="down_dx", tb=True, tn=1408)
    g_down = _mm(act, dh2, name="down_dw", ta=True, tm=1408)
    dup, g_ffn_cw, g_ffn_cb = _ffn_act_bwd(dact, up, ffn_cw, ffn_conv_b)
    dhn = _up_dx(dup, full["w_up"])
    g_up = _up_dw(hnt, dup)
    dh1, g_norm2 = _rms_bwd(dhn, h1, norm2_w, dh2, name="norm2_bwd")
    dmerged = _mm(dh1, full["w_out"], name="out_dx", tb=True)
    g_out = _mm(merged, dh1, name="out_dw", ta=True)
    dattn, dssd_out, dga, dgs, g_ba, g_bs = _merge_bwd(dmerged, proj, b_gate, attn, ssd_out)
    dyn = _mm(dssd_out, full["w_ssd_o"], name="ssd_o_dx", tb=True)
    g_ssd_o = _mm(yn, dssd_out, name="ssd_o_dw", ta=True)
    dy_ssd, dz, g_ssd_norm = _gate_norm_bwd(dyn, y_ssd, proj, ssd_norm_w)
    slot = lambda g: g.reshape(N_CHIPS, -1, D_MODEL)
    big_grads = {}
    red = ("w_down", "w_up")
    (dxbc, ddt, dvec), got = _ssd_bwd(xbc, proj, dtb128, alog128, dskip128, hprev, dy_ssd,
                                      side=_ReduceJob(red, [slot(g_down), g_up], at=(0.0, 0.2, 0.7, 0.95)))
    big_grads.update(zip(red, got))
    dxbc_raw, g_ssd_cw, g_ssd_cb = _ssd_conv_bwd(dxbc, proj, ssd_cw, ssd_conv_b)
    dattn_pre = _mm(full["w_attn_o"], dattn, name="attn_o_dx", tb=True)
    g_attn_o = _mm(attn_pre, dattn, name="attn_o_dw")
    red = ("w_out", "w_ssd_o", "w_attn_o")
    (dq, dk, dv, dsk), got = _attn_bwd(qkvt, sinks128, attn_pre, dattn_pre,
                                       side=_ReduceJob(red, [slot(g_out), slot(g_ssd_o), slot(g_attn_o)],
                                                       at=(0.0, 0.2, 0.5, 0.7)))
    big_grads.update(zip(red, got))
    pieces = [(dq.T, Q_DIM), (dk.T, KV_DIM), (dv.T, KV_DIM), (dz, D_INNER), (dxbc_raw, XBC_DIM), (ddt, N_SSD_HEADS),
              (dga, D_MODEL), (dgs, D_MODEL)]
    orig = [(i, c) for i, (_, w) in enumerate(pieces) for c in range(w)]
    dproj_sh = _assemble([p for p, _ in pieces],
                         [orig[j * wsh + c] if c < wsh else None for j in range(N_CHIPS) for c in range(W_IN_PAD)],
                         name="dproj_assemble")
    g_in = _proj_dw(xnt, dproj_sh)
    dxn, got = _proj_dx(dproj_sh, gathered["w_in"], side=_ReduceJob(("w_in",), [g_in], at=(0.0, 0.15, 0.75, 0.95)))
    big_grads["w_in"] = got[0]
    dx, g_norm1 = _rms_bwd(dxn, x2, norm1_w, dh1, name="norm1_bwd")


    small_g = dict(
        norm1_w=g_norm1, b_gate=jnp.concatenate([g_ba, g_bs], axis=1), attn_sinks=dsk[0:1, :16], ssd_conv_b=g_ssd_cb,
        dt_bias=dvec[0:1, :32], a_log=dvec[1:2, :32], d_skip=dvec[2:3, :32], ssd_norm_w=g_ssd_norm, norm2_w=g_norm2,
        ffn_conv_b=jnp.concatenate([g_ffn_cb[0], g_ffn_cb[1]], axis=1), final_norm_w=g_final, ssd_conv_w=g_ssd_cw,
        ffn_conv_w=jnp.concatenate([g_ffn_cw[0], g_ffn_cw[1]], axis=1))
    small_buf = _pack_small([small_g[n] for n in SMALL] + [loss_blk])
    small_sum = _all_reduce_small(small_buf)
    small_shapes = [(1, D_MODEL), (1, 2 * D_MODEL), (1, 16), (1, XBC_DIM), (1, 32), (1, 32), (1, 32), (1, D_INNER),
                    (1, D_MODEL), (1, 2 * D_FF), (D_MODEL,), (4, XBC_DIM), (3, 2 * D_FF), (1, 128)]
    small_list = _unpack_small(small_sum, small_shapes)
    loss = small_list[-1][0, 0]
    grads = dict(zip(SMALL, small_list[:-1]))
    grads["ssd_conv_w"] = lax.dynamic_slice_in_dim(grads["ssd_conv_w"], chip * (XBC_DIM // 4), XBC_DIM // 4, axis=1)
    grads["ffn_conv_w"] = lax.dynamic_slice_in_dim(grads["ffn_conv_w"], chip * (2 * D_FF // 4), 2 * D_FF // 4, axis=1)
    grads.update(big_grads)

    weights = dict(norm1_w=norm1_w, w_in=w_in, b_gate=b_gate, attn_sinks=attn_sinks, w_attn_o=w_attn_o, ssd_conv_w=ssd_conv_w,
                   ssd_conv_b=ssd_conv_b, dt_bias=dt_bias, a_log=a_log, d_skip=d_skip, ssd_norm_w=ssd_norm_w, w_ssd_o=w_ssd_o,
                   w_out=w_out, norm2_w=norm2_w, w_up=w_up, ffn_conv_w=ffn_conv_w, ffn_conv_b=ffn_conv_b, w_down=w_down,
                   final_norm_w=final_norm_w)
    ms = dict(norm1_w=m_norm1_w, w_in=m_w_in, b_gate=m_b_gate, attn_sinks=m_attn_sinks, w_attn_o=m_w_attn_o,
              ssd_conv_w=m_ssd_conv_w, ssd_conv_b=m_ssd_conv_b, dt_bias=m_dt_bias, a_log=m_a_log, d_skip=m_d_skip,
              ssd_norm_w=m_ssd_norm_w, w_ssd_o=m_w_ssd_o, w_out=m_w_out, norm2_w=m_norm2_w, w_up=m_w_up,
              ffn_conv_w=m_ffn_conv_w, ffn_conv_b=m_ffn_conv_b, w_down=m_w_down, final_norm_w=m_final_norm_w)
    vs = dict(norm1_w=v_norm1_w, w_in=v_w_in, b_gate=v_b_gate, attn_sinks=v_attn_sinks, w_attn_o=v_w_attn_o,
              ssd_conv_w=v_ssd_conv_w, ssd_conv_b=v_ssd_conv_b, dt_bias=v_dt_bias, a_log=v_a_log, d_skip=v_d_skip,
              ssd_norm_w=v_ssd_norm_w, w_ssd_o=v_w_ssd_o, w_out=v_w_out, norm2_w=v_norm2_w, w_up=v_w_up,
              ffn_conv_w=v_ffn_conv_w, ffn_conv_b=v_ffn_conv_b, w_down=v_w_down, final_norm_w=v_final_norm_w)
    order = list(weights)
    deltas, new_m, new_v = {}, {}, {}
    for n in BIG:
        shp = weights[n].shape
        res = _adamw(weights[n][0], grads[n], ms[n][0], vs[n][0], name="adamw_" + n)
        deltas[n], new_m[n], new_v[n], grads[n] = (a.reshape(shp) for a in res)
    smalls = [n for n in order if n not in BIG]
    as2d = lambda a: a.reshape(-1, a.shape[-1])
    res = _adamw_many(*[[as2d(src[n][0] if src[n].ndim == 3 else src[n]) for n in smalls] for src in (weights, grads, ms, vs)])
    for i, n in enumerate(smalls):
        deltas[n], new_m[n], new_v[n] = (res[q * len(smalls) + i].reshape(weights[n].shape) for q in range(3))
    out_grads = [grads[n].reshape(weights[n].shape) for n in order]
    return (loss, dx[None], *out_grads, *[deltas[n] for n in order], *[new_m[n] for n in order], *[new_v[n] for n in order])
```

```python
import functools

import jax
import jax.numpy as jnp
from jax import lax
from jax.experimental import pallas as pl
from jax.experimental.pallas import tpu as pltpu

F32 = jnp.float32
BF16 = jnp.bfloat16
HI = lax.Precision.HIGHEST

D_MODEL = 1024
Q_DIM = 1024
KV_DIM = 256
D_INNER = 2048
BC_DIM = 512
XBC_DIM = 3072
N_SSD_HEADS = 32
D_FF = 2816
IN_DIM = 8736
BLK = 128
EPS = 1e-5
NEG = -1e30

O_Q, O_K, O_V, O_Z, O_X, O_GA, O_GS, O_DT = 0, 1024, 1280, 1536, 3584, 6656, 7680, 8704
PW = 8960

ADAM_LR, ADAM_B1, ADAM_B2, ADAM_EPS, ADAM_WD, ADAM_STEP = 0.001, 0.9, 0.999, 1e-08, 0.01, 10

VMEM_LIMIT = 52 * 1024 * 1024
MESH = pl.DeviceIdType.MESH


def _cp(sem=None):
    return pltpu.CompilerParams(dimension_semantics=sem, vmem_limit_bytes=VMEM_LIMIT)


def _dot(a, b, prec=None):
    return jnp.dot(a, b, preferred_element_type=F32, precision=prec)


def _dot_nt(a, b, prec=None):
    return lax.dot_general(a, b, (((1,), (1,)), ((), ())), preferred_element_type=F32, precision=prec)


def _dot_tn(a, b, prec=None):
    return lax.dot_general(a, b, (((0,), (0,)), ((), ())), preferred_element_type=F32, precision=prec)


def _sigmoid(x):
    return 0.5 * jnp.tanh(0.5 * x) + 0.5


def _tile(n, want):
    t = min(n, want)
    while n % t:
        t -= 128
    return t


def _accumulate(acc, part, kk, nk, finish):
    if nk == 1:
        finish(part)
        return

    @pl.when(kk == 0)
    def _():
        acc[...] = part

    @pl.when(kk > 0)
    def _():
        acc[...] += part

    @pl.when(kk == nk - 1)
    def _():
        finish(acc[...])


def _mm(a, b, *, name, ta=False, tb=False, out_dtype=F32, resid=None, tm=1024, tn=1024, tk=1024, side=None):
    m, k = (a.shape[1], a.shape[0]) if ta else a.shape
    slots = b.ndim == 3
    if slots:
        n = b.shape[1] if tb else b.shape[0] * b.shape[2]
        tn, tk = (tn, b.shape[2]) if tb else (b.shape[2], tk)
    else:
        n = b.shape[0] if tb else b.shape[1]
    tm, tn, tk = _tile(m, tm), _tile(n, tn), _tile(k, tk)
    nk = k // tk
    dn = (((0 if ta else 1,), (1 if tb else 0,)), ((), ()))

    def body(*refs):
        if resid is None:
            a_ref, b_ref, o_ref, acc = refs
        else:
            a_ref, b_ref, r_ref, o_ref, acc = refs
        kk = pl.program_id(2)
        bv = b_ref[0] if slots else b_ref[...]
        part = lax.dot_general(a_ref[...].astype(BF16), bv.astype(BF16), dn, preferred_element_type=F32)

        def finish(r):
            if resid is not None:
                r = r + r_ref[...]
            o_ref[...] = r.astype(out_dtype)

        _accumulate(acc, part, kk, nk, finish)

    a_spec = pl.BlockSpec((tk, tm), lambda i, j, q: (q, i)) if ta else pl.BlockSpec((tm, tk), lambda i, j, q: (i, q))
    if slots:
        b_spec = (pl.BlockSpec((1, tn, tk), lambda i, j, q: (q, j, 0)) if tb
                  else pl.BlockSpec((1, tk, tn), lambda i, j, q: (j, q, 0)))
    else:
        b_spec = pl.BlockSpec((tn, tk), lambda i, j, q: (j, q)) if tb else pl.BlockSpec((tk, tn), lambda i, j, q: (q, j))
    o_spec = pl.BlockSpec((tm, tn), lambda i, j, q: (i, j))
    ins, specs = [a, b], [a_spec, b_spec]
    if resid is not None:
        ins.append(resid)
        specs.append(o_spec)
    own, extra = _hosted(
        body, name=name, grid=(m // tm, n // tn, nk), in_specs=specs, out_specs=[o_spec],
        out_shape=[jax.ShapeDtypeStruct((m, n), out_dtype)], scratch_shapes=[pltpu.VMEM((tm, tn), F32)],
        args=ins, sem=("parallel", "parallel", "arbitrary"), side=side)
    return own[0] if side is None else (own[0], extra)


def _rms_fwd(x, w, *, name, tm=512, with_t=False):
    s, d = x.shape
    tm = _tile(s, tm)

    def body(x_ref, w_ref, o_ref, *t_ref):
        xv = x_ref[...]
        r = lax.rsqrt(jnp.mean(xv * xv, axis=-1, keepdims=True) + EPS)
        y = (xv * r) * w_ref[...]
        o_ref[...] = y.astype(BF16)
        if with_t:
            t_ref[0][...] = y.T.astype(BF16)

    row = pl.BlockSpec((tm, d), lambda i: (i, 0))
    res = pl.pallas_call(
        body, name=name, grid=(s // tm,), in_specs=[row, pl.BlockSpec((1, d), lambda i: (0, 0))],
        out_specs=[row] + [pl.BlockSpec((d, tm), lambda i: (0, i))] * with_t,
        out_shape=[jax.ShapeDtypeStruct((s, d), BF16)] + [jax.ShapeDtypeStruct((d, s), BF16)] * with_t,
        compiler_params=_cp(("parallel",)),
    )(x, w)
    return res if with_t else res[0]


def _rms_bwd(dy, x, w, resid, *, name, tm=512):
    s, d = x.shape
    tm = _tile(s, tm)

    def body(dy_ref, x_ref, w_ref, r_ref, dx_ref, dw_ref):
        i = pl.program_id(0)
        xv = x_ref[...]
        r = lax.rsqrt(jnp.mean(xv * xv, axis=-1, keepdims=True) + EPS)
        xh = xv * r
        dyv = dy_ref[...]
        g = dyv * w_ref[...]
        dx_ref[...] = r_ref[...] + r * (g - xh * jnp.mean(g * xh, axis=-1, keepdims=True))
        part = jnp.sum(dyv * xh, axis=0, keepdims=True)

        @pl.when(i == 0)
        def _():
            dw_ref[...] = part

        @pl.when(i > 0)
        def _():
            dw_ref[...] += part

    row = pl.BlockSpec((tm, d), lambda i: (i, 0))
    vec = pl.BlockSpec((1, d), lambda i: (0, 0))
    return pl.pallas_call(
        body, name=name, grid=(s // tm,), in_specs=[row, row, vec, row], out_specs=[row, vec],
        out_shape=[jax.ShapeDtypeStruct((s, d), F32), jax.ShapeDtypeStruct((1, d), F32)],
        compiler_params=_cp(("arbitrary",)),
    )(dy, x, w, resid)


def _loss_bwd(h2, tgt, wf, *, tm=512):
    s, d = h2.shape
    tm = _tile(s, tm)

    def body(h_ref, t_ref, w_ref, dh_ref, loss_ref, dw_ref):
        i = pl.program_id(0)
        hv = h_ref[...]
        r = lax.rsqrt(jnp.mean(hv * hv, axis=-1, keepdims=True) + EPS)
        xh = hv * r
        wv = w_ref[...]
        e = xh * wv - t_ref[...]
        lpart = 0.5 * jnp.sum(jnp.mean(e * e, axis=-1, keepdims=True), axis=0, keepdims=True)
        dout = e * (1.0 / d)
        g = dout * wv
        dh_ref[...] = r * (g - xh * jnp.mean(g * xh, axis=-1, keepdims=True))
        part = jnp.sum(dout * xh, axis=0, keepdims=True)
        lrow = jnp.broadcast_to(lpart, (1, 128))

        @pl.when(i == 0)
        def _():
            dw_ref[...] = part
            loss_ref[...] = lrow

        @pl.when(i > 0)
        def _():
            dw_ref[...] += part
            loss_ref[...] += lrow

    row = pl.BlockSpec((tm, d), lambda i: (i, 0))
    vec = pl.BlockSpec((1, d), lambda i: (0, 0))
    return pl.pallas_call(
        body, name="loss_bwd", grid=(s // tm,), in_specs=[row, row, vec],
        out_specs=[row, pl.BlockSpec((1, 128), lambda i: (0, 0)), vec],
        out_shape=[jax.ShapeDtypeStruct((s, d), F32), jax.ShapeDtypeStruct((1, 128), F32),
                   jax.ShapeDtypeStruct((1, d), F32)],
        compiler_params=_cp(("arbitrary",)),
    )(h2, tgt, wf)


def _attn_mask(n):
    si = lax.broadcasted_iota(jnp.int32, (2 * BLK, 4 * BLK), 0)
    qi = lax.broadcasted_iota(jnp.int32, (2 * BLK, 4 * BLK), 1) & (BLK - 1)
    dist = BLK + qi - si
    kpos = n * BLK - BLK + si
    return (dist >= 0) & (dist < BLK) & (kpos >= 0)


def _attn_probs(q_ref, kc_ref, kp_ref, sk_ref, kvh, valid):
    rows = slice(kvh * 64, (kvh + 1) * 64)
    kt = jnp.concatenate([kp_ref[rows, :], kc_ref[rows, :]], axis=1).astype(BF16)
    qt = jnp.concatenate([q_ref[(kvh * 4 + g) * 64:(kvh * 4 + g + 1) * 64, :] for g in range(4)], axis=1).astype(BF16)
    s = _dot_tn(kt, qt) * 0.125
    s = jnp.where(valid, s, NEG)
    head = lax.broadcasted_iota(jnp.int32, (1, 4 * BLK), 1) >> 7
    sink = jnp.zeros((1, 4 * BLK), F32)
    for g in range(4):
        sink = jnp.where(head == g, sk_ref[0:1, kvh * 4 + g:kvh * 4 + g + 1], sink)
    m = jnp.maximum(jnp.max(s, axis=0, keepdims=True), sink)
    p = jnp.where(valid, jnp.exp(s - m), 0.0)
    es = jnp.exp(sink - m)
    inv = 1.0 / (jnp.sum(p, axis=0, keepdims=True) + es)
    return qt, kt, p * inv, es * inv


def _attn_in_specs(cur, prev):
    return [pl.BlockSpec((Q_DIM, BLK), lambda n: (0, cur(n))),
            pl.BlockSpec((KV_DIM, BLK), lambda n: (O_K // KV_DIM, cur(n))),
            pl.BlockSpec((KV_DIM, BLK), lambda n: (O_K // KV_DIM, prev(n))),
            pl.BlockSpec((KV_DIM, BLK), lambda n: (O_V // KV_DIM, cur(n))),
            pl.BlockSpec((KV_DIM, BLK), lambda n: (O_V // KV_DIM, prev(n))),
            pl.BlockSpec((1, 128), lambda n: (0, 0))]


def _attn_fwd(qkvt, sinks, side=None):
    s = qkvt.shape[1]
    nb = s // BLK

    def body(q_ref, kc_ref, kp_ref, vc_ref, vp_ref, sk_ref, o_ref):
        valid = _attn_mask(pl.program_id(0))
        for kvh in range(4):
            rows = slice(kvh * 64, (kvh + 1) * 64)
            _, _, probs, _ = _attn_probs(q_ref, kc_ref, kp_ref, sk_ref, kvh, valid)
            vt = jnp.concatenate([vp_ref[rows, :], vc_ref[rows, :]], axis=1).astype(BF16)
            o = _dot(vt, probs.astype(BF16))
            for g in range(4):
                h = kvh * 4 + g
                o_ref[h * 64:(h + 1) * 64, :] = o[:, g * BLK:(g + 1) * BLK].astype(BF16)

    own, extra = _hosted(
        body, name="attn_fwd", grid=(nb,), in_specs=_attn_in_specs(lambda n: n, lambda n: jnp.maximum(n - 1, 0)),
        out_specs=[pl.BlockSpec((Q_DIM, BLK), lambda n: (0, n))],
        out_shape=[jax.ShapeDtypeStruct((Q_DIM, s), BF16)], scratch_shapes=[],
        args=(qkvt, qkvt, qkvt, qkvt, qkvt, sinks), sem=("parallel",), side=side)
    return own[0] if side is None else (own[0], extra)


def _attn_bwd(qkvt, sinks, o, do, side=None):
    s = qkvt.shape[1]
    nb = s // BLK

    def body(q_ref, kc_ref, kp_ref, vc_ref, vp_ref, sk_ref, o_ref, do_ref, dq_ref, dk_ref, dv_ref, dsk_ref, ck, cv, nk, nv):
        n = pl.program_id(0)

        @pl.when(n == 0)
        def _():
            ck[...] = jnp.zeros_like(ck)
            cv[...] = jnp.zeros_like(cv)
            dsk_ref[...] = jnp.zeros_like(dsk_ref)

        @pl.when(n < nb)
        def _():
            valid = _attn_mask(n)
            lane = lax.broadcasted_iota(jnp.int32, (1, 128), 1)
            dsk = jnp.zeros((1, 128), F32)
            for kvh in range(4):
                rows = slice(kvh * 64, (kvh + 1) * 64)
                qt, kt, probs, psink = _attn_probs(q_ref, kc_ref, kp_ref, sk_ref, kvh, valid)
                vt = jnp.concatenate([vp_ref[rows, :], vc_ref[rows, :]], axis=1).astype(BF16)
                heads = [slice((kvh * 4 + g) * 64, (kvh * 4 + g + 1) * 64) for g in range(4)]
                dot = jnp.concatenate([do_ref[hh, :] for hh in heads], axis=1)
                ot = jnp.concatenate([o_ref[hh, :] for hh in heads], axis=1).astype(F32)
                delta = jnp.sum(dot * ot, axis=0, keepdims=True)
                dot16 = dot.astype(BF16)
                dp = _dot_tn(vt, dot16)
                ds = (probs * (dp - delta) * 0.125).astype(BF16)
                dqt = _dot(kt, ds)
                nk[rows, :] = _dot_nt(qt, ds)
                nv[rows, :] = _dot_nt(dot16, probs.astype(BF16))
                sd = psink * delta
                for g in range(4):
                    dq_ref[heads[g], :] = dqt[:, g * BLK:(g + 1) * BLK].astype(BF16)
                    val = -jnp.sum(sd[:, g * BLK:(g + 1) * BLK], axis=1, keepdims=True)
                    dsk = dsk + jnp.where(lane == kvh * 4 + g, val, 0.0)
            dsk_ref[0:1, :] += dsk
            dk_ref[...] = (ck[...] + nk[:, :BLK]).astype(BF16)
            dv_ref[...] = (cv[...] + nv[:, :BLK]).astype(BF16)
            ck[...] = nk[:, BLK:]
            cv[...] = nv[:, BLK:]

        @pl.when(n == nb)
        def _():
            dk_ref[...] = ck[...].astype(BF16)
            dv_ref[...] = cv[...].astype(BF16)

    cur = lambda n: jnp.minimum(n, nb - 1)
    prev = lambda n: jnp.maximum(jnp.minimum(n, nb - 1) - 1, 0)
    outb = lambda n: jnp.maximum(n - 1, 0)
    own, extra = _hosted(
        body, name="attn_bwd", grid=(nb + 1,),
        in_specs=_attn_in_specs(cur, prev) + [pl.BlockSpec((Q_DIM, BLK), lambda n: (0, cur(n))),
                                              pl.BlockSpec((Q_DIM, BLK), lambda n: (0, cur(n)))],
        out_specs=[pl.BlockSpec((Q_DIM, BLK), lambda n: (0, cur(n))),
                   pl.BlockSpec((KV_DIM, BLK), lambda n: (0, outb(n))),
                   pl.BlockSpec((KV_DIM, BLK), lambda n: (0, outb(n))),
                   pl.BlockSpec((8, 128), lambda n: (0, 0))],
        out_shape=[jax.ShapeDtypeStruct((Q_DIM, s), BF16), jax.ShapeDtypeStruct((KV_DIM, s), BF16),
                   jax.ShapeDtypeStruct((KV_DIM, s), BF16), jax.ShapeDtypeStruct((8, 128), F32)],
        scratch_shapes=[pltpu.VMEM((KV_DIM, BLK), F32)] * 2 + [pltpu.VMEM((KV_DIM, 2 * BLK), F32)] * 2,
        args=(qkvt, qkvt, qkvt, qkvt, qkvt, sinks, o, do), sem=("arbitrary",), side=side)
    return own if side is None else (own, extra)


def _shift_down(x, j):
    if j == 0:
        return x
    row = lax.broadcasted_iota(jnp.int32, x.shape, 0)
    return jnp.where(row >= j, pltpu.roll(x, j, 0), 0.0)


def _shift_up(x, j):
    if j == 0:
        return x
    s = x.shape[0]
    row = lax.broadcasted_iota(jnp.int32, x.shape, 0)
    return jnp.where(row < s - j, pltpu.roll(x, s - j, 0), 0.0)


def _conv(x, w_ref, b_ref):
    kk = w_ref.shape[0]
    y = _shift_down(x, kk - 1) * w_ref[0:1, :]
    for q in range(1, kk):
        y = y + _shift_down(x, kk - 1 - q) * w_ref[q:q + 1, :]
    return y + b_ref[...]


def _conv_bwd(dy, x, w_ref, dx_dtype):
    kk = w_ref.shape[0]
    dx = _shift_up(dy, kk - 1) * w_ref[0:1, :]
    dws = [jnp.sum(dy * _shift_down(x, kk - 1), axis=0, keepdims=True)]
    for q in range(1, kk):
        dx = dx + _shift_up(dy, kk - 1 - q) * w_ref[q:q + 1, :]
        dws.append(jnp.sum(dy * _shift_down(x, kk - 1 - q), axis=0, keepdims=True))
    return dx.astype(dx_dtype), dws, jnp.sum(dy, axis=0, keepdims=True)


def _dsilu(y, sg):
    return sg * (1.0 + y * (1.0 - sg))


CT = 256


def _ssd_conv_fwd(proj, w, b):
    s = proj.shape[0]

    def body(x_ref, w_ref, b_ref, o_ref):
        y = _conv(x_ref[...], w_ref, b_ref)
        o_ref[...] = y * _sigmoid(y)

    return pl.pallas_call(
        body, name="ssd_conv_fwd", grid=(XBC_DIM // CT,),
        in_specs=[pl.BlockSpec((s, CT), lambda i: (0, O_X // CT + i)), pl.BlockSpec((4, CT), lambda i: (0, i)),
                  pl.BlockSpec((1, CT), lambda i: (0, i))],
        out_specs=pl.BlockSpec((s, CT), lambda i: (0, i)),
        out_shape=jax.ShapeDtypeStruct((s, XBC_DIM), F32), compiler_params=_cp(("parallel",)),
    )(proj, w, b)


def _ssd_conv_bwd(dact, proj, w, b):
    s = proj.shape[0]

    def body(d_ref, x_ref, w_ref, b_ref, dx_ref, dw_ref, db_ref):
        x = x_ref[...]
        y = _conv(x, w_ref, b_ref)
        dy = d_ref[...] * _dsilu(y, _sigmoid(y))
        dx, dws, db = _conv_bwd(dy, x, w_ref, BF16)
        dx_ref[...] = dx
        for q in range(4):
            dw_ref[q:q + 1, :] = dws[q]
        db_ref[...] = db

    return pl.pallas_call(
        body, name="ssd_conv_bwd", grid=(XBC_DIM // CT,),
        in_specs=[pl.BlockSpec((s, CT), lambda i: (0, i)), pl.BlockSpec((s, CT), lambda i: (0, O_X // CT + i)),
                  pl.BlockSpec((4, CT), lambda i: (0, i)), pl.BlockSpec((1, CT), lambda i: (0, i))],
        out_specs=[pl.BlockSpec((s, CT), lambda i: (0, i)), pl.BlockSpec((4, CT), lambda i: (0, i)),
                   pl.BlockSpec((1, CT), lambda i: (0, i))],
        out_shape=[jax.ShapeDtypeStruct((s, XBC_DIM), BF16), jax.ShapeDtypeStruct((4, XBC_DIM), F32),
                   jax.ShapeDtypeStruct((1, XBC_DIM), F32)],
        compiler_params=_cp(("parallel",)),
    )(dact, proj, w, b)


NFT = D_FF // CT


def _ffn_act_fwd(up, w, b):
    s = up.shape[0]

    def body(v_ref, g_ref, wv_ref, wg_ref, bv_ref, bg_ref, o_ref):
        val = _conv(v_ref[...], wv_ref, bv_ref)
        gt = _conv(g_ref[...], wg_ref, bg_ref)
        o_ref[...] = ((gt * _sigmoid(gt)) * val).astype(BF16)

    col = lambda off: (lambda i: (0, off + i))
    return pl.pallas_call(
        body, name="ffn_act_fwd", grid=(NFT,),
        in_specs=[pl.BlockSpec((s, CT), col(0)), pl.BlockSpec((s, CT), col(NFT)),
                  pl.BlockSpec((3, CT), col(0)), pl.BlockSpec((3, CT), col(NFT)),
                  pl.BlockSpec((1, CT), col(0)), pl.BlockSpec((1, CT), col(NFT))],
        out_specs=pl.BlockSpec((s, CT), col(0)),
        out_shape=jax.ShapeDtypeStruct((s, D_FF), BF16), compiler_params=_cp(("parallel",)),
    )(up, up, w, w, b, b)


def _ffn_act_bwd(dact, up, w, b):
    s = up.shape[0]

    def body(d_ref, v_ref, g_ref, wv_ref, wg_ref, bv_ref, bg_ref, dx_ref, dw_ref, db_ref):
        xv, xg = v_ref[...], g_ref[...]
        val = _conv(xv, wv_ref, bv_ref)
        gt = _conv(xg, wg_ref, bg_ref)
        sg = _sigmoid(gt)
        d = d_ref[...]
        for half, (dy, x, w_ref) in enumerate(((d * (gt * sg), xv, wv_ref), (d * val * _dsilu(gt, sg), xg, wg_ref))):
            dx, dws, db = _conv_bwd(dy, x, w_ref, BF16)
            dx_ref[half] = dx
            for q in range(3):
                dw_ref[half, q:q + 1, :] = dws[q]
            db_ref[half] = db

    col = lambda off: (lambda i: (0, off + i))
    both = lambda i: (0, 0, i)
    return pl.pallas_call(
        body, name="ffn_act_bwd", grid=(NFT,),
        in_specs=[pl.BlockSpec((s, CT), col(0)), pl.BlockSpec((s, CT), col(0)), pl.BlockSpec((s, CT), col(NFT)),
                  pl.BlockSpec((3, CT), col(0)), pl.BlockSpec((3, CT), col(NFT)),
                  pl.BlockSpec((1, CT), col(0)), pl.BlockSpec((1, CT), col(NFT))],
        out_specs=[pl.BlockSpec((2, s, CT), both), pl.BlockSpec((2, 3, CT), both), pl.BlockSpec((2, 1, CT), both)],
        out_shape=[jax.ShapeDtypeStruct((2, s, D_FF), BF16), jax.ShapeDtypeStruct((2, 3, D_FF), F32),
                   jax.ShapeDtypeStruct((2, 1, D_FF), F32)],
        compiler_params=_cp(("parallel",)),
    )(dact, up, up, w, w, b, b)


def _expand_mat():
    r = lax.broadcasted_iota(jnp.int32, (128, D_INNER), 0)
    c = lax.broadcasted_iota(jnp.int32, (128, D_INNER), 1)
    return ((c >> 6) == r).astype(BF16)


def _reduce_mat():
    r = lax.broadcasted_iota(jnp.int32, (D_INNER, 128), 0)
    c = lax.broadcasted_iota(jnp.int32, (D_INNER, 128), 1)
    return ((r >> 6) == c).astype(BF16)


def _split(v, parts):
    out = []
    for _ in range(parts - 1):
        p = v.astype(BF16)
        out.append(p)
        v = v - p.astype(F32)
    out.append(v.astype(BF16))
    return out


def _sel_dot(v, sel, parts):
    acc = None
    for p in reversed(_split(v, parts)):
        t = _dot(p, sel)
        acc = t if acc is None else acc + t
    return acc


def _row8(v):
    return jnp.broadcast_to(v, (8, v.shape[1]))


def _tril():
    r = lax.broadcasted_iota(jnp.int32, (BLK, BLK), 0)
    c = lax.broadcasted_iota(jnp.int32, (BLK, BLK), 1)
    return r >= c


def _softplus(x):
    return jnp.maximum(x, 0.0) + jnp.log(1.0 + jnp.exp(-jnp.abs(x)))


def _ssd_common(dtraw_ref, dtb_ref, alog_ref):
    causal = _tril()
    e_mat = _expand_mat()
    a_neg = -jnp.exp(alog_ref[...])
    dt = _softplus(dtraw_ref[...] + dtb_ref[...])
    a_cs = _dot(causal.astype(F32), dt * a_neg, HI)
    a_cs_t = a_cs.T
    dt_x = _sel_dot(dt, e_mat, 3)
    acs_x = _sel_dot(a_cs, e_mat, 3)
    alast_x = acs_x[BLK - 1:BLK, :]
    ea_x = jnp.exp(acs_x)
    ds_x = jnp.exp(alast_x - acs_x)
    elast_x = jnp.exp(alast_x)
    return causal, e_mat, a_neg, dt, a_cs, a_cs_t, dt_x, ea_x, ds_x, elast_x


def _decay(a_cs, a_cs_t, h, causal):
    seg = a_cs[:, h:h + 1] - a_cs_t[h:h + 1, :]
    return jnp.where(causal, jnp.exp(jnp.where(causal, seg, 0.0)), 0.0)


def _ssd_fwd(xbc, proj, dt_bias, a_log, d_skip, side=None):
    s = xbc.shape[0]
    nc = s // BLK

    def body(xs_ref, b_ref, c_ref, dtraw_ref, dtb_ref, alog_ref, dskip_ref, y_ref, hp_ref, h_scr, xc16):
        @pl.when(pl.program_id(0) == 0)
        def _():
            h_scr[...] = jnp.zeros_like(h_scr)

        causal, e_mat, _, _, a_cs, a_cs_t, dt_x, ea_x, ds_x, elast_x = _ssd_common(dtraw_ref, dtb_ref, alog_ref)
        dskip_x = _sel_dot(_row8(dskip_ref[...]), e_mat, 3)[0:1]
        xs = xs_ref[...]
        xc = xs * dt_x
        xc16[...] = xc.astype(BF16)
        xcd = (xc * ds_x).astype(BF16)
        hp_ref[0] = h_scr[...]
        for g in range(4):
            gs = slice(g * 512, (g + 1) * 512)
            cg = c_ref[:, g * 128:(g + 1) * 128].astype(BF16)
            bg = b_ref[:, g * 128:(g + 1) * 128].astype(BF16)
            cb = _dot_nt(cg, bg)
            hg = h_scr[:, gs]
            yoff = _dot(cg, hg.astype(BF16)) * ea_x[:, gs]
            for j in range(8):
                h = g * 8 + j
                hsl = slice(h * 64, (h + 1) * 64)
                mm = (cb * _decay(a_cs, a_cs_t, h, causal)).astype(BF16)
                y_ref[:, hsl] = _dot(mm, xc16[:, hsl])
            y_ref[:, gs] += yoff + xs[:, gs] * dskip_x[:, gs]
            h_scr[:, gs] = hg * elast_x[:, gs] + _dot_tn(bg, xcd[:, gs])

    vec = pl.BlockSpec((1, 128), lambda c: (0, 0))
    own, extra = _hosted(
        body, name="ssd_fwd", grid=(nc,),
        in_specs=[pl.BlockSpec((BLK, D_INNER), lambda c: (c, 0)),
                  pl.BlockSpec((BLK, BC_DIM), lambda c: (c, D_INNER // BC_DIM)),
                  pl.BlockSpec((BLK, BC_DIM), lambda c: (c, D_INNER // BC_DIM + 1)),
                  pl.BlockSpec((BLK, 128), lambda c: (c, O_DT // 128)), vec, vec, vec],
        out_specs=[pl.BlockSpec((BLK, D_INNER), lambda c: (c, 0)),
                   pl.BlockSpec((1, 128, D_INNER), lambda c: (c, 0, 0))],
        out_shape=[jax.ShapeDtypeStruct((s, D_INNER), F32), jax.ShapeDtypeStruct((nc, 128, D_INNER), F32)],
        scratch_shapes=[pltpu.VMEM((128, D_INNER), F32), pltpu.VMEM((BLK, D_INNER), BF16)],
        args=(xbc, xbc, xbc, proj, dt_bias, a_log, d_skip), sem=("arbitrary",), side=side)
    return own if side is None else (own, extra)


def _ssd_bwd(xbc, proj, dt_bias, a_log, d_skip, hprev, dy, side=None):
    s = xbc.shape[0]
    nc = s // BLK

    def body(xs_ref, b_ref, c_ref, dtraw_ref, dtb_ref, alog_ref, dskip_ref, hp_ref, dy_ref,
             dxbc_ref, ddt_ref, dvec_ref, dh_scr, xc16, dy16, dxc_scr, dacs_r, tdiff):
        step = pl.program_id(0)
        dacs_r[...] = jnp.zeros_like(dacs_r)

        @pl.when(step == 0)
        def _():
            dh_scr[...] = jnp.zeros_like(dh_scr)
            dvec_ref[...] = jnp.zeros_like(dvec_ref)

        causal, e_mat, a_neg, dt, a_cs, a_cs_t, dt_x, ea_x, ds_x, elast_x = _ssd_common(dtraw_ref, dtb_ref, alog_ref)
        r_mat = _reduce_mat()
        lane = lax.broadcasted_iota(jnp.int32, (1, 128), 1)
        dskip_x = _sel_dot(_row8(dskip_ref[...]), e_mat, 3)[0:1]
        xs = xs_ref[...]
        dy = dy_ref[...]
        xc = xs * dt_x
        xcd = xc * ds_x
        xc16[...] = xc.astype(BF16)
        dy16[...] = dy.astype(BF16)
        dyea = dy * ea_x
        dh = dh_scr[...]
        hp = hp_ref[0]
        dalast_x = jnp.sum(dh * hp, axis=0, keepdims=True) * elast_x
        dacs = jnp.zeros((BLK, 128), F32)
        for g in range(4):
            gs = slice(g * 512, (g + 1) * 512)
            bsl = slice(g * 128, (g + 1) * 128)
            cg = c_ref[:, bsl].astype(BF16)
            bg = b_ref[:, bsl].astype(BF16)
            cb = _dot_nt(cg, bg)
            hg16 = hp[:, gs].astype(BF16)
            dhg16 = dh[:, gs].astype(BF16)
            raw = _dot(cg, hg16)
            draw16 = dyea[:, gs].astype(BF16)
            dcg = _dot_nt(draw16, hg16)
            dhp_g = _dot_tn(cg, draw16)
            dbg = _dot_nt(xcd[:, gs].astype(BF16), dhg16)
            dxcd = _dot(bg, dhg16)
            dcb = jnp.zeros((BLK, BLK), F32)
            for j in range(8):
                h = g * 8 + j
                hsl = slice(h * 64, (h + 1) * 64)
                decay = _decay(a_cs, a_cs_t, h, causal)
                m = cb * decay
                dm = _dot_nt(dy16[:, hsl], xc16[:, hsl])
                dxc_scr[:, hsl] = _dot_tn(m.astype(BF16), dy16[:, hsl])
                dcb = dcb + dm * decay
                dseg = dm * m
                oneh = jnp.where(lane == h, 1.0, 0.0)
                dacs = dacs + jnp.sum(dseg, axis=1, keepdims=True) * oneh
                dacs_r[h:h + 1, :] = jnp.sum(dseg, axis=0, keepdims=True)
            dcb16 = dcb.astype(BF16)
            dcg = dcg + _dot(dcb16, bg)
            dbg = dbg + _dot_tn(dcb16, cg)
            dxbc_ref[:, D_INNER + g * 128:D_INNER + (g + 1) * 128] = dbg
            dxbc_ref[:, D_INNER + BC_DIM + g * 128:D_INNER + BC_DIM + (g + 1) * 128] = dcg
            dxc_scr[:, gs] += dxcd * ds_x[:, gs]
            dh_scr[:, gs] = dh[:, gs] * elast_x[:, gs] + dhp_g
            tst = dxcd * xcd[:, gs]
            tdiff[:, gs] = dy[:, gs] * (raw * ea_x[:, gs]) - tst
            tdiff[BLK - 1:BLK, gs] += jnp.sum(tst, axis=0, keepdims=True)
        dxc = dxc_scr[...]
        row = lax.broadcasted_iota(jnp.int32, (BLK, D_INNER), 0)
        tfull = tdiff[...] + jnp.where(row == BLK - 1, dalast_x, 0.0)
        dacs = dacs + _sel_dot(tfull, r_mat, 2) - dacs_r[...].T
        da = _dot_tn(causal.astype(F32), dacs, HI)
        ddt = da * a_neg + _sel_dot(dxc * xs, r_mat, 2)
        lmask = lax.broadcasted_iota(jnp.int32, (BLK, 128), 1) < N_SSD_HEADS
        ddtraw = jnp.where(lmask, ddt * _sigmoid(dtraw_ref[...] + dtb_ref[...]), 0.0)
        ddt_ref[...] = ddtraw.astype(BF16)
        dxbc_ref[:, 0:D_INNER] = dy * dskip_x + dxc * dt_x
        dvec_ref[0:1, :] += jnp.sum(ddtraw, axis=0, keepdims=True)
        dvec_ref[1:2, :] += jnp.where(lane < N_SSD_HEADS, jnp.sum(da * dt, axis=0, keepdims=True) * a_neg, 0.0)
        dvec_ref[2:3, :] += _sel_dot(_row8(jnp.sum(dy * xs, axis=0, keepdims=True)), r_mat, 3)[0:1]

    rev = lambda c: nc - 1 - c
    vec = pl.BlockSpec((1, 128), lambda c: (0, 0))
    own, extra = _hosted(
        body, name="ssd_bwd", grid=(nc,),
        in_specs=[pl.BlockSpec((BLK, D_INNER), lambda c: (rev(c), 0)),
                  pl.BlockSpec((BLK, BC_DIM), lambda c: (rev(c), D_INNER // BC_DIM)),
                  pl.BlockSpec((BLK, BC_DIM), lambda c: (rev(c), D_INNER // BC_DIM + 1)),
                  pl.BlockSpec((BLK, 128), lambda c: (rev(c), O_DT // 128)), vec, vec, vec,
                  pl.BlockSpec((1, 128, D_INNER), lambda c: (rev(c), 0, 0)),
                  pl.BlockSpec((BLK, D_INNER), lambda c: (rev(c), 0))],
        out_specs=[pl.BlockSpec((BLK, XBC_DIM), lambda c: (rev(c), 0)),
                   pl.BlockSpec((BLK, 128), lambda c: (rev(c), 0)),
                   pl.BlockSpec((8, 128), lambda c: (0, 0))],
        out_shape=[jax.ShapeDtypeStruct((s, XBC_DIM), F32), jax.ShapeDtypeStruct((s, 128), BF16),
                   jax.ShapeDtypeStruct((8, 128), F32)],
        scratch_shapes=[pltpu.VMEM((128, D_INNER), F32), pltpu.VMEM((BLK, D_INNER), BF16),
                        pltpu.VMEM((BLK, D_INNER), BF16), pltpu.VMEM((BLK, D_INNER), F32),
                        pltpu.VMEM((128, BLK), F32), pltpu.VMEM((BLK, D_INNER), F32)],
        args=(xbc, xbc, xbc, proj, dt_bias, a_log, d_skip, hprev, dy), sem=("arbitrary",), side=side)
    return own if side is None else (own, extra)


GW = 512


def _gate_norm_fwd(y, proj, wn, *, tm=512):
    s = y.shape[0]
    tm = _tile(s, tm)

    def body(y_ref, z_ref, w_ref, o_ref):
        z = z_ref[...]
        y2 = y_ref[...] * (z * _sigmoid(z))
        r = lax.rsqrt(jnp.mean(y2 * y2, axis=-1, keepdims=True) + EPS)
        o_ref[...] = ((y2 * r) * w_ref[...]).astype(BF16)

    return pl.pallas_call(
        body, name="gate_norm_fwd", grid=(s // tm, 4),
        in_specs=[pl.BlockSpec((tm, GW), lambda i, g: (i, g)), pl.BlockSpec((tm, GW), lambda i, g: (i, O_Z // GW + g)),
                  pl.BlockSpec((1, GW), lambda i, g: (0, g))],
        out_specs=pl.BlockSpec((tm, GW), lambda i, g: (i, g)),
        out_shape=jax.ShapeDtypeStruct((s, D_INNER), BF16), compiler_params=_cp(("parallel", "parallel")),
    )(y, proj, wn)


def _gate_norm_bwd(dyn, y, proj, wn, *, tm=512):
    s = y.shape[0]
    tm = _tile(s, tm)

    def body(d_ref, y_ref, z_ref, w_ref, dy_ref, dz_ref, dw_ref):
        i = pl.program_id(1)
        z = z_ref[...]
        sg = _sigmoid(z)
        sz = z * sg
        yv = y_ref[...]
        y2 = yv * sz
        r = lax.rsqrt(jnp.mean(y2 * y2, axis=-1, keepdims=True) + EPS)
        xh = y2 * r
        dv = d_ref[...]
        g = dv * w_ref[...]
        dy2 = r * (g - xh * jnp.mean(g * xh, axis=-1, keepdims=True))
        dy_ref[...] = dy2 * sz
        dz_ref[...] = (dy2 * yv * _dsilu(z, sg)).astype(BF16)
        part = jnp.sum(dv * xh, axis=0, keepdims=True)

        @pl.when(i == 0)
        def _():
            dw_ref[...] = part

        @pl.when(i > 0)
        def _():
            dw_ref[...] += part

    blk = pl.BlockSpec((tm, GW), lambda g, i: (i, g))
    vec = pl.BlockSpec((1, GW), lambda g, i: (0, g))
    return pl.pallas_call(
        body, name="gate_norm_bwd", grid=(4, s // tm),
        in_specs=[blk, blk, pl.BlockSpec((tm, GW), lambda g, i: (i, O_Z // GW + g)), vec],
        out_specs=[blk, blk, vec],
        out_shape=[jax.ShapeDtypeStruct((s, D_INNER), F32), jax.ShapeDtypeStruct((s, D_INNER), BF16),
                   jax.ShapeDtypeStruct((1, D_INNER), F32)],
        compiler_params=_cp(("parallel", "arbitrary")),
    )(dyn, y, proj, wn)


def _merge_fwd(proj, b_gate, attn, ssd_out, *, tm=512):
    s = attn.shape[0]
    tm = _tile(s, tm)

    def body(ga_ref, gs_ref, ba_ref, bs_ref, a_ref, s_ref, o_ref):
        ga = _sigmoid(ga_ref[...] + ba_ref[...])
        gs = _sigmoid(gs_ref[...] + bs_ref[...])
        o_ref[...] = (ga * a_ref[...] + gs * s_ref[...]).astype(BF16)

    blk = pl.BlockSpec((tm, GW), lambda i, j: (i, j))
    return pl.pallas_call(
        body, name="merge_fwd", grid=(s // tm, 2),
        in_specs=[pl.BlockSpec((tm, GW), lambda i, j: (i, O_GA // GW + j)),
                  pl.BlockSpec((tm, GW), lambda i, j: (i, O_GS // GW + j)),
                  pl.BlockSpec((1, GW), lambda i, j: (0, j)), pl.BlockSpec((1, GW), lambda i, j: (0, 2 + j)), blk, blk],
        out_specs=blk, out_shape=jax.ShapeDtypeStruct((s, D_MODEL), BF16),
        compiler_params=_cp(("parallel", "parallel")),
    )(proj, proj, b_gate, b_gate, attn, ssd_out)


def _merge_bwd(dm, proj, b_gate, attn, ssd_out, *, tm=512):
    s = attn.shape[0]
    tm = _tile(s, tm)

    def body(d_ref, ga_ref, gs_ref, ba_ref, bs_ref, a_ref, s_ref, da_ref, ds_ref, dga_ref, dgs_ref, dba_ref, dbs_ref):
        i = pl.program_id(1)
        ga = _sigmoid(ga_ref[...] + ba_ref[...])
        gs = _sigmoid(gs_ref[...] + bs_ref[...])
        d = d_ref[...]
        da_ref[...] = (d * ga).astype(BF16)
        ds_ref[...] = (d * gs).astype(BF16)
        dga = d * a_ref[...] * (ga * (1.0 - ga))
        dgs = d * s_ref[...] * (gs * (1.0 - gs))
        dga_ref[...] = dga.astype(BF16)
        dgs_ref[...] = dgs.astype(BF16)
        pa = jnp.sum(dga, axis=0, keepdims=True)
        ps = jnp.sum(dgs, axis=0, keepdims=True)

        @pl.when(i == 0)
        def _():
            dba_ref[...] = pa
            dbs_ref[...] = ps

        @pl.when(i > 0)
        def _():
            dba_ref[...] += pa
            dbs_ref[...] += ps

    blk = pl.BlockSpec((tm, GW), lambda j, i: (i, j))
    vec = pl.BlockSpec((1, GW), lambda j, i: (0, j))
    sd = jax.ShapeDtypeStruct((s, D_MODEL), BF16)
    vd = jax.ShapeDtypeStruct((1, D_MODEL), F32)
    return pl.pallas_call(
        body, name="merge_bwd", grid=(2, s // tm),
        in_specs=[blk, pl.BlockSpec((tm, GW), lambda j, i: (i, O_GA // GW + j)),
                  pl.BlockSpec((tm, GW), lambda j, i: (i, O_GS // GW + j)),
                  vec, pl.BlockSpec((1, GW), lambda j, i: (0, 2 + j)), blk, blk],
        out_specs=[blk, blk, blk, blk, vec, vec], out_shape=[sd, sd, sd, sd, vd, vd],
        compiler_params=_cp(("parallel", "arbitrary")),
    )(dm, proj, proj, b_gate, b_gate, attn, ssd_out)


def _adamw_math(w, g, m, v):
    mn = ADAM_B1 * m + (1.0 - ADAM_B1) * g
    vn = ADAM_B2 * v + (1.0 - ADAM_B2) * (g * g)
    m_hat = mn / (1.0 - ADAM_B1 ** ADAM_STEP)
    v_hat = vn / (1.0 - ADAM_B2 ** ADAM_STEP)
    return -ADAM_LR * (m_hat / (jnp.sqrt(v_hat) + ADAM_EPS) + ADAM_WD * w), mn, vn


def _adamw_many(ws, gs, ms, vs):
    n = len(ws)

    def body(*refs):
        outs = refs[4 * n:]
        for i in range(n):
            res = _adamw_math(*[refs[q * n + i][...] for q in range(4)])
            for q in range(3):
                outs[q * n + i][...] = res[q]

    return pl.pallas_call(body, name="adamw_small", out_shape=[jax.ShapeDtypeStruct(w.shape, F32) for w in ws] * 3,
                          compiler_params=_cp())(*ws, *gs, *ms, *vs)


def _adamw(w, g, m, v, *, name, tm=128):
    r, c = w.shape
    tm = r if (r < tm or r % tm) else tm

    def body(w_ref, g_ref, m_ref, v_ref, d_ref, nm_ref, nv_ref, g_out):
        gv = g_ref[:, :c]
        d_ref[...], nm_ref[...], nv_ref[...] = _adamw_math(w_ref[...], gv, m_ref[...], v_ref[...])
        g_out[...] = gv

    blk = pl.BlockSpec((tm, c), lambda i: (i, 0))
    sd = jax.ShapeDtypeStruct((r, c), F32)
    return pl.pallas_call(
        body, name=name, grid=(r // tm,), in_specs=[blk, pl.BlockSpec((tm, g.shape[1]), lambda i: (i, 0)), blk, blk],
        out_specs=[blk] * 4, out_shape=[sd] * 4, compiler_params=_cp(("parallel",)),
    )(w, g, m, v)


ANY = pl.BlockSpec(memory_space=pl.ANY)
N_CHIPS = 4


def _chip_of(k, x, y):
    return (x ^ (k >> 1), y ^ (k & 1))


def _all_gather_small(shard):
    r, c = shard.shape
    hr = r // 2

    def body(sh_ref, out_ref, send_sems, recv_sems, local_sem):
        x, y, cc = lax.axis_index("x"), lax.axis_index("y"), lax.axis_index("c")

        def half(px, py, pc):
            return out_ref.at[2 * px + py, pl.ds(pc * hr, hr), :]

        def copy(k, px, py, pc, to, src=None):
            return pltpu.make_async_remote_copy(
                src_ref=half(px, py, pc) if src is None else src, dst_ref=half(px, py, pc),
                send_sem=send_sems.at[k], recv_sem=recv_sems.at[k], device_id=to, device_id_type=MESH)

        mine = pltpu.make_async_copy(sh_ref, out_ref.at[2 * x + y], local_sem)
        mine.start()
        chips = [_chip_of(k, x, y) for k in (1, 2, 3)]
        first = [copy(j, x, y, cc, (*chip, cc), src=sh_ref.at[pl.ds(cc * hr, hr), :]) for j, chip in enumerate(chips)]
        for cp in first:
            cp.start()
        passed = [copy(3 + j, *chip, cc, (x, y, 1 - cc)) for j, chip in enumerate(chips)]
        for j, chip in enumerate(chips):
            copy(j, *chip, cc, (x, y, cc)).wait_recv()
            passed[j].start()
        for j, chip in enumerate(chips):
            copy(3 + j, *chip, 1 - cc, (x, y, cc)).wait_recv()
        for cp in first + passed:
            cp.wait_send()
        mine.wait()

    return pl.pallas_call(
        body, name="all_gather_small", in_specs=[ANY], out_specs=ANY,
        out_shape=jax.ShapeDtypeStruct((N_CHIPS, r, c), shard.dtype),
        scratch_shapes=[pltpu.SemaphoreType.DMA((6,)), pltpu.SemaphoreType.DMA((6,)), pltpu.SemaphoreType.DMA],
    )(shard)


def _cast_bf16(a, *, name, tm=512):
    n, r, c = a.shape
    tm = _tile(r, tm) if r % 128 == 0 else r

    def body(a_ref, o_ref):
        o_ref[...] = a_ref[...].astype(BF16)

    blk = pl.BlockSpec((1, tm, c), lambda i, j: (i, j, 0))
    return pl.pallas_call(body, name=name, grid=(n, r // tm), in_specs=[blk], out_specs=blk,
                          out_shape=jax.ShapeDtypeStruct(a.shape, BF16), compiler_params=_cp(("parallel", "parallel")))(a)


def _pair_exchange(g16, hr):
    n, r, c = g16.shape

    def body(g_ref, out_ref, send_sem, recv_sem):
        x, y, cc = lax.axis_index("x"), lax.axis_index("y"), lax.axis_index("c")
        cp = pltpu.make_async_remote_copy(
            src_ref=g_ref.at[:, pl.ds((1 - cc) * hr, hr), :], dst_ref=out_ref, send_sem=send_sem, recv_sem=recv_sem,
            device_id=(x, y, 1 - cc), device_id_type=MESH)
        cp.start()
        cp.wait()

    return pl.pallas_call(
        body, name="grad_pair_exchange", in_specs=[ANY], out_specs=ANY,
        out_shape=jax.ShapeDtypeStruct((n, hr, c), g16.dtype),
        scratch_shapes=[pltpu.SemaphoreType.DMA, pltpu.SemaphoreType.DMA],
    )(g16)


def _pair_add(g, recv, half_idx, hr, *, tm=384):
    n, r, c = g.shape
    nt = hr // tm

    def body(hi_ref, g_ref, r_ref, o32_ref, o16_ref):
        v = g_ref[...] + r_ref[...].astype(F32)
        o32_ref[...] = v
        o16_ref[...] = v.astype(BF16)

    gs = pltpu.PrefetchScalarGridSpec(
        num_scalar_prefetch=1, grid=(n, nt),
        in_specs=[pl.BlockSpec((1, tm, c), lambda i, j, hi: (i, hi[0] * nt + j, 0)),
                  pl.BlockSpec((1, tm, c), lambda i, j, hi: (i, j, 0))],
        out_specs=[pl.BlockSpec((1, tm, c), lambda i, j, hi: (i, j, 0))] * 2)
    return pl.pallas_call(
        body, name="grad_pair_add", grid_spec=gs,
        out_shape=[jax.ShapeDtypeStruct((n, hr, c), F32), jax.ShapeDtypeStruct((n, hr, c), BF16)],
        compiler_params=_cp(("parallel", "parallel")),
    )(half_idx, g, recv)


def _chip_exchange(p16):
    n, hr, c = p16.shape

    def body(p_ref, out_ref, send_sems, recv_sems):
        x, y, cc = lax.axis_index("x"), lax.axis_index("y"), lax.axis_index("c")
        cps = []
        for j, k in enumerate((1, 2, 3)):
            px, py = _chip_of(k, x, y)
            cps.append(pltpu.make_async_remote_copy(
                src_ref=p_ref.at[2 * px + py], dst_ref=out_ref.at[j], send_sem=send_sems.at[j], recv_sem=recv_sems.at[j],
                device_id=(px, py, cc), device_id_type=MESH))
        for cp in cps:
            cp.start()
        for cp in cps:
            cp.wait()

    return pl.pallas_call(
        body, name="grad_chip_exchange", in_specs=[ANY], out_specs=ANY,
        out_shape=jax.ShapeDtypeStruct((3, hr, c), p16.dtype),
        scratch_shapes=[pltpu.SemaphoreType.DMA((3,)), pltpu.SemaphoreType.DMA((3,))],
    )(p16)


def _chip_add(p32, recv, chip_idx, *, tm=384):
    n, hr, c = p32.shape

    def body(ci_ref, p_ref, r_ref, o_ref):
        o_ref[...] = ((p_ref[0] + r_ref[0].astype(F32)) + r_ref[1].astype(F32)) + r_ref[2].astype(F32)

    gs = pltpu.PrefetchScalarGridSpec(
        num_scalar_prefetch=1, grid=(hr // tm,),
        in_specs=[pl.BlockSpec((1, tm, c), lambda j, ci: (ci[0], j, 0)), pl.BlockSpec((3, tm, c), lambda j, ci: (0, j, 0))],
        out_specs=pl.BlockSpec((tm, c), lambda j, ci: (j, 0)))
    return pl.pallas_call(
        body, name="grad_chip_add", grid_spec=gs, out_shape=jax.ShapeDtypeStruct((hr, c), F32),
        compiler_params=_cp(("parallel",)),
    )(chip_idx, p32, recv)


def _pair_gather(f):
    hr, c = f.shape

    def body(f_ref, out_ref, send_sem, recv_sem, local_sem):
        x, y, cc = lax.axis_index("x"), lax.axis_index("y"), lax.axis_index("c")
        mine = pltpu.make_async_copy(f_ref, out_ref.at[pl.ds(cc * hr, hr), :], local_sem)
        mine.start()
        cp = pltpu.make_async_remote_copy(
            src_ref=f_ref, dst_ref=out_ref.at[pl.ds(cc * hr, hr), :], send_sem=send_sem, recv_sem=recv_sem,
            device_id=(x, y, 1 - cc), device_id_type=MESH)
        cp.start()
        cp.wait()
        mine.wait()

    return pl.pallas_call(
        body, name="grad_pair_gather", in_specs=[ANY], out_specs=ANY,
        out_shape=jax.ShapeDtypeStruct((2 * hr, c), f.dtype),
        scratch_shapes=[pltpu.SemaphoreType.DMA, pltpu.SemaphoreType.DMA, pltpu.SemaphoreType.DMA],
    )(f)


def _all_reduce_small(buf):
    r, c = buf.shape

    def body(b_ref, out_ref, gat, send_sems, recv_sems):
        x, y, cc = lax.axis_index("x"), lax.axis_index("y"), lax.axis_index("c")
        me = 4 * x + 2 * y + cc
        gat[me] = b_ref[...]
        cps = []
        for k in range(1, 8):
            px, py, pc = x ^ (k >> 2), y ^ ((k >> 1) & 1), cc ^ (k & 1)
            cps.append(pltpu.make_async_remote_copy(
                src_ref=b_ref, dst_ref=gat.at[me], send_sem=send_sems.at[k - 1], recv_sem=recv_sems.at[k - 1],
                device_id=(px, py, pc), device_id_type=MESH))
        for cp in cps:
            cp.start()
        for cp in cps:
            cp.wait()
        acc = gat[0]
        for d in range(1, 8):
            acc = acc + gat[d]
        out_ref[...] = acc

    vm = pl.BlockSpec(memory_space=pltpu.VMEM)
    return pl.pallas_call(
        body, name="all_reduce_small", in_specs=[vm], out_specs=vm, out_shape=jax.ShapeDtypeStruct((r, c), F32),
        scratch_shapes=[pltpu.VMEM((8, r, c), F32), pltpu.SemaphoreType.DMA((7,)), pltpu.SemaphoreType.DMA((7,))],
        compiler_params=pltpu.CompilerParams(vmem_limit_bytes=VMEM_LIMIT),
    )(buf)


def _pipe(fn, ins, outs, tr, depth=4, slots=None):
    shape = ins[0].shape
    lead, (r, c) = shape[:-2], shape[-2:]
    assert len(lead) <= 1 and r % tr == 0
    nr = r // tr
    which = list(range(lead[0])) if lead and slots is None else slots
    n = nr * (len(which) if lead else 1)
    ni, no = len(ins), len(outs)

    def blk(ref, step):
        rows = pl.ds((step % nr) * tr, tr)
        return ref.at[which[step // nr], rows, :] if lead else ref.at[rows, :]

    def scoped(*bufs):
        ibufs, obufs, isem, osem = bufs[:ni], bufs[ni:ni + no], bufs[-2], bufs[-1]

        def in_copy(q, step, slot):
            return pltpu.make_async_copy(blk(ins[q], step), ibufs[q].at[slot], isem.at[q, slot])

        def out_copy(q, step, slot):
            return pltpu.make_async_copy(obufs[q].at[slot], blk(outs[q], step), osem.at[q, slot])

        for step in range(min(nbuf - 1, n)):
            for q in range(ni):
                in_copy(q, step, step % nbuf).start()
        for step in range(n):
            slot = step % nbuf
            if step + nbuf - 1 < n:
                for q in range(ni):
                    in_copy(q, step + nbuf - 1, (step + nbuf - 1) % nbuf).start()
            for q in range(ni):
                in_copy(q, step, slot).wait()
            if step >= nbuf:
                for q in range(no):
                    out_copy(q, step - nbuf, slot).wait()
            res = fn(*[ibufs[q][slot] for q in range(ni)])
            for q in range(no):
                obufs[q][slot] = res[q].astype(obufs[q].dtype)
                out_copy(q, step, slot).start()
        for step in range(max(n - nbuf, 0), n):
            for q in range(no):
                out_copy(q, step, step % nbuf).wait()

    assert n <= 8
    nbuf = min(n, depth)
    pl.run_scoped(scoped, *[pltpu.VMEM((nbuf, tr, c), q.dtype) for q in ins], *[pltpu.VMEM((nbuf, tr, c), q.dtype) for q in outs],
                  pltpu.SemaphoreType.DMA((ni, nbuf)), pltpu.SemaphoreType.DMA((no, nbuf)))


W_IN_PAD = 2304
BIG = ("w_in", "w_attn_o", "w_ssd_o", "w_out", "w_up", "w_down")
BIG_SHAPE = dict(w_in=(D_MODEL, W_IN_PAD), w_attn_o=(Q_DIM // 4, D_MODEL), w_ssd_o=(D_INNER // 4, D_MODEL),
                 w_out=(D_MODEL // 4, D_MODEL), w_up=(D_MODEL, 2 * D_FF // 4), w_down=(D_FF // 4, D_MODEL))
BIG_TR = dict(w_in=128, w_attn_o=128, w_ssd_o=128, w_out=128, w_up=128, w_down=176)
X_FIRST = dict(w_in=True, w_attn_o=True, w_ssd_o=False, w_out=True, w_up=False, w_down=False)


def _neighbours(x, y, x_first):
    xn, yn = (1 - x, y), (x, 1 - y)
    n1, n2 = (xn, yn) if x_first else (yn, xn)
    slot = lambda ch: 2 * ch[0] + ch[1]
    return n1, n2, slot(n1), slot(n2), slot((1 - x, 1 - y))


def _gather_big(shards):
    nt = len(BIG)

    def body(*refs):
        sh, out = refs[:nt], refs[nt:2 * nt]
        send_sems, recv_sems = refs[2 * nt:]
        x, y, cc = lax.axis_index("x"), lax.axis_index("y"), lax.axis_index("c")
        me = 2 * x + y
        sib = (x, y, 1 - cc)
        for t, n in enumerate(BIG):
            _pipe(lambda v: (v,), [sh[t]], [out[t].at[me]], BIG_TR[n])

        def copy(t, k, slot, pc, to):
            hr = BIG_SHAPE[BIG[t]][0] // 2
            ref = out[t].at[slot, pl.ds(pc * hr, hr), :]
            return pltpu.make_async_remote_copy(src_ref=ref, dst_ref=ref, send_sem=send_sems.at[6 * t + k],
                                                recv_sem=recv_sems.at[6 * t + k], device_id=to, device_id_type=MESH)

        started = []

        def start(cp):
            cp.start()
            started.append(cp)

        geo = [_neighbours(x, y, X_FIRST[n]) for n in BIG]
        for t in range(nt):
            n1, n2, _, _, _ = geo[t]
            start(copy(t, 0, me, cc, (*n1, cc)))
            start(copy(t, 1, me, cc, (*n2, cc)))
        for t in range(nt):
            n1, n2, s1, s2, sd = geo[t]
            copy(t, 0, s1, cc, sib).wait_recv()
            start(copy(t, 2, s1, cc, (*n2, cc)))
            start(copy(t, 3, s1, cc, sib))
            copy(t, 1, s2, cc, sib).wait_recv()
            start(copy(t, 4, s2, cc, sib))
        for t in range(nt):
            _, _, s1, s2, sd = geo[t]
            copy(t, 2, sd, cc, sib).wait_recv()
            start(copy(t, 5, sd, cc, sib))
        for t in range(nt):
            _, _, s1, s2, sd = geo[t]
            copy(t, 3, s1, 1 - cc, sib).wait_recv()
            copy(t, 4, s2, 1 - cc, sib).wait_recv()
            copy(t, 5, sd, 1 - cc, sib).wait_recv()
        for cp in started:
            cp.wait_send()

    return pl.pallas_call(
        body, name="gather_big", in_specs=[ANY] * nt, out_specs=[ANY] * nt,
        out_shape=[jax.ShapeDtypeStruct((N_CHIPS, *BIG_SHAPE[n]), BF16) for n in BIG],
        scratch_shapes=[pltpu.SemaphoreType.DMA((6 * nt,)), pltpu.SemaphoreType.DMA((6 * nt,))],
        compiler_params=pltpu.CompilerParams(vmem_limit_bytes=VMEM_LIMIT),
    )(*shards)


def _reduce_big(grads):
    nt = len(BIG)
    nw = 7

    def body(*refs):
        g = refs[:nt]
        fin = refs[nt:2 * nt]
        work = refs[2 * nt:2 * nt + nw * nt]
        send_sems, recv_sems = refs[2 * nt + nw * nt:]
        x, y, cc = lax.axis_index("x"), lax.axis_index("y"), lax.axis_index("c")
        me = 2 * x + y
        sib = (x, y, 1 - cc)
        started = []

        def rcopy(t, k, src, dst, to):
            cp = pltpu.make_async_remote_copy(src_ref=src, dst_ref=dst, send_sem=send_sems.at[5 * t + k],
                                              recv_sem=recv_sems.at[5 * t + k], device_id=to, device_id_type=MESH)
            return cp

        def start(cp):
            cp.start()
            started.append(cp)

        geo = [_neighbours(x, y, X_FIRST[n]) for n in BIG]
        hrs = [BIG_SHAPE[n][0] // 2 for n in BIG]
        wk = lambda t: work[nw * t:nw * (t + 1)]
        one = lambda ref, slot: ref.at[pl.ds(slot, 1)]
        for t in range(nt):
            recv_a = wk(t)[0]
            start(rcopy(t, 0, g[t].at[:, pl.ds((1 - cc) * hrs[t], hrs[t]), :], recv_a, sib))
        for t, n in enumerate(BIG):
            recv_a, p32, p16, r1, qme, qs2, r2 = wk(t)
            n1, n2, s1, s2, sd = geo[t]
            rcopy(t, 0, recv_a, recv_a, sib).wait_recv()
            _pipe(lambda a, b: (a + b, a + b), [g[t].at[:, pl.ds(cc * hrs[t], hrs[t]), :], recv_a], [p32, p16], BIG_TR[n])
            start(rcopy(t, 1, one(p16, s1), one(r1, 0), (*n1, cc)))
            start(rcopy(t, 2, one(p16, sd), one(r1, 1), (*n1, cc)))
        for t, n in enumerate(BIG):
            recv_a, p32, p16, r1, qme, qs2, r2 = wk(t)
            n1, n2, s1, s2, sd = geo[t]
            rcopy(t, 1, one(r1, 0), one(r1, 0), sib).wait_recv()
            rcopy(t, 2, one(r1, 1), one(r1, 1), sib).wait_recv()
            _pipe(lambda a, b: (a + b.astype(F32),), [one(p32, s2), one(r1, 1)], [qs2], BIG_TR[n])
            start(rcopy(t, 3, qs2, r2, (*n2, cc)))
            _pipe(lambda a, b: (a + b.astype(F32),), [one(p32, me), one(r1, 0)], [qme], BIG_TR[n])
        for t, n in enumerate(BIG):
            recv_a, p32, p16, r1, qme, qs2, r2 = wk(t)
            rcopy(t, 3, r2, r2, sib).wait_recv()
            mine = fin[t].at[pl.ds(cc * hrs[t], hrs[t]), :]
            _pipe(lambda a, b: (a + b.astype(F32),), [qme.at[0], r2.at[0]], [mine], BIG_TR[n])
            start(rcopy(t, 4, mine, mine, sib))
        for t in range(nt):
            other = fin[t].at[pl.ds((1 - cc) * hrs[t], hrs[t]), :]
            rcopy(t, 4, other, other, sib).wait_recv()
        for cp in started:
            cp.wait_send()

    outs = [jax.ShapeDtypeStruct(BIG_SHAPE[n], F32) for n in BIG]
    for n in BIG:
        r, c = BIG_SHAPE[n]
        hr = r // 2
        outs += [jax.ShapeDtypeStruct((4, hr, c), F32), jax.ShapeDtypeStruct((4, hr, c), F32),
                 jax.ShapeDtypeStruct((4, hr, c), BF16), jax.ShapeDtypeStruct((2, hr, c), BF16),
                 jax.ShapeDtypeStruct((1, hr, c), F32), jax.ShapeDtypeStruct((1, hr, c), BF16),
                 jax.ShapeDtypeStruct((1, hr, c), BF16)]
    res = pl.pallas_call(
        body, name="reduce_big", in_specs=[ANY] * nt, out_specs=[ANY] * len(outs), out_shape=outs,
        scratch_shapes=[pltpu.SemaphoreType.DMA((5 * nt,)), pltpu.SemaphoreType.DMA((5 * nt,))],
        compiler_params=pltpu.CompilerParams(vmem_limit_bytes=VMEM_LIMIT),
    )(*grads)
    return res[:nt]


WHOLE_X_FIRST = dict(w_ssd_o=True, w_out=False, w_attn_o=False)


def _quarters(names):
    out = []
    for i, n in enumerate(names):
        if n in WHOLE_X_FIRST:
            h = BIG_SHAPE[n][0] // 2
            out.append((i, WHOLE_X_FIRST[n], 0, h, 128))
        else:
            q = BIG_SHAPE[n][0] // 4
            tr = 128 if q % 128 == 0 else q
            out += [(i, True, 0, q, tr), (i, False, q, q, tr)]
    return out


class _GatherJob:
    def __init__(self, names, shards, at=None):
        self.names = names
        self.at = at
        self.inputs = list(shards)
        self.out_shapes = [jax.ShapeDtypeStruct((N_CHIPS, *BIG_SHAPE[n]), BF16) for n in names]
        self.ent = _quarters(names)
        self.scratch = [pltpu.SemaphoreType.DMA((6 * len(self.ent),)), pltpu.SemaphoreType.DMA((6 * len(self.ent),))]

    def phases(self, sh, out, scr):
        send_sems, recv_sems = scr
        names, ent = self.names, self.ent
        x, y, cc = lax.axis_index("x"), lax.axis_index("y"), lax.axis_index("c")
        me = 2 * x + y
        sib = (x, y, 1 - cc)
        geo = [_neighbours(x, y, e[1]) for e in ent]
        started = []

        def copy(i, k, slot, pc, to):
            arr, _, roff, rows, _ = ent[i]
            hr = BIG_SHAPE[names[arr]][0] // 2
            ref = out[arr].at[slot, pl.ds(pc * hr + roff, rows), :]
            return pltpu.make_async_remote_copy(src_ref=ref, dst_ref=ref, send_sem=send_sems.at[6 * i + k],
                                                recv_sem=recv_sems.at[6 * i + k], device_id=to, device_id_type=MESH)

        def start(*a):
            copy(*a).start()
            started.append(a)

        def p0():
            for t, n in enumerate(names):
                _pipe(lambda v: (v,), [sh[t]], [out[t].at[me]], BIG_TR[n])
            for i in range(len(ent)):
                n1, n2, _, _, _ = geo[i]
                start(i, 0, me, cc, (*n1, cc))
                start(i, 1, me, cc, (*n2, cc))

        def p1():
            for i in range(len(ent)):
                n1, n2, s1, s2, sd = geo[i]
                copy(i, 0, s1, cc, sib).wait_recv()
                start(i, 2, s1, cc, (*n2, cc))
                start(i, 3, s1, cc, sib)
                copy(i, 1, s2, cc, sib).wait_recv()
                start(i, 4, s2, cc, sib)

        def p2():
            for i in range(len(ent)):
                sd = geo[i][4]
                copy(i, 2, sd, cc, sib).wait_recv()
                start(i, 5, sd, cc, sib)

        def p3():
            for i in range(len(ent)):
                _, _, s1, s2, sd = geo[i]
                copy(i, 3, s1, 1 - cc, sib).wait_recv()
                copy(i, 4, s2, 1 - cc, sib).wait_recv()
                copy(i, 5, sd, 1 - cc, sib).wait_recv()
            for a in started:
                copy(*a).wait_send()

        return [p0, p1, p2, p3]


class _ReduceJob:
    NW = 7

    def __init__(self, names, grads, at=None):
        self.names = names
        self.at = at
        self.inputs = list(grads)
        self.ent = _quarters(names)
        self.out_shapes = [jax.ShapeDtypeStruct(BIG_SHAPE[n], F32) for n in names]
        for arr, _, _, rows, _ in self.ent:
            c = BIG_SHAPE[names[arr]][1]
            self.out_shapes += [jax.ShapeDtypeStruct((4, rows, c), F32), jax.ShapeDtypeStruct((4, rows, c), F32),
                                jax.ShapeDtypeStruct((4, rows, c), BF16), jax.ShapeDtypeStruct((2, rows, c), BF16),
                                jax.ShapeDtypeStruct((1, rows, c), F32), jax.ShapeDtypeStruct((1, rows, c), BF16),
                                jax.ShapeDtypeStruct((1, rows, c), BF16)]
        self.scratch = [pltpu.SemaphoreType.DMA((8 * len(self.ent),)), pltpu.SemaphoreType.DMA((8 * len(self.ent),))]

    def phases(self, g, outs, scr):
        send_sems, recv_sems = scr
        names, ent, nw = self.names, self.ent, self.NW
        nt = len(names)
        fin, work = outs[:nt], outs[nt:]
        x, y, cc = lax.axis_index("x"), lax.axis_index("y"), lax.axis_index("c")
        me = 2 * x + y
        sib = (x, y, 1 - cc)
        geo = [_neighbours(x, y, e[1]) for e in ent]
        started = []
        wk = lambda i: work[nw * i:nw * (i + 1)]
        one = lambda ref, slot: ref.at[pl.ds(slot, 1)]

        def rows_of(i, pc):
            arr, _, roff, rows, _ = ent[i]
            return pl.ds(pc * (BIG_SHAPE[names[arr]][0] // 2) + roff, rows)

        def rcopy(i, k, src, dst, to):
            return pltpu.make_async_remote_copy(src_ref=src, dst_ref=dst, send_sem=send_sems.at[8 * i + k],
                                                recv_sem=recv_sems.at[8 * i + k], device_id=to, device_id_type=MESH)

        def start(make):
            make().start()
            started.append(make)

        def pair(i, q, slot, pc):
            return rcopy(i, q, g[ent[i][0]].at[pl.ds(slot, 1), rows_of(i, pc), :], one(wk(i)[0], slot), sib)

        def p0():
            for i in range(len(ent)):
                _, _, s1, s2, sd = geo[i]
                for q, slot in enumerate((s1, sd, s2, me)):
                    start(lambda i=i, q=q, slot=slot: pair(i, q, slot, 1 - cc))

        def p1():
            for i, e in enumerate(ent):
                recv_a, _, p16, _ = wk(i)[:4]
                n1, n2, s1, s2, sd = geo[i]
                pair(i, 0, s1, cc).wait_recv()
                pair(i, 1, sd, cc).wait_recv()
                _pipe(lambda a, b: (a + b,), [g[e[0]].at[:, rows_of(i, cc), :], recv_a], [p16], e[4], slots=(s1, sd))
                start(lambda i=i, s1=s1, n1=n1: rcopy(i, 4, one(wk(i)[2], s1), one(wk(i)[3], 0), (*n1, cc)))
                start(lambda i=i, sd=sd, n1=n1: rcopy(i, 5, one(wk(i)[2], sd), one(wk(i)[3], 1), (*n1, cc)))
            for i, e in enumerate(ent):
                recv_a, p32 = wk(i)[:2]
                _, _, s1, s2, sd = geo[i]
                pair(i, 2, s2, cc).wait_recv()
                pair(i, 3, me, cc).wait_recv()
                _pipe(lambda a, b: (a + b,), [g[e[0]].at[:, rows_of(i, cc), :], recv_a], [p32], e[4], slots=(s2, me))

        def p2():
            for i, e in enumerate(ent):
                _, p32, _, r1, qme, qs2, r2 = wk(i)
                n1, n2, s1, s2, sd = geo[i]
                rcopy(i, 4, one(r1, 0), one(r1, 0), sib).wait_recv()
                rcopy(i, 5, one(r1, 1), one(r1, 1), sib).wait_recv()
                _pipe(lambda a, b, c, d: (a + b.astype(F32), c + d.astype(F32)),
                      [one(p32, s2), one(r1, 1), one(p32, me), one(r1, 0)], [qs2, qme], e[4])
                start(lambda i=i, n2=n2: rcopy(i, 6, wk(i)[5], wk(i)[6], (*n2, cc)))

        def p3():
            for i, e in enumerate(ent):
                qme, r2 = wk(i)[4], wk(i)[6]
                rcopy(i, 6, r2, r2, sib).wait_recv()
                mine = fin[e[0]].at[rows_of(i, cc), :]
                _pipe(lambda a, b: (a + b.astype(F32),), [qme.at[0], r2.at[0]], [mine], e[4])
                start(lambda i=i, e=e: rcopy(i, 7, fin[e[0]].at[rows_of(i, cc), :], fin[e[0]].at[rows_of(i, cc), :], sib))

        def p4():
            for i, e in enumerate(ent):
                other = fin[e[0]].at[rows_of(i, 1 - cc), :]
                rcopy(i, 7, other, other, sib).wait_recv()
            for make in started:
                make().wait_send()

        return [p0, p1, p2, p3, p4]


class _AdamJob:
    def __init__(self, names, ws, gs, ms, vs, groups):
        self.names, self.groups = names, groups
        self.inputs = [a for quad in zip(ws, gs, ms, vs) for a in quad]
        self.out_shapes = [jax.ShapeDtypeStruct(w.shape, F32) for w in ws for _ in range(4)]

    def work(self, ins, outs):
        def one(t):
            w, g, m, v = ins[4 * t:4 * t + 4]
            r = w.shape[1]
            tr = 128 if r % 128 == 0 else r // 4
            _pipe(lambda a, b, c, d: (*_adamw_math(a, b, c, d), b), [w.at[0], g, m.at[0], v.at[0]],
                  [o.at[0] for o in outs[4 * t:4 * t + 4]], tr, depth=2)

        def group(grp):
            def run():
                for n in grp:
                    one(self.names.index(n))
            return run

        return [group(grp) for grp in self.groups]


class _Interleaved:
    def __init__(self, job, work, at):
        self.job, self.wk, self.at = job, work, at
        self.inputs = job.inputs + work.inputs
        self.out_shapes = list(job.out_shapes) + list(work.out_shapes)
        self.scratch = job.scratch

    def phases(self, ins, outs, scr):
        nj, no = len(self.job.inputs), len(self.job.out_shapes)
        base = self.job.phases(ins[:nj], outs[:no], scr)
        work = self.wk.work(ins[nj:], outs[no:])
        mixed = []
        for k, ph in enumerate(base):
            mixed.append(ph)
            if k < len(work):
                mixed.append(work[k])
        return mixed


def _run_job(job, name):
    ni, no = len(job.inputs), len(job.out_shapes)

    def body(*refs):
        for ph in job.phases(refs[:ni], refs[ni:ni + no], refs[ni + no:]):
            ph()

    return pl.pallas_call(
        body, name=name, in_specs=[ANY] * ni, out_specs=[ANY] * no, out_shape=job.out_shapes, scratch_shapes=job.scratch,
        compiler_params=pltpu.CompilerParams(vmem_limit_bytes=VMEM_LIMIT),
    )(*job.inputs)


def _hosted(body, *, name, grid, in_specs, out_specs, out_shape, scratch_shapes, args, sem, side=None):
    if side is None:
        return pl.pallas_call(body, name=name, grid=grid, in_specs=in_specs, out_specs=out_specs, out_shape=out_shape,
                              scratch_shapes=scratch_shapes, compiler_params=_cp(sem))(*args), None
    job = side
    ni, no, ns = len(in_specs), len(out_specs), len(scratch_shapes)
    ji, jo = len(job.inputs), len(job.out_shapes)
    n_steps = 1
    for extent in grid:
        n_steps *= extent

    def wrapped(*refs):
        own_in, refs = refs[:ni], refs[ni:]
        job_in, refs = refs[:ji], refs[ji:]
        own_out, refs = refs[:no], refs[no:]
        job_out, refs = refs[:jo], refs[jo:]
        own_scr, job_scr = refs[:ns], refs[ns:]
        step = 0
        for d, extent in enumerate(grid):
            step = step * extent + pl.program_id(d)
        phases = job.phases(job_in, job_out, job_scr)
        steps = [min(int(f * n_steps), n_steps - 1) for f in job.at] + [n_steps - 1]
        assert len(steps) == len(phases) and steps == sorted(steps)
        for at, ph in zip(steps, phases):
            pl.when(step == at)(ph)
        body(*own_in, *own_out, *own_scr)

    res = pl.pallas_call(
        wrapped, name=name, grid=grid, in_specs=list(in_specs) + [ANY] * ji, out_specs=list(out_specs) + [ANY] * jo,
        out_shape=list(out_shape) + list(job.out_shapes), scratch_shapes=list(scratch_shapes) + list(job.scratch),
        compiler_params=_cp(("arbitrary",) * len(grid)),
    )(*args, *job.inputs)
    return res[:no], res[no:]


def _proj_dw(xnt, dproj_sh, *, tm=512, tk=2048):
    d, s = xnt.shape
    tk = _tile(s, tk)
    nk = s // tk

    def body(a_ref, b_ref, o_ref, acc):
        def finish(r):
            o_ref[0] = r

        _accumulate(acc, _dot(a_ref[...], b_ref[...]), pl.program_id(2), nk, finish)

    return pl.pallas_call(
        body, name="proj_dw", grid=(N_CHIPS, d // tm, nk),
        in_specs=[pl.BlockSpec((tm, tk), lambda j, i, q: (i, q)), pl.BlockSpec((tk, W_IN_PAD), lambda j, i, q: (q, j))],
        out_specs=pl.BlockSpec((1, tm, W_IN_PAD), lambda j, i, q: (j, i, 0)),
        out_shape=jax.ShapeDtypeStruct((N_CHIPS, d, W_IN_PAD), F32), scratch_shapes=[pltpu.VMEM((tm, W_IN_PAD), F32)],
        compiler_params=_cp(("parallel", "parallel", "arbitrary")),
    )(xnt, dproj_sh)


def _proj_dx(dproj_sh, w_sh, *, tm=1024, side=None):
    s = dproj_sh.shape[0]
    d = w_sh.shape[1]
    tm = _tile(s, tm)

    def body(a_ref, b_ref, o_ref, acc):
        kk = pl.program_id(1)
        part = _dot_nt(a_ref[...], b_ref[0])

        @pl.when(kk == 0)
        def _():
            acc[...] = part

        @pl.when(kk > 0)
        def _():
            acc[...] += part

        @pl.when(kk == N_CHIPS - 1)
        def _():
            o_ref[...] = acc[...]

    own, extra = _hosted(
        body, name="proj_dx", grid=(s // tm, N_CHIPS),
        in_specs=[pl.BlockSpec((tm, W_IN_PAD), lambda i, q: (i, q)), pl.BlockSpec((1, d, W_IN_PAD), lambda i, q: (q, 0, 0))],
        out_specs=[pl.BlockSpec((tm, d), lambda i, q: (i, 0))],
        out_shape=[jax.ShapeDtypeStruct((s, d), F32)], scratch_shapes=[pltpu.VMEM((tm, d), F32)],
        args=(dproj_sh, w_sh), sem=("parallel", "arbitrary"), side=side)
    return own[0] if side is None else (own[0], extra)


def _up_dx(dup, w_sh, *, tm=1024):
    s = dup.shape[1]
    d, wsh = w_sh.shape[1:]
    tm = _tile(s, tm)

    def body(a_ref, b_ref, o_ref, acc):
        kk = pl.program_id(1)
        part = _dot_nt(a_ref[0], b_ref[0])

        @pl.when(kk == 0)
        def _():
            acc[...] = part

        @pl.when(kk > 0)
        def _():
            acc[...] += part

        @pl.when(kk == N_CHIPS - 1)
        def _():
            o_ref[...] = acc[...]

    return pl.pallas_call(
        body, name="up_dx", grid=(s // tm, N_CHIPS),
        in_specs=[pl.BlockSpec((1, tm, wsh), lambda i, q: (q >> 1, i, q & 1)), pl.BlockSpec((1, d, wsh), lambda i, q: (q, 0, 0))],
        out_specs=pl.BlockSpec((tm, d), lambda i, q: (i, 0)),
        out_shape=jax.ShapeDtypeStruct((s, d), F32), scratch_shapes=[pltpu.VMEM((tm, d), F32)],
        compiler_params=_cp(("parallel", "arbitrary")),
    )(dup, w_sh)


def _up_dw(hnt, dup, *, tk=2048):
    d, s = hnt.shape
    wsh = 2 * D_FF // N_CHIPS
    tk = _tile(s, tk)
    nk = s // tk

    def body(a_ref, b_ref, o_ref, acc):
        def finish(r):
            o_ref[0] = r

        _accumulate(acc, _dot(a_ref[...], b_ref[0]), pl.program_id(1), nk, finish)

    return pl.pallas_call(
        body, name="up_dw", grid=(N_CHIPS, nk),
        in_specs=[pl.BlockSpec((d, tk), lambda j, q: (0, q)), pl.BlockSpec((1, tk, wsh), lambda j, q: (j >> 1, q, j & 1))],
        out_specs=pl.BlockSpec((1, d, wsh), lambda j, q: (j, 0, 0)),
        out_shape=jax.ShapeDtypeStruct((N_CHIPS, d, wsh), F32), scratch_shapes=[pltpu.VMEM((d, wsh), F32)],
        compiler_params=_cp(("parallel", "arbitrary")),
    )(hnt, dup)


BIG_ROWS =(IN_DIM // 4, Q_DIM // 4, D_INNER // 4, D_MODEL // 4, 2 * D_FF // 4, D_FF // 4)
PACK_ROWS = 5376


def _pack_shards(parts):
    rows = [p.reshape(-1, D_MODEL) for p in parts]
    pad = PACK_ROWS - sum(BIG_ROWS)
    return jnp.concatenate(rows + [jnp.zeros((pad, D_MODEL), rows[0].dtype)], axis=0)


def _unpack_shards(buf):
    out, off = [], 0
    for n in BIG_ROWS:
        out.append(buf[off:off + n])
        off += n
    return out


def _assemble(srcs, col_map, *, name, tr=256):
    arrays, lead = [], []
    for src in srcs:
        arr, j = src if isinstance(src, tuple) else (src, None)
        if not any(arr is a for a in arrays):
            arrays.append(arr)
        lead.append(([i for i, a in enumerate(arrays) if a is arr][0], j))
    rows = arrays[0].shape[-2]
    tr = _tile(rows, tr)
    out_w = len(col_map)
    tiles = []
    for t in range(out_w // 128):
        runs = []
        for lane in range(128):
            ent = col_map[t * 128 + lane]
            key = None if ent is None else (ent[0], ent[1] // 128, (lane - ent[1]) % 128)
            if runs and runs[-1][0] == key:
                runs[-1][2] = lane + 1
            else:
                runs.append([key, lane, lane + 1])
        tiles.append(runs)

    def body(*refs):
        o_ref = refs[-1]
        lane = lax.broadcasted_iota(jnp.int32, (tr, 128), 1)
        for t, runs in enumerate(tiles):
            acc = jnp.zeros((tr, 128), F32)
            for key, a, b in runs:
                if key is None:
                    continue
                sid, ct, shift = key
                ai, j = lead[sid]
                cols = slice(ct * 128, (ct + 1) * 128)
                piece = (refs[ai][:, cols] if j is None else refs[ai][j, :, cols]).astype(F32)
                if shift:
                    piece = pltpu.roll(piece, shift, 1)
                acc = piece if (a, b) == (0, 128) else jnp.where((lane >= a) & (lane < b), piece, acc)
            o_ref[:, t * 128:(t + 1) * 128] = acc.astype(BF16)

    specs = [pl.BlockSpec((tr, a.shape[1]), lambda i: (i, 0)) if a.ndim == 2
             else pl.BlockSpec((a.shape[0], tr, a.shape[2]), lambda i: (0, i, 0)) for a in arrays]
    return pl.pallas_call(
        body, name=name, grid=(rows // tr,), in_specs=specs, out_specs=pl.BlockSpec((tr, out_w), lambda i: (i, 0)),
        out_shape=jax.ShapeDtypeStruct((rows, out_w), BF16), compiler_params=_cp(("parallel",)),
    )(*arrays)


def _permute_cols_in(w):
    pad = jnp.zeros((w.shape[0], PW - IN_DIM), w.dtype)
    return jnp.concatenate([w[:, :6656], w[:, 6688:], w[:, 6656:6688], pad], axis=1)


def _unpermute_cols_in(g):
    return jnp.concatenate([g[:, :6656], g[:, O_DT:O_DT + 32], g[:, 6656:O_DT]], axis=1)


SMALL = ("norm1_w", "b_gate", "attn_sinks", "ssd_conv_b", "dt_bias", "a_log", "d_skip", "ssd_norm_w", "norm2_w",
         "ffn_conv_b", "final_norm_w", "ssd_conv_w", "ffn_conv_w")


def _pad128(v):
    v = v.reshape(-1)
    return jnp.pad(v, (0, (-v.shape[0]) % 128))


def _pack_small(parts):
    flat = jnp.concatenate([_pad128(p) for p in parts])
    flat = jnp.pad(flat, (0, (-flat.shape[0]) % 1024))
    return flat.reshape(-1, 128)


def _unpack_small(buf, shapes):
    flat, out, off = buf.reshape(-1), [], 0
    for shp in shapes:
        n = 1
        for q in shp:
            n *= q
        out.append(flat[off:off + n].reshape(shp))
        off += n + (-n) % 128
    return out


def _vec128(v):
    return jnp.pad(v.reshape(1, -1), ((0, 0), (0, 128 - v.shape[-1])))


def kernel(x, norm1_w, w_in, b_gate, attn_sinks, w_attn_o, ssd_conv_w, ssd_conv_b, dt_bias, a_log, d_skip, ssd_norm_w, w_ssd_o, w_out, norm2_w, w_up, ffn_conv_w, ffn_conv_b, w_down, final_norm_w, loss_target, m_norm1_w, m_w_in, m_b_gate, m_attn_sinks, m_w_attn_o, m_ssd_conv_w, m_ssd_conv_b, m_dt_bias, m_a_log, m_d_skip, m_ssd_norm_w, m_w_ssd_o, m_w_out, m_norm2_w, m_w_up, m_ffn_conv_w, m_ffn_conv_b, m_w_down, m_final_norm_w, v_norm1_w, v_w_in, v_b_gate, v_attn_sinks, v_w_attn_o, v_ssd_conv_w, v_ssd_conv_b, v_dt_bias, v_a_log, v_d_skip, v_ssd_norm_w, v_w_ssd_o, v_w_out, v_norm2_w, v_w_up, v_ffn_conv_w, v_ffn_conv_b, v_w_down, v_final_norm_w):
    ix, iy, ic = lax.axis_index("x"), lax.axis_index("y"), lax.axis_index("c")
    chip = 2 * ix + iy
    x2 = x[0]
    tgt = loss_target[0]
    s = x2.shape[0]

    wsh = IN_DIM // N_CHIPS
    big_shards = dict(w_in=jnp.pad(w_in[0], ((0, 0), (0, W_IN_PAD - wsh))), w_attn_o=w_attn_o[0], w_ssd_o=w_ssd_o[0],
                      w_out=w_out[0], w_up=w_up[0], w_down=w_down[0])
    gathered = {}
    (gathered["w_in"],) = _run_job(_GatherJob(("w_in",), [big_shards["w_in"]]), "gather_w_in")
    early = ("w_attn_o", "w_ssd_o", "w_out")
    gather_early = _GatherJob(early, [big_shards[n] for n in early], at=(0.0, 0.5, 0.8))
    gather_up = _GatherJob(("w_up",), [big_shards["w_up"]], at=(0.0, 0.55, 0.85))
    gather_down = _GatherJob(("w_down",), [big_shards["w_down"]], at=(0.0, 0.5, 0.8))
    gw = gathered["w_in"]
    perm = list(range(O_GA)) + list(range(O_GA + N_SSD_HEADS, IN_DIM)) + list(range(O_GA, O_GA + N_SSD_HEADS))
    w_in_p = _assemble([(gw, j) for j in range(N_CHIPS)], [divmod(o, wsh) for o in perm] + [None] * (PW - IN_DIM),
                       name="w_in_assemble")
    small_sh = _pack_small([ssd_conv_w[0], ffn_conv_w[0]])
    small_all = _all_gather_small(small_sh)
    sc_parts = [_unpack_small(small_all[j], [(4, XBC_DIM // 4), (3, 2 * D_FF // 4)]) for j in range(N_CHIPS)]
    ssd_cw = jnp.concatenate([p[0] for p in sc_parts], axis=1)
    ffn_cw = jnp.concatenate([p[1] for p in sc_parts], axis=1)

    sinks128 = _vec128(attn_sinks)
    dtb128, alog128, dskip128 = _vec128(dt_bias), _vec128(a_log), _vec128(d_skip)

    xn, xnt = _rms_fwd(x2, norm1_w, name="norm1_fwd", with_t=True)
    proj, got = _mm(xn, w_in_p, name="proj_fwd", tn=1280, side=gather_early)
    gathered.update(zip(early, got))
    qkvt = _mm(w_in_p[:, :O_Z], xnt, name="qkv_fwd", ta=True)
    attn_pre, (gathered["w_up"],) = _attn_fwd(qkvt, sinks128, side=gather_up)
    xbc = _ssd_conv_fwd(proj, ssd_cw, ssd_conv_b)
    (y_ssd, hprev), (gathered["w_down"],) = _ssd_fwd(xbc, proj, dtb128, alog128, dskip128, side=gather_down)
    full = {n: gathered[n].reshape(-1, D_MODEL) for n in ("w_attn_o", "w_ssd_o", "w_out", "w_down")}
    full["w_up"] = gathered["w_up"]
    attn = _mm(attn_pre, full["w_attn_o"], name="attn_o_fwd", ta=True)
    yn = _gate_norm_fwd(y_ssd, proj, ssd_norm_w)
    ssd_out = _mm(yn, full["w_ssd_o"], name="ssd_o_fwd")
    merged = _merge_fwd(proj, b_gate, attn, ssd_out)
    h1 = _mm(merged, full["w_out"], name="out_fwd", resid=x2)
    hn, hnt = _rms_fwd(h1, norm2_w, name="norm2_fwd", with_t=True)
    up = _mm(hn, full["w_up"], name="up_fwd")
    act = _ffn_act_fwd(up, ffn_cw, ffn_conv_b)
    h2 = _mm(act, full["w_down"], name="down_fwd", resid=h1, tk=1408)

    dh2, loss_blk, g_final = _loss_bwd(h2, tgt, final_norm_w.reshape(1, -1))
    dact = _mm(dh2, full["w_down"], name="down_dx", tb=True, tn=1408)
    g_down = _mm(act, dh2, name="down_dw", ta=True, tm=1408)
    dup, g_ffn_cw, g_ffn_cb = _ffn_act_bwd(dact, up, ffn_cw, ffn_conv_b)
    dhn = _up_dx(dup, full["w_up"])
    g_up = _up_dw(hnt, dup)
    dh1, g_norm2 = _rms_bwd(dhn, h1, norm2_w, dh2, name="norm2_bwd")
    dmerged = _mm(dh1, full["w_out"], name="out_dx", tb=True)
    g_out = _mm(merged, dh1, name="out_dw", ta=True)
    dattn, dssd_out, dga, dgs, g_ba, g_bs = _merge_bwd(dmerged, proj, b_gate, attn, ssd_out)
    dyn = _mm(dssd_out, full["w_ssd_o"], name="ssd_o_dx", tb=True)
    g_ssd_o = _mm(yn, dssd_out, name="ssd_o_dw", ta=True)
    dy_ssd, dz, g_ssd_norm = _gate_norm_bwd(dyn, y_ssd, proj, ssd_norm_w)
    slot = lambda g: g.reshape(N_CHIPS, -1, D_MODEL)
    big_grads = {}
    red = ("w_down", "w_up")
    (dxbc, ddt, dvec), got = _ssd_bwd(xbc, proj, dtb128, alog128, dskip128, hprev, dy_ssd,
                                      side=_ReduceJob(red, [slot(g_down), g_up], at=(0.0, 0.15, 0.6, 0.9)))
    big_grads.update(zip(red, got))
    dxbc_raw, g_ssd_cw, g_ssd_cb = _ssd_conv_bwd(dxbc, proj, ssd_cw, ssd_conv_b)
    dattn_pre = _mm(full["w_attn_o"], dattn, name="attn_o_dx", tb=True)
    g_attn_o = _mm(attn_pre, dattn, name="attn_o_dw")
    red = ("w_out", "w_ssd_o", "w_attn_o")
    (dq, dk, dv, dsk), got = _attn_bwd(qkvt, sinks128, attn_pre, dattn_pre,
                                       side=_ReduceJob(red, [slot(g_out), slot(g_ssd_o), slot(g_attn_o)],
                                                       at=(0.0, 0.1, 0.4, 0.7)))
    big_grads.update(zip(red, got))
    pieces = [(dq.T, Q_DIM), (dk.T, KV_DIM), (dv.T, KV_DIM), (dz, D_INNER), (dxbc_raw, XBC_DIM), (ddt, N_SSD_HEADS),
              (dga, D_MODEL), (dgs, D_MODEL)]
    orig = [(i, c) for i, (_, w) in enumerate(pieces) for c in range(w)]
    dproj_sh = _assemble([p for p, _ in pieces],
                         [orig[j * wsh + c] if c < wsh else None for j in range(N_CHIPS) for c in range(W_IN_PAD)],
                         name="dproj_assemble")
    g_in = _proj_dw(xnt, dproj_sh)
    dxn, got = _proj_dx(dproj_sh, gathered["w_in"], side=_ReduceJob(("w_in",), [g_in], at=(0.0, 0.15, 0.75, 0.95)))
    big_grads["w_in"] = got[0]
    dx, g_norm1 = _rms_bwd(dxn, x2, norm1_w, dh1, name="norm1_bwd")


    small_g = dict(
        norm1_w=g_norm1, b_gate=jnp.concatenate([g_ba, g_bs], axis=1), attn_sinks=dsk[0:1, :16], ssd_conv_b=g_ssd_cb,
        dt_bias=dvec[0:1, :32], a_log=dvec[1:2, :32], d_skip=dvec[2:3, :32], ssd_norm_w=g_ssd_norm, norm2_w=g_norm2,
        ffn_conv_b=jnp.concatenate([g_ffn_cb[0], g_ffn_cb[1]], axis=1), final_norm_w=g_final, ssd_conv_w=g_ssd_cw,
        ffn_conv_w=jnp.concatenate([g_ffn_cw[0], g_ffn_cw[1]], axis=1))
    small_buf = _pack_small([small_g[n] for n in SMALL] + [loss_blk])
    small_sum = _all_reduce_small(small_buf)
    small_shapes = [(1, D_MODEL), (1, 2 * D_MODEL), (1, 16), (1, XBC_DIM), (1, 32), (1, 32), (1, 32), (1, D_INNER),
                    (1, D_MODEL), (1, 2 * D_FF), (D_MODEL,), (4, XBC_DIM), (3, 2 * D_FF), (1, 128)]
    small_list = _unpack_small(small_sum, small_shapes)
    loss = small_list[-1][0, 0]
    grads = dict(zip(SMALL, small_list[:-1]))
    grads["ssd_conv_w"] = lax.dynamic_slice_in_dim(grads["ssd_conv_w"], chip * (XBC_DIM // 4), XBC_DIM // 4, axis=1)
    grads["ffn_conv_w"] = lax.dynamic_slice_in_dim(grads["ffn_conv_w"], chip * (2 * D_FF // 4), 2 * D_FF // 4, axis=1)
    grads.update(big_grads)

    weights = dict(norm1_w=norm1_w, w_in=w_in, b_gate=b_gate, attn_sinks=attn_sinks, w_attn_o=w_attn_o, ssd_conv_w=ssd_conv_w,
                   ssd_conv_b=ssd_conv_b, dt_bias=dt_bias, a_log=a_log, d_skip=d_skip, ssd_norm_w=ssd_norm_w, w_ssd_o=w_ssd_o,
                   w_out=w_out, norm2_w=norm2_w, w_up=w_up, ffn_conv_w=ffn_conv_w, ffn_conv_b=ffn_conv_b, w_down=w_down,
                   final_norm_w=final_norm_w)
    ms = dict(norm1_w=m_norm1_w, w_in=m_w_in, b_gate=m_b_gate, attn_sinks=m_attn_sinks, w_attn_o=m_w_attn_o,
              ssd_conv_w=m_ssd_conv_w, ssd_conv_b=m_ssd_conv_b, dt_bias=m_dt_bias, a_log=m_a_log, d_skip=m_d_skip,
              ssd_norm_w=m_ssd_norm_w, w_ssd_o=m_w_ssd_o, w_out=m_w_out, norm2_w=m_norm2_w, w_up=m_w_up,
              ffn_conv_w=m_ffn_conv_w, ffn_conv_b=m_ffn_conv_b, w_down=m_w_down, final_norm_w=m_final_norm_w)
    vs = dict(norm1_w=v_norm1_w, w_in=v_w_in, b_gate=v_b_gate, attn_sinks=v_attn_sinks, w_attn_o=v_w_attn_o,
              ssd_conv_w=v_ssd_conv_w, ssd_conv_b=v_ssd_conv_b, dt_bias=v_dt_bias, a_log=v_a_log, d_skip=v_d_skip,
              ssd_norm_w=v_ssd_norm_w, w_ssd_o=v_w_ssd_o, w_out=v_w_out, norm2_w=v_norm2_w, w_up=v_w_up,
              ffn_conv_w=v_ffn_conv_w, ffn_conv_b=v_ffn_conv_b, w_down=v_w_down, final_norm_w=v_final_norm_w)
    order = list(weights)
    deltas, new_m, new_v = {}, {}, {}
    for n in BIG:
        shp = weights[n].shape
        res = _adamw(weights[n][0], grads[n], ms[n][0], vs[n][0], name="adamw_" + n)
        deltas[n], new_m[n], new_v[n], grads[n] = (a.reshape(shp) for a in res)
    smalls = [n for n in order if n not in BIG]
    as2d = lambda a: a.reshape(-1, a.shape[-1])
    res = _adamw_many(*[[as2d(src[n][0] if src[n].ndim == 3 else src[n]) for n in smalls] for src in (weights, grads, ms, vs)])
    for i, n in enumerate(smalls):
        deltas[n], new_m[n], new_v[n] = (res[q * len(smalls) + i].reshape(weights[n].shape) for q in range(3))
    out_grads = [grads[n].reshape(weights[n].shape) for n in order]
    return (loss, dx[None], *out_grads, *[deltas[n] for n in order], *[new_m[n] for n in order], *[new_v[n] for n in order])
```

```python
import functools

import jax
import jax.numpy as jnp
from jax import lax
from jax.experimental import pallas as pl
from jax.experimental.pallas import tpu as pltpu

F32 = jnp.float32
BF16 = jnp.bfloat16
HI = lax.Precision.HIGHEST

D_MODEL = 1024
Q_DIM = 1024
KV_DIM = 256
D_INNER = 2048
BC_DIM = 512
XBC_DIM = 3072
N_SSD_HEADS = 32
D_FF = 2816
IN_DIM = 8736
BLK = 128
EPS = 1e-5
NEG = -1e30

O_Q, O_K, O_V, O_Z, O_X, O_GA, O_GS, O_DT = 0, 1024, 1280, 1536, 3584, 6656, 7680, 8704
PW = 8960

ADAM_LR, ADAM_B1, ADAM_B2, ADAM_EPS, ADAM_WD, ADAM_STEP = 0.001, 0.9, 0.999, 1e-08, 0.01, 10

VMEM_LIMIT = 52 * 1024 * 1024
MESH = pl.DeviceIdType.MESH


def _cp(sem=None):
    return pltpu.CompilerParams(dimension_semantics=sem, vmem_limit_bytes=VMEM_LIMIT)


def _dot(a, b, prec=None):
    return jnp.dot(a, b, preferred_element_type=F32, precision=prec)


def _dot_nt(a, b, prec=None):
    return lax.dot_general(a, b, (((1,), (1,)), ((), ())), preferred_element_type=F32, precision=prec)


def _dot_tn(a, b, prec=None):
    return lax.dot_general(a, b, (((0,), (0,)), ((), ())), preferred_element_type=F32, precision=prec)


def _sigmoid(x):
    return 0.5 * jnp.tanh(0.5 * x) + 0.5


def _tile(n, want):
    t = min(n, want)
    while n % t:
        t -= 128
    return t


def _accumulate(acc, part, kk, nk, finish):
    if nk == 1:
        finish(part)
        return

    @pl.when(kk == 0)
    def _():
        acc[...] = part

    @pl.when(kk > 0)
    def _():
        acc[...] += part

    @pl.when(kk == nk - 1)
    def _():
        finish(acc[...])


def _mm(a, b, *, name, ta=False, tb=False, out_dtype=F32, resid=None, tm=1024, tn=1024, tk=1024, side=None):
    m, k = (a.shape[1], a.shape[0]) if ta else a.shape
    slots = b.ndim == 3
    if slots:
        n = b.shape[1] if tb else b.shape[0] * b.shape[2]
        tn, tk = (tn, b.shape[2]) if tb else (b.shape[2], tk)
    else:
        n = b.shape[0] if tb else b.shape[1]
    tm, tn, tk = _tile(m, tm), _tile(n, tn), _tile(k, tk)
    nk = k // tk
    dn = (((0 if ta else 1,), (1 if tb else 0,)), ((), ()))

    def body(*refs):
        if resid is None:
            a_ref, b_ref, o_ref, acc = refs
        else:
            a_ref, b_ref, r_ref, o_ref, acc = refs
        kk = pl.program_id(2)
        bv = b_ref[0] if slots else b_ref[...]
        part = lax.dot_general(a_ref[...].astype(BF16), bv.astype(BF16), dn, preferred_element_type=F32)

        def finish(r):
            if resid is not None:
                r = r + r_ref[...]
            o_ref[...] = r.astype(out_dtype)

        _accumulate(acc, part, kk, nk, finish)

    a_spec = pl.BlockSpec((tk, tm), lambda i, j, q: (q, i)) if ta else pl.BlockSpec((tm, tk), lambda i, j, q: (i, q))
    if slots:
        b_spec = (pl.BlockSpec((1, tn, tk), lambda i, j, q: (q, j, 0)) if tb
                  else pl.BlockSpec((1, tk, tn), lambda i, j, q: (j, q, 0)))
    else:
        b_spec = pl.BlockSpec((tn, tk), lambda i, j, q: (j, q)) if tb else pl.BlockSpec((tk, tn), lambda i, j, q: (q, j))
    o_spec = pl.BlockSpec((tm, tn), lambda i, j, q: (i, j))
    ins, specs = [a, b], [a_spec, b_spec]
    if resid is not None:
        ins.append(resid)
        specs.append(o_spec)
    own, extra = _hosted(
        body, name=name, grid=(m // tm, n // tn, nk), in_specs=specs, out_specs=[o_spec],
        out_shape=[jax.ShapeDtypeStruct((m, n), out_dtype)], scratch_shapes=[pltpu.VMEM((tm, tn), F32)],
        args=ins, sem=("parallel", "parallel", "arbitrary"), side=side)
    return own[0] if side is None else (own[0], extra)


def _rms_fwd(x, w, *, name, tm=512, with_t=False, side=None):
    s, d = x.shape
    tm = _tile(s, tm)

    def body(x_ref, w_ref, o_ref, *t_ref):
        xv = x_ref[...]
        r = lax.rsqrt(jnp.mean(xv * xv, axis=-1, keepdims=True) + EPS)
        y = (xv * r) * w_ref[...]
        o_ref[...] = y.astype(BF16)
        if with_t:
            t_ref[0][...] = y.T.astype(BF16)

    row = pl.BlockSpec((tm, d), lambda i: (i, 0))
    res, extra = _hosted(
        body, name=name, grid=(s // tm,), in_specs=[row, pl.BlockSpec((1, d), lambda i: (0, 0))],
        out_specs=[row] + [pl.BlockSpec((d, tm), lambda i: (0, i))] * with_t,
        out_shape=[jax.ShapeDtypeStruct((s, d), BF16)] + [jax.ShapeDtypeStruct((d, s), BF16)] * with_t,
        scratch_shapes=[], args=(x, w), sem=("parallel",), side=side)
    res = res if with_t else res[0]
    return res if side is None else (res, extra)


def _rms_bwd(dy, x, w, resid, *, name, tm=512):
    s, d = x.shape
    tm = _tile(s, tm)

    def body(dy_ref, x_ref, w_ref, r_ref, dx_ref, dw_ref):
        i = pl.program_id(0)
        xv = x_ref[...]
        r = lax.rsqrt(jnp.mean(xv * xv, axis=-1, keepdims=True) + EPS)
        xh = xv * r
        dyv = dy_ref[...]
        g = dyv * w_ref[...]
        dx_ref[...] = r_ref[...] + r * (g - xh * jnp.mean(g * xh, axis=-1, keepdims=True))
        part = jnp.sum(dyv * xh, axis=0, keepdims=True)

        @pl.when(i == 0)
        def _():
            dw_ref[...] = part

        @pl.when(i > 0)
        def _():
            dw_ref[...] += part

    row = pl.BlockSpec((tm, d), lambda i: (i, 0))
    vec = pl.BlockSpec((1, d), lambda i: (0, 0))
    return pl.pallas_call(
        body, name=name, grid=(s // tm,), in_specs=[row, row, vec, row], out_specs=[row, vec],
        out_shape=[jax.ShapeDtypeStruct((s, d), F32), jax.ShapeDtypeStruct((1, d), F32)],
        compiler_params=_cp(("arbitrary",)),
    )(dy, x, w, resid)


def _loss_bwd(h2, tgt, wf, *, tm=512):
    s, d = h2.shape
    tm = _tile(s, tm)

    def body(h_ref, t_ref, w_ref, dh_ref, loss_ref, dw_ref):
        i = pl.program_id(0)
        hv = h_ref[...]
        r = lax.rsqrt(jnp.mean(hv * hv, axis=-1, keepdims=True) + EPS)
        xh = hv * r
        wv = w_ref[...]
        e = xh * wv - t_ref[...]
        lpart = 0.5 * jnp.sum(jnp.mean(e * e, axis=-1, keepdims=True), axis=0, keepdims=True)
        dout = e * (1.0 / d)
        g = dout * wv
        dh_ref[...] = r * (g - xh * jnp.mean(g * xh, axis=-1, keepdims=True))
        part = jnp.sum(dout * xh, axis=0, keepdims=True)
        lrow = jnp.broadcast_to(lpart, (1, 128))

        @pl.when(i == 0)
        def _():
            dw_ref[...] = part
            loss_ref[...] = lrow

        @pl.when(i > 0)
        def _():
            dw_ref[...] += part
            loss_ref[...] += lrow

    row = pl.BlockSpec((tm, d), lambda i: (i, 0))
    vec = pl.BlockSpec((1, d), lambda i: (0, 0))
    return pl.pallas_call(
        body, name="loss_bwd", grid=(s // tm,), in_specs=[row, row, vec],
        out_specs=[row, pl.BlockSpec((1, 128), lambda i: (0, 0)), vec],
        out_shape=[jax.ShapeDtypeStruct((s, d), F32), jax.ShapeDtypeStruct((1, 128), F32),
                   jax.ShapeDtypeStruct((1, d), F32)],
        compiler_params=_cp(("arbitrary",)),
    )(h2, tgt, wf)


def _attn_mask(n):
    si = lax.broadcasted_iota(jnp.int32, (2 * BLK, 4 * BLK), 0)
    qi = lax.broadcasted_iota(jnp.int32, (2 * BLK, 4 * BLK), 1) & (BLK - 1)
    dist = BLK + qi - si
    kpos = n * BLK - BLK + si
    return (dist >= 0) & (dist < BLK) & (kpos >= 0)


def _attn_probs(q_ref, kc_ref, kp_ref, sk_ref, kvh, valid):
    rows = slice(kvh * 64, (kvh + 1) * 64)
    kt = jnp.concatenate([kp_ref[rows, :], kc_ref[rows, :]], axis=1).astype(BF16)
    qt = jnp.concatenate([q_ref[(kvh * 4 + g) * 64:(kvh * 4 + g + 1) * 64, :] for g in range(4)], axis=1).astype(BF16)
    s = _dot_tn(kt, qt) * 0.125
    s = jnp.where(valid, s, NEG)
    head = lax.broadcasted_iota(jnp.int32, (1, 4 * BLK), 1) >> 7
    sink = jnp.zeros((1, 4 * BLK), F32)
    for g in range(4):
        sink = jnp.where(head == g, sk_ref[0:1, kvh * 4 + g:kvh * 4 + g + 1], sink)
    m = jnp.maximum(jnp.max(s, axis=0, keepdims=True), sink)
    p = jnp.where(valid, jnp.exp(s - m), 0.0)
    es = jnp.exp(sink - m)
    inv = 1.0 / (jnp.sum(p, axis=0, keepdims=True) + es)
    return qt, kt, p * inv, es * inv


def _attn_in_specs(cur, prev):
    return [pl.BlockSpec((Q_DIM, BLK), lambda n: (0, cur(n))),
            pl.BlockSpec((KV_DIM, BLK), lambda n: (O_K // KV_DIM, cur(n))),
            pl.BlockSpec((KV_DIM, BLK), lambda n: (O_K // KV_DIM, prev(n))),
            pl.BlockSpec((KV_DIM, BLK), lambda n: (O_V // KV_DIM, cur(n))),
            pl.BlockSpec((KV_DIM, BLK), lambda n: (O_V // KV_DIM, prev(n))),
            pl.BlockSpec((1, 128), lambda n: (0, 0))]


def _attn_fwd(qkvt, sinks, side=None):
    s = qkvt.shape[1]
    nb = s // BLK

    def body(q_ref, kc_ref, kp_ref, vc_ref, vp_ref, sk_ref, o_ref):
        valid = _attn_mask(pl.program_id(0))
        for kvh in range(4):
            rows = slice(kvh * 64, (kvh + 1) * 64)
            _, _, probs, _ = _attn_probs(q_ref, kc_ref, kp_ref, sk_ref, kvh, valid)
            vt = jnp.concatenate([vp_ref[rows, :], vc_ref[rows, :]], axis=1).astype(BF16)
            o = _dot(vt, probs.astype(BF16))
            for g in range(4):
                h = kvh * 4 + g
                o_ref[h * 64:(h + 1) * 64, :] = o[:, g * BLK:(g + 1) * BLK].astype(BF16)

    own, extra = _hosted(
        body, name="attn_fwd", grid=(nb,), in_specs=_attn_in_specs(lambda n: n, lambda n: jnp.maximum(n - 1, 0)),
        out_specs=[pl.BlockSpec((Q_DIM, BLK), lambda n: (0, n))],
        out_shape=[jax.ShapeDtypeStruct((Q_DIM, s), BF16)], scratch_shapes=[],
        args=(qkvt, qkvt, qkvt, qkvt, qkvt, sinks), sem=("parallel",), side=side)
    return own[0] if side is None else (own[0], extra)


def _attn_bwd(qkvt, sinks, o, do, side=None):
    s = qkvt.shape[1]
    nb = s // BLK

    def body(q_ref, kc_ref, kp_ref, vc_ref, vp_ref, sk_ref, o_ref, do_ref, dq_ref, dk_ref, dv_ref, dsk_ref, ck, cv, nk, nv):
        n = pl.program_id(0)

        @pl.when(n == 0)
        def _():
            ck[...] = jnp.zeros_like(ck)
            cv[...] = jnp.zeros_like(cv)
            dsk_ref[...] = jnp.zeros_like(dsk_ref)

        @pl.when(n < nb)
        def _():
            valid = _attn_mask(n)
            lane = lax.broadcasted_iota(jnp.int32, (1, 128), 1)
            dsk = jnp.zeros((1, 128), F32)
            for kvh in range(4):
                rows = slice(kvh * 64, (kvh + 1) * 64)
                qt, kt, probs, psink = _attn_probs(q_ref, kc_ref, kp_ref, sk_ref, kvh, valid)
                vt = jnp.concatenate([vp_ref[rows, :], vc_ref[rows, :]], axis=1).astype(BF16)
                heads = [slice((kvh * 4 + g) * 64, (kvh * 4 + g + 1) * 64) for g in range(4)]
                dot = jnp.concatenate([do_ref[hh, :] for hh in heads], axis=1)
                ot = jnp.concatenate([o_ref[hh, :] for hh in heads], axis=1).astype(F32)
                delta = jnp.sum(dot * ot, axis=0, keepdims=True)
                dot16 = dot.astype(BF16)
                dp = _dot_tn(vt, dot16)
                ds = (probs * (dp - delta) * 0.125).astype(BF16)
                dqt = _dot(kt, ds)
                nk[rows, :] = _dot_nt(qt, ds)
                nv[rows, :] = _dot_nt(dot16, probs.astype(BF16))
                sd = psink * delta
                for g in range(4):
                    dq_ref[heads[g], :] = dqt[:, g * BLK:(g + 1) * BLK].astype(BF16)
                    val = -jnp.sum(sd[:, g * BLK:(g + 1) * BLK], axis=1, keepdims=True)
                    dsk = dsk + jnp.where(lane == kvh * 4 + g, val, 0.0)
            dsk_ref[0:1, :] += dsk
            dk_ref[...] = (ck[...] + nk[:, :BLK]).astype(BF16)
            dv_ref[...] = (cv[...] + nv[:, :BLK]).astype(BF16)
            ck[...] = nk[:, BLK:]
            cv[...] = nv[:, BLK:]

        @pl.when(n == nb)
        def _():
            dk_ref[...] = ck[...].astype(BF16)
            dv_ref[...] = cv[...].astype(BF16)

    cur = lambda n: jnp.minimum(n, nb - 1)
    prev = lambda n: jnp.maximum(jnp.minimum(n, nb - 1) - 1, 0)
    outb = lambda n: jnp.maximum(n - 1, 0)
    own, extra = _hosted(
        body, name="attn_bwd", grid=(nb + 1,),
        in_specs=_attn_in_specs(cur, prev) + [pl.BlockSpec((Q_DIM, BLK), lambda n: (0, cur(n))),
                                              pl.BlockSpec((Q_DIM, BLK), lambda n: (0, cur(n)))],
        out_specs=[pl.BlockSpec((Q_DIM, BLK), lambda n: (0, cur(n))),
                   pl.BlockSpec((KV_DIM, BLK), lambda n: (0, outb(n))),
                   pl.BlockSpec((KV_DIM, BLK), lambda n: (0, outb(n))),
                   pl.BlockSpec((8, 128), lambda n: (0, 0))],
        out_shape=[jax.ShapeDtypeStruct((Q_DIM, s), BF16), jax.ShapeDtypeStruct((KV_DIM, s), BF16),
                   jax.ShapeDtypeStruct((KV_DIM, s), BF16), jax.ShapeDtypeStruct((8, 128), F32)],
        scratch_shapes=[pltpu.VMEM((KV_DIM, BLK), F32)] * 2 + [pltpu.VMEM((KV_DIM, 2 * BLK), F32)] * 2,
        args=(qkvt, qkvt, qkvt, qkvt, qkvt, sinks, o, do), sem=("arbitrary",), side=side)
    return own if side is None else (own, extra)


def _shift_down(x, j):
    if j == 0:
        return x
    row = lax.broadcasted_iota(jnp.int32, x.shape, 0)
    return jnp.where(row >= j, pltpu.roll(x, j, 0), 0.0)


def _shift_up(x, j):
    if j == 0:
        return x
    s = x.shape[0]
    row = lax.broadcasted_iota(jnp.int32, x.shape, 0)
    return jnp.where(row < s - j, pltpu.roll(x, s - j, 0), 0.0)


def _conv(x, w_ref, b_ref):
    kk = w_ref.shape[0]
    y = _shift_down(x, kk - 1) * w_ref[0:1, :]
    for q in range(1, kk):
        y = y + _shift_down(x, kk - 1 - q) * w_ref[q:q + 1, :]
    return y + b_ref[...]


def _conv_bwd(dy, x, w_ref, dx_dtype):
    kk = w_ref.shape[0]
    dx = _shift_up(dy, kk - 1) * w_ref[0:1, :]
    dws = [jnp.sum(dy * _shift_down(x, kk - 1), axis=0, keepdims=True)]
    for q in range(1, kk):
        dx = dx + _shift_up(dy, kk - 1 - q) * w_ref[q:q + 1, :]
        dws.append(jnp.sum(dy * _shift_down(x, kk - 1 - q), axis=0, keepdims=True))
    return dx.astype(dx_dtype), dws, jnp.sum(dy, axis=0, keepdims=True)


def _dsilu(y, sg):
    return sg * (1.0 + y * (1.0 - sg))


CT = 256


def _ssd_conv_fwd(proj, w, b):
    s = proj.shape[0]

    def body(x_ref, w_ref, b_ref, o_ref):
        y = _conv(x_ref[...], w_ref, b_ref)
        o_ref[...] = y * _sigmoid(y)

    return pl.pallas_call(
        body, name="ssd_conv_fwd", grid=(XBC_DIM // CT,),
        in_specs=[pl.BlockSpec((s, CT), lambda i: (0, O_X // CT + i)), pl.BlockSpec((4, CT), lambda i: (0, i)),
                  pl.BlockSpec((1, CT), lambda i: (0, i))],
        out_specs=pl.BlockSpec((s, CT), lambda i: (0, i)),
        out_shape=jax.ShapeDtypeStruct((s, XBC_DIM), F32), compiler_params=_cp(("parallel",)),
    )(proj, w, b)


def _ssd_conv_bwd(dact, proj, w, b):
    s = proj.shape[0]

    def body(d_ref, x_ref, w_ref, b_ref, dx_ref, dw_ref, db_ref):
        x = x_ref[...]
        y = _conv(x, w_ref, b_ref)
        dy = d_ref[...] * _dsilu(y, _sigmoid(y))
        dx, dws, db = _conv_bwd(dy, x, w_ref, BF16)
        dx_ref[...] = dx
        for q in range(4):
            dw_ref[q:q + 1, :] = dws[q]
        db_ref[...] = db

    return pl.pallas_call(
        body, name="ssd_conv_bwd", grid=(XBC_DIM // CT,),
        in_specs=[pl.BlockSpec((s, CT), lambda i: (0, i)), pl.BlockSpec((s, CT), lambda i: (0, O_X // CT + i)),
                  pl.BlockSpec((4, CT), lambda i: (0, i)), pl.BlockSpec((1, CT), lambda i: (0, i))],
        out_specs=[pl.BlockSpec((s, CT), lambda i: (0, i)), pl.BlockSpec((4, CT), lambda i: (0, i)),
                   pl.BlockSpec((1, CT), lambda i: (0, i))],
        out_shape=[jax.ShapeDtypeStruct((s, XBC_DIM), BF16), jax.ShapeDtypeStruct((4, XBC_DIM), F32),
                   jax.ShapeDtypeStruct((1, XBC_DIM), F32)],
        compiler_params=_cp(("parallel",)),
    )(dact, proj, w, b)


NFT = D_FF // CT


def _ffn_act_fwd(up, w, b):
    s = up.shape[0]

    def body(v_ref, g_ref, wv_ref, wg_ref, bv_ref, bg_ref, o_ref):
        val = _conv(v_ref[...], wv_ref, bv_ref)
        gt = _conv(g_ref[...], wg_ref, bg_ref)
        o_ref[...] = ((gt * _sigmoid(gt)) * val).astype(BF16)

    col = lambda off: (lambda i: (0, off + i))
    return pl.pallas_call(
        body, name="ffn_act_fwd", grid=(NFT,),
        in_specs=[pl.BlockSpec((s, CT), col(0)), pl.BlockSpec((s, CT), col(NFT)),
                  pl.BlockSpec((3, CT), col(0)), pl.BlockSpec((3, CT), col(NFT)),
                  pl.BlockSpec((1, CT), col(0)), pl.BlockSpec((1, CT), col(NFT))],
        out_specs=pl.BlockSpec((s, CT), col(0)),
        out_shape=jax.ShapeDtypeStruct((s, D_FF), BF16), compiler_params=_cp(("parallel",)),
    )(up, up, w, w, b, b)


def _ffn_act_bwd(dact, up, w, b):
    s = up.shape[0]

    def body(d_ref, v_ref, g_ref, wv_ref, wg_ref, bv_ref, bg_ref, dx_ref, dw_ref, db_ref):
        xv, xg = v_ref[...], g_ref[...]
        val = _conv(xv, wv_ref, bv_ref)
        gt = _conv(xg, wg_ref, bg_ref)
        sg = _sigmoid(gt)
        d = d_ref[...]
        for half, (dy, x, w_ref) in enumerate(((d * (gt * sg), xv, wv_ref), (d * val * _dsilu(gt, sg), xg, wg_ref))):
            dx, dws, db = _conv_bwd(dy, x, w_ref, BF16)
            dx_ref[half] = dx
            for q in range(3):
                dw_ref[half, q:q + 1, :] = dws[q]
            db_ref[half] = db

    col = lambda off: (lambda i: (0, off + i))
    both = lambda i: (0, 0, i)
    return pl.pallas_call(
        body, name="ffn_act_bwd", grid=(NFT,),
        in_specs=[pl.BlockSpec((s, CT), col(0)), pl.BlockSpec((s, CT), col(0)), pl.BlockSpec((s, CT), col(NFT)),
                  pl.BlockSpec((3, CT), col(0)), pl.BlockSpec((3, CT), col(NFT)),
                  pl.BlockSpec((1, CT), col(0)), pl.BlockSpec((1, CT), col(NFT))],
        out_specs=[pl.BlockSpec((2, s, CT), both), pl.BlockSpec((2, 3, CT), both), pl.BlockSpec((2, 1, CT), both)],
        out_shape=[jax.ShapeDtypeStruct((2, s, D_FF), BF16), jax.ShapeDtypeStruct((2, 3, D_FF), F32),
                   jax.ShapeDtypeStruct((2, 1, D_FF), F32)],
        compiler_params=_cp(("parallel",)),
    )(dact, up, up, w, w, b, b)


def _expand_mat():
    r = lax.broadcasted_iota(jnp.int32, (128, D_INNER), 0)
    c = lax.broadcasted_iota(jnp.int32, (128, D_INNER), 1)
    return ((c >> 6) == r).astype(BF16)


def _reduce_mat():
    r = lax.broadcasted_iota(jnp.int32, (D_INNER, 128), 0)
    c = lax.broadcasted_iota(jnp.int32, (D_INNER, 128), 1)
    return ((r >> 6) == c).astype(BF16)


def _split(v, parts):
    out = []
    for _ in range(parts - 1):
        p = v.astype(BF16)
        out.append(p)
        v = v - p.astype(F32)
    out.append(v.astype(BF16))
    return out


def _sel_dot(v, sel, parts):
    acc = None
    for p in reversed(_split(v, parts)):
        t = _dot(p, sel)
        acc = t if acc is None else acc + t
    return acc


def _row8(v):
    return jnp.broadcast_to(v, (8, v.shape[1]))


def _tril():
    r = lax.broadcasted_iota(jnp.int32, (BLK, BLK), 0)
    c = lax.broadcasted_iota(jnp.int32, (BLK, BLK), 1)
    return r >= c


def _softplus(x):
    return jnp.maximum(x, 0.0) + jnp.log(1.0 + jnp.exp(-jnp.abs(x)))


def _ssd_common(dtraw_ref, dtb_ref, alog_ref):
    causal = _tril()
    e_mat = _expand_mat()
    a_neg = -jnp.exp(alog_ref[...])
    dt = _softplus(dtraw_ref[...] + dtb_ref[...])
    a_cs = _dot(causal.astype(F32), dt * a_neg, HI)
    a_cs_t = a_cs.T
    dt_x = _sel_dot(dt, e_mat, 3)
    acs_x = _sel_dot(a_cs, e_mat, 3)
    alast_x = acs_x[BLK - 1:BLK, :]
    ea_x = jnp.exp(acs_x)
    ds_x = jnp.exp(alast_x - acs_x)
    elast_x = jnp.exp(alast_x)
    return causal, e_mat, a_neg, dt, a_cs, a_cs_t, dt_x, ea_x, ds_x, elast_x


def _decay(a_cs, a_cs_t, h, causal):
    seg = a_cs[:, h:h + 1] - a_cs_t[h:h + 1, :]
    return jnp.where(causal, jnp.exp(jnp.where(causal, seg, 0.0)), 0.0)


def _ssd_fwd(xbc, proj, dt_bias, a_log, d_skip, side=None):
    s = xbc.shape[0]
    nc = s // BLK

    def body(xs_ref, b_ref, c_ref, dtraw_ref, dtb_ref, alog_ref, dskip_ref, y_ref, hp_ref, h_scr, xc16):
        @pl.when(pl.program_id(0) == 0)
        def _():
            h_scr[...] = jnp.zeros_like(h_scr)

        causal, e_mat, _, _, a_cs, a_cs_t, dt_x, ea_x, ds_x, elast_x = _ssd_common(dtraw_ref, dtb_ref, alog_ref)
        dskip_x = _sel_dot(_row8(dskip_ref[...]), e_mat, 3)[0:1]
        xs = xs_ref[...]
        xc = xs * dt_x
        xc16[...] = xc.astype(BF16)
        xcd = (xc * ds_x).astype(BF16)
        hp_ref[0] = h_scr[...]
        for g in range(4):
            gs = slice(g * 512, (g + 1) * 512)
            cg = c_ref[:, g * 128:(g + 1) * 128].astype(BF16)
            bg = b_ref[:, g * 128:(g + 1) * 128].astype(BF16)
            cb = _dot_nt(cg, bg)
            hg = h_scr[:, gs]
            yoff = _dot(cg, hg.astype(BF16)) * ea_x[:, gs]
            for j in range(8):
                h = g * 8 + j
                hsl = slice(h * 64, (h + 1) * 64)
                mm = (cb * _decay(a_cs, a_cs_t, h, causal)).astype(BF16)
                y_ref[:, hsl] = _dot(mm, xc16[:, hsl])
            y_ref[:, gs] += yoff + xs[:, gs] * dskip_x[:, gs]
            h_scr[:, gs] = hg * elast_x[:, gs] + _dot_tn(bg, xcd[:, gs])

    vec = pl.BlockSpec((1, 128), lambda c: (0, 0))
    own, extra = _hosted(
        body, name="ssd_fwd", grid=(nc,),
        in_specs=[pl.BlockSpec((BLK, D_INNER), lambda c: (c, 0)),
                  pl.BlockSpec((BLK, BC_DIM), lambda c: (c, D_INNER // BC_DIM)),
                  pl.BlockSpec((BLK, BC_DIM), lambda c: (c, D_INNER // BC_DIM + 1)),
                  pl.BlockSpec((BLK, 128), lambda c: (c, O_DT // 128)), vec, vec, vec],
        out_specs=[pl.BlockSpec((BLK, D_INNER), lambda c: (c, 0)),
                   pl.BlockSpec((1, 128, D_INNER), lambda c: (c, 0, 0))],
        out_shape=[jax.ShapeDtypeStruct((s, D_INNER), F32), jax.ShapeDtypeStruct((nc, 128, D_INNER), F32)],
        scratch_shapes=[pltpu.VMEM((128, D_INNER), F32), pltpu.VMEM((BLK, D_INNER), BF16)],
        args=(xbc, xbc, xbc, proj, dt_bias, a_log, d_skip), sem=("arbitrary",), side=side)
    return own if side is None else (own, extra)


def _ssd_bwd(xbc, proj, dt_bias, a_log, d_skip, hprev, dy, side=None):
    s = xbc.shape[0]
    nc = s // BLK

    def body(xs_ref, b_ref, c_ref, dtraw_ref, dtb_ref, alog_ref, dskip_ref, hp_ref, dy_ref,
             dxbc_ref, ddt_ref, dvec_ref, dh_scr, xc16, dy16, dxc_scr, dacs_r, tdiff):
        step = pl.program_id(0)
        dacs_r[...] = jnp.zeros_like(dacs_r)

        @pl.when(step == 0)
        def _():
            dh_scr[...] = jnp.zeros_like(dh_scr)
            dvec_ref[...] = jnp.zeros_like(dvec_ref)

        causal, e_mat, a_neg, dt, a_cs, a_cs_t, dt_x, ea_x, ds_x, elast_x = _ssd_common(dtraw_ref, dtb_ref, alog_ref)
        r_mat = _reduce_mat()
        lane = lax.broadcasted_iota(jnp.int32, (1, 128), 1)
        dskip_x = _sel_dot(_row8(dskip_ref[...]), e_mat, 3)[0:1]
        xs = xs_ref[...]
        dy = dy_ref[...]
        xc = xs * dt_x
        xcd = xc * ds_x
        xc16[...] = xc.astype(BF16)
        dy16[...] = dy.astype(BF16)
        dyea = dy * ea_x
        dh = dh_scr[...]
        hp = hp_ref[0]
        dalast_x = jnp.sum(dh * hp, axis=0, keepdims=True) * elast_x
        dacs = jnp.zeros((BLK, 128), F32)
        for g in range(4):
            gs = slice(g * 512, (g + 1) * 512)
            bsl = slice(g * 128, (g + 1) * 128)
            cg = c_ref[:, bsl].astype(BF16)
            bg = b_ref[:, bsl].astype(BF16)
            cb = _dot_nt(cg, bg)
            hg16 = hp[:, gs].astype(BF16)
            dhg16 = dh[:, gs].astype(BF16)
            raw = _dot(cg, hg16)
            draw16 = dyea[:, gs].astype(BF16)
            dcg = _dot_nt(draw16, hg16)
            dhp_g = _dot_tn(cg, draw16)
            dbg = _dot_nt(xcd[:, gs].astype(BF16), dhg16)
            dxcd = _dot(bg, dhg16)
            dcb = jnp.zeros((BLK, BLK), F32)
            for j in range(8):
                h = g * 8 + j
                hsl = slice(h * 64, (h + 1) * 64)
                decay = _decay(a_cs, a_cs_t, h, causal)
                m = cb * decay
                dm = _dot_nt(dy16[:, hsl], xc16[:, hsl])
                dxc_scr[:, hsl] = _dot_tn(m.astype(BF16), dy16[:, hsl])
                dcb = dcb + dm * decay
                dseg = dm * m
                oneh = jnp.where(lane == h, 1.0, 0.0)
                dacs = dacs + jnp.sum(dseg, axis=1, keepdims=True) * oneh
                dacs_r[h:h + 1, :] = jnp.sum(dseg, axis=0, keepdims=True)
            dcb16 = dcb.astype(BF16)
            dcg = dcg + _dot(dcb16, bg)
            dbg = dbg + _dot_tn(dcb16, cg)
            dxbc_ref[:, D_INNER + g * 128:D_INNER + (g + 1) * 128] = dbg
            dxbc_ref[:, D_INNER + BC_DIM + g * 128:D_INNER + BC_DIM + (g + 1) * 128] = dcg
            dxc_scr[:, gs] += dxcd * ds_x[:, gs]
            dh_scr[:, gs] = dh[:, gs] * elast_x[:, gs] + dhp_g
            tst = dxcd * xcd[:, gs]
            tdiff[:, gs] = dy[:, gs] * (raw * ea_x[:, gs]) - tst
            tdiff[BLK - 1:BLK, gs] += jnp.sum(tst, axis=0, keepdims=True)
        dxc = dxc_scr[...]
        row = lax.broadcasted_iota(jnp.int32, (BLK, D_INNER), 0)
        tfull = tdiff[...] + jnp.where(row == BLK - 1, dalast_x, 0.0)
        dacs = dacs + _sel_dot(tfull, r_mat, 2) - dacs_r[...].T
        da = _dot_tn(causal.astype(F32), dacs, HI)
        ddt = da * a_neg + _sel_dot(dxc * xs, r_mat, 2)
        lmask = lax.broadcasted_iota(jnp.int32, (BLK, 128), 1) < N_SSD_HEADS
        ddtraw = jnp.where(lmask, ddt * _sigmoid(dtraw_ref[...] + dtb_ref[...]), 0.0)
        ddt_ref[...] = ddtraw.astype(BF16)
        dxbc_ref[:, 0:D_INNER] = dy * dskip_x + dxc * dt_x
        dvec_ref[0:1, :] += jnp.sum(ddtraw, axis=0, keepdims=True)
        dvec_ref[1:2, :] += jnp.where(lane < N_SSD_HEADS, jnp.sum(da * dt, axis=0, keepdims=True) * a_neg, 0.0)
        dvec_ref[2:3, :] += _sel_dot(_row8(jnp.sum(dy * xs, axis=0, keepdims=True)), r_mat, 3)[0:1]

    rev = lambda c: nc - 1 - c
    vec = pl.BlockSpec((1, 128), lambda c: (0, 0))
    own, extra = _hosted(
        body, name="ssd_bwd", grid=(nc,),
        in_specs=[pl.BlockSpec((BLK, D_INNER), lambda c: (rev(c), 0)),
                  pl.BlockSpec((BLK, BC_DIM), lambda c: (rev(c), D_INNER // BC_DIM)),
                  pl.BlockSpec((BLK, BC_DIM), lambda c: (rev(c), D_INNER // BC_DIM + 1)),
                  pl.BlockSpec((BLK, 128), lambda c: (rev(c), O_DT // 128)), vec, vec, vec,
                  pl.BlockSpec((1, 128, D_INNER), lambda c: (rev(c), 0, 0)),
                  pl.BlockSpec((BLK, D_INNER), lambda c: (rev(c), 0))],
        out_specs=[pl.BlockSpec((BLK, XBC_DIM), lambda c: (rev(c), 0)),
                   pl.BlockSpec((BLK, 128), lambda c: (rev(c), 0)),
                   pl.BlockSpec((8, 128), lambda c: (0, 0))],
        out_shape=[jax.ShapeDtypeStruct((s, XBC_DIM), F32), jax.ShapeDtypeStruct((s, 128), BF16),
                   jax.ShapeDtypeStruct((8, 128), F32)],
        scratch_shapes=[pltpu.VMEM((128, D_INNER), F32), pltpu.VMEM((BLK, D_INNER), BF16),
                        pltpu.VMEM((BLK, D_INNER), BF16), pltpu.VMEM((BLK, D_INNER), F32),
                        pltpu.VMEM((128, BLK), F32), pltpu.VMEM((BLK, D_INNER), F32)],
        args=(xbc, xbc, xbc, proj, dt_bias, a_log, d_skip, hprev, dy), sem=("arbitrary",), side=side)
    return own if side is None else (own, extra)


GW = 512


def _gate_norm_fwd(y, proj, wn, *, tm=512):
    s = y.shape[0]
    tm = _tile(s, tm)

    def body(y_ref, z_ref, w_ref, o_ref):
        z = z_ref[...]
        y2 = y_ref[...] * (z * _sigmoid(z))
        r = lax.rsqrt(jnp.mean(y2 * y2, axis=-1, keepdims=True) + EPS)
        o_ref[...] = ((y2 * r) * w_ref[...]).astype(BF16)

    return pl.pallas_call(
        body, name="gate_norm_fwd", grid=(s // tm, 4),
        in_specs=[pl.BlockSpec((tm, GW), lambda i, g: (i, g)), pl.BlockSpec((tm, GW), lambda i, g: (i, O_Z // GW + g)),
                  pl.BlockSpec((1, GW), lambda i, g: (0, g))],
        out_specs=pl.BlockSpec((tm, GW), lambda i, g: (i, g)),
        out_shape=jax.ShapeDtypeStruct((s, D_INNER), BF16), compiler_params=_cp(("parallel", "parallel")),
    )(y, proj, wn)


def _gate_norm_bwd(dyn, y, proj, wn, *, tm=512):
    s = y.shape[0]
    tm = _tile(s, tm)

    def body(d_ref, y_ref, z_ref, w_ref, dy_ref, dz_ref, dw_ref):
        i = pl.program_id(1)
        z = z_ref[...]
        sg = _sigmoid(z)
        sz = z * sg
        yv = y_ref[...]
        y2 = yv * sz
        r = lax.rsqrt(jnp.mean(y2 * y2, axis=-1, keepdims=True) + EPS)
        xh = y2 * r
        dv = d_ref[...]
        g = dv * w_ref[...]
        dy2 = r * (g - xh * jnp.mean(g * xh, axis=-1, keepdims=True))
        dy_ref[...] = dy2 * sz
        dz_ref[...] = (dy2 * yv * _dsilu(z, sg)).astype(BF16)
        part = jnp.sum(dv * xh, axis=0, keepdims=True)

        @pl.when(i == 0)
        def _():
            dw_ref[...] = part

        @pl.when(i > 0)
        def _():
            dw_ref[...] += part

    blk = pl.BlockSpec((tm, GW), lambda g, i: (i, g))
    vec = pl.BlockSpec((1, GW), lambda g, i: (0, g))
    return pl.pallas_call(
        body, name="gate_norm_bwd", grid=(4, s // tm),
        in_specs=[blk, blk, pl.BlockSpec((tm, GW), lambda g, i: (i, O_Z // GW + g)), vec],
        out_specs=[blk, blk, vec],
        out_shape=[jax.ShapeDtypeStruct((s, D_INNER), F32), jax.ShapeDtypeStruct((s, D_INNER), BF16),
                   jax.ShapeDtypeStruct((1, D_INNER), F32)],
        compiler_params=_cp(("parallel", "arbitrary")),
    )(dyn, y, proj, wn)


def _merge_fwd(proj, b_gate, attn, ssd_out, *, tm=512):
    s = attn.shape[0]
    tm = _tile(s, tm)

    def body(ga_ref, gs_ref, ba_ref, bs_ref, a_ref, s_ref, o_ref):
        ga = _sigmoid(ga_ref[...] + ba_ref[...])
        gs = _sigmoid(gs_ref[...] + bs_ref[...])
        o_ref[...] = (ga * a_ref[...] + gs * s_ref[...]).astype(BF16)

    blk = pl.BlockSpec((tm, GW), lambda i, j: (i, j))
    return pl.pallas_call(
        body, name="merge_fwd", grid=(s // tm, 2),
        in_specs=[pl.BlockSpec((tm, GW), lambda i, j: (i, O_GA // GW + j)),
                  pl.BlockSpec((tm, GW), lambda i, j: (i, O_GS // GW + j)),
                  pl.BlockSpec((1, GW), lambda i, j: (0, j)), pl.BlockSpec((1, GW), lambda i, j: (0, 2 + j)), blk, blk],
        out_specs=blk, out_shape=jax.ShapeDtypeStruct((s, D_MODEL), BF16),
        compiler_params=_cp(("parallel", "parallel")),
    )(proj, proj, b_gate, b_gate, attn, ssd_out)


def _merge_bwd(dm, proj, b_gate, attn, ssd_out, *, tm=512):
    s = attn.shape[0]
    tm = _tile(s, tm)

    def body(d_ref, ga_ref, gs_ref, ba_ref, bs_ref, a_ref, s_ref, da_ref, ds_ref, dga_ref, dgs_ref, dba_ref, dbs_ref):
        i = pl.program_id(1)
        ga = _sigmoid(ga_ref[...] + ba_ref[...])
        gs = _sigmoid(gs_ref[...] + bs_ref[...])
        d = d_ref[...]
        da_ref[...] = (d * ga).astype(BF16)
        ds_ref[...] = (d * gs).astype(BF16)
        dga = d * a_ref[...] * (ga * (1.0 - ga))
        dgs = d * s_ref[...] * (gs * (1.0 - gs))
        dga_ref[...] = dga.astype(BF16)
        dgs_ref[...] = dgs.astype(BF16)
        pa = jnp.sum(dga, axis=0, keepdims=True)
        ps = jnp.sum(dgs, axis=0, keepdims=True)

        @pl.when(i == 0)
        def _():
            dba_ref[...] = pa
            dbs_ref[...] = ps

        @pl.when(i > 0)
        def _():
            dba_ref[...] += pa
            dbs_ref[...] += ps

    blk = pl.BlockSpec((tm, GW), lambda j, i: (i, j))
    vec = pl.BlockSpec((1, GW), lambda j, i: (0, j))
    sd = jax.ShapeDtypeStruct((s, D_MODEL), BF16)
    vd = jax.ShapeDtypeStruct((1, D_MODEL), F32)
    return pl.pallas_call(
        body, name="merge_bwd", grid=(2, s // tm),
        in_specs=[blk, pl.BlockSpec((tm, GW), lambda j, i: (i, O_GA // GW + j)),
                  pl.BlockSpec((tm, GW), lambda j, i: (i, O_GS // GW + j)),
                  vec, pl.BlockSpec((1, GW), lambda j, i: (0, 2 + j)), blk, blk],
        out_specs=[blk, blk, blk, blk, vec, vec], out_shape=[sd, sd, sd, sd, vd, vd],
        compiler_params=_cp(("parallel", "arbitrary")),
    )(dm, proj, proj, b_gate, b_gate, attn, ssd_out)


def _adamw_math(w, g, m, v):
    mn = ADAM_B1 * m + (1.0 - ADAM_B1) * g
    vn = ADAM_B2 * v + (1.0 - ADAM_B2) * (g * g)
    m_hat = mn / (1.0 - ADAM_B1 ** ADAM_STEP)
    v_hat = vn / (1.0 - ADAM_B2 ** ADAM_STEP)
    return -ADAM_LR * (m_hat / (jnp.sqrt(v_hat) + ADAM_EPS) + ADAM_WD * w), mn, vn


def _adamw_many(ws, gs, ms, vs):
    n = len(ws)

    def body(*refs):
        outs = refs[4 * n:]
        for i in range(n):
            res = _adamw_math(*[refs[q * n + i][...] for q in range(4)])
            for q in range(3):
                outs[q * n + i][...] = res[q]

    return pl.pallas_call(body, name="adamw_small", out_shape=[jax.ShapeDtypeStruct(w.shape, F32) for w in ws] * 3,
                          compiler_params=_cp())(*ws, *gs, *ms, *vs)


def _adamw(w, g, m, v, *, name, tm=128):
    r, c = w.shape
    tm = r if (r < tm or r % tm) else tm

    def body(w_ref, g_ref, m_ref, v_ref, d_ref, nm_ref, nv_ref, g_out):
        gv = g_ref[:, :c]
        d_ref[...], nm_ref[...], nv_ref[...] = _adamw_math(w_ref[...], gv, m_ref[...], v_ref[...])
        g_out[...] = gv

    blk = pl.BlockSpec((tm, c), lambda i: (i, 0))
    sd = jax.ShapeDtypeStruct((r, c), F32)
    return pl.pallas_call(
        body, name=name, grid=(r // tm,), in_specs=[blk, pl.BlockSpec((tm, g.shape[1]), lambda i: (i, 0)), blk, blk],
        out_specs=[blk] * 4, out_shape=[sd] * 4, compiler_params=_cp(("parallel",)),
    )(w, g, m, v)


ANY = pl.BlockSpec(memory_space=pl.ANY)
N_CHIPS = 4


def _chip_of(k, x, y):
    return (x ^ (k >> 1), y ^ (k & 1))


def _all_gather_small(shard):
    r, c = shard.shape
    hr = r // 2

    def body(sh_ref, out_ref, send_sems, recv_sems, local_sem):
        x, y, cc = lax.axis_index("x"), lax.axis_index("y"), lax.axis_index("c")

        def half(px, py, pc):
            return out_ref.at[2 * px + py, pl.ds(pc * hr, hr), :]

        def copy(k, px, py, pc, to, src=None):
            return pltpu.make_async_remote_copy(
                src_ref=half(px, py, pc) if src is None else src, dst_ref=half(px, py, pc),
                send_sem=send_sems.at[k], recv_sem=recv_sems.at[k], device_id=to, device_id_type=MESH)

        mine = pltpu.make_async_copy(sh_ref, out_ref.at[2 * x + y], local_sem)
        mine.start()
        chips = [_chip_of(k, x, y) for k in (1, 2, 3)]
        first = [copy(j, x, y, cc, (*chip, cc), src=sh_ref.at[pl.ds(cc * hr, hr), :]) for j, chip in enumerate(chips)]
        for cp in first:
            cp.start()
        passed = [copy(3 + j, *chip, cc, (x, y, 1 - cc)) for j, chip in enumerate(chips)]
        for j, chip in enumerate(chips):
            copy(j, *chip, cc, (x, y, cc)).wait_recv()
            passed[j].start()
        for j, chip in enumerate(chips):
            copy(3 + j, *chip, 1 - cc, (x, y, cc)).wait_recv()
        for cp in first + passed:
            cp.wait_send()
        mine.wait()

    return pl.pallas_call(
        body, name="all_gather_small", in_specs=[ANY], out_specs=ANY,
        out_shape=jax.ShapeDtypeStruct((N_CHIPS, r, c), shard.dtype),
        scratch_shapes=[pltpu.SemaphoreType.DMA((6,)), pltpu.SemaphoreType.DMA((6,)), pltpu.SemaphoreType.DMA],
    )(shard)


def _cast_bf16(a, *, name, tm=512):
    n, r, c = a.shape
    tm = _tile(r, tm) if r % 128 == 0 else r

    def body(a_ref, o_ref):
        o_ref[...] = a_ref[...].astype(BF16)

    blk = pl.BlockSpec((1, tm, c), lambda i, j: (i, j, 0))
    return pl.pallas_call(body, name=name, grid=(n, r // tm), in_specs=[blk], out_specs=blk,
                          out_shape=jax.ShapeDtypeStruct(a.shape, BF16), compiler_params=_cp(("parallel", "parallel")))(a)


def _pair_exchange(g16, hr):
    n, r, c = g16.shape

    def body(g_ref, out_ref, send_sem, recv_sem):
        x, y, cc = lax.axis_index("x"), lax.axis_index("y"), lax.axis_index("c")
        cp = pltpu.make_async_remote_copy(
            src_ref=g_ref.at[:, pl.ds((1 - cc) * hr, hr), :], dst_ref=out_ref, send_sem=send_sem, recv_sem=recv_sem,
            device_id=(x, y, 1 - cc), device_id_type=MESH)
        cp.start()
        cp.wait()

    return pl.pallas_call(
        body, name="grad_pair_exchange", in_specs=[ANY], out_specs=ANY,
        out_shape=jax.ShapeDtypeStruct((n, hr, c), g16.dtype),
        scratch_shapes=[pltpu.SemaphoreType.DMA, pltpu.SemaphoreType.DMA],
    )(g16)


def _pair_add(g, recv, half_idx, hr, *, tm=384):
    n, r, c = g.shape
    nt = hr // tm

    def body(hi_ref, g_ref, r_ref, o32_ref, o16_ref):
        v = g_ref[...] + r_ref[...].astype(F32)
        o32_ref[...] = v
        o16_ref[...] = v.astype(BF16)

    gs = pltpu.PrefetchScalarGridSpec(
        num_scalar_prefetch=1, grid=(n, nt),
        in_specs=[pl.BlockSpec((1, tm, c), lambda i, j, hi: (i, hi[0] * nt + j, 0)),
                  pl.BlockSpec((1, tm, c), lambda i, j, hi: (i, j, 0))],
        out_specs=[pl.BlockSpec((1, tm, c), lambda i, j, hi: (i, j, 0))] * 2)
    return pl.pallas_call(
        body, name="grad_pair_add", grid_spec=gs,
        out_shape=[jax.ShapeDtypeStruct((n, hr, c), F32), jax.ShapeDtypeStruct((n, hr, c), BF16)],
        compiler_params=_cp(("parallel", "parallel")),
    )(half_idx, g, recv)


def _chip_exchange(p16):
    n, hr, c = p16.shape

    def body(p_ref, out_ref, send_sems, recv_sems):
        x, y, cc = lax.axis_index("x"), lax.axis_index("y"), lax.axis_index("c")
        cps = []
        for j, k in enumerate((1, 2, 3)):
            px, py = _chip_of(k, x, y)
            cps.append(pltpu.make_async_remote_copy(
                src_ref=p_ref.at[2 * px + py], dst_ref=out_ref.at[j], send_sem=send_sems.at[j], recv_sem=recv_sems.at[j],
                device_id=(px, py, cc), device_id_type=MESH))
        for cp in cps:
            cp.start()
        for cp in cps:
            cp.wait()

    return pl.pallas_call(
        body, name="grad_chip_exchange", in_specs=[ANY], out_specs=ANY,
        out_shape=jax.ShapeDtypeStruct((3, hr, c), p16.dtype),
        scratch_shapes=[pltpu.SemaphoreType.DMA((3,)), pltpu.SemaphoreType.DMA((3,))],
    )(p16)


def _chip_add(p32, recv, chip_idx, *, tm=384):
    n, hr, c = p32.shape

    def body(ci_ref, p_ref, r_ref, o_ref):
        o_ref[...] = ((p_ref[0] + r_ref[0].astype(F32)) + r_ref[1].astype(F32)) + r_ref[2].astype(F32)

    gs = pltpu.PrefetchScalarGridSpec(
        num_scalar_prefetch=1, grid=(hr // tm,),
        in_specs=[pl.BlockSpec((1, tm, c), lambda j, ci: (ci[0], j, 0)), pl.BlockSpec((3, tm, c), lambda j, ci: (0, j, 0))],
        out_specs=pl.BlockSpec((tm, c), lambda j, ci: (j, 0)))
    return pl.pallas_call(
        body, name="grad_chip_add", grid_spec=gs, out_shape=jax.ShapeDtypeStruct((hr, c), F32),
        compiler_params=_cp(("parallel",)),
    )(chip_idx, p32, recv)


def _pair_gather(f):
    hr, c = f.shape

    def body(f_ref, out_ref, send_sem, recv_sem, local_sem):
        x, y, cc = lax.axis_index("x"), lax.axis_index("y"), lax.axis_index("c")
        mine = pltpu.make_async_copy(f_ref, out_ref.at[pl.ds(cc * hr, hr), :], local_sem)
        mine.start()
        cp = pltpu.make_async_remote_copy(
            src_ref=f_ref, dst_ref=out_ref.at[pl.ds(cc * hr, hr), :], send_sem=send_sem, recv_sem=recv_sem,
            device_id=(x, y, 1 - cc), device_id_type=MESH)
        cp.start()
        cp.wait()
        mine.wait()

    return pl.pallas_call(
        body, name="grad_pair_gather", in_specs=[ANY], out_specs=ANY,
        out_shape=jax.ShapeDtypeStruct((2 * hr, c), f.dtype),
        scratch_shapes=[pltpu.SemaphoreType.DMA, pltpu.SemaphoreType.DMA, pltpu.SemaphoreType.DMA],
    )(f)


def _all_reduce_small(buf):
    r, c = buf.shape

    def body(b_ref, out_ref, gat, send_sems, recv_sems):
        x, y, cc = lax.axis_index("x"), lax.axis_index("y"), lax.axis_index("c")
        me = 4 * x + 2 * y + cc
        gat[me] = b_ref[...]
        cps = []
        for k in range(1, 8):
            px, py, pc = x ^ (k >> 2), y ^ ((k >> 1) & 1), cc ^ (k & 1)
            cps.append(pltpu.make_async_remote_copy(
                src_ref=b_ref, dst_ref=gat.at[me], send_sem=send_sems.at[k - 1], recv_sem=recv_sems.at[k - 1],
                device_id=(px, py, pc), device_id_type=MESH))
        for cp in cps:
            cp.start()
        for cp in cps:
            cp.wait()
        acc = gat[0]
        for d in range(1, 8):
            acc = acc + gat[d]
        out_ref[...] = acc

    vm = pl.BlockSpec(memory_space=pltpu.VMEM)
    return pl.pallas_call(
        body, name="all_reduce_small", in_specs=[vm], out_specs=vm, out_shape=jax.ShapeDtypeStruct((r, c), F32),
        scratch_shapes=[pltpu.VMEM((8, r, c), F32), pltpu.SemaphoreType.DMA((7,)), pltpu.SemaphoreType.DMA((7,))],
        compiler_params=pltpu.CompilerParams(vmem_limit_bytes=VMEM_LIMIT),
    )(buf)


def _pipe(fn, ins, outs, tr, depth=4, slots=None):
    shape = ins[0].shape
    lead, (r, c) = shape[:-2], shape[-2:]
    assert len(lead) <= 1 and r % tr == 0
    nr = r // tr
    which = list(range(lead[0])) if lead and slots is None else slots
    n = nr * (len(which) if lead else 1)
    ni, no = len(ins), len(outs)

    def blk(ref, step):
        rows = pl.ds((step % nr) * tr, tr)
        return ref.at[which[step // nr], rows, :] if lead else ref.at[rows, :]

    def scoped(*bufs):
        ibufs, obufs, isem, osem = bufs[:ni], bufs[ni:ni + no], bufs[-2], bufs[-1]

        def in_copy(q, step, slot):
            return pltpu.make_async_copy(blk(ins[q], step), ibufs[q].at[slot], isem.at[q, slot])

        def out_copy(q, step, slot):
            return pltpu.make_async_copy(obufs[q].at[slot], blk(outs[q], step), osem.at[q, slot])

        for step in range(min(nbuf - 1, n)):
            for q in range(ni):
                in_copy(q, step, step % nbuf).start()
        for step in range(n):
            slot = step % nbuf
            if step + nbuf - 1 < n:
                for q in range(ni):
                    in_copy(q, step + nbuf - 1, (step + nbuf - 1) % nbuf).start()
            for q in range(ni):
                in_copy(q, step, slot).wait()
            if step >= nbuf:
                for q in range(no):
                    out_copy(q, step - nbuf, slot).wait()
            res = fn(*[ibufs[q][slot] for q in range(ni)])
            for q in range(no):
                obufs[q][slot] = res[q].astype(obufs[q].dtype)
                out_copy(q, step, slot).start()
        for step in range(max(n - nbuf, 0), n):
            for q in range(no):
                out_copy(q, step, step % nbuf).wait()

    assert n <= 8
    nbuf = min(n, depth)
    pl.run_scoped(scoped, *[pltpu.VMEM((nbuf, tr, c), q.dtype) for q in ins], *[pltpu.VMEM((nbuf, tr, c), q.dtype) for q in outs],
                  pltpu.SemaphoreType.DMA((ni, nbuf)), pltpu.SemaphoreType.DMA((no, nbuf)))


W_IN_PAD = 2304
BIG = ("w_in", "w_attn_o", "w_ssd_o", "w_out", "w_up", "w_down")
BIG_SHAPE = dict(w_in=(D_MODEL, W_IN_PAD), w_attn_o=(Q_DIM // 4, D_MODEL), w_ssd_o=(D_INNER // 4, D_MODEL),
                 w_out=(D_MODEL // 4, D_MODEL), w_up=(D_MODEL, 2 * D_FF // 4), w_down=(D_FF // 4, D_MODEL))
BIG_TR = dict(w_in=128, w_attn_o=128, w_ssd_o=128, w_out=128, w_up=128, w_down=176)
X_FIRST = dict(w_in=True, w_attn_o=True, w_ssd_o=False, w_out=True, w_up=False, w_down=False)


def _neighbours(x, y, x_first):
    xn, yn = (1 - x, y), (x, 1 - y)
    n1, n2 = (xn, yn) if x_first else (yn, xn)
    slot = lambda ch: 2 * ch[0] + ch[1]
    return n1, n2, slot(n1), slot(n2), slot((1 - x, 1 - y))


def _gather_big(shards):
    nt = len(BIG)

    def body(*refs):
        sh, out = refs[:nt], refs[nt:2 * nt]
        send_sems, recv_sems = refs[2 * nt:]
        x, y, cc = lax.axis_index("x"), lax.axis_index("y"), lax.axis_index("c")
        me = 2 * x + y
        sib = (x, y, 1 - cc)
        for t, n in enumerate(BIG):
            _pipe(lambda v: (v,), [sh[t]], [out[t].at[me]], BIG_TR[n])

        def copy(t, k, slot, pc, to):
            hr = BIG_SHAPE[BIG[t]][0] // 2
            ref = out[t].at[slot, pl.ds(pc * hr, hr), :]
            return pltpu.make_async_remote_copy(src_ref=ref, dst_ref=ref, send_sem=send_sems.at[6 * t + k],
                                                recv_sem=recv_sems.at[6 * t + k], device_id=to, device_id_type=MESH)

        started = []

        def start(cp):
            cp.start()
            started.append(cp)

        geo = [_neighbours(x, y, X_FIRST[n]) for n in BIG]
        for t in range(nt):
            n1, n2, _, _, _ = geo[t]
            start(copy(t, 0, me, cc, (*n1, cc)))
            start(copy(t, 1, me, cc, (*n2, cc)))
        for t in range(nt):
            n1, n2, s1, s2, sd = geo[t]
            copy(t, 0, s1, cc, sib).wait_recv()
            start(copy(t, 2, s1, cc, (*n2, cc)))
            start(copy(t, 3, s1, cc, sib))
            copy(t, 1, s2, cc, sib).wait_recv()
            start(copy(t, 4, s2, cc, sib))
        for t in range(nt):
            _, _, s1, s2, sd = geo[t]
            copy(t, 2, sd, cc, sib).wait_recv()
            start(copy(t, 5, sd, cc, sib))
        for t in range(nt):
            _, _, s1, s2, sd = geo[t]
            copy(t, 3, s1, 1 - cc, sib).wait_recv()
            copy(t, 4, s2, 1 - cc, sib).wait_recv()
            copy(t, 5, sd, 1 - cc, sib).wait_recv()
        for cp in started:
            cp.wait_send()

    return pl.pallas_call(
        body, name="gather_big", in_specs=[ANY] * nt, out_specs=[ANY] * nt,
        out_shape=[jax.ShapeDtypeStruct((N_CHIPS, *BIG_SHAPE[n]), BF16) for n in BIG],
        scratch_shapes=[pltpu.SemaphoreType.DMA((6 * nt,)), pltpu.SemaphoreType.DMA((6 * nt,))],
        compiler_params=pltpu.CompilerParams(vmem_limit_bytes=VMEM_LIMIT),
    )(*shards)


def _reduce_big(grads):
    nt = len(BIG)
    nw = 7

    def body(*refs):
        g = refs[:nt]
        fin = refs[nt:2 * nt]
        work = refs[2 * nt:2 * nt + nw * nt]
        send_sems, recv_sems = refs[2 * nt + nw * nt:]
        x, y, cc = lax.axis_index("x"), lax.axis_index("y"), lax.axis_index("c")
        me = 2 * x + y
        sib = (x, y, 1 - cc)
        started = []

        def rcopy(t, k, src, dst, to):
            cp = pltpu.make_async_remote_copy(src_ref=src, dst_ref=dst, send_sem=send_sems.at[5 * t + k],
                                              recv_sem=recv_sems.at[5 * t + k], device_id=to, device_id_type=MESH)
            return cp

        def start(cp):
            cp.start()
            started.append(cp)

        geo = [_neighbours(x, y, X_FIRST[n]) for n in BIG]
        hrs = [BIG_SHAPE[n][0] // 2 for n in BIG]
        wk = lambda t: work[nw * t:nw * (t + 1)]
        one = lambda ref, slot: ref.at[pl.ds(slot, 1)]
        for t in range(nt):
            recv_a = wk(t)[0]
            start(rcopy(t, 0, g[t].at[:, pl.ds((1 - cc) * hrs[t], hrs[t]), :], recv_a, sib))
        for t, n in enumerate(BIG):
            recv_a, p32, p16, r1, qme, qs2, r2 = wk(t)
            n1, n2, s1, s2, sd = geo[t]
            rcopy(t, 0, recv_a, recv_a, sib).wait_recv()
            _pipe(lambda a, b: (a + b, a + b), [g[t].at[:, pl.ds(cc * hrs[t], hrs[t]), :], recv_a], [p32, p16], BIG_TR[n])
            start(rcopy(t, 1, one(p16, s1), one(r1, 0), (*n1, cc)))
            start(rcopy(t, 2, one(p16, sd), one(r1, 1), (*n1, cc)))
        for t, n in enumerate(BIG):
            recv_a, p32, p16, r1, qme, qs2, r2 = wk(t)
            n1, n2, s1, s2, sd = geo[t]
            rcopy(t, 1, one(r1, 0), one(r1, 0), sib).wait_recv()
            rcopy(t, 2, one(r1, 1), one(r1, 1), sib).wait_recv()
            _pipe(lambda a, b: (a + b.astype(F32),), [one(p32, s2), one(r1, 1)], [qs2], BIG_TR[n])
            start(rcopy(t, 3, qs2, r2, (*n2, cc)))
            _pipe(lambda a, b: (a + b.astype(F32),), [one(p32, me), one(r1, 0)], [qme], BIG_TR[n])
        for t, n in enumerate(BIG):
            recv_a, p32, p16, r1, qme, qs2, r2 = wk(t)
            rcopy(t, 3, r2, r2, sib).wait_recv()
            mine = fin[t].at[pl.ds(cc * hrs[t], hrs[t]), :]
            _pipe(lambda a, b: (a + b.astype(F32),), [qme.at[0], r2.at[0]], [mine], BIG_TR[n])
            start(rcopy(t, 4, mine, mine, sib))
        for t in range(nt):
            other = fin[t].at[pl.ds((1 - cc) * hrs[t], hrs[t]), :]
            rcopy(t, 4, other, other, sib).wait_recv()
        for cp in started:
            cp.wait_send()

    outs = [jax.ShapeDtypeStruct(BIG_SHAPE[n], F32) for n in BIG]
    for n in BIG:
        r, c = BIG_SHAPE[n]
        hr = r // 2
        outs += [jax.ShapeDtypeStruct((4, hr, c), F32), jax.ShapeDtypeStruct((4, hr, c), F32),
                 jax.ShapeDtypeStruct((4, hr, c), BF16), jax.ShapeDtypeStruct((2, hr, c), BF16),
                 jax.ShapeDtypeStruct((1, hr, c), F32), jax.ShapeDtypeStruct((1, hr, c), BF16),
                 jax.ShapeDtypeStruct((1, hr, c), BF16)]
    res = pl.pallas_call(
        body, name="reduce_big", in_specs=[ANY] * nt, out_specs=[ANY] * len(outs), out_shape=outs,
        scratch_shapes=[pltpu.SemaphoreType.DMA((5 * nt,)), pltpu.SemaphoreType.DMA((5 * nt,))],
        compiler_params=pltpu.CompilerParams(vmem_limit_bytes=VMEM_LIMIT),
    )(*grads)
    return res[:nt]


WHOLE_X_FIRST = dict(w_ssd_o=True, w_out=False, w_attn_o=False)


def _quarters(names):
    out = []
    for i, n in enumerate(names):
        if n in WHOLE_X_FIRST:
            h = BIG_SHAPE[n][0] // 2
            out.append((i, WHOLE_X_FIRST[n], 0, h, 128))
        else:
            q = BIG_SHAPE[n][0] // 4
            tr = 128 if q % 128 == 0 else q
            out += [(i, True, 0, q, tr), (i, False, q, q, tr)]
    return out


class _GatherJob:
    def __init__(self, names, shards, at=None):
        self.names = names
        self.at = at
        self.inputs = list(shards)
        self.out_shapes = [jax.ShapeDtypeStruct((N_CHIPS, *BIG_SHAPE[n]), BF16) for n in names]
        self.ent = _quarters(names)
        self.scratch = [pltpu.SemaphoreType.DMA((6 * len(self.ent),)), pltpu.SemaphoreType.DMA((6 * len(self.ent),))]

    def phases(self, sh, out, scr):
        send_sems, recv_sems = scr
        names, ent = self.names, self.ent
        x, y, cc = lax.axis_index("x"), lax.axis_index("y"), lax.axis_index("c")
        me = 2 * x + y
        sib = (x, y, 1 - cc)
        geo = [_neighbours(x, y, e[1]) for e in ent]
        started = []

        def copy(i, k, slot, pc, to):
            arr, _, roff, rows, _ = ent[i]
            hr = BIG_SHAPE[names[arr]][0] // 2
            ref = out[arr].at[slot, pl.ds(pc * hr + roff, rows), :]
            return pltpu.make_async_remote_copy(src_ref=ref, dst_ref=ref, send_sem=send_sems.at[6 * i + k],
                                                recv_sem=recv_sems.at[6 * i + k], device_id=to, device_id_type=MESH)

        def start(*a):
            copy(*a).start()
            started.append(a)

        def p0():
            for t, n in enumerate(names):
                _pipe(lambda v: (v,), [sh[t]], [out[t].at[me]], BIG_TR[n])
            for i in range(len(ent)):
                n1, n2, _, _, _ = geo[i]
                start(i, 0, me, cc, (*n1, cc))
                start(i, 1, me, cc, (*n2, cc))

        def p1():
            for i in range(len(ent)):
                n1, n2, s1, s2, sd = geo[i]
                copy(i, 0, s1, cc, sib).wait_recv()
                start(i, 2, s1, cc, (*n2, cc))
                start(i, 3, s1, cc, sib)
                copy(i, 1, s2, cc, sib).wait_recv()
                start(i, 4, s2, cc, sib)

        def p2():
            for i in range(len(ent)):
                sd = geo[i][4]
                copy(i, 2, sd, cc, sib).wait_recv()
                start(i, 5, sd, cc, sib)

        def p3():
            for i in range(len(ent)):
                _, _, s1, s2, sd = geo[i]
                copy(i, 3, s1, 1 - cc, sib).wait_recv()
                copy(i, 4, s2, 1 - cc, sib).wait_recv()
                copy(i, 5, sd, 1 - cc, sib).wait_recv()
            for a in started:
                copy(*a).wait_send()

        return [p0, p1, p2, p3]


class _ReduceJob:
    NW = 7

    def __init__(self, names, grads, at=None):
        self.names = names
        self.at = at
        self.inputs = list(grads)
        self.ent = _quarters(names)
        self.out_shapes = [jax.ShapeDtypeStruct(BIG_SHAPE[n], F32) for n in names]
        for arr, _, _, rows, _ in self.ent:
            c = BIG_SHAPE[names[arr]][1]
            self.out_shapes += [jax.ShapeDtypeStruct((4, rows, c), F32), jax.ShapeDtypeStruct((4, rows, c), F32),
                                jax.ShapeDtypeStruct((4, rows, c), BF16), jax.ShapeDtypeStruct((2, rows, c), BF16),
                                jax.ShapeDtypeStruct((1, rows, c), F32), jax.ShapeDtypeStruct((1, rows, c), BF16),
                                jax.ShapeDtypeStruct((1, rows, c), BF16)]
        self.scratch = [pltpu.SemaphoreType.DMA((8 * len(self.ent),)), pltpu.SemaphoreType.DMA((8 * len(self.ent),))]

    def phases(self, g, outs, scr):
        send_sems, recv_sems = scr
        names, ent, nw = self.names, self.ent, self.NW
        nt = len(names)
        fin, work = outs[:nt], outs[nt:]
        x, y, cc = lax.axis_index("x"), lax.axis_index("y"), lax.axis_index("c")
        me = 2 * x + y
        sib = (x, y, 1 - cc)
        geo = [_neighbours(x, y, e[1]) for e in ent]
        started = []
        wk = lambda i: work[nw * i:nw * (i + 1)]
        one = lambda ref, slot: ref.at[pl.ds(slot, 1)]

        def rows_of(i, pc):
            arr, _, roff, rows, _ = ent[i]
            return pl.ds(pc * (BIG_SHAPE[names[arr]][0] // 2) + roff, rows)

        def rcopy(i, k, src, dst, to):
            return pltpu.make_async_remote_copy(src_ref=src, dst_ref=dst, send_sem=send_sems.at[8 * i + k],
                                                recv_sem=recv_sems.at[8 * i + k], device_id=to, device_id_type=MESH)

        def start(make):
            make().start()
            started.append(make)

        def pair(i, q, slot, pc):
            return rcopy(i, q, g[ent[i][0]].at[pl.ds(slot, 1), rows_of(i, pc), :], one(wk(i)[0], slot), sib)

        def p0():
            for i in range(len(ent)):
                _, _, s1, s2, sd = geo[i]
                for q, slot in enumerate((s1, sd, s2, me)):
                    start(lambda i=i, q=q, slot=slot: pair(i, q, slot, 1 - cc))

        def p1():
            for i, e in enumerate(ent):
                recv_a, _, p16, _ = wk(i)[:4]
                n1, n2, s1, s2, sd = geo[i]
                pair(i, 0, s1, cc).wait_recv()
                pair(i, 1, sd, cc).wait_recv()
                _pipe(lambda a, b: (a + b,), [g[e[0]].at[:, rows_of(i, cc), :], recv_a], [p16], e[4], slots=(s1, sd))
                start(lambda i=i, s1=s1, n1=n1: rcopy(i, 4, one(wk(i)[2], s1), one(wk(i)[3], 0), (*n1, cc)))
                start(lambda i=i, sd=sd, n1=n1: rcopy(i, 5, one(wk(i)[2], sd), one(wk(i)[3], 1), (*n1, cc)))
            for i, e in enumerate(ent):
                recv_a, p32 = wk(i)[:2]
                _, _, s1, s2, sd = geo[i]
                pair(i, 2, s2, cc).wait_recv()
                pair(i, 3, me, cc).wait_recv()
                _pipe(lambda a, b: (a + b,), [g[e[0]].at[:, rows_of(i, cc), :], recv_a], [p32], e[4], slots=(s2, me))

        def p2():
            for i, e in enumerate(ent):
                _, p32, _, r1, qme, qs2, r2 = wk(i)
                n1, n2, s1, s2, sd = geo[i]
                rcopy(i, 4, one(r1, 0), one(r1, 0), sib).wait_recv()
                rcopy(i, 5, one(r1, 1), one(r1, 1), sib).wait_recv()
                _pipe(lambda a, b, c, d: (a + b.astype(F32), c + d.astype(F32)),
                      [one(p32, s2), one(r1, 1), one(p32, me), one(r1, 0)], [qs2, qme], e[4])
                start(lambda i=i, n2=n2: rcopy(i, 6, wk(i)[5], wk(i)[6], (*n2, cc)))

        def p3():
            for i, e in enumerate(ent):
                qme, r2 = wk(i)[4], wk(i)[6]
                rcopy(i, 6, r2, r2, sib).wait_recv()
                mine = fin[e[0]].at[rows_of(i, cc), :]
                _pipe(lambda a, b: (a + b.astype(F32),), [qme.at[0], r2.at[0]], [mine], e[4])
                start(lambda i=i, e=e: rcopy(i, 7, fin[e[0]].at[rows_of(i, cc), :], fin[e[0]].at[rows_of(i, cc), :], sib))

        def p4():
            for i, e in enumerate(ent):
                other = fin[e[0]].at[rows_of(i, 1 - cc), :]
                rcopy(i, 7, other, other, sib).wait_recv()
            for make in started:
                make().wait_send()

        return [p0, p1, p2, p3, p4]


class _AdamJob:
    def __init__(self, names, ws, gs, ms, vs, groups):
        self.names, self.groups = names, groups
        self.inputs = [a for quad in zip(ws, gs, ms, vs) for a in quad]
        self.out_shapes = [jax.ShapeDtypeStruct(w.shape, F32) for w in ws for _ in range(4)]

    def work(self, ins, outs):
        def one(t):
            w, g, m, v = ins[4 * t:4 * t + 4]
            r = w.shape[1]
            tr = 128 if r % 128 == 0 else r // 4
            _pipe(lambda a, b, c, d: (*_adamw_math(a, b, c, d), b), [w.at[0], g, m.at[0], v.at[0]],
                  [o.at[0] for o in outs[4 * t:4 * t + 4]], tr, depth=2)

        def group(grp):
            def run():
                for n in grp:
                    one(self.names.index(n))
            return run

        return [group(grp) for grp in self.groups]


class _Interleaved:
    def __init__(self, job, work, at):
        self.job, self.wk, self.at = job, work, at
        self.inputs = job.inputs + work.inputs
        self.out_shapes = list(job.out_shapes) + list(work.out_shapes)
        self.scratch = job.scratch

    def phases(self, ins, outs, scr):
        nj, no = len(self.job.inputs), len(self.job.out_shapes)
        base = self.job.phases(ins[:nj], outs[:no], scr)
        work = self.wk.work(ins[nj:], outs[no:])
        mixed = []
        for k, ph in enumerate(base):
            mixed.append(ph)
            if k < len(work):
                mixed.append(work[k])
        return mixed


def _run_job(job, name):
    ni, no = len(job.inputs), len(job.out_shapes)

    def body(*refs):
        for ph in job.phases(refs[:ni], refs[ni:ni + no], refs[ni + no:]):
            ph()

    return pl.pallas_call(
        body, name=name, in_specs=[ANY] * ni, out_specs=[ANY] * no, out_shape=job.out_shapes, scratch_shapes=job.scratch,
        compiler_params=pltpu.CompilerParams(vmem_limit_bytes=VMEM_LIMIT),
    )(*job.inputs)


def _hosted(body, *, name, grid, in_specs, out_specs, out_shape, scratch_shapes, args, sem, side=None):
    if side is None:
        return pl.pallas_call(body, name=name, grid=grid, in_specs=in_specs, out_specs=out_specs, out_shape=out_shape,
                              scratch_shapes=scratch_shapes, compiler_params=_cp(sem))(*args), None
    job = side
    ni, no, ns = len(in_specs), len(out_specs), len(scratch_shapes)
    ji, jo = len(job.inputs), len(job.out_shapes)
    n_steps = 1
    for extent in grid:
        n_steps *= extent

    def wrapped(*refs):
        own_in, refs = refs[:ni], refs[ni:]
        job_in, refs = refs[:ji], refs[ji:]
        own_out, refs = refs[:no], refs[no:]
        job_out, refs = refs[:jo], refs[jo:]
        own_scr, job_scr = refs[:ns], refs[ns:]
        step = 0
        for d, extent in enumerate(grid):
            step = step * extent + pl.program_id(d)
        phases = job.phases(job_in, job_out, job_scr)
        steps = [min(int(f * n_steps), n_steps - 1) for f in job.at] + [n_steps - 1]
        assert len(steps) == len(phases) and steps == sorted(steps)
        for at, ph in zip(steps, phases):
            pl.when(step == at)(ph)
        body(*own_in, *own_out, *own_scr)

    res = pl.pallas_call(
        wrapped, name=name, grid=grid, in_specs=list(in_specs) + [ANY] * ji, out_specs=list(out_specs) + [ANY] * jo,
        out_shape=list(out_shape) + list(job.out_shapes), scratch_shapes=list(scratch_shapes) + list(job.scratch),
        compiler_params=_cp(("arbitrary",) * len(grid)),
    )(*args, *job.inputs)
    return res[:no], res[no:]


def _proj_dw(xnt, dproj_sh, *, tm=512, tk=2048):
    d, s = xnt.shape
    tk = _tile(s, tk)
    nk = s // tk

    def body(a_ref, b_ref, o_ref, acc):
        def finish(r):
            o_ref[0] = r

        _accumulate(acc, _dot(a_ref[...], b_ref[...]), pl.program_id(2), nk, finish)

    return pl.pallas_call(
        body, name="proj_dw", grid=(N_CHIPS, d // tm, nk),
        in_specs=[pl.BlockSpec((tm, tk), lambda j, i, q: (i, q)), pl.BlockSpec((tk, W_IN_PAD), lambda j, i, q: (q, j))],
        out_specs=pl.BlockSpec((1, tm, W_IN_PAD), lambda j, i, q: (j, i, 0)),
        out_shape=jax.ShapeDtypeStruct((N_CHIPS, d, W_IN_PAD), F32), scratch_shapes=[pltpu.VMEM((tm, W_IN_PAD), F32)],
        compiler_params=_cp(("parallel", "parallel", "arbitrary")),
    )(xnt, dproj_sh)


def _proj_dx(dproj_sh, w_sh, *, tm=1024, side=None):
    s = dproj_sh.shape[0]
    d = w_sh.shape[1]
    tm = _tile(s, tm)

    def body(a_ref, b_ref, o_ref, acc):
        kk = pl.program_id(1)
        part = _dot_nt(a_ref[...], b_ref[0])

        @pl.when(kk == 0)
        def _():
            acc[...] = part

        @pl.when(kk > 0)
        def _():
            acc[...] += part

        @pl.when(kk == N_CHIPS - 1)
        def _():
            o_ref[...] = acc[...]

    own, extra = _hosted(
        body, name="proj_dx", grid=(s // tm, N_CHIPS),
        in_specs=[pl.BlockSpec((tm, W_IN_PAD), lambda i, q: (i, q)), pl.BlockSpec((1, d, W_IN_PAD), lambda i, q: (q, 0, 0))],
        out_specs=[pl.BlockSpec((tm, d), lambda i, q: (i, 0))],
        out_shape=[jax.ShapeDtypeStruct((s, d), F32)], scratch_shapes=[pltpu.VMEM((tm, d), F32)],
        args=(dproj_sh, w_sh), sem=("parallel", "arbitrary"), side=side)
    return own[0] if side is None else (own[0], extra)


def _up_dx(dup, w_sh, *, tm=1024):
    s = dup.shape[1]
    d, wsh = w_sh.shape[1:]
    tm = _tile(s, tm)

    def body(a_ref, b_ref, o_ref, acc):
        kk = pl.program_id(1)
        part = _dot_nt(a_ref[0], b_ref[0])

        @pl.when(kk == 0)
        def _():
            acc[...] = part

        @pl.when(kk > 0)
        def _():
            acc[...] += part

        @pl.when(kk == N_CHIPS - 1)
        def _():
            o_ref[...] = acc[...]

    return pl.pallas_call(
        body, name="up_dx", grid=(s // tm, N_CHIPS),
        in_specs=[pl.BlockSpec((1, tm, wsh), lambda i, q: (q >> 1, i, q & 1)), pl.BlockSpec((1, d, wsh), lambda i, q: (q, 0, 0))],
        out_specs=pl.BlockSpec((tm, d), lambda i, q: (i, 0)),
        out_shape=jax.ShapeDtypeStruct((s, d), F32), scratch_shapes=[pltpu.VMEM((tm, d), F32)],
        compiler_params=_cp(("parallel", "arbitrary")),
    )(dup, w_sh)


def _up_dw(hnt, dup, *, tk=2048):
    d, s = hnt.shape
    wsh = 2 * D_FF // N_CHIPS
    tk = _tile(s, tk)
    nk = s // tk

    def body(a_ref, b_ref, o_ref, acc):
        def finish(r):
            o_ref[0] = r

        _accumulate(acc, _dot(a_ref[...], b_ref[0]), pl.program_id(1), nk, finish)

    return pl.pallas_call(
        body, name="up_dw", grid=(N_CHIPS, nk),
        in_specs=[pl.BlockSpec((d, tk), lambda j, q: (0, q)), pl.BlockSpec((1, tk, wsh), lambda j, q: (j >> 1, q, j & 1))],
        out_specs=pl.BlockSpec((1, d, wsh), lambda j, q: (j, 0, 0)),
        out_shape=jax.ShapeDtypeStruct((N_CHIPS, d, wsh), F32), scratch_shapes=[pltpu.VMEM((d, wsh), F32)],
        compiler_params=_cp(("parallel", "arbitrary")),
    )(hnt, dup)


BIG_ROWS =(IN_DIM // 4, Q_DIM // 4, D_INNER // 4, D_MODEL // 4, 2 * D_FF // 4, D_FF // 4)
PACK_ROWS = 5376


def _pack_shards(parts):
    rows = [p.reshape(-1, D_MODEL) for p in parts]
    pad = PACK_ROWS - sum(BIG_ROWS)
    return jnp.concatenate(rows + [jnp.zeros((pad, D_MODEL), rows[0].dtype)], axis=0)


def _unpack_shards(buf):
    out, off = [], 0
    for n in BIG_ROWS:
        out.append(buf[off:off + n])
        off += n
    return out


def _assemble(srcs, col_map, *, name, tr=256):
    arrays, lead = [], []
    for src in srcs:
        arr, j = src if isinstance(src, tuple) else (src, None)
        if not any(arr is a for a in arrays):
            arrays.append(arr)
        lead.append(([i for i, a in enumerate(arrays) if a is arr][0], j))
    rows = arrays[0].shape[-2]
    tr = _tile(rows, tr)
    out_w = len(col_map)
    tiles = []
    for t in range(out_w // 128):
        runs = []
        for lane in range(128):
            ent = col_map[t * 128 + lane]
            key = None if ent is None else (ent[0], ent[1] // 128, (lane - ent[1]) % 128)
            if runs and runs[-1][0] == key:
                runs[-1][2] = lane + 1
            else:
                runs.append([key, lane, lane + 1])
        tiles.append(runs)

    def body(*refs):
        o_ref = refs[-1]
        lane = lax.broadcasted_iota(jnp.int32, (tr, 128), 1)
        for t, runs in enumerate(tiles):
            acc = jnp.zeros((tr, 128), F32)
            for key, a, b in runs:
                if key is None:
                    continue
                sid, ct, shift = key
                ai, j = lead[sid]
                cols = slice(ct * 128, (ct + 1) * 128)
                piece = (refs[ai][:, cols] if j is None else refs[ai][j, :, cols]).astype(F32)
                if shift:
                    piece = pltpu.roll(piece, shift, 1)
                acc = piece if (a, b) == (0, 128) else jnp.where((lane >= a) & (lane < b), piece, acc)
            o_ref[:, t * 128:(t + 1) * 128] = acc.astype(BF16)

    specs = [pl.BlockSpec((tr, a.shape[1]), lambda i: (i, 0)) if a.ndim == 2
             else pl.BlockSpec((a.shape[0], tr, a.shape[2]), lambda i: (0, i, 0)) for a in arrays]
    return pl.pallas_call(
        body, name=name, grid=(rows // tr,), in_specs=specs, out_specs=pl.BlockSpec((tr, out_w), lambda i: (i, 0)),
        out_shape=jax.ShapeDtypeStruct((rows, out_w), BF16), compiler_params=_cp(("parallel",)),
    )(*arrays)


def _permute_cols_in(w):
    pad = jnp.zeros((w.shape[0], PW - IN_DIM), w.dtype)
    return jnp.concatenate([w[:, :6656], w[:, 6688:], w[:, 6656:6688], pad], axis=1)


def _unpermute_cols_in(g):
    return jnp.concatenate([g[:, :6656], g[:, O_DT:O_DT + 32], g[:, 6656:O_DT]], axis=1)


SMALL = ("norm1_w", "b_gate", "attn_sinks", "ssd_conv_b", "dt_bias", "a_log", "d_skip", "ssd_norm_w", "norm2_w",
         "ffn_conv_b", "final_norm_w", "ssd_conv_w", "ffn_conv_w")


def _pad128(v):
    v = v.reshape(-1)
    return jnp.pad(v, (0, (-v.shape[0]) % 128))


def _pack_small(parts):
    flat = jnp.concatenate([_pad128(p) for p in parts])
    flat = jnp.pad(flat, (0, (-flat.shape[0]) % 1024))
    return flat.reshape(-1, 128)


def _unpack_small(buf, shapes):
    flat, out, off = buf.reshape(-1), [], 0
    for shp in shapes:
        n = 1
        for q in shp:
            n *= q
        out.append(flat[off:off + n].reshape(shp))
        off += n + (-n) % 128
    return out


def _vec128(v):
    return jnp.pad(v.reshape(1, -1), ((0, 0), (0, 128 - v.shape[-1])))


def kernel(x, norm1_w, w_in, b_gate, attn_sinks, w_attn_o, ssd_conv_w, ssd_conv_b, dt_bias, a_log, d_skip, ssd_norm_w, w_ssd_o, w_out, norm2_w, w_up, ffn_conv_w, ffn_conv_b, w_down, final_norm_w, loss_target, m_norm1_w, m_w_in, m_b_gate, m_attn_sinks, m_w_attn_o, m_ssd_conv_w, m_ssd_conv_b, m_dt_bias, m_a_log, m_d_skip, m_ssd_norm_w, m_w_ssd_o, m_w_out, m_norm2_w, m_w_up, m_ffn_conv_w, m_ffn_conv_b, m_w_down, m_final_norm_w, v_norm1_w, v_w_in, v_b_gate, v_attn_sinks, v_w_attn_o, v_ssd_conv_w, v_ssd_conv_b, v_dt_bias, v_a_log, v_d_skip, v_ssd_norm_w, v_w_ssd_o, v_w_out, v_norm2_w, v_w_up, v_ffn_conv_w, v_ffn_conv_b, v_w_down, v_final_norm_w):
    ix, iy, ic = lax.axis_index("x"), lax.axis_index("y"), lax.axis_index("c")
    chip = 2 * ix + iy
    x2 = x[0]
    tgt = loss_target[0]
    s = x2.shape[0]

    wsh = IN_DIM // N_CHIPS
    big_shards = dict(w_in=jnp.pad(w_in[0], ((0, 0), (0, W_IN_PAD - wsh))), w_attn_o=w_attn_o[0], w_ssd_o=w_ssd_o[0],
                      w_out=w_out[0], w_up=w_up[0], w_down=w_down[0])
    gathered = {}
    (xn, xnt), (gathered["w_in"],) = _rms_fwd(x2, norm1_w, name="norm1_fwd", with_t=True,
                                              side=_GatherJob(("w_in",), [big_shards["w_in"]], at=(0.0, 0.5, 0.75)))
    early = ("w_attn_o", "w_ssd_o", "w_out")
    gather_early = _GatherJob(early, [big_shards[n] for n in early], at=(0.0, 0.5, 0.8))
    gather_up = _GatherJob(("w_up",), [big_shards["w_up"]], at=(0.0, 0.55, 0.85))
    gather_down = _GatherJob(("w_down",), [big_shards["w_down"]], at=(0.0, 0.5, 0.8))
    gw = gathered["w_in"]
    perm = list(range(O_GA)) + list(range(O_GA + N_SSD_HEADS, IN_DIM)) + list(range(O_GA, O_GA + N_SSD_HEADS))
    w_in_p = _assemble([(gw, j) for j in range(N_CHIPS)], [divmod(o, wsh) for o in perm] + [None] * (PW - IN_DIM),
                       name="w_in_assemble")
    small_sh = _pack_small([ssd_conv_w[0], ffn_conv_w[0]])
    small_all = _all_gather_small(small_sh)
    sc_parts = [_unpack_small(small_all[j], [(4, XBC_DIM // 4), (3, 2 * D_FF // 4)]) for j in range(N_CHIPS)]
    ssd_cw = jnp.concatenate([p[0] for p in sc_parts], axis=1)
    ffn_cw = jnp.concatenate([p[1] for p in sc_parts], axis=1)

    sinks128 = _vec128(attn_sinks)
    dtb128, alog128, dskip128 = _vec128(dt_bias), _vec128(a_log), _vec128(d_skip)

    proj, got = _mm(xn, w_in_p, name="proj_fwd", tn=1280, side=gather_early)
    gathered.update(zip(early, got))
    qkvt = _mm(w_in_p[:, :O_Z], xnt, name="qkv_fwd", ta=True)
    attn_pre, (gathered["w_up"],) = _attn_fwd(qkvt, sinks128, side=gather_up)
    xbc = _ssd_conv_fwd(proj, ssd_cw, ssd_conv_b)
    (y_ssd, hprev), (gathered["w_down"],) = _ssd_fwd(xbc, proj, dtb128, alog128, dskip128, side=gather_down)
    full = {n: gathered[n].reshape(-1, D_MODEL) for n in ("w_attn_o", "w_ssd_o", "w_out", "w_down")}
    full["w_up"] = gathered["w_up"]
    attn = _mm(attn_pre, full["w_attn_o"], name="attn_o_fwd", ta=True)
    yn = _gate_norm_fwd(y_ssd, proj, ssd_norm_w)
    ssd_out = _mm(yn, full["w_ssd_o"], name="ssd_o_fwd")
    merged = _merge_fwd(proj, b_gate, attn, ssd_out)
    h1 = _mm(merged, full["w_out"], name="out_fwd", resid=x2)
    hn, hnt = _rms_fwd(h1, norm2_w, name="norm2_fwd", with_t=True)
    up = _mm(hn, full["w_up"], name="up_fwd")
    act = _ffn_act_fwd(up, ffn_cw, ffn_conv_b)
    h2 = _mm(act, full["w_down"], name="down_fwd", resid=h1, tk=1408)

    dh2, loss_blk, g_final = _loss_bwd(h2, tgt, final_norm_w.reshape(1, -1))
    dact = _mm(dh2, full["w_down"], name="down_dx", tb=True, tn=1408)
    g_down = _mm(act, dh2, name="down_dw", ta=True, tm=1408)
    dup, g_ffn_cw, g_ffn_cb = _ffn_act_bwd(dact, up, ffn_cw, ffn_conv_b)
    dhn = _up_dx(dup, full["w_up"])
    g_up = _up_dw(hnt, dup)
    dh1, g_norm2 = _rms_bwd(dhn, h1, norm2_w, dh2, name="norm2_bwd")
    dmerged = _mm(dh1, full["w_out"], name="out_dx", tb=True)
    g_out = _mm(merged, dh1, name="out_dw", ta=True)
    dattn, dssd_out, dga, dgs, g_ba, g_bs = _merge_bwd(dmerged, proj, b_gate, attn, ssd_out)
    dyn = _mm(dssd_out, full["w_ssd_o"], name="ssd_o_dx", tb=True)
    g_ssd_o = _mm(yn, dssd_out, name="ssd_o_dw", ta=True)
    dy_ssd, dz, g_ssd_norm = _gate_norm_bwd(dyn, y_ssd, proj, ssd_norm_w)
    slot = lambda g: g.reshape(N_CHIPS, -1, D_MODEL)
    big_grads = {}
    red = ("w_down", "w_up")
    (dxbc, ddt, dvec), got = _ssd_bwd(xbc, proj, dtb128, alog128, dskip128, hprev, dy_ssd,
                                      side=_ReduceJob(red, [slot(g_down), g_up], at=(0.0, 0.2, 0.7, 0.95)))
    big_grads.update(zip(red, got))
    dxbc_raw, g_ssd_cw, g_ssd_cb = _ssd_conv_bwd(dxbc, proj, ssd_cw, ssd_conv_b)
    dattn_pre = _mm(full["w_attn_o"], dattn, name="attn_o_dx", tb=True)
    g_attn_o = _mm(attn_pre, dattn, name="attn_o_dw")
    red = ("w_out", "w_ssd_o", "w_attn_o")
    (dq, dk, dv, dsk), got = _attn_bwd(qkvt, sinks128, attn_pre, dattn_pre,
                                       side=_ReduceJob(red, [slot(g_out), slot(g_ssd_o), slot(g_attn_o)],
                                                       at=(0.0, 0.2, 0.5, 0.7)))
    big_grads.update(zip(red, got))
    pieces = [(dq.T, Q_DIM), (dk.T, KV_DIM), (dv.T, KV_DIM), (dz, D_INNER), (dxbc_raw, XBC_DIM), (ddt, N_SSD_HEADS),
              (dga, D_MODEL), (dgs, D_MODEL)]
    orig = [(i, c) for i, (_, w) in enumerate(pieces) for c in range(w)]
    dproj_sh = _assemble([p for p, _ in pieces],
                         [orig[j * wsh + c] if c < wsh else None for j in range(N_CHIPS) for c in range(W_IN_PAD)],
                         name="dproj_assemble")
    g_in = _proj_dw(xnt, dproj_sh)
    dxn, got = _proj_dx(dproj_sh, gathered["w_in"], side=_ReduceJob(("w_in",), [g_in], at=(0.0, 0.15, 0.75, 0.95)))
    big_grads["w_in"] = got[0]
    dx, g_norm1 = _rms_bwd(dxn, x2, norm1_w, dh1, name="norm1_bwd")


    small_g = dict(
        norm1_w=g_norm1, b_gate=jnp.concatenate([g_ba, g_bs], axis=1), attn_sinks=dsk[0:1, :16], ssd_conv_b=g_ssd_cb,
        dt_bias=dvec[0:1, :32], a_log=dvec[1:2, :32], d_skip=dvec[2:3, :32], ssd_norm_w=g_ssd_norm, norm2_w=g_norm2,
        ffn_conv_b=jnp.concatenate([g_ffn_cb[0], g_ffn_cb[1]], axis=1), final_norm_w=g_final, ssd_conv_w=g_ssd_cw,
        ffn_conv_w=jnp.concatenate([g_ffn_cw[0], g_ffn_cw[1]], axis=1))
    small_buf = _pack_small([small_g[n] for n in SMALL] + [loss_blk])
    small_sum = _all_reduce_small(small_buf)
    small_shapes = [(1, D_MODEL), (1, 2 * D_MODEL), (1, 16), (1, XBC_DIM), (1, 32), (1, 32), (1, 32), (1, D_INNER),
                    (1, D_MODEL), (1, 2 * D_FF), (D_MODEL,), (4, XBC_DIM), (3, 2 * D_FF), (1, 128)]
    small_list = _unpack_small(small_sum, small_shapes)
    loss = small_list[-1][0, 0]
    grads = dict(zip(SMALL, small_list[:-1]))
    grads["ssd_conv_w"] = lax.dynamic_slice_in_dim(grads["ssd_conv_w"], chip * (XBC_DIM // 4), XBC_DIM // 4, axis=1)
    grads["ffn_conv_w"] = lax.dynamic_slice_in_dim(grads["ffn_conv_w"], chip * (2 * D_FF // 4), 2 * D_FF // 4, axis=1)
    grads.update(big_grads)

    weights = dict(norm1_w=norm1_w, w_in=w_in, b_gate=b_gate, attn_sinks=attn_sinks, w_attn_o=w_attn_o, ssd_conv_w=ssd_conv_w,
                   ssd_conv_b=ssd_conv_b, dt_bias=dt_bias, a_log=a_log, d_skip=d_skip, ssd_norm_w=ssd_norm_w, w_ssd_o=w_ssd_o,
                   w_out=w_out, norm2_w=norm2_w, w_up=w_up, ffn_conv_w=ffn_conv_w, ffn_conv_b=ffn_conv_b, w_down=w_down,
                   final_norm_w=final_norm_w)
    ms = dict(norm1_w=m_norm1_w, w_in=m_w_in, b_gate=m_b_gate, attn_sinks=m_attn_sinks, w_attn_o=m_w_attn_o,
              ssd_conv_w=m_ssd_conv_w, ssd_conv_b=m_ssd_conv_b, dt_bias=m_dt_bias, a_log=m_a_log, d_skip=m_d_skip,
              ssd_norm_w=m_ssd_norm_w, w_ssd_o=m_w_ssd_o, w_out=m_w_out, norm2_w=m_norm2_w, w_up=m_w_up,
              ffn_conv_w=m_ffn_conv_w, ffn_conv_b=m_ffn_conv_b, w_down=m_w_down, final_norm_w=m_final_norm_w)
    vs = dict(norm1_w=v_norm1_w, w_in=v_w_in, b_gate=v_b_gate, attn_sinks=v_attn_sinks, w_attn_o=v_w_attn_o,
              ssd_conv_w=v_ssd_conv_w, ssd_conv_b=v_ssd_conv_b, dt_bias=v_dt_bias, a_log=v_a_log, d_skip=v_d_skip,
              ssd_norm_w=v_ssd_norm_w, w_ssd_o=v_w_ssd_o, w_out=v_w_out, norm2_w=v_norm2_w, w_up=v_w_up,
              ffn_conv_w=v_ffn_conv_w, ffn_conv_b=v_ffn_conv_b, w_down=v_w_down, final_norm_w=v_final_norm_w)
    order = list(weights)
    deltas, new_m, new_v = {}, {}, {}
    for n in BIG:
        shp = weights[n].shape
        res = _adamw(weights[n][0], grads[n], ms[n][0], vs[n][0], name="adamw_" + n)
        deltas[n], new_m[n], new_v[n], grads[n] = (a.reshape(shp) for a in res)
    smalls = [n for n in order if n not in BIG]
    as2d = lambda a: a.reshape(-1, a.shape[-1])
    res = _adamw_many(*[[as2d(src[n][0] if src[n].ndim == 3 else src[n]) for n in smalls] for src in (weights, grads, ms, vs)])
    for i, n in enumerate(smalls):
        deltas[n], new_m[n], new_v[n] = (res[q * len(smalls) + i].reshape(weights[n].shape) for q in range(3))
    out_grads = [grads[n].reshape(weights[n].shape) for n in order]
    return (loss, dx[None], *out_grads, *[deltas[n] for n in order], *[new_m[n] for n in order], *[new_v[n] for n in order])
```

```python
import functools

import jax
import jax.numpy as jnp
from jax import lax
from jax.experimental import pallas as pl
from jax.experimental.pallas import tpu as pltpu

F32 = jnp.float32
BF16 = jnp.bfloat16
HI = lax.Precision.HIGHEST

D_MODEL = 1024
Q_DIM = 1024
KV_DIM = 256
D_INNER = 2048
BC_DIM = 512
XBC_DIM = 3072
N_SSD_HEADS = 32
D_FF = 2816
IN_DIM = 8736
BLK = 128
EPS = 1e-5
NEG = -1e30

O_Q, O_K, O_V, O_Z, O_X, O_GA, O_GS, O_DT = 0, 1024, 1280, 1536, 3584, 6656, 7680, 8704
PW = 8960

ADAM_LR, ADAM_B1, ADAM_B2, ADAM_EPS, ADAM_WD, ADAM_STEP = 0.001, 0.9, 0.999, 1e-08, 0.01, 10

VMEM_LIMIT = 52 * 1024 * 1024
MESH = pl.DeviceIdType.MESH


def _cp(sem=None):
    return pltpu.CompilerParams(dimension_semantics=sem, vmem_limit_bytes=VMEM_LIMIT)


def _dot(a, b, prec=None):
    return jnp.dot(a, b, preferred_element_type=F32, precision=prec)


def _dot_nt(a, b, prec=None):
    return lax.dot_general(a, b, (((1,), (1,)), ((), ())), preferred_element_type=F32, precision=prec)


def _dot_tn(a, b, prec=None):
    return lax.dot_general(a, b, (((0,), (0,)), ((), ())), preferred_element_type=F32, precision=prec)


def _sigmoid(x):
    return 0.5 * jnp.tanh(0.5 * x) + 0.5


def _tile(n, want):
    t = min(n, want)
    while n % t:
        t -= 128
    return t


def _accumulate(acc, part, kk, nk, finish):
    if nk == 1:
        finish(part)
        return

    @pl.when(kk == 0)
    def _():
        acc[...] = part

    @pl.when(kk > 0)
    def _():
        acc[...] += part

    @pl.when(kk == nk - 1)
    def _():
        finish(acc[...])


def _mm(a, b, *, name, ta=False, tb=False, out_dtype=F32, resid=None, tm=1024, tn=1024, tk=1024, side=None):
    m, k = (a.shape[1], a.shape[0]) if ta else a.shape
    slots = b.ndim == 3
    if slots:
        n = b.shape[1] if tb else b.shape[0] * b.shape[2]
        tn, tk = (tn, b.shape[2]) if tb else (b.shape[2], tk)
    else:
        n = b.shape[0] if tb else b.shape[1]
    tm, tn, tk = _tile(m, tm), _tile(n, tn), _tile(k, tk)
    nk = k // tk
    dn = (((0 if ta else 1,), (1 if tb else 0,)), ((), ()))

    def body(*refs):
        if resid is None:
            a_ref, b_ref, o_ref, acc = refs
        else:
            a_ref, b_ref, r_ref, o_ref, acc = refs
        kk = pl.program_id(2)
        bv = b_ref[0] if slots else b_ref[...]
        part = lax.dot_general(a_ref[...].astype(BF16), bv.astype(BF16), dn, preferred_element_type=F32)

        def finish(r):
            if resid is not None:
                r = r + r_ref[...]
            o_ref[...] = r.astype(out_dtype)

        _accumulate(acc, part, kk, nk, finish)

    a_spec = pl.BlockSpec((tk, tm), lambda i, j, q: (q, i)) if ta else pl.BlockSpec((tm, tk), lambda i, j, q: (i, q))
    if slots:
        b_spec = (pl.BlockSpec((1, tn, tk), lambda i, j, q: (q, j, 0)) if tb
                  else pl.BlockSpec((1, tk, tn), lambda i, j, q: (j, q, 0)))
    else:
        b_spec = pl.BlockSpec((tn, tk), lambda i, j, q: (j, q)) if tb else pl.BlockSpec((tk, tn), lambda i, j, q: (q, j))
    o_spec = pl.BlockSpec((tm, tn), lambda i, j, q: (i, j))
    ins, specs = [a, b], [a_spec, b_spec]
    if resid is not None:
        ins.append(resid)
        specs.append(o_spec)
    own, extra = _hosted(
        body, name=name, grid=(m // tm, n // tn, nk), in_specs=specs, out_specs=[o_spec],
        out_shape=[jax.ShapeDtypeStruct((m, n), out_dtype)], scratch_shapes=[pltpu.VMEM((tm, tn), F32)],
        args=ins, sem=("parallel", "parallel", "arbitrary"), side=side)
    return own[0] if side is None else (own[0], extra)


def _rms_fwd(x, w, *, name, tm=512, with_t=False, side=None):
    s, d = x.shape
    tm = _tile(s, tm)

    def body(x_ref, w_ref, o_ref, *t_ref):
        xv = x_ref[...]
        r = lax.rsqrt(jnp.mean(xv * xv, axis=-1, keepdims=True) + EPS)
        y = (xv * r) * w_ref[...]
        o_ref[...] = y.astype(BF16)
        if with_t:
            t_ref[0][...] = y.T.astype(BF16)

    row = pl.BlockSpec((tm, d), lambda i: (i, 0))
    res, extra = _hosted(
        body, name=name, grid=(s // tm,), in_specs=[row, pl.BlockSpec((1, d), lambda i: (0, 0))],
        out_specs=[row] + [pl.BlockSpec((d, tm), lambda i: (0, i))] * with_t,
        out_shape=[jax.ShapeDtypeStruct((s, d), BF16)] + [jax.ShapeDtypeStruct((d, s), BF16)] * with_t,
        scratch_shapes=[], args=(x, w), sem=("parallel",), side=side)
    res = res if with_t else res[0]
    return res if side is None else (res, extra)


def _rms_bwd(dy, x, w, resid, *, name, tm=512):
    s, d = x.shape
    tm = _tile(s, tm)

    def body(dy_ref, x_ref, w_ref, r_ref, dx_ref, dw_ref):
        i = pl.program_id(0)
        xv = x_ref[...]
        r = lax.rsqrt(jnp.mean(xv * xv, axis=-1, keepdims=True) + EPS)
        xh = xv * r
        dyv = dy_ref[...]
        g = dyv * w_ref[...]
        dx_ref[...] = r_ref[...] + r * (g - xh * jnp.mean(g * xh, axis=-1, keepdims=True))
        part = jnp.sum(dyv * xh, axis=0, keepdims=True)

        @pl.when(i == 0)
        def _():
            dw_ref[...] = part

        @pl.when(i > 0)
        def _():
            dw_ref[...] += part

    row = pl.BlockSpec((tm, d), lambda i: (i, 0))
    vec = pl.BlockSpec((1, d), lambda i: (0, 0))
    return pl.pallas_call(
        body, name=name, grid=(s // tm,), in_specs=[row, row, vec, row], out_specs=[row, vec],
        out_shape=[jax.ShapeDtypeStruct((s, d), F32), jax.ShapeDtypeStruct((1, d), F32)],
        compiler_params=_cp(("arbitrary",)),
    )(dy, x, w, resid)


def _loss_bwd(h2, tgt, wf, *, tm=512):
    s, d = h2.shape
    tm = _tile(s, tm)

    def body(h_ref, t_ref, w_ref, dh_ref, loss_ref, dw_ref):
        i = pl.program_id(0)
        hv = h_ref[...]
        r = lax.rsqrt(jnp.mean(hv * hv, axis=-1, keepdims=True) + EPS)
        xh = hv * r
        wv = w_ref[...]
        e = xh * wv - t_ref[...]
        lpart = 0.5 * jnp.sum(jnp.mean(e * e, axis=-1, keepdims=True), axis=0, keepdims=True)
        dout = e * (1.0 / d)
        g = dout * wv
        dh_ref[...] = r * (g - xh * jnp.mean(g * xh, axis=-1, keepdims=True))
        part = jnp.sum(dout * xh, axis=0, keepdims=True)
        lrow = jnp.broadcast_to(lpart, (1, 128))

        @pl.when(i == 0)
        def _():
            dw_ref[...] = part
            loss_ref[...] = lrow

        @pl.when(i > 0)
        def _():
            dw_ref[...] += part
            loss_ref[...] += lrow

    row = pl.BlockSpec((tm, d), lambda i: (i, 0))
    vec = pl.BlockSpec((1, d), lambda i: (0, 0))
    return pl.pallas_call(
        body, name="loss_bwd", grid=(s // tm,), in_specs=[row, row, vec],
        out_specs=[row, pl.BlockSpec((1, 128), lambda i: (0, 0)), vec],
        out_shape=[jax.ShapeDtypeStruct((s, d), F32), jax.ShapeDtypeStruct((1, 128), F32),
                   jax.ShapeDtypeStruct((1, d), F32)],
        compiler_params=_cp(("arbitrary",)),
    )(h2, tgt, wf)


def _attn_mask(n):
    si = lax.broadcasted_iota(jnp.int32, (2 * BLK, 4 * BLK), 0)
    qi = lax.broadcasted_iota(jnp.int32, (2 * BLK, 4 * BLK), 1) & (BLK - 1)
    dist = BLK + qi - si
    kpos = n * BLK - BLK + si
    return (dist >= 0) & (dist < BLK) & (kpos >= 0)


def _attn_probs(q_ref, kc_ref, kp_ref, sk_ref, kvh, valid):
    rows = slice(kvh * 64, (kvh + 1) * 64)
    kt = jnp.concatenate([kp_ref[rows, :], kc_ref[rows, :]], axis=1).astype(BF16)
    qt = jnp.concatenate([q_ref[(kvh * 4 + g) * 64:(kvh * 4 + g + 1) * 64, :] for g in range(4)], axis=1).astype(BF16)
    s = _dot_tn(kt, qt) * 0.125
    s = jnp.where(valid, s, NEG)
    head = lax.broadcasted_iota(jnp.int32, (1, 4 * BLK), 1) >> 7
    sink = jnp.zeros((1, 4 * BLK), F32)
    for g in range(4):
        sink = jnp.where(head == g, sk_ref[0:1, kvh * 4 + g:kvh * 4 + g + 1], sink)
    m = jnp.maximum(jnp.max(s, axis=0, keepdims=True), sink)
    p = jnp.where(valid, jnp.exp(s - m), 0.0)
    es = jnp.exp(sink - m)
    inv = 1.0 / (jnp.sum(p, axis=0, keepdims=True) + es)
    return qt, kt, p * inv, es * inv


def _attn_in_specs(cur, prev):
    return [pl.BlockSpec((Q_DIM, BLK), lambda n: (0, cur(n))),
            pl.BlockSpec((KV_DIM, BLK), lambda n: (O_K // KV_DIM, cur(n))),
            pl.BlockSpec((KV_DIM, BLK), lambda n: (O_K // KV_DIM, prev(n))),
            pl.BlockSpec((KV_DIM, BLK), lambda n: (O_V // KV_DIM, cur(n))),
            pl.BlockSpec((KV_DIM, BLK), lambda n: (O_V // KV_DIM, prev(n))),
            pl.BlockSpec((1, 128), lambda n: (0, 0))]


def _attn_fwd(qkvt, sinks, side=None):
    s = qkvt.shape[1]
    nb = s // BLK

    def body(q_ref, kc_ref, kp_ref, vc_ref, vp_ref, sk_ref, o_ref):
        valid = _attn_mask(pl.program_id(0))
        for kvh in range(4):
            rows = slice(kvh * 64, (kvh + 1) * 64)
            _, _, probs, _ = _attn_probs(q_ref, kc_ref, kp_ref, sk_ref, kvh, valid)
            vt = jnp.concatenate([vp_ref[rows, :], vc_ref[rows, :]], axis=1).astype(BF16)
            o = _dot(vt, probs.astype(BF16))
            for g in range(4):
                h = kvh * 4 + g
                o_ref[h * 64:(h + 1) * 64, :] = o[:, g * BLK:(g + 1) * BLK].astype(BF16)

    own, extra = _hosted(
        body, name="attn_fwd", grid=(nb,), in_specs=_attn_in_specs(lambda n: n, lambda n: jnp.maximum(n - 1, 0)),
        out_specs=[pl.BlockSpec((Q_DIM, BLK), lambda n: (0, n))],
        out_shape=[jax.ShapeDtypeStruct((Q_DIM, s), BF16)], scratch_shapes=[],
        args=(qkvt, qkvt, qkvt, qkvt, qkvt, sinks), sem=("parallel",), side=side)
    return own[0] if side is None else (own[0], extra)


def _attn_bwd(qkvt, sinks, o, do, side=None):
    s = qkvt.shape[1]
    nb = s // BLK

    def body(q_ref, kc_ref, kp_ref, vc_ref, vp_ref, sk_ref, o_ref, do_ref, dq_ref, dk_ref, dv_ref, dsk_ref, ck, cv, nk, nv):
        n = pl.program_id(0)

        @pl.when(n == 0)
        def _():
            ck[...] = jnp.zeros_like(ck)
            cv[...] = jnp.zeros_like(cv)
            dsk_ref[...] = jnp.zeros_like(dsk_ref)

        @pl.when(n < nb)
        def _():
            valid = _attn_mask(n)
            lane = lax.broadcasted_iota(jnp.int32, (1, 128), 1)
            dsk = jnp.zeros((1, 128), F32)
            for kvh in range(4):
                rows = slice(kvh * 64, (kvh + 1) * 64)
                qt, kt, probs, psink = _attn_probs(q_ref, kc_ref, kp_ref, sk_ref, kvh, valid)
                vt = jnp.concatenate([vp_ref[rows, :], vc_ref[rows, :]], axis=1).astype(BF16)
                heads = [slice((kvh * 4 + g) * 64, (kvh * 4 + g + 1) * 64) for g in range(4)]
                dot = jnp.concatenate([do_ref[hh, :] for hh in heads], axis=1)
                ot = jnp.concatenate([o_ref[hh, :] for hh in heads], axis=1).astype(F32)
                delta = jnp.sum(dot * ot, axis=0, keepdims=True)
                dot16 = dot.astype(BF16)
                dp = _dot_tn(vt, dot16)
                ds = (probs * (dp - delta) * 0.125).astype(BF16)
                dqt = _dot(kt, ds)
                nk[rows, :] = _dot_nt(qt, ds)
                nv[rows, :] = _dot_nt(dot16, probs.astype(BF16))
                sd = psink * delta
                for g in range(4):
                    dq_ref[heads[g], :] = dqt[:, g * BLK:(g + 1) * BLK].astype(BF16)
                    val = -jnp.sum(sd[:, g * BLK:(g + 1) * BLK], axis=1, keepdims=True)
                    dsk = dsk + jnp.where(lane == kvh * 4 + g, val, 0.0)
            dsk_ref[0:1, :] += dsk
            dk_ref[...] = (ck[...] + nk[:, :BLK]).astype(BF16)
            dv_ref[...] = (cv[...] + nv[:, :BLK]).astype(BF16)
            ck[...] = nk[:, BLK:]
            cv[...] = nv[:, BLK:]

        @pl.when(n == nb)
        def _():
            dk_ref[...] = ck[...].astype(BF16)
            dv_ref[...] = cv[...].astype(BF16)

    cur = lambda n: jnp.minimum(n, nb - 1)
    prev = lambda n: jnp.maximum(jnp.minimum(n, nb - 1) - 1, 0)
    outb = lambda n: jnp.maximum(n - 1, 0)
    own, extra = _hosted(
        body, name="attn_bwd", grid=(nb + 1,),
        in_specs=_attn_in_specs(cur, prev) + [pl.BlockSpec((Q_DIM, BLK), lambda n: (0, cur(n))),
                                              pl.BlockSpec((Q_DIM, BLK), lambda n: (0, cur(n)))],
        out_specs=[pl.BlockSpec((Q_DIM, BLK), lambda n: (0, cur(n))),
                   pl.BlockSpec((KV_DIM, BLK), lambda n: (0, outb(n))),
                   pl.BlockSpec((KV_DIM, BLK), lambda n: (0, outb(n))),
                   pl.BlockSpec((8, 128), lambda n: (0, 0))],
        out_shape=[jax.ShapeDtypeStruct((Q_DIM, s), BF16), jax.ShapeDtypeStruct((KV_DIM, s), BF16),
                   jax.ShapeDtypeStruct((KV_DIM, s), BF16), jax.ShapeDtypeStruct((8, 128), F32)],
        scratch_shapes=[pltpu.VMEM((KV_DIM, BLK), F32)] * 2 + [pltpu.VMEM((KV_DIM, 2 * BLK), F32)] * 2,
        args=(qkvt, qkvt, qkvt, qkvt, qkvt, sinks, o, do), sem=("arbitrary",), side=side)
    return own if side is None else (own, extra)


def _shift_down(x, j):
    if j == 0:
        return x
    row = lax.broadcasted_iota(jnp.int32, x.shape, 0)
    return jnp.where(row >= j, pltpu.roll(x, j, 0), 0.0)


def _shift_up(x, j):
    if j == 0:
        return x
    s = x.shape[0]
    row = lax.broadcasted_iota(jnp.int32, x.shape, 0)
    return jnp.where(row < s - j, pltpu.roll(x, s - j, 0), 0.0)


def _conv(x, w_ref, b_ref):
    kk = w_ref.shape[0]
    y = _shift_down(x, kk - 1) * w_ref[0:1, :]
    for q in range(1, kk):
        y = y + _shift_down(x, kk - 1 - q) * w_ref[q:q + 1, :]
    return y + b_ref[...]


def _conv_bwd(dy, x, w_ref, dx_dtype):
    kk = w_ref.shape[0]
    dx = _shift_up(dy, kk - 1) * w_ref[0:1, :]
    dws = [jnp.sum(dy * _shift_down(x, kk - 1), axis=0, keepdims=True)]
    for q in range(1, kk):
        dx = dx + _shift_up(dy, kk - 1 - q) * w_ref[q:q + 1, :]
        dws.append(jnp.sum(dy * _shift_down(x, kk - 1 - q), axis=0, keepdims=True))
    return dx.astype(dx_dtype), dws, jnp.sum(dy, axis=0, keepdims=True)


def _dsilu(y, sg):
    return sg * (1.0 + y * (1.0 - sg))


CT = 256


def _ssd_conv_fwd(proj, w, b):
    s = proj.shape[0]

    def body(x_ref, w_ref, b_ref, o_ref):
        y = _conv(x_ref[...], w_ref, b_ref)
        o_ref[...] = y * _sigmoid(y)

    return pl.pallas_call(
        body, name="ssd_conv_fwd", grid=(XBC_DIM // CT,),
        in_specs=[pl.BlockSpec((s, CT), lambda i: (0, O_X // CT + i)), pl.BlockSpec((4, CT), lambda i: (0, i)),
                  pl.BlockSpec((1, CT), lambda i: (0, i))],
        out_specs=pl.BlockSpec((s, CT), lambda i: (0, i)),
        out_shape=jax.ShapeDtypeStruct((s, XBC_DIM), F32), compiler_params=_cp(("parallel",)),
    )(proj, w, b)


def _ssd_conv_bwd(dact, proj, w, b):
    s = proj.shape[0]

    def body(d_ref, x_ref, w_ref, b_ref, dx_ref, dw_ref, db_ref):
        x = x_ref[...]
        y = _conv(x, w_ref, b_ref)
        dy = d_ref[...] * _dsilu(y, _sigmoid(y))
        dx, dws, db = _conv_bwd(dy, x, w_ref, BF16)
        dx_ref[...] = dx
        for q in range(4):
            dw_ref[q:q + 1, :] = dws[q]
        db_ref[...] = db

    return pl.pallas_call(
        body, name="ssd_conv_bwd", grid=(XBC_DIM // CT,),
        in_specs=[pl.BlockSpec((s, CT), lambda i: (0, i)), pl.BlockSpec((s, CT), lambda i: (0, O_X // CT + i)),
                  pl.BlockSpec((4, CT), lambda i: (0, i)), pl.BlockSpec((1, CT), lambda i: (0, i))],
        out_specs=[pl.BlockSpec((s, CT), lambda i: (0, i)), pl.BlockSpec((4, CT), lambda i: (0, i)),
                   pl.BlockSpec((1, CT), lambda i: (0, i))],
        out_shape=[jax.ShapeDtypeStruct((s, XBC_DIM), BF16), jax.ShapeDtypeStruct((4, XBC_DIM), F32),
                   jax.ShapeDtypeStruct((1, XBC_DIM), F32)],
        compiler_params=_cp(("parallel",)),
    )(dact, proj, w, b)


NFT = D_FF // CT


def _ffn_act_fwd(up, w, b):
    s = up.shape[0]

    def body(v_ref, g_ref, wv_ref, wg_ref, bv_ref, bg_ref, o_ref):
        val = _conv(v_ref[...], wv_ref, bv_ref)
        gt = _conv(g_ref[...], wg_ref, bg_ref)
        o_ref[...] = ((gt * _sigmoid(gt)) * val).astype(BF16)

    col = lambda off: (lambda i: (0, off + i))
    return pl.pallas_call(
        body, name="ffn_act_fwd", grid=(NFT,),
        in_specs=[pl.BlockSpec((s, CT), col(0)), pl.BlockSpec((s, CT), col(NFT)),
                  pl.BlockSpec((3, CT), col(0)), pl.BlockSpec((3, CT), col(NFT)),
                  pl.BlockSpec((1, CT), col(0)), pl.BlockSpec((1, CT), col(NFT))],
        out_specs=pl.BlockSpec((s, CT), col(0)),
        out_shape=jax.ShapeDtypeStruct((s, D_FF), BF16), compiler_params=_cp(("parallel",)),
    )(up, up, w, w, b, b)


def _ffn_act_bwd(dact, up, w, b):
    s = up.shape[0]

    def body(d_ref, v_ref, g_ref, wv_ref, wg_ref, bv_ref, bg_ref, dx_ref, dw_ref, db_ref):
        xv, xg = v_ref[...], g_ref[...]
        val = _conv(xv, wv_ref, bv_ref)
        gt = _conv(xg, wg_ref, bg_ref)
        sg = _sigmoid(gt)
        d = d_ref[...]
        for half, (dy, x, w_ref) in enumerate(((d * (gt * sg), xv, wv_ref), (d * val * _dsilu(gt, sg), xg, wg_ref))):
            dx, dws, db = _conv_bwd(dy, x, w_ref, BF16)
            dx_ref[half] = dx
            for q in range(3):
                dw_ref[half, q:q + 1, :] = dws[q]
            db_ref[half] = db

    col = lambda off: (lambda i: (0, off + i))
    both = lambda i: (0, 0, i)
    return pl.pallas_call(
        body, name="ffn_act_bwd", grid=(NFT,),
        in_specs=[pl.BlockSpec((s, CT), col(0)), pl.BlockSpec((s, CT), col(0)), pl.BlockSpec((s, CT), col(NFT)),
                  pl.BlockSpec((3, CT), col(0)), pl.BlockSpec((3, CT), col(NFT)),
                  pl.BlockSpec((1, CT), col(0)), pl.BlockSpec((1, CT), col(NFT))],
        out_specs=[pl.BlockSpec((2, s, CT), both), pl.BlockSpec((2, 3, CT), both), pl.BlockSpec((2, 1, CT), both)],
        out_shape=[jax.ShapeDtypeStruct((2, s, D_FF), BF16), jax.ShapeDtypeStruct((2, 3, D_FF), F32),
                   jax.ShapeDtypeStruct((2, 1, D_FF), F32)],
        compiler_params=_cp(("parallel",)),
    )(dact, up, up, w, w, b, b)


def _expand_mat():
    r = lax.broadcasted_iota(jnp.int32, (128, D_INNER), 0)
    c = lax.broadcasted_iota(jnp.int32, (128, D_INNER), 1)
    return ((c >> 6) == r).astype(BF16)


def _reduce_mat():
    r = lax.broadcasted_iota(jnp.int32, (D_INNER, 128), 0)
    c = lax.broadcasted_iota(jnp.int32, (D_INNER, 128), 1)
    return ((r >> 6) == c).astype(BF16)


def _split(v, parts):
    out = []
    for _ in range(parts - 1):
        p = v.astype(BF16)
        out.append(p)
        v = v - p.astype(F32)
    out.append(v.astype(BF16))
    return out


def _sel_dot(v, sel, parts):
    acc = None
    for p in reversed(_split(v, parts)):
        t = _dot(p, sel)
        acc = t if acc is None else acc + t
    return acc


def _row8(v):
    return jnp.broadcast_to(v, (8, v.shape[1]))


def _tril():
    r = lax.broadcasted_iota(jnp.int32, (BLK, BLK), 0)
    c = lax.broadcasted_iota(jnp.int32, (BLK, BLK), 1)
    return r >= c


def _softplus(x):
    return jnp.maximum(x, 0.0) + jnp.log(1.0 + jnp.exp(-jnp.abs(x)))


def _ssd_common(dtraw_ref, dtb_ref, alog_ref):
    causal = _tril()
    e_mat = _expand_mat()
    a_neg = -jnp.exp(alog_ref[...])
    dt = _softplus(dtraw_ref[...] + dtb_ref[...])
    a_cs = _dot(causal.astype(F32), dt * a_neg, HI)
    a_cs_t = a_cs.T
    dt_x = _sel_dot(dt, e_mat, 3)
    acs_x = _sel_dot(a_cs, e_mat, 3)
    alast_x = acs_x[BLK - 1:BLK, :]
    ea_x = jnp.exp(acs_x)
    ds_x = jnp.exp(alast_x - acs_x)
    elast_x = jnp.exp(alast_x)
    return causal, e_mat, a_neg, dt, a_cs, a_cs_t, dt_x, ea_x, ds_x, elast_x


def _decay(a_cs, a_cs_t, h, causal):
    seg = a_cs[:, h:h + 1] - a_cs_t[h:h + 1, :]
    return jnp.where(causal, jnp.exp(jnp.where(causal, seg, 0.0)), 0.0)


def _ssd_fwd(xbc, proj, dt_bias, a_log, d_skip, side=None):
    s = xbc.shape[0]
    nc = s // BLK

    def body(xs_ref, b_ref, c_ref, dtraw_ref, dtb_ref, alog_ref, dskip_ref, y_ref, hp_ref, h_scr, xc16):
        @pl.when(pl.program_id(0) == 0)
        def _():
            h_scr[...] = jnp.zeros_like(h_scr)

        causal, e_mat, _, _, a_cs, a_cs_t, dt_x, ea_x, ds_x, elast_x = _ssd_common(dtraw_ref, dtb_ref, alog_ref)
        dskip_x = _sel_dot(_row8(dskip_ref[...]), e_mat, 3)[0:1]
        xs = xs_ref[...]
        xc = xs * dt_x
        xc16[...] = xc.astype(BF16)
        xcd = (xc * ds_x).astype(BF16)
        hp_ref[0] = h_scr[...]
        for g in range(4):
            gs = slice(g * 512, (g + 1) * 512)
            cg = c_ref[:, g * 128:(g + 1) * 128].astype(BF16)
            bg = b_ref[:, g * 128:(g + 1) * 128].astype(BF16)
            cb = _dot_nt(cg, bg)
            hg = h_scr[:, gs]
            yoff = _dot(cg, hg.astype(BF16)) * ea_x[:, gs]
            for j in range(8):
                h = g * 8 + j
                hsl = slice(h * 64, (h + 1) * 64)
                mm = (cb * _decay(a_cs, a_cs_t, h, causal)).astype(BF16)
                y_ref[:, hsl] = _dot(mm, xc16[:, hsl])
            y_ref[:, gs] += yoff + xs[:, gs] * dskip_x[:, gs]
            h_scr[:, gs] = hg * elast_x[:, gs] + _dot_tn(bg, xcd[:, gs])

    vec = pl.BlockSpec((1, 128), lambda c: (0, 0))
    own, extra = _hosted(
        body, name="ssd_fwd", grid=(nc,),
        in_specs=[pl.BlockSpec((BLK, D_INNER), lambda c: (c, 0)),
                  pl.BlockSpec((BLK, BC_DIM), lambda c: (c, D_INNER // BC_DIM)),
                  pl.BlockSpec((BLK, BC_DIM), lambda c: (c, D_INNER // BC_DIM + 1)),
                  pl.BlockSpec((BLK, 128), lambda c: (c, O_DT // 128)), vec, vec, vec],
        out_specs=[pl.BlockSpec((BLK, D_INNER), lambda c: (c, 0)),
                   pl.BlockSpec((1, 128, D_INNER), lambda c: (c, 0, 0))],
        out_shape=[jax.ShapeDtypeStruct((s, D_INNER), F32), jax.ShapeDtypeStruct((nc, 128, D_INNER), F32)],
        scratch_shapes=[pltpu.VMEM((128, D_INNER), F32), pltpu.VMEM((BLK, D_INNER), BF16)],
        args=(xbc, xbc, xbc, proj, dt_bias, a_log, d_skip), sem=("arbitrary",), side=side)
    return own if side is None else (own, extra)


def _ssd_bwd(xbc, proj, dt_bias, a_log, d_skip, hprev, dy, side=None):
    s = xbc.shape[0]
    nc = s // BLK

    def body(xs_ref, b_ref, c_ref, dtraw_ref, dtb_ref, alog_ref, dskip_ref, hp_ref, dy_ref,
             dxbc_ref, ddt_ref, dvec_ref, dh_scr, xc16, dy16, dxc_scr, dacs_r, tdiff):
        step = pl.program_id(0)
        dacs_r[...] = jnp.zeros_like(dacs_r)

        @pl.when(step == 0)
        def _():
            dh_scr[...] = jnp.zeros_like(dh_scr)
            dvec_ref[...] = jnp.zeros_like(dvec_ref)

        causal, e_mat, a_neg, dt, a_cs, a_cs_t, dt_x, ea_x, ds_x, elast_x = _ssd_common(dtraw_ref, dtb_ref, alog_ref)
        r_mat = _reduce_mat()
        lane = lax.broadcasted_iota(jnp.int32, (1, 128), 1)
        dskip_x = _sel_dot(_row8(dskip_ref[...]), e_mat, 3)[0:1]
        xs = xs_ref[...]
        dy = dy_ref[...]
        xc = xs * dt_x
        xcd = xc * ds_x
        xc16[...] = xc.astype(BF16)
        dy16[...] = dy.astype(BF16)
        dyea = dy * ea_x
        dh = dh_scr[...]
        hp = hp_ref[0]
        dalast_x = jnp.sum(dh * hp, axis=0, keepdims=True) * elast_x
        dacs = jnp.zeros((BLK, 128), F32)
        for g in range(4):
            gs = slice(g * 512, (g + 1) * 512)
            bsl = slice(g * 128, (g + 1) * 128)
            cg = c_ref[:, bsl].astype(BF16)
            bg = b_ref[:, bsl].astype(BF16)
            cb = _dot_nt(cg, bg)
            hg16 = hp[:, gs].astype(BF16)
            dhg16 = dh[:, gs].astype(BF16)
            raw = _dot(cg, hg16)
            draw16 = dyea[:, gs].astype(BF16)
            dcg = _dot_nt(draw16, hg16)
            dhp_g = _dot_tn(cg, draw16)
            dbg = _dot_nt(xcd[:, gs].astype(BF16), dhg16)
            dxcd = _dot(bg, dhg16)
            dcb = jnp.zeros((BLK, BLK), F32)
            for j in range(8):
                h = g * 8 + j
                hsl = slice(h * 64, (h + 1) * 64)
                decay = _decay(a_cs, a_cs_t, h, causal)
                m = cb * decay
                dm = _dot_nt(dy16[:, hsl], xc16[:, hsl])
                dxc_scr[:, hsl] = _dot_tn(m.astype(BF16), dy16[:, hsl])
                dcb = dcb + dm * decay
                dseg = dm * m
                oneh = jnp.where(lane == h, 1.0, 0.0)
                dacs = dacs + jnp.sum(dseg, axis=1, keepdims=True) * oneh
                dacs_r[h:h + 1, :] = jnp.sum(dseg, axis=0, keepdims=True)
            dcb16 = dcb.astype(BF16)
            dcg = dcg + _dot(dcb16, bg)
            dbg = dbg + _dot_tn(dcb16, cg)
            dxbc_ref[:, D_INNER + g * 128:D_INNER + (g + 1) * 128] = dbg
            dxbc_ref[:, D_INNER + BC_DIM + g * 128:D_INNER + BC_DIM + (g + 1) * 128] = dcg
            dxc_scr[:, gs] += dxcd * ds_x[:, gs]
            dh_scr[:, gs] = dh[:, gs] * elast_x[:, gs] + dhp_g
            tst = dxcd * xcd[:, gs]
            tdiff[:, gs] = dy[:, gs] * (raw * ea_x[:, gs]) - tst
            tdiff[BLK - 1:BLK, gs] += jnp.sum(tst, axis=0, keepdims=True)
        dxc = dxc_scr[...]
        row = lax.broadcasted_iota(jnp.int32, (BLK, D_INNER), 0)
        tfull = tdiff[...] + jnp.where(row == BLK - 1, dalast_x, 0.0)
        dacs = dacs + _sel_dot(tfull, r_mat, 2) - dacs_r[...].T
        da = _dot_tn(causal.astype(F32), dacs, HI)
        ddt = da * a_neg + _sel_dot(dxc * xs, r_mat, 2)
        lmask = lax.broadcasted_iota(jnp.int32, (BLK, 128), 1) < N_SSD_HEADS
        ddtraw = jnp.where(lmask, ddt * _sigmoid(dtraw_ref[...] + dtb_ref[...]), 0.0)
        ddt_ref[...] = ddtraw.astype(BF16)
        dxbc_ref[:, 0:D_INNER] = dy * dskip_x + dxc * dt_x
        dvec_ref[0:1, :] += jnp.sum(ddtraw, axis=0, keepdims=True)
        dvec_ref[1:2, :] += jnp.where(lane < N_SSD_HEADS, jnp.sum(da * dt, axis=0, keepdims=True) * a_neg, 0.0)
        dvec_ref[2:3, :] += _sel_dot(_row8(jnp.sum(dy * xs, axis=0, keepdims=True)), r_mat, 3)[0:1]

    rev = lambda c: nc - 1 - c
    vec = pl.BlockSpec((1, 128), lambda c: (0, 0))
    own, extra = _hosted(
        body, name="ssd_bwd", grid=(nc,),
        in_specs=[pl.BlockSpec((BLK, D_INNER), lambda c: (rev(c), 0)),
                  pl.BlockSpec((BLK, BC_DIM), lambda c: (rev(c), D_INNER // BC_DIM)),
                  pl.BlockSpec((BLK, BC_DIM), lambda c: (rev(c), D_INNER // BC_DIM + 1)),
                  pl.BlockSpec((BLK, 128), lambda c: (rev(c), O_DT // 128)), vec, vec, vec,
                  pl.BlockSpec((1, 128, D_INNER), lambda c: (rev(c), 0, 0)),
                  pl.BlockSpec((BLK, D_INNER), lambda c: (rev(c), 0))],
        out_specs=[pl.BlockSpec((BLK, XBC_DIM), lambda c: (rev(c), 0)),
                   pl.BlockSpec((BLK, 128), lambda c: (rev(c), 0)),
                   pl.BlockSpec((8, 128), lambda c: (0, 0))],
        out_shape=[jax.ShapeDtypeStruct((s, XBC_DIM), F32), jax.ShapeDtypeStruct((s, 128), BF16),
                   jax.ShapeDtypeStruct((8, 128), F32)],
        scratch_shapes=[pltpu.VMEM((128, D_INNER), F32), pltpu.VMEM((BLK, D_INNER), BF16),
                        pltpu.VMEM((BLK, D_INNER), BF16), pltpu.VMEM((BLK, D_INNER), F32),
                        pltpu.VMEM((128, BLK), F32), pltpu.VMEM((BLK, D_INNER), F32)],
        args=(xbc, xbc, xbc, proj, dt_bias, a_log, d_skip, hprev, dy), sem=("arbitrary",), side=side)
    return own if side is None else (own, extra)


GW = 512


def _gate_norm_fwd(y, proj, wn, *, tm=512):
    s = y.shape[0]
    tm = _tile(s, tm)

    def body(y_ref, z_ref, w_ref, o_ref):
        z = z_ref[...]
        y2 = y_ref[...] * (z * _sigmoid(z))
        r = lax.rsqrt(jnp.mean(y2 * y2, axis=-1, keepdims=True) + EPS)
        o_ref[...] = ((y2 * r) * w_ref[...]).astype(BF16)

    return pl.pallas_call(
        body, name="gate_norm_fwd", grid=(s // tm, 4),
        in_specs=[pl.BlockSpec((tm, GW), lambda i, g: (i, g)), pl.BlockSpec((tm, GW), lambda i, g: (i, O_Z // GW + g)),
                  pl.BlockSpec((1, GW), lambda i, g: (0, g))],
        out_specs=pl.BlockSpec((tm, GW), lambda i, g: (i, g)),
        out_shape=jax.ShapeDtypeStruct((s, D_INNER), BF16), compiler_params=_cp(("parallel", "parallel")),
    )(y, proj, wn)


def _gate_norm_bwd(dyn, y, proj, wn, *, tm=512):
    s = y.shape[0]
    tm = _tile(s, tm)

    def body(d_ref, y_ref, z_ref, w_ref, dy_ref, dz_ref, dw_ref):
        i = pl.program_id(1)
        z = z_ref[...]
        sg = _sigmoid(z)
        sz = z * sg
        yv = y_ref[...]
        y2 = yv * sz
        r = lax.rsqrt(jnp.mean(y2 * y2, axis=-1, keepdims=True) + EPS)
        xh = y2 * r
        dv = d_ref[...]
        g = dv * w_ref[...]
        dy2 = r * (g - xh * jnp.mean(g * xh, axis=-1, keepdims=True))
        dy_ref[...] = dy2 * sz
        dz_ref[...] = (dy2 * yv * _dsilu(z, sg)).astype(BF16)
        part = jnp.sum(dv * xh, axis=0, keepdims=True)

        @pl.when(i == 0)
        def _():
            dw_ref[...] = part

        @pl.when(i > 0)
        def _():
            dw_ref[...] += part

    blk = pl.BlockSpec((tm, GW), lambda g, i: (i, g))
    vec = pl.BlockSpec((1, GW), lambda g, i: (0, g))
    return pl.pallas_call(
        body, name="gate_norm_bwd", grid=(4, s // tm),
        in_specs=[blk, blk, pl.BlockSpec((tm, GW), lambda g, i: (i, O_Z // GW + g)), vec],
        out_specs=[blk, blk, vec],
        out_shape=[jax.ShapeDtypeStruct((s, D_INNER), F32), jax.ShapeDtypeStruct((s, D_INNER), BF16),
                   jax.ShapeDtypeStruct((1, D_INNER), F32)],
        compiler_params=_cp(("parallel", "arbitrary")),
    )(dyn, y, proj, wn)


def _merge_fwd(proj, b_gate, attn, ssd_out, *, tm=512):
    s = attn.shape[0]
    tm = _tile(s, tm)

    def body(ga_ref, gs_ref, ba_ref, bs_ref, a_ref, s_ref, o_ref):
        ga = _sigmoid(ga_ref[...] + ba_ref[...])
        gs = _sigmoid(gs_ref[...] + bs_ref[...])
        o_ref[...] = (ga * a_ref[...] + gs * s_ref[...]).astype(BF16)

    blk = pl.BlockSpec((tm, GW), lambda i, j: (i, j))
    return pl.pallas_call(
        body, name="merge_fwd", grid=(s // tm, 2),
        in_specs=[pl.BlockSpec((tm, GW), lambda i, j: (i, O_GA // GW + j)),
                  pl.BlockSpec((tm, GW), lambda i, j: (i, O_GS // GW + j)),
                  pl.BlockSpec((1, GW), lambda i, j: (0, j)), pl.BlockSpec((1, GW), lambda i, j: (0, 2 + j)), blk, blk],
        out_specs=blk, out_shape=jax.ShapeDtypeStruct((s, D_MODEL), BF16),
        compiler_params=_cp(("parallel", "parallel")),
    )(proj, proj, b_gate, b_gate, attn, ssd_out)


def _merge_bwd(dm, proj, b_gate, attn, ssd_out, *, tm=512):
    s = attn.shape[0]
    tm = _tile(s, tm)

    def body(d_ref, ga_ref, gs_ref, ba_ref, bs_ref, a_ref, s_ref, da_ref, ds_ref, dga_ref, dgs_ref, dba_ref, dbs_ref):
        i = pl.program_id(1)
        ga = _sigmoid(ga_ref[...] + ba_ref[...])
        gs = _sigmoid(gs_ref[...] + bs_ref[...])
        d = d_ref[...]
        da_ref[...] = (d * ga).astype(BF16)
        ds_ref[...] = (d * gs).astype(BF16)
        dga = d * a_ref[...] * (ga * (1.0 - ga))
        dgs = d * s_ref[...] * (gs * (1.0 - gs))
        dga_ref[...] = dga.astype(BF16)
        dgs_ref[...] = dgs.astype(BF16)
        pa = jnp.sum(dga, axis=0, keepdims=True)
        ps = jnp.sum(dgs, axis=0, keepdims=True)

        @pl.when(i == 0)
        def _():
            dba_ref[...] = pa
            dbs_ref[...] = ps

        @pl.when(i > 0)
        def _():
            dba_ref[...] += pa
            dbs_ref[...] += ps

    blk = pl.BlockSpec((tm, GW), lambda j, i: (i, j))
    vec = pl.BlockSpec((1, GW), lambda j, i: (0, j))
    sd = jax.ShapeDtypeStruct((s, D_MODEL), BF16)
    vd = jax.ShapeDtypeStruct((1, D_MODEL), F32)
    return pl.pallas_call(
        body, name="merge_bwd", grid=(2, s // tm),
        in_specs=[blk, pl.BlockSpec((tm, GW), lambda j, i: (i, O_GA // GW + j)),
                  pl.BlockSpec((tm, GW), lambda j, i: (i, O_GS // GW + j)),
                  vec, pl.BlockSpec((1, GW), lambda j, i: (0, 2 + j)), blk, blk],
        out_specs=[blk, blk, blk, blk, vec, vec], out_shape=[sd, sd, sd, sd, vd, vd],
        compiler_params=_cp(("parallel", "arbitrary")),
    )(dm, proj, proj, b_gate, b_gate, attn, ssd_out)


def _adamw_math(w, g, m, v):
    mn = ADAM_B1 * m + (1.0 - ADAM_B1) * g
    vn = ADAM_B2 * v + (1.0 - ADAM_B2) * (g * g)
    m_hat = mn / (1.0 - ADAM_B1 ** ADAM_STEP)
    v_hat = vn / (1.0 - ADAM_B2 ** ADAM_STEP)
    return -ADAM_LR * (m_hat / (jnp.sqrt(v_hat) + ADAM_EPS) + ADAM_WD * w), mn, vn


def _adamw_many(ws, gs, ms, vs):
    n = len(ws)

    def body(*refs):
        outs = refs[4 * n:]
        for i in range(n):
            res = _adamw_math(*[refs[q * n + i][...] for q in range(4)])
            for q in range(3):
                outs[q * n + i][...] = res[q]

    return pl.pallas_call(body, name="adamw_small", out_shape=[jax.ShapeDtypeStruct(w.shape, F32) for w in ws] * 3,
                          compiler_params=_cp())(*ws, *gs, *ms, *vs)


def _adamw(w, g, m, v, *, name, tm=128):
    r, c = w.shape
    tm = r if (r < tm or r % tm) else tm

    def body(w_ref, g_ref, m_ref, v_ref, d_ref, nm_ref, nv_ref, g_out):
        gv = g_ref[:, :c]
        d_ref[...], nm_ref[...], nv_ref[...] = _adamw_math(w_ref[...], gv, m_ref[...], v_ref[...])
        g_out[...] = gv

    blk = pl.BlockSpec((tm, c), lambda i: (i, 0))
    sd = jax.ShapeDtypeStruct((r, c), F32)
    return pl.pallas_call(
        body, name=name, grid=(r // tm,), in_specs=[blk, pl.BlockSpec((tm, g.shape[1]), lambda i: (i, 0)), blk, blk],
        out_specs=[blk] * 4, out_shape=[sd] * 4, compiler_params=_cp(("parallel",)),
    )(w, g, m, v)


ANY = pl.BlockSpec(memory_space=pl.ANY)
N_CHIPS = 4


def _chip_of(k, x, y):
    return (x ^ (k >> 1), y ^ (k & 1))


def _all_gather_small(shard):
    r, c = shard.shape
    hr = r // 2

    def body(sh_ref, out_ref, send_sems, recv_sems, local_sem):
        x, y, cc = lax.axis_index("x"), lax.axis_index("y"), lax.axis_index("c")

        def half(px, py, pc):
            return out_ref.at[2 * px + py, pl.ds(pc * hr, hr), :]

        def copy(k, px, py, pc, to, src=None):
            return pltpu.make_async_remote_copy(
                src_ref=half(px, py, pc) if src is None else src, dst_ref=half(px, py, pc),
                send_sem=send_sems.at[k], recv_sem=recv_sems.at[k], device_id=to, device_id_type=MESH)

        mine = pltpu.make_async_copy(sh_ref, out_ref.at[2 * x + y], local_sem)
        mine.start()
        chips = [_chip_of(k, x, y) for k in (1, 2, 3)]
        first = [copy(j, x, y, cc, (*chip, cc), src=sh_ref.at[pl.ds(cc * hr, hr), :]) for j, chip in enumerate(chips)]
        for cp in first:
            cp.start()
        passed = [copy(3 + j, *chip, cc, (x, y, 1 - cc)) for j, chip in enumerate(chips)]
        for j, chip in enumerate(chips):
            copy(j, *chip, cc, (x, y, cc)).wait_recv()
            passed[j].start()
        for j, chip in enumerate(chips):
            copy(3 + j, *chip, 1 - cc, (x, y, cc)).wait_recv()
        for cp in first + passed:
            cp.wait_send()
        mine.wait()

    return pl.pallas_call(
        body, name="all_gather_small", in_specs=[ANY], out_specs=ANY,
        out_shape=jax.ShapeDtypeStruct((N_CHIPS, r, c), shard.dtype),
        scratch_shapes=[pltpu.SemaphoreType.DMA((6,)), pltpu.SemaphoreType.DMA((6,)), pltpu.SemaphoreType.DMA],
    )(shard)


def _cast_bf16(a, *, name, tm=512):
    n, r, c = a.shape
    tm = _tile(r, tm) if r % 128 == 0 else r

    def body(a_ref, o_ref):
        o_ref[...] = a_ref[...].astype(BF16)

    blk = pl.BlockSpec((1, tm, c), lambda i, j: (i, j, 0))
    return pl.pallas_call(body, name=name, grid=(n, r // tm), in_specs=[blk], out_specs=blk,
                          out_shape=jax.ShapeDtypeStruct(a.shape, BF16), compiler_params=_cp(("parallel", "parallel")))(a)


def _pair_exchange(g16, hr):
    n, r, c = g16.shape

    def body(g_ref, out_ref, send_sem, recv_sem):
        x, y, cc = lax.axis_index("x"), lax.axis_index("y"), lax.axis_index("c")
        cp = pltpu.make_async_remote_copy(
            src_ref=g_ref.at[:, pl.ds((1 - cc) * hr, hr), :], dst_ref=out_ref, send_sem=send_sem, recv_sem=recv_sem,
            device_id=(x, y, 1 - cc), device_id_type=MESH)
        cp.start()
        cp.wait()

    return pl.pallas_call(
        body, name="grad_pair_exchange", in_specs=[ANY], out_specs=ANY,
        out_shape=jax.ShapeDtypeStruct((n, hr, c), g16.dtype),
        scratch_shapes=[pltpu.SemaphoreType.DMA, pltpu.SemaphoreType.DMA],
    )(g16)


def _pair_add(g, recv, half_idx, hr, *, tm=384):
    n, r, c = g.shape
    nt = hr // tm

    def body(hi_ref, g_ref, r_ref, o32_ref, o16_ref):
        v = g_ref[...] + r_ref[...].astype(F32)
        o32_ref[...] = v
        o16_ref[...] = v.astype(BF16)

    gs = pltpu.PrefetchScalarGridSpec(
        num_scalar_prefetch=1, grid=(n, nt),
        in_specs=[pl.BlockSpec((1, tm, c), lambda i, j, hi: (i, hi[0] * nt + j, 0)),
                  pl.BlockSpec((1, tm, c), lambda i, j, hi: (i, j, 0))],
        out_specs=[pl.BlockSpec((1, tm, c), lambda i, j, hi: (i, j, 0))] * 2)
    return pl.pallas_call(
        body, name="grad_pair_add", grid_spec=gs,
        out_shape=[jax.ShapeDtypeStruct((n, hr, c), F32), jax.ShapeDtypeStruct((n, hr, c), BF16)],
        compiler_params=_cp(("parallel", "parallel")),
    )(half_idx, g, recv)


def _chip_exchange(p16):
    n, hr, c = p16.shape

    def body(p_ref, out_ref, send_sems, recv_sems):
        x, y, cc = lax.axis_index("x"), lax.axis_index("y"), lax.axis_index("c")
        cps = []
        for j, k in enumerate((1, 2, 3)):
            px, py = _chip_of(k, x, y)
            cps.append(pltpu.make_async_remote_copy(
                src_ref=p_ref.at[2 * px + py], dst_ref=out_ref.at[j], send_sem=send_sems.at[j], recv_sem=recv_sems.at[j],
                device_id=(px, py, cc), device_id_type=MESH))
        for cp in cps:
            cp.start()
        for cp in cps:
            cp.wait()

    return pl.pallas_call(
        body, name="grad_chip_exchange", in_specs=[ANY], out_specs=ANY,
        out_shape=jax.ShapeDtypeStruct((3, hr, c), p16.dtype),
        scratch_shapes=[pltpu.SemaphoreType.DMA((3,)), pltpu.SemaphoreType.DMA((3,))],
    )(p16)


def _chip_add(p32, recv, chip_idx, *, tm=384):
    n, hr, c = p32.shape

    def body(ci_ref, p_ref, r_ref, o_ref):
        o_ref[...] = ((p_ref[0] + r_ref[0].astype(F32)) + r_ref[1].astype(F32)) + r_ref[2].astype(F32)

    gs = pltpu.PrefetchScalarGridSpec(
        num_scalar_prefetch=1, grid=(hr // tm,),
        in_specs=[pl.BlockSpec((1, tm, c), lambda j, ci: (ci[0], j, 0)), pl.BlockSpec((3, tm, c), lambda j, ci: (0, j, 0))],
        out_specs=pl.BlockSpec((tm, c), lambda j, ci: (j, 0)))
    return pl.pallas_call(
        body, name="grad_chip_add", grid_spec=gs, out_shape=jax.ShapeDtypeStruct((hr, c), F32),
        compiler_params=_cp(("parallel",)),
    )(chip_idx, p32, recv)


def _pair_gather(f):
    hr, c = f.shape

    def body(f_ref, out_ref, send_sem, recv_sem, local_sem):
        x, y, cc = lax.axis_index("x"), lax.axis_index("y"), lax.axis_index("c")
        mine = pltpu.make_async_copy(f_ref, out_ref.at[pl.ds(cc * hr, hr), :], local_sem)
        mine.start()
        cp = pltpu.make_async_remote_copy(
            src_ref=f_ref, dst_ref=out_ref.at[pl.ds(cc * hr, hr), :], send_sem=send_sem, recv_sem=recv_sem,
            device_id=(x, y, 1 - cc), device_id_type=MESH)
        cp.start()
        cp.wait()
        mine.wait()

    return pl.pallas_call(
        body, name="grad_pair_gather", in_specs=[ANY], out_specs=ANY,
        out_shape=jax.ShapeDtypeStruct((2 * hr, c), f.dtype),
        scratch_shapes=[pltpu.SemaphoreType.DMA, pltpu.SemaphoreType.DMA, pltpu.SemaphoreType.DMA],
    )(f)


def _all_reduce_small(buf):
    r, c = buf.shape

    def body(b_ref, out_ref, gat, send_sems, recv_sems):
        x, y, cc = lax.axis_index("x"), lax.axis_index("y"), lax.axis_index("c")
        me = 4 * x + 2 * y + cc
        gat[me] = b_ref[...]
        cps = []
        for k in range(1, 8):
            px, py, pc = x ^ (k >> 2), y ^ ((k >> 1) & 1), cc ^ (k & 1)
            cps.append(pltpu.make_async_remote_copy(
                src_ref=b_ref, dst_ref=gat.at[me], send_sem=send_sems.at[k - 1], recv_sem=recv_sems.at[k - 1],
                device_id=(px, py, pc), device_id_type=MESH))
        for cp in cps:
            cp.start()
        for cp in cps:
            cp.wait()
        acc = gat[0]
        for d in range(1, 8):
            acc = acc + gat[d]
        out_ref[...] = acc

    vm = pl.BlockSpec(memory_space=pltpu.VMEM)
    return pl.pallas_call(
        body, name="all_reduce_small", in_specs=[vm], out_specs=vm, out_shape=jax.ShapeDtypeStruct((r, c), F32),
        scratch_shapes=[pltpu.VMEM((8, r, c), F32), pltpu.SemaphoreType.DMA((7,)), pltpu.SemaphoreType.DMA((7,))],
        compiler_params=pltpu.CompilerParams(vmem_limit_bytes=VMEM_LIMIT),
    )(buf)


def _pipe(fn, ins, outs, tr, depth=4, slots=None):
    shape = ins[0].shape
    lead, (r, c) = shape[:-2], shape[-2:]
    assert len(lead) <= 1 and r % tr == 0
    nr = r // tr
    which = list(range(lead[0])) if lead and slots is None else slots
    n = nr * (len(which) if lead else 1)
    ni, no = len(ins), len(outs)

    def blk(ref, step):
        rows = pl.ds((step % nr) * tr, tr)
        return ref.at[which[step // nr], rows, :] if lead else ref.at[rows, :]

    def scoped(*bufs):
        ibufs, obufs, isem, osem = bufs[:ni], bufs[ni:ni + no], bufs[-2], bufs[-1]

        def in_copy(q, step, slot):
            return pltpu.make_async_copy(blk(ins[q], step), ibufs[q].at[slot], isem.at[q, slot])

        def out_copy(q, step, slot):
            return pltpu.make_async_copy(obufs[q].at[slot], blk(outs[q], step), osem.at[q, slot])

        for step in range(min(nbuf - 1, n)):
            for q in range(ni):
                in_copy(q, step, step % nbuf).start()
        for step in range(n):
            slot = step % nbuf
            if step + nbuf - 1 < n:
                for q in range(ni):
                    in_copy(q, step + nbuf - 1, (step + nbuf - 1) % nbuf).start()
            for q in range(ni):
                in_copy(q, step, slot).wait()
            if step >= nbuf:
                for q in range(no):
                    out_copy(q, step - nbuf, slot).wait()
            res = fn(*[ibufs[q][slot] for q in range(ni)])
            for q in range(no):
                obufs[q][slot] = res[q].astype(obufs[q].dtype)
                out_copy(q, step, slot).start()
        for step in range(max(n - nbuf, 0), n):
            for q in range(no):
                out_copy(q, step, step % nbuf).wait()

    assert n <= 8
    nbuf = min(n, depth)
    pl.run_scoped(scoped, *[pltpu.VMEM((nbuf, tr, c), q.dtype) for q in ins], *[pltpu.VMEM((nbuf, tr, c), q.dtype) for q in outs],
                  pltpu.SemaphoreType.DMA((ni, nbuf)), pltpu.SemaphoreType.DMA((no, nbuf)))


W_IN_PAD = 2304
BIG = ("w_in", "w_attn_o", "w_ssd_o", "w_out", "w_up", "w_down")
BIG_SHAPE = dict(w_in=(D_MODEL, W_IN_PAD), w_attn_o=(Q_DIM // 4, D_MODEL), w_ssd_o=(D_INNER // 4, D_MODEL),
                 w_out=(D_MODEL // 4, D_MODEL), w_up=(D_MODEL, 2 * D_FF // 4), w_down=(D_FF // 4, D_MODEL))
BIG_TR = dict(w_in=128, w_attn_o=128, w_ssd_o=128, w_out=128, w_up=128, w_down=176)
X_FIRST = dict(w_in=True, w_attn_o=True, w_ssd_o=False, w_out=True, w_up=False, w_down=False)


def _neighbours(x, y, x_first):
    xn, yn = (1 - x, y), (x, 1 - y)
    n1, n2 = (xn, yn) if x_first else (yn, xn)
    slot = lambda ch: 2 * ch[0] + ch[1]
    return n1, n2, slot(n1), slot(n2), slot((1 - x, 1 - y))


def _gather_big(shards):
    nt = len(BIG)

    def body(*refs):
        sh, out = refs[:nt], refs[nt:2 * nt]
        send_sems, recv_sems = refs[2 * nt:]
        x, y, cc = lax.axis_index("x"), lax.axis_index("y"), lax.axis_index("c")
        me = 2 * x + y
        sib = (x, y, 1 - cc)
        for t, n in enumerate(BIG):
            _pipe(lambda v: (v,), [sh[t]], [out[t].at[me]], BIG_TR[n])

        def copy(t, k, slot, pc, to):
            hr = BIG_SHAPE[BIG[t]][0] // 2
            ref = out[t].at[slot, pl.ds(pc * hr, hr), :]
            return pltpu.make_async_remote_copy(src_ref=ref, dst_ref=ref, send_sem=send_sems.at[6 * t + k],
                                                recv_sem=recv_sems.at[6 * t + k], device_id=to, device_id_type=MESH)

        started = []

        def start(cp):
            cp.start()
            started.append(cp)

        geo = [_neighbours(x, y, X_FIRST[n]) for n in BIG]
        for t in range(nt):
            n1, n2, _, _, _ = geo[t]
            start(copy(t, 0, me, cc, (*n1, cc)))
            start(copy(t, 1, me, cc, (*n2, cc)))
        for t in range(nt):
            n1, n2, s1, s2, sd = geo[t]
            copy(t, 0, s1, cc, sib).wait_recv()
            start(copy(t, 2, s1, cc, (*n2, cc)))
            start(copy(t, 3, s1, cc, sib))
            copy(t, 1, s2, cc, sib).wait_recv()
            start(copy(t, 4, s2, cc, sib))
        for t in range(nt):
            _, _, s1, s2, sd = geo[t]
            copy(t, 2, sd, cc, sib).wait_recv()
            start(copy(t, 5, sd, cc, sib))
        for t in range(nt):
            _, _, s1, s2, sd = geo[t]
            copy(t, 3, s1, 1 - cc, sib).wait_recv()
            copy(t, 4, s2, 1 - cc, sib).wait_recv()
            copy(t, 5, sd, 1 - cc, sib).wait_recv()
        for cp in started:
            cp.wait_send()

    return pl.pallas_call(
        body, name="gather_big", in_specs=[ANY] * nt, out_specs=[ANY] * nt,
        out_shape=[jax.ShapeDtypeStruct((N_CHIPS, *BIG_SHAPE[n]), BF16) for n in BIG],
        scratch_shapes=[pltpu.SemaphoreType.DMA((6 * nt,)), pltpu.SemaphoreType.DMA((6 * nt,))],
        compiler_params=pltpu.CompilerParams(vmem_limit_bytes=VMEM_LIMIT),
    )(*shards)


def _reduce_big(grads):
    nt = len(BIG)
    nw = 7

    def body(*refs):
        g = refs[:nt]
        fin = refs[nt:2 * nt]
        work = refs[2 * nt:2 * nt + nw * nt]
        send_sems, recv_sems = refs[2 * nt + nw * nt:]
        x, y, cc = lax.axis_index("x"), lax.axis_index("y"), lax.axis_index("c")
        me = 2 * x + y
        sib = (x, y, 1 - cc)
        started = []

        def rcopy(t, k, src, dst, to):
            cp = pltpu.make_async_remote_copy(src_ref=src, dst_ref=dst, send_sem=send_sems.at[5 * t + k],
                                              recv_sem=recv_sems.at[5 * t + k], device_id=to, device_id_type=MESH)
            return cp

        def start(cp):
            cp.start()
            started.append(cp)

        geo = [_neighbours(x, y, X_FIRST[n]) for n in BIG]
        hrs = [BIG_SHAPE[n][0] // 2 for n in BIG]
        wk = lambda t: work[nw * t:nw * (t + 1)]
        one = lambda ref, slot: ref.at[pl.ds(slot, 1)]
        for t in range(nt):
            recv_a = wk(t)[0]
            start(rcopy(t, 0, g[t].at[:, pl.ds((1 - cc) * hrs[t], hrs[t]), :], recv_a, sib))
        for t, n in enumerate(BIG):
            recv_a, p32, p16, r1, qme, qs2, r2 = wk(t)
            n1, n2, s1, s2, sd = geo[t]
            rcopy(t, 0, recv_a, recv_a, sib).wait_recv()
            _pipe(lambda a, b: (a + b, a + b), [g[t].at[:, pl.ds(cc * hrs[t], hrs[t]), :], recv_a], [p32, p16], BIG_TR[n])
            start(rcopy(t, 1, one(p16, s1), one(r1, 0), (*n1, cc)))
            start(rcopy(t, 2, one(p16, sd), one(r1, 1), (*n1, cc)))
        for t, n in enumerate(BIG):
            recv_a, p32, p16, r1, qme, qs2, r2 = wk(t)
            n1, n2, s1, s2, sd = geo[t]
            rcopy(t, 1, one(r1, 0), one(r1, 0), sib).wait_recv()
            rcopy(t, 2, one(r1, 1), one(r1, 1), sib).wait_recv()
            _pipe(lambda a, b: (a + b.astype(F32),), [one(p32, s2), one(r1, 1)], [qs2], BIG_TR[n])
            start(rcopy(t, 3, qs2, r2, (*n2, cc)))
            _pipe(lambda a, b: (a + b.astype(F32),), [one(p32, me), one(r1, 0)], [qme], BIG_TR[n])
        for t, n in enumerate(BIG):
            recv_a, p32, p16, r1, qme, qs2, r2 = wk(t)
            rcopy(t, 3, r2, r2, sib).wait_recv()
            mine = fin[t].at[pl.ds(cc * hrs[t], hrs[t]), :]
            _pipe(lambda a, b: (a + b.astype(F32),), [qme.at[0], r2.at[0]], [mine], BIG_TR[n])
            start(rcopy(t, 4, mine, mine, sib))
        for t in range(nt):
            other = fin[t].at[pl.ds((1 - cc) * hrs[t], hrs[t]), :]
            rcopy(t, 4, other, other, sib).wait_recv()
        for cp in started:
            cp.wait_send()

    outs = [jax.ShapeDtypeStruct(BIG_SHAPE[n], F32) for n in BIG]
    for n in BIG:
        r, c = BIG_SHAPE[n]
        hr = r // 2
        outs += [jax.ShapeDtypeStruct((4, hr, c), F32), jax.ShapeDtypeStruct((4, hr, c), F32),
                 jax.ShapeDtypeStruct((4, hr, c), BF16), jax.ShapeDtypeStruct((2, hr, c), BF16),
                 jax.ShapeDtypeStruct((1, hr, c), F32), jax.ShapeDtypeStruct((1, hr, c), BF16),
                 jax.ShapeDtypeStruct((1, hr, c), BF16)]
    res = pl.pallas_call(
        body, name="reduce_big", in_specs=[ANY] * nt, out_specs=[ANY] * len(outs), out_shape=outs,
        scratch_shapes=[pltpu.SemaphoreType.DMA((5 * nt,)), pltpu.SemaphoreType.DMA((5 * nt,))],
        compiler_params=pltpu.CompilerParams(vmem_limit_bytes=VMEM_LIMIT),
    )(*grads)
    return res[:nt]


WHOLE_X_FIRST = dict(w_ssd_o=True, w_out=False, w_attn_o=False)


def _quarters(names):
    out = []
    for i, n in enumerate(names):
        if n in WHOLE_X_FIRST:
            h = BIG_SHAPE[n][0] // 2
            out.append((i, WHOLE_X_FIRST[n], 0, h, 128))
        else:
            q = BIG_SHAPE[n][0] // 4
            tr = 128 if q % 128 == 0 else q
            out += [(i, True, 0, q, tr), (i, False, q, q, tr)]
    return out


class _GatherJob:
    def __init__(self, names, shards, at=None):
        self.names = names
        self.at = at
        self.inputs = list(shards)
        self.out_shapes = [jax.ShapeDtypeStruct((N_CHIPS, *BIG_SHAPE[n]), BF16) for n in names]
        self.ent = _quarters(names)
        self.scratch = [pltpu.SemaphoreType.DMA((6 * len(self.ent),)), pltpu.SemaphoreType.DMA((6 * len(self.ent),))]

    def phases(self, sh, out, scr):
        send_sems, recv_sems = scr
        names, ent = self.names, self.ent
        x, y, cc = lax.axis_index("x"), lax.axis_index("y"), lax.axis_index("c")
        me = 2 * x + y
        sib = (x, y, 1 - cc)
        geo = [_neighbours(x, y, e[1]) for e in ent]
        started = []

        def copy(i, k, slot, pc, to):
            arr, _, roff, rows, _ = ent[i]
            hr = BIG_SHAPE[names[arr]][0] // 2
            ref = out[arr].at[slot, pl.ds(pc * hr + roff, rows), :]
            return pltpu.make_async_remote_copy(src_ref=ref, dst_ref=ref, send_sem=send_sems.at[6 * i + k],
                                                recv_sem=recv_sems.at[6 * i + k], device_id=to, device_id_type=MESH)

        def start(*a):
            copy(*a).start()
            started.append(a)

        def p0():
            for t, n in enumerate(names):
                _pipe(lambda v: (v,), [sh[t]], [out[t].at[me]], BIG_TR[n])
            for i in range(len(ent)):
                n1, n2, _, _, _ = geo[i]
                start(i, 0, me, cc, (*n1, cc))
                start(i, 1, me, cc, (*n2, cc))

        def p1():
            for i in range(len(ent)):
                n1, n2, s1, s2, sd = geo[i]
                copy(i, 0, s1, cc, sib).wait_recv()
                start(i, 2, s1, cc, (*n2, cc))
                start(i, 3, s1, cc, sib)
                copy(i, 1, s2, cc, sib).wait_recv()
                start(i, 4, s2, cc, sib)

        def p2():
            for i in range(len(ent)):
                sd = geo[i][4]
                copy(i, 2, sd, cc, sib).wait_recv()
                start(i, 5, sd, cc, sib)

        def p3():
            for i in range(len(ent)):
                _, _, s1, s2, sd = geo[i]
                copy(i, 3, s1, 1 - cc, sib).wait_recv()
                copy(i, 4, s2, 1 - cc, sib).wait_recv()
                copy(i, 5, sd, 1 - cc, sib).wait_recv()
            for a in started:
                copy(*a).wait_send()

        return [p0, p1, p2, p3]


class _ReduceJob:
    NW = 7

    def __init__(self, names, grads, at=None):
        self.names = names
        self.at = at
        self.inputs = list(grads)
        self.ent = _quarters(names)
        self.out_shapes = [jax.ShapeDtypeStruct(BIG_SHAPE[n], F32) for n in names]
        for arr, _, _, rows, _ in self.ent:
            c = BIG_SHAPE[names[arr]][1]
            self.out_shapes += [jax.ShapeDtypeStruct((4, rows, c), F32), jax.ShapeDtypeStruct((4, rows, c), F32),
                                jax.ShapeDtypeStruct((4, rows, c), BF16), jax.ShapeDtypeStruct((2, rows, c), BF16),
                                jax.ShapeDtypeStruct((1, rows, c), F32), jax.ShapeDtypeStruct((1, rows, c), BF16),
                                jax.ShapeDtypeStruct((1, rows, c), BF16)]
        self.scratch = [pltpu.SemaphoreType.DMA((8 * len(self.ent),)), pltpu.SemaphoreType.DMA((8 * len(self.ent),))]

    def phases(self, g, outs, scr):
        send_sems, recv_sems = scr
        names, ent, nw = self.names, self.ent, self.NW
        nt = len(names)
        fin, work = outs[:nt], outs[nt:]
        x, y, cc = lax.axis_index("x"), lax.axis_index("y"), lax.axis_index("c")
        me = 2 * x + y
        sib = (x, y, 1 - cc)
        geo = [_neighbours(x, y, e[1]) for e in ent]
        started = []
        wk = lambda i: work[nw * i:nw * (i + 1)]
        one = lambda ref, slot: ref.at[pl.ds(slot, 1)]

        def rows_of(i, pc):
            arr, _, roff, rows, _ = ent[i]
            return pl.ds(pc * (BIG_SHAPE[names[arr]][0] // 2) + roff, rows)

        def rcopy(i, k, src, dst, to):
            return pltpu.make_async_remote_copy(src_ref=src, dst_ref=dst, send_sem=send_sems.at[8 * i + k],
                                                recv_sem=recv_sems.at[8 * i + k], device_id=to, device_id_type=MESH)

        def start(make):
            make().start()
            started.append(make)

        def pair(i, q, slot, pc):
            return rcopy(i, q, g[ent[i][0]].at[pl.ds(slot, 1), rows_of(i, pc), :], one(wk(i)[0], slot), sib)

        def p0():
            for i in range(len(ent)):
                _, _, s1, s2, sd = geo[i]
                for q, slot in enumerate((s1, sd, s2, me)):
                    start(lambda i=i, q=q, slot=slot: pair(i, q, slot, 1 - cc))

        def p1():
            for i, e in enumerate(ent):
                recv_a, _, p16, _ = wk(i)[:4]
                n1, n2, s1, s2, sd = geo[i]
                pair(i, 0, s1, cc).wait_recv()
                pair(i, 1, sd, cc).wait_recv()
                _pipe(lambda a, b: (a + b,), [g[e[0]].at[:, rows_of(i, cc), :], recv_a], [p16], e[4], slots=(s1, sd))
                start(lambda i=i, s1=s1, n1=n1: rcopy(i, 4, one(wk(i)[2], s1), one(wk(i)[3], 0), (*n1, cc)))
                start(lambda i=i, sd=sd, n1=n1: rcopy(i, 5, one(wk(i)[2], sd), one(wk(i)[3], 1), (*n1, cc)))
            for i, e in enumerate(ent):
                recv_a, p32 = wk(i)[:2]
                _, _, s1, s2, sd = geo[i]
                pair(i, 2, s2, cc).wait_recv()
                pair(i, 3, me, cc).wait_recv()
                _pipe(lambda a, b: (a + b,), [g[e[0]].at[:, rows_of(i, cc), :], recv_a], [p32], e[4], slots=(s2, me))

        def p2():
            for i, e in enumerate(ent):
                _, p32, _, r1, qme, qs2, r2 = wk(i)
                n1, n2, s1, s2, sd = geo[i]
                rcopy(i, 4, one(r1, 0), one(r1, 0), sib).wait_recv()
                rcopy(i, 5, one(r1, 1), one(r1, 1), sib).wait_recv()
                _pipe(lambda a, b, c, d: (a + b.astype(F32), c + d.astype(F32)),
                      [one(p32, s2), one(r1, 1), one(p32, me), one(r1, 0)], [qs2, qme], e[4])
                start(lambda i=i, n2=n2: rcopy(i, 6, wk(i)[5], wk(i)[6], (*n2, cc)))

        def p3():
            for i, e in enumerate(ent):
                qme, r2 = wk(i)[4], wk(i)[6]
                rcopy(i, 6, r2, r2, sib).wait_recv()
                mine = fin[e[0]].at[rows_of(i, cc), :]
                _pipe(lambda a, b: (a + b.astype(F32),), [qme.at[0], r2.at[0]], [mine], e[4])
                start(lambda i=i, e=e: rcopy(i, 7, fin[e[0]].at[rows_of(i, cc), :], fin[e[0]].at[rows_of(i, cc), :], sib))

        def p4():
            for i, e in enumerate(ent):
                other = fin[e[0]].at[rows_of(i, 1 - cc), :]
                rcopy(i, 7, other, other, sib).wait_recv()
            for make in started:
                make().wait_send()

        return [p0, p1, p2, p3, p4]


class _AdamJob:
    def __init__(self, names, ws, gs, ms, vs, groups):
        self.names, self.groups = names, groups
        self.inputs = [a for quad in zip(ws, gs, ms, vs) for a in quad]
        self.out_shapes = [jax.ShapeDtypeStruct(w.shape, F32) for w in ws for _ in range(4)]

    def work(self, ins, outs):
        def one(t):
            w, g, m, v = ins[4 * t:4 * t + 4]
            r = w.shape[1]
            tr = 128 if r % 128 == 0 else r // 4
            _pipe(lambda a, b, c, d: (*_adamw_math(a, b, c, d), b), [w.at[0], g, m.at[0], v.at[0]],
                  [o.at[0] for o in outs[4 * t:4 * t + 4]], tr, depth=2)

        def group(grp):
            def run():
                for n in grp:
                    one(self.names.index(n))
            return run

        return [group(grp) for grp in self.groups]


class _Interleaved:
    def __init__(self, job, work, at):
        self.job, self.wk, self.at = job, work, at
        self.inputs = job.inputs + work.inputs
        self.out_shapes = list(job.out_shapes) + list(work.out_shapes)
        self.scratch = job.scratch

    def phases(self, ins, outs, scr):
        nj, no = len(self.job.inputs), len(self.job.out_shapes)
        base = self.job.phases(ins[:nj], outs[:no], scr)
        work = self.wk.work(ins[nj:], outs[no:])
        mixed = []
        for k, ph in enumerate(base):
            mixed.append(ph)
            if k < len(work):
                mixed.append(work[k])
        return mixed


def _run_job(job, name):
    ni, no = len(job.inputs), len(job.out_shapes)

    def body(*refs):
        for ph in job.phases(refs[:ni], refs[ni:ni + no], refs[ni + no:]):
            ph()

    return pl.pallas_call(
        body, name=name, in_specs=[ANY] * ni, out_specs=[ANY] * no, out_shape=job.out_shapes, scratch_shapes=job.scratch,
        compiler_params=pltpu.CompilerParams(vmem_limit_bytes=VMEM_LIMIT),
    )(*job.inputs)


def _hosted(body, *, name, grid, in_specs, out_specs, out_shape, scratch_shapes, args, sem, side=None):
    if side is None:
        return pl.pallas_call(body, name=name, grid=grid, in_specs=in_specs, out_specs=out_specs, out_shape=out_shape,
                              scratch_shapes=scratch_shapes, compiler_params=_cp(sem))(*args), None
    job = side
    ni, no, ns = len(in_specs), len(out_specs), len(scratch_shapes)
    ji, jo = len(job.inputs), len(job.out_shapes)
    n_steps = 1
    for extent in grid:
        n_steps *= extent

    def wrapped(*refs):
        own_in, refs = refs[:ni], refs[ni:]
        job_in, refs = refs[:ji], refs[ji:]
        own_out, refs = refs[:no], refs[no:]
        job_out, refs = refs[:jo], refs[jo:]
        own_scr, job_scr = refs[:ns], refs[ns:]
        step = 0
        for d, extent in enumerate(grid):
            step = step * extent + pl.program_id(d)
        phases = job.phases(job_in, job_out, job_scr)
        steps = [min(int(f * n_steps), n_steps - 1) for f in job.at] + [n_steps - 1]
        assert len(steps) == len(phases) and steps == sorted(steps)
        for at, ph in zip(steps, phases):
            pl.when(step == at)(ph)
        body(*own_in, *own_out, *own_scr)

    res = pl.pallas_call(
        wrapped, name=name, grid=grid, in_specs=list(in_specs) + [ANY] * ji, out_specs=list(out_specs) + [ANY] * jo,
        out_shape=list(out_shape) + list(job.out_shapes), scratch_shapes=list(scratch_shapes) + list(job.scratch),
        compiler_params=_cp(("arbitrary",) * len(grid)),
    )(*args, *job.inputs)
    return res[:no], res[no:]


def _proj_dw(xnt, dproj_sh, *, tm=512, tk=2048):
    d, s = xnt.shape
    tk = _tile(s, tk)
    nk = s // tk

    def body(a_ref, b_ref, o_ref, acc):
        def finish(r):
            o_ref[0] = r

        _accumulate(acc, _dot(a_ref[...], b_ref[...]), pl.program_id(2), nk, finish)

    return pl.pallas_call(
        body, name="proj_dw", grid=(N_CHIPS, d // tm, nk),
        in_specs=[pl.BlockSpec((tm, tk), lambda j, i, q: (i, q)), pl.BlockSpec((tk, W_IN_PAD), lambda j, i, q: (q, j))],
        out_specs=pl.BlockSpec((1, tm, W_IN_PAD), lambda j, i, q: (j, i, 0)),
        out_shape=jax.ShapeDtypeStruct((N_CHIPS, d, W_IN_PAD), F32), scratch_shapes=[pltpu.VMEM((tm, W_IN_PAD), F32)],
        compiler_params=_cp(("parallel", "parallel", "arbitrary")),
    )(xnt, dproj_sh)


def _proj_dx(dproj_sh, w_sh, *, tm=1024, side=None):
    s = dproj_sh.shape[0]
    d = w_sh.shape[1]
    tm = _tile(s, tm)

    def body(a_ref, b_ref, o_ref, acc):
        kk = pl.program_id(1)
        part = _dot_nt(a_ref[...], b_ref[0])

        @pl.when(kk == 0)
        def _():
            acc[...] = part

        @pl.when(kk > 0)
        def _():
            acc[...] += part

        @pl.when(kk == N_CHIPS - 1)
        def _():
            o_ref[...] = acc[...]

    own, extra = _hosted(
        body, name="proj_dx", grid=(s // tm, N_CHIPS),
        in_specs=[pl.BlockSpec((tm, W_IN_PAD), lambda i, q: (i, q)), pl.BlockSpec((1, d, W_IN_PAD), lambda i, q: (q, 0, 0))],
        out_specs=[pl.BlockSpec((tm, d), lambda i, q: (i, 0))],
        out_shape=[jax.ShapeDtypeStruct((s, d), F32)], scratch_shapes=[pltpu.VMEM((tm, d), F32)],
        args=(dproj_sh, w_sh), sem=("parallel", "arbitrary"), side=side)
    return own[0] if side is None else (own[0], extra)


def _up_dx(dup, w_sh, *, tm=1024):
    s = dup.shape[1]
    d, wsh = w_sh.shape[1:]
    tm = _tile(s, tm)

    def body(a_ref, b_ref, o_ref, acc):
        kk = pl.program_id(1)
        part = _dot_nt(a_ref[0], b_ref[0])

        @pl.when(kk == 0)
        def _():
            acc[...] = part

        @pl.when(kk > 0)
        def _():
            acc[...] += part

        @pl.when(kk == N_CHIPS - 1)
        def _():
            o_ref[...] = acc[...]

    return pl.pallas_call(
        body, name="up_dx", grid=(s // tm, N_CHIPS),
        in_specs=[pl.BlockSpec((1, tm, wsh), lambda i, q: (q >> 1, i, q & 1)), pl.BlockSpec((1, d, wsh), lambda i, q: (q, 0, 0))],
        out_specs=pl.BlockSpec((tm, d), lambda i, q: (i, 0)),
        out_shape=jax.ShapeDtypeStruct((s, d), F32), scratch_shapes=[pltpu.VMEM((tm, d), F32)],
        compiler_params=_cp(("parallel", "arbitrary")),
    )(dup, w_sh)


def _up_dw(hnt, dup, *, tk=2048):
    d, s = hnt.shape
    wsh = 2 * D_FF // N_CHIPS
    tk = _tile(s, tk)
    nk = s // tk

    def body(a_ref, b_ref, o_ref, acc):
        def finish(r):
            o_ref[0] = r

        _accumulate(acc, _dot(a_ref[...], b_ref[0]), pl.program_id(1), nk, finish)

    return pl.pallas_call(
        body, name="up_dw", grid=(N_CHIPS, nk),
        in_specs=[pl.BlockSpec((d, tk), lambda j, q: (0, q)), pl.BlockSpec((1, tk, wsh), lambda j, q: (j >> 1, q, j & 1))],
        out_specs=pl.BlockSpec((1, d, wsh), lambda j, q: (j, 0, 0)),
        out_shape=jax.ShapeDtypeStruct((N_CHIPS, d, wsh), F32), scratch_shapes=[pltpu.VMEM((d, wsh), F32)],
        compiler_params=_cp(("parallel", "arbitrary")),
    )(hnt, dup)


BIG_ROWS =(IN_DIM // 4, Q_DIM // 4, D_INNER // 4, D_MODEL // 4, 2 * D_FF // 4, D_FF // 4)
PACK_ROWS = 5376


def _pack_shards(parts):
    rows = [p.reshape(-1, D_MODEL) for p in parts]
    pad = PACK_ROWS - sum(BIG_ROWS)
    return jnp.concatenate(rows + [jnp.zeros((pad, D_MODEL), rows[0].dtype)], axis=0)


def _unpack_shards(buf):
    out, off = [], 0
    for n in BIG_ROWS:
        out.append(buf[off:off + n])
        off += n
    return out


def _assemble(srcs, col_map, *, name, tr=256):
    arrays, lead = [], []
    for src in srcs:
        arr, j = src if isinstance(src, tuple) else (src, None)
        if not any(arr is a for a in arrays):
            arrays.append(arr)
        lead.append(([i for i, a in enumerate(arrays) if a is arr][0], j))
    rows = arrays[0].shape[-2]
    tr = _tile(rows, tr)
    out_w = len(col_map)
    tiles = []
    for t in range(out_w // 128):
        runs = []
        for lane in range(128):
            ent = col_map[t * 128 + lane]
            key = None if ent is None else (ent[0], ent[1] // 128, (lane - ent[1]) % 128)
            if runs and runs[-1][0] == key:
                runs[-1][2] = lane + 1
            else:
                runs.append([key, lane, lane + 1])
        tiles.append(runs)

    def body(*refs):
        o_ref = refs[-1]
        lane = lax.broadcasted_iota(jnp.int32, (tr, 128), 1)
        for t, runs in enumerate(tiles):
            acc = jnp.zeros((tr, 128), F32)
            for key, a, b in runs:
                if key is None:
                    continue
                sid, ct, shift = key
                ai, j = lead[sid]
                cols = slice(ct * 128, (ct + 1) * 128)
                piece = (refs[ai][:, cols] if j is None else refs[ai][j, :, cols]).astype(F32)
                if shift:
                    piece = pltpu.roll(piece, shift, 1)
                acc = piece if (a, b) == (0, 128) else jnp.where((lane >= a) & (lane < b), piece, acc)
            o_ref[:, t * 128:(t + 1) * 128] = acc.astype(BF16)

    specs = [pl.BlockSpec((tr, a.shape[1]), lambda i: (i, 0)) if a.ndim == 2
             else pl.BlockSpec((a.shape[0], tr, a.shape[2]), lambda i: (0, i, 0)) for a in arrays]
    return pl.pallas_call(
        body, name=name, grid=(rows // tr,), in_specs=specs, out_specs=pl.BlockSpec((tr, out_w), lambda i: (i, 0)),
        out_shape=jax.ShapeDtypeStruct((rows, out_w), BF16), compiler_params=_cp(("parallel",)),
    )(*arrays)


def _permute_cols_in(w):
    pad = jnp.zeros((w.shape[0], PW - IN_DIM), w.dtype)
    return jnp.concatenate([w[:, :6656], w[:, 6688:], w[:, 6656:6688], pad], axis=1)


def _unpermute_cols_in(g):
    return jnp.concatenate([g[:, :6656], g[:, O_DT:O_DT + 32], g[:, 6656:O_DT]], axis=1)


SMALL = ("norm1_w", "b_gate", "attn_sinks", "ssd_conv_b", "dt_bias", "a_log", "d_skip", "ssd_norm_w", "norm2_w",
         "ffn_conv_b", "final_norm_w", "ssd_conv_w", "ffn_conv_w")


def _pad128(v):
    v = v.reshape(-1)
    return jnp.pad(v, (0, (-v.shape[0]) % 128))


def _pack_small(parts):
    flat = jnp.concatenate([_pad128(p) for p in parts])
    flat = jnp.pad(flat, (0, (-flat.shape[0]) % 1024))
    return flat.reshape(-1, 128)


def _unpack_small(buf, shapes):
    flat, out, off = buf.reshape(-1), [], 0
    for shp in shapes:
        n = 1
        for q in shp:
            n *= q
        out.append(flat[off:off + n].reshape(shp))
        off += n + (-n) % 128
    return out


def _vec128(v):
    return jnp.pad(v.reshape(1, -1), ((0, 0), (0, 128 - v.shape[-1])))


def kernel(x, norm1_w, w_in, b_gate, attn_sinks, w_attn_o, ssd_conv_w, ssd_conv_b, dt_bias, a_log, d_skip, ssd_norm_w, w_ssd_o, w_out, norm2_w, w_up, ffn_conv_w, ffn_conv_b, w_down, final_norm_w, loss_target, m_norm1_w, m_w_in, m_b_gate, m_attn_sinks, m_w_attn_o, m_ssd_conv_w, m_ssd_conv_b, m_dt_bias, m_a_log, m_d_skip, m_ssd_norm_w, m_w_ssd_o, m_w_out, m_norm2_w, m_w_up, m_ffn_conv_w, m_ffn_conv_b, m_w_down, m_final_norm_w, v_norm1_w, v_w_in, v_b_gate, v_attn_sinks, v_w_attn_o, v_ssd_conv_w, v_ssd_conv_b, v_dt_bias, v_a_log, v_d_skip, v_ssd_norm_w, v_w_ssd_o, v_w_out, v_norm2_w, v_w_up, v_ffn_conv_w, v_ffn_conv_b, v_w_down, v_final_norm_w):
    ix, iy, ic = lax.axis_index("x"), lax.axis_index("y"), lax.axis_index("c")
    chip = 2 * ix + iy
    x2 = x[0]
    tgt = loss_target[0]
    s = x2.shape[0]

    wsh = IN_DIM // N_CHIPS
    big_shards = dict(w_in=jnp.pad(w_in[0], ((0, 0), (0, W_IN_PAD - wsh))), w_attn_o=w_attn_o[0], w_ssd_o=w_ssd_o[0],
                      w_out=w_out[0], w_up=w_up[0], w_down=w_down[0])
    gathered = {}
    (xn, xnt), (gathered["w_in"],) = _rms_fwd(x2, norm1_w, name="norm1_fwd", with_t=True,
                                              side=_GatherJob(("w_in",), [big_shards["w_in"]], at=(0.0, 0.5, 0.75)))
    early = ("w_attn_o", "w_ssd_o", "w_out")
    gather_early = _GatherJob(early, [big_shards[n] for n in early], at=(0.0, 0.5, 0.8))
    gather_up = _GatherJob(("w_up",), [big_shards["w_up"]], at=(0.0, 0.55, 0.85))
    gather_down = _GatherJob(("w_down",), [big_shards["w_down"]], at=(0.0, 0.5, 0.8))
    gw = gathered["w_in"]
    perm = list(range(O_GA)) + list(range(O_GA + N_SSD_HEADS, IN_DIM)) + list(range(O_GA, O_GA + N_SSD_HEADS))
    w_in_p = _assemble([(gw, j) for j in range(N_CHIPS)], [divmod(o, wsh) for o in perm] + [None] * (PW - IN_DIM),
                       name="w_in_assemble")
    small_sh = _pack_small([ssd_conv_w[0], ffn_conv_w[0]])
    small_all = _all_gather_small(small_sh)
    sc_parts = [_unpack_small(small_all[j], [(4, XBC_DIM // 4), (3, 2 * D_FF // 4)]) for j in range(N_CHIPS)]
    ssd_cw = jnp.concatenate([p[0] for p in sc_parts], axis=1)
    ffn_cw = jnp.concatenate([p[1] for p in sc_parts], axis=1)

    sinks128 = _vec128(attn_sinks)
    dtb128, alog128, dskip128 = _vec128(dt_bias), _vec128(a_log), _vec128(d_skip)

    proj, got = _mm(xn, w_in_p, name="proj_fwd", tn=1280, side=gather_early)
    gathered.update(zip(early, got))
    qkvt = _mm(w_in_p[:, :O_Z], xnt, name="qkv_fwd", ta=True)
    attn_pre, (gathered["w_up"],) = _attn_fwd(qkvt, sinks128, side=gather_up)
    xbc = _ssd_conv_fwd(proj, ssd_cw, ssd_conv_b)
    (y_ssd, hprev), (gathered["w_down"],) = _ssd_fwd(xbc, proj, dtb128, alog128, dskip128, side=gather_down)
    full = {n: gathered[n].reshape(-1, D_MODEL) for n in ("w_attn_o", "w_ssd_o", "w_out", "w_down")}
    full["w_up"] = gathered["w_up"]
    attn = _mm(attn_pre, full["w_attn_o"], name="attn_o_fwd", ta=True)
    yn = _gate_norm_fwd(y_ssd, proj, ssd_norm_w)
    ssd_out = _mm(yn, full["w_ssd_o"], name="ssd_o_fwd")
    merged = _merge_fwd(proj, b_gate, attn, ssd_out)
    h1 = _mm(merged, full["w_out"], name="out_fwd", resid=x2)
    hn, hnt = _rms_fwd(h1, norm2_w, name="norm2_fwd", with_t=True)
    up = _mm(hn, full["w_up"], name="up_fwd")
    act = _ffn_act_fwd(up, ffn_cw, ffn_conv_b)
    h2 = _mm(act, full["w_down"], name="down_fwd", resid=h1, tk=1408)

    dh2, loss_blk, g_final = _loss_bwd(h2, tgt, final_norm_w.reshape(1, -1))
    dact = _mm(dh2, full["w_down"], name="down_dx", tb=True, tn=1408)
    g_down = _mm(act, dh2, name="down_dw", ta=True, tm=1408, tk=2048)
    dup, g_ffn_cw, g_ffn_cb = _ffn_act_bwd(dact, up, ffn_cw, ffn_conv_b)
    dhn = _up_dx(dup, full["w_up"])
    g_up = _up_dw(hnt, dup)
    dh1, g_norm2 = _rms_bwd(dhn, h1, norm2_w, dh2, name="norm2_bwd")
    dmerged = _mm(dh1, full["w_out"], name="out_dx", tb=True)
    g_out = _mm(merged, dh1, name="out_dw", ta=True, tk=2048)
    dattn, dssd_out, dga, dgs, g_ba, g_bs = _merge_bwd(dmerged, proj, b_gate, attn, ssd_out)
    dyn = _mm(dssd_out, full["w_ssd_o"], name="ssd_o_dx", tb=True)
    g_ssd_o = _mm(yn, dssd_out, name="ssd_o_dw", ta=True, tk=2048)
    dy_ssd, dz, g_ssd_norm = _gate_norm_bwd(dyn, y_ssd, proj, ssd_norm_w)
    slot = lambda g: g.reshape(N_CHIPS, -1, D_MODEL)
    big_grads = {}
    red = ("w_down", "w_up")
    (dxbc, ddt, dvec), got = _ssd_bwd(xbc, proj, dtb128, alog128, dskip128, hprev, dy_ssd,
                                      side=_ReduceJob(red, [slot(g_down), g_up], at=(0.0, 0.2, 0.7, 0.95)))
    big_grads.update(zip(red, got))
    dxbc_raw, g_ssd_cw, g_ssd_cb = _ssd_conv_bwd(dxbc, proj, ssd_cw, ssd_conv_b)
    dattn_pre = _mm(full["w_attn_o"], dattn, name="attn_o_dx", tb=True)
    g_attn_o = _mm(attn_pre, dattn, name="attn_o_dw", tk=2048)
    red = ("w_out", "w_ssd_o", "w_attn_o")
    (dq, dk, dv, dsk), got = _attn_bwd(qkvt, sinks128, attn_pre, dattn_pre,
                                       side=_ReduceJob(red, [slot(g_out), slot(g_ssd_o), slot(g_attn_o)],
                                                       at=(0.0, 0.2, 0.5, 0.7)))
    big_grads.update(zip(red, got))
    pieces = [(dq.T, Q_DIM), (dk.T, KV_DIM), (dv.T, KV_DIM), (dz, D_INNER), (dxbc_raw, XBC_DIM), (ddt, N_SSD_HEADS),
              (dga, D_MODEL), (dgs, D_MODEL)]
    orig = [(i, c) for i, (_, w) in enumerate(pieces) for c in range(w)]
    dproj_sh = _assemble([p for p, _ in pieces],
                         [orig[j * wsh + c] if c < wsh else None for j in range(N_CHIPS) for c in range(W_IN_PAD)],
                         name="dproj_assemble")
    g_in = _proj_dw(xnt, dproj_sh)
    dxn, got = _proj_dx(dproj_sh, gathered["w_in"], side=_ReduceJob(("w_in",), [g_in], at=(0.0, 0.15, 0.75, 0.95)))
    big_grads["w_in"] = got[0]
    dx, g_norm1 = _rms_bwd(dxn, x2, norm1_w, dh1, name="norm1_bwd")


    small_g = dict(
        norm1_w=g_norm1, b_gate=jnp.concatenate([g_ba, g_bs], axis=1), attn_sinks=dsk[0:1, :16], ssd_conv_b=g_ssd_cb,
        dt_bias=dvec[0:1, :32], a_log=dvec[1:2, :32], d_skip=dvec[2:3, :32], ssd_norm_w=g_ssd_norm, norm2_w=g_norm2,
        ffn_conv_b=jnp.concatenate([g_ffn_cb[0], g_ffn_cb[1]], axis=1), final_norm_w=g_final, ssd_conv_w=g_ssd_cw,
        ffn_conv_w=jnp.concatenate([g_ffn_cw[0], g_ffn_cw[1]], axis=1))
    small_buf = _pack_small([small_g[n] for n in SMALL] + [loss_blk])
    small_sum = _all_reduce_small(small_buf)
    small_shapes = [(1, D_MODEL), (1, 2 * D_MODEL), (1, 16), (1, XBC_DIM), (1, 32), (1, 32), (1, 32), (1, D_INNER),
                    (1, D_MODEL), (1, 2 * D_FF), (D_MODEL,), (4, XBC_DIM), (3, 2 * D_FF), (1, 128)]
    small_list = _unpack_small(small_sum, small_shapes)
    loss = small_list[-1][0, 0]
    grads = dict(zip(SMALL, small_list[:-1]))
    grads["ssd_conv_w"] = lax.dynamic_slice_in_dim(grads["ssd_conv_w"], chip * (XBC_DIM // 4), XBC_DIM // 4, axis=1)
    grads["ffn_conv_w"] = lax.dynamic_slice_in_dim(grads["ffn_conv_w"], chip * (2 * D_FF // 4), 2 * D_FF // 4, axis=1)
    grads.update(big_grads)

    weights = dict(norm1_w=norm1_w, w_in=w_in, b_gate=b_gate, attn_sinks=attn_sinks, w_attn_o=w_attn_o, ssd_conv_w=ssd_conv_w,
                   ssd_conv_b=ssd_conv_b, dt_bias=dt_bias, a_log=a_log, d_skip=d_skip, ssd_norm_w=ssd_norm_w, w_ssd_o=w_ssd_o,
                   w_out=w_out, norm2_w=norm2_w, w_up=w_up, ffn_conv_w=ffn_conv_w, ffn_conv_b=ffn_conv_b, w_down=w_down,
                   final_norm_w=final_norm_w)
    ms = dict(norm1_w=m_norm1_w, w_in=m_w_in, b_gate=m_b_gate, attn_sinks=m_attn_sinks, w_attn_o=m_w_attn_o,
              ssd_conv_w=m_ssd_conv_w, ssd_conv_b=m_ssd_conv_b, dt_bias=m_dt_bias, a_log=m_a_log, d_skip=m_d_skip,
              ssd_norm_w=m_ssd_norm_w, w_ssd_o=m_w_ssd_o, w_out=m_w_out, norm2_w=m_norm2_w, w_up=m_w_up,
              ffn_conv_w=m_ffn_conv_w, ffn_conv_b=m_ffn_conv_b, w_down=m_w_down, final_norm_w=m_final_norm_w)
    vs = dict(norm1_w=v_norm1_w, w_in=v_w_in, b_gate=v_b_gate, attn_sinks=v_attn_sinks, w_attn_o=v_w_attn_o,
              ssd_conv_w=v_ssd_conv_w, ssd_conv_b=v_ssd_conv_b, dt_bias=v_dt_bias, a_log=v_a_log, d_skip=v_d_skip,
              ssd_norm_w=v_ssd_norm_w, w_ssd_o=v_w_ssd_o, w_out=v_w_out, norm2_w=v_norm2_w, w_up=v_w_up,
              ffn_conv_w=v_ffn_conv_w, ffn_conv_b=v_ffn_conv_b, w_down=v_w_down, final_norm_w=v_final_norm_w)
    order = list(weights)
    deltas, new_m, new_v = {}, {}, {}
    for n in BIG:
        shp = weights[n].shape
        res = _adamw(weights[n][0], grads[n], ms[n][0], vs[n][0], name="adamw_" + n)
        deltas[n], new_m[n], new_v[n], grads[n] = (a.reshape(shp) for a in res)
    smalls = [n for n in order if n not in BIG]
    as2d = lambda a: a.reshape(-1, a.shape[-1])
    res = _adamw_many(*[[as2d(src[n][0] if src[n].ndim == 3 else src[n]) for n in smalls] for src in (weights, grads, ms, vs)])
    for i, n in enumerate(smalls):
        deltas[n], new_m[n], new_v[n] = (res[q * len(smalls) + i].reshape(weights[n].shape) for q in range(3))
    out_grads = [grads[n].reshape(weights[n].shape) for n in order]
    return (loss, dx[None], *out_grads, *[deltas[n] for n in order], *[new_m[n] for n in order], *[new_v[n] for n in order])
```

```python
import functools

import jax
import jax.numpy as jnp
from jax import lax
from jax.experimental import pallas as pl
from jax.experimental.pallas import tpu as pltpu

F32 = jnp.float32
BF16 = jnp.bfloat16
HI = lax.Precision.HIGHEST

D_MODEL = 1024
Q_DIM = 1024
KV_DIM = 256
D_INNER = 2048
BC_DIM = 512
XBC_DIM = 3072
N_SSD_HEADS = 32
D_FF = 2816
IN_DIM = 8736
BLK = 128
EPS = 1e-5
NEG = -1e30

O_Q, O_K, O_V, O_Z, O_X, O_GA, O_GS, O_DT = 0, 1024, 1280, 1536, 3584, 6656, 7680, 8704
PW = 8960

ADAM_LR, ADAM_B1, ADAM_B2, ADAM_EPS, ADAM_WD, ADAM_STEP = 0.001, 0.9, 0.999, 1e-08, 0.01, 10

VMEM_LIMIT = 52 * 1024 * 1024
MESH = pl.DeviceIdType.MESH


def _cp(sem=None):
    return pltpu.CompilerParams(dimension_semantics=sem, vmem_limit_bytes=VMEM_LIMIT)


def _dot(a, b, prec=None):
    return jnp.dot(a, b, preferred_element_type=F32, precision=prec)


def _dot_nt(a, b, prec=None):
    return lax.dot_general(a, b, (((1,), (1,)), ((), ())), preferred_element_type=F32, precision=prec)


def _dot_tn(a, b, prec=None):
    return lax.dot_general(a, b, (((0,), (0,)), ((), ())), preferred_element_type=F32, precision=prec)


def _sigmoid(x):
    return 0.5 * jnp.tanh(0.5 * x) + 0.5


def _tile(n, want):
    t = min(n, want)
    while n % t:
        t -= 128
    return t


def _accumulate(acc, part, kk, nk, finish):
    if nk == 1:
        finish(part)
        return

    @pl.when(kk == 0)
    def _():
        acc[...] = part

    @pl.when(kk > 0)
    def _():
        acc[...] += part

    @pl.when(kk == nk - 1)
    def _():
        finish(acc[...])


def _mm(a, b, *, name, ta=False, tb=False, out_dtype=F32, resid=None, tm=1024, tn=1024, tk=1024, side=None):
    m, k = (a.shape[1], a.shape[0]) if ta else a.shape
    slots = b.ndim == 3
    if slots:
        n = b.shape[1] if tb else b.shape[0] * b.shape[2]
        tn, tk = (tn, b.shape[2]) if tb else (b.shape[2], tk)
    else:
        n = b.shape[0] if tb else b.shape[1]
    tm, tn, tk = _tile(m, tm), _tile(n, tn), _tile(k, tk)
    nk = k // tk
    dn = (((0 if ta else 1,), (1 if tb else 0,)), ((), ()))

    def body(*refs):
        if resid is None:
            a_ref, b_ref, o_ref, acc = refs
        else:
            a_ref, b_ref, r_ref, o_ref, acc = refs
        kk = pl.program_id(2)
        bv = b_ref[0] if slots else b_ref[...]
        part = lax.dot_general(a_ref[...].astype(BF16), bv.astype(BF16), dn, preferred_element_type=F32)

        def finish(r):
            if resid is not None:
                r = r + r_ref[...]
            o_ref[...] = r.astype(out_dtype)

        _accumulate(acc, part, kk, nk, finish)

    a_spec = pl.BlockSpec((tk, tm), lambda i, j, q: (q, i)) if ta else pl.BlockSpec((tm, tk), lambda i, j, q: (i, q))
    if slots:
        b_spec = (pl.BlockSpec((1, tn, tk), lambda i, j, q: (q, j, 0)) if tb
                  else pl.BlockSpec((1, tk, tn), lambda i, j, q: (j, q, 0)))
    else:
        b_spec = pl.BlockSpec((tn, tk), lambda i, j, q: (j, q)) if tb else pl.BlockSpec((tk, tn), lambda i, j, q: (q, j))
    o_spec = pl.BlockSpec((tm, tn), lambda i, j, q: (i, j))
    ins, specs = [a, b], [a_spec, b_spec]
    if resid is not None:
        ins.append(resid)
        specs.append(o_spec)
    own, extra = _hosted(
        body, name=name, grid=(m // tm, n // tn, nk), in_specs=specs, out_specs=[o_spec],
        out_shape=[jax.ShapeDtypeStruct((m, n), out_dtype)], scratch_shapes=[pltpu.VMEM((tm, tn), F32)],
        args=ins, sem=("parallel", "parallel", "arbitrary"), side=side)
    return own[0] if side is None else (own[0], extra)


def _rms_fwd(x, w, *, name, tm=512, with_t=False, side=None):
    s, d = x.shape
    tm = _tile(s, tm)

    def body(x_ref, w_ref, o_ref, *t_ref):
        xv = x_ref[...]
        r = lax.rsqrt(jnp.mean(xv * xv, axis=-1, keepdims=True) + EPS)
        y = (xv * r) * w_ref[...]
        o_ref[...] = y.astype(BF16)
        if with_t:
            t_ref[0][...] = y.T.astype(BF16)

    row = pl.BlockSpec((tm, d), lambda i: (i, 0))
    res, extra = _hosted(
        body, name=name, grid=(s // tm,), in_specs=[row, pl.BlockSpec((1, d), lambda i: (0, 0))],
        out_specs=[row] + [pl.BlockSpec((d, tm), lambda i: (0, i))] * with_t,
        out_shape=[jax.ShapeDtypeStruct((s, d), BF16)] + [jax.ShapeDtypeStruct((d, s), BF16)] * with_t,
        scratch_shapes=[], args=(x, w), sem=("parallel",), side=side)
    res = res if with_t else res[0]
    return res if side is None else (res, extra)


def _rms_bwd(dy, x, w, resid, *, name, tm=512):
    s, d = x.shape
    tm = _tile(s, tm)

    def body(dy_ref, x_ref, w_ref, r_ref, dx_ref, dw_ref):
        i = pl.program_id(0)
        xv = x_ref[...]
        r = lax.rsqrt(jnp.mean(xv * xv, axis=-1, keepdims=True) + EPS)
        xh = xv * r
        dyv = dy_ref[...]
        g = dyv * w_ref[...]
        dx_ref[...] = r_ref[...] + r * (g - xh * jnp.mean(g * xh, axis=-1, keepdims=True))
        part = jnp.sum(dyv * xh, axis=0, keepdims=True)

        @pl.when(i == 0)
        def _():
            dw_ref[...] = part

        @pl.when(i > 0)
        def _():
            dw_ref[...] += part

    row = pl.BlockSpec((tm, d), lambda i: (i, 0))
    vec = pl.BlockSpec((1, d), lambda i: (0, 0))
    return pl.pallas_call(
        body, name=name, grid=(s // tm,), in_specs=[row, row, vec, row], out_specs=[row, vec],
        out_shape=[jax.ShapeDtypeStruct((s, d), F32), jax.ShapeDtypeStruct((1, d), F32)],
        compiler_params=_cp(("arbitrary",)),
    )(dy, x, w, resid)


def _loss_bwd(h2, tgt, wf, *, tm=512):
    s, d = h2.shape
    tm = _tile(s, tm)

    def body(h_ref, t_ref, w_ref, dh_ref, loss_ref, dw_ref):
        i = pl.program_id(0)
        hv = h_ref[...]
        r = lax.rsqrt(jnp.mean(hv * hv, axis=-1, keepdims=True) + EPS)
        xh = hv * r
        wv = w_ref[...]
        e = xh * wv - t_ref[...]
        lpart = 0.5 * jnp.sum(jnp.mean(e * e, axis=-1, keepdims=True), axis=0, keepdims=True)
        dout = e * (1.0 / d)
        g = dout * wv
        dh_ref[...] = r * (g - xh * jnp.mean(g * xh, axis=-1, keepdims=True))
        part = jnp.sum(dout * xh, axis=0, keepdims=True)
        lrow = jnp.broadcast_to(lpart, (1, 128))

        @pl.when(i == 0)
        def _():
            dw_ref[...] = part
            loss_ref[...] = lrow

        @pl.when(i > 0)
        def _():
            dw_ref[...] += part
            loss_ref[...] += lrow

    row = pl.BlockSpec((tm, d), lambda i: (i, 0))
    vec = pl.BlockSpec((1, d), lambda i: (0, 0))
    return pl.pallas_call(
        body, name="loss_bwd", grid=(s // tm,), in_specs=[row, row, vec],
        out_specs=[row, pl.BlockSpec((1, 128), lambda i: (0, 0)), vec],
        out_shape=[jax.ShapeDtypeStruct((s, d), F32), jax.ShapeDtypeStruct((1, 128), F32),
                   jax.ShapeDtypeStruct((1, d), F32)],
        compiler_params=_cp(("arbitrary",)),
    )(h2, tgt, wf)


def _attn_mask(n):
    si = lax.broadcasted_iota(jnp.int32, (2 * BLK, 4 * BLK), 0)
    qi = lax.broadcasted_iota(jnp.int32, (2 * BLK, 4 * BLK), 1) & (BLK - 1)
    dist = BLK + qi - si
    kpos = n * BLK - BLK + si
    return (dist >= 0) & (dist < BLK) & (kpos >= 0)


def _attn_probs(q_ref, kc_ref, kp_ref, sk_ref, kvh, valid):
    rows = slice(kvh * 64, (kvh + 1) * 64)
    kt = jnp.concatenate([kp_ref[rows, :], kc_ref[rows, :]], axis=1).astype(BF16)
    qt = jnp.concatenate([q_ref[(kvh * 4 + g) * 64:(kvh * 4 + g + 1) * 64, :] for g in range(4)], axis=1).astype(BF16)
    s = _dot_tn(kt, qt) * 0.125
    s = jnp.where(valid, s, NEG)
    head = lax.broadcasted_iota(jnp.int32, (1, 4 * BLK), 1) >> 7
    sink = jnp.zeros((1, 4 * BLK), F32)
    for g in range(4):
        sink = jnp.where(head == g, sk_ref[0:1, kvh * 4 + g:kvh * 4 + g + 1], sink)
    m = jnp.maximum(jnp.max(s, axis=0, keepdims=True), sink)
    p = jnp.where(valid, jnp.exp(s - m), 0.0)
    es = jnp.exp(sink - m)
    inv = 1.0 / (jnp.sum(p, axis=0, keepdims=True) + es)
    return qt, kt, p * inv, es * inv


def _attn_in_specs(cur, prev):
    return [pl.BlockSpec((Q_DIM, BLK), lambda n: (0, cur(n))),
            pl.BlockSpec((KV_DIM, BLK), lambda n: (O_K // KV_DIM, cur(n))),
            pl.BlockSpec((KV_DIM, BLK), lambda n: (O_K // KV_DIM, prev(n))),
            pl.BlockSpec((KV_DIM, BLK), lambda n: (O_V // KV_DIM, cur(n))),
            pl.BlockSpec((KV_DIM, BLK), lambda n: (O_V // KV_DIM, prev(n))),
            pl.BlockSpec((1, 128), lambda n: (0, 0))]


def _attn_fwd(qkvt, sinks, side=None):
    s = qkvt.shape[1]
    nb = s // BLK

    def body(q_ref, kc_ref, kp_ref, vc_ref, vp_ref, sk_ref, o_ref):
        valid = _attn_mask(pl.program_id(0))
        for kvh in range(4):
            rows = slice(kvh * 64, (kvh + 1) * 64)
            _, _, probs, _ = _attn_probs(q_ref, kc_ref, kp_ref, sk_ref, kvh, valid)
            vt = jnp.concatenate([vp_ref[rows, :], vc_ref[rows, :]], axis=1).astype(BF16)
            o = _dot(vt, probs.astype(BF16))
            for g in range(4):
                h = kvh * 4 + g
                o_ref[h * 64:(h + 1) * 64, :] = o[:, g * BLK:(g + 1) * BLK].astype(BF16)

    own, extra = _hosted(
        body, name="attn_fwd", grid=(nb,), in_specs=_attn_in_specs(lambda n: n, lambda n: jnp.maximum(n - 1, 0)),
        out_specs=[pl.BlockSpec((Q_DIM, BLK), lambda n: (0, n))],
        out_shape=[jax.ShapeDtypeStruct((Q_DIM, s), BF16)], scratch_shapes=[],
        args=(qkvt, qkvt, qkvt, qkvt, qkvt, sinks), sem=("parallel",), side=side)
    return own[0] if side is None else (own[0], extra)


def _attn_bwd(qkvt, sinks, o, do, side=None):
    s = qkvt.shape[1]
    nb = s // BLK

    def body(q_ref, kc_ref, kp_ref, vc_ref, vp_ref, sk_ref, o_ref, do_ref, dq_ref, dk_ref, dv_ref, dsk_ref, ck, cv, nk, nv):
        n = pl.program_id(0)

        @pl.when(n == 0)
        def _():
            ck[...] = jnp.zeros_like(ck)
            cv[...] = jnp.zeros_like(cv)
            dsk_ref[...] = jnp.zeros_like(dsk_ref)

        @pl.when(n < nb)
        def _():
            valid = _attn_mask(n)
            lane = lax.broadcasted_iota(jnp.int32, (1, 128), 1)
            dsk = jnp.zeros((1, 128), F32)
            for kvh in range(4):
                rows = slice(kvh * 64, (kvh + 1) * 64)
                qt, kt, probs, psink = _attn_probs(q_ref, kc_ref, kp_ref, sk_ref, kvh, valid)
                vt = jnp.concatenate([vp_ref[rows, :], vc_ref[rows, :]], axis=1).astype(BF16)
                heads = [slice((kvh * 4 + g) * 64, (kvh * 4 + g + 1) * 64) for g in range(4)]
                dot = jnp.concatenate([do_ref[hh, :] for hh in heads], axis=1)
                ot = jnp.concatenate([o_ref[hh, :] for hh in heads], axis=1).astype(F32)
                delta = jnp.sum(dot * ot, axis=0, keepdims=True)
                dot16 = dot.astype(BF16)
                dp = _dot_tn(vt, dot16)
                ds = (probs * (dp - delta) * 0.125).astype(BF16)
                dqt = _dot(kt, ds)
                nk[rows, :] = _dot_nt(qt, ds)
                nv[rows, :] = _dot_nt(dot16, probs.astype(BF16))
                sd = psink * delta
                for g in range(4):
                    dq_ref[heads[g], :] = dqt[:, g * BLK:(g + 1) * BLK].astype(BF16)
                    val = -jnp.sum(sd[:, g * BLK:(g + 1) * BLK], axis=1, keepdims=True)
                    dsk = dsk + jnp.where(lane == kvh * 4 + g, val, 0.0)
            dsk_ref[0:1, :] += dsk
            dk_ref[...] = (ck[...] + nk[:, :BLK]).astype(BF16)
            dv_ref[...] = (cv[...] + nv[:, :BLK]).astype(BF16)
            ck[...] = nk[:, BLK:]
            cv[...] = nv[:, BLK:]

        @pl.when(n == nb)
        def _():
            dk_ref[...] = ck[...].astype(BF16)
            dv_ref[...] = cv[...].astype(BF16)

    cur = lambda n: jnp.minimum(n, nb - 1)
    prev = lambda n: jnp.maximum(jnp.minimum(n, nb - 1) - 1, 0)
    outb = lambda n: jnp.maximum(n - 1, 0)
    own, extra = _hosted(
        body, name="attn_bwd", grid=(nb + 1,),
        in_specs=_attn_in_specs(cur, prev) + [pl.BlockSpec((Q_DIM, BLK), lambda n: (0, cur(n))),
                                              pl.BlockSpec((Q_DIM, BLK), lambda n: (0, cur(n)))],
        out_specs=[pl.BlockSpec((Q_DIM, BLK), lambda n: (0, cur(n))),
                   pl.BlockSpec((KV_DIM, BLK), lambda n: (0, outb(n))),
                   pl.BlockSpec((KV_DIM, BLK), lambda n: (0, outb(n))),
                   pl.BlockSpec((8, 128), lambda n: (0, 0))],
        out_shape=[jax.ShapeDtypeStruct((Q_DIM, s), BF16), jax.ShapeDtypeStruct((KV_DIM, s), BF16),
                   jax.ShapeDtypeStruct((KV_DIM, s), BF16), jax.ShapeDtypeStruct((8, 128), F32)],
        scratch_shapes=[pltpu.VMEM((KV_DIM, BLK), F32)] * 2 + [pltpu.VMEM((KV_DIM, 2 * BLK), F32)] * 2,
        args=(qkvt, qkvt, qkvt, qkvt, qkvt, sinks, o, do), sem=("arbitrary",), side=side)
    return own if side is None else (own, extra)


def _shift_down(x, j):
    if j == 0:
        return x
    row = lax.broadcasted_iota(jnp.int32, x.shape, 0)
    return jnp.where(row >= j, pltpu.roll(x, j, 0), 0.0)


def _shift_up(x, j):
    if j == 0:
        return x
    s = x.shape[0]
    row = lax.broadcasted_iota(jnp.int32, x.shape, 0)
    return jnp.where(row < s - j, pltpu.roll(x, s - j, 0), 0.0)


def _conv(x, w_ref, b_ref):
    kk = w_ref.shape[0]
    y = _shift_down(x, kk - 1) * w_ref[0:1, :]
    for q in range(1, kk):
        y = y + _shift_down(x, kk - 1 - q) * w_ref[q:q + 1, :]
    return y + b_ref[...]


def _conv_bwd(dy, x, w_ref, dx_dtype):
    kk = w_ref.shape[0]
    dx = _shift_up(dy, kk - 1) * w_ref[0:1, :]
    dws = [jnp.sum(dy * _shift_down(x, kk - 1), axis=0, keepdims=True)]
    for q in range(1, kk):
        dx = dx + _shift_up(dy, kk - 1 - q) * w_ref[q:q + 1, :]
        dws.append(jnp.sum(dy * _shift_down(x, kk - 1 - q), axis=0, keepdims=True))
    return dx.astype(dx_dtype), dws, jnp.sum(dy, axis=0, keepdims=True)


def _dsilu(y, sg):
    return sg * (1.0 + y * (1.0 - sg))


CT = 256


def _ssd_conv_fwd(proj, w, b):
    s = proj.shape[0]

    def body(x_ref, w_ref, b_ref, o_ref):
        y = _conv(x_ref[...], w_ref, b_ref)
        o_ref[...] = y * _sigmoid(y)

    return pl.pallas_call(
        body, name="ssd_conv_fwd", grid=(XBC_DIM // CT,),
        in_specs=[pl.BlockSpec((s, CT), lambda i: (0, O_X // CT + i)), pl.BlockSpec((4, CT), lambda i: (0, i)),
                  pl.BlockSpec((1, CT), lambda i: (0, i))],
        out_specs=pl.BlockSpec((s, CT), lambda i: (0, i)),
        out_shape=jax.ShapeDtypeStruct((s, XBC_DIM), F32), compiler_params=_cp(("parallel",)),
    )(proj, w, b)


def _ssd_conv_bwd(dact, proj, w, b):
    s = proj.shape[0]

    def body(d_ref, x_ref, w_ref, b_ref, dx_ref, dw_ref, db_ref):
        x = x_ref[...]
        y = _conv(x, w_ref, b_ref)
        dy = d_ref[...] * _dsilu(y, _sigmoid(y))
        dx, dws, db = _conv_bwd(dy, x, w_ref, BF16)
        dx_ref[...] = dx
        for q in range(4):
            dw_ref[q:q + 1, :] = dws[q]
        db_ref[...] = db

    return pl.pallas_call(
        body, name="ssd_conv_bwd", grid=(XBC_DIM // CT,),
        in_specs=[pl.BlockSpec((s, CT), lambda i: (0, i)), pl.BlockSpec((s, CT), lambda i: (0, O_X // CT + i)),
                  pl.BlockSpec((4, CT), lambda i: (0, i)), pl.BlockSpec((1, CT), lambda i: (0, i))],
        out_specs=[pl.BlockSpec((s, CT), lambda i: (0, i)), pl.BlockSpec((4, CT), lambda i: (0, i)),
                   pl.BlockSpec((1, CT), lambda i: (0, i))],
        out_shape=[jax.ShapeDtypeStruct((s, XBC_DIM), BF16), jax.ShapeDtypeStruct((4, XBC_DIM), F32),
                   jax.ShapeDtypeStruct((1, XBC_DIM), F32)],
        compiler_params=_cp(("parallel",)),
    )(dact, proj, w, b)


NFT = D_FF // CT


def _ffn_act_fwd(up, w, b):
    s = up.shape[0]

    def body(v_ref, g_ref, wv_ref, wg_ref, bv_ref, bg_ref, o_ref):
        val = _conv(v_ref[...], wv_ref, bv_ref)
        gt = _conv(g_ref[...], wg_ref, bg_ref)
        o_ref[...] = ((gt * _sigmoid(gt)) * val).astype(BF16)

    col = lambda off: (lambda i: (0, off + i))
    return pl.pallas_call(
        body, name="ffn_act_fwd", grid=(NFT,),
        in_specs=[pl.BlockSpec((s, CT), col(0)), pl.BlockSpec((s, CT), col(NFT)),
                  pl.BlockSpec((3, CT), col(0)), pl.BlockSpec((3, CT), col(NFT)),
                  pl.BlockSpec((1, CT), col(0)), pl.BlockSpec((1, CT), col(NFT))],
        out_specs=pl.BlockSpec((s, CT), col(0)),
        out_shape=jax.ShapeDtypeStruct((s, D_FF), BF16), compiler_params=_cp(("parallel",)),
    )(up, up, w, w, b, b)


def _ffn_act_bwd(dact, up, w, b):
    s = up.shape[0]

    def body(d_ref, v_ref, g_ref, wv_ref, wg_ref, bv_ref, bg_ref, dx_ref, dw_ref, db_ref):
        xv, xg = v_ref[...], g_ref[...]
        val = _conv(xv, wv_ref, bv_ref)
        gt = _conv(xg, wg_ref, bg_ref)
        sg = _sigmoid(gt)
        d = d_ref[...]
        for half, (dy, x, w_ref) in enumerate(((d * (gt * sg), xv, wv_ref), (d * val * _dsilu(gt, sg), xg, wg_ref))):
            dx, dws, db = _conv_bwd(dy, x, w_ref, BF16)
            dx_ref[half] = dx
            for q in range(3):
                dw_ref[half, q:q + 1, :] = dws[q]
            db_ref[half] = db

    col = lambda off: (lambda i: (0, off + i))
    both = lambda i: (0, 0, i)
    return pl.pallas_call(
        body, name="ffn_act_bwd", grid=(NFT,),
        in_specs=[pl.BlockSpec((s, CT), col(0)), pl.BlockSpec((s, CT), col(0)), pl.BlockSpec((s, CT), col(NFT)),
                  pl.BlockSpec((3, CT), col(0)), pl.BlockSpec((3, CT), col(NFT)),
                  pl.BlockSpec((1, CT), col(0)), pl.BlockSpec((1, CT), col(NFT))],
        out_specs=[pl.BlockSpec((2, s, CT), both), pl.BlockSpec((2, 3, CT), both), pl.BlockSpec((2, 1, CT), both)],
        out_shape=[jax.ShapeDtypeStruct((2, s, D_FF), BF16), jax.ShapeDtypeStruct((2, 3, D_FF), F32),
                   jax.ShapeDtypeStruct((2, 1, D_FF), F32)],
        compiler_params=_cp(("parallel",)),
    )(dact, up, up, w, w, b, b)


def _expand_mat():
    r = lax.broadcasted_iota(jnp.int32, (128, D_INNER), 0)
    c = lax.broadcasted_iota(jnp.int32, (128, D_INNER), 1)
    return ((c >> 6) == r).astype(BF16)


def _reduce_mat():
    r = lax.broadcasted_iota(jnp.int32, (D_INNER, 128), 0)
    c = lax.broadcasted_iota(jnp.int32, (D_INNER, 128), 1)
    return ((r >> 6) == c).astype(BF16)


def _split(v, parts):
    out = []
    for _ in range(parts - 1):
        p = v.astype(BF16)
        out.append(p)
        v = v - p.astype(F32)
    out.append(v.astype(BF16))
    return out


def _sel_dot(v, sel, parts):
    acc = None
    for p in reversed(_split(v, parts)):
        t = _dot(p, sel)
        acc = t if acc is None else acc + t
    return acc


def _row8(v):
    return jnp.broadcast_to(v, (8, v.shape[1]))


def _tril():
    r = lax.broadcasted_iota(jnp.int32, (BLK, BLK), 0)
    c = lax.broadcasted_iota(jnp.int32, (BLK, BLK), 1)
    return r >= c


def _softplus(x):
    return jnp.maximum(x, 0.0) + jnp.log(1.0 + jnp.exp(-jnp.abs(x)))


def _ssd_common(dtraw_ref, dtb_ref, alog_ref):
    causal = _tril()
    e_mat = _expand_mat()
    a_neg = -jnp.exp(alog_ref[...])
    dt = _softplus(dtraw_ref[...] + dtb_ref[...])
    a_cs = _dot(causal.astype(F32), dt * a_neg, HI)
    a_cs_t = a_cs.T
    dt_x = _sel_dot(dt, e_mat, 3)
    acs_x = _sel_dot(a_cs, e_mat, 3)
    alast_x = acs_x[BLK - 1:BLK, :]
    ea_x = jnp.exp(acs_x)
    ds_x = jnp.exp(alast_x - acs_x)
    elast_x = jnp.exp(alast_x)
    return causal, e_mat, a_neg, dt, a_cs, a_cs_t, dt_x, ea_x, ds_x, elast_x


def _decay(a_cs, a_cs_t, h, causal):
    seg = a_cs[:, h:h + 1] - a_cs_t[h:h + 1, :]
    return jnp.where(causal, jnp.exp(jnp.where(causal, seg, 0.0)), 0.0)


def _ssd_fwd(xbc, proj, dt_bias, a_log, d_skip, side=None):
    s = xbc.shape[0]
    nc = s // BLK

    def body(xs_ref, b_ref, c_ref, dtraw_ref, dtb_ref, alog_ref, dskip_ref, y_ref, hp_ref, h_scr, xc16):
        @pl.when(pl.program_id(0) == 0)
        def _():
            h_scr[...] = jnp.zeros_like(h_scr)

        causal, e_mat, _, _, a_cs, a_cs_t, dt_x, ea_x, ds_x, elast_x = _ssd_common(dtraw_ref, dtb_ref, alog_ref)
        dskip_x = _sel_dot(_row8(dskip_ref[...]), e_mat, 3)[0:1]
        xs = xs_ref[...]
        xc = xs * dt_x
        xc16[...] = xc.astype(BF16)
        xcd = (xc * ds_x).astype(BF16)
        hp_ref[0] = h_scr[...]
        for g in range(4):
            gs = slice(g * 512, (g + 1) * 512)
            cg = c_ref[:, g * 128:(g + 1) * 128].astype(BF16)
            bg = b_ref[:, g * 128:(g + 1) * 128].astype(BF16)
            cb = _dot_nt(cg, bg)
            hg = h_scr[:, gs]
            yoff = _dot(cg, hg.astype(BF16)) * ea_x[:, gs]
            for j in range(8):
                h = g * 8 + j
                hsl = slice(h * 64, (h + 1) * 64)
                mm = (cb * _decay(a_cs, a_cs_t, h, causal)).astype(BF16)
                y_ref[:, hsl] = _dot(mm, xc16[:, hsl])
            y_ref[:, gs] += yoff + xs[:, gs] * dskip_x[:, gs]
            h_scr[:, gs] = hg * elast_x[:, gs] + _dot_tn(bg, xcd[:, gs])

    vec = pl.BlockSpec((1, 128), lambda c: (0, 0))
    own, extra = _hosted(
        body, name="ssd_fwd", grid=(nc,),
        in_specs=[pl.BlockSpec((BLK, D_INNER), lambda c: (c, 0)),
                  pl.BlockSpec((BLK, BC_DIM), lambda c: (c, D_INNER // BC_DIM)),
                  pl.BlockSpec((BLK, BC_DIM), lambda c: (c, D_INNER // BC_DIM + 1)),
                  pl.BlockSpec((BLK, 128), lambda c: (c, O_DT // 128)), vec, vec, vec],
        out_specs=[pl.BlockSpec((BLK, D_INNER), lambda c: (c, 0)),
                   pl.BlockSpec((1, 128, D_INNER), lambda c: (c, 0, 0))],
        out_shape=[jax.ShapeDtypeStruct((s, D_INNER), F32), jax.ShapeDtypeStruct((nc, 128, D_INNER), F32)],
        scratch_shapes=[pltpu.VMEM((128, D_INNER), F32), pltpu.VMEM((BLK, D_INNER), BF16)],
        args=(xbc, xbc, xbc, proj, dt_bias, a_log, d_skip), sem=("arbitrary",), side=side)
    return own if side is None else (own, extra)


def _ssd_bwd(xbc, proj, dt_bias, a_log, d_skip, hprev, dy, side=None):
    s = xbc.shape[0]
    nc = s // BLK

    def body(xs_ref, b_ref, c_ref, dtraw_ref, dtb_ref, alog_ref, dskip_ref, hp_ref, dy_ref,
             dxbc_ref, ddt_ref, dvec_ref, dh_scr, xc16, dy16, dxc_scr, dacs_r, tdiff):
        step = pl.program_id(0)
        dacs_r[...] = jnp.zeros_like(dacs_r)

        @pl.when(step == 0)
        def _():
            dh_scr[...] = jnp.zeros_like(dh_scr)
            dvec_ref[...] = jnp.zeros_like(dvec_ref)

        causal, e_mat, a_neg, dt, a_cs, a_cs_t, dt_x, ea_x, ds_x, elast_x = _ssd_common(dtraw_ref, dtb_ref, alog_ref)
        r_mat = _reduce_mat()
        lane = lax.broadcasted_iota(jnp.int32, (1, 128), 1)
        dskip_x = _sel_dot(_row8(dskip_ref[...]), e_mat, 3)[0:1]
        xs = xs_ref[...]
        dy = dy_ref[...]
        xc = xs * dt_x
        xcd = xc * ds_x
        xc16[...] = xc.astype(BF16)
        dy16[...] = dy.astype(BF16)
        dyea = dy * ea_x
        dh = dh_scr[...]
        hp = hp_ref[0]
        dalast_x = jnp.sum(dh * hp, axis=0, keepdims=True) * elast_x
        dacs = jnp.zeros((BLK, 128), F32)
        for g in range(4):
            gs = slice(g * 512, (g + 1) * 512)
            bsl = slice(g * 128, (g + 1) * 128)
            cg = c_ref[:, bsl].astype(BF16)
            bg = b_ref[:, bsl].astype(BF16)
            cb = _dot_nt(cg, bg)
            hg16 = hp[:, gs].astype(BF16)
            dhg16 = dh[:, gs].astype(BF16)
            raw = _dot(cg, hg16)
            draw16 = dyea[:, gs].astype(BF16)
            dcg = _dot_nt(draw16, hg16)
            dhp_g = _dot_tn(cg, draw16)
            dbg = _dot_nt(xcd[:, gs].astype(BF16), dhg16)
            dxcd = _dot(bg, dhg16)
            dcb = jnp.zeros((BLK, BLK), F32)
            for j in range(8):
                h = g * 8 + j
                hsl = slice(h * 64, (h + 1) * 64)
                decay = _decay(a_cs, a_cs_t, h, causal)
                m = cb * decay
                dm = _dot_nt(dy16[:, hsl], xc16[:, hsl])
                dxc_scr[:, hsl] = _dot_tn(m.astype(BF16), dy16[:, hsl])
                dcb = dcb + dm * decay
                dseg = dm * m
                oneh = jnp.where(lane == h, 1.0, 0.0)
                dacs = dacs + jnp.sum(dseg, axis=1, keepdims=True) * oneh
                dacs_r[h:h + 1, :] = jnp.sum(dseg, axis=0, keepdims=True)
            dcb16 = dcb.astype(BF16)
            dcg = dcg + _dot(dcb16, bg)
            dbg = dbg + _dot_tn(dcb16, cg)
            dxbc_ref[:, D_INNER + g * 128:D_INNER + (g + 1) * 128] = dbg
            dxbc_ref[:, D_INNER + BC_DIM + g * 128:D_INNER + BC_DIM + (g + 1) * 128] = dcg
            dxc_scr[:, gs] += dxcd * ds_x[:, gs]
            dh_scr[:, gs] = dh[:, gs] * elast_x[:, gs] + dhp_g
            tst = dxcd * xcd[:, gs]
            tdiff[:, gs] = dy[:, gs] * (raw * ea_x[:, gs]) - tst
            tdiff[BLK - 1:BLK, gs] += jnp.sum(tst, axis=0, keepdims=True)
        dxc = dxc_scr[...]
        row = lax.broadcasted_iota(jnp.int32, (BLK, D_INNER), 0)
        tfull = tdiff[...] + jnp.where(row == BLK - 1, dalast_x, 0.0)
        dacs = dacs + _sel_dot(tfull, r_mat, 2) - dacs_r[...].T
        da = _dot_tn(causal.astype(F32), dacs, HI)
        ddt = da * a_neg + _sel_dot(dxc * xs, r_mat, 2)
        lmask = lax.broadcasted_iota(jnp.int32, (BLK, 128), 1) < N_SSD_HEADS
        ddtraw = jnp.where(lmask, ddt * _sigmoid(dtraw_ref[...] + dtb_ref[...]), 0.0)
        ddt_ref[...] = ddtraw.astype(BF16)
        dxbc_ref[:, 0:D_INNER] = dy * dskip_x + dxc * dt_x
        dvec_ref[0:1, :] += jnp.sum(ddtraw, axis=0, keepdims=True)
        dvec_ref[1:2, :] += jnp.where(lane < N_SSD_HEADS, jnp.sum(da * dt, axis=0, keepdims=True) * a_neg, 0.0)
        dvec_ref[2:3, :] += _sel_dot(_row8(jnp.sum(dy * xs, axis=0, keepdims=True)), r_mat, 3)[0:1]

    rev = lambda c: nc - 1 - c
    vec = pl.BlockSpec((1, 128), lambda c: (0, 0))
    own, extra = _hosted(
        body, name="ssd_bwd", grid=(nc,),
        in_specs=[pl.BlockSpec((BLK, D_INNER), lambda c: (rev(c), 0)),
                  pl.BlockSpec((BLK, BC_DIM), lambda c: (rev(c), D_INNER // BC_DIM)),
                  pl.BlockSpec((BLK, BC_DIM), lambda c: (rev(c), D_INNER // BC_DIM + 1)),
                  pl.BlockSpec((BLK, 128), lambda c: (rev(c), O_DT // 128)), vec, vec, vec,
                  pl.BlockSpec((1, 128, D_INNER), lambda c: (rev(c), 0, 0)),
                  pl.BlockSpec((BLK, D_INNER), lambda c: (rev(c), 0))],
        out_specs=[pl.BlockSpec((BLK, XBC_DIM), lambda c: (rev(c), 0)),
                   pl.BlockSpec((BLK, 128), lambda c: (rev(c), 0)),
                   pl.BlockSpec((8, 128), lambda c: (0, 0))],
        out_shape=[jax.ShapeDtypeStruct((s, XBC_DIM), F32), jax.ShapeDtypeStruct((s, 128), BF16),
                   jax.ShapeDtypeStruct((8, 128), F32)],
        scratch_shapes=[pltpu.VMEM((128, D_INNER), F32), pltpu.VMEM((BLK, D_INNER), BF16),
                        pltpu.VMEM((BLK, D_INNER), BF16), pltpu.VMEM((BLK, D_INNER), F32),
                        pltpu.VMEM((128, BLK), F32), pltpu.VMEM((BLK, D_INNER), F32)],
        args=(xbc, xbc, xbc, proj, dt_bias, a_log, d_skip, hprev, dy), sem=("arbitrary",), side=side)
    return own if side is None else (own, extra)


GW = 512


def _gate_norm_fwd(y, proj, wn, *, tm=512):
    s = y.shape[0]
    tm = _tile(s, tm)

    def body(y_ref, z_ref, w_ref, o_ref):
        z = z_ref[...]
        y2 = y_ref[...] * (z * _sigmoid(z))
        r = lax.rsqrt(jnp.mean(y2 * y2, axis=-1, keepdims=True) + EPS)
        o_ref[...] = ((y2 * r) * w_ref[...]).astype(BF16)

    return pl.pallas_call(
        body, name="gate_norm_fwd", grid=(s // tm, 4),
        in_specs=[pl.BlockSpec((tm, GW), lambda i, g: (i, g)), pl.BlockSpec((tm, GW), lambda i, g: (i, O_Z // GW + g)),
                  pl.BlockSpec((1, GW), lambda i, g: (0, g))],
        out_specs=pl.BlockSpec((tm, GW), lambda i, g: (i, g)),
        out_shape=jax.ShapeDtypeStruct((s, D_INNER), BF16), compiler_params=_cp(("parallel", "parallel")),
    )(y, proj, wn)


def _gate_norm_bwd(dyn, y, proj, wn, *, tm=512):
    s = y.shape[0]
    tm = _tile(s, tm)

    def body(d_ref, y_ref, z_ref, w_ref, dy_ref, dz_ref, dw_ref):
        i = pl.program_id(1)
        z = z_ref[...]
        sg = _sigmoid(z)
        sz = z * sg
        yv = y_ref[...]
        y2 = yv * sz
        r = lax.rsqrt(jnp.mean(y2 * y2, axis=-1, keepdims=True) + EPS)
        xh = y2 * r
        dv = d_ref[...]
        g = dv * w_ref[...]
        dy2 = r * (g - xh * jnp.mean(g * xh, axis=-1, keepdims=True))
        dy_ref[...] = dy2 * sz
        dz_ref[...] = (dy2 * yv * _dsilu(z, sg)).astype(BF16)
        part = jnp.sum(dv * xh, axis=0, keepdims=True)

        @pl.when(i == 0)
        def _():
            dw_ref[...] = part

        @pl.when(i > 0)
        def _():
            dw_ref[...] += part

    blk = pl.BlockSpec((tm, GW), lambda g, i: (i, g))
    vec = pl.BlockSpec((1, GW), lambda g, i: (0, g))
    return pl.pallas_call(
        body, name="gate_norm_bwd", grid=(4, s // tm),
        in_specs=[blk, blk, pl.BlockSpec((tm, GW), lambda g, i: (i, O_Z // GW + g)), vec],
        out_specs=[blk, blk, vec],
        out_shape=[jax.ShapeDtypeStruct((s, D_INNER), F32), jax.ShapeDtypeStruct((s, D_INNER), BF16),
                   jax.ShapeDtypeStruct((1, D_INNER), F32)],
        compiler_params=_cp(("parallel", "arbitrary")),
    )(dyn, y, proj, wn)


def _merge_fwd(proj, b_gate, attn, ssd_out, *, tm=512):
    s = attn.shape[0]
    tm = _tile(s, tm)

    def body(ga_ref, gs_ref, ba_ref, bs_ref, a_ref, s_ref, o_ref):
        ga = _sigmoid(ga_ref[...] + ba_ref[...])
        gs = _sigmoid(gs_ref[...] + bs_ref[...])
        o_ref[...] = (ga * a_ref[...] + gs * s_ref[...]).astype(BF16)

    blk = pl.BlockSpec((tm, GW), lambda i, j: (i, j))
    return pl.pallas_call(
        body, name="merge_fwd", grid=(s // tm, 2),
        in_specs=[pl.BlockSpec((tm, GW), lambda i, j: (i, O_GA // GW + j)),
                  pl.BlockSpec((tm, GW), lambda i, j: (i, O_GS // GW + j)),
                  pl.BlockSpec((1, GW), lambda i, j: (0, j)), pl.BlockSpec((1, GW), lambda i, j: (0, 2 + j)), blk, blk],
        out_specs=blk, out_shape=jax.ShapeDtypeStruct((s, D_MODEL), BF16),
        compiler_params=_cp(("parallel", "parallel")),
    )(proj, proj, b_gate, b_gate, attn, ssd_out)


def _merge_bwd(dm, proj, b_gate, attn, ssd_out, *, tm=512):
    s = attn.shape[0]
    tm = _tile(s, tm)

    def body(d_ref, ga_ref, gs_ref, ba_ref, bs_ref, a_ref, s_ref, da_ref, ds_ref, dga_ref, dgs_ref, dba_ref, dbs_ref):
        i = pl.program_id(1)
        ga = _sigmoid(ga_ref[...] + ba_ref[...])
        gs = _sigmoid(gs_ref[...] + bs_ref[...])
        d = d_ref[...]
        da_ref[...] = (d * ga).astype(BF16)
        ds_ref[...] = (d * gs).astype(BF16)
        dga = d * a_ref[...] * (ga * (1.0 - ga))
        dgs = d * s_ref[...] * (gs * (1.0 - gs))
        dga_ref[...] = dga.astype(BF16)
        dgs_ref[...] = dgs.astype(BF16)
        pa = jnp.sum(dga, axis=0, keepdims=True)
        ps = jnp.sum(dgs, axis=0, keepdims=True)

        @pl.when(i == 0)
        def _():
            dba_ref[...] = pa
            dbs_ref[...] = ps

        @pl.when(i > 0)
        def _():
            dba_ref[...] += pa
            dbs_ref[...] += ps

    blk = pl.BlockSpec((tm, GW), lambda j, i: (i, j))
    vec = pl.BlockSpec((1, GW), lambda j, i: (0, j))
    sd = jax.ShapeDtypeStruct((s, D_MODEL), BF16)
    vd = jax.ShapeDtypeStruct((1, D_MODEL), F32)
    return pl.pallas_call(
        body, name="merge_bwd", grid=(2, s // tm),
        in_specs=[blk, pl.BlockSpec((tm, GW), lambda j, i: (i, O_GA // GW + j)),
                  pl.BlockSpec((tm, GW), lambda j, i: (i, O_GS // GW + j)),
                  vec, pl.BlockSpec((1, GW), lambda j, i: (0, 2 + j)), blk, blk],
        out_specs=[blk, blk, blk, blk, vec, vec], out_shape=[sd, sd, sd, sd, vd, vd],
        compiler_params=_cp(("parallel", "arbitrary")),
    )(dm, proj, proj, b_gate, b_gate, attn, ssd_out)


def _adamw_math(w, g, m, v):
    mn = ADAM_B1 * m + (1.0 - ADAM_B1) * g
    vn = ADAM_B2 * v + (1.0 - ADAM_B2) * (g * g)
    m_hat = mn / (1.0 - ADAM_B1 ** ADAM_STEP)
    v_hat = vn / (1.0 - ADAM_B2 ** ADAM_STEP)
    return -ADAM_LR * (m_hat / (jnp.sqrt(v_hat) + ADAM_EPS) + ADAM_WD * w), mn, vn


def _adamw_many(ws, gs, ms, vs):
    n = len(ws)

    def body(*refs):
        outs = refs[4 * n:]
        for i in range(n):
            res = _adamw_math(*[refs[q * n + i][...] for q in range(4)])
            for q in range(3):
                outs[q * n + i][...] = res[q]

    return pl.pallas_call(body, name="adamw_small", out_shape=[jax.ShapeDtypeStruct(w.shape, F32) for w in ws] * 3,
                          compiler_params=_cp())(*ws, *gs, *ms, *vs)


def _adamw(w, g, m, v, *, name, tm=128):
    r, c = w.shape
    tm = r if (r < tm or r % tm) else tm

    def body(w_ref, g_ref, m_ref, v_ref, d_ref, nm_ref, nv_ref, g_out):
        gv = g_ref[:, :c]
        d_ref[...], nm_ref[...], nv_ref[...] = _adamw_math(w_ref[...], gv, m_ref[...], v_ref[...])
        g_out[...] = gv

    blk = pl.BlockSpec((tm, c), lambda i: (i, 0))
    sd = jax.ShapeDtypeStruct((r, c), F32)
    return pl.pallas_call(
        body, name=name, grid=(r // tm,), in_specs=[blk, pl.BlockSpec((tm, g.shape[1]), lambda i: (i, 0)), blk, blk],
        out_specs=[blk] * 4, out_shape=[sd] * 4, compiler_params=_cp(("parallel",)),
    )(w, g, m, v)


ANY = pl.BlockSpec(memory_space=pl.ANY)
N_CHIPS = 4


def _chip_of(k, x, y):
    return (x ^ (k >> 1), y ^ (k & 1))


def _all_gather_small(shard):
    r, c = shard.shape
    hr = r // 2

    def body(sh_ref, out_ref, send_sems, recv_sems, local_sem):
        x, y, cc = lax.axis_index("x"), lax.axis_index("y"), lax.axis_index("c")

        def half(px, py, pc):
            return out_ref.at[2 * px + py, pl.ds(pc * hr, hr), :]

        def copy(k, px, py, pc, to, src=None):
            return pltpu.make_async_remote_copy(
                src_ref=half(px, py, pc) if src is None else src, dst_ref=half(px, py, pc),
                send_sem=send_sems.at[k], recv_sem=recv_sems.at[k], device_id=to, device_id_type=MESH)

        mine = pltpu.make_async_copy(sh_ref, out_ref.at[2 * x + y], local_sem)
        mine.start()
        chips = [_chip_of(k, x, y) for k in (1, 2, 3)]
        first = [copy(j, x, y, cc, (*chip, cc), src=sh_ref.at[pl.ds(cc * hr, hr), :]) for j, chip in enumerate(chips)]
        for cp in first:
            cp.start()
        passed = [copy(3 + j, *chip, cc, (x, y, 1 - cc)) for j, chip in enumerate(chips)]
        for j, chip in enumerate(chips):
            copy(j, *chip, cc, (x, y, cc)).wait_recv()
            passed[j].start()
        for j, chip in enumerate(chips):
            copy(3 + j, *chip, 1 - cc, (x, y, cc)).wait_recv()
        for cp in first + passed:
            cp.wait_send()
        mine.wait()

    return pl.pallas_call(
        body, name="all_gather_small", in_specs=[ANY], out_specs=ANY,
        out_shape=jax.ShapeDtypeStruct((N_CHIPS, r, c), shard.dtype),
        scratch_shapes=[pltpu.SemaphoreType.DMA((6,)), pltpu.SemaphoreType.DMA((6,)), pltpu.SemaphoreType.DMA],
    )(shard)


def _cast_bf16(a, *, name, tm=512):
    n, r, c = a.shape
    tm = _tile(r, tm) if r % 128 == 0 else r

    def body(a_ref, o_ref):
        o_ref[...] = a_ref[...].astype(BF16)

    blk = pl.BlockSpec((1, tm, c), lambda i, j: (i, j, 0))
    return pl.pallas_call(body, name=name, grid=(n, r // tm), in_specs=[blk], out_specs=blk,
                          out_shape=jax.ShapeDtypeStruct(a.shape, BF16), compiler_params=_cp(("parallel", "parallel")))(a)


def _pair_exchange(g16, hr):
    n, r, c = g16.shape

    def body(g_ref, out_ref, send_sem, recv_sem):
        x, y, cc = lax.axis_index("x"), lax.axis_index("y"), lax.axis_index("c")
        cp = pltpu.make_async_remote_copy(
            src_ref=g_ref.at[:, pl.ds((1 - cc) * hr, hr), :], dst_ref=out_ref, send_sem=send_sem, recv_sem=recv_sem,
            device_id=(x, y, 1 - cc), device_id_type=MESH)
        cp.start()
        cp.wait()

    return pl.pallas_call(
        body, name="grad_pair_exchange", in_specs=[ANY], out_specs=ANY,
        out_shape=jax.ShapeDtypeStruct((n, hr, c), g16.dtype),
        scratch_shapes=[pltpu.SemaphoreType.DMA, pltpu.SemaphoreType.DMA],
    )(g16)


def _pair_add(g, recv, half_idx, hr, *, tm=384):
    n, r, c = g.shape
    nt = hr // tm

    def body(hi_ref, g_ref, r_ref, o32_ref, o16_ref):
        v = g_ref[...] + r_ref[...].astype(F32)
        o32_ref[...] = v
        o16_ref[...] = v.astype(BF16)

    gs = pltpu.PrefetchScalarGridSpec(
        num_scalar_prefetch=1, grid=(n, nt),
        in_specs=[pl.BlockSpec((1, tm, c), lambda i, j, hi: (i, hi[0] * nt + j, 0)),
                  pl.BlockSpec((1, tm, c), lambda i, j, hi: (i, j, 0))],
        out_specs=[pl.BlockSpec((1, tm, c), lambda i, j, hi: (i, j, 0))] * 2)
    return pl.pallas_call(
        body, name="grad_pair_add", grid_spec=gs,
        out_shape=[jax.ShapeDtypeStruct((n, hr, c), F32), jax.ShapeDtypeStruct((n, hr, c), BF16)],
        compiler_params=_cp(("parallel", "parallel")),
    )(half_idx, g, recv)


def _chip_exchange(p16):
    n, hr, c = p16.shape

    def body(p_ref, out_ref, send_sems, recv_sems):
        x, y, cc = lax.axis_index("x"), lax.axis_index("y"), lax.axis_index("c")
        cps = []
        for j, k in enumerate((1, 2, 3)):
            px, py = _chip_of(k, x, y)
            cps.append(pltpu.make_async_remote_copy(
                src_ref=p_ref.at[2 * px + py], dst_ref=out_ref.at[j], send_sem=send_sems.at[j], recv_sem=recv_sems.at[j],
                device_id=(px, py, cc), device_id_type=MESH))
        for cp in cps:
            cp.start()
        for cp in cps:
            cp.wait()

    return pl.pallas_call(
        body, name="grad_chip_exchange", in_specs=[ANY], out_specs=ANY,
        out_shape=jax.ShapeDtypeStruct((3, hr, c), p16.dtype),
        scratch_shapes=[pltpu.SemaphoreType.DMA((3,)), pltpu.SemaphoreType.DMA((3,))],
    )(p16)


def _chip_add(p32, recv, chip_idx, *, tm=384):
    n, hr, c = p32.shape

    def body(ci_ref, p_ref, r_ref, o_ref):
        o_ref[...] = ((p_ref[0] + r_ref[0].astype(F32)) + r_ref[1].astype(F32)) + r_ref[2].astype(F32)

    gs = pltpu.PrefetchScalarGridSpec(
        num_scalar_prefetch=1, grid=(hr // tm,),
        in_specs=[pl.BlockSpec((1, tm, c), lambda j, ci: (ci[0], j, 0)), pl.BlockSpec((3, tm, c), lambda j, ci: (0, j, 0))],
        out_specs=pl.BlockSpec((tm, c), lambda j, ci: (j, 0)))
    return pl.pallas_call(
        body, name="grad_chip_add", grid_spec=gs, out_shape=jax.ShapeDtypeStruct((hr, c), F32),
        compiler_params=_cp(("parallel",)),
    )(chip_idx, p32, recv)


def _pair_gather(f):
    hr, c = f.shape

    def body(f_ref, out_ref, send_sem, recv_sem, local_sem):
        x, y, cc = lax.axis_index("x"), lax.axis_index("y"), lax.axis_index("c")
        mine = pltpu.make_async_copy(f_ref, out_ref.at[pl.ds(cc * hr, hr), :], local_sem)
        mine.start()
        cp = pltpu.make_async_remote_copy(
            src_ref=f_ref, dst_ref=out_ref.at[pl.ds(cc * hr, hr), :], send_sem=send_sem, recv_sem=recv_sem,
            device_id=(x, y, 1 - cc), device_id_type=MESH)
        cp.start()
        cp.wait()
        mine.wait()

    return pl.pallas_call(
        body, name="grad_pair_gather", in_specs=[ANY], out_specs=ANY,
        out_shape=jax.ShapeDtypeStruct((2 * hr, c), f.dtype),
        scratch_shapes=[pltpu.SemaphoreType.DMA, pltpu.SemaphoreType.DMA, pltpu.SemaphoreType.DMA],
    )(f)


def _all_reduce_small(buf):
    r, c = buf.shape

    def body(b_ref, out_ref, gat, send_sems, recv_sems):
        x, y, cc = lax.axis_index("x"), lax.axis_index("y"), lax.axis_index("c")
        me = 4 * x + 2 * y + cc
        gat[me] = b_ref[...]
        cps = []
        for k in range(1, 8):
            px, py, pc = x ^ (k >> 2), y ^ ((k >> 1) & 1), cc ^ (k & 1)
            cps.append(pltpu.make_async_remote_copy(
                src_ref=b_ref, dst_ref=gat.at[me], send_sem=send_sems.at[k - 1], recv_sem=recv_sems.at[k - 1],
                device_id=(px, py, pc), device_id_type=MESH))
        for cp in cps:
            cp.start()
        for cp in cps:
            cp.wait()
        acc = gat[0]
        for d in range(1, 8):
            acc = acc + gat[d]
        out_ref[...] = acc

    vm = pl.BlockSpec(memory_space=pltpu.VMEM)
    return pl.pallas_call(
        body, name="all_reduce_small", in_specs=[vm], out_specs=vm, out_shape=jax.ShapeDtypeStruct((r, c), F32),
        scratch_shapes=[pltpu.VMEM((8, r, c), F32), pltpu.SemaphoreType.DMA((7,)), pltpu.SemaphoreType.DMA((7,))],
        compiler_params=pltpu.CompilerParams(vmem_limit_bytes=VMEM_LIMIT),
    )(buf)


def _pipe(fn, ins, outs, tr, depth=4, slots=None):
    shape = ins[0].shape
    lead, (r, c) = shape[:-2], shape[-2:]
    assert len(lead) <= 1 and r % tr == 0
    nr = r // tr
    which = list(range(lead[0])) if lead and slots is None else slots
    n = nr * (len(which) if lead else 1)
    ni, no = len(ins), len(outs)

    def blk(ref, step):
        rows = pl.ds((step % nr) * tr, tr)
        return ref.at[which[step // nr], rows, :] if lead else ref.at[rows, :]

    def scoped(*bufs):
        ibufs, obufs, isem, osem = bufs[:ni], bufs[ni:ni + no], bufs[-2], bufs[-1]

        def in_copy(q, step, slot):
            return pltpu.make_async_copy(blk(ins[q], step), ibufs[q].at[slot], isem.at[q, slot])

        def out_copy(q, step, slot):
            return pltpu.make_async_copy(obufs[q].at[slot], blk(outs[q], step), osem.at[q, slot])

        for step in range(min(nbuf - 1, n)):
            for q in range(ni):
                in_copy(q, step, step % nbuf).start()
        for step in range(n):
            slot = step % nbuf
            if step + nbuf - 1 < n:
                for q in range(ni):
                    in_copy(q, step + nbuf - 1, (step + nbuf - 1) % nbuf).start()
            for q in range(ni):
                in_copy(q, step, slot).wait()
            if step >= nbuf:
                for q in range(no):
                    out_copy(q, step - nbuf, slot).wait()
            res = fn(*[ibufs[q][slot] for q in range(ni)])
            for q in range(no):
                obufs[q][slot] = res[q].astype(obufs[q].dtype)
                out_copy(q, step, slot).start()
        for step in range(max(n - nbuf, 0), n):
            for q in range(no):
                out_copy(q, step, step % nbuf).wait()

    assert n <= 8
    nbuf = min(n, depth)
    pl.run_scoped(scoped, *[pltpu.VMEM((nbuf, tr, c), q.dtype) for q in ins], *[pltpu.VMEM((nbuf, tr, c), q.dtype) for q in outs],
                  pltpu.SemaphoreType.DMA((ni, nbuf)), pltpu.SemaphoreType.DMA((no, nbuf)))


W_IN_PAD = 2304
BIG = ("w_in", "w_attn_o", "w_ssd_o", "w_out", "w_up", "w_down")
BIG_SHAPE = dict(w_in=(D_MODEL, W_IN_PAD), w_attn_o=(Q_DIM // 4, D_MODEL), w_ssd_o=(D_INNER // 4, D_MODEL),
                 w_out=(D_MODEL // 4, D_MODEL), w_up=(D_MODEL, 2 * D_FF // 4), w_down=(D_FF // 4, D_MODEL))
BIG_TR = dict(w_in=128, w_attn_o=128, w_ssd_o=128, w_out=128, w_up=128, w_down=176)
X_FIRST = dict(w_in=True, w_attn_o=True, w_ssd_o=False, w_out=True, w_up=False, w_down=False)


def _neighbours(x, y, x_first):
    xn, yn = (1 - x, y), (x, 1 - y)
    n1, n2 = (xn, yn) if x_first else (yn, xn)
    slot = lambda ch: 2 * ch[0] + ch[1]
    return n1, n2, slot(n1), slot(n2), slot((1 - x, 1 - y))


def _gather_big(shards):
    nt = len(BIG)

    def body(*refs):
        sh, out = refs[:nt], refs[nt:2 * nt]
        send_sems, recv_sems = refs[2 * nt:]
        x, y, cc = lax.axis_index("x"), lax.axis_index("y"), lax.axis_index("c")
        me = 2 * x + y
        sib = (x, y, 1 - cc)
        for t, n in enumerate(BIG):
            _pipe(lambda v: (v,), [sh[t]], [out[t].at[me]], BIG_TR[n])

        def copy(t, k, slot, pc, to):
            hr = BIG_SHAPE[BIG[t]][0] // 2
            ref = out[t].at[slot, pl.ds(pc * hr, hr), :]
            return pltpu.make_async_remote_copy(src_ref=ref, dst_ref=ref, send_sem=send_sems.at[6 * t + k],
                                                recv_sem=recv_sems.at[6 * t + k], device_id=to, device_id_type=MESH)

        started = []

        def start(cp):
            cp.start()
            started.append(cp)

        geo = [_neighbours(x, y, X_FIRST[n]) for n in BIG]
        for t in range(nt):
            n1, n2, _, _, _ = geo[t]
            start(copy(t, 0, me, cc, (*n1, cc)))
            start(copy(t, 1, me, cc, (*n2, cc)))
        for t in range(nt):
            n1, n2, s1, s2, sd = geo[t]
            copy(t, 0, s1, cc, sib).wait_recv()
            start(copy(t, 2, s1, cc, (*n2, cc)))
            start(copy(t, 3, s1, cc, sib))
            copy(t, 1, s2, cc, sib).wait_recv()
            start(copy(t, 4, s2, cc, sib))
        for t in range(nt):
            _, _, s1, s2, sd = geo[t]
            copy(t, 2, sd, cc, sib).wait_recv()
            start(copy(t, 5, sd, cc, sib))
        for t in range(nt):
            _, _, s1, s2, sd = geo[t]
            copy(t, 3, s1, 1 - cc, sib).wait_recv()
            copy(t, 4, s2, 1 - cc, sib).wait_recv()
            copy(t, 5, sd, 1 - cc, sib).wait_recv()
        for cp in started:
            cp.wait_send()

    return pl.pallas_call(
        body, name="gather_big", in_specs=[ANY] * nt, out_specs=[ANY] * nt,
        out_shape=[jax.ShapeDtypeStruct((N_CHIPS, *BIG_SHAPE[n]), BF16) for n in BIG],
        scratch_shapes=[pltpu.SemaphoreType.DMA((6 * nt,)), pltpu.SemaphoreType.DMA((6 * nt,))],
        compiler_params=pltpu.CompilerParams(vmem_limit_bytes=VMEM_LIMIT),
    )(*shards)


def _reduce_big(grads):
    nt = len(BIG)
    nw = 7

    def body(*refs):
        g = refs[:nt]
        fin = refs[nt:2 * nt]
        work = refs[2 * nt:2 * nt + nw * nt]
        send_sems, recv_sems = refs[2 * nt + nw * nt:]
        x, y, cc = lax.axis_index("x"), lax.axis_index("y"), lax.axis_index("c")
        me = 2 * x + y
        sib = (x, y, 1 - cc)
        started = []

        def rcopy(t, k, src, dst, to):
            cp = pltpu.make_async_remote_copy(src_ref=src, dst_ref=dst, send_sem=send_sems.at[5 * t + k],
                                              recv_sem=recv_sems.at[5 * t + k], device_id=to, device_id_type=MESH)
            return cp

        def start(cp):
            cp.start()
            started.append(cp)

        geo = [_neighbours(x, y, X_FIRST[n]) for n in BIG]
        hrs = [BIG_SHAPE[n][0] // 2 for n in BIG]
        wk = lambda t: work[nw * t:nw * (t + 1)]
        one = lambda ref, slot: ref.at[pl.ds(slot, 1)]
        for t in range(nt):
            recv_a = wk(t)[0]
            start(rcopy(t, 0, g[t].at[:, pl.ds((1 - cc) * hrs[t], hrs[t]), :], recv_a, sib))
        for t, n in enumerate(BIG):
            recv_a, p32, p16, r1, qme, qs2, r2 = wk(t)
            n1, n2, s1, s2, sd = geo[t]
            rcopy(t, 0, recv_a, recv_a, sib).wait_recv()
            _pipe(lambda a, b: (a + b, a + b), [g[t].at[:, pl.ds(cc * hrs[t], hrs[t]), :], recv_a], [p32, p16], BIG_TR[n])
            start(rcopy(t, 1, one(p16, s1), one(r1, 0), (*n1, cc)))
            start(rcopy(t, 2, one(p16, sd), one(r1, 1), (*n1, cc)))
        for t, n in enumerate(BIG):
            recv_a, p32, p16, r1, qme, qs2, r2 = wk(t)
            n1, n2, s1, s2, sd = geo[t]
            rcopy(t, 1, one(r1, 0), one(r1, 0), sib).wait_recv()
            rcopy(t, 2, one(r1, 1), one(r1, 1), sib).wait_recv()
            _pipe(lambda a, b: (a + b.astype(F32),), [one(p32, s2), one(r1, 1)], [qs2], BIG_TR[n])
            start(rcopy(t, 3, qs2, r2, (*n2, cc)))
            _pipe(lambda a, b: (a + b.astype(F32),), [one(p32, me), one(r1, 0)], [qme], BIG_TR[n])
        for t, n in enumerate(BIG):
            recv_a, p32, p16, r1, qme, qs2, r2 = wk(t)
            rcopy(t, 3, r2, r2, sib).wait_recv()
            mine = fin[t].at[pl.ds(cc * hrs[t], hrs[t]), :]
            _pipe(lambda a, b: (a + b.astype(F32),), [qme.at[0], r2.at[0]], [mine], BIG_TR[n])
            start(rcopy(t, 4, mine, mine, sib))
        for t in range(nt):
            other = fin[t].at[pl.ds((1 - cc) * hrs[t], hrs[t]), :]
            rcopy(t, 4, other, other, sib).wait_recv()
        for cp in started:
            cp.wait_send()

    outs = [jax.ShapeDtypeStruct(BIG_SHAPE[n], F32) for n in BIG]
    for n in BIG:
        r, c = BIG_SHAPE[n]
        hr = r // 2
        outs += [jax.ShapeDtypeStruct((4, hr, c), F32), jax.ShapeDtypeStruct((4, hr, c), F32),
                 jax.ShapeDtypeStruct((4, hr, c), BF16), jax.ShapeDtypeStruct((2, hr, c), BF16),
                 jax.ShapeDtypeStruct((1, hr, c), F32), jax.ShapeDtypeStruct((1, hr, c), BF16),
                 jax.ShapeDtypeStruct((1, hr, c), BF16)]
    res = pl.pallas_call(
        body, name="reduce_big", in_specs=[ANY] * nt, out_specs=[ANY] * len(outs), out_shape=outs,
        scratch_shapes=[pltpu.SemaphoreType.DMA((5 * nt,)), pltpu.SemaphoreType.DMA((5 * nt,))],
        compiler_params=pltpu.CompilerParams(vmem_limit_bytes=VMEM_LIMIT),
    )(*grads)
    return res[:nt]


WHOLE_X_FIRST = dict(w_ssd_o=True, w_out=False, w_attn_o=False)


def _quarters(names):
    out = []
    for i, n in enumerate(names):
        if n in WHOLE_X_FIRST:
            h = BIG_SHAPE[n][0] // 2
            out.append((i, WHOLE_X_FIRST[n], 0, h, 128))
        else:
            q = BIG_SHAPE[n][0] // 4
            tr = 128 if q % 128 == 0 else q
            out += [(i, True, 0, q, tr), (i, False, q, q, tr)]
    return out


class _GatherJob:
    def __init__(self, names, shards, at=None):
        self.names = names
        self.at = at
        self.inputs = list(shards)
        self.out_shapes = [jax.ShapeDtypeStruct((N_CHIPS, *BIG_SHAPE[n]), BF16) for n in names]
        self.ent = _quarters(names)
        self.scratch = [pltpu.SemaphoreType.DMA((6 * len(self.ent),)), pltpu.SemaphoreType.DMA((6 * len(self.ent),))]

    def phases(self, sh, out, scr):
        send_sems, recv_sems = scr
        names, ent = self.names, self.ent
        x, y, cc = lax.axis_index("x"), lax.axis_index("y"), lax.axis_index("c")
        me = 2 * x + y
        sib = (x, y, 1 - cc)
        geo = [_neighbours(x, y, e[1]) for e in ent]
        started = []

        def copy(i, k, slot, pc, to):
            arr, _, roff, rows, _ = ent[i]
            hr = BIG_SHAPE[names[arr]][0] // 2
            ref = out[arr].at[slot, pl.ds(pc * hr + roff, rows), :]
            return pltpu.make_async_remote_copy(src_ref=ref, dst_ref=ref, send_sem=send_sems.at[6 * i + k],
                                                recv_sem=recv_sems.at[6 * i + k], device_id=to, device_id_type=MESH)

        def start(*a):
            copy(*a).start()
            started.append(a)

        def p0():
            for t, n in enumerate(names):
                _pipe(lambda v: (v,), [sh[t]], [out[t].at[me]], BIG_TR[n])
            for i in range(len(ent)):
                n1, n2, _, _, _ = geo[i]
                start(i, 0, me, cc, (*n1, cc))
                start(i, 1, me, cc, (*n2, cc))

        def p1():
            for i in range(len(ent)):
                n1, n2, s1, s2, sd = geo[i]
                copy(i, 0, s1, cc, sib).wait_recv()
                start(i, 2, s1, cc, (*n2, cc))
                start(i, 3, s1, cc, sib)
                copy(i, 1, s2, cc, sib).wait_recv()
                start(i, 4, s2, cc, sib)

        def p2():
            for i in range(len(ent)):
                sd = geo[i][4]
                copy(i, 2, sd, cc, sib).wait_recv()
                start(i, 5, sd, cc, sib)

        def p3():
            for i in range(len(ent)):
                _, _, s1, s2, sd = geo[i]
                copy(i, 3, s1, 1 - cc, sib).wait_recv()
                copy(i, 4, s2, 1 - cc, sib).wait_recv()
                copy(i, 5, sd, 1 - cc, sib).wait_recv()
            for a in started:
                copy(*a).wait_send()

        return [p0, p1, p2, p3]


class _ReduceJob:
    NW = 7

    def __init__(self, names, grads, at=None):
        self.names = names
        self.at = at
        self.inputs = list(grads)
        self.ent = _quarters(names)
        self.out_shapes = [jax.ShapeDtypeStruct(BIG_SHAPE[n], F32) for n in names]
        for arr, _, _, rows, _ in self.ent:
            c = BIG_SHAPE[names[arr]][1]
            self.out_shapes += [jax.ShapeDtypeStruct((4, rows, c), F32), jax.ShapeDtypeStruct((4, rows, c), F32),
                                jax.ShapeDtypeStruct((4, rows, c), BF16), jax.ShapeDtypeStruct((2, rows, c), BF16),
                                jax.ShapeDtypeStruct((1, rows, c), F32), jax.ShapeDtypeStruct((1, rows, c), BF16),
                                jax.ShapeDtypeStruct((1, rows, c), BF16)]
        self.scratch = [pltpu.SemaphoreType.DMA((8 * len(self.ent),)), pltpu.SemaphoreType.DMA((8 * len(self.ent),))]

    def phases(self, g, outs, scr):
        send_sems, recv_sems = scr
        names, ent, nw = self.names, self.ent, self.NW
        nt = len(names)
        fin, work = outs[:nt], outs[nt:]
        x, y, cc = lax.axis_index("x"), lax.axis_index("y"), lax.axis_index("c")
        me = 2 * x + y
        sib = (x, y, 1 - cc)
        geo = [_neighbours(x, y, e[1]) for e in ent]
        started = []
        wk = lambda i: work[nw * i:nw * (i + 1)]
        one = lambda ref, slot: ref.at[pl.ds(slot, 1)]

        def rows_of(i, pc):
            arr, _, roff, rows, _ = ent[i]
            return pl.ds(pc * (BIG_SHAPE[names[arr]][0] // 2) + roff, rows)

        def rcopy(i, k, src, dst, to):
            return pltpu.make_async_remote_copy(src_ref=src, dst_ref=dst, send_sem=send_sems.at[8 * i + k],
                                                recv_sem=recv_sems.at[8 * i + k], device_id=to, device_id_type=MESH)

        def start(make):
            make().start()
            started.append(make)

        def pair(i, q, slot, pc):
            return rcopy(i, q, g[ent[i][0]].at[pl.ds(slot, 1), rows_of(i, pc), :], one(wk(i)[0], slot), sib)

        def p0():
            for i in range(len(ent)):
                _, _, s1, s2, sd = geo[i]
                for q, slot in enumerate((s1, sd, s2, me)):
                    start(lambda i=i, q=q, slot=slot: pair(i, q, slot, 1 - cc))

        def p1():
            for i, e in enumerate(ent):
                recv_a, _, p16, _ = wk(i)[:4]
                n1, n2, s1, s2, sd = geo[i]
                pair(i, 0, s1, cc).wait_recv()
                pair(i, 1, sd, cc).wait_recv()
                _pipe(lambda a, b: (a + b,), [g[e[0]].at[:, rows_of(i, cc), :], recv_a], [p16], e[4], slots=(s1, sd))
                start(lambda i=i, s1=s1, n1=n1: rcopy(i, 4, one(wk(i)[2], s1), one(wk(i)[3], 0), (*n1, cc)))
                start(lambda i=i, sd=sd, n1=n1: rcopy(i, 5, one(wk(i)[2], sd), one(wk(i)[3], 1), (*n1, cc)))
            for i, e in enumerate(ent):
                recv_a, p32 = wk(i)[:2]
                _, _, s1, s2, sd = geo[i]
                pair(i, 2, s2, cc).wait_recv()
                pair(i, 3, me, cc).wait_recv()
                _pipe(lambda a, b: (a + b,), [g[e[0]].at[:, rows_of(i, cc), :], recv_a], [p32], e[4], slots=(s2, me))

        def p2():
            for i, e in enumerate(ent):
                _, p32, _, r1, qme, qs2, r2 = wk(i)
                n1, n2, s1, s2, sd = geo[i]
                rcopy(i, 4, one(r1, 0), one(r1, 0), sib).wait_recv()
                rcopy(i, 5, one(r1, 1), one(r1, 1), sib).wait_recv()
                _pipe(lambda a, b, c, d: (a + b.astype(F32), c + d.astype(F32)),
                      [one(p32, s2), one(r1, 1), one(p32, me), one(r1, 0)], [qs2, qme], e[4])
                start(lambda i=i, n2=n2: rcopy(i, 6, wk(i)[5], wk(i)[6], (*n2, cc)))

        def p3():
            for i, e in enumerate(ent):
                qme, r2 = wk(i)[4], wk(i)[6]
                rcopy(i, 6, r2, r2, sib).wait_recv()
                mine = fin[e[0]].at[rows_of(i, cc), :]
                _pipe(lambda a, b: (a + b.astype(F32),), [qme.at[0], r2.at[0]], [mine], e[4])
                start(lambda i=i, e=e: rcopy(i, 7, fin[e[0]].at[rows_of(i, cc), :], fin[e[0]].at[rows_of(i, cc), :], sib))

        def p4():
            for i, e in enumerate(ent):
                other = fin[e[0]].at[rows_of(i, 1 - cc), :]
                rcopy(i, 7, other, other, sib).wait_recv()
            for make in started:
                make().wait_send()

        return [p0, p1, p2, p3, p4]


class _AdamJob:
    def __init__(self, names, ws, gs, ms, vs, groups):
        self.names, self.groups = names, groups
        self.inputs = [a for quad in zip(ws, gs, ms, vs) for a in quad]
        self.out_shapes = [jax.ShapeDtypeStruct(w.shape, F32) for w in ws for _ in range(4)]

    def work(self, ins, outs):
        def one(t):
            w, g, m, v = ins[4 * t:4 * t + 4]
            r = w.shape[1]
            tr = 128 if r % 128 == 0 else r // 4
            _pipe(lambda a, b, c, d: (*_adamw_math(a, b, c, d), b), [w.at[0], g, m.at[0], v.at[0]],
                  [o.at[0] for o in outs[4 * t:4 * t + 4]], tr, depth=2)

        def group(grp):
            def run():
                for n in grp:
                    one(self.names.index(n))
            return run

        return [group(grp) for grp in self.groups]


class _Interleaved:
    def __init__(self, job, work, at):
        self.job, self.wk, self.at = job, work, at
        self.inputs = job.inputs + work.inputs
        self.out_shapes = list(job.out_shapes) + list(work.out_shapes)
        self.scratch = job.scratch

    def phases(self, ins, outs, scr):
        nj, no = len(self.job.inputs), len(self.job.out_shapes)
        base = self.job.phases(ins[:nj], outs[:no], scr)
        work = self.wk.work(ins[nj:], outs[no:])
        mixed = []
        for k, ph in enumerate(base):
            mixed.append(ph)
            if k < len(work):
                mixed.append(work[k])
        return mixed


def _run_job(job, name):
    ni, no = len(job.inputs), len(job.out_shapes)

    def body(*refs):
        for ph in job.phases(refs[:ni], refs[ni:ni + no], refs[ni + no:]):
            ph()

    return pl.pallas_call(
        body, name=name, in_specs=[ANY] * ni, out_specs=[ANY] * no, out_shape=job.out_shapes, scratch_shapes=job.scratch,
        compiler_params=pltpu.CompilerParams(vmem_limit_bytes=VMEM_LIMIT),
    )(*job.inputs)


def _hosted(body, *, name, grid, in_specs, out_specs, out_shape, scratch_shapes, args, sem, side=None):
    if side is None:
        return pl.pallas_call(body, name=name, grid=grid, in_specs=in_specs, out_specs=out_specs, out_shape=out_shape,
                              scratch_shapes=scratch_shapes, compiler_params=_cp(sem))(*args), None
    job = side
    ni, no, ns = len(in_specs), len(out_specs), len(scratch_shapes)
    ji, jo = len(job.inputs), len(job.out_shapes)
    n_steps = 1
    for extent in grid:
        n_steps *= extent

    def wrapped(*refs):
        own_in, refs = refs[:ni], refs[ni:]
        job_in, refs = refs[:ji], refs[ji:]
        own_out, refs = refs[:no], refs[no:]
        job_out, refs = refs[:jo], refs[jo:]
        own_scr, job_scr = refs[:ns], refs[ns:]
        step = 0
        for d, extent in enumerate(grid):
            step = step * extent + pl.program_id(d)
        phases = job.phases(job_in, job_out, job_scr)
        steps = [min(int(f * n_steps), n_steps - 1) for f in job.at] + [n_steps - 1]
        assert len(steps) == len(phases) and steps == sorted(steps)
        for at, ph in zip(steps, phases):
            pl.when(step == at)(ph)
        body(*own_in, *own_out, *own_scr)

    res = pl.pallas_call(
        wrapped, name=name, grid=grid, in_specs=list(in_specs) + [ANY] * ji, out_specs=list(out_specs) + [ANY] * jo,
        out_shape=list(out_shape) + list(job.out_shapes), scratch_shapes=list(scratch_shapes) + list(job.scratch),
        compiler_params=_cp(("arbitrary",) * len(grid)),
    )(*args, *job.inputs)
    return res[:no], res[no:]


def _proj_dw(xnt, dproj_sh, *, tm=512, tk=2048):
    d, s = xnt.shape
    tk = _tile(s, tk)
    nk = s // tk

    def body(a_ref, b_ref, o_ref, acc):
        def finish(r):
            o_ref[0] = r

        _accumulate(acc, _dot(a_ref[...], b_ref[...]), pl.program_id(2), nk, finish)

    return pl.pallas_call(
        body, name="proj_dw", grid=(N_CHIPS, d // tm, nk),
        in_specs=[pl.BlockSpec((tm, tk), lambda j, i, q: (i, q)), pl.BlockSpec((tk, W_IN_PAD), lambda j, i, q: (q, j))],
        out_specs=pl.BlockSpec((1, tm, W_IN_PAD), lambda j, i, q: (j, i, 0)),
        out_shape=jax.ShapeDtypeStruct((N_CHIPS, d, W_IN_PAD), F32), scratch_shapes=[pltpu.VMEM((tm, W_IN_PAD), F32)],
        compiler_params=_cp(("parallel", "parallel", "arbitrary")),
    )(xnt, dproj_sh)


def _proj_dx(dproj_sh, w_sh, *, tm=1024, side=None):
    s = dproj_sh.shape[0]
    d = w_sh.shape[1]
    tm = _tile(s, tm)

    def body(a_ref, b_ref, o_ref, acc):
        kk = pl.program_id(1)
        part = _dot_nt(a_ref[...], b_ref[0])

        @pl.when(kk == 0)
        def _():
            acc[...] = part

        @pl.when(kk > 0)
        def _():
            acc[...] += part

        @pl.when(kk == N_CHIPS - 1)
        def _():
            o_ref[...] = acc[...]

    own, extra = _hosted(
        body, name="proj_dx", grid=(s // tm, N_CHIPS),
        in_specs=[pl.BlockSpec((tm, W_IN_PAD), lambda i, q: (i, q)), pl.BlockSpec((1, d, W_IN_PAD), lambda i, q: (q, 0, 0))],
        out_specs=[pl.BlockSpec((tm, d), lambda i, q: (i, 0))],
        out_shape=[jax.ShapeDtypeStruct((s, d), F32)], scratch_shapes=[pltpu.VMEM((tm, d), F32)],
        args=(dproj_sh, w_sh), sem=("parallel", "arbitrary"), side=side)
    return own[0] if side is None else (own[0], extra)


def _up_dx(dup, w_sh, *, tm=1024):
    s = dup.shape[1]
    d, wsh = w_sh.shape[1:]
    tm = _tile(s, tm)

    def body(a_ref, b_ref, o_ref, acc):
        kk = pl.program_id(1)
        part = _dot_nt(a_ref[0], b_ref[0])

        @pl.when(kk == 0)
        def _():
            acc[...] = part

        @pl.when(kk > 0)
        def _():
            acc[...] += part

        @pl.when(kk == N_CHIPS - 1)
        def _():
            o_ref[...] = acc[...]

    return pl.pallas_call(
        body, name="up_dx", grid=(s // tm, N_CHIPS),
        in_specs=[pl.BlockSpec((1, tm, wsh), lambda i, q: (q >> 1, i, q & 1)), pl.BlockSpec((1, d, wsh), lambda i, q: (q, 0, 0))],
        out_specs=pl.BlockSpec((tm, d), lambda i, q: (i, 0)),
        out_shape=jax.ShapeDtypeStruct((s, d), F32), scratch_shapes=[pltpu.VMEM((tm, d), F32)],
        compiler_params=_cp(("parallel", "arbitrary")),
    )(dup, w_sh)


def _up_dw(hnt, dup, *, tk=2048):
    d, s = hnt.shape
    wsh = 2 * D_FF // N_CHIPS
    tk = _tile(s, tk)
    nk = s // tk

    def body(a_ref, b_ref, o_ref, acc):
        def finish(r):
            o_ref[0] = r

        _accumulate(acc, _dot(a_ref[...], b_ref[0]), pl.program_id(1), nk, finish)

    return pl.pallas_call(
        body, name="up_dw", grid=(N_CHIPS, nk),
        in_specs=[pl.BlockSpec((d, tk), lambda j, q: (0, q)), pl.BlockSpec((1, tk, wsh), lambda j, q: (j >> 1, q, j & 1))],
        out_specs=pl.BlockSpec((1, d, wsh), lambda j, q: (j, 0, 0)),
        out_shape=jax.ShapeDtypeStruct((N_CHIPS, d, wsh), F32), scratch_shapes=[pltpu.VMEM((d, wsh), F32)],
        compiler_params=_cp(("parallel", "arbitrary")),
    )(hnt, dup)


BIG_ROWS =(IN_DIM // 4, Q_DIM // 4, D_INNER // 4, D_MODEL // 4, 2 * D_FF // 4, D_FF // 4)
PACK_ROWS = 5376


def _pack_shards(parts):
    rows = [p.reshape(-1, D_MODEL) for p in parts]
    pad = PACK_ROWS - sum(BIG_ROWS)
    return jnp.concatenate(rows + [jnp.zeros((pad, D_MODEL), rows[0].dtype)], axis=0)


def _unpack_shards(buf):
    out, off = [], 0
    for n in BIG_ROWS:
        out.append(buf[off:off + n])
        off += n
    return out


def _assemble(srcs, col_map, *, name, tr=256):
    arrays, lead = [], []
    for src in srcs:
        arr, j = src if isinstance(src, tuple) else (src, None)
        if not any(arr is a for a in arrays):
            arrays.append(arr)
        lead.append(([i for i, a in enumerate(arrays) if a is arr][0], j))
    rows = arrays[0].shape[-2]
    tr = _tile(rows, tr)
    out_w = len(col_map)
    tiles = []
    for t in range(out_w // 128):
        runs = []
        for lane in range(128):
            ent = col_map[t * 128 + lane]
            key = None if ent is None else (ent[0], ent[1] // 128, (lane - ent[1]) % 128)
            if runs and runs[-1][0] == key:
                runs[-1][2] = lane + 1
            else:
                runs.append([key, lane, lane + 1])
        tiles.append(runs)

    def body(*refs):
        o_ref = refs[-1]
        lane = lax.broadcasted_iota(jnp.int32, (tr, 128), 1)
        for t, runs in enumerate(tiles):
            acc = jnp.zeros((tr, 128), F32)
            for key, a, b in runs:
                if key is None:
                    continue
                sid, ct, shift = key
                ai, j = lead[sid]
                cols = slice(ct * 128, (ct + 1) * 128)
                piece = (refs[ai][:, cols] if j is None else refs[ai][j, :, cols]).astype(F32)
                if shift:
                    piece = pltpu.roll(piece, shift, 1)
                acc = piece if (a, b) == (0, 128) else jnp.where((lane >= a) & (lane < b), piece, acc)
            o_ref[:, t * 128:(t + 1) * 128] = acc.astype(BF16)

    specs = [pl.BlockSpec((tr, a.shape[1]), lambda i: (i, 0)) if a.ndim == 2
             else pl.BlockSpec((a.shape[0], tr, a.shape[2]), lambda i: (0, i, 0)) for a in arrays]
    return pl.pallas_call(
        body, name=name, grid=(rows // tr,), in_specs=specs, out_specs=pl.BlockSpec((tr, out_w), lambda i: (i, 0)),
        out_shape=jax.ShapeDtypeStruct((rows, out_w), BF16), compiler_params=_cp(("parallel",)),
    )(*arrays)


def _permute_cols_in(w):
    pad = jnp.zeros((w.shape[0], PW - IN_DIM), w.dtype)
    return jnp.concatenate([w[:, :6656], w[:, 6688:], w[:, 6656:6688], pad], axis=1)


def _unpermute_cols_in(g):
    return jnp.concatenate([g[:, :6656], g[:, O_DT:O_DT + 32], g[:, 6656:O_DT]], axis=1)


SMALL = ("norm1_w", "b_gate", "attn_sinks", "ssd_conv_b", "dt_bias", "a_log", "d_skip", "ssd_norm_w", "norm2_w",
         "ffn_conv_b", "final_norm_w", "ssd_conv_w", "ffn_conv_w")


def _pad128(v):
    v = v.reshape(-1)
    return jnp.pad(v, (0, (-v.shape[0]) % 128))


def _pack_small(parts):
    flat = jnp.concatenate([_pad128(p) for p in parts])
    flat = jnp.pad(flat, (0, (-flat.shape[0]) % 1024))
    return flat.reshape(-1, 128)


def _unpack_small(buf, shapes):
    flat, out, off = buf.reshape(-1), [], 0
    for shp in shapes:
        n = 1
        for q in shp:
            n *= q
        out.append(flat[off:off + n].reshape(shp))
        off += n + (-n) % 128
    return out


def _vec128(v):
    return jnp.pad(v.reshape(1, -1), ((0, 0), (0, 128 - v.shape[-1])))


def kernel(x, norm1_w, w_in, b_gate, attn_sinks, w_attn_o, ssd_conv_w, ssd_conv_b, dt_bias, a_log, d_skip, ssd_norm_w, w_ssd_o, w_out, norm2_w, w_up, ffn_conv_w, ffn_conv_b, w_down, final_norm_w, loss_target, m_norm1_w, m_w_in, m_b_gate, m_attn_sinks, m_w_attn_o, m_ssd_conv_w, m_ssd_conv_b, m_dt_bias, m_a_log, m_d_skip, m_ssd_norm_w, m_w_ssd_o, m_w_out, m_norm2_w, m_w_up, m_ffn_conv_w, m_ffn_conv_b, m_w_down, m_final_norm_w, v_norm1_w, v_w_in, v_b_gate, v_attn_sinks, v_w_attn_o, v_ssd_conv_w, v_ssd_conv_b, v_dt_bias, v_a_log, v_d_skip, v_ssd_norm_w, v_w_ssd_o, v_w_out, v_norm2_w, v_w_up, v_ffn_conv_w, v_ffn_conv_b, v_w_down, v_final_norm_w):
    ix, iy, ic = lax.axis_index("x"), lax.axis_index("y"), lax.axis_index("c")
    chip = 2 * ix + iy
    x2 = x[0]
    tgt = loss_target[0]
    s = x2.shape[0]

    wsh = IN_DIM // N_CHIPS
    big_shards = dict(w_in=jnp.pad(w_in[0], ((0, 0), (0, W_IN_PAD - wsh))), w_attn_o=w_attn_o[0], w_ssd_o=w_ssd_o[0],
                      w_out=w_out[0], w_up=w_up[0], w_down=w_down[0])
    gathered = {}
    (xn, xnt), (gathered["w_in"],) = _rms_fwd(x2, norm1_w, name="norm1_fwd", with_t=True,
                                              side=_GatherJob(("w_in",), [big_shards["w_in"]], at=(0.0, 0.5, 0.75)))
    early = ("w_attn_o", "w_ssd_o", "w_out")
    gather_early = _GatherJob(early, [big_shards[n] for n in early], at=(0.0, 0.5, 0.8))
    gather_up = _GatherJob(("w_up",), [big_shards["w_up"]], at=(0.0, 0.55, 0.85))
    gather_down = _GatherJob(("w_down",), [big_shards["w_down"]], at=(0.0, 0.5, 0.8))
    gw = gathered["w_in"]
    perm = list(range(O_GA)) + list(range(O_GA + N_SSD_HEADS, IN_DIM)) + list(range(O_GA, O_GA + N_SSD_HEADS))
    w_in_p = _assemble([(gw, j) for j in range(N_CHIPS)], [divmod(o, wsh) for o in perm] + [None] * (PW - IN_DIM),
                       name="w_in_assemble")
    small_sh = _pack_small([ssd_conv_w[0], ffn_conv_w[0]])
    small_all = _all_gather_small(small_sh)
    sc_parts = [_unpack_small(small_all[j], [(4, XBC_DIM // 4), (3, 2 * D_FF // 4)]) for j in range(N_CHIPS)]
    ssd_cw = jnp.concatenate([p[0] for p in sc_parts], axis=1)
    ffn_cw = jnp.concatenate([p[1] for p in sc_parts], axis=1)

    sinks128 = _vec128(attn_sinks)
    dtb128, alog128, dskip128 = _vec128(dt_bias), _vec128(a_log), _vec128(d_skip)

    proj, got = _mm(xn, w_in_p, name="proj_fwd", tn=1280, side=gather_early)
    gathered.update(zip(early, got))
    qkvt = _mm(w_in_p[:, :O_Z], xnt, name="qkv_fwd", ta=True)
    attn_pre, (gathered["w_up"],) = _attn_fwd(qkvt, sinks128, side=gather_up)
    xbc = _ssd_conv_fwd(proj, ssd_cw, ssd_conv_b)
    (y_ssd, hprev), (gathered["w_down"],) = _ssd_fwd(xbc, proj, dtb128, alog128, dskip128, side=gather_down)
    full = {n: gathered[n].reshape(-1, D_MODEL) for n in ("w_attn_o", "w_ssd_o", "w_out", "w_down")}
    full["w_up"] = gathered["w_up"]
    attn = _mm(attn_pre, full["w_attn_o"], name="attn_o_fwd", ta=True)
    yn = _gate_norm_fwd(y_ssd, proj, ssd_norm_w)
    ssd_out = _mm(yn, full["w_ssd_o"], name="ssd_o_fwd", tk=2048)
    merged = _merge_fwd(proj, b_gate, attn, ssd_out)
    h1 = _mm(merged, full["w_out"], name="out_fwd", resid=x2)
    hn, hnt = _rms_fwd(h1, norm2_w, name="norm2_fwd", with_t=True)
    up = _mm(hn, full["w_up"], name="up_fwd")
    act = _ffn_act_fwd(up, ffn_cw, ffn_conv_b)
    h2 = _mm(act, full["w_down"], name="down_fwd", resid=h1, tk=2816)

    dh2, loss_blk, g_final = _loss_bwd(h2, tgt, final_norm_w.reshape(1, -1))
    dact = _mm(dh2, full["w_down"], name="down_dx", tb=True, tn=1408)
    g_down = _mm(act, dh2, name="down_dw", ta=True, tm=1408, tk=2048)
    dup, g_ffn_cw, g_ffn_cb = _ffn_act_bwd(dact, up, ffn_cw, ffn_conv_b)
    dhn = _up_dx(dup, full["w_up"])
    g_up = _up_dw(hnt, dup)
    dh1, g_norm2 = _rms_bwd(dhn, h1, norm2_w, dh2, name="norm2_bwd")
    dmerged = _mm(dh1, full["w_out"], name="out_dx", tb=True)
    g_out = _mm(merged, dh1, name="out_dw", ta=True, tk=2048)
    dattn, dssd_out, dga, dgs, g_ba, g_bs = _merge_bwd(dmerged, proj, b_gate, attn, ssd_out)
    dyn = _mm(dssd_out, full["w_ssd_o"], name="ssd_o_dx", tb=True)
    g_ssd_o = _mm(yn, dssd_out, name="ssd_o_dw", ta=True, tk=2048)
    dy_ssd, dz, g_ssd_norm = _gate_norm_bwd(dyn, y_ssd, proj, ssd_norm_w)
    slot = lambda g: g.reshape(N_CHIPS, -1, D_MODEL)
    big_grads = {}
    red = ("w_down", "w_up")
    (dxbc, ddt, dvec), got = _ssd_bwd(xbc, proj, dtb128, alog128, dskip128, hprev, dy_ssd,
                                      side=_ReduceJob(red, [slot(g_down), g_up], at=(0.0, 0.2, 0.7, 0.95)))
    big_grads.update(zip(red, got))
    dxbc_raw, g_ssd_cw, g_ssd_cb = _ssd_conv_bwd(dxbc, proj, ssd_cw, ssd_conv_b)
    dattn_pre = _mm(full["w_attn_o"], dattn, name="attn_o_dx", tb=True)
    g_attn_o = _mm(attn_pre, dattn, name="attn_o_dw", tk=2048)
    red = ("w_out", "w_ssd_o", "w_attn_o")
    (dq, dk, dv, dsk), got = _attn_bwd(qkvt, sinks128, attn_pre, dattn_pre,
                                       side=_ReduceJob(red, [slot(g_out), slot(g_ssd_o), slot(g_attn_o)],
                                                       at=(0.0, 0.2, 0.5, 0.7)))
    big_grads.update(zip(red, got))
    pieces = [(dq.T, Q_DIM), (dk.T, KV_DIM), (dv.T, KV_DIM), (dz, D_INNER), (dxbc_raw, XBC_DIM), (ddt, N_SSD_HEADS),
              (dga, D_MODEL), (dgs, D_MODEL)]
    orig = [(i, c) for i, (_, w) in enumerate(pieces) for c in range(w)]
    dproj_sh = _assemble([p for p, _ in pieces],
                         [orig[j * wsh + c] if c < wsh else None for j in range(N_CHIPS) for c in range(W_IN_PAD)],
                         name="dproj_assemble")
    g_in = _proj_dw(xnt, dproj_sh)
    dxn, got = _proj_dx(dproj_sh, gathered["w_in"], side=_ReduceJob(("w_in",), [g_in], at=(0.0, 0.15, 0.75, 0.95)))
    big_grads["w_in"] = got[0]
    dx, g_norm1 = _rms_bwd(dxn, x2, norm1_w, dh1, name="norm1_bwd")


    small_g = dict(
        norm1_w=g_norm1, b_gate=jnp.concatenate([g_ba, g_bs], axis=1), attn_sinks=dsk[0:1, :16], ssd_conv_b=g_ssd_cb,
        dt_bias=dvec[0:1, :32], a_log=dvec[1:2, :32], d_skip=dvec[2:3, :32], ssd_norm_w=g_ssd_norm, norm2_w=g_norm2,
        ffn_conv_b=jnp.concatenate([g_ffn_cb[0], g_ffn_cb[1]], axis=1), final_norm_w=g_final, ssd_conv_w=g_ssd_cw,
        ffn_conv_w=jnp.concatenate([g_ffn_cw[0], g_ffn_cw[1]], axis=1))
    small_buf = _pack_small([small_g[n] for n in SMALL] + [loss_blk])
    small_sum = _all_reduce_small(small_buf)
    small_shapes = [(1, D_MODEL), (1, 2 * D_MODEL), (1, 16), (1, XBC_DIM), (1, 32), (1, 32), (1, 32), (1, D_INNER),
                    (1, D_MODEL), (1, 2 * D_FF), (D_MODEL,), (4, XBC_DIM), (3, 2 * D_FF), (1, 128)]
    small_list = _unpack_small(small_sum, small_shapes)
    loss = small_list[-1][0, 0]
    grads = dict(zip(SMALL, small_list[:-1]))
    grads["ssd_conv_w"] = lax.dynamic_slice_in_dim(grads["ssd_conv_w"], chip * (XBC_DIM // 4), XBC_DIM // 4, axis=1)
    grads["ffn_conv_w"] = lax.dynamic_slice_in_dim(grads["ffn_conv_w"], chip * (2 * D_FF // 4), 2 * D_FF // 4, axis=1)
    grads.update(big_grads)

    weights = dict(norm1_w=norm1_w, w_in=w_in, b_gate=b_gate, attn_sinks=attn_sinks, w_attn_o=w_attn_o, ssd_conv_w=ssd_conv_w,
                   ssd_conv_b=ssd_conv_b, dt_bias=dt_bias, a_log=a_log, d_skip=d_skip, ssd_norm_w=ssd_norm_w, w_ssd_o=w_ssd_o,
                   w_out=w_out, norm2_w=norm2_w, w_up=w_up, ffn_conv_w=ffn_conv_w, ffn_conv_b=ffn_conv_b, w_down=w_down,
                   final_norm_w=final_norm_w)
    ms = dict(norm1_w=m_norm1_w, w_in=m_w_in, b_gate=m_b_gate, attn_sinks=m_attn_sinks, w_attn_o=m_w_attn_o,
              ssd_conv_w=m_ssd_conv_w, ssd_conv_b=m_ssd_conv_b, dt_bias=m_dt_bias, a_log=m_a_log, d_skip=m_d_skip,
              ssd_norm_w=m_ssd_norm_w, w_ssd_o=m_w_ssd_o, w_out=m_w_out, norm2_w=m_norm2_w, w_up=m_w_up,
              ffn_conv_w=m_ffn_conv_w, ffn_conv_b=m_ffn_conv_b, w_down=m_w_down, final_norm_w=m_final_norm_w)
    vs = dict(norm1_w=v_norm1_w, w_in=v_w_in, b_gate=v_b_gate, attn_sinks=v_attn_sinks, w_attn_o=v_w_attn_o,
              ssd_conv_w=v_ssd_conv_w, ssd_conv_b=v_ssd_conv_b, dt_bias=v_dt_bias, a_log=v_a_log, d_skip=v_d_skip,
              ssd_norm_w=v_ssd_norm_w, w_ssd_o=v_w_ssd_o, w_out=v_w_out, norm2_w=v_norm2_w, w_up=v_w_up,
              ffn_conv_w=v_ffn_conv_w, ffn_conv_b=v_ffn_conv_b, w_down=v_w_down, final_norm_w=v_final_norm_w)
    order = list(weights)
    deltas, new_m, new_v = {}, {}, {}
    for n in BIG:
        shp = weights[n].shape
        res = _adamw(weights[n][0], grads[n], ms[n][0], vs[n][0], name="adamw_" + n)
        deltas[n], new_m[n], new_v[n], grads[n] = (a.reshape(shp) for a in res)
    smalls = [n for n in order if n not in BIG]
    as2d = lambda a: a.reshape(-1, a.shape[-1])
    res = _adamw_many(*[[as2d(src[n][0] if src[n].ndim == 3 else src[n]) for n in smalls] for src in (weights, grads, ms, vs)])
    for i, n in enumerate(smalls):
        deltas[n], new_m[n], new_v[n] = (res[q * len(smalls) + i].reshape(weights[n].shape) for q in range(3))
    out_grads = [grads[n].reshape(weights[n].shape) for n in order]
    return (loss, dx[None], *out_grads, *[deltas[n] for n in order], *[new_m[n] for n in order], *[new_v[n] for n in order])
```

```python
import functools

import jax
import jax.numpy as jnp
from jax import lax
from jax.experimental import pallas as pl
from jax.experimental.pallas import tpu as pltpu

F32 = jnp.float32
BF16 = jnp.bfloat16
HI = lax.Precision.HIGHEST

D_MODEL = 1024
Q_DIM = 1024
KV_DIM = 256
D_INNER = 2048
BC_DIM = 512
XBC_DIM = 3072
N_SSD_HEADS = 32
D_FF = 2816
IN_DIM = 8736
BLK = 128
EPS = 1e-5
NEG = -1e30

O_Q, O_K, O_V, O_Z, O_X, O_GA, O_GS, O_DT = 0, 1024, 1280, 1536, 3584, 6656, 7680, 8704
PW = 8960

ADAM_LR, ADAM_B1, ADAM_B2, ADAM_EPS, ADAM_WD, ADAM_STEP = 0.001, 0.9, 0.999, 1e-08, 0.01, 10

VMEM_LIMIT = 52 * 1024 * 1024
MESH = pl.DeviceIdType.MESH


def _cp(sem=None):
    return pltpu.CompilerParams(dimension_semantics=sem, vmem_limit_bytes=VMEM_LIMIT)


def _dot(a, b, prec=None):
    return jnp.dot(a, b, preferred_element_type=F32, precision=prec)


def _dot_nt(a, b, prec=None):
    return lax.dot_general(a, b, (((1,), (1,)), ((), ())), preferred_element_type=F32, precision=prec)


def _dot_tn(a, b, prec=None):
    return lax.dot_general(a, b, (((0,), (0,)), ((), ())), preferred_element_type=F32, precision=prec)


def _sigmoid(x):
    return 0.5 * jnp.tanh(0.5 * x) + 0.5


def _tile(n, want):
    t = min(n, want)
    while n % t:
        t -= 128
    return t


def _accumulate(acc, part, kk, nk, finish):
    if nk == 1:
        finish(part)
        return

    @pl.when(kk == 0)
    def _():
        acc[...] = part

    @pl.when(kk > 0)
    def _():
        acc[...] += part

    @pl.when(kk == nk - 1)
    def _():
        finish(acc[...])


def _mm(a, b, *, name, ta=False, tb=False, out_dtype=F32, resid=None, tm=1024, tn=1024, tk=1024, side=None):
    m, k = (a.shape[1], a.shape[0]) if ta else a.shape
    slots = b.ndim == 3
    if slots:
        n = b.shape[1] if tb else b.shape[0] * b.shape[2]
        tn, tk = (tn, b.shape[2]) if tb else (b.shape[2], tk)
    else:
        n = b.shape[0] if tb else b.shape[1]
    tm, tn, tk = _tile(m, tm), _tile(n, tn), _tile(k, tk)
    nk = k // tk
    dn = (((0 if ta else 1,), (1 if tb else 0,)), ((), ()))

    def body(*refs):
        if resid is None:
            a_ref, b_ref, o_ref, acc = refs
        else:
            a_ref, b_ref, r_ref, o_ref, acc = refs
        kk = pl.program_id(2)
        bv = b_ref[0] if slots else b_ref[...]
        part = lax.dot_general(a_ref[...].astype(BF16), bv.astype(BF16), dn, preferred_element_type=F32)

        def finish(r):
            if resid is not None:
                r = r + r_ref[...]
            o_ref[...] = r.astype(out_dtype)

        _accumulate(acc, part, kk, nk, finish)

    a_spec = pl.BlockSpec((tk, tm), lambda i, j, q: (q, i)) if ta else pl.BlockSpec((tm, tk), lambda i, j, q: (i, q))
    if slots:
        b_spec = (pl.BlockSpec((1, tn, tk), lambda i, j, q: (q, j, 0)) if tb
                  else pl.BlockSpec((1, tk, tn), lambda i, j, q: (j, q, 0)))
    else:
        b_spec = pl.BlockSpec((tn, tk), lambda i, j, q: (j, q)) if tb else pl.BlockSpec((tk, tn), lambda i, j, q: (q, j))
    o_spec = pl.BlockSpec((tm, tn), lambda i, j, q: (i, j))
    ins, specs = [a, b], [a_spec, b_spec]
    if resid is not None:
        ins.append(resid)
        specs.append(o_spec)
    own, extra = _hosted(
        body, name=name, grid=(m // tm, n // tn, nk), in_specs=specs, out_specs=[o_spec],
        out_shape=[jax.ShapeDtypeStruct((m, n), out_dtype)], scratch_shapes=[pltpu.VMEM((tm, tn), F32)],
        args=ins, sem=("parallel", "parallel", "arbitrary"), side=side)
    return own[0] if side is None else (own[0], extra)


def _rms_fwd(x, w, *, name, tm=512, with_t=False, side=None):
    s, d = x.shape
    tm = _tile(s, tm)

    def body(x_ref, w_ref, o_ref, *t_ref):
        xv = x_ref[...]
        r = lax.rsqrt(jnp.mean(xv * xv, axis=-1, keepdims=True) + EPS)
        y = (xv * r) * w_ref[...]
        o_ref[...] = y.astype(BF16)
        if with_t:
            t_ref[0][...] = y.T.astype(BF16)

    row = pl.BlockSpec((tm, d), lambda i: (i, 0))
    res, extra = _hosted(
        body, name=name, grid=(s // tm,), in_specs=[row, pl.BlockSpec((1, d), lambda i: (0, 0))],
        out_specs=[row] + [pl.BlockSpec((d, tm), lambda i: (0, i))] * with_t,
        out_shape=[jax.ShapeDtypeStruct((s, d), BF16)] + [jax.ShapeDtypeStruct((d, s), BF16)] * with_t,
        scratch_shapes=[], args=(x, w), sem=("parallel",), side=side)
    res = res if with_t else res[0]
    return res if side is None else (res, extra)


def _rms_bwd(dy, x, w, resid, *, name, tm=512):
    s, d = x.shape
    tm = _tile(s, tm)

    def body(dy_ref, x_ref, w_ref, r_ref, dx_ref, dw_ref, dx16_ref):
        i = pl.program_id(0)
        xv = x_ref[...]
        r = lax.rsqrt(jnp.mean(xv * xv, axis=-1, keepdims=True) + EPS)
        xh = xv * r
        dyv = dy_ref[...]
        g = dyv * w_ref[...]
        dxv = r_ref[...] + r * (g - xh * jnp.mean(g * xh, axis=-1, keepdims=True))
        dx_ref[...] = dxv
        dx16_ref[...] = dxv.astype(BF16)
        part = jnp.sum(dyv * xh, axis=0, keepdims=True)

        @pl.when(i == 0)
        def _():
            dw_ref[...] = part

        @pl.when(i > 0)
        def _():
            dw_ref[...] += part

    row = pl.BlockSpec((tm, d), lambda i: (i, 0))
    vec = pl.BlockSpec((1, d), lambda i: (0, 0))
    return pl.pallas_call(
        body, name=name, grid=(s // tm,), in_specs=[row, row, vec, row], out_specs=[row, vec, row],
        out_shape=[jax.ShapeDtypeStruct((s, d), F32), jax.ShapeDtypeStruct((1, d), F32), jax.ShapeDtypeStruct((s, d), BF16)],
        compiler_params=_cp(("arbitrary",)),
    )(dy, x, w, resid)


def _loss_bwd(h2, tgt, wf, *, tm=512):
    s, d = h2.shape
    tm = _tile(s, tm)

    def body(h_ref, t_ref, w_ref, dh_ref, loss_ref, dw_ref, dh16_ref):
        i = pl.program_id(0)
        hv = h_ref[...]
        r = lax.rsqrt(jnp.mean(hv * hv, axis=-1, keepdims=True) + EPS)
        xh = hv * r
        wv = w_ref[...]
        e = xh * wv - t_ref[...]
        lpart = 0.5 * jnp.sum(jnp.mean(e * e, axis=-1, keepdims=True), axis=0, keepdims=True)
        dout = e * (1.0 / d)
        g = dout * wv
        dhv = r * (g - xh * jnp.mean(g * xh, axis=-1, keepdims=True))
        dh_ref[...] = dhv
        dh16_ref[...] = dhv.astype(BF16)
        part = jnp.sum(dout * xh, axis=0, keepdims=True)
        lrow = jnp.broadcast_to(lpart, (1, 128))

        @pl.when(i == 0)
        def _():
            dw_ref[...] = part
            loss_ref[...] = lrow

        @pl.when(i > 0)
        def _():
            dw_ref[...] += part
            loss_ref[...] += lrow

    row = pl.BlockSpec((tm, d), lambda i: (i, 0))
    vec = pl.BlockSpec((1, d), lambda i: (0, 0))
    return pl.pallas_call(
        body, name="loss_bwd", grid=(s // tm,), in_specs=[row, row, vec],
        out_specs=[row, pl.BlockSpec((1, 128), lambda i: (0, 0)), vec, row],
        out_shape=[jax.ShapeDtypeStruct((s, d), F32), jax.ShapeDtypeStruct((1, 128), F32),
                   jax.ShapeDtypeStruct((1, d), F32), jax.ShapeDtypeStruct((s, d), BF16)],
        compiler_params=_cp(("arbitrary",)),
    )(h2, tgt, wf)


def _attn_mask(n):
    si = lax.broadcasted_iota(jnp.int32, (2 * BLK, 4 * BLK), 0)
    qi = lax.broadcasted_iota(jnp.int32, (2 * BLK, 4 * BLK), 1) & (BLK - 1)
    dist = BLK + qi - si
    kpos = n * BLK - BLK + si
    return (dist >= 0) & (dist < BLK) & (kpos >= 0)


def _attn_probs(q_ref, kc_ref, kp_ref, sk_ref, kvh, valid):
    rows = slice(kvh * 64, (kvh + 1) * 64)
    kt = jnp.concatenate([kp_ref[rows, :], kc_ref[rows, :]], axis=1).astype(BF16)
    qt = jnp.concatenate([q_ref[(kvh * 4 + g) * 64:(kvh * 4 + g + 1) * 64, :] for g in range(4)], axis=1).astype(BF16)
    s = _dot_tn(kt, qt) * 0.125
    s = jnp.where(valid, s, NEG)
    head = lax.broadcasted_iota(jnp.int32, (1, 4 * BLK), 1) >> 7
    sink = jnp.zeros((1, 4 * BLK), F32)
    for g in range(4):
        sink = jnp.where(head == g, sk_ref[0:1, kvh * 4 + g:kvh * 4 + g + 1], sink)
    m = jnp.maximum(jnp.max(s, axis=0, keepdims=True), sink)
    p = jnp.where(valid, jnp.exp(s - m), 0.0)
    es = jnp.exp(sink - m)
    inv = 1.0 / (jnp.sum(p, axis=0, keepdims=True) + es)
    return qt, kt, p * inv, es * inv


def _attn_in_specs(cur, prev):
    return [pl.BlockSpec((Q_DIM, BLK), lambda n: (0, cur(n))),
            pl.BlockSpec((KV_DIM, BLK), lambda n: (O_K // KV_DIM, cur(n))),
            pl.BlockSpec((KV_DIM, BLK), lambda n: (O_K // KV_DIM, prev(n))),
            pl.BlockSpec((KV_DIM, BLK), lambda n: (O_V // KV_DIM, cur(n))),
            pl.BlockSpec((KV_DIM, BLK), lambda n: (O_V // KV_DIM, prev(n))),
            pl.BlockSpec((1, 128), lambda n: (0, 0))]


def _attn_fwd(qkvt, sinks, side=None):
    s = qkvt.shape[1]
    nb = s // BLK

    def body(q_ref, kc_ref, kp_ref, vc_ref, vp_ref, sk_ref, o_ref):
        valid = _attn_mask(pl.program_id(0))
        for kvh in range(4):
            rows = slice(kvh * 64, (kvh + 1) * 64)
            _, _, probs, _ = _attn_probs(q_ref, kc_ref, kp_ref, sk_ref, kvh, valid)
            vt = jnp.concatenate([vp_ref[rows, :], vc_ref[rows, :]], axis=1).astype(BF16)
            o = _dot(vt, probs.astype(BF16))
            for g in range(4):
                h = kvh * 4 + g
                o_ref[h * 64:(h + 1) * 64, :] = o[:, g * BLK:(g + 1) * BLK].astype(BF16)

    own, extra = _hosted(
        body, name="attn_fwd", grid=(nb,), in_specs=_attn_in_specs(lambda n: n, lambda n: jnp.maximum(n - 1, 0)),
        out_specs=[pl.BlockSpec((Q_DIM, BLK), lambda n: (0, n))],
        out_shape=[jax.ShapeDtypeStruct((Q_DIM, s), BF16)], scratch_shapes=[],
        args=(qkvt, qkvt, qkvt, qkvt, qkvt, sinks), sem=("parallel",), side=side)
    return own[0] if side is None else (own[0], extra)


def _attn_bwd(qkvt, sinks, o, do, side=None):
    s = qkvt.shape[1]
    nb = s // BLK

    def body(q_ref, kc_ref, kp_ref, vc_ref, vp_ref, sk_ref, o_ref, do_ref, dq_ref, dk_ref, dv_ref, dsk_ref, ck, cv, nk, nv):
        n = pl.program_id(0)

        @pl.when(n == 0)
        def _():
            ck[...] = jnp.zeros_like(ck)
            cv[...] = jnp.zeros_like(cv)
            dsk_ref[...] = jnp.zeros_like(dsk_ref)

        @pl.when(n < nb)
        def _():
            valid = _attn_mask(n)
            lane = lax.broadcasted_iota(jnp.int32, (1, 128), 1)
            dsk = jnp.zeros((1, 128), F32)
            for kvh in range(4):
                rows = slice(kvh * 64, (kvh + 1) * 64)
                qt, kt, probs, psink = _attn_probs(q_ref, kc_ref, kp_ref, sk_ref, kvh, valid)
                vt = jnp.concatenate([vp_ref[rows, :], vc_ref[rows, :]], axis=1).astype(BF16)
                heads = [slice((kvh * 4 + g) * 64, (kvh * 4 + g + 1) * 64) for g in range(4)]
                dot = jnp.concatenate([do_ref[hh, :] for hh in heads], axis=1)
                ot = jnp.concatenate([o_ref[hh, :] for hh in heads], axis=1).astype(F32)
                delta = jnp.sum(dot * ot, axis=0, keepdims=True)
                dot16 = dot.astype(BF16)
                dp = _dot_tn(vt, dot16)
                ds = (probs * (dp - delta) * 0.125).astype(BF16)
                dqt = _dot(kt, ds)
                nk[rows, :] = _dot_nt(qt, ds)
                nv[rows, :] = _dot_nt(dot16, probs.astype(BF16))
                sd = psink * delta
                for g in range(4):
                    dq_ref[heads[g], :] = dqt[:, g * BLK:(g + 1) * BLK].astype(BF16)
                    val = -jnp.sum(sd[:, g * BLK:(g + 1) * BLK], axis=1, keepdims=True)
                    dsk = dsk + jnp.where(lane == kvh * 4 + g, val, 0.0)
            dsk_ref[0:1, :] += dsk
            dk_ref[...] = (ck[...] + nk[:, :BLK]).astype(BF16)
            dv_ref[...] = (cv[...] + nv[:, :BLK]).astype(BF16)
            ck[...] = nk[:, BLK:]
            cv[...] = nv[:, BLK:]

        @pl.when(n == nb)
        def _():
            dk_ref[...] = ck[...].astype(BF16)
            dv_ref[...] = cv[...].astype(BF16)

    cur = lambda n: jnp.minimum(n, nb - 1)
    prev = lambda n: jnp.maximum(jnp.minimum(n, nb - 1) - 1, 0)
    outb = lambda n: jnp.maximum(n - 1, 0)
    own, extra = _hosted(
        body, name="attn_bwd", grid=(nb + 1,),
        in_specs=_attn_in_specs(cur, prev) + [pl.BlockSpec((Q_DIM, BLK), lambda n: (0, cur(n))),
                                              pl.BlockSpec((Q_DIM, BLK), lambda n: (0, cur(n)))],
        out_specs=[pl.BlockSpec((Q_DIM, BLK), lambda n: (0, cur(n))),
                   pl.BlockSpec((KV_DIM, BLK), lambda n: (0, outb(n))),
                   pl.BlockSpec((KV_DIM, BLK), lambda n: (0, outb(n))),
                   pl.BlockSpec((8, 128), lambda n: (0, 0))],
        out_shape=[jax.ShapeDtypeStruct((Q_DIM, s), BF16), jax.ShapeDtypeStruct((KV_DIM, s), BF16),
                   jax.ShapeDtypeStruct((KV_DIM, s), BF16), jax.ShapeDtypeStruct((8, 128), F32)],
        scratch_shapes=[pltpu.VMEM((KV_DIM, BLK), F32)] * 2 + [pltpu.VMEM((KV_DIM, 2 * BLK), F32)] * 2,
        args=(qkvt, qkvt, qkvt, qkvt, qkvt, sinks, o, do), sem=("arbitrary",), side=side)
    return own if side is None else (own, extra)


def _shift_down(x, j):
    if j == 0:
        return x
    row = lax.broadcasted_iota(jnp.int32, x.shape, 0)
    return jnp.where(row >= j, pltpu.roll(x, j, 0), 0.0)


def _shift_up(x, j):
    if j == 0:
        return x
    s = x.shape[0]
    row = lax.broadcasted_iota(jnp.int32, x.shape, 0)
    return jnp.where(row < s - j, pltpu.roll(x, s - j, 0), 0.0)


def _conv(x, w_ref, b_ref):
    kk = w_ref.shape[0]
    y = _shift_down(x, kk - 1) * w_ref[0:1, :]
    for q in range(1, kk):
        y = y + _shift_down(x, kk - 1 - q) * w_ref[q:q + 1, :]
    return y + b_ref[...]


def _conv_bwd(dy, x, w_ref, dx_dtype):
    kk = w_ref.shape[0]
    dx = _shift_up(dy, kk - 1) * w_ref[0:1, :]
    dws = [jnp.sum(dy * _shift_down(x, kk - 1), axis=0, keepdims=True)]
    for q in range(1, kk):
        dx = dx + _shift_up(dy, kk - 1 - q) * w_ref[q:q + 1, :]
        dws.append(jnp.sum(dy * _shift_down(x, kk - 1 - q), axis=0, keepdims=True))
    return dx.astype(dx_dtype), dws, jnp.sum(dy, axis=0, keepdims=True)


def _dsilu(y, sg):
    return sg * (1.0 + y * (1.0 - sg))


CT = 256


def _ssd_conv_fwd(proj, w, b):
    s = proj.shape[0]

    def body(x_ref, w_ref, b_ref, o_ref):
        y = _conv(x_ref[...], w_ref, b_ref)
        o_ref[...] = y * _sigmoid(y)

    return pl.pallas_call(
        body, name="ssd_conv_fwd", grid=(XBC_DIM // CT,),
        in_specs=[pl.BlockSpec((s, CT), lambda i: (0, O_X // CT + i)), pl.BlockSpec((4, CT), lambda i: (0, i)),
                  pl.BlockSpec((1, CT), lambda i: (0, i))],
        out_specs=pl.BlockSpec((s, CT), lambda i: (0, i)),
        out_shape=jax.ShapeDtypeStruct((s, XBC_DIM), F32), compiler_params=_cp(("parallel",)),
    )(proj, w, b)


def _ssd_conv_bwd(dact, proj, w, b):
    s = proj.shape[0]

    def body(d_ref, x_ref, w_ref, b_ref, dx_ref, dw_ref, db_ref):
        x = x_ref[...]
        y = _conv(x, w_ref, b_ref)
        dy = d_ref[...] * _dsilu(y, _sigmoid(y))
        dx, dws, db = _conv_bwd(dy, x, w_ref, BF16)
        dx_ref[...] = dx
        for q in range(4):
            dw_ref[q:q + 1, :] = dws[q]
        db_ref[...] = db

    return pl.pallas_call(
        body, name="ssd_conv_bwd", grid=(XBC_DIM // CT,),
        in_specs=[pl.BlockSpec((s, CT), lambda i: (0, i)), pl.BlockSpec((s, CT), lambda i: (0, O_X // CT + i)),
                  pl.BlockSpec((4, CT), lambda i: (0, i)), pl.BlockSpec((1, CT), lambda i: (0, i))],
        out_specs=[pl.BlockSpec((s, CT), lambda i: (0, i)), pl.BlockSpec((4, CT), lambda i: (0, i)),
                   pl.BlockSpec((1, CT), lambda i: (0, i))],
        out_shape=[jax.ShapeDtypeStruct((s, XBC_DIM), BF16), jax.ShapeDtypeStruct((4, XBC_DIM), F32),
                   jax.ShapeDtypeStruct((1, XBC_DIM), F32)],
        compiler_params=_cp(("parallel",)),
    )(dact, proj, w, b)


NFT = D_FF // CT


def _ffn_act_fwd(up, w, b):
    s = up.shape[0]

    def body(v_ref, g_ref, wv_ref, wg_ref, bv_ref, bg_ref, o_ref):
        val = _conv(v_ref[...], wv_ref, bv_ref)
        gt = _conv(g_ref[...], wg_ref, bg_ref)
        o_ref[...] = ((gt * _sigmoid(gt)) * val).astype(BF16)

    col = lambda off: (lambda i: (0, off + i))
    return pl.pallas_call(
        body, name="ffn_act_fwd", grid=(NFT,),
        in_specs=[pl.BlockSpec((s, CT), col(0)), pl.BlockSpec((s, CT), col(NFT)),
                  pl.BlockSpec((3, CT), col(0)), pl.BlockSpec((3, CT), col(NFT)),
                  pl.BlockSpec((1, CT), col(0)), pl.BlockSpec((1, CT), col(NFT))],
        out_specs=pl.BlockSpec((s, CT), col(0)),
        out_shape=jax.ShapeDtypeStruct((s, D_FF), BF16), compiler_params=_cp(("parallel",)),
    )(up, up, w, w, b, b)


def _ffn_act_bwd(dact, up, w, b):
    s = up.shape[0]

    def body(d_ref, v_ref, g_ref, wv_ref, wg_ref, bv_ref, bg_ref, dx_ref, dw_ref, db_ref):
        xv, xg = v_ref[...], g_ref[...]
        val = _conv(xv, wv_ref, bv_ref)
        gt = _conv(xg, wg_ref, bg_ref)
        sg = _sigmoid(gt)
        d = d_ref[...]
        for half, (dy, x, w_ref) in enumerate(((d * (gt * sg), xv, wv_ref), (d * val * _dsilu(gt, sg), xg, wg_ref))):
            dx, dws, db = _conv_bwd(dy, x, w_ref, BF16)
            dx_ref[half] = dx
            for q in range(3):
                dw_ref[half, q:q + 1, :] = dws[q]
            db_ref[half] = db

    col = lambda off: (lambda i: (0, off + i))
    both = lambda i: (0, 0, i)
    return pl.pallas_call(
        body, name="ffn_act_bwd", grid=(NFT,),
        in_specs=[pl.BlockSpec((s, CT), col(0)), pl.BlockSpec((s, CT), col(0)), pl.BlockSpec((s, CT), col(NFT)),
                  pl.BlockSpec((3, CT), col(0)), pl.BlockSpec((3, CT), col(NFT)),
                  pl.BlockSpec((1, CT), col(0)), pl.BlockSpec((1, CT), col(NFT))],
        out_specs=[pl.BlockSpec((2, s, CT), both), pl.BlockSpec((2, 3, CT), both), pl.BlockSpec((2, 1, CT), both)],
        out_shape=[jax.ShapeDtypeStruct((2, s, D_FF), BF16), jax.ShapeDtypeStruct((2, 3, D_FF), F32),
                   jax.ShapeDtypeStruct((2, 1, D_FF), F32)],
        compiler_params=_cp(("parallel",)),
    )(dact, up, up, w, w, b, b)


def _expand_mat():
    r = lax.broadcasted_iota(jnp.int32, (128, D_INNER), 0)
    c = lax.broadcasted_iota(jnp.int32, (128, D_INNER), 1)
    return ((c >> 6) == r).astype(BF16)


def _reduce_mat():
    r = lax.broadcasted_iota(jnp.int32, (D_INNER, 128), 0)
    c = lax.broadcasted_iota(jnp.int32, (D_INNER, 128), 1)
    return ((r >> 6) == c).astype(BF16)


def _split(v, parts):
    out = []
    for _ in range(parts - 1):
        p = v.astype(BF16)
        out.append(p)
        v = v - p.astype(F32)
    out.append(v.astype(BF16))
    return out


def _sel_dot(v, sel, parts):
    acc = None
    for p in reversed(_split(v, parts)):
        t = _dot(p, sel)
        acc = t if acc is None else acc + t
    return acc


def _row8(v):
    return jnp.broadcast_to(v, (8, v.shape[1]))


def _tril():
    r = lax.broadcasted_iota(jnp.int32, (BLK, BLK), 0)
    c = lax.broadcasted_iota(jnp.int32, (BLK, BLK), 1)
    return r >= c


def _softplus(x):
    return jnp.maximum(x, 0.0) + jnp.log(1.0 + jnp.exp(-jnp.abs(x)))


def _ssd_common(dtraw_ref, dtb_ref, alog_ref):
    causal = _tril()
    e_mat = _expand_mat()
    a_neg = -jnp.exp(alog_ref[...])
    dt = _softplus(dtraw_ref[...] + dtb_ref[...])
    a_cs = _dot(causal.astype(F32), dt * a_neg, HI)
    a_cs_t = a_cs.T
    dt_x = _sel_dot(dt, e_mat, 3)
    acs_x = _sel_dot(a_cs, e_mat, 3)
    alast_x = acs_x[BLK - 1:BLK, :]
    ea_x = jnp.exp(acs_x)
    ds_x = jnp.exp(alast_x - acs_x)
    elast_x = jnp.exp(alast_x)
    return causal, e_mat, a_neg, dt, a_cs, a_cs_t, dt_x, ea_x, ds_x, elast_x


def _decay(a_cs, a_cs_t, h, causal):
    seg = a_cs[:, h:h + 1] - a_cs_t[h:h + 1, :]
    return jnp.where(causal, jnp.exp(jnp.where(causal, seg, 0.0)), 0.0)


def _ssd_fwd(xbc, proj, dt_bias, a_log, d_skip, side=None):
    s = xbc.shape[0]
    nc = s // BLK

    def body(xs_ref, b_ref, c_ref, dtraw_ref, dtb_ref, alog_ref, dskip_ref, y_ref, hp_ref, h_scr, xc16):
        @pl.when(pl.program_id(0) == 0)
        def _():
            h_scr[...] = jnp.zeros_like(h_scr)

        causal, e_mat, _, _, a_cs, a_cs_t, dt_x, ea_x, ds_x, elast_x = _ssd_common(dtraw_ref, dtb_ref, alog_ref)
        dskip_x = _sel_dot(_row8(dskip_ref[...]), e_mat, 3)[0:1]
        xs = xs_ref[...]
        xc = xs * dt_x
        xc16[...] = xc.astype(BF16)
        xcd = (xc * ds_x).astype(BF16)
        hp_ref[0] = h_scr[...]
        for g in range(4):
            gs = slice(g * 512, (g + 1) * 512)
            cg = c_ref[:, g * 128:(g + 1) * 128].astype(BF16)
            bg = b_ref[:, g * 128:(g + 1) * 128].astype(BF16)
            cb = _dot_nt(cg, bg)
            hg = h_scr[:, gs]
            yoff = _dot(cg, hg.astype(BF16)) * ea_x[:, gs]
            for j in range(8):
                h = g * 8 + j
                hsl = slice(h * 64, (h + 1) * 64)
                mm = (cb * _decay(a_cs, a_cs_t, h, causal)).astype(BF16)
                y_ref[:, hsl] = _dot(mm, xc16[:, hsl])
            y_ref[:, gs] += yoff + xs[:, gs] * dskip_x[:, gs]
            h_scr[:, gs] = hg * elast_x[:, gs] + _dot_tn(bg, xcd[:, gs])

    vec = pl.BlockSpec((1, 128), lambda c: (0, 0))
    own, extra = _hosted(
        body, name="ssd_fwd", grid=(nc,),
        in_specs=[pl.BlockSpec((BLK, D_INNER), lambda c: (c, 0)),
                  pl.BlockSpec((BLK, BC_DIM), lambda c: (c, D_INNER // BC_DIM)),
                  pl.BlockSpec((BLK, BC_DIM), lambda c: (c, D_INNER // BC_DIM + 1)),
                  pl.BlockSpec((BLK, 128), lambda c: (c, O_DT // 128)), vec, vec, vec],
        out_specs=[pl.BlockSpec((BLK, D_INNER), lambda c: (c, 0)),
                   pl.BlockSpec((1, 128, D_INNER), lambda c: (c, 0, 0))],
        out_shape=[jax.ShapeDtypeStruct((s, D_INNER), F32), jax.ShapeDtypeStruct((nc, 128, D_INNER), F32)],
        scratch_shapes=[pltpu.VMEM((128, D_INNER), F32), pltpu.VMEM((BLK, D_INNER), BF16)],
        args=(xbc, xbc, xbc, proj, dt_bias, a_log, d_skip), sem=("arbitrary",), side=side)
    return own if side is None else (own, extra)


def _ssd_bwd(xbc, proj, dt_bias, a_log, d_skip, hprev, dy, side=None):
    s = xbc.shape[0]
    nc = s // BLK

    def body(xs_ref, b_ref, c_ref, dtraw_ref, dtb_ref, alog_ref, dskip_ref, hp_ref, dy_ref,
             dxbc_ref, ddt_ref, dvec_ref, dh_scr, xc16, dy16, dxc_scr, dacs_r, tdiff):
        step = pl.program_id(0)
        dacs_r[...] = jnp.zeros_like(dacs_r)

        @pl.when(step == 0)
        def _():
            dh_scr[...] = jnp.zeros_like(dh_scr)
            dvec_ref[...] = jnp.zeros_like(dvec_ref)

        causal, e_mat, a_neg, dt, a_cs, a_cs_t, dt_x, ea_x, ds_x, elast_x = _ssd_common(dtraw_ref, dtb_ref, alog_ref)
        r_mat = _reduce_mat()
        lane = lax.broadcasted_iota(jnp.int32, (1, 128), 1)
        dskip_x = _sel_dot(_row8(dskip_ref[...]), e_mat, 3)[0:1]
        xs = xs_ref[...]
        dy = dy_ref[...]
        xc = xs * dt_x
        xcd = xc * ds_x
        xc16[...] = xc.astype(BF16)
        dy16[...] = dy.astype(BF16)
        dyea = dy * ea_x
        dh = dh_scr[...]
        hp = hp_ref[0]
        dalast_x = jnp.sum(dh * hp, axis=0, keepdims=True) * elast_x
        dacs = jnp.zeros((BLK, 128), F32)
        for g in range(4):
            gs = slice(g * 512, (g + 1) * 512)
            bsl = slice(g * 128, (g + 1) * 128)
            cg = c_ref[:, bsl].astype(BF16)
            bg = b_ref[:, bsl].astype(BF16)
            cb = _dot_nt(cg, bg)
            hg16 = hp[:, gs].astype(BF16)
            dhg16 = dh[:, gs].astype(BF16)
            raw = _dot(cg, hg16)
            draw16 = dyea[:, gs].astype(BF16)
            dcg = _dot_nt(draw16, hg16)
            dhp_g = _dot_tn(cg, draw16)
            dbg = _dot_nt(xcd[:, gs].astype(BF16), dhg16)
            dxcd = _dot(bg, dhg16)
            dcb = jnp.zeros((BLK, BLK), F32)
            for j in range(8):
                h = g * 8 + j
                hsl = slice(h * 64, (h + 1) * 64)
                decay = _decay(a_cs, a_cs_t, h, causal)
                m = cb * decay
                dm = _dot_nt(dy16[:, hsl], xc16[:, hsl])
                dxc_scr[:, hsl] = _dot_tn(m.astype(BF16), dy16[:, hsl])
                dcb = dcb + dm * decay
                dseg = dm * m
                oneh = jnp.where(lane == h, 1.0, 0.0)
                dacs = dacs + jnp.sum(dseg, axis=1, keepdims=True) * oneh
                dacs_r[h:h + 1, :] = jnp.sum(dseg, axis=0, keepdims=True)
            dcb16 = dcb.astype(BF16)
            dcg = dcg + _dot(dcb16, bg)
            dbg = dbg + _dot_tn(dcb16, cg)
            dxbc_ref[:, D_INNER + g * 128:D_INNER + (g + 1) * 128] = dbg
            dxbc_ref[:, D_INNER + BC_DIM + g * 128:D_INNER + BC_DIM + (g + 1) * 128] = dcg
            dxc_scr[:, gs] += dxcd * ds_x[:, gs]
            dh_scr[:, gs] = dh[:, gs] * elast_x[:, gs] + dhp_g
            tst = dxcd * xcd[:, gs]
            tdiff[:, gs] = dy[:, gs] * (raw * ea_x[:, gs]) - tst
            tdiff[BLK - 1:BLK, gs] += jnp.sum(tst, axis=0, keepdims=True)
        dxc = dxc_scr[...]
        row = lax.broadcasted_iota(jnp.int32, (BLK, D_INNER), 0)
        tfull = tdiff[...] + jnp.where(row == BLK - 1, dalast_x, 0.0)
        dacs = dacs + _sel_dot(tfull, r_mat, 2) - dacs_r[...].T
        da = _dot_tn(causal.astype(F32), dacs, HI)
        ddt = da * a_neg + _sel_dot(dxc * xs, r_mat, 2)
        lmask = lax.broadcasted_iota(jnp.int32, (BLK, 128), 1) < N_SSD_HEADS
        ddtraw = jnp.where(lmask, ddt * _sigmoid(dtraw_ref[...] + dtb_ref[...]), 0.0)
        ddt_ref[...] = ddtraw.astype(BF16)
        dxbc_ref[:, 0:D_INNER] = dy * dskip_x + dxc * dt_x
        dvec_ref[0:1, :] += jnp.sum(ddtraw, axis=0, keepdims=True)
        dvec_ref[1:2, :] += jnp.where(lane < N_SSD_HEADS, jnp.sum(da * dt, axis=0, keepdims=True) * a_neg, 0.0)
        dvec_ref[2:3, :] += _sel_dot(_row8(jnp.sum(dy * xs, axis=0, keepdims=True)), r_mat, 3)[0:1]

    rev = lambda c: nc - 1 - c
    vec = pl.BlockSpec((1, 128), lambda c: (0, 0))
    own, extra = _hosted(
        body, name="ssd_bwd", grid=(nc,),
        in_specs=[pl.BlockSpec((BLK, D_INNER), lambda c: (rev(c), 0)),
                  pl.BlockSpec((BLK, BC_DIM), lambda c: (rev(c), D_INNER // BC_DIM)),
                  pl.BlockSpec((BLK, BC_DIM), lambda c: (rev(c), D_INNER // BC_DIM + 1)),
                  pl.BlockSpec((BLK, 128), lambda c: (rev(c), O_DT // 128)), vec, vec, vec,
                  pl.BlockSpec((1, 128, D_INNER), lambda c: (rev(c), 0, 0)),
                  pl.BlockSpec((BLK, D_INNER), lambda c: (rev(c), 0))],
        out_specs=[pl.BlockSpec((BLK, XBC_DIM), lambda c: (rev(c), 0)),
                   pl.BlockSpec((BLK, 128), lambda c: (rev(c), 0)),
                   pl.BlockSpec((8, 128), lambda c: (0, 0))],
        out_shape=[jax.ShapeDtypeStruct((s, XBC_DIM), F32), jax.ShapeDtypeStruct((s, 128), BF16),
                   jax.ShapeDtypeStruct((8, 128), F32)],
        scratch_shapes=[pltpu.VMEM((128, D_INNER), F32), pltpu.VMEM((BLK, D_INNER), BF16),
                        pltpu.VMEM((BLK, D_INNER), BF16), pltpu.VMEM((BLK, D_INNER), F32),
                        pltpu.VMEM((128, BLK), F32), pltpu.VMEM((BLK, D_INNER), F32)],
        args=(xbc, xbc, xbc, proj, dt_bias, a_log, d_skip, hprev, dy), sem=("arbitrary",), side=side)
    return own if side is None else (own, extra)


GW = 512


def _gate_norm_fwd(y, proj, wn, *, tm=512):
    s = y.shape[0]
    tm = _tile(s, tm)

    def body(y_ref, z_ref, w_ref, o_ref):
        z = z_ref[...]
        y2 = y_ref[...] * (z * _sigmoid(z))
        r = lax.rsqrt(jnp.mean(y2 * y2, axis=-1, keepdims=True) + EPS)
        o_ref[...] = ((y2 * r) * w_ref[...]).astype(BF16)

    return pl.pallas_call(
        body, name="gate_norm_fwd", grid=(s // tm, 4),
        in_specs=[pl.BlockSpec((tm, GW), lambda i, g: (i, g)), pl.BlockSpec((tm, GW), lambda i, g: (i, O_Z // GW + g)),
                  pl.BlockSpec((1, GW), lambda i, g: (0, g))],
        out_specs=pl.BlockSpec((tm, GW), lambda i, g: (i, g)),
        out_shape=jax.ShapeDtypeStruct((s, D_INNER), BF16), compiler_params=_cp(("parallel", "parallel")),
    )(y, proj, wn)


def _gate_norm_bwd(dyn, y, proj, wn, *, tm=512):
    s = y.shape[0]
    tm = _tile(s, tm)

    def body(d_ref, y_ref, z_ref, w_ref, dy_ref, dz_ref, dw_ref):
        i = pl.program_id(1)
        z = z_ref[...]
        sg = _sigmoid(z)
        sz = z * sg
        yv = y_ref[...]
        y2 = yv * sz
        r = lax.rsqrt(jnp.mean(y2 * y2, axis=-1, keepdims=True) + EPS)
        xh = y2 * r
        dv = d_ref[...]
        g = dv * w_ref[...]
        dy2 = r * (g - xh * jnp.mean(g * xh, axis=-1, keepdims=True))
        dy_ref[...] = dy2 * sz
        dz_ref[...] = (dy2 * yv * _dsilu(z, sg)).astype(BF16)
        part = jnp.sum(dv * xh, axis=0, keepdims=True)

        @pl.when(i == 0)
        def _():
            dw_ref[...] = part

        @pl.when(i > 0)
        def _():
            dw_ref[...] += part

    blk = pl.BlockSpec((tm, GW), lambda g, i: (i, g))
    vec = pl.BlockSpec((1, GW), lambda g, i: (0, g))
    return pl.pallas_call(
        body, name="gate_norm_bwd", grid=(4, s // tm),
        in_specs=[blk, blk, pl.BlockSpec((tm, GW), lambda g, i: (i, O_Z // GW + g)), vec],
        out_specs=[blk, blk, vec],
        out_shape=[jax.ShapeDtypeStruct((s, D_INNER), F32), jax.ShapeDtypeStruct((s, D_INNER), BF16),
                   jax.ShapeDtypeStruct((1, D_INNER), F32)],
        compiler_params=_cp(("parallel", "arbitrary")),
    )(dyn, y, proj, wn)


def _merge_fwd(proj, b_gate, attn, ssd_out, *, tm=512):
    s = attn.shape[0]
    tm = _tile(s, tm)

    def body(ga_ref, gs_ref, ba_ref, bs_ref, a_ref, s_ref, o_ref):
        ga = _sigmoid(ga_ref[...] + ba_ref[...])
        gs = _sigmoid(gs_ref[...] + bs_ref[...])
        o_ref[...] = (ga * a_ref[...] + gs * s_ref[...]).astype(BF16)

    blk = pl.BlockSpec((tm, GW), lambda i, j: (i, j))
    return pl.pallas_call(
        body, name="merge_fwd", grid=(s // tm, 2),
        in_specs=[pl.BlockSpec((tm, GW), lambda i, j: (i, O_GA // GW + j)),
                  pl.BlockSpec((tm, GW), lambda i, j: (i, O_GS // GW + j)),
                  pl.BlockSpec((1, GW), lambda i, j: (0, j)), pl.BlockSpec((1, GW), lambda i, j: (0, 2 + j)), blk, blk],
        out_specs=blk, out_shape=jax.ShapeDtypeStruct((s, D_MODEL), BF16),
        compiler_params=_cp(("parallel", "parallel")),
    )(proj, proj, b_gate, b_gate, attn, ssd_out)


def _merge_bwd(dm, proj, b_gate, attn, ssd_out, *, tm=512):
    s = attn.shape[0]
    tm = _tile(s, tm)

    def body(d_ref, ga_ref, gs_ref, ba_ref, bs_ref, a_ref, s_ref, da_ref, ds_ref, dga_ref, dgs_ref, dba_ref, dbs_ref):
        i = pl.program_id(1)
        ga = _sigmoid(ga_ref[...] + ba_ref[...])
        gs = _sigmoid(gs_ref[...] + bs_ref[...])
        d = d_ref[...]
        da_ref[...] = (d * ga).astype(BF16)
        ds_ref[...] = (d * gs).astype(BF16)
        dga = d * a_ref[...] * (ga * (1.0 - ga))
        dgs = d * s_ref[...] * (gs * (1.0 - gs))
        dga_ref[...] = dga.astype(BF16)
        dgs_ref[...] = dgs.astype(BF16)
        pa = jnp.sum(dga, axis=0, keepdims=True)
        ps = jnp.sum(dgs, axis=0, keepdims=True)

        @pl.when(i == 0)
        def _():
            dba_ref[...] = pa
            dbs_ref[...] = ps

        @pl.when(i > 0)
        def _():
            dba_ref[...] += pa
            dbs_ref[...] += ps

    blk = pl.BlockSpec((tm, GW), lambda j, i: (i, j))
    vec = pl.BlockSpec((1, GW), lambda j, i: (0, j))
    sd = jax.ShapeDtypeStruct((s, D_MODEL), BF16)
    vd = jax.ShapeDtypeStruct((1, D_MODEL), F32)
    return pl.pallas_call(
        body, name="merge_bwd", grid=(2, s // tm),
        in_specs=[blk, pl.BlockSpec((tm, GW), lambda j, i: (i, O_GA // GW + j)),
                  pl.BlockSpec((tm, GW), lambda j, i: (i, O_GS // GW + j)),
                  vec, pl.BlockSpec((1, GW), lambda j, i: (0, 2 + j)), blk, blk],
        out_specs=[blk, blk, blk, blk, vec, vec], out_shape=[sd, sd, sd, sd, vd, vd],
        compiler_params=_cp(("parallel", "arbitrary")),
    )(dm, proj, proj, b_gate, b_gate, attn, ssd_out)


def _adamw_math(w, g, m, v):
    mn = ADAM_B1 * m + (1.0 - ADAM_B1) * g
    vn = ADAM_B2 * v + (1.0 - ADAM_B2) * (g * g)
    m_hat = mn / (1.0 - ADAM_B1 ** ADAM_STEP)
    v_hat = vn / (1.0 - ADAM_B2 ** ADAM_STEP)
    return -ADAM_LR * (m_hat / (jnp.sqrt(v_hat) + ADAM_EPS) + ADAM_WD * w), mn, vn


def _adamw_many(ws, gs, ms, vs):
    n = len(ws)

    def body(*refs):
        outs = refs[4 * n:]
        for i in range(n):
            res = _adamw_math(*[refs[q * n + i][...] for q in range(4)])
            for q in range(3):
                outs[q * n + i][...] = res[q]

    return pl.pallas_call(body, name="adamw_small", out_shape=[jax.ShapeDtypeStruct(w.shape, F32) for w in ws] * 3,
                          compiler_params=_cp())(*ws, *gs, *ms, *vs)


def _adamw(w, g, m, v, *, name, tm=128):
    r, c = w.shape
    tm = r if (r < tm or r % tm) else tm

    def body(w_ref, g_ref, m_ref, v_ref, d_ref, nm_ref, nv_ref, g_out):
        gv = g_ref[:, :c]
        d_ref[...], nm_ref[...], nv_ref[...] = _adamw_math(w_ref[...], gv, m_ref[...], v_ref[...])
        g_out[...] = gv

    blk = pl.BlockSpec((tm, c), lambda i: (i, 0))
    sd = jax.ShapeDtypeStruct((r, c), F32)
    return pl.pallas_call(
        body, name=name, grid=(r // tm,), in_specs=[blk, pl.BlockSpec((tm, g.shape[1]), lambda i: (i, 0)), blk, blk],
        out_specs=[blk] * 4, out_shape=[sd] * 4, compiler_params=_cp(("parallel",)),
    )(w, g, m, v)


ANY = pl.BlockSpec(memory_space=pl.ANY)
N_CHIPS = 4


def _chip_of(k, x, y):
    return (x ^ (k >> 1), y ^ (k & 1))


def _all_gather_small(shard):
    r, c = shard.shape
    hr = r // 2

    def body(sh_ref, out_ref, send_sems, recv_sems, local_sem):
        x, y, cc = lax.axis_index("x"), lax.axis_index("y"), lax.axis_index("c")

        def half(px, py, pc):
            return out_ref.at[2 * px + py, pl.ds(pc * hr, hr), :]

        def copy(k, px, py, pc, to, src=None):
            return pltpu.make_async_remote_copy(
                src_ref=half(px, py, pc) if src is None else src, dst_ref=half(px, py, pc),
                send_sem=send_sems.at[k], recv_sem=recv_sems.at[k], device_id=to, device_id_type=MESH)

        mine = pltpu.make_async_copy(sh_ref, out_ref.at[2 * x + y], local_sem)
        mine.start()
        chips = [_chip_of(k, x, y) for k in (1, 2, 3)]
        first = [copy(j, x, y, cc, (*chip, cc), src=sh_ref.at[pl.ds(cc * hr, hr), :]) for j, chip in enumerate(chips)]
        for cp in first:
            cp.start()
        passed = [copy(3 + j, *chip, cc, (x, y, 1 - cc)) for j, chip in enumerate(chips)]
        for j, chip in enumerate(chips):
            copy(j, *chip, cc, (x, y, cc)).wait_recv()
            passed[j].start()
        for j, chip in enumerate(chips):
            copy(3 + j, *chip, 1 - cc, (x, y, cc)).wait_recv()
        for cp in first + passed:
            cp.wait_send()
        mine.wait()

    return pl.pallas_call(
        body, name="all_gather_small", in_specs=[ANY], out_specs=ANY,
        out_shape=jax.ShapeDtypeStruct((N_CHIPS, r, c), shard.dtype),
        scratch_shapes=[pltpu.SemaphoreType.DMA((6,)), pltpu.SemaphoreType.DMA((6,)), pltpu.SemaphoreType.DMA],
    )(shard)


def _cast_bf16(a, *, name, tm=512):
    n, r, c = a.shape
    tm = _tile(r, tm) if r % 128 == 0 else r

    def body(a_ref, o_ref):
        o_ref[...] = a_ref[...].astype(BF16)

    blk = pl.BlockSpec((1, tm, c), lambda i, j: (i, j, 0))
    return pl.pallas_call(body, name=name, grid=(n, r // tm), in_specs=[blk], out_specs=blk,
                          out_shape=jax.ShapeDtypeStruct(a.shape, BF16), compiler_params=_cp(("parallel", "parallel")))(a)


def _pair_exchange(g16, hr):
    n, r, c = g16.shape

    def body(g_ref, out_ref, send_sem, recv_sem):
        x, y, cc = lax.axis_index("x"), lax.axis_index("y"), lax.axis_index("c")
        cp = pltpu.make_async_remote_copy(
            src_ref=g_ref.at[:, pl.ds((1 - cc) * hr, hr), :], dst_ref=out_ref, send_sem=send_sem, recv_sem=recv_sem,
            device_id=(x, y, 1 - cc), device_id_type=MESH)
        cp.start()
        cp.wait()

    return pl.pallas_call(
        body, name="grad_pair_exchange", in_specs=[ANY], out_specs=ANY,
        out_shape=jax.ShapeDtypeStruct((n, hr, c), g16.dtype),
        scratch_shapes=[pltpu.SemaphoreType.DMA, pltpu.SemaphoreType.DMA],
    )(g16)


def _pair_add(g, recv, half_idx, hr, *, tm=384):
    n, r, c = g.shape
    nt = hr // tm

    def body(hi_ref, g_ref, r_ref, o32_ref, o16_ref):
        v = g_ref[...] + r_ref[...].astype(F32)
        o32_ref[...] = v
        o16_ref[...] = v.astype(BF16)

    gs = pltpu.PrefetchScalarGridSpec(
        num_scalar_prefetch=1, grid=(n, nt),
        in_specs=[pl.BlockSpec((1, tm, c), lambda i, j, hi: (i, hi[0] * nt + j, 0)),
                  pl.BlockSpec((1, tm, c), lambda i, j, hi: (i, j, 0))],
        out_specs=[pl.BlockSpec((1, tm, c), lambda i, j, hi: (i, j, 0))] * 2)
    return pl.pallas_call(
        body, name="grad_pair_add", grid_spec=gs,
        out_shape=[jax.ShapeDtypeStruct((n, hr, c), F32), jax.ShapeDtypeStruct((n, hr, c), BF16)],
        compiler_params=_cp(("parallel", "parallel")),
    )(half_idx, g, recv)


def _chip_exchange(p16):
    n, hr, c = p16.shape

    def body(p_ref, out_ref, send_sems, recv_sems):
        x, y, cc = lax.axis_index("x"), lax.axis_index("y"), lax.axis_index("c")
        cps = []
        for j, k in enumerate((1, 2, 3)):
            px, py = _chip_of(k, x, y)
            cps.append(pltpu.make_async_remote_copy(
                src_ref=p_ref.at[2 * px + py], dst_ref=out_ref.at[j], send_sem=send_sems.at[j], recv_sem=recv_sems.at[j],
                device_id=(px, py, cc), device_id_type=MESH))
        for cp in cps:
            cp.start()
        for cp in cps:
            cp.wait()

    return pl.pallas_call(
        body, name="grad_chip_exchange", in_specs=[ANY], out_specs=ANY,
        out_shape=jax.ShapeDtypeStruct((3, hr, c), p16.dtype),
        scratch_shapes=[pltpu.SemaphoreType.DMA((3,)), pltpu.SemaphoreType.DMA((3,))],
    )(p16)


def _chip_add(p32, recv, chip_idx, *, tm=384):
    n, hr, c = p32.shape

    def body(ci_ref, p_ref, r_ref, o_ref):
        o_ref[...] = ((p_ref[0] + r_ref[0].astype(F32)) + r_ref[1].astype(F32)) + r_ref[2].astype(F32)

    gs = pltpu.PrefetchScalarGridSpec(
        num_scalar_prefetch=1, grid=(hr // tm,),
        in_specs=[pl.BlockSpec((1, tm, c), lambda j, ci: (ci[0], j, 0)), pl.BlockSpec((3, tm, c), lambda j, ci: (0, j, 0))],
        out_specs=pl.BlockSpec((tm, c), lambda j, ci: (j, 0)))
    return pl.pallas_call(
        body, name="grad_chip_add", grid_spec=gs, out_shape=jax.ShapeDtypeStruct((hr, c), F32),
        compiler_params=_cp(("parallel",)),
    )(chip_idx, p32, recv)


def _pair_gather(f):
    hr, c = f.shape

    def body(f_ref, out_ref, send_sem, recv_sem, local_sem):
        x, y, cc = lax.axis_index("x"), lax.axis_index("y"), lax.axis_index("c")
        mine = pltpu.make_async_copy(f_ref, out_ref.at[pl.ds(cc * hr, hr), :], local_sem)
        mine.start()
        cp = pltpu.make_async_remote_copy(
            src_ref=f_ref, dst_ref=out_ref.at[pl.ds(cc * hr, hr), :], send_sem=send_sem, recv_sem=recv_sem,
            device_id=(x, y, 1 - cc), device_id_type=MESH)
        cp.start()
        cp.wait()
        mine.wait()

    return pl.pallas_call(
        body, name="grad_pair_gather", in_specs=[ANY], out_specs=ANY,
        out_shape=jax.ShapeDtypeStruct((2 * hr, c), f.dtype),
        scratch_shapes=[pltpu.SemaphoreType.DMA, pltpu.SemaphoreType.DMA, pltpu.SemaphoreType.DMA],
    )(f)


def _all_reduce_small(buf):
    r, c = buf.shape

    def body(b_ref, out_ref, gat, send_sems, recv_sems):
        x, y, cc = lax.axis_index("x"), lax.axis_index("y"), lax.axis_index("c")
        me = 4 * x + 2 * y + cc
        gat[me] = b_ref[...]
        cps = []
        for k in range(1, 8):
            px, py, pc = x ^ (k >> 2), y ^ ((k >> 1) & 1), cc ^ (k & 1)
            cps.append(pltpu.make_async_remote_copy(
                src_ref=b_ref, dst_ref=gat.at[me], send_sem=send_sems.at[k - 1], recv_sem=recv_sems.at[k - 1],
                device_id=(px, py, pc), device_id_type=MESH))
        for cp in cps:
            cp.start()
        for cp in cps:
            cp.wait()
        acc = gat[0]
        for d in range(1, 8):
            acc = acc + gat[d]
        out_ref[...] = acc

    vm = pl.BlockSpec(memory_space=pltpu.VMEM)
    return pl.pallas_call(
        body, name="all_reduce_small", in_specs=[vm], out_specs=vm, out_shape=jax.ShapeDtypeStruct((r, c), F32),
        scratch_shapes=[pltpu.VMEM((8, r, c), F32), pltpu.SemaphoreType.DMA((7,)), pltpu.SemaphoreType.DMA((7,))],
        compiler_params=pltpu.CompilerParams(vmem_limit_bytes=VMEM_LIMIT),
    )(buf)


def _pipe(fn, ins, outs, tr, depth=4, slots=None):
    shape = ins[0].shape
    lead, (r, c) = shape[:-2], shape[-2:]
    assert len(lead) <= 1 and r % tr == 0
    nr = r // tr
    which = list(range(lead[0])) if lead and slots is None else slots
    n = nr * (len(which) if lead else 1)
    ni, no = len(ins), len(outs)

    def blk(ref, step):
        rows = pl.ds((step % nr) * tr, tr)
        return ref.at[which[step // nr], rows, :] if lead else ref.at[rows, :]

    def scoped(*bufs):
        ibufs, obufs, isem, osem = bufs[:ni], bufs[ni:ni + no], bufs[-2], bufs[-1]

        def in_copy(q, step, slot):
            return pltpu.make_async_copy(blk(ins[q], step), ibufs[q].at[slot], isem.at[q, slot])

        def out_copy(q, step, slot):
            return pltpu.make_async_copy(obufs[q].at[slot], blk(outs[q], step), osem.at[q, slot])

        for step in range(min(nbuf - 1, n)):
            for q in range(ni):
                in_copy(q, step, step % nbuf).start()
        for step in range(n):
            slot = step % nbuf
            if step + nbuf - 1 < n:
                for q in range(ni):
                    in_copy(q, step + nbuf - 1, (step + nbuf - 1) % nbuf).start()
            for q in range(ni):
                in_copy(q, step, slot).wait()
            if step >= nbuf:
                for q in range(no):
                    out_copy(q, step - nbuf, slot).wait()
            res = fn(*[ibufs[q][slot] for q in range(ni)])
            for q in range(no):
                obufs[q][slot] = res[q].astype(obufs[q].dtype)
                out_copy(q, step, slot).start()
        for step in range(max(n - nbuf, 0), n):
            for q in range(no):
                out_copy(q, step, step % nbuf).wait()

    assert n <= 8
    nbuf = min(n, depth)
    pl.run_scoped(scoped, *[pltpu.VMEM((nbuf, tr, c), q.dtype) for q in ins], *[pltpu.VMEM((nbuf, tr, c), q.dtype) for q in outs],
                  pltpu.SemaphoreType.DMA((ni, nbuf)), pltpu.SemaphoreType.DMA((no, nbuf)))


W_IN_PAD = 2304
BIG = ("w_in", "w_attn_o", "w_ssd_o", "w_out", "w_up", "w_down")
BIG_SHAPE = dict(w_in=(D_MODEL, W_IN_PAD), w_attn_o=(Q_DIM // 4, D_MODEL), w_ssd_o=(D_INNER // 4, D_MODEL),
                 w_out=(D_MODEL // 4, D_MODEL), w_up=(D_MODEL, 2 * D_FF // 4), w_down=(D_FF // 4, D_MODEL))
BIG_TR = dict(w_in=128, w_attn_o=128, w_ssd_o=128, w_out=128, w_up=128, w_down=176)
X_FIRST = dict(w_in=True, w_attn_o=True, w_ssd_o=False, w_out=True, w_up=False, w_down=False)


def _neighbours(x, y, x_first):
    xn, yn = (1 - x, y), (x, 1 - y)
    n1, n2 = (xn, yn) if x_first else (yn, xn)
    slot = lambda ch: 2 * ch[0] + ch[1]
    return n1, n2, slot(n1), slot(n2), slot((1 - x, 1 - y))


def _gather_big(shards):
    nt = len(BIG)

    def body(*refs):
        sh, out = refs[:nt], refs[nt:2 * nt]
        send_sems, recv_sems = refs[2 * nt:]
        x, y, cc = lax.axis_index("x"), lax.axis_index("y"), lax.axis_index("c")
        me = 2 * x + y
        sib = (x, y, 1 - cc)
        for t, n in enumerate(BIG):
            _pipe(lambda v: (v,), [sh[t]], [out[t].at[me]], BIG_TR[n])

        def copy(t, k, slot, pc, to):
            hr = BIG_SHAPE[BIG[t]][0] // 2
            ref = out[t].at[slot, pl.ds(pc * hr, hr), :]
            return pltpu.make_async_remote_copy(src_ref=ref, dst_ref=ref, send_sem=send_sems.at[6 * t + k],
                                                recv_sem=recv_sems.at[6 * t + k], device_id=to, device_id_type=MESH)

        started = []

        def start(cp):
            cp.start()
            started.append(cp)

        geo = [_neighbours(x, y, X_FIRST[n]) for n in BIG]
        for t in range(nt):
            n1, n2, _, _, _ = geo[t]
            start(copy(t, 0, me, cc, (*n1, cc)))
            start(copy(t, 1, me, cc, (*n2, cc)))
        for t in range(nt):
            n1, n2, s1, s2, sd = geo[t]
            copy(t, 0, s1, cc, sib).wait_recv()
            start(copy(t, 2, s1, cc, (*n2, cc)))
            start(copy(t, 3, s1, cc, sib))
            copy(t, 1, s2, cc, sib).wait_recv()
            start(copy(t, 4, s2, cc, sib))
        for t in range(nt):
            _, _, s1, s2, sd = geo[t]
            copy(t, 2, sd, cc, sib).wait_recv()
            start(copy(t, 5, sd, cc, sib))
        for t in range(nt):
            _, _, s1, s2, sd = geo[t]
            copy(t, 3, s1, 1 - cc, sib).wait_recv()
            copy(t, 4, s2, 1 - cc, sib).wait_recv()
            copy(t, 5, sd, 1 - cc, sib).wait_recv()
        for cp in started:
            cp.wait_send()

    return pl.pallas_call(
        body, name="gather_big", in_specs=[ANY] * nt, out_specs=[ANY] * nt,
        out_shape=[jax.ShapeDtypeStruct((N_CHIPS, *BIG_SHAPE[n]), BF16) for n in BIG],
        scratch_shapes=[pltpu.SemaphoreType.DMA((6 * nt,)), pltpu.SemaphoreType.DMA((6 * nt,))],
        compiler_params=pltpu.CompilerParams(vmem_limit_bytes=VMEM_LIMIT),
    )(*shards)


def _reduce_big(grads):
    nt = len(BIG)
    nw = 7

    def body(*refs):
        g = refs[:nt]
        fin = refs[nt:2 * nt]
        work = refs[2 * nt:2 * nt + nw * nt]
        send_sems, recv_sems = refs[2 * nt + nw * nt:]
        x, y, cc = lax.axis_index("x"), lax.axis_index("y"), lax.axis_index("c")
        me = 2 * x + y
        sib = (x, y, 1 - cc)
        started = []

        def rcopy(t, k, src, dst, to):
            cp = pltpu.make_async_remote_copy(src_ref=src, dst_ref=dst, send_sem=send_sems.at[5 * t + k],
                                              recv_sem=recv_sems.at[5 * t + k], device_id=to, device_id_type=MESH)
            return cp

        def start(cp):
            cp.start()
            started.append(cp)

        geo = [_neighbours(x, y, X_FIRST[n]) for n in BIG]
        hrs = [BIG_SHAPE[n][0] // 2 for n in BIG]
        wk = lambda t: work[nw * t:nw * (t + 1)]
        one = lambda ref, slot: ref.at[pl.ds(slot, 1)]
        for t in range(nt):
            recv_a = wk(t)[0]
            start(rcopy(t, 0, g[t].at[:, pl.ds((1 - cc) * hrs[t], hrs[t]), :], recv_a, sib))
        for t, n in enumerate(BIG):
            recv_a, p32, p16, r1, qme, qs2, r2 = wk(t)
            n1, n2, s1, s2, sd = geo[t]
            rcopy(t, 0, recv_a, recv_a, sib).wait_recv()
            _pipe(lambda a, b: (a + b, a + b), [g[t].at[:, pl.ds(cc * hrs[t], hrs[t]), :], recv_a], [p32, p16], BIG_TR[n])
            start(rcopy(t, 1, one(p16, s1), one(r1, 0), (*n1, cc)))
            start(rcopy(t, 2, one(p16, sd), one(r1, 1), (*n1, cc)))
        for t, n in enumerate(BIG):
            recv_a, p32, p16, r1, qme, qs2, r2 = wk(t)
            n1, n2, s1, s2, sd = geo[t]
            rcopy(t, 1, one(r1, 0), one(r1, 0), sib).wait_recv()
            rcopy(t, 2, one(r1, 1), one(r1, 1), sib).wait_recv()
            _pipe(lambda a, b: (a + b.astype(F32),), [one(p32, s2), one(r1, 1)], [qs2], BIG_TR[n])
            start(rcopy(t, 3, qs2, r2, (*n2, cc)))
            _pipe(lambda a, b: (a + b.astype(F32),), [one(p32, me), one(r1, 0)], [qme], BIG_TR[n])
        for t, n in enumerate(BIG):
            recv_a, p32, p16, r1, qme, qs2, r2 = wk(t)
            rcopy(t, 3, r2, r2, sib).wait_recv()
            mine = fin[t].at[pl.ds(cc * hrs[t], hrs[t]), :]
            _pipe(lambda a, b: (a + b.astype(F32),), [qme.at[0], r2.at[0]], [mine], BIG_TR[n])
            start(rcopy(t, 4, mine, mine, sib))
        for t in range(nt):
            other = fin[t].at[pl.ds((1 - cc) * hrs[t], hrs[t]), :]
            rcopy(t, 4, other, other, sib).wait_recv()
        for cp in started:
            cp.wait_send()

    outs = [jax.ShapeDtypeStruct(BIG_SHAPE[n], F32) for n in BIG]
    for n in BIG:
        r, c = BIG_SHAPE[n]
        hr = r // 2
        outs += [jax.ShapeDtypeStruct((4, hr, c), F32), jax.ShapeDtypeStruct((4, hr, c), F32),
                 jax.ShapeDtypeStruct((4, hr, c), BF16), jax.ShapeDtypeStruct((2, hr, c), BF16),
                 jax.ShapeDtypeStruct((1, hr, c), F32), jax.ShapeDtypeStruct((1, hr, c), BF16),
                 jax.ShapeDtypeStruct((1, hr, c), BF16)]
    res = pl.pallas_call(
        body, name="reduce_big", in_specs=[ANY] * nt, out_specs=[ANY] * len(outs), out_shape=outs,
        scratch_shapes=[pltpu.SemaphoreType.DMA((5 * nt,)), pltpu.SemaphoreType.DMA((5 * nt,))],
        compiler_params=pltpu.CompilerParams(vmem_limit_bytes=VMEM_LIMIT),
    )(*grads)
    return res[:nt]


WHOLE_X_FIRST = dict(w_ssd_o=True, w_out=False, w_attn_o=False)


def _quarters(names):
    out = []
    for i, n in enumerate(names):
        if n in WHOLE_X_FIRST:
            h = BIG_SHAPE[n][0] // 2
            out.append((i, WHOLE_X_FIRST[n], 0, h, 128))
        else:
            q = BIG_SHAPE[n][0] // 4
            tr = 128 if q % 128 == 0 else q
            out += [(i, True, 0, q, tr), (i, False, q, q, tr)]
    return out


class _GatherJob:
    def __init__(self, names, shards, at=None):
        self.names = names
        self.at = at
        self.inputs = list(shards)
        self.out_shapes = [jax.ShapeDtypeStruct((N_CHIPS, *BIG_SHAPE[n]), BF16) for n in names]
        self.ent = _quarters(names)
        self.scratch = [pltpu.SemaphoreType.DMA((6 * len(self.ent),)), pltpu.SemaphoreType.DMA((6 * len(self.ent),))]

    def phases(self, sh, out, scr):
        send_sems, recv_sems = scr
        names, ent = self.names, self.ent
        x, y, cc = lax.axis_index("x"), lax.axis_index("y"), lax.axis_index("c")
        me = 2 * x + y
        sib = (x, y, 1 - cc)
        geo = [_neighbours(x, y, e[1]) for e in ent]
        started = []

        def copy(i, k, slot, pc, to):
            arr, _, roff, rows, _ = ent[i]
            hr = BIG_SHAPE[names[arr]][0] // 2
            ref = out[arr].at[slot, pl.ds(pc * hr + roff, rows), :]
            return pltpu.make_async_remote_copy(src_ref=ref, dst_ref=ref, send_sem=send_sems.at[6 * i + k],
                                                recv_sem=recv_sems.at[6 * i + k], device_id=to, device_id_type=MESH)

        def start(*a):
            copy(*a).start()
            started.append(a)

        def p0():
            for t, n in enumerate(names):
                _pipe(lambda v: (v,), [sh[t]], [out[t].at[me]], BIG_TR[n])
            for i in range(len(ent)):
                n1, n2, _, _, _ = geo[i]
                start(i, 0, me, cc, (*n1, cc))
                start(i, 1, me, cc, (*n2, cc))

        def p1():
            for i in range(len(ent)):
                n1, n2, s1, s2, sd = geo[i]
                copy(i, 0, s1, cc, sib).wait_recv()
                start(i, 2, s1, cc, (*n2, cc))
                start(i, 3, s1, cc, sib)
                copy(i, 1, s2, cc, sib).wait_recv()
                start(i, 4, s2, cc, sib)

        def p2():
            for i in range(len(ent)):
                sd = geo[i][4]
                copy(i, 2, sd, cc, sib).wait_recv()
                start(i, 5, sd, cc, sib)

        def p3():
            for i in range(len(ent)):
                _, _, s1, s2, sd = geo[i]
                copy(i, 3, s1, 1 - cc, sib).wait_recv()
                copy(i, 4, s2, 1 - cc, sib).wait_recv()
                copy(i, 5, sd, 1 - cc, sib).wait_recv()
            for a in started:
                copy(*a).wait_send()

        return [p0, p1, p2, p3]


class _ReduceJob:
    NW = 7

    def __init__(self, names, grads, at=None):
        self.names = names
        self.at = at
        self.inputs = list(grads)
        self.ent = _quarters(names)
        self.out_shapes = [jax.ShapeDtypeStruct(BIG_SHAPE[n], F32) for n in names]
        for arr, _, _, rows, _ in self.ent:
            c = BIG_SHAPE[names[arr]][1]
            self.out_shapes += [jax.ShapeDtypeStruct((4, rows, c), F32), jax.ShapeDtypeStruct((4, rows, c), F32),
                                jax.ShapeDtypeStruct((4, rows, c), BF16), jax.ShapeDtypeStruct((2, rows, c), BF16),
                                jax.ShapeDtypeStruct((1, rows, c), F32), jax.ShapeDtypeStruct((1, rows, c), BF16),
                                jax.ShapeDtypeStruct((1, rows, c), BF16)]
        self.scratch = [pltpu.SemaphoreType.DMA((8 * len(self.ent),)), pltpu.SemaphoreType.DMA((8 * len(self.ent),))]

    def phases(self, g, outs, scr):
        send_sems, recv_sems = scr
        names, ent, nw = self.names, self.ent, self.NW
        nt = len(names)
        fin, work = outs[:nt], outs[nt:]
        x, y, cc = lax.axis_index("x"), lax.axis_index("y"), lax.axis_index("c")
        me = 2 * x + y
        sib = (x, y, 1 - cc)
        geo = [_neighbours(x, y, e[1]) for e in ent]
        started = []
        wk = lambda i: work[nw * i:nw * (i + 1)]
        one = lambda ref, slot: ref.at[pl.ds(slot, 1)]

        def rows_of(i, pc):
            arr, _, roff, rows, _ = ent[i]
            return pl.ds(pc * (BIG_SHAPE[names[arr]][0] // 2) + roff, rows)

        def rcopy(i, k, src, dst, to):
            return pltpu.make_async_remote_copy(src_ref=src, dst_ref=dst, send_sem=send_sems.at[8 * i + k],
                                                recv_sem=recv_sems.at[8 * i + k], device_id=to, device_id_type=MESH)

        def start(make):
            make().start()
            started.append(make)

        def pair(i, q, slot, pc):
            return rcopy(i, q, g[ent[i][0]].at[pl.ds(slot, 1), rows_of(i, pc), :], one(wk(i)[0], slot), sib)

        def p0():
            for i in range(len(ent)):
                _, _, s1, s2, sd = geo[i]
                for q, slot in enumerate((s1, sd, s2, me)):
                    start(lambda i=i, q=q, slot=slot: pair(i, q, slot, 1 - cc))

        def p1():
            for i, e in enumerate(ent):
                recv_a, _, p16, _ = wk(i)[:4]
                n1, n2, s1, s2, sd = geo[i]
                pair(i, 0, s1, cc).wait_recv()
                pair(i, 1, sd, cc).wait_recv()
                _pipe(lambda a, b: (a + b,), [g[e[0]].at[:, rows_of(i, cc), :], recv_a], [p16], e[4], slots=(s1, sd))
                start(lambda i=i, s1=s1, n1=n1: rcopy(i, 4, one(wk(i)[2], s1), one(wk(i)[3], 0), (*n1, cc)))
                start(lambda i=i, sd=sd, n1=n1: rcopy(i, 5, one(wk(i)[2], sd), one(wk(i)[3], 1), (*n1, cc)))
            for i, e in enumerate(ent):
                recv_a, p32 = wk(i)[:2]
                _, _, s1, s2, sd = geo[i]
                pair(i, 2, s2, cc).wait_recv()
                pair(i, 3, me, cc).wait_recv()
                _pipe(lambda a, b: (a + b,), [g[e[0]].at[:, rows_of(i, cc), :], recv_a], [p32], e[4], slots=(s2, me))

        def p2():
            for i, e in enumerate(ent):
                _, p32, _, r1, qme, qs2, r2 = wk(i)
                n1, n2, s1, s2, sd = geo[i]
                rcopy(i, 4, one(r1, 0), one(r1, 0), sib).wait_recv()
                rcopy(i, 5, one(r1, 1), one(r1, 1), sib).wait_recv()
                _pipe(lambda a, b, c, d: (a + b.astype(F32), c + d.astype(F32)),
                      [one(p32, s2), one(r1, 1), one(p32, me), one(r1, 0)], [qs2, qme], e[4])
                start(lambda i=i, n2=n2: rcopy(i, 6, wk(i)[5], wk(i)[6], (*n2, cc)))

        def p3():
            for i, e in enumerate(ent):
                qme, r2 = wk(i)[4], wk(i)[6]
                rcopy(i, 6, r2, r2, sib).wait_recv()
                mine = fin[e[0]].at[rows_of(i, cc), :]
                _pipe(lambda a, b: (a + b.astype(F32),), [qme.at[0], r2.at[0]], [mine], e[4])
                start(lambda i=i, e=e: rcopy(i, 7, fin[e[0]].at[rows_of(i, cc), :], fin[e[0]].at[rows_of(i, cc), :], sib))

        def p4():
            for i, e in enumerate(ent):
                other = fin[e[0]].at[rows_of(i, 1 - cc), :]
                rcopy(i, 7, other, other, sib).wait_recv()
            for make in started:
                make().wait_send()

        return [p0, p1, p2, p3, p4]


class _AdamJob:
    def __init__(self, names, ws, gs, ms, vs, groups):
        self.names, self.groups = names, groups
        self.inputs = [a for quad in zip(ws, gs, ms, vs) for a in quad]
        self.out_shapes = [jax.ShapeDtypeStruct(w.shape, F32) for w in ws for _ in range(4)]

    def work(self, ins, outs):
        def one(t):
            w, g, m, v = ins[4 * t:4 * t + 4]
            r = w.shape[1]
            tr = 128 if r % 128 == 0 else r // 4
            _pipe(lambda a, b, c, d: (*_adamw_math(a, b, c, d), b), [w.at[0], g, m.at[0], v.at[0]],
                  [o.at[0] for o in outs[4 * t:4 * t + 4]], tr, depth=2)

        def group(grp):
            def run():
                for n in grp:
                    one(self.names.index(n))
            return run

        return [group(grp) for grp in self.groups]


class _Interleaved:
    def __init__(self, job, work, at):
        self.job, self.wk, self.at = job, work, at
        self.inputs = job.inputs + work.inputs
        self.out_shapes = list(job.out_shapes) + list(work.out_shapes)
        self.scratch = job.scratch

    def phases(self, ins, outs, scr):
        nj, no = len(self.job.inputs), len(self.job.out_shapes)
        base = self.job.phases(ins[:nj], outs[:no], scr)
        work = self.wk.work(ins[nj:], outs[no:])
        mixed = []
        for k, ph in enumerate(base):
            mixed.append(ph)
            if k < len(work):
                mixed.append(work[k])
        return mixed


def _run_job(job, name):
    ni, no = len(job.inputs), len(job.out_shapes)

    def body(*refs):
        for ph in job.phases(refs[:ni], refs[ni:ni + no], refs[ni + no:]):
            ph()

    return pl.pallas_call(
        body, name=name, in_specs=[ANY] * ni, out_specs=[ANY] * no, out_shape=job.out_shapes, scratch_shapes=job.scratch,
        compiler_params=pltpu.CompilerParams(vmem_limit_bytes=VMEM_LIMIT),
    )(*job.inputs)


def _hosted(body, *, name, grid, in_specs, out_specs, out_shape, scratch_shapes, args, sem, side=None):
    if side is None:
        return pl.pallas_call(body, name=name, grid=grid, in_specs=in_specs, out_specs=out_specs, out_shape=out_shape,
                              scratch_shapes=scratch_shapes, compiler_params=_cp(sem))(*args), None
    job = side
    ni, no, ns = len(in_specs), len(out_specs), len(scratch_shapes)
    ji, jo = len(job.inputs), len(job.out_shapes)
    n_steps = 1
    for extent in grid:
        n_steps *= extent

    def wrapped(*refs):
        own_in, refs = refs[:ni], refs[ni:]
        job_in, refs = refs[:ji], refs[ji:]
        own_out, refs = refs[:no], refs[no:]
        job_out, refs = refs[:jo], refs[jo:]
        own_scr, job_scr = refs[:ns], refs[ns:]
        step = 0
        for d, extent in enumerate(grid):
            step = step * extent + pl.program_id(d)
        phases = job.phases(job_in, job_out, job_scr)
        steps = [min(int(f * n_steps), n_steps - 1) for f in job.at] + [n_steps - 1]
        assert len(steps) == len(phases) and steps == sorted(steps)
        for at, ph in zip(steps, phases):
            pl.when(step == at)(ph)
        body(*own_in, *own_out, *own_scr)

    res = pl.pallas_call(
        wrapped, name=name, grid=grid, in_specs=list(in_specs) + [ANY] * ji, out_specs=list(out_specs) + [ANY] * jo,
        out_shape=list(out_shape) + list(job.out_shapes), scratch_shapes=list(scratch_shapes) + list(job.scratch),
        compiler_params=_cp(("arbitrary",) * len(grid)),
    )(*args, *job.inputs)
    return res[:no], res[no:]


def _proj_dw(xnt, dproj_sh, *, tm=512, tk=2048):
    d, s = xnt.shape
    tk = _tile(s, tk)
    nk = s // tk

    def body(a_ref, b_ref, o_ref, acc):
        def finish(r):
            o_ref[0] = r

        _accumulate(acc, _dot(a_ref[...], b_ref[...]), pl.program_id(2), nk, finish)

    return pl.pallas_call(
        body, name="proj_dw", grid=(N_CHIPS, d // tm, nk),
        in_specs=[pl.BlockSpec((tm, tk), lambda j, i, q: (i, q)), pl.BlockSpec((tk, W_IN_PAD), lambda j, i, q: (q, j))],
        out_specs=pl.BlockSpec((1, tm, W_IN_PAD), lambda j, i, q: (j, i, 0)),
        out_shape=jax.ShapeDtypeStruct((N_CHIPS, d, W_IN_PAD), F32), scratch_shapes=[pltpu.VMEM((tm, W_IN_PAD), F32)],
        compiler_params=_cp(("parallel", "parallel", "arbitrary")),
    )(xnt, dproj_sh)


def _proj_dx(dproj_sh, w_sh, *, tm=1024, side=None):
    s = dproj_sh.shape[0]
    d = w_sh.shape[1]
    tm = _tile(s, tm)

    def body(a_ref, b_ref, o_ref, acc):
        kk = pl.program_id(1)
        part = _dot_nt(a_ref[...], b_ref[0])

        @pl.when(kk == 0)
        def _():
            acc[...] = part

        @pl.when(kk > 0)
        def _():
            acc[...] += part

        @pl.when(kk == N_CHIPS - 1)
        def _():
            o_ref[...] = acc[...]

    own, extra = _hosted(
        body, name="proj_dx", grid=(s // tm, N_CHIPS),
        in_specs=[pl.BlockSpec((tm, W_IN_PAD), lambda i, q: (i, q)), pl.BlockSpec((1, d, W_IN_PAD), lambda i, q: (q, 0, 0))],
        out_specs=[pl.BlockSpec((tm, d), lambda i, q: (i, 0))],
        out_shape=[jax.ShapeDtypeStruct((s, d), F32)], scratch_shapes=[pltpu.VMEM((tm, d), F32)],
        args=(dproj_sh, w_sh), sem=("parallel", "arbitrary"), side=side)
    return own[0] if side is None else (own[0], extra)


def _up_dx(dup, w_sh, *, tm=1024):
    s = dup.shape[1]
    d, wsh = w_sh.shape[1:]
    tm = _tile(s, tm)

    def body(a_ref, b_ref, o_ref, acc):
        kk = pl.program_id(1)
        part = _dot_nt(a_ref[0], b_ref[0])

        @pl.when(kk == 0)
        def _():
            acc[...] = part

        @pl.when(kk > 0)
        def _():
            acc[...] += part

        @pl.when(kk == N_CHIPS - 1)
        def _():
            o_ref[...] = acc[...]

    return pl.pallas_call(
        body, name="up_dx", grid=(s // tm, N_CHIPS),
        in_specs=[pl.BlockSpec((1, tm, wsh), lambda i, q: (q >> 1, i, q & 1)), pl.BlockSpec((1, d, wsh), lambda i, q: (q, 0, 0))],
        out_specs=pl.BlockSpec((tm, d), lambda i, q: (i, 0)),
        out_shape=jax.ShapeDtypeStruct((s, d), F32), scratch_shapes=[pltpu.VMEM((tm, d), F32)],
        compiler_params=_cp(("parallel", "arbitrary")),
    )(dup, w_sh)


def _up_dw(hnt, dup, *, tk=2048):
    d, s = hnt.shape
    wsh = 2 * D_FF // N_CHIPS
    tk = _tile(s, tk)
    nk = s // tk

    def body(a_ref, b_ref, o_ref, acc):
        def finish(r):
            o_ref[0] = r

        _accumulate(acc, _dot(a_ref[...], b_ref[0]), pl.program_id(1), nk, finish)

    return pl.pallas_call(
        body, name="up_dw", grid=(N_CHIPS, nk),
        in_specs=[pl.BlockSpec((d, tk), lambda j, q: (0, q)), pl.BlockSpec((1, tk, wsh), lambda j, q: (j >> 1, q, j & 1))],
        out_specs=pl.BlockSpec((1, d, wsh), lambda j, q: (j, 0, 0)),
        out_shape=jax.ShapeDtypeStruct((N_CHIPS, d, wsh), F32), scratch_shapes=[pltpu.VMEM((d, wsh), F32)],
        compiler_params=_cp(("parallel", "arbitrary")),
    )(hnt, dup)


BIG_ROWS =(IN_DIM // 4, Q_DIM // 4, D_INNER // 4, D_MODEL // 4, 2 * D_FF // 4, D_FF // 4)
PACK_ROWS = 5376


def _pack_shards(parts):
    rows = [p.reshape(-1, D_MODEL) for p in parts]
    pad = PACK_ROWS - sum(BIG_ROWS)
    return jnp.concatenate(rows + [jnp.zeros((pad, D_MODEL), rows[0].dtype)], axis=0)


def _unpack_shards(buf):
    out, off = [], 0
    for n in BIG_ROWS:
        out.append(buf[off:off + n])
        off += n
    return out


def _assemble(srcs, col_map, *, name, tr=256):
    arrays, lead = [], []
    for src in srcs:
        arr, j = src if isinstance(src, tuple) else (src, None)
        if not any(arr is a for a in arrays):
            arrays.append(arr)
        lead.append(([i for i, a in enumerate(arrays) if a is arr][0], j))
    rows = arrays[0].shape[-2]
    tr = _tile(rows, tr)
    out_w = len(col_map)
    tiles = []
    for t in range(out_w // 128):
        runs = []
        for lane in range(128):
            ent = col_map[t * 128 + lane]
            key = None if ent is None else (ent[0], ent[1] // 128, (lane - ent[1]) % 128)
            if runs and runs[-1][0] == key:
                runs[-1][2] = lane + 1
            else:
                runs.append([key, lane, lane + 1])
        tiles.append(runs)

    def body(*refs):
        o_ref = refs[-1]
        lane = lax.broadcasted_iota(jnp.int32, (tr, 128), 1)
        for t, runs in enumerate(tiles):
            acc = jnp.zeros((tr, 128), F32)
            for key, a, b in runs:
                if key is None:
                    continue
                sid, ct, shift = key
                ai, j = lead[sid]
                cols = slice(ct * 128, (ct + 1) * 128)
                piece = (refs[ai][:, cols] if j is None else refs[ai][j, :, cols]).astype(F32)
                if shift:
                    piece = pltpu.roll(piece, shift, 1)
                acc = piece if (a, b) == (0, 128) else jnp.where((lane >= a) & (lane < b), piece, acc)
            o_ref[:, t * 128:(t + 1) * 128] = acc.astype(BF16)

    specs = [pl.BlockSpec((tr, a.shape[1]), lambda i: (i, 0)) if a.ndim == 2
             else pl.BlockSpec((a.shape[0], tr, a.shape[2]), lambda i: (0, i, 0)) for a in arrays]
    return pl.pallas_call(
        body, name=name, grid=(rows // tr,), in_specs=specs, out_specs=pl.BlockSpec((tr, out_w), lambda i: (i, 0)),
        out_shape=jax.ShapeDtypeStruct((rows, out_w), BF16), compiler_params=_cp(("parallel",)),
    )(*arrays)


def _permute_cols_in(w):
    pad = jnp.zeros((w.shape[0], PW - IN_DIM), w.dtype)
    return jnp.concatenate([w[:, :6656], w[:, 6688:], w[:, 6656:6688], pad], axis=1)


def _unpermute_cols_in(g):
    return jnp.concatenate([g[:, :6656], g[:, O_DT:O_DT + 32], g[:, 6656:O_DT]], axis=1)


SMALL = ("norm1_w", "b_gate", "attn_sinks", "ssd_conv_b", "dt_bias", "a_log", "d_skip", "ssd_norm_w", "norm2_w",
         "ffn_conv_b", "final_norm_w", "ssd_conv_w", "ffn_conv_w")


def _pad128(v):
    v = v.reshape(-1)
    return jnp.pad(v, (0, (-v.shape[0]) % 128))


def _pack_small(parts):
    flat = jnp.concatenate([_pad128(p) for p in parts])
    flat = jnp.pad(flat, (0, (-flat.shape[0]) % 1024))
    return flat.reshape(-1, 128)


def _unpack_small(buf, shapes):
    flat, out, off = buf.reshape(-1), [], 0
    for shp in shapes:
        n = 1
        for q in shp:
            n *= q
        out.append(flat[off:off + n].reshape(shp))
        off += n + (-n) % 128
    return out


def _vec128(v):
    return jnp.pad(v.reshape(1, -1), ((0, 0), (0, 128 - v.shape[-1])))


def kernel(x, norm1_w, w_in, b_gate, attn_sinks, w_attn_o, ssd_conv_w, ssd_conv_b, dt_bias, a_log, d_skip, ssd_norm_w, w_ssd_o, w_out, norm2_w, w_up, ffn_conv_w, ffn_conv_b, w_down, final_norm_w, loss_target, m_norm1_w, m_w_in, m_b_gate, m_attn_sinks, m_w_attn_o, m_ssd_conv_w, m_ssd_conv_b, m_dt_bias, m_a_log, m_d_skip, m_ssd_norm_w, m_w_ssd_o, m_w_out, m_norm2_w, m_w_up, m_ffn_conv_w, m_ffn_conv_b, m_w_down, m_final_norm_w, v_norm1_w, v_w_in, v_b_gate, v_attn_sinks, v_w_attn_o, v_ssd_conv_w, v_ssd_conv_b, v_dt_bias, v_a_log, v_d_skip, v_ssd_norm_w, v_w_ssd_o, v_w_out, v_norm2_w, v_w_up, v_ffn_conv_w, v_ffn_conv_b, v_w_down, v_final_norm_w):
    ix, iy, ic = lax.axis_index("x"), lax.axis_index("y"), lax.axis_index("c")
    chip = 2 * ix + iy
    x2 = x[0]
    tgt = loss_target[0]
    s = x2.shape[0]

    wsh = IN_DIM // N_CHIPS
    big_shards = dict(w_in=jnp.pad(w_in[0], ((0, 0), (0, W_IN_PAD - wsh))), w_attn_o=w_attn_o[0], w_ssd_o=w_ssd_o[0],
                      w_out=w_out[0], w_up=w_up[0], w_down=w_down[0])
    gathered = {}
    (xn, xnt), (gathered["w_in"],) = _rms_fwd(x2, norm1_w, name="norm1_fwd", with_t=True,
                                              side=_GatherJob(("w_in",), [big_shards["w_in"]], at=(0.0, 0.5, 0.75)))
    early = ("w_attn_o", "w_ssd_o", "w_out")
    gather_early = _GatherJob(early, [big_shards[n] for n in early], at=(0.0, 0.5, 0.8))
    gather_up = _GatherJob(("w_up",), [big_shards["w_up"]], at=(0.0, 0.55, 0.85))
    gather_down = _GatherJob(("w_down",), [big_shards["w_down"]], at=(0.0, 0.5, 0.8))
    gw = gathered["w_in"]
    perm = list(range(O_GA)) + list(range(O_GA + N_SSD_HEADS, IN_DIM)) + list(range(O_GA, O_GA + N_SSD_HEADS))
    w_in_p = _assemble([(gw, j) for j in range(N_CHIPS)], [divmod(o, wsh) for o in perm] + [None] * (PW - IN_DIM),
                       name="w_in_assemble")
    small_sh = _pack_small([ssd_conv_w[0], ffn_conv_w[0]])
    small_all = _all_gather_small(small_sh)
    sc_parts = [_unpack_small(small_all[j], [(4, XBC_DIM // 4), (3, 2 * D_FF // 4)]) for j in range(N_CHIPS)]
    ssd_cw = jnp.concatenate([p[0] for p in sc_parts], axis=1)
    ffn_cw = jnp.concatenate([p[1] for p in sc_parts], axis=1)

    sinks128 = _vec128(attn_sinks)
    dtb128, alog128, dskip128 = _vec128(dt_bias), _vec128(a_log), _vec128(d_skip)

    proj, got = _mm(xn, w_in_p, name="proj_fwd", tn=1280, side=gather_early)
    gathered.update(zip(early, got))
    qkvt = _mm(w_in_p[:, :O_Z], xnt, name="qkv_fwd", ta=True)
    attn_pre, (gathered["w_up"],) = _attn_fwd(qkvt, sinks128, side=gather_up)
    xbc = _ssd_conv_fwd(proj, ssd_cw, ssd_conv_b)
    (y_ssd, hprev), (gathered["w_down"],) = _ssd_fwd(xbc, proj, dtb128, alog128, dskip128, side=gather_down)
    full = {n: gathered[n].reshape(-1, D_MODEL) for n in ("w_attn_o", "w_ssd_o", "w_out", "w_down")}
    full["w_up"] = gathered["w_up"]
    attn = _mm(attn_pre, full["w_attn_o"], name="attn_o_fwd", ta=True)
    yn = _gate_norm_fwd(y_ssd, proj, ssd_norm_w)
    ssd_out = _mm(yn, full["w_ssd_o"], name="ssd_o_fwd", tk=2048)
    merged = _merge_fwd(proj, b_gate, attn, ssd_out)
    h1 = _mm(merged, full["w_out"], name="out_fwd", resid=x2)
    hn, hnt = _rms_fwd(h1, norm2_w, name="norm2_fwd", with_t=True)
    up = _mm(hn, full["w_up"], name="up_fwd")
    act = _ffn_act_fwd(up, ffn_cw, ffn_conv_b)
    h2 = _mm(act, full["w_down"], name="down_fwd", resid=h1, tk=2816)

    dh2, loss_blk, g_final, dh2_16 = _loss_bwd(h2, tgt, final_norm_w.reshape(1, -1))
    dact = _mm(dh2_16, full["w_down"], name="down_dx", tb=True, tn=1408)
    g_down = _mm(act, dh2_16, name="down_dw", ta=True, tm=1408, tk=2048)
    dup, g_ffn_cw, g_ffn_cb = _ffn_act_bwd(dact, up, ffn_cw, ffn_conv_b)
    dhn = _up_dx(dup, full["w_up"])
    g_up = _up_dw(hnt, dup)
    dh1, g_norm2, dh1_16 = _rms_bwd(dhn, h1, norm2_w, dh2, name="norm2_bwd")
    dmerged = _mm(dh1_16, full["w_out"], name="out_dx", tb=True)
    g_out = _mm(merged, dh1_16, name="out_dw", ta=True, tk=2048)
    dattn, dssd_out, dga, dgs, g_ba, g_bs = _merge_bwd(dmerged, proj, b_gate, attn, ssd_out)
    dyn = _mm(dssd_out, full["w_ssd_o"], name="ssd_o_dx", tb=True)
    g_ssd_o = _mm(yn, dssd_out, name="ssd_o_dw", ta=True, tk=2048)
    dy_ssd, dz, g_ssd_norm = _gate_norm_bwd(dyn, y_ssd, proj, ssd_norm_w)
    slot = lambda g: g.reshape(N_CHIPS, -1, D_MODEL)
    big_grads = {}
    red = ("w_down", "w_up")
    (dxbc, ddt, dvec), got = _ssd_bwd(xbc, proj, dtb128, alog128, dskip128, hprev, dy_ssd,
                                      side=_ReduceJob(red, [slot(g_down), g_up], at=(0.0, 0.2, 0.7, 0.95)))
    big_grads.update(zip(red, got))
    dxbc_raw, g_ssd_cw, g_ssd_cb = _ssd_conv_bwd(dxbc, proj, ssd_cw, ssd_conv_b)
    dattn_pre = _mm(full["w_attn_o"], dattn, name="attn_o_dx", tb=True)
    g_attn_o = _mm(attn_pre, dattn, name="attn_o_dw", tk=2048)
    red = ("w_out", "w_ssd_o", "w_attn_o")
    (dq, dk, dv, dsk), got = _attn_bwd(qkvt, sinks128, attn_pre, dattn_pre,
                                       side=_ReduceJob(red, [slot(g_out), slot(g_ssd_o), slot(g_attn_o)],
                                                       at=(0.0, 0.2, 0.5, 0.7)))
    big_grads.update(zip(red, got))
    pieces = [(dq.T, Q_DIM), (dk.T, KV_DIM), (dv.T, KV_DIM), (dz, D_INNER), (dxbc_raw, XBC_DIM), (ddt, N_SSD_HEADS),
              (dga, D_MODEL), (dgs, D_MODEL)]
    orig = [(i, c) for i, (_, w) in enumerate(pieces) for c in range(w)]
    dproj_sh = _assemble([p for p, _ in pieces],
                         [orig[j * wsh + c] if c < wsh else None for j in range(N_CHIPS) for c in range(W_IN_PAD)],
                         name="dproj_assemble")
    g_in = _proj_dw(xnt, dproj_sh)
    dxn, got = _proj_dx(dproj_sh, gathered["w_in"], side=_ReduceJob(("w_in",), [g_in], at=(0.0, 0.15, 0.75, 0.95)))
    big_grads["w_in"] = got[0]
    dx, g_norm1, _ = _rms_bwd(dxn, x2, norm1_w, dh1, name="norm1_bwd")


    small_g = dict(
        norm1_w=g_norm1, b_gate=jnp.concatenate([g_ba, g_bs], axis=1), attn_sinks=dsk[0:1, :16], ssd_conv_b=g_ssd_cb,
        dt_bias=dvec[0:1, :32], a_log=dvec[1:2, :32], d_skip=dvec[2:3, :32], ssd_norm_w=g_ssd_norm, norm2_w=g_norm2,
        ffn_conv_b=jnp.concatenate([g_ffn_cb[0], g_ffn_cb[1]], axis=1), final_norm_w=g_final, ssd_conv_w=g_ssd_cw,
        ffn_conv_w=jnp.concatenate([g_ffn_cw[0], g_ffn_cw[1]], axis=1))
    small_buf = _pack_small([small_g[n] for n in SMALL] + [loss_blk])
    small_sum = _all_reduce_small(small_buf)
    small_shapes = [(1, D_MODEL), (1, 2 * D_MODEL), (1, 16), (1, XBC_DIM), (1, 32), (1, 32), (1, 32), (1, D_INNER),
                    (1, D_MODEL), (1, 2 * D_FF), (D_MODEL,), (4, XBC_DIM), (3, 2 * D_FF), (1, 128)]
    small_list = _unpack_small(small_sum, small_shapes)
    loss = small_list[-1][0, 0]
    grads = dict(zip(SMALL, small_list[:-1]))
    grads["ssd_conv_w"] = lax.dynamic_slice_in_dim(grads["ssd_conv_w"], chip * (XBC_DIM // 4), XBC_DIM // 4, axis=1)
    grads["ffn_conv_w"] = lax.dynamic_slice_in_dim(grads["ffn_conv_w"], chip * (2 * D_FF // 4), 2 * D_FF // 4, axis=1)
    grads.update(big_grads)

    weights = dict(norm1_w=norm1_w, w_in=w_in, b_gate=b_gate, attn_sinks=attn_sinks, w_attn_o=w_attn_o, ssd_conv_w=ssd_conv_w,
                   ssd_conv_b=ssd_conv_b, dt_bias=dt_bias, a_log=a_log, d_skip=d_skip, ssd_norm_w=ssd_norm_w, w_ssd_o=w_ssd_o,
                   w_out=w_out, norm2_w=norm2_w, w_up=w_up, ffn_conv_w=ffn_conv_w, ffn_conv_b=ffn_conv_b, w_down=w_down,
                   final_norm_w=final_norm_w)
    ms = dict(norm1_w=m_norm1_w, w_in=m_w_in, b_gate=m_b_gate, attn_sinks=m_attn_sinks, w_attn_o=m_w_attn_o,
              ssd_conv_w=m_ssd_conv_w, ssd_conv_b=m_ssd_conv_b, dt_bias=m_dt_bias, a_log=m_a_log, d_skip=m_d_skip,
              ssd_norm_w=m_ssd_norm_w, w_ssd_o=m_w_ssd_o, w_out=m_w_out, norm2_w=m_norm2_w, w_up=m_w_up,
              ffn_conv_w=m_ffn_conv_w, ffn_conv_b=m_ffn_conv_b, w_down=m_w_down, final_norm_w=m_final_norm_w)
    vs = dict(norm1_w=v_norm1_w, w_in=v_w_in, b_gate=v_b_gate, attn_sinks=v_attn_sinks, w_attn_o=v_w_attn_o,
              ssd_conv_w=v_ssd_conv_w, ssd_conv_b=v_ssd_conv_b, dt_bias=v_dt_bias, a_log=v_a_log, d_skip=v_d_skip,
              ssd_norm_w=v_ssd_norm_w, w_ssd_o=v_w_ssd_o, w_out=v_w_out, norm2_w=v_norm2_w, w_up=v_w_up,
              ffn_conv_w=v_ffn_conv_w, ffn_conv_b=v_ffn_conv_b, w_down=v_w_down, final_norm_w=v_final_norm_w)
    order = list(weights)
    deltas, new_m, new_v = {}, {}, {}
    for n in BIG:
        shp = weights[n].shape
        res = _adamw(weights[n][0], grads[n], ms[n][0], vs[n][0], name="adamw_" + n)
        deltas[n], new_m[n], new_v[n], grads[n] = (a.reshape(shp) for a in res)
    smalls = [n for n in order if n not in BIG]
    as2d = lambda a: a.reshape(-1, a.shape[-1])
    res = _adamw_many(*[[as2d(src[n][0] if src[n].ndim == 3 else src[n]) for n in smalls] for src in (weights, grads, ms, vs)])
    for i, n in enumerate(smalls):
        deltas[n], new_m[n], new_v[n] = (res[q * len(smalls) + i].reshape(weights[n].shape) for q in range(3))
    out_grads = [grads[n].reshape(weights[n].shape) for n in order]
    return (loss, dx[None], *out_grads, *[deltas[n] for n in order], *[new_m[n] for n in order], *[new_v[n] for n in order])
```
